```python
import math
import jax, jax.numpy as jnp
from jax import lax
import numpy as np

D_MODEL = 1024
BATCH = 8
SEQ = 4096
DEPTH = 1

D_MIX = D_MODEL
HGRN_W = D_MIX // 2
CONV_W = D_MIX - HGRN_W
HGRN_HEADS = 4
HGRN_DV = HGRN_W // HGRN_HEADS
HGRN_DK = 128
HGRN_F = HGRN_HEADS * HGRN_DK
CONV_GROUPS = 8
CONV_K = 3
CHUNK = 64
MEM_LEN = 256
MEM_HEADS = 4
MEM_HD = D_MODEL // MEM_HEADS
D_FF = int(math.ceil(8 * D_MODEL / 3 / 256) * 256)
EPS = 1e-6
IN_COLS = 3 * HGRN_F // HGRN_F * 0 + HGRN_F + HGRN_F + HGRN_W + HGRN_W + 3 * CONV_W

kernel_name = "hybrid_hgrn2_shortconv_macaron_memxattn"


def rms_norm(x, g):
    xf = x.astype(jnp.float32)
    y = xf * lax.rsqrt(jnp.mean(xf * xf, axis=-1, keepdims=True) + EPS)
    return (y * g.astype(jnp.float32)).astype(x.dtype)


def swiglu(h, w_gate, w_up, w_down):
    return (jax.nn.silu(h @ w_gate) * (h @ w_up)) @ w_down


def hgrn2_chunkwise(q, k, v, logf):
    b_, s_, h_, dk = q.shape
    dv = v.shape[-1]
    n = s_ // CHUNK

    def to_chunks(t):
        return t.astype(jnp.float32).reshape(b_, n, CHUNK, h_, t.shape[-1]).transpose(1, 0, 3, 2, 4)

    qc, kc, vc, lc = to_chunks(q), to_chunks(k), to_chunks(v), to_chunks(logf)
    bc = jnp.cumsum(lc, axis=-2)
    causal = jnp.tril(jnp.ones((CHUNK, CHUNK), dtype=bool))[None, None, :, :, None]

    def step(state, inp):
        qi, ki, vi, bi = inp
        diff = bi[:, :, :, None, :] - bi[:, :, None, :, :]
        decay = jnp.exp(jnp.where(causal, diff, -jnp.inf))
        scores = jnp.einsum('bhtk,bhsk,bhtsk->bhts', qi, ki, decay)
        o = jnp.einsum('bhts,bhsv->bhtv', scores, vi) + \
            jnp.einsum('bhtk,bhkv->bhtv', qi * jnp.exp(bi), state)
        b_last = bi[:, :, -1:, :]
        new_state = jnp.exp(b_last[:, :, 0, :])[..., None] * state + \
            jnp.einsum('bhsk,bhsv->bhkv', ki * jnp.exp(b_last - bi), vi)
        return new_state, o

    s0 = jnp.zeros((b_, h_, dk, dv), jnp.float32)
    _, o = lax.scan(step, s0, (qc, kc, vc, bc))
    return o.transpose(1, 0, 3, 2, 4).reshape(b_, s_, h_, dv)


def causal_depthwise_conv(u, w):
    s_ = u.shape[1]
    up = jnp.pad(u, ((0, 0), (CONV_K - 1, 0), (0, 0)))
    return sum(w[:, j] * up[:, j:j + s_, :] for j in range(CONV_K))


def _fwd_setup_inputs(seed: int = 0) -> dict:
    key = jax.random.key(seed)
    ks = jax.random.split(key, 24)
    L = DEPTH

    def w(k, shape, fan_in):
        return jax.random.normal(k, shape, jnp.float32) * fan_in ** -0.5

    def gain(k, shape):
        return 1.0 + 0.02 * jax.random.normal(k, shape, jnp.float32)

    return {
        "x": jax.random.normal(ks[0], (BATCH, SEQ, D_MODEL), jnp.float32),
        "mem": jax.random.normal(ks[1], (BATCH, MEM_LEN, D_MODEL), jnp.float32),
        "ffn1_norm": gain(ks[2], (L, D_MODEL)),
        "ffn1_gate": w(ks[3], (L, D_MODEL, D_FF), D_MODEL),
        "ffn1_up": w(ks[4], (L, D_MODEL, D_FF), D_MODEL),
        "ffn1_down": w(ks[5], (L, D_FF, D_MODEL), D_FF),
        "mix_norm": gain(ks[6], (L, D_MODEL)),
        "w_in": w(ks[7], (L, D_MODEL, IN_COLS), D_MODEL),
        "lb_param": 0.1 * jax.random.normal(ks[8], (L + 1, HGRN_F), jnp.float32),
        "hgrn_out_norm": gain(ks[9], (L, HGRN_W)),
        "conv_w": w(ks[10], (L, CONV_W, CONV_K), CONV_K),
        "w_out": w(ks[11], (L, D_MIX, D_MODEL), D_MIX),
        "xattn_norm": gain(ks[12], (L, D_MODEL)),
        "mem_norm": gain(ks[13], (L, D_MODEL)),
        "w_q_mem": w(ks[14], (L, D_MODEL, D_MODEL), D_MODEL),
        "w_kv_mem": w(ks[15], (L, D_MODEL, 2 * D_MODEL), D_MODEL),
        "w_o_mem": w(ks[16], (L, D_MODEL, D_MODEL), D_MODEL),
        "ffn2_norm": gain(ks[17], (L, D_MODEL)),
        "ffn2_gate": w(ks[18], (L, D_MODEL, D_FF), D_MODEL),
        "ffn2_up": w(ks[19], (L, D_MODEL, D_FF), D_MODEL),
        "ffn2_down": w(ks[20], (L, D_FF, D_MODEL), D_FF),
        "final_norm": gain(ks[21], (D_MODEL,)),
    }


def _fwd_reference(x, mem, ffn1_norm, ffn1_gate, ffn1_up, ffn1_down, mix_norm, w_in,
              lb_param, hgrn_out_norm, conv_w, w_out, xattn_norm, mem_norm,
              w_q_mem, w_kv_mem, w_o_mem, ffn2_norm, ffn2_gate, ffn2_up,
              ffn2_down, final_norm):
    b_, s_, _ = x.shape
    lb_all = jnp.cumsum(jax.nn.softmax(lb_param.astype(jnp.float32), axis=0), axis=0)

    for l in range(DEPTH):
        x = x + 0.5 * swiglu(rms_norm(x, ffn1_norm[l]), ffn1_gate[l], ffn1_up[l], ffn1_down[l])

        h = rms_norm(x, mix_norm[l])
        z = h @ w_in[l]
        o1 = HGRN_F; o2 = o1 + HGRN_F; o3 = o2 + HGRN_W; o4 = o3 + HGRN_W
        o5 = o4 + CONV_W; o6 = o5 + CONV_W
        zq, zf, zi, zg = z[..., :o1], z[..., o1:o2], z[..., o2:o3], z[..., o3:o4]
        zb, zc, zu = z[..., o4:o5], z[..., o5:o6], z[..., o6:]

        lb = lb_all[l]
        f = lb + (1.0 - lb) * jax.nn.sigmoid(zf.astype(jnp.float32))
        logf = jnp.log(f)
        kk = 1.0 - f
        q = jax.nn.silu(zq.astype(jnp.float32)) * HGRN_DK ** -0.5
        shp_k = (b_, s_, HGRN_HEADS, HGRN_DK)
        o_h = hgrn2_chunkwise(q.reshape(shp_k), kk.reshape(shp_k),
                              zi.reshape(b_, s_, HGRN_HEADS, HGRN_DV), logf.reshape(shp_k))
        g_h = hgrn_out_norm[l].astype(jnp.float32).reshape(HGRN_HEADS, HGRN_DV)
        o_h = o_h * lax.rsqrt(jnp.mean(o_h * o_h, axis=-1, keepdims=True) + EPS) * g_h
        y_hgrn = (o_h.reshape(b_, s_, HGRN_W) * jax.nn.silu(zg.astype(jnp.float32))).astype(x.dtype)

        y_conv = zb * causal_depthwise_conv(zc * zu, conv_w[l])

        x = x + jnp.concatenate([y_hgrn, y_conv.astype(x.dtype)], axis=-1) @ w_out[l]

        hq = rms_norm(x, xattn_norm[l])
        mn = rms_norm(mem, mem_norm[l])
        qm = (hq @ w_q_mem[l]).reshape(b_, s_, MEM_HEADS, MEM_HD)
        kv = mn @ w_kv_mem[l]
        km = kv[..., :D_MODEL].reshape(b_, MEM_LEN, MEM_HEADS, MEM_HD)
        vm = kv[..., D_MODEL:].reshape(b_, MEM_LEN, MEM_HEADS, MEM_HD)
        sc = jnp.einsum('bshd,bmhd->bhsm', qm.astype(jnp.float32), km.astype(jnp.float32)) * MEM_HD ** -0.5
        p = jax.nn.softmax(sc, axis=-1)
        att = jnp.einsum('bhsm,bmhd->bshd', p, vm.astype(jnp.float32)).astype(x.dtype)
        x = x + att.reshape(b_, s_, D_MODEL) @ w_o_mem[l]

        x = x + 0.5 * swiglu(rms_norm(x, ffn2_norm[l]), ffn2_gate[l], ffn2_up[l], ffn2_down[l])

    return rms_norm(x, final_norm)


import jax as _jax
import jax.numpy as _jnp

TWIN_FORMAT = 'train_step'
FWD_PARAMS = ['x', 'mem', 'ffn1_norm', 'ffn1_gate', 'ffn1_up', 'ffn1_down', 'mix_norm', 'w_in', 'lb_param', 'hgrn_out_norm', 'conv_w', 'w_out', 'xattn_norm', 'mem_norm', 'w_q_mem', 'w_kv_mem', 'w_o_mem', 'ffn2_norm', 'ffn2_gate', 'ffn2_up', 'ffn2_down', 'final_norm']
TWIN_WEIGHTS = ['ffn1_norm', 'ffn1_gate', 'ffn1_up', 'ffn1_down', 'mix_norm', 'w_in', 'lb_param', 'hgrn_out_norm', 'conv_w', 'w_out', 'xattn_norm', 'mem_norm', 'w_q_mem', 'w_kv_mem', 'w_o_mem', 'ffn2_norm', 'ffn2_gate', 'ffn2_up', 'ffn2_down', 'final_norm']
TWIN_DIFF_INPUT = 'x'
TWIN_INPUTS = ['x', 'mem', 'ffn1_norm', 'ffn1_gate', 'ffn1_up', 'ffn1_down', 'mix_norm', 'w_in', 'lb_param', 'hgrn_out_norm', 'conv_w', 'w_out', 'xattn_norm', 'mem_norm', 'w_q_mem', 'w_kv_mem', 'w_o_mem', 'ffn2_norm', 'ffn2_gate', 'ffn2_up', 'ffn2_down', 'final_norm', 'loss_target', 'm_ffn1_norm', 'm_ffn1_gate', 'm_ffn1_up', 'm_ffn1_down', 'm_mix_norm', 'm_w_in', 'm_lb_param', 'm_hgrn_out_norm', 'm_conv_w', 'm_w_out', 'm_xattn_norm', 'm_mem_norm', 'm_w_q_mem', 'm_w_kv_mem', 'm_w_o_mem', 'm_ffn2_norm', 'm_ffn2_gate', 'm_ffn2_up', 'm_ffn2_down', 'm_final_norm', 'v_ffn1_norm', 'v_ffn1_gate', 'v_ffn1_up', 'v_ffn1_down', 'v_mix_norm', 'v_w_in', 'v_lb_param', 'v_hgrn_out_norm', 'v_conv_w', 'v_w_out', 'v_xattn_norm', 'v_mem_norm', 'v_w_q_mem', 'v_w_kv_mem', 'v_w_o_mem', 'v_ffn2_norm', 'v_ffn2_gate', 'v_ffn2_up', 'v_ffn2_down', 'v_final_norm']
TWIN_OUTPUTS = ['loss', 'grad_x', 'grad_ffn1_norm', 'grad_ffn1_gate', 'grad_ffn1_up', 'grad_ffn1_down', 'grad_mix_norm', 'grad_w_in', 'grad_lb_param', 'grad_hgrn_out_norm', 'grad_conv_w', 'grad_w_out', 'grad_xattn_norm', 'grad_mem_norm', 'grad_w_q_mem', 'grad_w_kv_mem', 'grad_w_o_mem', 'grad_ffn2_norm', 'grad_ffn2_gate', 'grad_ffn2_up', 'grad_ffn2_down', 'grad_final_norm', 'delta_ffn1_norm', 'delta_ffn1_gate', 'delta_ffn1_up', 'delta_ffn1_down', 'delta_mix_norm', 'delta_w_in', 'delta_lb_param', 'delta_hgrn_out_norm', 'delta_conv_w', 'delta_w_out', 'delta_xattn_norm', 'delta_mem_norm', 'delta_w_q_mem', 'delta_w_kv_mem', 'delta_w_o_mem', 'delta_ffn2_norm', 'delta_ffn2_gate', 'delta_ffn2_up', 'delta_ffn2_down', 'delta_final_norm', 'new_m_ffn1_norm', 'new_m_ffn1_gate', 'new_m_ffn1_up', 'new_m_ffn1_down', 'new_m_mix_norm', 'new_m_w_in', 'new_m_lb_param', 'new_m_hgrn_out_norm', 'new_m_conv_w', 'new_m_w_out', 'new_m_xattn_norm', 'new_m_mem_norm', 'new_m_w_q_mem', 'new_m_w_kv_mem', 'new_m_w_o_mem', 'new_m_ffn2_norm', 'new_m_ffn2_gate', 'new_m_ffn2_up', 'new_m_ffn2_down', 'new_m_final_norm', 'new_v_ffn1_norm', 'new_v_ffn1_gate', 'new_v_ffn1_up', 'new_v_ffn1_down', 'new_v_mix_norm', 'new_v_w_in', 'new_v_lb_param', 'new_v_hgrn_out_norm', 'new_v_conv_w', 'new_v_w_out', 'new_v_xattn_norm', 'new_v_mem_norm', 'new_v_w_q_mem', 'new_v_w_kv_mem', 'new_v_w_o_mem', 'new_v_ffn2_norm', 'new_v_ffn2_gate', 'new_v_ffn2_up', 'new_v_ffn2_down', 'new_v_final_norm']
TWIN_LEAF_KINDS = {'loss': 'loss', 'grad_x': 'grad_x', 'grad_ffn1_norm': 'grad_w', 'grad_ffn1_gate': 'grad_w', 'grad_ffn1_up': 'grad_w', 'grad_ffn1_down': 'grad_w', 'grad_mix_norm': 'grad_w', 'grad_w_in': 'grad_w', 'grad_lb_param': 'grad_w', 'grad_hgrn_out_norm': 'grad_w', 'grad_conv_w': 'grad_w', 'grad_w_out': 'grad_w', 'grad_xattn_norm': 'grad_w', 'grad_mem_norm': 'grad_w', 'grad_w_q_mem': 'grad_w', 'grad_w_kv_mem': 'grad_w', 'grad_w_o_mem': 'grad_w', 'grad_ffn2_norm': 'grad_w', 'grad_ffn2_gate': 'grad_w', 'grad_ffn2_up': 'grad_w', 'grad_ffn2_down': 'grad_w', 'grad_final_norm': 'grad_w', 'delta_ffn1_norm': 'delta_w', 'delta_ffn1_gate': 'delta_w', 'delta_ffn1_up': 'delta_w', 'delta_ffn1_down': 'delta_w', 'delta_mix_norm': 'delta_w', 'delta_w_in': 'delta_w', 'delta_lb_param': 'delta_w', 'delta_hgrn_out_norm': 'delta_w', 'delta_conv_w': 'delta_w', 'delta_w_out': 'delta_w', 'delta_xattn_norm': 'delta_w', 'delta_mem_norm': 'delta_w', 'delta_w_q_mem': 'delta_w', 'delta_w_kv_mem': 'delta_w', 'delta_w_o_mem': 'delta_w', 'delta_ffn2_norm': 'delta_w', 'delta_ffn2_gate': 'delta_w', 'delta_ffn2_up': 'delta_w', 'delta_ffn2_down': 'delta_w', 'delta_final_norm': 'delta_w', 'new_m_ffn1_norm': 'new_m', 'new_m_ffn1_gate': 'new_m', 'new_m_ffn1_up': 'new_m', 'new_m_ffn1_down': 'new_m', 'new_m_mix_norm': 'new_m', 'new_m_w_in': 'new_m', 'new_m_lb_param': 'new_m', 'new_m_hgrn_out_norm': 'new_m', 'new_m_conv_w': 'new_m', 'new_m_w_out': 'new_m', 'new_m_xattn_norm': 'new_m', 'new_m_mem_norm': 'new_m', 'new_m_w_q_mem': 'new_m', 'new_m_w_kv_mem': 'new_m', 'new_m_w_o_mem': 'new_m', 'new_m_ffn2_norm': 'new_m', 'new_m_ffn2_gate': 'new_m', 'new_m_ffn2_up': 'new_m', 'new_m_ffn2_down': 'new_m', 'new_m_final_norm': 'new_m', 'new_v_ffn1_norm': 'new_v', 'new_v_ffn1_gate': 'new_v', 'new_v_ffn1_up': 'new_v', 'new_v_ffn1_down': 'new_v', 'new_v_mix_norm': 'new_v', 'new_v_w_in': 'new_v', 'new_v_lb_param': 'new_v', 'new_v_hgrn_out_norm': 'new_v', 'new_v_conv_w': 'new_v', 'new_v_w_out': 'new_v', 'new_v_xattn_norm': 'new_v', 'new_v_mem_norm': 'new_v', 'new_v_w_q_mem': 'new_v', 'new_v_w_kv_mem': 'new_v', 'new_v_w_o_mem': 'new_v', 'new_v_ffn2_norm': 'new_v', 'new_v_ffn2_gate': 'new_v', 'new_v_ffn2_up': 'new_v', 'new_v_ffn2_down': 'new_v', 'new_v_final_norm': 'new_v'}


def _forward(args):
    return _fwd_reference(*[args[k] for k in FWD_PARAMS])


def _output_shape():
    def fwd():
        inp = _fwd_setup_inputs(0)
        return _fwd_reference(*[inp[k] for k in FWD_PARAMS])
    out = _jax.eval_shape(fwd)
    return out.shape, out.dtype

N_MICROBATCH = 1
ADAM_LR = 0.001
ADAM_B1 = 0.9
ADAM_B2 = 0.999
ADAM_EPS = 1e-08
ADAM_WD = 0.01
ADAM_STEP = 10
PER_EXAMPLE_BATCH_AXIS = {'x': 0, 'mem': 0, 'loss_target': 0}
SHARED_INPUTS = []
_WEIGHT_DTYPES = {'ffn1_norm': _jnp.float32, 'ffn1_gate': _jnp.float32, 'ffn1_up': _jnp.float32, 'ffn1_down': _jnp.float32, 'mix_norm': _jnp.float32, 'w_in': _jnp.float32, 'lb_param': _jnp.float32, 'hgrn_out_norm': _jnp.float32, 'conv_w': _jnp.float32, 'w_out': _jnp.float32, 'xattn_norm': _jnp.float32, 'mem_norm': _jnp.float32, 'w_q_mem': _jnp.float32, 'w_kv_mem': _jnp.float32, 'w_o_mem': _jnp.float32, 'ffn2_norm': _jnp.float32, 'ffn2_gate': _jnp.float32, 'ffn2_up': _jnp.float32, 'ffn2_down': _jnp.float32, 'final_norm': _jnp.float32}
MOMENT_SCALE = {'ffn1_norm': 9.906454e-02, 'ffn1_gate': 4.181253e-02, 'ffn1_up': 4.056599e-02, 'ffn1_down': 6.718514e-02, 'mix_norm': 1.875752e-01, 'w_in': 9.959313e-02, 'lb_param': 7.820576e-03, 'hgrn_out_norm': 8.926193e-02, 'conv_w': 1.412343e-01, 'w_out': 1.123624e-01, 'xattn_norm': 1.516025e-02, 'mem_norm': 2.163697e-02, 'w_q_mem': 1.440729e-02, 'w_kv_mem': 1.451623e-02, 'w_o_mem': 1.459267e-02, 'ffn2_norm': 5.776966e-02, 'ffn2_gate': 2.424535e-02, 'ffn2_up': 2.349354e-02, 'ffn2_down': 3.909551e-02, 'final_norm': 3.201317e+01}


def _to_microbatches(a, axis):
    t = _jnp.moveaxis(a, axis, 0)
    t = t.reshape((N_MICROBATCH, t.shape[0] // N_MICROBATCH) + t.shape[1:])
    return _jnp.moveaxis(t, 1, axis + 1)


def setup_inputs(seed: int = 0) -> dict:
    inp = _fwd_setup_inputs(seed)
    key = _jax.random.fold_in(_jax.random.key(seed), 7919)
    shape, _ = _output_shape()
    out = dict(inp)
    out["loss_target"] = _jax.random.normal(_jax.random.fold_in(key, 0), shape, _jnp.float32)
    for i, name in enumerate(TWIN_WEIGHTS):
        w = inp[name].astype(_jnp.float32)
        if MOMENT_SCALE is None:
            s = _jnp.sqrt(_jnp.mean(_jnp.square(w)) + 1e-30)
        else:
            s = MOMENT_SCALE[name]
        km, kv = _jax.random.split(_jax.random.fold_in(key, i + 1))
        out[name] = w
        out["m_" + name] = s * _jax.random.normal(km, w.shape, _jnp.float32)
        out["v_" + name] = (s * s) * _jax.random.uniform(kv, w.shape, _jnp.float32, 0.5, 1.5)
    if N_MICROBATCH > 1:
        for name, axis in PER_EXAMPLE_BATCH_AXIS.items():
            out[name] = _to_microbatches(out[name], axis)
    return {'x': out['x'], 'mem': out['mem'], 'ffn1_norm': out['ffn1_norm'], 'ffn1_gate': out['ffn1_gate'], 'ffn1_up': out['ffn1_up'], 'ffn1_down': out['ffn1_down'], 'mix_norm': out['mix_norm'], 'w_in': out['w_in'], 'lb_param': out['lb_param'], 'hgrn_out_norm': out['hgrn_out_norm'], 'conv_w': out['conv_w'], 'w_out': out['w_out'], 'xattn_norm': out['xattn_norm'], 'mem_norm': out['mem_norm'], 'w_q_mem': out['w_q_mem'], 'w_kv_mem': out['w_kv_mem'], 'w_o_mem': out['w_o_mem'], 'ffn2_norm': out['ffn2_norm'], 'ffn2_gate': out['ffn2_gate'], 'ffn2_up': out['ffn2_up'], 'ffn2_down': out['ffn2_down'], 'final_norm': out['final_norm'], 'loss_target': out['loss_target'], 'm_ffn1_norm': out['m_ffn1_norm'], 'm_ffn1_gate': out['m_ffn1_gate'], 'm_ffn1_up': out['m_ffn1_up'], 'm_ffn1_down': out['m_ffn1_down'], 'm_mix_norm': out['m_mix_norm'], 'm_w_in': out['m_w_in'], 'm_lb_param': out['m_lb_param'], 'm_hgrn_out_norm': out['m_hgrn_out_norm'], 'm_conv_w': out['m_conv_w'], 'm_w_out': out['m_w_out'], 'm_xattn_norm': out['m_xattn_norm'], 'm_mem_norm': out['m_mem_norm'], 'm_w_q_mem': out['m_w_q_mem'], 'm_w_kv_mem': out['m_w_kv_mem'], 'm_w_o_mem': out['m_w_o_mem'], 'm_ffn2_norm': out['m_ffn2_norm'], 'm_ffn2_gate': out['m_ffn2_gate'], 'm_ffn2_up': out['m_ffn2_up'], 'm_ffn2_down': out['m_ffn2_down'], 'm_final_norm': out['m_final_norm'], 'v_ffn1_norm': out['v_ffn1_norm'], 'v_ffn1_gate': out['v_ffn1_gate'], 'v_ffn1_up': out['v_ffn1_up'], 'v_ffn1_down': out['v_ffn1_down'], 'v_mix_norm': out['v_mix_norm'], 'v_w_in': out['v_w_in'], 'v_lb_param': out['v_lb_param'], 'v_hgrn_out_norm': out['v_hgrn_out_norm'], 'v_conv_w': out['v_conv_w'], 'v_w_out': out['v_w_out'], 'v_xattn_norm': out['v_xattn_norm'], 'v_mem_norm': out['v_mem_norm'], 'v_w_q_mem': out['v_w_q_mem'], 'v_w_kv_mem': out['v_w_kv_mem'], 'v_w_o_mem': out['v_w_o_mem'], 'v_ffn2_norm': out['v_ffn2_norm'], 'v_ffn2_gate': out['v_ffn2_gate'], 'v_ffn2_up': out['v_ffn2_up'], 'v_ffn2_down': out['v_ffn2_down'], 'v_final_norm': out['v_final_norm']}


def _loss(weights, diff, rest, loss_target):
    with _jax.named_scope("forward"):
        args = {**rest, TWIN_DIFF_INPUT: diff, **{k: w.astype(_WEIGHT_DTYPES[k]) for k, w in weights.items()}}
        y = _forward(args)
    with _jax.named_scope("loss_head"):
        err = _jnp.square(y.astype(_jnp.float32) - loss_target)
        return 0.5 * _jnp.sum(_jnp.mean(err, axis=-1)) if err.ndim else 0.5 * err


def _adamw(w, g, m, v):
    m = ADAM_B1 * m + (1.0 - ADAM_B1) * g
    v = ADAM_B2 * v + (1.0 - ADAM_B2) * _jnp.square(g)
    m_hat = m / (1.0 - ADAM_B1 ** ADAM_STEP)
    v_hat = v / (1.0 - ADAM_B2 ** ADAM_STEP)
    delta = -ADAM_LR * (m_hat / (_jnp.sqrt(v_hat) + ADAM_EPS) + ADAM_WD * w)
    return delta, m, v


def reference(x, mem, ffn1_norm, ffn1_gate, ffn1_up, ffn1_down, mix_norm, w_in, lb_param, hgrn_out_norm, conv_w, w_out, xattn_norm, mem_norm, w_q_mem, w_kv_mem, w_o_mem, ffn2_norm, ffn2_gate, ffn2_up, ffn2_down, final_norm, loss_target, m_ffn1_norm, m_ffn1_gate, m_ffn1_up, m_ffn1_down, m_mix_norm, m_w_in, m_lb_param, m_hgrn_out_norm, m_conv_w, m_w_out, m_xattn_norm, m_mem_norm, m_w_q_mem, m_w_kv_mem, m_w_o_mem, m_ffn2_norm, m_ffn2_gate, m_ffn2_up, m_ffn2_down, m_final_norm, v_ffn1_norm, v_ffn1_gate, v_ffn1_up, v_ffn1_down, v_mix_norm, v_w_in, v_lb_param, v_hgrn_out_norm, v_conv_w, v_w_out, v_xattn_norm, v_mem_norm, v_w_q_mem, v_w_kv_mem, v_w_o_mem, v_ffn2_norm, v_ffn2_gate, v_ffn2_up, v_ffn2_down, v_final_norm):
    given = dict(x=x, mem=mem, ffn1_norm=ffn1_norm, ffn1_gate=ffn1_gate, ffn1_up=ffn1_up, ffn1_down=ffn1_down, mix_norm=mix_norm, w_in=w_in, lb_param=lb_param, hgrn_out_norm=hgrn_out_norm, conv_w=conv_w, w_out=w_out, xattn_norm=xattn_norm, mem_norm=mem_norm, w_q_mem=w_q_mem, w_kv_mem=w_kv_mem, w_o_mem=w_o_mem, ffn2_norm=ffn2_norm, ffn2_gate=ffn2_gate, ffn2_up=ffn2_up, ffn2_down=ffn2_down, final_norm=final_norm, loss_target=loss_target, m_ffn1_norm=m_ffn1_norm, m_ffn1_gate=m_ffn1_gate, m_ffn1_up=m_ffn1_up, m_ffn1_down=m_ffn1_down, m_mix_norm=m_mix_norm, m_w_in=m_w_in, m_lb_param=m_lb_param, m_hgrn_out_norm=m_hgrn_out_norm, m_conv_w=m_conv_w, m_w_out=m_w_out, m_xattn_norm=m_xattn_norm, m_mem_norm=m_mem_norm, m_w_q_mem=m_w_q_mem, m_w_kv_mem=m_w_kv_mem, m_w_o_mem=m_w_o_mem, m_ffn2_norm=m_ffn2_norm, m_ffn2_gate=m_ffn2_gate, m_ffn2_up=m_ffn2_up, m_ffn2_down=m_ffn2_down, m_final_norm=m_final_norm, v_ffn1_norm=v_ffn1_norm, v_ffn1_gate=v_ffn1_gate, v_ffn1_up=v_ffn1_up, v_ffn1_down=v_ffn1_down, v_mix_norm=v_mix_norm, v_w_in=v_w_in, v_lb_param=v_lb_param, v_hgrn_out_norm=v_hgrn_out_norm, v_conv_w=v_conv_w, v_w_out=v_w_out, v_xattn_norm=v_xattn_norm, v_mem_norm=v_mem_norm, v_w_q_mem=v_w_q_mem, v_w_kv_mem=v_w_kv_mem, v_w_o_mem=v_w_o_mem, v_ffn2_norm=v_ffn2_norm, v_ffn2_gate=v_ffn2_gate, v_ffn2_up=v_ffn2_up, v_ffn2_down=v_ffn2_down, v_final_norm=v_final_norm)
    weights = {n: given[n] for n in TWIN_WEIGHTS}
    shared = {n: given[n] for n in SHARED_INPUTS}
    per_example = {n: given[n] for n in ['x', 'mem']}
    grad_fn = _jax.value_and_grad(_loss, argnums=(0, 1))

    def one_microbatch(ex, loss_target):
        ex = dict(ex)
        diff = ex.pop(TWIN_DIFF_INPUT)
        return grad_fn(weights, diff, {**shared, **ex}, loss_target)

    if N_MICROBATCH == 1:
        loss, (grad_w, grad_x) = one_microbatch(per_example, given["loss_target"])
    else:
        def body(carry, xs):
            loss_sum, grad_sum = carry
            l_k, (gw_k, gx_k) = one_microbatch(xs[0], xs[1])
            with _jax.named_scope("update"):
                return (loss_sum + l_k, _jax.tree.map(_jnp.add, grad_sum, gw_k)), gx_k

        init = (_jnp.zeros((), _jnp.float32), _jax.tree.map(_jnp.zeros_like, weights))
        (loss, grad_w), grad_x = _jax.lax.scan(body, init, (per_example, given["loss_target"]))
    with _jax.named_scope("update"):
        delta_w, new_m, new_v = {}, {}, {}
        for n in TWIN_WEIGHTS:
            delta_w[n], new_m[n], new_v[n] = _adamw(weights[n], grad_w[n], given["m_" + n], given["v_" + n])
    return (loss, grad_x, *[grad_w[n] for n in TWIN_WEIGHTS], *[delta_w[n] for n in TWIN_WEIGHTS],
            *[new_m[n] for n in TWIN_WEIGHTS], *[new_v[n] for n in TWIN_WEIGHTS])
```

```python
import jax
import jax.numpy as jnp
from jax import lax
from jax.experimental import pallas as pl
from jax.experimental.pallas import tpu as pltpu

F32 = jnp.float32
BF16 = jnp.bfloat16
MESH_IDS = pl.DeviceIdType.MESH

N_DEV = 8
EPS = 1e-6
HGRN_HEADS = 4
HGRN_DK = 128
HGRN_W = 512
CHUNK = 64
MEM_HEADS = 4
MEM_HD = 256
ADAM_LR = 0.001
ADAM_B1 = 0.9
ADAM_B2 = 0.999
ADAM_EPS = 1e-08
ADAM_WD = 0.01
ADAM_STEP = 10

TOKEN_TILE = 256
REDUCE_TILE = 1024
VMEM_LIMIT = 60 * 1024 * 1024
SMALL_ROWS = 16
NT = (((1,), (1,)), ((), ()))
TN = (((0,), (0,)), ((), ()))


def _params(sem=None):
    return pltpu.CompilerParams(dimension_semantics=sem, vmem_limit_bytes=VMEM_LIMIT)


def _dot(a, b, dims=None):
    if dims is None:
        return jnp.dot(a, b, preferred_element_type=F32)
    return lax.dot_general(a, b, dims, preferred_element_type=F32)


def _sigmoid(v):
    return 1.0 / (1.0 + jnp.exp(-v))


def _rms(x, g):
    r = lax.rsqrt(jnp.mean(x * x, axis=-1, keepdims=True) + EPS)
    xh = x * r
    return xh * g, xh, r


def _rms_bwd(dh, xh, r, g):
    dxh = dh * g
    return r * (dxh - xh * jnp.mean(dxh * xh, axis=-1, keepdims=True))


def _full(shape):
    return pl.BlockSpec(shape, lambda *_: (0,) * len(shape))


def _rows(tm, width):
    return pl.BlockSpec((tm, width), lambda i: (i, 0))


def _rows_rev(tm, width, n):
    return pl.BlockSpec((tm, width), lambda i: (n - 1 - i, 0))


def _accumulate(ref, first, value):
    @pl.when(first)
    def _():
        ref[...] = value

    @pl.when(jnp.logical_not(first))
    def _():
        ref[...] += value


def _ffn_fwd(x, g, wg, wu, wd):
    t, d = x.shape
    f = wg.shape[1]
    tm = min(TOKEN_TILE, t)

    def body(x_ref, g_ref, wg_ref, wu_ref, wd_ref, xo_ref, a_ref, b_ref, s_ref):
        xv = x_ref[...]
        h, _, _ = _rms(xv, g_ref[...])
        hb = h.astype(BF16)
        a = _dot(hb, wg_ref[...])
        b = _dot(hb, wu_ref[...])
        s = (a * _sigmoid(a) * b).astype(BF16)
        xo_ref[...] = xv + 0.5 * _dot(s, wd_ref[...])
        a_ref[...] = a.astype(BF16)
        b_ref[...] = b.astype(BF16)
        s_ref[...] = s

    return pl.pallas_call(
        body,
        name="ffn_fwd",
        grid=(t // tm,),
        in_specs=[_rows(tm, d), _full((1, d)), _full((d, f)), _full((d, f)), _full((f, d))],
        out_specs=[_rows(tm, d), _rows(tm, f), _rows(tm, f), _rows(tm, f)],
        out_shape=[
            jax.ShapeDtypeStruct((t, d), F32),
            jax.ShapeDtypeStruct((t, f), BF16),
            jax.ShapeDtypeStruct((t, f), BF16),
            jax.ShapeDtypeStruct((t, f), BF16),
        ],
        compiler_params=_params(("arbitrary",)),
    )(x, g, wg, wu, wd)


def _ffn_bwd(x, g, dxo, a, b, wg, wu, wd):
    t, d = x.shape
    f = wg.shape[1]
    tm = min(TOKEN_TILE, t)

    def body(x_ref, g_ref, dxo_ref, a_ref, b_ref, wg_ref, wu_ref, wd_ref, dx_ref, da_ref, db_ref, h_ref, dg_ref):
        gv = g_ref[...]
        h, xh, r = _rms(x_ref[...], gv)
        dxo = dxo_ref[...]
        ds = _dot((0.5 * dxo).astype(BF16), wd_ref[...], NT)
        af = a_ref[...].astype(F32)
        bf = b_ref[...].astype(F32)
        sg = _sigmoid(af)
        da = (ds * bf * (sg * (1.0 + af * (1.0 - sg)))).astype(BF16)
        db = (ds * (af * sg)).astype(BF16)
        dh = _dot(da, wg_ref[...], NT) + _dot(db, wu_ref[...], NT)
        dx_ref[...] = _rms_bwd(dh, xh, r, gv) + dxo
        da_ref[...] = da
        db_ref[...] = db
        h_ref[...] = h.astype(BF16)
        _accumulate(dg_ref, pl.program_id(0) == 0, jnp.sum(dh * xh, axis=0, keepdims=True))

    return pl.pallas_call(
        body,
        name="ffn_bwd",
        grid=(t // tm,),
        in_specs=[
            _rows(tm, d), _full((1, d)), _rows(tm, d), _rows(tm, f), _rows(tm, f),
            _full((d, f)), _full((d, f)), _full((f, d)),
        ],
        out_specs=[_rows(tm, d), _rows(tm, f), _rows(tm, f), _rows(tm, d), _full((1, d))],
        out_shape=[
            jax.ShapeDtypeStruct((t, d), F32),
            jax.ShapeDtypeStruct((t, f), BF16),
            jax.ShapeDtypeStruct((t, f), BF16),
            jax.ShapeDtypeStruct((t, d), BF16),
            jax.ShapeDtypeStruct((1, d), F32),
        ],
        compiler_params=_params(("arbitrary",)),
    )(x, g, dxo, a, b, wg, wu, wd)


def _matmul_tn(a, b, scale=1.0):
    t, m = a.shape
    n = b.shape[1]
    tk = min(REDUCE_TILE, t)
    nb = n
    while m * nb * 4 > 12 * 1024 * 1024:
        nb //= 2
    assert n % nb == 0 and nb % 128 == 0

    def body(a_ref, b_ref, o_ref):
        bv = b_ref[...]
        if scale != 1.0:
            bv = bv * scale
        part = _dot(a_ref[...].astype(BF16), bv.astype(BF16), TN)
        _accumulate(o_ref, pl.program_id(1) == 0, part)

    return pl.pallas_call(
        body,
        name="matmul_tn",
        grid=(n // nb, t // tk),
        in_specs=[pl.BlockSpec((tk, m), lambda j, k: (k, 0)), pl.BlockSpec((tk, nb), lambda j, k: (k, j))],
        out_specs=pl.BlockSpec((m, nb), lambda j, k: (0, j)),
        out_shape=jax.ShapeDtypeStruct((m, n), F32),
        compiler_params=_params(("parallel", "arbitrary")),
    )(a, b)


def _chunk_cumsum(v, reverse=False):
    n = v.shape[0]
    pos = lax.broadcasted_iota(jnp.int32, (n, 1), 0) % CHUNK
    shift = 1
    while shift < CHUNK:
        if reverse:
            moved = pltpu.roll(v, n - shift, axis=0)
            v = v + jnp.where(pos < CHUNK - shift, moved, 0.0)
        else:
            moved = pltpu.roll(v, shift, axis=0)
            v = v + jnp.where(pos >= shift, moved, 0.0)
        shift *= 2
    return v


def _shift_rows(v, shift, edge):
    n = v.shape[0]
    row = lax.broadcasted_iota(jnp.int32, (n, 1), 0)
    out = pltpu.roll(v, shift % n, axis=0)
    if shift > 0:
        for j in range(shift):
            out = jnp.where(row == j, edge[8 - shift + j:8 - shift + j + 1, :], out)
    else:
        for j in range(-shift):
            out = jnp.where(row == n + shift + j, edge[j:j + 1, :], out)
    return out


def _gates(z, lbp):
    w = HGRN_W
    lb = _sigmoid(lbp[0:1, :] - lbp[1:2, :])
    zq = z[:, 0:w]
    sig = _sigmoid(z[:, w:2 * w])
    f = lb + (1.0 - lb) * sig
    sq = _sigmoid(zq)
    q = zq * sq * HGRN_DK ** -0.5
    return lb, sig, f, sq, q


def _short_conv(u, edge, cw):
    return cw[0:1, :] * _shift_rows(u, 2, edge) + cw[1:2, :] * _shift_rows(u, 1, edge) + cw[2:3, :] * u


def _causal_mask():
    row = lax.broadcasted_iota(jnp.int32, (CHUNK, CHUNK), 0)
    col = lax.broadcasted_iota(jnp.int32, (CHUNK, CHUNK), 1)
    return col <= row


def _mix_fwd(x, g, w_in, lbp, gh, convw_t, w_out):
    t, d = x.shape
    zw = w_in.shape[1]
    w = HGRN_W
    tm = min(TOKEN_TILE, t)
    nc = tm // CHUNK
    n_chunks = t // CHUNK

    def body(x_ref, g_ref, win_ref, lbp_ref, gh_ref, cw_ref, wout_ref,
             xo_ref, z_ref, o_ref, st_ref, y_ref, state, ucarry):
        @pl.when(pl.program_id(0) == 0)
        def _():
            state[...] = jnp.zeros_like(state)
            ucarry[...] = jnp.zeros_like(ucarry)

        xv = x_ref[...]
        h, _, _ = _rms(xv, g_ref[...])
        z_ref[...] = _dot(h.astype(BF16), win_ref[...])
        z = z_ref[...]
        _, _, f, _, q = _gates(z, lbp_ref[...])
        bcum = _chunk_cumsum(jnp.log(f))
        kk = 1.0 - f
        vv = z[:, 2 * w:3 * w]
        mask = _causal_mask()
        for c in range(nc):
            rows = slice(c * CHUNK, (c + 1) * CHUNK)
            for hd in range(HGRN_HEADS):
                cols = slice(hd * HGRN_DK, (hd + 1) * HGRN_DK)
                b = bcum[rows, cols]
                blast = b[CHUNK - 1:CHUNK, :]
                qh = (q[rows, cols] * jnp.exp(b)).astype(BF16)
                kh = (kk[rows, cols] * jnp.exp(-b)).astype(BF16)
                kbar = (kk[rows, cols] * jnp.exp(blast - b)).astype(BF16)
                vb = vv[rows, cols].astype(BF16)
                st = state[hd]
                st_ref[c, hd] = st
                att = jnp.where(mask, _dot(qh, kh, NT), 0.0).astype(BF16)
                o_ref[rows, cols] = _dot(att, vb) + _dot(qh, st.astype(BF16), NT)
                state[hd] = st * jnp.exp(blast) + _dot(vb, kbar, TN)
        ghv = gh_ref[...]
        for hd in range(HGRN_HEADS):
            cols = slice(hd * HGRN_DK, (hd + 1) * HGRN_DK)
            on, _, _ = _rms(o_ref[:, cols], ghv[:, cols])
            zg = z[:, 3 * w + hd * HGRN_DK:3 * w + (hd + 1) * HGRN_DK]
            y_ref[:, cols] = (on * (zg * _sigmoid(zg))).astype(BF16)
        u = z[:, 5 * w:6 * w] * z[:, 6 * w:7 * w]
        conv = _short_conv(u, ucarry[...], cw_ref[...])
        ucarry[...] = u[tm - 8:tm, :]
        y_ref[:, w:2 * w] = (z[:, 4 * w:5 * w] * conv).astype(BF16)
        xo_ref[...] = xv + _dot(y_ref[...], wout_ref[...])

    return pl.pallas_call(
        body,
        name="mix_fwd",
        grid=(t // tm,),
        in_specs=[
            _rows(tm, d), _full((1, d)), _full((d, zw)), _full((2, w)), _full((1, w)), _full((3, w)),
            _full((2 * w, d)),
        ],
        out_specs=[
            _rows(tm, d), _rows(tm, zw), _rows(tm, w),
            pl.BlockSpec((nc, HGRN_HEADS, HGRN_DK, HGRN_DK), lambda i: (i, 0, 0, 0)),
            _rows(tm, 2 * w),
        ],
        out_shape=[
            jax.ShapeDtypeStruct((t, d), F32),
            jax.ShapeDtypeStruct((t, zw), F32),
            jax.ShapeDtypeStruct((t, w), F32),
            jax.ShapeDtypeStruct((n_chunks, HGRN_HEADS, HGRN_DK, HGRN_DK), F32),
            jax.ShapeDtypeStruct((t, 2 * w), BF16),
        ],
        scratch_shapes=[pltpu.VMEM((HGRN_HEADS, HGRN_DK, HGRN_DK), F32), pltpu.VMEM((8, w), F32)],
        compiler_params=_params(("arbitrary",)),
    )(x, g, w_in, lbp, gh, convw_t, w_out)


def _mix_bwd(x, g, dxo, z, o, states, w_in, lbp, gh, convw_t, w_out):
    t, d = x.shape
    zw = w_in.shape[1]
    w = HGRN_W
    tm = min(TOKEN_TILE, t)
    nc = tm // CHUNK
    n = t // tm

    def body(x_ref, g_ref, dxo_ref, z_ref, zprev_ref, o_ref, st_ref, win_ref, lbp_ref, gh_ref, cw_ref, wout_ref,
             dx_ref, dz_ref, h_ref, dg_ref, dlbp_ref, dgh_ref, dcw_ref,
             dstate, dcarry, do_buf, dq_buf, dk_buf, db_buf):
        first = pl.program_id(0) == 0

        @pl.when(first)
        def _():
            dstate[...] = jnp.zeros_like(dstate)
            dcarry[...] = jnp.zeros_like(dcarry)

        gv = g_ref[...]
        h, xh, r = _rms(x_ref[...], gv)
        h_ref[...] = h.astype(BF16)
        dxo = dxo_ref[...]
        dy = _dot(dxo.astype(BF16), wout_ref[...], NT)
        z = z_ref[...]
        lb, sig, f, sq, q = _gates(z, lbp_ref[...])
        bcum = _chunk_cumsum(jnp.log(f))
        kk = 1.0 - f
        vv = z[:, 2 * w:3 * w]

        ghv = gh_ref[...]
        dgh_parts = []
        for hd in range(HGRN_HEADS):
            cols = slice(hd * HGRN_DK, (hd + 1) * HGRN_DK)
            gcols = slice(3 * w + hd * HGRN_DK, 3 * w + (hd + 1) * HGRN_DK)
            on, oh, rr = _rms(o_ref[:, cols], ghv[:, cols])
            zg = z[:, gcols]
            sgz = _sigmoid(zg)
            dyh = dy[:, cols]
            don = dyh * (zg * sgz)
            dz_ref[:, gcols] = (dyh * on * (sgz * (1.0 + zg * (1.0 - sgz)))).astype(BF16)
            dgh_parts.append(jnp.sum(don * oh, axis=0, keepdims=True))
            do_buf[:, cols] = _rms_bwd(don, oh, rr, ghv[:, cols])
        _accumulate(dgh_ref, first, jnp.concatenate(dgh_parts, axis=1))

        zb = z[:, 4 * w:5 * w]
        zc = z[:, 5 * w:6 * w]
        zu = z[:, 6 * w:7 * w]
        u = zc * zu
        cw = cw_ref[...]
        zp = zprev_ref[...]
        uprev = jnp.where(pl.program_id(0) == n - 1, 0.0, zp[:, 5 * w:6 * w] * zp[:, 6 * w:7 * w])
        dyc = dy[:, w:2 * w]
        dz_ref[:, 4 * w:5 * w] = (dyc * _short_conv(u, uprev, cw)).astype(BF16)
        dconv = dyc * zb
        edge = dcarry[...]
        dconv1 = _shift_rows(dconv, -1, edge)
        dconv2 = _shift_rows(dconv, -2, edge)
        dcarry[...] = dconv[0:8, :]
        du = cw[2:3, :] * dconv + cw[1:2, :] * dconv1 + cw[0:1, :] * dconv2
        dz_ref[:, 5 * w:6 * w] = (du * zu).astype(BF16)
        dz_ref[:, 6 * w:7 * w] = (du * zc).astype(BF16)
        _accumulate(dcw_ref, first, jnp.concatenate([
            jnp.sum(u * dconv2, axis=0, keepdims=True),
            jnp.sum(u * dconv1, axis=0, keepdims=True),
            jnp.sum(u * dconv, axis=0, keepdims=True)], axis=0))

        mask = _causal_mask()
        last_row = lax.broadcasted_iota(jnp.int32, (CHUNK, 1), 0) == CHUNK - 1
        for c in reversed(range(nc)):
            rows = slice(c * CHUNK, (c + 1) * CHUNK)
            for hd in range(HGRN_HEADS):
                cols = slice(hd * HGRN_DK, (hd + 1) * HGRN_DK)
                b = bcum[rows, cols]
                blast = b[CHUNK - 1:CHUNK, :]
                eb = jnp.exp(b)
                enb = jnp.exp(-b)
                erest = jnp.exp(blast - b)
                elast = jnp.exp(blast)
                qh = q[rows, cols] * eb
                kh = kk[rows, cols] * enb
                kbar = kk[rows, cols] * erest
                qhb = qh.astype(BF16)
                khb = kh.astype(BF16)
                kbarb = kbar.astype(BF16)
                vb = vv[rows, cols].astype(BF16)
                st = st_ref[c, hd]
                dst = dstate[hd]
                dstb = dst.astype(BF16)
                dob = do_buf[rows, cols].astype(BF16)
                att = jnp.where(mask, _dot(qhb, khb, NT), 0.0).astype(BF16)
                datt = jnp.where(mask, _dot(dob, vb, NT), 0.0).astype(BF16)
                dv = _dot(att, dob, TN) + _dot(kbarb, dstb, NT)
                dqh = _dot(datt, khb) + _dot(dob, st.astype(BF16))
                dkh = _dot(datt, qhb, TN)
                dkbar = _dot(vb, dstb)
                dstate[hd] = dst * elast + _dot(dob, qhb, TN)
                kbar_dkbar = kbarb.astype(F32) * dkbar
                db = qhb.astype(F32) * dqh - khb.astype(F32) * dkh - kbar_dkbar
                db_last = (jnp.sum(kbar_dkbar, axis=0, keepdims=True)
                           + jnp.sum(dst * st, axis=0, keepdims=True) * elast)
                db_buf[rows, cols] = jnp.where(last_row, db + db_last, db)
                dq_buf[rows, cols] = dqh * eb
                dk_buf[rows, cols] = dkh * enb + dkbar * erest
                dz_ref[rows, 2 * w + hd * HGRN_DK:2 * w + (hd + 1) * HGRN_DK] = dv.astype(BF16)

        dlogf = _chunk_cumsum(db_buf[...], reverse=True)
        df = dlogf / f - dk_buf[...]
        zq = z[:, 0:w]
        dz_ref[:, 0:w] = (dq_buf[...] * HGRN_DK ** -0.5 * (sq * (1.0 + zq * (1.0 - sq)))).astype(BF16)
        dz_ref[:, w:2 * w] = (df * (1.0 - lb) * sig * (1.0 - sig)).astype(BF16)
        dlb = jnp.sum(df * (1.0 - sig), axis=0, keepdims=True) * lb * (1.0 - lb)
        _accumulate(dlbp_ref, first, jnp.concatenate([dlb, -dlb], axis=0))

        dh = _dot(dz_ref[...], win_ref[...], NT)
        dx_ref[...] = _rms_bwd(dh, xh, r, gv) + dxo
        _accumulate(dg_ref, first, jnp.sum(dh * xh, axis=0, keepdims=True))

    return pl.pallas_call(
        body,
        name="mix_bwd",
        grid=(n,),
        in_specs=[
            _rows_rev(tm, d, n), _full((1, d)), _rows_rev(tm, d, n), _rows_rev(tm, zw, n),
            pl.BlockSpec((8, zw), lambda i: (jnp.maximum((n - 1 - i) * (tm // 8) - 1, 0), 0)),
            _rows_rev(tm, w, n),
            pl.BlockSpec((nc, HGRN_HEADS, HGRN_DK, HGRN_DK), lambda i: (n - 1 - i, 0, 0, 0)),
            _full((d, zw)), _full((2, w)), _full((1, w)), _full((3, w)), _full((2 * w, d)),
        ],
        out_specs=[
            _rows_rev(tm, d, n), _rows_rev(tm, zw, n), _rows_rev(tm, d, n),
            _full((1, d)), _full((2, w)), _full((1, w)), _full((3, w)),
        ],
        out_shape=[
            jax.ShapeDtypeStruct((t, d), F32),
            jax.ShapeDtypeStruct((t, zw), BF16),
            jax.ShapeDtypeStruct((t, d), BF16),
            jax.ShapeDtypeStruct((1, d), F32),
            jax.ShapeDtypeStruct((2, w), F32),
            jax.ShapeDtypeStruct((1, w), F32),
            jax.ShapeDtypeStruct((3, w), F32),
        ],
        scratch_shapes=[
            pltpu.VMEM((HGRN_HEADS, HGRN_DK, HGRN_DK), F32), pltpu.VMEM((8, w), F32),
            pltpu.VMEM((tm, w), F32), pltpu.VMEM((tm, w), F32), pltpu.VMEM((tm, w), F32), pltpu.VMEM((tm, w), F32),
        ],
        compiler_params=_params(("arbitrary",)),
    )(x, g, dxo, z, z, o, states, w_in, lbp, gh, convw_t, w_out)


def _memkv_fwd(mem, g, wkv):
    m, d = mem.shape
    nb, _, cb = wkv.shape

    def body(mem_ref, g_ref, wkv_ref, kv_ref):
        mn, _, _ = _rms(mem_ref[...], g_ref[...])
        mnb = mn.astype(BF16)
        for j in range(nb):
            kv_ref[:, j * cb:(j + 1) * cb] = _dot(mnb, wkv_ref[j]).astype(BF16)

    return pl.pallas_call(
        body,
        name="memkv_fwd",
        out_shape=jax.ShapeDtypeStruct((m, nb * cb), BF16),
        compiler_params=_params(),
    )(mem, g, wkv)


def _memkv_bwd(mem, g, dkv, wkv):
    m, d = mem.shape
    nb, _, cb = wkv.shape

    def body(mem_ref, g_ref, dkv_ref, wkv_ref, dw_ref, dg_ref):
        mn, xh, _ = _rms(mem_ref[...], g_ref[...])
        mnb = mn.astype(BF16)
        dmn = jnp.zeros((m, d), F32)
        for j in range(nb):
            dkvb = dkv_ref[:, j * cb:(j + 1) * cb].astype(BF16)
            dw_ref[j] = _dot(mnb, dkvb, TN)
            dmn = dmn + _dot(dkvb, wkv_ref[j], NT)
        dg_ref[...] = jnp.sum(dmn * xh, axis=0, keepdims=True)

    return pl.pallas_call(
        body,
        name="memkv_bwd",
        out_shape=[jax.ShapeDtypeStruct((nb, d, cb), F32), jax.ShapeDtypeStruct((1, d), F32)],
        compiler_params=_params(),
    )(mem, g, dkv, wkv)


def _softmax_rows(qm_h, k_h):
    sc = _dot(qm_h, k_h, NT) * MEM_HD ** -0.5
    e = jnp.exp(sc - jnp.max(sc, axis=-1, keepdims=True))
    return e / jnp.sum(e, axis=-1, keepdims=True)


def _xattn_fwd(x, g, wq, kv, wo):
    t, d = x.shape
    m = kv.shape[0]
    tm = min(TOKEN_TILE, t)

    def body(x_ref, g_ref, wq_ref, kv_ref, wo_ref, xo_ref, hq_ref, qm_ref, att_ref):
        xv = x_ref[...]
        h, _, _ = _rms(xv, g_ref[...])
        hq_ref[...] = h.astype(BF16)
        qm_ref[...] = _dot(hq_ref[...], wq_ref[...]).astype(BF16)
        for hd in range(MEM_HEADS):
            cols = slice(hd * MEM_HD, (hd + 1) * MEM_HD)
            p = _softmax_rows(qm_ref[:, cols], kv_ref[:, cols])
            att_ref[:, cols] = _dot(p.astype(BF16), kv_ref[:, d + hd * MEM_HD:d + (hd + 1) * MEM_HD]).astype(BF16)
        xo_ref[...] = xv + _dot(att_ref[...], wo_ref[...])

    return pl.pallas_call(
        body,
        name="xattn_fwd",
        grid=(t // tm,),
        in_specs=[_rows(tm, d), _full((1, d)), _full((d, d)), _full((m, 2 * d)), _full((d, d))],
        out_specs=[_rows(tm, d), _rows(tm, d), _rows(tm, d), _rows(tm, d)],
        out_shape=[
            jax.ShapeDtypeStruct((t, d), F32),
            jax.ShapeDtypeStruct((t, d), BF16),
            jax.ShapeDtypeStruct((t, d), BF16),
            jax.ShapeDtypeStruct((t, d), BF16),
        ],
        compiler_params=_params(("arbitrary",)),
    )(x, g, wq, kv, wo)


def _xattn_bwd(x, g, dxo, qm, kv, wq, wo):
    t, d = x.shape
    m = kv.shape[0]
    tm = min(TOKEN_TILE, t)

    def body(x_ref, g_ref, dxo_ref, qm_ref, kv_ref, wq_ref, wo_ref, dx_ref, dqm_ref, dkv_ref, dg_ref):
        first = pl.program_id(0) == 0

        @pl.when(first)
        def _():
            dkv_ref[...] = jnp.zeros_like(dkv_ref)

        gv = g_ref[...]
        _, xh, r = _rms(x_ref[...], gv)
        dxo = dxo_ref[...]
        datt = _dot(dxo.astype(BF16), wo_ref[...], NT).astype(BF16)
        for hd in range(MEM_HEADS):
            cols = slice(hd * MEM_HD, (hd + 1) * MEM_HD)
            vcols = slice(d + hd * MEM_HD, d + (hd + 1) * MEM_HD)
            qm_h = qm_ref[:, cols]
            p = _softmax_rows(qm_h, kv_ref[:, cols])
            datt_h = datt[:, cols]
            dp = _dot(datt_h, kv_ref[:, vcols], NT)
            dsc = (p * (dp - jnp.sum(p * dp, axis=-1, keepdims=True)) * MEM_HD ** -0.5).astype(BF16)
            dqm_ref[:, cols] = _dot(dsc, kv_ref[:, cols]).astype(BF16)
            dkv_ref[:, cols] += _dot(dsc, qm_h, TN)
            dkv_ref[:, vcols] += _dot(p.astype(BF16), datt_h, TN)
        dh = _dot(dqm_ref[...], wq_ref[...], NT)
        dx_ref[...] = _rms_bwd(dh, xh, r, gv) + dxo
        _accumulate(dg_ref, first, jnp.sum(dh * xh, axis=0, keepdims=True))

    return pl.pallas_call(
        body,
        name="xattn_bwd",
        grid=(t // tm,),
        in_specs=[
            _rows(tm, d), _full((1, d)), _rows(tm, d), _rows(tm, d), _full((m, 2 * d)), _full((d, d)), _full((d, d)),
        ],
        out_specs=[_rows(tm, d), _rows(tm, d), _full((m, 2 * d)), _full((1, d))],
        out_shape=[
            jax.ShapeDtypeStruct((t, d), F32),
            jax.ShapeDtypeStruct((t, d), BF16),
            jax.ShapeDtypeStruct((m, 2 * d), F32),
            jax.ShapeDtypeStruct((1, d), F32),
        ],
        compiler_params=_params(("arbitrary",)),
    )(x, g, dxo, qm, kv, wq, wo)


def _final_loss(x, g, target):
    t, d = x.shape
    tm = min(TOKEN_TILE, t)

    def body(x_ref, g_ref, tgt_ref, dx_ref, loss_ref, dg_ref):
        first = pl.program_id(0) == 0
        gv = g_ref[...]
        y, xh, r = _rms(x_ref[...], gv)
        err = y - tgt_ref[...]
        dy = err * (1.0 / d)
        dx_ref[...] = _rms_bwd(dy, xh, r, gv)
        part = 0.5 * jnp.sum(jnp.sum(err * err, axis=-1, keepdims=True) * (1.0 / d), axis=0, keepdims=True)
        _accumulate(loss_ref, first, jnp.broadcast_to(part, (1, 128)))
        _accumulate(dg_ref, first, jnp.sum(dy * xh, axis=0, keepdims=True))

    return pl.pallas_call(
        body,
        name="final_loss",
        grid=(t // tm,),
        in_specs=[_rows(tm, d), _full((1, d)), _rows(tm, d)],
        out_specs=[_rows(tm, d), _full((1, 128)), _full((1, d))],
        out_shape=[
            jax.ShapeDtypeStruct((t, d), F32),
            jax.ShapeDtypeStruct((1, 128), F32),
            jax.ShapeDtypeStruct((1, d), F32),
        ],
        compiler_params=_params(("arbitrary",)),
    )(x, g, target)


def _mesh_place():
    x, y, c = lax.axis_index("x"), lax.axis_index("y"), lax.axis_index("c")
    return x, y, c, 4 * x + 2 * y + c


def _peer(x, y, c, k):
    px = 1 - x if k & 4 else x
    py = 1 - y if k & 2 else y
    pc = 1 - c if k & 1 else c
    return (px, py, pc), 4 * px + 2 * py + pc


def _all_gather(shards):
    n = len(shards)

    def body(*refs):
        src, dst = refs[:n], refs[n:2 * n]
        send_sems, recv_sems, local_sems = refs[2 * n:]
        x, y, c, me = _mesh_place()
        started = []
        for a in range(n):
            local = pltpu.make_async_copy(src[a], dst[a].at[me], local_sems.at[a])
            local.start()
            started.append(local)
            for k in range(1, N_DEV):
                peer, _ = _peer(x, y, c, k)
                pltpu.make_async_remote_copy(
                    src_ref=src[a], dst_ref=dst[a].at[me], send_sem=send_sems.at[a, k - 1],
                    recv_sem=recv_sems.at[a, k - 1], device_id=peer, device_id_type=MESH_IDS).start()
        for a in range(n):
            started[a].wait()
            for k in range(1, N_DEV):
                peer, peer_index = _peer(x, y, c, k)
                landed = pltpu.make_async_remote_copy(
                    src_ref=src[a], dst_ref=dst[a].at[peer_index], send_sem=send_sems.at[a, k - 1],
                    recv_sem=recv_sems.at[a, k - 1], device_id=peer, device_id_type=MESH_IDS)
                landed.wait_send()
                landed.wait_recv()

    hbm = pl.BlockSpec(memory_space=pltpu.HBM)
    return pl.pallas_call(
        body,
        name="all_gather",
        in_specs=[hbm] * n,
        out_specs=[hbm] * n,
        out_shape=[jax.ShapeDtypeStruct((N_DEV,) + s.shape, s.dtype) for s in shards],
        scratch_shapes=[
            pltpu.SemaphoreType.DMA((n, N_DEV - 1)), pltpu.SemaphoreType.DMA((n, N_DEV - 1)),
            pltpu.SemaphoreType.DMA((n,)),
        ],
        compiler_params=pltpu.CompilerParams(has_side_effects=True),
    )(*shards)


SMALL_LAYOUT = {
    "ffn1_norm": (0, 1, 1024), "mix_norm": (1, 1, 1024), "xattn_norm": (2, 1, 1024), "mem_norm": (3, 1, 1024),
    "ffn2_norm": (4, 1, 1024), "final_norm": (5, 1, 1024), "lb_param": (6, 2, 512), "hgrn_out_norm": (8, 1, 512),
    "conv_w": (9, 3, 512), "loss": (12, 1, 128),
}


def _reduce_scatter(blocks, small):
    n = len(blocks)
    names = list(small)
    width = 1024

    def body(*refs):
        src = refs[:n]
        pieces = refs[n:n + len(names)]
        dst = refs[n + len(names):2 * n + len(names)]
        total_ref = refs[2 * n + len(names)]
        pack, gathered, send_sems, recv_sems, local_sems, small_send, small_recv = refs[2 * n + len(names) + 1:]
        x, y, c, me = _mesh_place()
        pack[...] = jnp.zeros_like(pack)
        for name, piece in zip(names, pieces):
            row, nrows, ncols = SMALL_LAYOUT[name]
            pack[row:row + nrows, 0:ncols] = piece[...]
        for k in range(1, N_DEV):
            peer, _ = _peer(x, y, c, k)
            pltpu.make_async_remote_copy(
                src_ref=pack, dst_ref=gathered.at[me], send_sem=small_send.at[k - 1],
                recv_sem=small_recv.at[k - 1], device_id=peer, device_id_type=MESH_IDS).start()
        started = []
        for a in range(n):
            local = pltpu.make_async_copy(src[a].at[me], dst[a].at[me], local_sems.at[a])
            local.start()
            started.append(local)
            for k in range(1, N_DEV):
                peer, peer_index = _peer(x, y, c, k)
                pltpu.make_async_remote_copy(
                    src_ref=src[a].at[peer_index], dst_ref=dst[a].at[me], send_sem=send_sems.at[a, k - 1],
                    recv_sem=recv_sems.at[a, k - 1], device_id=peer, device_id_type=MESH_IDS).start()
        gathered[me] = pack[...]
        for k in range(1, N_DEV):
            peer, peer_index = _peer(x, y, c, k)
            landed = pltpu.make_async_remote_copy(
                src_ref=pack, dst_ref=gathered.at[peer_index], send_sem=small_send.at[k - 1],
                recv_sem=small_recv.at[k - 1], device_id=peer, device_id_type=MESH_IDS)
            landed.wait_send()
            landed.wait_recv()
        total = gathered[0]
        for j in range(1, N_DEV):
            total = total + gathered[j]
        total_ref[...] = total
        for a in range(n):
            started[a].wait()
            for k in range(1, N_DEV):
                peer, peer_index = _peer(x, y, c, k)
                landed = pltpu.make_async_remote_copy(
                    src_ref=src[a].at[peer_index], dst_ref=dst[a].at[peer_index], send_sem=send_sems.at[a, k - 1],
                    recv_sem=recv_sems.at[a, k - 1], device_id=peer, device_id_type=MESH_IDS)
                landed.wait_send()
                landed.wait_recv()

    hbm = pl.BlockSpec(memory_space=pltpu.HBM)
    vmem = pl.BlockSpec(memory_space=pltpu.VMEM)
    out = pl.pallas_call(
        body,
        name="reduce_scatter",
        in_specs=[hbm] * n + [vmem] * len(names),
        out_specs=[hbm] * n + [vmem],
        out_shape=[jax.ShapeDtypeStruct(b.shape, b.dtype) for b in blocks]
        + [jax.ShapeDtypeStruct((SMALL_ROWS, width), F32)],
        scratch_shapes=[
            pltpu.VMEM((SMALL_ROWS, width), F32), pltpu.VMEM((N_DEV, SMALL_ROWS, width), F32),
            pltpu.SemaphoreType.DMA((n, N_DEV - 1)), pltpu.SemaphoreType.DMA((n, N_DEV - 1)),
            pltpu.SemaphoreType.DMA((n,)),
            pltpu.SemaphoreType.DMA((N_DEV - 1,)), pltpu.SemaphoreType.DMA((N_DEV - 1,)),
        ],
        compiler_params=pltpu.CompilerParams(has_side_effects=True),
    )(*blocks, *[small[k] for k in names])
    return out[:n], out[n]


def _adamw_math(w, g, m, v):
    m = ADAM_B1 * m + (1.0 - ADAM_B1) * g
    v = ADAM_B2 * v + (1.0 - ADAM_B2) * (g * g)
    m_hat = m / (1.0 - ADAM_B1 ** ADAM_STEP)
    v_hat = v / (1.0 - ADAM_B2 ** ADAM_STEP)
    delta = -ADAM_LR * (m_hat / (jnp.sqrt(v_hat) + ADAM_EPS) + ADAM_WD * w)
    return delta, m, v


def _adamw_shard(parts, w, m, v):
    r, c = w.shape
    tr = r
    while tr > 512:
        tr //= 2

    def body(p_ref, w_ref, m_ref, v_ref, g_ref, d_ref, mo_ref, vo_ref):
        g = p_ref[0].astype(F32)
        for j in range(1, N_DEV):
            g = g + p_ref[j].astype(F32)
        delta, mn, vn = _adamw_math(w_ref[...], g, m_ref[...], v_ref[...])
        g_ref[...] = g
        d_ref[...] = delta
        mo_ref[...] = mn
        vo_ref[...] = vn

    tile = pl.BlockSpec((tr, c), lambda i: (i, 0))
    return pl.pallas_call(
        body,
        name="adamw_shard",
        grid=(r // tr,),
        in_specs=[pl.BlockSpec((N_DEV, tr, c), lambda i: (0, i, 0)), tile, tile, tile],
        out_specs=[tile] * 4,
        out_shape=[jax.ShapeDtypeStruct((r, c), F32)] * 4,
        compiler_params=_params(("parallel",)),
    )(parts, w, m, v)


def _adamw_small(gs, ws, ms, vs):
    n = len(gs)

    def body(*refs):
        g_refs, w_refs, m_refs, v_refs = refs[:n], refs[n:2 * n], refs[2 * n:3 * n], refs[3 * n:4 * n]
        d_out, m_out, v_out = refs[4 * n:5 * n], refs[5 * n:6 * n], refs[6 * n:7 * n]
        for i in range(n):
            delta, mn, vn = _adamw_math(w_refs[i][...], g_refs[i][...], m_refs[i][...], v_refs[i][...])
            d_out[i][...] = delta
            m_out[i][...] = mn
            v_out[i][...] = vn

    shapes = [jax.ShapeDtypeStruct(w.shape, F32) for w in ws]
    out = pl.pallas_call(
        body,
        name="adamw_small",
        out_shape=shapes * 3,
        compiler_params=_params(),
    )(*gs, *ws, *ms, *vs)
    return out[:n], out[n:2 * n], out[2 * n:]


COLUMN_SHARDED = ("ffn1_gate", "ffn1_up", "w_in", "ffn2_gate", "ffn2_up")
ROW_SHARDED = ("ffn1_down", "w_out", "w_q_mem", "w_o_mem", "ffn2_down")
LARGE = ("ffn1_gate", "ffn1_up", "ffn1_down", "w_in", "w_out", "w_q_mem", "w_kv_mem", "w_o_mem",
         "ffn2_gate", "ffn2_up", "ffn2_down")
SMALL = ("ffn1_norm", "mix_norm", "lb_param", "hgrn_out_norm", "conv_w", "xattn_norm", "mem_norm", "ffn2_norm",
         "final_norm")
WEIGHTS = ("ffn1_norm", "ffn1_gate", "ffn1_up", "ffn1_down", "mix_norm", "w_in", "lb_param", "hgrn_out_norm",
           "conv_w", "w_out", "xattn_norm", "mem_norm", "w_q_mem", "w_kv_mem", "w_o_mem", "ffn2_norm", "ffn2_gate",
           "ffn2_up", "ffn2_down", "final_norm")


def _as_blocks(grad, name):
    r, c = grad.shape
    if name in COLUMN_SHARDED:
        return grad.reshape(r, N_DEV, c // N_DEV).transpose(1, 0, 2).astype(BF16)
    return grad.reshape(N_DEV, r // N_DEV, c).astype(BF16)


def kernel(x, mem, ffn1_norm, ffn1_gate, ffn1_up, ffn1_down, mix_norm, w_in, lb_param, hgrn_out_norm, conv_w, w_out, xattn_norm, mem_norm, w_q_mem, w_kv_mem, w_o_mem, ffn2_norm, ffn2_gate, ffn2_up, ffn2_down, final_norm, loss_target, m_ffn1_norm, m_ffn1_gate, m_ffn1_up, m_ffn1_down, m_mix_norm, m_w_in, m_lb_param, m_hgrn_out_norm, m_conv_w, m_w_out, m_xattn_norm, m_mem_norm, m_w_q_mem, m_w_kv_mem, m_w_o_mem, m_ffn2_norm, m_ffn2_gate, m_ffn2_up, m_ffn2_down, m_final_norm, v_ffn1_norm, v_ffn1_gate, v_ffn1_up, v_ffn1_down, v_mix_norm, v_w_in, v_lb_param, v_hgrn_out_norm, v_conv_w, v_w_out, v_xattn_norm, v_mem_norm, v_w_q_mem, v_w_kv_mem, v_w_o_mem, v_ffn2_norm, v_ffn2_gate, v_ffn2_up, v_ffn2_down, v_final_norm):
    given = dict(locals())
    me = 4 * lax.axis_index("x") + 2 * lax.axis_index("y") + lax.axis_index("c")
    x0, memv, target = x[0], mem[0], loss_target[0]

    def shard(prefix, name):
        v = given[prefix + name]
        return v.reshape(1, -1) if v.ndim == 1 else (v[0] if v.ndim == 3 else v)

    w = {name: shard("", name) for name in WEIGHTS}
    m = {name: shard("m_", name) for name in WEIGHTS}
    v = {name: shard("v_", name) for name in WEIGHTS}

    conv_rows, conv_taps = w["conv_w"].shape
    conv_flat = jnp.pad(w["conv_w"].reshape(1, conv_rows * conv_taps), ((0, 7), (0, 256 - conv_rows * conv_taps)))
    gathered = _all_gather([w[name].astype(BF16) for name in LARGE] + [conv_flat])
    full = {}
    for name, blocks in zip(LARGE, gathered):
        _, r, c = blocks.shape
        if name in COLUMN_SHARDED:
            full[name] = blocks.transpose(1, 0, 2).reshape(r, N_DEV * c)
        elif name in ROW_SHARDED:
            full[name] = blocks.reshape(N_DEV * r, c)
        else:
            full[name] = blocks
    convw_t = gathered[-1][:, 0, :conv_rows * conv_taps].reshape(N_DEV * conv_rows, conv_taps).T

    x1, a1, b1, s1 = _ffn_fwd(x0, w["ffn1_norm"], full["ffn1_gate"], full["ffn1_up"], full["ffn1_down"])
    x2, z, o_raw, states, ycat = _mix_fwd(
        x1, w["mix_norm"], full["w_in"], w["lb_param"], w["hgrn_out_norm"], convw_t, full["w_out"])
    kv = _memkv_fwd(memv, w["mem_norm"], full["w_kv_mem"])
    x3, hq, qm, att = _xattn_fwd(x2, w["xattn_norm"], full["w_q_mem"], kv, full["w_o_mem"])
    x4, a2, b2, s2 = _ffn_fwd(x3, w["ffn2_norm"], full["ffn2_gate"], full["ffn2_up"], full["ffn2_down"])
    dx4, loss_part, d_final = _final_loss(x4, w["final_norm"], target)

    grads = {}
    dx3, da2, db2, h4, d_ffn2_norm = _ffn_bwd(
        x3, w["ffn2_norm"], dx4, a2, b2, full["ffn2_gate"], full["ffn2_up"], full["ffn2_down"])
    grads["ffn2_down"] = _matmul_tn(s2, dx4, 0.5)
    grads["ffn2_gate"] = _matmul_tn(h4, da2)
    grads["ffn2_up"] = _matmul_tn(h4, db2)
    dx2, dqm, dkv, d_xattn_norm = _xattn_bwd(x2, w["xattn_norm"], dx3, qm, kv, full["w_q_mem"], full["w_o_mem"])
    grads["w_o_mem"] = _matmul_tn(att, dx3)
    grads["w_q_mem"] = _matmul_tn(hq, dqm)
    d_wkv_blocks, d_mem_norm = _memkv_bwd(memv, w["mem_norm"], dkv, full["w_kv_mem"])
    dx1, dz, h2, d_mix_norm, d_lbp, d_gh, d_convw_t = _mix_bwd(
        x1, w["mix_norm"], dx2, z, o_raw, states, full["w_in"], w["lb_param"], w["hgrn_out_norm"], convw_t,
        full["w_out"])
    grads["w_out"] = _matmul_tn(ycat, dx2)
    grads["w_in"] = _matmul_tn(h2, dz)
    dx0, da1, db1, h1, d_ffn1_norm = _ffn_bwd(
        x0, w["ffn1_norm"], dx1, a1, b1, full["ffn1_gate"], full["ffn1_up"], full["ffn1_down"])
    grads["ffn1_down"] = _matmul_tn(s1, dx1, 0.5)
    grads["ffn1_gate"] = _matmul_tn(h1, da1)
    grads["ffn1_up"] = _matmul_tn(h1, db1)

    blocks = [d_wkv_blocks.astype(BF16) if name == "w_kv_mem" else _as_blocks(grads[name], name) for name in LARGE]
    small_parts = {
        "ffn1_norm": d_ffn1_norm, "mix_norm": d_mix_norm, "xattn_norm": d_xattn_norm, "mem_norm": d_mem_norm,
        "ffn2_norm": d_ffn2_norm, "final_norm": d_final, "lb_param": d_lbp, "hgrn_out_norm": d_gh,
        "conv_w": d_convw_t, "loss": loss_part,
    }
    parts, total = _reduce_scatter(blocks, small_parts)

    g_out, d_out, m_out, v_out = {}, {}, {}, {}
    for name, p in zip(LARGE, parts):
        g_out[name], d_out[name], m_out[name], v_out[name] = _adamw_shard(p, w[name], m[name], v[name])
    g_small = {}
    for name in SMALL:
        row, nrows, ncols = SMALL_LAYOUT[name]
        g_small[name] = total[row:row + nrows, 0:ncols]
    g_small["conv_w"] = lax.dynamic_slice_in_dim(g_small["conv_w"].T, me * conv_rows, conv_rows, axis=0)
    ds, ms, vs = _adamw_small(
        [g_small[k] for k in SMALL], [w[k] for k in SMALL], [m[k] for k in SMALL], [v[k] for k in SMALL])
    for i, name in enumerate(SMALL):
        g_out[name], d_out[name], m_out[name], v_out[name] = g_small[name], ds[i], ms[i], vs[i]

    def shaped(value, name):
        return value.reshape(given[name].shape)

    loss = total[SMALL_LAYOUT["loss"][0], 0]
    outs = [loss, dx0.reshape(x.shape)]
    for group in (g_out, d_out, m_out, v_out):
        outs += [shaped(group[name], name) for name in WEIGHTS]
    return tuple(outs)
```

```python
import jax
import jax.numpy as jnp
from jax import lax
from jax.experimental import pallas as pl
from jax.experimental.pallas import tpu as pltpu

F32 = jnp.float32
BF16 = jnp.bfloat16
MESH_IDS = pl.DeviceIdType.MESH

N_DEV = 8
EPS = 1e-6
HGRN_HEADS = 4
HGRN_DK = 128
HGRN_W = 512
CHUNK = 64
MEM_HEADS = 4
MEM_HD = 256
ADAM_LR = 0.001
ADAM_B1 = 0.9
ADAM_B2 = 0.999
ADAM_EPS = 1e-08
ADAM_WD = 0.01
ADAM_STEP = 10

TOKEN_TILE = 256
REDUCE_TILE = 1024
VMEM_LIMIT = 60 * 1024 * 1024
SMALL_ROWS = 16
NT = (((1,), (1,)), ((), ()))
TN = (((0,), (0,)), ((), ()))


def _params(sem=None):
    return pltpu.CompilerParams(dimension_semantics=sem, vmem_limit_bytes=VMEM_LIMIT)


def _dot(a, b, dims=None):
    if dims is None:
        return jnp.dot(a, b, preferred_element_type=F32)
    return lax.dot_general(a, b, dims, preferred_element_type=F32)


def _sigmoid(v):
    return 1.0 / (1.0 + jnp.exp(-v))


def _rms(x, g):
    r = lax.rsqrt(jnp.mean(x * x, axis=-1, keepdims=True) + EPS)
    xh = x * r
    return xh * g, xh, r


def _rms_bwd(dh, xh, r, g):
    dxh = dh * g
    return r * (dxh - xh * jnp.mean(dxh * xh, axis=-1, keepdims=True))


def _full(shape):
    return pl.BlockSpec(shape, lambda *_: (0,) * len(shape))


def _rows(tm, width):
    return pl.BlockSpec((tm, width), lambda i: (i, 0))


def _rows_rev(tm, width, n):
    return pl.BlockSpec((tm, width), lambda i: (n - 1 - i, 0))


def _accumulate(ref, first, value):
    @pl.when(first)
    def _():
        ref[...] = value

    @pl.when(jnp.logical_not(first))
    def _():
        ref[...] += value


def _ffn_fwd(x, g, wg, wu, wd):
    t, d = x.shape
    f = wg.shape[0]
    tm = min(TOKEN_TILE, t)

    def body(x_ref, g_ref, wg_ref, wu_ref, wd_ref, xo_ref, a_ref, b_ref, s_ref):
        xv = x_ref[...]
        h, _, _ = _rms(xv, g_ref[...])
        hb = h.astype(BF16)
        a = _dot(hb, wg_ref[...], NT)
        b = _dot(hb, wu_ref[...], NT)
        s = (a * _sigmoid(a) * b).astype(BF16)
        xo_ref[...] = xv + 0.5 * _dot(s, wd_ref[...])
        a_ref[...] = a.astype(BF16)
        b_ref[...] = b.astype(BF16)
        s_ref[...] = s

    return pl.pallas_call(
        body,
        name="ffn_fwd",
        grid=(t // tm,),
        in_specs=[_rows(tm, d), _full((1, d)), _full((f, d)), _full((f, d)), _full((f, d))],
        out_specs=[_rows(tm, d), _rows(tm, f), _rows(tm, f), _rows(tm, f)],
        out_shape=[
            jax.ShapeDtypeStruct((t, d), F32),
            jax.ShapeDtypeStruct((t, f), BF16),
            jax.ShapeDtypeStruct((t, f), BF16),
            jax.ShapeDtypeStruct((t, f), BF16),
        ],
        compiler_params=_params(("arbitrary",)),
    )(x, g, wg, wu, wd)


def _ffn_bwd(x, g, dxo, a, b, wg, wu, wd):
    t, d = x.shape
    f = wg.shape[0]
    tm = min(TOKEN_TILE, t)

    def body(x_ref, g_ref, dxo_ref, a_ref, b_ref, wg_ref, wu_ref, wd_ref, dx_ref, da_ref, db_ref, h_ref, dg_ref):
        gv = g_ref[...]
        h, xh, r = _rms(x_ref[...], gv)
        dxo = dxo_ref[...]
        ds = _dot((0.5 * dxo).astype(BF16), wd_ref[...], NT)
        af = a_ref[...].astype(F32)
        bf = b_ref[...].astype(F32)
        sg = _sigmoid(af)
        da = (ds * bf * (sg * (1.0 + af * (1.0 - sg)))).astype(BF16)
        db = (ds * (af * sg)).astype(BF16)
        dh = _dot(da, wg_ref[...]) + _dot(db, wu_ref[...])
        dx_ref[...] = _rms_bwd(dh, xh, r, gv) + dxo
        da_ref[...] = da
        db_ref[...] = db
        h_ref[...] = h.astype(BF16)
        _accumulate(dg_ref, pl.program_id(0) == 0, jnp.sum(dh * xh, axis=0, keepdims=True))

    return pl.pallas_call(
        body,
        name="ffn_bwd",
        grid=(t // tm,),
        in_specs=[
            _rows(tm, d), _full((1, d)), _rows(tm, d), _rows(tm, f), _rows(tm, f),
            _full((f, d)), _full((f, d)), _full((f, d)),
        ],
        out_specs=[_rows(tm, d), _rows(tm, f), _rows(tm, f), _rows(tm, d), _full((1, d))],
        out_shape=[
            jax.ShapeDtypeStruct((t, d), F32),
            jax.ShapeDtypeStruct((t, f), BF16),
            jax.ShapeDtypeStruct((t, f), BF16),
            jax.ShapeDtypeStruct((t, d), BF16),
            jax.ShapeDtypeStruct((1, d), F32),
        ],
        compiler_params=_params(("arbitrary",)),
    )(x, g, dxo, a, b, wg, wu, wd)


def _matmul_tn(a, b, scale=1.0):
    t, m = a.shape
    n = b.shape[1]
    tk = min(REDUCE_TILE, t)
    nb = n
    while m * nb * 4 > 12 * 1024 * 1024:
        nb //= 2
    assert n % nb == 0 and nb % 128 == 0

    def body(a_ref, b_ref, o_ref, acc):
        bv = b_ref[...]
        if scale != 1.0:
            bv = bv * scale
        part = _dot(a_ref[...].astype(BF16), bv.astype(BF16), TN)
        _accumulate(acc, pl.program_id(1) == 0, part)

        @pl.when(pl.program_id(1) == t // tk - 1)
        def _():
            o_ref[...] = acc[...].astype(BF16)

    return pl.pallas_call(
        body,
        name="matmul_tn",
        grid=(n // nb, t // tk),
        in_specs=[pl.BlockSpec((tk, m), lambda j, k: (k, 0)), pl.BlockSpec((tk, nb), lambda j, k: (k, j))],
        out_specs=pl.BlockSpec((m, nb), lambda j, k: (0, j)),
        out_shape=jax.ShapeDtypeStruct((m, n), BF16),
        scratch_shapes=[pltpu.VMEM((m, nb), F32)],
        compiler_params=_params(("parallel", "arbitrary")),
    )(a, b)


def _chunk_cumsum(v, reverse=False):
    n = v.shape[0]
    pos = lax.broadcasted_iota(jnp.int32, (n, 1), 0) % CHUNK
    shift = 1
    while shift < CHUNK:
        if reverse:
            moved = pltpu.roll(v, n - shift, axis=0)
            v = v + jnp.where(pos < CHUNK - shift, moved, 0.0)
        else:
            moved = pltpu.roll(v, shift, axis=0)
            v = v + jnp.where(pos >= shift, moved, 0.0)
        shift *= 2
    return v


def _shift_rows(v, shift, edge):
    n = v.shape[0]
    row = lax.broadcasted_iota(jnp.int32, (n, 1), 0)
    out = pltpu.roll(v, shift % n, axis=0)
    if shift > 0:
        for j in range(shift):
            out = jnp.where(row == j, edge[8 - shift + j:8 - shift + j + 1, :], out)
    else:
        for j in range(-shift):
            out = jnp.where(row == n + shift + j, edge[j:j + 1, :], out)
    return out


def _gates(z, lbp):
    w = HGRN_W
    lb = _sigmoid(lbp[0:1, :] - lbp[1:2, :])
    zq = z[:, 0:w]
    sig = _sigmoid(z[:, w:2 * w])
    f = lb + (1.0 - lb) * sig
    sq = _sigmoid(zq)
    q = zq * sq * HGRN_DK ** -0.5
    return lb, sig, f, sq, q


def _short_conv(u, edge, cw):
    return cw[0:1, :] * _shift_rows(u, 2, edge) + cw[1:2, :] * _shift_rows(u, 1, edge) + cw[2:3, :] * u


def _causal_mask():
    row = lax.broadcasted_iota(jnp.int32, (CHUNK, CHUNK), 0)
    col = lax.broadcasted_iota(jnp.int32, (CHUNK, CHUNK), 1)
    return col <= row


def _mix_fwd(x, g, w_in, lbp, gh, convw_t, w_out):
    t, d = x.shape
    zw = w_in.shape[0]
    w = HGRN_W
    tm = min(TOKEN_TILE, t)
    nc = tm // CHUNK
    n_chunks = t // CHUNK

    def body(x_ref, g_ref, win_ref, lbp_ref, gh_ref, cw_ref, wout_ref,
             xo_ref, z_ref, o_ref, st_ref, y_ref, state, ucarry):
        @pl.when(pl.program_id(0) == 0)
        def _():
            state[...] = jnp.zeros_like(state)
            ucarry[...] = jnp.zeros_like(ucarry)

        xv = x_ref[...]
        h, _, _ = _rms(xv, g_ref[...])
        z_ref[...] = _dot(h.astype(BF16), win_ref[...], NT)
        z = z_ref[...]
        _, _, f, _, q = _gates(z, lbp_ref[...])
        bcum = _chunk_cumsum(jnp.log(f))
        kk = 1.0 - f
        vv = z[:, 2 * w:3 * w]
        mask = _causal_mask()
        for c in range(nc):
            rows = slice(c * CHUNK, (c + 1) * CHUNK)
            for hd in range(HGRN_HEADS):
                cols = slice(hd * HGRN_DK, (hd + 1) * HGRN_DK)
                b = bcum[rows, cols]
                blast = b[CHUNK - 1:CHUNK, :]
                qh = (q[rows, cols] * jnp.exp(b)).astype(BF16)
                kh = (kk[rows, cols] * jnp.exp(-b)).astype(BF16)
                kbar = (kk[rows, cols] * jnp.exp(blast - b)).astype(BF16)
                vb = vv[rows, cols].astype(BF16)
                st = state[hd]
                st_ref[c, hd] = st
                att = jnp.where(mask, _dot(qh, kh, NT), 0.0).astype(BF16)
                o_ref[rows, cols] = _dot(att, vb) + _dot(qh, st.astype(BF16), NT)
                state[hd] = st * jnp.exp(blast) + _dot(vb, kbar, TN)
        ghv = gh_ref[...]
        for hd in range(HGRN_HEADS):
            cols = slice(hd * HGRN_DK, (hd + 1) * HGRN_DK)
            on, _, _ = _rms(o_ref[:, cols], ghv[:, cols])
            zg = z[:, 3 * w + hd * HGRN_DK:3 * w + (hd + 1) * HGRN_DK]
            y_ref[:, cols] = (on * (zg * _sigmoid(zg))).astype(BF16)
        u = z[:, 5 * w:6 * w] * z[:, 6 * w:7 * w]
        conv = _short_conv(u, ucarry[...], cw_ref[...])
        ucarry[...] = u[tm - 8:tm, :]
        y_ref[:, w:2 * w] = (z[:, 4 * w:5 * w] * conv).astype(BF16)
        xo_ref[...] = xv + _dot(y_ref[...], wout_ref[...])

    return pl.pallas_call(
        body,
        name="mix_fwd",
        grid=(t // tm,),
        in_specs=[
            _rows(tm, d), _full((1, d)), _full((zw, d)), _full((2, w)), _full((1, w)), _full((3, w)),
            _full((2 * w, d)),
        ],
        out_specs=[
            _rows(tm, d), _rows(tm, zw), _rows(tm, w),
            pl.BlockSpec((nc, HGRN_HEADS, HGRN_DK, HGRN_DK), lambda i: (i, 0, 0, 0)),
            _rows(tm, 2 * w),
        ],
        out_shape=[
            jax.ShapeDtypeStruct((t, d), F32),
            jax.ShapeDtypeStruct((t, zw), F32),
            jax.ShapeDtypeStruct((t, w), F32),
            jax.ShapeDtypeStruct((n_chunks, HGRN_HEADS, HGRN_DK, HGRN_DK), F32),
            jax.ShapeDtypeStruct((t, 2 * w), BF16),
        ],
        scratch_shapes=[pltpu.VMEM((HGRN_HEADS, HGRN_DK, HGRN_DK), F32), pltpu.VMEM((8, w), F32)],
        compiler_params=_params(("arbitrary",)),
    )(x, g, w_in, lbp, gh, convw_t, w_out)


def _mix_bwd(x, g, dxo, z, o, states, w_in, lbp, gh, convw_t, w_out):
    t, d = x.shape
    zw = w_in.shape[0]
    w = HGRN_W
    tm = min(TOKEN_TILE, t)
    nc = tm // CHUNK
    n = t // tm

    def body(x_ref, g_ref, dxo_ref, z_ref, zprev_ref, o_ref, st_ref, win_ref, lbp_ref, gh_ref, cw_ref, wout_ref,
             dx_ref, dz_ref, h_ref, dg_ref, dlbp_ref, dgh_ref, dcw_ref,
             dstate, dcarry, do_buf, dq_buf, dk_buf, db_buf):
        first = pl.program_id(0) == 0

        @pl.when(first)
        def _():
            dstate[...] = jnp.zeros_like(dstate)
            dcarry[...] = jnp.zeros_like(dcarry)

        gv = g_ref[...]
        h, xh, r = _rms(x_ref[...], gv)
        h_ref[...] = h.astype(BF16)
        dxo = dxo_ref[...]
        dy = _dot(dxo.astype(BF16), wout_ref[...], NT)
        z = z_ref[...]
        lb, sig, f, sq, q = _gates(z, lbp_ref[...])
        bcum = _chunk_cumsum(jnp.log(f))
        kk = 1.0 - f
        vv = z[:, 2 * w:3 * w]

        ghv = gh_ref[...]
        dgh_parts = []
        for hd in range(HGRN_HEADS):
            cols = slice(hd * HGRN_DK, (hd + 1) * HGRN_DK)
            gcols = slice(3 * w + hd * HGRN_DK, 3 * w + (hd + 1) * HGRN_DK)
            on, oh, rr = _rms(o_ref[:, cols], ghv[:, cols])
            zg = z[:, gcols]
            sgz = _sigmoid(zg)
            dyh = dy[:, cols]
            don = dyh * (zg * sgz)
            dz_ref[:, gcols] = (dyh * on * (sgz * (1.0 + zg * (1.0 - sgz)))).astype(BF16)
            dgh_parts.append(jnp.sum(don * oh, axis=0, keepdims=True))
            do_buf[:, cols] = _rms_bwd(don, oh, rr, ghv[:, cols])
        _accumulate(dgh_ref, first, jnp.concatenate(dgh_parts, axis=1))

        zb = z[:, 4 * w:5 * w]
        zc = z[:, 5 * w:6 * w]
        zu = z[:, 6 * w:7 * w]
        u = zc * zu
        cw = cw_ref[...]
        zp = zprev_ref[...]
        uprev = jnp.where(pl.program_id(0) == n - 1, 0.0, zp[:, 5 * w:6 * w] * zp[:, 6 * w:7 * w])
        dyc = dy[:, w:2 * w]
        dz_ref[:, 4 * w:5 * w] = (dyc * _short_conv(u, uprev, cw)).astype(BF16)
        dconv = dyc * zb
        edge = dcarry[...]
        dconv1 = _shift_rows(dconv, -1, edge)
        dconv2 = _shift_rows(dconv, -2, edge)
        dcarry[...] = dconv[0:8, :]
        du = cw[2:3, :] * dconv + cw[1:2, :] * dconv1 + cw[0:1, :] * dconv2
        dz_ref[:, 5 * w:6 * w] = (du * zu).astype(BF16)
        dz_ref[:, 6 * w:7 * w] = (du * zc).astype(BF16)
        _accumulate(dcw_ref, first, jnp.concatenate([
            jnp.sum(u * dconv2, axis=0, keepdims=True),
            jnp.sum(u * dconv1, axis=0, keepdims=True),
            jnp.sum(u * dconv, axis=0, keepdims=True)], axis=0))

        mask = _causal_mask()
        last_row = lax.broadcasted_iota(jnp.int32, (CHUNK, 1), 0) == CHUNK - 1
        for c in reversed(range(nc)):
            rows = slice(c * CHUNK, (c + 1) * CHUNK)
            for hd in range(HGRN_HEADS):
                cols = slice(hd * HGRN_DK, (hd + 1) * HGRN_DK)
                b = bcum[rows, cols]
                blast = b[CHUNK - 1:CHUNK, :]
                eb = jnp.exp(b)
                enb = jnp.exp(-b)
                erest = jnp.exp(blast - b)
                elast = jnp.exp(blast)
                qh = q[rows, cols] * eb
                kh = kk[rows, cols] * enb
                kbar = kk[rows, cols] * erest
                qhb = qh.astype(BF16)
                khb = kh.astype(BF16)
                kbarb = kbar.astype(BF16)
                vb = vv[rows, cols].astype(BF16)
                st = st_ref[c, hd]
                dst = dstate[hd]
                dstb = dst.astype(BF16)
                dob = do_buf[rows, cols].astype(BF16)
                att = jnp.where(mask, _dot(qhb, khb, NT), 0.0).astype(BF16)
                datt = jnp.where(mask, _dot(dob, vb, NT), 0.0).astype(BF16)
                dv = _dot(att, dob, TN) + _dot(kbarb, dstb, NT)
                dqh = _dot(datt, khb) + _dot(dob, st.astype(BF16))
                dkh = _dot(datt, qhb, TN)
                dkbar = _dot(vb, dstb)
                dstate[hd] = dst * elast + _dot(dob, qhb, TN)
                kbar_dkbar = kbarb.astype(F32) * dkbar
                db = qhb.astype(F32) * dqh - khb.astype(F32) * dkh - kbar_dkbar
                db_last = (jnp.sum(kbar_dkbar, axis=0, keepdims=True)
                           + jnp.sum(dst * st, axis=0, keepdims=True) * elast)
                db_buf[rows, cols] = jnp.where(last_row, db + db_last, db)
                dq_buf[rows, cols] = dqh * eb
                dk_buf[rows, cols] = dkh * enb + dkbar * erest
                dz_ref[rows, 2 * w + hd * HGRN_DK:2 * w + (hd + 1) * HGRN_DK] = dv.astype(BF16)

        dlogf = _chunk_cumsum(db_buf[...], reverse=True)
        df = dlogf / f - dk_buf[...]
        zq = z[:, 0:w]
        dz_ref[:, 0:w] = (dq_buf[...] * HGRN_DK ** -0.5 * (sq * (1.0 + zq * (1.0 - sq)))).astype(BF16)
        dz_ref[:, w:2 * w] = (df * (1.0 - lb) * sig * (1.0 - sig)).astype(BF16)
        dlb = jnp.sum(df * (1.0 - sig), axis=0, keepdims=True) * lb * (1.0 - lb)
        _accumulate(dlbp_ref, first, jnp.concatenate([dlb, -dlb], axis=0))

        dh = _dot(dz_ref[...], win_ref[...])
        dx_ref[...] = _rms_bwd(dh, xh, r, gv) + dxo
        _accumulate(dg_ref, first, jnp.sum(dh * xh, axis=0, keepdims=True))

    return pl.pallas_call(
        body,
        name="mix_bwd",
        grid=(n,),
        in_specs=[
            _rows_rev(tm, d, n), _full((1, d)), _rows_rev(tm, d, n), _rows_rev(tm, zw, n),
            pl.BlockSpec((8, zw), lambda i: (jnp.maximum((n - 1 - i) * (tm // 8) - 1, 0), 0)),
            _rows_rev(tm, w, n),
            pl.BlockSpec((nc, HGRN_HEADS, HGRN_DK, HGRN_DK), lambda i: (n - 1 - i, 0, 0, 0)),
            _full((zw, d)), _full((2, w)), _full((1, w)), _full((3, w)), _full((2 * w, d)),
        ],
        out_specs=[
            _rows_rev(tm, d, n), _rows_rev(tm, zw, n), _rows_rev(tm, d, n),
            _full((1, d)), _full((2, w)), _full((1, w)), _full((3, w)),
        ],
        out_shape=[
            jax.ShapeDtypeStruct((t, d), F32),
            jax.ShapeDtypeStruct((t, zw), BF16),
            jax.ShapeDtypeStruct((t, d), BF16),
            jax.ShapeDtypeStruct((1, d), F32),
            jax.ShapeDtypeStruct((2, w), F32),
            jax.ShapeDtypeStruct((1, w), F32),
            jax.ShapeDtypeStruct((3, w), F32),
        ],
        scratch_shapes=[
            pltpu.VMEM((HGRN_HEADS, HGRN_DK, HGRN_DK), F32), pltpu.VMEM((8, w), F32),
            pltpu.VMEM((tm, w), F32), pltpu.VMEM((tm, w), F32), pltpu.VMEM((tm, w), F32), pltpu.VMEM((tm, w), F32),
        ],
        compiler_params=_params(("arbitrary",)),
    )(x, g, dxo, z, z, o, states, w_in, lbp, gh, convw_t, w_out)


def _memkv_fwd(mem, g, wkv):
    m, d = mem.shape
    nb, _, cb = wkv.shape

    def body(mem_ref, g_ref, wkv_ref, kv_ref):
        mn, _, _ = _rms(mem_ref[...], g_ref[...])
        mnb = mn.astype(BF16)
        for j in range(nb):
            kv_ref[:, j * cb:(j + 1) * cb] = _dot(mnb, wkv_ref[j]).astype(BF16)

    return pl.pallas_call(
        body,
        name="memkv_fwd",
        out_shape=jax.ShapeDtypeStruct((m, nb * cb), BF16),
        compiler_params=_params(),
    )(mem, g, wkv)


def _memkv_bwd(mem, g, dkv, wkv):
    m, d = mem.shape
    nb, _, cb = wkv.shape

    def body(mem_ref, g_ref, dkv_ref, wkv_ref, dw_ref, dg_ref):
        mn, xh, _ = _rms(mem_ref[...], g_ref[...])
        mnb = mn.astype(BF16)
        dmn = jnp.zeros((m, d), F32)
        for j in range(nb):
            dkvb = dkv_ref[:, j * cb:(j + 1) * cb].astype(BF16)
            dw_ref[j] = _dot(mnb, dkvb, TN).astype(BF16)
            dmn = dmn + _dot(dkvb, wkv_ref[j], NT)
        dg_ref[...] = jnp.sum(dmn * xh, axis=0, keepdims=True)

    return pl.pallas_call(
        body,
        name="memkv_bwd",
        out_shape=[jax.ShapeDtypeStruct((nb, d, cb), BF16), jax.ShapeDtypeStruct((1, d), F32)],
        compiler_params=_params(),
    )(mem, g, dkv, wkv)


def _softmax_rows(qm_h, k_h):
    sc = _dot(qm_h, k_h, NT) * MEM_HD ** -0.5
    e = jnp.exp(sc - jnp.max(sc, axis=-1, keepdims=True))
    return e / jnp.sum(e, axis=-1, keepdims=True)


def _xattn_fwd(x, g, wq, kv, wo):
    t, d = x.shape
    m = kv.shape[0]
    tm = min(TOKEN_TILE, t)

    def body(x_ref, g_ref, wq_ref, kv_ref, wo_ref, xo_ref, hq_ref, qm_ref, att_ref):
        xv = x_ref[...]
        h, _, _ = _rms(xv, g_ref[...])
        hq_ref[...] = h.astype(BF16)
        qm_ref[...] = _dot(hq_ref[...], wq_ref[...]).astype(BF16)
        for hd in range(MEM_HEADS):
            cols = slice(hd * MEM_HD, (hd + 1) * MEM_HD)
            p = _softmax_rows(qm_ref[:, cols], kv_ref[:, cols])
            att_ref[:, cols] = _dot(p.astype(BF16), kv_ref[:, d + hd * MEM_HD:d + (hd + 1) * MEM_HD]).astype(BF16)
        xo_ref[...] = xv + _dot(att_ref[...], wo_ref[...])

    return pl.pallas_call(
        body,
        name="xattn_fwd",
        grid=(t // tm,),
        in_specs=[_rows(tm, d), _full((1, d)), _full((d, d)), _full((m, 2 * d)), _full((d, d))],
        out_specs=[_rows(tm, d), _rows(tm, d), _rows(tm, d), _rows(tm, d)],
        out_shape=[
            jax.ShapeDtypeStruct((t, d), F32),
            jax.ShapeDtypeStruct((t, d), BF16),
            jax.ShapeDtypeStruct((t, d), BF16),
            jax.ShapeDtypeStruct((t, d), BF16),
        ],
        compiler_params=_params(("arbitrary",)),
    )(x, g, wq, kv, wo)


def _xattn_bwd(x, g, dxo, qm, kv, wq, wo):
    t, d = x.shape
    m = kv.shape[0]
    tm = min(TOKEN_TILE, t)

    def body(x_ref, g_ref, dxo_ref, qm_ref, kv_ref, wq_ref, wo_ref, dx_ref, dqm_ref, dkv_ref, dg_ref):
        first = pl.program_id(0) == 0

        @pl.when(first)
        def _():
            dkv_ref[...] = jnp.zeros_like(dkv_ref)

        gv = g_ref[...]
        _, xh, r = _rms(x_ref[...], gv)
        dxo = dxo_ref[...]
        datt = _dot(dxo.astype(BF16), wo_ref[...], NT).astype(BF16)
        for hd in range(MEM_HEADS):
            cols = slice(hd * MEM_HD, (hd + 1) * MEM_HD)
            vcols = slice(d + hd * MEM_HD, d + (hd + 1) * MEM_HD)
            qm_h = qm_ref[:, cols]
            p = _softmax_rows(qm_h, kv_ref[:, cols])
            datt_h = datt[:, cols]
            dp = _dot(datt_h, kv_ref[:, vcols], NT)
            dsc = (p * (dp - jnp.sum(p * dp, axis=-1, keepdims=True)) * MEM_HD ** -0.5).astype(BF16)
            dqm_ref[:, cols] = _dot(dsc, kv_ref[:, cols]).astype(BF16)
            dkv_ref[:, cols] += _dot(dsc, qm_h, TN)
            dkv_ref[:, vcols] += _dot(p.astype(BF16), datt_h, TN)
        dh = _dot(dqm_ref[...], wq_ref[...], NT)
        dx_ref[...] = _rms_bwd(dh, xh, r, gv) + dxo
        _accumulate(dg_ref, first, jnp.sum(dh * xh, axis=0, keepdims=True))

    return pl.pallas_call(
        body,
        name="xattn_bwd",
        grid=(t // tm,),
        in_specs=[
            _rows(tm, d), _full((1, d)), _rows(tm, d), _rows(tm, d), _full((m, 2 * d)), _full((d, d)), _full((d, d)),
        ],
        out_specs=[_rows(tm, d), _rows(tm, d), _full((m, 2 * d)), _full((1, d))],
        out_shape=[
            jax.ShapeDtypeStruct((t, d), F32),
            jax.ShapeDtypeStruct((t, d), BF16),
            jax.ShapeDtypeStruct((m, 2 * d), F32),
            jax.ShapeDtypeStruct((1, d), F32),
        ],
        compiler_params=_params(("arbitrary",)),
    )(x, g, dxo, qm, kv, wq, wo)


def _final_loss(x, g, target):
    t, d = x.shape
    tm = min(TOKEN_TILE, t)

    def body(x_ref, g_ref, tgt_ref, dx_ref, loss_ref, dg_ref):
        first = pl.program_id(0) == 0
        gv = g_ref[...]
        y, xh, r = _rms(x_ref[...], gv)
        err = y - tgt_ref[...]
        dy = err * (1.0 / d)
        dx_ref[...] = _rms_bwd(dy, xh, r, gv)
        part = 0.5 * jnp.sum(jnp.sum(err * err, axis=-1, keepdims=True) * (1.0 / d), axis=0, keepdims=True)
        _accumulate(loss_ref, first, jnp.broadcast_to(part, (1, 128)))
        _accumulate(dg_ref, first, jnp.sum(dy * xh, axis=0, keepdims=True))

    return pl.pallas_call(
        body,
        name="final_loss",
        grid=(t // tm,),
        in_specs=[_rows(tm, d), _full((1, d)), _rows(tm, d)],
        out_specs=[_rows(tm, d), _full((1, 128)), _full((1, d))],
        out_shape=[
            jax.ShapeDtypeStruct((t, d), F32),
            jax.ShapeDtypeStruct((1, 128), F32),
            jax.ShapeDtypeStruct((1, d), F32),
        ],
        compiler_params=_params(("arbitrary",)),
    )(x, g, target)


def _mesh_place():
    x, y, c = lax.axis_index("x"), lax.axis_index("y"), lax.axis_index("c")
    return x, y, c, 4 * x + 2 * y + c


def _peer(x, y, c, k):
    px = 1 - x if k & 4 else x
    py = 1 - y if k & 2 else y
    pc = 1 - c if k & 1 else c
    return (px, py, pc), 4 * px + 2 * py + pc


def _all_gather(shards):
    n = len(shards)

    def body(*refs):
        src, dst = refs[:n], refs[n:2 * n]
        send_sems, recv_sems, local_sems = refs[2 * n:]
        x, y, c, me = _mesh_place()
        started = []
        for a in range(n):
            local = pltpu.make_async_copy(src[a], dst[a].at[me], local_sems.at[a])
            local.start()
            started.append(local)
            for k in range(1, N_DEV):
                peer, _ = _peer(x, y, c, k)
                pltpu.make_async_remote_copy(
                    src_ref=src[a], dst_ref=dst[a].at[me], send_sem=send_sems.at[a, k - 1],
                    recv_sem=recv_sems.at[a, k - 1], device_id=peer, device_id_type=MESH_IDS).start()
        for a in range(n):
            started[a].wait()
            for k in range(1, N_DEV):
                peer, peer_index = _peer(x, y, c, k)
                landed = pltpu.make_async_remote_copy(
                    src_ref=src[a], dst_ref=dst[a].at[peer_index], send_sem=send_sems.at[a, k - 1],
                    recv_sem=recv_sems.at[a, k - 1], device_id=peer, device_id_type=MESH_IDS)
                landed.wait_send()
                landed.wait_recv()

    hbm = pl.BlockSpec(memory_space=pltpu.HBM)
    return pl.pallas_call(
        body,
        name="all_gather",
        in_specs=[hbm] * n,
        out_specs=[hbm] * n,
        out_shape=[jax.ShapeDtypeStruct((N_DEV,) + s.shape, s.dtype) for s in shards],
        scratch_shapes=[
            pltpu.SemaphoreType.DMA((n, N_DEV - 1)), pltpu.SemaphoreType.DMA((n, N_DEV - 1)),
            pltpu.SemaphoreType.DMA((n,)),
        ],
        compiler_params=pltpu.CompilerParams(has_side_effects=True),
    )(*shards)


SMALL_LAYOUT = {
    "ffn1_norm": (0, 1, 1024), "mix_norm": (1, 1, 1024), "xattn_norm": (2, 1, 1024), "mem_norm": (3, 1, 1024),
    "ffn2_norm": (4, 1, 1024), "final_norm": (5, 1, 1024), "lb_param": (6, 2, 512), "hgrn_out_norm": (8, 1, 512),
    "conv_w": (9, 3, 512), "loss": (12, 1, 128),
}


def _reduce_scatter(blocks, small):
    n = len(blocks)
    names = list(small)
    width = 1024

    def body(*refs):
        src = refs[:n]
        pieces = refs[n:n + len(names)]
        dst = refs[n + len(names):2 * n + len(names)]
        total_ref = refs[2 * n + len(names)]
        pack, gathered, send_sems, recv_sems, local_sems, small_send, small_recv = refs[2 * n + len(names) + 1:]
        x, y, c, me = _mesh_place()
        pack[...] = jnp.zeros_like(pack)
        for name, piece in zip(names, pieces):
            row, nrows, ncols = SMALL_LAYOUT[name]
            pack[row:row + nrows, 0:ncols] = piece[...]
        for k in range(1, N_DEV):
            peer, _ = _peer(x, y, c, k)
            pltpu.make_async_remote_copy(
                src_ref=pack, dst_ref=gathered.at[me], send_sem=small_send.at[k - 1],
                recv_sem=small_recv.at[k - 1], device_id=peer, device_id_type=MESH_IDS).start()
        started = []
        for a in range(n):
            local = pltpu.make_async_copy(src[a].at[me], dst[a].at[me], local_sems.at[a])
            local.start()
            started.append(local)
            for k in range(1, N_DEV):
                peer, peer_index = _peer(x, y, c, k)
                pltpu.make_async_remote_copy(
                    src_ref=src[a].at[peer_index], dst_ref=dst[a].at[me], send_sem=send_sems.at[a, k - 1],
                    recv_sem=recv_sems.at[a, k - 1], device_id=peer, device_id_type=MESH_IDS).start()
        gathered[me] = pack[...]
        for k in range(1, N_DEV):
            peer, peer_index = _peer(x, y, c, k)
            landed = pltpu.make_async_remote_copy(
                src_ref=pack, dst_ref=gathered.at[peer_index], send_sem=small_send.at[k - 1],
                recv_sem=small_recv.at[k - 1], device_id=peer, device_id_type=MESH_IDS)
            landed.wait_send()
            landed.wait_recv()
        total = gathered[0]
        for j in range(1, N_DEV):
            total = total + gathered[j]
        total_ref[...] = total
        for a in range(n):
            started[a].wait()
            for k in range(1, N_DEV):
                peer, peer_index = _peer(x, y, c, k)
                landed = pltpu.make_async_remote_copy(
                    src_ref=src[a].at[peer_index], dst_ref=dst[a].at[peer_index], send_sem=send_sems.at[a, k - 1],
                    recv_sem=recv_sems.at[a, k - 1], device_id=peer, device_id_type=MESH_IDS)
                landed.wait_send()
                landed.wait_recv()

    hbm = pl.BlockSpec(memory_space=pltpu.HBM)
    vmem = pl.BlockSpec(memory_space=pltpu.VMEM)
    out = pl.pallas_call(
        body,
        name="reduce_scatter",
        in_specs=[hbm] * n + [vmem] * len(names),
        out_specs=[hbm] * n + [vmem],
        out_shape=[jax.ShapeDtypeStruct(b.shape, b.dtype) for b in blocks]
        + [jax.ShapeDtypeStruct((SMALL_ROWS, width), F32)],
        scratch_shapes=[
            pltpu.VMEM((SMALL_ROWS, width), F32), pltpu.VMEM((N_DEV, SMALL_ROWS, width), F32),
            pltpu.SemaphoreType.DMA((n, N_DEV - 1)), pltpu.SemaphoreType.DMA((n, N_DEV - 1)),
            pltpu.SemaphoreType.DMA((n,)),
            pltpu.SemaphoreType.DMA((N_DEV - 1,)), pltpu.SemaphoreType.DMA((N_DEV - 1,)),
        ],
        compiler_params=pltpu.CompilerParams(has_side_effects=True),
    )(*blocks, *[small[k] for k in names])
    return out[:n], out[n]


def _adamw_math(w, g, m, v):
    m = ADAM_B1 * m + (1.0 - ADAM_B1) * g
    v = ADAM_B2 * v + (1.0 - ADAM_B2) * (g * g)
    m_hat = m / (1.0 - ADAM_B1 ** ADAM_STEP)
    v_hat = v / (1.0 - ADAM_B2 ** ADAM_STEP)
    delta = -ADAM_LR * (m_hat / (jnp.sqrt(v_hat) + ADAM_EPS) + ADAM_WD * w)
    return delta, m, v


def _adamw_shard(parts, w, m, v):
    r, c = w.shape
    tr = r
    while tr > 512:
        tr //= 2

    def body(p_ref, w_ref, m_ref, v_ref, g_ref, d_ref, mo_ref, vo_ref):
        g = p_ref[0].astype(F32)
        for j in range(1, N_DEV):
            g = g + p_ref[j].astype(F32)
        delta, mn, vn = _adamw_math(w_ref[...], g, m_ref[...], v_ref[...])
        g_ref[...] = g
        d_ref[...] = delta
        mo_ref[...] = mn
        vo_ref[...] = vn

    tile = pl.BlockSpec((tr, c), lambda i: (i, 0))
    return pl.pallas_call(
        body,
        name="adamw_shard",
        grid=(r // tr,),
        in_specs=[pl.BlockSpec((N_DEV, tr, c), lambda i: (0, i, 0)), tile, tile, tile],
        out_specs=[tile] * 4,
        out_shape=[jax.ShapeDtypeStruct((r, c), F32)] * 4,
        compiler_params=_params(("parallel",)),
    )(parts, w, m, v)


def _adamw_small(gs, ws, ms, vs):
    n = len(gs)

    def body(*refs):
        g_refs, w_refs, m_refs, v_refs = refs[:n], refs[n:2 * n], refs[2 * n:3 * n], refs[3 * n:4 * n]
        d_out, m_out, v_out = refs[4 * n:5 * n], refs[5 * n:6 * n], refs[6 * n:7 * n]
        for i in range(n):
            delta, mn, vn = _adamw_math(w_refs[i][...], g_refs[i][...], m_refs[i][...], v_refs[i][...])
            d_out[i][...] = delta
            m_out[i][...] = mn
            v_out[i][...] = vn

    shapes = [jax.ShapeDtypeStruct(w.shape, F32) for w in ws]
    out = pl.pallas_call(
        body,
        name="adamw_small",
        out_shape=shapes * 3,
        compiler_params=_params(),
    )(*gs, *ws, *ms, *vs)
    return out[:n], out[n:2 * n], out[2 * n:]


TRANSPOSED = ("ffn1_gate", "ffn1_up", "w_in", "ffn2_gate", "ffn2_up", "conv_w")
LARGE = ("ffn1_gate", "ffn1_up", "ffn1_down", "w_in", "w_out", "w_q_mem", "w_kv_mem", "w_o_mem",
         "ffn2_gate", "ffn2_up", "ffn2_down")
SMALL = ("ffn1_norm", "mix_norm", "lb_param", "hgrn_out_norm", "conv_w", "xattn_norm", "mem_norm", "ffn2_norm",
         "final_norm")
WEIGHTS = ("ffn1_norm", "ffn1_gate", "ffn1_up", "ffn1_down", "mix_norm", "w_in", "lb_param", "hgrn_out_norm",
           "conv_w", "w_out", "xattn_norm", "mem_norm", "w_q_mem", "w_kv_mem", "w_o_mem", "ffn2_norm", "ffn2_gate",
           "ffn2_up", "ffn2_down", "final_norm")


def kernel(x, mem, ffn1_norm, ffn1_gate, ffn1_up, ffn1_down, mix_norm, w_in, lb_param, hgrn_out_norm, conv_w, w_out, xattn_norm, mem_norm, w_q_mem, w_kv_mem, w_o_mem, ffn2_norm, ffn2_gate, ffn2_up, ffn2_down, final_norm, loss_target, m_ffn1_norm, m_ffn1_gate, m_ffn1_up, m_ffn1_down, m_mix_norm, m_w_in, m_lb_param, m_hgrn_out_norm, m_conv_w, m_w_out, m_xattn_norm, m_mem_norm, m_w_q_mem, m_w_kv_mem, m_w_o_mem, m_ffn2_norm, m_ffn2_gate, m_ffn2_up, m_ffn2_down, m_final_norm, v_ffn1_norm, v_ffn1_gate, v_ffn1_up, v_ffn1_down, v_mix_norm, v_w_in, v_lb_param, v_hgrn_out_norm, v_conv_w, v_w_out, v_xattn_norm, v_mem_norm, v_w_q_mem, v_w_kv_mem, v_w_o_mem, v_ffn2_norm, v_ffn2_gate, v_ffn2_up, v_ffn2_down, v_final_norm):
    given = dict(locals())
    me = 4 * lax.axis_index("x") + 2 * lax.axis_index("y") + lax.axis_index("c")
    x0, memv, target = x[0], mem[0], loss_target[0]

    def shard(prefix, name):
        v = given[prefix + name]
        if v.ndim == 1:
            return v.reshape(1, -1)
        if v.ndim == 2:
            return v
        return v[0].T if name in TRANSPOSED else v[0]

    w = {name: shard("", name) for name in WEIGHTS}
    m = {name: shard("m_", name) for name in WEIGHTS}
    v = {name: shard("v_", name) for name in WEIGHTS}

    conv_taps, conv_rows = w["conv_w"].shape
    conv_tile = jnp.pad(w["conv_w"], ((0, 8 - conv_taps), (0, 128 - conv_rows)))
    gathered = _all_gather([w[name].astype(BF16) for name in LARGE] + [conv_tile])
    full = {}
    for name, blocks in zip(LARGE, gathered):
        _, r, c = blocks.shape
        full[name] = blocks if name == "w_kv_mem" else blocks.reshape(N_DEV * r, c)
    convw_t = gathered[-1][:, :conv_taps, :conv_rows].transpose(1, 0, 2).reshape(conv_taps, N_DEV * conv_rows)

    x1, a1, b1, s1 = _ffn_fwd(x0, w["ffn1_norm"], full["ffn1_gate"], full["ffn1_up"], full["ffn1_down"])
    x2, z, o_raw, states, ycat = _mix_fwd(
        x1, w["mix_norm"], full["w_in"], w["lb_param"], w["hgrn_out_norm"], convw_t, full["w_out"])
    kv = _memkv_fwd(memv, w["mem_norm"], full["w_kv_mem"])
    x3, hq, qm, att = _xattn_fwd(x2, w["xattn_norm"], full["w_q_mem"], kv, full["w_o_mem"])
    x4, a2, b2, s2 = _ffn_fwd(x3, w["ffn2_norm"], full["ffn2_gate"], full["ffn2_up"], full["ffn2_down"])
    dx4, loss_part, d_final = _final_loss(x4, w["final_norm"], target)

    grads = {}
    dx3, da2, db2, h4, d_ffn2_norm = _ffn_bwd(
        x3, w["ffn2_norm"], dx4, a2, b2, full["ffn2_gate"], full["ffn2_up"], full["ffn2_down"])
    grads["ffn2_down"] = _matmul_tn(s2, dx4, 0.5)
    grads["ffn2_gate"] = _matmul_tn(da2, h4)
    grads["ffn2_up"] = _matmul_tn(db2, h4)
    dx2, dqm, dkv, d_xattn_norm = _xattn_bwd(x2, w["xattn_norm"], dx3, qm, kv, full["w_q_mem"], full["w_o_mem"])
    grads["w_o_mem"] = _matmul_tn(att, dx3)
    grads["w_q_mem"] = _matmul_tn(hq, dqm)
    d_wkv_blocks, d_mem_norm = _memkv_bwd(memv, w["mem_norm"], dkv, full["w_kv_mem"])
    dx1, dz, h2, d_mix_norm, d_lbp, d_gh, d_convw_t = _mix_bwd(
        x1, w["mix_norm"], dx2, z, o_raw, states, full["w_in"], w["lb_param"], w["hgrn_out_norm"], convw_t,
        full["w_out"])
    grads["w_out"] = _matmul_tn(ycat, dx2)
    grads["w_in"] = _matmul_tn(dz, h2)
    dx0, da1, db1, h1, d_ffn1_norm = _ffn_bwd(
        x0, w["ffn1_norm"], dx1, a1, b1, full["ffn1_gate"], full["ffn1_up"], full["ffn1_down"])
    grads["ffn1_down"] = _matmul_tn(s1, dx1, 0.5)
    grads["ffn1_gate"] = _matmul_tn(da1, h1)
    grads["ffn1_up"] = _matmul_tn(db1, h1)

    blocks = [d_wkv_blocks if name == "w_kv_mem" else grads[name].reshape((N_DEV,) + w[name].shape)
              for name in LARGE]
    small_parts = {
        "ffn1_norm": d_ffn1_norm, "mix_norm": d_mix_norm, "xattn_norm": d_xattn_norm, "mem_norm": d_mem_norm,
        "ffn2_norm": d_ffn2_norm, "final_norm": d_final, "lb_param": d_lbp, "hgrn_out_norm": d_gh,
        "conv_w": d_convw_t, "loss": loss_part,
    }
    parts, total = _reduce_scatter(blocks, small_parts)

    g_out, d_out, m_out, v_out = {}, {}, {}, {}
    for name, p in zip(LARGE, parts):
        g_out[name], d_out[name], m_out[name], v_out[name] = _adamw_shard(p, w[name], m[name], v[name])
    g_small = {}
    for name in SMALL:
        row, nrows, ncols = SMALL_LAYOUT[name]
        g_small[name] = total[row:row + nrows, 0:ncols]
    g_small["conv_w"] = lax.dynamic_slice_in_dim(g_small["conv_w"], me * conv_rows, conv_rows, axis=1)
    ds, ms, vs = _adamw_small(
        [g_small[k] for k in SMALL], [w[k] for k in SMALL], [m[k] for k in SMALL], [v[k] for k in SMALL])
    for i, name in enumerate(SMALL):
        g_out[name], d_out[name], m_out[name], v_out[name] = g_small[name], ds[i], ms[i], vs[i]

    def shaped(value, name):
        return (value.T if name in TRANSPOSED else value).reshape(given[name].shape)

    loss = total[SMALL_LAYOUT["loss"][0], 0]
    outs = [loss, dx0.reshape(x.shape)]
    for group in (g_out, d_out, m_out, v_out):
        outs += [shaped(group[name], name) for name in WEIGHTS]
    return tuple(outs)
```

```python
import jax
import jax.numpy as jnp
from jax import lax
from jax.experimental import pallas as pl
from jax.experimental.pallas import tpu as pltpu

F32 = jnp.float32
BF16 = jnp.bfloat16
MESH_IDS = pl.DeviceIdType.MESH

N_DEV = 8
EPS = 1e-6
HGRN_HEADS = 4
HGRN_DK = 128
HGRN_W = 512
CHUNK = 64
MEM_HEADS = 4
MEM_HD = 256
ADAM_LR = 0.001
ADAM_B1 = 0.9
ADAM_B2 = 0.999
ADAM_EPS = 1e-08
ADAM_WD = 0.01
ADAM_STEP = 10

TOKEN_TILE = 256
REDUCE_TILE = 1024
VMEM_LIMIT = 60 * 1024 * 1024
SMALL_ROWS = 16
NT = (((1,), (1,)), ((), ()))
TN = (((0,), (0,)), ((), ()))


def _params(sem=None):
    return pltpu.CompilerParams(dimension_semantics=sem, vmem_limit_bytes=VMEM_LIMIT)


def _dot(a, b, dims=None):
    if dims is None:
        return jnp.dot(a, b, preferred_element_type=F32)
    return lax.dot_general(a, b, dims, preferred_element_type=F32)


def _sigmoid(v):
    return 1.0 / (1.0 + jnp.exp(-v))


def _rms(x, g):
    r = lax.rsqrt(jnp.mean(x * x, axis=-1, keepdims=True) + EPS)
    xh = x * r
    return xh * g, xh, r


def _rms_bwd(dh, xh, r, g):
    dxh = dh * g
    return r * (dxh - xh * jnp.mean(dxh * xh, axis=-1, keepdims=True))


def _full(shape):
    return pl.BlockSpec(shape, lambda *_: (0,) * len(shape))


def _rows(tm, width):
    return pl.BlockSpec((tm, width), lambda i: (i, 0))


def _rows_rev(tm, width, n):
    return pl.BlockSpec((tm, width), lambda i: (n - 1 - i, 0))


def _accumulate(ref, first, value):
    @pl.when(first)
    def _():
        ref[...] = value

    @pl.when(jnp.logical_not(first))
    def _():
        ref[...] += value


class _Exchange:
    def __init__(self, operands, out_shapes, scratch, start, finish):
        self.operands, self.out_shapes, self.scratch = list(operands), list(out_shapes), list(scratch)
        self.start, self.finish = start, finish


def _call(body, *, name, grid, in_specs, out_specs, out_shape, args, scratch_shapes=(), exchange=None):
    semantics = ("arbitrary",) * len(grid)
    if exchange is None:
        out = pl.pallas_call(
            body, name=name, grid=grid, in_specs=in_specs, out_specs=out_specs, out_shape=out_shape,
            scratch_shapes=list(scratch_shapes), compiler_params=_params(semantics))(*args)
        return out, []
    hbm = pl.BlockSpec(memory_space=pltpu.HBM)
    n_in, n_out, n_scr = len(in_specs), len(out_specs), len(scratch_shapes)
    e_in, e_out = len(exchange.operands), len(exchange.out_shapes)
    last = grid[0] - 1

    def carried(*refs):
        ins, rest = refs[:n_in], refs[n_in:]
        e_ins, rest = rest[:e_in], rest[e_in:]
        outs, rest = rest[:n_out], rest[n_out:]
        e_outs, rest = rest[:e_out], rest[e_out:]
        scr, e_scr = rest[:n_scr], rest[n_scr:]

        @pl.when(pl.program_id(0) == 0)
        def _():
            exchange.start(e_ins, e_outs, e_scr)

        body(*ins, *outs, *scr)

        @pl.when(pl.program_id(0) == last)
        def _():
            exchange.finish(e_ins, e_outs, e_scr)

    out = pl.pallas_call(
        carried, name=name, grid=grid, in_specs=list(in_specs) + [hbm] * e_in,
        out_specs=list(out_specs) + [hbm] * e_out, out_shape=list(out_shape) + exchange.out_shapes,
        scratch_shapes=list(scratch_shapes) + exchange.scratch,
        compiler_params=pltpu.CompilerParams(
            dimension_semantics=semantics, vmem_limit_bytes=VMEM_LIMIT, has_side_effects=True),
    )(*args, *exchange.operands)
    return out[:n_out], out[n_out:]


def _run_exchange(exchange, name):
    hbm = pl.BlockSpec(memory_space=pltpu.HBM)
    e_in, e_out = len(exchange.operands), len(exchange.out_shapes)

    def body(*refs):
        e_ins, e_outs, e_scr = refs[:e_in], refs[e_in:e_in + e_out], refs[e_in + e_out:]
        exchange.start(e_ins, e_outs, e_scr)
        exchange.finish(e_ins, e_outs, e_scr)

    return pl.pallas_call(
        body, name=name, in_specs=[hbm] * e_in, out_specs=[hbm] * e_out, out_shape=exchange.out_shapes,
        scratch_shapes=exchange.scratch, compiler_params=pltpu.CompilerParams(has_side_effects=True),
    )(*exchange.operands)


def _ffn_fwd(x, g, wg, wu, wd, exchange=None):
    t, d = x.shape
    f = wg.shape[0]
    tm = min(TOKEN_TILE, t)

    def body(x_ref, g_ref, wg_ref, wu_ref, wd_ref, xo_ref, a_ref, b_ref, s_ref):
        xv = x_ref[...]
        h, _, _ = _rms(xv, g_ref[...])
        hb = h.astype(BF16)
        a = _dot(hb, wg_ref[...], NT)
        b = _dot(hb, wu_ref[...], NT)
        s = (a * _sigmoid(a) * b).astype(BF16)
        xo_ref[...] = xv + 0.5 * _dot(s, wd_ref[...])
        a_ref[...] = a.astype(BF16)
        b_ref[...] = b.astype(BF16)
        s_ref[...] = s

    return _call(
        body,
        name="ffn_fwd",
        grid=(t // tm,),
        in_specs=[_rows(tm, d), _full((1, d)), _full((f, d)), _full((f, d)), _full((f, d))],
        out_specs=[_rows(tm, d), _rows(tm, f), _rows(tm, f), _rows(tm, f)],
        out_shape=[
            jax.ShapeDtypeStruct((t, d), F32),
            jax.ShapeDtypeStruct((t, f), BF16),
            jax.ShapeDtypeStruct((t, f), BF16),
            jax.ShapeDtypeStruct((t, f), BF16),
        ],
        args=(x, g, wg, wu, wd),
        exchange=exchange,
    )


def _ffn_bwd(x, g, dxo, a, b, wg, wu, wd, exchange=None):
    t, d = x.shape
    f = wg.shape[0]
    tm = min(TOKEN_TILE, t)

    def body(x_ref, g_ref, dxo_ref, a_ref, b_ref, wg_ref, wu_ref, wd_ref, dx_ref, da_ref, db_ref, h_ref, dg_ref):
        gv = g_ref[...]
        h, xh, r = _rms(x_ref[...], gv)
        dxo = dxo_ref[...]
        ds = _dot((0.5 * dxo).astype(BF16), wd_ref[...], NT)
        af = a_ref[...].astype(F32)
        bf = b_ref[...].astype(F32)
        sg = _sigmoid(af)
        da = (ds * bf * (sg * (1.0 + af * (1.0 - sg)))).astype(BF16)
        db = (ds * (af * sg)).astype(BF16)
        dh = _dot(da, wg_ref[...]) + _dot(db, wu_ref[...])
        dx_ref[...] = _rms_bwd(dh, xh, r, gv) + dxo
        da_ref[...] = da
        db_ref[...] = db
        h_ref[...] = h.astype(BF16)
        _accumulate(dg_ref, pl.program_id(0) == 0, jnp.sum(dh * xh, axis=0, keepdims=True))

    return _call(
        body,
        name="ffn_bwd",
        grid=(t // tm,),
        in_specs=[
            _rows(tm, d), _full((1, d)), _rows(tm, d), _rows(tm, f), _rows(tm, f),
            _full((f, d)), _full((f, d)), _full((f, d)),
        ],
        out_specs=[_rows(tm, d), _rows(tm, f), _rows(tm, f), _rows(tm, d), _full((1, d))],
        out_shape=[
            jax.ShapeDtypeStruct((t, d), F32),
            jax.ShapeDtypeStruct((t, f), BF16),
            jax.ShapeDtypeStruct((t, f), BF16),
            jax.ShapeDtypeStruct((t, d), BF16),
            jax.ShapeDtypeStruct((1, d), F32),
        ],
        args=(x, g, dxo, a, b, wg, wu, wd),
        exchange=exchange,
    )


def _matmul_tn(a, b, scale=1.0):
    t, m = a.shape
    n = b.shape[1]
    tk = min(REDUCE_TILE, t)
    nb = n
    while m * nb * 4 > 12 * 1024 * 1024:
        nb //= 2
    assert n % nb == 0 and nb % 128 == 0

    def body(a_ref, b_ref, o_ref, acc):
        bv = b_ref[...]
        if scale != 1.0:
            bv = bv * scale
        part = _dot(a_ref[...].astype(BF16), bv.astype(BF16), TN)
        _accumulate(acc, pl.program_id(1) == 0, part)

        @pl.when(pl.program_id(1) == t // tk - 1)
        def _():
            o_ref[...] = acc[...].astype(BF16)

    return pl.pallas_call(
        body,
        name="matmul_tn",
        grid=(n // nb, t // tk),
        in_specs=[pl.BlockSpec((tk, m), lambda j, k: (k, 0)), pl.BlockSpec((tk, nb), lambda j, k: (k, j))],
        out_specs=pl.BlockSpec((m, nb), lambda j, k: (0, j)),
        out_shape=jax.ShapeDtypeStruct((m, n), BF16),
        scratch_shapes=[pltpu.VMEM((m, nb), F32)],
        compiler_params=_params(("parallel", "arbitrary")),
    )(a, b)


def _chunk_cumsum(v, reverse=False):
    n = v.shape[0]
    pos = lax.broadcasted_iota(jnp.int32, (n, 1), 0) % CHUNK
    shift = 1
    while shift < CHUNK:
        if reverse:
            moved = pltpu.roll(v, n - shift, axis=0)
            v = v + jnp.where(pos < CHUNK - shift, moved, 0.0)
        else:
            moved = pltpu.roll(v, shift, axis=0)
            v = v + jnp.where(pos >= shift, moved, 0.0)
        shift *= 2
    return v


def _shift_rows(v, shift, edge):
    n = v.shape[0]
    row = lax.broadcasted_iota(jnp.int32, (n, 1), 0)
    out = pltpu.roll(v, shift % n, axis=0)
    if shift > 0:
        for j in range(shift):
            out = jnp.where(row == j, edge[8 - shift + j:8 - shift + j + 1, :], out)
    else:
        for j in range(-shift):
            out = jnp.where(row == n + shift + j, edge[j:j + 1, :], out)
    return out


def _gates(z, lbp):
    w = HGRN_W
    lb = _sigmoid(lbp[0:1, :] - lbp[1:2, :])
    zq = z[:, 0:w]
    sig = _sigmoid(z[:, w:2 * w])
    f = lb + (1.0 - lb) * sig
    sq = _sigmoid(zq)
    q = zq * sq * HGRN_DK ** -0.5
    return lb, sig, f, sq, q


def _short_conv(u, edge, cw):
    return cw[0:1, :] * _shift_rows(u, 2, edge) + cw[1:2, :] * _shift_rows(u, 1, edge) + cw[2:3, :] * u


def _causal_mask():
    row = lax.broadcasted_iota(jnp.int32, (CHUNK, CHUNK), 0)
    col = lax.broadcasted_iota(jnp.int32, (CHUNK, CHUNK), 1)
    return col <= row


def _mix_fwd(x, g, w_in, lbp, gh, convw_t, w_out, exchange=None):
    t, d = x.shape
    zw = w_in.shape[0]
    w = HGRN_W
    tm = min(TOKEN_TILE, t)
    nc = tm // CHUNK
    n_chunks = t // CHUNK

    def body(x_ref, g_ref, win_ref, lbp_ref, gh_ref, cw_ref, wout_ref,
             xo_ref, z_ref, o_ref, st_ref, y_ref, state, ucarry):
        @pl.when(pl.program_id(0) == 0)
        def _():
            state[...] = jnp.zeros_like(state)
            ucarry[...] = jnp.zeros_like(ucarry)

        xv = x_ref[...]
        h, _, _ = _rms(xv, g_ref[...])
        z_ref[...] = _dot(h.astype(BF16), win_ref[...], NT)
        z = z_ref[...]
        _, _, f, _, q = _gates(z, lbp_ref[...])
        bcum = _chunk_cumsum(jnp.log(f))
        kk = 1.0 - f
        vv = z[:, 2 * w:3 * w]
        mask = _causal_mask()
        for c in range(nc):
            rows = slice(c * CHUNK, (c + 1) * CHUNK)
            for hd in range(HGRN_HEADS):
                cols = slice(hd * HGRN_DK, (hd + 1) * HGRN_DK)
                b = bcum[rows, cols]
                blast = b[CHUNK - 1:CHUNK, :]
                qh = (q[rows, cols] * jnp.exp(b)).astype(BF16)
                kh = (kk[rows, cols] * jnp.exp(-b)).astype(BF16)
                kbar = (kk[rows, cols] * jnp.exp(blast - b)).astype(BF16)
                vb = vv[rows, cols].astype(BF16)
                st = state[hd]
                st_ref[c, hd] = st
                att = jnp.where(mask, _dot(qh, kh, NT), 0.0).astype(BF16)
                o_ref[rows, cols] = _dot(att, vb) + _dot(qh, st.astype(BF16), NT)
                state[hd] = st * jnp.exp(blast) + _dot(vb, kbar, TN)
        ghv = gh_ref[...]
        for hd in range(HGRN_HEADS):
            cols = slice(hd * HGRN_DK, (hd + 1) * HGRN_DK)
            on, _, _ = _rms(o_ref[:, cols], ghv[:, cols])
            zg = z[:, 3 * w + hd * HGRN_DK:3 * w + (hd + 1) * HGRN_DK]
            y_ref[:, cols] = (on * (zg * _sigmoid(zg))).astype(BF16)
        u = z[:, 5 * w:6 * w] * z[:, 6 * w:7 * w]
        conv = _short_conv(u, ucarry[...], cw_ref[...])
        ucarry[...] = u[tm - 8:tm, :]
        y_ref[:, w:2 * w] = (z[:, 4 * w:5 * w] * conv).astype(BF16)
        xo_ref[...] = xv + _dot(y_ref[...], wout_ref[...])

    return _call(
        body,
        name="mix_fwd",
        grid=(t // tm,),
        in_specs=[
            _rows(tm, d), _full((1, d)), _full((zw, d)), _full((2, w)), _full((1, w)), _full((3, w)),
            _full((2 * w, d)),
        ],
        out_specs=[
            _rows(tm, d), _rows(tm, zw), _rows(tm, w),
            pl.BlockSpec((nc, HGRN_HEADS, HGRN_DK, HGRN_DK), lambda i: (i, 0, 0, 0)),
            _rows(tm, 2 * w),
        ],
        out_shape=[
            jax.ShapeDtypeStruct((t, d), F32),
            jax.ShapeDtypeStruct((t, zw), F32),
            jax.ShapeDtypeStruct((t, w), F32),
            jax.ShapeDtypeStruct((n_chunks, HGRN_HEADS, HGRN_DK, HGRN_DK), F32),
            jax.ShapeDtypeStruct((t, 2 * w), BF16),
        ],
        scratch_shapes=[pltpu.VMEM((HGRN_HEADS, HGRN_DK, HGRN_DK), F32), pltpu.VMEM((8, w), F32)],
        args=(x, g, w_in, lbp, gh, convw_t, w_out),
        exchange=exchange,
    )


def _mix_bwd(x, g, dxo, z, o, states, w_in, lbp, gh, convw_t, w_out, exchange=None):
    t, d = x.shape
    zw = w_in.shape[0]
    w = HGRN_W
    tm = min(TOKEN_TILE, t)
    nc = tm // CHUNK
    n = t // tm

    def body(x_ref, g_ref, dxo_ref, z_ref, zprev_ref, o_ref, st_ref, win_ref, lbp_ref, gh_ref, cw_ref, wout_ref,
             dx_ref, dz_ref, h_ref, dg_ref, dlbp_ref, dgh_ref, dcw_ref,
             dstate, dcarry, do_buf, dq_buf, dk_buf, db_buf):
        first = pl.program_id(0) == 0

        @pl.when(first)
        def _():
            dstate[...] = jnp.zeros_like(dstate)
            dcarry[...] = jnp.zeros_like(dcarry)

        gv = g_ref[...]
        h, xh, r = _rms(x_ref[...], gv)
        h_ref[...] = h.astype(BF16)
        dxo = dxo_ref[...]
        dy = _dot(dxo.astype(BF16), wout_ref[...], NT)
        z = z_ref[...]
        lb, sig, f, sq, q = _gates(z, lbp_ref[...])
        bcum = _chunk_cumsum(jnp.log(f))
        kk = 1.0 - f
        vv = z[:, 2 * w:3 * w]

        ghv = gh_ref[...]
        dgh_parts = []
        for hd in range(HGRN_HEADS):
            cols = slice(hd * HGRN_DK, (hd + 1) * HGRN_DK)
            gcols = slice(3 * w + hd * HGRN_DK, 3 * w + (hd + 1) * HGRN_DK)
            on, oh, rr = _rms(o_ref[:, cols], ghv[:, cols])
            zg = z[:, gcols]
            sgz = _sigmoid(zg)
            dyh = dy[:, cols]
            don = dyh * (zg * sgz)
            dz_ref[:, gcols] = (dyh * on * (sgz * (1.0 + zg * (1.0 - sgz)))).astype(BF16)
            dgh_parts.append(jnp.sum(don * oh, axis=0, keepdims=True))
            do_buf[:, cols] = _rms_bwd(don, oh, rr, ghv[:, cols])
        _accumulate(dgh_ref, first, jnp.concatenate(dgh_parts, axis=1))

        zb = z[:, 4 * w:5 * w]
        zc = z[:, 5 * w:6 * w]
        zu = z[:, 6 * w:7 * w]
        u = zc * zu
        cw = cw_ref[...]
        zp = zprev_ref[...]
        uprev = jnp.where(pl.program_id(0) == n - 1, 0.0, zp[:, 5 * w:6 * w] * zp[:, 6 * w:7 * w])
        dyc = dy[:, w:2 * w]
        dz_ref[:, 4 * w:5 * w] = (dyc * _short_conv(u, uprev, cw)).astype(BF16)
        dconv = dyc * zb
        edge = dcarry[...]
        dconv1 = _shift_rows(dconv, -1, edge)
        dconv2 = _shift_rows(dconv, -2, edge)
        dcarry[...] = dconv[0:8, :]
        du = cw[2:3, :] * dconv + cw[1:2, :] * dconv1 + cw[0:1, :] * dconv2
        dz_ref[:, 5 * w:6 * w] = (du * zu).astype(BF16)
        dz_ref[:, 6 * w:7 * w] = (du * zc).astype(BF16)
        _accumulate(dcw_ref, first, jnp.concatenate([
            jnp.sum(u * dconv2, axis=0, keepdims=True),
            jnp.sum(u * dconv1, axis=0, keepdims=True),
            jnp.sum(u * dconv, axis=0, keepdims=True)], axis=0))

        mask = _causal_mask()
        last_row = lax.broadcasted_iota(jnp.int32, (CHUNK, 1), 0) == CHUNK - 1
        for c in reversed(range(nc)):
            rows = slice(c * CHUNK, (c + 1) * CHUNK)
            for hd in range(HGRN_HEADS):
                cols = slice(hd * HGRN_DK, (hd + 1) * HGRN_DK)
                b = bcum[rows, cols]
                blast = b[CHUNK - 1:CHUNK, :]
                eb = jnp.exp(b)
                enb = jnp.exp(-b)
                erest = jnp.exp(blast - b)
                elast = jnp.exp(blast)
                qh = q[rows, cols] * eb
                kh = kk[rows, cols] * enb
                kbar = kk[rows, cols] * erest
                qhb = qh.astype(BF16)
                khb = kh.astype(BF16)
                kbarb = kbar.astype(BF16)
                vb = vv[rows, cols].astype(BF16)
                st = st_ref[c, hd]
                dst = dstate[hd]
                dstb = dst.astype(BF16)
                dob = do_buf[rows, cols].astype(BF16)
                att = jnp.where(mask, _dot(qhb, khb, NT), 0.0).astype(BF16)
                datt = jnp.where(mask, _dot(dob, vb, NT), 0.0).astype(BF16)
                dv = _dot(att, dob, TN) + _dot(kbarb, dstb, NT)
                dqh = _dot(datt, khb) + _dot(dob, st.astype(BF16))
                dkh = _dot(datt, qhb, TN)
                dkbar = _dot(vb, dstb)
                dstate[hd] = dst * elast + _dot(dob, qhb, TN)
                kbar_dkbar = kbarb.astype(F32) * dkbar
                db = qhb.astype(F32) * dqh - khb.astype(F32) * dkh - kbar_dkbar
                db_last = (jnp.sum(kbar_dkbar, axis=0, keepdims=True)
                           + jnp.sum(dst * st, axis=0, keepdims=True) * elast)
                db_buf[rows, cols] = jnp.where(last_row, db + db_last, db)
                dq_buf[rows, cols] = dqh * eb
                dk_buf[rows, cols] = dkh * enb + dkbar * erest
                dz_ref[rows, 2 * w + hd * HGRN_DK:2 * w + (hd + 1) * HGRN_DK] = dv.astype(BF16)

        dlogf = _chunk_cumsum(db_buf[...], reverse=True)
        df = dlogf / f - dk_buf[...]
        zq = z[:, 0:w]
        dz_ref[:, 0:w] = (dq_buf[...] * HGRN_DK ** -0.5 * (sq * (1.0 + zq * (1.0 - sq)))).astype(BF16)
        dz_ref[:, w:2 * w] = (df * (1.0 - lb) * sig * (1.0 - sig)).astype(BF16)
        dlb = jnp.sum(df * (1.0 - sig), axis=0, keepdims=True) * lb * (1.0 - lb)
        _accumulate(dlbp_ref, first, jnp.concatenate([dlb, -dlb], axis=0))

        dh = _dot(dz_ref[...], win_ref[...])
        dx_ref[...] = _rms_bwd(dh, xh, r, gv) + dxo
        _accumulate(dg_ref, first, jnp.sum(dh * xh, axis=0, keepdims=True))

    return _call(
        body,
        name="mix_bwd",
        grid=(n,),
        in_specs=[
            _rows_rev(tm, d, n), _full((1, d)), _rows_rev(tm, d, n), _rows_rev(tm, zw, n),
            pl.BlockSpec((8, zw), lambda i: (jnp.maximum((n - 1 - i) * (tm // 8) - 1, 0), 0)),
            _rows_rev(tm, w, n),
            pl.BlockSpec((nc, HGRN_HEADS, HGRN_DK, HGRN_DK), lambda i: (n - 1 - i, 0, 0, 0)),
            _full((zw, d)), _full((2, w)), _full((1, w)), _full((3, w)), _full((2 * w, d)),
        ],
        out_specs=[
            _rows_rev(tm, d, n), _rows_rev(tm, zw, n), _rows_rev(tm, d, n),
            _full((1, d)), _full((2, w)), _full((1, w)), _full((3, w)),
        ],
        out_shape=[
            jax.ShapeDtypeStruct((t, d), F32),
            jax.ShapeDtypeStruct((t, zw), BF16),
            jax.ShapeDtypeStruct((t, d), BF16),
            jax.ShapeDtypeStruct((1, d), F32),
            jax.ShapeDtypeStruct((2, w), F32),
            jax.ShapeDtypeStruct((1, w), F32),
            jax.ShapeDtypeStruct((3, w), F32),
        ],
        scratch_shapes=[
            pltpu.VMEM((HGRN_HEADS, HGRN_DK, HGRN_DK), F32), pltpu.VMEM((8, w), F32),
            pltpu.VMEM((tm, w), F32), pltpu.VMEM((tm, w), F32), pltpu.VMEM((tm, w), F32), pltpu.VMEM((tm, w), F32),
        ],
        args=(x, g, dxo, z, z, o, states, w_in, lbp, gh, convw_t, w_out),
        exchange=exchange,
    )


def _memkv_fwd(mem, g, wkv):
    m, d = mem.shape
    nb, _, cb = wkv.shape

    def body(mem_ref, g_ref, wkv_ref, kv_ref):
        mn, _, _ = _rms(mem_ref[...], g_ref[...])
        mnb = mn.astype(BF16)
        for j in range(nb):
            kv_ref[:, j * cb:(j + 1) * cb] = _dot(mnb, wkv_ref[j]).astype(BF16)

    return pl.pallas_call(
        body,
        name="memkv_fwd",
        out_shape=jax.ShapeDtypeStruct((m, nb * cb), BF16),
        compiler_params=_params(),
    )(mem, g, wkv)


def _memkv_bwd(mem, g, dkv, wkv):
    m, d = mem.shape
    nb, _, cb = wkv.shape

    def body(mem_ref, g_ref, dkv_ref, wkv_ref, dw_ref, dg_ref):
        mn, xh, _ = _rms(mem_ref[...], g_ref[...])
        mnb = mn.astype(BF16)
        dmn = jnp.zeros((m, d), F32)
        for j in range(nb):
            dkvb = dkv_ref[:, j * cb:(j + 1) * cb].astype(BF16)
            dw_ref[j] = _dot(mnb, dkvb, TN).astype(BF16)
            dmn = dmn + _dot(dkvb, wkv_ref[j], NT)
        dg_ref[...] = jnp.sum(dmn * xh, axis=0, keepdims=True)

    return pl.pallas_call(
        body,
        name="memkv_bwd",
        out_shape=[jax.ShapeDtypeStruct((nb, d, cb), BF16), jax.ShapeDtypeStruct((1, d), F32)],
        compiler_params=_params(),
    )(mem, g, dkv, wkv)


def _softmax_rows(qm_h, k_h):
    sc = _dot(qm_h, k_h, NT) * MEM_HD ** -0.5
    e = jnp.exp(sc - jnp.max(sc, axis=-1, keepdims=True))
    return e / jnp.sum(e, axis=-1, keepdims=True)


def _xattn_fwd(x, g, wq, kv, wo):
    t, d = x.shape
    m = kv.shape[0]
    tm = min(TOKEN_TILE, t)

    def body(x_ref, g_ref, wq_ref, kv_ref, wo_ref, xo_ref, hq_ref, qm_ref, att_ref):
        xv = x_ref[...]
        h, _, _ = _rms(xv, g_ref[...])
        hq_ref[...] = h.astype(BF16)
        qm_ref[...] = _dot(hq_ref[...], wq_ref[...]).astype(BF16)
        for hd in range(MEM_HEADS):
            cols = slice(hd * MEM_HD, (hd + 1) * MEM_HD)
            p = _softmax_rows(qm_ref[:, cols], kv_ref[:, cols])
            att_ref[:, cols] = _dot(p.astype(BF16), kv_ref[:, d + hd * MEM_HD:d + (hd + 1) * MEM_HD]).astype(BF16)
        xo_ref[...] = xv + _dot(att_ref[...], wo_ref[...])

    return pl.pallas_call(
        body,
        name="xattn_fwd",
        grid=(t // tm,),
        in_specs=[_rows(tm, d), _full((1, d)), _full((d, d)), _full((m, 2 * d)), _full((d, d))],
        out_specs=[_rows(tm, d), _rows(tm, d), _rows(tm, d), _rows(tm, d)],
        out_shape=[
            jax.ShapeDtypeStruct((t, d), F32),
            jax.ShapeDtypeStruct((t, d), BF16),
            jax.ShapeDtypeStruct((t, d), BF16),
            jax.ShapeDtypeStruct((t, d), BF16),
        ],
        compiler_params=_params(("arbitrary",)),
    )(x, g, wq, kv, wo)


def _xattn_bwd(x, g, dxo, qm, kv, wq, wo, exchange=None):
    t, d = x.shape
    m = kv.shape[0]
    tm = min(TOKEN_TILE, t)

    def body(x_ref, g_ref, dxo_ref, qm_ref, kv_ref, wq_ref, wo_ref, dx_ref, dqm_ref, dkv_ref, dg_ref):
        first = pl.program_id(0) == 0

        @pl.when(first)
        def _():
            dkv_ref[...] = jnp.zeros_like(dkv_ref)

        gv = g_ref[...]
        _, xh, r = _rms(x_ref[...], gv)
        dxo = dxo_ref[...]
        datt = _dot(dxo.astype(BF16), wo_ref[...], NT).astype(BF16)
        for hd in range(MEM_HEADS):
            cols = slice(hd * MEM_HD, (hd + 1) * MEM_HD)
            vcols = slice(d + hd * MEM_HD, d + (hd + 1) * MEM_HD)
            qm_h = qm_ref[:, cols]
            p = _softmax_rows(qm_h, kv_ref[:, cols])
            datt_h = datt[:, cols]
            dp = _dot(datt_h, kv_ref[:, vcols], NT)
            dsc = (p * (dp - jnp.sum(p * dp, axis=-1, keepdims=True)) * MEM_HD ** -0.5).astype(BF16)
            dqm_ref[:, cols] = _dot(dsc, kv_ref[:, cols]).astype(BF16)
            dkv_ref[:, cols] += _dot(dsc, qm_h, TN)
            dkv_ref[:, vcols] += _dot(p.astype(BF16), datt_h, TN)
        dh = _dot(dqm_ref[...], wq_ref[...], NT)
        dx_ref[...] = _rms_bwd(dh, xh, r, gv) + dxo
        _accumulate(dg_ref, first, jnp.sum(dh * xh, axis=0, keepdims=True))

    return _call(
        body,
        name="xattn_bwd",
        grid=(t // tm,),
        in_specs=[
            _rows(tm, d), _full((1, d)), _rows(tm, d), _rows(tm, d), _full((m, 2 * d)), _full((d, d)), _full((d, d)),
        ],
        out_specs=[_rows(tm, d), _rows(tm, d), _full((m, 2 * d)), _full((1, d))],
        out_shape=[
            jax.ShapeDtypeStruct((t, d), F32),
            jax.ShapeDtypeStruct((t, d), BF16),
            jax.ShapeDtypeStruct((m, 2 * d), F32),
            jax.ShapeDtypeStruct((1, d), F32),
        ],
        args=(x, g, dxo, qm, kv, wq, wo),
        exchange=exchange,
    )


def _final_loss(x, g, target):
    t, d = x.shape
    tm = min(TOKEN_TILE, t)

    def body(x_ref, g_ref, tgt_ref, dx_ref, loss_ref, dg_ref):
        first = pl.program_id(0) == 0
        gv = g_ref[...]
        y, xh, r = _rms(x_ref[...], gv)
        err = y - tgt_ref[...]
        dy = err * (1.0 / d)
        dx_ref[...] = _rms_bwd(dy, xh, r, gv)
        part = 0.5 * jnp.sum(jnp.sum(err * err, axis=-1, keepdims=True) * (1.0 / d), axis=0, keepdims=True)
        _accumulate(loss_ref, first, jnp.broadcast_to(part, (1, 128)))
        _accumulate(dg_ref, first, jnp.sum(dy * xh, axis=0, keepdims=True))

    return pl.pallas_call(
        body,
        name="final_loss",
        grid=(t // tm,),
        in_specs=[_rows(tm, d), _full((1, d)), _rows(tm, d)],
        out_specs=[_rows(tm, d), _full((1, 128)), _full((1, d))],
        out_shape=[
            jax.ShapeDtypeStruct((t, d), F32),
            jax.ShapeDtypeStruct((1, 128), F32),
            jax.ShapeDtypeStruct((1, d), F32),
        ],
        compiler_params=_params(("arbitrary",)),
    )(x, g, target)


def _mesh_place():
    x, y, c = lax.axis_index("x"), lax.axis_index("y"), lax.axis_index("c")
    return x, y, c, 4 * x + 2 * y + c


def _peer(x, y, c, k):
    px = 1 - x if k & 4 else x
    py = 1 - y if k & 2 else y
    pc = 1 - c if k & 1 else c
    return (px, py, pc), 4 * px + 2 * py + pc


ICI_HOPS = (2, 4, 6)
N_HOPS = len(ICI_HOPS)


def _remote(src, dst, send_sem, recv_sem, peer):
    return pltpu.make_async_remote_copy(
        src_ref=src, dst_ref=dst, send_sem=send_sem, recv_sem=recv_sem, device_id=peer, device_id_type=MESH_IDS)


def _gather_exchange(shards):
    n = len(shards)

    def start(src, dst, sems):
        ici_send, ici_recv, pair_send, pair_recv, local = sems
        x, y, c, me = _mesh_place()
        sibling, _ = _peer(x, y, c, 1)
        for a in range(n):
            pltpu.make_async_copy(src[a], dst[a].at[me], local.at[a]).start()
            for j, k in enumerate(ICI_HOPS):
                peer, _ = _peer(x, y, c, k)
                _remote(src[a], dst[a].at[me], ici_send.at[a, j], ici_recv.at[a, j], peer).start()
            _remote(src[a], dst[a].at[me], pair_send.at[a, 0], pair_recv.at[a, 0], sibling).start()

    def finish(src, dst, sems):
        ici_send, ici_recv, pair_send, pair_recv, local = sems
        x, y, c, me = _mesh_place()
        sibling, sibling_index = _peer(x, y, c, 1)
        for a in range(n):
            for j, k in enumerate(ICI_HOPS):
                peer, peer_index = _peer(x, y, c, k)
                slot = dst[a].at[peer_index]
                _remote(src[a], slot, ici_send.at[a, j], ici_recv.at[a, j], peer).wait_recv()
                _remote(slot, slot, pair_send.at[a, 1 + j], pair_recv.at[a, 1 + j], sibling).start()
        for a in range(n):
            pltpu.make_async_copy(src[a], dst[a].at[me], local.at[a]).wait()
            for j, k in enumerate(ICI_HOPS):
                peer, _ = _peer(x, y, c, k)
                _remote(src[a], dst[a].at[me], ici_send.at[a, j], ici_recv.at[a, j], peer).wait_send()
            for j, k in enumerate((0,) + ICI_HOPS):
                _, from_sibling = _peer(x, y, c, k | 1)
                passed = _remote(src[a], dst[a].at[from_sibling], pair_send.at[a, j], pair_recv.at[a, j], sibling)
                passed.wait_send()
                passed.wait_recv()

    return _Exchange(
        shards,
        [jax.ShapeDtypeStruct((N_DEV,) + s.shape, s.dtype) for s in shards],
        [
            pltpu.SemaphoreType.DMA((n, N_HOPS)), pltpu.SemaphoreType.DMA((n, N_HOPS)),
            pltpu.SemaphoreType.DMA((n, N_HOPS + 1)), pltpu.SemaphoreType.DMA((n, N_HOPS + 1)),
            pltpu.SemaphoreType.DMA((n,)),
        ],
        start, finish)


def _pair_exchange(blocks):
    n = len(blocks)
    chips = N_DEV // 2

    def copies(src, dst, sems):
        send, recv = sems
        x, y, c, _ = _mesh_place()
        sibling, _ = _peer(x, y, c, 1)
        return [_remote(src[a].at[2 * q + (1 - c)], dst[a].at[q], send.at[a, q], recv.at[a, q], sibling)
                for a in range(n) for q in range(chips)]

    def start(src, dst, sems):
        for cp in copies(src, dst, sems):
            cp.start()

    def finish(src, dst, sems):
        for cp in copies(src, dst, sems):
            cp.wait_send()
            cp.wait_recv()

    return _Exchange(
        blocks,
        [jax.ShapeDtypeStruct((chips,) + b.shape[1:], b.dtype) for b in blocks],
        [pltpu.SemaphoreType.DMA((n, chips)), pltpu.SemaphoreType.DMA((n, chips))],
        start, finish)


def _pair_add(blocks, received, core):
    _, r, c = blocks.shape
    chips = N_DEV // 2
    tr = r
    while tr > 512:
        tr //= 2

    def body(core_ref, mine_ref, got_ref, o_ref):
        o_ref[...] = (mine_ref[...].astype(F32) + got_ref[...].astype(F32)).astype(BF16)

    return pl.pallas_call(
        body,
        name="pair_add",
        grid_spec=pltpu.PrefetchScalarGridSpec(
            num_scalar_prefetch=1,
            grid=(chips, r // tr),
            in_specs=[
                pl.BlockSpec((None, None, tr, c), lambda q, i, core_ref: (q, core_ref[0], i, 0)),
                pl.BlockSpec((None, tr, c), lambda q, i, core_ref: (q, i, 0)),
            ],
            out_specs=pl.BlockSpec((None, tr, c), lambda q, i, core_ref: (q, i, 0)),
        ),
        out_shape=jax.ShapeDtypeStruct((chips, r, c), BF16),
        compiler_params=_params(("parallel", "parallel")),
    )(core, blocks.reshape(chips, 2, r, c), received)


def _scatter_copies(src, dst, sems, n):
    send, recv, local = sems
    x, y, c, _ = _mesh_place()
    chip = 2 * x + y
    local_copies = [pltpu.make_async_copy(src[a].at[chip], dst[a].at[chip], local.at[a]) for a in range(n)]
    sends, arrivals = [], []
    for a in range(n):
        for j, k in enumerate(ICI_HOPS):
            peer, _ = _peer(x, y, c, k)
            peer_chip = 2 * peer[0] + peer[1]
            sends.append(_remote(src[a].at[peer_chip], dst[a].at[chip], send.at[a, j], recv.at[a, j], peer))
            arrivals.append(_remote(src[a].at[peer_chip], dst[a].at[peer_chip], send.at[a, j], recv.at[a, j], peer))
    return local_copies, sends, arrivals


def _scatter_scratch(n):
    return [pltpu.SemaphoreType.DMA((n, N_HOPS)), pltpu.SemaphoreType.DMA((n, N_HOPS)), pltpu.SemaphoreType.DMA((n,))]


def _scatter_exchange(partials):
    n = len(partials)

    def start(src, dst, sems):
        local_copies, sends, _ = _scatter_copies(src, dst, sems, n)
        for cp in local_copies + sends:
            cp.start()

    def finish(src, dst, sems):
        local_copies, sends, arrivals = _scatter_copies(src, dst, sems, n)
        for cp in local_copies:
            cp.wait()
        for cp in sends:
            cp.wait_send()
        for cp in arrivals:
            cp.wait_recv()

    return _Exchange(
        partials, [jax.ShapeDtypeStruct(p.shape, p.dtype) for p in partials], _scatter_scratch(n), start, finish)


SMALL_LAYOUT = {
    "ffn1_norm": (0, 1, 1024), "mix_norm": (1, 1, 1024), "xattn_norm": (2, 1, 1024), "mem_norm": (3, 1, 1024),
    "ffn2_norm": (4, 1, 1024), "final_norm": (5, 1, 1024), "lb_param": (6, 2, 512), "hgrn_out_norm": (8, 1, 512),
    "conv_w": (9, 3, 512), "loss": (12, 1, 128),
}


def _final_exchange(partials, small):
    n = len(partials)
    names = list(small)
    width = 1024

    def body(*refs):
        src = refs[:n]
        pieces = refs[n:n + len(names)]
        dst = refs[n + len(names):2 * n + len(names)]
        total_ref = refs[2 * n + len(names)]
        pack, gathered, small_send, small_recv = refs[2 * n + len(names) + 1:2 * n + len(names) + 5]
        sems = refs[2 * n + len(names) + 5:]
        x, y, c, me = _mesh_place()
        pack[...] = jnp.zeros_like(pack)
        for name, piece in zip(names, pieces):
            row, nrows, ncols = SMALL_LAYOUT[name]
            pack[row:row + nrows, 0:ncols] = piece[...]
        for k in range(1, N_DEV):
            peer, _ = _peer(x, y, c, k)
            _remote(pack, gathered.at[me], small_send.at[k - 1], small_recv.at[k - 1], peer).start()
        local_copies, sends, arrivals = _scatter_copies(src, dst, sems, n)
        for cp in local_copies + sends:
            cp.start()
        gathered[me] = pack[...]
        for k in range(1, N_DEV):
            peer, peer_index = _peer(x, y, c, k)
            landed = _remote(pack, gathered.at[peer_index], small_send.at[k - 1], small_recv.at[k - 1], peer)
            landed.wait_send()
            landed.wait_recv()
        total = gathered[0]
        for j in range(1, N_DEV):
            total = total + gathered[j]
        total_ref[...] = total
        for cp in local_copies:
            cp.wait()
        for cp in sends:
            cp.wait_send()
        for cp in arrivals:
            cp.wait_recv()

    hbm = pl.BlockSpec(memory_space=pltpu.HBM)
    vmem = pl.BlockSpec(memory_space=pltpu.VMEM)
    out = pl.pallas_call(
        body,
        name="final_exchange",
        in_specs=[hbm] * n + [vmem] * len(names),
        out_specs=[hbm] * n + [vmem],
        out_shape=[jax.ShapeDtypeStruct(p.shape, p.dtype) for p in partials]
        + [jax.ShapeDtypeStruct((SMALL_ROWS, width), F32)],
        scratch_shapes=[
            pltpu.VMEM((SMALL_ROWS, width), F32), pltpu.VMEM((N_DEV, SMALL_ROWS, width), F32),
            pltpu.SemaphoreType.DMA((N_DEV - 1,)), pltpu.SemaphoreType.DMA((N_DEV - 1,)),
        ] + _scatter_scratch(n),
        compiler_params=pltpu.CompilerParams(has_side_effects=True),
    )(*partials, *[small[k] for k in names])
    return out[:n], out[n]


def _adamw_math(w, g, m, v):
    m = ADAM_B1 * m + (1.0 - ADAM_B1) * g
    v = ADAM_B2 * v + (1.0 - ADAM_B2) * (g * g)
    m_hat = m / (1.0 - ADAM_B1 ** ADAM_STEP)
    v_hat = v / (1.0 - ADAM_B2 ** ADAM_STEP)
    delta = -ADAM_LR * (m_hat / (jnp.sqrt(v_hat) + ADAM_EPS) + ADAM_WD * w)
    return delta, m, v


def _adamw_shard(parts, w, m, v):
    r, c = w.shape
    n_parts = parts.shape[0]
    tr = r
    while tr > 512:
        tr //= 2

    def body(p_ref, w_ref, m_ref, v_ref, g_ref, d_ref, mo_ref, vo_ref):
        g = p_ref[0].astype(F32)
        for j in range(1, n_parts):
            g = g + p_ref[j].astype(F32)
        delta, mn, vn = _adamw_math(w_ref[...], g, m_ref[...], v_ref[...])
        g_ref[...] = g
        d_ref[...] = delta
        mo_ref[...] = mn
        vo_ref[...] = vn

    tile = pl.BlockSpec((tr, c), lambda i: (i, 0))
    return pl.pallas_call(
        body,
        name="adamw_shard",
        grid=(r // tr,),
        in_specs=[pl.BlockSpec((n_parts, tr, c), lambda i: (0, i, 0)), tile, tile, tile],
        out_specs=[tile] * 4,
        out_shape=[jax.ShapeDtypeStruct((r, c), F32)] * 4,
        compiler_params=_params(("parallel",)),
    )(parts, w, m, v)


def _adamw_small(gs, ws, ms, vs):
    n = len(gs)

    def body(*refs):
        g_refs, w_refs, m_refs, v_refs = refs[:n], refs[n:2 * n], refs[2 * n:3 * n], refs[3 * n:4 * n]
        d_out, m_out, v_out = refs[4 * n:5 * n], refs[5 * n:6 * n], refs[6 * n:7 * n]
        for i in range(n):
            delta, mn, vn = _adamw_math(w_refs[i][...], g_refs[i][...], m_refs[i][...], v_refs[i][...])
            d_out[i][...] = delta
            m_out[i][...] = mn
            v_out[i][...] = vn

    shapes = [jax.ShapeDtypeStruct(w.shape, F32) for w in ws]
    out = pl.pallas_call(
        body,
        name="adamw_small",
        out_shape=shapes * 3,
        compiler_params=_params(),
    )(*gs, *ws, *ms, *vs)
    return out[:n], out[n:2 * n], out[2 * n:]


TRANSPOSED = ("ffn1_gate", "ffn1_up", "w_in", "ffn2_gate", "ffn2_up", "conv_w")
GROUP_FFN1 = ("ffn1_gate", "ffn1_up", "ffn1_down")
GROUP_MIX = ("w_in", "w_out")
GROUP_XATTN = ("w_q_mem", "w_kv_mem", "w_o_mem")
GROUP_FFN2 = ("ffn2_gate", "ffn2_up", "ffn2_down")
LARGE = GROUP_FFN1 + GROUP_MIX + GROUP_XATTN + GROUP_FFN2
SMALL = ("ffn1_norm", "mix_norm", "lb_param", "hgrn_out_norm", "conv_w", "xattn_norm", "mem_norm", "ffn2_norm",
         "final_norm")
WEIGHTS = ("ffn1_norm", "ffn1_gate", "ffn1_up", "ffn1_down", "mix_norm", "w_in", "lb_param", "hgrn_out_norm",
           "conv_w", "w_out", "xattn_norm", "mem_norm", "w_q_mem", "w_kv_mem", "w_o_mem", "ffn2_norm", "ffn2_gate",
           "ffn2_up", "ffn2_down", "final_norm")


def kernel(x, mem, ffn1_norm, ffn1_gate, ffn1_up, ffn1_down, mix_norm, w_in, lb_param, hgrn_out_norm, conv_w, w_out, xattn_norm, mem_norm, w_q_mem, w_kv_mem, w_o_mem, ffn2_norm, ffn2_gate, ffn2_up, ffn2_down, final_norm, loss_target, m_ffn1_norm, m_ffn1_gate, m_ffn1_up, m_ffn1_down, m_mix_norm, m_w_in, m_lb_param, m_hgrn_out_norm, m_conv_w, m_w_out, m_xattn_norm, m_mem_norm, m_w_q_mem, m_w_kv_mem, m_w_o_mem, m_ffn2_norm, m_ffn2_gate, m_ffn2_up, m_ffn2_down, m_final_norm, v_ffn1_norm, v_ffn1_gate, v_ffn1_up, v_ffn1_down, v_mix_norm, v_w_in, v_lb_param, v_hgrn_out_norm, v_conv_w, v_w_out, v_xattn_norm, v_mem_norm, v_w_q_mem, v_w_kv_mem, v_w_o_mem, v_ffn2_norm, v_ffn2_gate, v_ffn2_up, v_ffn2_down, v_final_norm):
    given = dict(locals())
    me = 4 * lax.axis_index("x") + 2 * lax.axis_index("y") + lax.axis_index("c")
    x0, memv, target = x[0], mem[0], loss_target[0]

    def shard(prefix, name):
        v = given[prefix + name]
        if v.ndim == 1:
            return v.reshape(1, -1)
        if v.ndim == 2:
            return v
        return v[0].T if name in TRANSPOSED else v[0]

    w = {name: shard("", name) for name in WEIGHTS}
    m = {name: shard("m_", name) for name in WEIGHTS}
    v = {name: shard("v_", name) for name in WEIGHTS}

    conv_taps, conv_rows = w["conv_w"].shape
    conv_tile = jnp.pad(w["conv_w"], ((0, 8 - conv_taps), (0, 128 - conv_rows)))
    wire = {name: w[name].astype(BF16) for name in LARGE}
    full = {}

    def landed(names, gathered):
        for name, blocks in zip(names, gathered):
            _, r, c = blocks.shape
            full[name] = blocks if name == "w_kv_mem" else blocks.reshape(N_DEV * r, c)

    landed(GROUP_FFN1, _run_exchange(_gather_exchange([wire[k] for k in GROUP_FFN1]), "gather_first"))

    (x1, a1, b1, s1), gathered = _ffn_fwd(
        x0, w["ffn1_norm"], full["ffn1_gate"], full["ffn1_up"], full["ffn1_down"],
        exchange=_gather_exchange([wire[k] for k in GROUP_MIX + GROUP_XATTN] + [conv_tile]))
    landed(GROUP_MIX + GROUP_XATTN, gathered)
    convw_t = gathered[-1][:, :conv_taps, :conv_rows].transpose(1, 0, 2).reshape(conv_taps, N_DEV * conv_rows)
    (x2, z, o_raw, states, ycat), gathered = _mix_fwd(
        x1, w["mix_norm"], full["w_in"], w["lb_param"], w["hgrn_out_norm"], convw_t, full["w_out"],
        exchange=_gather_exchange([wire[k] for k in GROUP_FFN2]))
    landed(GROUP_FFN2, gathered)
    kv = _memkv_fwd(memv, w["mem_norm"], full["w_kv_mem"])
    x3, hq, qm, att = _xattn_fwd(x2, w["xattn_norm"], full["w_q_mem"], kv, full["w_o_mem"])
    (x4, a2, b2, s2), _ = _ffn_fwd(x3, w["ffn2_norm"], full["ffn2_gate"], full["ffn2_up"], full["ffn2_down"])
    dx4, loss_part, d_final = _final_loss(x4, w["final_norm"], target)

    core = lax.axis_index("c").astype(jnp.int32).reshape(1)
    grads = {}

    def chip_sums(names):
        blocks = [grads[name].reshape((N_DEV,) + w[name].shape) for name in names]
        received = _run_exchange(_pair_exchange(blocks), "pair_exchange")
        return [_pair_add(b, r, core) for b, r in zip(blocks, received)]

    (dx3, da2, db2, h4, d_ffn2_norm), _ = _ffn_bwd(
        x3, w["ffn2_norm"], dx4, a2, b2, full["ffn2_gate"], full["ffn2_up"], full["ffn2_down"])
    grads["ffn2_down"] = _matmul_tn(s2, dx4, 0.5)
    grads["ffn2_gate"] = _matmul_tn(da2, h4)
    grads["ffn2_up"] = _matmul_tn(db2, h4)
    parts = {}
    (dx2, dqm, dkv, d_xattn_norm), arrived = _xattn_bwd(
        x2, w["xattn_norm"], dx3, qm, kv, full["w_q_mem"], full["w_o_mem"],
        exchange=_scatter_exchange(chip_sums(GROUP_FFN2)))
    parts.update(zip(GROUP_FFN2, arrived))
    grads["w_o_mem"] = _matmul_tn(att, dx3)
    grads["w_q_mem"] = _matmul_tn(hq, dqm)
    grads["w_kv_mem"], d_mem_norm = _memkv_bwd(memv, w["mem_norm"], dkv, full["w_kv_mem"])
    (dx1, dz, h2, d_mix_norm, d_lbp, d_gh, d_convw_t), arrived = _mix_bwd(
        x1, w["mix_norm"], dx2, z, o_raw, states, full["w_in"], w["lb_param"], w["hgrn_out_norm"], convw_t,
        full["w_out"], exchange=_scatter_exchange(chip_sums(GROUP_XATTN)))
    parts.update(zip(GROUP_XATTN, arrived))
    grads["w_out"] = _matmul_tn(ycat, dx2)
    grads["w_in"] = _matmul_tn(dz, h2)
    (dx0, da1, db1, h1, d_ffn1_norm), arrived = _ffn_bwd(
        x0, w["ffn1_norm"], dx1, a1, b1, full["ffn1_gate"], full["ffn1_up"], full["ffn1_down"],
        exchange=_scatter_exchange(chip_sums(GROUP_MIX)))
    parts.update(zip(GROUP_MIX, arrived))
    grads["ffn1_down"] = _matmul_tn(s1, dx1, 0.5)
    grads["ffn1_gate"] = _matmul_tn(da1, h1)
    grads["ffn1_up"] = _matmul_tn(db1, h1)

    small_parts = {
        "ffn1_norm": d_ffn1_norm, "mix_norm": d_mix_norm, "xattn_norm": d_xattn_norm, "mem_norm": d_mem_norm,
        "ffn2_norm": d_ffn2_norm, "final_norm": d_final, "lb_param": d_lbp, "hgrn_out_norm": d_gh,
        "conv_w": d_convw_t, "loss": loss_part,
    }
    arrived, total = _final_exchange(chip_sums(GROUP_FFN1), small_parts)
    parts.update(zip(GROUP_FFN1, arrived))

    g_out, d_out, m_out, v_out = {}, {}, {}, {}
    for name in LARGE:
        g_out[name], d_out[name], m_out[name], v_out[name] = _adamw_shard(parts[name], w[name], m[name], v[name])
    g_small = {}
    for name in SMALL:
        row, nrows, ncols = SMALL_LAYOUT[name]
        g_small[name] = total[row:row + nrows, 0:ncols]
    g_small["conv_w"] = lax.dynamic_slice_in_dim(g_small["conv_w"], me * conv_rows, conv_rows, axis=1)
    ds, ms, vs = _adamw_small(
        [g_small[k] for k in SMALL], [w[k] for k in SMALL], [m[k] for k in SMALL], [v[k] for k in SMALL])
    for i, name in enumerate(SMALL):
        g_out[name], d_out[name], m_out[name], v_out[name] = g_small[name], ds[i], ms[i], vs[i]

    def shaped(value, name):
        return (value.T if name in TRANSPOSED else value).reshape(given[name].shape)

    loss = total[SMALL_LAYOUT["loss"][0], 0]
    outs = [loss, dx0.reshape(x.shape)]
    for group in (g_out, d_out, m_out, v_out):
        outs += [shaped(group[name], name) for name in WEIGHTS]
    return tuple(outs)
```

```python
import jax
import jax.numpy as jnp
from jax import lax
from jax.experimental import pallas as pl
from jax.experimental.pallas import tpu as pltpu

F32 = jnp.float32
BF16 = jnp.bfloat16
MESH_IDS = pl.DeviceIdType.MESH

N_DEV = 8
EPS = 1e-6
HGRN_HEADS = 4
HGRN_DK = 128
HGRN_W = 512
CHUNK = 64
MEM_HEADS = 4
MEM_HD = 256
ADAM_LR = 0.001
ADAM_B1 = 0.9
ADAM_B2 = 0.999
ADAM_EPS = 1e-08
ADAM_WD = 0.01
ADAM_STEP = 10

TOKEN_TILE = 256
REDUCE_TILE = 1024
VMEM_LIMIT = 60 * 1024 * 1024
SMALL_ROWS = 16
NT = (((1,), (1,)), ((), ()))
TN = (((0,), (0,)), ((), ()))


def _params(sem=None):
    return pltpu.CompilerParams(dimension_semantics=sem, vmem_limit_bytes=VMEM_LIMIT)


def _dot(a, b, dims=None):
    if dims is None:
        return jnp.dot(a, b, preferred_element_type=F32)
    return lax.dot_general(a, b, dims, preferred_element_type=F32)


def _sigmoid(v):
    return 1.0 / (1.0 + jnp.exp(-v))


def _rms(x, g):
    r = lax.rsqrt(jnp.mean(x * x, axis=-1, keepdims=True) + EPS)
    xh = x * r
    return xh * g, xh, r


def _rms_bwd(dh, xh, r, g):
    dxh = dh * g
    return r * (dxh - xh * jnp.mean(dxh * xh, axis=-1, keepdims=True))


def _full(shape):
    return pl.BlockSpec(shape, lambda *_: (0,) * len(shape))


def _rows(tm, width):
    return pl.BlockSpec((tm, width), lambda i: (i, 0))


def _rows_rev(tm, width, n):
    return pl.BlockSpec((tm, width), lambda i: (n - 1 - i, 0))


def _accumulate(ref, first, value):
    @pl.when(first)
    def _():
        ref[...] = value

    @pl.when(jnp.logical_not(first))
    def _():
        ref[...] += value


class _Exchange:
    def __init__(self, operands, out_shapes, scratch, start, finish):
        self.operands, self.out_shapes, self.scratch = list(operands), list(out_shapes), list(scratch)
        self.start, self.finish = start, finish


def _call(body, *, name, grid, in_specs, out_specs, out_shape, args, scratch_shapes=(), exchange=None):
    semantics = ("arbitrary",) * len(grid)
    if exchange is None:
        out = pl.pallas_call(
            body, name=name, grid=grid, in_specs=in_specs, out_specs=out_specs, out_shape=out_shape,
            scratch_shapes=list(scratch_shapes), compiler_params=_params(semantics))(*args)
        return out, []
    hbm = pl.BlockSpec(memory_space=pltpu.HBM)
    n_in, n_out, n_scr = len(in_specs), len(out_specs), len(scratch_shapes)
    e_in, e_out = len(exchange.operands), len(exchange.out_shapes)

    def carried(*refs):
        ins, rest = refs[:n_in], refs[n_in:]
        e_ins, rest = rest[:e_in], rest[e_in:]
        outs, rest = rest[:n_out], rest[n_out:]
        e_outs, rest = rest[:e_out], rest[e_out:]
        scr, e_scr = rest[:n_scr], rest[n_scr:]
        first = last = None
        for axis, size in enumerate(grid):
            at_start, at_end = pl.program_id(axis) == 0, pl.program_id(axis) == size - 1
            first = at_start if first is None else jnp.logical_and(first, at_start)
            last = at_end if last is None else jnp.logical_and(last, at_end)

        @pl.when(first)
        def _():
            exchange.start(e_ins, e_outs, e_scr)

        body(*ins, *outs, *scr)

        @pl.when(last)
        def _():
            exchange.finish(e_ins, e_outs, e_scr)

    out = pl.pallas_call(
        carried, name=name, grid=grid, in_specs=list(in_specs) + [hbm] * e_in,
        out_specs=list(out_specs) + [hbm] * e_out, out_shape=list(out_shape) + exchange.out_shapes,
        scratch_shapes=list(scratch_shapes) + exchange.scratch,
        compiler_params=pltpu.CompilerParams(
            dimension_semantics=semantics, vmem_limit_bytes=VMEM_LIMIT, has_side_effects=True),
    )(*args, *exchange.operands)
    return out[:n_out], out[n_out:]


def _run_exchange(exchange, name):
    hbm = pl.BlockSpec(memory_space=pltpu.HBM)
    e_in, e_out = len(exchange.operands), len(exchange.out_shapes)

    def body(*refs):
        e_ins, e_outs, e_scr = refs[:e_in], refs[e_in:e_in + e_out], refs[e_in + e_out:]
        exchange.start(e_ins, e_outs, e_scr)
        exchange.finish(e_ins, e_outs, e_scr)

    return pl.pallas_call(
        body, name=name, in_specs=[hbm] * e_in, out_specs=[hbm] * e_out, out_shape=exchange.out_shapes,
        scratch_shapes=exchange.scratch, compiler_params=pltpu.CompilerParams(has_side_effects=True),
    )(*exchange.operands)


def _ffn_fwd(x, g, wg, wu, wd, exchange=None):
    t, d = x.shape
    f = wg.shape[0]
    tm = min(TOKEN_TILE, t)

    def body(x_ref, g_ref, wg_ref, wu_ref, wd_ref, xo_ref, a_ref, b_ref, s_ref):
        xv = x_ref[...]
        h, _, _ = _rms(xv, g_ref[...])
        hb = h.astype(BF16)
        a = _dot(hb, wg_ref[...], NT)
        b = _dot(hb, wu_ref[...], NT)
        s = (a * _sigmoid(a) * b).astype(BF16)
        xo_ref[...] = xv + 0.5 * _dot(s, wd_ref[...])
        a_ref[...] = a.astype(BF16)
        b_ref[...] = b.astype(BF16)
        s_ref[...] = s

    return _call(
        body,
        name="ffn_fwd",
        grid=(t // tm,),
        in_specs=[_rows(tm, d), _full((1, d)), _full((f, d)), _full((f, d)), _full((f, d))],
        out_specs=[_rows(tm, d), _rows(tm, f), _rows(tm, f), _rows(tm, f)],
        out_shape=[
            jax.ShapeDtypeStruct((t, d), F32),
            jax.ShapeDtypeStruct((t, f), BF16),
            jax.ShapeDtypeStruct((t, f), BF16),
            jax.ShapeDtypeStruct((t, f), BF16),
        ],
        args=(x, g, wg, wu, wd),
        exchange=exchange,
    )


def _ffn_bwd(x, g, dxo, a, b, wg, wu, wd, exchange=None):
    t, d = x.shape
    f = wg.shape[0]
    tm = min(TOKEN_TILE, t)

    def body(x_ref, g_ref, dxo_ref, a_ref, b_ref, wg_ref, wu_ref, wd_ref, dx_ref, da_ref, db_ref, h_ref, dg_ref):
        gv = g_ref[...]
        h, xh, r = _rms(x_ref[...], gv)
        dxo = dxo_ref[...]
        ds = _dot((0.5 * dxo).astype(BF16), wd_ref[...], NT)
        af = a_ref[...].astype(F32)
        bf = b_ref[...].astype(F32)
        sg = _sigmoid(af)
        da = (ds * bf * (sg * (1.0 + af * (1.0 - sg)))).astype(BF16)
        db = (ds * (af * sg)).astype(BF16)
        dh = _dot(da, wg_ref[...]) + _dot(db, wu_ref[...])
        dx_ref[...] = _rms_bwd(dh, xh, r, gv) + dxo
        da_ref[...] = da
        db_ref[...] = db
        h_ref[...] = h.astype(BF16)
        _accumulate(dg_ref, pl.program_id(0) == 0, jnp.sum(dh * xh, axis=0, keepdims=True))

    return _call(
        body,
        name="ffn_bwd",
        grid=(t // tm,),
        in_specs=[
            _rows(tm, d), _full((1, d)), _rows(tm, d), _rows(tm, f), _rows(tm, f),
            _full((f, d)), _full((f, d)), _full((f, d)),
        ],
        out_specs=[_rows(tm, d), _rows(tm, f), _rows(tm, f), _rows(tm, d), _full((1, d))],
        out_shape=[
            jax.ShapeDtypeStruct((t, d), F32),
            jax.ShapeDtypeStruct((t, f), BF16),
            jax.ShapeDtypeStruct((t, f), BF16),
            jax.ShapeDtypeStruct((t, d), BF16),
            jax.ShapeDtypeStruct((1, d), F32),
        ],
        args=(x, g, dxo, a, b, wg, wu, wd),
        exchange=exchange,
    )


def _weight_grad(a, b, scale=1.0, exchange=None):
    t, m = a.shape
    n = b.shape[1]
    chips = N_DEV // 2
    r = m // N_DEV
    tk = min(REDUCE_TILE, t)
    halves = 2
    nb = n // halves
    nk = t // tk

    def body(a_ref, b_ref, o_ref, acc, mine, send_buf, recv_buf, send_sems, recv_sems):
        j, k = pl.program_id(0), pl.program_id(1)
        x, y, c, _ = _mesh_place()
        sibling, _ = _peer(x, y, c, 1)
        bv = b_ref[...]
        if scale != 1.0:
            bv = bv * scale
        part = _dot(a_ref[...].astype(BF16), bv.astype(BF16), TN)
        _accumulate(acc, k == 0, part)

        def to_sibling(half):
            return _remote(send_buf.at[half], recv_buf.at[half], send_sems.at[half], recv_sems.at[half], sibling)

        for half in range(halves):
            @pl.when(jnp.logical_and(k == nk - 1, j == half))
            def _():
                for q in range(chips):
                    send_buf[half, q] = acc[pl.ds(pl.multiple_of((2 * q + 1 - c) * r, 8), r), :].astype(BF16)
                    mine[half, q] = acc[pl.ds(pl.multiple_of((2 * q + c) * r, 8), r), :]
                to_sibling(half).start()

        @pl.when(jnp.logical_and(k == nk - 1, j == halves - 1))
        def _():
            for half in range(halves):
                to_sibling(half).wait_send()
                to_sibling(half).wait_recv()
                for q in range(chips):
                    o_ref[q, :, half * nb:(half + 1) * nb] = (
                        mine[half, q] + recv_buf[half, q].astype(F32)).astype(BF16)

    (partial,), arrived = _call(
        body,
        name="weight_grad",
        grid=(halves, nk),
        in_specs=[pl.BlockSpec((tk, m), lambda j, k: (k, 0)), pl.BlockSpec((tk, nb), lambda j, k: (k, j))],
        out_specs=[pl.BlockSpec((chips, r, n), lambda j, k: (0, 0, 0))],
        out_shape=[jax.ShapeDtypeStruct((chips, r, n), BF16)],
        scratch_shapes=[
            pltpu.VMEM((m, nb), F32), pltpu.VMEM((halves, chips, r, nb), F32),
            pltpu.VMEM((halves, chips, r, nb), BF16), pltpu.VMEM((halves, chips, r, nb), BF16),
            pltpu.SemaphoreType.DMA((halves,)), pltpu.SemaphoreType.DMA((halves,)),
        ],
        args=(a, b),
        exchange=exchange,
    )
    return partial, arrived


def _chunk_cumsum(v, reverse=False):
    n = v.shape[0]
    pos = lax.broadcasted_iota(jnp.int32, (n, 1), 0) % CHUNK
    shift = 1
    while shift < CHUNK:
        if reverse:
            moved = pltpu.roll(v, n - shift, axis=0)
            v = v + jnp.where(pos < CHUNK - shift, moved, 0.0)
        else:
            moved = pltpu.roll(v, shift, axis=0)
            v = v + jnp.where(pos >= shift, moved, 0.0)
        shift *= 2
    return v


def _shift_rows(v, shift, edge):
    n = v.shape[0]
    row = lax.broadcasted_iota(jnp.int32, (n, 1), 0)
    out = pltpu.roll(v, shift % n, axis=0)
    if shift > 0:
        for j in range(shift):
            out = jnp.where(row == j, edge[8 - shift + j:8 - shift + j + 1, :], out)
    else:
        for j in range(-shift):
            out = jnp.where(row == n + shift + j, edge[j:j + 1, :], out)
    return out


def _gates(z, lbp):
    w = HGRN_W
    lb = _sigmoid(lbp[0:1, :] - lbp[1:2, :])
    zq = z[:, 0:w]
    sig = _sigmoid(z[:, w:2 * w])
    f = lb + (1.0 - lb) * sig
    sq = _sigmoid(zq)
    q = zq * sq * HGRN_DK ** -0.5
    return lb, sig, f, sq, q


def _short_conv(u, edge, cw):
    return cw[0:1, :] * _shift_rows(u, 2, edge) + cw[1:2, :] * _shift_rows(u, 1, edge) + cw[2:3, :] * u


def _causal_mask():
    row = lax.broadcasted_iota(jnp.int32, (CHUNK, CHUNK), 0)
    col = lax.broadcasted_iota(jnp.int32, (CHUNK, CHUNK), 1)
    return col <= row


def _mix_fwd(x, g, w_in, lbp, gh, convw_t, w_out, exchange=None):
    t, d = x.shape
    zw = w_in.shape[0]
    w = HGRN_W
    tm = min(TOKEN_TILE, t)
    nc = tm // CHUNK
    n_chunks = t // CHUNK

    def body(x_ref, g_ref, win_ref, lbp_ref, gh_ref, cw_ref, wout_ref,
             xo_ref, z_ref, o_ref, st_ref, y_ref, state, ucarry):
        @pl.when(pl.program_id(0) == 0)
        def _():
            state[...] = jnp.zeros_like(state)
            ucarry[...] = jnp.zeros_like(ucarry)

        xv = x_ref[...]
        h, _, _ = _rms(xv, g_ref[...])
        z_ref[...] = _dot(h.astype(BF16), win_ref[...], NT)
        z = z_ref[...]
        _, _, f, _, q = _gates(z, lbp_ref[...])
        bcum = _chunk_cumsum(jnp.log(f))
        kk = 1.0 - f
        vv = z[:, 2 * w:3 * w]
        mask = _causal_mask()
        for c in range(nc):
            rows = slice(c * CHUNK, (c + 1) * CHUNK)
            for hd in range(HGRN_HEADS):
                cols = slice(hd * HGRN_DK, (hd + 1) * HGRN_DK)
                b = bcum[rows, cols]
                blast = b[CHUNK - 1:CHUNK, :]
                qh = (q[rows, cols] * jnp.exp(b)).astype(BF16)
                kh = (kk[rows, cols] * jnp.exp(-b)).astype(BF16)
                kbar = (kk[rows, cols] * jnp.exp(blast - b)).astype(BF16)
                vb = vv[rows, cols].astype(BF16)
                st = state[hd]
                st_ref[c, hd] = st
                att = jnp.where(mask, _dot(qh, kh, NT), 0.0).astype(BF16)
                o_ref[rows, cols] = _dot(att, vb) + _dot(qh, st.astype(BF16), NT)
                state[hd] = st * jnp.exp(blast) + _dot(vb, kbar, TN)
        ghv = gh_ref[...]
        for hd in range(HGRN_HEADS):
            cols = slice(hd * HGRN_DK, (hd + 1) * HGRN_DK)
            on, _, _ = _rms(o_ref[:, cols], ghv[:, cols])
            zg = z[:, 3 * w + hd * HGRN_DK:3 * w + (hd + 1) * HGRN_DK]
            y_ref[:, cols] = (on * (zg * _sigmoid(zg))).astype(BF16)
        u = z[:, 5 * w:6 * w] * z[:, 6 * w:7 * w]
        conv = _short_conv(u, ucarry[...], cw_ref[...])
        ucarry[...] = u[tm - 8:tm, :]
        y_ref[:, w:2 * w] = (z[:, 4 * w:5 * w] * conv).astype(BF16)
        xo_ref[...] = xv + _dot(y_ref[...], wout_ref[...])

    return _call(
        body,
        name="mix_fwd",
        grid=(t // tm,),
        in_specs=[
            _rows(tm, d), _full((1, d)), _full((zw, d)), _full((2, w)), _full((1, w)), _full((3, w)),
            _full((2 * w, d)),
        ],
        out_specs=[
            _rows(tm, d), _rows(tm, zw), _rows(tm, w),
            pl.BlockSpec((nc, HGRN_HEADS, HGRN_DK, HGRN_DK), lambda i: (i, 0, 0, 0)),
            _rows(tm, 2 * w),
        ],
        out_shape=[
            jax.ShapeDtypeStruct((t, d), F32),
            jax.ShapeDtypeStruct((t, zw), F32),
            jax.ShapeDtypeStruct((t, w), F32),
            jax.ShapeDtypeStruct((n_chunks, HGRN_HEADS, HGRN_DK, HGRN_DK), F32),
            jax.ShapeDtypeStruct((t, 2 * w), BF16),
        ],
        scratch_shapes=[pltpu.VMEM((HGRN_HEADS, HGRN_DK, HGRN_DK), F32), pltpu.VMEM((8, w), F32)],
        args=(x, g, w_in, lbp, gh, convw_t, w_out),
        exchange=exchange,
    )


def _mix_bwd(x, g, dxo, z, o, states, w_in, lbp, gh, convw_t, w_out, exchange=None):
    t, d = x.shape
    zw = w_in.shape[0]
    w = HGRN_W
    tm = min(TOKEN_TILE, t)
    nc = tm // CHUNK
    n = t // tm

    def body(x_ref, g_ref, dxo_ref, z_ref, zprev_ref, o_ref, st_ref, win_ref, lbp_ref, gh_ref, cw_ref, wout_ref,
             dx_ref, dz_ref, h_ref, dg_ref, dlbp_ref, dgh_ref, dcw_ref,
             dstate, dcarry, do_buf, dq_buf, dk_buf, db_buf):
        first = pl.program_id(0) == 0

        @pl.when(first)
        def _():
            dstate[...] = jnp.zeros_like(dstate)
            dcarry[...] = jnp.zeros_like(dcarry)

        gv = g_ref[...]
        h, xh, r = _rms(x_ref[...], gv)
        h_ref[...] = h.astype(BF16)
        dxo = dxo_ref[...]
        dy = _dot(dxo.astype(BF16), wout_ref[...], NT)
        z = z_ref[...]
        lb, sig, f, sq, q = _gates(z, lbp_ref[...])
        bcum = _chunk_cumsum(jnp.log(f))
        kk = 1.0 - f
        vv = z[:, 2 * w:3 * w]

        ghv = gh_ref[...]
        dgh_parts = []
        for hd in range(HGRN_HEADS):
            cols = slice(hd * HGRN_DK, (hd + 1) * HGRN_DK)
            gcols = slice(3 * w + hd * HGRN_DK, 3 * w + (hd + 1) * HGRN_DK)
            on, oh, rr = _rms(o_ref[:, cols], ghv[:, cols])
            zg = z[:, gcols]
            sgz = _sigmoid(zg)
            dyh = dy[:, cols]
            don = dyh * (zg * sgz)
            dz_ref[:, gcols] = (dyh * on * (sgz * (1.0 + zg * (1.0 - sgz)))).astype(BF16)
            dgh_parts.append(jnp.sum(don * oh, axis=0, keepdims=True))
            do_buf[:, cols] = _rms_bwd(don, oh, rr, ghv[:, cols])
        _accumulate(dgh_ref, first, jnp.concatenate(dgh_parts, axis=1))

        zb = z[:, 4 * w:5 * w]
        zc = z[:, 5 * w:6 * w]
        zu = z[:, 6 * w:7 * w]
        u = zc * zu
        cw = cw_ref[...]
        zp = zprev_ref[...]
        uprev = jnp.where(pl.program_id(0) == n - 1, 0.0, zp[:, 5 * w:6 * w] * zp[:, 6 * w:7 * w])
        dyc = dy[:, w:2 * w]
        dz_ref[:, 4 * w:5 * w] = (dyc * _short_conv(u, uprev, cw)).astype(BF16)
        dconv = dyc * zb
        edge = dcarry[...]
        dconv1 = _shift_rows(dconv, -1, edge)
        dconv2 = _shift_rows(dconv, -2, edge)
        dcarry[...] = dconv[0:8, :]
        du = cw[2:3, :] * dconv + cw[1:2, :] * dconv1 + cw[0:1, :] * dconv2
        dz_ref[:, 5 * w:6 * w] = (du * zu).astype(BF16)
        dz_ref[:, 6 * w:7 * w] = (du * zc).astype(BF16)
        _accumulate(dcw_ref, first, jnp.concatenate([
            jnp.sum(u * dconv2, axis=0, keepdims=True),
            jnp.sum(u * dconv1, axis=0, keepdims=True),
            jnp.sum(u * dconv, axis=0, keepdims=True)], axis=0))

        mask = _causal_mask()
        last_row = lax.broadcasted_iota(jnp.int32, (CHUNK, 1), 0) == CHUNK - 1
        for c in reversed(range(nc)):
            rows = slice(c * CHUNK, (c + 1) * CHUNK)
            for hd in range(HGRN_HEADS):
                cols = slice(hd * HGRN_DK, (hd + 1) * HGRN_DK)
                b = bcum[rows, cols]
                blast = b[CHUNK - 1:CHUNK, :]
                eb = jnp.exp(b)
                enb = jnp.exp(-b)
                erest = jnp.exp(blast - b)
                elast = jnp.exp(blast)
                qh = q[rows, cols] * eb
                kh = kk[rows, cols] * enb
                kbar = kk[rows, cols] * erest
                qhb = qh.astype(BF16)
                khb = kh.astype(BF16)
                kbarb = kbar.astype(BF16)
                vb = vv[rows, cols].astype(BF16)
                st = st_ref[c, hd]
                dst = dstate[hd]
                dstb = dst.astype(BF16)
                dob = do_buf[rows, cols].astype(BF16)
                att = jnp.where(mask, _dot(qhb, khb, NT), 0.0).astype(BF16)
                datt = jnp.where(mask, _dot(dob, vb, NT), 0.0).astype(BF16)
                dv = _dot(att, dob, TN) + _dot(kbarb, dstb, NT)
                dqh = _dot(datt, khb) + _dot(dob, st.astype(BF16))
                dkh = _dot(datt, qhb, TN)
                dkbar = _dot(vb, dstb)
                dstate[hd] = dst * elast + _dot(dob, qhb, TN)
                kbar_dkbar = kbarb.astype(F32) * dkbar
                db = qhb.astype(F32) * dqh - khb.astype(F32) * dkh - kbar_dkbar
                db_last = (jnp.sum(kbar_dkbar, axis=0, keepdims=True)
                           + jnp.sum(dst * st, axis=0, keepdims=True) * elast)
                db_buf[rows, cols] = jnp.where(last_row, db + db_last, db)
                dq_buf[rows, cols] = dqh * eb
                dk_buf[rows, cols] = dkh * enb + dkbar * erest
                dz_ref[rows, 2 * w + hd * HGRN_DK:2 * w + (hd + 1) * HGRN_DK] = dv.astype(BF16)

        dlogf = _chunk_cumsum(db_buf[...], reverse=True)
        df = dlogf / f - dk_buf[...]
        zq = z[:, 0:w]
        dz_ref[:, 0:w] = (dq_buf[...] * HGRN_DK ** -0.5 * (sq * (1.0 + zq * (1.0 - sq)))).astype(BF16)
        dz_ref[:, w:2 * w] = (df * (1.0 - lb) * sig * (1.0 - sig)).astype(BF16)
        dlb = jnp.sum(df * (1.0 - sig), axis=0, keepdims=True) * lb * (1.0 - lb)
        _accumulate(dlbp_ref, first, jnp.concatenate([dlb, -dlb], axis=0))

        dh = _dot(dz_ref[...], win_ref[...])
        dx_ref[...] = _rms_bwd(dh, xh, r, gv) + dxo
        _accumulate(dg_ref, first, jnp.sum(dh * xh, axis=0, keepdims=True))

    return _call(
        body,
        name="mix_bwd",
        grid=(n,),
        in_specs=[
            _rows_rev(tm, d, n), _full((1, d)), _rows_rev(tm, d, n), _rows_rev(tm, zw, n),
            pl.BlockSpec((8, zw), lambda i: (jnp.maximum((n - 1 - i) * (tm // 8) - 1, 0), 0)),
            _rows_rev(tm, w, n),
            pl.BlockSpec((nc, HGRN_HEADS, HGRN_DK, HGRN_DK), lambda i: (n - 1 - i, 0, 0, 0)),
            _full((zw, d)), _full((2, w)), _full((1, w)), _full((3, w)), _full((2 * w, d)),
        ],
        out_specs=[
            _rows_rev(tm, d, n), _rows_rev(tm, zw, n), _rows_rev(tm, d, n),
            _full((1, d)), _full((2, w)), _full((1, w)), _full((3, w)),
        ],
        out_shape=[
            jax.ShapeDtypeStruct((t, d), F32),
            jax.ShapeDtypeStruct((t, zw), BF16),
            jax.ShapeDtypeStruct((t, d), BF16),
            jax.ShapeDtypeStruct((1, d), F32),
            jax.ShapeDtypeStruct((2, w), F32),
            jax.ShapeDtypeStruct((1, w), F32),
            jax.ShapeDtypeStruct((3, w), F32),
        ],
        scratch_shapes=[
            pltpu.VMEM((HGRN_HEADS, HGRN_DK, HGRN_DK), F32), pltpu.VMEM((8, w), F32),
            pltpu.VMEM((tm, w), F32), pltpu.VMEM((tm, w), F32), pltpu.VMEM((tm, w), F32), pltpu.VMEM((tm, w), F32),
        ],
        args=(x, g, dxo, z, z, o, states, w_in, lbp, gh, convw_t, w_out),
        exchange=exchange,
    )


def _memkv_fwd(mem, g, wkv):
    m, d = mem.shape
    nb, _, cb = wkv.shape

    def body(mem_ref, g_ref, wkv_ref, kv_ref):
        mn, _, _ = _rms(mem_ref[...], g_ref[...])
        mnb = mn.astype(BF16)
        for j in range(nb):
            kv_ref[:, j * cb:(j + 1) * cb] = _dot(mnb, wkv_ref[j]).astype(BF16)

    return pl.pallas_call(
        body,
        name="memkv_fwd",
        out_shape=jax.ShapeDtypeStruct((m, nb * cb), BF16),
        compiler_params=_params(),
    )(mem, g, wkv)


def _memkv_bwd(mem, g, dkv, wkv):
    m, d = mem.shape
    nb, _, cb = wkv.shape

    def body(mem_ref, g_ref, dkv_ref, wkv_ref, dw_ref, dg_ref):
        mn, xh, _ = _rms(mem_ref[...], g_ref[...])
        mnb = mn.astype(BF16)
        dmn = jnp.zeros((m, d), F32)
        for j in range(nb):
            dkvb = dkv_ref[:, j * cb:(j + 1) * cb].astype(BF16)
            dw_ref[j] = _dot(mnb, dkvb, TN).astype(BF16)
            dmn = dmn + _dot(dkvb, wkv_ref[j], NT)
        dg_ref[...] = jnp.sum(dmn * xh, axis=0, keepdims=True)

    return pl.pallas_call(
        body,
        name="memkv_bwd",
        out_shape=[jax.ShapeDtypeStruct((nb, d, cb), BF16), jax.ShapeDtypeStruct((1, d), F32)],
        compiler_params=_params(),
    )(mem, g, dkv, wkv)


def _softmax_rows(qm_h, k_h):
    sc = _dot(qm_h, k_h, NT) * MEM_HD ** -0.5
    e = jnp.exp(sc - jnp.max(sc, axis=-1, keepdims=True))
    return e / jnp.sum(e, axis=-1, keepdims=True)


def _xattn_fwd(x, g, wq, kv, wo):
    t, d = x.shape
    m = kv.shape[0]
    tm = min(TOKEN_TILE, t)

    def body(x_ref, g_ref, wq_ref, kv_ref, wo_ref, xo_ref, hq_ref, qm_ref, att_ref):
        xv = x_ref[...]
        h, _, _ = _rms(xv, g_ref[...])
        hq_ref[...] = h.astype(BF16)
        qm_ref[...] = _dot(hq_ref[...], wq_ref[...]).astype(BF16)
        for hd in range(MEM_HEADS):
            cols = slice(hd * MEM_HD, (hd + 1) * MEM_HD)
            p = _softmax_rows(qm_ref[:, cols], kv_ref[:, cols])
            att_ref[:, cols] = _dot(p.astype(BF16), kv_ref[:, d + hd * MEM_HD:d + (hd + 1) * MEM_HD]).astype(BF16)
        xo_ref[...] = xv + _dot(att_ref[...], wo_ref[...])

    return pl.pallas_call(
        body,
        name="xattn_fwd",
        grid=(t // tm,),
        in_specs=[_rows(tm, d), _full((1, d)), _full((d, d)), _full((m, 2 * d)), _full((d, d))],
        out_specs=[_rows(tm, d), _rows(tm, d), _rows(tm, d), _rows(tm, d)],
        out_shape=[
            jax.ShapeDtypeStruct((t, d), F32),
            jax.ShapeDtypeStruct((t, d), BF16),
            jax.ShapeDtypeStruct((t, d), BF16),
            jax.ShapeDtypeStruct((t, d), BF16),
        ],
        compiler_params=_params(("arbitrary",)),
    )(x, g, wq, kv, wo)


def _xattn_bwd(x, g, dxo, qm, kv, wq, wo, exchange=None):
    t, d = x.shape
    m = kv.shape[0]
    tm = min(TOKEN_TILE, t)

    def body(x_ref, g_ref, dxo_ref, qm_ref, kv_ref, wq_ref, wo_ref, dx_ref, dqm_ref, dkv_ref, dg_ref):
        first = pl.program_id(0) == 0

        @pl.when(first)
        def _():
            dkv_ref[...] = jnp.zeros_like(dkv_ref)

        gv = g_ref[...]
        _, xh, r = _rms(x_ref[...], gv)
        dxo = dxo_ref[...]
        datt = _dot(dxo.astype(BF16), wo_ref[...], NT).astype(BF16)
        for hd in range(MEM_HEADS):
            cols = slice(hd * MEM_HD, (hd + 1) * MEM_HD)
            vcols = slice(d + hd * MEM_HD, d + (hd + 1) * MEM_HD)
            qm_h = qm_ref[:, cols]
            p = _softmax_rows(qm_h, kv_ref[:, cols])
            datt_h = datt[:, cols]
            dp = _dot(datt_h, kv_ref[:, vcols], NT)
            dsc = (p * (dp - jnp.sum(p * dp, axis=-1, keepdims=True)) * MEM_HD ** -0.5).astype(BF16)
            dqm_ref[:, cols] = _dot(dsc, kv_ref[:, cols]).astype(BF16)
            dkv_ref[:, cols] += _dot(dsc, qm_h, TN)
            dkv_ref[:, vcols] += _dot(p.astype(BF16), datt_h, TN)
        dh = _dot(dqm_ref[...], wq_ref[...], NT)
        dx_ref[...] = _rms_bwd(dh, xh, r, gv) + dxo
        _accumulate(dg_ref, first, jnp.sum(dh * xh, axis=0, keepdims=True))

    return _call(
        body,
        name="xattn_bwd",
        grid=(t // tm,),
        in_specs=[
            _rows(tm, d), _full((1, d)), _rows(tm, d), _rows(tm, d), _full((m, 2 * d)), _full((d, d)), _full((d, d)),
        ],
        out_specs=[_rows(tm, d), _rows(tm, d), _full((m, 2 * d)), _full((1, d))],
        out_shape=[
            jax.ShapeDtypeStruct((t, d), F32),
            jax.ShapeDtypeStruct((t, d), BF16),
            jax.ShapeDtypeStruct((m, 2 * d), F32),
            jax.ShapeDtypeStruct((1, d), F32),
        ],
        args=(x, g, dxo, qm, kv, wq, wo),
        exchange=exchange,
    )


def _final_loss(x, g, target):
    t, d = x.shape
    tm = min(TOKEN_TILE, t)

    def body(x_ref, g_ref, tgt_ref, dx_ref, loss_ref, dg_ref):
        first = pl.program_id(0) == 0
        gv = g_ref[...]
        y, xh, r = _rms(x_ref[...], gv)
        err = y - tgt_ref[...]
        dy = err * (1.0 / d)
        dx_ref[...] = _rms_bwd(dy, xh, r, gv)
        part = 0.5 * jnp.sum(jnp.sum(err * err, axis=-1, keepdims=True) * (1.0 / d), axis=0, keepdims=True)
        _accumulate(loss_ref, first, jnp.broadcast_to(part, (1, 128)))
        _accumulate(dg_ref, first, jnp.sum(dy * xh, axis=0, keepdims=True))

    return pl.pallas_call(
        body,
        name="final_loss",
        grid=(t // tm,),
        in_specs=[_rows(tm, d), _full((1, d)), _rows(tm, d)],
        out_specs=[_rows(tm, d), _full((1, 128)), _full((1, d))],
        out_shape=[
            jax.ShapeDtypeStruct((t, d), F32),
            jax.ShapeDtypeStruct((1, 128), F32),
            jax.ShapeDtypeStruct((1, d), F32),
        ],
        compiler_params=_params(("arbitrary",)),
    )(x, g, target)


def _mesh_place():
    x, y, c = lax.axis_index("x"), lax.axis_index("y"), lax.axis_index("c")
    return x, y, c, 4 * x + 2 * y + c


def _peer(x, y, c, k):
    px = 1 - x if k & 4 else x
    py = 1 - y if k & 2 else y
    pc = 1 - c if k & 1 else c
    return (px, py, pc), 4 * px + 2 * py + pc


ICI_HOPS = (2, 4, 6)
N_HOPS = len(ICI_HOPS)


def _remote(src, dst, send_sem, recv_sem, peer):
    return pltpu.make_async_remote_copy(
        src_ref=src, dst_ref=dst, send_sem=send_sem, recv_sem=recv_sem, device_id=peer, device_id_type=MESH_IDS)


def _gather_exchange(shards):
    n = len(shards)

    def start(src, dst, sems):
        ici_send, ici_recv, pair_send, pair_recv, local = sems
        x, y, c, me = _mesh_place()
        sibling, _ = _peer(x, y, c, 1)
        for a in range(n):
            pltpu.make_async_copy(src[a], dst[a].at[me], local.at[a]).start()
            for j, k in enumerate(ICI_HOPS):
                peer, _ = _peer(x, y, c, k)
                _remote(src[a], dst[a].at[me], ici_send.at[a, j], ici_recv.at[a, j], peer).start()
            _remote(src[a], dst[a].at[me], pair_send.at[a, 0], pair_recv.at[a, 0], sibling).start()

    def finish(src, dst, sems):
        ici_send, ici_recv, pair_send, pair_recv, local = sems
        x, y, c, me = _mesh_place()
        sibling, sibling_index = _peer(x, y, c, 1)
        for a in range(n):
            for j, k in enumerate(ICI_HOPS):
                peer, peer_index = _peer(x, y, c, k)
                slot = dst[a].at[peer_index]
                _remote(src[a], slot, ici_send.at[a, j], ici_recv.at[a, j], peer).wait_recv()
                _remote(slot, slot, pair_send.at[a, 1 + j], pair_recv.at[a, 1 + j], sibling).start()
        for a in range(n):
            pltpu.make_async_copy(src[a], dst[a].at[me], local.at[a]).wait()
            for j, k in enumerate(ICI_HOPS):
                peer, _ = _peer(x, y, c, k)
                _remote(src[a], dst[a].at[me], ici_send.at[a, j], ici_recv.at[a, j], peer).wait_send()
            for j, k in enumerate((0,) + ICI_HOPS):
                _, from_sibling = _peer(x, y, c, k | 1)
                passed = _remote(src[a], dst[a].at[from_sibling], pair_send.at[a, j], pair_recv.at[a, j], sibling)
                passed.wait_send()
                passed.wait_recv()

    return _Exchange(
        shards,
        [jax.ShapeDtypeStruct((N_DEV,) + s.shape, s.dtype) for s in shards],
        [
            pltpu.SemaphoreType.DMA((n, N_HOPS)), pltpu.SemaphoreType.DMA((n, N_HOPS)),
            pltpu.SemaphoreType.DMA((n, N_HOPS + 1)), pltpu.SemaphoreType.DMA((n, N_HOPS + 1)),
            pltpu.SemaphoreType.DMA((n,)),
        ],
        start, finish)


def _pair_exchange(blocks):
    n = len(blocks)
    chips = N_DEV // 2

    def copies(src, dst, sems):
        send, recv = sems
        x, y, c, _ = _mesh_place()
        sibling, _ = _peer(x, y, c, 1)
        return [_remote(src[a].at[2 * q + (1 - c)], dst[a].at[q], send.at[a, q], recv.at[a, q], sibling)
                for a in range(n) for q in range(chips)]

    def start(src, dst, sems):
        for cp in copies(src, dst, sems):
            cp.start()

    def finish(src, dst, sems):
        for cp in copies(src, dst, sems):
            cp.wait_send()
            cp.wait_recv()

    return _Exchange(
        blocks,
        [jax.ShapeDtypeStruct((chips,) + b.shape[1:], b.dtype) for b in blocks],
        [pltpu.SemaphoreType.DMA((n, chips)), pltpu.SemaphoreType.DMA((n, chips))],
        start, finish)


def _pair_add(blocks, received, core):
    _, r, c = blocks.shape
    chips = N_DEV // 2
    tr = r
    while tr > 512:
        tr //= 2

    def body(core_ref, mine_ref, got_ref, o_ref):
        o_ref[...] = (mine_ref[...].astype(F32) + got_ref[...].astype(F32)).astype(BF16)

    return pl.pallas_call(
        body,
        name="pair_add",
        grid_spec=pltpu.PrefetchScalarGridSpec(
            num_scalar_prefetch=1,
            grid=(chips, r // tr),
            in_specs=[
                pl.BlockSpec((None, None, tr, c), lambda q, i, core_ref: (q, core_ref[0], i, 0)),
                pl.BlockSpec((None, tr, c), lambda q, i, core_ref: (q, i, 0)),
            ],
            out_specs=pl.BlockSpec((None, tr, c), lambda q, i, core_ref: (q, i, 0)),
        ),
        out_shape=jax.ShapeDtypeStruct((chips, r, c), BF16),
        compiler_params=_params(("parallel", "parallel")),
    )(core, blocks.reshape(chips, 2, r, c), received)


def _scatter_copies(src, dst, sems, n):
    send, recv, local = sems
    x, y, c, _ = _mesh_place()
    chip = 2 * x + y
    local_copies = [pltpu.make_async_copy(src[a].at[chip], dst[a].at[chip], local.at[a]) for a in range(n)]
    sends, arrivals = [], []
    for a in range(n):
        for j, k in enumerate(ICI_HOPS):
            peer, _ = _peer(x, y, c, k)
            peer_chip = 2 * peer[0] + peer[1]
            sends.append(_remote(src[a].at[peer_chip], dst[a].at[chip], send.at[a, j], recv.at[a, j], peer))
            arrivals.append(_remote(src[a].at[peer_chip], dst[a].at[peer_chip], send.at[a, j], recv.at[a, j], peer))
    return local_copies, sends, arrivals


def _scatter_scratch(n):
    return [pltpu.SemaphoreType.DMA((n, N_HOPS)), pltpu.SemaphoreType.DMA((n, N_HOPS)), pltpu.SemaphoreType.DMA((n,))]


def _scatter_exchange(partials):
    n = len(partials)

    def start(src, dst, sems):
        local_copies, sends, _ = _scatter_copies(src, dst, sems, n)
        for cp in local_copies + sends:
            cp.start()

    def finish(src, dst, sems):
        local_copies, sends, arrivals = _scatter_copies(src, dst, sems, n)
        for cp in local_copies:
            cp.wait()
        for cp in sends:
            cp.wait_send()
        for cp in arrivals:
            cp.wait_recv()

    return _Exchange(
        partials, [jax.ShapeDtypeStruct(p.shape, p.dtype) for p in partials], _scatter_scratch(n), start, finish)


SMALL_LAYOUT = {
    "ffn1_norm": (0, 1, 1024), "mix_norm": (1, 1, 1024), "xattn_norm": (2, 1, 1024), "mem_norm": (3, 1, 1024),
    "ffn2_norm": (4, 1, 1024), "final_norm": (5, 1, 1024), "lb_param": (6, 2, 512), "hgrn_out_norm": (8, 1, 512),
    "conv_w": (9, 3, 512), "loss": (12, 1, 128),
}


def _final_exchange(partials, small):
    n = len(partials)
    names = list(small)
    width = 1024

    def body(*refs):
        src = refs[:n]
        pieces = refs[n:n + len(names)]
        dst = refs[n + len(names):2 * n + len(names)]
        total_ref = refs[2 * n + len(names)]
        pack, gathered, small_send, small_recv = refs[2 * n + len(names) + 1:2 * n + len(names) + 5]
        sems = refs[2 * n + len(names) + 5:]
        x, y, c, me = _mesh_place()
        pack[...] = jnp.zeros_like(pack)
        for name, piece in zip(names, pieces):
            row, nrows, ncols = SMALL_LAYOUT[name]
            pack[row:row + nrows, 0:ncols] = piece[...]
        for k in range(1, N_DEV):
            peer, _ = _peer(x, y, c, k)
            _remote(pack, gathered.at[me], small_send.at[k - 1], small_recv.at[k - 1], peer).start()
        local_copies, sends, arrivals = _scatter_copies(src, dst, sems, n)
        for cp in local_copies + sends:
            cp.start()
        gathered[me] = pack[...]
        for k in range(1, N_DEV):
            peer, peer_index = _peer(x, y, c, k)
            landed = _remote(pack, gathered.at[peer_index], small_send.at[k - 1], small_recv.at[k - 1], peer)
            landed.wait_send()
            landed.wait_recv()
        total = gathered[0]
        for j in range(1, N_DEV):
            total = total + gathered[j]
        total_ref[...] = total
        for cp in local_copies:
            cp.wait()
        for cp in sends:
            cp.wait_send()
        for cp in arrivals:
            cp.wait_recv()

    hbm = pl.BlockSpec(memory_space=pltpu.HBM)
    vmem = pl.BlockSpec(memory_space=pltpu.VMEM)
    out = pl.pallas_call(
        body,
        name="final_exchange",
        in_specs=[hbm] * n + [vmem] * len(names),
        out_specs=[hbm] * n + [vmem],
        out_shape=[jax.ShapeDtypeStruct(p.shape, p.dtype) for p in partials]
        + [jax.ShapeDtypeStruct((SMALL_ROWS, width), F32)],
        scratch_shapes=[
            pltpu.VMEM((SMALL_ROWS, width), F32), pltpu.VMEM((N_DEV, SMALL_ROWS, width), F32),
            pltpu.SemaphoreType.DMA((N_DEV - 1,)), pltpu.SemaphoreType.DMA((N_DEV - 1,)),
        ] + _scatter_scratch(n),
        compiler_params=pltpu.CompilerParams(has_side_effects=True),
    )(*partials, *[small[k] for k in names])
    return out[:n], out[n]


def _adamw_math(w, g, m, v):
    m = ADAM_B1 * m + (1.0 - ADAM_B1) * g
    v = ADAM_B2 * v + (1.0 - ADAM_B2) * (g * g)
    m_hat = m / (1.0 - ADAM_B1 ** ADAM_STEP)
    v_hat = v / (1.0 - ADAM_B2 ** ADAM_STEP)
    delta = -ADAM_LR * (m_hat / (jnp.sqrt(v_hat) + ADAM_EPS) + ADAM_WD * w)
    return delta, m, v


def _adamw_shard(parts, w, m, v):
    r, c = w.shape
    n_parts = parts.shape[0]
    tr = r
    while tr > 512:
        tr //= 2

    def body(p_ref, w_ref, m_ref, v_ref, g_ref, d_ref, mo_ref, vo_ref):
        g = p_ref[0].astype(F32)
        for j in range(1, n_parts):
            g = g + p_ref[j].astype(F32)
        delta, mn, vn = _adamw_math(w_ref[...], g, m_ref[...], v_ref[...])
        g_ref[...] = g
        d_ref[...] = delta
        mo_ref[...] = mn
        vo_ref[...] = vn

    tile = pl.BlockSpec((tr, c), lambda i: (i, 0))
    return pl.pallas_call(
        body,
        name="adamw_shard",
        grid=(r // tr,),
        in_specs=[pl.BlockSpec((n_parts, tr, c), lambda i: (0, i, 0)), tile, tile, tile],
        out_specs=[tile] * 4,
        out_shape=[jax.ShapeDtypeStruct((r, c), F32)] * 4,
        compiler_params=_params(("parallel",)),
    )(parts, w, m, v)


def _adamw_small(gs, ws, ms, vs):
    n = len(gs)

    def body(*refs):
        g_refs, w_refs, m_refs, v_refs = refs[:n], refs[n:2 * n], refs[2 * n:3 * n], refs[3 * n:4 * n]
        d_out, m_out, v_out = refs[4 * n:5 * n], refs[5 * n:6 * n], refs[6 * n:7 * n]
        for i in range(n):
            delta, mn, vn = _adamw_math(w_refs[i][...], g_refs[i][...], m_refs[i][...], v_refs[i][...])
            d_out[i][...] = delta
            m_out[i][...] = mn
            v_out[i][...] = vn

    shapes = [jax.ShapeDtypeStruct(w.shape, F32) for w in ws]
    out = pl.pallas_call(
        body,
        name="adamw_small",
        out_shape=shapes * 3,
        compiler_params=_params(),
    )(*gs, *ws, *ms, *vs)
    return out[:n], out[n:2 * n], out[2 * n:]


TRANSPOSED = ("ffn1_gate", "ffn1_up", "w_in", "ffn2_gate", "ffn2_up", "conv_w")
GROUP_FFN1 = ("ffn1_gate", "ffn1_up", "ffn1_down")
GROUP_MIX = ("w_in", "w_out")
GROUP_XATTN = ("w_q_mem", "w_kv_mem", "w_o_mem")
GROUP_FFN2 = ("ffn2_gate", "ffn2_up", "ffn2_down")
LARGE = GROUP_FFN1 + GROUP_MIX + GROUP_XATTN + GROUP_FFN2
SMALL = ("ffn1_norm", "mix_norm", "lb_param", "hgrn_out_norm", "conv_w", "xattn_norm", "mem_norm", "ffn2_norm",
         "final_norm")
WEIGHTS = ("ffn1_norm", "ffn1_gate", "ffn1_up", "ffn1_down", "mix_norm", "w_in", "lb_param", "hgrn_out_norm",
           "conv_w", "w_out", "xattn_norm", "mem_norm", "w_q_mem", "w_kv_mem", "w_o_mem", "ffn2_norm", "ffn2_gate",
           "ffn2_up", "ffn2_down", "final_norm")


def kernel(x, mem, ffn1_norm, ffn1_gate, ffn1_up, ffn1_down, mix_norm, w_in, lb_param, hgrn_out_norm, conv_w, w_out, xattn_norm, mem_norm, w_q_mem, w_kv_mem, w_o_mem, ffn2_norm, ffn2_gate, ffn2_up, ffn2_down, final_norm, loss_target, m_ffn1_norm, m_ffn1_gate, m_ffn1_up, m_ffn1_down, m_mix_norm, m_w_in, m_lb_param, m_hgrn_out_norm, m_conv_w, m_w_out, m_xattn_norm, m_mem_norm, m_w_q_mem, m_w_kv_mem, m_w_o_mem, m_ffn2_norm, m_ffn2_gate, m_ffn2_up, m_ffn2_down, m_final_norm, v_ffn1_norm, v_ffn1_gate, v_ffn1_up, v_ffn1_down, v_mix_norm, v_w_in, v_lb_param, v_hgrn_out_norm, v_conv_w, v_w_out, v_xattn_norm, v_mem_norm, v_w_q_mem, v_w_kv_mem, v_w_o_mem, v_ffn2_norm, v_ffn2_gate, v_ffn2_up, v_ffn2_down, v_final_norm):
    given = dict(locals())
    me = 4 * lax.axis_index("x") + 2 * lax.axis_index("y") + lax.axis_index("c")
    x0, memv, target = x[0], mem[0], loss_target[0]

    def shard(prefix, name):
        v = given[prefix + name]
        if v.ndim == 1:
            return v.reshape(1, -1)
        if v.ndim == 2:
            return v
        return v[0].T if name in TRANSPOSED else v[0]

    w = {name: shard("", name) for name in WEIGHTS}
    m = {name: shard("m_", name) for name in WEIGHTS}
    v = {name: shard("v_", name) for name in WEIGHTS}

    conv_taps, conv_rows = w["conv_w"].shape
    conv_tile = jnp.pad(w["conv_w"], ((0, 8 - conv_taps), (0, 128 - conv_rows)))
    wire = {name: w[name].astype(BF16) for name in LARGE}
    full = {}

    def landed(names, gathered):
        for name, blocks in zip(names, gathered):
            _, r, c = blocks.shape
            full[name] = blocks if name == "w_kv_mem" else blocks.reshape(N_DEV * r, c)

    landed(GROUP_FFN1, _run_exchange(_gather_exchange([wire[k] for k in GROUP_FFN1]), "gather_first"))

    (x1, a1, b1, s1), gathered = _ffn_fwd(
        x0, w["ffn1_norm"], full["ffn1_gate"], full["ffn1_up"], full["ffn1_down"],
        exchange=_gather_exchange([wire[k] for k in GROUP_MIX + GROUP_XATTN] + [conv_tile]))
    landed(GROUP_MIX + GROUP_XATTN, gathered)
    convw_t = gathered[-1][:, :conv_taps, :conv_rows].transpose(1, 0, 2).reshape(conv_taps, N_DEV * conv_rows)
    (x2, z, o_raw, states, ycat), gathered = _mix_fwd(
        x1, w["mix_norm"], full["w_in"], w["lb_param"], w["hgrn_out_norm"], convw_t, full["w_out"],
        exchange=_gather_exchange([wire[k] for k in GROUP_FFN2]))
    landed(GROUP_FFN2, gathered)
    kv = _memkv_fwd(memv, w["mem_norm"], full["w_kv_mem"])
    x3, hq, qm, att = _xattn_fwd(x2, w["xattn_norm"], full["w_q_mem"], kv, full["w_o_mem"])
    (x4, a2, b2, s2), _ = _ffn_fwd(x3, w["ffn2_norm"], full["ffn2_gate"], full["ffn2_up"], full["ffn2_down"])
    dx4, loss_part, d_final = _final_loss(x4, w["final_norm"], target)

    core = lax.axis_index("c").astype(jnp.int32).reshape(1)
    parts = {}
    waiting = []

    def carried():
        names = [name for name, _ in waiting]
        exchange = _scatter_exchange([p for _, p in waiting]) if waiting else None
        del waiting[:]
        return names, exchange

    def weight_grad(name, a, b, scale=1.0):
        names, exchange = carried()
        partial, arrived = _weight_grad(a, b, scale, exchange=exchange)
        parts.update(zip(names, arrived))
        waiting.append((name, partial))

    (dx3, da2, db2, h4, d_ffn2_norm), _ = _ffn_bwd(
        x3, w["ffn2_norm"], dx4, a2, b2, full["ffn2_gate"], full["ffn2_up"], full["ffn2_down"])
    weight_grad("ffn2_down", s2, dx4, 0.5)
    weight_grad("ffn2_gate", da2, h4)
    weight_grad("ffn2_up", db2, h4)
    names, exchange = carried()
    (dx2, dqm, dkv, d_xattn_norm), arrived = _xattn_bwd(
        x2, w["xattn_norm"], dx3, qm, kv, full["w_q_mem"], full["w_o_mem"], exchange=exchange)
    parts.update(zip(names, arrived))
    weight_grad("w_o_mem", att, dx3)
    weight_grad("w_q_mem", hq, dqm)
    d_wkv_blocks, d_mem_norm = _memkv_bwd(memv, w["mem_norm"], dkv, full["w_kv_mem"])
    (from_sibling,) = _run_exchange(_pair_exchange([d_wkv_blocks]), "pair_exchange")
    waiting.append(("w_kv_mem", _pair_add(d_wkv_blocks, from_sibling, core)))
    weight_grad("w_out", ycat, dx2)
    names, exchange = carried()
    (dx1, dz, h2, d_mix_norm, d_lbp, d_gh, d_convw_t), arrived = _mix_bwd(
        x1, w["mix_norm"], dx2, z, o_raw, states, full["w_in"], w["lb_param"], w["hgrn_out_norm"], convw_t,
        full["w_out"], exchange=exchange)
    parts.update(zip(names, arrived))
    weight_grad("w_in", dz, h2)
    weight_grad("ffn1_down", s1, dx1, 0.5)
    names, exchange = carried()
    (dx0, da1, db1, h1, d_ffn1_norm), arrived = _ffn_bwd(
        x0, w["ffn1_norm"], dx1, a1, b1, full["ffn1_gate"], full["ffn1_up"], full["ffn1_down"], exchange=exchange)
    parts.update(zip(names, arrived))
    weight_grad("ffn1_gate", da1, h1)
    weight_grad("ffn1_up", db1, h1)

    small_parts = {
        "ffn1_norm": d_ffn1_norm, "mix_norm": d_mix_norm, "xattn_norm": d_xattn_norm, "mem_norm": d_mem_norm,
        "ffn2_norm": d_ffn2_norm, "final_norm": d_final, "lb_param": d_lbp, "hgrn_out_norm": d_gh,
        "conv_w": d_convw_t, "loss": loss_part,
    }
    names = [name for name, _ in waiting]
    arrived, total = _final_exchange([p for _, p in waiting], small_parts)
    parts.update(zip(names, arrived))

    g_out, d_out, m_out, v_out = {}, {}, {}, {}
    for name in LARGE:
        g_out[name], d_out[name], m_out[name], v_out[name] = _adamw_shard(parts[name], w[name], m[name], v[name])
    g_small = {}
    for name in SMALL:
        row, nrows, ncols = SMALL_LAYOUT[name]
        g_small[name] = total[row:row + nrows, 0:ncols]
    g_small["conv_w"] = lax.dynamic_slice_in_dim(g_small["conv_w"], me * conv_rows, conv_rows, axis=1)
    ds, ms, vs = _adamw_small(
        [g_small[k] for k in SMALL], [w[k] for k in SMALL], [m[k] for k in SMALL], [v[k] for k in SMALL])
    for i, name in enumerate(SMALL):
        g_out[name], d_out[name], m_out[name], v_out[name] = g_small[name], ds[i], ms[i], vs[i]

    def shaped(value, name):
        return (value.T if name in TRANSPOSED else value).reshape(given[name].shape)

    loss = total[SMALL_LAYOUT["loss"][0], 0]
    outs = [loss, dx0.reshape(x.shape)]
    for group in (g_out, d_out, m_out, v_out):
        outs += [shaped(group[name], name) for name in WEIGHTS]
    return tuple(outs)
```

```python
import jax
import jax.numpy as jnp
from jax import lax
from jax.experimental import pallas as pl
from jax.experimental.pallas import tpu as pltpu

F32 = jnp.float32
BF16 = jnp.bfloat16
MESH_IDS = pl.DeviceIdType.MESH

N_DEV = 8
EPS = 1e-6
HGRN_HEADS = 4
HGRN_DK = 128
HGRN_W = 512
CHUNK = 64
MEM_HEADS = 4
MEM_HD = 256
ADAM_LR = 0.001
ADAM_B1 = 0.9
ADAM_B2 = 0.999
ADAM_EPS = 1e-08
ADAM_WD = 0.01
ADAM_STEP = 10

TOKEN_TILE = 256
REDUCE_TILE = 1024
ADAMW_TILE_ELEMENTS = 64 * 1024
MXU_ROWS = 256
VMEM_LIMIT = 60 * 1024 * 1024
SMALL_ROWS = 16
NT = (((1,), (1,)), ((), ()))
TN = (((0,), (0,)), ((), ()))


def _params(sem=None):
    return pltpu.CompilerParams(dimension_semantics=sem, vmem_limit_bytes=VMEM_LIMIT)


def _dot(a, b, dims=None):
    if dims is None:
        return jnp.dot(a, b, preferred_element_type=F32)
    return lax.dot_general(a, b, dims, preferred_element_type=F32)


def _sigmoid(v):
    return 1.0 / (1.0 + jnp.exp(-v))


def _rms(x, g):
    r = lax.rsqrt(jnp.mean(x * x, axis=-1, keepdims=True) + EPS)
    xh = x * r
    return xh * g, xh, r


def _rms_bwd(dh, xh, r, g):
    dxh = dh * g
    return r * (dxh - xh * jnp.mean(dxh * xh, axis=-1, keepdims=True))


def _full(shape):
    return pl.BlockSpec(shape, lambda *_: (0,) * len(shape))


def _rows(tm, width):
    return pl.BlockSpec((tm, width), lambda i: (i, 0))


def _rows_rev(tm, width, n):
    return pl.BlockSpec((tm, width), lambda i: (n - 1 - i, 0))


def _accumulate(ref, first, value):
    @pl.when(first)
    def _():
        ref[...] = value

    @pl.when(jnp.logical_not(first))
    def _():
        ref[...] += value


class _Exchange:
    def __init__(self, operands, out_shapes, scratch, start, finish):
        self.operands, self.out_shapes, self.scratch = list(operands), list(out_shapes), list(scratch)
        self.start, self.finish = start, finish


def _call(body, *, name, grid, in_specs, out_specs, out_shape, args, scratch_shapes=(), exchange=None):
    semantics = ("arbitrary",) * len(grid)
    if exchange is None:
        out = pl.pallas_call(
            body, name=name, grid=grid, in_specs=in_specs, out_specs=out_specs, out_shape=out_shape,
            scratch_shapes=list(scratch_shapes), compiler_params=_params(semantics))(*args)
        return out, []
    hbm = pl.BlockSpec(memory_space=pltpu.HBM)
    n_in, n_out, n_scr = len(in_specs), len(out_specs), len(scratch_shapes)
    e_in, e_out = len(exchange.operands), len(exchange.out_shapes)

    def carried(*refs):
        ins, rest = refs[:n_in], refs[n_in:]
        e_ins, rest = rest[:e_in], rest[e_in:]
        outs, rest = rest[:n_out], rest[n_out:]
        e_outs, rest = rest[:e_out], rest[e_out:]
        scr, e_scr = rest[:n_scr], rest[n_scr:]
        first = last = None
        for axis, size in enumerate(grid):
            at_start, at_end = pl.program_id(axis) == 0, pl.program_id(axis) == size - 1
            first = at_start if first is None else jnp.logical_and(first, at_start)
            last = at_end if last is None else jnp.logical_and(last, at_end)

        @pl.when(first)
        def _():
            exchange.start(e_ins, e_outs, e_scr)

        body(*ins, *outs, *scr)

        @pl.when(last)
        def _():
            exchange.finish(e_ins, e_outs, e_scr)

    out = pl.pallas_call(
        carried, name=name, grid=grid, in_specs=list(in_specs) + [hbm] * e_in,
        out_specs=list(out_specs) + [hbm] * e_out, out_shape=list(out_shape) + exchange.out_shapes,
        scratch_shapes=list(scratch_shapes) + exchange.scratch,
        compiler_params=pltpu.CompilerParams(
            dimension_semantics=semantics, vmem_limit_bytes=VMEM_LIMIT, has_side_effects=True),
    )(*args, *exchange.operands)
    return out[:n_out], out[n_out:]


def _run_exchange(exchange, name):
    hbm = pl.BlockSpec(memory_space=pltpu.HBM)
    e_in, e_out = len(exchange.operands), len(exchange.out_shapes)

    def body(*refs):
        e_ins, e_outs, e_scr = refs[:e_in], refs[e_in:e_in + e_out], refs[e_in + e_out:]
        exchange.start(e_ins, e_outs, e_scr)
        exchange.finish(e_ins, e_outs, e_scr)

    return pl.pallas_call(
        body, name=name, in_specs=[hbm] * e_in, out_specs=[hbm] * e_out, out_shape=exchange.out_shapes,
        scratch_shapes=exchange.scratch, compiler_params=pltpu.CompilerParams(has_side_effects=True),
    )(*exchange.operands)


def _loss_head(xo, gf, tgt):
    d = xo.shape[1]
    y, xh, r = _rms(xo, gf)
    err = y - tgt
    dy = err * (1.0 / d)
    loss = 0.5 * jnp.sum(jnp.sum(err * err, axis=-1, keepdims=True) * (1.0 / d), axis=0, keepdims=True)
    return _rms_bwd(dy, xh, r, gf), loss, jnp.sum(dy * xh, axis=0, keepdims=True)


def _ffn_fwd(x, g, wg, wu, wd, exchange=None, head=None):
    t, d = x.shape
    f = wg.shape[0]
    tm = min(TOKEN_TILE, t)

    def body(x_ref, g_ref, wg_ref, wu_ref, wd_ref, *rest):
        xv = x_ref[...]
        h, _, _ = _rms(xv, g_ref[...])
        hb = h.astype(BF16)
        a = _dot(hb, wg_ref[...], NT)
        b = _dot(hb, wu_ref[...], NT)
        s = (a * _sigmoid(a) * b).astype(BF16)
        xo = xv + 0.5 * _dot(s, wd_ref[...])
        if head is None:
            xo_ref, a_ref, b_ref, s_ref = rest
            xo_ref[...] = xo
        else:
            gf_ref, tgt_ref, xo_ref, a_ref, b_ref, s_ref, loss_ref, dgf_ref = rest
            first = pl.program_id(0) == 0
            xo_ref[...], loss, dgf = _loss_head(xo, gf_ref[...], tgt_ref[...])
            _accumulate(loss_ref, first, jnp.broadcast_to(loss, (1, 128)))
            _accumulate(dgf_ref, first, dgf)
        a_ref[...] = a.astype(BF16)
        b_ref[...] = b.astype(BF16)
        s_ref[...] = s

    in_specs = [_rows(tm, d), _full((1, d)), _full((f, d)), _full((f, d)), _full((f, d))]
    out_specs = [_rows(tm, d), _rows(tm, f), _rows(tm, f), _rows(tm, f)]
    out_shape = [
        jax.ShapeDtypeStruct((t, d), F32),
        jax.ShapeDtypeStruct((t, f), BF16),
        jax.ShapeDtypeStruct((t, f), BF16),
        jax.ShapeDtypeStruct((t, f), BF16),
    ]
    args = (x, g, wg, wu, wd)
    if head is not None:
        in_specs += [_full((1, d)), _rows(tm, d)]
        out_specs += [_full((1, 128)), _full((1, d))]
        out_shape += [jax.ShapeDtypeStruct((1, 128), F32), jax.ShapeDtypeStruct((1, d), F32)]
        args += tuple(head)
    return _call(
        body, name="ffn_fwd", grid=(t // tm,), in_specs=in_specs, out_specs=out_specs, out_shape=out_shape,
        args=args, exchange=exchange)


def _ffn_bwd(x, g, dxo, a, b, wg, wu, wd, exchange=None):
    t, d = x.shape
    f = wg.shape[0]
    tm = min(TOKEN_TILE, t)

    def body(x_ref, g_ref, dxo_ref, a_ref, b_ref, wg_ref, wu_ref, wd_ref, dx_ref, da_ref, db_ref, h_ref, dg_ref):
        gv = g_ref[...]
        h, xh, r = _rms(x_ref[...], gv)
        dxo = dxo_ref[...]
        ds = _dot((0.5 * dxo).astype(BF16), wd_ref[...], NT)
        af = a_ref[...].astype(F32)
        bf = b_ref[...].astype(F32)
        sg = _sigmoid(af)
        da = (ds * bf * (sg * (1.0 + af * (1.0 - sg)))).astype(BF16)
        db = (ds * (af * sg)).astype(BF16)
        dh = _dot(da, wg_ref[...]) + _dot(db, wu_ref[...])
        dx_ref[...] = _rms_bwd(dh, xh, r, gv) + dxo
        da_ref[...] = da
        db_ref[...] = db
        h_ref[...] = h.astype(BF16)
        _accumulate(dg_ref, pl.program_id(0) == 0, jnp.sum(dh * xh, axis=0, keepdims=True))

    return _call(
        body,
        name="ffn_bwd",
        grid=(t // tm,),
        in_specs=[
            _rows(tm, d), _full((1, d)), _rows(tm, d), _rows(tm, f), _rows(tm, f),
            _full((f, d)), _full((f, d)), _full((f, d)),
        ],
        out_specs=[_rows(tm, d), _rows(tm, f), _rows(tm, f), _rows(tm, d), _full((1, d))],
        out_shape=[
            jax.ShapeDtypeStruct((t, d), F32),
            jax.ShapeDtypeStruct((t, f), BF16),
            jax.ShapeDtypeStruct((t, f), BF16),
            jax.ShapeDtypeStruct((t, d), BF16),
            jax.ShapeDtypeStruct((1, d), F32),
        ],
        args=(x, g, dxo, a, b, wg, wu, wd),
        exchange=exchange,
    )


def _weight_grad(a, b, scale=1.0, exchange=None):
    t, m = a.shape
    n = b.shape[1]
    chips = N_DEV // 2
    r = m // N_DEV
    tk = min(REDUCE_TILE, t)
    halves = 2
    nb = n // halves
    nk = t // tk

    def body(a_ref, b_ref, o_ref, acc, mine, send_buf, recv_buf, send_sems, recv_sems):
        j, k = pl.program_id(0), pl.program_id(1)
        x, y, c, _ = _mesh_place()
        sibling, _ = _peer(x, y, c, 1)
        bv = b_ref[...]
        if scale != 1.0:
            bv = bv * scale
        bb = bv.astype(BF16)

        @pl.when(k == 0)
        def _():
            acc[...] = jnp.zeros_like(acc)

        for i in range(m // MXU_ROWS):
            rows = slice(i * MXU_ROWS, (i + 1) * MXU_ROWS)
            acc[rows, :] += _dot(a_ref[:, rows].astype(BF16), bb, TN)

        def to_sibling(half):
            return _remote(send_buf.at[half], recv_buf.at[half], send_sems.at[half], recv_sems.at[half], sibling)

        for half in range(halves):
            @pl.when(jnp.logical_and(k == nk - 1, j == half))
            def _():
                for q in range(chips):
                    send_buf[half, q] = acc[pl.ds(pl.multiple_of((2 * q + 1 - c) * r, 8), r), :].astype(BF16)
                    mine[half, q] = acc[pl.ds(pl.multiple_of((2 * q + c) * r, 8), r), :]
                to_sibling(half).start()

        @pl.when(jnp.logical_and(k == nk - 1, j == halves - 1))
        def _():
            for half in range(halves):
                to_sibling(half).wait_send()
                to_sibling(half).wait_recv()
                for q in range(chips):
                    o_ref[q, :, half * nb:(half + 1) * nb] = (
                        mine[half, q] + recv_buf[half, q].astype(F32)).astype(BF16)

    (partial,), arrived = _call(
        body,
        name="weight_grad",
        grid=(halves, nk),
        in_specs=[pl.BlockSpec((tk, m), lambda j, k: (k, 0)), pl.BlockSpec((tk, nb), lambda j, k: (k, j))],
        out_specs=[pl.BlockSpec((chips, r, n), lambda j, k: (0, 0, 0))],
        out_shape=[jax.ShapeDtypeStruct((chips, r, n), BF16)],
        scratch_shapes=[
            pltpu.VMEM((m, nb), F32), pltpu.VMEM((halves, chips, r, nb), F32),
            pltpu.VMEM((halves, chips, r, nb), BF16), pltpu.VMEM((halves, chips, r, nb), BF16),
            pltpu.SemaphoreType.DMA((halves,)), pltpu.SemaphoreType.DMA((halves,)),
        ],
        args=(a, b),
        exchange=exchange,
    )
    return partial, arrived


def _chunk_cumsum(v, reverse=False):
    n = v.shape[0]
    pos = lax.broadcasted_iota(jnp.int32, (n, 1), 0) % CHUNK
    shift = 1
    while shift < CHUNK:
        if reverse:
            moved = pltpu.roll(v, n - shift, axis=0)
            v = v + jnp.where(pos < CHUNK - shift, moved, 0.0)
        else:
            moved = pltpu.roll(v, shift, axis=0)
            v = v + jnp.where(pos >= shift, moved, 0.0)
        shift *= 2
    return v


def _shift_rows(v, shift, edge):
    n = v.shape[0]
    row = lax.broadcasted_iota(jnp.int32, (n, 1), 0)
    out = pltpu.roll(v, shift % n, axis=0)
    if shift > 0:
        for j in range(shift):
            out = jnp.where(row == j, edge[8 - shift + j:8 - shift + j + 1, :], out)
    else:
        for j in range(-shift):
            out = jnp.where(row == n + shift + j, edge[j:j + 1, :], out)
    return out


def _gates(z, lbp):
    w = HGRN_W
    lb = _sigmoid(lbp[0:1, :] - lbp[1:2, :])
    zq = z[:, 0:w]
    sig = _sigmoid(z[:, w:2 * w])
    f = lb + (1.0 - lb) * sig
    sq = _sigmoid(zq)
    q = zq * sq * HGRN_DK ** -0.5
    return lb, sig, f, sq, q


def _short_conv(u, edge, cw):
    return cw[0:1, :] * _shift_rows(u, 2, edge) + cw[1:2, :] * _shift_rows(u, 1, edge) + cw[2:3, :] * u


def _causal_mask():
    row = lax.broadcasted_iota(jnp.int32, (CHUNK, CHUNK), 0)
    col = lax.broadcasted_iota(jnp.int32, (CHUNK, CHUNK), 1)
    return col <= row


def _mix_fwd(x, g, w_in, lbp, gh, convw_t, w_out, exchange=None):
    t, d = x.shape
    zw = w_in.shape[0]
    w = HGRN_W
    tm = min(TOKEN_TILE, t)
    nc = tm // CHUNK
    n_chunks = t // CHUNK

    def body(x_ref, g_ref, win_ref, lbp_ref, gh_ref, cw_ref, wout_ref,
             xo_ref, z_ref, o_ref, st_ref, y_ref, state, ucarry):
        @pl.when(pl.program_id(0) == 0)
        def _():
            state[...] = jnp.zeros_like(state)
            ucarry[...] = jnp.zeros_like(ucarry)

        xv = x_ref[...]
        h, _, _ = _rms(xv, g_ref[...])
        z_ref[...] = _dot(h.astype(BF16), win_ref[...], NT)
        z = z_ref[...]
        _, _, f, _, q = _gates(z, lbp_ref[...])
        bcum = _chunk_cumsum(jnp.log(f))
        kk = 1.0 - f
        vv = z[:, 2 * w:3 * w]
        mask = _causal_mask()
        for c in range(nc):
            rows = slice(c * CHUNK, (c + 1) * CHUNK)
            for hd in range(HGRN_HEADS):
                cols = slice(hd * HGRN_DK, (hd + 1) * HGRN_DK)
                b = bcum[rows, cols]
                blast = b[CHUNK - 1:CHUNK, :]
                qh = (q[rows, cols] * jnp.exp(b)).astype(BF16)
                kh = (kk[rows, cols] * jnp.exp(-b)).astype(BF16)
                kbar = (kk[rows, cols] * jnp.exp(blast - b)).astype(BF16)
                vb = vv[rows, cols].astype(BF16)
                st = state[hd]
                st_ref[c, hd] = st
                att = jnp.where(mask, _dot(qh, kh, NT), 0.0).astype(BF16)
                o_ref[rows, cols] = _dot(att, vb) + _dot(qh, st.astype(BF16), NT)
                state[hd] = st * jnp.exp(blast) + _dot(vb, kbar, TN)
        ghv = gh_ref[...]
        for hd in range(HGRN_HEADS):
            cols = slice(hd * HGRN_DK, (hd + 1) * HGRN_DK)
            on, _, _ = _rms(o_ref[:, cols], ghv[:, cols])
            zg = z[:, 3 * w + hd * HGRN_DK:3 * w + (hd + 1) * HGRN_DK]
            y_ref[:, cols] = (on * (zg * _sigmoid(zg))).astype(BF16)
        u = z[:, 5 * w:6 * w] * z[:, 6 * w:7 * w]
        conv = _short_conv(u, ucarry[...], cw_ref[...])
        ucarry[...] = u[tm - 8:tm, :]
        y_ref[:, w:2 * w] = (z[:, 4 * w:5 * w] * conv).astype(BF16)
        xo_ref[...] = xv + _dot(y_ref[...], wout_ref[...])

    return _call(
        body,
        name="mix_fwd",
        grid=(t // tm,),
        in_specs=[
            _rows(tm, d), _full((1, d)), _full((zw, d)), _full((2, w)), _full((1, w)), _full((3, w)),
            _full((2 * w, d)),
        ],
        out_specs=[
            _rows(tm, d), _rows(tm, zw), _rows(tm, w),
            pl.BlockSpec((nc, HGRN_HEADS, HGRN_DK, HGRN_DK), lambda i: (i, 0, 0, 0)),
            _rows(tm, 2 * w),
        ],
        out_shape=[
            jax.ShapeDtypeStruct((t, d), F32),
            jax.ShapeDtypeStruct((t, zw), F32),
            jax.ShapeDtypeStruct((t, w), F32),
            jax.ShapeDtypeStruct((n_chunks, HGRN_HEADS, HGRN_DK, HGRN_DK), F32),
            jax.ShapeDtypeStruct((t, 2 * w), BF16),
        ],
        scratch_shapes=[pltpu.VMEM((HGRN_HEADS, HGRN_DK, HGRN_DK), F32), pltpu.VMEM((8, w), F32)],
        args=(x, g, w_in, lbp, gh, convw_t, w_out),
        exchange=exchange,
    )


def _mix_bwd(x, g, dxo, z, o, states, w_in, lbp, gh, convw_t, w_out, exchange=None):
    t, d = x.shape
    zw = w_in.shape[0]
    w = HGRN_W
    tm = min(TOKEN_TILE, t)
    nc = tm // CHUNK
    n = t // tm

    def body(x_ref, g_ref, dxo_ref, z_ref, zprev_ref, o_ref, st_ref, win_ref, lbp_ref, gh_ref, cw_ref, wout_ref,
             dx_ref, dz_ref, h_ref, dg_ref, dlbp_ref, dgh_ref, dcw_ref,
             dstate, dcarry, do_buf, dq_buf, dk_buf, db_buf):
        first = pl.program_id(0) == 0

        @pl.when(first)
        def _():
            dstate[...] = jnp.zeros_like(dstate)
            dcarry[...] = jnp.zeros_like(dcarry)

        gv = g_ref[...]
        h, xh, r = _rms(x_ref[...], gv)
        h_ref[...] = h.astype(BF16)
        dxo = dxo_ref[...]
        dy = _dot(dxo.astype(BF16), wout_ref[...], NT)
        z = z_ref[...]
        lb, sig, f, sq, q = _gates(z, lbp_ref[...])
        bcum = _chunk_cumsum(jnp.log(f))
        kk = 1.0 - f
        vv = z[:, 2 * w:3 * w]

        ghv = gh_ref[...]
        dgh_parts = []
        for hd in range(HGRN_HEADS):
            cols = slice(hd * HGRN_DK, (hd + 1) * HGRN_DK)
            gcols = slice(3 * w + hd * HGRN_DK, 3 * w + (hd + 1) * HGRN_DK)
            on, oh, rr = _rms(o_ref[:, cols], ghv[:, cols])
            zg = z[:, gcols]
            sgz = _sigmoid(zg)
            dyh = dy[:, cols]
            don = dyh * (zg * sgz)
            dz_ref[:, gcols] = (dyh * on * (sgz * (1.0 + zg * (1.0 - sgz)))).astype(BF16)
            dgh_parts.append(jnp.sum(don * oh, axis=0, keepdims=True))
            do_buf[:, cols] = _rms_bwd(don, oh, rr, ghv[:, cols])
        _accumulate(dgh_ref, first, jnp.concatenate(dgh_parts, axis=1))

        zb = z[:, 4 * w:5 * w]
        zc = z[:, 5 * w:6 * w]
        zu = z[:, 6 * w:7 * w]
        u = zc * zu
        cw = cw_ref[...]
        zp = zprev_ref[...]
        uprev = jnp.where(pl.program_id(0) == n - 1, 0.0, zp[:, 5 * w:6 * w] * zp[:, 6 * w:7 * w])
        dyc = dy[:, w:2 * w]
        dz_ref[:, 4 * w:5 * w] = (dyc * _short_conv(u, uprev, cw)).astype(BF16)
        dconv = dyc * zb
        edge = dcarry[...]
        dconv1 = _shift_rows(dconv, -1, edge)
        dconv2 = _shift_rows(dconv, -2, edge)
        dcarry[...] = dconv[0:8, :]
        du = cw[2:3, :] * dconv + cw[1:2, :] * dconv1 + cw[0:1, :] * dconv2
        dz_ref[:, 5 * w:6 * w] = (du * zu).astype(BF16)
        dz_ref[:, 6 * w:7 * w] = (du * zc).astype(BF16)
        _accumulate(dcw_ref, first, jnp.concatenate([
            jnp.sum(u * dconv2, axis=0, keepdims=True),
            jnp.sum(u * dconv1, axis=0, keepdims=True),
            jnp.sum(u * dconv, axis=0, keepdims=True)], axis=0))

        mask = _causal_mask()
        last_row = lax.broadcasted_iota(jnp.int32, (CHUNK, 1), 0) == CHUNK - 1
        for c in reversed(range(nc)):
            rows = slice(c * CHUNK, (c + 1) * CHUNK)
            for hd in range(HGRN_HEADS):
                cols = slice(hd * HGRN_DK, (hd + 1) * HGRN_DK)
                b = bcum[rows, cols]
                blast = b[CHUNK - 1:CHUNK, :]
                eb = jnp.exp(b)
                enb = jnp.exp(-b)
                erest = jnp.exp(blast - b)
                elast = jnp.exp(blast)
                qh = q[rows, cols] * eb
                kh = kk[rows, cols] * enb
                kbar = kk[rows, cols] * erest
                qhb = qh.astype(BF16)
                khb = kh.astype(BF16)
                kbarb = kbar.astype(BF16)
                vb = vv[rows, cols].astype(BF16)
                st = st_ref[c, hd]
                dst = dstate[hd]
                dstb = dst.astype(BF16)
                dob = do_buf[rows, cols].astype(BF16)
                att = jnp.where(mask, _dot(qhb, khb, NT), 0.0).astype(BF16)
                datt = jnp.where(mask, _dot(dob, vb, NT), 0.0).astype(BF16)
                dv = _dot(att, dob, TN) + _dot(kbarb, dstb, NT)
                dqh = _dot(datt, khb) + _dot(dob, st.astype(BF16))
                dkh = _dot(datt, qhb, TN)
                dkbar = _dot(vb, dstb)
                dstate[hd] = dst * elast + _dot(dob, qhb, TN)
                kbar_dkbar = kbarb.astype(F32) * dkbar
                db = qhb.astype(F32) * dqh - khb.astype(F32) * dkh - kbar_dkbar
                db_last = (jnp.sum(kbar_dkbar, axis=0, keepdims=True)
                           + jnp.sum(dst * st, axis=0, keepdims=True) * elast)
                db_buf[rows, cols] = jnp.where(last_row, db + db_last, db)
                dq_buf[rows, cols] = dqh * eb
                dk_buf[rows, cols] = dkh * enb + dkbar * erest
                dz_ref[rows, 2 * w + hd * HGRN_DK:2 * w + (hd + 1) * HGRN_DK] = dv.astype(BF16)

        dlogf = _chunk_cumsum(db_buf[...], reverse=True)
        df = dlogf / f - dk_buf[...]
        zq = z[:, 0:w]
        dz_ref[:, 0:w] = (dq_buf[...] * HGRN_DK ** -0.5 * (sq * (1.0 + zq * (1.0 - sq)))).astype(BF16)
        dz_ref[:, w:2 * w] = (df * (1.0 - lb) * sig * (1.0 - sig)).astype(BF16)
        dlb = jnp.sum(df * (1.0 - sig), axis=0, keepdims=True) * lb * (1.0 - lb)
        _accumulate(dlbp_ref, first, jnp.concatenate([dlb, -dlb], axis=0))

        dh = _dot(dz_ref[...], win_ref[...])
        dx_ref[...] = _rms_bwd(dh, xh, r, gv) + dxo
        _accumulate(dg_ref, first, jnp.sum(dh * xh, axis=0, keepdims=True))

    return _call(
        body,
        name="mix_bwd",
        grid=(n,),
        in_specs=[
            _rows_rev(tm, d, n), _full((1, d)), _rows_rev(tm, d, n), _rows_rev(tm, zw, n),
            pl.BlockSpec((8, zw), lambda i: (jnp.maximum((n - 1 - i) * (tm // 8) - 1, 0), 0)),
            _rows_rev(tm, w, n),
            pl.BlockSpec((nc, HGRN_HEADS, HGRN_DK, HGRN_DK), lambda i: (n - 1 - i, 0, 0, 0)),
            _full((zw, d)), _full((2, w)), _full((1, w)), _full((3, w)), _full((2 * w, d)),
        ],
        out_specs=[
            _rows_rev(tm, d, n), _rows_rev(tm, zw, n), _rows_rev(tm, d, n),
            _full((1, d)), _full((2, w)), _full((1, w)), _full((3, w)),
        ],
        out_shape=[
            jax.ShapeDtypeStruct((t, d), F32),
            jax.ShapeDtypeStruct((t, zw), BF16),
            jax.ShapeDtypeStruct((t, d), BF16),
            jax.ShapeDtypeStruct((1, d), F32),
            jax.ShapeDtypeStruct((2, w), F32),
            jax.ShapeDtypeStruct((1, w), F32),
            jax.ShapeDtypeStruct((3, w), F32),
        ],
        scratch_shapes=[
            pltpu.VMEM((HGRN_HEADS, HGRN_DK, HGRN_DK), F32), pltpu.VMEM((8, w), F32),
            pltpu.VMEM((tm, w), F32), pltpu.VMEM((tm, w), F32), pltpu.VMEM((tm, w), F32), pltpu.VMEM((tm, w), F32),
        ],
        args=(x, g, dxo, z, z, o, states, w_in, lbp, gh, convw_t, w_out),
        exchange=exchange,
    )


def _memkv_fwd(mem, g, wkv):
    m, d = mem.shape
    nb, _, cb = wkv.shape

    def body(mem_ref, g_ref, wkv_ref, kv_ref):
        mn, _, _ = _rms(mem_ref[...], g_ref[...])
        mnb = mn.astype(BF16)
        for j in range(nb):
            kv_ref[:, j * cb:(j + 1) * cb] = _dot(mnb, wkv_ref[j]).astype(BF16)

    return pl.pallas_call(
        body,
        name="memkv_fwd",
        out_shape=jax.ShapeDtypeStruct((m, nb * cb), BF16),
        compiler_params=_params(),
    )(mem, g, wkv)


def _memkv_bwd(mem, g, dkv, wkv):
    m, d = mem.shape
    nb, _, cb = wkv.shape

    def body(mem_ref, g_ref, dkv_ref, wkv_ref, dw_ref, dg_ref):
        mn, xh, _ = _rms(mem_ref[...], g_ref[...])
        mnb = mn.astype(BF16)
        dmn = jnp.zeros((m, d), F32)
        for j in range(nb):
            dkvb = dkv_ref[:, j * cb:(j + 1) * cb].astype(BF16)
            dw_ref[j] = _dot(mnb, dkvb, TN).astype(BF16)
            dmn = dmn + _dot(dkvb, wkv_ref[j], NT)
        dg_ref[...] = jnp.sum(dmn * xh, axis=0, keepdims=True)

    return pl.pallas_call(
        body,
        name="memkv_bwd",
        out_shape=[jax.ShapeDtypeStruct((nb, d, cb), BF16), jax.ShapeDtypeStruct((1, d), F32)],
        compiler_params=_params(),
    )(mem, g, dkv, wkv)


def _softmax_rows(qm_h, k_h):
    sc = _dot(qm_h, k_h, NT) * MEM_HD ** -0.5
    e = jnp.exp(sc - jnp.max(sc, axis=-1, keepdims=True))
    return e / jnp.sum(e, axis=-1, keepdims=True)


def _xattn_fwd(x, g, wq, kv, wo):
    t, d = x.shape
    m = kv.shape[0]
    tm = min(TOKEN_TILE, t)

    def body(x_ref, g_ref, wq_ref, kv_ref, wo_ref, xo_ref, hq_ref, qm_ref, att_ref):
        xv = x_ref[...]
        h, _, _ = _rms(xv, g_ref[...])
        hq_ref[...] = h.astype(BF16)
        qm_ref[...] = _dot(hq_ref[...], wq_ref[...]).astype(BF16)
        for hd in range(MEM_HEADS):
            cols = slice(hd * MEM_HD, (hd + 1) * MEM_HD)
            p = _softmax_rows(qm_ref[:, cols], kv_ref[:, cols])
            att_ref[:, cols] = _dot(p.astype(BF16), kv_ref[:, d + hd * MEM_HD:d + (hd + 1) * MEM_HD]).astype(BF16)
        xo_ref[...] = xv + _dot(att_ref[...], wo_ref[...])

    return pl.pallas_call(
        body,
        name="xattn_fwd",
        grid=(t // tm,),
        in_specs=[_rows(tm, d), _full((1, d)), _full((d, d)), _full((m, 2 * d)), _full((d, d))],
        out_specs=[_rows(tm, d), _rows(tm, d), _rows(tm, d), _rows(tm, d)],
        out_shape=[
            jax.ShapeDtypeStruct((t, d), F32),
            jax.ShapeDtypeStruct((t, d), BF16),
            jax.ShapeDtypeStruct((t, d), BF16),
            jax.ShapeDtypeStruct((t, d), BF16),
        ],
        compiler_params=_params(("arbitrary",)),
    )(x, g, wq, kv, wo)


def _xattn_bwd(x, g, dxo, qm, kv, wq, wo, exchange=None):
    t, d = x.shape
    m = kv.shape[0]
    tm = min(TOKEN_TILE, t)

    def body(x_ref, g_ref, dxo_ref, qm_ref, kv_ref, wq_ref, wo_ref, dx_ref, dqm_ref, dkv_ref, dg_ref):
        first = pl.program_id(0) == 0

        @pl.when(first)
        def _():
            dkv_ref[...] = jnp.zeros_like(dkv_ref)

        gv = g_ref[...]
        _, xh, r = _rms(x_ref[...], gv)
        dxo = dxo_ref[...]
        datt = _dot(dxo.astype(BF16), wo_ref[...], NT).astype(BF16)
        for hd in range(MEM_HEADS):
            cols = slice(hd * MEM_HD, (hd + 1) * MEM_HD)
            vcols = slice(d + hd * MEM_HD, d + (hd + 1) * MEM_HD)
            qm_h = qm_ref[:, cols]
            p = _softmax_rows(qm_h, kv_ref[:, cols])
            datt_h = datt[:, cols]
            dp = _dot(datt_h, kv_ref[:, vcols], NT)
            dsc = (p * (dp - jnp.sum(p * dp, axis=-1, keepdims=True)) * MEM_HD ** -0.5).astype(BF16)
            dqm_ref[:, cols] = _dot(dsc, kv_ref[:, cols]).astype(BF16)
            dkv_ref[:, cols] += _dot(dsc, qm_h, TN)
            dkv_ref[:, vcols] += _dot(p.astype(BF16), datt_h, TN)
        dh = _dot(dqm_ref[...], wq_ref[...], NT)
        dx_ref[...] = _rms_bwd(dh, xh, r, gv) + dxo
        _accumulate(dg_ref, first, jnp.sum(dh * xh, axis=0, keepdims=True))

    return _call(
        body,
        name="xattn_bwd",
        grid=(t // tm,),
        in_specs=[
            _rows(tm, d), _full((1, d)), _rows(tm, d), _rows(tm, d), _full((m, 2 * d)), _full((d, d)), _full((d, d)),
        ],
        out_specs=[_rows(tm, d), _rows(tm, d), _full((m, 2 * d)), _full((1, d))],
        out_shape=[
            jax.ShapeDtypeStruct((t, d), F32),
            jax.ShapeDtypeStruct((t, d), BF16),
            jax.ShapeDtypeStruct((m, 2 * d), F32),
            jax.ShapeDtypeStruct((1, d), F32),
        ],
        args=(x, g, dxo, qm, kv, wq, wo),
        exchange=exchange,
    )


def _mesh_place():
    x, y, c = lax.axis_index("x"), lax.axis_index("y"), lax.axis_index("c")
    return x, y, c, 4 * x + 2 * y + c


def _peer(x, y, c, k):
    px = 1 - x if k & 4 else x
    py = 1 - y if k & 2 else y
    pc = 1 - c if k & 1 else c
    return (px, py, pc), 4 * px + 2 * py + pc


ICI_HOPS = (2, 4, 6)
N_HOPS = len(ICI_HOPS)


def _remote(src, dst, send_sem, recv_sem, peer):
    return pltpu.make_async_remote_copy(
        src_ref=src, dst_ref=dst, send_sem=send_sem, recv_sem=recv_sem, device_id=peer, device_id_type=MESH_IDS)


def _gather_exchange(shards):
    n = len(shards)

    def start(src, dst, sems):
        ici_send, ici_recv, pair_send, pair_recv, local = sems
        x, y, c, me = _mesh_place()
        sibling, _ = _peer(x, y, c, 1)
        for a in range(n):
            pltpu.make_async_copy(src[a], dst[a].at[me], local.at[a]).start()
            for j, k in enumerate(ICI_HOPS):
                peer, _ = _peer(x, y, c, k)
                _remote(src[a], dst[a].at[me], ici_send.at[a, j], ici_recv.at[a, j], peer).start()
            _remote(src[a], dst[a].at[me], pair_send.at[a, 0], pair_recv.at[a, 0], sibling).start()

    def finish(src, dst, sems):
        ici_send, ici_recv, pair_send, pair_recv, local = sems
        x, y, c, me = _mesh_place()
        sibling, sibling_index = _peer(x, y, c, 1)
        for a in range(n):
            for j, k in enumerate(ICI_HOPS):
                peer, peer_index = _peer(x, y, c, k)
                slot = dst[a].at[peer_index]
                _remote(src[a], slot, ici_send.at[a, j], ici_recv.at[a, j], peer).wait_recv()
                _remote(slot, slot, pair_send.at[a, 1 + j], pair_recv.at[a, 1 + j], sibling).start()
        for a in range(n):
            pltpu.make_async_copy(src[a], dst[a].at[me], local.at[a]).wait()
            for j, k in enumerate(ICI_HOPS):
                peer, _ = _peer(x, y, c, k)
                _remote(src[a], dst[a].at[me], ici_send.at[a, j], ici_recv.at[a, j], peer).wait_send()
            for j, k in enumerate((0,) + ICI_HOPS):
                _, from_sibling = _peer(x, y, c, k | 1)
                passed = _remote(src[a], dst[a].at[from_sibling], pair_send.at[a, j], pair_recv.at[a, j], sibling)
                passed.wait_send()
                passed.wait_recv()

    return _Exchange(
        shards,
        [jax.ShapeDtypeStruct((N_DEV,) + s.shape, s.dtype) for s in shards],
        [
            pltpu.SemaphoreType.DMA((n, N_HOPS)), pltpu.SemaphoreType.DMA((n, N_HOPS)),
            pltpu.SemaphoreType.DMA((n, N_HOPS + 1)), pltpu.SemaphoreType.DMA((n, N_HOPS + 1)),
            pltpu.SemaphoreType.DMA((n,)),
        ],
        start, finish)


def _pair_exchange(blocks):
    n = len(blocks)
    chips = N_DEV // 2

    def copies(src, dst, sems):
        send, recv = sems
        x, y, c, _ = _mesh_place()
        sibling, _ = _peer(x, y, c, 1)
        return [_remote(src[a].at[2 * q + (1 - c)], dst[a].at[q], send.at[a, q], recv.at[a, q], sibling)
                for a in range(n) for q in range(chips)]

    def start(src, dst, sems):
        for cp in copies(src, dst, sems):
            cp.start()

    def finish(src, dst, sems):
        for cp in copies(src, dst, sems):
            cp.wait_send()
            cp.wait_recv()

    return _Exchange(
        blocks,
        [jax.ShapeDtypeStruct((chips,) + b.shape[1:], b.dtype) for b in blocks],
        [pltpu.SemaphoreType.DMA((n, chips)), pltpu.SemaphoreType.DMA((n, chips))],
        start, finish)


def _pair_add(blocks, received, core):
    _, r, c = blocks.shape
    chips = N_DEV // 2
    tr = r
    while tr > 512:
        tr //= 2

    def body(core_ref, mine_ref, got_ref, o_ref):
        o_ref[...] = (mine_ref[...].astype(F32) + got_ref[...].astype(F32)).astype(BF16)

    return pl.pallas_call(
        body,
        name="pair_add",
        grid_spec=pltpu.PrefetchScalarGridSpec(
            num_scalar_prefetch=1,
            grid=(chips, r // tr),
            in_specs=[
                pl.BlockSpec((None, None, tr, c), lambda q, i, core_ref: (q, core_ref[0], i, 0)),
                pl.BlockSpec((None, tr, c), lambda q, i, core_ref: (q, i, 0)),
            ],
            out_specs=pl.BlockSpec((None, tr, c), lambda q, i, core_ref: (q, i, 0)),
        ),
        out_shape=jax.ShapeDtypeStruct((chips, r, c), BF16),
        compiler_params=_params(("parallel", "parallel")),
    )(core, blocks.reshape(chips, 2, r, c), received)


def _scatter_copies(src, dst, sems, n, arrivals=False):
    send, recv, local = sems
    x, y, c, _ = _mesh_place()
    chip = 2 * x + y
    if arrivals is None:
        return [pltpu.make_async_copy(src[a].at[chip], dst[a].at[chip], local.at[a]) for a in range(n)]
    copies = []
    for a in range(n):
        for j, k in enumerate(ICI_HOPS):
            peer, _ = _peer(x, y, c, k)
            peer_chip = 2 * peer[0] + peer[1]
            slot = dst[a].at[peer_chip if arrivals else chip]
            copies.append(_remote(src[a].at[peer_chip], slot, send.at[a, j], recv.at[a, j], peer))
    return copies


def _scatter_start(src, dst, sems, n):
    for cp in _scatter_copies(src, dst, sems, n, arrivals=None) + _scatter_copies(src, dst, sems, n):
        cp.start()


def _scatter_finish(src, dst, sems, n):
    for cp in _scatter_copies(src, dst, sems, n, arrivals=None):
        cp.wait()
    for cp in _scatter_copies(src, dst, sems, n):
        cp.wait_send()
    for cp in _scatter_copies(src, dst, sems, n, arrivals=True):
        cp.wait_recv()


def _scatter_scratch(n):
    return [pltpu.SemaphoreType.DMA((n, N_HOPS)), pltpu.SemaphoreType.DMA((n, N_HOPS)), pltpu.SemaphoreType.DMA((n,))]


def _scatter_exchange(partials):
    n = len(partials)
    return _Exchange(
        partials, [jax.ShapeDtypeStruct(p.shape, p.dtype) for p in partials], _scatter_scratch(n),
        lambda src, dst, sems: _scatter_start(src, dst, sems, n),
        lambda src, dst, sems: _scatter_finish(src, dst, sems, n))


SMALL_LAYOUT = {
    "ffn1_norm": (0, 1, 1024), "mix_norm": (1, 1, 1024), "xattn_norm": (2, 1, 1024), "mem_norm": (3, 1, 1024),
    "ffn2_norm": (4, 1, 1024), "final_norm": (5, 1, 1024), "lb_param": (6, 2, 512), "hgrn_out_norm": (8, 1, 512),
    "conv_w": (9, 3, 512), "loss": (12, 1, 128),
}


def _final_exchange(partials, small):
    n = len(partials)
    names = list(small)
    width = 1024

    def body(*refs):
        src = refs[:n]
        pieces = refs[n:n + len(names)]
        dst = refs[n + len(names):2 * n + len(names)]
        total_ref = refs[2 * n + len(names)]
        pack, gathered, small_send, small_recv = refs[2 * n + len(names) + 1:2 * n + len(names) + 5]
        sems = refs[2 * n + len(names) + 5:]
        x, y, c, me = _mesh_place()
        pack[...] = jnp.zeros_like(pack)
        for name, piece in zip(names, pieces):
            row, nrows, ncols = SMALL_LAYOUT[name]
            pack[row:row + nrows, 0:ncols] = piece[...]
        for k in range(1, N_DEV):
            peer, _ = _peer(x, y, c, k)
            _remote(pack, gathered.at[me], small_send.at[k - 1], small_recv.at[k - 1], peer).start()
        _scatter_start(src, dst, sems, n)
        gathered[me] = pack[...]
        for k in range(1, N_DEV):
            peer, peer_index = _peer(x, y, c, k)
            landed = _remote(pack, gathered.at[peer_index], small_send.at[k - 1], small_recv.at[k - 1], peer)
            landed.wait_send()
            landed.wait_recv()
        total = gathered[0]
        for j in range(1, N_DEV):
            total = total + gathered[j]
        total_ref[...] = total
        _scatter_finish(src, dst, sems, n)

    hbm = pl.BlockSpec(memory_space=pltpu.HBM)
    vmem = pl.BlockSpec(memory_space=pltpu.VMEM)
    out = pl.pallas_call(
        body,
        name="final_exchange",
        in_specs=[hbm] * n + [vmem] * len(names),
        out_specs=[hbm] * n + [vmem],
        out_shape=[jax.ShapeDtypeStruct(p.shape, p.dtype) for p in partials]
        + [jax.ShapeDtypeStruct((SMALL_ROWS, width), F32)],
        scratch_shapes=[
            pltpu.VMEM((SMALL_ROWS, width), F32), pltpu.VMEM((N_DEV, SMALL_ROWS, width), F32),
            pltpu.SemaphoreType.DMA((N_DEV - 1,)), pltpu.SemaphoreType.DMA((N_DEV - 1,)),
        ] + _scatter_scratch(n),
        compiler_params=pltpu.CompilerParams(has_side_effects=True),
    )(*partials, *[small[k] for k in names])
    return out[:n], out[n]


def _adamw_math(w, g, m, v):
    m = ADAM_B1 * m + (1.0 - ADAM_B1) * g
    v = ADAM_B2 * v + (1.0 - ADAM_B2) * (g * g)
    m_hat = m / (1.0 - ADAM_B1 ** ADAM_STEP)
    v_hat = v / (1.0 - ADAM_B2 ** ADAM_STEP)
    delta = -ADAM_LR * (m_hat / (jnp.sqrt(v_hat) + ADAM_EPS) + ADAM_WD * w)
    return delta, m, v


def _adamw_shard(parts, w, m, v):
    r, c = w.shape
    n_parts = parts.shape[0]
    tr = max(rows for rows in range(16, r + 1, 16) if r % rows == 0 and rows * c <= ADAMW_TILE_ELEMENTS)

    def body(p_ref, w_ref, m_ref, v_ref, g_ref, d_ref, mo_ref, vo_ref):
        g = p_ref[0].astype(F32)
        for j in range(1, n_parts):
            g = g + p_ref[j].astype(F32)
        delta, mn, vn = _adamw_math(w_ref[...], g, m_ref[...], v_ref[...])
        g_ref[...] = g
        d_ref[...] = delta
        mo_ref[...] = mn
        vo_ref[...] = vn

    tile = pl.BlockSpec((tr, c), lambda i: (i, 0))
    return pl.pallas_call(
        body,
        name="adamw_shard",
        grid=(r // tr,),
        in_specs=[pl.BlockSpec((n_parts, tr, c), lambda i: (0, i, 0)), tile, tile, tile],
        out_specs=[tile] * 4,
        out_shape=[jax.ShapeDtypeStruct((r, c), F32)] * 4,
        compiler_params=_params(("parallel",)),
    )(parts, w, m, v)


def _adamw_small(gs, ws, ms, vs):
    n = len(gs)

    def body(*refs):
        g_refs, w_refs, m_refs, v_refs = refs[:n], refs[n:2 * n], refs[2 * n:3 * n], refs[3 * n:4 * n]
        d_out, m_out, v_out = refs[4 * n:5 * n], refs[5 * n:6 * n], refs[6 * n:7 * n]
        for i in range(n):
            delta, mn, vn = _adamw_math(w_refs[i][...], g_refs[i][...], m_refs[i][...], v_refs[i][...])
            d_out[i][...] = delta
            m_out[i][...] = mn
            v_out[i][...] = vn

    shapes = [jax.ShapeDtypeStruct(w.shape, F32) for w in ws]
    out = pl.pallas_call(
        body,
        name="adamw_small",
        out_shape=shapes * 3,
        compiler_params=_params(),
    )(*gs, *ws, *ms, *vs)
    return out[:n], out[n:2 * n], out[2 * n:]


TRANSPOSED = ("ffn1_gate", "ffn1_up", "w_in", "ffn2_gate", "ffn2_up", "conv_w")
GROUP_FFN1 = ("ffn1_gate", "ffn1_up", "ffn1_down")
GROUP_MIX = ("w_in", "w_out")
GROUP_XATTN = ("w_q_mem", "w_kv_mem", "w_o_mem")
GROUP_FFN2 = ("ffn2_gate", "ffn2_up", "ffn2_down")
LARGE = GROUP_FFN1 + GROUP_MIX + GROUP_XATTN + GROUP_FFN2
SMALL = ("ffn1_norm", "mix_norm", "lb_param", "hgrn_out_norm", "conv_w", "xattn_norm", "mem_norm", "ffn2_norm",
         "final_norm")
WEIGHTS = ("ffn1_norm", "ffn1_gate", "ffn1_up", "ffn1_down", "mix_norm", "w_in", "lb_param", "hgrn_out_norm",
           "conv_w", "w_out", "xattn_norm", "mem_norm", "w_q_mem", "w_kv_mem", "w_o_mem", "ffn2_norm", "ffn2_gate",
           "ffn2_up", "ffn2_down", "final_norm")


def kernel(x, mem, ffn1_norm, ffn1_gate, ffn1_up, ffn1_down, mix_norm, w_in, lb_param, hgrn_out_norm, conv_w, w_out, xattn_norm, mem_norm, w_q_mem, w_kv_mem, w_o_mem, ffn2_norm, ffn2_gate, ffn2_up, ffn2_down, final_norm, loss_target, m_ffn1_norm, m_ffn1_gate, m_ffn1_up, m_ffn1_down, m_mix_norm, m_w_in, m_lb_param, m_hgrn_out_norm, m_conv_w, m_w_out, m_xattn_norm, m_mem_norm, m_w_q_mem, m_w_kv_mem, m_w_o_mem, m_ffn2_norm, m_ffn2_gate, m_ffn2_up, m_ffn2_down, m_final_norm, v_ffn1_norm, v_ffn1_gate, v_ffn1_up, v_ffn1_down, v_mix_norm, v_w_in, v_lb_param, v_hgrn_out_norm, v_conv_w, v_w_out, v_xattn_norm, v_mem_norm, v_w_q_mem, v_w_kv_mem, v_w_o_mem, v_ffn2_norm, v_ffn2_gate, v_ffn2_up, v_ffn2_down, v_final_norm):
    given = dict(locals())
    me = 4 * lax.axis_index("x") + 2 * lax.axis_index("y") + lax.axis_index("c")
    x0, memv, target = x[0], mem[0], loss_target[0]

    def shard(prefix, name):
        v = given[prefix + name]
        if v.ndim == 1:
            return v.reshape(1, -1)
        if v.ndim == 2:
            return v
        return v[0].T if name in TRANSPOSED else v[0]

    w = {name: shard("", name) for name in WEIGHTS}
    m = {name: shard("m_", name) for name in WEIGHTS}
    v = {name: shard("v_", name) for name in WEIGHTS}

    conv_taps, conv_rows = w["conv_w"].shape
    conv_tile = jnp.pad(w["conv_w"], ((0, 8 - conv_taps), (0, 128 - conv_rows)))
    wire = {name: w[name].astype(BF16) for name in LARGE}
    full = {}

    def landed(names, gathered):
        for name, blocks in zip(names, gathered):
            _, r, c = blocks.shape
            full[name] = blocks if name == "w_kv_mem" else blocks.reshape(N_DEV * r, c)

    landed(GROUP_FFN1, _run_exchange(_gather_exchange([wire[k] for k in GROUP_FFN1]), "gather_first"))

    (x1, a1, b1, s1), gathered = _ffn_fwd(
        x0, w["ffn1_norm"], full["ffn1_gate"], full["ffn1_up"], full["ffn1_down"],
        exchange=_gather_exchange([wire[k] for k in GROUP_MIX + GROUP_XATTN] + [conv_tile]))
    landed(GROUP_MIX + GROUP_XATTN, gathered)
    convw_t = gathered[-1][:, :conv_taps, :conv_rows].transpose(1, 0, 2).reshape(conv_taps, N_DEV * conv_rows)
    (x2, z, o_raw, states, ycat), gathered = _mix_fwd(
        x1, w["mix_norm"], full["w_in"], w["lb_param"], w["hgrn_out_norm"], convw_t, full["w_out"],
        exchange=_gather_exchange([wire[k] for k in GROUP_FFN2]))
    landed(GROUP_FFN2, gathered)
    kv = _memkv_fwd(memv, w["mem_norm"], full["w_kv_mem"])
    x3, hq, qm, att = _xattn_fwd(x2, w["xattn_norm"], full["w_q_mem"], kv, full["w_o_mem"])
    (dx4, a2, b2, s2, loss_part, d_final), _ = _ffn_fwd(
        x3, w["ffn2_norm"], full["ffn2_gate"], full["ffn2_up"], full["ffn2_down"], head=(w["final_norm"], target))

    core = lax.axis_index("c").astype(jnp.int32).reshape(1)
    parts = {}
    waiting = []

    def carried():
        names = [name for name, _ in waiting]
        exchange = _scatter_exchange([p for _, p in waiting]) if waiting else None
        del waiting[:]
        return names, exchange

    def weight_grad(name, a, b, scale=1.0):
        names, exchange = carried()
        partial, arrived = _weight_grad(a, b, scale, exchange=exchange)
        parts.update(zip(names, arrived))
        waiting.append((name, partial))

    (dx3, da2, db2, h4, d_ffn2_norm), _ = _ffn_bwd(
        x3, w["ffn2_norm"], dx4, a2, b2, full["ffn2_gate"], full["ffn2_up"], full["ffn2_down"])
    weight_grad("ffn2_down", s2, dx4, 0.5)
    weight_grad("ffn2_gate", da2, h4)
    weight_grad("ffn2_up", db2, h4)
    names, exchange = carried()
    (dx2, dqm, dkv, d_xattn_norm), arrived = _xattn_bwd(
        x2, w["xattn_norm"], dx3, qm, kv, full["w_q_mem"], full["w_o_mem"], exchange=exchange)
    parts.update(zip(names, arrived))
    weight_grad("w_o_mem", att, dx3)
    weight_grad("w_q_mem", hq, dqm)
    d_wkv_blocks, d_mem_norm = _memkv_bwd(memv, w["mem_norm"], dkv, full["w_kv_mem"])
    (from_sibling,) = _run_exchange(_pair_exchange([d_wkv_blocks]), "pair_exchange")
    waiting.append(("w_kv_mem", _pair_add(d_wkv_blocks, from_sibling, core)))
    weight_grad("w_out", ycat, dx2)
    names, exchange = carried()
    (dx1, dz, h2, d_mix_norm, d_lbp, d_gh, d_convw_t), arrived = _mix_bwd(
        x1, w["mix_norm"], dx2, z, o_raw, states, full["w_in"], w["lb_param"], w["hgrn_out_norm"], convw_t,
        full["w_out"], exchange=exchange)
    parts.update(zip(names, arrived))
    weight_grad("w_in", dz, h2)
    weight_grad("ffn1_down", s1, dx1, 0.5)
    (dx0, da1, db1, h1, d_ffn1_norm), _ = _ffn_bwd(
        x0, w["ffn1_norm"], dx1, a1, b1, full["ffn1_gate"], full["ffn1_up"], full["ffn1_down"])
    weight_grad("ffn1_gate", da1, h1)
    weight_grad("ffn1_up", db1, h1)

    small_parts = {
        "ffn1_norm": d_ffn1_norm, "mix_norm": d_mix_norm, "xattn_norm": d_xattn_norm, "mem_norm": d_mem_norm,
        "ffn2_norm": d_ffn2_norm, "final_norm": d_final, "lb_param": d_lbp, "hgrn_out_norm": d_gh,
        "conv_w": d_convw_t, "loss": loss_part,
    }
    names = [name for name, _ in waiting]
    arrived, total = _final_exchange([p for _, p in waiting], small_parts)
    parts.update(zip(names, arrived))

    g_out, d_out, m_out, v_out = {}, {}, {}, {}
    for name in LARGE:
        g_out[name], d_out[name], m_out[name], v_out[name] = _adamw_shard(parts[name], w[name], m[name], v[name])
    g_small = {}
    for name in SMALL:
        row, nrows, ncols = SMALL_LAYOUT[name]
        g_small[name] = total[row:row + nrows, 0:ncols]
    g_small["conv_w"] = lax.dynamic_slice_in_dim(g_small["conv_w"], me * conv_rows, conv_rows, axis=1)
    ds, ms, vs = _adamw_small(
        [g_small[k] for k in SMALL], [w[k] for k in SMALL], [m[k] for k in SMALL], [v[k] for k in SMALL])
    for i, name in enumerate(SMALL):
        g_out[name], d_out[name], m_out[name], v_out[name] = g_small[name], ds[i], ms[i], vs[i]

    def shaped(value, name):
        return (value.T if name in TRANSPOSED else value).reshape(given[name].shape)

    loss = total[SMALL_LAYOUT["loss"][0], 0]
    outs = [loss, dx0.reshape(x.shape)]
    for group in (g_out, d_out, m_out, v_out):
        outs += [shaped(group[name], name) for name in WEIGHTS]
    return tuple(outs)
```

```python
import jax
import jax.numpy as jnp
from jax import lax
from jax.experimental import pallas as pl
from jax.experimental.pallas import tpu as pltpu

F32 = jnp.float32
BF16 = jnp.bfloat16
MESH_IDS = pl.DeviceIdType.MESH

N_DEV = 8
EPS = 1e-6
HGRN_HEADS = 4
HGRN_DK = 128
HGRN_W = 512
CHUNK = 64
MEM_HEADS = 4
MEM_HD = 256
ADAM_LR = 0.001
ADAM_B1 = 0.9
ADAM_B2 = 0.999
ADAM_EPS = 1e-08
ADAM_WD = 0.01
ADAM_STEP = 10

TOKEN_TILE = 256
REDUCE_TILE = 1024
ADAMW_TILE_ELEMENTS = 256 * 1024
MXU_ROWS = 256
VMEM_LIMIT = 60 * 1024 * 1024
SMALL_ROWS = 16
NT = (((1,), (1,)), ((), ()))
TN = (((0,), (0,)), ((), ()))


def _params(sem=None):
    return pltpu.CompilerParams(dimension_semantics=sem, vmem_limit_bytes=VMEM_LIMIT)


def _dot(a, b, dims=None):
    if dims is None:
        return jnp.dot(a, b, preferred_element_type=F32)
    return lax.dot_general(a, b, dims, preferred_element_type=F32)


def _sigmoid(v):
    return 1.0 / (1.0 + jnp.exp(-v))


def _rms(x, g):
    r = lax.rsqrt(jnp.mean(x * x, axis=-1, keepdims=True) + EPS)
    xh = x * r
    return xh * g, xh, r


def _rms_bwd(dh, xh, r, g):
    dxh = dh * g
    return r * (dxh - xh * jnp.mean(dxh * xh, axis=-1, keepdims=True))


def _full(shape):
    return pl.BlockSpec(shape, lambda *_: (0,) * len(shape))


def _rows(tm, width):
    return pl.BlockSpec((tm, width), lambda i: (i, 0))


def _rows_rev(tm, width, n):
    return pl.BlockSpec((tm, width), lambda i: (n - 1 - i, 0))


def _accumulate(ref, first, value):
    @pl.when(first)
    def _():
        ref[...] = value

    @pl.when(jnp.logical_not(first))
    def _():
        ref[...] += value


class _Exchange:
    def __init__(self, operands, out_shapes, scratch, start, finish):
        self.operands, self.out_shapes, self.scratch = list(operands), list(out_shapes), list(scratch)
        self.start, self.finish = start, finish


def _call(body, *, name, grid, in_specs, out_specs, out_shape, args, scratch_shapes=(), exchange=None):
    semantics = ("arbitrary",) * len(grid)
    if exchange is None:
        out = pl.pallas_call(
            body, name=name, grid=grid, in_specs=in_specs, out_specs=out_specs, out_shape=out_shape,
            scratch_shapes=list(scratch_shapes), compiler_params=_params(semantics))(*args)
        return out, []
    hbm = pl.BlockSpec(memory_space=pltpu.HBM)
    n_in, n_out, n_scr = len(in_specs), len(out_specs), len(scratch_shapes)
    e_in, e_out = len(exchange.operands), len(exchange.out_shapes)

    def carried(*refs):
        ins, rest = refs[:n_in], refs[n_in:]
        e_ins, rest = rest[:e_in], rest[e_in:]
        outs, rest = rest[:n_out], rest[n_out:]
        e_outs, rest = rest[:e_out], rest[e_out:]
        scr, e_scr = rest[:n_scr], rest[n_scr:]
        first = last = None
        for axis, size in enumerate(grid):
            at_start, at_end = pl.program_id(axis) == 0, pl.program_id(axis) == size - 1
            first = at_start if first is None else jnp.logical_and(first, at_start)
            last = at_end if last is None else jnp.logical_and(last, at_end)

        @pl.when(first)
        def _():
            exchange.start(e_ins, e_outs, e_scr)

        body(*ins, *outs, *scr)

        @pl.when(last)
        def _():
            exchange.finish(e_ins, e_outs, e_scr)

    out = pl.pallas_call(
        carried, name=name, grid=grid, in_specs=list(in_specs) + [hbm] * e_in,
        out_specs=list(out_specs) + [hbm] * e_out, out_shape=list(out_shape) + exchange.out_shapes,
        scratch_shapes=list(scratch_shapes) + exchange.scratch,
        compiler_params=pltpu.CompilerParams(
            dimension_semantics=semantics, vmem_limit_bytes=VMEM_LIMIT, has_side_effects=True),
    )(*args, *exchange.operands)
    return out[:n_out], out[n_out:]


def _run_exchange(exchange, name):
    hbm = pl.BlockSpec(memory_space=pltpu.HBM)
    e_in, e_out = len(exchange.operands), len(exchange.out_shapes)

    def body(*refs):
        e_ins, e_outs, e_scr = refs[:e_in], refs[e_in:e_in + e_out], refs[e_in + e_out:]
        exchange.start(e_ins, e_outs, e_scr)
        exchange.finish(e_ins, e_outs, e_scr)

    return pl.pallas_call(
        body, name=name, in_specs=[hbm] * e_in, out_specs=[hbm] * e_out, out_shape=exchange.out_shapes,
        scratch_shapes=exchange.scratch, compiler_params=pltpu.CompilerParams(has_side_effects=True),
    )(*exchange.operands)


def _loss_head(xo, gf, tgt):
    d = xo.shape[1]
    y, xh, r = _rms(xo, gf)
    err = y - tgt
    dy = err * (1.0 / d)
    loss = 0.5 * jnp.sum(jnp.sum(err * err, axis=-1, keepdims=True) * (1.0 / d), axis=0, keepdims=True)
    return _rms_bwd(dy, xh, r, gf), loss, jnp.sum(dy * xh, axis=0, keepdims=True)


def _ffn_fwd(x, g, wg, wu, wd, exchange=None, head=None):
    t, d = x.shape
    f = wg.shape[0]
    tm = min(TOKEN_TILE, t)

    def body(x_ref, g_ref, wg_ref, wu_ref, wd_ref, *rest):
        xv = x_ref[...]
        h, _, _ = _rms(xv, g_ref[...])
        hb = h.astype(BF16)
        a = _dot(hb, wg_ref[...], NT)
        b = _dot(hb, wu_ref[...], NT)
        s = (a * _sigmoid(a) * b).astype(BF16)
        xo = xv + 0.5 * _dot(s, wd_ref[...])
        if head is None:
            xo_ref, a_ref, b_ref, s_ref = rest
            xo_ref[...] = xo
        else:
            gf_ref, tgt_ref, xo_ref, a_ref, b_ref, s_ref, loss_ref, dgf_ref = rest
            first = pl.program_id(0) == 0
            xo_ref[...], loss, dgf = _loss_head(xo, gf_ref[...], tgt_ref[...])
            _accumulate(loss_ref, first, jnp.broadcast_to(loss, (1, 128)))
            _accumulate(dgf_ref, first, dgf)
        a_ref[...] = a.astype(BF16)
        b_ref[...] = b.astype(BF16)
        s_ref[...] = s

    in_specs = [_rows(tm, d), _full((1, d)), _full((f, d)), _full((f, d)), _full((f, d))]
    out_specs = [_rows(tm, d), _rows(tm, f), _rows(tm, f), _rows(tm, f)]
    out_shape = [
        jax.ShapeDtypeStruct((t, d), F32),
        jax.ShapeDtypeStruct((t, f), BF16),
        jax.ShapeDtypeStruct((t, f), BF16),
        jax.ShapeDtypeStruct((t, f), BF16),
    ]
    args = (x, g, wg, wu, wd)
    if head is not None:
        in_specs += [_full((1, d)), _rows(tm, d)]
        out_specs += [_full((1, 128)), _full((1, d))]
        out_shape += [jax.ShapeDtypeStruct((1, 128), F32), jax.ShapeDtypeStruct((1, d), F32)]
        args += tuple(head)
    return _call(
        body, name="ffn_fwd", grid=(t // tm,), in_specs=in_specs, out_specs=out_specs, out_shape=out_shape,
        args=args, exchange=exchange)


def _ffn_bwd(x, g, dxo, a, b, wg, wu, wd, exchange=None):
    t, d = x.shape
    f = wg.shape[0]
    tm = min(TOKEN_TILE, t)

    def body(x_ref, g_ref, dxo_ref, a_ref, b_ref, wg_ref, wu_ref, wd_ref, dx_ref, da_ref, db_ref, h_ref, dg_ref):
        gv = g_ref[...]
        h, xh, r = _rms(x_ref[...], gv)
        dxo = dxo_ref[...]
        ds = _dot((0.5 * dxo).astype(BF16), wd_ref[...], NT)
        af = a_ref[...].astype(F32)
        bf = b_ref[...].astype(F32)
        sg = _sigmoid(af)
        da = (ds * bf * (sg * (1.0 + af * (1.0 - sg)))).astype(BF16)
        db = (ds * (af * sg)).astype(BF16)
        dh = _dot(da, wg_ref[...]) + _dot(db, wu_ref[...])
        dx_ref[...] = _rms_bwd(dh, xh, r, gv) + dxo
        da_ref[...] = da
        db_ref[...] = db
        h_ref[...] = h.astype(BF16)
        _accumulate(dg_ref, pl.program_id(0) == 0, jnp.sum(dh * xh, axis=0, keepdims=True))

    return _call(
        body,
        name="ffn_bwd",
        grid=(t // tm,),
        in_specs=[
            _rows(tm, d), _full((1, d)), _rows(tm, d), _rows(tm, f), _rows(tm, f),
            _full((f, d)), _full((f, d)), _full((f, d)),
        ],
        out_specs=[_rows(tm, d), _rows(tm, f), _rows(tm, f), _rows(tm, d), _full((1, d))],
        out_shape=[
            jax.ShapeDtypeStruct((t, d), F32),
            jax.ShapeDtypeStruct((t, f), BF16),
            jax.ShapeDtypeStruct((t, f), BF16),
            jax.ShapeDtypeStruct((t, d), BF16),
            jax.ShapeDtypeStruct((1, d), F32),
        ],
        args=(x, g, dxo, a, b, wg, wu, wd),
        exchange=exchange,
    )


def _weight_grad(a, b, scale=1.0, exchange=None):
    t, m = a.shape
    n = b.shape[1]
    chips = N_DEV // 2
    r = m // N_DEV
    tk = min(REDUCE_TILE, t)
    halves = 2
    nb = n // halves
    nk = t // tk

    def body(a_ref, b_ref, o_ref, acc, send_buf, recv_buf, send_sems, recv_sems):
        k, j = pl.program_id(0), pl.program_id(1)
        x, y, c, _ = _mesh_place()
        sibling, _ = _peer(x, y, c, 1)
        bv = b_ref[...]
        if scale != 1.0:
            bv = bv * scale
        bb = bv.astype(BF16)
        acc_half = acc.at[j]

        @pl.when(k == 0)
        def _():
            acc_half[...] = jnp.zeros_like(acc_half)

        for i in range(m // MXU_ROWS):
            rows = slice(i * MXU_ROWS, (i + 1) * MXU_ROWS)
            acc_half[rows, :] += _dot(a_ref[:, rows].astype(BF16), bb, TN)

        def to_sibling(half):
            return _remote(send_buf.at[half], recv_buf.at[half], send_sems.at[half], recv_sems.at[half], sibling)

        def owned_rows(q, core):
            return pl.ds(pl.multiple_of((2 * q + core) * r, 8), r)

        for half in range(halves):
            @pl.when(jnp.logical_and(k == nk - 1, j == half))
            def _():
                for q in range(chips):
                    send_buf[half, q] = acc[half, owned_rows(q, 1 - c), :].astype(BF16)
                to_sibling(half).start()

        @pl.when(jnp.logical_and(k == nk - 1, j == halves - 1))
        def _():
            for half in range(halves):
                to_sibling(half).wait_send()
                to_sibling(half).wait_recv()
                for q in range(chips):
                    o_ref[q, :, half * nb:(half + 1) * nb] = (
                        acc[half, owned_rows(q, c), :] + recv_buf[half, q].astype(F32)).astype(BF16)

    (partial,), arrived = _call(
        body,
        name="weight_grad",
        grid=(nk, halves),
        in_specs=[pl.BlockSpec((tk, m), lambda k, j: (k, 0)), pl.BlockSpec((tk, nb), lambda k, j: (k, j))],
        out_specs=[pl.BlockSpec((chips, r, n), lambda k, j: (0, 0, 0))],
        out_shape=[jax.ShapeDtypeStruct((chips, r, n), BF16)],
        scratch_shapes=[
            pltpu.VMEM((halves, m, nb), F32),
            pltpu.VMEM((halves, chips, r, nb), BF16), pltpu.VMEM((halves, chips, r, nb), BF16),
            pltpu.SemaphoreType.DMA((halves,)), pltpu.SemaphoreType.DMA((halves,)),
        ],
        args=(a, b),
        exchange=exchange,
    )
    return partial, arrived


def _chunk_cumsum(v, reverse=False):
    n = v.shape[0]
    pos = lax.broadcasted_iota(jnp.int32, (n, 1), 0) % CHUNK
    shift = 1
    while shift < CHUNK:
        if reverse:
            moved = pltpu.roll(v, n - shift, axis=0)
            v = v + jnp.where(pos < CHUNK - shift, moved, 0.0)
        else:
            moved = pltpu.roll(v, shift, axis=0)
            v = v + jnp.where(pos >= shift, moved, 0.0)
        shift *= 2
    return v


def _shift_rows(v, shift, edge):
    n = v.shape[0]
    row = lax.broadcasted_iota(jnp.int32, (n, 1), 0)
    out = pltpu.roll(v, shift % n, axis=0)
    if shift > 0:
        for j in range(shift):
            out = jnp.where(row == j, edge[8 - shift + j:8 - shift + j + 1, :], out)
    else:
        for j in range(-shift):
            out = jnp.where(row == n + shift + j, edge[j:j + 1, :], out)
    return out


def _gates(z, lbp):
    w = HGRN_W
    lb = _sigmoid(lbp[0:1, :] - lbp[1:2, :])
    zq = z[:, 0:w]
    sig = _sigmoid(z[:, w:2 * w])
    f = lb + (1.0 - lb) * sig
    sq = _sigmoid(zq)
    q = zq * sq * HGRN_DK ** -0.5
    return lb, sig, f, sq, q


def _short_conv(u, edge, cw):
    return cw[0:1, :] * _shift_rows(u, 2, edge) + cw[1:2, :] * _shift_rows(u, 1, edge) + cw[2:3, :] * u


def _causal_mask():
    row = lax.broadcasted_iota(jnp.int32, (CHUNK, CHUNK), 0)
    col = lax.broadcasted_iota(jnp.int32, (CHUNK, CHUNK), 1)
    return col <= row


def _mix_fwd(x, g, w_in, lbp, gh, convw_t, w_out, exchange=None):
    t, d = x.shape
    zw = w_in.shape[0]
    w = HGRN_W
    tm = min(TOKEN_TILE, t)
    nc = tm // CHUNK
    n_chunks = t // CHUNK

    def body(x_ref, g_ref, win_ref, lbp_ref, gh_ref, cw_ref, wout_ref,
             xo_ref, z_ref, o_ref, st_ref, y_ref, state, ucarry):
        @pl.when(pl.program_id(0) == 0)
        def _():
            state[...] = jnp.zeros_like(state)
            ucarry[...] = jnp.zeros_like(ucarry)

        xv = x_ref[...]
        h, _, _ = _rms(xv, g_ref[...])
        z_ref[...] = _dot(h.astype(BF16), win_ref[...], NT)
        z = z_ref[...]
        _, _, f, _, q = _gates(z, lbp_ref[...])
        bcum = _chunk_cumsum(jnp.log(f))
        kk = 1.0 - f
        vv = z[:, 2 * w:3 * w]
        mask = _causal_mask()
        for c in range(nc):
            rows = slice(c * CHUNK, (c + 1) * CHUNK)
            for hd in range(HGRN_HEADS):
                cols = slice(hd * HGRN_DK, (hd + 1) * HGRN_DK)
                b = bcum[rows, cols]
                blast = b[CHUNK - 1:CHUNK, :]
                qh = (q[rows, cols] * jnp.exp(b)).astype(BF16)
                kh = (kk[rows, cols] * jnp.exp(-b)).astype(BF16)
                kbar = (kk[rows, cols] * jnp.exp(blast - b)).astype(BF16)
                vb = vv[rows, cols].astype(BF16)
                st = state[hd]
                st_ref[c, hd] = st
                att = jnp.where(mask, _dot(qh, kh, NT), 0.0).astype(BF16)
                o_ref[rows, cols] = _dot(att, vb) + _dot(qh, st.astype(BF16), NT)
                state[hd] = st * jnp.exp(blast) + _dot(vb, kbar, TN)
        ghv = gh_ref[...]
        for hd in range(HGRN_HEADS):
            cols = slice(hd * HGRN_DK, (hd + 1) * HGRN_DK)
            on, _, _ = _rms(o_ref[:, cols], ghv[:, cols])
            zg = z[:, 3 * w + hd * HGRN_DK:3 * w + (hd + 1) * HGRN_DK]
            y_ref[:, cols] = (on * (zg * _sigmoid(zg))).astype(BF16)
        u = z[:, 5 * w:6 * w] * z[:, 6 * w:7 * w]
        conv = _short_conv(u, ucarry[...], cw_ref[...])
        ucarry[...] = u[tm - 8:tm, :]
        y_ref[:, w:2 * w] = (z[:, 4 * w:5 * w] * conv).astype(BF16)
        xo_ref[...] = xv + _dot(y_ref[...], wout_ref[...])

    return _call(
        body,
        name="mix_fwd",
        grid=(t // tm,),
        in_specs=[
            _rows(tm, d), _full((1, d)), _full((zw, d)), _full((2, w)), _full((1, w)), _full((3, w)),
            _full((2 * w, d)),
        ],
        out_specs=[
            _rows(tm, d), _rows(tm, zw), _rows(tm, w),
            pl.BlockSpec((nc, HGRN_HEADS, HGRN_DK, HGRN_DK), lambda i: (i, 0, 0, 0)),
            _rows(tm, 2 * w),
        ],
        out_shape=[
            jax.ShapeDtypeStruct((t, d), F32),
            jax.ShapeDtypeStruct((t, zw), F32),
            jax.ShapeDtypeStruct((t, w), F32),
            jax.ShapeDtypeStruct((n_chunks, HGRN_HEADS, HGRN_DK, HGRN_DK), F32),
            jax.ShapeDtypeStruct((t, 2 * w), BF16),
        ],
        scratch_shapes=[pltpu.VMEM((HGRN_HEADS, HGRN_DK, HGRN_DK), F32), pltpu.VMEM((8, w), F32)],
        args=(x, g, w_in, lbp, gh, convw_t, w_out),
        exchange=exchange,
    )


def _mix_bwd(x, g, dxo, z, o, states, w_in, lbp, gh, convw_t, w_out, exchange=None):
    t, d = x.shape
    zw = w_in.shape[0]
    w = HGRN_W
    tm = min(TOKEN_TILE, t)
    nc = tm // CHUNK
    n = t // tm

    def body(x_ref, g_ref, dxo_ref, z_ref, zprev_ref, o_ref, st_ref, win_ref, lbp_ref, gh_ref, cw_ref, wout_ref,
             dx_ref, dz_ref, h_ref, dg_ref, dlbp_ref, dgh_ref, dcw_ref,
             dstate, dcarry, do_buf, dq_buf, dk_buf, db_buf):
        first = pl.program_id(0) == 0

        @pl.when(first)
        def _():
            dstate[...] = jnp.zeros_like(dstate)
            dcarry[...] = jnp.zeros_like(dcarry)

        gv = g_ref[...]
        h, xh, r = _rms(x_ref[...], gv)
        h_ref[...] = h.astype(BF16)
        dxo = dxo_ref[...]
        dy = _dot(dxo.astype(BF16), wout_ref[...], NT)
        z = z_ref[...]
        lb, sig, f, sq, q = _gates(z, lbp_ref[...])
        bcum = _chunk_cumsum(jnp.log(f))
        kk = 1.0 - f
        vv = z[:, 2 * w:3 * w]

        ghv = gh_ref[...]
        dgh_parts = []
        for hd in range(HGRN_HEADS):
            cols = slice(hd * HGRN_DK, (hd + 1) * HGRN_DK)
            gcols = slice(3 * w + hd * HGRN_DK, 3 * w + (hd + 1) * HGRN_DK)
            on, oh, rr = _rms(o_ref[:, cols], ghv[:, cols])
            zg = z[:, gcols]
            sgz = _sigmoid(zg)
            dyh = dy[:, cols]
            don = dyh * (zg * sgz)
            dz_ref[:, gcols] = (dyh * on * (sgz * (1.0 + zg * (1.0 - sgz)))).astype(BF16)
            dgh_parts.append(jnp.sum(don * oh, axis=0, keepdims=True))
            do_buf[:, cols] = _rms_bwd(don, oh, rr, ghv[:, cols])
        _accumulate(dgh_ref, first, jnp.concatenate(dgh_parts, axis=1))

        zb = z[:, 4 * w:5 * w]
        zc = z[:, 5 * w:6 * w]
        zu = z[:, 6 * w:7 * w]
        u = zc * zu
        cw = cw_ref[...]
        zp = zprev_ref[...]
        uprev = jnp.where(pl.program_id(0) == n - 1, 0.0, zp[:, 5 * w:6 * w] * zp[:, 6 * w:7 * w])
        dyc = dy[:, w:2 * w]
        dz_ref[:, 4 * w:5 * w] = (dyc * _short_conv(u, uprev, cw)).astype(BF16)
        dconv = dyc * zb
        edge = dcarry[...]
        dconv1 = _shift_rows(dconv, -1, edge)
        dconv2 = _shift_rows(dconv, -2, edge)
        dcarry[...] = dconv[0:8, :]
        du = cw[2:3, :] * dconv + cw[1:2, :] * dconv1 + cw[0:1, :] * dconv2
        dz_ref[:, 5 * w:6 * w] = (du * zu).astype(BF16)
        dz_ref[:, 6 * w:7 * w] = (du * zc).astype(BF16)
        _accumulate(dcw_ref, first, jnp.concatenate([
            jnp.sum(u * dconv2, axis=0, keepdims=True),
            jnp.sum(u * dconv1, axis=0, keepdims=True),
            jnp.sum(u * dconv, axis=0, keepdims=True)], axis=0))

        mask = _causal_mask()
        last_row = lax.broadcasted_iota(jnp.int32, (CHUNK, 1), 0) == CHUNK - 1
        for c in reversed(range(nc)):
            rows = slice(c * CHUNK, (c + 1) * CHUNK)
            for hd in range(HGRN_HEADS):
                cols = slice(hd * HGRN_DK, (hd + 1) * HGRN_DK)
                b = bcum[rows, cols]
                blast = b[CHUNK - 1:CHUNK, :]
                eb = jnp.exp(b)
                enb = jnp.exp(-b)
                erest = jnp.exp(blast - b)
                elast = jnp.exp(blast)
                qh = q[rows, cols] * eb
                kh = kk[rows, cols] * enb
                kbar = kk[rows, cols] * erest
                qhb = qh.astype(BF16)
                khb = kh.astype(BF16)
                kbarb = kbar.astype(BF16)
                vb = vv[rows, cols].astype(BF16)
                st = st_ref[c, hd]
                dst = dstate[hd]
                dstb = dst.astype(BF16)
                dob = do_buf[rows, cols].astype(BF16)
                att = jnp.where(mask, _dot(qhb, khb, NT), 0.0).astype(BF16)
                datt = jnp.where(mask, _dot(dob, vb, NT), 0.0).astype(BF16)
                dv = _dot(att, dob, TN) + _dot(kbarb, dstb, NT)
                dqh = _dot(datt, khb) + _dot(dob, st.astype(BF16))
                dkh = _dot(datt, qhb, TN)
                dkbar = _dot(vb, dstb)
                dstate[hd] = dst * elast + _dot(dob, qhb, TN)
                kbar_dkbar = kbarb.astype(F32) * dkbar
                db = qhb.astype(F32) * dqh - khb.astype(F32) * dkh - kbar_dkbar
                db_last = (jnp.sum(kbar_dkbar, axis=0, keepdims=True)
                           + jnp.sum(dst * st, axis=0, keepdims=True) * elast)
                db_buf[rows, cols] = jnp.where(last_row, db + db_last, db)
                dq_buf[rows, cols] = dqh * eb
                dk_buf[rows, cols] = dkh * enb + dkbar * erest
                dz_ref[rows, 2 * w + hd * HGRN_DK:2 * w + (hd + 1) * HGRN_DK] = dv.astype(BF16)

        dlogf = _chunk_cumsum(db_buf[...], reverse=True)
        df = dlogf / f - dk_buf[...]
        zq = z[:, 0:w]
        dz_ref[:, 0:w] = (dq_buf[...] * HGRN_DK ** -0.5 * (sq * (1.0 + zq * (1.0 - sq)))).astype(BF16)
        dz_ref[:, w:2 * w] = (df * (1.0 - lb) * sig * (1.0 - sig)).astype(BF16)
        dlb = jnp.sum(df * (1.0 - sig), axis=0, keepdims=True) * lb * (1.0 - lb)
        _accumulate(dlbp_ref, first, jnp.concatenate([dlb, -dlb], axis=0))

        dh = _dot(dz_ref[...], win_ref[...])
        dx_ref[...] = _rms_bwd(dh, xh, r, gv) + dxo
        _accumulate(dg_ref, first, jnp.sum(dh * xh, axis=0, keepdims=True))

    return _call(
        body,
        name="mix_bwd",
        grid=(n,),
        in_specs=[
            _rows_rev(tm, d, n), _full((1, d)), _rows_rev(tm, d, n), _rows_rev(tm, zw, n),
            pl.BlockSpec((8, zw), lambda i: (jnp.maximum((n - 1 - i) * (tm // 8) - 1, 0), 0)),
            _rows_rev(tm, w, n),
            pl.BlockSpec((nc, HGRN_HEADS, HGRN_DK, HGRN_DK), lambda i: (n - 1 - i, 0, 0, 0)),
            _full((zw, d)), _full((2, w)), _full((1, w)), _full((3, w)), _full((2 * w, d)),
        ],
        out_specs=[
            _rows_rev(tm, d, n), _rows_rev(tm, zw, n), _rows_rev(tm, d, n),
            _full((1, d)), _full((2, w)), _full((1, w)), _full((3, w)),
        ],
        out_shape=[
            jax.ShapeDtypeStruct((t, d), F32),
            jax.ShapeDtypeStruct((t, zw), BF16),
            jax.ShapeDtypeStruct((t, d), BF16),
            jax.ShapeDtypeStruct((1, d), F32),
            jax.ShapeDtypeStruct((2, w), F32),
            jax.ShapeDtypeStruct((1, w), F32),
            jax.ShapeDtypeStruct((3, w), F32),
        ],
        scratch_shapes=[
            pltpu.VMEM((HGRN_HEADS, HGRN_DK, HGRN_DK), F32), pltpu.VMEM((8, w), F32),
            pltpu.VMEM((tm, w), F32), pltpu.VMEM((tm, w), F32), pltpu.VMEM((tm, w), F32), pltpu.VMEM((tm, w), F32),
        ],
        args=(x, g, dxo, z, z, o, states, w_in, lbp, gh, convw_t, w_out),
        exchange=exchange,
    )


def _memkv_fwd(mem, g, wkv):
    m, d = mem.shape
    nb, _, cb = wkv.shape

    def body(mem_ref, g_ref, wkv_ref, kv_ref):
        mn, _, _ = _rms(mem_ref[...], g_ref[...])
        mnb = mn.astype(BF16)
        for j in range(nb):
            kv_ref[:, j * cb:(j + 1) * cb] = _dot(mnb, wkv_ref[j]).astype(BF16)

    return pl.pallas_call(
        body,
        name="memkv_fwd",
        out_shape=jax.ShapeDtypeStruct((m, nb * cb), BF16),
        compiler_params=_params(),
    )(mem, g, wkv)


def _memkv_bwd(mem, g, dkv, wkv):
    m, d = mem.shape
    nb, _, cb = wkv.shape

    def body(mem_ref, g_ref, dkv_ref, wkv_ref, dw_ref, dg_ref):
        mn, xh, _ = _rms(mem_ref[...], g_ref[...])
        mnb = mn.astype(BF16)
        dmn = jnp.zeros((m, d), F32)
        for j in range(nb):
            dkvb = dkv_ref[:, j * cb:(j + 1) * cb].astype(BF16)
            dw_ref[j] = _dot(mnb, dkvb, TN).astype(BF16)
            dmn = dmn + _dot(dkvb, wkv_ref[j], NT)
        dg_ref[...] = jnp.sum(dmn * xh, axis=0, keepdims=True)

    return pl.pallas_call(
        body,
        name="memkv_bwd",
        out_shape=[jax.ShapeDtypeStruct((nb, d, cb), BF16), jax.ShapeDtypeStruct((1, d), F32)],
        compiler_params=_params(),
    )(mem, g, dkv, wkv)


def _softmax_rows(qm_h, k_h):
    sc = _dot(qm_h, k_h, NT) * MEM_HD ** -0.5
    e = jnp.exp(sc - jnp.max(sc, axis=-1, keepdims=True))
    return e / jnp.sum(e, axis=-1, keepdims=True)


def _xattn_fwd(x, g, wq, kv, wo):
    t, d = x.shape
    m = kv.shape[0]
    tm = min(TOKEN_TILE, t)

    def body(x_ref, g_ref, wq_ref, kv_ref, wo_ref, xo_ref, hq_ref, qm_ref, att_ref):
        xv = x_ref[...]
        h, _, _ = _rms(xv, g_ref[...])
        hq_ref[...] = h.astype(BF16)
        qm_ref[...] = _dot(hq_ref[...], wq_ref[...]).astype(BF16)
        for hd in range(MEM_HEADS):
            cols = slice(hd * MEM_HD, (hd + 1) * MEM_HD)
            p = _softmax_rows(qm_ref[:, cols], kv_ref[:, cols])
            att_ref[:, cols] = _dot(p.astype(BF16), kv_ref[:, d + hd * MEM_HD:d + (hd + 1) * MEM_HD]).astype(BF16)
        xo_ref[...] = xv + _dot(att_ref[...], wo_ref[...])

    return pl.pallas_call(
        body,
        name="xattn_fwd",
        grid=(t // tm,),
        in_specs=[_rows(tm, d), _full((1, d)), _full((d, d)), _full((m, 2 * d)), _full((d, d))],
        out_specs=[_rows(tm, d), _rows(tm, d), _rows(tm, d), _rows(tm, d)],
        out_shape=[
            jax.ShapeDtypeStruct((t, d), F32),
            jax.ShapeDtypeStruct((t, d), BF16),
            jax.ShapeDtypeStruct((t, d), BF16),
            jax.ShapeDtypeStruct((t, d), BF16),
        ],
        compiler_params=_params(("arbitrary",)),
    )(x, g, wq, kv, wo)


def _xattn_bwd(x, g, dxo, qm, kv, wq, wo, exchange=None):
    t, d = x.shape
    m = kv.shape[0]
    tm = min(TOKEN_TILE, t)

    def body(x_ref, g_ref, dxo_ref, qm_ref, kv_ref, wq_ref, wo_ref, dx_ref, dqm_ref, dkv_ref, dg_ref):
        first = pl.program_id(0) == 0

        @pl.when(first)
        def _():
            dkv_ref[...] = jnp.zeros_like(dkv_ref)

        gv = g_ref[...]
        _, xh, r = _rms(x_ref[...], gv)
        dxo = dxo_ref[...]
        datt = _dot(dxo.astype(BF16), wo_ref[...], NT).astype(BF16)
        for hd in range(MEM_HEADS):
            cols = slice(hd * MEM_HD, (hd + 1) * MEM_HD)
            vcols = slice(d + hd * MEM_HD, d + (hd + 1) * MEM_HD)
            qm_h = qm_ref[:, cols]
            p = _softmax_rows(qm_h, kv_ref[:, cols])
            datt_h = datt[:, cols]
            dp = _dot(datt_h, kv_ref[:, vcols], NT)
            dsc = (p * (dp - jnp.sum(p * dp, axis=-1, keepdims=True)) * MEM_HD ** -0.5).astype(BF16)
            dqm_ref[:, cols] = _dot(dsc, kv_ref[:, cols]).astype(BF16)
            dkv_ref[:, cols] += _dot(dsc, qm_h, TN)
            dkv_ref[:, vcols] += _dot(p.astype(BF16), datt_h, TN)
        dh = _dot(dqm_ref[...], wq_ref[...], NT)
        dx_ref[...] = _rms_bwd(dh, xh, r, gv) + dxo
        _accumulate(dg_ref, first, jnp.sum(dh * xh, axis=0, keepdims=True))

    return _call(
        body,
        name="xattn_bwd",
        grid=(t // tm,),
        in_specs=[
            _rows(tm, d), _full((1, d)), _rows(tm, d), _rows(tm, d), _full((m, 2 * d)), _full((d, d)), _full((d, d)),
        ],
        out_specs=[_rows(tm, d), _rows(tm, d), _full((m, 2 * d)), _full((1, d))],
        out_shape=[
            jax.ShapeDtypeStruct((t, d), F32),
            jax.ShapeDtypeStruct((t, d), BF16),
            jax.ShapeDtypeStruct((m, 2 * d), F32),
            jax.ShapeDtypeStruct((1, d), F32),
        ],
        args=(x, g, dxo, qm, kv, wq, wo),
        exchange=exchange,
    )


def _mesh_place():
    x, y, c = lax.axis_index("x"), lax.axis_index("y"), lax.axis_index("c")
    return x, y, c, 4 * x + 2 * y + c


def _peer(x, y, c, k):
    px = 1 - x if k & 4 else x
    py = 1 - y if k & 2 else y
    pc = 1 - c if k & 1 else c
    return (px, py, pc), 4 * px + 2 * py + pc


ICI_HOPS = (2, 4, 6)
N_HOPS = len(ICI_HOPS)


def _remote(src, dst, send_sem, recv_sem, peer):
    return pltpu.make_async_remote_copy(
        src_ref=src, dst_ref=dst, send_sem=send_sem, recv_sem=recv_sem, device_id=peer, device_id_type=MESH_IDS)


def _gather_exchange(shards):
    n = len(shards)

    def start(src, dst, sems):
        ici_send, ici_recv, pair_send, pair_recv, local = sems
        x, y, c, me = _mesh_place()
        sibling, _ = _peer(x, y, c, 1)
        for a in range(n):
            pltpu.make_async_copy(src[a], dst[a].at[me], local.at[a]).start()
            for j, k in enumerate(ICI_HOPS):
                peer, _ = _peer(x, y, c, k)
                _remote(src[a], dst[a].at[me], ici_send.at[a, j], ici_recv.at[a, j], peer).start()
            _remote(src[a], dst[a].at[me], pair_send.at[a, 0], pair_recv.at[a, 0], sibling).start()

    def finish(src, dst, sems):
        ici_send, ici_recv, pair_send, pair_recv, local = sems
        x, y, c, me = _mesh_place()
        sibling, sibling_index = _peer(x, y, c, 1)
        for a in range(n):
            for j, k in enumerate(ICI_HOPS):
                peer, peer_index = _peer(x, y, c, k)
                slot = dst[a].at[peer_index]
                _remote(src[a], slot, ici_send.at[a, j], ici_recv.at[a, j], peer).wait_recv()
                _remote(slot, slot, pair_send.at[a, 1 + j], pair_recv.at[a, 1 + j], sibling).start()
        for a in range(n):
            pltpu.make_async_copy(src[a], dst[a].at[me], local.at[a]).wait()
            for j, k in enumerate(ICI_HOPS):
                peer, _ = _peer(x, y, c, k)
                _remote(src[a], dst[a].at[me], ici_send.at[a, j], ici_recv.at[a, j], peer).wait_send()
            for j, k in enumerate((0,) + ICI_HOPS):
                _, from_sibling = _peer(x, y, c, k | 1)
                passed = _remote(src[a], dst[a].at[from_sibling], pair_send.at[a, j], pair_recv.at[a, j], sibling)
                passed.wait_send()
                passed.wait_recv()

    return _Exchange(
        shards,
        [jax.ShapeDtypeStruct((N_DEV,) + s.shape, s.dtype) for s in shards],
        [
            pltpu.SemaphoreType.DMA((n, N_HOPS)), pltpu.SemaphoreType.DMA((n, N_HOPS)),
            pltpu.SemaphoreType.DMA((n, N_HOPS + 1)), pltpu.SemaphoreType.DMA((n, N_HOPS + 1)),
            pltpu.SemaphoreType.DMA((n,)),
        ],
        start, finish)


def _pair_exchange(blocks):
    n = len(blocks)
    chips = N_DEV // 2

    def copies(src, dst, sems):
        send, recv = sems
        x, y, c, _ = _mesh_place()
        sibling, _ = _peer(x, y, c, 1)
        return [_remote(src[a].at[2 * q + (1 - c)], dst[a].at[q], send.at[a, q], recv.at[a, q], sibling)
                for a in range(n) for q in range(chips)]

    def start(src, dst, sems):
        for cp in copies(src, dst, sems):
            cp.start()

    def finish(src, dst, sems):
        for cp in copies(src, dst, sems):
            cp.wait_send()
            cp.wait_recv()

    return _Exchange(
        blocks,
        [jax.ShapeDtypeStruct((chips,) + b.shape[1:], b.dtype) for b in blocks],
        [pltpu.SemaphoreType.DMA((n, chips)), pltpu.SemaphoreType.DMA((n, chips))],
        start, finish)


def _pair_add(blocks, received, core):
    _, r, c = blocks.shape
    chips = N_DEV // 2
    tr = r
    while tr > 512:
        tr //= 2

    def body(core_ref, mine_ref, got_ref, o_ref):
        o_ref[...] = (mine_ref[...].astype(F32) + got_ref[...].astype(F32)).astype(BF16)

    return pl.pallas_call(
        body,
        name="pair_add",
        grid_spec=pltpu.PrefetchScalarGridSpec(
            num_scalar_prefetch=1,
            grid=(chips, r // tr),
            in_specs=[
                pl.BlockSpec((None, None, tr, c), lambda q, i, core_ref: (q, core_ref[0], i, 0)),
                pl.BlockSpec((None, tr, c), lambda q, i, core_ref: (q, i, 0)),
            ],
            out_specs=pl.BlockSpec((None, tr, c), lambda q, i, core_ref: (q, i, 0)),
        ),
        out_shape=jax.ShapeDtypeStruct((chips, r, c), BF16),
        compiler_params=_params(("parallel", "parallel")),
    )(core, blocks.reshape(chips, 2, r, c), received)


def _scatter_copies(src, dst, sems, n, arrivals=False):
    send, recv, local = sems
    x, y, c, _ = _mesh_place()
    chip = 2 * x + y
    if arrivals is None:
        return [pltpu.make_async_copy(src[a].at[chip], dst[a].at[chip], local.at[a]) for a in range(n)]
    copies = []
    for a in range(n):
        for j, k in enumerate(ICI_HOPS):
            peer, _ = _peer(x, y, c, k)
            peer_chip = 2 * peer[0] + peer[1]
            slot = dst[a].at[peer_chip if arrivals else chip]
            copies.append(_remote(src[a].at[peer_chip], slot, send.at[a, j], recv.at[a, j], peer))
    return copies


def _scatter_start(src, dst, sems, n):
    for cp in _scatter_copies(src, dst, sems, n, arrivals=None) + _scatter_copies(src, dst, sems, n):
        cp.start()


def _scatter_finish(src, dst, sems, n):
    for cp in _scatter_copies(src, dst, sems, n, arrivals=None):
        cp.wait()
    for cp in _scatter_copies(src, dst, sems, n):
        cp.wait_send()
    for cp in _scatter_copies(src, dst, sems, n, arrivals=True):
        cp.wait_recv()


def _scatter_scratch(n):
    return [pltpu.SemaphoreType.DMA((n, N_HOPS)), pltpu.SemaphoreType.DMA((n, N_HOPS)), pltpu.SemaphoreType.DMA((n,))]


def _scatter_exchange(partials):
    n = len(partials)
    return _Exchange(
        partials, [jax.ShapeDtypeStruct(p.shape, p.dtype) for p in partials], _scatter_scratch(n),
        lambda src, dst, sems: _scatter_start(src, dst, sems, n),
        lambda src, dst, sems: _scatter_finish(src, dst, sems, n))


SMALL_LAYOUT = {
    "ffn1_norm": (0, 1, 1024), "mix_norm": (1, 1, 1024), "xattn_norm": (2, 1, 1024), "mem_norm": (3, 1, 1024),
    "ffn2_norm": (4, 1, 1024), "final_norm": (5, 1, 1024), "lb_param": (6, 2, 512), "hgrn_out_norm": (8, 1, 512),
    "conv_w": (9, 3, 512), "loss": (12, 1, 128),
}


def _final_exchange(partials, small):
    n = len(partials)
    names = list(small)
    width = 1024

    def body(*refs):
        src = refs[:n]
        pieces = refs[n:n + len(names)]
        dst = refs[n + len(names):2 * n + len(names)]
        total_ref = refs[2 * n + len(names)]
        pack, gathered, small_send, small_recv = refs[2 * n + len(names) + 1:2 * n + len(names) + 5]
        sems = refs[2 * n + len(names) + 5:]
        x, y, c, me = _mesh_place()
        pack[...] = jnp.zeros_like(pack)
        for name, piece in zip(names, pieces):
            row, nrows, ncols = SMALL_LAYOUT[name]
            pack[row:row + nrows, 0:ncols] = piece[...]
        for k in range(1, N_DEV):
            peer, _ = _peer(x, y, c, k)
            _remote(pack, gathered.at[me], small_send.at[k - 1], small_recv.at[k - 1], peer).start()
        _scatter_start(src, dst, sems, n)
        gathered[me] = pack[...]
        for k in range(1, N_DEV):
            peer, peer_index = _peer(x, y, c, k)
            landed = _remote(pack, gathered.at[peer_index], small_send.at[k - 1], small_recv.at[k - 1], peer)
            landed.wait_send()
            landed.wait_recv()
        total = gathered[0]
        for j in range(1, N_DEV):
            total = total + gathered[j]
        total_ref[...] = total
        _scatter_finish(src, dst, sems, n)

    hbm = pl.BlockSpec(memory_space=pltpu.HBM)
    vmem = pl.BlockSpec(memory_space=pltpu.VMEM)
    out = pl.pallas_call(
        body,
        name="final_exchange",
        in_specs=[hbm] * n + [vmem] * len(names),
        out_specs=[hbm] * n + [vmem],
        out_shape=[jax.ShapeDtypeStruct(p.shape, p.dtype) for p in partials]
        + [jax.ShapeDtypeStruct((SMALL_ROWS, width), F32)],
        scratch_shapes=[
            pltpu.VMEM((SMALL_ROWS, width), F32), pltpu.VMEM((N_DEV, SMALL_ROWS, width), F32),
            pltpu.SemaphoreType.DMA((N_DEV - 1,)), pltpu.SemaphoreType.DMA((N_DEV - 1,)),
        ] + _scatter_scratch(n),
        compiler_params=pltpu.CompilerParams(has_side_effects=True),
    )(*partials, *[small[k] for k in names])
    return out[:n], out[n]


def _adamw_math(w, g, m, v):
    m = ADAM_B1 * m + (1.0 - ADAM_B1) * g
    v = ADAM_B2 * v + (1.0 - ADAM_B2) * (g * g)
    m_hat = m / (1.0 - ADAM_B1 ** ADAM_STEP)
    v_hat = v / (1.0 - ADAM_B2 ** ADAM_STEP)
    delta = -ADAM_LR * (m_hat / (jnp.sqrt(v_hat) + ADAM_EPS) + ADAM_WD * w)
    return delta, m, v


def _adamw_shard(parts, w, m, v):
    r, c = w.shape
    n_parts = parts.shape[0]
    tr = max(rows for rows in range(16, r + 1, 16) if r % rows == 0 and rows * c <= ADAMW_TILE_ELEMENTS)

    def body(p_ref, w_ref, m_ref, v_ref, g_ref, d_ref, mo_ref, vo_ref):
        g = p_ref[0].astype(F32)
        for j in range(1, n_parts):
            g = g + p_ref[j].astype(F32)
        delta, mn, vn = _adamw_math(w_ref[...], g, m_ref[...], v_ref[...])
        g_ref[...] = g
        d_ref[...] = delta
        mo_ref[...] = mn
        vo_ref[...] = vn

    tile = pl.BlockSpec((tr, c), lambda i: (i, 0))
    return pl.pallas_call(
        body,
        name="adamw_shard",
        grid=(r // tr,),
        in_specs=[pl.BlockSpec((n_parts, tr, c), lambda i: (0, i, 0)), tile, tile, tile],
        out_specs=[tile] * 4,
        out_shape=[jax.ShapeDtypeStruct((r, c), F32)] * 4,
        compiler_params=_params(("parallel",)),
    )(parts, w, m, v)


def _adamw_small(gs, ws, ms, vs):
    n = len(gs)

    def body(*refs):
        g_refs, w_refs, m_refs, v_refs = refs[:n], refs[n:2 * n], refs[2 * n:3 * n], refs[3 * n:4 * n]
        d_out, m_out, v_out = refs[4 * n:5 * n], refs[5 * n:6 * n], refs[6 * n:7 * n]
        for i in range(n):
            delta, mn, vn = _adamw_math(w_refs[i][...], g_refs[i][...], m_refs[i][...], v_refs[i][...])
            d_out[i][...] = delta
            m_out[i][...] = mn
            v_out[i][...] = vn

    shapes = [jax.ShapeDtypeStruct(w.shape, F32) for w in ws]
    out = pl.pallas_call(
        body,
        name="adamw_small",
        out_shape=shapes * 3,
        compiler_params=_params(),
    )(*gs, *ws, *ms, *vs)
    return out[:n], out[n:2 * n], out[2 * n:]


TRANSPOSED = ("ffn1_gate", "ffn1_up", "w_in", "ffn2_gate", "ffn2_up", "conv_w")
GROUP_FFN1 = ("ffn1_gate", "ffn1_up", "ffn1_down")
GROUP_MIX = ("w_in", "w_out")
GROUP_XATTN = ("w_q_mem", "w_kv_mem", "w_o_mem")
GROUP_FFN2 = ("ffn2_gate", "ffn2_up", "ffn2_down")
LARGE = GROUP_FFN1 + GROUP_MIX + GROUP_XATTN + GROUP_FFN2
SMALL = ("ffn1_norm", "mix_norm", "lb_param", "hgrn_out_norm", "conv_w", "xattn_norm", "mem_norm", "ffn2_norm",
         "final_norm")
WEIGHTS = ("ffn1_norm", "ffn1_gate", "ffn1_up", "ffn1_down", "mix_norm", "w_in", "lb_param", "hgrn_out_norm",
           "conv_w", "w_out", "xattn_norm", "mem_norm", "w_q_mem", "w_kv_mem", "w_o_mem", "ffn2_norm", "ffn2_gate",
           "ffn2_up", "ffn2_down", "final_norm")


def kernel(x, mem, ffn1_norm, ffn1_gate, ffn1_up, ffn1_down, mix_norm, w_in, lb_param, hgrn_out_norm, conv_w, w_out, xattn_norm, mem_norm, w_q_mem, w_kv_mem, w_o_mem, ffn2_norm, ffn2_gate, ffn2_up, ffn2_down, final_norm, loss_target, m_ffn1_norm, m_ffn1_gate, m_ffn1_up, m_ffn1_down, m_mix_norm, m_w_in, m_lb_param, m_hgrn_out_norm, m_conv_w, m_w_out, m_xattn_norm, m_mem_norm, m_w_q_mem, m_w_kv_mem, m_w_o_mem, m_ffn2_norm, m_ffn2_gate, m_ffn2_up, m_ffn2_down, m_final_norm, v_ffn1_norm, v_ffn1_gate, v_ffn1_up, v_ffn1_down, v_mix_norm, v_w_in, v_lb_param, v_hgrn_out_norm, v_conv_w, v_w_out, v_xattn_norm, v_mem_norm, v_w_q_mem, v_w_kv_mem, v_w_o_mem, v_ffn2_norm, v_ffn2_gate, v_ffn2_up, v_ffn2_down, v_final_norm):
    given = dict(locals())
    me = 4 * lax.axis_index("x") + 2 * lax.axis_index("y") + lax.axis_index("c")
    x0, memv, target = x[0], mem[0], loss_target[0]

    def shard(prefix, name):
        v = given[prefix + name]
        if v.ndim == 1:
            return v.reshape(1, -1)
        if v.ndim == 2:
            return v
        return v[0].T if name in TRANSPOSED else v[0]

    w = {name: shard("", name) for name in WEIGHTS}
    m = {name: shard("m_", name) for name in WEIGHTS}
    v = {name: shard("v_", name) for name in WEIGHTS}

    conv_taps, conv_rows = w["conv_w"].shape
    conv_tile = jnp.pad(w["conv_w"], ((0, 8 - conv_taps), (0, 128 - conv_rows)))
    wire = {name: w[name].astype(BF16) for name in LARGE}
    full = {}

    def landed(names, gathered):
        for name, blocks in zip(names, gathered):
            _, r, c = blocks.shape
            full[name] = blocks if name == "w_kv_mem" else blocks.reshape(N_DEV * r, c)

    landed(GROUP_FFN1, _run_exchange(_gather_exchange([wire[k] for k in GROUP_FFN1]), "gather_first"))

    (x1, a1, b1, s1), gathered = _ffn_fwd(
        x0, w["ffn1_norm"], full["ffn1_gate"], full["ffn1_up"], full["ffn1_down"],
        exchange=_gather_exchange([wire[k] for k in GROUP_MIX + GROUP_XATTN] + [conv_tile]))
    landed(GROUP_MIX + GROUP_XATTN, gathered)
    convw_t = gathered[-1][:, :conv_taps, :conv_rows].transpose(1, 0, 2).reshape(conv_taps, N_DEV * conv_rows)
    (x2, z, o_raw, states, ycat), gathered = _mix_fwd(
        x1, w["mix_norm"], full["w_in"], w["lb_param"], w["hgrn_out_norm"], convw_t, full["w_out"],
        exchange=_gather_exchange([wire[k] for k in GROUP_FFN2]))
    landed(GROUP_FFN2, gathered)
    kv = _memkv_fwd(memv, w["mem_norm"], full["w_kv_mem"])
    x3, hq, qm, att = _xattn_fwd(x2, w["xattn_norm"], full["w_q_mem"], kv, full["w_o_mem"])
    (dx4, a2, b2, s2, loss_part, d_final), _ = _ffn_fwd(
        x3, w["ffn2_norm"], full["ffn2_gate"], full["ffn2_up"], full["ffn2_down"], head=(w["final_norm"], target))

    core = lax.axis_index("c").astype(jnp.int32).reshape(1)
    parts = {}
    waiting = []

    def carried():
        names = [name for name, _ in waiting]
        exchange = _scatter_exchange([p for _, p in waiting]) if waiting else None
        del waiting[:]
        return names, exchange

    def weight_grad(name, a, b, scale=1.0):
        names, exchange = carried()
        partial, arrived = _weight_grad(a, b, scale, exchange=exchange)
        parts.update(zip(names, arrived))
        waiting.append((name, partial))

    (dx3, da2, db2, h4, d_ffn2_norm), _ = _ffn_bwd(
        x3, w["ffn2_norm"], dx4, a2, b2, full["ffn2_gate"], full["ffn2_up"], full["ffn2_down"])
    weight_grad("ffn2_down", s2, dx4, 0.5)
    weight_grad("ffn2_gate", da2, h4)
    weight_grad("ffn2_up", db2, h4)
    names, exchange = carried()
    (dx2, dqm, dkv, d_xattn_norm), arrived = _xattn_bwd(
        x2, w["xattn_norm"], dx3, qm, kv, full["w_q_mem"], full["w_o_mem"], exchange=exchange)
    parts.update(zip(names, arrived))
    weight_grad("w_o_mem", att, dx3)
    weight_grad("w_q_mem", hq, dqm)
    d_wkv_blocks, d_mem_norm = _memkv_bwd(memv, w["mem_norm"], dkv, full["w_kv_mem"])
    (from_sibling,) = _run_exchange(_pair_exchange([d_wkv_blocks]), "pair_exchange")
    waiting.append(("w_kv_mem", _pair_add(d_wkv_blocks, from_sibling, core)))
    weight_grad("w_out", ycat, dx2)
    names, exchange = carried()
    (dx1, dz, h2, d_mix_norm, d_lbp, d_gh, d_convw_t), arrived = _mix_bwd(
        x1, w["mix_norm"], dx2, z, o_raw, states, full["w_in"], w["lb_param"], w["hgrn_out_norm"], convw_t,
        full["w_out"], exchange=exchange)
    parts.update(zip(names, arrived))
    weight_grad("w_in", dz, h2)
    weight_grad("ffn1_down", s1, dx1, 0.5)
    (dx0, da1, db1, h1, d_ffn1_norm), _ = _ffn_bwd(
        x0, w["ffn1_norm"], dx1, a1, b1, full["ffn1_gate"], full["ffn1_up"], full["ffn1_down"])
    weight_grad("ffn1_gate", da1, h1)
    weight_grad("ffn1_up", db1, h1)

    small_parts = {
        "ffn1_norm": d_ffn1_norm, "mix_norm": d_mix_norm, "xattn_norm": d_xattn_norm, "mem_norm": d_mem_norm,
        "ffn2_norm": d_ffn2_norm, "final_norm": d_final, "lb_param": d_lbp, "hgrn_out_norm": d_gh,
        "conv_w": d_convw_t, "loss": loss_part,
    }
    names = [name for name, _ in waiting]
    arrived, total = _final_exchange([p for _, p in waiting], small_parts)
    parts.update(zip(names, arrived))

    g_out, d_out, m_out, v_out = {}, {}, {}, {}
    for name in LARGE:
        g_out[name], d_out[name], m_out[name], v_out[name] = _adamw_shard(parts[name], w[name], m[name], v[name])
    g_small = {}
    for name in SMALL:
        row, nrows, ncols = SMALL_LAYOUT[name]
        g_small[name] = total[row:row + nrows, 0:ncols]
    g_small["conv_w"] = lax.dynamic_slice_in_dim(g_small["conv_w"], me * conv_rows, conv_rows, axis=1)
    ds, ms, vs = _adamw_small(
        [g_small[k] for k in SMALL], [w[k] for k in SMALL], [m[k] for k in SMALL], [v[k] for k in SMALL])
    for i, name in enumerate(SMALL):
        g_out[name], d_out[name], m_out[name], v_out[name] = g_small[name], ds[i], ms[i], vs[i]

    def shaped(value, name):
        return (value.T if name in TRANSPOSED else value).reshape(given[name].shape)

    loss = total[SMALL_LAYOUT["loss"][0], 0]
    outs = [loss, dx0.reshape(x.shape)]
    for group in (g_out, d_out, m_out, v_out):
        outs += [shaped(group[name], name) for name in WEIGHTS]
    return tuple(outs)
```

```python
import jax
import jax.numpy as jnp
from jax import lax
from jax.experimental import pallas as pl
from jax.experimental.pallas import tpu as pltpu

F32 = jnp.float32
BF16 = jnp.bfloat16
MESH_IDS = pl.DeviceIdType.MESH

N_DEV = 8
EPS = 1e-6
HGRN_HEADS = 4
HGRN_DK = 128
HGRN_W = 512
CHUNK = 64
MEM_HEADS = 4
MEM_HD = 256
ADAM_LR = 0.001
ADAM_B1 = 0.9
ADAM_B2 = 0.999
ADAM_EPS = 1e-08
ADAM_WD = 0.01
ADAM_STEP = 10

TOKEN_TILE = 256
REDUCE_TILE = 1024
ADAMW_TILE_ELEMENTS = 256 * 1024
MXU_ROWS = 256
VMEM_LIMIT = 60 * 1024 * 1024
SMALL_ROWS = 16
NT = (((1,), (1,)), ((), ()))
TN = (((0,), (0,)), ((), ()))


def _params(sem=None):
    return pltpu.CompilerParams(dimension_semantics=sem, vmem_limit_bytes=VMEM_LIMIT)


def _dot(a, b, dims=None):
    if dims is None:
        return jnp.dot(a, b, preferred_element_type=F32)
    return lax.dot_general(a, b, dims, preferred_element_type=F32)


def _sigmoid(v):
    return 1.0 / (1.0 + jnp.exp(-v))


def _rms(x, g):
    r = lax.rsqrt(jnp.mean(x * x, axis=-1, keepdims=True) + EPS)
    xh = x * r
    return xh * g, xh, r


def _rms_bwd(dh, xh, r, g):
    dxh = dh * g
    return r * (dxh - xh * jnp.mean(dxh * xh, axis=-1, keepdims=True))


def _full(shape):
    return pl.BlockSpec(shape, lambda *_: (0,) * len(shape))


def _rows(tm, width):
    return pl.BlockSpec((tm, width), lambda i: (i, 0))


def _rows_rev(tm, width, n):
    return pl.BlockSpec((tm, width), lambda i: (n - 1 - i, 0))


def _accumulate(ref, first, value):
    @pl.when(first)
    def _():
        ref[...] = value

    @pl.when(jnp.logical_not(first))
    def _():
        ref[...] += value


class _Exchange:
    def __init__(self, operands, out_shapes, scratch, start, finish):
        self.operands, self.out_shapes, self.scratch = list(operands), list(out_shapes), list(scratch)
        self.start, self.finish = start, finish


def _call(body, *, name, grid, in_specs, out_specs, out_shape, args, scratch_shapes=(), exchange=None):
    semantics = ("arbitrary",) * len(grid)
    if exchange is None:
        out = pl.pallas_call(
            body, name=name, grid=grid, in_specs=in_specs, out_specs=out_specs, out_shape=out_shape,
            scratch_shapes=list(scratch_shapes), compiler_params=_params(semantics))(*args)
        return out, []
    hbm = pl.BlockSpec(memory_space=pltpu.HBM)
    n_in, n_out, n_scr = len(in_specs), len(out_specs), len(scratch_shapes)
    e_in, e_out = len(exchange.operands), len(exchange.out_shapes)

    def carried(*refs):
        ins, rest = refs[:n_in], refs[n_in:]
        e_ins, rest = rest[:e_in], rest[e_in:]
        outs, rest = rest[:n_out], rest[n_out:]
        e_outs, rest = rest[:e_out], rest[e_out:]
        scr, e_scr = rest[:n_scr], rest[n_scr:]
        first = last = None
        for axis, size in enumerate(grid):
            at_start, at_end = pl.program_id(axis) == 0, pl.program_id(axis) == size - 1
            first = at_start if first is None else jnp.logical_and(first, at_start)
            last = at_end if last is None else jnp.logical_and(last, at_end)

        @pl.when(first)
        def _():
            exchange.start(e_ins, e_outs, e_scr)

        body(*ins, *outs, *scr)

        @pl.when(last)
        def _():
            exchange.finish(e_ins, e_outs, e_scr)

    out = pl.pallas_call(
        carried, name=name, grid=grid, in_specs=list(in_specs) + [hbm] * e_in,
        out_specs=list(out_specs) + [hbm] * e_out, out_shape=list(out_shape) + exchange.out_shapes,
        scratch_shapes=list(scratch_shapes) + exchange.scratch,
        compiler_params=pltpu.CompilerParams(
            dimension_semantics=semantics, vmem_limit_bytes=VMEM_LIMIT, has_side_effects=True),
    )(*args, *exchange.operands)
    return out[:n_out], out[n_out:]


def _run_exchange(exchange, name):
    hbm = pl.BlockSpec(memory_space=pltpu.HBM)
    e_in, e_out = len(exchange.operands), len(exchange.out_shapes)

    def body(*refs):
        e_ins, e_outs, e_scr = refs[:e_in], refs[e_in:e_in + e_out], refs[e_in + e_out:]
        exchange.start(e_ins, e_outs, e_scr)
        exchange.finish(e_ins, e_outs, e_scr)

    return pl.pallas_call(
        body, name=name, in_specs=[hbm] * e_in, out_specs=[hbm] * e_out, out_shape=exchange.out_shapes,
        scratch_shapes=exchange.scratch, compiler_params=pltpu.CompilerParams(has_side_effects=True),
    )(*exchange.operands)


def _loss_head(xo, gf, tgt):
    d = xo.shape[1]
    y, xh, r = _rms(xo, gf)
    err = y - tgt
    dy = err * (1.0 / d)
    loss = 0.5 * jnp.sum(jnp.sum(err * err, axis=-1, keepdims=True) * (1.0 / d), axis=0, keepdims=True)
    return _rms_bwd(dy, xh, r, gf), loss, jnp.sum(dy * xh, axis=0, keepdims=True)


def _ffn_fwd(x, g, wg, wu, wd, exchange=None, head=None):
    t, d = x.shape
    f = wg.shape[0]
    tm = min(TOKEN_TILE, t)

    def body(x_ref, g_ref, wg_ref, wu_ref, wd_ref, *rest):
        xv = x_ref[...]
        h, _, _ = _rms(xv, g_ref[...])
        hb = h.astype(BF16)
        a = _dot(hb, wg_ref[...], NT)
        b = _dot(hb, wu_ref[...], NT)
        s = (a * _sigmoid(a) * b).astype(BF16)
        xo = xv + 0.5 * _dot(s, wd_ref[...])
        if head is None:
            xo_ref, a_ref, b_ref, s_ref = rest
            xo_ref[...] = xo
        else:
            gf_ref, tgt_ref, xo_ref, a_ref, b_ref, s_ref, loss_ref, dgf_ref = rest
            first = pl.program_id(0) == 0
            xo_ref[...], loss, dgf = _loss_head(xo, gf_ref[...], tgt_ref[...])
            _accumulate(loss_ref, first, jnp.broadcast_to(loss, (1, 128)))
            _accumulate(dgf_ref, first, dgf)
        a_ref[...] = a.astype(BF16)
        b_ref[...] = b.astype(BF16)
        s_ref[...] = s

    in_specs = [_rows(tm, d), _full((1, d)), _full((f, d)), _full((f, d)), _full((f, d))]
    out_specs = [_rows(tm, d), _rows(tm, f), _rows(tm, f), _rows(tm, f)]
    out_shape = [
        jax.ShapeDtypeStruct((t, d), F32),
        jax.ShapeDtypeStruct((t, f), BF16),
        jax.ShapeDtypeStruct((t, f), BF16),
        jax.ShapeDtypeStruct((t, f), BF16),
    ]
    args = (x, g, wg, wu, wd)
    if head is not None:
        in_specs += [_full((1, d)), _rows(tm, d)]
        out_specs += [_full((1, 128)), _full((1, d))]
        out_shape += [jax.ShapeDtypeStruct((1, 128), F32), jax.ShapeDtypeStruct((1, d), F32)]
        args += tuple(head)
    return _call(
        body, name="ffn_fwd", grid=(t // tm,), in_specs=in_specs, out_specs=out_specs, out_shape=out_shape,
        args=args, exchange=exchange)


def _ffn_bwd(x, g, dxo, a, b, wg, wu, wd, exchange=None):
    t, d = x.shape
    f = wg.shape[0]
    tm = min(TOKEN_TILE, t)

    def body(x_ref, g_ref, dxo_ref, a_ref, b_ref, wg_ref, wu_ref, wd_ref, dx_ref, da_ref, db_ref, h_ref, dg_ref):
        gv = g_ref[...]
        h, xh, r = _rms(x_ref[...], gv)
        dxo = dxo_ref[...]
        ds = _dot((0.5 * dxo).astype(BF16), wd_ref[...], NT)
        af = a_ref[...].astype(F32)
        bf = b_ref[...].astype(F32)
        sg = _sigmoid(af)
        da = (ds * bf * (sg * (1.0 + af * (1.0 - sg)))).astype(BF16)
        db = (ds * (af * sg)).astype(BF16)
        dh = _dot(da, wg_ref[...]) + _dot(db, wu_ref[...])
        dx_ref[...] = _rms_bwd(dh, xh, r, gv) + dxo
        da_ref[...] = da
        db_ref[...] = db
        h_ref[...] = h.astype(BF16)
        _accumulate(dg_ref, pl.program_id(0) == 0, jnp.sum(dh * xh, axis=0, keepdims=True))

    return _call(
        body,
        name="ffn_bwd",
        grid=(t // tm,),
        in_specs=[
            _rows(tm, d), _full((1, d)), _rows(tm, d), _rows(tm, f), _rows(tm, f),
            _full((f, d)), _full((f, d)), _full((f, d)),
        ],
        out_specs=[_rows(tm, d), _rows(tm, f), _rows(tm, f), _rows(tm, d), _full((1, d))],
        out_shape=[
            jax.ShapeDtypeStruct((t, d), F32),
            jax.ShapeDtypeStruct((t, f), BF16),
            jax.ShapeDtypeStruct((t, f), BF16),
            jax.ShapeDtypeStruct((t, d), BF16),
            jax.ShapeDtypeStruct((1, d), F32),
        ],
        args=(x, g, dxo, a, b, wg, wu, wd),
        exchange=exchange,
    )


def _weight_grad(a, b, scale=1.0, exchange=None):
    t, m = a.shape
    n = b.shape[1]
    chips = N_DEV // 2
    r = m // N_DEV
    tk = min(REDUCE_TILE, t)
    halves = 2
    nb = n // halves
    nk = t // tk

    def body(a_ref, b_ref, o_ref, acc, send_buf, recv_buf, send_sems, recv_sems):
        k, j = pl.program_id(0), pl.program_id(1)
        x, y, c, _ = _mesh_place()
        sibling, _ = _peer(x, y, c, 1)
        bv = b_ref[...]
        if scale != 1.0:
            bv = bv * scale
        bb = bv.astype(BF16)
        acc_half = acc.at[j]

        @pl.when(k == 0)
        def _():
            acc_half[...] = jnp.zeros_like(acc_half)

        for i in range(m // MXU_ROWS):
            rows = slice(i * MXU_ROWS, (i + 1) * MXU_ROWS)
            acc_half[rows, :] += _dot(a_ref[:, rows].astype(BF16), bb, TN)

        def to_sibling(half):
            return _remote(send_buf.at[half], recv_buf.at[half], send_sems.at[half], recv_sems.at[half], sibling)

        def owned_rows(q, core):
            return pl.ds(pl.multiple_of((2 * q + core) * r, 8), r)

        for half in range(halves):
            @pl.when(jnp.logical_and(k == nk - 1, j == half))
            def _():
                for q in range(chips):
                    send_buf[half, q] = acc[half, owned_rows(q, 1 - c), :].astype(BF16)
                to_sibling(half).start()

        @pl.when(jnp.logical_and(k == nk - 1, j == halves - 1))
        def _():
            for half in range(halves):
                to_sibling(half).wait_send()
                to_sibling(half).wait_recv()
                for q in range(chips):
                    o_ref[q, :, half * nb:(half + 1) * nb] = (
                        acc[half, owned_rows(q, c), :] + recv_buf[half, q].astype(F32)).astype(BF16)

    (partial,), arrived = _call(
        body,
        name="weight_grad",
        grid=(nk, halves),
        in_specs=[pl.BlockSpec((tk, m), lambda k, j: (k, 0)), pl.BlockSpec((tk, nb), lambda k, j: (k, j))],
        out_specs=[pl.BlockSpec((chips, r, n), lambda k, j: (0, 0, 0))],
        out_shape=[jax.ShapeDtypeStruct((chips, r, n), BF16)],
        scratch_shapes=[
            pltpu.VMEM((halves, m, nb), F32),
            pltpu.VMEM((halves, chips, r, nb), BF16), pltpu.VMEM((halves, chips, r, nb), BF16),
            pltpu.SemaphoreType.DMA((halves,)), pltpu.SemaphoreType.DMA((halves,)),
        ],
        args=(a, b),
        exchange=exchange,
    )
    return partial, arrived


def _chunk_cumsum(v, reverse=False):
    n, width = v.shape
    row = lax.broadcasted_iota(jnp.int32, (n, n), 0)
    col = lax.broadcasted_iota(jnp.int32, (n, n), 1)
    earlier = col >= row if reverse else col <= row
    tri = jnp.where(jnp.logical_and(row // CHUNK == col // CHUNK, earlier), 1.0, 0.0).astype(BF16)
    hi = v.astype(BF16)
    rest = v - hi.astype(F32)
    mid = rest.astype(BF16)
    low = (rest - mid.astype(F32)).astype(BF16)
    sums = _dot(tri, jnp.concatenate([hi, mid, low], axis=1))
    return sums[:, 0:width] + sums[:, width:2 * width] + sums[:, 2 * width:3 * width]


def _shift_rows(v, shift, edge):
    n = v.shape[0]
    row = lax.broadcasted_iota(jnp.int32, (n, 1), 0)
    out = pltpu.roll(v, shift % n, axis=0)
    if shift > 0:
        for j in range(shift):
            out = jnp.where(row == j, edge[8 - shift + j:8 - shift + j + 1, :], out)
    else:
        for j in range(-shift):
            out = jnp.where(row == n + shift + j, edge[j:j + 1, :], out)
    return out


def _gates(z, lbp):
    w = HGRN_W
    lb = _sigmoid(lbp[0:1, :] - lbp[1:2, :])
    zq = z[:, 0:w]
    sig = _sigmoid(z[:, w:2 * w])
    f = lb + (1.0 - lb) * sig
    sq = _sigmoid(zq)
    q = zq * sq * HGRN_DK ** -0.5
    return lb, sig, f, sq, q


def _decayed_operands(q, f, v, qh_buf, kh_buf, kbar_buf, v_buf, etot_buf):
    logf = jnp.log(f)
    bcum = _chunk_cumsum(logf)
    rest = _chunk_cumsum(logf, reverse=True) - logf
    eb, enb, erest = jnp.exp(bcum), jnp.exp(-bcum), jnp.exp(rest)
    kk = 1.0 - f
    qh_buf[...] = (q * eb).astype(BF16)
    kh_buf[...] = (kk * enb).astype(BF16)
    kbar_buf[...] = (kk * erest).astype(BF16)
    v_buf[...] = v.astype(BF16)
    etot_buf[...] = jnp.exp(bcum + rest)
    return eb, enb, erest


def _short_conv(u, edge, cw):
    return cw[0:1, :] * _shift_rows(u, 2, edge) + cw[1:2, :] * _shift_rows(u, 1, edge) + cw[2:3, :] * u


def _block_causal_mask(n):
    row = lax.broadcasted_iota(jnp.int32, (n, n), 0)
    col = lax.broadcasted_iota(jnp.int32, (n, n), 1)
    return jnp.logical_and(row // CHUNK == col // CHUNK, col <= row)


def _spread(v, chunk_of_row, nc):
    return jnp.concatenate([jnp.where(chunk_of_row == c, v, jnp.zeros_like(v)) for c in range(nc)], axis=1)


def _pick(r, chunk_of_row, nc):
    out = jnp.where(chunk_of_row == 0, r[:, 0:HGRN_DK], 0.0)
    for c in range(1, nc):
        out = out + jnp.where(chunk_of_row == c, r[:, c * HGRN_DK:(c + 1) * HGRN_DK], 0.0)
    return out


def _mix_fwd(x, g, w_in, lbp, gh, convw_t, w_out, exchange=None):
    t, d = x.shape
    zw = w_in.shape[0]
    w = HGRN_W
    tm = min(TOKEN_TILE, t)
    nc = tm // CHUNK
    n_chunks = t // CHUNK

    def body(x_ref, g_ref, win_ref, lbp_ref, gh_ref, cw_ref, wout_ref,
             xo_ref, z_ref, o_ref, st_ref, y_ref, state, ucarry, qh_buf, kh_buf, kbar_buf, v_buf, etot_buf):
        @pl.when(pl.program_id(0) == 0)
        def _():
            state[...] = jnp.zeros_like(state)
            ucarry[...] = jnp.zeros_like(ucarry)

        xv = x_ref[...]
        h, _, _ = _rms(xv, g_ref[...])
        z_ref[...] = _dot(h.astype(BF16), win_ref[...], NT)
        z = z_ref[...]
        _, _, f, _, q = _gates(z, lbp_ref[...])
        _decayed_operands(q, f, z[:, 2 * w:3 * w], qh_buf, kh_buf, kbar_buf, v_buf, etot_buf)
        mask = _block_causal_mask(tm)
        chunk_of_row = lax.broadcasted_iota(jnp.int32, (tm, 1), 0) // CHUNK
        for hd in range(HGRN_HEADS):
            cols = slice(hd * HGRN_DK, (hd + 1) * HGRN_DK)
            qh, kh, kbar, vb = qh_buf[:, cols], kh_buf[:, cols], kbar_buf[:, cols], v_buf[:, cols]
            scores = jnp.where(mask, _dot(qh, kh, NT), 0.0).astype(BF16)
            gains = _dot(_spread(vb, chunk_of_row, nc), kbar, TN)
            entering = []
            st = state[hd]
            for c in range(nc):
                entering.append(st)
                st_ref[c, hd] = st
                st = st * etot_buf[c * CHUNK:c * CHUNK + 1, cols] + gains[c * HGRN_DK:(c + 1) * HGRN_DK, :]
            state[hd] = st
            from_states = _dot(qh, jnp.concatenate(entering, axis=0).astype(BF16), NT)
            o_ref[:, cols] = _dot(scores, vb) + _pick(from_states, chunk_of_row, nc)
        ghv = gh_ref[...]
        for hd in range(HGRN_HEADS):
            cols = slice(hd * HGRN_DK, (hd + 1) * HGRN_DK)
            on, _, _ = _rms(o_ref[:, cols], ghv[:, cols])
            zg = z[:, 3 * w + hd * HGRN_DK:3 * w + (hd + 1) * HGRN_DK]
            y_ref[:, cols] = (on * (zg * _sigmoid(zg))).astype(BF16)
        u = z[:, 5 * w:6 * w] * z[:, 6 * w:7 * w]
        conv = _short_conv(u, ucarry[...], cw_ref[...])
        ucarry[...] = u[tm - 8:tm, :]
        y_ref[:, w:2 * w] = (z[:, 4 * w:5 * w] * conv).astype(BF16)
        xo_ref[...] = xv + _dot(y_ref[...], wout_ref[...])

    return _call(
        body,
        name="mix_fwd",
        grid=(t // tm,),
        in_specs=[
            _rows(tm, d), _full((1, d)), _full((zw, d)), _full((2, w)), _full((1, w)), _full((3, w)),
            _full((2 * w, d)),
        ],
        out_specs=[
            _rows(tm, d), _rows(tm, zw), _rows(tm, w),
            pl.BlockSpec((nc, HGRN_HEADS, HGRN_DK, HGRN_DK), lambda i: (i, 0, 0, 0)),
            _rows(tm, 2 * w),
        ],
        out_shape=[
            jax.ShapeDtypeStruct((t, d), F32),
            jax.ShapeDtypeStruct((t, zw), F32),
            jax.ShapeDtypeStruct((t, w), F32),
            jax.ShapeDtypeStruct((n_chunks, HGRN_HEADS, HGRN_DK, HGRN_DK), F32),
            jax.ShapeDtypeStruct((t, 2 * w), BF16),
        ],
        scratch_shapes=[
            pltpu.VMEM((HGRN_HEADS, HGRN_DK, HGRN_DK), F32), pltpu.VMEM((8, w), F32),
            pltpu.VMEM((tm, w), BF16), pltpu.VMEM((tm, w), BF16), pltpu.VMEM((tm, w), BF16),
            pltpu.VMEM((tm, w), BF16), pltpu.VMEM((tm, w), F32),
        ],
        args=(x, g, w_in, lbp, gh, convw_t, w_out),
        exchange=exchange,
    )


def _mix_bwd(x, g, dxo, z, o, states, w_in, lbp, gh, convw_t, w_out, exchange=None):
    t, d = x.shape
    zw = w_in.shape[0]
    w = HGRN_W
    tm = min(TOKEN_TILE, t)
    nc = tm // CHUNK
    n = t // tm

    def body(x_ref, g_ref, dxo_ref, z_ref, zprev_ref, o_ref, st_ref, win_ref, lbp_ref, gh_ref, cw_ref, wout_ref,
             dx_ref, dz_ref, h_ref, dg_ref, dlbp_ref, dgh_ref, dcw_ref,
             dstate, dcarry, do_buf, dqh_buf, dkh_buf, dkbar_buf, carry_buf,
             qh_buf, kh_buf, kbar_buf, v_buf, etot_buf):
        first = pl.program_id(0) == 0

        @pl.when(first)
        def _():
            dstate[...] = jnp.zeros_like(dstate)
            dcarry[...] = jnp.zeros_like(dcarry)

        gv = g_ref[...]
        h, xh, r = _rms(x_ref[...], gv)
        h_ref[...] = h.astype(BF16)
        dxo = dxo_ref[...]
        dy = _dot(dxo.astype(BF16), wout_ref[...], NT)
        z = z_ref[...]
        lb, sig, f, sq, q = _gates(z, lbp_ref[...])
        eb, enb, erest = _decayed_operands(q, f, z[:, 2 * w:3 * w], qh_buf, kh_buf, kbar_buf, v_buf, etot_buf)

        ghv = gh_ref[...]
        dgh_parts = []
        for hd in range(HGRN_HEADS):
            cols = slice(hd * HGRN_DK, (hd + 1) * HGRN_DK)
            gcols = slice(3 * w + hd * HGRN_DK, 3 * w + (hd + 1) * HGRN_DK)
            on, oh, rr = _rms(o_ref[:, cols], ghv[:, cols])
            zg = z[:, gcols]
            sgz = _sigmoid(zg)
            dyh = dy[:, cols]
            don = dyh * (zg * sgz)
            dz_ref[:, gcols] = (dyh * on * (sgz * (1.0 + zg * (1.0 - sgz)))).astype(BF16)
            dgh_parts.append(jnp.sum(don * oh, axis=0, keepdims=True))
            do_buf[:, cols] = _rms_bwd(don, oh, rr, ghv[:, cols]).astype(BF16)
        _accumulate(dgh_ref, first, jnp.concatenate(dgh_parts, axis=1))

        zb = z[:, 4 * w:5 * w]
        zc = z[:, 5 * w:6 * w]
        zu = z[:, 6 * w:7 * w]
        u = zc * zu
        cw = cw_ref[...]
        zp = zprev_ref[...]
        uprev = jnp.where(pl.program_id(0) == n - 1, 0.0, zp[:, 5 * w:6 * w] * zp[:, 6 * w:7 * w])
        dyc = dy[:, w:2 * w]
        dz_ref[:, 4 * w:5 * w] = (dyc * _short_conv(u, uprev, cw)).astype(BF16)
        dconv = dyc * zb
        edge = dcarry[...]
        dconv1 = _shift_rows(dconv, -1, edge)
        dconv2 = _shift_rows(dconv, -2, edge)
        dcarry[...] = dconv[0:8, :]
        du = cw[2:3, :] * dconv + cw[1:2, :] * dconv1 + cw[0:1, :] * dconv2
        dz_ref[:, 5 * w:6 * w] = (du * zu).astype(BF16)
        dz_ref[:, 6 * w:7 * w] = (du * zc).astype(BF16)
        _accumulate(dcw_ref, first, jnp.concatenate([
            jnp.sum(u * dconv2, axis=0, keepdims=True),
            jnp.sum(u * dconv1, axis=0, keepdims=True),
            jnp.sum(u * dconv, axis=0, keepdims=True)], axis=0))

        mask = _block_causal_mask(tm)
        chunk_of_row = lax.broadcasted_iota(jnp.int32, (tm, 1), 0) // CHUNK
        for hd in range(HGRN_HEADS):
            cols = slice(hd * HGRN_DK, (hd + 1) * HGRN_DK)
            qhb, khb, kbarb, vb = qh_buf[:, cols], kh_buf[:, cols], kbar_buf[:, cols], v_buf[:, cols]
            dob = do_buf[:, cols]
            scores = jnp.where(mask, _dot(qhb, khb, NT), 0.0).astype(BF16)
            dscores = jnp.where(mask, _dot(dob, vb, NT), 0.0).astype(BF16)
            gains = _dot(_spread(dob, chunk_of_row, nc), qhb, TN)
            entering = [st_ref[c, hd] for c in range(nc)]
            leaving = [None] * nc
            dst = dstate[hd]
            for c in reversed(range(nc)):
                elast = etot_buf[c * CHUNK:c * CHUNK + 1, cols]
                leaving[c] = dst
                carry_buf[c:c + 1, cols] = jnp.sum(dst * entering[c], axis=0, keepdims=True) * elast
                dst = dst * elast + gains[c * HGRN_DK:(c + 1) * HGRN_DK, :]
            dstate[hd] = dst
            dst_rows = jnp.concatenate(leaving, axis=0).astype(BF16)
            dst_lanes = jnp.concatenate(leaving, axis=1).astype(BF16)
            st_lanes = jnp.concatenate(entering, axis=1).astype(BF16)
            dv = _dot(scores, dob, TN) + _pick(_dot(kbarb, dst_rows, NT), chunk_of_row, nc)
            dz_ref[:, 2 * w + hd * HGRN_DK:2 * w + (hd + 1) * HGRN_DK] = dv.astype(BF16)
            dqh_buf[:, cols] = _dot(dscores, khb) + _pick(_dot(dob, st_lanes), chunk_of_row, nc)
            dkh_buf[:, cols] = _dot(dscores, qhb, TN)
            dkbar_buf[:, cols] = _pick(_dot(vb, dst_lanes), chunk_of_row, nc)

        dqh, dkh, dkbar = dqh_buf[...], dkh_buf[...], dkbar_buf[...]
        kbar_dkbar = kbar_buf[...].astype(F32) * dkbar
        db = qh_buf[...].astype(F32) * dqh - kh_buf[...].astype(F32) * dkh - kbar_dkbar
        through_last = jnp.concatenate([
            jnp.broadcast_to(
                jnp.sum(kbar_dkbar[c * CHUNK:(c + 1) * CHUNK], axis=0, keepdims=True) + carry_buf[c:c + 1, :],
                (CHUNK, w))
            for c in range(nc)], axis=0)
        dlogf = _chunk_cumsum(db, reverse=True) + through_last
        df = dlogf / f - (dkh * enb + dkbar * erest)
        zq = z[:, 0:w]
        dz_ref[:, 0:w] = (dqh * eb * HGRN_DK ** -0.5 * (sq * (1.0 + zq * (1.0 - sq)))).astype(BF16)
        dz_ref[:, w:2 * w] = (df * (1.0 - lb) * sig * (1.0 - sig)).astype(BF16)
        dlb = jnp.sum(df * (1.0 - sig), axis=0, keepdims=True) * lb * (1.0 - lb)
        _accumulate(dlbp_ref, first, jnp.concatenate([dlb, -dlb], axis=0))

        dh = _dot(dz_ref[...], win_ref[...])
        dx_ref[...] = _rms_bwd(dh, xh, r, gv) + dxo
        _accumulate(dg_ref, first, jnp.sum(dh * xh, axis=0, keepdims=True))

    return _call(
        body,
        name="mix_bwd",
        grid=(n,),
        in_specs=[
            _rows_rev(tm, d, n), _full((1, d)), _rows_rev(tm, d, n), _rows_rev(tm, zw, n),
            pl.BlockSpec((8, zw), lambda i: (jnp.maximum((n - 1 - i) * (tm // 8) - 1, 0), 0)),
            _rows_rev(tm, w, n),
            pl.BlockSpec((nc, HGRN_HEADS, HGRN_DK, HGRN_DK), lambda i: (n - 1 - i, 0, 0, 0)),
            _full((zw, d)), _full((2, w)), _full((1, w)), _full((3, w)), _full((2 * w, d)),
        ],
        out_specs=[
            _rows_rev(tm, d, n), _rows_rev(tm, zw, n), _rows_rev(tm, d, n),
            _full((1, d)), _full((2, w)), _full((1, w)), _full((3, w)),
        ],
        out_shape=[
            jax.ShapeDtypeStruct((t, d), F32),
            jax.ShapeDtypeStruct((t, zw), BF16),
            jax.ShapeDtypeStruct((t, d), BF16),
            jax.ShapeDtypeStruct((1, d), F32),
            jax.ShapeDtypeStruct((2, w), F32),
            jax.ShapeDtypeStruct((1, w), F32),
            jax.ShapeDtypeStruct((3, w), F32),
        ],
        scratch_shapes=[
            pltpu.VMEM((HGRN_HEADS, HGRN_DK, HGRN_DK), F32), pltpu.VMEM((8, w), F32),
            pltpu.VMEM((tm, w), BF16), pltpu.VMEM((tm, w), F32), pltpu.VMEM((tm, w), F32), pltpu.VMEM((tm, w), F32),
            pltpu.VMEM((8, w), F32),
            pltpu.VMEM((tm, w), BF16), pltpu.VMEM((tm, w), BF16), pltpu.VMEM((tm, w), BF16),
            pltpu.VMEM((tm, w), BF16), pltpu.VMEM((tm, w), F32),
        ],
        args=(x, g, dxo, z, z, o, states, w_in, lbp, gh, convw_t, w_out),
        exchange=exchange,
    )


def _memkv_fwd(mem, g, wkv):
    m, d = mem.shape
    nb, _, cb = wkv.shape

    def body(mem_ref, g_ref, wkv_ref, kv_ref):
        mn, _, _ = _rms(mem_ref[...], g_ref[...])
        mnb = mn.astype(BF16)
        for j in range(nb):
            kv_ref[:, j * cb:(j + 1) * cb] = _dot(mnb, wkv_ref[j]).astype(BF16)

    return pl.pallas_call(
        body,
        name="memkv_fwd",
        out_shape=jax.ShapeDtypeStruct((m, nb * cb), BF16),
        compiler_params=_params(),
    )(mem, g, wkv)


def _memkv_bwd(mem, g, dkv, wkv):
    m, d = mem.shape
    nb, _, cb = wkv.shape

    def body(mem_ref, g_ref, dkv_ref, wkv_ref, dw_ref, dg_ref):
        mn, xh, _ = _rms(mem_ref[...], g_ref[...])
        mnb = mn.astype(BF16)
        dmn = jnp.zeros((m, d), F32)
        for j in range(nb):
            dkvb = dkv_ref[:, j * cb:(j + 1) * cb].astype(BF16)
            dw_ref[j] = _dot(mnb, dkvb, TN).astype(BF16)
            dmn = dmn + _dot(dkvb, wkv_ref[j], NT)
        dg_ref[...] = jnp.sum(dmn * xh, axis=0, keepdims=True)

    return pl.pallas_call(
        body,
        name="memkv_bwd",
        out_shape=[jax.ShapeDtypeStruct((nb, d, cb), BF16), jax.ShapeDtypeStruct((1, d), F32)],
        compiler_params=_params(),
    )(mem, g, dkv, wkv)


def _softmax_rows(qm_h, k_h):
    sc = _dot(qm_h, k_h, NT) * MEM_HD ** -0.5
    e = jnp.exp(sc - jnp.max(sc, axis=-1, keepdims=True))
    return e / jnp.sum(e, axis=-1, keepdims=True)


def _xattn_fwd(x, g, wq, kv, wo, exchange=None):
    t, d = x.shape
    m = kv.shape[0]
    tm = min(TOKEN_TILE, t)

    def body(x_ref, g_ref, wq_ref, kv_ref, wo_ref, xo_ref, hq_ref, qm_ref, att_ref):
        xv = x_ref[...]
        h, _, _ = _rms(xv, g_ref[...])
        hq_ref[...] = h.astype(BF16)
        qm_ref[...] = _dot(hq_ref[...], wq_ref[...]).astype(BF16)
        for hd in range(MEM_HEADS):
            cols = slice(hd * MEM_HD, (hd + 1) * MEM_HD)
            p = _softmax_rows(qm_ref[:, cols], kv_ref[:, cols])
            att_ref[:, cols] = _dot(p.astype(BF16), kv_ref[:, d + hd * MEM_HD:d + (hd + 1) * MEM_HD]).astype(BF16)
        xo_ref[...] = xv + _dot(att_ref[...], wo_ref[...])

    return _call(
        body,
        name="xattn_fwd",
        grid=(t // tm,),
        in_specs=[_rows(tm, d), _full((1, d)), _full((d, d)), _full((m, 2 * d)), _full((d, d))],
        out_specs=[_rows(tm, d), _rows(tm, d), _rows(tm, d), _rows(tm, d)],
        out_shape=[
            jax.ShapeDtypeStruct((t, d), F32),
            jax.ShapeDtypeStruct((t, d), BF16),
            jax.ShapeDtypeStruct((t, d), BF16),
            jax.ShapeDtypeStruct((t, d), BF16),
        ],
        args=(x, g, wq, kv, wo),
        exchange=exchange,
    )


def _xattn_bwd(x, g, dxo, qm, kv, wq, wo, exchange=None):
    t, d = x.shape
    m = kv.shape[0]
    tm = min(TOKEN_TILE, t)

    def body(x_ref, g_ref, dxo_ref, qm_ref, kv_ref, wq_ref, wo_ref, dx_ref, dqm_ref, dkv_ref, dg_ref):
        first = pl.program_id(0) == 0

        @pl.when(first)
        def _():
            dkv_ref[...] = jnp.zeros_like(dkv_ref)

        gv = g_ref[...]
        _, xh, r = _rms(x_ref[...], gv)
        dxo = dxo_ref[...]
        datt = _dot(dxo.astype(BF16), wo_ref[...], NT).astype(BF16)
        for hd in range(MEM_HEADS):
            cols = slice(hd * MEM_HD, (hd + 1) * MEM_HD)
            vcols = slice(d + hd * MEM_HD, d + (hd + 1) * MEM_HD)
            qm_h = qm_ref[:, cols]
            p = _softmax_rows(qm_h, kv_ref[:, cols])
            datt_h = datt[:, cols]
            dp = _dot(datt_h, kv_ref[:, vcols], NT)
            dsc = (p * (dp - jnp.sum(p * dp, axis=-1, keepdims=True)) * MEM_HD ** -0.5).astype(BF16)
            dqm_ref[:, cols] = _dot(dsc, kv_ref[:, cols]).astype(BF16)
            dkv_ref[:, cols] += _dot(dsc, qm_h, TN)
            dkv_ref[:, vcols] += _dot(p.astype(BF16), datt_h, TN)
        dh = _dot(dqm_ref[...], wq_ref[...], NT)
        dx_ref[...] = _rms_bwd(dh, xh, r, gv) + dxo
        _accumulate(dg_ref, first, jnp.sum(dh * xh, axis=0, keepdims=True))

    return _call(
        body,
        name="xattn_bwd",
        grid=(t // tm,),
        in_specs=[
            _rows(tm, d), _full((1, d)), _rows(tm, d), _rows(tm, d), _full((m, 2 * d)), _full((d, d)), _full((d, d)),
        ],
        out_specs=[_rows(tm, d), _rows(tm, d), _full((m, 2 * d)), _full((1, d))],
        out_shape=[
            jax.ShapeDtypeStruct((t, d), F32),
            jax.ShapeDtypeStruct((t, d), BF16),
            jax.ShapeDtypeStruct((m, 2 * d), F32),
            jax.ShapeDtypeStruct((1, d), F32),
        ],
        args=(x, g, dxo, qm, kv, wq, wo),
        exchange=exchange,
    )


def _mesh_place():
    x, y, c = lax.axis_index("x"), lax.axis_index("y"), lax.axis_index("c")
    return x, y, c, 4 * x + 2 * y + c


def _peer(x, y, c, k):
    px = 1 - x if k & 4 else x
    py = 1 - y if k & 2 else y
    pc = 1 - c if k & 1 else c
    return (px, py, pc), 4 * px + 2 * py + pc


ICI_HOPS = (2, 4, 6)
N_HOPS = len(ICI_HOPS)


def _remote(src, dst, send_sem, recv_sem, peer):
    return pltpu.make_async_remote_copy(
        src_ref=src, dst_ref=dst, send_sem=send_sem, recv_sem=recv_sem, device_id=peer, device_id_type=MESH_IDS)


def _gather_exchange(shards):
    n = len(shards)

    def start(src, dst, sems):
        ici_send, ici_recv, pair_send, pair_recv, local = sems
        x, y, c, me = _mesh_place()
        sibling, _ = _peer(x, y, c, 1)
        for a in range(n):
            pltpu.make_async_copy(src[a], dst[a].at[me], local.at[a]).start()
            for j, k in enumerate(ICI_HOPS):
                peer, _ = _peer(x, y, c, k)
                _remote(src[a], dst[a].at[me], ici_send.at[a, j], ici_recv.at[a, j], peer).start()
            _remote(src[a], dst[a].at[me], pair_send.at[a, 0], pair_recv.at[a, 0], sibling).start()

    def finish(src, dst, sems):
        ici_send, ici_recv, pair_send, pair_recv, local = sems
        x, y, c, me = _mesh_place()
        sibling, sibling_index = _peer(x, y, c, 1)
        for a in range(n):
            for j, k in enumerate(ICI_HOPS):
                peer, peer_index = _peer(x, y, c, k)
                slot = dst[a].at[peer_index]
                _remote(src[a], slot, ici_send.at[a, j], ici_recv.at[a, j], peer).wait_recv()
                _remote(slot, slot, pair_send.at[a, 1 + j], pair_recv.at[a, 1 + j], sibling).start()
        for a in range(n):
            pltpu.make_async_copy(src[a], dst[a].at[me], local.at[a]).wait()
            for j, k in enumerate(ICI_HOPS):
                peer, _ = _peer(x, y, c, k)
                _remote(src[a], dst[a].at[me], ici_send.at[a, j], ici_recv.at[a, j], peer).wait_send()
            for j, k in enumerate((0,) + ICI_HOPS):
                _, from_sibling = _peer(x, y, c, k | 1)
                passed = _remote(src[a], dst[a].at[from_sibling], pair_send.at[a, j], pair_recv.at[a, j], sibling)
                passed.wait_send()
                passed.wait_recv()

    return _Exchange(
        shards,
        [jax.ShapeDtypeStruct((N_DEV,) + s.shape, s.dtype) for s in shards],
        [
            pltpu.SemaphoreType.DMA((n, N_HOPS)), pltpu.SemaphoreType.DMA((n, N_HOPS)),
            pltpu.SemaphoreType.DMA((n, N_HOPS + 1)), pltpu.SemaphoreType.DMA((n, N_HOPS + 1)),
            pltpu.SemaphoreType.DMA((n,)),
        ],
        start, finish)


def _pair_exchange(blocks):
    n = len(blocks)
    chips = N_DEV // 2

    def copies(src, dst, sems):
        send, recv = sems
        x, y, c, _ = _mesh_place()
        sibling, _ = _peer(x, y, c, 1)
        return [_remote(src[a].at[2 * q + (1 - c)], dst[a].at[q], send.at[a, q], recv.at[a, q], sibling)
                for a in range(n) for q in range(chips)]

    def start(src, dst, sems):
        for cp in copies(src, dst, sems):
            cp.start()

    def finish(src, dst, sems):
        for cp in copies(src, dst, sems):
            cp.wait_send()
            cp.wait_recv()

    return _Exchange(
        blocks,
        [jax.ShapeDtypeStruct((chips,) + b.shape[1:], b.dtype) for b in blocks],
        [pltpu.SemaphoreType.DMA((n, chips)), pltpu.SemaphoreType.DMA((n, chips))],
        start, finish)


def _pair_add(blocks, received, core):
    _, r, c = blocks.shape
    chips = N_DEV // 2
    tr = r
    while tr > 512:
        tr //= 2

    def body(core_ref, mine_ref, got_ref, o_ref):
        o_ref[...] = (mine_ref[...].astype(F32) + got_ref[...].astype(F32)).astype(BF16)

    return pl.pallas_call(
        body,
        name="pair_add",
        grid_spec=pltpu.PrefetchScalarGridSpec(
            num_scalar_prefetch=1,
            grid=(chips, r // tr),
            in_specs=[
                pl.BlockSpec((None, None, tr, c), lambda q, i, core_ref: (q, core_ref[0], i, 0)),
                pl.BlockSpec((None, tr, c), lambda q, i, core_ref: (q, i, 0)),
            ],
            out_specs=pl.BlockSpec((None, tr, c), lambda q, i, core_ref: (q, i, 0)),
        ),
        out_shape=jax.ShapeDtypeStruct((chips, r, c), BF16),
        compiler_params=_params(("parallel", "parallel")),
    )(core, blocks.reshape(chips, 2, r, c), received)


def _scatter_copies(src, dst, sems, n, arrivals=False):
    send, recv, local = sems
    x, y, c, _ = _mesh_place()
    chip = 2 * x + y
    if arrivals is None:
        return [pltpu.make_async_copy(src[a].at[chip], dst[a].at[chip], local.at[a]) for a in range(n)]
    copies = []
    for a in range(n):
        for j, k in enumerate(ICI_HOPS):
            peer, _ = _peer(x, y, c, k)
            peer_chip = 2 * peer[0] + peer[1]
            slot = dst[a].at[peer_chip if arrivals else chip]
            copies.append(_remote(src[a].at[peer_chip], slot, send.at[a, j], recv.at[a, j], peer))
    return copies


def _scatter_start(src, dst, sems, n):
    for cp in _scatter_copies(src, dst, sems, n, arrivals=None) + _scatter_copies(src, dst, sems, n):
        cp.start()


def _scatter_finish(src, dst, sems, n):
    for cp in _scatter_copies(src, dst, sems, n, arrivals=None):
        cp.wait()
    for cp in _scatter_copies(src, dst, sems, n):
        cp.wait_send()
    for cp in _scatter_copies(src, dst, sems, n, arrivals=True):
        cp.wait_recv()


def _scatter_scratch(n):
    return [pltpu.SemaphoreType.DMA((n, N_HOPS)), pltpu.SemaphoreType.DMA((n, N_HOPS)), pltpu.SemaphoreType.DMA((n,))]


def _scatter_exchange(partials):
    n = len(partials)
    return _Exchange(
        partials, [jax.ShapeDtypeStruct(p.shape, p.dtype) for p in partials], _scatter_scratch(n),
        lambda src, dst, sems: _scatter_start(src, dst, sems, n),
        lambda src, dst, sems: _scatter_finish(src, dst, sems, n))


SMALL_LAYOUT = {
    "ffn1_norm": (0, 1, 1024), "mix_norm": (1, 1, 1024), "xattn_norm": (2, 1, 1024), "mem_norm": (3, 1, 1024),
    "ffn2_norm": (4, 1, 1024), "final_norm": (5, 1, 1024), "lb_param": (6, 2, 512), "hgrn_out_norm": (8, 1, 512),
    "conv_w": (9, 3, 512), "loss": (12, 1, 128),
}


def _final_exchange(partials, small):
    n = len(partials)
    names = list(small)
    width = 1024

    def body(*refs):
        src = refs[:n]
        pieces = refs[n:n + len(names)]
        dst = refs[n + len(names):2 * n + len(names)]
        total_ref = refs[2 * n + len(names)]
        pack, gathered, small_send, small_recv = refs[2 * n + len(names) + 1:2 * n + len(names) + 5]
        sems = refs[2 * n + len(names) + 5:]
        x, y, c, me = _mesh_place()
        pack[...] = jnp.zeros_like(pack)
        for name, piece in zip(names, pieces):
            row, nrows, ncols = SMALL_LAYOUT[name]
            pack[row:row + nrows, 0:ncols] = piece[...]
        for k in range(1, N_DEV):
            peer, _ = _peer(x, y, c, k)
            _remote(pack, gathered.at[me], small_send.at[k - 1], small_recv.at[k - 1], peer).start()
        _scatter_start(src, dst, sems, n)
        gathered[me] = pack[...]
        for k in range(1, N_DEV):
            peer, peer_index = _peer(x, y, c, k)
            landed = _remote(pack, gathered.at[peer_index], small_send.at[k - 1], small_recv.at[k - 1], peer)
            landed.wait_send()
            landed.wait_recv()
        total = gathered[0]
        for j in range(1, N_DEV):
            total = total + gathered[j]
        total_ref[...] = total
        _scatter_finish(src, dst, sems, n)

    hbm = pl.BlockSpec(memory_space=pltpu.HBM)
    vmem = pl.BlockSpec(memory_space=pltpu.VMEM)
    out = pl.pallas_call(
        body,
        name="final_exchange",
        in_specs=[hbm] * n + [vmem] * len(names),
        out_specs=[hbm] * n + [vmem],
        out_shape=[jax.ShapeDtypeStruct(p.shape, p.dtype) for p in partials]
        + [jax.ShapeDtypeStruct((SMALL_ROWS, width), F32)],
        scratch_shapes=[
            pltpu.VMEM((SMALL_ROWS, width), F32), pltpu.VMEM((N_DEV, SMALL_ROWS, width), F32),
            pltpu.SemaphoreType.DMA((N_DEV - 1,)), pltpu.SemaphoreType.DMA((N_DEV - 1,)),
        ] + _scatter_scratch(n),
        compiler_params=pltpu.CompilerParams(has_side_effects=True),
    )(*partials, *[small[k] for k in names])
    return out[:n], out[n]


def _adamw_math(w, g, m, v):
    m = ADAM_B1 * m + (1.0 - ADAM_B1) * g
    v = ADAM_B2 * v + (1.0 - ADAM_B2) * (g * g)
    m_hat = m / (1.0 - ADAM_B1 ** ADAM_STEP)
    v_hat = v / (1.0 - ADAM_B2 ** ADAM_STEP)
    delta = -ADAM_LR * (m_hat / (jnp.sqrt(v_hat) + ADAM_EPS) + ADAM_WD * w)
    return delta, m, v


def _adamw_shard(parts, w, m, v):
    r, c = w.shape
    n_parts = parts.shape[0]
    tr = max(rows for rows in range(16, r + 1, 16) if r % rows == 0 and rows * c <= ADAMW_TILE_ELEMENTS)

    def body(p_ref, w_ref, m_ref, v_ref, g_ref, d_ref, mo_ref, vo_ref):
        g = p_ref[0].astype(F32)
        for j in range(1, n_parts):
            g = g + p_ref[j].astype(F32)
        delta, mn, vn = _adamw_math(w_ref[...], g, m_ref[...], v_ref[...])
        g_ref[...] = g
        d_ref[...] = delta
        mo_ref[...] = mn
        vo_ref[...] = vn

    tile = pl.BlockSpec((tr, c), lambda i: (i, 0))
    return pl.pallas_call(
        body,
        name="adamw_shard",
        grid=(r // tr,),
        in_specs=[pl.BlockSpec((n_parts, tr, c), lambda i: (0, i, 0)), tile, tile, tile],
        out_specs=[tile] * 4,
        out_shape=[jax.ShapeDtypeStruct((r, c), F32)] * 4,
        compiler_params=_params(("parallel",)),
    )(parts, w, m, v)


def _adamw_small(gs, ws, ms, vs):
    n = len(gs)

    def body(*refs):
        g_refs, w_refs, m_refs, v_refs = refs[:n], refs[n:2 * n], refs[2 * n:3 * n], refs[3 * n:4 * n]
        d_out, m_out, v_out = refs[4 * n:5 * n], refs[5 * n:6 * n], refs[6 * n:7 * n]
        for i in range(n):
            delta, mn, vn = _adamw_math(w_refs[i][...], g_refs[i][...], m_refs[i][...], v_refs[i][...])
            d_out[i][...] = delta
            m_out[i][...] = mn
            v_out[i][...] = vn

    shapes = [jax.ShapeDtypeStruct(w.shape, F32) for w in ws]
    out = pl.pallas_call(
        body,
        name="adamw_small",
        out_shape=shapes * 3,
        compiler_params=_params(),
    )(*gs, *ws, *ms, *vs)
    return out[:n], out[n:2 * n], out[2 * n:]


TRANSPOSED = ("ffn1_gate", "ffn1_up", "w_in", "ffn2_gate", "ffn2_up", "conv_w")
GROUP_FFN1 = ("ffn1_gate", "ffn1_up", "ffn1_down")
GROUP_MIX = ("w_in", "w_out")
GROUP_XATTN = ("w_q_mem", "w_kv_mem", "w_o_mem")
GROUP_FFN2 = ("ffn2_gate", "ffn2_up", "ffn2_down")
LARGE = GROUP_FFN1 + GROUP_MIX + GROUP_XATTN + GROUP_FFN2
SMALL = ("ffn1_norm", "mix_norm", "lb_param", "hgrn_out_norm", "conv_w", "xattn_norm", "mem_norm", "ffn2_norm",
         "final_norm")
WEIGHTS = ("ffn1_norm", "ffn1_gate", "ffn1_up", "ffn1_down", "mix_norm", "w_in", "lb_param", "hgrn_out_norm",
           "conv_w", "w_out", "xattn_norm", "mem_norm", "w_q_mem", "w_kv_mem", "w_o_mem", "ffn2_norm", "ffn2_gate",
           "ffn2_up", "ffn2_down", "final_norm")


def kernel(x, mem, ffn1_norm, ffn1_gate, ffn1_up, ffn1_down, mix_norm, w_in, lb_param, hgrn_out_norm, conv_w, w_out, xattn_norm, mem_norm, w_q_mem, w_kv_mem, w_o_mem, ffn2_norm, ffn2_gate, ffn2_up, ffn2_down, final_norm, loss_target, m_ffn1_norm, m_ffn1_gate, m_ffn1_up, m_ffn1_down, m_mix_norm, m_w_in, m_lb_param, m_hgrn_out_norm, m_conv_w, m_w_out, m_xattn_norm, m_mem_norm, m_w_q_mem, m_w_kv_mem, m_w_o_mem, m_ffn2_norm, m_ffn2_gate, m_ffn2_up, m_ffn2_down, m_final_norm, v_ffn1_norm, v_ffn1_gate, v_ffn1_up, v_ffn1_down, v_mix_norm, v_w_in, v_lb_param, v_hgrn_out_norm, v_conv_w, v_w_out, v_xattn_norm, v_mem_norm, v_w_q_mem, v_w_kv_mem, v_w_o_mem, v_ffn2_norm, v_ffn2_gate, v_ffn2_up, v_ffn2_down, v_final_norm):
    given = dict(locals())
    me = 4 * lax.axis_index("x") + 2 * lax.axis_index("y") + lax.axis_index("c")
    x0, memv, target = x[0], mem[0], loss_target[0]

    def shard(prefix, name):
        v = given[prefix + name]
        if v.ndim == 1:
            return v.reshape(1, -1)
        if v.ndim == 2:
            return v
        return v[0].T if name in TRANSPOSED else v[0]

    w = {name: shard("", name) for name in WEIGHTS}
    m = {name: shard("m_", name) for name in WEIGHTS}
    v = {name: shard("v_", name) for name in WEIGHTS}

    conv_taps, conv_rows = w["conv_w"].shape
    conv_tile = jnp.pad(w["conv_w"], ((0, 8 - conv_taps), (0, 128 - conv_rows)))
    wire = {name: w[name].astype(BF16) for name in LARGE}
    full = {}

    def landed(names, gathered):
        for name, blocks in zip(names, gathered):
            _, r, c = blocks.shape
            full[name] = blocks if name == "w_kv_mem" else blocks.reshape(N_DEV * r, c)

    landed(GROUP_FFN1, _run_exchange(_gather_exchange([wire[k] for k in GROUP_FFN1]), "gather_first"))

    riders = (GROUP_MIX + ("w_kv_mem", "w_q_mem"), ("w_o_mem", "ffn2_gate", "ffn2_up"), ("ffn2_down",))
    (x1, a1, b1, s1), gathered = _ffn_fwd(
        x0, w["ffn1_norm"], full["ffn1_gate"], full["ffn1_up"], full["ffn1_down"],
        exchange=_gather_exchange([wire[k] for k in riders[0]] + [conv_tile]))
    landed(riders[0], gathered)
    convw_t = gathered[-1][:, :conv_taps, :conv_rows].transpose(1, 0, 2).reshape(conv_taps, N_DEV * conv_rows)
    (x2, z, o_raw, states, ycat), gathered = _mix_fwd(
        x1, w["mix_norm"], full["w_in"], w["lb_param"], w["hgrn_out_norm"], convw_t, full["w_out"],
        exchange=_gather_exchange([wire[k] for k in riders[1]]))
    landed(riders[1], gathered)
    kv = _memkv_fwd(memv, w["mem_norm"], full["w_kv_mem"])
    (x3, hq, qm, att), gathered = _xattn_fwd(
        x2, w["xattn_norm"], full["w_q_mem"], kv, full["w_o_mem"],
        exchange=_gather_exchange([wire[k] for k in riders[2]]))
    landed(riders[2], gathered)
    (dx4, a2, b2, s2, loss_part, d_final), _ = _ffn_fwd(
        x3, w["ffn2_norm"], full["ffn2_gate"], full["ffn2_up"], full["ffn2_down"], head=(w["final_norm"], target))

    core = lax.axis_index("c").astype(jnp.int32).reshape(1)
    parts = {}
    waiting = []

    def carried():
        names = [name for name, _ in waiting]
        exchange = _scatter_exchange([p for _, p in waiting]) if waiting else None
        del waiting[:]
        return names, exchange

    def weight_grad(name, a, b, scale=1.0):
        names, exchange = carried()
        partial, arrived = _weight_grad(a, b, scale, exchange=exchange)
        parts.update(zip(names, arrived))
        waiting.append((name, partial))

    (dx3, da2, db2, h4, d_ffn2_norm), _ = _ffn_bwd(
        x3, w["ffn2_norm"], dx4, a2, b2, full["ffn2_gate"], full["ffn2_up"], full["ffn2_down"])
    weight_grad("ffn2_down", s2, dx4, 0.5)
    weight_grad("ffn2_gate", da2, h4)
    weight_grad("ffn2_up", db2, h4)
    names, exchange = carried()
    (dx2, dqm, dkv, d_xattn_norm), arrived = _xattn_bwd(
        x2, w["xattn_norm"], dx3, qm, kv, full["w_q_mem"], full["w_o_mem"], exchange=exchange)
    parts.update(zip(names, arrived))
    weight_grad("w_o_mem", att, dx3)
    weight_grad("w_q_mem", hq, dqm)
    d_wkv_blocks, d_mem_norm = _memkv_bwd(memv, w["mem_norm"], dkv, full["w_kv_mem"])
    (from_sibling,) = _run_exchange(_pair_exchange([d_wkv_blocks]), "pair_exchange")
    waiting.append(("w_kv_mem", _pair_add(d_wkv_blocks, from_sibling, core)))
    weight_grad("w_out", ycat, dx2)
    names, exchange = carried()
    (dx1, dz, h2, d_mix_norm, d_lbp, d_gh, d_convw_t), arrived = _mix_bwd(
        x1, w["mix_norm"], dx2, z, o_raw, states, full["w_in"], w["lb_param"], w["hgrn_out_norm"], convw_t,
        full["w_out"], exchange=exchange)
    parts.update(zip(names, arrived))
    weight_grad("w_in", dz, h2)
    weight_grad("ffn1_down", s1, dx1, 0.5)
    (dx0, da1, db1, h1, d_ffn1_norm), _ = _ffn_bwd(
        x0, w["ffn1_norm"], dx1, a1, b1, full["ffn1_gate"], full["ffn1_up"], full["ffn1_down"])
    weight_grad("ffn1_gate", da1, h1)
    weight_grad("ffn1_up", db1, h1)

    small_parts = {
        "ffn1_norm": d_ffn1_norm, "mix_norm": d_mix_norm, "xattn_norm": d_xattn_norm, "mem_norm": d_mem_norm,
        "ffn2_norm": d_ffn2_norm, "final_norm": d_final, "lb_param": d_lbp, "hgrn_out_norm": d_gh,
        "conv_w": d_convw_t, "loss": loss_part,
    }
    names = [name for name, _ in waiting]
    arrived, total = _final_exchange([p for _, p in waiting], small_parts)
    parts.update(zip(names, arrived))

    g_out, d_out, m_out, v_out = {}, {}, {}, {}
    for name in LARGE:
        g_out[name], d_out[name], m_out[name], v_out[name] = _adamw_shard(parts[name], w[name], m[name], v[name])
    g_small = {}
    for name in SMALL:
        row, nrows, ncols = SMALL_LAYOUT[name]
        g_small[name] = total[row:row + nrows, 0:ncols]
    g_small["conv_w"] = lax.dynamic_slice_in_dim(g_small["conv_w"], me * conv_rows, conv_rows, axis=1)
    ds, ms, vs = _adamw_small(
        [g_small[k] for k in SMALL], [w[k] for k in SMALL], [m[k] for k in SMALL], [v[k] for k in SMALL])
    for i, name in enumerate(SMALL):
        g_out[name], d_out[name], m_out[name], v_out[name] = g_small[name], ds[i], ms[i], vs[i]

    def shaped(value, name):
        return (value.T if name in TRANSPOSED else value).reshape(given[name].shape)

    loss = total[SMALL_LAYOUT["loss"][0], 0]
    outs = [loss, dx0.reshape(x.shape)]
    for group in (g_out, d_out, m_out, v_out):
        outs += [shaped(group[name], name) for name in WEIGHTS]
    return tuple(outs)
```

```python
import jax
import jax.numpy as jnp
from jax import lax
from jax.experimental import pallas as pl
from jax.experimental.pallas import tpu as pltpu

F32 = jnp.float32
BF16 = jnp.bfloat16
MESH_IDS = pl.DeviceIdType.MESH

N_DEV = 8
EPS = 1e-6
HGRN_HEADS = 4
HGRN_DK = 128
HGRN_W = 512
CHUNK = 64
MEM_HEADS = 4
MEM_HD = 256
ADAM_LR = 0.001
ADAM_B1 = 0.9
ADAM_B2 = 0.999
ADAM_EPS = 1e-08
ADAM_WD = 0.01
ADAM_STEP = 10

TOKEN_TILE = 256
REDUCE_TILE = 1024
ADAMW_TILE_ELEMENTS = 256 * 1024
MIDDLE_EIGHTHS = 5
MXU_ROWS = 256
VMEM_LIMIT = 60 * 1024 * 1024
SMALL_ROWS = 16
NT = (((1,), (1,)), ((), ()))
TN = (((0,), (0,)), ((), ()))


def _params(sem=None):
    return pltpu.CompilerParams(dimension_semantics=sem, vmem_limit_bytes=VMEM_LIMIT)


def _dot(a, b, dims=None):
    if dims is None:
        return jnp.dot(a, b, preferred_element_type=F32)
    return lax.dot_general(a, b, dims, preferred_element_type=F32)


def _sigmoid(v):
    return 1.0 / (1.0 + jnp.exp(-v))


def _rms(x, g):
    r = lax.rsqrt(jnp.mean(x * x, axis=-1, keepdims=True) + EPS)
    xh = x * r
    return xh * g, xh, r


def _rms_bwd(dh, xh, r, g):
    dxh = dh * g
    return r * (dxh - xh * jnp.mean(dxh * xh, axis=-1, keepdims=True))


def _full(shape):
    return pl.BlockSpec(shape, lambda *_: (0,) * len(shape))


def _rows(tm, width):
    return pl.BlockSpec((tm, width), lambda i: (i, 0))


def _rows_rev(tm, width, n):
    return pl.BlockSpec((tm, width), lambda i: (n - 1 - i, 0))


def _accumulate(ref, first, value):
    @pl.when(first)
    def _():
        ref[...] = value

    @pl.when(jnp.logical_not(first))
    def _():
        ref[...] += value


class _Exchange:
    def __init__(self, operands, out_shapes, scratch, start, finish, middle=None):
        self.operands, self.out_shapes, self.scratch = list(operands), list(out_shapes), list(scratch)
        self.start, self.middle, self.finish = start, middle, finish


def _call(body, *, name, grid, in_specs, out_specs, out_shape, args, scratch_shapes=(), exchange=None):
    semantics = ("arbitrary",) * len(grid)
    if exchange is None:
        out = pl.pallas_call(
            body, name=name, grid=grid, in_specs=in_specs, out_specs=out_specs, out_shape=out_shape,
            scratch_shapes=list(scratch_shapes), compiler_params=_params(semantics))(*args)
        return out, []
    hbm = pl.BlockSpec(memory_space=pltpu.HBM)
    n_in, n_out, n_scr = len(in_specs), len(out_specs), len(scratch_shapes)
    e_in, e_out = len(exchange.operands), len(exchange.out_shapes)

    def carried(*refs):
        ins, rest = refs[:n_in], refs[n_in:]
        e_ins, rest = rest[:e_in], rest[e_in:]
        outs, rest = rest[:n_out], rest[n_out:]
        e_outs, rest = rest[:e_out], rest[e_out:]
        scr, e_scr = rest[:n_scr], rest[n_scr:]
        first = last = None
        for axis, size in enumerate(grid):
            at_start, at_end = pl.program_id(axis) == 0, pl.program_id(axis) == size - 1
            first = at_start if first is None else jnp.logical_and(first, at_start)
            last = at_end if last is None else jnp.logical_and(last, at_end)

        @pl.when(first)
        def _():
            exchange.start(e_ins, e_outs, e_scr)

        body(*ins, *outs, *scr)

        if exchange.middle is not None:
            assert len(grid) == 1

            @pl.when(pl.program_id(0) == (grid[0] * MIDDLE_EIGHTHS) // 8)
            def _():
                exchange.middle(e_ins, e_outs, e_scr)

        @pl.when(last)
        def _():
            exchange.finish(e_ins, e_outs, e_scr)

    out = pl.pallas_call(
        carried, name=name, grid=grid, in_specs=list(in_specs) + [hbm] * e_in,
        out_specs=list(out_specs) + [hbm] * e_out, out_shape=list(out_shape) + exchange.out_shapes,
        scratch_shapes=list(scratch_shapes) + exchange.scratch,
        compiler_params=pltpu.CompilerParams(
            dimension_semantics=semantics, vmem_limit_bytes=VMEM_LIMIT, has_side_effects=True),
    )(*args, *exchange.operands)
    return out[:n_out], out[n_out:]


def _run_exchange(exchange, name):
    hbm = pl.BlockSpec(memory_space=pltpu.HBM)
    e_in, e_out = len(exchange.operands), len(exchange.out_shapes)

    def body(*refs):
        e_ins, e_outs, e_scr = refs[:e_in], refs[e_in:e_in + e_out], refs[e_in + e_out:]
        exchange.start(e_ins, e_outs, e_scr)
        if exchange.middle is not None:
            exchange.middle(e_ins, e_outs, e_scr)
        exchange.finish(e_ins, e_outs, e_scr)

    return pl.pallas_call(
        body, name=name, in_specs=[hbm] * e_in, out_specs=[hbm] * e_out, out_shape=exchange.out_shapes,
        scratch_shapes=exchange.scratch, compiler_params=pltpu.CompilerParams(has_side_effects=True),
    )(*exchange.operands)


def _loss_head(xo, gf, tgt):
    d = xo.shape[1]
    y, xh, r = _rms(xo, gf)
    err = y - tgt
    dy = err * (1.0 / d)
    loss = 0.5 * jnp.sum(jnp.sum(err * err, axis=-1, keepdims=True) * (1.0 / d), axis=0, keepdims=True)
    return _rms_bwd(dy, xh, r, gf), loss, jnp.sum(dy * xh, axis=0, keepdims=True)


def _ffn_fwd(x, g, wg, wu, wd, exchange=None, head=None):
    t, d = x.shape
    f = wg.shape[0]
    tm = min(TOKEN_TILE, t)

    def body(x_ref, g_ref, wg_ref, wu_ref, wd_ref, *rest):
        xv = x_ref[...]
        h, _, _ = _rms(xv, g_ref[...])
        hb = h.astype(BF16)
        a = _dot(hb, wg_ref[...], NT)
        b = _dot(hb, wu_ref[...], NT)
        s = (a * _sigmoid(a) * b).astype(BF16)
        xo = xv + 0.5 * _dot(s, wd_ref[...])
        if head is None:
            xo_ref, a_ref, b_ref, s_ref = rest
            xo_ref[...] = xo
        else:
            gf_ref, tgt_ref, xo_ref, a_ref, b_ref, s_ref, loss_ref, dgf_ref = rest
            first = pl.program_id(0) == 0
            xo_ref[...], loss, dgf = _loss_head(xo, gf_ref[...], tgt_ref[...])
            _accumulate(loss_ref, first, jnp.broadcast_to(loss, (1, 128)))
            _accumulate(dgf_ref, first, dgf)
        a_ref[...] = a.astype(BF16)
        b_ref[...] = b.astype(BF16)
        s_ref[...] = s

    in_specs = [_rows(tm, d), _full((1, d)), _full((f, d)), _full((f, d)), _full((f, d))]
    out_specs = [_rows(tm, d), _rows(tm, f), _rows(tm, f), _rows(tm, f)]
    out_shape = [
        jax.ShapeDtypeStruct((t, d), F32),
        jax.ShapeDtypeStruct((t, f), BF16),
        jax.ShapeDtypeStruct((t, f), BF16),
        jax.ShapeDtypeStruct((t, f), BF16),
    ]
    args = (x, g, wg, wu, wd)
    if head is not None:
        in_specs += [_full((1, d)), _rows(tm, d)]
        out_specs += [_full((1, 128)), _full((1, d))]
        out_shape += [jax.ShapeDtypeStruct((1, 128), F32), jax.ShapeDtypeStruct((1, d), F32)]
        args += tuple(head)
    return _call(
        body, name="ffn_fwd", grid=(t // tm,), in_specs=in_specs, out_specs=out_specs, out_shape=out_shape,
        args=args, exchange=exchange)


def _ffn_bwd(x, g, dxo, a, b, wg, wu, wd, exchange=None):
    t, d = x.shape
    f = wg.shape[0]
    tm = min(TOKEN_TILE, t)

    def body(x_ref, g_ref, dxo_ref, a_ref, b_ref, wg_ref, wu_ref, wd_ref, dx_ref, da_ref, db_ref, h_ref, dg_ref):
        gv = g_ref[...]
        h, xh, r = _rms(x_ref[...], gv)
        dxo = dxo_ref[...]
        ds = _dot((0.5 * dxo).astype(BF16), wd_ref[...], NT)
        af = a_ref[...].astype(F32)
        bf = b_ref[...].astype(F32)
        sg = _sigmoid(af)
        da = (ds * bf * (sg * (1.0 + af * (1.0 - sg)))).astype(BF16)
        db = (ds * (af * sg)).astype(BF16)
        dh = _dot(da, wg_ref[...]) + _dot(db, wu_ref[...])
        dx_ref[...] = _rms_bwd(dh, xh, r, gv) + dxo
        da_ref[...] = da
        db_ref[...] = db
        h_ref[...] = h.astype(BF16)
        _accumulate(dg_ref, pl.program_id(0) == 0, jnp.sum(dh * xh, axis=0, keepdims=True))

    return _call(
        body,
        name="ffn_bwd",
        grid=(t // tm,),
        in_specs=[
            _rows(tm, d), _full((1, d)), _rows(tm, d), _rows(tm, f), _rows(tm, f),
            _full((f, d)), _full((f, d)), _full((f, d)),
        ],
        out_specs=[_rows(tm, d), _rows(tm, f), _rows(tm, f), _rows(tm, d), _full((1, d))],
        out_shape=[
            jax.ShapeDtypeStruct((t, d), F32),
            jax.ShapeDtypeStruct((t, f), BF16),
            jax.ShapeDtypeStruct((t, f), BF16),
            jax.ShapeDtypeStruct((t, d), BF16),
            jax.ShapeDtypeStruct((1, d), F32),
        ],
        args=(x, g, dxo, a, b, wg, wu, wd),
        exchange=exchange,
    )


def _weight_grad(a, b, scale=1.0, exchange=None):
    t, m = a.shape
    n = b.shape[1]
    chips = N_DEV // 2
    r = m // N_DEV
    tk = min(REDUCE_TILE, t)
    halves = 2
    nb = n // halves
    nk = t // tk

    def body(a_ref, b_ref, o_ref, acc, send_buf, recv_buf, send_sems, recv_sems):
        k, j = pl.program_id(0), pl.program_id(1)
        x, y, c, _ = _mesh_place()
        sibling, _ = _peer(x, y, c, 1)
        bv = b_ref[...]
        if scale != 1.0:
            bv = bv * scale
        bb = bv.astype(BF16)
        acc_half = acc.at[j]

        @pl.when(k == 0)
        def _():
            acc_half[...] = jnp.zeros_like(acc_half)

        for i in range(m // MXU_ROWS):
            rows = slice(i * MXU_ROWS, (i + 1) * MXU_ROWS)
            acc_half[rows, :] += _dot(a_ref[:, rows].astype(BF16), bb, TN)

        def to_sibling(half):
            return _remote(send_buf.at[half], recv_buf.at[half], send_sems.at[half], recv_sems.at[half], sibling)

        def owned_rows(q, core):
            return pl.ds(pl.multiple_of((2 * q + core) * r, 8), r)

        for half in range(halves):
            @pl.when(jnp.logical_and(k == nk - 1, j == half))
            def _():
                for q in range(chips):
                    send_buf[half, q] = acc[half, owned_rows(q, 1 - c), :].astype(BF16)
                to_sibling(half).start()

        @pl.when(jnp.logical_and(k == nk - 1, j == halves - 1))
        def _():
            for half in range(halves):
                to_sibling(half).wait_send()
                to_sibling(half).wait_recv()
                for q in range(chips):
                    o_ref[q, :, half * nb:(half + 1) * nb] = (
                        acc[half, owned_rows(q, c), :] + recv_buf[half, q].astype(F32)).astype(BF16)

    (partial,), arrived = _call(
        body,
        name="weight_grad",
        grid=(nk, halves),
        in_specs=[pl.BlockSpec((tk, m), lambda k, j: (k, 0)), pl.BlockSpec((tk, nb), lambda k, j: (k, j))],
        out_specs=[pl.BlockSpec((chips, r, n), lambda k, j: (0, 0, 0))],
        out_shape=[jax.ShapeDtypeStruct((chips, r, n), BF16)],
        scratch_shapes=[
            pltpu.VMEM((halves, m, nb), F32),
            pltpu.VMEM((halves, chips, r, nb), BF16), pltpu.VMEM((halves, chips, r, nb), BF16),
            pltpu.SemaphoreType.DMA((halves,)), pltpu.SemaphoreType.DMA((halves,)),
        ],
        args=(a, b),
        exchange=exchange,
    )
    return partial, arrived


def _chunk_cumsum(v, reverse=False):
    n, width = v.shape
    row = lax.broadcasted_iota(jnp.int32, (n, n), 0)
    col = lax.broadcasted_iota(jnp.int32, (n, n), 1)
    earlier = col >= row if reverse else col <= row
    tri = jnp.where(jnp.logical_and(row // CHUNK == col // CHUNK, earlier), 1.0, 0.0).astype(BF16)
    hi = v.astype(BF16)
    rest = v - hi.astype(F32)
    mid = rest.astype(BF16)
    low = (rest - mid.astype(F32)).astype(BF16)
    sums = _dot(tri, jnp.concatenate([hi, mid, low], axis=1))
    return sums[:, 0:width] + sums[:, width:2 * width] + sums[:, 2 * width:3 * width]


def _shift_rows(v, shift, edge):
    n = v.shape[0]
    row = lax.broadcasted_iota(jnp.int32, (n, 1), 0)
    out = pltpu.roll(v, shift % n, axis=0)
    if shift > 0:
        for j in range(shift):
            out = jnp.where(row == j, edge[8 - shift + j:8 - shift + j + 1, :], out)
    else:
        for j in range(-shift):
            out = jnp.where(row == n + shift + j, edge[j:j + 1, :], out)
    return out


def _gates(z, lbp):
    w = HGRN_W
    lb = _sigmoid(lbp[0:1, :] - lbp[1:2, :])
    zq = z[:, 0:w]
    sig = _sigmoid(z[:, w:2 * w])
    f = lb + (1.0 - lb) * sig
    sq = _sigmoid(zq)
    q = zq * sq * HGRN_DK ** -0.5
    return lb, sig, f, sq, q


def _decayed_operands(q, f, v, qh_buf, kh_buf, kbar_buf, v_buf, etot_buf):
    logf = jnp.log(f)
    bcum = _chunk_cumsum(logf)
    rest = _chunk_cumsum(logf, reverse=True) - logf
    eb, enb, erest = jnp.exp(bcum), jnp.exp(-bcum), jnp.exp(rest)
    kk = 1.0 - f
    qh_buf[...] = (q * eb).astype(BF16)
    kh_buf[...] = (kk * enb).astype(BF16)
    kbar_buf[...] = (kk * erest).astype(BF16)
    v_buf[...] = v.astype(BF16)
    etot_buf[...] = jnp.exp(bcum + rest)
    return eb, enb, erest


def _short_conv(u, edge, cw):
    return cw[0:1, :] * _shift_rows(u, 2, edge) + cw[1:2, :] * _shift_rows(u, 1, edge) + cw[2:3, :] * u


def _block_causal_mask(n):
    row = lax.broadcasted_iota(jnp.int32, (n, n), 0)
    col = lax.broadcasted_iota(jnp.int32, (n, n), 1)
    return jnp.logical_and(row // CHUNK == col // CHUNK, col <= row)


def _spread(v, chunk_of_row, nc):
    return jnp.concatenate([jnp.where(chunk_of_row == c, v, jnp.zeros_like(v)) for c in range(nc)], axis=1)


def _pick(r, chunk_of_row, nc):
    out = jnp.where(chunk_of_row == 0, r[:, 0:HGRN_DK], 0.0)
    for c in range(1, nc):
        out = out + jnp.where(chunk_of_row == c, r[:, c * HGRN_DK:(c + 1) * HGRN_DK], 0.0)
    return out


def _mix_fwd(x, g, w_in, lbp, gh, convw_t, w_out, exchange=None):
    t, d = x.shape
    zw = w_in.shape[0]
    w = HGRN_W
    tm = min(TOKEN_TILE, t)
    nc = tm // CHUNK
    n_chunks = t // CHUNK

    def body(x_ref, g_ref, win_ref, lbp_ref, gh_ref, cw_ref, wout_ref,
             xo_ref, z_ref, o_ref, st_ref, y_ref, state, ucarry, qh_buf, kh_buf, kbar_buf, v_buf, etot_buf):
        @pl.when(pl.program_id(0) == 0)
        def _():
            state[...] = jnp.zeros_like(state)
            ucarry[...] = jnp.zeros_like(ucarry)

        xv = x_ref[...]
        h, _, _ = _rms(xv, g_ref[...])
        z_ref[...] = _dot(h.astype(BF16), win_ref[...], NT)
        z = z_ref[...]
        _, _, f, _, q = _gates(z, lbp_ref[...])
        _decayed_operands(q, f, z[:, 2 * w:3 * w], qh_buf, kh_buf, kbar_buf, v_buf, etot_buf)
        mask = _block_causal_mask(tm)
        chunk_of_row = lax.broadcasted_iota(jnp.int32, (tm, 1), 0) // CHUNK
        for hd in range(HGRN_HEADS):
            cols = slice(hd * HGRN_DK, (hd + 1) * HGRN_DK)
            qh, kh, kbar, vb = qh_buf[:, cols], kh_buf[:, cols], kbar_buf[:, cols], v_buf[:, cols]
            scores = jnp.where(mask, _dot(qh, kh, NT), 0.0).astype(BF16)
            gains = _dot(_spread(vb, chunk_of_row, nc), kbar, TN)
            entering = []
            st = state[hd]
            for c in range(nc):
                entering.append(st)
                st_ref[c, hd] = st
                st = st * etot_buf[c * CHUNK:c * CHUNK + 1, cols] + gains[c * HGRN_DK:(c + 1) * HGRN_DK, :]
            state[hd] = st
            from_states = _dot(qh, jnp.concatenate(entering, axis=0).astype(BF16), NT)
            o_ref[:, cols] = _dot(scores, vb) + _pick(from_states, chunk_of_row, nc)
        ghv = gh_ref[...]
        for hd in range(HGRN_HEADS):
            cols = slice(hd * HGRN_DK, (hd + 1) * HGRN_DK)
            on, _, _ = _rms(o_ref[:, cols], ghv[:, cols])
            zg = z[:, 3 * w + hd * HGRN_DK:3 * w + (hd + 1) * HGRN_DK]
            y_ref[:, cols] = (on * (zg * _sigmoid(zg))).astype(BF16)
        u = z[:, 5 * w:6 * w] * z[:, 6 * w:7 * w]
        conv = _short_conv(u, ucarry[...], cw_ref[...])
        ucarry[...] = u[tm - 8:tm, :]
        y_ref[:, w:2 * w] = (z[:, 4 * w:5 * w] * conv).astype(BF16)
        xo_ref[...] = xv + _dot(y_ref[...], wout_ref[...])

    return _call(
        body,
        name="mix_fwd",
        grid=(t // tm,),
        in_specs=[
            _rows(tm, d), _full((1, d)), _full((zw, d)), _full((2, w)), _full((1, w)), _full((3, w)),
            _full((2 * w, d)),
        ],
        out_specs=[
            _rows(tm, d), _rows(tm, zw), _rows(tm, w),
            pl.BlockSpec((nc, HGRN_HEADS, HGRN_DK, HGRN_DK), lambda i: (i, 0, 0, 0)),
            _rows(tm, 2 * w),
        ],
        out_shape=[
            jax.ShapeDtypeStruct((t, d), F32),
            jax.ShapeDtypeStruct((t, zw), F32),
            jax.ShapeDtypeStruct((t, w), F32),
            jax.ShapeDtypeStruct((n_chunks, HGRN_HEADS, HGRN_DK, HGRN_DK), F32),
            jax.ShapeDtypeStruct((t, 2 * w), BF16),
        ],
        scratch_shapes=[
            pltpu.VMEM((HGRN_HEADS, HGRN_DK, HGRN_DK), F32), pltpu.VMEM((8, w), F32),
            pltpu.VMEM((tm, w), BF16), pltpu.VMEM((tm, w), BF16), pltpu.VMEM((tm, w), BF16),
            pltpu.VMEM((tm, w), BF16), pltpu.VMEM((tm, w), F32),
        ],
        args=(x, g, w_in, lbp, gh, convw_t, w_out),
        exchange=exchange,
    )


def _mix_bwd(x, g, dxo, z, o, states, w_in, lbp, gh, convw_t, w_out, exchange=None):
    t, d = x.shape
    zw = w_in.shape[0]
    w = HGRN_W
    tm = min(TOKEN_TILE, t)
    nc = tm // CHUNK
    n = t // tm

    def body(x_ref, g_ref, dxo_ref, z_ref, zprev_ref, o_ref, st_ref, win_ref, lbp_ref, gh_ref, cw_ref, wout_ref,
             dx_ref, dz_ref, h_ref, dg_ref, dlbp_ref, dgh_ref, dcw_ref,
             dstate, dcarry, do_buf, dqh_buf, dkh_buf, dkbar_buf, carry_buf,
             qh_buf, kh_buf, kbar_buf, v_buf, etot_buf):
        first = pl.program_id(0) == 0

        @pl.when(first)
        def _():
            dstate[...] = jnp.zeros_like(dstate)
            dcarry[...] = jnp.zeros_like(dcarry)

        gv = g_ref[...]
        h, xh, r = _rms(x_ref[...], gv)
        h_ref[...] = h.astype(BF16)
        dxo = dxo_ref[...]
        dy = _dot(dxo.astype(BF16), wout_ref[...], NT)
        z = z_ref[...]
        lb, sig, f, sq, q = _gates(z, lbp_ref[...])
        eb, enb, erest = _decayed_operands(q, f, z[:, 2 * w:3 * w], qh_buf, kh_buf, kbar_buf, v_buf, etot_buf)

        ghv = gh_ref[...]
        dgh_parts = []
        for hd in range(HGRN_HEADS):
            cols = slice(hd * HGRN_DK, (hd + 1) * HGRN_DK)
            gcols = slice(3 * w + hd * HGRN_DK, 3 * w + (hd + 1) * HGRN_DK)
            on, oh, rr = _rms(o_ref[:, cols], ghv[:, cols])
            zg = z[:, gcols]
            sgz = _sigmoid(zg)
            dyh = dy[:, cols]
            don = dyh * (zg * sgz)
            dz_ref[:, gcols] = (dyh * on * (sgz * (1.0 + zg * (1.0 - sgz)))).astype(BF16)
            dgh_parts.append(jnp.sum(don * oh, axis=0, keepdims=True))
            do_buf[:, cols] = _rms_bwd(don, oh, rr, ghv[:, cols]).astype(BF16)
        _accumulate(dgh_ref, first, jnp.concatenate(dgh_parts, axis=1))

        zb = z[:, 4 * w:5 * w]
        zc = z[:, 5 * w:6 * w]
        zu = z[:, 6 * w:7 * w]
        u = zc * zu
        cw = cw_ref[...]
        zp = zprev_ref[...]
        uprev = jnp.where(pl.program_id(0) == n - 1, 0.0, zp[:, 5 * w:6 * w] * zp[:, 6 * w:7 * w])
        dyc = dy[:, w:2 * w]
        dz_ref[:, 4 * w:5 * w] = (dyc * _short_conv(u, uprev, cw)).astype(BF16)
        dconv = dyc * zb
        edge = dcarry[...]
        dconv1 = _shift_rows(dconv, -1, edge)
        dconv2 = _shift_rows(dconv, -2, edge)
        dcarry[...] = dconv[0:8, :]
        du = cw[2:3, :] * dconv + cw[1:2, :] * dconv1 + cw[0:1, :] * dconv2
        dz_ref[:, 5 * w:6 * w] = (du * zu).astype(BF16)
        dz_ref[:, 6 * w:7 * w] = (du * zc).astype(BF16)
        _accumulate(dcw_ref, first, jnp.concatenate([
            jnp.sum(u * dconv2, axis=0, keepdims=True),
            jnp.sum(u * dconv1, axis=0, keepdims=True),
            jnp.sum(u * dconv, axis=0, keepdims=True)], axis=0))

        mask = _block_causal_mask(tm)
        chunk_of_row = lax.broadcasted_iota(jnp.int32, (tm, 1), 0) // CHUNK
        for hd in range(HGRN_HEADS):
            cols = slice(hd * HGRN_DK, (hd + 1) * HGRN_DK)
            qhb, khb, kbarb, vb = qh_buf[:, cols], kh_buf[:, cols], kbar_buf[:, cols], v_buf[:, cols]
            dob = do_buf[:, cols]
            scores = jnp.where(mask, _dot(qhb, khb, NT), 0.0).astype(BF16)
            dscores = jnp.where(mask, _dot(dob, vb, NT), 0.0).astype(BF16)
            gains = _dot(_spread(dob, chunk_of_row, nc), qhb, TN)
            entering = [st_ref[c, hd] for c in range(nc)]
            leaving = [None] * nc
            dst = dstate[hd]
            for c in reversed(range(nc)):
                elast = etot_buf[c * CHUNK:c * CHUNK + 1, cols]
                leaving[c] = dst
                carry_buf[c:c + 1, cols] = jnp.sum(dst * entering[c], axis=0, keepdims=True) * elast
                dst = dst * elast + gains[c * HGRN_DK:(c + 1) * HGRN_DK, :]
            dstate[hd] = dst
            dst_rows = jnp.concatenate(leaving, axis=0).astype(BF16)
            dst_lanes = jnp.concatenate(leaving, axis=1).astype(BF16)
            st_lanes = jnp.concatenate(entering, axis=1).astype(BF16)
            dv = _dot(scores, dob, TN) + _pick(_dot(kbarb, dst_rows, NT), chunk_of_row, nc)
            dz_ref[:, 2 * w + hd * HGRN_DK:2 * w + (hd + 1) * HGRN_DK] = dv.astype(BF16)
            dqh_buf[:, cols] = _dot(dscores, khb) + _pick(_dot(dob, st_lanes), chunk_of_row, nc)
            dkh_buf[:, cols] = _dot(dscores, qhb, TN)
            dkbar_buf[:, cols] = _pick(_dot(vb, dst_lanes), chunk_of_row, nc)

        dqh, dkh, dkbar = dqh_buf[...], dkh_buf[...], dkbar_buf[...]
        kbar_dkbar = kbar_buf[...].astype(F32) * dkbar
        db = qh_buf[...].astype(F32) * dqh - kh_buf[...].astype(F32) * dkh - kbar_dkbar
        through_last = jnp.concatenate([
            jnp.broadcast_to(
                jnp.sum(kbar_dkbar[c * CHUNK:(c + 1) * CHUNK], axis=0, keepdims=True) + carry_buf[c:c + 1, :],
                (CHUNK, w))
            for c in range(nc)], axis=0)
        dlogf = _chunk_cumsum(db, reverse=True) + through_last
        df = dlogf / f - (dkh * enb + dkbar * erest)
        zq = z[:, 0:w]
        dz_ref[:, 0:w] = (dqh * eb * HGRN_DK ** -0.5 * (sq * (1.0 + zq * (1.0 - sq)))).astype(BF16)
        dz_ref[:, w:2 * w] = (df * (1.0 - lb) * sig * (1.0 - sig)).astype(BF16)
        dlb = jnp.sum(df * (1.0 - sig), axis=0, keepdims=True) * lb * (1.0 - lb)
        _accumulate(dlbp_ref, first, jnp.concatenate([dlb, -dlb], axis=0))

        dh = _dot(dz_ref[...], win_ref[...])
        dx_ref[...] = _rms_bwd(dh, xh, r, gv) + dxo
        _accumulate(dg_ref, first, jnp.sum(dh * xh, axis=0, keepdims=True))

    return _call(
        body,
        name="mix_bwd",
        grid=(n,),
        in_specs=[
            _rows_rev(tm, d, n), _full((1, d)), _rows_rev(tm, d, n), _rows_rev(tm, zw, n),
            pl.BlockSpec((8, zw), lambda i: (jnp.maximum((n - 1 - i) * (tm // 8) - 1, 0), 0)),
            _rows_rev(tm, w, n),
            pl.BlockSpec((nc, HGRN_HEADS, HGRN_DK, HGRN_DK), lambda i: (n - 1 - i, 0, 0, 0)),
            _full((zw, d)), _full((2, w)), _full((1, w)), _full((3, w)), _full((2 * w, d)),
        ],
        out_specs=[
            _rows_rev(tm, d, n), _rows_rev(tm, zw, n), _rows_rev(tm, d, n),
            _full((1, d)), _full((2, w)), _full((1, w)), _full((3, w)),
        ],
        out_shape=[
            jax.ShapeDtypeStruct((t, d), F32),
            jax.ShapeDtypeStruct((t, zw), BF16),
            jax.ShapeDtypeStruct((t, d), BF16),
            jax.ShapeDtypeStruct((1, d), F32),
            jax.ShapeDtypeStruct((2, w), F32),
            jax.ShapeDtypeStruct((1, w), F32),
            jax.ShapeDtypeStruct((3, w), F32),
        ],
        scratch_shapes=[
            pltpu.VMEM((HGRN_HEADS, HGRN_DK, HGRN_DK), F32), pltpu.VMEM((8, w), F32),
            pltpu.VMEM((tm, w), BF16), pltpu.VMEM((tm, w), F32), pltpu.VMEM((tm, w), F32), pltpu.VMEM((tm, w), F32),
            pltpu.VMEM((8, w), F32),
            pltpu.VMEM((tm, w), BF16), pltpu.VMEM((tm, w), BF16), pltpu.VMEM((tm, w), BF16),
            pltpu.VMEM((tm, w), BF16), pltpu.VMEM((tm, w), F32),
        ],
        args=(x, g, dxo, z, z, o, states, w_in, lbp, gh, convw_t, w_out),
        exchange=exchange,
    )


def _memkv_fwd(mem, g, wkv):
    m, d = mem.shape
    nb, _, cb = wkv.shape

    def body(mem_ref, g_ref, wkv_ref, kv_ref):
        mn, _, _ = _rms(mem_ref[...], g_ref[...])
        mnb = mn.astype(BF16)
        for j in range(nb):
            kv_ref[:, j * cb:(j + 1) * cb] = _dot(mnb, wkv_ref[j]).astype(BF16)

    return pl.pallas_call(
        body,
        name="memkv_fwd",
        out_shape=jax.ShapeDtypeStruct((m, nb * cb), BF16),
        compiler_params=_params(),
    )(mem, g, wkv)


def _memkv_bwd(mem, g, dkv, wkv):
    m, d = mem.shape
    nb, _, cb = wkv.shape

    def body(mem_ref, g_ref, dkv_ref, wkv_ref, dw_ref, dg_ref):
        mn, xh, _ = _rms(mem_ref[...], g_ref[...])
        mnb = mn.astype(BF16)
        dmn = jnp.zeros((m, d), F32)
        for j in range(nb):
            dkvb = dkv_ref[:, j * cb:(j + 1) * cb].astype(BF16)
            dw_ref[j] = _dot(mnb, dkvb, TN).astype(BF16)
            dmn = dmn + _dot(dkvb, wkv_ref[j], NT)
        dg_ref[...] = jnp.sum(dmn * xh, axis=0, keepdims=True)

    return pl.pallas_call(
        body,
        name="memkv_bwd",
        out_shape=[jax.ShapeDtypeStruct((nb, d, cb), BF16), jax.ShapeDtypeStruct((1, d), F32)],
        compiler_params=_params(),
    )(mem, g, dkv, wkv)


def _softmax_rows(qm_h, k_h):
    sc = _dot(qm_h, k_h, NT) * MEM_HD ** -0.5
    e = jnp.exp(sc - jnp.max(sc, axis=-1, keepdims=True))
    return e / jnp.sum(e, axis=-1, keepdims=True)


def _xattn_fwd(x, g, wq, kv, wo, exchange=None):
    t, d = x.shape
    m = kv.shape[0]
    tm = min(TOKEN_TILE, t)

    def body(x_ref, g_ref, wq_ref, kv_ref, wo_ref, xo_ref, hq_ref, qm_ref, att_ref):
        xv = x_ref[...]
        h, _, _ = _rms(xv, g_ref[...])
        hq_ref[...] = h.astype(BF16)
        qm_ref[...] = _dot(hq_ref[...], wq_ref[...]).astype(BF16)
        for hd in range(MEM_HEADS):
            cols = slice(hd * MEM_HD, (hd + 1) * MEM_HD)
            p = _softmax_rows(qm_ref[:, cols], kv_ref[:, cols])
            att_ref[:, cols] = _dot(p.astype(BF16), kv_ref[:, d + hd * MEM_HD:d + (hd + 1) * MEM_HD]).astype(BF16)
        xo_ref[...] = xv + _dot(att_ref[...], wo_ref[...])

    return _call(
        body,
        name="xattn_fwd",
        grid=(t // tm,),
        in_specs=[_rows(tm, d), _full((1, d)), _full((d, d)), _full((m, 2 * d)), _full((d, d))],
        out_specs=[_rows(tm, d), _rows(tm, d), _rows(tm, d), _rows(tm, d)],
        out_shape=[
            jax.ShapeDtypeStruct((t, d), F32),
            jax.ShapeDtypeStruct((t, d), BF16),
            jax.ShapeDtypeStruct((t, d), BF16),
            jax.ShapeDtypeStruct((t, d), BF16),
        ],
        args=(x, g, wq, kv, wo),
        exchange=exchange,
    )


def _xattn_bwd(x, g, dxo, qm, kv, wq, wo, exchange=None):
    t, d = x.shape
    m = kv.shape[0]
    tm = min(TOKEN_TILE, t)

    def body(x_ref, g_ref, dxo_ref, qm_ref, kv_ref, wq_ref, wo_ref, dx_ref, dqm_ref, dkv_ref, dg_ref):
        first = pl.program_id(0) == 0

        @pl.when(first)
        def _():
            dkv_ref[...] = jnp.zeros_like(dkv_ref)

        gv = g_ref[...]
        _, xh, r = _rms(x_ref[...], gv)
        dxo = dxo_ref[...]
        datt = _dot(dxo.astype(BF16), wo_ref[...], NT).astype(BF16)
        for hd in range(MEM_HEADS):
            cols = slice(hd * MEM_HD, (hd + 1) * MEM_HD)
            vcols = slice(d + hd * MEM_HD, d + (hd + 1) * MEM_HD)
            qm_h = qm_ref[:, cols]
            p = _softmax_rows(qm_h, kv_ref[:, cols])
            datt_h = datt[:, cols]
            dp = _dot(datt_h, kv_ref[:, vcols], NT)
            dsc = (p * (dp - jnp.sum(p * dp, axis=-1, keepdims=True)) * MEM_HD ** -0.5).astype(BF16)
            dqm_ref[:, cols] = _dot(dsc, kv_ref[:, cols]).astype(BF16)
            dkv_ref[:, cols] += _dot(dsc, qm_h, TN)
            dkv_ref[:, vcols] += _dot(p.astype(BF16), datt_h, TN)
        dh = _dot(dqm_ref[...], wq_ref[...], NT)
        dx_ref[...] = _rms_bwd(dh, xh, r, gv) + dxo
        _accumulate(dg_ref, first, jnp.sum(dh * xh, axis=0, keepdims=True))

    return _call(
        body,
        name="xattn_bwd",
        grid=(t // tm,),
        in_specs=[
            _rows(tm, d), _full((1, d)), _rows(tm, d), _rows(tm, d), _full((m, 2 * d)), _full((d, d)), _full((d, d)),
        ],
        out_specs=[_rows(tm, d), _rows(tm, d), _full((m, 2 * d)), _full((1, d))],
        out_shape=[
            jax.ShapeDtypeStruct((t, d), F32),
            jax.ShapeDtypeStruct((t, d), BF16),
            jax.ShapeDtypeStruct((m, 2 * d), F32),
            jax.ShapeDtypeStruct((1, d), F32),
        ],
        args=(x, g, dxo, qm, kv, wq, wo),
        exchange=exchange,
    )


def _mesh_place():
    x, y, c = lax.axis_index("x"), lax.axis_index("y"), lax.axis_index("c")
    return x, y, c, 4 * x + 2 * y + c


def _peer(x, y, c, k):
    px = 1 - x if k & 4 else x
    py = 1 - y if k & 2 else y
    pc = 1 - c if k & 1 else c
    return (px, py, pc), 4 * px + 2 * py + pc


ICI_HOPS = (2, 4, 6)
N_HOPS = len(ICI_HOPS)


def _remote(src, dst, send_sem, recv_sem, peer):
    return pltpu.make_async_remote_copy(
        src_ref=src, dst_ref=dst, send_sem=send_sem, recv_sem=recv_sem, device_id=peer, device_id_type=MESH_IDS)


def _gather_exchange(shards):
    n = len(shards)

    def place():
        x, y, c, me = _mesh_place()
        sibling, _ = _peer(x, y, c, 1)
        to_x, from_x = _peer(x, y, c, 4)
        to_y, from_y = _peer(x, y, c, 2)
        _, from_diagonal = _peer(x, y, c, 6)
        onward = (c * to_y[0] + (1 - c) * to_x[0], c * to_y[1] + (1 - c) * to_x[1], c)
        passed_on = c * from_x + (1 - c) * from_y
        return me, sibling, (to_x, to_y, onward), (from_x, from_y, from_diagonal), passed_on

    def start(src, dst, sems):
        ici_send, ici_recv, pair_send, pair_recv, local = sems
        me, sibling, targets, _, _ = place()
        for a in range(n):
            pltpu.make_async_copy(src[a], dst[a].at[me], local.at[a]).start()
            for j in range(2):
                _remote(src[a], dst[a].at[me], ici_send.at[a, j], ici_recv.at[a, j], targets[j]).start()
            _remote(src[a], dst[a].at[me], pair_send.at[a, 0], pair_recv.at[a, 0], sibling).start()

    def to_sibling(dst, sems, a, j, origin, sibling):
        _, _, pair_send, pair_recv, _ = sems
        slot = dst[a].at[origin]
        return _remote(slot, slot, pair_send.at[a, 1 + j], pair_recv.at[a, 1 + j], sibling)

    def middle(src, dst, sems):
        ici_send, ici_recv, _, _, _ = sems
        _, sibling, targets, origins, passed_on = place()
        for a in range(n):
            for j in range(2):
                _remote(src[a], dst[a].at[origins[j]], ici_send.at[a, j], ici_recv.at[a, j], targets[j]).wait_recv()
            slot = dst[a].at[passed_on]
            _remote(slot, slot, ici_send.at[a, 2], ici_recv.at[a, 2], targets[2]).start()
            for j in range(2):
                to_sibling(dst, sems, a, j, origins[j], sibling).start()

    def finish(src, dst, sems):
        ici_send, ici_recv, pair_send, pair_recv, local = sems
        me, sibling, targets, origins, _ = place()
        for a in range(n):
            _remote(src[a], dst[a].at[origins[2]], ici_send.at[a, 2], ici_recv.at[a, 2], targets[2]).wait_recv()
            to_sibling(dst, sems, a, 2, origins[2], sibling).start()
        for a in range(n):
            pltpu.make_async_copy(src[a], dst[a].at[me], local.at[a]).wait()
            for j in range(N_HOPS):
                _remote(src[a], dst[a].at[me], ici_send.at[a, j], ici_recv.at[a, j], targets[j]).wait_send()
            for j, origin in enumerate((me,) + origins):
                from_sibling = origin + 1 - 2 * (origin % 2)
                passed = _remote(src[a], dst[a].at[from_sibling], pair_send.at[a, j], pair_recv.at[a, j], sibling)
                passed.wait_send()
                passed.wait_recv()

    return _Exchange(
        shards,
        [jax.ShapeDtypeStruct((N_DEV,) + s.shape, s.dtype) for s in shards],
        [
            pltpu.SemaphoreType.DMA((n, N_HOPS)), pltpu.SemaphoreType.DMA((n, N_HOPS)),
            pltpu.SemaphoreType.DMA((n, N_HOPS + 1)), pltpu.SemaphoreType.DMA((n, N_HOPS + 1)),
            pltpu.SemaphoreType.DMA((n,)),
        ],
        start, finish, middle)


def _pair_exchange(blocks):
    n = len(blocks)
    chips = N_DEV // 2

    def copies(src, dst, sems):
        send, recv = sems
        x, y, c, _ = _mesh_place()
        sibling, _ = _peer(x, y, c, 1)
        return [_remote(src[a].at[2 * q + (1 - c)], dst[a].at[q], send.at[a, q], recv.at[a, q], sibling)
                for a in range(n) for q in range(chips)]

    def start(src, dst, sems):
        for cp in copies(src, dst, sems):
            cp.start()

    def finish(src, dst, sems):
        for cp in copies(src, dst, sems):
            cp.wait_send()
            cp.wait_recv()

    return _Exchange(
        blocks,
        [jax.ShapeDtypeStruct((chips,) + b.shape[1:], b.dtype) for b in blocks],
        [pltpu.SemaphoreType.DMA((n, chips)), pltpu.SemaphoreType.DMA((n, chips))],
        start, finish)


def _pair_add(blocks, received, core):
    _, r, c = blocks.shape
    chips = N_DEV // 2
    tr = r
    while tr > 512:
        tr //= 2

    def body(core_ref, mine_ref, got_ref, o_ref):
        o_ref[...] = (mine_ref[...].astype(F32) + got_ref[...].astype(F32)).astype(BF16)

    return pl.pallas_call(
        body,
        name="pair_add",
        grid_spec=pltpu.PrefetchScalarGridSpec(
            num_scalar_prefetch=1,
            grid=(chips, r // tr),
            in_specs=[
                pl.BlockSpec((None, None, tr, c), lambda q, i, core_ref: (q, core_ref[0], i, 0)),
                pl.BlockSpec((None, tr, c), lambda q, i, core_ref: (q, i, 0)),
            ],
            out_specs=pl.BlockSpec((None, tr, c), lambda q, i, core_ref: (q, i, 0)),
        ),
        out_shape=jax.ShapeDtypeStruct((chips, r, c), BF16),
        compiler_params=_params(("parallel", "parallel")),
    )(core, blocks.reshape(chips, 2, r, c), received)


def _scatter_copies(src, dst, sems, n, arrivals=False):
    send, recv, local = sems
    x, y, c, _ = _mesh_place()
    chip = 2 * x + y
    if arrivals is None:
        return [pltpu.make_async_copy(src[a].at[chip], dst[a].at[chip], local.at[a]) for a in range(n)]
    copies = []
    for a in range(n):
        for j, k in enumerate(ICI_HOPS):
            peer, _ = _peer(x, y, c, k)
            peer_chip = 2 * peer[0] + peer[1]
            slot = dst[a].at[peer_chip if arrivals else chip]
            copies.append(_remote(src[a].at[peer_chip], slot, send.at[a, j], recv.at[a, j], peer))
    return copies


def _scatter_start(src, dst, sems, n):
    for cp in _scatter_copies(src, dst, sems, n, arrivals=None) + _scatter_copies(src, dst, sems, n):
        cp.start()


def _scatter_finish(src, dst, sems, n):
    for cp in _scatter_copies(src, dst, sems, n, arrivals=None):
        cp.wait()
    for cp in _scatter_copies(src, dst, sems, n):
        cp.wait_send()
    for cp in _scatter_copies(src, dst, sems, n, arrivals=True):
        cp.wait_recv()


def _scatter_scratch(n):
    return [pltpu.SemaphoreType.DMA((n, N_HOPS)), pltpu.SemaphoreType.DMA((n, N_HOPS)), pltpu.SemaphoreType.DMA((n,))]


def _scatter_exchange(partials):
    n = len(partials)
    return _Exchange(
        partials, [jax.ShapeDtypeStruct(p.shape, p.dtype) for p in partials], _scatter_scratch(n),
        lambda src, dst, sems: _scatter_start(src, dst, sems, n),
        lambda src, dst, sems: _scatter_finish(src, dst, sems, n))


SMALL_LAYOUT = {
    "ffn1_norm": (0, 1, 1024), "mix_norm": (1, 1, 1024), "xattn_norm": (2, 1, 1024), "mem_norm": (3, 1, 1024),
    "ffn2_norm": (4, 1, 1024), "final_norm": (5, 1, 1024), "lb_param": (6, 2, 512), "hgrn_out_norm": (8, 1, 512),
    "conv_w": (9, 3, 512), "loss": (12, 1, 128),
}


def _final_exchange(partials, small):
    n = len(partials)
    names = list(small)
    width = 1024

    def body(*refs):
        src = refs[:n]
        pieces = refs[n:n + len(names)]
        dst = refs[n + len(names):2 * n + len(names)]
        total_ref = refs[2 * n + len(names)]
        pack, gathered, small_send, small_recv = refs[2 * n + len(names) + 1:2 * n + len(names) + 5]
        sems = refs[2 * n + len(names) + 5:]
        x, y, c, me = _mesh_place()
        pack[...] = jnp.zeros_like(pack)
        for name, piece in zip(names, pieces):
            row, nrows, ncols = SMALL_LAYOUT[name]
            pack[row:row + nrows, 0:ncols] = piece[...]
        for k in range(1, N_DEV):
            peer, _ = _peer(x, y, c, k)
            _remote(pack, gathered.at[me], small_send.at[k - 1], small_recv.at[k - 1], peer).start()
        _scatter_start(src, dst, sems, n)
        gathered[me] = pack[...]
        for k in range(1, N_DEV):
            peer, peer_index = _peer(x, y, c, k)
            landed = _remote(pack, gathered.at[peer_index], small_send.at[k - 1], small_recv.at[k - 1], peer)
            landed.wait_send()
            landed.wait_recv()
        total = gathered[0]
        for j in range(1, N_DEV):
            total = total + gathered[j]
        total_ref[...] = total
        _scatter_finish(src, dst, sems, n)

    hbm = pl.BlockSpec(memory_space=pltpu.HBM)
    vmem = pl.BlockSpec(memory_space=pltpu.VMEM)
    out = pl.pallas_call(
        body,
        name="final_exchange",
        in_specs=[hbm] * n + [vmem] * len(names),
        out_specs=[hbm] * n + [vmem],
        out_shape=[jax.ShapeDtypeStruct(p.shape, p.dtype) for p in partials]
        + [jax.ShapeDtypeStruct((SMALL_ROWS, width), F32)],
        scratch_shapes=[
            pltpu.VMEM((SMALL_ROWS, width), F32), pltpu.VMEM((N_DEV, SMALL_ROWS, width), F32),
            pltpu.SemaphoreType.DMA((N_DEV - 1,)), pltpu.SemaphoreType.DMA((N_DEV - 1,)),
        ] + _scatter_scratch(n),
        compiler_params=pltpu.CompilerParams(has_side_effects=True),
    )(*partials, *[small[k] for k in names])
    return out[:n], out[n]


def _adamw_math(w, g, m, v):
    m = ADAM_B1 * m + (1.0 - ADAM_B1) * g
    v = ADAM_B2 * v + (1.0 - ADAM_B2) * (g * g)
    m_hat = m / (1.0 - ADAM_B1 ** ADAM_STEP)
    v_hat = v / (1.0 - ADAM_B2 ** ADAM_STEP)
    delta = -ADAM_LR * (m_hat / (jnp.sqrt(v_hat) + ADAM_EPS) + ADAM_WD * w)
    return delta, m, v


def _adamw_shard(parts, w, m, v):
    r, c = w.shape
    n_parts = parts.shape[0]
    tr = max(rows for rows in range(16, r + 1, 16) if r % rows == 0 and rows * c <= ADAMW_TILE_ELEMENTS)

    def body(p_ref, w_ref, m_ref, v_ref, g_ref, d_ref, mo_ref, vo_ref):
        g = p_ref[0].astype(F32)
        for j in range(1, n_parts):
            g = g + p_ref[j].astype(F32)
        delta, mn, vn = _adamw_math(w_ref[...], g, m_ref[...], v_ref[...])
        g_ref[...] = g
        d_ref[...] = delta
        mo_ref[...] = mn
        vo_ref[...] = vn

    tile = pl.BlockSpec((tr, c), lambda i: (i, 0))
    return pl.pallas_call(
        body,
        name="adamw_shard",
        grid=(r // tr,),
        in_specs=[pl.BlockSpec((n_parts, tr, c), lambda i: (0, i, 0)), tile, tile, tile],
        out_specs=[tile] * 4,
        out_shape=[jax.ShapeDtypeStruct((r, c), F32)] * 4,
        compiler_params=_params(("parallel",)),
    )(parts, w, m, v)


def _adamw_small(gs, ws, ms, vs):
    n = len(gs)

    def body(*refs):
        g_refs, w_refs, m_refs, v_refs = refs[:n], refs[n:2 * n], refs[2 * n:3 * n], refs[3 * n:4 * n]
        d_out, m_out, v_out = refs[4 * n:5 * n], refs[5 * n:6 * n], refs[6 * n:7 * n]
        for i in range(n):
            delta, mn, vn = _adamw_math(w_refs[i][...], g_refs[i][...], m_refs[i][...], v_refs[i][...])
            d_out[i][...] = delta
            m_out[i][...] = mn
            v_out[i][...] = vn

    shapes = [jax.ShapeDtypeStruct(w.shape, F32) for w in ws]
    out = pl.pallas_call(
        body,
        name="adamw_small",
        out_shape=shapes * 3,
        compiler_params=_params(),
    )(*gs, *ws, *ms, *vs)
    return out[:n], out[n:2 * n], out[2 * n:]


TRANSPOSED = ("ffn1_gate", "ffn1_up", "w_in", "ffn2_gate", "ffn2_up", "conv_w")
GROUP_FFN1 = ("ffn1_gate", "ffn1_up", "ffn1_down")
GROUP_MIX = ("w_in", "w_out")
GROUP_XATTN = ("w_q_mem", "w_kv_mem", "w_o_mem")
GROUP_FFN2 = ("ffn2_gate", "ffn2_up", "ffn2_down")
LARGE = GROUP_FFN1 + GROUP_MIX + GROUP_XATTN + GROUP_FFN2
SMALL = ("ffn1_norm", "mix_norm", "lb_param", "hgrn_out_norm", "conv_w", "xattn_norm", "mem_norm", "ffn2_norm",
         "final_norm")
WEIGHTS = ("ffn1_norm", "ffn1_gate", "ffn1_up", "ffn1_down", "mix_norm", "w_in", "lb_param", "hgrn_out_norm",
           "conv_w", "w_out", "xattn_norm", "mem_norm", "w_q_mem", "w_kv_mem", "w_o_mem", "ffn2_norm", "ffn2_gate",
           "ffn2_up", "ffn2_down", "final_norm")


def kernel(x, mem, ffn1_norm, ffn1_gate, ffn1_up, ffn1_down, mix_norm, w_in, lb_param, hgrn_out_norm, conv_w, w_out, xattn_norm, mem_norm, w_q_mem, w_kv_mem, w_o_mem, ffn2_norm, ffn2_gate, ffn2_up, ffn2_down, final_norm, loss_target, m_ffn1_norm, m_ffn1_gate, m_ffn1_up, m_ffn1_down, m_mix_norm, m_w_in, m_lb_param, m_hgrn_out_norm, m_conv_w, m_w_out, m_xattn_norm, m_mem_norm, m_w_q_mem, m_w_kv_mem, m_w_o_mem, m_ffn2_norm, m_ffn2_gate, m_ffn2_up, m_ffn2_down, m_final_norm, v_ffn1_norm, v_ffn1_gate, v_ffn1_up, v_ffn1_down, v_mix_norm, v_w_in, v_lb_param, v_hgrn_out_norm, v_conv_w, v_w_out, v_xattn_norm, v_mem_norm, v_w_q_mem, v_w_kv_mem, v_w_o_mem, v_ffn2_norm, v_ffn2_gate, v_ffn2_up, v_ffn2_down, v_final_norm):
    given = dict(locals())
    me = 4 * lax.axis_index("x") + 2 * lax.axis_index("y") + lax.axis_index("c")
    x0, memv, target = x[0], mem[0], loss_target[0]

    def shard(prefix, name):
        v = given[prefix + name]
        if v.ndim == 1:
            return v.reshape(1, -1)
        if v.ndim == 2:
            return v
        return v[0].T if name in TRANSPOSED else v[0]

    w = {name: shard("", name) for name in WEIGHTS}
    m = {name: shard("m_", name) for name in WEIGHTS}
    v = {name: shard("v_", name) for name in WEIGHTS}

    conv_taps, conv_rows = w["conv_w"].shape
    conv_tile = jnp.pad(w["conv_w"], ((0, 8 - conv_taps), (0, 128 - conv_rows)))
    wire = {name: w[name].astype(BF16) for name in LARGE}
    full = {}

    def landed(names, gathered):
        for name, blocks in zip(names, gathered):
            _, r, c = blocks.shape
            full[name] = blocks if name == "w_kv_mem" else blocks.reshape(N_DEV * r, c)

    landed(GROUP_FFN1, _run_exchange(_gather_exchange([wire[k] for k in GROUP_FFN1]), "gather_first"))

    riders = (GROUP_MIX + ("w_kv_mem", "w_q_mem"), ("w_o_mem", "ffn2_gate", "ffn2_up"), ("ffn2_down",))
    (x1, a1, b1, s1), gathered = _ffn_fwd(
        x0, w["ffn1_norm"], full["ffn1_gate"], full["ffn1_up"], full["ffn1_down"],
        exchange=_gather_exchange([wire[k] for k in riders[0]] + [conv_tile]))
    landed(riders[0], gathered)
    convw_t = gathered[-1][:, :conv_taps, :conv_rows].transpose(1, 0, 2).reshape(conv_taps, N_DEV * conv_rows)
    (x2, z, o_raw, states, ycat), gathered = _mix_fwd(
        x1, w["mix_norm"], full["w_in"], w["lb_param"], w["hgrn_out_norm"], convw_t, full["w_out"],
        exchange=_gather_exchange([wire[k] for k in riders[1]]))
    landed(riders[1], gathered)
    kv = _memkv_fwd(memv, w["mem_norm"], full["w_kv_mem"])
    (x3, hq, qm, att), gathered = _xattn_fwd(
        x2, w["xattn_norm"], full["w_q_mem"], kv, full["w_o_mem"],
        exchange=_gather_exchange([wire[k] for k in riders[2]]))
    landed(riders[2], gathered)
    (dx4, a2, b2, s2, loss_part, d_final), _ = _ffn_fwd(
        x3, w["ffn2_norm"], full["ffn2_gate"], full["ffn2_up"], full["ffn2_down"], head=(w["final_norm"], target))

    core = lax.axis_index("c").astype(jnp.int32).reshape(1)
    parts = {}
    waiting = []

    def carried():
        names = [name for name, _ in waiting]
        exchange = _scatter_exchange([p for _, p in waiting]) if waiting else None
        del waiting[:]
        return names, exchange

    def weight_grad(name, a, b, scale=1.0):
        names, exchange = carried()
        partial, arrived = _weight_grad(a, b, scale, exchange=exchange)
        parts.update(zip(names, arrived))
        waiting.append((name, partial))

    (dx3, da2, db2, h4, d_ffn2_norm), _ = _ffn_bwd(
        x3, w["ffn2_norm"], dx4, a2, b2, full["ffn2_gate"], full["ffn2_up"], full["ffn2_down"])
    weight_grad("ffn2_down", s2, dx4, 0.5)
    weight_grad("ffn2_gate", da2, h4)
    weight_grad("ffn2_up", db2, h4)
    names, exchange = carried()
    (dx2, dqm, dkv, d_xattn_norm), arrived = _xattn_bwd(
        x2, w["xattn_norm"], dx3, qm, kv, full["w_q_mem"], full["w_o_mem"], exchange=exchange)
    parts.update(zip(names, arrived))
    weight_grad("w_o_mem", att, dx3)
    weight_grad("w_q_mem", hq, dqm)
    d_wkv_blocks, d_mem_norm = _memkv_bwd(memv, w["mem_norm"], dkv, full["w_kv_mem"])
    (from_sibling,) = _run_exchange(_pair_exchange([d_wkv_blocks]), "pair_exchange")
    waiting.append(("w_kv_mem", _pair_add(d_wkv_blocks, from_sibling, core)))
    weight_grad("w_out", ycat, dx2)
    names, exchange = carried()
    (dx1, dz, h2, d_mix_norm, d_lbp, d_gh, d_convw_t), arrived = _mix_bwd(
        x1, w["mix_norm"], dx2, z, o_raw, states, full["w_in"], w["lb_param"], w["hgrn_out_norm"], convw_t,
        full["w_out"], exchange=exchange)
    parts.update(zip(names, arrived))
    weight_grad("w_in", dz, h2)
    weight_grad("ffn1_down", s1, dx1, 0.5)
    (dx0, da1, db1, h1, d_ffn1_norm), _ = _ffn_bwd(
        x0, w["ffn1_norm"], dx1, a1, b1, full["ffn1_gate"], full["ffn1_up"], full["ffn1_down"])
    weight_grad("ffn1_gate", da1, h1)
    weight_grad("ffn1_up", db1, h1)

    small_parts = {
        "ffn1_norm": d_ffn1_norm, "mix_norm": d_mix_norm, "xattn_norm": d_xattn_norm, "mem_norm": d_mem_norm,
        "ffn2_norm": d_ffn2_norm, "final_norm": d_final, "lb_param": d_lbp, "hgrn_out_norm": d_gh,
        "conv_w": d_convw_t, "loss": loss_part,
    }
    names = [name for name, _ in waiting]
    arrived, total = _final_exchange([p for _, p in waiting], small_parts)
    parts.update(zip(names, arrived))

    g_out, d_out, m_out, v_out = {}, {}, {}, {}
    for name in LARGE:
        g_out[name], d_out[name], m_out[name], v_out[name] = _adamw_shard(parts[name], w[name], m[name], v[name])
    g_small = {}
    for name in SMALL:
        row, nrows, ncols = SMALL_LAYOUT[name]
        g_small[name] = total[row:row + nrows, 0:ncols]
    g_small["conv_w"] = lax.dynamic_slice_in_dim(g_small["conv_w"], me * conv_rows, conv_rows, axis=1)
    ds, ms, vs = _adamw_small(
        [g_small[k] for k in SMALL], [w[k] for k in SMALL], [m[k] for k in SMALL], [v[k] for k in SMALL])
    for i, name in enumerate(SMALL):
        g_out[name], d_out[name], m_out[name], v_out[name] = g_small[name], ds[i], ms[i], vs[i]

    def shaped(value, name):
        return (value.T if name in TRANSPOSED else value).reshape(given[name].shape)

    loss = total[SMALL_LAYOUT["loss"][0], 0]
    outs = [loss, dx0.reshape(x.shape)]
    for group in (g_out, d_out, m_out, v_out):
        outs += [shaped(group[name], name) for name in WEIGHTS]
    return tuple(outs)
```

```python
import jax
import jax.numpy as jnp
from jax import lax
from jax.experimental import pallas as pl
from jax.experimental.pallas import tpu as pltpu

F32 = jnp.float32
BF16 = jnp.bfloat16
MESH_IDS = pl.DeviceIdType.MESH

N_DEV = 8
EPS = 1e-6
HGRN_HEADS = 4
HGRN_DK = 128
HGRN_W = 512
CHUNK = 64
MEM_HEADS = 4
MEM_HD = 256
ADAM_LR = 0.001
ADAM_B1 = 0.9
ADAM_B2 = 0.999
ADAM_EPS = 1e-08
ADAM_WD = 0.01
ADAM_STEP = 10

TOKEN_TILE = 256
REDUCE_TILE = 1024
ADAMW_TILE_ELEMENTS = 256 * 1024
MIDDLE_EIGHTHS = 5
MXU_ROWS = 256
VMEM_LIMIT = 60 * 1024 * 1024
SMALL_ROWS = 16
NT = (((1,), (1,)), ((), ()))
TN = (((0,), (0,)), ((), ()))


def _params(sem=None):
    return pltpu.CompilerParams(dimension_semantics=sem, vmem_limit_bytes=VMEM_LIMIT)


def _dot(a, b, dims=None):
    if dims is None:
        return jnp.dot(a, b, preferred_element_type=F32)
    return lax.dot_general(a, b, dims, preferred_element_type=F32)


def _sigmoid(v):
    return 1.0 / (1.0 + jnp.exp(-v))


def _rms(x, g):
    r = lax.rsqrt(jnp.mean(x * x, axis=-1, keepdims=True) + EPS)
    xh = x * r
    return xh * g, xh, r


def _rms_bwd(dh, xh, r, g):
    dxh = dh * g
    return r * (dxh - xh * jnp.mean(dxh * xh, axis=-1, keepdims=True))


def _full(shape):
    return pl.BlockSpec(shape, lambda *_: (0,) * len(shape))


def _rows(tm, width):
    return pl.BlockSpec((tm, width), lambda i: (i, 0))


def _rows_rev(tm, width, n):
    return pl.BlockSpec((tm, width), lambda i: (n - 1 - i, 0))


def _accumulate(ref, first, value):
    @pl.when(first)
    def _():
        ref[...] = value

    @pl.when(jnp.logical_not(first))
    def _():
        ref[...] += value


class _Exchange:
    def __init__(self, operands, out_shapes, scratch, start, finish, middle=None):
        self.operands, self.out_shapes, self.scratch = list(operands), list(out_shapes), list(scratch)
        self.start, self.middle, self.finish = start, middle, finish


def _call(body, *, name, grid, in_specs, out_specs, out_shape, args, scratch_shapes=(), exchange=None):
    semantics = ("arbitrary",) * len(grid)
    if exchange is None:
        out = pl.pallas_call(
            body, name=name, grid=grid, in_specs=in_specs, out_specs=out_specs, out_shape=out_shape,
            scratch_shapes=list(scratch_shapes), compiler_params=_params(semantics))(*args)
        return out, []
    hbm = pl.BlockSpec(memory_space=pltpu.HBM)
    n_in, n_out, n_scr = len(in_specs), len(out_specs), len(scratch_shapes)
    e_in, e_out = len(exchange.operands), len(exchange.out_shapes)

    def carried(*refs):
        ins, rest = refs[:n_in], refs[n_in:]
        e_ins, rest = rest[:e_in], rest[e_in:]
        outs, rest = rest[:n_out], rest[n_out:]
        e_outs, rest = rest[:e_out], rest[e_out:]
        scr, e_scr = rest[:n_scr], rest[n_scr:]
        first = last = None
        for axis, size in enumerate(grid):
            at_start, at_end = pl.program_id(axis) == 0, pl.program_id(axis) == size - 1
            first = at_start if first is None else jnp.logical_and(first, at_start)
            last = at_end if last is None else jnp.logical_and(last, at_end)

        @pl.when(first)
        def _():
            exchange.start(e_ins, e_outs, e_scr)

        body(*ins, *outs, *scr)

        if exchange.middle is not None:
            assert len(grid) == 1

            @pl.when(pl.program_id(0) == (grid[0] * MIDDLE_EIGHTHS) // 8)
            def _():
                exchange.middle(e_ins, e_outs, e_scr)

        @pl.when(last)
        def _():
            exchange.finish(e_ins, e_outs, e_scr)

    out = pl.pallas_call(
        carried, name=name, grid=grid, in_specs=list(in_specs) + [hbm] * e_in,
        out_specs=list(out_specs) + [hbm] * e_out, out_shape=list(out_shape) + exchange.out_shapes,
        scratch_shapes=list(scratch_shapes) + exchange.scratch,
        compiler_params=pltpu.CompilerParams(
            dimension_semantics=semantics, vmem_limit_bytes=VMEM_LIMIT, has_side_effects=True),
    )(*args, *exchange.operands)
    return out[:n_out], out[n_out:]


def _run_exchange(exchange, name):
    hbm = pl.BlockSpec(memory_space=pltpu.HBM)
    e_in, e_out = len(exchange.operands), len(exchange.out_shapes)

    def body(*refs):
        e_ins, e_outs, e_scr = refs[:e_in], refs[e_in:e_in + e_out], refs[e_in + e_out:]
        exchange.start(e_ins, e_outs, e_scr)
        if exchange.middle is not None:
            exchange.middle(e_ins, e_outs, e_scr)
        exchange.finish(e_ins, e_outs, e_scr)

    return pl.pallas_call(
        body, name=name, in_specs=[hbm] * e_in, out_specs=[hbm] * e_out, out_shape=exchange.out_shapes,
        scratch_shapes=exchange.scratch, compiler_params=pltpu.CompilerParams(has_side_effects=True),
    )(*exchange.operands)


def _loss_head(xo, gf, tgt):
    d = xo.shape[1]
    y, xh, r = _rms(xo, gf)
    err = y - tgt
    dy = err * (1.0 / d)
    loss = 0.5 * jnp.sum(jnp.sum(err * err, axis=-1, keepdims=True) * (1.0 / d), axis=0, keepdims=True)
    return _rms_bwd(dy, xh, r, gf), loss, jnp.sum(dy * xh, axis=0, keepdims=True)


def _ffn_fwd(x, g, wg, wu, wd, exchange=None, head=None):
    t, d = x.shape
    f = wg.shape[0]
    tm = min(TOKEN_TILE, t)

    def body(x_ref, g_ref, wg_ref, wu_ref, wd_ref, *rest):
        xv = x_ref[...]
        h, _, _ = _rms(xv, g_ref[...])
        hb = h.astype(BF16)
        a = _dot(hb, wg_ref[...], NT)
        b = _dot(hb, wu_ref[...], NT)
        s = (a * _sigmoid(a) * b).astype(BF16)
        xo = xv + 0.5 * _dot(s, wd_ref[...])
        if head is None:
            xo_ref, a_ref, b_ref, s_ref = rest
            xo_ref[...] = xo
        else:
            gf_ref, tgt_ref, xo_ref, a_ref, b_ref, s_ref, loss_ref, dgf_ref = rest
            first = pl.program_id(0) == 0
            xo_ref[...], loss, dgf = _loss_head(xo, gf_ref[...], tgt_ref[...])
            _accumulate(loss_ref, first, jnp.broadcast_to(loss, (1, 128)))
            _accumulate(dgf_ref, first, dgf)
        a_ref[...] = a.astype(BF16)
        b_ref[...] = b.astype(BF16)
        s_ref[...] = s

    in_specs = [_rows(tm, d), _full((1, d)), _full((f, d)), _full((f, d)), _full((f, d))]
    out_specs = [_rows(tm, d), _rows(tm, f), _rows(tm, f), _rows(tm, f)]
    out_shape = [
        jax.ShapeDtypeStruct((t, d), F32),
        jax.ShapeDtypeStruct((t, f), BF16),
        jax.ShapeDtypeStruct((t, f), BF16),
        jax.ShapeDtypeStruct((t, f), BF16),
    ]
    args = (x, g, wg, wu, wd)
    if head is not None:
        in_specs += [_full((1, d)), _rows(tm, d)]
        out_specs += [_full((1, 128)), _full((1, d))]
        out_shape += [jax.ShapeDtypeStruct((1, 128), F32), jax.ShapeDtypeStruct((1, d), F32)]
        args += tuple(head)
    return _call(
        body, name="ffn_fwd", grid=(t // tm,), in_specs=in_specs, out_specs=out_specs, out_shape=out_shape,
        args=args, exchange=exchange)


def _ffn_up(x, g, wg, wu, exchange=None):
    t, d = x.shape
    f = wg.shape[0]
    tm = min(TOKEN_TILE, t)

    def body(x_ref, g_ref, wg_ref, wu_ref, a_ref, b_ref, s_ref):
        h, _, _ = _rms(x_ref[...], g_ref[...])
        hb = h.astype(BF16)
        a = _dot(hb, wg_ref[...], NT)
        b = _dot(hb, wu_ref[...], NT)
        a_ref[...] = a.astype(BF16)
        b_ref[...] = b.astype(BF16)
        s_ref[...] = (a * _sigmoid(a) * b).astype(BF16)

    return _call(
        body, name="ffn_up", grid=(t // tm,),
        in_specs=[_rows(tm, d), _full((1, d)), _full((f, d)), _full((f, d))],
        out_specs=[_rows(tm, f)] * 3, out_shape=[jax.ShapeDtypeStruct((t, f), BF16)] * 3,
        args=(x, g, wg, wu), exchange=exchange)


def _ffn_down(x, s, wd, exchange=None):
    t, d = x.shape
    f = wd.shape[0]
    tm = min(TOKEN_TILE, t)

    def body(x_ref, s_ref, wd_ref, xo_ref):
        xo_ref[...] = x_ref[...] + 0.5 * _dot(s_ref[...], wd_ref[...])

    return _call(
        body, name="ffn_down", grid=(t // tm,),
        in_specs=[_rows(tm, d), _rows(tm, f), _full((f, d))],
        out_specs=[_rows(tm, d)], out_shape=[jax.ShapeDtypeStruct((t, d), F32)],
        args=(x, s, wd), exchange=exchange)


def _ffn_bwd(x, g, dxo, a, b, wg, wu, wd, exchange=None):
    t, d = x.shape
    f = wg.shape[0]
    tm = min(TOKEN_TILE, t)

    def body(x_ref, g_ref, dxo_ref, a_ref, b_ref, wg_ref, wu_ref, wd_ref, dx_ref, da_ref, db_ref, h_ref, dg_ref):
        gv = g_ref[...]
        h, xh, r = _rms(x_ref[...], gv)
        dxo = dxo_ref[...]
        ds = _dot((0.5 * dxo).astype(BF16), wd_ref[...], NT)
        af = a_ref[...].astype(F32)
        bf = b_ref[...].astype(F32)
        sg = _sigmoid(af)
        da = (ds * bf * (sg * (1.0 + af * (1.0 - sg)))).astype(BF16)
        db = (ds * (af * sg)).astype(BF16)
        dh = _dot(da, wg_ref[...]) + _dot(db, wu_ref[...])
        dx_ref[...] = _rms_bwd(dh, xh, r, gv) + dxo
        da_ref[...] = da
        db_ref[...] = db
        h_ref[...] = h.astype(BF16)
        _accumulate(dg_ref, pl.program_id(0) == 0, jnp.sum(dh * xh, axis=0, keepdims=True))

    return _call(
        body,
        name="ffn_bwd",
        grid=(t // tm,),
        in_specs=[
            _rows(tm, d), _full((1, d)), _rows(tm, d), _rows(tm, f), _rows(tm, f),
            _full((f, d)), _full((f, d)), _full((f, d)),
        ],
        out_specs=[_rows(tm, d), _rows(tm, f), _rows(tm, f), _rows(tm, d), _full((1, d))],
        out_shape=[
            jax.ShapeDtypeStruct((t, d), F32),
            jax.ShapeDtypeStruct((t, f), BF16),
            jax.ShapeDtypeStruct((t, f), BF16),
            jax.ShapeDtypeStruct((t, d), BF16),
            jax.ShapeDtypeStruct((1, d), F32),
        ],
        args=(x, g, dxo, a, b, wg, wu, wd),
        exchange=exchange,
    )


def _weight_grad(a, b, scale=1.0, exchange=None):
    t, m = a.shape
    n = b.shape[1]
    chips = N_DEV // 2
    r = m // N_DEV
    tk = min(REDUCE_TILE, t)
    halves = 2
    nb = n // halves
    nk = t // tk

    def body(a_ref, b_ref, o_ref, acc, send_buf, recv_buf, send_sems, recv_sems):
        k, j = pl.program_id(0), pl.program_id(1)
        x, y, c, _ = _mesh_place()
        sibling, _ = _peer(x, y, c, 1)
        bv = b_ref[...]
        if scale != 1.0:
            bv = bv * scale
        bb = bv.astype(BF16)
        acc_half = acc.at[j]

        @pl.when(k == 0)
        def _():
            acc_half[...] = jnp.zeros_like(acc_half)

        for i in range(m // MXU_ROWS):
            rows = slice(i * MXU_ROWS, (i + 1) * MXU_ROWS)
            acc_half[rows, :] += _dot(a_ref[:, rows].astype(BF16), bb, TN)

        def to_sibling(half):
            return _remote(send_buf.at[half], recv_buf.at[half], send_sems.at[half], recv_sems.at[half], sibling)

        def owned_rows(q, core):
            return pl.ds(pl.multiple_of((2 * q + core) * r, 8), r)

        for half in range(halves):
            @pl.when(jnp.logical_and(k == nk - 1, j == half))
            def _():
                for q in range(chips):
                    send_buf[half, q] = acc[half, owned_rows(q, 1 - c), :].astype(BF16)
                to_sibling(half).start()

        @pl.when(jnp.logical_and(k == nk - 1, j == halves - 1))
        def _():
            for half in range(halves):
                to_sibling(half).wait_send()
                to_sibling(half).wait_recv()
                for q in range(chips):
                    o_ref[q, :, half * nb:(half + 1) * nb] = (
                        acc[half, owned_rows(q, c), :] + recv_buf[half, q].astype(F32)).astype(BF16)

    (partial,), arrived = _call(
        body,
        name="weight_grad",
        grid=(nk, halves),
        in_specs=[pl.BlockSpec((tk, m), lambda k, j: (k, 0)), pl.BlockSpec((tk, nb), lambda k, j: (k, j))],
        out_specs=[pl.BlockSpec((chips, r, n), lambda k, j: (0, 0, 0))],
        out_shape=[jax.ShapeDtypeStruct((chips, r, n), BF16)],
        scratch_shapes=[
            pltpu.VMEM((halves, m, nb), F32),
            pltpu.VMEM((halves, chips, r, nb), BF16), pltpu.VMEM((halves, chips, r, nb), BF16),
            pltpu.SemaphoreType.DMA((halves,)), pltpu.SemaphoreType.DMA((halves,)),
        ],
        args=(a, b),
        exchange=exchange,
    )
    return partial, arrived


def _chunk_cumsum(v, reverse=False):
    n, width = v.shape
    row = lax.broadcasted_iota(jnp.int32, (n, n), 0)
    col = lax.broadcasted_iota(jnp.int32, (n, n), 1)
    earlier = col >= row if reverse else col <= row
    tri = jnp.where(jnp.logical_and(row // CHUNK == col // CHUNK, earlier), 1.0, 0.0).astype(BF16)
    hi = v.astype(BF16)
    rest = v - hi.astype(F32)
    mid = rest.astype(BF16)
    low = (rest - mid.astype(F32)).astype(BF16)
    sums = _dot(tri, jnp.concatenate([hi, mid, low], axis=1))
    return sums[:, 0:width] + sums[:, width:2 * width] + sums[:, 2 * width:3 * width]


def _shift_rows(v, shift, edge):
    n = v.shape[0]
    row = lax.broadcasted_iota(jnp.int32, (n, 1), 0)
    out = pltpu.roll(v, shift % n, axis=0)
    if shift > 0:
        for j in range(shift):
            out = jnp.where(row == j, edge[8 - shift + j:8 - shift + j + 1, :], out)
    else:
        for j in range(-shift):
            out = jnp.where(row == n + shift + j, edge[j:j + 1, :], out)
    return out


def _gates(z, lbp):
    w = HGRN_W
    lb = _sigmoid(lbp[0:1, :] - lbp[1:2, :])
    zq = z[:, 0:w]
    sig = _sigmoid(z[:, w:2 * w])
    f = lb + (1.0 - lb) * sig
    sq = _sigmoid(zq)
    q = zq * sq * HGRN_DK ** -0.5
    return lb, sig, f, sq, q


def _decayed_operands(q, f, v, qh_buf, kh_buf, kbar_buf, v_buf, etot_buf):
    logf = jnp.log(f)
    bcum = _chunk_cumsum(logf)
    rest = _chunk_cumsum(logf, reverse=True) - logf
    eb, enb, erest = jnp.exp(bcum), jnp.exp(-bcum), jnp.exp(rest)
    kk = 1.0 - f
    qh_buf[...] = (q * eb).astype(BF16)
    kh_buf[...] = (kk * enb).astype(BF16)
    kbar_buf[...] = (kk * erest).astype(BF16)
    v_buf[...] = v.astype(BF16)
    etot_buf[...] = jnp.exp(bcum + rest)
    return eb, enb, erest


def _short_conv(u, edge, cw):
    return cw[0:1, :] * _shift_rows(u, 2, edge) + cw[1:2, :] * _shift_rows(u, 1, edge) + cw[2:3, :] * u


def _block_causal_mask(n):
    row = lax.broadcasted_iota(jnp.int32, (n, n), 0)
    col = lax.broadcasted_iota(jnp.int32, (n, n), 1)
    return jnp.logical_and(row // CHUNK == col // CHUNK, col <= row)


def _spread(v, chunk_of_row, nc):
    return jnp.concatenate([jnp.where(chunk_of_row == c, v, jnp.zeros_like(v)) for c in range(nc)], axis=1)


def _pick(r, chunk_of_row, nc):
    out = jnp.where(chunk_of_row == 0, r[:, 0:HGRN_DK], 0.0)
    for c in range(1, nc):
        out = out + jnp.where(chunk_of_row == c, r[:, c * HGRN_DK:(c + 1) * HGRN_DK], 0.0)
    return out


def _mix_fwd(x, g, w_in, lbp, gh, convw_t, w_out, exchange=None):
    t, d = x.shape
    zw = w_in.shape[0]
    w = HGRN_W
    tm = min(TOKEN_TILE, t)
    nc = tm // CHUNK
    n_chunks = t // CHUNK

    def body(x_ref, g_ref, win_ref, lbp_ref, gh_ref, cw_ref, wout_ref,
             xo_ref, z_ref, o_ref, st_ref, y_ref, state, ucarry, qh_buf, kh_buf, kbar_buf, v_buf, etot_buf):
        @pl.when(pl.program_id(0) == 0)
        def _():
            state[...] = jnp.zeros_like(state)
            ucarry[...] = jnp.zeros_like(ucarry)

        xv = x_ref[...]
        h, _, _ = _rms(xv, g_ref[...])
        z_ref[...] = _dot(h.astype(BF16), win_ref[...], NT)
        z = z_ref[...]
        _, _, f, _, q = _gates(z, lbp_ref[...])
        _decayed_operands(q, f, z[:, 2 * w:3 * w], qh_buf, kh_buf, kbar_buf, v_buf, etot_buf)
        mask = _block_causal_mask(tm)
        chunk_of_row = lax.broadcasted_iota(jnp.int32, (tm, 1), 0) // CHUNK
        for hd in range(HGRN_HEADS):
            cols = slice(hd * HGRN_DK, (hd + 1) * HGRN_DK)
            qh, kh, kbar, vb = qh_buf[:, cols], kh_buf[:, cols], kbar_buf[:, cols], v_buf[:, cols]
            scores = jnp.where(mask, _dot(qh, kh, NT), 0.0).astype(BF16)
            gains = _dot(_spread(vb, chunk_of_row, nc), kbar, TN)
            entering = []
            st = state[hd]
            for c in range(nc):
                entering.append(st)
                st_ref[c, hd] = st
                st = st * etot_buf[c * CHUNK:c * CHUNK + 1, cols] + gains[c * HGRN_DK:(c + 1) * HGRN_DK, :]
            state[hd] = st
            from_states = _dot(qh, jnp.concatenate(entering, axis=0).astype(BF16), NT)
            o_ref[:, cols] = _dot(scores, vb) + _pick(from_states, chunk_of_row, nc)
        ghv = gh_ref[...]
        for hd in range(HGRN_HEADS):
            cols = slice(hd * HGRN_DK, (hd + 1) * HGRN_DK)
            on, _, _ = _rms(o_ref[:, cols], ghv[:, cols])
            zg = z[:, 3 * w + hd * HGRN_DK:3 * w + (hd + 1) * HGRN_DK]
            y_ref[:, cols] = (on * (zg * _sigmoid(zg))).astype(BF16)
        u = z[:, 5 * w:6 * w] * z[:, 6 * w:7 * w]
        conv = _short_conv(u, ucarry[...], cw_ref[...])
        ucarry[...] = u[tm - 8:tm, :]
        y_ref[:, w:2 * w] = (z[:, 4 * w:5 * w] * conv).astype(BF16)
        xo_ref[...] = xv + _dot(y_ref[...], wout_ref[...])

    return _call(
        body,
        name="mix_fwd",
        grid=(t // tm,),
        in_specs=[
            _rows(tm, d), _full((1, d)), _full((zw, d)), _full((2, w)), _full((1, w)), _full((3, w)),
            _full((2 * w, d)),
        ],
        out_specs=[
            _rows(tm, d), _rows(tm, zw), _rows(tm, w),
            pl.BlockSpec((nc, HGRN_HEADS, HGRN_DK, HGRN_DK), lambda i: (i, 0, 0, 0)),
            _rows(tm, 2 * w),
        ],
        out_shape=[
            jax.ShapeDtypeStruct((t, d), F32),
            jax.ShapeDtypeStruct((t, zw), F32),
            jax.ShapeDtypeStruct((t, w), F32),
            jax.ShapeDtypeStruct((n_chunks, HGRN_HEADS, HGRN_DK, HGRN_DK), F32),
            jax.ShapeDtypeStruct((t, 2 * w), BF16),
        ],
        scratch_shapes=[
            pltpu.VMEM((HGRN_HEADS, HGRN_DK, HGRN_DK), F32), pltpu.VMEM((8, w), F32),
            pltpu.VMEM((tm, w), BF16), pltpu.VMEM((tm, w), BF16), pltpu.VMEM((tm, w), BF16),
            pltpu.VMEM((tm, w), BF16), pltpu.VMEM((tm, w), F32),
        ],
        args=(x, g, w_in, lbp, gh, convw_t, w_out),
        exchange=exchange,
    )


def _mix_bwd(x, g, dxo, z, o, states, w_in, lbp, gh, convw_t, w_out, exchange=None):
    t, d = x.shape
    zw = w_in.shape[0]
    w = HGRN_W
    tm = min(TOKEN_TILE, t)
    nc = tm // CHUNK
    n = t // tm

    def body(x_ref, g_ref, dxo_ref, z_ref, zprev_ref, o_ref, st_ref, win_ref, lbp_ref, gh_ref, cw_ref, wout_ref,
             dx_ref, dz_ref, h_ref, dg_ref, dlbp_ref, dgh_ref, dcw_ref,
             dstate, dcarry, do_buf, dqh_buf, dkh_buf, dkbar_buf, carry_buf,
             qh_buf, kh_buf, kbar_buf, v_buf, etot_buf):
        first = pl.program_id(0) == 0

        @pl.when(first)
        def _():
            dstate[...] = jnp.zeros_like(dstate)
            dcarry[...] = jnp.zeros_like(dcarry)

        gv = g_ref[...]
        h, xh, r = _rms(x_ref[...], gv)
        h_ref[...] = h.astype(BF16)
        dxo = dxo_ref[...]
        dy = _dot(dxo.astype(BF16), wout_ref[...], NT)
        z = z_ref[...]
        lb, sig, f, sq, q = _gates(z, lbp_ref[...])
        eb, enb, erest = _decayed_operands(q, f, z[:, 2 * w:3 * w], qh_buf, kh_buf, kbar_buf, v_buf, etot_buf)

        ghv = gh_ref[...]
        dgh_parts = []
        for hd in range(HGRN_HEADS):
            cols = slice(hd * HGRN_DK, (hd + 1) * HGRN_DK)
            gcols = slice(3 * w + hd * HGRN_DK, 3 * w + (hd + 1) * HGRN_DK)
            on, oh, rr = _rms(o_ref[:, cols], ghv[:, cols])
            zg = z[:, gcols]
            sgz = _sigmoid(zg)
            dyh = dy[:, cols]
            don = dyh * (zg * sgz)
            dz_ref[:, gcols] = (dyh * on * (sgz * (1.0 + zg * (1.0 - sgz)))).astype(BF16)
            dgh_parts.append(jnp.sum(don * oh, axis=0, keepdims=True))
            do_buf[:, cols] = _rms_bwd(don, oh, rr, ghv[:, cols]).astype(BF16)
        _accumulate(dgh_ref, first, jnp.concatenate(dgh_parts, axis=1))

        zb = z[:, 4 * w:5 * w]
        zc = z[:, 5 * w:6 * w]
        zu = z[:, 6 * w:7 * w]
        u = zc * zu
        cw = cw_ref[...]
        zp = zprev_ref[...]
        uprev = jnp.where(pl.program_id(0) == n - 1, 0.0, zp[:, 5 * w:6 * w] * zp[:, 6 * w:7 * w])
        dyc = dy[:, w:2 * w]
        dz_ref[:, 4 * w:5 * w] = (dyc * _short_conv(u, uprev, cw)).astype(BF16)
        dconv = dyc * zb
        edge = dcarry[...]
        dconv1 = _shift_rows(dconv, -1, edge)
        dconv2 = _shift_rows(dconv, -2, edge)
        dcarry[...] = dconv[0:8, :]
        du = cw[2:3, :] * dconv + cw[1:2, :] * dconv1 + cw[0:1, :] * dconv2
        dz_ref[:, 5 * w:6 * w] = (du * zu).astype(BF16)
        dz_ref[:, 6 * w:7 * w] = (du * zc).astype(BF16)
        _accumulate(dcw_ref, first, jnp.concatenate([
            jnp.sum(u * dconv2, axis=0, keepdims=True),
            jnp.sum(u * dconv1, axis=0, keepdims=True),
            jnp.sum(u * dconv, axis=0, keepdims=True)], axis=0))

        mask = _block_causal_mask(tm)
        chunk_of_row = lax.broadcasted_iota(jnp.int32, (tm, 1), 0) // CHUNK
        for hd in range(HGRN_HEADS):
            cols = slice(hd * HGRN_DK, (hd + 1) * HGRN_DK)
            qhb, khb, kbarb, vb = qh_buf[:, cols], kh_buf[:, cols], kbar_buf[:, cols], v_buf[:, cols]
            dob = do_buf[:, cols]
            scores = jnp.where(mask, _dot(qhb, khb, NT), 0.0).astype(BF16)
            dscores = jnp.where(mask, _dot(dob, vb, NT), 0.0).astype(BF16)
            gains = _dot(_spread(dob, chunk_of_row, nc), qhb, TN)
            entering = [st_ref[c, hd] for c in range(nc)]
            leaving = [None] * nc
            dst = dstate[hd]
            for c in reversed(range(nc)):
                elast = etot_buf[c * CHUNK:c * CHUNK + 1, cols]
                leaving[c] = dst
                carry_buf[c:c + 1, cols] = jnp.sum(dst * entering[c], axis=0, keepdims=True) * elast
                dst = dst * elast + gains[c * HGRN_DK:(c + 1) * HGRN_DK, :]
            dstate[hd] = dst
            dst_rows = jnp.concatenate(leaving, axis=0).astype(BF16)
            dst_lanes = jnp.concatenate(leaving, axis=1).astype(BF16)
            st_lanes = jnp.concatenate(entering, axis=1).astype(BF16)
            dv = _dot(scores, dob, TN) + _pick(_dot(kbarb, dst_rows, NT), chunk_of_row, nc)
            dz_ref[:, 2 * w + hd * HGRN_DK:2 * w + (hd + 1) * HGRN_DK] = dv.astype(BF16)
            dqh_buf[:, cols] = _dot(dscores, khb) + _pick(_dot(dob, st_lanes), chunk_of_row, nc)
            dkh_buf[:, cols] = _dot(dscores, qhb, TN)
            dkbar_buf[:, cols] = _pick(_dot(vb, dst_lanes), chunk_of_row, nc)

        dqh, dkh, dkbar = dqh_buf[...], dkh_buf[...], dkbar_buf[...]
        kbar_dkbar = kbar_buf[...].astype(F32) * dkbar
        db = qh_buf[...].astype(F32) * dqh - kh_buf[...].astype(F32) * dkh - kbar_dkbar
        through_last = jnp.concatenate([
            jnp.broadcast_to(
                jnp.sum(kbar_dkbar[c * CHUNK:(c + 1) * CHUNK], axis=0, keepdims=True) + carry_buf[c:c + 1, :],
                (CHUNK, w))
            for c in range(nc)], axis=0)
        dlogf = _chunk_cumsum(db, reverse=True) + through_last
        df = dlogf / f - (dkh * enb + dkbar * erest)
        zq = z[:, 0:w]
        dz_ref[:, 0:w] = (dqh * eb * HGRN_DK ** -0.5 * (sq * (1.0 + zq * (1.0 - sq)))).astype(BF16)
        dz_ref[:, w:2 * w] = (df * (1.0 - lb) * sig * (1.0 - sig)).astype(BF16)
        dlb = jnp.sum(df * (1.0 - sig), axis=0, keepdims=True) * lb * (1.0 - lb)
        _accumulate(dlbp_ref, first, jnp.concatenate([dlb, -dlb], axis=0))

        dh = _dot(dz_ref[...], win_ref[...])
        dx_ref[...] = _rms_bwd(dh, xh, r, gv) + dxo
        _accumulate(dg_ref, first, jnp.sum(dh * xh, axis=0, keepdims=True))

    return _call(
        body,
        name="mix_bwd",
        grid=(n,),
        in_specs=[
            _rows_rev(tm, d, n), _full((1, d)), _rows_rev(tm, d, n), _rows_rev(tm, zw, n),
            pl.BlockSpec((8, zw), lambda i: (jnp.maximum((n - 1 - i) * (tm // 8) - 1, 0), 0)),
            _rows_rev(tm, w, n),
            pl.BlockSpec((nc, HGRN_HEADS, HGRN_DK, HGRN_DK), lambda i: (n - 1 - i, 0, 0, 0)),
            _full((zw, d)), _full((2, w)), _full((1, w)), _full((3, w)), _full((2 * w, d)),
        ],
        out_specs=[
            _rows_rev(tm, d, n), _rows_rev(tm, zw, n), _rows_rev(tm, d, n),
            _full((1, d)), _full((2, w)), _full((1, w)), _full((3, w)),
        ],
        out_shape=[
            jax.ShapeDtypeStruct((t, d), F32),
            jax.ShapeDtypeStruct((t, zw), BF16),
            jax.ShapeDtypeStruct((t, d), BF16),
            jax.ShapeDtypeStruct((1, d), F32),
            jax.ShapeDtypeStruct((2, w), F32),
            jax.ShapeDtypeStruct((1, w), F32),
            jax.ShapeDtypeStruct((3, w), F32),
        ],
        scratch_shapes=[
            pltpu.VMEM((HGRN_HEADS, HGRN_DK, HGRN_DK), F32), pltpu.VMEM((8, w), F32),
            pltpu.VMEM((tm, w), BF16), pltpu.VMEM((tm, w), F32), pltpu.VMEM((tm, w), F32), pltpu.VMEM((tm, w), F32),
            pltpu.VMEM((8, w), F32),
            pltpu.VMEM((tm, w), BF16), pltpu.VMEM((tm, w), BF16), pltpu.VMEM((tm, w), BF16),
            pltpu.VMEM((tm, w), BF16), pltpu.VMEM((tm, w), F32),
        ],
        args=(x, g, dxo, z, z, o, states, w_in, lbp, gh, convw_t, w_out),
        exchange=exchange,
    )


def _memkv_fwd(mem, g, wkv):
    m, d = mem.shape
    nb, _, cb = wkv.shape

    def body(mem_ref, g_ref, wkv_ref, kv_ref):
        mn, _, _ = _rms(mem_ref[...], g_ref[...])
        mnb = mn.astype(BF16)
        for j in range(nb):
            kv_ref[:, j * cb:(j + 1) * cb] = _dot(mnb, wkv_ref[j]).astype(BF16)

    return pl.pallas_call(
        body,
        name="memkv_fwd",
        out_shape=jax.ShapeDtypeStruct((m, nb * cb), BF16),
        compiler_params=_params(),
    )(mem, g, wkv)


def _memkv_bwd(mem, g, dkv, wkv):
    m, d = mem.shape
    nb, _, cb = wkv.shape

    def body(mem_ref, g_ref, dkv_ref, wkv_ref, dw_ref, dg_ref):
        mn, xh, _ = _rms(mem_ref[...], g_ref[...])
        mnb = mn.astype(BF16)
        dmn = jnp.zeros((m, d), F32)
        for j in range(nb):
            dkvb = dkv_ref[:, j * cb:(j + 1) * cb].astype(BF16)
            dw_ref[j] = _dot(mnb, dkvb, TN).astype(BF16)
            dmn = dmn + _dot(dkvb, wkv_ref[j], NT)
        dg_ref[...] = jnp.sum(dmn * xh, axis=0, keepdims=True)

    return pl.pallas_call(
        body,
        name="memkv_bwd",
        out_shape=[jax.ShapeDtypeStruct((nb, d, cb), BF16), jax.ShapeDtypeStruct((1, d), F32)],
        compiler_params=_params(),
    )(mem, g, dkv, wkv)


def _softmax_rows(qm_h, k_h):
    sc = _dot(qm_h, k_h, NT) * MEM_HD ** -0.5
    e = jnp.exp(sc - jnp.max(sc, axis=-1, keepdims=True))
    return e / jnp.sum(e, axis=-1, keepdims=True)


def _xattn_fwd(x, g, wq, kv, wo, exchange=None):
    t, d = x.shape
    m = kv.shape[0]
    tm = min(TOKEN_TILE, t)

    def body(x_ref, g_ref, wq_ref, kv_ref, wo_ref, xo_ref, hq_ref, qm_ref, att_ref):
        xv = x_ref[...]
        h, _, _ = _rms(xv, g_ref[...])
        hq_ref[...] = h.astype(BF16)
        qm_ref[...] = _dot(hq_ref[...], wq_ref[...]).astype(BF16)
        for hd in range(MEM_HEADS):
            cols = slice(hd * MEM_HD, (hd + 1) * MEM_HD)
            p = _softmax_rows(qm_ref[:, cols], kv_ref[:, cols])
            att_ref[:, cols] = _dot(p.astype(BF16), kv_ref[:, d + hd * MEM_HD:d + (hd + 1) * MEM_HD]).astype(BF16)
        xo_ref[...] = xv + _dot(att_ref[...], wo_ref[...])

    return _call(
        body,
        name="xattn_fwd",
        grid=(t // tm,),
        in_specs=[_rows(tm, d), _full((1, d)), _full((d, d)), _full((m, 2 * d)), _full((d, d))],
        out_specs=[_rows(tm, d), _rows(tm, d), _rows(tm, d), _rows(tm, d)],
        out_shape=[
            jax.ShapeDtypeStruct((t, d), F32),
            jax.ShapeDtypeStruct((t, d), BF16),
            jax.ShapeDtypeStruct((t, d), BF16),
            jax.ShapeDtypeStruct((t, d), BF16),
        ],
        args=(x, g, wq, kv, wo),
        exchange=exchange,
    )


def _xattn_bwd(x, g, dxo, qm, kv, wq, wo, exchange=None):
    t, d = x.shape
    m = kv.shape[0]
    tm = min(TOKEN_TILE, t)

    def body(x_ref, g_ref, dxo_ref, qm_ref, kv_ref, wq_ref, wo_ref, dx_ref, dqm_ref, dkv_ref, dg_ref):
        first = pl.program_id(0) == 0

        @pl.when(first)
        def _():
            dkv_ref[...] = jnp.zeros_like(dkv_ref)

        gv = g_ref[...]
        _, xh, r = _rms(x_ref[...], gv)
        dxo = dxo_ref[...]
        datt = _dot(dxo.astype(BF16), wo_ref[...], NT).astype(BF16)
        for hd in range(MEM_HEADS):
            cols = slice(hd * MEM_HD, (hd + 1) * MEM_HD)
            vcols = slice(d + hd * MEM_HD, d + (hd + 1) * MEM_HD)
            qm_h = qm_ref[:, cols]
            p = _softmax_rows(qm_h, kv_ref[:, cols])
            datt_h = datt[:, cols]
            dp = _dot(datt_h, kv_ref[:, vcols], NT)
            dsc = (p * (dp - jnp.sum(p * dp, axis=-1, keepdims=True)) * MEM_HD ** -0.5).astype(BF16)
            dqm_ref[:, cols] = _dot(dsc, kv_ref[:, cols]).astype(BF16)
            dkv_ref[:, cols] += _dot(dsc, qm_h, TN)
            dkv_ref[:, vcols] += _dot(p.astype(BF16), datt_h, TN)
        dh = _dot(dqm_ref[...], wq_ref[...], NT)
        dx_ref[...] = _rms_bwd(dh, xh, r, gv) + dxo
        _accumulate(dg_ref, first, jnp.sum(dh * xh, axis=0, keepdims=True))

    return _call(
        body,
        name="xattn_bwd",
        grid=(t // tm,),
        in_specs=[
            _rows(tm, d), _full((1, d)), _rows(tm, d), _rows(tm, d), _full((m, 2 * d)), _full((d, d)), _full((d, d)),
        ],
        out_specs=[_rows(tm, d), _rows(tm, d), _full((m, 2 * d)), _full((1, d))],
        out_shape=[
            jax.ShapeDtypeStruct((t, d), F32),
            jax.ShapeDtypeStruct((t, d), BF16),
            jax.ShapeDtypeStruct((m, 2 * d), F32),
            jax.ShapeDtypeStruct((1, d), F32),
        ],
        args=(x, g, dxo, qm, kv, wq, wo),
        exchange=exchange,
    )


def _mesh_place():
    x, y, c = lax.axis_index("x"), lax.axis_index("y"), lax.axis_index("c")
    return x, y, c, 4 * x + 2 * y + c


def _peer(x, y, c, k):
    px = 1 - x if k & 4 else x
    py = 1 - y if k & 2 else y
    pc = 1 - c if k & 1 else c
    return (px, py, pc), 4 * px + 2 * py + pc


ICI_HOPS = (2, 4, 6)
N_HOPS = len(ICI_HOPS)


def _remote(src, dst, send_sem, recv_sem, peer):
    return pltpu.make_async_remote_copy(
        src_ref=src, dst_ref=dst, send_sem=send_sem, recv_sem=recv_sem, device_id=peer, device_id_type=MESH_IDS)


def _gather_exchange(shards):
    n = len(shards)

    def place():
        x, y, c, me = _mesh_place()
        sibling, _ = _peer(x, y, c, 1)
        to_x, from_x = _peer(x, y, c, 4)
        to_y, from_y = _peer(x, y, c, 2)
        _, from_diagonal = _peer(x, y, c, 6)
        onward = (c * to_y[0] + (1 - c) * to_x[0], c * to_y[1] + (1 - c) * to_x[1], c)
        passed_on = c * from_x + (1 - c) * from_y
        return me, sibling, (to_x, to_y, onward), (from_x, from_y, from_diagonal), passed_on

    def start(src, dst, sems):
        ici_send, ici_recv, pair_send, pair_recv, local = sems
        me, sibling, targets, _, _ = place()
        for a in range(n):
            pltpu.make_async_copy(src[a], dst[a].at[me], local.at[a]).start()
            for j in range(2):
                _remote(src[a], dst[a].at[me], ici_send.at[a, j], ici_recv.at[a, j], targets[j]).start()
            _remote(src[a], dst[a].at[me], pair_send.at[a, 0], pair_recv.at[a, 0], sibling).start()

    def to_sibling(dst, sems, a, j, origin, sibling):
        _, _, pair_send, pair_recv, _ = sems
        slot = dst[a].at[origin]
        return _remote(slot, slot, pair_send.at[a, 1 + j], pair_recv.at[a, 1 + j], sibling)

    def middle(src, dst, sems):
        ici_send, ici_recv, _, _, _ = sems
        _, sibling, targets, origins, passed_on = place()
        for a in range(n):
            for j in range(2):
                _remote(src[a], dst[a].at[origins[j]], ici_send.at[a, j], ici_recv.at[a, j], targets[j]).wait_recv()
            slot = dst[a].at[passed_on]
            _remote(slot, slot, ici_send.at[a, 2], ici_recv.at[a, 2], targets[2]).start()
            for j in range(2):
                to_sibling(dst, sems, a, j, origins[j], sibling).start()

    def finish(src, dst, sems):
        ici_send, ici_recv, pair_send, pair_recv, local = sems
        me, sibling, targets, origins, _ = place()
        for a in range(n):
            _remote(src[a], dst[a].at[origins[2]], ici_send.at[a, 2], ici_recv.at[a, 2], targets[2]).wait_recv()
            to_sibling(dst, sems, a, 2, origins[2], sibling).start()
        for a in range(n):
            pltpu.make_async_copy(src[a], dst[a].at[me], local.at[a]).wait()
            for j in range(N_HOPS):
                _remote(src[a], dst[a].at[me], ici_send.at[a, j], ici_recv.at[a, j], targets[j]).wait_send()
            for j, origin in enumerate((me,) + origins):
                from_sibling = origin + 1 - 2 * (origin % 2)
                passed = _remote(src[a], dst[a].at[from_sibling], pair_send.at[a, j], pair_recv.at[a, j], sibling)
                passed.wait_send()
                passed.wait_recv()

    return _Exchange(
        shards,
        [jax.ShapeDtypeStruct((N_DEV,) + s.shape, s.dtype) for s in shards],
        [
            pltpu.SemaphoreType.DMA((n, N_HOPS)), pltpu.SemaphoreType.DMA((n, N_HOPS)),
            pltpu.SemaphoreType.DMA((n, N_HOPS + 1)), pltpu.SemaphoreType.DMA((n, N_HOPS + 1)),
            pltpu.SemaphoreType.DMA((n,)),
        ],
        start, finish, middle)


def _pair_exchange(blocks):
    n = len(blocks)
    chips = N_DEV // 2

    def copies(src, dst, sems):
        send, recv = sems
        x, y, c, _ = _mesh_place()
        sibling, _ = _peer(x, y, c, 1)
        return [_remote(src[a].at[2 * q + (1 - c)], dst[a].at[q], send.at[a, q], recv.at[a, q], sibling)
                for a in range(n) for q in range(chips)]

    def start(src, dst, sems):
        for cp in copies(src, dst, sems):
            cp.start()

    def finish(src, dst, sems):
        for cp in copies(src, dst, sems):
            cp.wait_send()
            cp.wait_recv()

    return _Exchange(
        blocks,
        [jax.ShapeDtypeStruct((chips,) + b.shape[1:], b.dtype) for b in blocks],
        [pltpu.SemaphoreType.DMA((n, chips)), pltpu.SemaphoreType.DMA((n, chips))],
        start, finish)


def _pair_add(blocks, received, core):
    _, r, c = blocks.shape
    chips = N_DEV // 2
    tr = r
    while tr > 512:
        tr //= 2

    def body(core_ref, mine_ref, got_ref, o_ref):
        o_ref[...] = (mine_ref[...].astype(F32) + got_ref[...].astype(F32)).astype(BF16)

    return pl.pallas_call(
        body,
        name="pair_add",
        grid_spec=pltpu.PrefetchScalarGridSpec(
            num_scalar_prefetch=1,
            grid=(chips, r // tr),
            in_specs=[
                pl.BlockSpec((None, None, tr, c), lambda q, i, core_ref: (q, core_ref[0], i, 0)),
                pl.BlockSpec((None, tr, c), lambda q, i, core_ref: (q, i, 0)),
            ],
            out_specs=pl.BlockSpec((None, tr, c), lambda q, i, core_ref: (q, i, 0)),
        ),
        out_shape=jax.ShapeDtypeStruct((chips, r, c), BF16),
        compiler_params=_params(("parallel", "parallel")),
    )(core, blocks.reshape(chips, 2, r, c), received)


def _scatter_copies(src, dst, sems, n, arrivals=False):
    send, recv, local = sems
    x, y, c, _ = _mesh_place()
    chip = 2 * x + y
    if arrivals is None:
        return [pltpu.make_async_copy(src[a].at[chip], dst[a].at[chip], local.at[a]) for a in range(n)]
    copies = []
    for a in range(n):
        for j, k in enumerate(ICI_HOPS):
            peer, _ = _peer(x, y, c, k)
            peer_chip = 2 * peer[0] + peer[1]
            slot = dst[a].at[peer_chip if arrivals else chip]
            copies.append(_remote(src[a].at[peer_chip], slot, send.at[a, j], recv.at[a, j], peer))
    return copies


def _scatter_start(src, dst, sems, n):
    for cp in _scatter_copies(src, dst, sems, n, arrivals=None) + _scatter_copies(src, dst, sems, n):
        cp.start()


def _scatter_finish(src, dst, sems, n):
    for cp in _scatter_copies(src, dst, sems, n, arrivals=None):
        cp.wait()
    for cp in _scatter_copies(src, dst, sems, n):
        cp.wait_send()
    for cp in _scatter_copies(src, dst, sems, n, arrivals=True):
        cp.wait_recv()


def _scatter_scratch(n):
    return [pltpu.SemaphoreType.DMA((n, N_HOPS)), pltpu.SemaphoreType.DMA((n, N_HOPS)), pltpu.SemaphoreType.DMA((n,))]


def _scatter_exchange(partials):
    n = len(partials)
    return _Exchange(
        partials, [jax.ShapeDtypeStruct(p.shape, p.dtype) for p in partials], _scatter_scratch(n),
        lambda src, dst, sems: _scatter_start(src, dst, sems, n),
        lambda src, dst, sems: _scatter_finish(src, dst, sems, n))


SMALL_LAYOUT = {
    "ffn1_norm": (0, 1, 1024), "mix_norm": (1, 1, 1024), "xattn_norm": (2, 1, 1024), "mem_norm": (3, 1, 1024),
    "ffn2_norm": (4, 1, 1024), "final_norm": (5, 1, 1024), "lb_param": (6, 2, 512), "hgrn_out_norm": (8, 1, 512),
    "conv_w": (9, 3, 512), "loss": (12, 1, 128),
}


def _final_exchange(partials, small):
    n = len(partials)
    names = list(small)
    width = 1024

    def body(*refs):
        src = refs[:n]
        pieces = refs[n:n + len(names)]
        dst = refs[n + len(names):2 * n + len(names)]
        total_ref = refs[2 * n + len(names)]
        pack, gathered, small_send, small_recv = refs[2 * n + len(names) + 1:2 * n + len(names) + 5]
        sems = refs[2 * n + len(names) + 5:]
        x, y, c, me = _mesh_place()
        pack[...] = jnp.zeros_like(pack)
        for name, piece in zip(names, pieces):
            row, nrows, ncols = SMALL_LAYOUT[name]
            pack[row:row + nrows, 0:ncols] = piece[...]
        for k in range(1, N_DEV):
            peer, _ = _peer(x, y, c, k)
            _remote(pack, gathered.at[me], small_send.at[k - 1], small_recv.at[k - 1], peer).start()
        _scatter_start(src, dst, sems, n)
        gathered[me] = pack[...]
        for k in range(1, N_DEV):
            peer, peer_index = _peer(x, y, c, k)
            landed = _remote(pack, gathered.at[peer_index], small_send.at[k - 1], small_recv.at[k - 1], peer)
            landed.wait_send()
            landed.wait_recv()
        total = gathered[0]
        for j in range(1, N_DEV):
            total = total + gathered[j]
        total_ref[...] = total
        _scatter_finish(src, dst, sems, n)

    hbm = pl.BlockSpec(memory_space=pltpu.HBM)
    vmem = pl.BlockSpec(memory_space=pltpu.VMEM)
    out = pl.pallas_call(
        body,
        name="final_exchange",
        in_specs=[hbm] * n + [vmem] * len(names),
        out_specs=[hbm] * n + [vmem],
        out_shape=[jax.ShapeDtypeStruct(p.shape, p.dtype) for p in partials]
        + [jax.ShapeDtypeStruct((SMALL_ROWS, width), F32)],
        scratch_shapes=[
            pltpu.VMEM((SMALL_ROWS, width), F32), pltpu.VMEM((N_DEV, SMALL_ROWS, width), F32),
            pltpu.SemaphoreType.DMA((N_DEV - 1,)), pltpu.SemaphoreType.DMA((N_DEV - 1,)),
        ] + _scatter_scratch(n),
        compiler_params=pltpu.CompilerParams(has_side_effects=True),
    )(*partials, *[small[k] for k in names])
    return out[:n], out[n]


def _adamw_math(w, g, m, v):
    m = ADAM_B1 * m + (1.0 - ADAM_B1) * g
    v = ADAM_B2 * v + (1.0 - ADAM_B2) * (g * g)
    m_hat = m / (1.0 - ADAM_B1 ** ADAM_STEP)
    v_hat = v / (1.0 - ADAM_B2 ** ADAM_STEP)
    delta = -ADAM_LR * (m_hat / (jnp.sqrt(v_hat) + ADAM_EPS) + ADAM_WD * w)
    return delta, m, v


def _adamw_shard(parts, w, m, v):
    r, c = w.shape
    n_parts = parts.shape[0]
    tr = max(rows for rows in range(16, r + 1, 16) if r % rows == 0 and rows * c <= ADAMW_TILE_ELEMENTS)

    def body(p_ref, w_ref, m_ref, v_ref, g_ref, d_ref, mo_ref, vo_ref):
        g = p_ref[0].astype(F32)
        for j in range(1, n_parts):
            g = g + p_ref[j].astype(F32)
        delta, mn, vn = _adamw_math(w_ref[...], g, m_ref[...], v_ref[...])
        g_ref[...] = g
        d_ref[...] = delta
        mo_ref[...] = mn
        vo_ref[...] = vn

    tile = pl.BlockSpec((tr, c), lambda i: (i, 0))
    return pl.pallas_call(
        body,
        name="adamw_shard",
        grid=(r // tr,),
        in_specs=[pl.BlockSpec((n_parts, tr, c), lambda i: (0, i, 0)), tile, tile, tile],
        out_specs=[tile] * 4,
        out_shape=[jax.ShapeDtypeStruct((r, c), F32)] * 4,
        compiler_params=_params(("parallel",)),
    )(parts, w, m, v)


def _adamw_small(gs, ws, ms, vs):
    n = len(gs)

    def body(*refs):
        g_refs, w_refs, m_refs, v_refs = refs[:n], refs[n:2 * n], refs[2 * n:3 * n], refs[3 * n:4 * n]
        d_out, m_out, v_out = refs[4 * n:5 * n], refs[5 * n:6 * n], refs[6 * n:7 * n]
        for i in range(n):
            delta, mn, vn = _adamw_math(w_refs[i][...], g_refs[i][...], m_refs[i][...], v_refs[i][...])
            d_out[i][...] = delta
            m_out[i][...] = mn
            v_out[i][...] = vn

    shapes = [jax.ShapeDtypeStruct(w.shape, F32) for w in ws]
    out = pl.pallas_call(
        body,
        name="adamw_small",
        out_shape=shapes * 3,
        compiler_params=_params(),
    )(*gs, *ws, *ms, *vs)
    return out[:n], out[n:2 * n], out[2 * n:]


TRANSPOSED = ("ffn1_gate", "ffn1_up", "w_in", "ffn2_gate", "ffn2_up", "conv_w")
GROUP_FFN1 = ("ffn1_gate", "ffn1_up", "ffn1_down")
GROUP_MIX = ("w_in", "w_out")
GROUP_XATTN = ("w_q_mem", "w_kv_mem", "w_o_mem")
GROUP_FFN2 = ("ffn2_gate", "ffn2_up", "ffn2_down")
LARGE = GROUP_FFN1 + GROUP_MIX + GROUP_XATTN + GROUP_FFN2
SMALL = ("ffn1_norm", "mix_norm", "lb_param", "hgrn_out_norm", "conv_w", "xattn_norm", "mem_norm", "ffn2_norm",
         "final_norm")
WEIGHTS = ("ffn1_norm", "ffn1_gate", "ffn1_up", "ffn1_down", "mix_norm", "w_in", "lb_param", "hgrn_out_norm",
           "conv_w", "w_out", "xattn_norm", "mem_norm", "w_q_mem", "w_kv_mem", "w_o_mem", "ffn2_norm", "ffn2_gate",
           "ffn2_up", "ffn2_down", "final_norm")


def kernel(x, mem, ffn1_norm, ffn1_gate, ffn1_up, ffn1_down, mix_norm, w_in, lb_param, hgrn_out_norm, conv_w, w_out, xattn_norm, mem_norm, w_q_mem, w_kv_mem, w_o_mem, ffn2_norm, ffn2_gate, ffn2_up, ffn2_down, final_norm, loss_target, m_ffn1_norm, m_ffn1_gate, m_ffn1_up, m_ffn1_down, m_mix_norm, m_w_in, m_lb_param, m_hgrn_out_norm, m_conv_w, m_w_out, m_xattn_norm, m_mem_norm, m_w_q_mem, m_w_kv_mem, m_w_o_mem, m_ffn2_norm, m_ffn2_gate, m_ffn2_up, m_ffn2_down, m_final_norm, v_ffn1_norm, v_ffn1_gate, v_ffn1_up, v_ffn1_down, v_mix_norm, v_w_in, v_lb_param, v_hgrn_out_norm, v_conv_w, v_w_out, v_xattn_norm, v_mem_norm, v_w_q_mem, v_w_kv_mem, v_w_o_mem, v_ffn2_norm, v_ffn2_gate, v_ffn2_up, v_ffn2_down, v_final_norm):
    given = dict(locals())
    me = 4 * lax.axis_index("x") + 2 * lax.axis_index("y") + lax.axis_index("c")
    x0, memv, target = x[0], mem[0], loss_target[0]

    def shard(prefix, name):
        v = given[prefix + name]
        if v.ndim == 1:
            return v.reshape(1, -1)
        if v.ndim == 2:
            return v
        return v[0].T if name in TRANSPOSED else v[0]

    w = {name: shard("", name) for name in WEIGHTS}
    m = {name: shard("m_", name) for name in WEIGHTS}
    v = {name: shard("v_", name) for name in WEIGHTS}

    conv_taps, conv_rows = w["conv_w"].shape
    conv_tile = jnp.pad(w["conv_w"], ((0, 8 - conv_taps), (0, 128 - conv_rows)))
    wire = {name: w[name].astype(BF16) for name in LARGE}
    full = {}

    def landed(names, gathered):
        for name, blocks in zip(names, gathered):
            _, r, c = blocks.shape
            full[name] = blocks if name == "w_kv_mem" else blocks.reshape(N_DEV * r, c)

    first = ("ffn1_gate", "ffn1_up")
    landed(first, _run_exchange(_gather_exchange([wire[k] for k in first]), "gather_first"))

    riders = (("ffn1_down", "w_in"), ("w_out", "w_kv_mem"), ("w_q_mem", "w_o_mem", "ffn2_gate", "ffn2_up"),
              ("ffn2_down",))
    (a1, b1, s1), gathered = _ffn_up(
        x0, w["ffn1_norm"], full["ffn1_gate"], full["ffn1_up"],
        exchange=_gather_exchange([wire[k] for k in riders[0]]))
    landed(riders[0], gathered)
    (x1,), gathered = _ffn_down(
        x0, s1, full["ffn1_down"], exchange=_gather_exchange([wire[k] for k in riders[1]] + [conv_tile]))
    landed(riders[1], gathered)
    convw_t = gathered[-1][:, :conv_taps, :conv_rows].transpose(1, 0, 2).reshape(conv_taps, N_DEV * conv_rows)
    (x2, z, o_raw, states, ycat), gathered = _mix_fwd(
        x1, w["mix_norm"], full["w_in"], w["lb_param"], w["hgrn_out_norm"], convw_t, full["w_out"],
        exchange=_gather_exchange([wire[k] for k in riders[2]]))
    landed(riders[2], gathered)
    kv = _memkv_fwd(memv, w["mem_norm"], full["w_kv_mem"])
    (x3, hq, qm, att), gathered = _xattn_fwd(
        x2, w["xattn_norm"], full["w_q_mem"], kv, full["w_o_mem"],
        exchange=_gather_exchange([wire[k] for k in riders[3]]))
    landed(riders[3], gathered)
    (dx4, a2, b2, s2, loss_part, d_final), _ = _ffn_fwd(
        x3, w["ffn2_norm"], full["ffn2_gate"], full["ffn2_up"], full["ffn2_down"], head=(w["final_norm"], target))

    core = lax.axis_index("c").astype(jnp.int32).reshape(1)
    parts = {}
    waiting = []

    def carried():
        names = [name for name, _ in waiting]
        exchange = _scatter_exchange([p for _, p in waiting]) if waiting else None
        del waiting[:]
        return names, exchange

    def weight_grad(name, a, b, scale=1.0):
        names, exchange = carried()
        partial, arrived = _weight_grad(a, b, scale, exchange=exchange)
        parts.update(zip(names, arrived))
        waiting.append((name, partial))

    (dx3, da2, db2, h4, d_ffn2_norm), _ = _ffn_bwd(
        x3, w["ffn2_norm"], dx4, a2, b2, full["ffn2_gate"], full["ffn2_up"], full["ffn2_down"])
    weight_grad("ffn2_down", s2, dx4, 0.5)
    weight_grad("ffn2_gate", da2, h4)
    weight_grad("ffn2_up", db2, h4)
    names, exchange = carried()
    (dx2, dqm, dkv, d_xattn_norm), arrived = _xattn_bwd(
        x2, w["xattn_norm"], dx3, qm, kv, full["w_q_mem"], full["w_o_mem"], exchange=exchange)
    parts.update(zip(names, arrived))
    d_wkv_blocks, d_mem_norm = _memkv_bwd(memv, w["mem_norm"], dkv, full["w_kv_mem"])
    (from_sibling,) = _run_exchange(_pair_exchange([d_wkv_blocks]), "pair_exchange")
    waiting.append(("w_kv_mem", _pair_add(d_wkv_blocks, from_sibling, core)))
    names, exchange = carried()
    (dx1, dz, h2, d_mix_norm, d_lbp, d_gh, d_convw_t), arrived = _mix_bwd(
        x1, w["mix_norm"], dx2, z, o_raw, states, full["w_in"], w["lb_param"], w["hgrn_out_norm"], convw_t,
        full["w_out"], exchange=exchange)
    parts.update(zip(names, arrived))
    weight_grad("w_in", dz, h2)
    weight_grad("ffn1_down", s1, dx1, 0.5)
    (dx0, da1, db1, h1, d_ffn1_norm), _ = _ffn_bwd(
        x0, w["ffn1_norm"], dx1, a1, b1, full["ffn1_gate"], full["ffn1_up"], full["ffn1_down"])
    weight_grad("ffn1_gate", da1, h1)
    weight_grad("ffn1_up", db1, h1)
    weight_grad("w_o_mem", att, dx3)
    weight_grad("w_q_mem", hq, dqm)
    weight_grad("w_out", ycat, dx2)

    small_parts = {
        "ffn1_norm": d_ffn1_norm, "mix_norm": d_mix_norm, "xattn_norm": d_xattn_norm, "mem_norm": d_mem_norm,
        "ffn2_norm": d_ffn2_norm, "final_norm": d_final, "lb_param": d_lbp, "hgrn_out_norm": d_gh,
        "conv_w": d_convw_t, "loss": loss_part,
    }
    names = [name for name, _ in waiting]
    arrived, total = _final_exchange([p for _, p in waiting], small_parts)
    parts.update(zip(names, arrived))

    g_out, d_out, m_out, v_out = {}, {}, {}, {}
    for name in LARGE:
        g_out[name], d_out[name], m_out[name], v_out[name] = _adamw_shard(parts[name], w[name], m[name], v[name])
    g_small = {}
    for name in SMALL:
        row, nrows, ncols = SMALL_LAYOUT[name]
        g_small[name] = total[row:row + nrows, 0:ncols]
    g_small["conv_w"] = lax.dynamic_slice_in_dim(g_small["conv_w"], me * conv_rows, conv_rows, axis=1)
    ds, ms, vs = _adamw_small(
        [g_small[k] for k in SMALL], [w[k] for k in SMALL], [m[k] for k in SMALL], [v[k] for k in SMALL])
    for i, name in enumerate(SMALL):
        g_out[name], d_out[name], m_out[name], v_out[name] = g_small[name], ds[i], ms[i], vs[i]

    def shaped(value, name):
        return (value.T if name in TRANSPOSED else value).reshape(given[name].shape)

    loss = total[SMALL_LAYOUT["loss"][0], 0]
    outs = [loss, dx0.reshape(x.shape)]
    for group in (g_out, d_out, m_out, v_out):
        outs += [shaped(group[name], name) for name in WEIGHTS]
    return tuple(outs)
```

```python
import jax
import jax.numpy as jnp
from jax import lax
from jax.experimental import pallas as pl
from jax.experimental.pallas import tpu as pltpu

F32 = jnp.float32
BF16 = jnp.bfloat16
MESH_IDS = pl.DeviceIdType.MESH

N_DEV = 8
EPS = 1e-6
HGRN_HEADS = 4
HGRN_DK = 128
HGRN_W = 512
CHUNK = 64
MEM_HEADS = 4
MEM_HD = 256
ADAM_LR = 0.001
ADAM_B1 = 0.9
ADAM_B2 = 0.999
ADAM_EPS = 1e-08
ADAM_WD = 0.01
ADAM_STEP = 10

TOKEN_TILE = 256
REDUCE_TILE = 1024
ADAMW_TILE_ELEMENTS = 256 * 1024
MIDDLE_EIGHTHS = 5
MXU_ROWS = 256
VMEM_LIMIT = 60 * 1024 * 1024
SMALL_ROWS = 16
NT = (((1,), (1,)), ((), ()))
TN = (((0,), (0,)), ((), ()))


def _params(sem=None):
    return pltpu.CompilerParams(dimension_semantics=sem, vmem_limit_bytes=VMEM_LIMIT)


def _dot(a, b, dims=None):
    if dims is None:
        return jnp.dot(a, b, preferred_element_type=F32)
    return lax.dot_general(a, b, dims, preferred_element_type=F32)


def _sigmoid(v):
    return 1.0 / (1.0 + jnp.exp(-v))


def _rms(x, g):
    r = lax.rsqrt(jnp.mean(x * x, axis=-1, keepdims=True) + EPS)
    xh = x * r
    return xh * g, xh, r


def _rms_bwd(dh, xh, r, g):
    dxh = dh * g
    return r * (dxh - xh * jnp.mean(dxh * xh, axis=-1, keepdims=True))


def _full(shape):
    return pl.BlockSpec(shape, lambda *_: (0,) * len(shape))


def _rows(tm, width):
    return pl.BlockSpec((tm, width), lambda i: (i, 0))


def _rows_rev(tm, width, n):
    return pl.BlockSpec((tm, width), lambda i: (n - 1 - i, 0))


def _accumulate(ref, first, value):
    @pl.when(first)
    def _():
        ref[...] = value

    @pl.when(jnp.logical_not(first))
    def _():
        ref[...] += value


class _Exchange:
    def __init__(self, operands, out_shapes, scratch, start, finish, middle=None):
        self.operands, self.out_shapes, self.scratch = list(operands), list(out_shapes), list(scratch)
        self.start, self.middle, self.finish = start, middle, finish


def _call(body, *, name, grid, in_specs, out_specs, out_shape, args, scratch_shapes=(), exchange=None):
    semantics = ("arbitrary",) * len(grid)
    if exchange is None:
        out = pl.pallas_call(
            body, name=name, grid=grid, in_specs=in_specs, out_specs=out_specs, out_shape=out_shape,
            scratch_shapes=list(scratch_shapes), compiler_params=_params(semantics))(*args)
        return out, []
    hbm = pl.BlockSpec(memory_space=pltpu.HBM)
    n_in, n_out, n_scr = len(in_specs), len(out_specs), len(scratch_shapes)
    e_in, e_out = len(exchange.operands), len(exchange.out_shapes)

    def carried(*refs):
        ins, rest = refs[:n_in], refs[n_in:]
        e_ins, rest = rest[:e_in], rest[e_in:]
        outs, rest = rest[:n_out], rest[n_out:]
        e_outs, rest = rest[:e_out], rest[e_out:]
        scr, e_scr = rest[:n_scr], rest[n_scr:]
        first = last = None
        for axis, size in enumerate(grid):
            at_start, at_end = pl.program_id(axis) == 0, pl.program_id(axis) == size - 1
            first = at_start if first is None else jnp.logical_and(first, at_start)
            last = at_end if last is None else jnp.logical_and(last, at_end)

        @pl.when(first)
        def _():
            exchange.start(e_ins, e_outs, e_scr)

        body(*ins, *outs, *scr)

        if exchange.middle is not None:
            assert len(grid) == 1

            @pl.when(pl.program_id(0) == (grid[0] * MIDDLE_EIGHTHS) // 8)
            def _():
                exchange.middle(e_ins, e_outs, e_scr)

        @pl.when(last)
        def _():
            exchange.finish(e_ins, e_outs, e_scr)

    out = pl.pallas_call(
        carried, name=name, grid=grid, in_specs=list(in_specs) + [hbm] * e_in,
        out_specs=list(out_specs) + [hbm] * e_out, out_shape=list(out_shape) + exchange.out_shapes,
        scratch_shapes=list(scratch_shapes) + exchange.scratch,
        compiler_params=pltpu.CompilerParams(
            dimension_semantics=semantics, vmem_limit_bytes=VMEM_LIMIT, has_side_effects=True),
    )(*args, *exchange.operands)
    return out[:n_out], out[n_out:]


def _run_exchange(exchange, name):
    hbm = pl.BlockSpec(memory_space=pltpu.HBM)
    e_in, e_out = len(exchange.operands), len(exchange.out_shapes)

    def body(*refs):
        e_ins, e_outs, e_scr = refs[:e_in], refs[e_in:e_in + e_out], refs[e_in + e_out:]
        exchange.start(e_ins, e_outs, e_scr)
        if exchange.middle is not None:
            exchange.middle(e_ins, e_outs, e_scr)
        exchange.finish(e_ins, e_outs, e_scr)

    return pl.pallas_call(
        body, name=name, in_specs=[hbm] * e_in, out_specs=[hbm] * e_out, out_shape=exchange.out_shapes,
        scratch_shapes=exchange.scratch, compiler_params=pltpu.CompilerParams(has_side_effects=True),
    )(*exchange.operands)


def _loss_head(xo, gf, tgt):
    d = xo.shape[1]
    y, xh, r = _rms(xo, gf)
    err = y - tgt
    dy = err * (1.0 / d)
    loss = 0.5 * jnp.sum(jnp.sum(err * err, axis=-1, keepdims=True) * (1.0 / d), axis=0, keepdims=True)
    return _rms_bwd(dy, xh, r, gf), loss, jnp.sum(dy * xh, axis=0, keepdims=True)


def _ffn_fwd(x, g, wg, wu, wd, exchange=None, head=None):
    t, d = x.shape
    f = wg.shape[0]
    tm = min(TOKEN_TILE, t)

    def body(x_ref, g_ref, wg_ref, wu_ref, wd_ref, *rest):
        xv = x_ref[...]
        h, _, _ = _rms(xv, g_ref[...])
        hb = h.astype(BF16)
        a = _dot(hb, wg_ref[...], NT)
        b = _dot(hb, wu_ref[...], NT)
        s = (a * _sigmoid(a) * b).astype(BF16)
        xo = xv + 0.5 * _dot(s, wd_ref[...])
        if head is None:
            xo_ref, a_ref, b_ref, s_ref = rest
            xo_ref[...] = xo
        else:
            gf_ref, tgt_ref, xo_ref, a_ref, b_ref, s_ref, loss_ref, dgf_ref = rest
            first = pl.program_id(0) == 0
            xo_ref[...], loss, dgf = _loss_head(xo, gf_ref[...], tgt_ref[...])
            _accumulate(loss_ref, first, jnp.broadcast_to(loss, (1, 128)))
            _accumulate(dgf_ref, first, dgf)
        a_ref[...] = a.astype(BF16)
        b_ref[...] = b.astype(BF16)
        s_ref[...] = s

    in_specs = [_rows(tm, d), _full((1, d)), _full((f, d)), _full((f, d)), _full((f, d))]
    out_specs = [_rows(tm, d), _rows(tm, f), _rows(tm, f), _rows(tm, f)]
    out_shape = [
        jax.ShapeDtypeStruct((t, d), F32),
        jax.ShapeDtypeStruct((t, f), BF16),
        jax.ShapeDtypeStruct((t, f), BF16),
        jax.ShapeDtypeStruct((t, f), BF16),
    ]
    args = (x, g, wg, wu, wd)
    if head is not None:
        in_specs += [_full((1, d)), _rows(tm, d)]
        out_specs += [_full((1, 128)), _full((1, d))]
        out_shape += [jax.ShapeDtypeStruct((1, 128), F32), jax.ShapeDtypeStruct((1, d), F32)]
        args += tuple(head)
    return _call(
        body, name="ffn_fwd", grid=(t // tm,), in_specs=in_specs, out_specs=out_specs, out_shape=out_shape,
        args=args, exchange=exchange)


def _ffn_up(x, g, wg, wu, exchange=None):
    t, d = x.shape
    f = wg.shape[0]
    tm = min(TOKEN_TILE, t)

    def body(x_ref, g_ref, wg_ref, wu_ref, a_ref, b_ref, s_ref):
        h, _, _ = _rms(x_ref[...], g_ref[...])
        hb = h.astype(BF16)
        a = _dot(hb, wg_ref[...], NT)
        b = _dot(hb, wu_ref[...], NT)
        a_ref[...] = a.astype(BF16)
        b_ref[...] = b.astype(BF16)
        s_ref[...] = (a * _sigmoid(a) * b).astype(BF16)

    return _call(
        body, name="ffn_up", grid=(t // tm,),
        in_specs=[_rows(tm, d), _full((1, d)), _full((f, d)), _full((f, d))],
        out_specs=[_rows(tm, f)] * 3, out_shape=[jax.ShapeDtypeStruct((t, f), BF16)] * 3,
        args=(x, g, wg, wu), exchange=exchange)


def _ffn_down(x, s, wd, exchange=None):
    t, d = x.shape
    f = wd.shape[0]
    tm = min(TOKEN_TILE, t)

    def body(x_ref, s_ref, wd_ref, xo_ref):
        xo_ref[...] = x_ref[...] + 0.5 * _dot(s_ref[...], wd_ref[...])

    return _call(
        body, name="ffn_down", grid=(t // tm,),
        in_specs=[_rows(tm, d), _rows(tm, f), _full((f, d))],
        out_specs=[_rows(tm, d)], out_shape=[jax.ShapeDtypeStruct((t, d), F32)],
        args=(x, s, wd), exchange=exchange)


def _ffn_bwd(x, g, dxo, a, b, wg, wu, wd, exchange=None):
    t, d = x.shape
    f = wg.shape[0]
    tm = min(TOKEN_TILE, t)

    def body(x_ref, g_ref, dxo_ref, a_ref, b_ref, wg_ref, wu_ref, wd_ref, dx_ref, da_ref, db_ref, h_ref, dg_ref):
        gv = g_ref[...]
        h, xh, r = _rms(x_ref[...], gv)
        dxo = dxo_ref[...]
        ds = _dot((0.5 * dxo).astype(BF16), wd_ref[...], NT)
        af = a_ref[...].astype(F32)
        bf = b_ref[...].astype(F32)
        sg = _sigmoid(af)
        da = (ds * bf * (sg * (1.0 + af * (1.0 - sg)))).astype(BF16)
        db = (ds * (af * sg)).astype(BF16)
        dh = _dot(da, wg_ref[...]) + _dot(db, wu_ref[...])
        dx_ref[...] = _rms_bwd(dh, xh, r, gv) + dxo
        da_ref[...] = da
        db_ref[...] = db
        h_ref[...] = h.astype(BF16)
        _accumulate(dg_ref, pl.program_id(0) == 0, jnp.sum(dh * xh, axis=0, keepdims=True))

    return _call(
        body,
        name="ffn_bwd",
        grid=(t // tm,),
        in_specs=[
            _rows(tm, d), _full((1, d)), _rows(tm, d), _rows(tm, f), _rows(tm, f),
            _full((f, d)), _full((f, d)), _full((f, d)),
        ],
        out_specs=[_rows(tm, d), _rows(tm, f), _rows(tm, f), _rows(tm, d), _full((1, d))],
        out_shape=[
            jax.ShapeDtypeStruct((t, d), F32),
            jax.ShapeDtypeStruct((t, f), BF16),
            jax.ShapeDtypeStruct((t, f), BF16),
            jax.ShapeDtypeStruct((t, d), BF16),
            jax.ShapeDtypeStruct((1, d), F32),
        ],
        args=(x, g, dxo, a, b, wg, wu, wd),
        exchange=exchange,
    )


def _weight_grad(a, b, scale=1.0, exchange=None):
    t, m = a.shape
    n = b.shape[1]
    chips = N_DEV // 2
    r = m // N_DEV
    tk = min(REDUCE_TILE, t)
    halves = 2
    nb = n // halves
    nk = t // tk

    def body(a_ref, b_ref, o_ref, acc, send_buf, recv_buf, send_sems, recv_sems):
        k, j = pl.program_id(0), pl.program_id(1)
        x, y, c, _ = _mesh_place()
        sibling, _ = _peer(x, y, c, 1)
        bv = b_ref[...]
        if scale != 1.0:
            bv = bv * scale
        bb = bv.astype(BF16)
        acc_half = acc.at[j]

        @pl.when(k == 0)
        def _():
            acc_half[...] = jnp.zeros_like(acc_half)

        for i in range(m // MXU_ROWS):
            rows = slice(i * MXU_ROWS, (i + 1) * MXU_ROWS)
            acc_half[rows, :] += _dot(a_ref[:, rows].astype(BF16), bb, TN)

        def to_sibling(half):
            return _remote(send_buf.at[half], recv_buf.at[half], send_sems.at[half], recv_sems.at[half], sibling)

        def owned_rows(q, core):
            return pl.ds(pl.multiple_of((2 * q + core) * r, 8), r)

        for half in range(halves):
            @pl.when(jnp.logical_and(k == nk - 1, j == half))
            def _():
                for q in range(chips):
                    send_buf[half, q] = acc[half, owned_rows(q, 1 - c), :].astype(BF16)
                to_sibling(half).start()

        @pl.when(jnp.logical_and(k == nk - 1, j == halves - 1))
        def _():
            for half in range(halves):
                to_sibling(half).wait_send()
                to_sibling(half).wait_recv()
                for q in range(chips):
                    o_ref[q, :, half * nb:(half + 1) * nb] = (
                        acc[half, owned_rows(q, c), :] + recv_buf[half, q].astype(F32)).astype(BF16)

    (partial,), arrived = _call(
        body,
        name="weight_grad",
        grid=(nk, halves),
        in_specs=[pl.BlockSpec((tk, m), lambda k, j: (k, 0)), pl.BlockSpec((tk, nb), lambda k, j: (k, j))],
        out_specs=[pl.BlockSpec((chips, r, n), lambda k, j: (0, 0, 0))],
        out_shape=[jax.ShapeDtypeStruct((chips, r, n), BF16)],
        scratch_shapes=[
            pltpu.VMEM((halves, m, nb), F32),
            pltpu.VMEM((halves, chips, r, nb), BF16), pltpu.VMEM((halves, chips, r, nb), BF16),
            pltpu.SemaphoreType.DMA((halves,)), pltpu.SemaphoreType.DMA((halves,)),
        ],
        args=(a, b),
        exchange=exchange,
    )
    return partial, arrived


def _chunk_cumsum(v, reverse=False):
    n, width = v.shape
    row = lax.broadcasted_iota(jnp.int32, (n, n), 0)
    col = lax.broadcasted_iota(jnp.int32, (n, n), 1)
    earlier = col >= row if reverse else col <= row
    tri = jnp.where(jnp.logical_and(row // CHUNK == col // CHUNK, earlier), 1.0, 0.0).astype(BF16)
    hi = v.astype(BF16)
    rest = v - hi.astype(F32)
    mid = rest.astype(BF16)
    low = (rest - mid.astype(F32)).astype(BF16)
    sums = _dot(tri, jnp.concatenate([hi, mid, low], axis=1))
    return sums[:, 0:width] + sums[:, width:2 * width] + sums[:, 2 * width:3 * width]


def _shift_rows(v, shift, edge):
    n = v.shape[0]
    row = lax.broadcasted_iota(jnp.int32, (n, 1), 0)
    out = pltpu.roll(v, shift % n, axis=0)
    if shift > 0:
        for j in range(shift):
            out = jnp.where(row == j, edge[8 - shift + j:8 - shift + j + 1, :], out)
    else:
        for j in range(-shift):
            out = jnp.where(row == n + shift + j, edge[j:j + 1, :], out)
    return out


def _gates(z, lbp):
    w = HGRN_W
    lb = _sigmoid(lbp[0:1, :] - lbp[1:2, :])
    zq = z[:, 0:w]
    sig = _sigmoid(z[:, w:2 * w])
    f = lb + (1.0 - lb) * sig
    sq = _sigmoid(zq)
    q = zq * sq * HGRN_DK ** -0.5
    return lb, sig, f, sq, q


def _decayed_operands(q, f, v, qh_buf, kh_buf, kbar_buf, v_buf, etot_buf):
    logf = jnp.log(f)
    bcum = _chunk_cumsum(logf)
    rest = _chunk_cumsum(logf, reverse=True) - logf
    eb, enb, erest = jnp.exp(bcum), jnp.exp(-bcum), jnp.exp(rest)
    kk = 1.0 - f
    qh_buf[...] = (q * eb).astype(BF16)
    kh_buf[...] = (kk * enb).astype(BF16)
    kbar_buf[...] = (kk * erest).astype(BF16)
    v_buf[...] = v.astype(BF16)
    etot_buf[...] = jnp.exp(bcum + rest)
    return eb, enb, erest


def _short_conv(u, edge, cw):
    return cw[0:1, :] * _shift_rows(u, 2, edge) + cw[1:2, :] * _shift_rows(u, 1, edge) + cw[2:3, :] * u


def _block_causal_mask(n):
    row = lax.broadcasted_iota(jnp.int32, (n, n), 0)
    col = lax.broadcasted_iota(jnp.int32, (n, n), 1)
    return jnp.logical_and(row // CHUNK == col // CHUNK, col <= row)


def _spread(v, chunk_of_row, nc):
    return jnp.concatenate([jnp.where(chunk_of_row == c, v, jnp.zeros_like(v)) for c in range(nc)], axis=1)


def _pick(r, chunk_of_row, nc):
    out = jnp.where(chunk_of_row == 0, r[:, 0:HGRN_DK], 0.0)
    for c in range(1, nc):
        out = out + jnp.where(chunk_of_row == c, r[:, c * HGRN_DK:(c + 1) * HGRN_DK], 0.0)
    return out


def _mix_fwd(x, g, w_in, lbp, gh, convw_t, w_out, exchange=None):
    t, d = x.shape
    zw = w_in.shape[0]
    w = HGRN_W
    tm = min(TOKEN_TILE, t)
    nc = tm // CHUNK
    n_chunks = t // CHUNK

    def body(x_ref, g_ref, win_ref, lbp_ref, gh_ref, cw_ref, wout_ref,
             xo_ref, z_ref, o_ref, st_ref, y_ref, state, ucarry, qh_buf, kh_buf, kbar_buf, v_buf, etot_buf):
        @pl.when(pl.program_id(0) == 0)
        def _():
            state[...] = jnp.zeros_like(state)
            ucarry[...] = jnp.zeros_like(ucarry)

        xv = x_ref[...]
        h, _, _ = _rms(xv, g_ref[...])
        z_ref[...] = _dot(h.astype(BF16), win_ref[...], NT)
        z = z_ref[...]
        _, _, f, _, q = _gates(z, lbp_ref[...])
        _decayed_operands(q, f, z[:, 2 * w:3 * w], qh_buf, kh_buf, kbar_buf, v_buf, etot_buf)
        mask = _block_causal_mask(tm)
        chunk_of_row = lax.broadcasted_iota(jnp.int32, (tm, 1), 0) // CHUNK
        for hd in range(HGRN_HEADS):
            cols = slice(hd * HGRN_DK, (hd + 1) * HGRN_DK)
            qh, kh, kbar, vb = qh_buf[:, cols], kh_buf[:, cols], kbar_buf[:, cols], v_buf[:, cols]
            scores = jnp.where(mask, _dot(qh, kh, NT), 0.0).astype(BF16)
            gains = _dot(_spread(vb, chunk_of_row, nc), kbar, TN)
            entering = []
            st = state[hd]
            for c in range(nc):
                entering.append(st)
                st_ref[c, hd] = st
                st = st * etot_buf[c * CHUNK:c * CHUNK + 1, cols] + gains[c * HGRN_DK:(c + 1) * HGRN_DK, :]
            state[hd] = st
            from_states = _dot(qh, jnp.concatenate(entering, axis=0).astype(BF16), NT)
            o_ref[:, cols] = _dot(scores, vb) + _pick(from_states, chunk_of_row, nc)
        ghv = gh_ref[...]
        for hd in range(HGRN_HEADS):
            cols = slice(hd * HGRN_DK, (hd + 1) * HGRN_DK)
            on, _, _ = _rms(o_ref[:, cols], ghv[:, cols])
            zg = z[:, 3 * w + hd * HGRN_DK:3 * w + (hd + 1) * HGRN_DK]
            y_ref[:, cols] = (on * (zg * _sigmoid(zg))).astype(BF16)
        u = z[:, 5 * w:6 * w] * z[:, 6 * w:7 * w]
        conv = _short_conv(u, ucarry[...], cw_ref[...])
        ucarry[...] = u[tm - 8:tm, :]
        y_ref[:, w:2 * w] = (z[:, 4 * w:5 * w] * conv).astype(BF16)
        xo_ref[...] = xv + _dot(y_ref[...], wout_ref[...])

    return _call(
        body,
        name="mix_fwd",
        grid=(t // tm,),
        in_specs=[
            _rows(tm, d), _full((1, d)), _full((zw, d)), _full((2, w)), _full((1, w)), _full((3, w)),
            _full((2 * w, d)),
        ],
        out_specs=[
            _rows(tm, d), _rows(tm, zw), _rows(tm, w),
            pl.BlockSpec((nc, HGRN_HEADS, HGRN_DK, HGRN_DK), lambda i: (i, 0, 0, 0)),
            _rows(tm, 2 * w),
        ],
        out_shape=[
            jax.ShapeDtypeStruct((t, d), F32),
            jax.ShapeDtypeStruct((t, zw), F32),
            jax.ShapeDtypeStruct((t, w), F32),
            jax.ShapeDtypeStruct((n_chunks, HGRN_HEADS, HGRN_DK, HGRN_DK), F32),
            jax.ShapeDtypeStruct((t, 2 * w), BF16),
        ],
        scratch_shapes=[
            pltpu.VMEM((HGRN_HEADS, HGRN_DK, HGRN_DK), F32), pltpu.VMEM((8, w), F32),
            pltpu.VMEM((tm, w), BF16), pltpu.VMEM((tm, w), BF16), pltpu.VMEM((tm, w), BF16),
            pltpu.VMEM((tm, w), BF16), pltpu.VMEM((tm, w), F32),
        ],
        args=(x, g, w_in, lbp, gh, convw_t, w_out),
        exchange=exchange,
    )


def _mix_bwd(x, g, dxo, z, o, states, w_in, lbp, gh, convw_t, w_out, exchange=None):
    t, d = x.shape
    zw = w_in.shape[0]
    w = HGRN_W
    tm = min(TOKEN_TILE, t)
    nc = tm // CHUNK
    n = t // tm

    def body(x_ref, g_ref, dxo_ref, z_ref, zprev_ref, o_ref, st_ref, win_ref, lbp_ref, gh_ref, cw_ref, wout_ref,
             dx_ref, dz_ref, h_ref, dg_ref, dlbp_ref, dgh_ref, dcw_ref,
             dstate, dcarry, do_buf, dqh_buf, dkh_buf, dkbar_buf, carry_buf,
             qh_buf, kh_buf, kbar_buf, v_buf, etot_buf):
        first = pl.program_id(0) == 0

        @pl.when(first)
        def _():
            dstate[...] = jnp.zeros_like(dstate)
            dcarry[...] = jnp.zeros_like(dcarry)

        gv = g_ref[...]
        h, xh, r = _rms(x_ref[...], gv)
        h_ref[...] = h.astype(BF16)
        dxo = dxo_ref[...]
        dy = _dot(dxo.astype(BF16), wout_ref[...], NT)
        z = z_ref[...]
        lb, sig, f, sq, q = _gates(z, lbp_ref[...])
        eb, enb, erest = _decayed_operands(q, f, z[:, 2 * w:3 * w], qh_buf, kh_buf, kbar_buf, v_buf, etot_buf)

        ghv = gh_ref[...]
        dgh_parts = []
        for hd in range(HGRN_HEADS):
            cols = slice(hd * HGRN_DK, (hd + 1) * HGRN_DK)
            gcols = slice(3 * w + hd * HGRN_DK, 3 * w + (hd + 1) * HGRN_DK)
            on, oh, rr = _rms(o_ref[:, cols], ghv[:, cols])
            zg = z[:, gcols]
            sgz = _sigmoid(zg)
            dyh = dy[:, cols]
            don = dyh * (zg * sgz)
            dz_ref[:, gcols] = (dyh * on * (sgz * (1.0 + zg * (1.0 - sgz)))).astype(BF16)
            dgh_parts.append(jnp.sum(don * oh, axis=0, keepdims=True))
            do_buf[:, cols] = _rms_bwd(don, oh, rr, ghv[:, cols]).astype(BF16)
        _accumulate(dgh_ref, first, jnp.concatenate(dgh_parts, axis=1))

        zb = z[:, 4 * w:5 * w]
        zc = z[:, 5 * w:6 * w]
        zu = z[:, 6 * w:7 * w]
        u = zc * zu
        cw = cw_ref[...]
        zp = zprev_ref[...]
        uprev = jnp.where(pl.program_id(0) == n - 1, 0.0, zp[:, 5 * w:6 * w] * zp[:, 6 * w:7 * w])
        dyc = dy[:, w:2 * w]
        dz_ref[:, 4 * w:5 * w] = (dyc * _short_conv(u, uprev, cw)).astype(BF16)
        dconv = dyc * zb
        edge = dcarry[...]
        dconv1 = _shift_rows(dconv, -1, edge)
        dconv2 = _shift_rows(dconv, -2, edge)
        dcarry[...] = dconv[0:8, :]
        du = cw[2:3, :] * dconv + cw[1:2, :] * dconv1 + cw[0:1, :] * dconv2
        dz_ref[:, 5 * w:6 * w] = (du * zu).astype(BF16)
        dz_ref[:, 6 * w:7 * w] = (du * zc).astype(BF16)
        _accumulate(dcw_ref, first, jnp.concatenate([
            jnp.sum(u * dconv2, axis=0, keepdims=True),
            jnp.sum(u * dconv1, axis=0, keepdims=True),
            jnp.sum(u * dconv, axis=0, keepdims=True)], axis=0))

        mask = _block_causal_mask(tm)
        chunk_of_row = lax.broadcasted_iota(jnp.int32, (tm, 1), 0) // CHUNK
        for hd in range(HGRN_HEADS):
            cols = slice(hd * HGRN_DK, (hd + 1) * HGRN_DK)
            qhb, khb, kbarb, vb = qh_buf[:, cols], kh_buf[:, cols], kbar_buf[:, cols], v_buf[:, cols]
            dob = do_buf[:, cols]
            scores = jnp.where(mask, _dot(qhb, khb, NT), 0.0).astype(BF16)
            dscores = jnp.where(mask, _dot(dob, vb, NT), 0.0).astype(BF16)
            gains = _dot(_spread(dob, chunk_of_row, nc), qhb, TN)
            entering = [st_ref[c, hd] for c in range(nc)]
            leaving = [None] * nc
            dst = dstate[hd]
            for c in reversed(range(nc)):
                elast = etot_buf[c * CHUNK:c * CHUNK + 1, cols]
                leaving[c] = dst
                carry_buf[c:c + 1, cols] = jnp.sum(dst * entering[c], axis=0, keepdims=True) * elast
                dst = dst * elast + gains[c * HGRN_DK:(c + 1) * HGRN_DK, :]
            dstate[hd] = dst
            dst_rows = jnp.concatenate(leaving, axis=0).astype(BF16)
            dst_lanes = jnp.concatenate(leaving, axis=1).astype(BF16)
            st_lanes = jnp.concatenate(entering, axis=1).astype(BF16)
            dv = _dot(scores, dob, TN) + _pick(_dot(kbarb, dst_rows, NT), chunk_of_row, nc)
            dz_ref[:, 2 * w + hd * HGRN_DK:2 * w + (hd + 1) * HGRN_DK] = dv.astype(BF16)
            dqh_buf[:, cols] = _dot(dscores, khb) + _pick(_dot(dob, st_lanes), chunk_of_row, nc)
            dkh_buf[:, cols] = _dot(dscores, qhb, TN)
            dkbar_buf[:, cols] = _pick(_dot(vb, dst_lanes), chunk_of_row, nc)

        dqh, dkh, dkbar = dqh_buf[...], dkh_buf[...], dkbar_buf[...]
        kbar_dkbar = kbar_buf[...].astype(F32) * dkbar
        db = qh_buf[...].astype(F32) * dqh - kh_buf[...].astype(F32) * dkh - kbar_dkbar
        through_last = jnp.concatenate([
            jnp.broadcast_to(
                jnp.sum(kbar_dkbar[c * CHUNK:(c + 1) * CHUNK], axis=0, keepdims=True) + carry_buf[c:c + 1, :],
                (CHUNK, w))
            for c in range(nc)], axis=0)
        dlogf = _chunk_cumsum(db, reverse=True) + through_last
        df = dlogf / f - (dkh * enb + dkbar * erest)
        zq = z[:, 0:w]
        dz_ref[:, 0:w] = (dqh * eb * HGRN_DK ** -0.5 * (sq * (1.0 + zq * (1.0 - sq)))).astype(BF16)
        dz_ref[:, w:2 * w] = (df * (1.0 - lb) * sig * (1.0 - sig)).astype(BF16)
        dlb = jnp.sum(df * (1.0 - sig), axis=0, keepdims=True) * lb * (1.0 - lb)
        _accumulate(dlbp_ref, first, jnp.concatenate([dlb, -dlb], axis=0))

        dh = _dot(dz_ref[...], win_ref[...])
        dx_ref[...] = _rms_bwd(dh, xh, r, gv) + dxo
        _accumulate(dg_ref, first, jnp.sum(dh * xh, axis=0, keepdims=True))

    return _call(
        body,
        name="mix_bwd",
        grid=(n,),
        in_specs=[
            _rows_rev(tm, d, n), _full((1, d)), _rows_rev(tm, d, n), _rows_rev(tm, zw, n),
            pl.BlockSpec((8, zw), lambda i: (jnp.maximum((n - 1 - i) * (tm // 8) - 1, 0), 0)),
            _rows_rev(tm, w, n),
            pl.BlockSpec((nc, HGRN_HEADS, HGRN_DK, HGRN_DK), lambda i: (n - 1 - i, 0, 0, 0)),
            _full((zw, d)), _full((2, w)), _full((1, w)), _full((3, w)), _full((2 * w, d)),
        ],
        out_specs=[
            _rows_rev(tm, d, n), _rows_rev(tm, zw, n), _rows_rev(tm, d, n),
            _full((1, d)), _full((2, w)), _full((1, w)), _full((3, w)),
        ],
        out_shape=[
            jax.ShapeDtypeStruct((t, d), F32),
            jax.ShapeDtypeStruct((t, zw), BF16),
            jax.ShapeDtypeStruct((t, d), BF16),
            jax.ShapeDtypeStruct((1, d), F32),
            jax.ShapeDtypeStruct((2, w), F32),
            jax.ShapeDtypeStruct((1, w), F32),
            jax.ShapeDtypeStruct((3, w), F32),
        ],
        scratch_shapes=[
            pltpu.VMEM((HGRN_HEADS, HGRN_DK, HGRN_DK), F32), pltpu.VMEM((8, w), F32),
            pltpu.VMEM((tm, w), BF16), pltpu.VMEM((tm, w), F32), pltpu.VMEM((tm, w), F32), pltpu.VMEM((tm, w), F32),
            pltpu.VMEM((8, w), F32),
            pltpu.VMEM((tm, w), BF16), pltpu.VMEM((tm, w), BF16), pltpu.VMEM((tm, w), BF16),
            pltpu.VMEM((tm, w), BF16), pltpu.VMEM((tm, w), F32),
        ],
        args=(x, g, dxo, z, z, o, states, w_in, lbp, gh, convw_t, w_out),
        exchange=exchange,
    )


def _memkv_fwd(mem, g, wkv):
    m, d = mem.shape
    nb, _, cb = wkv.shape

    def body(mem_ref, g_ref, wkv_ref, kv_ref):
        mn, _, _ = _rms(mem_ref[...], g_ref[...])
        mnb = mn.astype(BF16)
        for j in range(nb):
            kv_ref[:, j * cb:(j + 1) * cb] = _dot(mnb, wkv_ref[j]).astype(BF16)

    return pl.pallas_call(
        body,
        name="memkv_fwd",
        out_shape=jax.ShapeDtypeStruct((m, nb * cb), BF16),
        compiler_params=_params(),
    )(mem, g, wkv)


def _memkv_bwd(mem, g, dkv, wkv):
    m, d = mem.shape
    nb, _, cb = wkv.shape
    chips = nb // 2

    def body(mem_ref, g_ref, dkv_ref, wkv_ref, dw_ref, dg_ref, dw_all, send_buf, recv_buf, send_sem, recv_sem):
        x, y, c, _ = _mesh_place()
        sibling, _ = _peer(x, y, c, 1)
        mn, xh, _ = _rms(mem_ref[...], g_ref[...])
        mnb = mn.astype(BF16)
        dmn = jnp.zeros((m, d), F32)
        for j in range(nb):
            dkvb = dkv_ref[:, j * cb:(j + 1) * cb].astype(BF16)
            dw_all[j] = _dot(mnb, dkvb, TN)
            dmn = dmn + _dot(dkvb, wkv_ref[j], NT)
        dg_ref[...] = jnp.sum(dmn * xh, axis=0, keepdims=True)
        for q in range(chips):
            send_buf[q] = dw_all[2 * q + 1 - c].astype(BF16)
        to_sibling = _remote(send_buf, recv_buf, send_sem, recv_sem, sibling)
        to_sibling.start()
        to_sibling.wait_send()
        to_sibling.wait_recv()
        for q in range(chips):
            dw_ref[q] = (dw_all[2 * q + c] + recv_buf[q].astype(F32)).astype(BF16)

    return pl.pallas_call(
        body,
        name="memkv_bwd",
        out_shape=[jax.ShapeDtypeStruct((chips, d, cb), BF16), jax.ShapeDtypeStruct((1, d), F32)],
        scratch_shapes=[
            pltpu.VMEM((nb, d, cb), F32), pltpu.VMEM((chips, d, cb), BF16), pltpu.VMEM((chips, d, cb), BF16),
            pltpu.SemaphoreType.DMA, pltpu.SemaphoreType.DMA,
        ],
        compiler_params=_params(),
    )(mem, g, dkv, wkv)


def _softmax_rows(qm_h, k_h):
    sc = _dot(qm_h, k_h, NT) * MEM_HD ** -0.5
    e = jnp.exp(sc - jnp.max(sc, axis=-1, keepdims=True))
    return e / jnp.sum(e, axis=-1, keepdims=True)


def _xattn_fwd(x, g, wq, kv, wo, exchange=None):
    t, d = x.shape
    m = kv.shape[0]
    tm = min(TOKEN_TILE, t)

    def body(x_ref, g_ref, wq_ref, kv_ref, wo_ref, xo_ref, hq_ref, qm_ref, att_ref):
        xv = x_ref[...]
        h, _, _ = _rms(xv, g_ref[...])
        hq_ref[...] = h.astype(BF16)
        qm_ref[...] = _dot(hq_ref[...], wq_ref[...]).astype(BF16)
        for hd in range(MEM_HEADS):
            cols = slice(hd * MEM_HD, (hd + 1) * MEM_HD)
            p = _softmax_rows(qm_ref[:, cols], kv_ref[:, cols])
            att_ref[:, cols] = _dot(p.astype(BF16), kv_ref[:, d + hd * MEM_HD:d + (hd + 1) * MEM_HD]).astype(BF16)
        xo_ref[...] = xv + _dot(att_ref[...], wo_ref[...])

    return _call(
        body,
        name="xattn_fwd",
        grid=(t // tm,),
        in_specs=[_rows(tm, d), _full((1, d)), _full((d, d)), _full((m, 2 * d)), _full((d, d))],
        out_specs=[_rows(tm, d), _rows(tm, d), _rows(tm, d), _rows(tm, d)],
        out_shape=[
            jax.ShapeDtypeStruct((t, d), F32),
            jax.ShapeDtypeStruct((t, d), BF16),
            jax.ShapeDtypeStruct((t, d), BF16),
            jax.ShapeDtypeStruct((t, d), BF16),
        ],
        args=(x, g, wq, kv, wo),
        exchange=exchange,
    )


def _xattn_bwd(x, g, dxo, qm, kv, wq, wo, exchange=None):
    t, d = x.shape
    m = kv.shape[0]
    tm = min(TOKEN_TILE, t)

    def body(x_ref, g_ref, dxo_ref, qm_ref, kv_ref, wq_ref, wo_ref, dx_ref, dqm_ref, dkv_ref, dg_ref):
        first = pl.program_id(0) == 0

        @pl.when(first)
        def _():
            dkv_ref[...] = jnp.zeros_like(dkv_ref)

        gv = g_ref[...]
        _, xh, r = _rms(x_ref[...], gv)
        dxo = dxo_ref[...]
        datt = _dot(dxo.astype(BF16), wo_ref[...], NT).astype(BF16)
        for hd in range(MEM_HEADS):
            cols = slice(hd * MEM_HD, (hd + 1) * MEM_HD)
            vcols = slice(d + hd * MEM_HD, d + (hd + 1) * MEM_HD)
            qm_h = qm_ref[:, cols]
            p = _softmax_rows(qm_h, kv_ref[:, cols])
            datt_h = datt[:, cols]
            dp = _dot(datt_h, kv_ref[:, vcols], NT)
            dsc = (p * (dp - jnp.sum(p * dp, axis=-1, keepdims=True)) * MEM_HD ** -0.5).astype(BF16)
            dqm_ref[:, cols] = _dot(dsc, kv_ref[:, cols]).astype(BF16)
            dkv_ref[:, cols] += _dot(dsc, qm_h, TN)
            dkv_ref[:, vcols] += _dot(p.astype(BF16), datt_h, TN)
        dh = _dot(dqm_ref[...], wq_ref[...], NT)
        dx_ref[...] = _rms_bwd(dh, xh, r, gv) + dxo
        _accumulate(dg_ref, first, jnp.sum(dh * xh, axis=0, keepdims=True))

    return _call(
        body,
        name="xattn_bwd",
        grid=(t // tm,),
        in_specs=[
            _rows(tm, d), _full((1, d)), _rows(tm, d), _rows(tm, d), _full((m, 2 * d)), _full((d, d)), _full((d, d)),
        ],
        out_specs=[_rows(tm, d), _rows(tm, d), _full((m, 2 * d)), _full((1, d))],
        out_shape=[
            jax.ShapeDtypeStruct((t, d), F32),
            jax.ShapeDtypeStruct((t, d), BF16),
            jax.ShapeDtypeStruct((m, 2 * d), F32),
            jax.ShapeDtypeStruct((1, d), F32),
        ],
        args=(x, g, dxo, qm, kv, wq, wo),
        exchange=exchange,
    )


def _mesh_place():
    x, y, c = lax.axis_index("x"), lax.axis_index("y"), lax.axis_index("c")
    return x, y, c, 4 * x + 2 * y + c


def _peer(x, y, c, k):
    px = 1 - x if k & 4 else x
    py = 1 - y if k & 2 else y
    pc = 1 - c if k & 1 else c
    return (px, py, pc), 4 * px + 2 * py + pc


ICI_HOPS = (2, 4, 6)
N_HOPS = len(ICI_HOPS)


def _remote(src, dst, send_sem, recv_sem, peer):
    return pltpu.make_async_remote_copy(
        src_ref=src, dst_ref=dst, send_sem=send_sem, recv_sem=recv_sem, device_id=peer, device_id_type=MESH_IDS)


def _gather_exchange(shards):
    n = len(shards)

    def place():
        x, y, c, me = _mesh_place()
        sibling, _ = _peer(x, y, c, 1)
        to_x, from_x = _peer(x, y, c, 4)
        to_y, from_y = _peer(x, y, c, 2)
        _, from_diagonal = _peer(x, y, c, 6)
        onward = (c * to_y[0] + (1 - c) * to_x[0], c * to_y[1] + (1 - c) * to_x[1], c)
        passed_on = c * from_x + (1 - c) * from_y
        return me, sibling, (to_x, to_y, onward), (from_x, from_y, from_diagonal), passed_on

    def start(src, dst, sems):
        ici_send, ici_recv, pair_send, pair_recv, local = sems
        me, sibling, targets, _, _ = place()
        for a in range(n):
            pltpu.make_async_copy(src[a], dst[a].at[me], local.at[a]).start()
            for j in range(2):
                _remote(src[a], dst[a].at[me], ici_send.at[a, j], ici_recv.at[a, j], targets[j]).start()
            _remote(src[a], dst[a].at[me], pair_send.at[a, 0], pair_recv.at[a, 0], sibling).start()

    def to_sibling(dst, sems, a, j, origin, sibling):
        _, _, pair_send, pair_recv, _ = sems
        slot = dst[a].at[origin]
        return _remote(slot, slot, pair_send.at[a, 1 + j], pair_recv.at[a, 1 + j], sibling)

    def middle(src, dst, sems):
        ici_send, ici_recv, _, _, _ = sems
        _, sibling, targets, origins, passed_on = place()
        for a in range(n):
            for j in range(2):
                _remote(src[a], dst[a].at[origins[j]], ici_send.at[a, j], ici_recv.at[a, j], targets[j]).wait_recv()
            slot = dst[a].at[passed_on]
            _remote(slot, slot, ici_send.at[a, 2], ici_recv.at[a, 2], targets[2]).start()
            for j in range(2):
                to_sibling(dst, sems, a, j, origins[j], sibling).start()

    def finish(src, dst, sems):
        ici_send, ici_recv, pair_send, pair_recv, local = sems
        me, sibling, targets, origins, _ = place()
        for a in range(n):
            _remote(src[a], dst[a].at[origins[2]], ici_send.at[a, 2], ici_recv.at[a, 2], targets[2]).wait_recv()
            to_sibling(dst, sems, a, 2, origins[2], sibling).start()
        for a in range(n):
            pltpu.make_async_copy(src[a], dst[a].at[me], local.at[a]).wait()
            for j in range(N_HOPS):
                _remote(src[a], dst[a].at[me], ici_send.at[a, j], ici_recv.at[a, j], targets[j]).wait_send()
            for j, origin in enumerate((me,) + origins):
                from_sibling = origin + 1 - 2 * (origin % 2)
                passed = _remote(src[a], dst[a].at[from_sibling], pair_send.at[a, j], pair_recv.at[a, j], sibling)
                passed.wait_send()
                passed.wait_recv()

    return _Exchange(
        shards,
        [jax.ShapeDtypeStruct((N_DEV,) + s.shape, s.dtype) for s in shards],
        [
            pltpu.SemaphoreType.DMA((n, N_HOPS)), pltpu.SemaphoreType.DMA((n, N_HOPS)),
            pltpu.SemaphoreType.DMA((n, N_HOPS + 1)), pltpu.SemaphoreType.DMA((n, N_HOPS + 1)),
            pltpu.SemaphoreType.DMA((n,)),
        ],
        start, finish, middle)


def _scatter_copies(src, dst, sems, n, arrivals=False):
    send, recv, local = sems
    x, y, c, _ = _mesh_place()
    chip = 2 * x + y
    if arrivals is None:
        return [pltpu.make_async_copy(src[a].at[chip], dst[a].at[chip], local.at[a]) for a in range(n)]
    copies = []
    for a in range(n):
        for j, k in enumerate(ICI_HOPS):
            peer, _ = _peer(x, y, c, k)
            peer_chip = 2 * peer[0] + peer[1]
            slot = dst[a].at[peer_chip if arrivals else chip]
            copies.append(_remote(src[a].at[peer_chip], slot, send.at[a, j], recv.at[a, j], peer))
    return copies


def _scatter_start(src, dst, sems, n):
    for cp in _scatter_copies(src, dst, sems, n, arrivals=None) + _scatter_copies(src, dst, sems, n):
        cp.start()


def _scatter_finish(src, dst, sems, n):
    for cp in _scatter_copies(src, dst, sems, n, arrivals=None):
        cp.wait()
    for cp in _scatter_copies(src, dst, sems, n):
        cp.wait_send()
    for cp in _scatter_copies(src, dst, sems, n, arrivals=True):
        cp.wait_recv()


def _scatter_scratch(n):
    return [pltpu.SemaphoreType.DMA((n, N_HOPS)), pltpu.SemaphoreType.DMA((n, N_HOPS)), pltpu.SemaphoreType.DMA((n,))]


def _scatter_exchange(partials):
    n = len(partials)
    return _Exchange(
        partials, [jax.ShapeDtypeStruct(p.shape, p.dtype) for p in partials], _scatter_scratch(n),
        lambda src, dst, sems: _scatter_start(src, dst, sems, n),
        lambda src, dst, sems: _scatter_finish(src, dst, sems, n))


SMALL_LAYOUT = {
    "ffn1_norm": (0, 1, 1024), "mix_norm": (1, 1, 1024), "xattn_norm": (2, 1, 1024), "mem_norm": (3, 1, 1024),
    "ffn2_norm": (4, 1, 1024), "final_norm": (5, 1, 1024), "lb_param": (6, 2, 512), "hgrn_out_norm": (8, 1, 512),
    "conv_w": (9, 3, 512), "loss": (12, 1, 128),
}


def _final_exchange(partials, small):
    n = len(partials)
    names = list(small)
    width = 1024

    def body(*refs):
        src = refs[:n]
        pieces = refs[n:n + len(names)]
        dst = refs[n + len(names):2 * n + len(names)]
        total_ref = refs[2 * n + len(names)]
        pack, gathered, small_send, small_recv = refs[2 * n + len(names) + 1:2 * n + len(names) + 5]
        sems = refs[2 * n + len(names) + 5:]
        x, y, c, me = _mesh_place()
        pack[...] = jnp.zeros_like(pack)
        for name, piece in zip(names, pieces):
            row, nrows, ncols = SMALL_LAYOUT[name]
            pack[row:row + nrows, 0:ncols] = piece[...]
        for k in range(1, N_DEV):
            peer, _ = _peer(x, y, c, k)
            _remote(pack, gathered.at[me], small_send.at[k - 1], small_recv.at[k - 1], peer).start()
        _scatter_start(src, dst, sems, n)
        gathered[me] = pack[...]
        for k in range(1, N_DEV):
            peer, peer_index = _peer(x, y, c, k)
            landed = _remote(pack, gathered.at[peer_index], small_send.at[k - 1], small_recv.at[k - 1], peer)
            landed.wait_send()
            landed.wait_recv()
        total = gathered[0]
        for j in range(1, N_DEV):
            total = total + gathered[j]
        total_ref[...] = total
        _scatter_finish(src, dst, sems, n)

    hbm = pl.BlockSpec(memory_space=pltpu.HBM)
    vmem = pl.BlockSpec(memory_space=pltpu.VMEM)
    out = pl.pallas_call(
        body,
        name="final_exchange",
        in_specs=[hbm] * n + [vmem] * len(names),
        out_specs=[hbm] * n + [vmem],
        out_shape=[jax.ShapeDtypeStruct(p.shape, p.dtype) for p in partials]
        + [jax.ShapeDtypeStruct((SMALL_ROWS, width), F32)],
        scratch_shapes=[
            pltpu.VMEM((SMALL_ROWS, width), F32), pltpu.VMEM((N_DEV, SMALL_ROWS, width), F32),
            pltpu.SemaphoreType.DMA((N_DEV - 1,)), pltpu.SemaphoreType.DMA((N_DEV - 1,)),
        ] + _scatter_scratch(n),
        compiler_params=pltpu.CompilerParams(has_side_effects=True),
    )(*partials, *[small[k] for k in names])
    return out[:n], out[n]


def _adamw_math(w, g, m, v):
    m = ADAM_B1 * m + (1.0 - ADAM_B1) * g
    v = ADAM_B2 * v + (1.0 - ADAM_B2) * (g * g)
    m_hat = m / (1.0 - ADAM_B1 ** ADAM_STEP)
    v_hat = v / (1.0 - ADAM_B2 ** ADAM_STEP)
    delta = -ADAM_LR * (m_hat / (jnp.sqrt(v_hat) + ADAM_EPS) + ADAM_WD * w)
    return delta, m, v


def _adamw_shard(parts, w, m, v):
    r, c = w.shape
    n_parts = parts.shape[0]
    tr = max(rows for rows in range(16, r + 1, 16) if r % rows == 0 and rows * c <= ADAMW_TILE_ELEMENTS)

    def body(p_ref, w_ref, m_ref, v_ref, g_ref, d_ref, mo_ref, vo_ref):
        g = p_ref[0].astype(F32)
        for j in range(1, n_parts):
            g = g + p_ref[j].astype(F32)
        delta, mn, vn = _adamw_math(w_ref[...], g, m_ref[...], v_ref[...])
        g_ref[...] = g
        d_ref[...] = delta
        mo_ref[...] = mn
        vo_ref[...] = vn

    tile = pl.BlockSpec((tr, c), lambda i: (i, 0))
    return pl.pallas_call(
        body,
        name="adamw_shard",
        grid=(r // tr,),
        in_specs=[pl.BlockSpec((n_parts, tr, c), lambda i: (0, i, 0)), tile, tile, tile],
        out_specs=[tile] * 4,
        out_shape=[jax.ShapeDtypeStruct((r, c), F32)] * 4,
        compiler_params=_params(("parallel",)),
    )(parts, w, m, v)


def _adamw_small(gs, ws, ms, vs):
    n = len(gs)

    def body(*refs):
        g_refs, w_refs, m_refs, v_refs = refs[:n], refs[n:2 * n], refs[2 * n:3 * n], refs[3 * n:4 * n]
        d_out, m_out, v_out = refs[4 * n:5 * n], refs[5 * n:6 * n], refs[6 * n:7 * n]
        for i in range(n):
            delta, mn, vn = _adamw_math(w_refs[i][...], g_refs[i][...], m_refs[i][...], v_refs[i][...])
            d_out[i][...] = delta
            m_out[i][...] = mn
            v_out[i][...] = vn

    shapes = [jax.ShapeDtypeStruct(w.shape, F32) for w in ws]
    out = pl.pallas_call(
        body,
        name="adamw_small",
        out_shape=shapes * 3,
        compiler_params=_params(),
    )(*gs, *ws, *ms, *vs)
    return out[:n], out[n:2 * n], out[2 * n:]


TRANSPOSED = ("ffn1_gate", "ffn1_up", "w_in", "ffn2_gate", "ffn2_up", "conv_w")
GROUP_FFN1 = ("ffn1_gate", "ffn1_up", "ffn1_down")
GROUP_MIX = ("w_in", "w_out")
GROUP_XATTN = ("w_q_mem", "w_kv_mem", "w_o_mem")
GROUP_FFN2 = ("ffn2_gate", "ffn2_up", "ffn2_down")
LARGE = GROUP_FFN1 + GROUP_MIX + GROUP_XATTN + GROUP_FFN2
SMALL = ("ffn1_norm", "mix_norm", "lb_param", "hgrn_out_norm", "conv_w", "xattn_norm", "mem_norm", "ffn2_norm",
         "final_norm")
WEIGHTS = ("ffn1_norm", "ffn1_gate", "ffn1_up", "ffn1_down", "mix_norm", "w_in", "lb_param", "hgrn_out_norm",
           "conv_w", "w_out", "xattn_norm", "mem_norm", "w_q_mem", "w_kv_mem", "w_o_mem", "ffn2_norm", "ffn2_gate",
           "ffn2_up", "ffn2_down", "final_norm")


def kernel(x, mem, ffn1_norm, ffn1_gate, ffn1_up, ffn1_down, mix_norm, w_in, lb_param, hgrn_out_norm, conv_w, w_out, xattn_norm, mem_norm, w_q_mem, w_kv_mem, w_o_mem, ffn2_norm, ffn2_gate, ffn2_up, ffn2_down, final_norm, loss_target, m_ffn1_norm, m_ffn1_gate, m_ffn1_up, m_ffn1_down, m_mix_norm, m_w_in, m_lb_param, m_hgrn_out_norm, m_conv_w, m_w_out, m_xattn_norm, m_mem_norm, m_w_q_mem, m_w_kv_mem, m_w_o_mem, m_ffn2_norm, m_ffn2_gate, m_ffn2_up, m_ffn2_down, m_final_norm, v_ffn1_norm, v_ffn1_gate, v_ffn1_up, v_ffn1_down, v_mix_norm, v_w_in, v_lb_param, v_hgrn_out_norm, v_conv_w, v_w_out, v_xattn_norm, v_mem_norm, v_w_q_mem, v_w_kv_mem, v_w_o_mem, v_ffn2_norm, v_ffn2_gate, v_ffn2_up, v_ffn2_down, v_final_norm):
    given = dict(locals())
    me = 4 * lax.axis_index("x") + 2 * lax.axis_index("y") + lax.axis_index("c")
    x0, memv, target = x[0], mem[0], loss_target[0]

    def shard(prefix, name):
        v = given[prefix + name]
        if v.ndim == 1:
            return v.reshape(1, -1)
        if v.ndim == 2:
            return v
        return v[0].T if name in TRANSPOSED else v[0]

    w = {name: shard("", name) for name in WEIGHTS}
    m = {name: shard("m_", name) for name in WEIGHTS}
    v = {name: shard("v_", name) for name in WEIGHTS}

    conv_taps, conv_rows = w["conv_w"].shape
    conv_tile = jnp.pad(w["conv_w"], ((0, 8 - conv_taps), (0, 128 - conv_rows)))
    wire = {name: w[name].astype(BF16) for name in LARGE}
    full = {}

    def landed(names, gathered):
        for name, blocks in zip(names, gathered):
            _, r, c = blocks.shape
            full[name] = blocks if name == "w_kv_mem" else blocks.reshape(N_DEV * r, c)

    first = ("ffn1_gate", "ffn1_up")
    landed(first, _run_exchange(_gather_exchange([wire[k] for k in first]), "gather_first"))

    riders = (("ffn1_down", "w_in"), ("w_out", "w_kv_mem"), ("w_q_mem", "w_o_mem", "ffn2_gate", "ffn2_up"),
              ("ffn2_down",))
    (a1, b1, s1), gathered = _ffn_up(
        x0, w["ffn1_norm"], full["ffn1_gate"], full["ffn1_up"],
        exchange=_gather_exchange([wire[k] for k in riders[0]]))
    landed(riders[0], gathered)
    (x1,), gathered = _ffn_down(
        x0, s1, full["ffn1_down"], exchange=_gather_exchange([wire[k] for k in riders[1]] + [conv_tile]))
    landed(riders[1], gathered)
    convw_t = gathered[-1][:, :conv_taps, :conv_rows].transpose(1, 0, 2).reshape(conv_taps, N_DEV * conv_rows)
    (x2, z, o_raw, states, ycat), gathered = _mix_fwd(
        x1, w["mix_norm"], full["w_in"], w["lb_param"], w["hgrn_out_norm"], convw_t, full["w_out"],
        exchange=_gather_exchange([wire[k] for k in riders[2]]))
    landed(riders[2], gathered)
    kv = _memkv_fwd(memv, w["mem_norm"], full["w_kv_mem"])
    (x3, hq, qm, att), gathered = _xattn_fwd(
        x2, w["xattn_norm"], full["w_q_mem"], kv, full["w_o_mem"],
        exchange=_gather_exchange([wire[k] for k in riders[3]]))
    landed(riders[3], gathered)
    (dx4, a2, b2, s2, loss_part, d_final), _ = _ffn_fwd(
        x3, w["ffn2_norm"], full["ffn2_gate"], full["ffn2_up"], full["ffn2_down"], head=(w["final_norm"], target))

    parts = {}
    waiting = []

    def carried():
        names = [name for name, _ in waiting]
        exchange = _scatter_exchange([p for _, p in waiting]) if waiting else None
        del waiting[:]
        return names, exchange

    def weight_grad(name, a, b, scale=1.0):
        names, exchange = carried()
        partial, arrived = _weight_grad(a, b, scale, exchange=exchange)
        parts.update(zip(names, arrived))
        waiting.append((name, partial))

    (dx3, da2, db2, h4, d_ffn2_norm), _ = _ffn_bwd(
        x3, w["ffn2_norm"], dx4, a2, b2, full["ffn2_gate"], full["ffn2_up"], full["ffn2_down"])
    weight_grad("ffn2_down", s2, dx4, 0.5)
    weight_grad("ffn2_gate", da2, h4)
    weight_grad("ffn2_up", db2, h4)
    names, exchange = carried()
    (dx2, dqm, dkv, d_xattn_norm), arrived = _xattn_bwd(
        x2, w["xattn_norm"], dx3, qm, kv, full["w_q_mem"], full["w_o_mem"], exchange=exchange)
    parts.update(zip(names, arrived))
    d_wkv, d_mem_norm = _memkv_bwd(memv, w["mem_norm"], dkv, full["w_kv_mem"])
    waiting.append(("w_kv_mem", d_wkv))
    names, exchange = carried()
    (dx1, dz, h2, d_mix_norm, d_lbp, d_gh, d_convw_t), arrived = _mix_bwd(
        x1, w["mix_norm"], dx2, z, o_raw, states, full["w_in"], w["lb_param"], w["hgrn_out_norm"], convw_t,
        full["w_out"], exchange=exchange)
    parts.update(zip(names, arrived))
    weight_grad("w_in", dz, h2)
    weight_grad("ffn1_down", s1, dx1, 0.5)
    (dx0, da1, db1, h1, d_ffn1_norm), _ = _ffn_bwd(
        x0, w["ffn1_norm"], dx1, a1, b1, full["ffn1_gate"], full["ffn1_up"], full["ffn1_down"])
    weight_grad("ffn1_gate", da1, h1)
    weight_grad("ffn1_up", db1, h1)
    weight_grad("w_o_mem", att, dx3)
    weight_grad("w_q_mem", hq, dqm)
    weight_grad("w_out", ycat, dx2)

    small_parts = {
        "ffn1_norm": d_ffn1_norm, "mix_norm": d_mix_norm, "xattn_norm": d_xattn_norm, "mem_norm": d_mem_norm,
        "ffn2_norm": d_ffn2_norm, "final_norm": d_final, "lb_param": d_lbp, "hgrn_out_norm": d_gh,
        "conv_w": d_convw_t, "loss": loss_part,
    }
    names = [name for name, _ in waiting]
    arrived, total = _final_exchange([p for _, p in waiting], small_parts)
    parts.update(zip(names, arrived))

    g_out, d_out, m_out, v_out = {}, {}, {}, {}
    for name in LARGE:
        g_out[name], d_out[name], m_out[name], v_out[name] = _adamw_shard(parts[name], w[name], m[name], v[name])
    g_small = {}
    for name in SMALL:
        row, nrows, ncols = SMALL_LAYOUT[name]
        g_small[name] = total[row:row + nrows, 0:ncols]
    g_small["conv_w"] = lax.dynamic_slice_in_dim(g_small["conv_w"], me * conv_rows, conv_rows, axis=1)
    ds, ms, vs = _adamw_small(
        [g_small[k] for k in SMALL], [w[k] for k in SMALL], [m[k] for k in SMALL], [v[k] for k in SMALL])
    for i, name in enumerate(SMALL):
        g_out[name], d_out[name], m_out[name], v_out[name] = g_small[name], ds[i], ms[i], vs[i]

    def shaped(value, name):
        return (value.T if name in TRANSPOSED else value).reshape(given[name].shape)

    loss = total[SMALL_LAYOUT["loss"][0], 0]
    outs = [loss, dx0.reshape(x.shape)]
    for group in (g_out, d_out, m_out, v_out):
        outs += [shaped(group[name], name) for name in WEIGHTS]
    return tuple(outs)
```

```python
import jax
import jax.numpy as jnp
from jax import lax
from jax.experimental import pallas as pl
from jax.experimental.pallas import tpu as pltpu

F32 = jnp.float32
BF16 = jnp.bfloat16
MESH_IDS = pl.DeviceIdType.MESH

N_DEV = 8
EPS = 1e-6
HGRN_HEADS = 4
HGRN_DK = 128
HGRN_W = 512
CHUNK = 64
MEM_HEADS = 4
MEM_HD = 256
ADAM_LR = 0.001
ADAM_B1 = 0.9
ADAM_B2 = 0.999
ADAM_EPS = 1e-08
ADAM_WD = 0.01
ADAM_STEP = 10

TOKEN_TILE = 256
REDUCE_TILE = 1024
ADAMW_TILE_ELEMENTS = 256 * 1024
MIDDLE_EIGHTHS = 5
MXU_ROWS = 256
VMEM_LIMIT = 60 * 1024 * 1024
SMALL_ROWS = 16
NT = (((1,), (1,)), ((), ()))
TN = (((0,), (0,)), ((), ()))


def _params(sem=None):
    return pltpu.CompilerParams(dimension_semantics=sem, vmem_limit_bytes=VMEM_LIMIT)


def _dot(a, b, dims=None):
    if dims is None:
        return jnp.dot(a, b, preferred_element_type=F32)
    return lax.dot_general(a, b, dims, preferred_element_type=F32)


def _sigmoid(v):
    return 1.0 / (1.0 + jnp.exp(-v))


def _rms(x, g):
    r = lax.rsqrt(jnp.mean(x * x, axis=-1, keepdims=True) + EPS)
    xh = x * r
    return xh * g, xh, r


def _rms_bwd(dh, xh, r, g):
    dxh = dh * g
    return r * (dxh - xh * jnp.mean(dxh * xh, axis=-1, keepdims=True))


def _full(shape):
    return pl.BlockSpec(shape, lambda *_: (0,) * len(shape))


def _rows(tm, width):
    return pl.BlockSpec((tm, width), lambda i: (i, 0))


def _rows_rev(tm, width, n):
    return pl.BlockSpec((tm, width), lambda i: (n - 1 - i, 0))


def _zero_at_start(*refs):
    @pl.when(pl.program_id(0) == 0)
    def _():
        for ref in refs:
            ref[...] = jnp.zeros_like(ref)


class _Exchange:
    def __init__(self, operands, out_shapes, scratch, start, finish, middle=None):
        self.operands, self.out_shapes, self.scratch = list(operands), list(out_shapes), list(scratch)
        self.start, self.middle, self.finish = start, middle, finish


def _call(body, *, name, grid, in_specs, out_specs, out_shape, args, scratch_shapes=(), exchange=None):
    semantics = ("arbitrary",) * len(grid)
    if exchange is None:
        out = pl.pallas_call(
            body, name=name, grid=grid, in_specs=in_specs, out_specs=out_specs, out_shape=out_shape,
            scratch_shapes=list(scratch_shapes), compiler_params=_params(semantics))(*args)
        return out, []
    hbm = pl.BlockSpec(memory_space=pltpu.HBM)
    n_in, n_out, n_scr = len(in_specs), len(out_specs), len(scratch_shapes)
    e_in, e_out = len(exchange.operands), len(exchange.out_shapes)

    def carried(*refs):
        ins, rest = refs[:n_in], refs[n_in:]
        e_ins, rest = rest[:e_in], rest[e_in:]
        outs, rest = rest[:n_out], rest[n_out:]
        e_outs, rest = rest[:e_out], rest[e_out:]
        scr, e_scr = rest[:n_scr], rest[n_scr:]
        first = last = None
        for axis, size in enumerate(grid):
            at_start, at_end = pl.program_id(axis) == 0, pl.program_id(axis) == size - 1
            first = at_start if first is None else jnp.logical_and(first, at_start)
            last = at_end if last is None else jnp.logical_and(last, at_end)

        @pl.when(first)
        def _():
            exchange.start(e_ins, e_outs, e_scr)

        body(*ins, *outs, *scr)

        if exchange.middle is not None:
            assert len(grid) == 1

            @pl.when(pl.program_id(0) == (grid[0] * MIDDLE_EIGHTHS) // 8)
            def _():
                exchange.middle(e_ins, e_outs, e_scr)

        @pl.when(last)
        def _():
            exchange.finish(e_ins, e_outs, e_scr)

    out = pl.pallas_call(
        carried, name=name, grid=grid, in_specs=list(in_specs) + [hbm] * e_in,
        out_specs=list(out_specs) + [hbm] * e_out, out_shape=list(out_shape) + exchange.out_shapes,
        scratch_shapes=list(scratch_shapes) + exchange.scratch,
        compiler_params=pltpu.CompilerParams(
            dimension_semantics=semantics, vmem_limit_bytes=VMEM_LIMIT, has_side_effects=True),
    )(*args, *exchange.operands)
    return out[:n_out], out[n_out:]


def _run_exchange(exchange, name):
    hbm = pl.BlockSpec(memory_space=pltpu.HBM)
    e_in, e_out = len(exchange.operands), len(exchange.out_shapes)

    def body(*refs):
        e_ins, e_outs, e_scr = refs[:e_in], refs[e_in:e_in + e_out], refs[e_in + e_out:]
        exchange.start(e_ins, e_outs, e_scr)
        if exchange.middle is not None:
            exchange.middle(e_ins, e_outs, e_scr)
        exchange.finish(e_ins, e_outs, e_scr)

    return pl.pallas_call(
        body, name=name, in_specs=[hbm] * e_in, out_specs=[hbm] * e_out, out_shape=exchange.out_shapes,
        scratch_shapes=exchange.scratch, compiler_params=pltpu.CompilerParams(has_side_effects=True),
    )(*exchange.operands)


def _loss_head(xo, gf, tgt):
    d = xo.shape[1]
    y, xh, r = _rms(xo, gf)
    err = y - tgt
    dy = err * (1.0 / d)
    loss = 0.5 * jnp.sum(jnp.sum(err * err, axis=-1, keepdims=True) * (1.0 / d), axis=0, keepdims=True)
    return _rms_bwd(dy, xh, r, gf), loss, jnp.sum(dy * xh, axis=0, keepdims=True)


def _ffn_fwd(x, g, wg, wu, wd, exchange=None, head=None):
    t, d = x.shape
    f = wg.shape[0]
    tm = min(TOKEN_TILE, t)

    def body(x_ref, g_ref, wg_ref, wu_ref, wd_ref, *rest):
        if head is None:
            xo_ref, a_ref, b_ref, s_ref = rest
        else:
            gf_ref, tgt_ref, xo_ref, a_ref, b_ref, s_ref, loss_ref, dgf_ref = rest
            _zero_at_start(loss_ref, dgf_ref)
        xv = x_ref[...]
        h, _, _ = _rms(xv, g_ref[...])
        hb = h.astype(BF16)
        a = _dot(hb, wg_ref[...], NT)
        b = _dot(hb, wu_ref[...], NT)
        s = (a * _sigmoid(a) * b).astype(BF16)
        xo = xv + 0.5 * _dot(s, wd_ref[...])
        if head is None:
            xo_ref[...] = xo
        else:
            xo_ref[...], loss, dgf = _loss_head(xo, gf_ref[...], tgt_ref[...])
            loss_ref[...] += jnp.broadcast_to(loss, (1, 128))
            dgf_ref[...] += dgf
        a_ref[...] = a.astype(BF16)
        b_ref[...] = b.astype(BF16)
        s_ref[...] = s

    in_specs = [_rows(tm, d), _full((1, d)), _full((f, d)), _full((f, d)), _full((f, d))]
    out_specs = [_rows(tm, d), _rows(tm, f), _rows(tm, f), _rows(tm, f)]
    out_shape = [
        jax.ShapeDtypeStruct((t, d), F32),
        jax.ShapeDtypeStruct((t, f), BF16),
        jax.ShapeDtypeStruct((t, f), BF16),
        jax.ShapeDtypeStruct((t, f), BF16),
    ]
    args = (x, g, wg, wu, wd)
    if head is not None:
        in_specs += [_full((1, d)), _rows(tm, d)]
        out_specs += [_full((1, 128)), _full((1, d))]
        out_shape += [jax.ShapeDtypeStruct((1, 128), F32), jax.ShapeDtypeStruct((1, d), F32)]
        args += tuple(head)
    return _call(
        body, name="ffn_fwd", grid=(t // tm,), in_specs=in_specs, out_specs=out_specs, out_shape=out_shape,
        args=args, exchange=exchange)


def _ffn_up(x, g, wg, wu, exchange=None):
    t, d = x.shape
    f = wg.shape[0]
    tm = min(TOKEN_TILE, t)

    def body(x_ref, g_ref, wg_ref, wu_ref, a_ref, b_ref, s_ref):
        h, _, _ = _rms(x_ref[...], g_ref[...])
        hb = h.astype(BF16)
        a = _dot(hb, wg_ref[...], NT)
        b = _dot(hb, wu_ref[...], NT)
        a_ref[...] = a.astype(BF16)
        b_ref[...] = b.astype(BF16)
        s_ref[...] = (a * _sigmoid(a) * b).astype(BF16)

    return _call(
        body, name="ffn_up", grid=(t // tm,),
        in_specs=[_rows(tm, d), _full((1, d)), _full((f, d)), _full((f, d))],
        out_specs=[_rows(tm, f)] * 3, out_shape=[jax.ShapeDtypeStruct((t, f), BF16)] * 3,
        args=(x, g, wg, wu), exchange=exchange)


def _ffn_down(x, s, wd, exchange=None):
    t, d = x.shape
    f = wd.shape[0]
    tm = min(TOKEN_TILE, t)

    def body(x_ref, s_ref, wd_ref, xo_ref):
        xo_ref[...] = x_ref[...] + 0.5 * _dot(s_ref[...], wd_ref[...])

    return _call(
        body, name="ffn_down", grid=(t // tm,),
        in_specs=[_rows(tm, d), _rows(tm, f), _full((f, d))],
        out_specs=[_rows(tm, d)], out_shape=[jax.ShapeDtypeStruct((t, d), F32)],
        args=(x, s, wd), exchange=exchange)


def _ffn_bwd(x, g, dxo, a, b, wg, wu, wd, exchange=None):
    t, d = x.shape
    f = wg.shape[0]
    tm = min(TOKEN_TILE, t)

    def body(x_ref, g_ref, dxo_ref, a_ref, b_ref, wg_ref, wu_ref, wd_ref, dx_ref, da_ref, db_ref, h_ref, dg_ref):
        _zero_at_start(dg_ref)
        gv = g_ref[...]
        h, xh, r = _rms(x_ref[...], gv)
        dxo = dxo_ref[...]
        ds = _dot((0.5 * dxo).astype(BF16), wd_ref[...], NT)
        af = a_ref[...].astype(F32)
        bf = b_ref[...].astype(F32)
        sg = _sigmoid(af)
        da = (ds * bf * (sg * (1.0 + af * (1.0 - sg)))).astype(BF16)
        db = (ds * (af * sg)).astype(BF16)
        dh = _dot(da, wg_ref[...]) + _dot(db, wu_ref[...])
        dx_ref[...] = _rms_bwd(dh, xh, r, gv) + dxo
        da_ref[...] = da
        db_ref[...] = db
        h_ref[...] = h.astype(BF16)
        dg_ref[...] += jnp.sum(dh * xh, axis=0, keepdims=True)

    return _call(
        body,
        name="ffn_bwd",
        grid=(t // tm,),
        in_specs=[
            _rows(tm, d), _full((1, d)), _rows(tm, d), _rows(tm, f), _rows(tm, f),
            _full((f, d)), _full((f, d)), _full((f, d)),
        ],
        out_specs=[_rows(tm, d), _rows(tm, f), _rows(tm, f), _rows(tm, d), _full((1, d))],
        out_shape=[
            jax.ShapeDtypeStruct((t, d), F32),
            jax.ShapeDtypeStruct((t, f), BF16),
            jax.ShapeDtypeStruct((t, f), BF16),
            jax.ShapeDtypeStruct((t, d), BF16),
            jax.ShapeDtypeStruct((1, d), F32),
        ],
        args=(x, g, dxo, a, b, wg, wu, wd),
        exchange=exchange,
    )


def _weight_grad(a, b, scale=1.0, exchange=None):
    t, m = a.shape
    n = b.shape[1]
    chips = N_DEV // 2
    r = m // N_DEV
    tk = min(REDUCE_TILE, t)
    halves = 2
    nb = n // halves
    nk = t // tk

    def body(a_ref, b_ref, o_ref, acc, send_buf, recv_buf, send_sems, recv_sems):
        k, j = pl.program_id(0), pl.program_id(1)
        x, y, c, _ = _mesh_place()
        sibling, _ = _peer(x, y, c, 1)
        bv = b_ref[...]
        if scale != 1.0:
            bv = bv * scale
        bb = bv.astype(BF16)
        acc_half = acc.at[j]

        @pl.when(k == 0)
        def _():
            acc_half[...] = jnp.zeros_like(acc_half)

        for i in range(m // MXU_ROWS):
            rows = slice(i * MXU_ROWS, (i + 1) * MXU_ROWS)
            acc_half[rows, :] += _dot(a_ref[:, rows].astype(BF16), bb, TN)

        def to_sibling(half):
            return _remote(send_buf.at[half], recv_buf.at[half], send_sems.at[half], recv_sems.at[half], sibling)

        def owned_rows(q, core):
            return pl.ds(pl.multiple_of((2 * q + core) * r, 8), r)

        for half in range(halves):
            @pl.when(jnp.logical_and(k == nk - 1, j == half))
            def _():
                for q in range(chips):
                    send_buf[half, q] = acc[half, owned_rows(q, 1 - c), :].astype(BF16)
                to_sibling(half).start()

        @pl.when(jnp.logical_and(k == nk - 1, j == halves - 1))
        def _():
            for half in range(halves):
                to_sibling(half).wait_send()
                to_sibling(half).wait_recv()
                for q in range(chips):
                    o_ref[q, :, half * nb:(half + 1) * nb] = (
                        acc[half, owned_rows(q, c), :] + recv_buf[half, q].astype(F32)).astype(BF16)

    (partial,), arrived = _call(
        body,
        name="weight_grad",
        grid=(nk, halves),
        in_specs=[pl.BlockSpec((tk, m), lambda k, j: (k, 0)), pl.BlockSpec((tk, nb), lambda k, j: (k, j))],
        out_specs=[pl.BlockSpec((chips, r, n), lambda k, j: (0, 0, 0))],
        out_shape=[jax.ShapeDtypeStruct((chips, r, n), BF16)],
        scratch_shapes=[
            pltpu.VMEM((halves, m, nb), F32),
            pltpu.VMEM((halves, chips, r, nb), BF16), pltpu.VMEM((halves, chips, r, nb), BF16),
            pltpu.SemaphoreType.DMA((halves,)), pltpu.SemaphoreType.DMA((halves,)),
        ],
        args=(a, b),
        exchange=exchange,
    )
    return partial, arrived


def _chunk_cumsum(v, reverse=False):
    n, width = v.shape
    row = lax.broadcasted_iota(jnp.int32, (n, n), 0)
    col = lax.broadcasted_iota(jnp.int32, (n, n), 1)
    earlier = col >= row if reverse else col <= row
    tri = jnp.where(jnp.logical_and(row // CHUNK == col // CHUNK, earlier), 1.0, 0.0).astype(BF16)
    hi = v.astype(BF16)
    rest = v - hi.astype(F32)
    mid = rest.astype(BF16)
    low = (rest - mid.astype(F32)).astype(BF16)
    sums = _dot(tri, jnp.concatenate([hi, mid, low], axis=1))
    return sums[:, 0:width] + sums[:, width:2 * width] + sums[:, 2 * width:3 * width]


def _shift_rows(v, shift, edge):
    n = v.shape[0]
    row = lax.broadcasted_iota(jnp.int32, (n, 1), 0)
    out = pltpu.roll(v, shift % n, axis=0)
    if shift > 0:
        for j in range(shift):
            out = jnp.where(row == j, edge[8 - shift + j:8 - shift + j + 1, :], out)
    else:
        for j in range(-shift):
            out = jnp.where(row == n + shift + j, edge[j:j + 1, :], out)
    return out


def _gates(z, lbp):
    w = HGRN_W
    lb = _sigmoid(lbp[0:1, :] - lbp[1:2, :])
    zq = z[:, 0:w]
    sig = _sigmoid(z[:, w:2 * w])
    f = lb + (1.0 - lb) * sig
    sq = _sigmoid(zq)
    q = zq * sq * HGRN_DK ** -0.5
    return lb, sig, f, sq, q


def _decayed_operands(q, f, v, qh_buf, kh_buf, kbar_buf, v_buf, etot_buf):
    logf = jnp.log(f)
    bcum = _chunk_cumsum(logf)
    rest = _chunk_cumsum(logf, reverse=True) - logf
    eb, enb, erest = jnp.exp(bcum), jnp.exp(-bcum), jnp.exp(rest)
    kk = 1.0 - f
    qh_buf[...] = (q * eb).astype(BF16)
    kh_buf[...] = (kk * enb).astype(BF16)
    kbar_buf[...] = (kk * erest).astype(BF16)
    v_buf[...] = v.astype(BF16)
    etot_buf[...] = jnp.exp(bcum + rest)
    return eb, enb, erest


def _short_conv(u, edge, cw):
    return cw[0:1, :] * _shift_rows(u, 2, edge) + cw[1:2, :] * _shift_rows(u, 1, edge) + cw[2:3, :] * u


def _block_causal_mask(n):
    row = lax.broadcasted_iota(jnp.int32, (n, n), 0)
    col = lax.broadcasted_iota(jnp.int32, (n, n), 1)
    return jnp.logical_and(row // CHUNK == col // CHUNK, col <= row)


def _spread(v, chunk_of_row, nc):
    return jnp.concatenate([jnp.where(chunk_of_row == c, v, jnp.zeros_like(v)) for c in range(nc)], axis=1)


def _pick(r, chunk_of_row, nc):
    out = jnp.where(chunk_of_row == 0, r[:, 0:HGRN_DK], 0.0)
    for c in range(1, nc):
        out = out + jnp.where(chunk_of_row == c, r[:, c * HGRN_DK:(c + 1) * HGRN_DK], 0.0)
    return out


def _mix_fwd(x, g, w_in, lbp, gh, convw_t, w_out, exchange=None):
    t, d = x.shape
    zw = w_in.shape[0]
    w = HGRN_W
    tm = min(TOKEN_TILE, t)
    nc = tm // CHUNK
    n_chunks = t // CHUNK

    def body(x_ref, g_ref, win_ref, lbp_ref, gh_ref, cw_ref, wout_ref,
             xo_ref, z_ref, o_ref, st_ref, y_ref, state, ucarry, qh_buf, kh_buf, kbar_buf, v_buf, etot_buf):
        _zero_at_start(state, ucarry)
        xv = x_ref[...]
        h, _, _ = _rms(xv, g_ref[...])
        z_ref[...] = _dot(h.astype(BF16), win_ref[...], NT)
        z = z_ref[...]
        _, _, f, _, q = _gates(z, lbp_ref[...])
        _decayed_operands(q, f, z[:, 2 * w:3 * w], qh_buf, kh_buf, kbar_buf, v_buf, etot_buf)
        mask = _block_causal_mask(tm)
        chunk_of_row = lax.broadcasted_iota(jnp.int32, (tm, 1), 0) // CHUNK
        for hd in range(HGRN_HEADS):
            cols = slice(hd * HGRN_DK, (hd + 1) * HGRN_DK)
            qh, kh, kbar, vb = qh_buf[:, cols], kh_buf[:, cols], kbar_buf[:, cols], v_buf[:, cols]
            scores = jnp.where(mask, _dot(qh, kh, NT), 0.0).astype(BF16)
            gains = _dot(_spread(vb, chunk_of_row, nc), kbar, TN)
            entering = []
            st = state[hd]
            for c in range(nc):
                entering.append(st)
                st_ref[c, hd] = st
                st = st * etot_buf[c * CHUNK:c * CHUNK + 1, cols] + gains[c * HGRN_DK:(c + 1) * HGRN_DK, :]
            state[hd] = st
            from_states = _dot(qh, jnp.concatenate(entering, axis=0).astype(BF16), NT)
            o_ref[:, cols] = _dot(scores, vb) + _pick(from_states, chunk_of_row, nc)
        ghv = gh_ref[...]
        for hd in range(HGRN_HEADS):
            cols = slice(hd * HGRN_DK, (hd + 1) * HGRN_DK)
            on, _, _ = _rms(o_ref[:, cols], ghv[:, cols])
            zg = z[:, 3 * w + hd * HGRN_DK:3 * w + (hd + 1) * HGRN_DK]
            y_ref[:, cols] = (on * (zg * _sigmoid(zg))).astype(BF16)
        u = z[:, 5 * w:6 * w] * z[:, 6 * w:7 * w]
        conv = _short_conv(u, ucarry[...], cw_ref[...])
        ucarry[...] = u[tm - 8:tm, :]
        y_ref[:, w:2 * w] = (z[:, 4 * w:5 * w] * conv).astype(BF16)
        xo_ref[...] = xv + _dot(y_ref[...], wout_ref[...])

    return _call(
        body,
        name="mix_fwd",
        grid=(t // tm,),
        in_specs=[
            _rows(tm, d), _full((1, d)), _full((zw, d)), _full((2, w)), _full((1, w)), _full((3, w)),
            _full((2 * w, d)),
        ],
        out_specs=[
            _rows(tm, d), _rows(tm, zw), _rows(tm, w),
            pl.BlockSpec((nc, HGRN_HEADS, HGRN_DK, HGRN_DK), lambda i: (i, 0, 0, 0)),
            _rows(tm, 2 * w),
        ],
        out_shape=[
            jax.ShapeDtypeStruct((t, d), F32),
            jax.ShapeDtypeStruct((t, zw), F32),
            jax.ShapeDtypeStruct((t, w), F32),
            jax.ShapeDtypeStruct((n_chunks, HGRN_HEADS, HGRN_DK, HGRN_DK), F32),
            jax.ShapeDtypeStruct((t, 2 * w), BF16),
        ],
        scratch_shapes=[
            pltpu.VMEM((HGRN_HEADS, HGRN_DK, HGRN_DK), F32), pltpu.VMEM((8, w), F32),
            pltpu.VMEM((tm, w), BF16), pltpu.VMEM((tm, w), BF16), pltpu.VMEM((tm, w), BF16),
            pltpu.VMEM((tm, w), BF16), pltpu.VMEM((tm, w), F32),
        ],
        args=(x, g, w_in, lbp, gh, convw_t, w_out),
        exchange=exchange,
    )


def _mix_bwd(x, g, dxo, z, o, states, w_in, lbp, gh, convw_t, w_out, exchange=None):
    t, d = x.shape
    zw = w_in.shape[0]
    w = HGRN_W
    tm = min(TOKEN_TILE, t)
    nc = tm // CHUNK
    n = t // tm

    def body(x_ref, g_ref, dxo_ref, z_ref, zprev_ref, o_ref, st_ref, win_ref, lbp_ref, gh_ref, cw_ref, wout_ref,
             dx_ref, dz_ref, h_ref, dg_ref, dlbp_ref, dgh_ref, dcw_ref,
             dstate, dcarry, do_buf, dqh_buf, dkh_buf, dkbar_buf, carry_buf,
             qh_buf, kh_buf, kbar_buf, v_buf, etot_buf):
        _zero_at_start(dstate, dcarry, dg_ref, dlbp_ref, dgh_ref, dcw_ref)
        gv = g_ref[...]
        h, xh, r = _rms(x_ref[...], gv)
        h_ref[...] = h.astype(BF16)
        dxo = dxo_ref[...]
        dy = _dot(dxo.astype(BF16), wout_ref[...], NT)
        z = z_ref[...]
        lb, sig, f, sq, q = _gates(z, lbp_ref[...])
        eb, enb, erest = _decayed_operands(q, f, z[:, 2 * w:3 * w], qh_buf, kh_buf, kbar_buf, v_buf, etot_buf)

        ghv = gh_ref[...]
        dgh_parts = []
        for hd in range(HGRN_HEADS):
            cols = slice(hd * HGRN_DK, (hd + 1) * HGRN_DK)
            gcols = slice(3 * w + hd * HGRN_DK, 3 * w + (hd + 1) * HGRN_DK)
            on, oh, rr = _rms(o_ref[:, cols], ghv[:, cols])
            zg = z[:, gcols]
            sgz = _sigmoid(zg)
            dyh = dy[:, cols]
            don = dyh * (zg * sgz)
            dz_ref[:, gcols] = (dyh * on * (sgz * (1.0 + zg * (1.0 - sgz)))).astype(BF16)
            dgh_parts.append(jnp.sum(don * oh, axis=0, keepdims=True))
            do_buf[:, cols] = _rms_bwd(don, oh, rr, ghv[:, cols]).astype(BF16)
        dgh_ref[...] += jnp.concatenate(dgh_parts, axis=1)

        zb = z[:, 4 * w:5 * w]
        zc = z[:, 5 * w:6 * w]
        zu = z[:, 6 * w:7 * w]
        u = zc * zu
        cw = cw_ref[...]
        zp = zprev_ref[...]
        uprev = jnp.where(pl.program_id(0) == n - 1, 0.0, zp[:, 5 * w:6 * w] * zp[:, 6 * w:7 * w])
        dyc = dy[:, w:2 * w]
        dz_ref[:, 4 * w:5 * w] = (dyc * _short_conv(u, uprev, cw)).astype(BF16)
        dconv = dyc * zb
        edge = dcarry[...]
        dconv1 = _shift_rows(dconv, -1, edge)
        dconv2 = _shift_rows(dconv, -2, edge)
        dcarry[...] = dconv[0:8, :]
        du = cw[2:3, :] * dconv + cw[1:2, :] * dconv1 + cw[0:1, :] * dconv2
        dz_ref[:, 5 * w:6 * w] = (du * zu).astype(BF16)
        dz_ref[:, 6 * w:7 * w] = (du * zc).astype(BF16)
        dcw_ref[...] += jnp.concatenate([
            jnp.sum(u * dconv2, axis=0, keepdims=True),
            jnp.sum(u * dconv1, axis=0, keepdims=True),
            jnp.sum(u * dconv, axis=0, keepdims=True)], axis=0)

        mask = _block_causal_mask(tm)
        chunk_of_row = lax.broadcasted_iota(jnp.int32, (tm, 1), 0) // CHUNK
        for hd in range(HGRN_HEADS):
            cols = slice(hd * HGRN_DK, (hd + 1) * HGRN_DK)
            qhb, khb, kbarb, vb = qh_buf[:, cols], kh_buf[:, cols], kbar_buf[:, cols], v_buf[:, cols]
            dob = do_buf[:, cols]
            scores = jnp.where(mask, _dot(qhb, khb, NT), 0.0).astype(BF16)
            dscores = jnp.where(mask, _dot(dob, vb, NT), 0.0).astype(BF16)
            gains = _dot(_spread(dob, chunk_of_row, nc), qhb, TN)
            entering = [st_ref[c, hd] for c in range(nc)]
            leaving = [None] * nc
            dst = dstate[hd]
            for c in reversed(range(nc)):
                elast = etot_buf[c * CHUNK:c * CHUNK + 1, cols]
                leaving[c] = dst
                carry_buf[c:c + 1, cols] = jnp.sum(dst * entering[c], axis=0, keepdims=True) * elast
                dst = dst * elast + gains[c * HGRN_DK:(c + 1) * HGRN_DK, :]
            dstate[hd] = dst
            dst_rows = jnp.concatenate(leaving, axis=0).astype(BF16)
            dst_lanes = jnp.concatenate(leaving, axis=1).astype(BF16)
            st_lanes = jnp.concatenate(entering, axis=1).astype(BF16)
            dv = _dot(scores, dob, TN) + _pick(_dot(kbarb, dst_rows, NT), chunk_of_row, nc)
            dz_ref[:, 2 * w + hd * HGRN_DK:2 * w + (hd + 1) * HGRN_DK] = dv.astype(BF16)
            dqh_buf[:, cols] = _dot(dscores, khb) + _pick(_dot(dob, st_lanes), chunk_of_row, nc)
            dkh_buf[:, cols] = _dot(dscores, qhb, TN)
            dkbar_buf[:, cols] = _pick(_dot(vb, dst_lanes), chunk_of_row, nc)

        dqh, dkh, dkbar = dqh_buf[...], dkh_buf[...], dkbar_buf[...]
        kbar_dkbar = kbar_buf[...].astype(F32) * dkbar
        db = qh_buf[...].astype(F32) * dqh - kh_buf[...].astype(F32) * dkh - kbar_dkbar
        through_last = jnp.concatenate([
            jnp.broadcast_to(
                jnp.sum(kbar_dkbar[c * CHUNK:(c + 1) * CHUNK], axis=0, keepdims=True) + carry_buf[c:c + 1, :],
                (CHUNK, w))
            for c in range(nc)], axis=0)
        dlogf = _chunk_cumsum(db, reverse=True) + through_last
        df = dlogf / f - (dkh * enb + dkbar * erest)
        zq = z[:, 0:w]
        dz_ref[:, 0:w] = (dqh * eb * HGRN_DK ** -0.5 * (sq * (1.0 + zq * (1.0 - sq)))).astype(BF16)
        dz_ref[:, w:2 * w] = (df * (1.0 - lb) * sig * (1.0 - sig)).astype(BF16)
        dlb = jnp.sum(df * (1.0 - sig), axis=0, keepdims=True) * lb * (1.0 - lb)
        dlbp_ref[...] += jnp.concatenate([dlb, -dlb], axis=0)

        dh = _dot(dz_ref[...], win_ref[...])
        dx_ref[...] = _rms_bwd(dh, xh, r, gv) + dxo
        dg_ref[...] += jnp.sum(dh * xh, axis=0, keepdims=True)

    return _call(
        body,
        name="mix_bwd",
        grid=(n,),
        in_specs=[
            _rows_rev(tm, d, n), _full((1, d)), _rows_rev(tm, d, n), _rows_rev(tm, zw, n),
            pl.BlockSpec((8, zw), lambda i: (jnp.maximum((n - 1 - i) * (tm // 8) - 1, 0), 0)),
            _rows_rev(tm, w, n),
            pl.BlockSpec((nc, HGRN_HEADS, HGRN_DK, HGRN_DK), lambda i: (n - 1 - i, 0, 0, 0)),
            _full((zw, d)), _full((2, w)), _full((1, w)), _full((3, w)), _full((2 * w, d)),
        ],
        out_specs=[
            _rows_rev(tm, d, n), _rows_rev(tm, zw, n), _rows_rev(tm, d, n),
            _full((1, d)), _full((2, w)), _full((1, w)), _full((3, w)),
        ],
        out_shape=[
            jax.ShapeDtypeStruct((t, d), F32),
            jax.ShapeDtypeStruct((t, zw), BF16),
            jax.ShapeDtypeStruct((t, d), BF16),
            jax.ShapeDtypeStruct((1, d), F32),
            jax.ShapeDtypeStruct((2, w), F32),
            jax.ShapeDtypeStruct((1, w), F32),
            jax.ShapeDtypeStruct((3, w), F32),
        ],
        scratch_shapes=[
            pltpu.VMEM((HGRN_HEADS, HGRN_DK, HGRN_DK), F32), pltpu.VMEM((8, w), F32),
            pltpu.VMEM((tm, w), BF16), pltpu.VMEM((tm, w), F32), pltpu.VMEM((tm, w), F32), pltpu.VMEM((tm, w), F32),
            pltpu.VMEM((8, w), F32),
            pltpu.VMEM((tm, w), BF16), pltpu.VMEM((tm, w), BF16), pltpu.VMEM((tm, w), BF16),
            pltpu.VMEM((tm, w), BF16), pltpu.VMEM((tm, w), F32),
        ],
        args=(x, g, dxo, z, z, o, states, w_in, lbp, gh, convw_t, w_out),
        exchange=exchange,
    )


def _memkv_fwd(mem, g, wkv):
    m, d = mem.shape
    nb, _, cb = wkv.shape

    def body(mem_ref, g_ref, wkv_ref, kv_ref):
        mn, _, _ = _rms(mem_ref[...], g_ref[...])
        mnb = mn.astype(BF16)
        for j in range(nb):
            kv_ref[:, j * cb:(j + 1) * cb] = _dot(mnb, wkv_ref[j]).astype(BF16)

    return pl.pallas_call(
        body,
        name="memkv_fwd",
        out_shape=jax.ShapeDtypeStruct((m, nb * cb), BF16),
        compiler_params=_params(),
    )(mem, g, wkv)


def _memkv_bwd(mem, g, dkv, wkv):
    m, d = mem.shape
    nb, _, cb = wkv.shape
    chips = nb // 2

    def body(mem_ref, g_ref, dkv_ref, wkv_ref, dw_ref, dg_ref, dw_all, send_buf, recv_buf, send_sem, recv_sem):
        x, y, c, _ = _mesh_place()
        sibling, _ = _peer(x, y, c, 1)
        mn, xh, _ = _rms(mem_ref[...], g_ref[...])
        mnb = mn.astype(BF16)
        dmn = jnp.zeros((m, d), F32)
        for j in range(nb):
            dkvb = dkv_ref[:, j * cb:(j + 1) * cb].astype(BF16)
            dw_all[j] = _dot(mnb, dkvb, TN)
            dmn = dmn + _dot(dkvb, wkv_ref[j], NT)
        dg_ref[...] = jnp.sum(dmn * xh, axis=0, keepdims=True)
        for q in range(chips):
            send_buf[q] = dw_all[2 * q + 1 - c].astype(BF16)
        to_sibling = _remote(send_buf, recv_buf, send_sem, recv_sem, sibling)
        to_sibling.start()
        to_sibling.wait_send()
        to_sibling.wait_recv()
        for q in range(chips):
            dw_ref[q] = (dw_all[2 * q + c] + recv_buf[q].astype(F32)).astype(BF16)

    return pl.pallas_call(
        body,
        name="memkv_bwd",
        out_shape=[jax.ShapeDtypeStruct((chips, d, cb), BF16), jax.ShapeDtypeStruct((1, d), F32)],
        scratch_shapes=[
            pltpu.VMEM((nb, d, cb), F32), pltpu.VMEM((chips, d, cb), BF16), pltpu.VMEM((chips, d, cb), BF16),
            pltpu.SemaphoreType.DMA, pltpu.SemaphoreType.DMA,
        ],
        compiler_params=_params(),
    )(mem, g, dkv, wkv)


def _softmax_rows(qm_h, k_h):
    sc = _dot(qm_h, k_h, NT) * MEM_HD ** -0.5
    e = jnp.exp(sc - jnp.max(sc, axis=-1, keepdims=True))
    return e / jnp.sum(e, axis=-1, keepdims=True)


def _xattn_fwd(x, g, wq, kv, wo, exchange=None):
    t, d = x.shape
    m = kv.shape[0]
    tm = min(TOKEN_TILE, t)

    def body(x_ref, g_ref, wq_ref, kv_ref, wo_ref, xo_ref, hq_ref, qm_ref, att_ref):
        xv = x_ref[...]
        h, _, _ = _rms(xv, g_ref[...])
        hq_ref[...] = h.astype(BF16)
        qm_ref[...] = _dot(hq_ref[...], wq_ref[...]).astype(BF16)
        for hd in range(MEM_HEADS):
            cols = slice(hd * MEM_HD, (hd + 1) * MEM_HD)
            p = _softmax_rows(qm_ref[:, cols], kv_ref[:, cols])
            att_ref[:, cols] = _dot(p.astype(BF16), kv_ref[:, d + hd * MEM_HD:d + (hd + 1) * MEM_HD]).astype(BF16)
        xo_ref[...] = xv + _dot(att_ref[...], wo_ref[...])

    return _call(
        body,
        name="xattn_fwd",
        grid=(t // tm,),
        in_specs=[_rows(tm, d), _full((1, d)), _full((d, d)), _full((m, 2 * d)), _full((d, d))],
        out_specs=[_rows(tm, d), _rows(tm, d), _rows(tm, d), _rows(tm, d)],
        out_shape=[
            jax.ShapeDtypeStruct((t, d), F32),
            jax.ShapeDtypeStruct((t, d), BF16),
            jax.ShapeDtypeStruct((t, d), BF16),
            jax.ShapeDtypeStruct((t, d), BF16),
        ],
        args=(x, g, wq, kv, wo),
        exchange=exchange,
    )


def _xattn_bwd(x, g, dxo, qm, kv, wq, wo, exchange=None):
    t, d = x.shape
    m = kv.shape[0]
    tm = min(TOKEN_TILE, t)

    def body(x_ref, g_ref, dxo_ref, qm_ref, kv_ref, wq_ref, wo_ref, dx_ref, dqm_ref, dkv_ref, dg_ref):
        _zero_at_start(dkv_ref, dg_ref)
        gv = g_ref[...]
        _, xh, r = _rms(x_ref[...], gv)
        dxo = dxo_ref[...]
        datt = _dot(dxo.astype(BF16), wo_ref[...], NT).astype(BF16)
        for hd in range(MEM_HEADS):
            cols = slice(hd * MEM_HD, (hd + 1) * MEM_HD)
            vcols = slice(d + hd * MEM_HD, d + (hd + 1) * MEM_HD)
            qm_h = qm_ref[:, cols]
            p = _softmax_rows(qm_h, kv_ref[:, cols])
            datt_h = datt[:, cols]
            dp = _dot(datt_h, kv_ref[:, vcols], NT)
            dsc = (p * (dp - jnp.sum(p * dp, axis=-1, keepdims=True)) * MEM_HD ** -0.5).astype(BF16)
            dqm_ref[:, cols] = _dot(dsc, kv_ref[:, cols]).astype(BF16)
            dkv_ref[:, cols] += _dot(dsc, qm_h, TN)
            dkv_ref[:, vcols] += _dot(p.astype(BF16), datt_h, TN)
        dh = _dot(dqm_ref[...], wq_ref[...], NT)
        dx_ref[...] = _rms_bwd(dh, xh, r, gv) + dxo
        dg_ref[...] += jnp.sum(dh * xh, axis=0, keepdims=True)

    return _call(
        body,
        name="xattn_bwd",
        grid=(t // tm,),
        in_specs=[
            _rows(tm, d), _full((1, d)), _rows(tm, d), _rows(tm, d), _full((m, 2 * d)), _full((d, d)), _full((d, d)),
        ],
        out_specs=[_rows(tm, d), _rows(tm, d), _full((m, 2 * d)), _full((1, d))],
        out_shape=[
            jax.ShapeDtypeStruct((t, d), F32),
            jax.ShapeDtypeStruct((t, d), BF16),
            jax.ShapeDtypeStruct((m, 2 * d), F32),
            jax.ShapeDtypeStruct((1, d), F32),
        ],
        args=(x, g, dxo, qm, kv, wq, wo),
        exchange=exchange,
    )


def _mesh_place():
    x, y, c = lax.axis_index("x"), lax.axis_index("y"), lax.axis_index("c")
    return x, y, c, 4 * x + 2 * y + c


def _peer(x, y, c, k):
    px = 1 - x if k & 4 else x
    py = 1 - y if k & 2 else y
    pc = 1 - c if k & 1 else c
    return (px, py, pc), 4 * px + 2 * py + pc


ICI_HOPS = (2, 4, 6)
N_HOPS = len(ICI_HOPS)


def _remote(src, dst, send_sem, recv_sem, peer):
    return pltpu.make_async_remote_copy(
        src_ref=src, dst_ref=dst, send_sem=send_sem, recv_sem=recv_sem, device_id=peer, device_id_type=MESH_IDS)


def _gather_exchange(shards):
    n = len(shards)

    def place():
        x, y, c, me = _mesh_place()
        sibling, _ = _peer(x, y, c, 1)
        to_x, from_x = _peer(x, y, c, 4)
        to_y, from_y = _peer(x, y, c, 2)
        _, from_diagonal = _peer(x, y, c, 6)
        onward = (c * to_y[0] + (1 - c) * to_x[0], c * to_y[1] + (1 - c) * to_x[1], c)
        passed_on = c * from_x + (1 - c) * from_y
        return me, sibling, (to_x, to_y, onward), (from_x, from_y, from_diagonal), passed_on

    def start(src, dst, sems):
        ici_send, ici_recv, pair_send, pair_recv, local = sems
        me, sibling, targets, _, _ = place()
        for a in range(n):
            pltpu.make_async_copy(src[a], dst[a].at[me], local.at[a]).start()
            for j in range(2):
                _remote(src[a], dst[a].at[me], ici_send.at[a, j], ici_recv.at[a, j], targets[j]).start()
            _remote(src[a], dst[a].at[me], pair_send.at[a, 0], pair_recv.at[a, 0], sibling).start()

    def to_sibling(dst, sems, a, j, origin, sibling):
        _, _, pair_send, pair_recv, _ = sems
        slot = dst[a].at[origin]
        return _remote(slot, slot, pair_send.at[a, 1 + j], pair_recv.at[a, 1 + j], sibling)

    def middle(src, dst, sems):
        ici_send, ici_recv, _, _, _ = sems
        _, sibling, targets, origins, passed_on = place()
        for a in range(n):
            for j in range(2):
                _remote(src[a], dst[a].at[origins[j]], ici_send.at[a, j], ici_recv.at[a, j], targets[j]).wait_recv()
            slot = dst[a].at[passed_on]
            _remote(slot, slot, ici_send.at[a, 2], ici_recv.at[a, 2], targets[2]).start()
            for j in range(2):
                to_sibling(dst, sems, a, j, origins[j], sibling).start()

    def finish(src, dst, sems):
        ici_send, ici_recv, pair_send, pair_recv, local = sems
        me, sibling, targets, origins, _ = place()
        for a in range(n):
            _remote(src[a], dst[a].at[origins[2]], ici_send.at[a, 2], ici_recv.at[a, 2], targets[2]).wait_recv()
            to_sibling(dst, sems, a, 2, origins[2], sibling).start()
        for a in range(n):
            pltpu.make_async_copy(src[a], dst[a].at[me], local.at[a]).wait()
            for j in range(N_HOPS):
                _remote(src[a], dst[a].at[me], ici_send.at[a, j], ici_recv.at[a, j], targets[j]).wait_send()
            for j, origin in enumerate((me,) + origins):
                from_sibling = origin + 1 - 2 * (origin % 2)
                passed = _remote(src[a], dst[a].at[from_sibling], pair_send.at[a, j], pair_recv.at[a, j], sibling)
                passed.wait_send()
                passed.wait_recv()

    return _Exchange(
        shards,
        [jax.ShapeDtypeStruct((N_DEV,) + s.shape, s.dtype) for s in shards],
        [
            pltpu.SemaphoreType.DMA((n, N_HOPS)), pltpu.SemaphoreType.DMA((n, N_HOPS)),
            pltpu.SemaphoreType.DMA((n, N_HOPS + 1)), pltpu.SemaphoreType.DMA((n, N_HOPS + 1)),
            pltpu.SemaphoreType.DMA((n,)),
        ],
        start, finish, middle)


def _scatter_copies(src, dst, sems, n, arrivals=False):
    send, recv, local = sems
    x, y, c, _ = _mesh_place()
    chip = 2 * x + y
    if arrivals is None:
        return [pltpu.make_async_copy(src[a].at[chip], dst[a].at[chip], local.at[a]) for a in range(n)]
    copies = []
    for a in range(n):
        for j, k in enumerate(ICI_HOPS):
            peer, _ = _peer(x, y, c, k)
            peer_chip = 2 * peer[0] + peer[1]
            slot = dst[a].at[peer_chip if arrivals else chip]
            copies.append(_remote(src[a].at[peer_chip], slot, send.at[a, j], recv.at[a, j], peer))
    return copies


def _scatter_start(src, dst, sems, n):
    for cp in _scatter_copies(src, dst, sems, n, arrivals=None) + _scatter_copies(src, dst, sems, n):
        cp.start()


def _scatter_finish(src, dst, sems, n):
    for cp in _scatter_copies(src, dst, sems, n, arrivals=None):
        cp.wait()
    for cp in _scatter_copies(src, dst, sems, n):
        cp.wait_send()
    for cp in _scatter_copies(src, dst, sems, n, arrivals=True):
        cp.wait_recv()


def _scatter_scratch(n):
    return [pltpu.SemaphoreType.DMA((n, N_HOPS)), pltpu.SemaphoreType.DMA((n, N_HOPS)), pltpu.SemaphoreType.DMA((n,))]


def _scatter_exchange(partials):
    n = len(partials)
    return _Exchange(
        partials, [jax.ShapeDtypeStruct(p.shape, p.dtype) for p in partials], _scatter_scratch(n),
        lambda src, dst, sems: _scatter_start(src, dst, sems, n),
        lambda src, dst, sems: _scatter_finish(src, dst, sems, n))


SMALL_LAYOUT = {
    "ffn1_norm": (0, 1, 1024), "mix_norm": (1, 1, 1024), "xattn_norm": (2, 1, 1024), "mem_norm": (3, 1, 1024),
    "ffn2_norm": (4, 1, 1024), "final_norm": (5, 1, 1024), "lb_param": (6, 2, 512), "hgrn_out_norm": (8, 1, 512),
    "conv_w": (9, 3, 512), "loss": (12, 1, 128),
}


def _final_exchange(partials, small):
    n = len(partials)
    names = list(small)
    width = 1024

    def body(*refs):
        src = refs[:n]
        pieces = refs[n:n + len(names)]
        dst = refs[n + len(names):2 * n + len(names)]
        total_ref = refs[2 * n + len(names)]
        pack, gathered, small_send, small_recv = refs[2 * n + len(names) + 1:2 * n + len(names) + 5]
        sems = refs[2 * n + len(names) + 5:]
        x, y, c, me = _mesh_place()
        pack[...] = jnp.zeros_like(pack)
        for name, piece in zip(names, pieces):
            row, nrows, ncols = SMALL_LAYOUT[name]
            pack[row:row + nrows, 0:ncols] = piece[...]
        for k in range(1, N_DEV):
            peer, _ = _peer(x, y, c, k)
            _remote(pack, gathered.at[me], small_send.at[k - 1], small_recv.at[k - 1], peer).start()
        _scatter_start(src, dst, sems, n)
        gathered[me] = pack[...]
        for k in range(1, N_DEV):
            peer, peer_index = _peer(x, y, c, k)
            landed = _remote(pack, gathered.at[peer_index], small_send.at[k - 1], small_recv.at[k - 1], peer)
            landed.wait_send()
            landed.wait_recv()
        total = gathered[0]
        for j in range(1, N_DEV):
            total = total + gathered[j]
        total_ref[...] = total
        _scatter_finish(src, dst, sems, n)

    hbm = pl.BlockSpec(memory_space=pltpu.HBM)
    vmem = pl.BlockSpec(memory_space=pltpu.VMEM)
    out = pl.pallas_call(
        body,
        name="final_exchange",
        in_specs=[hbm] * n + [vmem] * len(names),
        out_specs=[hbm] * n + [vmem],
        out_shape=[jax.ShapeDtypeStruct(p.shape, p.dtype) for p in partials]
        + [jax.ShapeDtypeStruct((SMALL_ROWS, width), F32)],
        scratch_shapes=[
            pltpu.VMEM((SMALL_ROWS, width), F32), pltpu.VMEM((N_DEV, SMALL_ROWS, width), F32),
            pltpu.SemaphoreType.DMA((N_DEV - 1,)), pltpu.SemaphoreType.DMA((N_DEV - 1,)),
        ] + _scatter_scratch(n),
        compiler_params=pltpu.CompilerParams(has_side_effects=True),
    )(*partials, *[small[k] for k in names])
    return out[:n], out[n]


def _adamw_math(w, g, m, v):
    m = ADAM_B1 * m + (1.0 - ADAM_B1) * g
    v = ADAM_B2 * v + (1.0 - ADAM_B2) * (g * g)
    m_hat = m / (1.0 - ADAM_B1 ** ADAM_STEP)
    v_hat = v / (1.0 - ADAM_B2 ** ADAM_STEP)
    delta = -ADAM_LR * (m_hat / (jnp.sqrt(v_hat) + ADAM_EPS) + ADAM_WD * w)
    return delta, m, v


def _adamw_shard(parts, w, m, v):
    r, c = w.shape
    n_parts = parts.shape[0]
    tr = max(rows for rows in range(16, r + 1, 16) if r % rows == 0 and rows * c <= ADAMW_TILE_ELEMENTS)

    def body(p_ref, w_ref, m_ref, v_ref, g_ref, d_ref, mo_ref, vo_ref):
        g = p_ref[0].astype(F32)
        for j in range(1, n_parts):
            g = g + p_ref[j].astype(F32)
        delta, mn, vn = _adamw_math(w_ref[...], g, m_ref[...], v_ref[...])
        g_ref[...] = g
        d_ref[...] = delta
        mo_ref[...] = mn
        vo_ref[...] = vn

    tile = pl.BlockSpec((tr, c), lambda i: (i, 0))
    return pl.pallas_call(
        body,
        name="adamw_shard",
        grid=(r // tr,),
        in_specs=[pl.BlockSpec((n_parts, tr, c), lambda i: (0, i, 0)), tile, tile, tile],
        out_specs=[tile] * 4,
        out_shape=[jax.ShapeDtypeStruct((r, c), F32)] * 4,
        compiler_params=_params(("parallel",)),
    )(parts, w, m, v)


def _adamw_small(gs, ws, ms, vs):
    n = len(gs)

    def body(*refs):
        g_refs, w_refs, m_refs, v_refs = refs[:n], refs[n:2 * n], refs[2 * n:3 * n], refs[3 * n:4 * n]
        d_out, m_out, v_out = refs[4 * n:5 * n], refs[5 * n:6 * n], refs[6 * n:7 * n]
        for i in range(n):
            delta, mn, vn = _adamw_math(w_refs[i][...], g_refs[i][...], m_refs[i][...], v_refs[i][...])
            d_out[i][...] = delta
            m_out[i][...] = mn
            v_out[i][...] = vn

    shapes = [jax.ShapeDtypeStruct(w.shape, F32) for w in ws]
    out = pl.pallas_call(
        body,
        name="adamw_small",
        out_shape=shapes * 3,
        compiler_params=_params(),
    )(*gs, *ws, *ms, *vs)
    return out[:n], out[n:2 * n], out[2 * n:]


TRANSPOSED = ("ffn1_gate", "ffn1_up", "w_in", "ffn2_gate", "ffn2_up", "conv_w")
GROUP_FFN1 = ("ffn1_gate", "ffn1_up", "ffn1_down")
GROUP_MIX = ("w_in", "w_out")
GROUP_XATTN = ("w_q_mem", "w_kv_mem", "w_o_mem")
GROUP_FFN2 = ("ffn2_gate", "ffn2_up", "ffn2_down")
LARGE = GROUP_FFN1 + GROUP_MIX + GROUP_XATTN + GROUP_FFN2
SMALL = ("ffn1_norm", "mix_norm", "lb_param", "hgrn_out_norm", "conv_w", "xattn_norm", "mem_norm", "ffn2_norm",
         "final_norm")
WEIGHTS = ("ffn1_norm", "ffn1_gate", "ffn1_up", "ffn1_down", "mix_norm", "w_in", "lb_param", "hgrn_out_norm",
           "conv_w", "w_out", "xattn_norm", "mem_norm", "w_q_mem", "w_kv_mem", "w_o_mem", "ffn2_norm", "ffn2_gate",
           "ffn2_up", "ffn2_down", "final_norm")


def kernel(x, mem, ffn1_norm, ffn1_gate, ffn1_up, ffn1_down, mix_norm, w_in, lb_param, hgrn_out_norm, conv_w, w_out, xattn_norm, mem_norm, w_q_mem, w_kv_mem, w_o_mem, ffn2_norm, ffn2_gate, ffn2_up, ffn2_down, final_norm, loss_target, m_ffn1_norm, m_ffn1_gate, m_ffn1_up, m_ffn1_down, m_mix_norm, m_w_in, m_lb_param, m_hgrn_out_norm, m_conv_w, m_w_out, m_xattn_norm, m_mem_norm, m_w_q_mem, m_w_kv_mem, m_w_o_mem, m_ffn2_norm, m_ffn2_gate, m_ffn2_up, m_ffn2_down, m_final_norm, v_ffn1_norm, v_ffn1_gate, v_ffn1_up, v_ffn1_down, v_mix_norm, v_w_in, v_lb_param, v_hgrn_out_norm, v_conv_w, v_w_out, v_xattn_norm, v_mem_norm, v_w_q_mem, v_w_kv_mem, v_w_o_mem, v_ffn2_norm, v_ffn2_gate, v_ffn2_up, v_ffn2_down, v_final_norm):
    given = dict(locals())
    me = 4 * lax.axis_index("x") + 2 * lax.axis_index("y") + lax.axis_index("c")
    x0, memv, target = x[0], mem[0], loss_target[0]

    def shard(prefix, name):
        v = given[prefix + name]
        if v.ndim == 1:
            return v.reshape(1, -1)
        if v.ndim == 2:
            return v
        return v[0].T if name in TRANSPOSED else v[0]

    w = {name: shard("", name) for name in WEIGHTS}
    m = {name: shard("m_", name) for name in WEIGHTS}
    v = {name: shard("v_", name) for name in WEIGHTS}

    conv_taps, conv_rows = w["conv_w"].shape
    conv_tile = jnp.pad(w["conv_w"], ((0, 8 - conv_taps), (0, 128 - conv_rows)))
    wire = {name: w[name].astype(BF16) for name in LARGE}
    full = {}

    def landed(names, gathered):
        for name, blocks in zip(names, gathered):
            _, r, c = blocks.shape
            full[name] = blocks if name == "w_kv_mem" else blocks.reshape(N_DEV * r, c)

    first = ("ffn1_gate", "ffn1_up")
    landed(first, _run_exchange(_gather_exchange([wire[k] for k in first]), "gather_first"))

    riders = (("ffn1_down", "w_in"), ("w_out", "w_kv_mem"), ("w_q_mem", "w_o_mem", "ffn2_gate", "ffn2_up"),
              ("ffn2_down",))
    (a1, b1, s1), gathered = _ffn_up(
        x0, w["ffn1_norm"], full["ffn1_gate"], full["ffn1_up"],
        exchange=_gather_exchange([wire[k] for k in riders[0]]))
    landed(riders[0], gathered)
    (x1,), gathered = _ffn_down(
        x0, s1, full["ffn1_down"], exchange=_gather_exchange([wire[k] for k in riders[1]] + [conv_tile]))
    landed(riders[1], gathered)
    convw_t = gathered[-1][:, :conv_taps, :conv_rows].transpose(1, 0, 2).reshape(conv_taps, N_DEV * conv_rows)
    (x2, z, o_raw, states, ycat), gathered = _mix_fwd(
        x1, w["mix_norm"], full["w_in"], w["lb_param"], w["hgrn_out_norm"], convw_t, full["w_out"],
        exchange=_gather_exchange([wire[k] for k in riders[2]]))
    landed(riders[2], gathered)
    kv = _memkv_fwd(memv, w["mem_norm"], full["w_kv_mem"])
    (x3, hq, qm, att), gathered = _xattn_fwd(
        x2, w["xattn_norm"], full["w_q_mem"], kv, full["w_o_mem"],
        exchange=_gather_exchange([wire[k] for k in riders[3]]))
    landed(riders[3], gathered)
    (dx4, a2, b2, s2, loss_part, d_final), _ = _ffn_fwd(
        x3, w["ffn2_norm"], full["ffn2_gate"], full["ffn2_up"], full["ffn2_down"], head=(w["final_norm"], target))

    parts = {}
    waiting = []

    def carried():
        names = [name for name, _ in waiting]
        exchange = _scatter_exchange([p for _, p in waiting]) if waiting else None
        del waiting[:]
        return names, exchange

    def weight_grad(name, a, b, scale=1.0):
        names, exchange = carried()
        partial, arrived = _weight_grad(a, b, scale, exchange=exchange)
        parts.update(zip(names, arrived))
        waiting.append((name, partial))

    (dx3, da2, db2, h4, d_ffn2_norm), _ = _ffn_bwd(
        x3, w["ffn2_norm"], dx4, a2, b2, full["ffn2_gate"], full["ffn2_up"], full["ffn2_down"])
    weight_grad("ffn2_down", s2, dx4, 0.5)
    weight_grad("ffn2_gate", da2, h4)
    weight_grad("ffn2_up", db2, h4)
    names, exchange = carried()
    (dx2, dqm, dkv, d_xattn_norm), arrived = _xattn_bwd(
        x2, w["xattn_norm"], dx3, qm, kv, full["w_q_mem"], full["w_o_mem"], exchange=exchange)
    parts.update(zip(names, arrived))
    d_wkv, d_mem_norm = _memkv_bwd(memv, w["mem_norm"], dkv, full["w_kv_mem"])
    waiting.append(("w_kv_mem", d_wkv))
    names, exchange = carried()
    (dx1, dz, h2, d_mix_norm, d_lbp, d_gh, d_convw_t), arrived = _mix_bwd(
        x1, w["mix_norm"], dx2, z, o_raw, states, full["w_in"], w["lb_param"], w["hgrn_out_norm"], convw_t,
        full["w_out"], exchange=exchange)
    parts.update(zip(names, arrived))
    weight_grad("w_in", dz, h2)
    weight_grad("ffn1_down", s1, dx1, 0.5)
    (dx0, da1, db1, h1, d_ffn1_norm), _ = _ffn_bwd(
        x0, w["ffn1_norm"], dx1, a1, b1, full["ffn1_gate"], full["ffn1_up"], full["ffn1_down"])
    weight_grad("ffn1_gate", da1, h1)
    weight_grad("ffn1_up", db1, h1)
    weight_grad("w_o_mem", att, dx3)
    weight_grad("w_q_mem", hq, dqm)
    weight_grad("w_out", ycat, dx2)

    small_parts = {
        "ffn1_norm": d_ffn1_norm, "mix_norm": d_mix_norm, "xattn_norm": d_xattn_norm, "mem_norm": d_mem_norm,
        "ffn2_norm": d_ffn2_norm, "final_norm": d_final, "lb_param": d_lbp, "hgrn_out_norm": d_gh,
        "conv_w": d_convw_t, "loss": loss_part,
    }
    names = [name for name, _ in waiting]
    arrived, total = _final_exchange([p for _, p in waiting], small_parts)
    parts.update(zip(names, arrived))

    g_out, d_out, m_out, v_out = {}, {}, {}, {}
    for name in LARGE:
        g_out[name], d_out[name], m_out[name], v_out[name] = _adamw_shard(parts[name], w[name], m[name], v[name])
    g_small = {}
    for name in SMALL:
        row, nrows, ncols = SMALL_LAYOUT[name]
        g_small[name] = total[row:row + nrows, 0:ncols]
    g_small["conv_w"] = lax.dynamic_slice_in_dim(g_small["conv_w"], me * conv_rows, conv_rows, axis=1)
    ds, ms, vs = _adamw_small(
        [g_small[k] for k in SMALL], [w[k] for k in SMALL], [m[k] for k in SMALL], [v[k] for k in SMALL])
    for i, name in enumerate(SMALL):
        g_out[name], d_out[name], m_out[name], v_out[name] = g_small[name], ds[i], ms[i], vs[i]

    def shaped(value, name):
        return (value.T if name in TRANSPOSED else value).reshape(given[name].shape)

    loss = total[SMALL_LAYOUT["loss"][0], 0]
    outs = [loss, dx0.reshape(x.shape)]
    for group in (g_out, d_out, m_out, v_out):
        outs += [shaped(group[name], name) for name in WEIGHTS]
    return tuple(outs)
```

```python
import jax
import jax.numpy as jnp
from jax import lax
from jax.experimental import pallas as pl
from jax.experimental.pallas import tpu as pltpu

F32 = jnp.float32
BF16 = jnp.bfloat16
MESH_IDS = pl.DeviceIdType.MESH

N_DEV = 8
EPS = 1e-6
HGRN_HEADS = 4
HGRN_DK = 128
HGRN_W = 512
CHUNK = 64
MEM_HEADS = 4
MEM_HD = 256
ADAM_LR = 0.001
ADAM_B1 = 0.9
ADAM_B2 = 0.999
ADAM_EPS = 1e-08
ADAM_WD = 0.01
ADAM_STEP = 10

TOKEN_TILE = 256
REDUCE_TILE = 1024
ADAMW_TILE_ELEMENTS = 256 * 1024
MIDDLE_EIGHTHS = 5
MXU_ROWS = 256
VMEM_LIMIT = 60 * 1024 * 1024
SMALL_ROWS = 16
NT = (((1,), (1,)), ((), ()))
TN = (((0,), (0,)), ((), ()))


def _params(sem=None):
    return pltpu.CompilerParams(dimension_semantics=sem, vmem_limit_bytes=VMEM_LIMIT)


def _dot(a, b, dims=None):
    if dims is None:
        return jnp.dot(a, b, preferred_element_type=F32)
    return lax.dot_general(a, b, dims, preferred_element_type=F32)


def _sigmoid(v):
    return 1.0 / (1.0 + jnp.exp(-v))


def _rms(x, g):
    r = lax.rsqrt(jnp.mean(x * x, axis=-1, keepdims=True) + EPS)
    xh = x * r
    return xh * g, xh, r


def _rms_bwd(dh, xh, r, g):
    dxh = dh * g
    return r * (dxh - xh * jnp.mean(dxh * xh, axis=-1, keepdims=True))


def _full(shape):
    return pl.BlockSpec(shape, lambda *_: (0,) * len(shape))


def _rows(tm, width):
    return pl.BlockSpec((tm, width), lambda i: (i, 0))


def _rows_rev(tm, width, n):
    return pl.BlockSpec((tm, width), lambda i: (n - 1 - i, 0))


def _zero_at_start(*refs):
    @pl.when(pl.program_id(0) == 0)
    def _():
        for ref in refs:
            ref[...] = jnp.zeros_like(ref)


class _Exchange:
    def __init__(self, operands, out_shapes, scratch, start, finish, middle=None):
        self.operands, self.out_shapes, self.scratch = list(operands), list(out_shapes), list(scratch)
        self.start, self.middle, self.finish = start, middle, finish


def _call(body, *, name, grid, in_specs, out_specs, out_shape, args, scratch_shapes=(), exchange=None):
    semantics = ("arbitrary",) * len(grid)
    if exchange is None:
        out = pl.pallas_call(
            body, name=name, grid=grid, in_specs=in_specs, out_specs=out_specs, out_shape=out_shape,
            scratch_shapes=list(scratch_shapes), compiler_params=_params(semantics))(*args)
        return out, []
    hbm = pl.BlockSpec(memory_space=pltpu.HBM)
    n_in, n_out, n_scr = len(in_specs), len(out_specs), len(scratch_shapes)
    e_in, e_out = len(exchange.operands), len(exchange.out_shapes)

    def carried(*refs):
        ins, rest = refs[:n_in], refs[n_in:]
        e_ins, rest = rest[:e_in], rest[e_in:]
        outs, rest = rest[:n_out], rest[n_out:]
        e_outs, rest = rest[:e_out], rest[e_out:]
        scr, e_scr = rest[:n_scr], rest[n_scr:]
        first = last = None
        for axis, size in enumerate(grid):
            at_start, at_end = pl.program_id(axis) == 0, pl.program_id(axis) == size - 1
            first = at_start if first is None else jnp.logical_and(first, at_start)
            last = at_end if last is None else jnp.logical_and(last, at_end)

        @pl.when(first)
        def _():
            exchange.start(e_ins, e_outs, e_scr)

        body(*ins, *outs, *scr)

        if exchange.middle is not None:
            assert len(grid) == 1

            @pl.when(pl.program_id(0) == (grid[0] * MIDDLE_EIGHTHS) // 8)
            def _():
                exchange.middle(e_ins, e_outs, e_scr)

        @pl.when(last)
        def _():
            exchange.finish(e_ins, e_outs, e_scr)

    out = pl.pallas_call(
        carried, name=name, grid=grid, in_specs=list(in_specs) + [hbm] * e_in,
        out_specs=list(out_specs) + [hbm] * e_out, out_shape=list(out_shape) + exchange.out_shapes,
        scratch_shapes=list(scratch_shapes) + exchange.scratch,
        compiler_params=pltpu.CompilerParams(
            dimension_semantics=semantics, vmem_limit_bytes=VMEM_LIMIT, has_side_effects=True),
    )(*args, *exchange.operands)
    return out[:n_out], out[n_out:]


def _run_exchange(exchange, name):
    hbm = pl.BlockSpec(memory_space=pltpu.HBM)
    e_in, e_out = len(exchange.operands), len(exchange.out_shapes)

    def body(*refs):
        e_ins, e_outs, e_scr = refs[:e_in], refs[e_in:e_in + e_out], refs[e_in + e_out:]
        exchange.start(e_ins, e_outs, e_scr)
        if exchange.middle is not None:
            exchange.middle(e_ins, e_outs, e_scr)
        exchange.finish(e_ins, e_outs, e_scr)

    return pl.pallas_call(
        body, name=name, in_specs=[hbm] * e_in, out_specs=[hbm] * e_out, out_shape=exchange.out_shapes,
        scratch_shapes=exchange.scratch, compiler_params=pltpu.CompilerParams(has_side_effects=True),
    )(*exchange.operands)


def _loss_head(xo, gf, tgt):
    d = xo.shape[1]
    y, xh, r = _rms(xo, gf)
    err = y - tgt
    dy = err * (1.0 / d)
    loss = 0.5 * jnp.sum(jnp.sum(err * err, axis=-1, keepdims=True) * (1.0 / d), axis=0, keepdims=True)
    return _rms_bwd(dy, xh, r, gf), loss, jnp.sum(dy * xh, axis=0, keepdims=True)


def _ffn_fwd(x, g, wg, wu, wd, exchange=None, head=None):
    t, d = x.shape
    f = wg.shape[0]
    tm = min(TOKEN_TILE, t)

    def body(x_ref, g_ref, wg_ref, wu_ref, wd_ref, *rest):
        if head is None:
            xo_ref, a_ref, b_ref, s_ref = rest
        else:
            gf_ref, tgt_ref, xo_ref, a_ref, b_ref, s_ref, loss_ref, dgf_ref = rest
            _zero_at_start(loss_ref, dgf_ref)
        xv = x_ref[...]
        h, _, _ = _rms(xv, g_ref[...])
        hb = h.astype(BF16)
        a = _dot(hb, wg_ref[...], NT)
        b = _dot(hb, wu_ref[...], NT)
        s = (a * _sigmoid(a) * b).astype(BF16)
        xo = xv + 0.5 * _dot(s, wd_ref[...])
        if head is None:
            xo_ref[...] = xo
        else:
            xo_ref[...], loss, dgf = _loss_head(xo, gf_ref[...], tgt_ref[...])
            loss_ref[...] += jnp.broadcast_to(loss, (1, 128))
            dgf_ref[...] += dgf
        a_ref[...] = a.astype(BF16)
        b_ref[...] = b.astype(BF16)
        s_ref[...] = s

    in_specs = [_rows(tm, d), _full((1, d)), _full((f, d)), _full((f, d)), _full((f, d))]
    out_specs = [_rows(tm, d), _rows(tm, f), _rows(tm, f), _rows(tm, f)]
    out_shape = [
        jax.ShapeDtypeStruct((t, d), F32),
        jax.ShapeDtypeStruct((t, f), BF16),
        jax.ShapeDtypeStruct((t, f), BF16),
        jax.ShapeDtypeStruct((t, f), BF16),
    ]
    args = (x, g, wg, wu, wd)
    if head is not None:
        in_specs += [_full((1, d)), _rows(tm, d)]
        out_specs += [_full((1, 128)), _full((1, d))]
        out_shape += [jax.ShapeDtypeStruct((1, 128), F32), jax.ShapeDtypeStruct((1, d), F32)]
        args += tuple(head)
    return _call(
        body, name="ffn_fwd", grid=(t // tm,), in_specs=in_specs, out_specs=out_specs, out_shape=out_shape,
        args=args, exchange=exchange)


def _ffn_up(x, g, wg, wu, exchange=None):
    t, d = x.shape
    f = wg.shape[0]
    tm = min(TOKEN_TILE, t)

    def body(x_ref, g_ref, wg_ref, wu_ref, a_ref, b_ref, s_ref):
        h, _, _ = _rms(x_ref[...], g_ref[...])
        hb = h.astype(BF16)
        a = _dot(hb, wg_ref[...], NT)
        b = _dot(hb, wu_ref[...], NT)
        a_ref[...] = a.astype(BF16)
        b_ref[...] = b.astype(BF16)
        s_ref[...] = (a * _sigmoid(a) * b).astype(BF16)

    return _call(
        body, name="ffn_up", grid=(t // tm,),
        in_specs=[_rows(tm, d), _full((1, d)), _full((f, d)), _full((f, d))],
        out_specs=[_rows(tm, f)] * 3, out_shape=[jax.ShapeDtypeStruct((t, f), BF16)] * 3,
        args=(x, g, wg, wu), exchange=exchange)


def _ffn_down(x, s, wd, exchange=None):
    t, d = x.shape
    f = wd.shape[0]
    tm = min(TOKEN_TILE, t)

    def body(x_ref, s_ref, wd_ref, xo_ref):
        xo_ref[...] = x_ref[...] + 0.5 * _dot(s_ref[...], wd_ref[...])

    return _call(
        body, name="ffn_down", grid=(t // tm,),
        in_specs=[_rows(tm, d), _rows(tm, f), _full((f, d))],
        out_specs=[_rows(tm, d)], out_shape=[jax.ShapeDtypeStruct((t, d), F32)],
        args=(x, s, wd), exchange=exchange)


def _ffn_bwd(x, g, dxo, a, b, wg, wu, wd, exchange=None):
    t, d = x.shape
    f = wg.shape[0]
    tm = min(TOKEN_TILE, t)

    def body(x_ref, g_ref, dxo_ref, a_ref, b_ref, wg_ref, wu_ref, wd_ref, dx_ref, da_ref, db_ref, h_ref, dg_ref):
        _zero_at_start(dg_ref)
        gv = g_ref[...]
        h, xh, r = _rms(x_ref[...], gv)
        dxo = dxo_ref[...]
        ds = _dot((0.5 * dxo).astype(BF16), wd_ref[...], NT)
        af = a_ref[...].astype(F32)
        bf = b_ref[...].astype(F32)
        sg = _sigmoid(af)
        da = (ds * bf * (sg * (1.0 + af * (1.0 - sg)))).astype(BF16)
        db = (ds * (af * sg)).astype(BF16)
        dh = _dot(da, wg_ref[...]) + _dot(db, wu_ref[...])
        dx_ref[...] = _rms_bwd(dh, xh, r, gv) + dxo
        da_ref[...] = da
        db_ref[...] = db
        h_ref[...] = h.astype(BF16)
        dg_ref[...] += jnp.sum(dh * xh, axis=0, keepdims=True)

    return _call(
        body,
        name="ffn_bwd",
        grid=(t // tm,),
        in_specs=[
            _rows(tm, d), _full((1, d)), _rows(tm, d), _rows(tm, f), _rows(tm, f),
            _full((f, d)), _full((f, d)), _full((f, d)),
        ],
        out_specs=[_rows(tm, d), _rows(tm, f), _rows(tm, f), _rows(tm, d), _full((1, d))],
        out_shape=[
            jax.ShapeDtypeStruct((t, d), F32),
            jax.ShapeDtypeStruct((t, f), BF16),
            jax.ShapeDtypeStruct((t, f), BF16),
            jax.ShapeDtypeStruct((t, d), BF16),
            jax.ShapeDtypeStruct((1, d), F32),
        ],
        args=(x, g, dxo, a, b, wg, wu, wd),
        exchange=exchange,
    )


def _weight_grad(a, b, scale=1.0, exchange=None):
    t, m = a.shape
    n = b.shape[1]
    chips = N_DEV // 2
    r = m // N_DEV
    tk = min(REDUCE_TILE, t)
    halves = 2
    nb = n // halves
    nk = t // tk

    def body(a_ref, b_ref, o_ref, acc, send_buf, recv_buf, send_sems, recv_sems):
        k, j = pl.program_id(0), pl.program_id(1)
        x, y, c, _ = _mesh_place()
        sibling, _ = _peer(x, y, c, 1)
        bv = b_ref[...]
        if scale != 1.0:
            bv = bv * scale
        bb = bv.astype(BF16)
        acc_half = acc.at[j]

        @pl.when(k == 0)
        def _():
            acc_half[...] = jnp.zeros_like(acc_half)

        for i in range(m // MXU_ROWS):
            rows = slice(i * MXU_ROWS, (i + 1) * MXU_ROWS)
            acc_half[rows, :] += _dot(a_ref[:, rows].astype(BF16), bb, TN)

        def to_sibling(half):
            return _remote(send_buf.at[half], recv_buf.at[half], send_sems.at[half], recv_sems.at[half], sibling)

        def owned_rows(q, core):
            return pl.ds(pl.multiple_of((2 * q + core) * r, 8), r)

        for half in range(halves):
            @pl.when(jnp.logical_and(k == nk - 1, j == half))
            def _():
                for q in range(chips):
                    send_buf[half, q] = acc[half, owned_rows(q, 1 - c), :].astype(BF16)
                to_sibling(half).start()

        @pl.when(jnp.logical_and(k == nk - 1, j == halves - 1))
        def _():
            for half in range(halves):
                to_sibling(half).wait_send()
                to_sibling(half).wait_recv()
                for q in range(chips):
                    o_ref[q, :, half * nb:(half + 1) * nb] = (
                        acc[half, owned_rows(q, c), :] + recv_buf[half, q].astype(F32)).astype(BF16)

    (partial,), arrived = _call(
        body,
        name="weight_grad",
        grid=(nk, halves),
        in_specs=[pl.BlockSpec((tk, m), lambda k, j: (k, 0)), pl.BlockSpec((tk, nb), lambda k, j: (k, j))],
        out_specs=[pl.BlockSpec((chips, r, n), lambda k, j: (0, 0, 0))],
        out_shape=[jax.ShapeDtypeStruct((chips, r, n), BF16)],
        scratch_shapes=[
            pltpu.VMEM((halves, m, nb), F32),
            pltpu.VMEM((halves, chips, r, nb), BF16), pltpu.VMEM((halves, chips, r, nb), BF16),
            pltpu.SemaphoreType.DMA((halves,)), pltpu.SemaphoreType.DMA((halves,)),
        ],
        args=(a, b),
        exchange=exchange,
    )
    return partial, arrived


def _chunk_cumsum(v, reverse=False):
    n, width = v.shape
    row = lax.broadcasted_iota(jnp.int32, (n, n), 0)
    col = lax.broadcasted_iota(jnp.int32, (n, n), 1)
    earlier = col >= row if reverse else col <= row
    tri = jnp.where(jnp.logical_and(row // CHUNK == col // CHUNK, earlier), 1.0, 0.0).astype(BF16)
    hi = v.astype(BF16)
    rest = v - hi.astype(F32)
    mid = rest.astype(BF16)
    low = (rest - mid.astype(F32)).astype(BF16)
    sums = _dot(tri, jnp.concatenate([hi, mid, low], axis=1))
    return sums[:, 0:width] + sums[:, width:2 * width] + sums[:, 2 * width:3 * width]


def _shift_rows(v, shift, edge):
    n = v.shape[0]
    row = lax.broadcasted_iota(jnp.int32, (n, 1), 0)
    out = pltpu.roll(v, shift % n, axis=0)
    if shift > 0:
        for j in range(shift):
            out = jnp.where(row == j, edge[8 - shift + j:8 - shift + j + 1, :], out)
    else:
        for j in range(-shift):
            out = jnp.where(row == n + shift + j, edge[j:j + 1, :], out)
    return out


def _gates(z, lbp):
    w = HGRN_W
    lb = _sigmoid(lbp[0:1, :] - lbp[1:2, :])
    zq = z[:, 0:w]
    sig = _sigmoid(z[:, w:2 * w])
    f = lb + (1.0 - lb) * sig
    sq = _sigmoid(zq)
    q = zq * sq * HGRN_DK ** -0.5
    return lb, sig, f, sq, q


def _decayed_operands(q, f, v, qh_buf, kh_buf, kbar_buf, v_buf, etot_buf):
    logf = jnp.log(f)
    bcum = _chunk_cumsum(logf)
    rest = _chunk_cumsum(logf, reverse=True) - logf
    eb, enb, erest = jnp.exp(bcum), jnp.exp(-bcum), jnp.exp(rest)
    kk = 1.0 - f
    qh_buf[...] = (q * eb).astype(BF16)
    kh_buf[...] = (kk * enb).astype(BF16)
    kbar_buf[...] = (kk * erest).astype(BF16)
    v_buf[...] = v.astype(BF16)
    etot_buf[...] = jnp.exp(bcum + rest)
    return eb, enb, erest


def _short_conv(u, edge, cw):
    return cw[0:1, :] * _shift_rows(u, 2, edge) + cw[1:2, :] * _shift_rows(u, 1, edge) + cw[2:3, :] * u


def _block_causal_mask(n):
    row = lax.broadcasted_iota(jnp.int32, (n, n), 0)
    col = lax.broadcasted_iota(jnp.int32, (n, n), 1)
    return jnp.logical_and(row // CHUNK == col // CHUNK, col <= row)


def _spread(v, chunk_of_row, nc):
    return jnp.concatenate([jnp.where(chunk_of_row == c, v, jnp.zeros_like(v)) for c in range(nc)], axis=1)


def _pick(r, chunk_of_row, nc):
    out = jnp.where(chunk_of_row == 0, r[:, 0:HGRN_DK], 0.0)
    for c in range(1, nc):
        out = out + jnp.where(chunk_of_row == c, r[:, c * HGRN_DK:(c + 1) * HGRN_DK], 0.0)
    return out


def _mix_fwd(x, g, w_in, lbp, gh, convw_t, w_out, exchange=None):
    t, d = x.shape
    zw = w_in.shape[0]
    w = HGRN_W
    tm = min(TOKEN_TILE, t)
    nc = tm // CHUNK
    n_chunks = t // CHUNK

    def body(x_ref, g_ref, win_ref, lbp_ref, gh_ref, cw_ref, wout_ref,
             xo_ref, z_ref, o_ref, st_ref, y_ref, state, ucarry, qh_buf, kh_buf, kbar_buf, v_buf, etot_buf):
        _zero_at_start(state, ucarry)
        xv = x_ref[...]
        h, _, _ = _rms(xv, g_ref[...])
        z_ref[...] = _dot(h.astype(BF16), win_ref[...], NT)
        z = z_ref[...]
        _, _, f, _, q = _gates(z, lbp_ref[...])
        _decayed_operands(q, f, z[:, 2 * w:3 * w], qh_buf, kh_buf, kbar_buf, v_buf, etot_buf)
        mask = _block_causal_mask(tm)
        chunk_of_row = lax.broadcasted_iota(jnp.int32, (tm, 1), 0) // CHUNK
        heads = range(HGRN_HEADS)
        hcols = [slice(hd * HGRN_DK, (hd + 1) * HGRN_DK) for hd in heads]
        qh = [qh_buf[:, hcols[hd]] for hd in heads]
        vb = [v_buf[:, hcols[hd]] for hd in heads]
        scores = [jnp.where(mask, _dot(qh[hd], kh_buf[:, hcols[hd]], NT), 0.0).astype(BF16) for hd in heads]
        gains = [_dot(_spread(vb[hd], chunk_of_row, nc), kbar_buf[:, hcols[hd]], TN) for hd in heads]
        entering = []
        for hd in heads:
            states, st = [], state[hd]
            for c in range(nc):
                states.append(st)
                st_ref[c, hd] = st
                st = st * etot_buf[c * CHUNK:c * CHUNK + 1, hcols[hd]] + gains[hd][c * HGRN_DK:(c + 1) * HGRN_DK, :]
            state[hd] = st
            entering.append(jnp.concatenate(states, axis=0).astype(BF16))
        from_states = [_dot(qh[hd], entering[hd], NT) for hd in heads]
        o_ref[...] = jnp.concatenate(
            [_dot(scores[hd], vb[hd]) + _pick(from_states[hd], chunk_of_row, nc) for hd in heads], axis=1)
        ghv = gh_ref[...]
        for hd in range(HGRN_HEADS):
            cols = slice(hd * HGRN_DK, (hd + 1) * HGRN_DK)
            on, _, _ = _rms(o_ref[:, cols], ghv[:, cols])
            zg = z[:, 3 * w + hd * HGRN_DK:3 * w + (hd + 1) * HGRN_DK]
            y_ref[:, cols] = (on * (zg * _sigmoid(zg))).astype(BF16)
        u = z[:, 5 * w:6 * w] * z[:, 6 * w:7 * w]
        conv = _short_conv(u, ucarry[...], cw_ref[...])
        ucarry[...] = u[tm - 8:tm, :]
        y_ref[:, w:2 * w] = (z[:, 4 * w:5 * w] * conv).astype(BF16)
        xo_ref[...] = xv + _dot(y_ref[...], wout_ref[...])

    return _call(
        body,
        name="mix_fwd",
        grid=(t // tm,),
        in_specs=[
            _rows(tm, d), _full((1, d)), _full((zw, d)), _full((2, w)), _full((1, w)), _full((3, w)),
            _full((2 * w, d)),
        ],
        out_specs=[
            _rows(tm, d), _rows(tm, zw), _rows(tm, w),
            pl.BlockSpec((nc, HGRN_HEADS, HGRN_DK, HGRN_DK), lambda i: (i, 0, 0, 0)),
            _rows(tm, 2 * w),
        ],
        out_shape=[
            jax.ShapeDtypeStruct((t, d), F32),
            jax.ShapeDtypeStruct((t, zw), F32),
            jax.ShapeDtypeStruct((t, w), F32),
            jax.ShapeDtypeStruct((n_chunks, HGRN_HEADS, HGRN_DK, HGRN_DK), F32),
            jax.ShapeDtypeStruct((t, 2 * w), BF16),
        ],
        scratch_shapes=[
            pltpu.VMEM((HGRN_HEADS, HGRN_DK, HGRN_DK), F32), pltpu.VMEM((8, w), F32),
            pltpu.VMEM((tm, w), BF16), pltpu.VMEM((tm, w), BF16), pltpu.VMEM((tm, w), BF16),
            pltpu.VMEM((tm, w), BF16), pltpu.VMEM((tm, w), F32),
        ],
        args=(x, g, w_in, lbp, gh, convw_t, w_out),
        exchange=exchange,
    )


def _mix_bwd(x, g, dxo, z, o, states, w_in, lbp, gh, convw_t, w_out, exchange=None):
    t, d = x.shape
    zw = w_in.shape[0]
    w = HGRN_W
    tm = min(TOKEN_TILE, t)
    nc = tm // CHUNK
    n = t // tm

    def body(x_ref, g_ref, dxo_ref, z_ref, zprev_ref, o_ref, st_ref, win_ref, lbp_ref, gh_ref, cw_ref, wout_ref,
             dx_ref, dz_ref, h_ref, dg_ref, dlbp_ref, dgh_ref, dcw_ref,
             dstate, dcarry, do_buf, qh_buf, kh_buf, kbar_buf, v_buf, etot_buf):
        _zero_at_start(dstate, dcarry, dg_ref, dlbp_ref, dgh_ref, dcw_ref)
        gv = g_ref[...]
        h, xh, r = _rms(x_ref[...], gv)
        h_ref[...] = h.astype(BF16)
        dxo = dxo_ref[...]
        dy = _dot(dxo.astype(BF16), wout_ref[...], NT)
        z = z_ref[...]
        lb, sig, f, sq, q = _gates(z, lbp_ref[...])
        eb, enb, erest = _decayed_operands(q, f, z[:, 2 * w:3 * w], qh_buf, kh_buf, kbar_buf, v_buf, etot_buf)

        ghv = gh_ref[...]
        dgh_parts = []
        for hd in range(HGRN_HEADS):
            cols = slice(hd * HGRN_DK, (hd + 1) * HGRN_DK)
            gcols = slice(3 * w + hd * HGRN_DK, 3 * w + (hd + 1) * HGRN_DK)
            on, oh, rr = _rms(o_ref[:, cols], ghv[:, cols])
            zg = z[:, gcols]
            sgz = _sigmoid(zg)
            dyh = dy[:, cols]
            don = dyh * (zg * sgz)
            dz_ref[:, gcols] = (dyh * on * (sgz * (1.0 + zg * (1.0 - sgz)))).astype(BF16)
            dgh_parts.append(jnp.sum(don * oh, axis=0, keepdims=True))
            do_buf[:, cols] = _rms_bwd(don, oh, rr, ghv[:, cols]).astype(BF16)
        dgh_ref[...] += jnp.concatenate(dgh_parts, axis=1)

        zb = z[:, 4 * w:5 * w]
        zc = z[:, 5 * w:6 * w]
        zu = z[:, 6 * w:7 * w]
        u = zc * zu
        cw = cw_ref[...]
        zp = zprev_ref[...]
        uprev = jnp.where(pl.program_id(0) == n - 1, 0.0, zp[:, 5 * w:6 * w] * zp[:, 6 * w:7 * w])
        dyc = dy[:, w:2 * w]
        dz_ref[:, 4 * w:5 * w] = (dyc * _short_conv(u, uprev, cw)).astype(BF16)
        dconv = dyc * zb
        edge = dcarry[...]
        dconv1 = _shift_rows(dconv, -1, edge)
        dconv2 = _shift_rows(dconv, -2, edge)
        dcarry[...] = dconv[0:8, :]
        du = cw[2:3, :] * dconv + cw[1:2, :] * dconv1 + cw[0:1, :] * dconv2
        dz_ref[:, 5 * w:6 * w] = (du * zu).astype(BF16)
        dz_ref[:, 6 * w:7 * w] = (du * zc).astype(BF16)
        dcw_ref[...] += jnp.concatenate([
            jnp.sum(u * dconv2, axis=0, keepdims=True),
            jnp.sum(u * dconv1, axis=0, keepdims=True),
            jnp.sum(u * dconv, axis=0, keepdims=True)], axis=0)

        mask = _block_causal_mask(tm)
        chunk_of_row = lax.broadcasted_iota(jnp.int32, (tm, 1), 0) // CHUNK
        heads = range(HGRN_HEADS)
        hcols = [slice(hd * HGRN_DK, (hd + 1) * HGRN_DK) for hd in heads]
        qhb = [qh_buf[:, hcols[hd]] for hd in heads]
        khb = [kh_buf[:, hcols[hd]] for hd in heads]
        vb = [v_buf[:, hcols[hd]] for hd in heads]
        dob = [do_buf[:, hcols[hd]] for hd in heads]
        scores = [jnp.where(mask, _dot(qhb[hd], khb[hd], NT), 0.0).astype(BF16) for hd in heads]
        dscores = [jnp.where(mask, _dot(dob[hd], vb[hd], NT), 0.0).astype(BF16) for hd in heads]
        gains = [_dot(_spread(dob[hd], chunk_of_row, nc), qhb[hd], TN) for hd in heads]
        dst_rows, dst_lanes, st_lanes, carries = [], [], [], []
        for hd in heads:
            entering = [st_ref[c, hd] for c in range(nc)]
            leaving, carried_back = [None] * nc, [None] * nc
            dst = dstate[hd]
            for c in reversed(range(nc)):
                elast = etot_buf[c * CHUNK:c * CHUNK + 1, hcols[hd]]
                leaving[c] = dst
                carried_back[c] = jnp.sum(dst * entering[c], axis=0, keepdims=True) * elast
                dst = dst * elast + gains[hd][c * HGRN_DK:(c + 1) * HGRN_DK, :]
            dstate[hd] = dst
            dst_rows.append(jnp.concatenate(leaving, axis=0).astype(BF16))
            dst_lanes.append(jnp.concatenate(leaving, axis=1).astype(BF16))
            st_lanes.append(jnp.concatenate(entering, axis=1).astype(BF16))
            carries.append(carried_back)
        dv = [_dot(scores[hd], dob[hd], TN) + _pick(_dot(kbar_buf[:, hcols[hd]], dst_rows[hd], NT), chunk_of_row, nc)
              for hd in heads]
        dz_ref[:, 2 * w:3 * w] = jnp.concatenate(dv, axis=1).astype(BF16)
        dqh = jnp.concatenate(
            [_dot(dscores[hd], khb[hd]) + _pick(_dot(dob[hd], st_lanes[hd]), chunk_of_row, nc) for hd in heads], axis=1)
        dkh = jnp.concatenate([_dot(dscores[hd], qhb[hd], TN) for hd in heads], axis=1)
        dkbar = jnp.concatenate([_pick(_dot(vb[hd], dst_lanes[hd]), chunk_of_row, nc) for hd in heads], axis=1)

        kbar_dkbar = kbar_buf[...].astype(F32) * dkbar
        db = qh_buf[...].astype(F32) * dqh - kh_buf[...].astype(F32) * dkh - kbar_dkbar
        through_last = jnp.concatenate([
            jnp.broadcast_to(
                jnp.sum(kbar_dkbar[c * CHUNK:(c + 1) * CHUNK], axis=0, keepdims=True)
                + jnp.concatenate([carries[hd][c] for hd in heads], axis=1),
                (CHUNK, w))
            for c in range(nc)], axis=0)
        dlogf = _chunk_cumsum(db, reverse=True) + through_last
        df = dlogf / f - (dkh * enb + dkbar * erest)
        zq = z[:, 0:w]
        dz_ref[:, 0:w] = (dqh * eb * HGRN_DK ** -0.5 * (sq * (1.0 + zq * (1.0 - sq)))).astype(BF16)
        dz_ref[:, w:2 * w] = (df * (1.0 - lb) * sig * (1.0 - sig)).astype(BF16)
        dlb = jnp.sum(df * (1.0 - sig), axis=0, keepdims=True) * lb * (1.0 - lb)
        dlbp_ref[...] += jnp.concatenate([dlb, -dlb], axis=0)

        dh = _dot(dz_ref[...], win_ref[...])
        dx_ref[...] = _rms_bwd(dh, xh, r, gv) + dxo
        dg_ref[...] += jnp.sum(dh * xh, axis=0, keepdims=True)

    return _call(
        body,
        name="mix_bwd",
        grid=(n,),
        in_specs=[
            _rows_rev(tm, d, n), _full((1, d)), _rows_rev(tm, d, n), _rows_rev(tm, zw, n),
            pl.BlockSpec((8, zw), lambda i: (jnp.maximum((n - 1 - i) * (tm // 8) - 1, 0), 0)),
            _rows_rev(tm, w, n),
            pl.BlockSpec((nc, HGRN_HEADS, HGRN_DK, HGRN_DK), lambda i: (n - 1 - i, 0, 0, 0)),
            _full((zw, d)), _full((2, w)), _full((1, w)), _full((3, w)), _full((2 * w, d)),
        ],
        out_specs=[
            _rows_rev(tm, d, n), _rows_rev(tm, zw, n), _rows_rev(tm, d, n),
            _full((1, d)), _full((2, w)), _full((1, w)), _full((3, w)),
        ],
        out_shape=[
            jax.ShapeDtypeStruct((t, d), F32),
            jax.ShapeDtypeStruct((t, zw), BF16),
            jax.ShapeDtypeStruct((t, d), BF16),
            jax.ShapeDtypeStruct((1, d), F32),
            jax.ShapeDtypeStruct((2, w), F32),
            jax.ShapeDtypeStruct((1, w), F32),
            jax.ShapeDtypeStruct((3, w), F32),
        ],
        scratch_shapes=[
            pltpu.VMEM((HGRN_HEADS, HGRN_DK, HGRN_DK), F32), pltpu.VMEM((8, w), F32),
            pltpu.VMEM((tm, w), BF16),
            pltpu.VMEM((tm, w), BF16), pltpu.VMEM((tm, w), BF16), pltpu.VMEM((tm, w), BF16),
            pltpu.VMEM((tm, w), BF16), pltpu.VMEM((tm, w), F32),
        ],
        args=(x, g, dxo, z, z, o, states, w_in, lbp, gh, convw_t, w_out),
        exchange=exchange,
    )


def _memkv_fwd(mem, g, wkv):
    m, d = mem.shape
    nb, _, cb = wkv.shape

    def body(mem_ref, g_ref, wkv_ref, kv_ref):
        mn, _, _ = _rms(mem_ref[...], g_ref[...])
        mnb = mn.astype(BF16)
        for j in range(nb):
            kv_ref[:, j * cb:(j + 1) * cb] = _dot(mnb, wkv_ref[j]).astype(BF16)

    return pl.pallas_call(
        body,
        name="memkv_fwd",
        out_shape=jax.ShapeDtypeStruct((m, nb * cb), BF16),
        compiler_params=_params(),
    )(mem, g, wkv)


def _memkv_bwd(mem, g, dkv, wkv):
    m, d = mem.shape
    nb, _, cb = wkv.shape
    chips = nb // 2

    def body(mem_ref, g_ref, dkv_ref, wkv_ref, dw_ref, dg_ref, dw_all, send_buf, recv_buf, send_sem, recv_sem):
        x, y, c, _ = _mesh_place()
        sibling, _ = _peer(x, y, c, 1)
        mn, xh, _ = _rms(mem_ref[...], g_ref[...])
        mnb = mn.astype(BF16)
        dmn = jnp.zeros((m, d), F32)
        for j in range(nb):
            dkvb = dkv_ref[:, j * cb:(j + 1) * cb].astype(BF16)
            dw_all[j] = _dot(mnb, dkvb, TN)
            dmn = dmn + _dot(dkvb, wkv_ref[j], NT)
        dg_ref[...] = jnp.sum(dmn * xh, axis=0, keepdims=True)
        for q in range(chips):
            send_buf[q] = dw_all[2 * q + 1 - c].astype(BF16)
        to_sibling = _remote(send_buf, recv_buf, send_sem, recv_sem, sibling)
        to_sibling.start()
        to_sibling.wait_send()
        to_sibling.wait_recv()
        for q in range(chips):
            dw_ref[q] = (dw_all[2 * q + c] + recv_buf[q].astype(F32)).astype(BF16)

    return pl.pallas_call(
        body,
        name="memkv_bwd",
        out_shape=[jax.ShapeDtypeStruct((chips, d, cb), BF16), jax.ShapeDtypeStruct((1, d), F32)],
        scratch_shapes=[
            pltpu.VMEM((nb, d, cb), F32), pltpu.VMEM((chips, d, cb), BF16), pltpu.VMEM((chips, d, cb), BF16),
            pltpu.SemaphoreType.DMA, pltpu.SemaphoreType.DMA,
        ],
        compiler_params=_params(),
    )(mem, g, dkv, wkv)


def _softmax_rows(qm_h, k_h):
    sc = _dot(qm_h, k_h, NT) * MEM_HD ** -0.5
    e = jnp.exp(sc - jnp.max(sc, axis=-1, keepdims=True))
    return e / jnp.sum(e, axis=-1, keepdims=True)


def _xattn_fwd(x, g, wq, kv, wo, exchange=None):
    t, d = x.shape
    m = kv.shape[0]
    tm = min(TOKEN_TILE, t)

    def body(x_ref, g_ref, wq_ref, kv_ref, wo_ref, xo_ref, hq_ref, qm_ref, att_ref):
        xv = x_ref[...]
        h, _, _ = _rms(xv, g_ref[...])
        hb = h.astype(BF16)
        hq_ref[...] = hb
        qm = _dot(hb, wq_ref[...]).astype(BF16)
        qm_ref[...] = qm
        heads = range(MEM_HEADS)
        kcols = [slice(hd * MEM_HD, (hd + 1) * MEM_HD) for hd in heads]
        p = [_softmax_rows(qm[:, kcols[hd]], kv_ref[:, kcols[hd]]) for hd in heads]
        att = jnp.concatenate(
            [_dot(p[hd].astype(BF16), kv_ref[:, d + hd * MEM_HD:d + (hd + 1) * MEM_HD]) for hd in heads],
            axis=1).astype(BF16)
        att_ref[...] = att
        xo_ref[...] = xv + _dot(att, wo_ref[...])

    return _call(
        body,
        name="xattn_fwd",
        grid=(t // tm,),
        in_specs=[_rows(tm, d), _full((1, d)), _full((d, d)), _full((m, 2 * d)), _full((d, d))],
        out_specs=[_rows(tm, d), _rows(tm, d), _rows(tm, d), _rows(tm, d)],
        out_shape=[
            jax.ShapeDtypeStruct((t, d), F32),
            jax.ShapeDtypeStruct((t, d), BF16),
            jax.ShapeDtypeStruct((t, d), BF16),
            jax.ShapeDtypeStruct((t, d), BF16),
        ],
        args=(x, g, wq, kv, wo),
        exchange=exchange,
    )


def _xattn_bwd(x, g, dxo, qm, kv, wq, wo, exchange=None):
    t, d = x.shape
    m = kv.shape[0]
    tm = min(TOKEN_TILE, t)

    def body(x_ref, g_ref, dxo_ref, qm_ref, kv_ref, wq_ref, wo_ref, dx_ref, dqm_ref, dkv_ref, dg_ref):
        _zero_at_start(dkv_ref, dg_ref)
        gv = g_ref[...]
        _, xh, r = _rms(x_ref[...], gv)
        dxo = dxo_ref[...]
        datt = _dot(dxo.astype(BF16), wo_ref[...], NT).astype(BF16)
        heads = range(MEM_HEADS)
        kcols = [slice(hd * MEM_HD, (hd + 1) * MEM_HD) for hd in heads]
        vcols = [slice(d + hd * MEM_HD, d + (hd + 1) * MEM_HD) for hd in heads]
        qm_h = [qm_ref[:, kcols[hd]] for hd in heads]
        p = [_softmax_rows(qm_h[hd], kv_ref[:, kcols[hd]]) for hd in heads]
        dp = [_dot(datt[:, kcols[hd]], kv_ref[:, vcols[hd]], NT) for hd in heads]
        dsc = [(p[hd] * (dp[hd] - jnp.sum(p[hd] * dp[hd], axis=-1, keepdims=True)) * MEM_HD ** -0.5).astype(BF16)
               for hd in heads]
        dqm = jnp.concatenate([_dot(dsc[hd], kv_ref[:, kcols[hd]]) for hd in heads], axis=1).astype(BF16)
        dqm_ref[...] = dqm
        dkv_ref[...] += jnp.concatenate(
            [_dot(dsc[hd], qm_h[hd], TN) for hd in heads]
            + [_dot(p[hd].astype(BF16), datt[:, kcols[hd]], TN) for hd in heads], axis=1)
        dh = _dot(dqm, wq_ref[...], NT)
        dx_ref[...] = _rms_bwd(dh, xh, r, gv) + dxo
        dg_ref[...] += jnp.sum(dh * xh, axis=0, keepdims=True)

    return _call(
        body,
        name="xattn_bwd",
        grid=(t // tm,),
        in_specs=[
            _rows(tm, d), _full((1, d)), _rows(tm, d), _rows(tm, d), _full((m, 2 * d)), _full((d, d)), _full((d, d)),
        ],
        out_specs=[_rows(tm, d), _rows(tm, d), _full((m, 2 * d)), _full((1, d))],
        out_shape=[
            jax.ShapeDtypeStruct((t, d), F32),
            jax.ShapeDtypeStruct((t, d), BF16),
            jax.ShapeDtypeStruct((m, 2 * d), F32),
            jax.ShapeDtypeStruct((1, d), F32),
        ],
        args=(x, g, dxo, qm, kv, wq, wo),
        exchange=exchange,
    )


def _mesh_place():
    x, y, c = lax.axis_index("x"), lax.axis_index("y"), lax.axis_index("c")
    return x, y, c, 4 * x + 2 * y + c


def _peer(x, y, c, k):
    px = 1 - x if k & 4 else x
    py = 1 - y if k & 2 else y
    pc = 1 - c if k & 1 else c
    return (px, py, pc), 4 * px + 2 * py + pc


ICI_HOPS = (2, 4, 6)
N_HOPS = len(ICI_HOPS)


def _remote(src, dst, send_sem, recv_sem, peer):
    return pltpu.make_async_remote_copy(
        src_ref=src, dst_ref=dst, send_sem=send_sem, recv_sem=recv_sem, device_id=peer, device_id_type=MESH_IDS)


def _gather_exchange(shards):
    n = len(shards)

    def place():
        x, y, c, me = _mesh_place()
        sibling, _ = _peer(x, y, c, 1)
        to_x, from_x = _peer(x, y, c, 4)
        to_y, from_y = _peer(x, y, c, 2)
        _, from_diagonal = _peer(x, y, c, 6)
        onward = (c * to_y[0] + (1 - c) * to_x[0], c * to_y[1] + (1 - c) * to_x[1], c)
        passed_on = c * from_x + (1 - c) * from_y
        return me, sibling, (to_x, to_y, onward), (from_x, from_y, from_diagonal), passed_on

    def start(src, dst, sems):
        ici_send, ici_recv, pair_send, pair_recv, local = sems
        me, sibling, targets, _, _ = place()
        for a in range(n):
            pltpu.make_async_copy(src[a], dst[a].at[me], local.at[a]).start()
            for j in range(2):
                _remote(src[a], dst[a].at[me], ici_send.at[a, j], ici_recv.at[a, j], targets[j]).start()
            _remote(src[a], dst[a].at[me], pair_send.at[a, 0], pair_recv.at[a, 0], sibling).start()

    def to_sibling(dst, sems, a, j, origin, sibling):
        _, _, pair_send, pair_recv, _ = sems
        slot = dst[a].at[origin]
        return _remote(slot, slot, pair_send.at[a, 1 + j], pair_recv.at[a, 1 + j], sibling)

    def middle(src, dst, sems):
        ici_send, ici_recv, _, _, _ = sems
        _, sibling, targets, origins, passed_on = place()
        for a in range(n):
            for j in range(2):
                _remote(src[a], dst[a].at[origins[j]], ici_send.at[a, j], ici_recv.at[a, j], targets[j]).wait_recv()
            slot = dst[a].at[passed_on]
            _remote(slot, slot, ici_send.at[a, 2], ici_recv.at[a, 2], targets[2]).start()
            for j in range(2):
                to_sibling(dst, sems, a, j, origins[j], sibling).start()

    def finish(src, dst, sems):
        ici_send, ici_recv, pair_send, pair_recv, local = sems
        me, sibling, targets, origins, _ = place()
        for a in range(n):
            _remote(src[a], dst[a].at[origins[2]], ici_send.at[a, 2], ici_recv.at[a, 2], targets[2]).wait_recv()
            to_sibling(dst, sems, a, 2, origins[2], sibling).start()
        for a in range(n):
            pltpu.make_async_copy(src[a], dst[a].at[me], local.at[a]).wait()
            for j in range(N_HOPS):
                _remote(src[a], dst[a].at[me], ici_send.at[a, j], ici_recv.at[a, j], targets[j]).wait_send()
            for j, origin in enumerate((me,) + origins):
                from_sibling = origin + 1 - 2 * (origin % 2)
                passed = _remote(src[a], dst[a].at[from_sibling], pair_send.at[a, j], pair_recv.at[a, j], sibling)
                passed.wait_send()
                passed.wait_recv()

    return _Exchange(
        shards,
        [jax.ShapeDtypeStruct((N_DEV,) + s.shape, s.dtype) for s in shards],
        [
            pltpu.SemaphoreType.DMA((n, N_HOPS)), pltpu.SemaphoreType.DMA((n, N_HOPS)),
            pltpu.SemaphoreType.DMA((n, N_HOPS + 1)), pltpu.SemaphoreType.DMA((n, N_HOPS + 1)),
            pltpu.SemaphoreType.DMA((n,)),
        ],
        start, finish, middle)


def _scatter_copies(src, dst, sems, n, arrivals=False):
    send, recv, local = sems
    x, y, c, _ = _mesh_place()
    chip = 2 * x + y
    if arrivals is None:
        return [pltpu.make_async_copy(src[a].at[chip], dst[a].at[chip], local.at[a]) for a in range(n)]
    copies = []
    for a in range(n):
        for j, k in enumerate(ICI_HOPS):
            peer, _ = _peer(x, y, c, k)
            peer_chip = 2 * peer[0] + peer[1]
            slot = dst[a].at[peer_chip if arrivals else chip]
            copies.append(_remote(src[a].at[peer_chip], slot, send.at[a, j], recv.at[a, j], peer))
    return copies


def _scatter_start(src, dst, sems, n):
    for cp in _scatter_copies(src, dst, sems, n, arrivals=None) + _scatter_copies(src, dst, sems, n):
        cp.start()


def _scatter_finish(src, dst, sems, n):
    for cp in _scatter_copies(src, dst, sems, n, arrivals=None):
        cp.wait()
    for cp in _scatter_copies(src, dst, sems, n):
        cp.wait_send()
    for cp in _scatter_copies(src, dst, sems, n, arrivals=True):
        cp.wait_recv()


def _scatter_scratch(n):
    return [pltpu.SemaphoreType.DMA((n, N_HOPS)), pltpu.SemaphoreType.DMA((n, N_HOPS)), pltpu.SemaphoreType.DMA((n,))]


def _scatter_exchange(partials):
    n = len(partials)
    return _Exchange(
        partials, [jax.ShapeDtypeStruct(p.shape, p.dtype) for p in partials], _scatter_scratch(n),
        lambda src, dst, sems: _scatter_start(src, dst, sems, n),
        lambda src, dst, sems: _scatter_finish(src, dst, sems, n))


SMALL_LAYOUT = {
    "ffn1_norm": (0, 1, 1024), "mix_norm": (1, 1, 1024), "xattn_norm": (2, 1, 1024), "mem_norm": (3, 1, 1024),
    "ffn2_norm": (4, 1, 1024), "final_norm": (5, 1, 1024), "lb_param": (6, 2, 512), "hgrn_out_norm": (8, 1, 512),
    "conv_w": (9, 3, 512), "loss": (12, 1, 128),
}


def _final_exchange(partials, small):
    n = len(partials)
    names = list(small)
    width = 1024

    def body(*refs):
        src = refs[:n]
        pieces = refs[n:n + len(names)]
        dst = refs[n + len(names):2 * n + len(names)]
        total_ref = refs[2 * n + len(names)]
        pack, gathered, small_send, small_recv = refs[2 * n + len(names) + 1:2 * n + len(names) + 5]
        sems = refs[2 * n + len(names) + 5:]
        x, y, c, me = _mesh_place()
        pack[...] = jnp.zeros_like(pack)
        for name, piece in zip(names, pieces):
            row, nrows, ncols = SMALL_LAYOUT[name]
            pack[row:row + nrows, 0:ncols] = piece[...]
        for k in range(1, N_DEV):
            peer, _ = _peer(x, y, c, k)
            _remote(pack, gathered.at[me], small_send.at[k - 1], small_recv.at[k - 1], peer).start()
        _scatter_start(src, dst, sems, n)
        gathered[me] = pack[...]
        for k in range(1, N_DEV):
            peer, peer_index = _peer(x, y, c, k)
            landed = _remote(pack, gathered.at[peer_index], small_send.at[k - 1], small_recv.at[k - 1], peer)
            landed.wait_send()
            landed.wait_recv()
        total = gathered[0]
        for j in range(1, N_DEV):
            total = total + gathered[j]
        total_ref[...] = total
        _scatter_finish(src, dst, sems, n)

    hbm = pl.BlockSpec(memory_space=pltpu.HBM)
    vmem = pl.BlockSpec(memory_space=pltpu.VMEM)
    out = pl.pallas_call(
        body,
        name="final_exchange",
        in_specs=[hbm] * n + [vmem] * len(names),
        out_specs=[hbm] * n + [vmem],
        out_shape=[jax.ShapeDtypeStruct(p.shape, p.dtype) for p in partials]
        + [jax.ShapeDtypeStruct((SMALL_ROWS, width), F32)],
        scratch_shapes=[
            pltpu.VMEM((SMALL_ROWS, width), F32), pltpu.VMEM((N_DEV, SMALL_ROWS, width), F32),
            pltpu.SemaphoreType.DMA((N_DEV - 1,)), pltpu.SemaphoreType.DMA((N_DEV - 1,)),
        ] + _scatter_scratch(n),
        compiler_params=pltpu.CompilerParams(has_side_effects=True),
    )(*partials, *[small[k] for k in names])
    return out[:n], out[n]


def _adamw_math(w, g, m, v):
    m = ADAM_B1 * m + (1.0 - ADAM_B1) * g
    v = ADAM_B2 * v + (1.0 - ADAM_B2) * (g * g)
    m_hat = m / (1.0 - ADAM_B1 ** ADAM_STEP)
    v_hat = v / (1.0 - ADAM_B2 ** ADAM_STEP)
    delta = -ADAM_LR * (m_hat / (jnp.sqrt(v_hat) + ADAM_EPS) + ADAM_WD * w)
    return delta, m, v


def _adamw_shard(parts, w, m, v):
    r, c = w.shape
    n_parts = parts.shape[0]
    tr = max(rows for rows in range(16, r + 1, 16) if r % rows == 0 and rows * c <= ADAMW_TILE_ELEMENTS)

    def body(p_ref, w_ref, m_ref, v_ref, g_ref, d_ref, mo_ref, vo_ref):
        g = p_ref[0].astype(F32)
        for j in range(1, n_parts):
            g = g + p_ref[j].astype(F32)
        delta, mn, vn = _adamw_math(w_ref[...], g, m_ref[...], v_ref[...])
        g_ref[...] = g
        d_ref[...] = delta
        mo_ref[...] = mn
        vo_ref[...] = vn

    tile = pl.BlockSpec((tr, c), lambda i: (i, 0))
    return pl.pallas_call(
        body,
        name="adamw_shard",
        grid=(r // tr,),
        in_specs=[pl.BlockSpec((n_parts, tr, c), lambda i: (0, i, 0)), tile, tile, tile],
        out_specs=[tile] * 4,
        out_shape=[jax.ShapeDtypeStruct((r, c), F32)] * 4,
        compiler_params=_params(("parallel",)),
    )(parts, w, m, v)


def _adamw_small(gs, ws, ms, vs):
    n = len(gs)

    def body(*refs):
        g_refs, w_refs, m_refs, v_refs = refs[:n], refs[n:2 * n], refs[2 * n:3 * n], refs[3 * n:4 * n]
        d_out, m_out, v_out = refs[4 * n:5 * n], refs[5 * n:6 * n], refs[6 * n:7 * n]
        for i in range(n):
            delta, mn, vn = _adamw_math(w_refs[i][...], g_refs[i][...], m_refs[i][...], v_refs[i][...])
            d_out[i][...] = delta
            m_out[i][...] = mn
            v_out[i][...] = vn

    shapes = [jax.ShapeDtypeStruct(w.shape, F32) for w in ws]
    out = pl.pallas_call(
        body,
        name="adamw_small",
        out_shape=shapes * 3,
        compiler_params=_params(),
    )(*gs, *ws, *ms, *vs)
    return out[:n], out[n:2 * n], out[2 * n:]


TRANSPOSED = ("ffn1_gate", "ffn1_up", "w_in", "ffn2_gate", "ffn2_up", "conv_w")
GROUP_FFN1 = ("ffn1_gate", "ffn1_up", "ffn1_down")
GROUP_MIX = ("w_in", "w_out")
GROUP_XATTN = ("w_q_mem", "w_kv_mem", "w_o_mem")
GROUP_FFN2 = ("ffn2_gate", "ffn2_up", "ffn2_down")
LARGE = GROUP_FFN1 + GROUP_MIX + GROUP_XATTN + GROUP_FFN2
SMALL = ("ffn1_norm", "mix_norm", "lb_param", "hgrn_out_norm", "conv_w", "xattn_norm", "mem_norm", "ffn2_norm",
         "final_norm")
WEIGHTS = ("ffn1_norm", "ffn1_gate", "ffn1_up", "ffn1_down", "mix_norm", "w_in", "lb_param", "hgrn_out_norm",
           "conv_w", "w_out", "xattn_norm", "mem_norm", "w_q_mem", "w_kv_mem", "w_o_mem", "ffn2_norm", "ffn2_gate",
           "ffn2_up", "ffn2_down", "final_norm")


def kernel(x, mem, ffn1_norm, ffn1_gate, ffn1_up, ffn1_down, mix_norm, w_in, lb_param, hgrn_out_norm, conv_w, w_out, xattn_norm, mem_norm, w_q_mem, w_kv_mem, w_o_mem, ffn2_norm, ffn2_gate, ffn2_up, ffn2_down, final_norm, loss_target, m_ffn1_norm, m_ffn1_gate, m_ffn1_up, m_ffn1_down, m_mix_norm, m_w_in, m_lb_param, m_hgrn_out_norm, m_conv_w, m_w_out, m_xattn_norm, m_mem_norm, m_w_q_mem, m_w_kv_mem, m_w_o_mem, m_ffn2_norm, m_ffn2_gate, m_ffn2_up, m_ffn2_down, m_final_norm, v_ffn1_norm, v_ffn1_gate, v_ffn1_up, v_ffn1_down, v_mix_norm, v_w_in, v_lb_param, v_hgrn_out_norm, v_conv_w, v_w_out, v_xattn_norm, v_mem_norm, v_w_q_mem, v_w_kv_mem, v_w_o_mem, v_ffn2_norm, v_ffn2_gate, v_ffn2_up, v_ffn2_down, v_final_norm):
    given = dict(locals())
    me = 4 * lax.axis_index("x") + 2 * lax.axis_index("y") + lax.axis_index("c")
    x0, memv, target = x[0], mem[0], loss_target[0]

    def shard(prefix, name):
        v = given[prefix + name]
        if v.ndim == 1:
            return v.reshape(1, -1)
        if v.ndim == 2:
            return v
        return v[0].T if name in TRANSPOSED else v[0]

    w = {name: shard("", name) for name in WEIGHTS}
    m = {name: shard("m_", name) for name in WEIGHTS}
    v = {name: shard("v_", name) for name in WEIGHTS}

    conv_taps, conv_rows = w["conv_w"].shape
    conv_tile = jnp.pad(w["conv_w"], ((0, 8 - conv_taps), (0, 128 - conv_rows)))
    wire = {name: w[name].astype(BF16) for name in LARGE}
    full = {}

    def landed(names, gathered):
        for name, blocks in zip(names, gathered):
            _, r, c = blocks.shape
            full[name] = blocks if name == "w_kv_mem" else blocks.reshape(N_DEV * r, c)

    first = ("ffn1_gate", "ffn1_up")
    landed(first, _run_exchange(_gather_exchange([wire[k] for k in first]), "gather_first"))

    riders = (("ffn1_down", "w_in"), ("w_out", "w_kv_mem"), ("w_q_mem", "w_o_mem", "ffn2_gate", "ffn2_up"),
              ("ffn2_down",))
    (a1, b1, s1), gathered = _ffn_up(
        x0, w["ffn1_norm"], full["ffn1_gate"], full["ffn1_up"],
        exchange=_gather_exchange([wire[k] for k in riders[0]]))
    landed(riders[0], gathered)
    (x1,), gathered = _ffn_down(
        x0, s1, full["ffn1_down"], exchange=_gather_exchange([wire[k] for k in riders[1]] + [conv_tile]))
    landed(riders[1], gathered)
    convw_t = gathered[-1][:, :conv_taps, :conv_rows].transpose(1, 0, 2).reshape(conv_taps, N_DEV * conv_rows)
    (x2, z, o_raw, states, ycat), gathered = _mix_fwd(
        x1, w["mix_norm"], full["w_in"], w["lb_param"], w["hgrn_out_norm"], convw_t, full["w_out"],
        exchange=_gather_exchange([wire[k] for k in riders[2]]))
    landed(riders[2], gathered)
    kv = _memkv_fwd(memv, w["mem_norm"], full["w_kv_mem"])
    (x3, hq, qm, att), gathered = _xattn_fwd(
        x2, w["xattn_norm"], full["w_q_mem"], kv, full["w_o_mem"],
        exchange=_gather_exchange([wire[k] for k in riders[3]]))
    landed(riders[3], gathered)
    (dx4, a2, b2, s2, loss_part, d_final), _ = _ffn_fwd(
        x3, w["ffn2_norm"], full["ffn2_gate"], full["ffn2_up"], full["ffn2_down"], head=(w["final_norm"], target))

    parts = {}
    waiting = []

    def carried():
        names = [name for name, _ in waiting]
        exchange = _scatter_exchange([p for _, p in waiting]) if waiting else None
        del waiting[:]
        return names, exchange

    def weight_grad(name, a, b, scale=1.0):
        names, exchange = carried()
        partial, arrived = _weight_grad(a, b, scale, exchange=exchange)
        parts.update(zip(names, arrived))
        waiting.append((name, partial))

    (dx3, da2, db2, h4, d_ffn2_norm), _ = _ffn_bwd(
        x3, w["ffn2_norm"], dx4, a2, b2, full["ffn2_gate"], full["ffn2_up"], full["ffn2_down"])
    weight_grad("ffn2_down", s2, dx4, 0.5)
    weight_grad("ffn2_gate", da2, h4)
    weight_grad("ffn2_up", db2, h4)
    names, exchange = carried()
    (dx2, dqm, dkv, d_xattn_norm), arrived = _xattn_bwd(
        x2, w["xattn_norm"], dx3, qm, kv, full["w_q_mem"], full["w_o_mem"], exchange=exchange)
    parts.update(zip(names, arrived))
    d_wkv, d_mem_norm = _memkv_bwd(memv, w["mem_norm"], dkv, full["w_kv_mem"])
    waiting.append(("w_kv_mem", d_wkv))
    names, exchange = carried()
    (dx1, dz, h2, d_mix_norm, d_lbp, d_gh, d_convw_t), arrived = _mix_bwd(
        x1, w["mix_norm"], dx2, z, o_raw, states, full["w_in"], w["lb_param"], w["hgrn_out_norm"], convw_t,
        full["w_out"], exchange=exchange)
    parts.update(zip(names, arrived))
    weight_grad("w_in", dz, h2)
    weight_grad("ffn1_down", s1, dx1, 0.5)
    (dx0, da1, db1, h1, d_ffn1_norm), _ = _ffn_bwd(
        x0, w["ffn1_norm"], dx1, a1, b1, full["ffn1_gate"], full["ffn1_up"], full["ffn1_down"])
    weight_grad("ffn1_gate", da1, h1)
    weight_grad("ffn1_up", db1, h1)
    weight_grad("w_o_mem", att, dx3)
    weight_grad("w_q_mem", hq, dqm)
    weight_grad("w_out", ycat, dx2)

    small_parts = {
        "ffn1_norm": d_ffn1_norm, "mix_norm": d_mix_norm, "xattn_norm": d_xattn_norm, "mem_norm": d_mem_norm,
        "ffn2_norm": d_ffn2_norm, "final_norm": d_final, "lb_param": d_lbp, "hgrn_out_norm": d_gh,
        "conv_w": d_convw_t, "loss": loss_part,
    }
    names = [name for name, _ in waiting]
    arrived, total = _final_exchange([p for _, p in waiting], small_parts)
    parts.update(zip(names, arrived))

    g_out, d_out, m_out, v_out = {}, {}, {}, {}
    for name in LARGE:
        g_out[name], d_out[name], m_out[name], v_out[name] = _adamw_shard(parts[name], w[name], m[name], v[name])
    g_small = {}
    for name in SMALL:
        row, nrows, ncols = SMALL_LAYOUT[name]
        g_small[name] = total[row:row + nrows, 0:ncols]
    g_small["conv_w"] = lax.dynamic_slice_in_dim(g_small["conv_w"], me * conv_rows, conv_rows, axis=1)
    ds, ms, vs = _adamw_small(
        [g_small[k] for k in SMALL], [w[k] for k in SMALL], [m[k] for k in SMALL], [v[k] for k in SMALL])
    for i, name in enumerate(SMALL):
        g_out[name], d_out[name], m_out[name], v_out[name] = g_small[name], ds[i], ms[i], vs[i]

    def shaped(value, name):
        return (value.T if name in TRANSPOSED else value).reshape(given[name].shape)

    loss = total[SMALL_LAYOUT["loss"][0], 0]
    outs = [loss, dx0.reshape(x.shape)]
    for group in (g_out, d_out, m_out, v_out):
        outs += [shaped(group[name], name) for name in WEIGHTS]
    return tuple(outs)
```

```python
import jax
import jax.numpy as jnp
from jax import lax
from jax.experimental import pallas as pl
from jax.experimental.pallas import tpu as pltpu

F32 = jnp.float32
BF16 = jnp.bfloat16
MESH_IDS = pl.DeviceIdType.MESH

N_DEV = 8
EPS = 1e-6
HGRN_HEADS = 4
HGRN_DK = 128
HGRN_W = 512
CHUNK = 64
MEM_HEADS = 4
MEM_HD = 256
ADAM_LR = 0.001
ADAM_B1 = 0.9
ADAM_B2 = 0.999
ADAM_EPS = 1e-08
ADAM_WD = 0.01
ADAM_STEP = 10

TOKEN_TILE = 256
REDUCE_TILE = 1024
ADAMW_TILE_ELEMENTS = 256 * 1024
MIDDLE_EIGHTHS = 5
MXU_ROWS = 256
VMEM_LIMIT = 60 * 1024 * 1024
SMALL_ROWS = 16
NT = (((1,), (1,)), ((), ()))
TN = (((0,), (0,)), ((), ()))


def _params(sem=None):
    return pltpu.CompilerParams(dimension_semantics=sem, vmem_limit_bytes=VMEM_LIMIT)


def _dot(a, b, dims=None):
    if dims is None:
        return jnp.dot(a, b, preferred_element_type=F32)
    return lax.dot_general(a, b, dims, preferred_element_type=F32)


def _sigmoid(v):
    return 1.0 / (1.0 + jnp.exp(-v))


def _rms(x, g):
    r = lax.rsqrt(jnp.mean(x * x, axis=-1, keepdims=True) + EPS)
    xh = x * r
    return xh * g, xh, r


def _rms_bwd(dh, xh, r, g):
    dxh = dh * g
    return r * (dxh - xh * jnp.mean(dxh * xh, axis=-1, keepdims=True))


def _full(shape):
    return pl.BlockSpec(shape, lambda *_: (0,) * len(shape))


def _rows(tm, width):
    return pl.BlockSpec((tm, width), lambda i: (i, 0))


def _rows_rev(tm, width, n):
    return pl.BlockSpec((tm, width), lambda i: (n - 1 - i, 0))


def _zero_at_start(*refs):
    @pl.when(pl.program_id(0) == 0)
    def _():
        for ref in refs:
            ref[...] = jnp.zeros_like(ref)


class _Exchange:
    def __init__(self, operands, out_shapes, scratch, start, finish, middle=None):
        self.operands, self.out_shapes, self.scratch = list(operands), list(out_shapes), list(scratch)
        self.start, self.middle, self.finish = start, middle, finish


def _call(body, *, name, grid, in_specs, out_specs, out_shape, args, scratch_shapes=(), exchange=None):
    semantics = ("arbitrary",) * len(grid)
    if exchange is None:
        out = pl.pallas_call(
            body, name=name, grid=grid, in_specs=in_specs, out_specs=out_specs, out_shape=out_shape,
            scratch_shapes=list(scratch_shapes), compiler_params=_params(semantics))(*args)
        return out, []
    hbm = pl.BlockSpec(memory_space=pltpu.HBM)
    n_in, n_out, n_scr = len(in_specs), len(out_specs), len(scratch_shapes)
    e_in, e_out = len(exchange.operands), len(exchange.out_shapes)

    def carried(*refs):
        ins, rest = refs[:n_in], refs[n_in:]
        e_ins, rest = rest[:e_in], rest[e_in:]
        outs, rest = rest[:n_out], rest[n_out:]
        e_outs, rest = rest[:e_out], rest[e_out:]
        scr, e_scr = rest[:n_scr], rest[n_scr:]
        first = last = None
        for axis, size in enumerate(grid):
            at_start, at_end = pl.program_id(axis) == 0, pl.program_id(axis) == size - 1
            first = at_start if first is None else jnp.logical_and(first, at_start)
            last = at_end if last is None else jnp.logical_and(last, at_end)

        @pl.when(first)
        def _():
            exchange.start(e_ins, e_outs, e_scr)

        body(*ins, *outs, *scr)

        if exchange.middle is not None:
            assert len(grid) == 1

            @pl.when(pl.program_id(0) == (grid[0] * MIDDLE_EIGHTHS) // 8)
            def _():
                exchange.middle(e_ins, e_outs, e_scr)

        @pl.when(last)
        def _():
            exchange.finish(e_ins, e_outs, e_scr)

    out = pl.pallas_call(
        carried, name=name, grid=grid, in_specs=list(in_specs) + [hbm] * e_in,
        out_specs=list(out_specs) + [hbm] * e_out, out_shape=list(out_shape) + exchange.out_shapes,
        scratch_shapes=list(scratch_shapes) + exchange.scratch,
        compiler_params=pltpu.CompilerParams(
            dimension_semantics=semantics, vmem_limit_bytes=VMEM_LIMIT, has_side_effects=True),
    )(*args, *exchange.operands)
    return out[:n_out], out[n_out:]


def _run_exchange(exchange, name):
    hbm = pl.BlockSpec(memory_space=pltpu.HBM)
    e_in, e_out = len(exchange.operands), len(exchange.out_shapes)

    def body(*refs):
        e_ins, e_outs, e_scr = refs[:e_in], refs[e_in:e_in + e_out], refs[e_in + e_out:]
        exchange.start(e_ins, e_outs, e_scr)
        if exchange.middle is not None:
            exchange.middle(e_ins, e_outs, e_scr)
        exchange.finish(e_ins, e_outs, e_scr)

    return pl.pallas_call(
        body, name=name, in_specs=[hbm] * e_in, out_specs=[hbm] * e_out, out_shape=exchange.out_shapes,
        scratch_shapes=exchange.scratch, compiler_params=pltpu.CompilerParams(has_side_effects=True),
    )(*exchange.operands)


def _loss_head(xo, gf, tgt):
    d = xo.shape[1]
    y, xh, r = _rms(xo, gf)
    err = y - tgt
    dy = err * (1.0 / d)
    loss = 0.5 * jnp.sum(jnp.sum(err * err, axis=-1, keepdims=True) * (1.0 / d), axis=0, keepdims=True)
    return _rms_bwd(dy, xh, r, gf), loss, jnp.sum(dy * xh, axis=0, keepdims=True)


def _ffn_fwd(x, g, wg, wu, wd, exchange=None, head=None):
    t, d = x.shape
    f = wg.shape[0]
    tm = min(TOKEN_TILE, t)

    def body(x_ref, g_ref, wg_ref, wu_ref, wd_ref, *rest):
        if head is None:
            xo_ref, a_ref, b_ref, s_ref = rest
        else:
            gf_ref, tgt_ref, xo_ref, a_ref, b_ref, s_ref, loss_ref, dgf_ref = rest
            _zero_at_start(loss_ref, dgf_ref)
        xv = x_ref[...]
        h, _, _ = _rms(xv, g_ref[...])
        hb = h.astype(BF16)
        a = _dot(hb, wg_ref[...], NT)
        b = _dot(hb, wu_ref[...], NT)
        s = (a * _sigmoid(a) * b).astype(BF16)
        xo = xv + 0.5 * _dot(s, wd_ref[...])
        if head is None:
            xo_ref[...] = xo
        else:
            xo_ref[...], loss, dgf = _loss_head(xo, gf_ref[...], tgt_ref[...])
            loss_ref[...] += jnp.broadcast_to(loss, (1, 128))
            dgf_ref[...] += dgf
        a_ref[...] = a.astype(BF16)
        b_ref[...] = b.astype(BF16)
        s_ref[...] = s

    in_specs = [_rows(tm, d), _full((1, d)), _full((f, d)), _full((f, d)), _full((f, d))]
    out_specs = [_rows(tm, d), _rows(tm, f), _rows(tm, f), _rows(tm, f)]
    out_shape = [
        jax.ShapeDtypeStruct((t, d), F32),
        jax.ShapeDtypeStruct((t, f), BF16),
        jax.ShapeDtypeStruct((t, f), BF16),
        jax.ShapeDtypeStruct((t, f), BF16),
    ]
    args = (x, g, wg, wu, wd)
    if head is not None:
        in_specs += [_full((1, d)), _rows(tm, d)]
        out_specs += [_full((1, 128)), _full((1, d))]
        out_shape += [jax.ShapeDtypeStruct((1, 128), F32), jax.ShapeDtypeStruct((1, d), F32)]
        args += tuple(head)
    return _call(
        body, name="ffn_fwd", grid=(t // tm,), in_specs=in_specs, out_specs=out_specs, out_shape=out_shape,
        args=args, exchange=exchange)


def _ffn_up(x, g, wg, wu, exchange=None):
    t, d = x.shape
    f = wg.shape[0]
    tm = min(TOKEN_TILE, t)

    def body(x_ref, g_ref, wg_ref, wu_ref, a_ref, b_ref, s_ref):
        h, _, _ = _rms(x_ref[...], g_ref[...])
        hb = h.astype(BF16)
        a = _dot(hb, wg_ref[...], NT)
        b = _dot(hb, wu_ref[...], NT)
        a_ref[...] = a.astype(BF16)
        b_ref[...] = b.astype(BF16)
        s_ref[...] = (a * _sigmoid(a) * b).astype(BF16)

    return _call(
        body, name="ffn_up", grid=(t // tm,),
        in_specs=[_rows(tm, d), _full((1, d)), _full((f, d)), _full((f, d))],
        out_specs=[_rows(tm, f)] * 3, out_shape=[jax.ShapeDtypeStruct((t, f), BF16)] * 3,
        args=(x, g, wg, wu), exchange=exchange)


def _ffn_down(x, s, wd, exchange=None):
    t, d = x.shape
    f = wd.shape[0]
    tm = min(TOKEN_TILE, t)

    def body(x_ref, s_ref, wd_ref, xo_ref):
        xo_ref[...] = x_ref[...] + 0.5 * _dot(s_ref[...], wd_ref[...])

    return _call(
        body, name="ffn_down", grid=(t // tm,),
        in_specs=[_rows(tm, d), _rows(tm, f), _full((f, d))],
        out_specs=[_rows(tm, d)], out_shape=[jax.ShapeDtypeStruct((t, d), F32)],
        args=(x, s, wd), exchange=exchange)


def _ffn_bwd(x, g, dxo, a, b, wg, wu, wd, exchange=None):
    t, d = x.shape
    f = wg.shape[0]
    tm = min(TOKEN_TILE, t)

    def body(x_ref, g_ref, dxo_ref, a_ref, b_ref, wg_ref, wu_ref, wd_ref, dx_ref, da_ref, db_ref, h_ref, dg_ref):
        _zero_at_start(dg_ref)
        gv = g_ref[...]
        h, xh, r = _rms(x_ref[...], gv)
        dxo = dxo_ref[...]
        ds = _dot((0.5 * dxo).astype(BF16), wd_ref[...], NT)
        af = a_ref[...].astype(F32)
        bf = b_ref[...].astype(F32)
        sg = _sigmoid(af)
        da = (ds * bf * (sg * (1.0 + af * (1.0 - sg)))).astype(BF16)
        db = (ds * (af * sg)).astype(BF16)
        dh = _dot(da, wg_ref[...]) + _dot(db, wu_ref[...])
        dx_ref[...] = _rms_bwd(dh, xh, r, gv) + dxo
        da_ref[...] = da
        db_ref[...] = db
        h_ref[...] = h.astype(BF16)
        dg_ref[...] += jnp.sum(dh * xh, axis=0, keepdims=True)

    return _call(
        body,
        name="ffn_bwd",
        grid=(t // tm,),
        in_specs=[
            _rows(tm, d), _full((1, d)), _rows(tm, d), _rows(tm, f), _rows(tm, f),
            _full((f, d)), _full((f, d)), _full((f, d)),
        ],
        out_specs=[_rows(tm, d), _rows(tm, f), _rows(tm, f), _rows(tm, d), _full((1, d))],
        out_shape=[
            jax.ShapeDtypeStruct((t, d), F32),
            jax.ShapeDtypeStruct((t, f), BF16),
            jax.ShapeDtypeStruct((t, f), BF16),
            jax.ShapeDtypeStruct((t, d), BF16),
            jax.ShapeDtypeStruct((1, d), F32),
        ],
        args=(x, g, dxo, a, b, wg, wu, wd),
        exchange=exchange,
    )


def _weight_grad(a, b, scale=1.0, exchange=None):
    t, m = a.shape
    n = b.shape[1]
    chips = N_DEV // 2
    r = m // N_DEV
    tk = min(REDUCE_TILE, t)
    halves = 2
    nb = n // halves
    nk = t // tk

    def body(a_ref, b_ref, o_ref, acc, send_buf, recv_buf, send_sems, recv_sems):
        k, j = pl.program_id(0), pl.program_id(1)
        x, y, c, _ = _mesh_place()
        sibling, _ = _peer(x, y, c, 1)
        bv = b_ref[...]
        if scale != 1.0:
            bv = bv * scale
        bb = bv.astype(BF16)
        acc_half = acc.at[j]

        @pl.when(k == 0)
        def _():
            acc_half[...] = jnp.zeros_like(acc_half)

        for i in range(m // MXU_ROWS):
            rows = slice(i * MXU_ROWS, (i + 1) * MXU_ROWS)
            acc_half[rows, :] += _dot(a_ref[:, rows].astype(BF16), bb, TN)

        def to_sibling(half):
            return _remote(send_buf.at[half], recv_buf.at[half], send_sems.at[half], recv_sems.at[half], sibling)

        def owned_rows(q, core):
            return pl.ds(pl.multiple_of((2 * q + core) * r, 8), r)

        for half in range(halves):
            @pl.when(jnp.logical_and(k == nk - 1, j == half))
            def _():
                for q in range(chips):
                    send_buf[half, q] = acc[half, owned_rows(q, 1 - c), :].astype(BF16)
                to_sibling(half).start()

        @pl.when(jnp.logical_and(k == nk - 1, j == halves - 1))
        def _():
            for half in range(halves):
                to_sibling(half).wait_send()
                to_sibling(half).wait_recv()
                for q in range(chips):
                    o_ref[q, :, half * nb:(half + 1) * nb] = (
                        acc[half, owned_rows(q, c), :] + recv_buf[half, q].astype(F32)).astype(BF16)

    (partial,), arrived = _call(
        body,
        name="weight_grad",
        grid=(nk, halves),
        in_specs=[pl.BlockSpec((tk, m), lambda k, j: (k, 0)), pl.BlockSpec((tk, nb), lambda k, j: (k, j))],
        out_specs=[pl.BlockSpec((chips, r, n), lambda k, j: (0, 0, 0))],
        out_shape=[jax.ShapeDtypeStruct((chips, r, n), BF16)],
        scratch_shapes=[
            pltpu.VMEM((halves, m, nb), F32),
            pltpu.VMEM((halves, chips, r, nb), BF16), pltpu.VMEM((halves, chips, r, nb), BF16),
            pltpu.SemaphoreType.DMA((halves,)), pltpu.SemaphoreType.DMA((halves,)),
        ],
        args=(a, b),
        exchange=exchange,
    )
    return partial, arrived


def _chunk_cumsum(v, reverse=False):
    n, width = v.shape
    row = lax.broadcasted_iota(jnp.int32, (n, n), 0)
    col = lax.broadcasted_iota(jnp.int32, (n, n), 1)
    earlier = col >= row if reverse else col <= row
    tri = jnp.where(jnp.logical_and(row // CHUNK == col // CHUNK, earlier), 1.0, 0.0).astype(BF16)
    hi = v.astype(BF16)
    rest = v - hi.astype(F32)
    mid = rest.astype(BF16)
    low = (rest - mid.astype(F32)).astype(BF16)
    sums = _dot(tri, jnp.concatenate([hi, mid, low], axis=1))
    return sums[:, 0:width] + sums[:, width:2 * width] + sums[:, 2 * width:3 * width]


def _shift_rows(v, shift, edge):
    n = v.shape[0]
    row = lax.broadcasted_iota(jnp.int32, (n, 1), 0)
    out = pltpu.roll(v, shift % n, axis=0)
    if shift > 0:
        for j in range(shift):
            out = jnp.where(row == j, edge[8 - shift + j:8 - shift + j + 1, :], out)
    else:
        for j in range(-shift):
            out = jnp.where(row == n + shift + j, edge[j:j + 1, :], out)
    return out


def _gates(z, lbp):
    w = HGRN_W
    lb = _sigmoid(lbp[0:1, :] - lbp[1:2, :])
    zq = z[:, 0:w]
    sig = _sigmoid(z[:, w:2 * w])
    f = lb + (1.0 - lb) * sig
    sq = _sigmoid(zq)
    q = zq * sq * HGRN_DK ** -0.5
    return lb, sig, f, sq, q


def _decayed_operands(q, f, v, qh_buf, kh_buf, kbar_buf, v_buf, etot_buf):
    n, width = f.shape
    bcum = _chunk_cumsum(jnp.log(f))
    total = jnp.concatenate(
        [jnp.broadcast_to(bcum[c + CHUNK - 1:c + CHUNK, :], (CHUNK, width)) for c in range(0, n, CHUNK)], axis=0)
    eb, enb, erest = jnp.exp(bcum), jnp.exp(-bcum), jnp.exp(total - bcum)
    kk = 1.0 - f
    qh_buf[...] = (q * eb).astype(BF16)
    kh_buf[...] = (kk * enb).astype(BF16)
    kbar_buf[...] = (kk * erest).astype(BF16)
    v_buf[...] = v.astype(BF16)
    etot_buf[...] = jnp.exp(total)
    return eb, enb, erest


def _short_conv(u, edge, cw):
    return cw[0:1, :] * _shift_rows(u, 2, edge) + cw[1:2, :] * _shift_rows(u, 1, edge) + cw[2:3, :] * u


def _block_causal_mask(n):
    row = lax.broadcasted_iota(jnp.int32, (n, n), 0)
    col = lax.broadcasted_iota(jnp.int32, (n, n), 1)
    return jnp.logical_and(row // CHUNK == col // CHUNK, col <= row)


def _spread(v, chunk_of_row, nc):
    return jnp.concatenate([jnp.where(chunk_of_row == c, v, jnp.zeros_like(v)) for c in range(nc)], axis=1)


def _pick(r, chunk_of_row, nc):
    out = jnp.where(chunk_of_row == 0, r[:, 0:HGRN_DK], 0.0)
    for c in range(1, nc):
        out = out + jnp.where(chunk_of_row == c, r[:, c * HGRN_DK:(c + 1) * HGRN_DK], 0.0)
    return out


def _mix_fwd(x, g, w_in, lbp, gh, convw_t, w_out, exchange=None):
    t, d = x.shape
    zw = w_in.shape[0]
    w = HGRN_W
    tm = min(TOKEN_TILE, t)
    nc = tm // CHUNK
    n_chunks = t // CHUNK

    def body(x_ref, g_ref, win_ref, lbp_ref, gh_ref, cw_ref, wout_ref,
             xo_ref, z_ref, o_ref, st_ref, y_ref, state, ucarry, qh_buf, kh_buf, kbar_buf, v_buf, etot_buf):
        _zero_at_start(state, ucarry)
        xv = x_ref[...]
        h, _, _ = _rms(xv, g_ref[...])
        z_ref[...] = _dot(h.astype(BF16), win_ref[...], NT)
        z = z_ref[...]
        _, _, f, _, q = _gates(z, lbp_ref[...])
        _decayed_operands(q, f, z[:, 2 * w:3 * w], qh_buf, kh_buf, kbar_buf, v_buf, etot_buf)
        mask = _block_causal_mask(tm)
        chunk_of_row = lax.broadcasted_iota(jnp.int32, (tm, 1), 0) // CHUNK
        heads = range(HGRN_HEADS)
        hcols = [slice(hd * HGRN_DK, (hd + 1) * HGRN_DK) for hd in heads]
        qh = [qh_buf[:, hcols[hd]] for hd in heads]
        vb = [v_buf[:, hcols[hd]] for hd in heads]
        scores = [jnp.where(mask, _dot(qh[hd], kh_buf[:, hcols[hd]], NT), 0.0).astype(BF16) for hd in heads]
        gains = [_dot(_spread(vb[hd], chunk_of_row, nc), kbar_buf[:, hcols[hd]], TN) for hd in heads]
        entering = []
        for hd in heads:
            states, st = [], state[hd]
            for c in range(nc):
                states.append(st)
                st_ref[c, hd] = st
                st = st * etot_buf[c * CHUNK:c * CHUNK + 1, hcols[hd]] + gains[hd][c * HGRN_DK:(c + 1) * HGRN_DK, :]
            state[hd] = st
            entering.append(jnp.concatenate(states, axis=0).astype(BF16))
        from_states = [_dot(qh[hd], entering[hd], NT) for hd in heads]
        o_ref[...] = jnp.concatenate(
            [_dot(scores[hd], vb[hd]) + _pick(from_states[hd], chunk_of_row, nc) for hd in heads], axis=1)
        ghv = gh_ref[...]
        for hd in range(HGRN_HEADS):
            cols = slice(hd * HGRN_DK, (hd + 1) * HGRN_DK)
            on, _, _ = _rms(o_ref[:, cols], ghv[:, cols])
            zg = z[:, 3 * w + hd * HGRN_DK:3 * w + (hd + 1) * HGRN_DK]
            y_ref[:, cols] = (on * (zg * _sigmoid(zg))).astype(BF16)
        u = z[:, 5 * w:6 * w] * z[:, 6 * w:7 * w]
        conv = _short_conv(u, ucarry[...], cw_ref[...])
        ucarry[...] = u[tm - 8:tm, :]
        y_ref[:, w:2 * w] = (z[:, 4 * w:5 * w] * conv).astype(BF16)
        xo_ref[...] = xv + _dot(y_ref[...], wout_ref[...])

    return _call(
        body,
        name="mix_fwd",
        grid=(t // tm,),
        in_specs=[
            _rows(tm, d), _full((1, d)), _full((zw, d)), _full((2, w)), _full((1, w)), _full((3, w)),
            _full((2 * w, d)),
        ],
        out_specs=[
            _rows(tm, d), _rows(tm, zw), _rows(tm, w),
            pl.BlockSpec((nc, HGRN_HEADS, HGRN_DK, HGRN_DK), lambda i: (i, 0, 0, 0)),
            _rows(tm, 2 * w),
        ],
        out_shape=[
            jax.ShapeDtypeStruct((t, d), F32),
            jax.ShapeDtypeStruct((t, zw), F32),
            jax.ShapeDtypeStruct((t, w), F32),
            jax.ShapeDtypeStruct((n_chunks, HGRN_HEADS, HGRN_DK, HGRN_DK), F32),
            jax.ShapeDtypeStruct((t, 2 * w), BF16),
        ],
        scratch_shapes=[
            pltpu.VMEM((HGRN_HEADS, HGRN_DK, HGRN_DK), F32), pltpu.VMEM((8, w), F32),
            pltpu.VMEM((tm, w), BF16), pltpu.VMEM((tm, w), BF16), pltpu.VMEM((tm, w), BF16),
            pltpu.VMEM((tm, w), BF16), pltpu.VMEM((tm, w), F32),
        ],
        args=(x, g, w_in, lbp, gh, convw_t, w_out),
        exchange=exchange,
    )


def _mix_bwd(x, g, dxo, z, o, states, w_in, lbp, gh, convw_t, w_out, exchange=None):
    t, d = x.shape
    zw = w_in.shape[0]
    w = HGRN_W
    tm = min(TOKEN_TILE, t)
    nc = tm // CHUNK
    n = t // tm

    def body(x_ref, g_ref, dxo_ref, z_ref, zprev_ref, o_ref, st_ref, win_ref, lbp_ref, gh_ref, cw_ref, wout_ref,
             dx_ref, dz_ref, h_ref, dg_ref, dlbp_ref, dgh_ref, dcw_ref,
             dstate, dcarry, do_buf, qh_buf, kh_buf, kbar_buf, v_buf, etot_buf):
        _zero_at_start(dstate, dcarry, dg_ref, dlbp_ref, dgh_ref, dcw_ref)
        gv = g_ref[...]
        h, xh, r = _rms(x_ref[...], gv)
        h_ref[...] = h.astype(BF16)
        dxo = dxo_ref[...]
        dy = _dot(dxo.astype(BF16), wout_ref[...], NT)
        z = z_ref[...]
        lb, sig, f, sq, q = _gates(z, lbp_ref[...])
        eb, enb, erest = _decayed_operands(q, f, z[:, 2 * w:3 * w], qh_buf, kh_buf, kbar_buf, v_buf, etot_buf)

        ghv = gh_ref[...]
        dgh_parts = []
        for hd in range(HGRN_HEADS):
            cols = slice(hd * HGRN_DK, (hd + 1) * HGRN_DK)
            gcols = slice(3 * w + hd * HGRN_DK, 3 * w + (hd + 1) * HGRN_DK)
            on, oh, rr = _rms(o_ref[:, cols], ghv[:, cols])
            zg = z[:, gcols]
            sgz = _sigmoid(zg)
            dyh = dy[:, cols]
            don = dyh * (zg * sgz)
            dz_ref[:, gcols] = (dyh * on * (sgz * (1.0 + zg * (1.0 - sgz)))).astype(BF16)
            dgh_parts.append(jnp.sum(don * oh, axis=0, keepdims=True))
            do_buf[:, cols] = _rms_bwd(don, oh, rr, ghv[:, cols]).astype(BF16)
        dgh_ref[...] += jnp.concatenate(dgh_parts, axis=1)

        zb = z[:, 4 * w:5 * w]
        zc = z[:, 5 * w:6 * w]
        zu = z[:, 6 * w:7 * w]
        u = zc * zu
        cw = cw_ref[...]
        zp = zprev_ref[...]
        uprev = jnp.where(pl.program_id(0) == n - 1, 0.0, zp[:, 5 * w:6 * w] * zp[:, 6 * w:7 * w])
        dyc = dy[:, w:2 * w]
        dz_ref[:, 4 * w:5 * w] = (dyc * _short_conv(u, uprev, cw)).astype(BF16)
        dconv = dyc * zb
        edge = dcarry[...]
        dconv1 = _shift_rows(dconv, -1, edge)
        dconv2 = _shift_rows(dconv, -2, edge)
        dcarry[...] = dconv[0:8, :]
        du = cw[2:3, :] * dconv + cw[1:2, :] * dconv1 + cw[0:1, :] * dconv2
        dz_ref[:, 5 * w:6 * w] = (du * zu).astype(BF16)
        dz_ref[:, 6 * w:7 * w] = (du * zc).astype(BF16)
        dcw_ref[...] += jnp.concatenate([
            jnp.sum(u * dconv2, axis=0, keepdims=True),
            jnp.sum(u * dconv1, axis=0, keepdims=True),
            jnp.sum(u * dconv, axis=0, keepdims=True)], axis=0)

        mask = _block_causal_mask(tm)
        chunk_of_row = lax.broadcasted_iota(jnp.int32, (tm, 1), 0) // CHUNK
        heads = range(HGRN_HEADS)
        hcols = [slice(hd * HGRN_DK, (hd + 1) * HGRN_DK) for hd in heads]
        qhb = [qh_buf[:, hcols[hd]] for hd in heads]
        khb = [kh_buf[:, hcols[hd]] for hd in heads]
        vb = [v_buf[:, hcols[hd]] for hd in heads]
        dob = [do_buf[:, hcols[hd]] for hd in heads]
        scores = [jnp.where(mask, _dot(qhb[hd], khb[hd], NT), 0.0).astype(BF16) for hd in heads]
        dscores = [jnp.where(mask, _dot(dob[hd], vb[hd], NT), 0.0).astype(BF16) for hd in heads]
        gains = [_dot(_spread(dob[hd], chunk_of_row, nc), qhb[hd], TN) for hd in heads]
        dst_rows, dst_lanes, st_lanes, carries = [], [], [], []
        for hd in heads:
            entering = [st_ref[c, hd] for c in range(nc)]
            leaving, carried_back = [None] * nc, [None] * nc
            dst = dstate[hd]
            for c in reversed(range(nc)):
                elast = etot_buf[c * CHUNK:c * CHUNK + 1, hcols[hd]]
                leaving[c] = dst
                carried_back[c] = jnp.sum(dst * entering[c], axis=0, keepdims=True) * elast
                dst = dst * elast + gains[hd][c * HGRN_DK:(c + 1) * HGRN_DK, :]
            dstate[hd] = dst
            dst_rows.append(jnp.concatenate(leaving, axis=0).astype(BF16))
            dst_lanes.append(jnp.concatenate(leaving, axis=1).astype(BF16))
            st_lanes.append(jnp.concatenate(entering, axis=1).astype(BF16))
            carries.append(carried_back)
        dv = [_dot(scores[hd], dob[hd], TN) + _pick(_dot(kbar_buf[:, hcols[hd]], dst_rows[hd], NT), chunk_of_row, nc)
              for hd in heads]
        dz_ref[:, 2 * w:3 * w] = jnp.concatenate(dv, axis=1).astype(BF16)
        dqh = jnp.concatenate(
            [_dot(dscores[hd], khb[hd]) + _pick(_dot(dob[hd], st_lanes[hd]), chunk_of_row, nc) for hd in heads], axis=1)
        dkh = jnp.concatenate([_dot(dscores[hd], qhb[hd], TN) for hd in heads], axis=1)
        dkbar = jnp.concatenate([_pick(_dot(vb[hd], dst_lanes[hd]), chunk_of_row, nc) for hd in heads], axis=1)

        kbar_dkbar = kbar_buf[...].astype(F32) * dkbar
        db = qh_buf[...].astype(F32) * dqh - kh_buf[...].astype(F32) * dkh - kbar_dkbar
        through_last = jnp.concatenate([
            jnp.broadcast_to(
                jnp.sum(kbar_dkbar[c * CHUNK:(c + 1) * CHUNK], axis=0, keepdims=True)
                + jnp.concatenate([carries[hd][c] for hd in heads], axis=1),
                (CHUNK, w))
            for c in range(nc)], axis=0)
        dlogf = _chunk_cumsum(db, reverse=True) + through_last
        df = dlogf / f - (dkh * enb + dkbar * erest)
        zq = z[:, 0:w]
        dz_ref[:, 0:w] = (dqh * eb * HGRN_DK ** -0.5 * (sq * (1.0 + zq * (1.0 - sq)))).astype(BF16)
        dz_ref[:, w:2 * w] = (df * (1.0 - lb) * sig * (1.0 - sig)).astype(BF16)
        dlb = jnp.sum(df * (1.0 - sig), axis=0, keepdims=True) * lb * (1.0 - lb)
        dlbp_ref[...] += jnp.concatenate([dlb, -dlb], axis=0)

        dh = _dot(dz_ref[...], win_ref[...])
        dx_ref[...] = _rms_bwd(dh, xh, r, gv) + dxo
        dg_ref[...] += jnp.sum(dh * xh, axis=0, keepdims=True)

    return _call(
        body,
        name="mix_bwd",
        grid=(n,),
        in_specs=[
            _rows_rev(tm, d, n), _full((1, d)), _rows_rev(tm, d, n), _rows_rev(tm, zw, n),
            pl.BlockSpec((8, zw), lambda i: (jnp.maximum((n - 1 - i) * (tm // 8) - 1, 0), 0)),
            _rows_rev(tm, w, n),
            pl.BlockSpec((nc, HGRN_HEADS, HGRN_DK, HGRN_DK), lambda i: (n - 1 - i, 0, 0, 0)),
            _full((zw, d)), _full((2, w)), _full((1, w)), _full((3, w)), _full((2 * w, d)),
        ],
        out_specs=[
            _rows_rev(tm, d, n), _rows_rev(tm, zw, n), _rows_rev(tm, d, n),
            _full((1, d)), _full((2, w)), _full((1, w)), _full((3, w)),
        ],
        out_shape=[
            jax.ShapeDtypeStruct((t, d), F32),
            jax.ShapeDtypeStruct((t, zw), BF16),
            jax.ShapeDtypeStruct((t, d), BF16),
            jax.ShapeDtypeStruct((1, d), F32),
            jax.ShapeDtypeStruct((2, w), F32),
            jax.ShapeDtypeStruct((1, w), F32),
            jax.ShapeDtypeStruct((3, w), F32),
        ],
        scratch_shapes=[
            pltpu.VMEM((HGRN_HEADS, HGRN_DK, HGRN_DK), F32), pltpu.VMEM((8, w), F32),
            pltpu.VMEM((tm, w), BF16),
            pltpu.VMEM((tm, w), BF16), pltpu.VMEM((tm, w), BF16), pltpu.VMEM((tm, w), BF16),
            pltpu.VMEM((tm, w), BF16), pltpu.VMEM((tm, w), F32),
        ],
        args=(x, g, dxo, z, z, o, states, w_in, lbp, gh, convw_t, w_out),
        exchange=exchange,
    )


def _memkv_fwd(mem, g, wkv):
    m, d = mem.shape
    nb, _, cb = wkv.shape

    def body(mem_ref, g_ref, wkv_ref, kv_ref):
        mn, _, _ = _rms(mem_ref[...], g_ref[...])
        mnb = mn.astype(BF16)
        for j in range(nb):
            kv_ref[:, j * cb:(j + 1) * cb] = _dot(mnb, wkv_ref[j]).astype(BF16)

    return pl.pallas_call(
        body,
        name="memkv_fwd",
        out_shape=jax.ShapeDtypeStruct((m, nb * cb), BF16),
        compiler_params=_params(),
    )(mem, g, wkv)


def _memkv_bwd(mem, g, dkv, wkv):
    m, d = mem.shape
    nb, _, cb = wkv.shape
    chips = nb // 2

    def body(mem_ref, g_ref, dkv_ref, wkv_ref, dw_ref, dg_ref, dw_all, send_buf, recv_buf, send_sem, recv_sem):
        x, y, c, _ = _mesh_place()
        sibling, _ = _peer(x, y, c, 1)
        mn, xh, _ = _rms(mem_ref[...], g_ref[...])
        mnb = mn.astype(BF16)
        dmn = jnp.zeros((m, d), F32)
        for j in range(nb):
            dkvb = dkv_ref[:, j * cb:(j + 1) * cb].astype(BF16)
            dw_all[j] = _dot(mnb, dkvb, TN)
            dmn = dmn + _dot(dkvb, wkv_ref[j], NT)
        dg_ref[...] = jnp.sum(dmn * xh, axis=0, keepdims=True)
        for q in range(chips):
            send_buf[q] = dw_all[2 * q + 1 - c].astype(BF16)
        to_sibling = _remote(send_buf, recv_buf, send_sem, recv_sem, sibling)
        to_sibling.start()
        to_sibling.wait_send()
        to_sibling.wait_recv()
        for q in range(chips):
            dw_ref[q] = (dw_all[2 * q + c] + recv_buf[q].astype(F32)).astype(BF16)

    return pl.pallas_call(
        body,
        name="memkv_bwd",
        out_shape=[jax.ShapeDtypeStruct((chips, d, cb), BF16), jax.ShapeDtypeStruct((1, d), F32)],
        scratch_shapes=[
            pltpu.VMEM((nb, d, cb), F32), pltpu.VMEM((chips, d, cb), BF16), pltpu.VMEM((chips, d, cb), BF16),
            pltpu.SemaphoreType.DMA, pltpu.SemaphoreType.DMA,
        ],
        compiler_params=_params(),
    )(mem, g, dkv, wkv)


def _softmax_rows(qm_h, k_h):
    sc = _dot(qm_h, k_h, NT) * MEM_HD ** -0.5
    e = jnp.exp(sc - jnp.max(sc, axis=-1, keepdims=True))
    return e / jnp.sum(e, axis=-1, keepdims=True)


def _xattn_fwd(x, g, wq, kv, wo, exchange=None):
    t, d = x.shape
    m = kv.shape[0]
    tm = min(TOKEN_TILE, t)

    def body(x_ref, g_ref, wq_ref, kv_ref, wo_ref, xo_ref, hq_ref, qm_ref, att_ref):
        xv = x_ref[...]
        h, _, _ = _rms(xv, g_ref[...])
        hb = h.astype(BF16)
        hq_ref[...] = hb
        qm = _dot(hb, wq_ref[...]).astype(BF16)
        qm_ref[...] = qm
        heads = range(MEM_HEADS)
        kcols = [slice(hd * MEM_HD, (hd + 1) * MEM_HD) for hd in heads]
        p = [_softmax_rows(qm[:, kcols[hd]], kv_ref[:, kcols[hd]]) for hd in heads]
        att = jnp.concatenate(
            [_dot(p[hd].astype(BF16), kv_ref[:, d + hd * MEM_HD:d + (hd + 1) * MEM_HD]) for hd in heads],
            axis=1).astype(BF16)
        att_ref[...] = att
        xo_ref[...] = xv + _dot(att, wo_ref[...])

    return _call(
        body,
        name="xattn_fwd",
        grid=(t // tm,),
        in_specs=[_rows(tm, d), _full((1, d)), _full((d, d)), _full((m, 2 * d)), _full((d, d))],
        out_specs=[_rows(tm, d), _rows(tm, d), _rows(tm, d), _rows(tm, d)],
        out_shape=[
            jax.ShapeDtypeStruct((t, d), F32),
            jax.ShapeDtypeStruct((t, d), BF16),
            jax.ShapeDtypeStruct((t, d), BF16),
            jax.ShapeDtypeStruct((t, d), BF16),
        ],
        args=(x, g, wq, kv, wo),
        exchange=exchange,
    )


def _xattn_bwd(x, g, dxo, qm, kv, wq, wo, exchange=None):
    t, d = x.shape
    m = kv.shape[0]
    tm = min(TOKEN_TILE, t)

    def body(x_ref, g_ref, dxo_ref, qm_ref, kv_ref, wq_ref, wo_ref, dx_ref, dqm_ref, dkv_ref, dg_ref):
        _zero_at_start(dkv_ref, dg_ref)
        gv = g_ref[...]
        _, xh, r = _rms(x_ref[...], gv)
        dxo = dxo_ref[...]
        datt = _dot(dxo.astype(BF16), wo_ref[...], NT).astype(BF16)
        heads = range(MEM_HEADS)
        kcols = [slice(hd * MEM_HD, (hd + 1) * MEM_HD) for hd in heads]
        vcols = [slice(d + hd * MEM_HD, d + (hd + 1) * MEM_HD) for hd in heads]
        qm_h = [qm_ref[:, kcols[hd]] for hd in heads]
        p = [_softmax_rows(qm_h[hd], kv_ref[:, kcols[hd]]) for hd in heads]
        dp = [_dot(datt[:, kcols[hd]], kv_ref[:, vcols[hd]], NT) for hd in heads]
        dsc = [(p[hd] * (dp[hd] - jnp.sum(p[hd] * dp[hd], axis=-1, keepdims=True)) * MEM_HD ** -0.5).astype(BF16)
               for hd in heads]
        dqm = jnp.concatenate([_dot(dsc[hd], kv_ref[:, kcols[hd]]) for hd in heads], axis=1).astype(BF16)
        dqm_ref[...] = dqm
        dkv_ref[...] += jnp.concatenate(
            [_dot(dsc[hd], qm_h[hd], TN) for hd in heads]
            + [_dot(p[hd].astype(BF16), datt[:, kcols[hd]], TN) for hd in heads], axis=1)
        dh = _dot(dqm, wq_ref[...], NT)
        dx_ref[...] = _rms_bwd(dh, xh, r, gv) + dxo
        dg_ref[...] += jnp.sum(dh * xh, axis=0, keepdims=True)

    return _call(
        body,
        name="xattn_bwd",
        grid=(t // tm,),
        in_specs=[
            _rows(tm, d), _full((1, d)), _rows(tm, d), _rows(tm, d), _full((m, 2 * d)), _full((d, d)), _full((d, d)),
        ],
        out_specs=[_rows(tm, d), _rows(tm, d), _full((m, 2 * d)), _full((1, d))],
        out_shape=[
            jax.ShapeDtypeStruct((t, d), F32),
            jax.ShapeDtypeStruct((t, d), BF16),
            jax.ShapeDtypeStruct((m, 2 * d), F32),
            jax.ShapeDtypeStruct((1, d), F32),
        ],
        args=(x, g, dxo, qm, kv, wq, wo),
        exchange=exchange,
    )


def _mesh_place():
    x, y, c = lax.axis_index("x"), lax.axis_index("y"), lax.axis_index("c")
    return x, y, c, 4 * x + 2 * y + c


def _peer(x, y, c, k):
    px = 1 - x if k & 4 else x
    py = 1 - y if k & 2 else y
    pc = 1 - c if k & 1 else c
    return (px, py, pc), 4 * px + 2 * py + pc


ICI_HOPS = (2, 4, 6)
N_HOPS = len(ICI_HOPS)


def _remote(src, dst, send_sem, recv_sem, peer):
    return pltpu.make_async_remote_copy(
        src_ref=src, dst_ref=dst, send_sem=send_sem, recv_sem=recv_sem, device_id=peer, device_id_type=MESH_IDS)


def _gather_exchange(shards):
    n = len(shards)

    def place():
        x, y, c, me = _mesh_place()
        sibling, _ = _peer(x, y, c, 1)
        to_x, from_x = _peer(x, y, c, 4)
        to_y, from_y = _peer(x, y, c, 2)
        _, from_diagonal = _peer(x, y, c, 6)
        onward = (c * to_y[0] + (1 - c) * to_x[0], c * to_y[1] + (1 - c) * to_x[1], c)
        passed_on = c * from_x + (1 - c) * from_y
        return me, sibling, (to_x, to_y, onward), (from_x, from_y, from_diagonal), passed_on

    def start(src, dst, sems):
        ici_send, ici_recv, pair_send, pair_recv, local = sems
        me, sibling, targets, _, _ = place()
        for a in range(n):
            pltpu.make_async_copy(src[a], dst[a].at[me], local.at[a]).start()
            for j in range(2):
                _remote(src[a], dst[a].at[me], ici_send.at[a, j], ici_recv.at[a, j], targets[j]).start()
            _remote(src[a], dst[a].at[me], pair_send.at[a, 0], pair_recv.at[a, 0], sibling).start()

    def to_sibling(dst, sems, a, j, origin, sibling):
        _, _, pair_send, pair_recv, _ = sems
        slot = dst[a].at[origin]
        return _remote(slot, slot, pair_send.at[a, 1 + j], pair_recv.at[a, 1 + j], sibling)

    def middle(src, dst, sems):
        ici_send, ici_recv, _, _, _ = sems
        _, sibling, targets, origins, passed_on = place()
        for a in range(n):
            for j in range(2):
                _remote(src[a], dst[a].at[origins[j]], ici_send.at[a, j], ici_recv.at[a, j], targets[j]).wait_recv()
            slot = dst[a].at[passed_on]
            _remote(slot, slot, ici_send.at[a, 2], ici_recv.at[a, 2], targets[2]).start()
            for j in range(2):
                to_sibling(dst, sems, a, j, origins[j], sibling).start()

    def finish(src, dst, sems):
        ici_send, ici_recv, pair_send, pair_recv, local = sems
        me, sibling, targets, origins, _ = place()
        for a in range(n):
            _remote(src[a], dst[a].at[origins[2]], ici_send.at[a, 2], ici_recv.at[a, 2], targets[2]).wait_recv()
            to_sibling(dst, sems, a, 2, origins[2], sibling).start()
        for a in range(n):
            pltpu.make_async_copy(src[a], dst[a].at[me], local.at[a]).wait()
            for j in range(N_HOPS):
                _remote(src[a], dst[a].at[me], ici_send.at[a, j], ici_recv.at[a, j], targets[j]).wait_send()
            for j, origin in enumerate((me,) + origins):
                from_sibling = origin + 1 - 2 * (origin % 2)
                passed = _remote(src[a], dst[a].at[from_sibling], pair_send.at[a, j], pair_recv.at[a, j], sibling)
                passed.wait_send()
                passed.wait_recv()

    return _Exchange(
        shards,
        [jax.ShapeDtypeStruct((N_DEV,) + s.shape, s.dtype) for s in shards],
        [
            pltpu.SemaphoreType.DMA((n, N_HOPS)), pltpu.SemaphoreType.DMA((n, N_HOPS)),
            pltpu.SemaphoreType.DMA((n, N_HOPS + 1)), pltpu.SemaphoreType.DMA((n, N_HOPS + 1)),
            pltpu.SemaphoreType.DMA((n,)),
        ],
        start, finish, middle)


def _scatter_copies(src, dst, sems, n, arrivals=False):
    send, recv, local = sems
    x, y, c, _ = _mesh_place()
    chip = 2 * x + y
    if arrivals is None:
        return [pltpu.make_async_copy(src[a].at[chip], dst[a].at[chip], local.at[a]) for a in range(n)]
    copies = []
    for a in range(n):
        for j, k in enumerate(ICI_HOPS):
            peer, _ = _peer(x, y, c, k)
            peer_chip = 2 * peer[0] + peer[1]
            slot = dst[a].at[peer_chip if arrivals else chip]
            copies.append(_remote(src[a].at[peer_chip], slot, send.at[a, j], recv.at[a, j], peer))
    return copies


def _scatter_start(src, dst, sems, n):
    for cp in _scatter_copies(src, dst, sems, n, arrivals=None) + _scatter_copies(src, dst, sems, n):
        cp.start()


def _scatter_finish(src, dst, sems, n):
    for cp in _scatter_copies(src, dst, sems, n, arrivals=None):
        cp.wait()
    for cp in _scatter_copies(src, dst, sems, n):
        cp.wait_send()
    for cp in _scatter_copies(src, dst, sems, n, arrivals=True):
        cp.wait_recv()


def _scatter_scratch(n):
    return [pltpu.SemaphoreType.DMA((n, N_HOPS)), pltpu.SemaphoreType.DMA((n, N_HOPS)), pltpu.SemaphoreType.DMA((n,))]


def _scatter_exchange(partials):
    n = len(partials)
    return _Exchange(
        partials, [jax.ShapeDtypeStruct(p.shape, p.dtype) for p in partials], _scatter_scratch(n),
        lambda src, dst, sems: _scatter_start(src, dst, sems, n),
        lambda src, dst, sems: _scatter_finish(src, dst, sems, n))


SMALL_LAYOUT = {
    "ffn1_norm": (0, 1, 1024), "mix_norm": (1, 1, 1024), "xattn_norm": (2, 1, 1024), "mem_norm": (3, 1, 1024),
    "ffn2_norm": (4, 1, 1024), "final_norm": (5, 1, 1024), "lb_param": (6, 2, 512), "hgrn_out_norm": (8, 1, 512),
    "conv_w": (9, 3, 512), "loss": (12, 1, 128),
}


def _final_exchange(partials, small):
    n = len(partials)
    names = list(small)
    width = 1024

    def body(*refs):
        src = refs[:n]
        pieces = refs[n:n + len(names)]
        dst = refs[n + len(names):2 * n + len(names)]
        total_ref = refs[2 * n + len(names)]
        pack, gathered, small_send, small_recv = refs[2 * n + len(names) + 1:2 * n + len(names) + 5]
        sems = refs[2 * n + len(names) + 5:]
        x, y, c, me = _mesh_place()
        pack[...] = jnp.zeros_like(pack)
        for name, piece in zip(names, pieces):
            row, nrows, ncols = SMALL_LAYOUT[name]
            pack[row:row + nrows, 0:ncols] = piece[...]
        for k in range(1, N_DEV):
            peer, _ = _peer(x, y, c, k)
            _remote(pack, gathered.at[me], small_send.at[k - 1], small_recv.at[k - 1], peer).start()
        _scatter_start(src, dst, sems, n)
        gathered[me] = pack[...]
        for k in range(1, N_DEV):
            peer, peer_index = _peer(x, y, c, k)
            landed = _remote(pack, gathered.at[peer_index], small_send.at[k - 1], small_recv.at[k - 1], peer)
            landed.wait_send()
            landed.wait_recv()
        total = gathered[0]
        for j in range(1, N_DEV):
            total = total + gathered[j]
        total_ref[...] = total
        _scatter_finish(src, dst, sems, n)

    hbm = pl.BlockSpec(memory_space=pltpu.HBM)
    vmem = pl.BlockSpec(memory_space=pltpu.VMEM)
    out = pl.pallas_call(
        body,
        name="final_exchange",
        in_specs=[hbm] * n + [vmem] * len(names),
        out_specs=[hbm] * n + [vmem],
        out_shape=[jax.ShapeDtypeStruct(p.shape, p.dtype) for p in partials]
        + [jax.ShapeDtypeStruct((SMALL_ROWS, width), F32)],
        scratch_shapes=[
            pltpu.VMEM((SMALL_ROWS, width), F32), pltpu.VMEM((N_DEV, SMALL_ROWS, width), F32),
            pltpu.SemaphoreType.DMA((N_DEV - 1,)), pltpu.SemaphoreType.DMA((N_DEV - 1,)),
        ] + _scatter_scratch(n),
        compiler_params=pltpu.CompilerParams(has_side_effects=True),
    )(*partials, *[small[k] for k in names])
    return out[:n], out[n]


def _adamw_math(w, g, m, v):
    m = ADAM_B1 * m + (1.0 - ADAM_B1) * g
    v = ADAM_B2 * v + (1.0 - ADAM_B2) * (g * g)
    m_hat = m / (1.0 - ADAM_B1 ** ADAM_STEP)
    v_hat = v / (1.0 - ADAM_B2 ** ADAM_STEP)
    delta = -ADAM_LR * (m_hat / (jnp.sqrt(v_hat) + ADAM_EPS) + ADAM_WD * w)
    return delta, m, v


def _adamw_shard(parts, w, m, v):
    r, c = w.shape
    n_parts = parts.shape[0]
    tr = max(rows for rows in range(16, r + 1, 16) if r % rows == 0 and rows * c <= ADAMW_TILE_ELEMENTS)

    def body(p_ref, w_ref, m_ref, v_ref, g_ref, d_ref, mo_ref, vo_ref):
        g = p_ref[0].astype(F32)
        for j in range(1, n_parts):
            g = g + p_ref[j].astype(F32)
        delta, mn, vn = _adamw_math(w_ref[...], g, m_ref[...], v_ref[...])
        g_ref[...] = g
        d_ref[...] = delta
        mo_ref[...] = mn
        vo_ref[...] = vn

    tile = pl.BlockSpec((tr, c), lambda i: (i, 0))
    return pl.pallas_call(
        body,
        name="adamw_shard",
        grid=(r // tr,),
        in_specs=[pl.BlockSpec((n_parts, tr, c), lambda i: (0, i, 0)), tile, tile, tile],
        out_specs=[tile] * 4,
        out_shape=[jax.ShapeDtypeStruct((r, c), F32)] * 4,
        compiler_params=_params(("parallel",)),
    )(parts, w, m, v)


def _adamw_small(gs, ws, ms, vs):
    n = len(gs)

    def body(*refs):
        g_refs, w_refs, m_refs, v_refs = refs[:n], refs[n:2 * n], refs[2 * n:3 * n], refs[3 * n:4 * n]
        d_out, m_out, v_out = refs[4 * n:5 * n], refs[5 * n:6 * n], refs[6 * n:7 * n]
        for i in range(n):
            delta, mn, vn = _adamw_math(w_refs[i][...], g_refs[i][...], m_refs[i][...], v_refs[i][...])
            d_out[i][...] = delta
            m_out[i][...] = mn
            v_out[i][...] = vn

    shapes = [jax.ShapeDtypeStruct(w.shape, F32) for w in ws]
    out = pl.pallas_call(
        body,
        name="adamw_small",
        out_shape=shapes * 3,
        compiler_params=_params(),
    )(*gs, *ws, *ms, *vs)
    return out[:n], out[n:2 * n], out[2 * n:]


TRANSPOSED = ("ffn1_gate", "ffn1_up", "w_in", "ffn2_gate", "ffn2_up", "conv_w")
GROUP_FFN1 = ("ffn1_gate", "ffn1_up", "ffn1_down")
GROUP_MIX = ("w_in", "w_out")
GROUP_XATTN = ("w_q_mem", "w_kv_mem", "w_o_mem")
GROUP_FFN2 = ("ffn2_gate", "ffn2_up", "ffn2_down")
LARGE = GROUP_FFN1 + GROUP_MIX + GROUP_XATTN + GROUP_FFN2
SMALL = ("ffn1_norm", "mix_norm", "lb_param", "hgrn_out_norm", "conv_w", "xattn_norm", "mem_norm", "ffn2_norm",
         "final_norm")
WEIGHTS = ("ffn1_norm", "ffn1_gate", "ffn1_up", "ffn1_down", "mix_norm", "w_in", "lb_param", "hgrn_out_norm",
           "conv_w", "w_out", "xattn_norm", "mem_norm", "w_q_mem", "w_kv_mem", "w_o_mem", "ffn2_norm", "ffn2_gate",
           "ffn2_up", "ffn2_down", "final_norm")


def kernel(x, mem, ffn1_norm, ffn1_gate, ffn1_up, ffn1_down, mix_norm, w_in, lb_param, hgrn_out_norm, conv_w, w_out, xattn_norm, mem_norm, w_q_mem, w_kv_mem, w_o_mem, ffn2_norm, ffn2_gate, ffn2_up, ffn2_down, final_norm, loss_target, m_ffn1_norm, m_ffn1_gate, m_ffn1_up, m_ffn1_down, m_mix_norm, m_w_in, m_lb_param, m_hgrn_out_norm, m_conv_w, m_w_out, m_xattn_norm, m_mem_norm, m_w_q_mem, m_w_kv_mem, m_w_o_mem, m_ffn2_norm, m_ffn2_gate, m_ffn2_up, m_ffn2_down, m_final_norm, v_ffn1_norm, v_ffn1_gate, v_ffn1_up, v_ffn1_down, v_mix_norm, v_w_in, v_lb_param, v_hgrn_out_norm, v_conv_w, v_w_out, v_xattn_norm, v_mem_norm, v_w_q_mem, v_w_kv_mem, v_w_o_mem, v_ffn2_norm, v_ffn2_gate, v_ffn2_up, v_ffn2_down, v_final_norm):
    given = dict(locals())
    me = 4 * lax.axis_index("x") + 2 * lax.axis_index("y") + lax.axis_index("c")
    x0, memv, target = x[0], mem[0], loss_target[0]

    def shard(prefix, name):
        v = given[prefix + name]
        if v.ndim == 1:
            return v.reshape(1, -1)
        if v.ndim == 2:
            return v
        return v[0].T if name in TRANSPOSED else v[0]

    w = {name: shard("", name) for name in WEIGHTS}
    m = {name: shard("m_", name) for name in WEIGHTS}
    v = {name: shard("v_", name) for name in WEIGHTS}

    conv_taps, conv_rows = w["conv_w"].shape
    conv_tile = jnp.pad(w["conv_w"], ((0, 8 - conv_taps), (0, 128 - conv_rows)))
    wire = {name: w[name].astype(BF16) for name in LARGE}
    full = {}

    def landed(names, gathered):
        for name, blocks in zip(names, gathered):
            _, r, c = blocks.shape
            full[name] = blocks if name == "w_kv_mem" else blocks.reshape(N_DEV * r, c)

    first = ("ffn1_gate", "ffn1_up")
    landed(first, _run_exchange(_gather_exchange([wire[k] for k in first]), "gather_first"))

    riders = (("ffn1_down", "w_in"), ("w_out", "w_kv_mem"), ("w_q_mem", "w_o_mem", "ffn2_gate", "ffn2_up"),
              ("ffn2_down",))
    (a1, b1, s1), gathered = _ffn_up(
        x0, w["ffn1_norm"], full["ffn1_gate"], full["ffn1_up"],
        exchange=_gather_exchange([wire[k] for k in riders[0]]))
    landed(riders[0], gathered)
    (x1,), gathered = _ffn_down(
        x0, s1, full["ffn1_down"], exchange=_gather_exchange([wire[k] for k in riders[1]] + [conv_tile]))
    landed(riders[1], gathered)
    convw_t = gathered[-1][:, :conv_taps, :conv_rows].transpose(1, 0, 2).reshape(conv_taps, N_DEV * conv_rows)
    (x2, z, o_raw, states, ycat), gathered = _mix_fwd(
        x1, w["mix_norm"], full["w_in"], w["lb_param"], w["hgrn_out_norm"], convw_t, full["w_out"],
        exchange=_gather_exchange([wire[k] for k in riders[2]]))
    landed(riders[2], gathered)
    kv = _memkv_fwd(memv, w["mem_norm"], full["w_kv_mem"])
    (x3, hq, qm, att), gathered = _xattn_fwd(
        x2, w["xattn_norm"], full["w_q_mem"], kv, full["w_o_mem"],
        exchange=_gather_exchange([wire[k] for k in riders[3]]))
    landed(riders[3], gathered)
    (dx4, a2, b2, s2, loss_part, d_final), _ = _ffn_fwd(
        x3, w["ffn2_norm"], full["ffn2_gate"], full["ffn2_up"], full["ffn2_down"], head=(w["final_norm"], target))

    parts = {}
    waiting = []

    def carried():
        names = [name for name, _ in waiting]
        exchange = _scatter_exchange([p for _, p in waiting]) if waiting else None
        del waiting[:]
        return names, exchange

    def weight_grad(name, a, b, scale=1.0):
        names, exchange = carried()
        partial, arrived = _weight_grad(a, b, scale, exchange=exchange)
        parts.update(zip(names, arrived))
        waiting.append((name, partial))

    (dx3, da2, db2, h4, d_ffn2_norm), _ = _ffn_bwd(
        x3, w["ffn2_norm"], dx4, a2, b2, full["ffn2_gate"], full["ffn2_up"], full["ffn2_down"])
    weight_grad("ffn2_down", s2, dx4, 0.5)
    weight_grad("ffn2_gate", da2, h4)
    weight_grad("ffn2_up", db2, h4)
    names, exchange = carried()
    (dx2, dqm, dkv, d_xattn_norm), arrived = _xattn_bwd(
        x2, w["xattn_norm"], dx3, qm, kv, full["w_q_mem"], full["w_o_mem"], exchange=exchange)
    parts.update(zip(names, arrived))
    d_wkv, d_mem_norm = _memkv_bwd(memv, w["mem_norm"], dkv, full["w_kv_mem"])
    waiting.append(("w_kv_mem", d_wkv))
    names, exchange = carried()
    (dx1, dz, h2, d_mix_norm, d_lbp, d_gh, d_convw_t), arrived = _mix_bwd(
        x1, w["mix_norm"], dx2, z, o_raw, states, full["w_in"], w["lb_param"], w["hgrn_out_norm"], convw_t,
        full["w_out"], exchange=exchange)
    parts.update(zip(names, arrived))
    weight_grad("w_in", dz, h2)
    weight_grad("ffn1_down", s1, dx1, 0.5)
    (dx0, da1, db1, h1, d_ffn1_norm), _ = _ffn_bwd(
        x0, w["ffn1_norm"], dx1, a1, b1, full["ffn1_gate"], full["ffn1_up"], full["ffn1_down"])
    weight_grad("ffn1_gate", da1, h1)
    weight_grad("ffn1_up", db1, h1)
    weight_grad("w_o_mem", att, dx3)
    weight_grad("w_q_mem", hq, dqm)
    weight_grad("w_out", ycat, dx2)

    small_parts = {
        "ffn1_norm": d_ffn1_norm, "mix_norm": d_mix_norm, "xattn_norm": d_xattn_norm, "mem_norm": d_mem_norm,
        "ffn2_norm": d_ffn2_norm, "final_norm": d_final, "lb_param": d_lbp, "hgrn_out_norm": d_gh,
        "conv_w": d_convw_t, "loss": loss_part,
    }
    names = [name for name, _ in waiting]
    arrived, total = _final_exchange([p for _, p in waiting], small_parts)
    parts.update(zip(names, arrived))

    g_out, d_out, m_out, v_out = {}, {}, {}, {}
    for name in LARGE:
        g_out[name], d_out[name], m_out[name], v_out[name] = _adamw_shard(parts[name], w[name], m[name], v[name])
    g_small = {}
    for name in SMALL:
        row, nrows, ncols = SMALL_LAYOUT[name]
        g_small[name] = total[row:row + nrows, 0:ncols]
    g_small["conv_w"] = lax.dynamic_slice_in_dim(g_small["conv_w"], me * conv_rows, conv_rows, axis=1)
    ds, ms, vs = _adamw_small(
        [g_small[k] for k in SMALL], [w[k] for k in SMALL], [m[k] for k in SMALL], [v[k] for k in SMALL])
    for i, name in enumerate(SMALL):
        g_out[name], d_out[name], m_out[name], v_out[name] = g_small[name], ds[i], ms[i], vs[i]

    def shaped(value, name):
        return (value.T if name in TRANSPOSED else value).reshape(given[name].shape)

    loss = total[SMALL_LAYOUT["loss"][0], 0]
    outs = [loss, dx0.reshape(x.shape)]
    for group in (g_out, d_out, m_out, v_out):
        outs += [shaped(group[name], name) for name in WEIGHTS]
    return tuple(outs)
```

```python
import jax
import jax.numpy as jnp
from jax import lax
from jax.experimental import pallas as pl
from jax.experimental.pallas import tpu as pltpu

F32 = jnp.float32
BF16 = jnp.bfloat16
MESH_IDS = pl.DeviceIdType.MESH

N_DEV = 8
EPS = 1e-6
HGRN_HEADS = 4
HGRN_DK = 128
HGRN_W = 512
CHUNK = 64
MEM_HEADS = 4
MEM_HD = 256
ADAM_LR = 0.001
ADAM_B1 = 0.9
ADAM_B2 = 0.999
ADAM_EPS = 1e-08
ADAM_WD = 0.01
ADAM_STEP = 10

TOKEN_TILE = 256
REDUCE_TILE = 1024
ADAMW_TILE_ELEMENTS = 256 * 1024
MIDDLE_EIGHTHS = 5
MXU_ROWS = 256
VMEM_LIMIT = 60 * 1024 * 1024
SMALL_ROWS = 16
NT = (((1,), (1,)), ((), ()))
TN = (((0,), (0,)), ((), ()))


def _params(sem=None):
    return pltpu.CompilerParams(dimension_semantics=sem, vmem_limit_bytes=VMEM_LIMIT)


def _dot(a, b, dims=None):
    if dims is None:
        return jnp.dot(a, b, preferred_element_type=F32)
    return lax.dot_general(a, b, dims, preferred_element_type=F32)


def _sigmoid(v):
    return 1.0 / (1.0 + jnp.exp(-v))


def _rms(x, g):
    r = lax.rsqrt(jnp.mean(x * x, axis=-1, keepdims=True) + EPS)
    xh = x * r
    return xh * g, xh, r


def _rms_bwd(dh, xh, r, g):
    dxh = dh * g
    return r * (dxh - xh * jnp.mean(dxh * xh, axis=-1, keepdims=True))


def _full(shape):
    return pl.BlockSpec(shape, lambda *_: (0,) * len(shape))


def _rows(tm, width):
    return pl.BlockSpec((tm, width), lambda i: (i, 0))


def _rows_rev(tm, width, n):
    return pl.BlockSpec((tm, width), lambda i: (n - 1 - i, 0))


def _zero_at_start(*refs):
    @pl.when(pl.program_id(0) == 0)
    def _():
        for ref in refs:
            ref[...] = jnp.zeros_like(ref)


class _Exchange:
    def __init__(self, operands, out_shapes, scratch, start, finish, middle=None):
        self.operands, self.out_shapes, self.scratch = list(operands), list(out_shapes), list(scratch)
        self.start, self.middle, self.finish = start, middle, finish


def _call(body, *, name, grid, in_specs, out_specs, out_shape, args, scratch_shapes=(), exchange=None):
    semantics = ("arbitrary",) * len(grid)
    if exchange is None:
        out = pl.pallas_call(
            body, name=name, grid=grid, in_specs=in_specs, out_specs=out_specs, out_shape=out_shape,
            scratch_shapes=list(scratch_shapes), compiler_params=_params(semantics))(*args)
        return out, []
    hbm = pl.BlockSpec(memory_space=pltpu.HBM)
    n_in, n_out, n_scr = len(in_specs), len(out_specs), len(scratch_shapes)
    e_in, e_out = len(exchange.operands), len(exchange.out_shapes)

    def carried(*refs):
        ins, rest = refs[:n_in], refs[n_in:]
        e_ins, rest = rest[:e_in], rest[e_in:]
        outs, rest = rest[:n_out], rest[n_out:]
        e_outs, rest = rest[:e_out], rest[e_out:]
        scr, e_scr = rest[:n_scr], rest[n_scr:]
        first = last = None
        for axis, size in enumerate(grid):
            at_start, at_end = pl.program_id(axis) == 0, pl.program_id(axis) == size - 1
            first = at_start if first is None else jnp.logical_and(first, at_start)
            last = at_end if last is None else jnp.logical_and(last, at_end)

        @pl.when(first)
        def _():
            exchange.start(e_ins, e_outs, e_scr)

        body(*ins, *outs, *scr)

        if exchange.middle is not None:
            assert len(grid) == 1

            @pl.when(pl.program_id(0) == (grid[0] * MIDDLE_EIGHTHS) // 8)
            def _():
                exchange.middle(e_ins, e_outs, e_scr)

        @pl.when(last)
        def _():
            exchange.finish(e_ins, e_outs, e_scr)

    out = pl.pallas_call(
        carried, name=name, grid=grid, in_specs=list(in_specs) + [hbm] * e_in,
        out_specs=list(out_specs) + [hbm] * e_out, out_shape=list(out_shape) + exchange.out_shapes,
        scratch_shapes=list(scratch_shapes) + exchange.scratch,
        compiler_params=pltpu.CompilerParams(
            dimension_semantics=semantics, vmem_limit_bytes=VMEM_LIMIT, has_side_effects=True),
    )(*args, *exchange.operands)
    return out[:n_out], out[n_out:]


def _run_exchange(exchange, name):
    hbm = pl.BlockSpec(memory_space=pltpu.HBM)
    e_in, e_out = len(exchange.operands), len(exchange.out_shapes)

    def body(*refs):
        e_ins, e_outs, e_scr = refs[:e_in], refs[e_in:e_in + e_out], refs[e_in + e_out:]
        exchange.start(e_ins, e_outs, e_scr)
        if exchange.middle is not None:
            exchange.middle(e_ins, e_outs, e_scr)
        exchange.finish(e_ins, e_outs, e_scr)

    return pl.pallas_call(
        body, name=name, in_specs=[hbm] * e_in, out_specs=[hbm] * e_out, out_shape=exchange.out_shapes,
        scratch_shapes=exchange.scratch, compiler_params=pltpu.CompilerParams(has_side_effects=True),
    )(*exchange.operands)


def _loss_head(xo, gf, tgt):
    d = xo.shape[1]
    y, xh, r = _rms(xo, gf)
    err = y - tgt
    dy = err * (1.0 / d)
    loss = 0.5 * jnp.sum(jnp.sum(err * err, axis=-1, keepdims=True) * (1.0 / d), axis=0, keepdims=True)
    return _rms_bwd(dy, xh, r, gf), loss, jnp.sum(dy * xh, axis=0, keepdims=True)


def _ffn_fwd(x, g, wg, wu, wd, exchange=None, head=None):
    t, d = x.shape
    f = wg.shape[0]
    tm = min(TOKEN_TILE, t)

    def body(x_ref, g_ref, wg_ref, wu_ref, wd_ref, *rest):
        if head is None:
            xo_ref, a_ref, b_ref, s_ref = rest
        else:
            gf_ref, tgt_ref, xo_ref, a_ref, b_ref, s_ref, loss_ref, dgf_ref = rest
            _zero_at_start(loss_ref, dgf_ref)
        xv = x_ref[...]
        h, _, _ = _rms(xv, g_ref[...])
        hb = h.astype(BF16)
        a = _dot(hb, wg_ref[...], NT)
        b = _dot(hb, wu_ref[...], NT)
        s = (a * _sigmoid(a) * b).astype(BF16)
        xo = xv + 0.5 * _dot(s, wd_ref[...])
        if head is None:
            xo_ref[...] = xo
        else:
            xo_ref[...], loss, dgf = _loss_head(xo, gf_ref[...], tgt_ref[...])
            loss_ref[...] += jnp.broadcast_to(loss, (1, 128))
            dgf_ref[...] += dgf
        a_ref[...] = a.astype(BF16)
        b_ref[...] = b.astype(BF16)
        s_ref[...] = s

    in_specs = [_rows(tm, d), _full((1, d)), _full((f, d)), _full((f, d)), _full((f, d))]
    out_specs = [_rows(tm, d), _rows(tm, f), _rows(tm, f), _rows(tm, f)]
    out_shape = [
        jax.ShapeDtypeStruct((t, d), F32),
        jax.ShapeDtypeStruct((t, f), BF16),
        jax.ShapeDtypeStruct((t, f), BF16),
        jax.ShapeDtypeStruct((t, f), BF16),
    ]
    args = (x, g, wg, wu, wd)
    if head is not None:
        in_specs += [_full((1, d)), _rows(tm, d)]
        out_specs += [_full((1, 128)), _full((1, d))]
        out_shape += [jax.ShapeDtypeStruct((1, 128), F32), jax.ShapeDtypeStruct((1, d), F32)]
        args += tuple(head)
    return _call(
        body, name="ffn_fwd", grid=(t // tm,), in_specs=in_specs, out_specs=out_specs, out_shape=out_shape,
        args=args, exchange=exchange)


def _ffn_up(x, g, wg, wu, exchange=None):
    t, d = x.shape
    f = wg.shape[0]
    tm = min(TOKEN_TILE, t)

    def body(x_ref, g_ref, wg_ref, wu_ref, a_ref, b_ref, s_ref):
        h, _, _ = _rms(x_ref[...], g_ref[...])
        hb = h.astype(BF16)
        a = _dot(hb, wg_ref[...], NT)
        b = _dot(hb, wu_ref[...], NT)
        a_ref[...] = a.astype(BF16)
        b_ref[...] = b.astype(BF16)
        s_ref[...] = (a * _sigmoid(a) * b).astype(BF16)

    return _call(
        body, name="ffn_up", grid=(t // tm,),
        in_specs=[_rows(tm, d), _full((1, d)), _full((f, d)), _full((f, d))],
        out_specs=[_rows(tm, f)] * 3, out_shape=[jax.ShapeDtypeStruct((t, f), BF16)] * 3,
        args=(x, g, wg, wu), exchange=exchange)


def _ffn_down(x, s, wd, exchange=None):
    t, d = x.shape
    f = wd.shape[0]
    tm = min(TOKEN_TILE, t)

    def body(x_ref, s_ref, wd_ref, xo_ref):
        xo_ref[...] = x_ref[...] + 0.5 * _dot(s_ref[...], wd_ref[...])

    return _call(
        body, name="ffn_down", grid=(t // tm,),
        in_specs=[_rows(tm, d), _rows(tm, f), _full((f, d))],
        out_specs=[_rows(tm, d)], out_shape=[jax.ShapeDtypeStruct((t, d), F32)],
        args=(x, s, wd), exchange=exchange)


def _ffn_bwd(x, g, dxo, a, b, wg, wu, wd, exchange=None):
    t, d = x.shape
    f = wg.shape[0]
    tm = min(TOKEN_TILE, t)

    def body(x_ref, g_ref, dxo_ref, a_ref, b_ref, wg_ref, wu_ref, wd_ref, dx_ref, da_ref, db_ref, h_ref, dg_ref):
        _zero_at_start(dg_ref)
        gv = g_ref[...]
        h, xh, r = _rms(x_ref[...], gv)
        dxo = dxo_ref[...]
        ds = _dot((0.5 * dxo).astype(BF16), wd_ref[...], NT)
        af = a_ref[...].astype(F32)
        bf = b_ref[...].astype(F32)
        sg = _sigmoid(af)
        da = (ds * bf * (sg * (1.0 + af * (1.0 - sg)))).astype(BF16)
        db = (ds * (af * sg)).astype(BF16)
        dh = _dot(da, wg_ref[...]) + _dot(db, wu_ref[...])
        dx_ref[...] = _rms_bwd(dh, xh, r, gv) + dxo
        da_ref[...] = da
        db_ref[...] = db
        h_ref[...] = h.astype(BF16)
        dg_ref[...] += jnp.sum(dh * xh, axis=0, keepdims=True)

    return _call(
        body,
        name="ffn_bwd",
        grid=(t // tm,),
        in_specs=[
            _rows(tm, d), _full((1, d)), _rows(tm, d), _rows(tm, f), _rows(tm, f),
            _full((f, d)), _full((f, d)), _full((f, d)),
        ],
        out_specs=[_rows(tm, d), _rows(tm, f), _rows(tm, f), _rows(tm, d), _full((1, d))],
        out_shape=[
            jax.ShapeDtypeStruct((t, d), F32),
            jax.ShapeDtypeStruct((t, f), BF16),
            jax.ShapeDtypeStruct((t, f), BF16),
            jax.ShapeDtypeStruct((t, d), BF16),
            jax.ShapeDtypeStruct((1, d), F32),
        ],
        args=(x, g, dxo, a, b, wg, wu, wd),
        exchange=exchange,
    )


def _weight_grad(a, b, scale=1.0, exchange=None):
    t, m = a.shape
    n = b.shape[1]
    chips = N_DEV // 2
    r = m // N_DEV
    tk = min(REDUCE_TILE, t)
    halves = 2
    nb = n // halves
    nk = t // tk

    def body(a_ref, b_ref, o_ref, acc, send_buf, recv_buf, send_sems, recv_sems):
        k, j = pl.program_id(0), pl.program_id(1)
        x, y, c, _ = _mesh_place()
        sibling, _ = _peer(x, y, c, 1)
        bv = b_ref[...]
        if scale != 1.0:
            bv = bv * scale
        bb = bv.astype(BF16)
        acc_half = acc.at[j]

        @pl.when(k == 0)
        def _():
            acc_half[...] = jnp.zeros_like(acc_half)

        for i in range(m // MXU_ROWS):
            rows = slice(i * MXU_ROWS, (i + 1) * MXU_ROWS)
            acc_half[rows, :] += _dot(a_ref[:, rows].astype(BF16), bb, TN)

        def to_sibling(half):
            return _remote(send_buf.at[half], recv_buf.at[half], send_sems.at[half], recv_sems.at[half], sibling)

        def owned_rows(q, core):
            return pl.ds(pl.multiple_of((2 * q + core) * r, 8), r)

        for half in range(halves):
            @pl.when(jnp.logical_and(k == nk - 1, j == half))
            def _():
                for q in range(chips):
                    send_buf[half, q] = acc[half, owned_rows(q, 1 - c), :].astype(BF16)
                to_sibling(half).start()

        @pl.when(jnp.logical_and(k == nk - 1, j == halves - 1))
        def _():
            for half in range(halves):
                to_sibling(half).wait_send()
                to_sibling(half).wait_recv()
                for q in range(chips):
                    o_ref[q, :, half * nb:(half + 1) * nb] = (
                        acc[half, owned_rows(q, c), :] + recv_buf[half, q].astype(F32)).astype(BF16)

    (partial,), arrived = _call(
        body,
        name="weight_grad",
        grid=(nk, halves),
        in_specs=[pl.BlockSpec((tk, m), lambda k, j: (k, 0)), pl.BlockSpec((tk, nb), lambda k, j: (k, j))],
        out_specs=[pl.BlockSpec((chips, r, n), lambda k, j: (0, 0, 0))],
        out_shape=[jax.ShapeDtypeStruct((chips, r, n), BF16)],
        scratch_shapes=[
            pltpu.VMEM((halves, m, nb), F32),
            pltpu.VMEM((halves, chips, r, nb), BF16), pltpu.VMEM((halves, chips, r, nb), BF16),
            pltpu.SemaphoreType.DMA((halves,)), pltpu.SemaphoreType.DMA((halves,)),
        ],
        args=(a, b),
        exchange=exchange,
    )
    return partial, arrived


def _chunk_cumsum(v, reverse=False):
    n, width = v.shape
    row = lax.broadcasted_iota(jnp.int32, (n, n), 0)
    col = lax.broadcasted_iota(jnp.int32, (n, n), 1)
    earlier = col >= row if reverse else col <= row
    tri = jnp.where(jnp.logical_and(row // CHUNK == col // CHUNK, earlier), 1.0, 0.0).astype(BF16)
    hi = v.astype(BF16)
    rest = v - hi.astype(F32)
    mid = rest.astype(BF16)
    low = (rest - mid.astype(F32)).astype(BF16)
    sums = _dot(tri, jnp.concatenate([hi, mid, low], axis=1))
    return sums[:, 0:width] + sums[:, width:2 * width] + sums[:, 2 * width:3 * width]


def _shift_rows(v, shift, edge):
    n = v.shape[0]
    row = lax.broadcasted_iota(jnp.int32, (n, 1), 0)
    out = pltpu.roll(v, shift % n, axis=0)
    if shift > 0:
        for j in range(shift):
            out = jnp.where(row == j, edge[8 - shift + j:8 - shift + j + 1, :], out)
    else:
        for j in range(-shift):
            out = jnp.where(row == n + shift + j, edge[j:j + 1, :], out)
    return out


def _gates(z, lbp):
    w = HGRN_W
    lb = _sigmoid(lbp[0:1, :] - lbp[1:2, :])
    zq = z[:, 0:w]
    sig = _sigmoid(z[:, w:2 * w])
    f = lb + (1.0 - lb) * sig
    sq = _sigmoid(zq)
    q = zq * sq * HGRN_DK ** -0.5
    return lb, sig, f, sq, q


def _decayed_operands(q, f, v, qh_buf, kh_buf, kbar_buf, v_buf, etot_buf):
    n, width = f.shape
    bcum = _chunk_cumsum(jnp.log(f))
    total = jnp.concatenate(
        [jnp.broadcast_to(bcum[c + CHUNK - 1:c + CHUNK, :], (CHUNK, width)) for c in range(0, n, CHUNK)], axis=0)
    eb, enb, erest = jnp.exp(bcum), jnp.exp(-bcum), jnp.exp(total - bcum)
    kk = 1.0 - f
    qh_buf[...] = (q * eb).astype(BF16)
    kh_buf[...] = (kk * enb).astype(BF16)
    kbar_buf[...] = (kk * erest).astype(BF16)
    v_buf[...] = v.astype(BF16)
    etot_buf[...] = jnp.exp(total)
    return eb, enb, erest


def _short_conv(u, edge, cw):
    return cw[0:1, :] * _shift_rows(u, 2, edge) + cw[1:2, :] * _shift_rows(u, 1, edge) + cw[2:3, :] * u


def _block_causal_mask(n):
    row = lax.broadcasted_iota(jnp.int32, (n, n), 0)
    col = lax.broadcasted_iota(jnp.int32, (n, n), 1)
    return jnp.logical_and(row // CHUNK == col // CHUNK, col <= row)


def _spread(v, chunk_of_row, nc):
    return jnp.concatenate([jnp.where(chunk_of_row == c, v, jnp.zeros_like(v)) for c in range(nc)], axis=1)


def _pick(r, chunk_of_row, nc):
    out = jnp.where(chunk_of_row == 0, r[:, 0:HGRN_DK], 0.0)
    for c in range(1, nc):
        out = out + jnp.where(chunk_of_row == c, r[:, c * HGRN_DK:(c + 1) * HGRN_DK], 0.0)
    return out


def _mix_fwd(x, g, w_in, lbp, gh, convw_t, w_out, exchange=None):
    t, d = x.shape
    zw = w_in.shape[0]
    w = HGRN_W
    tm = min(TOKEN_TILE, t)
    nc = tm // CHUNK
    n_chunks = t // CHUNK

    def body(x_ref, g_ref, win_ref, lbp_ref, gh_ref, cw_ref, wout_ref,
             xo_ref, z_ref, o_ref, st_ref, y_ref, state, ucarry, qh_buf, kh_buf, kbar_buf, v_buf, etot_buf):
        _zero_at_start(state, ucarry)
        xv = x_ref[...]
        h, _, _ = _rms(xv, g_ref[...])
        z_ref[...] = _dot(h.astype(BF16), win_ref[...], NT)
        z = z_ref[...]
        _, _, f, _, q = _gates(z, lbp_ref[...])
        _decayed_operands(q, f, z[:, 2 * w:3 * w], qh_buf, kh_buf, kbar_buf, v_buf, etot_buf)
        mask = _block_causal_mask(tm)
        chunk_of_row = lax.broadcasted_iota(jnp.int32, (tm, 1), 0) // CHUNK
        heads = range(HGRN_HEADS)
        hcols = [slice(hd * HGRN_DK, (hd + 1) * HGRN_DK) for hd in heads]
        qh = [qh_buf[:, hcols[hd]] for hd in heads]
        vb = [v_buf[:, hcols[hd]] for hd in heads]
        scores = [jnp.where(mask, _dot(qh[hd], kh_buf[:, hcols[hd]], NT), 0.0).astype(BF16) for hd in heads]
        gains = [_dot(_spread(vb[hd], chunk_of_row, nc), kbar_buf[:, hcols[hd]], TN) for hd in heads]
        entering = []
        for hd in heads:
            states, st = [], state[hd]
            for c in range(nc):
                states.append(st)
                st_ref[c, hd] = st
                st = st * etot_buf[c * CHUNK:c * CHUNK + 1, hcols[hd]] + gains[hd][c * HGRN_DK:(c + 1) * HGRN_DK, :]
            state[hd] = st
            entering.append(jnp.concatenate(states, axis=0).astype(BF16))
        from_states = [_dot(qh[hd], entering[hd], NT) for hd in heads]
        o_heads = [_dot(scores[hd], vb[hd]) + _pick(from_states[hd], chunk_of_row, nc) for hd in heads]
        o_ref[...] = jnp.concatenate(o_heads, axis=1)
        ghv = gh_ref[...]
        normed = jnp.concatenate([_rms(o_heads[hd], ghv[:, hcols[hd]])[0] for hd in heads], axis=1)
        zg = z[:, 3 * w:4 * w]
        u = z[:, 5 * w:6 * w] * z[:, 6 * w:7 * w]
        conv = _short_conv(u, ucarry[...], cw_ref[...])
        ucarry[...] = u[tm - 8:tm, :]
        y = jnp.concatenate([normed * (zg * _sigmoid(zg)), z[:, 4 * w:5 * w] * conv], axis=1).astype(BF16)
        y_ref[...] = y
        xo_ref[...] = xv + _dot(y, wout_ref[...])

    return _call(
        body,
        name="mix_fwd",
        grid=(t // tm,),
        in_specs=[
            _rows(tm, d), _full((1, d)), _full((zw, d)), _full((2, w)), _full((1, w)), _full((3, w)),
            _full((2 * w, d)),
        ],
        out_specs=[
            _rows(tm, d), _rows(tm, zw), _rows(tm, w),
            pl.BlockSpec((nc, HGRN_HEADS, HGRN_DK, HGRN_DK), lambda i: (i, 0, 0, 0)),
            _rows(tm, 2 * w),
        ],
        out_shape=[
            jax.ShapeDtypeStruct((t, d), F32),
            jax.ShapeDtypeStruct((t, zw), F32),
            jax.ShapeDtypeStruct((t, w), F32),
            jax.ShapeDtypeStruct((n_chunks, HGRN_HEADS, HGRN_DK, HGRN_DK), F32),
            jax.ShapeDtypeStruct((t, 2 * w), BF16),
        ],
        scratch_shapes=[
            pltpu.VMEM((HGRN_HEADS, HGRN_DK, HGRN_DK), F32), pltpu.VMEM((8, w), F32),
            pltpu.VMEM((tm, w), BF16), pltpu.VMEM((tm, w), BF16), pltpu.VMEM((tm, w), BF16),
            pltpu.VMEM((tm, w), BF16), pltpu.VMEM((tm, w), F32),
        ],
        args=(x, g, w_in, lbp, gh, convw_t, w_out),
        exchange=exchange,
    )


def _mix_bwd(x, g, dxo, z, o, states, w_in, lbp, gh, convw_t, w_out, exchange=None):
    t, d = x.shape
    zw = w_in.shape[0]
    w = HGRN_W
    tm = min(TOKEN_TILE, t)
    nc = tm // CHUNK
    n = t // tm

    def body(x_ref, g_ref, dxo_ref, z_ref, zprev_ref, o_ref, st_ref, win_ref, lbp_ref, gh_ref, cw_ref, wout_ref,
             dx_ref, dz_ref, h_ref, dg_ref, dlbp_ref, dgh_ref, dcw_ref,
             dstate, dcarry, do_buf, qh_buf, kh_buf, kbar_buf, v_buf, etot_buf):
        _zero_at_start(dstate, dcarry, dg_ref, dlbp_ref, dgh_ref, dcw_ref)
        gv = g_ref[...]
        h, xh, r = _rms(x_ref[...], gv)
        h_ref[...] = h.astype(BF16)
        dxo = dxo_ref[...]
        dy = _dot(dxo.astype(BF16), wout_ref[...], NT)
        z = z_ref[...]
        lb, sig, f, sq, q = _gates(z, lbp_ref[...])
        eb, enb, erest = _decayed_operands(q, f, z[:, 2 * w:3 * w], qh_buf, kh_buf, kbar_buf, v_buf, etot_buf)

        ghv = gh_ref[...]
        zg = z[:, 3 * w:4 * w]
        sgz = _sigmoid(zg)
        dyh = dy[:, 0:w]
        don = dyh * (zg * sgz)
        heads = range(HGRN_HEADS)
        hcols = [slice(hd * HGRN_DK, (hd + 1) * HGRN_DK) for hd in heads]
        norms = [_rms(o_ref[:, hcols[hd]], ghv[:, hcols[hd]]) for hd in heads]
        on = jnp.concatenate([norms[hd][0] for hd in heads], axis=1)
        oh = jnp.concatenate([norms[hd][1] for hd in heads], axis=1)
        dz_ref[:, 3 * w:4 * w] = (dyh * on * (sgz * (1.0 + zg * (1.0 - sgz)))).astype(BF16)
        dgh_ref[...] += jnp.sum(don * oh, axis=0, keepdims=True)
        do_buf[...] = jnp.concatenate(
            [_rms_bwd(don[:, hcols[hd]], norms[hd][1], norms[hd][2], ghv[:, hcols[hd]]) for hd in heads],
            axis=1).astype(BF16)

        zb = z[:, 4 * w:5 * w]
        zc = z[:, 5 * w:6 * w]
        zu = z[:, 6 * w:7 * w]
        u = zc * zu
        cw = cw_ref[...]
        zp = zprev_ref[...]
        uprev = jnp.where(pl.program_id(0) == n - 1, 0.0, zp[:, 5 * w:6 * w] * zp[:, 6 * w:7 * w])
        dyc = dy[:, w:2 * w]
        dz_ref[:, 4 * w:5 * w] = (dyc * _short_conv(u, uprev, cw)).astype(BF16)
        dconv = dyc * zb
        edge = dcarry[...]
        dconv1 = _shift_rows(dconv, -1, edge)
        dconv2 = _shift_rows(dconv, -2, edge)
        dcarry[...] = dconv[0:8, :]
        du = cw[2:3, :] * dconv + cw[1:2, :] * dconv1 + cw[0:1, :] * dconv2
        dz_ref[:, 5 * w:6 * w] = (du * zu).astype(BF16)
        dz_ref[:, 6 * w:7 * w] = (du * zc).astype(BF16)
        dcw_ref[...] += jnp.concatenate([
            jnp.sum(u * dconv2, axis=0, keepdims=True),
            jnp.sum(u * dconv1, axis=0, keepdims=True),
            jnp.sum(u * dconv, axis=0, keepdims=True)], axis=0)

        mask = _block_causal_mask(tm)
        chunk_of_row = lax.broadcasted_iota(jnp.int32, (tm, 1), 0) // CHUNK
        heads = range(HGRN_HEADS)
        hcols = [slice(hd * HGRN_DK, (hd + 1) * HGRN_DK) for hd in heads]
        qhb = [qh_buf[:, hcols[hd]] for hd in heads]
        khb = [kh_buf[:, hcols[hd]] for hd in heads]
        vb = [v_buf[:, hcols[hd]] for hd in heads]
        dob = [do_buf[:, hcols[hd]] for hd in heads]
        scores = [jnp.where(mask, _dot(qhb[hd], khb[hd], NT), 0.0).astype(BF16) for hd in heads]
        dscores = [jnp.where(mask, _dot(dob[hd], vb[hd], NT), 0.0).astype(BF16) for hd in heads]
        gains = [_dot(_spread(dob[hd], chunk_of_row, nc), qhb[hd], TN) for hd in heads]
        dst_rows, dst_lanes, st_lanes, carries = [], [], [], []
        for hd in heads:
            entering = [st_ref[c, hd] for c in range(nc)]
            leaving, carried_back = [None] * nc, [None] * nc
            dst = dstate[hd]
            for c in reversed(range(nc)):
                elast = etot_buf[c * CHUNK:c * CHUNK + 1, hcols[hd]]
                leaving[c] = dst
                carried_back[c] = jnp.sum(dst * entering[c], axis=0, keepdims=True) * elast
                dst = dst * elast + gains[hd][c * HGRN_DK:(c + 1) * HGRN_DK, :]
            dstate[hd] = dst
            dst_rows.append(jnp.concatenate(leaving, axis=0).astype(BF16))
            dst_lanes.append(jnp.concatenate(leaving, axis=1).astype(BF16))
            st_lanes.append(jnp.concatenate(entering, axis=1).astype(BF16))
            carries.append(carried_back)
        dv = [_dot(scores[hd], dob[hd], TN) + _pick(_dot(kbar_buf[:, hcols[hd]], dst_rows[hd], NT), chunk_of_row, nc)
              for hd in heads]
        dz_ref[:, 2 * w:3 * w] = jnp.concatenate(dv, axis=1).astype(BF16)
        dqh = jnp.concatenate(
            [_dot(dscores[hd], khb[hd]) + _pick(_dot(dob[hd], st_lanes[hd]), chunk_of_row, nc) for hd in heads], axis=1)
        dkh = jnp.concatenate([_dot(dscores[hd], qhb[hd], TN) for hd in heads], axis=1)
        dkbar = jnp.concatenate([_pick(_dot(vb[hd], dst_lanes[hd]), chunk_of_row, nc) for hd in heads], axis=1)

        kbar_dkbar = kbar_buf[...].astype(F32) * dkbar
        db = qh_buf[...].astype(F32) * dqh - kh_buf[...].astype(F32) * dkh - kbar_dkbar
        through_last = jnp.concatenate([
            jnp.broadcast_to(
                jnp.sum(kbar_dkbar[c * CHUNK:(c + 1) * CHUNK], axis=0, keepdims=True)
                + jnp.concatenate([carries[hd][c] for hd in heads], axis=1),
                (CHUNK, w))
            for c in range(nc)], axis=0)
        dlogf = _chunk_cumsum(db, reverse=True) + through_last
        df = dlogf / f - (dkh * enb + dkbar * erest)
        zq = z[:, 0:w]
        dz_ref[:, 0:w] = (dqh * eb * HGRN_DK ** -0.5 * (sq * (1.0 + zq * (1.0 - sq)))).astype(BF16)
        dz_ref[:, w:2 * w] = (df * (1.0 - lb) * sig * (1.0 - sig)).astype(BF16)
        dlb = jnp.sum(df * (1.0 - sig), axis=0, keepdims=True) * lb * (1.0 - lb)
        dlbp_ref[...] += jnp.concatenate([dlb, -dlb], axis=0)

        dh = _dot(dz_ref[...], win_ref[...])
        dx_ref[...] = _rms_bwd(dh, xh, r, gv) + dxo
        dg_ref[...] += jnp.sum(dh * xh, axis=0, keepdims=True)

    return _call(
        body,
        name="mix_bwd",
        grid=(n,),
        in_specs=[
            _rows_rev(tm, d, n), _full((1, d)), _rows_rev(tm, d, n), _rows_rev(tm, zw, n),
            pl.BlockSpec((8, zw), lambda i: (jnp.maximum((n - 1 - i) * (tm // 8) - 1, 0), 0)),
            _rows_rev(tm, w, n),
            pl.BlockSpec((nc, HGRN_HEADS, HGRN_DK, HGRN_DK), lambda i: (n - 1 - i, 0, 0, 0)),
            _full((zw, d)), _full((2, w)), _full((1, w)), _full((3, w)), _full((2 * w, d)),
        ],
        out_specs=[
            _rows_rev(tm, d, n), _rows_rev(tm, zw, n), _rows_rev(tm, d, n),
            _full((1, d)), _full((2, w)), _full((1, w)), _full((3, w)),
        ],
        out_shape=[
            jax.ShapeDtypeStruct((t, d), F32),
            jax.ShapeDtypeStruct((t, zw), BF16),
            jax.ShapeDtypeStruct((t, d), BF16),
            jax.ShapeDtypeStruct((1, d), F32),
            jax.ShapeDtypeStruct((2, w), F32),
            jax.ShapeDtypeStruct((1, w), F32),
            jax.ShapeDtypeStruct((3, w), F32),
        ],
        scratch_shapes=[
            pltpu.VMEM((HGRN_HEADS, HGRN_DK, HGRN_DK), F32), pltpu.VMEM((8, w), F32),
            pltpu.VMEM((tm, w), BF16),
            pltpu.VMEM((tm, w), BF16), pltpu.VMEM((tm, w), BF16), pltpu.VMEM((tm, w), BF16),
            pltpu.VMEM((tm, w), BF16), pltpu.VMEM((tm, w), F32),
        ],
        args=(x, g, dxo, z, z, o, states, w_in, lbp, gh, convw_t, w_out),
        exchange=exchange,
    )


def _memkv_fwd(mem, g, wkv):
    m, d = mem.shape
    nb, _, cb = wkv.shape

    def body(mem_ref, g_ref, wkv_ref, kv_ref):
        mn, _, _ = _rms(mem_ref[...], g_ref[...])
        mnb = mn.astype(BF16)
        for j in range(nb):
            kv_ref[:, j * cb:(j + 1) * cb] = _dot(mnb, wkv_ref[j]).astype(BF16)

    return pl.pallas_call(
        body,
        name="memkv_fwd",
        out_shape=jax.ShapeDtypeStruct((m, nb * cb), BF16),
        compiler_params=_params(),
    )(mem, g, wkv)


def _memkv_bwd(mem, g, dkv, wkv):
    m, d = mem.shape
    nb, _, cb = wkv.shape
    chips = nb // 2

    def body(mem_ref, g_ref, dkv_ref, wkv_ref, dw_ref, dg_ref, dw_all, send_buf, recv_buf, send_sem, recv_sem):
        x, y, c, _ = _mesh_place()
        sibling, _ = _peer(x, y, c, 1)
        mn, xh, _ = _rms(mem_ref[...], g_ref[...])
        mnb = mn.astype(BF16)
        dmn = jnp.zeros((m, d), F32)
        for j in range(nb):
            dkvb = dkv_ref[:, j * cb:(j + 1) * cb].astype(BF16)
            dw_all[j] = _dot(mnb, dkvb, TN)
            dmn = dmn + _dot(dkvb, wkv_ref[j], NT)
        dg_ref[...] = jnp.sum(dmn * xh, axis=0, keepdims=True)
        for q in range(chips):
            send_buf[q] = dw_all[2 * q + 1 - c].astype(BF16)
        to_sibling = _remote(send_buf, recv_buf, send_sem, recv_sem, sibling)
        to_sibling.start()
        to_sibling.wait_send()
        to_sibling.wait_recv()
        for q in range(chips):
            dw_ref[q] = (dw_all[2 * q + c] + recv_buf[q].astype(F32)).astype(BF16)

    return pl.pallas_call(
        body,
        name="memkv_bwd",
        out_shape=[jax.ShapeDtypeStruct((chips, d, cb), BF16), jax.ShapeDtypeStruct((1, d), F32)],
        scratch_shapes=[
            pltpu.VMEM((nb, d, cb), F32), pltpu.VMEM((chips, d, cb), BF16), pltpu.VMEM((chips, d, cb), BF16),
            pltpu.SemaphoreType.DMA, pltpu.SemaphoreType.DMA,
        ],
        compiler_params=_params(),
    )(mem, g, dkv, wkv)


def _softmax_rows(qm_h, k_h):
    sc = _dot(qm_h, k_h, NT) * MEM_HD ** -0.5
    e = jnp.exp(sc - jnp.max(sc, axis=-1, keepdims=True))
    return e / jnp.sum(e, axis=-1, keepdims=True)


def _xattn_fwd(x, g, wq, kv, wo, exchange=None):
    t, d = x.shape
    m = kv.shape[0]
    tm = min(TOKEN_TILE, t)

    def body(x_ref, g_ref, wq_ref, kv_ref, wo_ref, xo_ref, hq_ref, qm_ref, att_ref):
        xv = x_ref[...]
        h, _, _ = _rms(xv, g_ref[...])
        hb = h.astype(BF16)
        hq_ref[...] = hb
        qm = _dot(hb, wq_ref[...]).astype(BF16)
        qm_ref[...] = qm
        heads = range(MEM_HEADS)
        kcols = [slice(hd * MEM_HD, (hd + 1) * MEM_HD) for hd in heads]
        p = [_softmax_rows(qm[:, kcols[hd]], kv_ref[:, kcols[hd]]) for hd in heads]
        att = jnp.concatenate(
            [_dot(p[hd].astype(BF16), kv_ref[:, d + hd * MEM_HD:d + (hd + 1) * MEM_HD]) for hd in heads],
            axis=1).astype(BF16)
        att_ref[...] = att
        xo_ref[...] = xv + _dot(att, wo_ref[...])

    return _call(
        body,
        name="xattn_fwd",
        grid=(t // tm,),
        in_specs=[_rows(tm, d), _full((1, d)), _full((d, d)), _full((m, 2 * d)), _full((d, d))],
        out_specs=[_rows(tm, d), _rows(tm, d), _rows(tm, d), _rows(tm, d)],
        out_shape=[
            jax.ShapeDtypeStruct((t, d), F32),
            jax.ShapeDtypeStruct((t, d), BF16),
            jax.ShapeDtypeStruct((t, d), BF16),
            jax.ShapeDtypeStruct((t, d), BF16),
        ],
        args=(x, g, wq, kv, wo),
        exchange=exchange,
    )


def _xattn_bwd(x, g, dxo, qm, kv, wq, wo, exchange=None):
    t, d = x.shape
    m = kv.shape[0]
    tm = min(TOKEN_TILE, t)

    def body(x_ref, g_ref, dxo_ref, qm_ref, kv_ref, wq_ref, wo_ref, dx_ref, dqm_ref, dkv_ref, dg_ref):
        _zero_at_start(dkv_ref, dg_ref)
        gv = g_ref[...]
        _, xh, r = _rms(x_ref[...], gv)
        dxo = dxo_ref[...]
        datt = _dot(dxo.astype(BF16), wo_ref[...], NT).astype(BF16)
        heads = range(MEM_HEADS)
        kcols = [slice(hd * MEM_HD, (hd + 1) * MEM_HD) for hd in heads]
        vcols = [slice(d + hd * MEM_HD, d + (hd + 1) * MEM_HD) for hd in heads]
        qm_h = [qm_ref[:, kcols[hd]] for hd in heads]
        p = [_softmax_rows(qm_h[hd], kv_ref[:, kcols[hd]]) for hd in heads]
        dp = [_dot(datt[:, kcols[hd]], kv_ref[:, vcols[hd]], NT) for hd in heads]
        dsc = [(p[hd] * (dp[hd] - jnp.sum(p[hd] * dp[hd], axis=-1, keepdims=True)) * MEM_HD ** -0.5).astype(BF16)
               for hd in heads]
        dqm = jnp.concatenate([_dot(dsc[hd], kv_ref[:, kcols[hd]]) for hd in heads], axis=1).astype(BF16)
        dqm_ref[...] = dqm
        dkv_ref[...] += jnp.concatenate(
            [_dot(dsc[hd], qm_h[hd], TN) for hd in heads]
            + [_dot(p[hd].astype(BF16), datt[:, kcols[hd]], TN) for hd in heads], axis=1)
        dh = _dot(dqm, wq_ref[...], NT)
        dx_ref[...] = _rms_bwd(dh, xh, r, gv) + dxo
        dg_ref[...] += jnp.sum(dh * xh, axis=0, keepdims=True)

    return _call(
        body,
        name="xattn_bwd",
        grid=(t // tm,),
        in_specs=[
            _rows(tm, d), _full((1, d)), _rows(tm, d), _rows(tm, d), _full((m, 2 * d)), _full((d, d)), _full((d, d)),
        ],
        out_specs=[_rows(tm, d), _rows(tm, d), _full((m, 2 * d)), _full((1, d))],
        out_shape=[
            jax.ShapeDtypeStruct((t, d), F32),
            jax.ShapeDtypeStruct((t, d), BF16),
            jax.ShapeDtypeStruct((m, 2 * d), F32),
            jax.ShapeDtypeStruct((1, d), F32),
        ],
        args=(x, g, dxo, qm, kv, wq, wo),
        exchange=exchange,
    )


def _mesh_place():
    x, y, c = lax.axis_index("x"), lax.axis_index("y"), lax.axis_index("c")
    return x, y, c, 4 * x + 2 * y + c


def _peer(x, y, c, k):
    px = 1 - x if k & 4 else x
    py = 1 - y if k & 2 else y
    pc = 1 - c if k & 1 else c
    return (px, py, pc), 4 * px + 2 * py + pc


ICI_HOPS = (2, 4, 6)
N_HOPS = len(ICI_HOPS)


def _remote(src, dst, send_sem, recv_sem, peer):
    return pltpu.make_async_remote_copy(
        src_ref=src, dst_ref=dst, send_sem=send_sem, recv_sem=recv_sem, device_id=peer, device_id_type=MESH_IDS)


def _gather_exchange(shards):
    n = len(shards)

    def place():
        x, y, c, me = _mesh_place()
        sibling, _ = _peer(x, y, c, 1)
        to_x, from_x = _peer(x, y, c, 4)
        to_y, from_y = _peer(x, y, c, 2)
        _, from_diagonal = _peer(x, y, c, 6)
        onward = (c * to_y[0] + (1 - c) * to_x[0], c * to_y[1] + (1 - c) * to_x[1], c)
        passed_on = c * from_x + (1 - c) * from_y
        return me, sibling, (to_x, to_y, onward), (from_x, from_y, from_diagonal), passed_on

    def start(src, dst, sems):
        ici_send, ici_recv, pair_send, pair_recv, local = sems
        me, sibling, targets, _, _ = place()
        for a in range(n):
            pltpu.make_async_copy(src[a], dst[a].at[me], local.at[a]).start()
            for j in range(2):
                _remote(src[a], dst[a].at[me], ici_send.at[a, j], ici_recv.at[a, j], targets[j]).start()
            _remote(src[a], dst[a].at[me], pair_send.at[a, 0], pair_recv.at[a, 0], sibling).start()

    def to_sibling(dst, sems, a, j, origin, sibling):
        _, _, pair_send, pair_recv, _ = sems
        slot = dst[a].at[origin]
        return _remote(slot, slot, pair_send.at[a, 1 + j], pair_recv.at[a, 1 + j], sibling)

    def middle(src, dst, sems):
        ici_send, ici_recv, _, _, _ = sems
        _, sibling, targets, origins, passed_on = place()
        for a in range(n):
            for j in range(2):
                _remote(src[a], dst[a].at[origins[j]], ici_send.at[a, j], ici_recv.at[a, j], targets[j]).wait_recv()
            slot = dst[a].at[passed_on]
            _remote(slot, slot, ici_send.at[a, 2], ici_recv.at[a, 2], targets[2]).start()
            for j in range(2):
                to_sibling(dst, sems, a, j, origins[j], sibling).start()

    def finish(src, dst, sems):
        ici_send, ici_recv, pair_send, pair_recv, local = sems
        me, sibling, targets, origins, _ = place()
        for a in range(n):
            _remote(src[a], dst[a].at[origins[2]], ici_send.at[a, 2], ici_recv.at[a, 2], targets[2]).wait_recv()
            to_sibling(dst, sems, a, 2, origins[2], sibling).start()
        for a in range(n):
            pltpu.make_async_copy(src[a], dst[a].at[me], local.at[a]).wait()
            for j in range(N_HOPS):
                _remote(src[a], dst[a].at[me], ici_send.at[a, j], ici_recv.at[a, j], targets[j]).wait_send()
            for j, origin in enumerate((me,) + origins):
                from_sibling = origin + 1 - 2 * (origin % 2)
                passed = _remote(src[a], dst[a].at[from_sibling], pair_send.at[a, j], pair_recv.at[a, j], sibling)
                passed.wait_send()
                passed.wait_recv()

    return _Exchange(
        shards,
        [jax.ShapeDtypeStruct((N_DEV,) + s.shape, s.dtype) for s in shards],
        [
            pltpu.SemaphoreType.DMA((n, N_HOPS)), pltpu.SemaphoreType.DMA((n, N_HOPS)),
            pltpu.SemaphoreType.DMA((n, N_HOPS + 1)), pltpu.SemaphoreType.DMA((n, N_HOPS + 1)),
            pltpu.SemaphoreType.DMA((n,)),
        ],
        start, finish, middle)


def _scatter_copies(src, dst, sems, n, arrivals=False):
    send, recv, local = sems
    x, y, c, _ = _mesh_place()
    chip = 2 * x + y
    if arrivals is None:
        return [pltpu.make_async_copy(src[a].at[chip], dst[a].at[chip], local.at[a]) for a in range(n)]
    copies = []
    for a in range(n):
        for j, k in enumerate(ICI_HOPS):
            peer, _ = _peer(x, y, c, k)
            peer_chip = 2 * peer[0] + peer[1]
            slot = dst[a].at[peer_chip if arrivals else chip]
            copies.append(_remote(src[a].at[peer_chip], slot, send.at[a, j], recv.at[a, j], peer))
    return copies


def _scatter_start(src, dst, sems, n):
    for cp in _scatter_copies(src, dst, sems, n, arrivals=None) + _scatter_copies(src, dst, sems, n):
        cp.start()


def _scatter_finish(src, dst, sems, n):
    for cp in _scatter_copies(src, dst, sems, n, arrivals=None):
        cp.wait()
    for cp in _scatter_copies(src, dst, sems, n):
        cp.wait_send()
    for cp in _scatter_copies(src, dst, sems, n, arrivals=True):
        cp.wait_recv()


def _scatter_scratch(n):
    return [pltpu.SemaphoreType.DMA((n, N_HOPS)), pltpu.SemaphoreType.DMA((n, N_HOPS)), pltpu.SemaphoreType.DMA((n,))]


def _scatter_exchange(partials):
    n = len(partials)
    return _Exchange(
        partials, [jax.ShapeDtypeStruct(p.shape, p.dtype) for p in partials], _scatter_scratch(n),
        lambda src, dst, sems: _scatter_start(src, dst, sems, n),
        lambda src, dst, sems: _scatter_finish(src, dst, sems, n))


SMALL_LAYOUT = {
    "ffn1_norm": (0, 1, 1024), "mix_norm": (1, 1, 1024), "xattn_norm": (2, 1, 1024), "mem_norm": (3, 1, 1024),
    "ffn2_norm": (4, 1, 1024), "final_norm": (5, 1, 1024), "lb_param": (6, 2, 512), "hgrn_out_norm": (8, 1, 512),
    "conv_w": (9, 3, 512), "loss": (12, 1, 128),
}


def _final_exchange(partials, small):
    n = len(partials)
    names = list(small)
    width = 1024

    def body(*refs):
        src = refs[:n]
        pieces = refs[n:n + len(names)]
        dst = refs[n + len(names):2 * n + len(names)]
        total_ref = refs[2 * n + len(names)]
        pack, gathered, small_send, small_recv = refs[2 * n + len(names) + 1:2 * n + len(names) + 5]
        sems = refs[2 * n + len(names) + 5:]
        x, y, c, me = _mesh_place()
        pack[...] = jnp.zeros_like(pack)
        for name, piece in zip(names, pieces):
            row, nrows, ncols = SMALL_LAYOUT[name]
            pack[row:row + nrows, 0:ncols] = piece[...]
        for k in range(1, N_DEV):
            peer, _ = _peer(x, y, c, k)
            _remote(pack, gathered.at[me], small_send.at[k - 1], small_recv.at[k - 1], peer).start()
        _scatter_start(src, dst, sems, n)
        gathered[me] = pack[...]
        for k in range(1, N_DEV):
            peer, peer_index = _peer(x, y, c, k)
            landed = _remote(pack, gathered.at[peer_index], small_send.at[k - 1], small_recv.at[k - 1], peer)
            landed.wait_send()
            landed.wait_recv()
        total = gathered[0]
        for j in range(1, N_DEV):
            total = total + gathered[j]
        total_ref[...] = total
        _scatter_finish(src, dst, sems, n)

    hbm = pl.BlockSpec(memory_space=pltpu.HBM)
    vmem = pl.BlockSpec(memory_space=pltpu.VMEM)
    out = pl.pallas_call(
        body,
        name="final_exchange",
        in_specs=[hbm] * n + [vmem] * len(names),
        out_specs=[hbm] * n + [vmem],
        out_shape=[jax.ShapeDtypeStruct(p.shape, p.dtype) for p in partials]
        + [jax.ShapeDtypeStruct((SMALL_ROWS, width), F32)],
        scratch_shapes=[
            pltpu.VMEM((SMALL_ROWS, width), F32), pltpu.VMEM((N_DEV, SMALL_ROWS, width), F32),
            pltpu.SemaphoreType.DMA((N_DEV - 1,)), pltpu.SemaphoreType.DMA((N_DEV - 1,)),
        ] + _scatter_scratch(n),
        compiler_params=pltpu.CompilerParams(has_side_effects=True),
    )(*partials, *[small[k] for k in names])
    return out[:n], out[n]


def _adamw_math(w, g, m, v):
    m = ADAM_B1 * m + (1.0 - ADAM_B1) * g
    v = ADAM_B2 * v + (1.0 - ADAM_B2) * (g * g)
    m_hat = m / (1.0 - ADAM_B1 ** ADAM_STEP)
    v_hat = v / (1.0 - ADAM_B2 ** ADAM_STEP)
    delta = -ADAM_LR * (m_hat / (jnp.sqrt(v_hat) + ADAM_EPS) + ADAM_WD * w)
    return delta, m, v


def _adamw_shard(parts, w, m, v):
    r, c = w.shape
    n_parts = parts.shape[0]
    tr = max(rows for rows in range(16, r + 1, 16) if r % rows == 0 and rows * c <= ADAMW_TILE_ELEMENTS)

    def body(p_ref, w_ref, m_ref, v_ref, g_ref, d_ref, mo_ref, vo_ref):
        g = p_ref[0].astype(F32)
        for j in range(1, n_parts):
            g = g + p_ref[j].astype(F32)
        delta, mn, vn = _adamw_math(w_ref[...], g, m_ref[...], v_ref[...])
        g_ref[...] = g
        d_ref[...] = delta
        mo_ref[...] = mn
        vo_ref[...] = vn

    tile = pl.BlockSpec((tr, c), lambda i: (i, 0))
    return pl.pallas_call(
        body,
        name="adamw_shard",
        grid=(r // tr,),
        in_specs=[pl.BlockSpec((n_parts, tr, c), lambda i: (0, i, 0)), tile, tile, tile],
        out_specs=[tile] * 4,
        out_shape=[jax.ShapeDtypeStruct((r, c), F32)] * 4,
        compiler_params=_params(("parallel",)),
    )(parts, w, m, v)


def _adamw_small(gs, ws, ms, vs):
    n = len(gs)

    def body(*refs):
        g_refs, w_refs, m_refs, v_refs = refs[:n], refs[n:2 * n], refs[2 * n:3 * n], refs[3 * n:4 * n]
        d_out, m_out, v_out = refs[4 * n:5 * n], refs[5 * n:6 * n], refs[6 * n:7 * n]
        for i in range(n):
            delta, mn, vn = _adamw_math(w_refs[i][...], g_refs[i][...], m_refs[i][...], v_refs[i][...])
            d_out[i][...] = delta
            m_out[i][...] = mn
            v_out[i][...] = vn

    shapes = [jax.ShapeDtypeStruct(w.shape, F32) for w in ws]
    out = pl.pallas_call(
        body,
        name="adamw_small",
        out_shape=shapes * 3,
        compiler_params=_params(),
    )(*gs, *ws, *ms, *vs)
    return out[:n], out[n:2 * n], out[2 * n:]


TRANSPOSED = ("ffn1_gate", "ffn1_up", "w_in", "ffn2_gate", "ffn2_up", "conv_w")
GROUP_FFN1 = ("ffn1_gate", "ffn1_up", "ffn1_down")
GROUP_MIX = ("w_in", "w_out")
GROUP_XATTN = ("w_q_mem", "w_kv_mem", "w_o_mem")
GROUP_FFN2 = ("ffn2_gate", "ffn2_up", "ffn2_down")
LARGE = GROUP_FFN1 + GROUP_MIX + GROUP_XATTN + GROUP_FFN2
SMALL = ("ffn1_norm", "mix_norm", "lb_param", "hgrn_out_norm", "conv_w", "xattn_norm", "mem_norm", "ffn2_norm",
         "final_norm")
WEIGHTS = ("ffn1_norm", "ffn1_gate", "ffn1_up", "ffn1_down", "mix_norm", "w_in", "lb_param", "hgrn_out_norm",
           "conv_w", "w_out", "xattn_norm", "mem_norm", "w_q_mem", "w_kv_mem", "w_o_mem", "ffn2_norm", "ffn2_gate",
           "ffn2_up", "ffn2_down", "final_norm")


def kernel(x, mem, ffn1_norm, ffn1_gate, ffn1_up, ffn1_down, mix_norm, w_in, lb_param, hgrn_out_norm, conv_w, w_out, xattn_norm, mem_norm, w_q_mem, w_kv_mem, w_o_mem, ffn2_norm, ffn2_gate, ffn2_up, ffn2_down, final_norm, loss_target, m_ffn1_norm, m_ffn1_gate, m_ffn1_up, m_ffn1_down, m_mix_norm, m_w_in, m_lb_param, m_hgrn_out_norm, m_conv_w, m_w_out, m_xattn_norm, m_mem_norm, m_w_q_mem, m_w_kv_mem, m_w_o_mem, m_ffn2_norm, m_ffn2_gate, m_ffn2_up, m_ffn2_down, m_final_norm, v_ffn1_norm, v_ffn1_gate, v_ffn1_up, v_ffn1_down, v_mix_norm, v_w_in, v_lb_param, v_hgrn_out_norm, v_conv_w, v_w_out, v_xattn_norm, v_mem_norm, v_w_q_mem, v_w_kv_mem, v_w_o_mem, v_ffn2_norm, v_ffn2_gate, v_ffn2_up, v_ffn2_down, v_final_norm):
    given = dict(locals())
    me = 4 * lax.axis_index("x") + 2 * lax.axis_index("y") + lax.axis_index("c")
    x0, memv, target = x[0], mem[0], loss_target[0]

    def shard(prefix, name):
        v = given[prefix + name]
        if v.ndim == 1:
            return v.reshape(1, -1)
        if v.ndim == 2:
            return v
        return v[0].T if name in TRANSPOSED else v[0]

    w = {name: shard("", name) for name in WEIGHTS}
    m = {name: shard("m_", name) for name in WEIGHTS}
    v = {name: shard("v_", name) for name in WEIGHTS}

    conv_taps, conv_rows = w["conv_w"].shape
    conv_tile = jnp.pad(w["conv_w"], ((0, 8 - conv_taps), (0, 128 - conv_rows)))
    wire = {name: w[name].astype(BF16) for name in LARGE}
    full = {}

    def landed(names, gathered):
        for name, blocks in zip(names, gathered):
            _, r, c = blocks.shape
            full[name] = blocks if name == "w_kv_mem" else blocks.reshape(N_DEV * r, c)

    first = ("ffn1_gate", "ffn1_up")
    landed(first, _run_exchange(_gather_exchange([wire[k] for k in first]), "gather_first"))

    riders = (("ffn1_down", "w_in"), ("w_out", "w_kv_mem"), ("w_q_mem", "w_o_mem", "ffn2_gate", "ffn2_up"),
              ("ffn2_down",))
    (a1, b1, s1), gathered = _ffn_up(
        x0, w["ffn1_norm"], full["ffn1_gate"], full["ffn1_up"],
        exchange=_gather_exchange([wire[k] for k in riders[0]]))
    landed(riders[0], gathered)
    (x1,), gathered = _ffn_down(
        x0, s1, full["ffn1_down"], exchange=_gather_exchange([wire[k] for k in riders[1]] + [conv_tile]))
    landed(riders[1], gathered)
    convw_t = gathered[-1][:, :conv_taps, :conv_rows].transpose(1, 0, 2).reshape(conv_taps, N_DEV * conv_rows)
    (x2, z, o_raw, states, ycat), gathered = _mix_fwd(
        x1, w["mix_norm"], full["w_in"], w["lb_param"], w["hgrn_out_norm"], convw_t, full["w_out"],
        exchange=_gather_exchange([wire[k] for k in riders[2]]))
    landed(riders[2], gathered)
    kv = _memkv_fwd(memv, w["mem_norm"], full["w_kv_mem"])
    (x3, hq, qm, att), gathered = _xattn_fwd(
        x2, w["xattn_norm"], full["w_q_mem"], kv, full["w_o_mem"],
        exchange=_gather_exchange([wire[k] for k in riders[3]]))
    landed(riders[3], gathered)
    (dx4, a2, b2, s2, loss_part, d_final), _ = _ffn_fwd(
        x3, w["ffn2_norm"], full["ffn2_gate"], full["ffn2_up"], full["ffn2_down"], head=(w["final_norm"], target))

    parts = {}
    waiting = []

    def carried():
        names = [name for name, _ in waiting]
        exchange = _scatter_exchange([p for _, p in waiting]) if waiting else None
        del waiting[:]
        return names, exchange

    def weight_grad(name, a, b, scale=1.0):
        names, exchange = carried()
        partial, arrived = _weight_grad(a, b, scale, exchange=exchange)
        parts.update(zip(names, arrived))
        waiting.append((name, partial))

    (dx3, da2, db2, h4, d_ffn2_norm), _ = _ffn_bwd(
        x3, w["ffn2_norm"], dx4, a2, b2, full["ffn2_gate"], full["ffn2_up"], full["ffn2_down"])
    weight_grad("ffn2_down", s2, dx4, 0.5)
    weight_grad("ffn2_gate", da2, h4)
    weight_grad("ffn2_up", db2, h4)
    names, exchange = carried()
    (dx2, dqm, dkv, d_xattn_norm), arrived = _xattn_bwd(
        x2, w["xattn_norm"], dx3, qm, kv, full["w_q_mem"], full["w_o_mem"], exchange=exchange)
    parts.update(zip(names, arrived))
    d_wkv, d_mem_norm = _memkv_bwd(memv, w["mem_norm"], dkv, full["w_kv_mem"])
    waiting.append(("w_kv_mem", d_wkv))
    names, exchange = carried()
    (dx1, dz, h2, d_mix_norm, d_lbp, d_gh, d_convw_t), arrived = _mix_bwd(
        x1, w["mix_norm"], dx2, z, o_raw, states, full["w_in"], w["lb_param"], w["hgrn_out_norm"], convw_t,
        full["w_out"], exchange=exchange)
    parts.update(zip(names, arrived))
    weight_grad("w_in", dz, h2)
    weight_grad("ffn1_down", s1, dx1, 0.5)
    (dx0, da1, db1, h1, d_ffn1_norm), _ = _ffn_bwd(
        x0, w["ffn1_norm"], dx1, a1, b1, full["ffn1_gate"], full["ffn1_up"], full["ffn1_down"])
    weight_grad("ffn1_gate", da1, h1)
    weight_grad("ffn1_up", db1, h1)
    weight_grad("w_o_mem", att, dx3)
    weight_grad("w_q_mem", hq, dqm)
    weight_grad("w_out", ycat, dx2)

    small_parts = {
        "ffn1_norm": d_ffn1_norm, "mix_norm": d_mix_norm, "xattn_norm": d_xattn_norm, "mem_norm": d_mem_norm,
        "ffn2_norm": d_ffn2_norm, "final_norm": d_final, "lb_param": d_lbp, "hgrn_out_norm": d_gh,
        "conv_w": d_convw_t, "loss": loss_part,
    }
    names = [name for name, _ in waiting]
    arrived, total = _final_exchange([p for _, p in waiting], small_parts)
    parts.update(zip(names, arrived))

    g_out, d_out, m_out, v_out = {}, {}, {}, {}
    for name in LARGE:
        g_out[name], d_out[name], m_out[name], v_out[name] = _adamw_shard(parts[name], w[name], m[name], v[name])
    g_small = {}
    for name in SMALL:
        row, nrows, ncols = SMALL_LAYOUT[name]
        g_small[name] = total[row:row + nrows, 0:ncols]
    g_small["conv_w"] = lax.dynamic_slice_in_dim(g_small["conv_w"], me * conv_rows, conv_rows, axis=1)
    ds, ms, vs = _adamw_small(
        [g_small[k] for k in SMALL], [w[k] for k in SMALL], [m[k] for k in SMALL], [v[k] for k in SMALL])
    for i, name in enumerate(SMALL):
        g_out[name], d_out[name], m_out[name], v_out[name] = g_small[name], ds[i], ms[i], vs[i]

    def shaped(value, name):
        return (value.T if name in TRANSPOSED else value).reshape(given[name].shape)

    loss = total[SMALL_LAYOUT["loss"][0], 0]
    outs = [loss, dx0.reshape(x.shape)]
    for group in (g_out, d_out, m_out, v_out):
        outs += [shaped(group[name], name) for name in WEIGHTS]
    return tuple(outs)
```

```python
import jax
import jax.numpy as jnp
from jax import lax
from jax.experimental import pallas as pl
from jax.experimental.pallas import tpu as pltpu

F32 = jnp.float32
BF16 = jnp.bfloat16
MESH_IDS = pl.DeviceIdType.MESH

N_DEV = 8
EPS = 1e-6
HGRN_HEADS = 4
HGRN_DK = 128
HGRN_W = 512
CHUNK = 64
MEM_HEADS = 4
MEM_HD = 256
ADAM_LR = 0.001
ADAM_B1 = 0.9
ADAM_B2 = 0.999
ADAM_EPS = 1e-08
ADAM_WD = 0.01
ADAM_STEP = 10

TOKEN_TILE = 256
REDUCE_TILE = 1024
ADAMW_TILE_ELEMENTS = 256 * 1024
MIDDLE_EIGHTHS = 5
MXU_ROWS = 256
VMEM_LIMIT = 60 * 1024 * 1024
SMALL_ROWS = 16
NT = (((1,), (1,)), ((), ()))
TN = (((0,), (0,)), ((), ()))


def _params(sem=None):
    return pltpu.CompilerParams(dimension_semantics=sem, vmem_limit_bytes=VMEM_LIMIT)


def _dot(a, b, dims=None):
    if dims is None:
        return jnp.dot(a, b, preferred_element_type=F32)
    return lax.dot_general(a, b, dims, preferred_element_type=F32)


def _sigmoid(v):
    return 1.0 / (1.0 + jnp.exp(-v))


def _rms(x, g):
    r = lax.rsqrt(jnp.mean(x * x, axis=-1, keepdims=True) + EPS)
    xh = x * r
    return xh * g, xh, r


def _rms_bwd(dh, xh, r, g):
    dxh = dh * g
    return r * (dxh - xh * jnp.mean(dxh * xh, axis=-1, keepdims=True))


def _full(shape):
    return pl.BlockSpec(shape, lambda *_: (0,) * len(shape))


def _rows(tm, width):
    return pl.BlockSpec((tm, width), lambda i: (i, 0))


def _rows_rev(tm, width, n):
    return pl.BlockSpec((tm, width), lambda i: (n - 1 - i, 0))


def _zero_at_start(*refs):
    @pl.when(pl.program_id(0) == 0)
    def _():
        for ref in refs:
            ref[...] = jnp.zeros_like(ref)


class _Exchange:
    def __init__(self, operands, out_shapes, scratch, start, finish, middle=None):
        self.operands, self.out_shapes, self.scratch = list(operands), list(out_shapes), list(scratch)
        self.start, self.middle, self.finish = start, middle, finish


def _call(body, *, name, grid, in_specs, out_specs, out_shape, args, scratch_shapes=(), exchange=None):
    semantics = ("arbitrary",) * len(grid)
    if exchange is None:
        out = pl.pallas_call(
            body, name=name, grid=grid, in_specs=in_specs, out_specs=out_specs, out_shape=out_shape,
            scratch_shapes=list(scratch_shapes), compiler_params=_params(semantics))(*args)
        return out, []
    hbm = pl.BlockSpec(memory_space=pltpu.HBM)
    n_in, n_out, n_scr = len(in_specs), len(out_specs), len(scratch_shapes)
    e_in, e_out = len(exchange.operands), len(exchange.out_shapes)

    def carried(*refs):
        ins, rest = refs[:n_in], refs[n_in:]
        e_ins, rest = rest[:e_in], rest[e_in:]
        outs, rest = rest[:n_out], rest[n_out:]
        e_outs, rest = rest[:e_out], rest[e_out:]
        scr, e_scr = rest[:n_scr], rest[n_scr:]
        first = last = None
        for axis, size in enumerate(grid):
            at_start, at_end = pl.program_id(axis) == 0, pl.program_id(axis) == size - 1
            first = at_start if first is None else jnp.logical_and(first, at_start)
            last = at_end if last is None else jnp.logical_and(last, at_end)

        @pl.when(first)
        def _():
            exchange.start(e_ins, e_outs, e_scr)

        body(*ins, *outs, *scr)

        if exchange.middle is not None:
            assert len(grid) == 1

            @pl.when(pl.program_id(0) == (grid[0] * MIDDLE_EIGHTHS) // 8)
            def _():
                exchange.middle(e_ins, e_outs, e_scr)

        @pl.when(last)
        def _():
            exchange.finish(e_ins, e_outs, e_scr)

    out = pl.pallas_call(
        carried, name=name, grid=grid, in_specs=list(in_specs) + [hbm] * e_in,
        out_specs=list(out_specs) + [hbm] * e_out, out_shape=list(out_shape) + exchange.out_shapes,
        scratch_shapes=list(scratch_shapes) + exchange.scratch,
        compiler_params=pltpu.CompilerParams(
            dimension_semantics=semantics, vmem_limit_bytes=VMEM_LIMIT, has_side_effects=True),
    )(*args, *exchange.operands)
    return out[:n_out], out[n_out:]


def _run_exchange(exchange, name):
    hbm = pl.BlockSpec(memory_space=pltpu.HBM)
    e_in, e_out = len(exchange.operands), len(exchange.out_shapes)

    def body(*refs):
        e_ins, e_outs, e_scr = refs[:e_in], refs[e_in:e_in + e_out], refs[e_in + e_out:]
        exchange.start(e_ins, e_outs, e_scr)
        if exchange.middle is not None:
            exchange.middle(e_ins, e_outs, e_scr)
        exchange.finish(e_ins, e_outs, e_scr)

    return pl.pallas_call(
        body, name=name, in_specs=[hbm] * e_in, out_specs=[hbm] * e_out, out_shape=exchange.out_shapes,
        scratch_shapes=exchange.scratch, compiler_params=pltpu.CompilerParams(has_side_effects=True),
    )(*exchange.operands)


def _loss_head(xo, gf, tgt):
    d = xo.shape[1]
    y, xh, r = _rms(xo, gf)
    err = y - tgt
    dy = err * (1.0 / d)
    loss = 0.5 * jnp.sum(jnp.sum(err * err, axis=-1, keepdims=True) * (1.0 / d), axis=0, keepdims=True)
    return _rms_bwd(dy, xh, r, gf), loss, jnp.sum(dy * xh, axis=0, keepdims=True)


def _ffn_fwd(x, g, wg, wu, wd, exchange=None, head=None):
    t, d = x.shape
    f = wg.shape[0]
    tm = min(TOKEN_TILE, t)

    def body(x_ref, g_ref, wg_ref, wu_ref, wd_ref, *rest):
        if head is None:
            xo_ref, a_ref, b_ref, s_ref = rest
        else:
            gf_ref, tgt_ref, xo_ref, a_ref, b_ref, s_ref, loss_ref, dgf_ref = rest
            _zero_at_start(loss_ref, dgf_ref)
        xv = x_ref[...]
        h, _, _ = _rms(xv, g_ref[...])
        hb = h.astype(BF16)
        a = _dot(hb, wg_ref[...], NT)
        b = _dot(hb, wu_ref[...], NT)
        s = (a * _sigmoid(a) * b).astype(BF16)
        xo = xv + 0.5 * _dot(s, wd_ref[...])
        if head is None:
            xo_ref[...] = xo
        else:
            xo_ref[...], loss, dgf = _loss_head(xo, gf_ref[...], tgt_ref[...])
            loss_ref[...] += jnp.broadcast_to(loss, (1, 128))
            dgf_ref[...] += dgf
        a_ref[...] = a.astype(BF16)
        b_ref[...] = b.astype(BF16)
        s_ref[...] = s

    in_specs = [_rows(tm, d), _full((1, d)), _full((f, d)), _full((f, d)), _full((f, d))]
    out_specs = [_rows(tm, d), _rows(tm, f), _rows(tm, f), _rows(tm, f)]
    out_shape = [
        jax.ShapeDtypeStruct((t, d), F32),
        jax.ShapeDtypeStruct((t, f), BF16),
        jax.ShapeDtypeStruct((t, f), BF16),
        jax.ShapeDtypeStruct((t, f), BF16),
    ]
    args = (x, g, wg, wu, wd)
    if head is not None:
        in_specs += [_full((1, d)), _rows(tm, d)]
        out_specs += [_full((1, 128)), _full((1, d))]
        out_shape += [jax.ShapeDtypeStruct((1, 128), F32), jax.ShapeDtypeStruct((1, d), F32)]
        args += tuple(head)
    return _call(
        body, name="ffn_fwd", grid=(t // tm,), in_specs=in_specs, out_specs=out_specs, out_shape=out_shape,
        args=args, exchange=exchange)


def _ffn_up(x, g, wg, wu, exchange=None):
    t, d = x.shape
    f = wg.shape[0]
    tm = min(TOKEN_TILE, t)

    def body(x_ref, g_ref, wg_ref, wu_ref, a_ref, b_ref, s_ref):
        h, _, _ = _rms(x_ref[...], g_ref[...])
        hb = h.astype(BF16)
        a = _dot(hb, wg_ref[...], NT)
        b = _dot(hb, wu_ref[...], NT)
        a_ref[...] = a.astype(BF16)
        b_ref[...] = b.astype(BF16)
        s_ref[...] = (a * _sigmoid(a) * b).astype(BF16)

    return _call(
        body, name="ffn_up", grid=(t // tm,),
        in_specs=[_rows(tm, d), _full((1, d)), _full((f, d)), _full((f, d))],
        out_specs=[_rows(tm, f)] * 3, out_shape=[jax.ShapeDtypeStruct((t, f), BF16)] * 3,
        args=(x, g, wg, wu), exchange=exchange)


def _ffn_down(x, s, wd, exchange=None):
    t, d = x.shape
    f = wd.shape[0]
    tm = min(TOKEN_TILE, t)

    def body(x_ref, s_ref, wd_ref, xo_ref):
        xo_ref[...] = x_ref[...] + 0.5 * _dot(s_ref[...], wd_ref[...])

    return _call(
        body, name="ffn_down", grid=(t // tm,),
        in_specs=[_rows(tm, d), _rows(tm, f), _full((f, d))],
        out_specs=[_rows(tm, d)], out_shape=[jax.ShapeDtypeStruct((t, d), F32)],
        args=(x, s, wd), exchange=exchange)


def _ffn_bwd(x, g, dxo, a, b, wg, wu, wd, exchange=None):
    t, d = x.shape
    f = wg.shape[0]
    tm = min(TOKEN_TILE, t)

    def body(x_ref, g_ref, dxo_ref, a_ref, b_ref, wg_ref, wu_ref, wd_ref, dx_ref, da_ref, db_ref, h_ref, dg_ref):
        _zero_at_start(dg_ref)
        gv = g_ref[...]
        h, xh, r = _rms(x_ref[...], gv)
        dxo = dxo_ref[...]
        ds = _dot((0.5 * dxo).astype(BF16), wd_ref[...], NT)
        af = a_ref[...].astype(F32)
        bf = b_ref[...].astype(F32)
        sg = _sigmoid(af)
        da = (ds * bf * (sg * (1.0 + af * (1.0 - sg)))).astype(BF16)
        db = (ds * (af * sg)).astype(BF16)
        dh = _dot(da, wg_ref[...]) + _dot(db, wu_ref[...])
        dx_ref[...] = _rms_bwd(dh, xh, r, gv) + dxo
        da_ref[...] = da
        db_ref[...] = db
        h_ref[...] = h.astype(BF16)
        dg_ref[...] += jnp.sum(dh * xh, axis=0, keepdims=True)

    return _call(
        body,
        name="ffn_bwd",
        grid=(t // tm,),
        in_specs=[
            _rows(tm, d), _full((1, d)), _rows(tm, d), _rows(tm, f), _rows(tm, f),
            _full((f, d)), _full((f, d)), _full((f, d)),
        ],
        out_specs=[_rows(tm, d), _rows(tm, f), _rows(tm, f), _rows(tm, d), _full((1, d))],
        out_shape=[
            jax.ShapeDtypeStruct((t, d), F32),
            jax.ShapeDtypeStruct((t, f), BF16),
            jax.ShapeDtypeStruct((t, f), BF16),
            jax.ShapeDtypeStruct((t, d), BF16),
            jax.ShapeDtypeStruct((1, d), F32),
        ],
        args=(x, g, dxo, a, b, wg, wu, wd),
        exchange=exchange,
    )


def _weight_grad(lhs, b, scale=1.0, exchange=None):
    count = len(lhs)
    t, m = lhs[0].shape
    n = b.shape[1]
    chips = N_DEV // 2
    r = m // N_DEV
    tk = min(REDUCE_TILE // count, t)
    halves = 2
    nb = n // halves
    nk = t // tk

    def body(*refs):
        a_refs, b_ref, o_refs = refs[:count], refs[count], refs[count + 1:2 * count + 1]
        acc, send_buf, recv_buf, send_sems, recv_sems, out_sems = refs[2 * count + 1:]
        k, j = pl.program_id(0), pl.program_id(1)
        x, y, c, _ = _mesh_place()
        sibling, _ = _peer(x, y, c, 1)
        bv = b_ref[...]
        if scale != 1.0:
            bv = bv * scale
        bb = bv.astype(BF16)

        @pl.when(k == 0)
        def _():
            for ia in range(count):
                acc[ia, j] = jnp.zeros((m, nb), F32)

        for ia in range(count):
            acc_half = acc.at[ia, j]
            for i in range(m // MXU_ROWS):
                rows = slice(i * MXU_ROWS, (i + 1) * MXU_ROWS)
                acc_half[rows, :] += _dot(a_refs[ia][:, rows].astype(BF16), bb, TN)

        def to_sibling(ia, half):
            return _remote(send_buf.at[ia, half], recv_buf.at[ia, half], send_sems.at[ia, half],
                           recv_sems.at[ia, half], sibling)

        def copy_out(ia, half):
            return pltpu.make_async_copy(
                recv_buf.at[ia, half], o_refs[ia].at[:, :, pl.ds(half * nb, nb)], out_sems.at[ia, half])

        def owned_rows(q, core):
            return pl.ds(pl.multiple_of((2 * q + core) * r, 8), r)

        for half in range(halves):
            @pl.when(jnp.logical_and(k == nk - 1, j == half))
            def _():
                for ia in range(count):
                    for q in range(chips):
                        send_buf[ia, half, q] = acc[ia, half, owned_rows(q, 1 - c), :].astype(BF16)
                    to_sibling(ia, half).start()

        @pl.when(jnp.logical_and(k == nk - 1, j == halves - 1))
        def _():
            for ia in range(count):
                for half in range(halves):
                    to_sibling(ia, half).wait_send()
                    to_sibling(ia, half).wait_recv()
                    for q in range(chips):
                        recv_buf[ia, half, q] = (
                            acc[ia, half, owned_rows(q, c), :] + recv_buf[ia, half, q].astype(F32)).astype(BF16)
                    copy_out(ia, half).start()
            for ia in range(count):
                for half in range(halves):
                    copy_out(ia, half).wait()

    partials, arrived = _call(
        body,
        name="weight_grad",
        grid=(nk, halves),
        in_specs=[pl.BlockSpec((tk, m), lambda k, j: (k, 0))] * count + [pl.BlockSpec((tk, nb), lambda k, j: (k, j))],
        out_specs=[pl.BlockSpec(memory_space=pltpu.HBM)] * count,
        out_shape=[jax.ShapeDtypeStruct((chips, r, n), BF16)] * count,
        scratch_shapes=[
            pltpu.VMEM((count, halves, m, nb), F32),
            pltpu.VMEM((count, halves, chips, r, nb), BF16), pltpu.VMEM((count, halves, chips, r, nb), BF16),
            pltpu.SemaphoreType.DMA((count, halves)), pltpu.SemaphoreType.DMA((count, halves)),
            pltpu.SemaphoreType.DMA((count, halves)),
        ],
        args=(*lhs, b),
        exchange=exchange,
    )
    return partials, arrived


def _chunk_cumsum(v, reverse=False):
    n, width = v.shape
    row = lax.broadcasted_iota(jnp.int32, (n, n), 0)
    col = lax.broadcasted_iota(jnp.int32, (n, n), 1)
    earlier = col >= row if reverse else col <= row
    tri = jnp.where(jnp.logical_and(row // CHUNK == col // CHUNK, earlier), 1.0, 0.0).astype(BF16)
    hi = v.astype(BF16)
    rest = v - hi.astype(F32)
    mid = rest.astype(BF16)
    low = (rest - mid.astype(F32)).astype(BF16)
    sums = _dot(tri, jnp.concatenate([hi, mid, low], axis=1))
    return sums[:, 0:width] + sums[:, width:2 * width] + sums[:, 2 * width:3 * width]


def _shift_rows(v, shift, edge):
    n = v.shape[0]
    row = lax.broadcasted_iota(jnp.int32, (n, 1), 0)
    out = pltpu.roll(v, shift % n, axis=0)
    if shift > 0:
        for j in range(shift):
            out = jnp.where(row == j, edge[8 - shift + j:8 - shift + j + 1, :], out)
    else:
        for j in range(-shift):
            out = jnp.where(row == n + shift + j, edge[j:j + 1, :], out)
    return out


def _gates(z, lbp):
    w = HGRN_W
    lb = _sigmoid(lbp[0:1, :] - lbp[1:2, :])
    zq = z[:, 0:w]
    sig = _sigmoid(z[:, w:2 * w])
    f = lb + (1.0 - lb) * sig
    sq = _sigmoid(zq)
    q = zq * sq * HGRN_DK ** -0.5
    return lb, sig, f, sq, q


def _decayed_operands(q, f, v, qh_buf, kh_buf, kbar_buf, v_buf, etot_buf):
    n, width = f.shape
    bcum = _chunk_cumsum(jnp.log(f))
    total = jnp.concatenate(
        [jnp.broadcast_to(bcum[c + CHUNK - 1:c + CHUNK, :], (CHUNK, width)) for c in range(0, n, CHUNK)], axis=0)
    eb, enb, erest = jnp.exp(bcum), jnp.exp(-bcum), jnp.exp(total - bcum)
    kk = 1.0 - f
    qh_buf[...] = (q * eb).astype(BF16)
    kh_buf[...] = (kk * enb).astype(BF16)
    kbar_buf[...] = (kk * erest).astype(BF16)
    v_buf[...] = v.astype(BF16)
    etot_buf[...] = jnp.exp(total)
    return eb, enb, erest


def _short_conv(u, edge, cw):
    return cw[0:1, :] * _shift_rows(u, 2, edge) + cw[1:2, :] * _shift_rows(u, 1, edge) + cw[2:3, :] * u


def _block_causal_mask(n):
    row = lax.broadcasted_iota(jnp.int32, (n, n), 0)
    col = lax.broadcasted_iota(jnp.int32, (n, n), 1)
    return jnp.logical_and(row // CHUNK == col // CHUNK, col <= row)


def _spread(v, chunk_of_row, nc):
    return jnp.concatenate([jnp.where(chunk_of_row == c, v, jnp.zeros_like(v)) for c in range(nc)], axis=1)


def _pick(r, chunk_of_row, nc):
    out = jnp.where(chunk_of_row == 0, r[:, 0:HGRN_DK], 0.0)
    for c in range(1, nc):
        out = out + jnp.where(chunk_of_row == c, r[:, c * HGRN_DK:(c + 1) * HGRN_DK], 0.0)
    return out


def _mix_fwd(x, g, w_in, lbp, gh, convw_t, w_out, exchange=None):
    t, d = x.shape
    zw = w_in.shape[0]
    w = HGRN_W
    tm = min(TOKEN_TILE, t)
    nc = tm // CHUNK
    n_chunks = t // CHUNK

    def body(x_ref, g_ref, win_ref, lbp_ref, gh_ref, cw_ref, wout_ref,
             xo_ref, z_ref, o_ref, st_ref, y_ref, state, ucarry, qh_buf, kh_buf, kbar_buf, v_buf, etot_buf):
        _zero_at_start(state, ucarry)
        xv = x_ref[...]
        h, _, _ = _rms(xv, g_ref[...])
        z_ref[...] = _dot(h.astype(BF16), win_ref[...], NT)
        z = z_ref[...]
        _, _, f, _, q = _gates(z, lbp_ref[...])
        _decayed_operands(q, f, z[:, 2 * w:3 * w], qh_buf, kh_buf, kbar_buf, v_buf, etot_buf)
        mask = _block_causal_mask(tm)
        chunk_of_row = lax.broadcasted_iota(jnp.int32, (tm, 1), 0) // CHUNK
        heads = range(HGRN_HEADS)
        hcols = [slice(hd * HGRN_DK, (hd + 1) * HGRN_DK) for hd in heads]
        qh = [qh_buf[:, hcols[hd]] for hd in heads]
        vb = [v_buf[:, hcols[hd]] for hd in heads]
        scores = [jnp.where(mask, _dot(qh[hd], kh_buf[:, hcols[hd]], NT), 0.0).astype(BF16) for hd in heads]
        gains = [_dot(_spread(vb[hd], chunk_of_row, nc), kbar_buf[:, hcols[hd]], TN) for hd in heads]
        entering = []
        for hd in heads:
            states, st = [], state[hd]
            for c in range(nc):
                states.append(st)
                st_ref[c, hd] = st
                st = st * etot_buf[c * CHUNK:c * CHUNK + 1, hcols[hd]] + gains[hd][c * HGRN_DK:(c + 1) * HGRN_DK, :]
            state[hd] = st
            entering.append(jnp.concatenate(states, axis=0).astype(BF16))
        from_states = [_dot(qh[hd], entering[hd], NT) for hd in heads]
        o_heads = [_dot(scores[hd], vb[hd]) + _pick(from_states[hd], chunk_of_row, nc) for hd in heads]
        o_ref[...] = jnp.concatenate(o_heads, axis=1)
        ghv = gh_ref[...]
        normed = jnp.concatenate([_rms(o_heads[hd], ghv[:, hcols[hd]])[0] for hd in heads], axis=1)
        zg = z[:, 3 * w:4 * w]
        u = z[:, 5 * w:6 * w] * z[:, 6 * w:7 * w]
        conv = _short_conv(u, ucarry[...], cw_ref[...])
        ucarry[...] = u[tm - 8:tm, :]
        y = jnp.concatenate([normed * (zg * _sigmoid(zg)), z[:, 4 * w:5 * w] * conv], axis=1).astype(BF16)
        y_ref[...] = y
        xo_ref[...] = xv + _dot(y, wout_ref[...])

    return _call(
        body,
        name="mix_fwd",
        grid=(t // tm,),
        in_specs=[
            _rows(tm, d), _full((1, d)), _full((zw, d)), _full((2, w)), _full((1, w)), _full((3, w)),
            _full((2 * w, d)),
        ],
        out_specs=[
            _rows(tm, d), _rows(tm, zw), _rows(tm, w),
            pl.BlockSpec((nc, HGRN_HEADS, HGRN_DK, HGRN_DK), lambda i: (i, 0, 0, 0)),
            _rows(tm, 2 * w),
        ],
        out_shape=[
            jax.ShapeDtypeStruct((t, d), F32),
            jax.ShapeDtypeStruct((t, zw), F32),
            jax.ShapeDtypeStruct((t, w), F32),
            jax.ShapeDtypeStruct((n_chunks, HGRN_HEADS, HGRN_DK, HGRN_DK), F32),
            jax.ShapeDtypeStruct((t, 2 * w), BF16),
        ],
        scratch_shapes=[
            pltpu.VMEM((HGRN_HEADS, HGRN_DK, HGRN_DK), F32), pltpu.VMEM((8, w), F32),
            pltpu.VMEM((tm, w), BF16), pltpu.VMEM((tm, w), BF16), pltpu.VMEM((tm, w), BF16),
            pltpu.VMEM((tm, w), BF16), pltpu.VMEM((tm, w), F32),
        ],
        args=(x, g, w_in, lbp, gh, convw_t, w_out),
        exchange=exchange,
    )


def _mix_bwd(x, g, dxo, z, o, states, w_in, lbp, gh, convw_t, w_out, exchange=None):
    t, d = x.shape
    zw = w_in.shape[0]
    w = HGRN_W
    tm = min(TOKEN_TILE, t)
    nc = tm // CHUNK
    n = t // tm

    def body(x_ref, g_ref, dxo_ref, z_ref, zprev_ref, o_ref, st_ref, win_ref, lbp_ref, gh_ref, cw_ref, wout_ref,
             dx_ref, dz_ref, h_ref, dg_ref, dlbp_ref, dgh_ref, dcw_ref,
             dstate, dcarry, do_buf, qh_buf, kh_buf, kbar_buf, v_buf, etot_buf):
        _zero_at_start(dstate, dcarry, dg_ref, dlbp_ref, dgh_ref, dcw_ref)
        gv = g_ref[...]
        h, xh, r = _rms(x_ref[...], gv)
        h_ref[...] = h.astype(BF16)
        dxo = dxo_ref[...]
        dy = _dot(dxo.astype(BF16), wout_ref[...], NT)
        z = z_ref[...]
        lb, sig, f, sq, q = _gates(z, lbp_ref[...])
        eb, enb, erest = _decayed_operands(q, f, z[:, 2 * w:3 * w], qh_buf, kh_buf, kbar_buf, v_buf, etot_buf)

        ghv = gh_ref[...]
        zg = z[:, 3 * w:4 * w]
        sgz = _sigmoid(zg)
        dyh = dy[:, 0:w]
        don = dyh * (zg * sgz)
        heads = range(HGRN_HEADS)
        hcols = [slice(hd * HGRN_DK, (hd + 1) * HGRN_DK) for hd in heads]
        norms = [_rms(o_ref[:, hcols[hd]], ghv[:, hcols[hd]]) for hd in heads]
        on = jnp.concatenate([norms[hd][0] for hd in heads], axis=1)
        oh = jnp.concatenate([norms[hd][1] for hd in heads], axis=1)
        dz_ref[:, 3 * w:4 * w] = (dyh * on * (sgz * (1.0 + zg * (1.0 - sgz)))).astype(BF16)
        dgh_ref[...] += jnp.sum(don * oh, axis=0, keepdims=True)
        do_buf[...] = jnp.concatenate(
            [_rms_bwd(don[:, hcols[hd]], norms[hd][1], norms[hd][2], ghv[:, hcols[hd]]) for hd in heads],
            axis=1).astype(BF16)

        zb = z[:, 4 * w:5 * w]
        zc = z[:, 5 * w:6 * w]
        zu = z[:, 6 * w:7 * w]
        u = zc * zu
        cw = cw_ref[...]
        zp = zprev_ref[...]
        uprev = jnp.where(pl.program_id(0) == n - 1, 0.0, zp[:, 5 * w:6 * w] * zp[:, 6 * w:7 * w])
        dyc = dy[:, w:2 * w]
        dz_ref[:, 4 * w:5 * w] = (dyc * _short_conv(u, uprev, cw)).astype(BF16)
        dconv = dyc * zb
        edge = dcarry[...]
        dconv1 = _shift_rows(dconv, -1, edge)
        dconv2 = _shift_rows(dconv, -2, edge)
        dcarry[...] = dconv[0:8, :]
        du = cw[2:3, :] * dconv + cw[1:2, :] * dconv1 + cw[0:1, :] * dconv2
        dz_ref[:, 5 * w:6 * w] = (du * zu).astype(BF16)
        dz_ref[:, 6 * w:7 * w] = (du * zc).astype(BF16)
        dcw_ref[...] += jnp.concatenate([
            jnp.sum(u * dconv2, axis=0, keepdims=True),
            jnp.sum(u * dconv1, axis=0, keepdims=True),
            jnp.sum(u * dconv, axis=0, keepdims=True)], axis=0)

        mask = _block_causal_mask(tm)
        chunk_of_row = lax.broadcasted_iota(jnp.int32, (tm, 1), 0) // CHUNK
        heads = range(HGRN_HEADS)
        hcols = [slice(hd * HGRN_DK, (hd + 1) * HGRN_DK) for hd in heads]
        qhb = [qh_buf[:, hcols[hd]] for hd in heads]
        khb = [kh_buf[:, hcols[hd]] for hd in heads]
        vb = [v_buf[:, hcols[hd]] for hd in heads]
        dob = [do_buf[:, hcols[hd]] for hd in heads]
        scores = [jnp.where(mask, _dot(qhb[hd], khb[hd], NT), 0.0).astype(BF16) for hd in heads]
        dscores = [jnp.where(mask, _dot(dob[hd], vb[hd], NT), 0.0).astype(BF16) for hd in heads]
        gains = [_dot(_spread(dob[hd], chunk_of_row, nc), qhb[hd], TN) for hd in heads]
        dst_rows, dst_lanes, st_lanes, carries = [], [], [], []
        for hd in heads:
            entering = [st_ref[c, hd] for c in range(nc)]
            leaving, carried_back = [None] * nc, [None] * nc
            dst = dstate[hd]
            for c in reversed(range(nc)):
                elast = etot_buf[c * CHUNK:c * CHUNK + 1, hcols[hd]]
                leaving[c] = dst
                carried_back[c] = jnp.sum(dst * entering[c], axis=0, keepdims=True) * elast
                dst = dst * elast + gains[hd][c * HGRN_DK:(c + 1) * HGRN_DK, :]
            dstate[hd] = dst
            dst_rows.append(jnp.concatenate(leaving, axis=0).astype(BF16))
            dst_lanes.append(jnp.concatenate(leaving, axis=1).astype(BF16))
            st_lanes.append(jnp.concatenate(entering, axis=1).astype(BF16))
            carries.append(carried_back)
        dv = [_dot(scores[hd], dob[hd], TN) + _pick(_dot(kbar_buf[:, hcols[hd]], dst_rows[hd], NT), chunk_of_row, nc)
              for hd in heads]
        dz_ref[:, 2 * w:3 * w] = jnp.concatenate(dv, axis=1).astype(BF16)
        dqh = jnp.concatenate(
            [_dot(dscores[hd], khb[hd]) + _pick(_dot(dob[hd], st_lanes[hd]), chunk_of_row, nc) for hd in heads], axis=1)
        dkh = jnp.concatenate([_dot(dscores[hd], qhb[hd], TN) for hd in heads], axis=1)
        dkbar = jnp.concatenate([_pick(_dot(vb[hd], dst_lanes[hd]), chunk_of_row, nc) for hd in heads], axis=1)

        kbar_dkbar = kbar_buf[...].astype(F32) * dkbar
        db = qh_buf[...].astype(F32) * dqh - kh_buf[...].astype(F32) * dkh - kbar_dkbar
        through_last = jnp.concatenate([
            jnp.broadcast_to(
                jnp.sum(kbar_dkbar[c * CHUNK:(c + 1) * CHUNK], axis=0, keepdims=True)
                + jnp.concatenate([carries[hd][c] for hd in heads], axis=1),
                (CHUNK, w))
            for c in range(nc)], axis=0)
        dlogf = _chunk_cumsum(db, reverse=True) + through_last
        df = dlogf / f - (dkh * enb + dkbar * erest)
        zq = z[:, 0:w]
        dz_ref[:, 0:w] = (dqh * eb * HGRN_DK ** -0.5 * (sq * (1.0 + zq * (1.0 - sq)))).astype(BF16)
        dz_ref[:, w:2 * w] = (df * (1.0 - lb) * sig * (1.0 - sig)).astype(BF16)
        dlb = jnp.sum(df * (1.0 - sig), axis=0, keepdims=True) * lb * (1.0 - lb)
        dlbp_ref[...] += jnp.concatenate([dlb, -dlb], axis=0)

        dh = _dot(dz_ref[...], win_ref[...])
        dx_ref[...] = _rms_bwd(dh, xh, r, gv) + dxo
        dg_ref[...] += jnp.sum(dh * xh, axis=0, keepdims=True)

    return _call(
        body,
        name="mix_bwd",
        grid=(n,),
        in_specs=[
            _rows_rev(tm, d, n), _full((1, d)), _rows_rev(tm, d, n), _rows_rev(tm, zw, n),
            pl.BlockSpec((8, zw), lambda i: (jnp.maximum((n - 1 - i) * (tm // 8) - 1, 0), 0)),
            _rows_rev(tm, w, n),
            pl.BlockSpec((nc, HGRN_HEADS, HGRN_DK, HGRN_DK), lambda i: (n - 1 - i, 0, 0, 0)),
            _full((zw, d)), _full((2, w)), _full((1, w)), _full((3, w)), _full((2 * w, d)),
        ],
        out_specs=[
            _rows_rev(tm, d, n), _rows_rev(tm, zw, n), _rows_rev(tm, d, n),
            _full((1, d)), _full((2, w)), _full((1, w)), _full((3, w)),
        ],
        out_shape=[
            jax.ShapeDtypeStruct((t, d), F32),
            jax.ShapeDtypeStruct((t, zw), BF16),
            jax.ShapeDtypeStruct((t, d), BF16),
            jax.ShapeDtypeStruct((1, d), F32),
            jax.ShapeDtypeStruct((2, w), F32),
            jax.ShapeDtypeStruct((1, w), F32),
            jax.ShapeDtypeStruct((3, w), F32),
        ],
        scratch_shapes=[
            pltpu.VMEM((HGRN_HEADS, HGRN_DK, HGRN_DK), F32), pltpu.VMEM((8, w), F32),
            pltpu.VMEM((tm, w), BF16),
            pltpu.VMEM((tm, w), BF16), pltpu.VMEM((tm, w), BF16), pltpu.VMEM((tm, w), BF16),
            pltpu.VMEM((tm, w), BF16), pltpu.VMEM((tm, w), F32),
        ],
        args=(x, g, dxo, z, z, o, states, w_in, lbp, gh, convw_t, w_out),
        exchange=exchange,
    )


def _memkv_fwd(mem, g, wkv):
    m, d = mem.shape
    nb, _, cb = wkv.shape

    def body(mem_ref, g_ref, wkv_ref, kv_ref):
        mn, _, _ = _rms(mem_ref[...], g_ref[...])
        mnb = mn.astype(BF16)
        for j in range(nb):
            kv_ref[:, j * cb:(j + 1) * cb] = _dot(mnb, wkv_ref[j]).astype(BF16)

    return pl.pallas_call(
        body,
        name="memkv_fwd",
        out_shape=jax.ShapeDtypeStruct((m, nb * cb), BF16),
        compiler_params=_params(),
    )(mem, g, wkv)


def _memkv_bwd(mem, g, dkv, wkv):
    m, d = mem.shape
    nb, _, cb = wkv.shape
    chips = nb // 2

    def body(mem_ref, g_ref, dkv_ref, wkv_ref, dw_ref, dg_ref, dw_all, send_buf, recv_buf, send_sem, recv_sem):
        x, y, c, _ = _mesh_place()
        sibling, _ = _peer(x, y, c, 1)
        mn, xh, _ = _rms(mem_ref[...], g_ref[...])
        mnb = mn.astype(BF16)
        dmn = jnp.zeros((m, d), F32)
        for j in range(nb):
            dkvb = dkv_ref[:, j * cb:(j + 1) * cb].astype(BF16)
            dw_all[j] = _dot(mnb, dkvb, TN)
            dmn = dmn + _dot(dkvb, wkv_ref[j], NT)
        dg_ref[...] = jnp.sum(dmn * xh, axis=0, keepdims=True)
        for q in range(chips):
            send_buf[q] = dw_all[2 * q + 1 - c].astype(BF16)
        to_sibling = _remote(send_buf, recv_buf, send_sem, recv_sem, sibling)
        to_sibling.start()
        to_sibling.wait_send()
        to_sibling.wait_recv()
        for q in range(chips):
            dw_ref[q] = (dw_all[2 * q + c] + recv_buf[q].astype(F32)).astype(BF16)

    return pl.pallas_call(
        body,
        name="memkv_bwd",
        out_shape=[jax.ShapeDtypeStruct((chips, d, cb), BF16), jax.ShapeDtypeStruct((1, d), F32)],
        scratch_shapes=[
            pltpu.VMEM((nb, d, cb), F32), pltpu.VMEM((chips, d, cb), BF16), pltpu.VMEM((chips, d, cb), BF16),
            pltpu.SemaphoreType.DMA, pltpu.SemaphoreType.DMA,
        ],
        compiler_params=_params(),
    )(mem, g, dkv, wkv)


def _softmax_rows(qm_h, k_h):
    sc = _dot(qm_h, k_h, NT) * MEM_HD ** -0.5
    e = jnp.exp(sc - jnp.max(sc, axis=-1, keepdims=True))
    return e / jnp.sum(e, axis=-1, keepdims=True)


def _xattn_fwd(x, g, wq, kv, wo, exchange=None):
    t, d = x.shape
    m = kv.shape[0]
    tm = min(TOKEN_TILE, t)

    def body(x_ref, g_ref, wq_ref, kv_ref, wo_ref, xo_ref, hq_ref, qm_ref, att_ref):
        xv = x_ref[...]
        h, _, _ = _rms(xv, g_ref[...])
        hb = h.astype(BF16)
        hq_ref[...] = hb
        qm = _dot(hb, wq_ref[...]).astype(BF16)
        qm_ref[...] = qm
        heads = range(MEM_HEADS)
        kcols = [slice(hd * MEM_HD, (hd + 1) * MEM_HD) for hd in heads]
        p = [_softmax_rows(qm[:, kcols[hd]], kv_ref[:, kcols[hd]]) for hd in heads]
        att = jnp.concatenate(
            [_dot(p[hd].astype(BF16), kv_ref[:, d + hd * MEM_HD:d + (hd + 1) * MEM_HD]) for hd in heads],
            axis=1).astype(BF16)
        att_ref[...] = att
        xo_ref[...] = xv + _dot(att, wo_ref[...])

    return _call(
        body,
        name="xattn_fwd",
        grid=(t // tm,),
        in_specs=[_rows(tm, d), _full((1, d)), _full((d, d)), _full((m, 2 * d)), _full((d, d))],
        out_specs=[_rows(tm, d), _rows(tm, d), _rows(tm, d), _rows(tm, d)],
        out_shape=[
            jax.ShapeDtypeStruct((t, d), F32),
            jax.ShapeDtypeStruct((t, d), BF16),
            jax.ShapeDtypeStruct((t, d), BF16),
            jax.ShapeDtypeStruct((t, d), BF16),
        ],
        args=(x, g, wq, kv, wo),
        exchange=exchange,
    )


def _xattn_bwd(x, g, dxo, qm, kv, wq, wo, exchange=None):
    t, d = x.shape
    m = kv.shape[0]
    tm = min(TOKEN_TILE, t)

    def body(x_ref, g_ref, dxo_ref, qm_ref, kv_ref, wq_ref, wo_ref, dx_ref, dqm_ref, dkv_ref, dg_ref):
        _zero_at_start(dkv_ref, dg_ref)
        gv = g_ref[...]
        _, xh, r = _rms(x_ref[...], gv)
        dxo = dxo_ref[...]
        datt = _dot(dxo.astype(BF16), wo_ref[...], NT).astype(BF16)
        heads = range(MEM_HEADS)
        kcols = [slice(hd * MEM_HD, (hd + 1) * MEM_HD) for hd in heads]
        vcols = [slice(d + hd * MEM_HD, d + (hd + 1) * MEM_HD) for hd in heads]
        qm_h = [qm_ref[:, kcols[hd]] for hd in heads]
        p = [_softmax_rows(qm_h[hd], kv_ref[:, kcols[hd]]) for hd in heads]
        dp = [_dot(datt[:, kcols[hd]], kv_ref[:, vcols[hd]], NT) for hd in heads]
        dsc = [(p[hd] * (dp[hd] - jnp.sum(p[hd] * dp[hd], axis=-1, keepdims=True)) * MEM_HD ** -0.5).astype(BF16)
               for hd in heads]
        dqm = jnp.concatenate([_dot(dsc[hd], kv_ref[:, kcols[hd]]) for hd in heads], axis=1).astype(BF16)
        dqm_ref[...] = dqm
        dkv_ref[...] += jnp.concatenate(
            [_dot(dsc[hd], qm_h[hd], TN) for hd in heads]
            + [_dot(p[hd].astype(BF16), datt[:, kcols[hd]], TN) for hd in heads], axis=1)
        dh = _dot(dqm, wq_ref[...], NT)
        dx_ref[...] = _rms_bwd(dh, xh, r, gv) + dxo
        dg_ref[...] += jnp.sum(dh * xh, axis=0, keepdims=True)

    return _call(
        body,
        name="xattn_bwd",
        grid=(t // tm,),
        in_specs=[
            _rows(tm, d), _full((1, d)), _rows(tm, d), _rows(tm, d), _full((m, 2 * d)), _full((d, d)), _full((d, d)),
        ],
        out_specs=[_rows(tm, d), _rows(tm, d), _full((m, 2 * d)), _full((1, d))],
        out_shape=[
            jax.ShapeDtypeStruct((t, d), F32),
            jax.ShapeDtypeStruct((t, d), BF16),
            jax.ShapeDtypeStruct((m, 2 * d), F32),
            jax.ShapeDtypeStruct((1, d), F32),
        ],
        args=(x, g, dxo, qm, kv, wq, wo),
        exchange=exchange,
    )


def _mesh_place():
    x, y, c = lax.axis_index("x"), lax.axis_index("y"), lax.axis_index("c")
    return x, y, c, 4 * x + 2 * y + c


def _peer(x, y, c, k):
    px = 1 - x if k & 4 else x
    py = 1 - y if k & 2 else y
    pc = 1 - c if k & 1 else c
    return (px, py, pc), 4 * px + 2 * py + pc


ICI_HOPS = (2, 4, 6)
N_HOPS = len(ICI_HOPS)


def _remote(src, dst, send_sem, recv_sem, peer):
    return pltpu.make_async_remote_copy(
        src_ref=src, dst_ref=dst, send_sem=send_sem, recv_sem=recv_sem, device_id=peer, device_id_type=MESH_IDS)


def _gather_exchange(shards):
    n = len(shards)

    def place():
        x, y, c, me = _mesh_place()
        sibling, _ = _peer(x, y, c, 1)
        to_x, from_x = _peer(x, y, c, 4)
        to_y, from_y = _peer(x, y, c, 2)
        _, from_diagonal = _peer(x, y, c, 6)
        onward = (c * to_y[0] + (1 - c) * to_x[0], c * to_y[1] + (1 - c) * to_x[1], c)
        passed_on = c * from_x + (1 - c) * from_y
        return me, sibling, (to_x, to_y, onward), (from_x, from_y, from_diagonal), passed_on

    def start(src, dst, sems):
        ici_send, ici_recv, pair_send, pair_recv, local = sems
        me, sibling, targets, _, _ = place()
        for a in range(n):
            pltpu.make_async_copy(src[a], dst[a].at[me], local.at[a]).start()
            for j in range(2):
                _remote(src[a], dst[a].at[me], ici_send.at[a, j], ici_recv.at[a, j], targets[j]).start()
            _remote(src[a], dst[a].at[me], pair_send.at[a, 0], pair_recv.at[a, 0], sibling).start()

    def to_sibling(dst, sems, a, j, origin, sibling):
        _, _, pair_send, pair_recv, _ = sems
        slot = dst[a].at[origin]
        return _remote(slot, slot, pair_send.at[a, 1 + j], pair_recv.at[a, 1 + j], sibling)

    def middle(src, dst, sems):
        ici_send, ici_recv, _, _, _ = sems
        _, sibling, targets, origins, passed_on = place()
        for a in range(n):
            for j in range(2):
                _remote(src[a], dst[a].at[origins[j]], ici_send.at[a, j], ici_recv.at[a, j], targets[j]).wait_recv()
            slot = dst[a].at[passed_on]
            _remote(slot, slot, ici_send.at[a, 2], ici_recv.at[a, 2], targets[2]).start()
            for j in range(2):
                to_sibling(dst, sems, a, j, origins[j], sibling).start()

    def finish(src, dst, sems):
        ici_send, ici_recv, pair_send, pair_recv, local = sems
        me, sibling, targets, origins, _ = place()
        for a in range(n):
            _remote(src[a], dst[a].at[origins[2]], ici_send.at[a, 2], ici_recv.at[a, 2], targets[2]).wait_recv()
            to_sibling(dst, sems, a, 2, origins[2], sibling).start()
        for a in range(n):
            pltpu.make_async_copy(src[a], dst[a].at[me], local.at[a]).wait()
            for j in range(N_HOPS):
                _remote(src[a], dst[a].at[me], ici_send.at[a, j], ici_recv.at[a, j], targets[j]).wait_send()
            for j, origin in enumerate((me,) + origins):
                from_sibling = origin + 1 - 2 * (origin % 2)
                passed = _remote(src[a], dst[a].at[from_sibling], pair_send.at[a, j], pair_recv.at[a, j], sibling)
                passed.wait_send()
                passed.wait_recv()

    return _Exchange(
        shards,
        [jax.ShapeDtypeStruct((N_DEV,) + s.shape, s.dtype) for s in shards],
        [
            pltpu.SemaphoreType.DMA((n, N_HOPS)), pltpu.SemaphoreType.DMA((n, N_HOPS)),
            pltpu.SemaphoreType.DMA((n, N_HOPS + 1)), pltpu.SemaphoreType.DMA((n, N_HOPS + 1)),
            pltpu.SemaphoreType.DMA((n,)),
        ],
        start, finish, middle)


def _scatter_copies(src, dst, sems, n, arrivals=False):
    send, recv, local = sems
    x, y, c, _ = _mesh_place()
    chip = 2 * x + y
    if arrivals is None:
        return [pltpu.make_async_copy(src[a].at[chip], dst[a].at[chip], local.at[a]) for a in range(n)]
    copies = []
    for a in range(n):
        for j, k in enumerate(ICI_HOPS):
            peer, _ = _peer(x, y, c, k)
            peer_chip = 2 * peer[0] + peer[1]
            slot = dst[a].at[peer_chip if arrivals else chip]
            copies.append(_remote(src[a].at[peer_chip], slot, send.at[a, j], recv.at[a, j], peer))
    return copies


def _scatter_start(src, dst, sems, n):
    for cp in _scatter_copies(src, dst, sems, n, arrivals=None) + _scatter_copies(src, dst, sems, n):
        cp.start()


def _scatter_finish(src, dst, sems, n):
    for cp in _scatter_copies(src, dst, sems, n, arrivals=None):
        cp.wait()
    for cp in _scatter_copies(src, dst, sems, n):
        cp.wait_send()
    for cp in _scatter_copies(src, dst, sems, n, arrivals=True):
        cp.wait_recv()


def _scatter_scratch(n):
    return [pltpu.SemaphoreType.DMA((n, N_HOPS)), pltpu.SemaphoreType.DMA((n, N_HOPS)), pltpu.SemaphoreType.DMA((n,))]


def _scatter_exchange(partials):
    n = len(partials)
    return _Exchange(
        partials, [jax.ShapeDtypeStruct(p.shape, p.dtype) for p in partials], _scatter_scratch(n),
        lambda src, dst, sems: _scatter_start(src, dst, sems, n),
        lambda src, dst, sems: _scatter_finish(src, dst, sems, n))


SMALL_LAYOUT = {
    "ffn1_norm": (0, 1, 1024), "mix_norm": (1, 1, 1024), "xattn_norm": (2, 1, 1024), "mem_norm": (3, 1, 1024),
    "ffn2_norm": (4, 1, 1024), "final_norm": (5, 1, 1024), "lb_param": (6, 2, 512), "hgrn_out_norm": (8, 1, 512),
    "conv_w": (9, 3, 512), "loss": (12, 1, 128),
}


def _final_exchange(partials, small):
    n = len(partials)
    names = list(small)
    width = 1024

    def body(*refs):
        src = refs[:n]
        pieces = refs[n:n + len(names)]
        dst = refs[n + len(names):2 * n + len(names)]
        total_ref = refs[2 * n + len(names)]
        pack, gathered, small_send, small_recv = refs[2 * n + len(names) + 1:2 * n + len(names) + 5]
        sems = refs[2 * n + len(names) + 5:]
        x, y, c, me = _mesh_place()
        pack[...] = jnp.zeros_like(pack)
        for name, piece in zip(names, pieces):
            row, nrows, ncols = SMALL_LAYOUT[name]
            pack[row:row + nrows, 0:ncols] = piece[...]
        for k in range(1, N_DEV):
            peer, _ = _peer(x, y, c, k)
            _remote(pack, gathered.at[me], small_send.at[k - 1], small_recv.at[k - 1], peer).start()
        _scatter_start(src, dst, sems, n)
        gathered[me] = pack[...]
        for k in range(1, N_DEV):
            peer, peer_index = _peer(x, y, c, k)
            landed = _remote(pack, gathered.at[peer_index], small_send.at[k - 1], small_recv.at[k - 1], peer)
            landed.wait_send()
            landed.wait_recv()
        total = gathered[0]
        for j in range(1, N_DEV):
            total = total + gathered[j]
        total_ref[...] = total
        _scatter_finish(src, dst, sems, n)

    hbm = pl.BlockSpec(memory_space=pltpu.HBM)
    vmem = pl.BlockSpec(memory_space=pltpu.VMEM)
    out = pl.pallas_call(
        body,
        name="final_exchange",
        in_specs=[hbm] * n + [vmem] * len(names),
        out_specs=[hbm] * n + [vmem],
        out_shape=[jax.ShapeDtypeStruct(p.shape, p.dtype) for p in partials]
        + [jax.ShapeDtypeStruct((SMALL_ROWS, width), F32)],
        scratch_shapes=[
            pltpu.VMEM((SMALL_ROWS, width), F32), pltpu.VMEM((N_DEV, SMALL_ROWS, width), F32),
            pltpu.SemaphoreType.DMA((N_DEV - 1,)), pltpu.SemaphoreType.DMA((N_DEV - 1,)),
        ] + _scatter_scratch(n),
        compiler_params=pltpu.CompilerParams(has_side_effects=True),
    )(*partials, *[small[k] for k in names])
    return out[:n], out[n]


def _adamw_math(w, g, m, v):
    m = ADAM_B1 * m + (1.0 - ADAM_B1) * g
    v = ADAM_B2 * v + (1.0 - ADAM_B2) * (g * g)
    m_hat = m / (1.0 - ADAM_B1 ** ADAM_STEP)
    v_hat = v / (1.0 - ADAM_B2 ** ADAM_STEP)
    delta = -ADAM_LR * (m_hat / (jnp.sqrt(v_hat) + ADAM_EPS) + ADAM_WD * w)
    return delta, m, v


def _adamw_shard(parts, w, m, v):
    r, c = w.shape
    n_parts = parts.shape[0]
    tr = max(rows for rows in range(16, r + 1, 16) if r % rows == 0 and rows * c <= ADAMW_TILE_ELEMENTS)

    def body(p_ref, w_ref, m_ref, v_ref, g_ref, d_ref, mo_ref, vo_ref):
        g = p_ref[0].astype(F32)
        for j in range(1, n_parts):
            g = g + p_ref[j].astype(F32)
        delta, mn, vn = _adamw_math(w_ref[...], g, m_ref[...], v_ref[...])
        g_ref[...] = g
        d_ref[...] = delta
        mo_ref[...] = mn
        vo_ref[...] = vn

    tile = pl.BlockSpec((tr, c), lambda i: (i, 0))
    return pl.pallas_call(
        body,
        name="adamw_shard",
        grid=(r // tr,),
        in_specs=[pl.BlockSpec((n_parts, tr, c), lambda i: (0, i, 0)), tile, tile, tile],
        out_specs=[tile] * 4,
        out_shape=[jax.ShapeDtypeStruct((r, c), F32)] * 4,
        compiler_params=_params(("parallel",)),
    )(parts, w, m, v)


def _adamw_small(gs, ws, ms, vs):
    n = len(gs)

    def body(*refs):
        g_refs, w_refs, m_refs, v_refs = refs[:n], refs[n:2 * n], refs[2 * n:3 * n], refs[3 * n:4 * n]
        d_out, m_out, v_out = refs[4 * n:5 * n], refs[5 * n:6 * n], refs[6 * n:7 * n]
        for i in range(n):
            delta, mn, vn = _adamw_math(w_refs[i][...], g_refs[i][...], m_refs[i][...], v_refs[i][...])
            d_out[i][...] = delta
            m_out[i][...] = mn
            v_out[i][...] = vn

    shapes = [jax.ShapeDtypeStruct(w.shape, F32) for w in ws]
    out = pl.pallas_call(
        body,
        name="adamw_small",
        out_shape=shapes * 3,
        compiler_params=_params(),
    )(*gs, *ws, *ms, *vs)
    return out[:n], out[n:2 * n], out[2 * n:]


TRANSPOSED = ("ffn1_gate", "ffn1_up", "w_in", "ffn2_gate", "ffn2_up", "conv_w")
GROUP_FFN1 = ("ffn1_gate", "ffn1_up", "ffn1_down")
GROUP_MIX = ("w_in", "w_out")
GROUP_XATTN = ("w_q_mem", "w_kv_mem", "w_o_mem")
GROUP_FFN2 = ("ffn2_gate", "ffn2_up", "ffn2_down")
LARGE = GROUP_FFN1 + GROUP_MIX + GROUP_XATTN + GROUP_FFN2
SMALL = ("ffn1_norm", "mix_norm", "lb_param", "hgrn_out_norm", "conv_w", "xattn_norm", "mem_norm", "ffn2_norm",
         "final_norm")
WEIGHTS = ("ffn1_norm", "ffn1_gate", "ffn1_up", "ffn1_down", "mix_norm", "w_in", "lb_param", "hgrn_out_norm",
           "conv_w", "w_out", "xattn_norm", "mem_norm", "w_q_mem", "w_kv_mem", "w_o_mem", "ffn2_norm", "ffn2_gate",
           "ffn2_up", "ffn2_down", "final_norm")


def kernel(x, mem, ffn1_norm, ffn1_gate, ffn1_up, ffn1_down, mix_norm, w_in, lb_param, hgrn_out_norm, conv_w, w_out, xattn_norm, mem_norm, w_q_mem, w_kv_mem, w_o_mem, ffn2_norm, ffn2_gate, ffn2_up, ffn2_down, final_norm, loss_target, m_ffn1_norm, m_ffn1_gate, m_ffn1_up, m_ffn1_down, m_mix_norm, m_w_in, m_lb_param, m_hgrn_out_norm, m_conv_w, m_w_out, m_xattn_norm, m_mem_norm, m_w_q_mem, m_w_kv_mem, m_w_o_mem, m_ffn2_norm, m_ffn2_gate, m_ffn2_up, m_ffn2_down, m_final_norm, v_ffn1_norm, v_ffn1_gate, v_ffn1_up, v_ffn1_down, v_mix_norm, v_w_in, v_lb_param, v_hgrn_out_norm, v_conv_w, v_w_out, v_xattn_norm, v_mem_norm, v_w_q_mem, v_w_kv_mem, v_w_o_mem, v_ffn2_norm, v_ffn2_gate, v_ffn2_up, v_ffn2_down, v_final_norm):
    given = dict(locals())
    me = 4 * lax.axis_index("x") + 2 * lax.axis_index("y") + lax.axis_index("c")
    x0, memv, target = x[0], mem[0], loss_target[0]

    def shard(prefix, name):
        v = given[prefix + name]
        if v.ndim == 1:
            return v.reshape(1, -1)
        if v.ndim == 2:
            return v
        return v[0].T if name in TRANSPOSED else v[0]

    w = {name: shard("", name) for name in WEIGHTS}
    m = {name: shard("m_", name) for name in WEIGHTS}
    v = {name: shard("v_", name) for name in WEIGHTS}

    conv_taps, conv_rows = w["conv_w"].shape
    conv_tile = jnp.pad(w["conv_w"], ((0, 8 - conv_taps), (0, 128 - conv_rows)))
    wire = {name: w[name].astype(BF16) for name in LARGE}
    full = {}

    def landed(names, gathered):
        for name, blocks in zip(names, gathered):
            _, r, c = blocks.shape
            full[name] = blocks if name == "w_kv_mem" else blocks.reshape(N_DEV * r, c)

    first = ("ffn1_gate", "ffn1_up")
    landed(first, _run_exchange(_gather_exchange([wire[k] for k in first]), "gather_first"))

    riders = (("ffn1_down", "w_in"), ("w_out", "w_kv_mem"), ("w_q_mem", "w_o_mem", "ffn2_gate", "ffn2_up"),
              ("ffn2_down",))
    (a1, b1, s1), gathered = _ffn_up(
        x0, w["ffn1_norm"], full["ffn1_gate"], full["ffn1_up"],
        exchange=_gather_exchange([wire[k] for k in riders[0]]))
    landed(riders[0], gathered)
    (x1,), gathered = _ffn_down(
        x0, s1, full["ffn1_down"], exchange=_gather_exchange([wire[k] for k in riders[1]] + [conv_tile]))
    landed(riders[1], gathered)
    convw_t = gathered[-1][:, :conv_taps, :conv_rows].transpose(1, 0, 2).reshape(conv_taps, N_DEV * conv_rows)
    (x2, z, o_raw, states, ycat), gathered = _mix_fwd(
        x1, w["mix_norm"], full["w_in"], w["lb_param"], w["hgrn_out_norm"], convw_t, full["w_out"],
        exchange=_gather_exchange([wire[k] for k in riders[2]]))
    landed(riders[2], gathered)
    kv = _memkv_fwd(memv, w["mem_norm"], full["w_kv_mem"])
    (x3, hq, qm, att), gathered = _xattn_fwd(
        x2, w["xattn_norm"], full["w_q_mem"], kv, full["w_o_mem"],
        exchange=_gather_exchange([wire[k] for k in riders[3]]))
    landed(riders[3], gathered)
    (dx4, a2, b2, s2, loss_part, d_final), _ = _ffn_fwd(
        x3, w["ffn2_norm"], full["ffn2_gate"], full["ffn2_up"], full["ffn2_down"], head=(w["final_norm"], target))

    parts = {}
    waiting = []

    def carried(limit=None):
        taken = waiting[:limit]
        del waiting[:len(taken)]
        return [name for name, _ in taken], _scatter_exchange([p for _, p in taken]) if taken else None

    def weight_grad(names, lhs, b, scale=1.0, limit=None):
        carried_names, exchange = carried(limit)
        partials, arrived = _weight_grad(lhs, b, scale, exchange=exchange)
        parts.update(zip(carried_names, arrived))
        waiting.extend(zip(names, partials))

    (dx3, da2, db2, h4, d_ffn2_norm), _ = _ffn_bwd(
        x3, w["ffn2_norm"], dx4, a2, b2, full["ffn2_gate"], full["ffn2_up"], full["ffn2_down"])
    weight_grad(("ffn2_down",), (s2,), dx4, 0.5)
    weight_grad(("ffn2_gate", "ffn2_up"), (da2, db2), h4)
    names, exchange = carried()
    (dx2, dqm, dkv, d_xattn_norm), arrived = _xattn_bwd(
        x2, w["xattn_norm"], dx3, qm, kv, full["w_q_mem"], full["w_o_mem"], exchange=exchange)
    parts.update(zip(names, arrived))
    d_wkv, d_mem_norm = _memkv_bwd(memv, w["mem_norm"], dkv, full["w_kv_mem"])
    waiting.append(("w_kv_mem", d_wkv))
    names, exchange = carried()
    (dx1, dz, h2, d_mix_norm, d_lbp, d_gh, d_convw_t), arrived = _mix_bwd(
        x1, w["mix_norm"], dx2, z, o_raw, states, full["w_in"], w["lb_param"], w["hgrn_out_norm"], convw_t,
        full["w_out"], exchange=exchange)
    parts.update(zip(names, arrived))
    weight_grad(("w_in",), (dz,), h2)
    weight_grad(("ffn1_down",), (s1,), dx1, 0.5)
    (dx0, da1, db1, h1, d_ffn1_norm), _ = _ffn_bwd(
        x0, w["ffn1_norm"], dx1, a1, b1, full["ffn1_gate"], full["ffn1_up"], full["ffn1_down"])
    weight_grad(("ffn1_gate", "ffn1_up"), (da1, db1), h1)
    weight_grad(("w_o_mem",), (att,), dx3, limit=1)
    weight_grad(("w_q_mem",), (hq,), dqm)
    weight_grad(("w_out",), (ycat,), dx2)

    small_parts = {
        "ffn1_norm": d_ffn1_norm, "mix_norm": d_mix_norm, "xattn_norm": d_xattn_norm, "mem_norm": d_mem_norm,
        "ffn2_norm": d_ffn2_norm, "final_norm": d_final, "lb_param": d_lbp, "hgrn_out_norm": d_gh,
        "conv_w": d_convw_t, "loss": loss_part,
    }
    names = [name for name, _ in waiting]
    arrived, total = _final_exchange([p for _, p in waiting], small_parts)
    parts.update(zip(names, arrived))

    g_out, d_out, m_out, v_out = {}, {}, {}, {}
    for name in LARGE:
        g_out[name], d_out[name], m_out[name], v_out[name] = _adamw_shard(parts[name], w[name], m[name], v[name])
    g_small = {}
    for name in SMALL:
        row, nrows, ncols = SMALL_LAYOUT[name]
        g_small[name] = total[row:row + nrows, 0:ncols]
    g_small["conv_w"] = lax.dynamic_slice_in_dim(g_small["conv_w"], me * conv_rows, conv_rows, axis=1)
    ds, ms, vs = _adamw_small(
        [g_small[k] for k in SMALL], [w[k] for k in SMALL], [m[k] for k in SMALL], [v[k] for k in SMALL])
    for i, name in enumerate(SMALL):
        g_out[name], d_out[name], m_out[name], v_out[name] = g_small[name], ds[i], ms[i], vs[i]

    def shaped(value, name):
        return (value.T if name in TRANSPOSED else value).reshape(given[name].shape)

    loss = total[SMALL_LAYOUT["loss"][0], 0]
    outs = [loss, dx0.reshape(x.shape)]
    for group in (g_out, d_out, m_out, v_out):
        outs += [shaped(group[name], name) for name in WEIGHTS]
    return tuple(outs)
```

```python
import jax
import jax.numpy as jnp
from jax import lax
from jax.experimental import pallas as pl
from jax.experimental.pallas import tpu as pltpu

F32 = jnp.float32
BF16 = jnp.bfloat16
MESH_IDS = pl.DeviceIdType.MESH

N_DEV = 8
EPS = 1e-6
HGRN_HEADS = 4
HGRN_DK = 128
HGRN_W = 512
CHUNK = 64
MEM_HEADS = 4
MEM_HD = 256
ADAM_LR = 0.001
ADAM_B1 = 0.9
ADAM_B2 = 0.999
ADAM_EPS = 1e-08
ADAM_WD = 0.01
ADAM_STEP = 10

TOKEN_TILE = 256
XATTN_TILE = 512
REDUCE_TILE = 1024
ADAMW_TILE_ELEMENTS = 256 * 1024
MIDDLE_EIGHTHS = 5
MXU_ROWS = 256
VMEM_LIMIT = 60 * 1024 * 1024
SMALL_ROWS = 16
NT = (((1,), (1,)), ((), ()))
TN = (((0,), (0,)), ((), ()))


def _params(sem=None):
    return pltpu.CompilerParams(dimension_semantics=sem, vmem_limit_bytes=VMEM_LIMIT)


def _dot(a, b, dims=None):
    if dims is None:
        return jnp.dot(a, b, preferred_element_type=F32)
    return lax.dot_general(a, b, dims, preferred_element_type=F32)


def _sigmoid(v):
    return 1.0 / (1.0 + jnp.exp(-v))


def _rms(x, g):
    r = lax.rsqrt(jnp.mean(x * x, axis=-1, keepdims=True) + EPS)
    xh = x * r
    return xh * g, xh, r


def _rms_bwd(dh, xh, r, g):
    dxh = dh * g
    return r * (dxh - xh * jnp.mean(dxh * xh, axis=-1, keepdims=True))


def _full(shape):
    return pl.BlockSpec(shape, lambda *_: (0,) * len(shape))


def _rows(tm, width):
    return pl.BlockSpec((tm, width), lambda i: (i, 0))


def _rows_rev(tm, width, n):
    return pl.BlockSpec((tm, width), lambda i: (n - 1 - i, 0))


def _zero_at_start(*refs):
    @pl.when(pl.program_id(0) == 0)
    def _():
        for ref in refs:
            ref[...] = jnp.zeros_like(ref)


class _Exchange:
    def __init__(self, operands, out_shapes, scratch, start, finish, middle=None):
        self.operands, self.out_shapes, self.scratch = list(operands), list(out_shapes), list(scratch)
        self.start, self.middle, self.finish = start, middle, finish


def _call(body, *, name, grid, in_specs, out_specs, out_shape, args, scratch_shapes=(), exchange=None):
    semantics = ("arbitrary",) * len(grid)
    if exchange is None:
        out = pl.pallas_call(
            body, name=name, grid=grid, in_specs=in_specs, out_specs=out_specs, out_shape=out_shape,
            scratch_shapes=list(scratch_shapes), compiler_params=_params(semantics))(*args)
        return out, []
    hbm = pl.BlockSpec(memory_space=pltpu.HBM)
    n_in, n_out, n_scr = len(in_specs), len(out_specs), len(scratch_shapes)
    e_in, e_out = len(exchange.operands), len(exchange.out_shapes)

    def carried(*refs):
        ins, rest = refs[:n_in], refs[n_in:]
        e_ins, rest = rest[:e_in], rest[e_in:]
        outs, rest = rest[:n_out], rest[n_out:]
        e_outs, rest = rest[:e_out], rest[e_out:]
        scr, e_scr = rest[:n_scr], rest[n_scr:]
        first = last = None
        for axis, size in enumerate(grid):
            at_start, at_end = pl.program_id(axis) == 0, pl.program_id(axis) == size - 1
            first = at_start if first is None else jnp.logical_and(first, at_start)
            last = at_end if last is None else jnp.logical_and(last, at_end)

        @pl.when(first)
        def _():
            exchange.start(e_ins, e_outs, e_scr)

        body(*ins, *outs, *scr)

        if exchange.middle is not None:
            assert len(grid) == 1

            @pl.when(pl.program_id(0) == (grid[0] * MIDDLE_EIGHTHS) // 8)
            def _():
                exchange.middle(e_ins, e_outs, e_scr)

        @pl.when(last)
        def _():
            exchange.finish(e_ins, e_outs, e_scr)

    out = pl.pallas_call(
        carried, name=name, grid=grid, in_specs=list(in_specs) + [hbm] * e_in,
        out_specs=list(out_specs) + [hbm] * e_out, out_shape=list(out_shape) + exchange.out_shapes,
        scratch_shapes=list(scratch_shapes) + exchange.scratch,
        compiler_params=pltpu.CompilerParams(
            dimension_semantics=semantics, vmem_limit_bytes=VMEM_LIMIT, has_side_effects=True),
    )(*args, *exchange.operands)
    return out[:n_out], out[n_out:]


def _run_exchange(exchange, name):
    hbm = pl.BlockSpec(memory_space=pltpu.HBM)
    e_in, e_out = len(exchange.operands), len(exchange.out_shapes)

    def body(*refs):
        e_ins, e_outs, e_scr = refs[:e_in], refs[e_in:e_in + e_out], refs[e_in + e_out:]
        exchange.start(e_ins, e_outs, e_scr)
        if exchange.middle is not None:
            exchange.middle(e_ins, e_outs, e_scr)
        exchange.finish(e_ins, e_outs, e_scr)

    return pl.pallas_call(
        body, name=name, in_specs=[hbm] * e_in, out_specs=[hbm] * e_out, out_shape=exchange.out_shapes,
        scratch_shapes=exchange.scratch, compiler_params=pltpu.CompilerParams(has_side_effects=True),
    )(*exchange.operands)


def _loss_head(xo, gf, tgt):
    d = xo.shape[1]
    y, xh, r = _rms(xo, gf)
    err = y - tgt
    dy = err * (1.0 / d)
    loss = 0.5 * jnp.sum(jnp.sum(err * err, axis=-1, keepdims=True) * (1.0 / d), axis=0, keepdims=True)
    return _rms_bwd(dy, xh, r, gf), loss, jnp.sum(dy * xh, axis=0, keepdims=True)


def _ffn_fwd(x, g, wg, wu, wd, exchange=None, head=None):
    t, d = x.shape
    f = wg.shape[0]
    tm = min(TOKEN_TILE, t)

    def body(x_ref, g_ref, wg_ref, wu_ref, wd_ref, *rest):
        if head is None:
            xo_ref, a_ref, b_ref, s_ref = rest
        else:
            gf_ref, tgt_ref, xo_ref, a_ref, b_ref, s_ref, loss_ref, dgf_ref = rest
            _zero_at_start(loss_ref, dgf_ref)
        xv = x_ref[...]
        h, _, _ = _rms(xv, g_ref[...])
        hb = h.astype(BF16)
        a = _dot(hb, wg_ref[...], NT)
        b = _dot(hb, wu_ref[...], NT)
        s = (a * _sigmoid(a) * b).astype(BF16)
        xo = xv + 0.5 * _dot(s, wd_ref[...])
        if head is None:
            xo_ref[...] = xo
        else:
            xo_ref[...], loss, dgf = _loss_head(xo, gf_ref[...], tgt_ref[...])
            loss_ref[...] += jnp.broadcast_to(loss, (1, 128))
            dgf_ref[...] += dgf
        a_ref[...] = a.astype(BF16)
        b_ref[...] = b.astype(BF16)
        s_ref[...] = s

    in_specs = [_rows(tm, d), _full((1, d)), _full((f, d)), _full((f, d)), _full((f, d))]
    out_specs = [_rows(tm, d), _rows(tm, f), _rows(tm, f), _rows(tm, f)]
    out_shape = [
        jax.ShapeDtypeStruct((t, d), F32),
        jax.ShapeDtypeStruct((t, f), BF16),
        jax.ShapeDtypeStruct((t, f), BF16),
        jax.ShapeDtypeStruct((t, f), BF16),
    ]
    args = (x, g, wg, wu, wd)
    if head is not None:
        in_specs += [_full((1, d)), _rows(tm, d)]
        out_specs += [_full((1, 128)), _full((1, d))]
        out_shape += [jax.ShapeDtypeStruct((1, 128), F32), jax.ShapeDtypeStruct((1, d), F32)]
        args += tuple(head)
    return _call(
        body, name="ffn_fwd", grid=(t // tm,), in_specs=in_specs, out_specs=out_specs, out_shape=out_shape,
        args=args, exchange=exchange)


def _ffn_up(x, g, wg, wu, exchange=None):
    t, d = x.shape
    f = wg.shape[0]
    tm = min(TOKEN_TILE, t)

    def body(x_ref, g_ref, wg_ref, wu_ref, a_ref, b_ref, s_ref):
        h, _, _ = _rms(x_ref[...], g_ref[...])
        hb = h.astype(BF16)
        a = _dot(hb, wg_ref[...], NT)
        b = _dot(hb, wu_ref[...], NT)
        a_ref[...] = a.astype(BF16)
        b_ref[...] = b.astype(BF16)
        s_ref[...] = (a * _sigmoid(a) * b).astype(BF16)

    return _call(
        body, name="ffn_up", grid=(t // tm,),
        in_specs=[_rows(tm, d), _full((1, d)), _full((f, d)), _full((f, d))],
        out_specs=[_rows(tm, f)] * 3, out_shape=[jax.ShapeDtypeStruct((t, f), BF16)] * 3,
        args=(x, g, wg, wu), exchange=exchange)


def _ffn_down(x, s, wd, exchange=None):
    t, d = x.shape
    f = wd.shape[0]
    tm = min(TOKEN_TILE, t)

    def body(x_ref, s_ref, wd_ref, xo_ref):
        xo_ref[...] = x_ref[...] + 0.5 * _dot(s_ref[...], wd_ref[...])

    return _call(
        body, name="ffn_down", grid=(t // tm,),
        in_specs=[_rows(tm, d), _rows(tm, f), _full((f, d))],
        out_specs=[_rows(tm, d)], out_shape=[jax.ShapeDtypeStruct((t, d), F32)],
        args=(x, s, wd), exchange=exchange)


def _ffn_bwd(x, g, dxo, a, b, wg, wu, wd, exchange=None):
    t, d = x.shape
    f = wg.shape[0]
    tm = min(TOKEN_TILE, t)

    def body(x_ref, g_ref, dxo_ref, a_ref, b_ref, wg_ref, wu_ref, wd_ref, dx_ref, da_ref, db_ref, h_ref, dg_ref):
        _zero_at_start(dg_ref)
        gv = g_ref[...]
        h, xh, r = _rms(x_ref[...], gv)
        dxo = dxo_ref[...]
        ds = _dot((0.5 * dxo).astype(BF16), wd_ref[...], NT)
        af = a_ref[...].astype(F32)
        bf = b_ref[...].astype(F32)
        sg = _sigmoid(af)
        da = (ds * bf * (sg * (1.0 + af * (1.0 - sg)))).astype(BF16)
        db = (ds * (af * sg)).astype(BF16)
        dh = _dot(da, wg_ref[...]) + _dot(db, wu_ref[...])
        dx_ref[...] = _rms_bwd(dh, xh, r, gv) + dxo
        da_ref[...] = da
        db_ref[...] = db
        h_ref[...] = h.astype(BF16)
        dg_ref[...] += jnp.sum(dh * xh, axis=0, keepdims=True)

    return _call(
        body,
        name="ffn_bwd",
        grid=(t // tm,),
        in_specs=[
            _rows(tm, d), _full((1, d)), _rows(tm, d), _rows(tm, f), _rows(tm, f),
            _full((f, d)), _full((f, d)), _full((f, d)),
        ],
        out_specs=[_rows(tm, d), _rows(tm, f), _rows(tm, f), _rows(tm, d), _full((1, d))],
        out_shape=[
            jax.ShapeDtypeStruct((t, d), F32),
            jax.ShapeDtypeStruct((t, f), BF16),
            jax.ShapeDtypeStruct((t, f), BF16),
            jax.ShapeDtypeStruct((t, d), BF16),
            jax.ShapeDtypeStruct((1, d), F32),
        ],
        args=(x, g, dxo, a, b, wg, wu, wd),
        exchange=exchange,
    )


def _weight_grad(a, b, scale=1.0, exchange=None):
    t, m = a.shape
    n = b.shape[1]
    chips = N_DEV // 2
    r = m // N_DEV
    tk = min(REDUCE_TILE, t)
    halves = 2
    nb = n // halves
    nk = t // tk

    def body(a_ref, b_ref, o_ref, acc, send_buf, recv_buf, send_sems, recv_sems):
        k, j = pl.program_id(0), pl.program_id(1)
        x, y, c, _ = _mesh_place()
        sibling, _ = _peer(x, y, c, 1)
        bv = b_ref[...]
        if scale != 1.0:
            bv = bv * scale
        bb = bv.astype(BF16)
        acc_half = acc.at[j]

        @pl.when(k == 0)
        def _():
            acc_half[...] = jnp.zeros_like(acc_half)

        for i in range(m // MXU_ROWS):
            rows = slice(i * MXU_ROWS, (i + 1) * MXU_ROWS)
            acc_half[rows, :] += _dot(a_ref[:, rows].astype(BF16), bb, TN)

        def to_sibling(half):
            return _remote(send_buf.at[half], recv_buf.at[half], send_sems.at[half], recv_sems.at[half], sibling)

        def owned_rows(q, core):
            return pl.ds(pl.multiple_of((2 * q + core) * r, 8), r)

        for half in range(halves):
            @pl.when(jnp.logical_and(k == nk - 1, j == half))
            def _():
                for q in range(chips):
                    send_buf[half, q] = acc[half, owned_rows(q, 1 - c), :].astype(BF16)
                to_sibling(half).start()

        @pl.when(jnp.logical_and(k == nk - 1, j == halves - 1))
        def _():
            for half in range(halves):
                to_sibling(half).wait_send()
                to_sibling(half).wait_recv()
                for q in range(chips):
                    o_ref[q, :, half * nb:(half + 1) * nb] = (
                        acc[half, owned_rows(q, c), :] + recv_buf[half, q].astype(F32)).astype(BF16)

    (partial,), arrived = _call(
        body,
        name="weight_grad",
        grid=(nk, halves),
        in_specs=[pl.BlockSpec((tk, m), lambda k, j: (k, 0)), pl.BlockSpec((tk, nb), lambda k, j: (k, j))],
        out_specs=[pl.BlockSpec((chips, r, n), lambda k, j: (0, 0, 0))],
        out_shape=[jax.ShapeDtypeStruct((chips, r, n), BF16)],
        scratch_shapes=[
            pltpu.VMEM((halves, m, nb), F32),
            pltpu.VMEM((halves, chips, r, nb), BF16), pltpu.VMEM((halves, chips, r, nb), BF16),
            pltpu.SemaphoreType.DMA((halves,)), pltpu.SemaphoreType.DMA((halves,)),
        ],
        args=(a, b),
        exchange=exchange,
    )
    return partial, arrived


def _chunk_cumsum(v, reverse=False):
    n, width = v.shape
    row = lax.broadcasted_iota(jnp.int32, (n, n), 0)
    col = lax.broadcasted_iota(jnp.int32, (n, n), 1)
    earlier = col >= row if reverse else col <= row
    tri = jnp.where(jnp.logical_and(row // CHUNK == col // CHUNK, earlier), 1.0, 0.0).astype(BF16)
    hi = v.astype(BF16)
    rest = v - hi.astype(F32)
    mid = rest.astype(BF16)
    low = (rest - mid.astype(F32)).astype(BF16)
    sums = _dot(tri, jnp.concatenate([hi, mid, low], axis=1))
    return sums[:, 0:width] + sums[:, width:2 * width] + sums[:, 2 * width:3 * width]


def _shift_rows(v, shift, edge):
    n = v.shape[0]
    row = lax.broadcasted_iota(jnp.int32, (n, 1), 0)
    out = pltpu.roll(v, shift % n, axis=0)
    if shift > 0:
        for j in range(shift):
            out = jnp.where(row == j, edge[8 - shift + j:8 - shift + j + 1, :], out)
    else:
        for j in range(-shift):
            out = jnp.where(row == n + shift + j, edge[j:j + 1, :], out)
    return out


def _gates(z, lbp):
    w = HGRN_W
    lb = _sigmoid(lbp[0:1, :] - lbp[1:2, :])
    zq = z[:, 0:w]
    sig = _sigmoid(z[:, w:2 * w])
    f = lb + (1.0 - lb) * sig
    sq = _sigmoid(zq)
    q = zq * sq * HGRN_DK ** -0.5
    return lb, sig, f, sq, q


def _decayed_operands(q, f, v, qh_buf, kh_buf, kbar_buf, v_buf, etot_buf):
    n, width = f.shape
    bcum = _chunk_cumsum(jnp.log(f))
    total = jnp.concatenate(
        [jnp.broadcast_to(bcum[c + CHUNK - 1:c + CHUNK, :], (CHUNK, width)) for c in range(0, n, CHUNK)], axis=0)
    eb, enb, erest = jnp.exp(bcum), jnp.exp(-bcum), jnp.exp(total - bcum)
    kk = 1.0 - f
    qh_buf[...] = (q * eb).astype(BF16)
    kh_buf[...] = (kk * enb).astype(BF16)
    kbar_buf[...] = (kk * erest).astype(BF16)
    v_buf[...] = v.astype(BF16)
    etot_buf[...] = jnp.exp(total)
    return eb, enb, erest


def _short_conv(u, edge, cw):
    return cw[0:1, :] * _shift_rows(u, 2, edge) + cw[1:2, :] * _shift_rows(u, 1, edge) + cw[2:3, :] * u


def _block_causal_mask(n):
    row = lax.broadcasted_iota(jnp.int32, (n, n), 0)
    col = lax.broadcasted_iota(jnp.int32, (n, n), 1)
    return jnp.logical_and(row // CHUNK == col // CHUNK, col <= row)


def _spread(v, chunk_of_row, nc):
    return jnp.concatenate([jnp.where(chunk_of_row == c, v, jnp.zeros_like(v)) for c in range(nc)], axis=1)


def _pick(r, chunk_of_row, nc):
    out = jnp.where(chunk_of_row == 0, r[:, 0:HGRN_DK], 0.0)
    for c in range(1, nc):
        out = out + jnp.where(chunk_of_row == c, r[:, c * HGRN_DK:(c + 1) * HGRN_DK], 0.0)
    return out


def _mix_fwd(x, g, w_in, lbp, gh, convw_t, w_out, exchange=None):
    t, d = x.shape
    zw = w_in.shape[0]
    w = HGRN_W
    tm = min(TOKEN_TILE, t)
    nc = tm // CHUNK
    n_chunks = t // CHUNK

    def body(x_ref, g_ref, win_ref, lbp_ref, gh_ref, cw_ref, wout_ref,
             xo_ref, z_ref, o_ref, st_ref, y_ref, state, ucarry, qh_buf, kh_buf, kbar_buf, v_buf, etot_buf):
        _zero_at_start(state, ucarry)
        xv = x_ref[...]
        h, _, _ = _rms(xv, g_ref[...])
        z_ref[...] = _dot(h.astype(BF16), win_ref[...], NT)
        z = z_ref[...]
        _, _, f, _, q = _gates(z, lbp_ref[...])
        _decayed_operands(q, f, z[:, 2 * w:3 * w], qh_buf, kh_buf, kbar_buf, v_buf, etot_buf)
        mask = _block_causal_mask(tm)
        chunk_of_row = lax.broadcasted_iota(jnp.int32, (tm, 1), 0) // CHUNK
        heads = range(HGRN_HEADS)
        hcols = [slice(hd * HGRN_DK, (hd + 1) * HGRN_DK) for hd in heads]
        qh = [qh_buf[:, hcols[hd]] for hd in heads]
        vb = [v_buf[:, hcols[hd]] for hd in heads]
        scores = [jnp.where(mask, _dot(qh[hd], kh_buf[:, hcols[hd]], NT), 0.0).astype(BF16) for hd in heads]
        gains = [_dot(_spread(vb[hd], chunk_of_row, nc), kbar_buf[:, hcols[hd]], TN) for hd in heads]
        entering = []
        for hd in heads:
            states, st = [], state[hd]
            for c in range(nc):
                states.append(st)
                st_ref[c, hd] = st
                st = st * etot_buf[c * CHUNK:c * CHUNK + 1, hcols[hd]] + gains[hd][c * HGRN_DK:(c + 1) * HGRN_DK, :]
            state[hd] = st
            entering.append(jnp.concatenate(states, axis=0).astype(BF16))
        from_states = [_dot(qh[hd], entering[hd], NT) for hd in heads]
        o_heads = [_dot(scores[hd], vb[hd]) + _pick(from_states[hd], chunk_of_row, nc) for hd in heads]
        o_ref[...] = jnp.concatenate(o_heads, axis=1)
        ghv = gh_ref[...]
        normed = jnp.concatenate([_rms(o_heads[hd], ghv[:, hcols[hd]])[0] for hd in heads], axis=1)
        zg = z[:, 3 * w:4 * w]
        u = z[:, 5 * w:6 * w] * z[:, 6 * w:7 * w]
        conv = _short_conv(u, ucarry[...], cw_ref[...])
        ucarry[...] = u[tm - 8:tm, :]
        y = jnp.concatenate([normed * (zg * _sigmoid(zg)), z[:, 4 * w:5 * w] * conv], axis=1).astype(BF16)
        y_ref[...] = y
        xo_ref[...] = xv + _dot(y, wout_ref[...])

    return _call(
        body,
        name="mix_fwd",
        grid=(t // tm,),
        in_specs=[
            _rows(tm, d), _full((1, d)), _full((zw, d)), _full((2, w)), _full((1, w)), _full((3, w)),
            _full((2 * w, d)),
        ],
        out_specs=[
            _rows(tm, d), _rows(tm, zw), _rows(tm, w),
            pl.BlockSpec((nc, HGRN_HEADS, HGRN_DK, HGRN_DK), lambda i: (i, 0, 0, 0)),
            _rows(tm, 2 * w),
        ],
        out_shape=[
            jax.ShapeDtypeStruct((t, d), F32),
            jax.ShapeDtypeStruct((t, zw), F32),
            jax.ShapeDtypeStruct((t, w), F32),
            jax.ShapeDtypeStruct((n_chunks, HGRN_HEADS, HGRN_DK, HGRN_DK), F32),
            jax.ShapeDtypeStruct((t, 2 * w), BF16),
        ],
        scratch_shapes=[
            pltpu.VMEM((HGRN_HEADS, HGRN_DK, HGRN_DK), F32), pltpu.VMEM((8, w), F32),
            pltpu.VMEM((tm, w), BF16), pltpu.VMEM((tm, w), BF16), pltpu.VMEM((tm, w), BF16),
            pltpu.VMEM((tm, w), BF16), pltpu.VMEM((tm, w), F32),
        ],
        args=(x, g, w_in, lbp, gh, convw_t, w_out),
        exchange=exchange,
    )


def _mix_bwd(x, g, dxo, z, o, states, w_in, lbp, gh, convw_t, w_out, exchange=None):
    t, d = x.shape
    zw = w_in.shape[0]
    w = HGRN_W
    tm = min(TOKEN_TILE, t)
    nc = tm // CHUNK
    n = t // tm

    def body(x_ref, g_ref, dxo_ref, z_ref, zprev_ref, o_ref, st_ref, win_ref, lbp_ref, gh_ref, cw_ref, wout_ref,
             dx_ref, dz_ref, h_ref, dg_ref, dlbp_ref, dgh_ref, dcw_ref,
             dstate, dcarry, do_buf, qh_buf, kh_buf, kbar_buf, v_buf, etot_buf):
        _zero_at_start(dstate, dcarry, dg_ref, dlbp_ref, dgh_ref, dcw_ref)
        gv = g_ref[...]
        h, xh, r = _rms(x_ref[...], gv)
        h_ref[...] = h.astype(BF16)
        dxo = dxo_ref[...]
        dy = _dot(dxo.astype(BF16), wout_ref[...], NT)
        z = z_ref[...]
        lb, sig, f, sq, q = _gates(z, lbp_ref[...])
        eb, enb, erest = _decayed_operands(q, f, z[:, 2 * w:3 * w], qh_buf, kh_buf, kbar_buf, v_buf, etot_buf)

        ghv = gh_ref[...]
        zg = z[:, 3 * w:4 * w]
        sgz = _sigmoid(zg)
        dyh = dy[:, 0:w]
        don = dyh * (zg * sgz)
        heads = range(HGRN_HEADS)
        hcols = [slice(hd * HGRN_DK, (hd + 1) * HGRN_DK) for hd in heads]
        norms = [_rms(o_ref[:, hcols[hd]], ghv[:, hcols[hd]]) for hd in heads]
        on = jnp.concatenate([norms[hd][0] for hd in heads], axis=1)
        oh = jnp.concatenate([norms[hd][1] for hd in heads], axis=1)
        dz_ref[:, 3 * w:4 * w] = (dyh * on * (sgz * (1.0 + zg * (1.0 - sgz)))).astype(BF16)
        dgh_ref[...] += jnp.sum(don * oh, axis=0, keepdims=True)
        do_buf[...] = jnp.concatenate(
            [_rms_bwd(don[:, hcols[hd]], norms[hd][1], norms[hd][2], ghv[:, hcols[hd]]) for hd in heads],
            axis=1).astype(BF16)

        zb = z[:, 4 * w:5 * w]
        zc = z[:, 5 * w:6 * w]
        zu = z[:, 6 * w:7 * w]
        u = zc * zu
        cw = cw_ref[...]
        zp = zprev_ref[...]
        uprev = jnp.where(pl.program_id(0) == n - 1, 0.0, zp[:, 5 * w:6 * w] * zp[:, 6 * w:7 * w])
        dyc = dy[:, w:2 * w]
        dz_ref[:, 4 * w:5 * w] = (dyc * _short_conv(u, uprev, cw)).astype(BF16)
        dconv = dyc * zb
        edge = dcarry[...]
        dconv1 = _shift_rows(dconv, -1, edge)
        dconv2 = _shift_rows(dconv, -2, edge)
        dcarry[...] = dconv[0:8, :]
        du = cw[2:3, :] * dconv + cw[1:2, :] * dconv1 + cw[0:1, :] * dconv2
        dz_ref[:, 5 * w:6 * w] = (du * zu).astype(BF16)
        dz_ref[:, 6 * w:7 * w] = (du * zc).astype(BF16)
        dcw_ref[...] += jnp.concatenate([
            jnp.sum(u * dconv2, axis=0, keepdims=True),
            jnp.sum(u * dconv1, axis=0, keepdims=True),
            jnp.sum(u * dconv, axis=0, keepdims=True)], axis=0)

        mask = _block_causal_mask(tm)
        chunk_of_row = lax.broadcasted_iota(jnp.int32, (tm, 1), 0) // CHUNK
        heads = range(HGRN_HEADS)
        hcols = [slice(hd * HGRN_DK, (hd + 1) * HGRN_DK) for hd in heads]
        qhb = [qh_buf[:, hcols[hd]] for hd in heads]
        khb = [kh_buf[:, hcols[hd]] for hd in heads]
        vb = [v_buf[:, hcols[hd]] for hd in heads]
        dob = [do_buf[:, hcols[hd]] for hd in heads]
        scores = [jnp.where(mask, _dot(qhb[hd], khb[hd], NT), 0.0).astype(BF16) for hd in heads]
        dscores = [jnp.where(mask, _dot(dob[hd], vb[hd], NT), 0.0).astype(BF16) for hd in heads]
        gains = [_dot(_spread(dob[hd], chunk_of_row, nc), qhb[hd], TN) for hd in heads]
        dst_rows, dst_lanes, st_lanes, carries = [], [], [], []
        for hd in heads:
            entering = [st_ref[c, hd] for c in range(nc)]
            leaving, carried_back = [None] * nc, [None] * nc
            dst = dstate[hd]
            for c in reversed(range(nc)):
                elast = etot_buf[c * CHUNK:c * CHUNK + 1, hcols[hd]]
                leaving[c] = dst
                carried_back[c] = jnp.sum(dst * entering[c], axis=0, keepdims=True) * elast
                dst = dst * elast + gains[hd][c * HGRN_DK:(c + 1) * HGRN_DK, :]
            dstate[hd] = dst
            dst_rows.append(jnp.concatenate(leaving, axis=0).astype(BF16))
            dst_lanes.append(jnp.concatenate(leaving, axis=1).astype(BF16))
            st_lanes.append(jnp.concatenate(entering, axis=1).astype(BF16))
            carries.append(carried_back)
        dv = [_dot(scores[hd], dob[hd], TN) + _pick(_dot(kbar_buf[:, hcols[hd]], dst_rows[hd], NT), chunk_of_row, nc)
              for hd in heads]
        dz_ref[:, 2 * w:3 * w] = jnp.concatenate(dv, axis=1).astype(BF16)
        dqh = jnp.concatenate(
            [_dot(dscores[hd], khb[hd]) + _pick(_dot(dob[hd], st_lanes[hd]), chunk_of_row, nc) for hd in heads], axis=1)
        dkh = jnp.concatenate([_dot(dscores[hd], qhb[hd], TN) for hd in heads], axis=1)
        dkbar = jnp.concatenate([_pick(_dot(vb[hd], dst_lanes[hd]), chunk_of_row, nc) for hd in heads], axis=1)

        kbar_dkbar = kbar_buf[...].astype(F32) * dkbar
        db = qh_buf[...].astype(F32) * dqh - kh_buf[...].astype(F32) * dkh - kbar_dkbar
        through_last = jnp.concatenate([
            jnp.broadcast_to(
                jnp.sum(kbar_dkbar[c * CHUNK:(c + 1) * CHUNK], axis=0, keepdims=True)
                + jnp.concatenate([carries[hd][c] for hd in heads], axis=1),
                (CHUNK, w))
            for c in range(nc)], axis=0)
        dlogf = _chunk_cumsum(db, reverse=True) + through_last
        df = dlogf / f - (dkh * enb + dkbar * erest)
        zq = z[:, 0:w]
        dz_ref[:, 0:w] = (dqh * eb * HGRN_DK ** -0.5 * (sq * (1.0 + zq * (1.0 - sq)))).astype(BF16)
        dz_ref[:, w:2 * w] = (df * (1.0 - lb) * sig * (1.0 - sig)).astype(BF16)
        dlb = jnp.sum(df * (1.0 - sig), axis=0, keepdims=True) * lb * (1.0 - lb)
        dlbp_ref[...] += jnp.concatenate([dlb, -dlb], axis=0)

        dh = _dot(dz_ref[...], win_ref[...])
        dx_ref[...] = _rms_bwd(dh, xh, r, gv) + dxo
        dg_ref[...] += jnp.sum(dh * xh, axis=0, keepdims=True)

    return _call(
        body,
        name="mix_bwd",
        grid=(n,),
        in_specs=[
            _rows_rev(tm, d, n), _full((1, d)), _rows_rev(tm, d, n), _rows_rev(tm, zw, n),
            pl.BlockSpec((8, zw), lambda i: (jnp.maximum((n - 1 - i) * (tm // 8) - 1, 0), 0)),
            _rows_rev(tm, w, n),
            pl.BlockSpec((nc, HGRN_HEADS, HGRN_DK, HGRN_DK), lambda i: (n - 1 - i, 0, 0, 0)),
            _full((zw, d)), _full((2, w)), _full((1, w)), _full((3, w)), _full((2 * w, d)),
        ],
        out_specs=[
            _rows_rev(tm, d, n), _rows_rev(tm, zw, n), _rows_rev(tm, d, n),
            _full((1, d)), _full((2, w)), _full((1, w)), _full((3, w)),
        ],
        out_shape=[
            jax.ShapeDtypeStruct((t, d), F32),
            jax.ShapeDtypeStruct((t, zw), BF16),
            jax.ShapeDtypeStruct((t, d), BF16),
            jax.ShapeDtypeStruct((1, d), F32),
            jax.ShapeDtypeStruct((2, w), F32),
            jax.ShapeDtypeStruct((1, w), F32),
            jax.ShapeDtypeStruct((3, w), F32),
        ],
        scratch_shapes=[
            pltpu.VMEM((HGRN_HEADS, HGRN_DK, HGRN_DK), F32), pltpu.VMEM((8, w), F32),
            pltpu.VMEM((tm, w), BF16),
            pltpu.VMEM((tm, w), BF16), pltpu.VMEM((tm, w), BF16), pltpu.VMEM((tm, w), BF16),
            pltpu.VMEM((tm, w), BF16), pltpu.VMEM((tm, w), F32),
        ],
        args=(x, g, dxo, z, z, o, states, w_in, lbp, gh, convw_t, w_out),
        exchange=exchange,
    )


def _memkv_fwd(mem, g, wkv):
    m, d = mem.shape
    nb, _, cb = wkv.shape

    def body(mem_ref, g_ref, wkv_ref, kv_ref):
        mn, _, _ = _rms(mem_ref[...], g_ref[...])
        mnb = mn.astype(BF16)
        for j in range(nb):
            kv_ref[:, j * cb:(j + 1) * cb] = _dot(mnb, wkv_ref[j]).astype(BF16)

    return pl.pallas_call(
        body,
        name="memkv_fwd",
        out_shape=jax.ShapeDtypeStruct((m, nb * cb), BF16),
        compiler_params=_params(),
    )(mem, g, wkv)


def _memkv_bwd(mem, g, dkv, wkv):
    m, d = mem.shape
    nb, _, cb = wkv.shape
    chips = nb // 2

    def body(mem_ref, g_ref, dkv_ref, wkv_ref, dw_ref, dg_ref, dw_all, send_buf, recv_buf, send_sem, recv_sem):
        x, y, c, _ = _mesh_place()
        sibling, _ = _peer(x, y, c, 1)
        mn, xh, _ = _rms(mem_ref[...], g_ref[...])
        mnb = mn.astype(BF16)
        dmn = jnp.zeros((m, d), F32)
        for j in range(nb):
            dkvb = dkv_ref[:, j * cb:(j + 1) * cb].astype(BF16)
            dw_all[j] = _dot(mnb, dkvb, TN)
            dmn = dmn + _dot(dkvb, wkv_ref[j], NT)
        dg_ref[...] = jnp.sum(dmn * xh, axis=0, keepdims=True)
        for q in range(chips):
            send_buf[q] = dw_all[2 * q + 1 - c].astype(BF16)
        to_sibling = _remote(send_buf, recv_buf, send_sem, recv_sem, sibling)
        to_sibling.start()
        to_sibling.wait_send()
        to_sibling.wait_recv()
        for q in range(chips):
            dw_ref[q] = (dw_all[2 * q + c] + recv_buf[q].astype(F32)).astype(BF16)

    return pl.pallas_call(
        body,
        name="memkv_bwd",
        out_shape=[jax.ShapeDtypeStruct((chips, d, cb), BF16), jax.ShapeDtypeStruct((1, d), F32)],
        scratch_shapes=[
            pltpu.VMEM((nb, d, cb), F32), pltpu.VMEM((chips, d, cb), BF16), pltpu.VMEM((chips, d, cb), BF16),
            pltpu.SemaphoreType.DMA, pltpu.SemaphoreType.DMA,
        ],
        compiler_params=_params(),
    )(mem, g, dkv, wkv)


def _softmax_rows(qm_h, k_h):
    sc = _dot(qm_h, k_h, NT) * MEM_HD ** -0.5
    e = jnp.exp(sc - jnp.max(sc, axis=-1, keepdims=True))
    return e / jnp.sum(e, axis=-1, keepdims=True)


def _xattn_fwd(x, g, wq, kv, wo, exchange=None):
    t, d = x.shape
    m = kv.shape[0]
    tm = min(XATTN_TILE, t)

    def body(x_ref, g_ref, wq_ref, kv_ref, wo_ref, xo_ref, hq_ref, qm_ref, att_ref):
        xv = x_ref[...]
        h, _, _ = _rms(xv, g_ref[...])
        hb = h.astype(BF16)
        hq_ref[...] = hb
        qm = _dot(hb, wq_ref[...]).astype(BF16)
        qm_ref[...] = qm
        heads = range(MEM_HEADS)
        kcols = [slice(hd * MEM_HD, (hd + 1) * MEM_HD) for hd in heads]
        p = [_softmax_rows(qm[:, kcols[hd]], kv_ref[:, kcols[hd]]) for hd in heads]
        att = jnp.concatenate(
            [_dot(p[hd].astype(BF16), kv_ref[:, d + hd * MEM_HD:d + (hd + 1) * MEM_HD]) for hd in heads],
            axis=1).astype(BF16)
        att_ref[...] = att
        xo_ref[...] = xv + _dot(att, wo_ref[...])

    return _call(
        body,
        name="xattn_fwd",
        grid=(t // tm,),
        in_specs=[_rows(tm, d), _full((1, d)), _full((d, d)), _full((m, 2 * d)), _full((d, d))],
        out_specs=[_rows(tm, d), _rows(tm, d), _rows(tm, d), _rows(tm, d)],
        out_shape=[
            jax.ShapeDtypeStruct((t, d), F32),
            jax.ShapeDtypeStruct((t, d), BF16),
            jax.ShapeDtypeStruct((t, d), BF16),
            jax.ShapeDtypeStruct((t, d), BF16),
        ],
        args=(x, g, wq, kv, wo),
        exchange=exchange,
    )


def _xattn_bwd(x, g, dxo, qm, kv, wq, wo, exchange=None):
    t, d = x.shape
    m = kv.shape[0]
    tm = min(XATTN_TILE, t)

    def body(x_ref, g_ref, dxo_ref, qm_ref, kv_ref, wq_ref, wo_ref, dx_ref, dqm_ref, dkv_ref, dg_ref):
        _zero_at_start(dkv_ref, dg_ref)
        gv = g_ref[...]
        _, xh, r = _rms(x_ref[...], gv)
        dxo = dxo_ref[...]
        datt = _dot(dxo.astype(BF16), wo_ref[...], NT).astype(BF16)
        heads = range(MEM_HEADS)
        kcols = [slice(hd * MEM_HD, (hd + 1) * MEM_HD) for hd in heads]
        vcols = [slice(d + hd * MEM_HD, d + (hd + 1) * MEM_HD) for hd in heads]
        qm_h = [qm_ref[:, kcols[hd]] for hd in heads]
        p = [_softmax_rows(qm_h[hd], kv_ref[:, kcols[hd]]) for hd in heads]
        dp = [_dot(datt[:, kcols[hd]], kv_ref[:, vcols[hd]], NT) for hd in heads]
        dsc = [(p[hd] * (dp[hd] - jnp.sum(p[hd] * dp[hd], axis=-1, keepdims=True)) * MEM_HD ** -0.5).astype(BF16)
               for hd in heads]
        dqm = jnp.concatenate([_dot(dsc[hd], kv_ref[:, kcols[hd]]) for hd in heads], axis=1).astype(BF16)
        dqm_ref[...] = dqm
        dkv_ref[...] += jnp.concatenate(
            [_dot(dsc[hd], qm_h[hd], TN) for hd in heads]
            + [_dot(p[hd].astype(BF16), datt[:, kcols[hd]], TN) for hd in heads], axis=1)
        dh = _dot(dqm, wq_ref[...], NT)
        dx_ref[...] = _rms_bwd(dh, xh, r, gv) + dxo
        dg_ref[...] += jnp.sum(dh * xh, axis=0, keepdims=True)

    return _call(
        body,
        name="xattn_bwd",
        grid=(t // tm,),
        in_specs=[
            _rows(tm, d), _full((1, d)), _rows(tm, d), _rows(tm, d), _full((m, 2 * d)), _full((d, d)), _full((d, d)),
        ],
        out_specs=[_rows(tm, d), _rows(tm, d), _full((m, 2 * d)), _full((1, d))],
        out_shape=[
            jax.ShapeDtypeStruct((t, d), F32),
            jax.ShapeDtypeStruct((t, d), BF16),
            jax.ShapeDtypeStruct((m, 2 * d), F32),
            jax.ShapeDtypeStruct((1, d), F32),
        ],
        args=(x, g, dxo, qm, kv, wq, wo),
        exchange=exchange,
    )


def _mesh_place():
    x, y, c = lax.axis_index("x"), lax.axis_index("y"), lax.axis_index("c")
    return x, y, c, 4 * x + 2 * y + c


def _peer(x, y, c, k):
    px = 1 - x if k & 4 else x
    py = 1 - y if k & 2 else y
    pc = 1 - c if k & 1 else c
    return (px, py, pc), 4 * px + 2 * py + pc


ICI_HOPS = (2, 4, 6)
N_HOPS = len(ICI_HOPS)


def _remote(src, dst, send_sem, recv_sem, peer):
    return pltpu.make_async_remote_copy(
        src_ref=src, dst_ref=dst, send_sem=send_sem, recv_sem=recv_sem, device_id=peer, device_id_type=MESH_IDS)


def _gather_exchange(shards):
    n = len(shards)

    def place():
        x, y, c, me = _mesh_place()
        sibling, _ = _peer(x, y, c, 1)
        to_x, from_x = _peer(x, y, c, 4)
        to_y, from_y = _peer(x, y, c, 2)
        _, from_diagonal = _peer(x, y, c, 6)
        onward = (c * to_y[0] + (1 - c) * to_x[0], c * to_y[1] + (1 - c) * to_x[1], c)
        passed_on = c * from_x + (1 - c) * from_y
        return me, sibling, (to_x, to_y, onward), (from_x, from_y, from_diagonal), passed_on

    def start(src, dst, sems):
        ici_send, ici_recv, pair_send, pair_recv, local = sems
        me, sibling, targets, _, _ = place()
        for a in range(n):
            pltpu.make_async_copy(src[a], dst[a].at[me], local.at[a]).start()
            for j in range(2):
                _remote(src[a], dst[a].at[me], ici_send.at[a, j], ici_recv.at[a, j], targets[j]).start()
            _remote(src[a], dst[a].at[me], pair_send.at[a, 0], pair_recv.at[a, 0], sibling).start()

    def to_sibling(dst, sems, a, j, origin, sibling):
        _, _, pair_send, pair_recv, _ = sems
        slot = dst[a].at[origin]
        return _remote(slot, slot, pair_send.at[a, 1 + j], pair_recv.at[a, 1 + j], sibling)

    def middle(src, dst, sems):
        ici_send, ici_recv, _, _, _ = sems
        _, sibling, targets, origins, passed_on = place()
        for a in range(n):
            for j in range(2):
                _remote(src[a], dst[a].at[origins[j]], ici_send.at[a, j], ici_recv.at[a, j], targets[j]).wait_recv()
            slot = dst[a].at[passed_on]
            _remote(slot, slot, ici_send.at[a, 2], ici_recv.at[a, 2], targets[2]).start()
            for j in range(2):
                to_sibling(dst, sems, a, j, origins[j], sibling).start()

    def finish(src, dst, sems):
        ici_send, ici_recv, pair_send, pair_recv, local = sems
        me, sibling, targets, origins, _ = place()
        for a in range(n):
            _remote(src[a], dst[a].at[origins[2]], ici_send.at[a, 2], ici_recv.at[a, 2], targets[2]).wait_recv()
            to_sibling(dst, sems, a, 2, origins[2], sibling).start()
        for a in range(n):
            pltpu.make_async_copy(src[a], dst[a].at[me], local.at[a]).wait()
            for j in range(N_HOPS):
                _remote(src[a], dst[a].at[me], ici_send.at[a, j], ici_recv.at[a, j], targets[j]).wait_send()
            for j, origin in enumerate((me,) + origins):
                from_sibling = origin + 1 - 2 * (origin % 2)
                passed = _remote(src[a], dst[a].at[from_sibling], pair_send.at[a, j], pair_recv.at[a, j], sibling)
                passed.wait_send()
                passed.wait_recv()

    return _Exchange(
        shards,
        [jax.ShapeDtypeStruct((N_DEV,) + s.shape, s.dtype) for s in shards],
        [
            pltpu.SemaphoreType.DMA((n, N_HOPS)), pltpu.SemaphoreType.DMA((n, N_HOPS)),
            pltpu.SemaphoreType.DMA((n, N_HOPS + 1)), pltpu.SemaphoreType.DMA((n, N_HOPS + 1)),
            pltpu.SemaphoreType.DMA((n,)),
        ],
        start, finish, middle)


def _scatter_copies(src, dst, sems, n, arrivals=False):
    send, recv, local = sems
    x, y, c, _ = _mesh_place()
    chip = 2 * x + y
    if arrivals is None:
        return [pltpu.make_async_copy(src[a].at[chip], dst[a].at[chip], local.at[a]) for a in range(n)]
    copies = []
    for a in range(n):
        for j, k in enumerate(ICI_HOPS):
            peer, _ = _peer(x, y, c, k)
            peer_chip = 2 * peer[0] + peer[1]
            slot = dst[a].at[peer_chip if arrivals else chip]
            copies.append(_remote(src[a].at[peer_chip], slot, send.at[a, j], recv.at[a, j], peer))
    return copies


def _scatter_start(src, dst, sems, n):
    for cp in _scatter_copies(src, dst, sems, n, arrivals=None) + _scatter_copies(src, dst, sems, n):
        cp.start()


def _scatter_finish(src, dst, sems, n):
    for cp in _scatter_copies(src, dst, sems, n, arrivals=None):
        cp.wait()
    for cp in _scatter_copies(src, dst, sems, n):
        cp.wait_send()
    for cp in _scatter_copies(src, dst, sems, n, arrivals=True):
        cp.wait_recv()


def _scatter_scratch(n):
    return [pltpu.SemaphoreType.DMA((n, N_HOPS)), pltpu.SemaphoreType.DMA((n, N_HOPS)), pltpu.SemaphoreType.DMA((n,))]


def _scatter_exchange(partials):
    n = len(partials)
    return _Exchange(
        partials, [jax.ShapeDtypeStruct(p.shape, p.dtype) for p in partials], _scatter_scratch(n),
        lambda src, dst, sems: _scatter_start(src, dst, sems, n),
        lambda src, dst, sems: _scatter_finish(src, dst, sems, n))


SMALL_LAYOUT = {
    "ffn1_norm": (0, 1, 1024), "mix_norm": (1, 1, 1024), "xattn_norm": (2, 1, 1024), "mem_norm": (3, 1, 1024),
    "ffn2_norm": (4, 1, 1024), "final_norm": (5, 1, 1024), "lb_param": (6, 2, 512), "hgrn_out_norm": (8, 1, 512),
    "conv_w": (9, 3, 512), "loss": (12, 1, 128),
}


def _final_exchange(partials, small):
    n = len(partials)
    names = list(small)
    width = 1024

    def body(*refs):
        src = refs[:n]
        pieces = refs[n:n + len(names)]
        dst = refs[n + len(names):2 * n + len(names)]
        total_ref = refs[2 * n + len(names)]
        pack, gathered, small_send, small_recv = refs[2 * n + len(names) + 1:2 * n + len(names) + 5]
        sems = refs[2 * n + len(names) + 5:]
        x, y, c, me = _mesh_place()
        pack[...] = jnp.zeros_like(pack)
        for name, piece in zip(names, pieces):
            row, nrows, ncols = SMALL_LAYOUT[name]
            pack[row:row + nrows, 0:ncols] = piece[...]
        for k in range(1, N_DEV):
            peer, _ = _peer(x, y, c, k)
            _remote(pack, gathered.at[me], small_send.at[k - 1], small_recv.at[k - 1], peer).start()
        _scatter_start(src, dst, sems, n)
        gathered[me] = pack[...]
        for k in range(1, N_DEV):
            peer, peer_index = _peer(x, y, c, k)
            landed = _remote(pack, gathered.at[peer_index], small_send.at[k - 1], small_recv.at[k - 1], peer)
            landed.wait_send()
            landed.wait_recv()
        total = gathered[0]
        for j in range(1, N_DEV):
            total = total + gathered[j]
        total_ref[...] = total
        _scatter_finish(src, dst, sems, n)

    hbm = pl.BlockSpec(memory_space=pltpu.HBM)
    vmem = pl.BlockSpec(memory_space=pltpu.VMEM)
    out = pl.pallas_call(
        body,
        name="final_exchange",
        in_specs=[hbm] * n + [vmem] * len(names),
        out_specs=[hbm] * n + [vmem],
        out_shape=[jax.ShapeDtypeStruct(p.shape, p.dtype) for p in partials]
        + [jax.ShapeDtypeStruct((SMALL_ROWS, width), F32)],
        scratch_shapes=[
            pltpu.VMEM((SMALL_ROWS, width), F32), pltpu.VMEM((N_DEV, SMALL_ROWS, width), F32),
            pltpu.SemaphoreType.DMA((N_DEV - 1,)), pltpu.SemaphoreType.DMA((N_DEV - 1,)),
        ] + _scatter_scratch(n),
        compiler_params=pltpu.CompilerParams(has_side_effects=True),
    )(*partials, *[small[k] for k in names])
    return out[:n], out[n]


def _adamw_math(w, g, m, v):
    m = ADAM_B1 * m + (1.0 - ADAM_B1) * g
    v = ADAM_B2 * v + (1.0 - ADAM_B2) * (g * g)
    m_hat = m / (1.0 - ADAM_B1 ** ADAM_STEP)
    v_hat = v / (1.0 - ADAM_B2 ** ADAM_STEP)
    delta = -ADAM_LR * (m_hat / (jnp.sqrt(v_hat) + ADAM_EPS) + ADAM_WD * w)
    return delta, m, v


def _adamw_shard(parts, w, m, v):
    r, c = w.shape
    n_parts = parts.shape[0]
    tr = max(rows for rows in range(16, r + 1, 16) if r % rows == 0 and rows * c <= ADAMW_TILE_ELEMENTS)

    def body(p_ref, w_ref, m_ref, v_ref, g_ref, d_ref, mo_ref, vo_ref):
        g = p_ref[0].astype(F32)
        for j in range(1, n_parts):
            g = g + p_ref[j].astype(F32)
        delta, mn, vn = _adamw_math(w_ref[...], g, m_ref[...], v_ref[...])
        g_ref[...] = g
        d_ref[...] = delta
        mo_ref[...] = mn
        vo_ref[...] = vn

    tile = pl.BlockSpec((tr, c), lambda i: (i, 0))
    return pl.pallas_call(
        body,
        name="adamw_shard",
        grid=(r // tr,),
        in_specs=[pl.BlockSpec((n_parts, tr, c), lambda i: (0, i, 0)), tile, tile, tile],
        out_specs=[tile] * 4,
        out_shape=[jax.ShapeDtypeStruct((r, c), F32)] * 4,
        compiler_params=_params(("parallel",)),
    )(parts, w, m, v)


def _adamw_small(gs, ws, ms, vs):
    n = len(gs)

    def body(*refs):
        g_refs, w_refs, m_refs, v_refs = refs[:n], refs[n:2 * n], refs[2 * n:3 * n], refs[3 * n:4 * n]
        d_out, m_out, v_out = refs[4 * n:5 * n], refs[5 * n:6 * n], refs[6 * n:7 * n]
        for i in range(n):
            delta, mn, vn = _adamw_math(w_refs[i][...], g_refs[i][...], m_refs[i][...], v_refs[i][...])
            d_out[i][...] = delta
            m_out[i][...] = mn
            v_out[i][...] = vn

    shapes = [jax.ShapeDtypeStruct(w.shape, F32) for w in ws]
    out = pl.pallas_call(
        body,
        name="adamw_small",
        out_shape=shapes * 3,
        compiler_params=_params(),
    )(*gs, *ws, *ms, *vs)
    return out[:n], out[n:2 * n], out[2 * n:]


TRANSPOSED = ("ffn1_gate", "ffn1_up", "w_in", "ffn2_gate", "ffn2_up", "conv_w")
GROUP_FFN1 = ("ffn1_gate", "ffn1_up", "ffn1_down")
GROUP_MIX = ("w_in", "w_out")
GROUP_XATTN = ("w_q_mem", "w_kv_mem", "w_o_mem")
GROUP_FFN2 = ("ffn2_gate", "ffn2_up", "ffn2_down")
LARGE = GROUP_FFN1 + GROUP_MIX + GROUP_XATTN + GROUP_FFN2
SMALL = ("ffn1_norm", "mix_norm", "lb_param", "hgrn_out_norm", "conv_w", "xattn_norm", "mem_norm", "ffn2_norm",
         "final_norm")
WEIGHTS = ("ffn1_norm", "ffn1_gate", "ffn1_up", "ffn1_down", "mix_norm", "w_in", "lb_param", "hgrn_out_norm",
           "conv_w", "w_out", "xattn_norm", "mem_norm", "w_q_mem", "w_kv_mem", "w_o_mem", "ffn2_norm", "ffn2_gate",
           "ffn2_up", "ffn2_down", "final_norm")


def kernel(x, mem, ffn1_norm, ffn1_gate, ffn1_up, ffn1_down, mix_norm, w_in, lb_param, hgrn_out_norm, conv_w, w_out, xattn_norm, mem_norm, w_q_mem, w_kv_mem, w_o_mem, ffn2_norm, ffn2_gate, ffn2_up, ffn2_down, final_norm, loss_target, m_ffn1_norm, m_ffn1_gate, m_ffn1_up, m_ffn1_down, m_mix_norm, m_w_in, m_lb_param, m_hgrn_out_norm, m_conv_w, m_w_out, m_xattn_norm, m_mem_norm, m_w_q_mem, m_w_kv_mem, m_w_o_mem, m_ffn2_norm, m_ffn2_gate, m_ffn2_up, m_ffn2_down, m_final_norm, v_ffn1_norm, v_ffn1_gate, v_ffn1_up, v_ffn1_down, v_mix_norm, v_w_in, v_lb_param, v_hgrn_out_norm, v_conv_w, v_w_out, v_xattn_norm, v_mem_norm, v_w_q_mem, v_w_kv_mem, v_w_o_mem, v_ffn2_norm, v_ffn2_gate, v_ffn2_up, v_ffn2_down, v_final_norm):
    given = dict(locals())
    me = 4 * lax.axis_index("x") + 2 * lax.axis_index("y") + lax.axis_index("c")
    x0, memv, target = x[0], mem[0], loss_target[0]

    def shard(prefix, name):
        v = given[prefix + name]
        if v.ndim == 1:
            return v.reshape(1, -1)
        if v.ndim == 2:
            return v
        return v[0].T if name in TRANSPOSED else v[0]

    w = {name: shard("", name) for name in WEIGHTS}
    m = {name: shard("m_", name) for name in WEIGHTS}
    v = {name: shard("v_", name) for name in WEIGHTS}

    conv_taps, conv_rows = w["conv_w"].shape
    conv_tile = jnp.pad(w["conv_w"], ((0, 8 - conv_taps), (0, 128 - conv_rows)))
    wire = {name: w[name].astype(BF16) for name in LARGE}
    full = {}

    def landed(names, gathered):
        for name, blocks in zip(names, gathered):
            _, r, c = blocks.shape
            full[name] = blocks if name == "w_kv_mem" else blocks.reshape(N_DEV * r, c)

    first = ("ffn1_gate", "ffn1_up")
    landed(first, _run_exchange(_gather_exchange([wire[k] for k in first]), "gather_first"))

    riders = (("ffn1_down", "w_in"), ("w_out", "w_kv_mem"), ("w_q_mem", "w_o_mem", "ffn2_gate", "ffn2_up"),
              ("ffn2_down",))
    (a1, b1, s1), gathered = _ffn_up(
        x0, w["ffn1_norm"], full["ffn1_gate"], full["ffn1_up"],
        exchange=_gather_exchange([wire[k] for k in riders[0]]))
    landed(riders[0], gathered)
    (x1,), gathered = _ffn_down(
        x0, s1, full["ffn1_down"], exchange=_gather_exchange([wire[k] for k in riders[1]] + [conv_tile]))
    landed(riders[1], gathered)
    convw_t = gathered[-1][:, :conv_taps, :conv_rows].transpose(1, 0, 2).reshape(conv_taps, N_DEV * conv_rows)
    (x2, z, o_raw, states, ycat), gathered = _mix_fwd(
        x1, w["mix_norm"], full["w_in"], w["lb_param"], w["hgrn_out_norm"], convw_t, full["w_out"],
        exchange=_gather_exchange([wire[k] for k in riders[2]]))
    landed(riders[2], gathered)
    kv = _memkv_fwd(memv, w["mem_norm"], full["w_kv_mem"])
    (x3, hq, qm, att), gathered = _xattn_fwd(
        x2, w["xattn_norm"], full["w_q_mem"], kv, full["w_o_mem"],
        exchange=_gather_exchange([wire[k] for k in riders[3]]))
    landed(riders[3], gathered)
    (dx4, a2, b2, s2, loss_part, d_final), _ = _ffn_fwd(
        x3, w["ffn2_norm"], full["ffn2_gate"], full["ffn2_up"], full["ffn2_down"], head=(w["final_norm"], target))

    parts = {}
    waiting = []

    def carried():
        names = [name for name, _ in waiting]
        exchange = _scatter_exchange([p for _, p in waiting]) if waiting else None
        del waiting[:]
        return names, exchange

    def weight_grad(name, a, b, scale=1.0):
        names, exchange = carried()
        partial, arrived = _weight_grad(a, b, scale, exchange=exchange)
        parts.update(zip(names, arrived))
        waiting.append((name, partial))

    (dx3, da2, db2, h4, d_ffn2_norm), _ = _ffn_bwd(
        x3, w["ffn2_norm"], dx4, a2, b2, full["ffn2_gate"], full["ffn2_up"], full["ffn2_down"])
    weight_grad("ffn2_down", s2, dx4, 0.5)
    weight_grad("ffn2_gate", da2, h4)
    weight_grad("ffn2_up", db2, h4)
    names, exchange = carried()
    (dx2, dqm, dkv, d_xattn_norm), arrived = _xattn_bwd(
        x2, w["xattn_norm"], dx3, qm, kv, full["w_q_mem"], full["w_o_mem"], exchange=exchange)
    parts.update(zip(names, arrived))
    d_wkv, d_mem_norm = _memkv_bwd(memv, w["mem_norm"], dkv, full["w_kv_mem"])
    waiting.append(("w_kv_mem", d_wkv))
    names, exchange = carried()
    (dx1, dz, h2, d_mix_norm, d_lbp, d_gh, d_convw_t), arrived = _mix_bwd(
        x1, w["mix_norm"], dx2, z, o_raw, states, full["w_in"], w["lb_param"], w["hgrn_out_norm"], convw_t,
        full["w_out"], exchange=exchange)
    parts.update(zip(names, arrived))
    weight_grad("w_in", dz, h2)
    weight_grad("ffn1_down", s1, dx1, 0.5)
    (dx0, da1, db1, h1, d_ffn1_norm), _ = _ffn_bwd(
        x0, w["ffn1_norm"], dx1, a1, b1, full["ffn1_gate"], full["ffn1_up"], full["ffn1_down"])
    weight_grad("ffn1_gate", da1, h1)
    weight_grad("ffn1_up", db1, h1)
    weight_grad("w_o_mem", att, dx3)
    weight_grad("w_q_mem", hq, dqm)
    weight_grad("w_out", ycat, dx2)

    small_parts = {
        "ffn1_norm": d_ffn1_norm, "mix_norm": d_mix_norm, "xattn_norm": d_xattn_norm, "mem_norm": d_mem_norm,
        "ffn2_norm": d_ffn2_norm, "final_norm": d_final, "lb_param": d_lbp, "hgrn_out_norm": d_gh,
        "conv_w": d_convw_t, "loss": loss_part,
    }
    names = [name for name, _ in waiting]
    arrived, total = _final_exchange([p for _, p in waiting], small_parts)
    parts.update(zip(names, arrived))

    g_out, d_out, m_out, v_out = {}, {}, {}, {}
    for name in LARGE:
        g_out[name], d_out[name], m_out[name], v_out[name] = _adamw_shard(parts[name], w[name], m[name], v[name])
    g_small = {}
    for name in SMALL:
        row, nrows, ncols = SMALL_LAYOUT[name]
        g_small[name] = total[row:row + nrows, 0:ncols]
    g_small["conv_w"] = lax.dynamic_slice_in_dim(g_small["conv_w"], me * conv_rows, conv_rows, axis=1)
    ds, ms, vs = _adamw_small(
        [g_small[k] for k in SMALL], [w[k] for k in SMALL], [m[k] for k in SMALL], [v[k] for k in SMALL])
    for i, name in enumerate(SMALL):
        g_out[name], d_out[name], m_out[name], v_out[name] = g_small[name], ds[i], ms[i], vs[i]

    def shaped(value, name):
        return (value.T if name in TRANSPOSED else value).reshape(given[name].shape)

    loss = total[SMALL_LAYOUT["loss"][0], 0]
    outs = [loss, dx0.reshape(x.shape)]
    for group in (g_out, d_out, m_out, v_out):
        outs += [shaped(group[name], name) for name in WEIGHTS]
    return tuple(outs)
```

```python
import jax
import jax.numpy as jnp
from jax import lax
from jax.experimental import pallas as pl
from jax.experimental.pallas import tpu as pltpu

F32 = jnp.float32
BF16 = jnp.bfloat16
MESH_IDS = pl.DeviceIdType.MESH

N_DEV = 8
EPS = 1e-6
HGRN_HEADS = 4
HGRN_DK = 128
HGRN_W = 512
CHUNK = 64
MEM_HEADS = 4
MEM_HD = 256
ADAM_LR = 0.001
ADAM_B1 = 0.9
ADAM_B2 = 0.999
ADAM_EPS = 1e-08
ADAM_WD = 0.01
ADAM_STEP = 10

TOKEN_TILE = 256
XATTN_TILE = 512
REDUCE_TILE = 1024
ADAMW_TILE_ELEMENTS = 256 * 1024
MIDDLE_EIGHTHS = 4
MXU_ROWS = 256
VMEM_LIMIT = 60 * 1024 * 1024
SMALL_ROWS = 16
NT = (((1,), (1,)), ((), ()))
TN = (((0,), (0,)), ((), ()))


def _params(sem=None):
    return pltpu.CompilerParams(dimension_semantics=sem, vmem_limit_bytes=VMEM_LIMIT)


def _dot(a, b, dims=None):
    if dims is None:
        return jnp.dot(a, b, preferred_element_type=F32)
    return lax.dot_general(a, b, dims, preferred_element_type=F32)


def _sigmoid(v):
    return 1.0 / (1.0 + jnp.exp(-v))


def _rms(x, g):
    r = lax.rsqrt(jnp.mean(x * x, axis=-1, keepdims=True) + EPS)
    xh = x * r
    return xh * g, xh, r


def _rms_bwd(dh, xh, r, g):
    dxh = dh * g
    return r * (dxh - xh * jnp.mean(dxh * xh, axis=-1, keepdims=True))


def _full(shape):
    return pl.BlockSpec(shape, lambda *_: (0,) * len(shape))


def _rows(tm, width):
    return pl.BlockSpec((tm, width), lambda i: (i, 0))


def _rows_rev(tm, width, n):
    return pl.BlockSpec((tm, width), lambda i: (n - 1 - i, 0))


def _zero_at_start(*refs):
    @pl.when(pl.program_id(0) == 0)
    def _():
        for ref in refs:
            ref[...] = jnp.zeros_like(ref)


class _Exchange:
    def __init__(self, operands, out_shapes, scratch, start, finish, middle=None):
        self.operands, self.out_shapes, self.scratch = list(operands), list(out_shapes), list(scratch)
        self.start, self.middle, self.finish = start, middle, finish


def _call(body, *, name, grid, in_specs, out_specs, out_shape, args, scratch_shapes=(), exchange=None):
    semantics = ("arbitrary",) * len(grid)
    if exchange is None:
        out = pl.pallas_call(
            body, name=name, grid=grid, in_specs=in_specs, out_specs=out_specs, out_shape=out_shape,
            scratch_shapes=list(scratch_shapes), compiler_params=_params(semantics))(*args)
        return out, []
    hbm = pl.BlockSpec(memory_space=pltpu.HBM)
    n_in, n_out, n_scr = len(in_specs), len(out_specs), len(scratch_shapes)
    e_in, e_out = len(exchange.operands), len(exchange.out_shapes)

    def carried(*refs):
        ins, rest = refs[:n_in], refs[n_in:]
        e_ins, rest = rest[:e_in], rest[e_in:]
        outs, rest = rest[:n_out], rest[n_out:]
        e_outs, rest = rest[:e_out], rest[e_out:]
        scr, e_scr = rest[:n_scr], rest[n_scr:]
        first = last = None
        for axis, size in enumerate(grid):
            at_start, at_end = pl.program_id(axis) == 0, pl.program_id(axis) == size - 1
            first = at_start if first is None else jnp.logical_and(first, at_start)
            last = at_end if last is None else jnp.logical_and(last, at_end)

        @pl.when(first)
        def _():
            exchange.start(e_ins, e_outs, e_scr)

        body(*ins, *outs, *scr)

        if exchange.middle is not None:
            assert len(grid) == 1

            @pl.when(pl.program_id(0) == (grid[0] * MIDDLE_EIGHTHS) // 8)
            def _():
                exchange.middle(e_ins, e_outs, e_scr)

        @pl.when(last)
        def _():
            exchange.finish(e_ins, e_outs, e_scr)

    out = pl.pallas_call(
        carried, name=name, grid=grid, in_specs=list(in_specs) + [hbm] * e_in,
        out_specs=list(out_specs) + [hbm] * e_out, out_shape=list(out_shape) + exchange.out_shapes,
        scratch_shapes=list(scratch_shapes) + exchange.scratch,
        compiler_params=pltpu.CompilerParams(
            dimension_semantics=semantics, vmem_limit_bytes=VMEM_LIMIT, has_side_effects=True),
    )(*args, *exchange.operands)
    return out[:n_out], out[n_out:]


def _run_exchange(exchange, name):
    hbm = pl.BlockSpec(memory_space=pltpu.HBM)
    e_in, e_out = len(exchange.operands), len(exchange.out_shapes)

    def body(*refs):
        e_ins, e_outs, e_scr = refs[:e_in], refs[e_in:e_in + e_out], refs[e_in + e_out:]
        exchange.start(e_ins, e_outs, e_scr)
        if exchange.middle is not None:
            exchange.middle(e_ins, e_outs, e_scr)
        exchange.finish(e_ins, e_outs, e_scr)

    return pl.pallas_call(
        body, name=name, in_specs=[hbm] * e_in, out_specs=[hbm] * e_out, out_shape=exchange.out_shapes,
        scratch_shapes=exchange.scratch, compiler_params=pltpu.CompilerParams(has_side_effects=True),
    )(*exchange.operands)


def _loss_head(xo, gf, tgt):
    d = xo.shape[1]
    y, xh, r = _rms(xo, gf)
    err = y - tgt
    dy = err * (1.0 / d)
    loss = 0.5 * jnp.sum(jnp.sum(err * err, axis=-1, keepdims=True) * (1.0 / d), axis=0, keepdims=True)
    return _rms_bwd(dy, xh, r, gf), loss, jnp.sum(dy * xh, axis=0, keepdims=True)


def _ffn_fwd(x, g, wg, wu, wd, exchange=None, head=None):
    t, d = x.shape
    f = wg.shape[0]
    tm = min(TOKEN_TILE, t)

    def body(x_ref, g_ref, wg_ref, wu_ref, wd_ref, *rest):
        if head is None:
            xo_ref, a_ref, b_ref, s_ref = rest
        else:
            gf_ref, tgt_ref, xo_ref, a_ref, b_ref, s_ref, loss_ref, dgf_ref = rest
            _zero_at_start(loss_ref, dgf_ref)
        xv = x_ref[...]
        h, _, _ = _rms(xv, g_ref[...])
        hb = h.astype(BF16)
        a = _dot(hb, wg_ref[...], NT)
        b = _dot(hb, wu_ref[...], NT)
        s = (a * _sigmoid(a) * b).astype(BF16)
        xo = xv + 0.5 * _dot(s, wd_ref[...])
        if head is None:
            xo_ref[...] = xo
        else:
            xo_ref[...], loss, dgf = _loss_head(xo, gf_ref[...], tgt_ref[...])
            loss_ref[...] += jnp.broadcast_to(loss, (1, 128))
            dgf_ref[...] += dgf
        a_ref[...] = a.astype(BF16)
        b_ref[...] = b.astype(BF16)
        s_ref[...] = s

    in_specs = [_rows(tm, d), _full((1, d)), _full((f, d)), _full((f, d)), _full((f, d))]
    out_specs = [_rows(tm, d), _rows(tm, f), _rows(tm, f), _rows(tm, f)]
    out_shape = [
        jax.ShapeDtypeStruct((t, d), F32),
        jax.ShapeDtypeStruct((t, f), BF16),
        jax.ShapeDtypeStruct((t, f), BF16),
        jax.ShapeDtypeStruct((t, f), BF16),
    ]
    args = (x, g, wg, wu, wd)
    if head is not None:
        in_specs += [_full((1, d)), _rows(tm, d)]
        out_specs += [_full((1, 128)), _full((1, d))]
        out_shape += [jax.ShapeDtypeStruct((1, 128), F32), jax.ShapeDtypeStruct((1, d), F32)]
        args += tuple(head)
    return _call(
        body, name="ffn_fwd", grid=(t // tm,), in_specs=in_specs, out_specs=out_specs, out_shape=out_shape,
        args=args, exchange=exchange)


def _ffn_up(x, g, wg, wu, exchange=None):
    t, d = x.shape
    f = wg.shape[0]
    tm = min(TOKEN_TILE, t)

    def body(x_ref, g_ref, wg_ref, wu_ref, a_ref, b_ref, s_ref):
        h, _, _ = _rms(x_ref[...], g_ref[...])
        hb = h.astype(BF16)
        a = _dot(hb, wg_ref[...], NT)
        b = _dot(hb, wu_ref[...], NT)
        a_ref[...] = a.astype(BF16)
        b_ref[...] = b.astype(BF16)
        s_ref[...] = (a * _sigmoid(a) * b).astype(BF16)

    return _call(
        body, name="ffn_up", grid=(t // tm,),
        in_specs=[_rows(tm, d), _full((1, d)), _full((f, d)), _full((f, d))],
        out_specs=[_rows(tm, f)] * 3, out_shape=[jax.ShapeDtypeStruct((t, f), BF16)] * 3,
        args=(x, g, wg, wu), exchange=exchange)


def _ffn_down(x, s, wd, exchange=None):
    t, d = x.shape
    f = wd.shape[0]
    tm = min(TOKEN_TILE, t)

    def body(x_ref, s_ref, wd_ref, xo_ref):
        xo_ref[...] = x_ref[...] + 0.5 * _dot(s_ref[...], wd_ref[...])

    return _call(
        body, name="ffn_down", grid=(t // tm,),
        in_specs=[_rows(tm, d), _rows(tm, f), _full((f, d))],
        out_specs=[_rows(tm, d)], out_shape=[jax.ShapeDtypeStruct((t, d), F32)],
        args=(x, s, wd), exchange=exchange)


def _ffn_bwd(x, g, dxo, a, b, wg, wu, wd, exchange=None):
    t, d = x.shape
    f = wg.shape[0]
    tm = min(TOKEN_TILE, t)

    def body(x_ref, g_ref, dxo_ref, a_ref, b_ref, wg_ref, wu_ref, wd_ref, dx_ref, da_ref, db_ref, h_ref, dg_ref):
        _zero_at_start(dg_ref)
        gv = g_ref[...]
        h, xh, r = _rms(x_ref[...], gv)
        dxo = dxo_ref[...]
        ds = _dot((0.5 * dxo).astype(BF16), wd_ref[...], NT)
        af = a_ref[...].astype(F32)
        bf = b_ref[...].astype(F32)
        sg = _sigmoid(af)
        da = (ds * bf * (sg * (1.0 + af * (1.0 - sg)))).astype(BF16)
        db = (ds * (af * sg)).astype(BF16)
        dh = _dot(da, wg_ref[...]) + _dot(db, wu_ref[...])
        dx_ref[...] = _rms_bwd(dh, xh, r, gv) + dxo
        da_ref[...] = da
        db_ref[...] = db
        h_ref[...] = h.astype(BF16)
        dg_ref[...] += jnp.sum(dh * xh, axis=0, keepdims=True)

    return _call(
        body,
        name="ffn_bwd",
        grid=(t // tm,),
        in_specs=[
            _rows(tm, d), _full((1, d)), _rows(tm, d), _rows(tm, f), _rows(tm, f),
            _full((f, d)), _full((f, d)), _full((f, d)),
        ],
        out_specs=[_rows(tm, d), _rows(tm, f), _rows(tm, f), _rows(tm, d), _full((1, d))],
        out_shape=[
            jax.ShapeDtypeStruct((t, d), F32),
            jax.ShapeDtypeStruct((t, f), BF16),
            jax.ShapeDtypeStruct((t, f), BF16),
            jax.ShapeDtypeStruct((t, d), BF16),
            jax.ShapeDtypeStruct((1, d), F32),
        ],
        args=(x, g, dxo, a, b, wg, wu, wd),
        exchange=exchange,
    )


def _weight_grad(a, b, scale=1.0, exchange=None):
    t, m = a.shape
    n = b.shape[1]
    chips = N_DEV // 2
    r = m // N_DEV
    tk = min(REDUCE_TILE, t)
    halves = 2
    nb = n // halves
    nk = t // tk

    def body(a_ref, b_ref, o_ref, acc, send_buf, recv_buf, send_sems, recv_sems):
        k, j = pl.program_id(0), pl.program_id(1)
        x, y, c, _ = _mesh_place()
        sibling, _ = _peer(x, y, c, 1)
        bv = b_ref[...]
        if scale != 1.0:
            bv = bv * scale
        bb = bv.astype(BF16)
        acc_half = acc.at[j]

        @pl.when(k == 0)
        def _():
            acc_half[...] = jnp.zeros_like(acc_half)

        for i in range(m // MXU_ROWS):
            rows = slice(i * MXU_ROWS, (i + 1) * MXU_ROWS)
            acc_half[rows, :] += _dot(a_ref[:, rows].astype(BF16), bb, TN)

        def to_sibling(half):
            return _remote(send_buf.at[half], recv_buf.at[half], send_sems.at[half], recv_sems.at[half], sibling)

        def owned_rows(q, core):
            return pl.ds(pl.multiple_of((2 * q + core) * r, 8), r)

        for half in range(halves):
            @pl.when(jnp.logical_and(k == nk - 1, j == half))
            def _():
                for q in range(chips):
                    send_buf[half, q] = acc[half, owned_rows(q, 1 - c), :].astype(BF16)
                to_sibling(half).start()

        @pl.when(jnp.logical_and(k == nk - 1, j == halves - 1))
        def _():
            for half in range(halves):
                to_sibling(half).wait_send()
                to_sibling(half).wait_recv()
                for q in range(chips):
                    o_ref[q, :, half * nb:(half + 1) * nb] = (
                        acc[half, owned_rows(q, c), :] + recv_buf[half, q].astype(F32)).astype(BF16)

    (partial,), arrived = _call(
        body,
        name="weight_grad",
        grid=(nk, halves),
        in_specs=[pl.BlockSpec((tk, m), lambda k, j: (k, 0)), pl.BlockSpec((tk, nb), lambda k, j: (k, j))],
        out_specs=[pl.BlockSpec((chips, r, n), lambda k, j: (0, 0, 0))],
        out_shape=[jax.ShapeDtypeStruct((chips, r, n), BF16)],
        scratch_shapes=[
            pltpu.VMEM((halves, m, nb), F32),
            pltpu.VMEM((halves, chips, r, nb), BF16), pltpu.VMEM((halves, chips, r, nb), BF16),
            pltpu.SemaphoreType.DMA((halves,)), pltpu.SemaphoreType.DMA((halves,)),
        ],
        args=(a, b),
        exchange=exchange,
    )
    return partial, arrived


def _chunk_cumsum(v, reverse=False):
    n, width = v.shape
    row = lax.broadcasted_iota(jnp.int32, (n, n), 0)
    col = lax.broadcasted_iota(jnp.int32, (n, n), 1)
    earlier = col >= row if reverse else col <= row
    tri = jnp.where(jnp.logical_and(row // CHUNK == col // CHUNK, earlier), 1.0, 0.0).astype(BF16)
    hi = v.astype(BF16)
    rest = v - hi.astype(F32)
    mid = rest.astype(BF16)
    low = (rest - mid.astype(F32)).astype(BF16)
    sums = _dot(tri, jnp.concatenate([hi, mid, low], axis=1))
    return sums[:, 0:width] + sums[:, width:2 * width] + sums[:, 2 * width:3 * width]


def _shift_rows(v, shift, edge):
    n = v.shape[0]
    row = lax.broadcasted_iota(jnp.int32, (n, 1), 0)
    out = pltpu.roll(v, shift % n, axis=0)
    if shift > 0:
        for j in range(shift):
            out = jnp.where(row == j, edge[8 - shift + j:8 - shift + j + 1, :], out)
    else:
        for j in range(-shift):
            out = jnp.where(row == n + shift + j, edge[j:j + 1, :], out)
    return out


def _gates(z, lbp):
    w = HGRN_W
    lb = _sigmoid(lbp[0:1, :] - lbp[1:2, :])
    zq = z[:, 0:w]
    sig = _sigmoid(z[:, w:2 * w])
    f = lb + (1.0 - lb) * sig
    sq = _sigmoid(zq)
    q = zq * sq * HGRN_DK ** -0.5
    return lb, sig, f, sq, q


def _decayed_operands(q, f, v, qh_buf, kh_buf, kbar_buf, v_buf, etot_buf):
    n, width = f.shape
    bcum = _chunk_cumsum(jnp.log(f))
    total = jnp.concatenate(
        [jnp.broadcast_to(bcum[c + CHUNK - 1:c + CHUNK, :], (CHUNK, width)) for c in range(0, n, CHUNK)], axis=0)
    eb, enb, erest = jnp.exp(bcum), jnp.exp(-bcum), jnp.exp(total - bcum)
    kk = 1.0 - f
    qh_buf[...] = (q * eb).astype(BF16)
    kh_buf[...] = (kk * enb).astype(BF16)
    kbar_buf[...] = (kk * erest).astype(BF16)
    v_buf[...] = v.astype(BF16)
    etot_buf[...] = jnp.exp(total)
    return eb, enb, erest


def _short_conv(u, edge, cw):
    return cw[0:1, :] * _shift_rows(u, 2, edge) + cw[1:2, :] * _shift_rows(u, 1, edge) + cw[2:3, :] * u


def _block_causal_mask(n):
    row = lax.broadcasted_iota(jnp.int32, (n, n), 0)
    col = lax.broadcasted_iota(jnp.int32, (n, n), 1)
    return jnp.logical_and(row // CHUNK == col // CHUNK, col <= row)


def _spread(v, chunk_of_row, nc):
    return jnp.concatenate([jnp.where(chunk_of_row == c, v, jnp.zeros_like(v)) for c in range(nc)], axis=1)


def _pick(r, chunk_of_row, nc):
    out = jnp.where(chunk_of_row == 0, r[:, 0:HGRN_DK], 0.0)
    for c in range(1, nc):
        out = out + jnp.where(chunk_of_row == c, r[:, c * HGRN_DK:(c + 1) * HGRN_DK], 0.0)
    return out


def _mix_fwd(x, g, w_in, lbp, gh, convw_t, w_out, exchange=None):
    t, d = x.shape
    zw = w_in.shape[0]
    w = HGRN_W
    tm = min(TOKEN_TILE, t)
    nc = tm // CHUNK
    n_chunks = t // CHUNK

    def body(x_ref, g_ref, win_ref, lbp_ref, gh_ref, cw_ref, wout_ref,
             xo_ref, z_ref, o_ref, st_ref, y_ref, state, ucarry, qh_buf, kh_buf, kbar_buf, v_buf, etot_buf):
        _zero_at_start(state, ucarry)
        xv = x_ref[...]
        h, _, _ = _rms(xv, g_ref[...])
        z_ref[...] = _dot(h.astype(BF16), win_ref[...], NT)
        z = z_ref[...]
        _, _, f, _, q = _gates(z, lbp_ref[...])
        _decayed_operands(q, f, z[:, 2 * w:3 * w], qh_buf, kh_buf, kbar_buf, v_buf, etot_buf)
        mask = _block_causal_mask(tm)
        chunk_of_row = lax.broadcasted_iota(jnp.int32, (tm, 1), 0) // CHUNK
        heads = range(HGRN_HEADS)
        hcols = [slice(hd * HGRN_DK, (hd + 1) * HGRN_DK) for hd in heads]
        qh = [qh_buf[:, hcols[hd]] for hd in heads]
        vb = [v_buf[:, hcols[hd]] for hd in heads]
        scores = [jnp.where(mask, _dot(qh[hd], kh_buf[:, hcols[hd]], NT), 0.0).astype(BF16) for hd in heads]
        gains = [_dot(_spread(vb[hd], chunk_of_row, nc), kbar_buf[:, hcols[hd]], TN) for hd in heads]
        entering = []
        for hd in heads:
            states, st = [], state[hd]
            for c in range(nc):
                states.append(st)
                st_ref[c, hd] = st
                st = st * etot_buf[c * CHUNK:c * CHUNK + 1, hcols[hd]] + gains[hd][c * HGRN_DK:(c + 1) * HGRN_DK, :]
            state[hd] = st
            entering.append(jnp.concatenate(states, axis=0).astype(BF16))
        from_states = [_dot(qh[hd], entering[hd], NT) for hd in heads]
        o_heads = [_dot(scores[hd], vb[hd]) + _pick(from_states[hd], chunk_of_row, nc) for hd in heads]
        o_ref[...] = jnp.concatenate(o_heads, axis=1)
        ghv = gh_ref[...]
        normed = jnp.concatenate([_rms(o_heads[hd], ghv[:, hcols[hd]])[0] for hd in heads], axis=1)
        zg = z[:, 3 * w:4 * w]
        u = z[:, 5 * w:6 * w] * z[:, 6 * w:7 * w]
        conv = _short_conv(u, ucarry[...], cw_ref[...])
        ucarry[...] = u[tm - 8:tm, :]
        y = jnp.concatenate([normed * (zg * _sigmoid(zg)), z[:, 4 * w:5 * w] * conv], axis=1).astype(BF16)
        y_ref[...] = y
        xo_ref[...] = xv + _dot(y, wout_ref[...])

    return _call(
        body,
        name="mix_fwd",
        grid=(t // tm,),
        in_specs=[
            _rows(tm, d), _full((1, d)), _full((zw, d)), _full((2, w)), _full((1, w)), _full((3, w)),
            _full((2 * w, d)),
        ],
        out_specs=[
            _rows(tm, d), _rows(tm, zw), _rows(tm, w),
            pl.BlockSpec((nc, HGRN_HEADS, HGRN_DK, HGRN_DK), lambda i: (i, 0, 0, 0)),
            _rows(tm, 2 * w),
        ],
        out_shape=[
            jax.ShapeDtypeStruct((t, d), F32),
            jax.ShapeDtypeStruct((t, zw), F32),
            jax.ShapeDtypeStruct((t, w), F32),
            jax.ShapeDtypeStruct((n_chunks, HGRN_HEADS, HGRN_DK, HGRN_DK), F32),
            jax.ShapeDtypeStruct((t, 2 * w), BF16),
        ],
        scratch_shapes=[
            pltpu.VMEM((HGRN_HEADS, HGRN_DK, HGRN_DK), F32), pltpu.VMEM((8, w), F32),
            pltpu.VMEM((tm, w), BF16), pltpu.VMEM((tm, w), BF16), pltpu.VMEM((tm, w), BF16),
            pltpu.VMEM((tm, w), BF16), pltpu.VMEM((tm, w), F32),
        ],
        args=(x, g, w_in, lbp, gh, convw_t, w_out),
        exchange=exchange,
    )


def _mix_bwd(x, g, dxo, z, o, states, w_in, lbp, gh, convw_t, w_out, exchange=None):
    t, d = x.shape
    zw = w_in.shape[0]
    w = HGRN_W
    tm = min(TOKEN_TILE, t)
    nc = tm // CHUNK
    n = t // tm

    def body(x_ref, g_ref, dxo_ref, z_ref, zprev_ref, o_ref, st_ref, win_ref, lbp_ref, gh_ref, cw_ref, wout_ref,
             dx_ref, dz_ref, h_ref, dg_ref, dlbp_ref, dgh_ref, dcw_ref,
             dstate, dcarry, do_buf, qh_buf, kh_buf, kbar_buf, v_buf, etot_buf):
        _zero_at_start(dstate, dcarry, dg_ref, dlbp_ref, dgh_ref, dcw_ref)
        gv = g_ref[...]
        h, xh, r = _rms(x_ref[...], gv)
        h_ref[...] = h.astype(BF16)
        dxo = dxo_ref[...]
        dy = _dot(dxo.astype(BF16), wout_ref[...], NT)
        z = z_ref[...]
        lb, sig, f, sq, q = _gates(z, lbp_ref[...])
        eb, enb, erest = _decayed_operands(q, f, z[:, 2 * w:3 * w], qh_buf, kh_buf, kbar_buf, v_buf, etot_buf)

        ghv = gh_ref[...]
        zg = z[:, 3 * w:4 * w]
        sgz = _sigmoid(zg)
        dyh = dy[:, 0:w]
        don = dyh * (zg * sgz)
        heads = range(HGRN_HEADS)
        hcols = [slice(hd * HGRN_DK, (hd + 1) * HGRN_DK) for hd in heads]
        norms = [_rms(o_ref[:, hcols[hd]], ghv[:, hcols[hd]]) for hd in heads]
        on = jnp.concatenate([norms[hd][0] for hd in heads], axis=1)
        oh = jnp.concatenate([norms[hd][1] for hd in heads], axis=1)
        dz_ref[:, 3 * w:4 * w] = (dyh * on * (sgz * (1.0 + zg * (1.0 - sgz)))).astype(BF16)
        dgh_ref[...] += jnp.sum(don * oh, axis=0, keepdims=True)
        do_buf[...] = jnp.concatenate(
            [_rms_bwd(don[:, hcols[hd]], norms[hd][1], norms[hd][2], ghv[:, hcols[hd]]) for hd in heads],
            axis=1).astype(BF16)

        zb = z[:, 4 * w:5 * w]
        zc = z[:, 5 * w:6 * w]
        zu = z[:, 6 * w:7 * w]
        u = zc * zu
        cw = cw_ref[...]
        zp = zprev_ref[...]
        uprev = jnp.where(pl.program_id(0) == n - 1, 0.0, zp[:, 5 * w:6 * w] * zp[:, 6 * w:7 * w])
        dyc = dy[:, w:2 * w]
        dz_ref[:, 4 * w:5 * w] = (dyc * _short_conv(u, uprev, cw)).astype(BF16)
        dconv = dyc * zb
        edge = dcarry[...]
        dconv1 = _shift_rows(dconv, -1, edge)
        dconv2 = _shift_rows(dconv, -2, edge)
        dcarry[...] = dconv[0:8, :]
        du = cw[2:3, :] * dconv + cw[1:2, :] * dconv1 + cw[0:1, :] * dconv2
        dz_ref[:, 5 * w:6 * w] = (du * zu).astype(BF16)
        dz_ref[:, 6 * w:7 * w] = (du * zc).astype(BF16)
        dcw_ref[...] += jnp.concatenate([
            jnp.sum(u * dconv2, axis=0, keepdims=True),
            jnp.sum(u * dconv1, axis=0, keepdims=True),
            jnp.sum(u * dconv, axis=0, keepdims=True)], axis=0)

        mask = _block_causal_mask(tm)
        chunk_of_row = lax.broadcasted_iota(jnp.int32, (tm, 1), 0) // CHUNK
        heads = range(HGRN_HEADS)
        hcols = [slice(hd * HGRN_DK, (hd + 1) * HGRN_DK) for hd in heads]
        qhb = [qh_buf[:, hcols[hd]] for hd in heads]
        khb = [kh_buf[:, hcols[hd]] for hd in heads]
        vb = [v_buf[:, hcols[hd]] for hd in heads]
        dob = [do_buf[:, hcols[hd]] for hd in heads]
        scores = [jnp.where(mask, _dot(qhb[hd], khb[hd], NT), 0.0).astype(BF16) for hd in heads]
        dscores = [jnp.where(mask, _dot(dob[hd], vb[hd], NT), 0.0).astype(BF16) for hd in heads]
        gains = [_dot(_spread(dob[hd], chunk_of_row, nc), qhb[hd], TN) for hd in heads]
        dst_rows, dst_lanes, st_lanes, carries = [], [], [], []
        for hd in heads:
            entering = [st_ref[c, hd] for c in range(nc)]
            leaving, carried_back = [None] * nc, [None] * nc
            dst = dstate[hd]
            for c in reversed(range(nc)):
                elast = etot_buf[c * CHUNK:c * CHUNK + 1, hcols[hd]]
                leaving[c] = dst
                carried_back[c] = jnp.sum(dst * entering[c], axis=0, keepdims=True) * elast
                dst = dst * elast + gains[hd][c * HGRN_DK:(c + 1) * HGRN_DK, :]
            dstate[hd] = dst
            dst_rows.append(jnp.concatenate(leaving, axis=0).astype(BF16))
            dst_lanes.append(jnp.concatenate(leaving, axis=1).astype(BF16))
            st_lanes.append(jnp.concatenate(entering, axis=1).astype(BF16))
            carries.append(carried_back)
        dv = [_dot(scores[hd], dob[hd], TN) + _pick(_dot(kbar_buf[:, hcols[hd]], dst_rows[hd], NT), chunk_of_row, nc)
              for hd in heads]
        dz_ref[:, 2 * w:3 * w] = jnp.concatenate(dv, axis=1).astype(BF16)
        dqh = jnp.concatenate(
            [_dot(dscores[hd], khb[hd]) + _pick(_dot(dob[hd], st_lanes[hd]), chunk_of_row, nc) for hd in heads], axis=1)
        dkh = jnp.concatenate([_dot(dscores[hd], qhb[hd], TN) for hd in heads], axis=1)
        dkbar = jnp.concatenate([_pick(_dot(vb[hd], dst_lanes[hd]), chunk_of_row, nc) for hd in heads], axis=1)

        kbar_dkbar = kbar_buf[...].astype(F32) * dkbar
        db = qh_buf[...].astype(F32) * dqh - kh_buf[...].astype(F32) * dkh - kbar_dkbar
        through_last = jnp.concatenate([
            jnp.broadcast_to(
                jnp.sum(kbar_dkbar[c * CHUNK:(c + 1) * CHUNK], axis=0, keepdims=True)
                + jnp.concatenate([carries[hd][c] for hd in heads], axis=1),
                (CHUNK, w))
            for c in range(nc)], axis=0)
        dlogf = _chunk_cumsum(db, reverse=True) + through_last
        df = dlogf / f - (dkh * enb + dkbar * erest)
        zq = z[:, 0:w]
        dz_ref[:, 0:w] = (dqh * eb * HGRN_DK ** -0.5 * (sq * (1.0 + zq * (1.0 - sq)))).astype(BF16)
        dz_ref[:, w:2 * w] = (df * (1.0 - lb) * sig * (1.0 - sig)).astype(BF16)
        dlb = jnp.sum(df * (1.0 - sig), axis=0, keepdims=True) * lb * (1.0 - lb)
        dlbp_ref[...] += jnp.concatenate([dlb, -dlb], axis=0)

        dh = _dot(dz_ref[...], win_ref[...])
        dx_ref[...] = _rms_bwd(dh, xh, r, gv) + dxo
        dg_ref[...] += jnp.sum(dh * xh, axis=0, keepdims=True)

    return _call(
        body,
        name="mix_bwd",
        grid=(n,),
        in_specs=[
            _rows_rev(tm, d, n), _full((1, d)), _rows_rev(tm, d, n), _rows_rev(tm, zw, n),
            pl.BlockSpec((8, zw), lambda i: (jnp.maximum((n - 1 - i) * (tm // 8) - 1, 0), 0)),
            _rows_rev(tm, w, n),
            pl.BlockSpec((nc, HGRN_HEADS, HGRN_DK, HGRN_DK), lambda i: (n - 1 - i, 0, 0, 0)),
            _full((zw, d)), _full((2, w)), _full((1, w)), _full((3, w)), _full((2 * w, d)),
        ],
        out_specs=[
            _rows_rev(tm, d, n), _rows_rev(tm, zw, n), _rows_rev(tm, d, n),
            _full((1, d)), _full((2, w)), _full((1, w)), _full((3, w)),
        ],
        out_shape=[
            jax.ShapeDtypeStruct((t, d), F32),
            jax.ShapeDtypeStruct((t, zw), BF16),
            jax.ShapeDtypeStruct((t, d), BF16),
            jax.ShapeDtypeStruct((1, d), F32),
            jax.ShapeDtypeStruct((2, w), F32),
            jax.ShapeDtypeStruct((1, w), F32),
            jax.ShapeDtypeStruct((3, w), F32),
        ],
        scratch_shapes=[
            pltpu.VMEM((HGRN_HEADS, HGRN_DK, HGRN_DK), F32), pltpu.VMEM((8, w), F32),
            pltpu.VMEM((tm, w), BF16),
            pltpu.VMEM((tm, w), BF16), pltpu.VMEM((tm, w), BF16), pltpu.VMEM((tm, w), BF16),
            pltpu.VMEM((tm, w), BF16), pltpu.VMEM((tm, w), F32),
        ],
        args=(x, g, dxo, z, z, o, states, w_in, lbp, gh, convw_t, w_out),
        exchange=exchange,
    )


def _memkv_fwd(mem, g, wkv):
    m, d = mem.shape
    nb, _, cb = wkv.shape

    def body(mem_ref, g_ref, wkv_ref, kv_ref):
        mn, _, _ = _rms(mem_ref[...], g_ref[...])
        mnb = mn.astype(BF16)
        for j in range(nb):
            kv_ref[:, j * cb:(j + 1) * cb] = _dot(mnb, wkv_ref[j]).astype(BF16)

    return pl.pallas_call(
        body,
        name="memkv_fwd",
        out_shape=jax.ShapeDtypeStruct((m, nb * cb), BF16),
        compiler_params=_params(),
    )(mem, g, wkv)


def _memkv_bwd(mem, g, dkv, wkv):
    m, d = mem.shape
    nb, _, cb = wkv.shape
    chips = nb // 2

    def body(mem_ref, g_ref, dkv_ref, wkv_ref, dw_ref, dg_ref, dw_all, send_buf, recv_buf, send_sem, recv_sem):
        x, y, c, _ = _mesh_place()
        sibling, _ = _peer(x, y, c, 1)
        mn, xh, _ = _rms(mem_ref[...], g_ref[...])
        mnb = mn.astype(BF16)
        dmn = jnp.zeros((m, d), F32)
        for j in range(nb):
            dkvb = dkv_ref[:, j * cb:(j + 1) * cb].astype(BF16)
            dw_all[j] = _dot(mnb, dkvb, TN)
            dmn = dmn + _dot(dkvb, wkv_ref[j], NT)
        dg_ref[...] = jnp.sum(dmn * xh, axis=0, keepdims=True)
        for q in range(chips):
            send_buf[q] = dw_all[2 * q + 1 - c].astype(BF16)
        to_sibling = _remote(send_buf, recv_buf, send_sem, recv_sem, sibling)
        to_sibling.start()
        to_sibling.wait_send()
        to_sibling.wait_recv()
        for q in range(chips):
            dw_ref[q] = (dw_all[2 * q + c] + recv_buf[q].astype(F32)).astype(BF16)

    return pl.pallas_call(
        body,
        name="memkv_bwd",
        out_shape=[jax.ShapeDtypeStruct((chips, d, cb), BF16), jax.ShapeDtypeStruct((1, d), F32)],
        scratch_shapes=[
            pltpu.VMEM((nb, d, cb), F32), pltpu.VMEM((chips, d, cb), BF16), pltpu.VMEM((chips, d, cb), BF16),
            pltpu.SemaphoreType.DMA, pltpu.SemaphoreType.DMA,
        ],
        compiler_params=_params(),
    )(mem, g, dkv, wkv)


def _softmax_rows(qm_h, k_h):
    sc = _dot(qm_h, k_h, NT) * MEM_HD ** -0.5
    e = jnp.exp(sc - jnp.max(sc, axis=-1, keepdims=True))
    return e / jnp.sum(e, axis=-1, keepdims=True)


def _xattn_fwd(x, g, wq, kv, wo, exchange=None):
    t, d = x.shape
    m = kv.shape[0]
    tm = min(XATTN_TILE, t)

    def body(x_ref, g_ref, wq_ref, kv_ref, wo_ref, xo_ref, hq_ref, qm_ref, att_ref):
        xv = x_ref[...]
        h, _, _ = _rms(xv, g_ref[...])
        hb = h.astype(BF16)
        hq_ref[...] = hb
        qm = _dot(hb, wq_ref[...]).astype(BF16)
        qm_ref[...] = qm
        heads = range(MEM_HEADS)
        kcols = [slice(hd * MEM_HD, (hd + 1) * MEM_HD) for hd in heads]
        p = [_softmax_rows(qm[:, kcols[hd]], kv_ref[:, kcols[hd]]) for hd in heads]
        att = jnp.concatenate(
            [_dot(p[hd].astype(BF16), kv_ref[:, d + hd * MEM_HD:d + (hd + 1) * MEM_HD]) for hd in heads],
            axis=1).astype(BF16)
        att_ref[...] = att
        xo_ref[...] = xv + _dot(att, wo_ref[...])

    return _call(
        body,
        name="xattn_fwd",
        grid=(t // tm,),
        in_specs=[_rows(tm, d), _full((1, d)), _full((d, d)), _full((m, 2 * d)), _full((d, d))],
        out_specs=[_rows(tm, d), _rows(tm, d), _rows(tm, d), _rows(tm, d)],
        out_shape=[
            jax.ShapeDtypeStruct((t, d), F32),
            jax.ShapeDtypeStruct((t, d), BF16),
            jax.ShapeDtypeStruct((t, d), BF16),
            jax.ShapeDtypeStruct((t, d), BF16),
        ],
        args=(x, g, wq, kv, wo),
        exchange=exchange,
    )


def _xattn_bwd(x, g, dxo, qm, kv, wq, wo, exchange=None):
    t, d = x.shape
    m = kv.shape[0]
    tm = min(XATTN_TILE, t)

    def body(x_ref, g_ref, dxo_ref, qm_ref, kv_ref, wq_ref, wo_ref, dx_ref, dqm_ref, dkv_ref, dg_ref):
        _zero_at_start(dkv_ref, dg_ref)
        gv = g_ref[...]
        _, xh, r = _rms(x_ref[...], gv)
        dxo = dxo_ref[...]
        datt = _dot(dxo.astype(BF16), wo_ref[...], NT).astype(BF16)
        heads = range(MEM_HEADS)
        kcols = [slice(hd * MEM_HD, (hd + 1) * MEM_HD) for hd in heads]
        vcols = [slice(d + hd * MEM_HD, d + (hd + 1) * MEM_HD) for hd in heads]
        qm_h = [qm_ref[:, kcols[hd]] for hd in heads]
        p = [_softmax_rows(qm_h[hd], kv_ref[:, kcols[hd]]) for hd in heads]
        dp = [_dot(datt[:, kcols[hd]], kv_ref[:, vcols[hd]], NT) for hd in heads]
        dsc = [(p[hd] * (dp[hd] - jnp.sum(p[hd] * dp[hd], axis=-1, keepdims=True)) * MEM_HD ** -0.5).astype(BF16)
               for hd in heads]
        dqm = jnp.concatenate([_dot(dsc[hd], kv_ref[:, kcols[hd]]) for hd in heads], axis=1).astype(BF16)
        dqm_ref[...] = dqm
        dkv_ref[...] += jnp.concatenate(
            [_dot(dsc[hd], qm_h[hd], TN) for hd in heads]
            + [_dot(p[hd].astype(BF16), datt[:, kcols[hd]], TN) for hd in heads], axis=1)
        dh = _dot(dqm, wq_ref[...], NT)
        dx_ref[...] = _rms_bwd(dh, xh, r, gv) + dxo
        dg_ref[...] += jnp.sum(dh * xh, axis=0, keepdims=True)

    return _call(
        body,
        name="xattn_bwd",
        grid=(t // tm,),
        in_specs=[
            _rows(tm, d), _full((1, d)), _rows(tm, d), _rows(tm, d), _full((m, 2 * d)), _full((d, d)), _full((d, d)),
        ],
        out_specs=[_rows(tm, d), _rows(tm, d), _full((m, 2 * d)), _full((1, d))],
        out_shape=[
            jax.ShapeDtypeStruct((t, d), F32),
            jax.ShapeDtypeStruct((t, d), BF16),
            jax.ShapeDtypeStruct((m, 2 * d), F32),
            jax.ShapeDtypeStruct((1, d), F32),
        ],
        args=(x, g, dxo, qm, kv, wq, wo),
        exchange=exchange,
    )


def _mesh_place():
    x, y, c = lax.axis_index("x"), lax.axis_index("y"), lax.axis_index("c")
    return x, y, c, 4 * x + 2 * y + c


def _peer(x, y, c, k):
    px = 1 - x if k & 4 else x
    py = 1 - y if k & 2 else y
    pc = 1 - c if k & 1 else c
    return (px, py, pc), 4 * px + 2 * py + pc


ICI_HOPS = (2, 4, 6)
N_HOPS = len(ICI_HOPS)


def _remote(src, dst, send_sem, recv_sem, peer):
    return pltpu.make_async_remote_copy(
        src_ref=src, dst_ref=dst, send_sem=send_sem, recv_sem=recv_sem, device_id=peer, device_id_type=MESH_IDS)


def _gather_exchange(shards):
    n = len(shards)

    def place():
        x, y, c, me = _mesh_place()
        sibling, _ = _peer(x, y, c, 1)
        to_x, from_x = _peer(x, y, c, 4)
        to_y, from_y = _peer(x, y, c, 2)
        _, from_diagonal = _peer(x, y, c, 6)
        onward = (c * to_y[0] + (1 - c) * to_x[0], c * to_y[1] + (1 - c) * to_x[1], c)
        passed_on = c * from_x + (1 - c) * from_y
        return me, sibling, (to_x, to_y, onward), (from_x, from_y, from_diagonal), passed_on

    def start(src, dst, sems):
        ici_send, ici_recv, pair_send, pair_recv, local = sems
        me, sibling, targets, _, _ = place()
        for a in range(n):
            pltpu.make_async_copy(src[a], dst[a].at[me], local.at[a]).start()
            for j in range(2):
                _remote(src[a], dst[a].at[me], ici_send.at[a, j], ici_recv.at[a, j], targets[j]).start()
            _remote(src[a], dst[a].at[me], pair_send.at[a, 0], pair_recv.at[a, 0], sibling).start()

    def to_sibling(dst, sems, a, j, origin, sibling):
        _, _, pair_send, pair_recv, _ = sems
        slot = dst[a].at[origin]
        return _remote(slot, slot, pair_send.at[a, 1 + j], pair_recv.at[a, 1 + j], sibling)

    def middle(src, dst, sems):
        ici_send, ici_recv, _, _, _ = sems
        _, sibling, targets, origins, passed_on = place()
        for a in range(n):
            for j in range(2):
                _remote(src[a], dst[a].at[origins[j]], ici_send.at[a, j], ici_recv.at[a, j], targets[j]).wait_recv()
            slot = dst[a].at[passed_on]
            _remote(slot, slot, ici_send.at[a, 2], ici_recv.at[a, 2], targets[2]).start()
            for j in range(2):
                to_sibling(dst, sems, a, j, origins[j], sibling).start()

    def finish(src, dst, sems):
        ici_send, ici_recv, pair_send, pair_recv, local = sems
        me, sibling, targets, origins, _ = place()
        for a in range(n):
            _remote(src[a], dst[a].at[origins[2]], ici_send.at[a, 2], ici_recv.at[a, 2], targets[2]).wait_recv()
            to_sibling(dst, sems, a, 2, origins[2], sibling).start()
        for a in range(n):
            pltpu.make_async_copy(src[a], dst[a].at[me], local.at[a]).wait()
            for j in range(N_HOPS):
                _remote(src[a], dst[a].at[me], ici_send.at[a, j], ici_recv.at[a, j], targets[j]).wait_send()
            for j, origin in enumerate((me,) + origins):
                from_sibling = origin + 1 - 2 * (origin % 2)
                passed = _remote(src[a], dst[a].at[from_sibling], pair_send.at[a, j], pair_recv.at[a, j], sibling)
                passed.wait_send()
                passed.wait_recv()

    return _Exchange(
        shards,
        [jax.ShapeDtypeStruct((N_DEV,) + s.shape, s.dtype) for s in shards],
        [
            pltpu.SemaphoreType.DMA((n, N_HOPS)), pltpu.SemaphoreType.DMA((n, N_HOPS)),
            pltpu.SemaphoreType.DMA((n, N_HOPS + 1)), pltpu.SemaphoreType.DMA((n, N_HOPS + 1)),
            pltpu.SemaphoreType.DMA((n,)),
        ],
        start, finish, middle)


def _scatter_copies(src, dst, sems, n, arrivals=False):
    send, recv, local = sems
    x, y, c, _ = _mesh_place()
    chip = 2 * x + y
    if arrivals is None:
        return [pltpu.make_async_copy(src[a].at[chip], dst[a].at[chip], local.at[a]) for a in range(n)]
    copies = []
    for a in range(n):
        for j, k in enumerate(ICI_HOPS):
            peer, _ = _peer(x, y, c, k)
            peer_chip = 2 * peer[0] + peer[1]
            slot = dst[a].at[peer_chip if arrivals else chip]
            copies.append(_remote(src[a].at[peer_chip], slot, send.at[a, j], recv.at[a, j], peer))
    return copies


def _scatter_start(src, dst, sems, n):
    for cp in _scatter_copies(src, dst, sems, n, arrivals=None) + _scatter_copies(src, dst, sems, n):
        cp.start()


def _scatter_finish(src, dst, sems, n):
    for cp in _scatter_copies(src, dst, sems, n, arrivals=None):
        cp.wait()
    for cp in _scatter_copies(src, dst, sems, n):
        cp.wait_send()
    for cp in _scatter_copies(src, dst, sems, n, arrivals=True):
        cp.wait_recv()


def _scatter_scratch(n):
    return [pltpu.SemaphoreType.DMA((n, N_HOPS)), pltpu.SemaphoreType.DMA((n, N_HOPS)), pltpu.SemaphoreType.DMA((n,))]


def _scatter_exchange(partials):
    n = len(partials)
    return _Exchange(
        partials, [jax.ShapeDtypeStruct(p.shape, p.dtype) for p in partials], _scatter_scratch(n),
        lambda src, dst, sems: _scatter_start(src, dst, sems, n),
        lambda src, dst, sems: _scatter_finish(src, dst, sems, n))


SMALL_LAYOUT = {
    "ffn1_norm": (0, 1, 1024), "mix_norm": (1, 1, 1024), "xattn_norm": (2, 1, 1024), "mem_norm": (3, 1, 1024),
    "ffn2_norm": (4, 1, 1024), "final_norm": (5, 1, 1024), "lb_param": (6, 2, 512), "hgrn_out_norm": (8, 1, 512),
    "conv_w": (9, 3, 512), "loss": (12, 1, 128),
}


def _final_exchange(partials, small):
    n = len(partials)
    names = list(small)
    width = 1024

    def body(*refs):
        src = refs[:n]
        pieces = refs[n:n + len(names)]
        dst = refs[n + len(names):2 * n + len(names)]
        total_ref = refs[2 * n + len(names)]
        pack, gathered, small_send, small_recv = refs[2 * n + len(names) + 1:2 * n + len(names) + 5]
        sems = refs[2 * n + len(names) + 5:]
        x, y, c, me = _mesh_place()
        pack[...] = jnp.zeros_like(pack)
        for name, piece in zip(names, pieces):
            row, nrows, ncols = SMALL_LAYOUT[name]
            pack[row:row + nrows, 0:ncols] = piece[...]
        for k in range(1, N_DEV):
            peer, _ = _peer(x, y, c, k)
            _remote(pack, gathered.at[me], small_send.at[k - 1], small_recv.at[k - 1], peer).start()
        _scatter_start(src, dst, sems, n)
        gathered[me] = pack[...]
        for k in range(1, N_DEV):
            peer, peer_index = _peer(x, y, c, k)
            landed = _remote(pack, gathered.at[peer_index], small_send.at[k - 1], small_recv.at[k - 1], peer)
            landed.wait_send()
            landed.wait_recv()
        total = gathered[0]
        for j in range(1, N_DEV):
            total = total + gathered[j]
        total_ref[...] = total
        _scatter_finish(src, dst, sems, n)

    hbm = pl.BlockSpec(memory_space=pltpu.HBM)
    vmem = pl.BlockSpec(memory_space=pltpu.VMEM)
    out = pl.pallas_call(
        body,
        name="final_exchange",
        in_specs=[hbm] * n + [vmem] * len(names),
        out_specs=[hbm] * n + [vmem],
        out_shape=[jax.ShapeDtypeStruct(p.shape, p.dtype) for p in partials]
        + [jax.ShapeDtypeStruct((SMALL_ROWS, width), F32)],
        scratch_shapes=[
            pltpu.VMEM((SMALL_ROWS, width), F32), pltpu.VMEM((N_DEV, SMALL_ROWS, width), F32),
            pltpu.SemaphoreType.DMA((N_DEV - 1,)), pltpu.SemaphoreType.DMA((N_DEV - 1,)),
        ] + _scatter_scratch(n),
        compiler_params=pltpu.CompilerParams(has_side_effects=True),
    )(*partials, *[small[k] for k in names])
    return out[:n], out[n]


def _adamw_math(w, g, m, v):
    m = ADAM_B1 * m + (1.0 - ADAM_B1) * g
    v = ADAM_B2 * v + (1.0 - ADAM_B2) * (g * g)
    m_hat = m / (1.0 - ADAM_B1 ** ADAM_STEP)
    v_hat = v / (1.0 - ADAM_B2 ** ADAM_STEP)
    delta = -ADAM_LR * (m_hat / (jnp.sqrt(v_hat) + ADAM_EPS) + ADAM_WD * w)
    return delta, m, v


def _adamw_shard(parts, w, m, v):
    r, c = w.shape
    n_parts = parts.shape[0]
    tr = max(rows for rows in range(16, r + 1, 16) if r % rows == 0 and rows * c <= ADAMW_TILE_ELEMENTS)

    def body(p_ref, w_ref, m_ref, v_ref, g_ref, d_ref, mo_ref, vo_ref):
        g = p_ref[0].astype(F32)
        for j in range(1, n_parts):
            g = g + p_ref[j].astype(F32)
        delta, mn, vn = _adamw_math(w_ref[...], g, m_ref[...], v_ref[...])
        g_ref[...] = g
        d_ref[...] = delta
        mo_ref[...] = mn
        vo_ref[...] = vn

    tile = pl.BlockSpec((tr, c), lambda i: (i, 0))
    return pl.pallas_call(
        body,
        name="adamw_shard",
        grid=(r // tr,),
        in_specs=[pl.BlockSpec((n_parts, tr, c), lambda i: (0, i, 0)), tile, tile, tile],
        out_specs=[tile] * 4,
        out_shape=[jax.ShapeDtypeStruct((r, c), F32)] * 4,
        compiler_params=_params(("parallel",)),
    )(parts, w, m, v)


def _adamw_small(gs, ws, ms, vs):
    n = len(gs)

    def body(*refs):
        g_refs, w_refs, m_refs, v_refs = refs[:n], refs[n:2 * n], refs[2 * n:3 * n], refs[3 * n:4 * n]
        d_out, m_out, v_out = refs[4 * n:5 * n], refs[5 * n:6 * n], refs[6 * n:7 * n]
        for i in range(n):
            delta, mn, vn = _adamw_math(w_refs[i][...], g_refs[i][...], m_refs[i][...], v_refs[i][...])
            d_out[i][...] = delta
            m_out[i][...] = mn
            v_out[i][...] = vn

    shapes = [jax.ShapeDtypeStruct(w.shape, F32) for w in ws]
    out = pl.pallas_call(
        body,
        name="adamw_small",
        out_shape=shapes * 3,
        compiler_params=_params(),
    )(*gs, *ws, *ms, *vs)
    return out[:n], out[n:2 * n], out[2 * n:]


TRANSPOSED = ("ffn1_gate", "ffn1_up", "w_in", "ffn2_gate", "ffn2_up", "conv_w")
GROUP_FFN1 = ("ffn1_gate", "ffn1_up", "ffn1_down")
GROUP_MIX = ("w_in", "w_out")
GROUP_XATTN = ("w_q_mem", "w_kv_mem", "w_o_mem")
GROUP_FFN2 = ("ffn2_gate", "ffn2_up", "ffn2_down")
LARGE = GROUP_FFN1 + GROUP_MIX + GROUP_XATTN + GROUP_FFN2
SMALL = ("ffn1_norm", "mix_norm", "lb_param", "hgrn_out_norm", "conv_w", "xattn_norm", "mem_norm", "ffn2_norm",
         "final_norm")
WEIGHTS = ("ffn1_norm", "ffn1_gate", "ffn1_up", "ffn1_down", "mix_norm", "w_in", "lb_param", "hgrn_out_norm",
           "conv_w", "w_out", "xattn_norm", "mem_norm", "w_q_mem", "w_kv_mem", "w_o_mem", "ffn2_norm", "ffn2_gate",
           "ffn2_up", "ffn2_down", "final_norm")


def kernel(x, mem, ffn1_norm, ffn1_gate, ffn1_up, ffn1_down, mix_norm, w_in, lb_param, hgrn_out_norm, conv_w, w_out, xattn_norm, mem_norm, w_q_mem, w_kv_mem, w_o_mem, ffn2_norm, ffn2_gate, ffn2_up, ffn2_down, final_norm, loss_target, m_ffn1_norm, m_ffn1_gate, m_ffn1_up, m_ffn1_down, m_mix_norm, m_w_in, m_lb_param, m_hgrn_out_norm, m_conv_w, m_w_out, m_xattn_norm, m_mem_norm, m_w_q_mem, m_w_kv_mem, m_w_o_mem, m_ffn2_norm, m_ffn2_gate, m_ffn2_up, m_ffn2_down, m_final_norm, v_ffn1_norm, v_ffn1_gate, v_ffn1_up, v_ffn1_down, v_mix_norm, v_w_in, v_lb_param, v_hgrn_out_norm, v_conv_w, v_w_out, v_xattn_norm, v_mem_norm, v_w_q_mem, v_w_kv_mem, v_w_o_mem, v_ffn2_norm, v_ffn2_gate, v_ffn2_up, v_ffn2_down, v_final_norm):
    given = dict(locals())
    me = 4 * lax.axis_index("x") + 2 * lax.axis_index("y") + lax.axis_index("c")
    x0, memv, target = x[0], mem[0], loss_target[0]

    def shard(prefix, name):
        v = given[prefix + name]
        if v.ndim == 1:
            return v.reshape(1, -1)
        if v.ndim == 2:
            return v
        return v[0].T if name in TRANSPOSED else v[0]

    w = {name: shard("", name) for name in WEIGHTS}
    m = {name: shard("m_", name) for name in WEIGHTS}
    v = {name: shard("v_", name) for name in WEIGHTS}

    conv_taps, conv_rows = w["conv_w"].shape
    conv_tile = jnp.pad(w["conv_w"], ((0, 8 - conv_taps), (0, 128 - conv_rows)))
    wire = {name: w[name].astype(BF16) for name in LARGE}
    full = {}

    def landed(names, gathered):
        for name, blocks in zip(names, gathered):
            _, r, c = blocks.shape
            full[name] = blocks if name == "w_kv_mem" else blocks.reshape(N_DEV * r, c)

    first = ("ffn1_gate", "ffn1_up")
    landed(first, _run_exchange(_gather_exchange([wire[k] for k in first]), "gather_first"))

    riders = (("ffn1_down", "w_in"), ("w_out", "w_kv_mem"), ("w_q_mem", "w_o_mem", "ffn2_gate", "ffn2_up"),
              ("ffn2_down",))
    (a1, b1, s1), gathered = _ffn_up(
        x0, w["ffn1_norm"], full["ffn1_gate"], full["ffn1_up"],
        exchange=_gather_exchange([wire[k] for k in riders[0]]))
    landed(riders[0], gathered)
    (x1,), gathered = _ffn_down(
        x0, s1, full["ffn1_down"], exchange=_gather_exchange([wire[k] for k in riders[1]] + [conv_tile]))
    landed(riders[1], gathered)
    convw_t = gathered[-1][:, :conv_taps, :conv_rows].transpose(1, 0, 2).reshape(conv_taps, N_DEV * conv_rows)
    (x2, z, o_raw, states, ycat), gathered = _mix_fwd(
        x1, w["mix_norm"], full["w_in"], w["lb_param"], w["hgrn_out_norm"], convw_t, full["w_out"],
        exchange=_gather_exchange([wire[k] for k in riders[2]]))
    landed(riders[2], gathered)
    kv = _memkv_fwd(memv, w["mem_norm"], full["w_kv_mem"])
    (x3, hq, qm, att), gathered = _xattn_fwd(
        x2, w["xattn_norm"], full["w_q_mem"], kv, full["w_o_mem"],
        exchange=_gather_exchange([wire[k] for k in riders[3]]))
    landed(riders[3], gathered)
    (dx4, a2, b2, s2, loss_part, d_final), _ = _ffn_fwd(
        x3, w["ffn2_norm"], full["ffn2_gate"], full["ffn2_up"], full["ffn2_down"], head=(w["final_norm"], target))

    parts = {}
    waiting = []

    def carried():
        names = [name for name, _ in waiting]
        exchange = _scatter_exchange([p for _, p in waiting]) if waiting else None
        del waiting[:]
        return names, exchange

    def weight_grad(name, a, b, scale=1.0):
        names, exchange = carried()
        partial, arrived = _weight_grad(a, b, scale, exchange=exchange)
        parts.update(zip(names, arrived))
        waiting.append((name, partial))

    (dx3, da2, db2, h4, d_ffn2_norm), _ = _ffn_bwd(
        x3, w["ffn2_norm"], dx4, a2, b2, full["ffn2_gate"], full["ffn2_up"], full["ffn2_down"])
    weight_grad("ffn2_down", s2, dx4, 0.5)
    weight_grad("ffn2_gate", da2, h4)
    weight_grad("ffn2_up", db2, h4)
    names, exchange = carried()
    (dx2, dqm, dkv, d_xattn_norm), arrived = _xattn_bwd(
        x2, w["xattn_norm"], dx3, qm, kv, full["w_q_mem"], full["w_o_mem"], exchange=exchange)
    parts.update(zip(names, arrived))
    d_wkv, d_mem_norm = _memkv_bwd(memv, w["mem_norm"], dkv, full["w_kv_mem"])
    waiting.append(("w_kv_mem", d_wkv))
    names, exchange = carried()
    (dx1, dz, h2, d_mix_norm, d_lbp, d_gh, d_convw_t), arrived = _mix_bwd(
        x1, w["mix_norm"], dx2, z, o_raw, states, full["w_in"], w["lb_param"], w["hgrn_out_norm"], convw_t,
        full["w_out"], exchange=exchange)
    parts.update(zip(names, arrived))
    weight_grad("w_in", dz, h2)
    weight_grad("ffn1_down", s1, dx1, 0.5)
    (dx0, da1, db1, h1, d_ffn1_norm), _ = _ffn_bwd(
        x0, w["ffn1_norm"], dx1, a1, b1, full["ffn1_gate"], full["ffn1_up"], full["ffn1_down"])
    weight_grad("ffn1_gate", da1, h1)
    weight_grad("ffn1_up", db1, h1)
    weight_grad("w_o_mem", att, dx3)
    weight_grad("w_q_mem", hq, dqm)
    weight_grad("w_out", ycat, dx2)

    small_parts = {
        "ffn1_norm": d_ffn1_norm, "mix_norm": d_mix_norm, "xattn_norm": d_xattn_norm, "mem_norm": d_mem_norm,
        "ffn2_norm": d_ffn2_norm, "final_norm": d_final, "lb_param": d_lbp, "hgrn_out_norm": d_gh,
        "conv_w": d_convw_t, "loss": loss_part,
    }
    names = [name for name, _ in waiting]
    arrived, total = _final_exchange([p for _, p in waiting], small_parts)
    parts.update(zip(names, arrived))

    g_out, d_out, m_out, v_out = {}, {}, {}, {}
    for name in LARGE:
        g_out[name], d_out[name], m_out[name], v_out[name] = _adamw_shard(parts[name], w[name], m[name], v[name])
    g_small = {}
    for name in SMALL:
        row, nrows, ncols = SMALL_LAYOUT[name]
        g_small[name] = total[row:row + nrows, 0:ncols]
    g_small["conv_w"] = lax.dynamic_slice_in_dim(g_small["conv_w"], me * conv_rows, conv_rows, axis=1)
    ds, ms, vs = _adamw_small(
        [g_small[k] for k in SMALL], [w[k] for k in SMALL], [m[k] for k in SMALL], [v[k] for k in SMALL])
    for i, name in enumerate(SMALL):
        g_out[name], d_out[name], m_out[name], v_out[name] = g_small[name], ds[i], ms[i], vs[i]

    def shaped(value, name):
        return (value.T if name in TRANSPOSED else value).reshape(given[name].shape)

    loss = total[SMALL_LAYOUT["loss"][0], 0]
    outs = [loss, dx0.reshape(x.shape)]
    for group in (g_out, d_out, m_out, v_out):
        outs += [shaped(group[name], name) for name in WEIGHTS]
    return tuple(outs)
```

```python
import jax
import jax.numpy as jnp
from jax import lax
from jax.experimental import pallas as pl
from jax.experimental.pallas import tpu as pltpu

F32 = jnp.float32
BF16 = jnp.bfloat16
MESH_IDS = pl.DeviceIdType.MESH

N_DEV = 8
EPS = 1e-6
HGRN_HEADS = 4
HGRN_DK = 128
HGRN_W = 512
CHUNK = 64
MEM_HEADS = 4
MEM_HD = 256
ADAM_LR = 0.001
ADAM_B1 = 0.9
ADAM_B2 = 0.999
ADAM_EPS = 1e-08
ADAM_WD = 0.01
ADAM_STEP = 10

TOKEN_TILE = 256
XATTN_TILE = 512
REDUCE_TILE = 1024
ADAMW_TILE_ELEMENTS = 256 * 1024
MIDDLE_EIGHTHS = 5
EARLY_MIDDLE_EIGHTHS = 4
MXU_ROWS = 256
VMEM_LIMIT = 60 * 1024 * 1024
SMALL_ROWS = 16
NT = (((1,), (1,)), ((), ()))
TN = (((0,), (0,)), ((), ()))


def _params(sem=None):
    return pltpu.CompilerParams(dimension_semantics=sem, vmem_limit_bytes=VMEM_LIMIT)


def _dot(a, b, dims=None):
    if dims is None:
        return jnp.dot(a, b, preferred_element_type=F32)
    return lax.dot_general(a, b, dims, preferred_element_type=F32)


def _sigmoid(v):
    return 1.0 / (1.0 + jnp.exp(-v))


def _rms(x, g):
    r = lax.rsqrt(jnp.mean(x * x, axis=-1, keepdims=True) + EPS)
    xh = x * r
    return xh * g, xh, r


def _rms_bwd(dh, xh, r, g):
    dxh = dh * g
    return r * (dxh - xh * jnp.mean(dxh * xh, axis=-1, keepdims=True))


def _full(shape):
    return pl.BlockSpec(shape, lambda *_: (0,) * len(shape))


def _rows(tm, width):
    return pl.BlockSpec((tm, width), lambda i: (i, 0))


def _rows_rev(tm, width, n):
    return pl.BlockSpec((tm, width), lambda i: (n - 1 - i, 0))


def _zero_at_start(*refs):
    @pl.when(pl.program_id(0) == 0)
    def _():
        for ref in refs:
            ref[...] = jnp.zeros_like(ref)


class _Exchange:
    def __init__(self, operands, out_shapes, scratch, start, finish, middle=None, middle_eighths=MIDDLE_EIGHTHS):
        self.operands, self.out_shapes, self.scratch = list(operands), list(out_shapes), list(scratch)
        self.start, self.middle, self.finish, self.middle_eighths = start, middle, finish, middle_eighths


def _call(body, *, name, grid, in_specs, out_specs, out_shape, args, scratch_shapes=(), exchange=None):
    semantics = ("arbitrary",) * len(grid)
    if exchange is None:
        out = pl.pallas_call(
            body, name=name, grid=grid, in_specs=in_specs, out_specs=out_specs, out_shape=out_shape,
            scratch_shapes=list(scratch_shapes), compiler_params=_params(semantics))(*args)
        return out, []
    hbm = pl.BlockSpec(memory_space=pltpu.HBM)
    n_in, n_out, n_scr = len(in_specs), len(out_specs), len(scratch_shapes)
    e_in, e_out = len(exchange.operands), len(exchange.out_shapes)

    def carried(*refs):
        ins, rest = refs[:n_in], refs[n_in:]
        e_ins, rest = rest[:e_in], rest[e_in:]
        outs, rest = rest[:n_out], rest[n_out:]
        e_outs, rest = rest[:e_out], rest[e_out:]
        scr, e_scr = rest[:n_scr], rest[n_scr:]
        first = last = None
        for axis, size in enumerate(grid):
            at_start, at_end = pl.program_id(axis) == 0, pl.program_id(axis) == size - 1
            first = at_start if first is None else jnp.logical_and(first, at_start)
            last = at_end if last is None else jnp.logical_and(last, at_end)

        @pl.when(first)
        def _():
            exchange.start(e_ins, e_outs, e_scr)

        body(*ins, *outs, *scr)

        if exchange.middle is not None:
            assert len(grid) == 1

            @pl.when(pl.program_id(0) == (grid[0] * exchange.middle_eighths) // 8)
            def _():
                exchange.middle(e_ins, e_outs, e_scr)

        @pl.when(last)
        def _():
            exchange.finish(e_ins, e_outs, e_scr)

    out = pl.pallas_call(
        carried, name=name, grid=grid, in_specs=list(in_specs) + [hbm] * e_in,
        out_specs=list(out_specs) + [hbm] * e_out, out_shape=list(out_shape) + exchange.out_shapes,
        scratch_shapes=list(scratch_shapes) + exchange.scratch,
        compiler_params=pltpu.CompilerParams(
            dimension_semantics=semantics, vmem_limit_bytes=VMEM_LIMIT, has_side_effects=True),
    )(*args, *exchange.operands)
    return out[:n_out], out[n_out:]


def _run_exchange(exchange, name):
    hbm = pl.BlockSpec(memory_space=pltpu.HBM)
    e_in, e_out = len(exchange.operands), len(exchange.out_shapes)

    def body(*refs):
        e_ins, e_outs, e_scr = refs[:e_in], refs[e_in:e_in + e_out], refs[e_in + e_out:]
        exchange.start(e_ins, e_outs, e_scr)
        if exchange.middle is not None:
            exchange.middle(e_ins, e_outs, e_scr)
        exchange.finish(e_ins, e_outs, e_scr)

    return pl.pallas_call(
        body, name=name, in_specs=[hbm] * e_in, out_specs=[hbm] * e_out, out_shape=exchange.out_shapes,
        scratch_shapes=exchange.scratch, compiler_params=pltpu.CompilerParams(has_side_effects=True),
    )(*exchange.operands)


def _loss_head(xo, gf, tgt):
    d = xo.shape[1]
    y, xh, r = _rms(xo, gf)
    err = y - tgt
    dy = err * (1.0 / d)
    loss = 0.5 * jnp.sum(jnp.sum(err * err, axis=-1, keepdims=True) * (1.0 / d), axis=0, keepdims=True)
    return _rms_bwd(dy, xh, r, gf), loss, jnp.sum(dy * xh, axis=0, keepdims=True)


def _ffn_fwd(x, g, wg, wu, wd, exchange=None, head=None):
    t, d = x.shape
    f = wg.shape[0]
    tm = min(TOKEN_TILE, t)

    def body(x_ref, g_ref, wg_ref, wu_ref, wd_ref, *rest):
        if head is None:
            xo_ref, a_ref, b_ref, s_ref = rest
        else:
            gf_ref, tgt_ref, xo_ref, a_ref, b_ref, s_ref, loss_ref, dgf_ref = rest
            _zero_at_start(loss_ref, dgf_ref)
        xv = x_ref[...]
        h, _, _ = _rms(xv, g_ref[...])
        hb = h.astype(BF16)
        a = _dot(hb, wg_ref[...], NT)
        b = _dot(hb, wu_ref[...], NT)
        s = (a * _sigmoid(a) * b).astype(BF16)
        xo = xv + 0.5 * _dot(s, wd_ref[...])
        if head is None:
            xo_ref[...] = xo
        else:
            xo_ref[...], loss, dgf = _loss_head(xo, gf_ref[...], tgt_ref[...])
            loss_ref[...] += jnp.broadcast_to(loss, (1, 128))
            dgf_ref[...] += dgf
        a_ref[...] = a.astype(BF16)
        b_ref[...] = b.astype(BF16)
        s_ref[...] = s

    in_specs = [_rows(tm, d), _full((1, d)), _full((f, d)), _full((f, d)), _full((f, d))]
    out_specs = [_rows(tm, d), _rows(tm, f), _rows(tm, f), _rows(tm, f)]
    out_shape = [
        jax.ShapeDtypeStruct((t, d), F32),
        jax.ShapeDtypeStruct((t, f), BF16),
        jax.ShapeDtypeStruct((t, f), BF16),
        jax.ShapeDtypeStruct((t, f), BF16),
    ]
    args = (x, g, wg, wu, wd)
    if head is not None:
        in_specs += [_full((1, d)), _rows(tm, d)]
        out_specs += [_full((1, 128)), _full((1, d))]
        out_shape += [jax.ShapeDtypeStruct((1, 128), F32), jax.ShapeDtypeStruct((1, d), F32)]
        args += tuple(head)
    return _call(
        body, name="ffn_fwd", grid=(t // tm,), in_specs=in_specs, out_specs=out_specs, out_shape=out_shape,
        args=args, exchange=exchange)


def _ffn_up(x, g, wg, wu, exchange=None):
    t, d = x.shape
    f = wg.shape[0]
    tm = min(TOKEN_TILE, t)

    def body(x_ref, g_ref, wg_ref, wu_ref, a_ref, b_ref, s_ref):
        h, _, _ = _rms(x_ref[...], g_ref[...])
        hb = h.astype(BF16)
        a = _dot(hb, wg_ref[...], NT)
        b = _dot(hb, wu_ref[...], NT)
        a_ref[...] = a.astype(BF16)
        b_ref[...] = b.astype(BF16)
        s_ref[...] = (a * _sigmoid(a) * b).astype(BF16)

    return _call(
        body, name="ffn_up", grid=(t // tm,),
        in_specs=[_rows(tm, d), _full((1, d)), _full((f, d)), _full((f, d))],
        out_specs=[_rows(tm, f)] * 3, out_shape=[jax.ShapeDtypeStruct((t, f), BF16)] * 3,
        args=(x, g, wg, wu), exchange=exchange)


def _ffn_down(x, s, wd, exchange=None):
    t, d = x.shape
    f = wd.shape[0]
    tm = min(TOKEN_TILE, t)

    def body(x_ref, s_ref, wd_ref, xo_ref):
        xo_ref[...] = x_ref[...] + 0.5 * _dot(s_ref[...], wd_ref[...])

    return _call(
        body, name="ffn_down", grid=(t // tm,),
        in_specs=[_rows(tm, d), _rows(tm, f), _full((f, d))],
        out_specs=[_rows(tm, d)], out_shape=[jax.ShapeDtypeStruct((t, d), F32)],
        args=(x, s, wd), exchange=exchange)


def _ffn_bwd(x, g, dxo, a, b, wg, wu, wd, exchange=None):
    t, d = x.shape
    f = wg.shape[0]
    tm = min(TOKEN_TILE, t)

    def body(x_ref, g_ref, dxo_ref, a_ref, b_ref, wg_ref, wu_ref, wd_ref, dx_ref, da_ref, db_ref, h_ref, dg_ref):
        _zero_at_start(dg_ref)
        gv = g_ref[...]
        h, xh, r = _rms(x_ref[...], gv)
        dxo = dxo_ref[...]
        ds = _dot((0.5 * dxo).astype(BF16), wd_ref[...], NT)
        af = a_ref[...].astype(F32)
        bf = b_ref[...].astype(F32)
        sg = _sigmoid(af)
        da = (ds * bf * (sg * (1.0 + af * (1.0 - sg)))).astype(BF16)
        db = (ds * (af * sg)).astype(BF16)
        dh = _dot(da, wg_ref[...]) + _dot(db, wu_ref[...])
        dx_ref[...] = _rms_bwd(dh, xh, r, gv) + dxo
        da_ref[...] = da
        db_ref[...] = db
        h_ref[...] = h.astype(BF16)
        dg_ref[...] += jnp.sum(dh * xh, axis=0, keepdims=True)

    return _call(
        body,
        name="ffn_bwd",
        grid=(t // tm,),
        in_specs=[
            _rows(tm, d), _full((1, d)), _rows(tm, d), _rows(tm, f), _rows(tm, f),
            _full((f, d)), _full((f, d)), _full((f, d)),
        ],
        out_specs=[_rows(tm, d), _rows(tm, f), _rows(tm, f), _rows(tm, d), _full((1, d))],
        out_shape=[
            jax.ShapeDtypeStruct((t, d), F32),
            jax.ShapeDtypeStruct((t, f), BF16),
            jax.ShapeDtypeStruct((t, f), BF16),
            jax.ShapeDtypeStruct((t, d), BF16),
            jax.ShapeDtypeStruct((1, d), F32),
        ],
        args=(x, g, dxo, a, b, wg, wu, wd),
        exchange=exchange,
    )


def _weight_grad(a, b, scale=1.0, exchange=None):
    t, m = a.shape
    n = b.shape[1]
    chips = N_DEV // 2
    r = m // N_DEV
    tk = min(REDUCE_TILE, t)
    halves = 2
    nb = n // halves
    nk = t // tk

    def body(a_ref, b_ref, o_ref, acc, send_buf, recv_buf, send_sems, recv_sems):
        k, j = pl.program_id(0), pl.program_id(1)
        x, y, c, _ = _mesh_place()
        sibling, _ = _peer(x, y, c, 1)
        bv = b_ref[...]
        if scale != 1.0:
            bv = bv * scale
        bb = bv.astype(BF16)
        acc_half = acc.at[j]

        @pl.when(k == 0)
        def _():
            acc_half[...] = jnp.zeros_like(acc_half)

        for i in range(m // MXU_ROWS):
            rows = slice(i * MXU_ROWS, (i + 1) * MXU_ROWS)
            acc_half[rows, :] += _dot(a_ref[:, rows].astype(BF16), bb, TN)

        def to_sibling(half):
            return _remote(send_buf.at[half], recv_buf.at[half], send_sems.at[half], recv_sems.at[half], sibling)

        def owned_rows(q, core):
            return pl.ds(pl.multiple_of((2 * q + core) * r, 8), r)

        for half in range(halves):
            @pl.when(jnp.logical_and(k == nk - 1, j == half))
            def _():
                for q in range(chips):
                    send_buf[half, q] = acc[half, owned_rows(q, 1 - c), :].astype(BF16)
                to_sibling(half).start()

        @pl.when(jnp.logical_and(k == nk - 1, j == halves - 1))
        def _():
            for half in range(halves):
                to_sibling(half).wait_send()
                to_sibling(half).wait_recv()
                for q in range(chips):
                    o_ref[q, :, half * nb:(half + 1) * nb] = (
                        acc[half, owned_rows(q, c), :] + recv_buf[half, q].astype(F32)).astype(BF16)

    (partial,), arrived = _call(
        body,
        name="weight_grad",
        grid=(nk, halves),
        in_specs=[pl.BlockSpec((tk, m), lambda k, j: (k, 0)), pl.BlockSpec((tk, nb), lambda k, j: (k, j))],
        out_specs=[pl.BlockSpec((chips, r, n), lambda k, j: (0, 0, 0))],
        out_shape=[jax.ShapeDtypeStruct((chips, r, n), BF16)],
        scratch_shapes=[
            pltpu.VMEM((halves, m, nb), F32),
            pltpu.VMEM((halves, chips, r, nb), BF16), pltpu.VMEM((halves, chips, r, nb), BF16),
            pltpu.SemaphoreType.DMA((halves,)), pltpu.SemaphoreType.DMA((halves,)),
        ],
        args=(a, b),
        exchange=exchange,
    )
    return partial, arrived


def _chunk_cumsum(v, reverse=False):
    n, width = v.shape
    row = lax.broadcasted_iota(jnp.int32, (n, n), 0)
    col = lax.broadcasted_iota(jnp.int32, (n, n), 1)
    earlier = col >= row if reverse else col <= row
    tri = jnp.where(jnp.logical_and(row // CHUNK == col // CHUNK, earlier), 1.0, 0.0).astype(BF16)
    hi = v.astype(BF16)
    rest = v - hi.astype(F32)
    mid = rest.astype(BF16)
    low = (rest - mid.astype(F32)).astype(BF16)
    sums = _dot(tri, jnp.concatenate([hi, mid, low], axis=1))
    return sums[:, 0:width] + sums[:, width:2 * width] + sums[:, 2 * width:3 * width]


def _shift_rows(v, shift, edge):
    n = v.shape[0]
    row = lax.broadcasted_iota(jnp.int32, (n, 1), 0)
    out = pltpu.roll(v, shift % n, axis=0)
    if shift > 0:
        for j in range(shift):
            out = jnp.where(row == j, edge[8 - shift + j:8 - shift + j + 1, :], out)
    else:
        for j in range(-shift):
            out = jnp.where(row == n + shift + j, edge[j:j + 1, :], out)
    return out


def _gates(z, lbp):
    w = HGRN_W
    lb = _sigmoid(lbp[0:1, :] - lbp[1:2, :])
    zq = z[:, 0:w]
    sig = _sigmoid(z[:, w:2 * w])
    f = lb + (1.0 - lb) * sig
    sq = _sigmoid(zq)
    q = zq * sq * HGRN_DK ** -0.5
    return lb, sig, f, sq, q


def _decayed_operands(q, f, v, qh_buf, kh_buf, kbar_buf, v_buf, etot_buf):
    n, width = f.shape
    bcum = _chunk_cumsum(jnp.log(f))
    total = jnp.concatenate(
        [jnp.broadcast_to(bcum[c + CHUNK - 1:c + CHUNK, :], (CHUNK, width)) for c in range(0, n, CHUNK)], axis=0)
    eb, enb, erest = jnp.exp(bcum), jnp.exp(-bcum), jnp.exp(total - bcum)
    kk = 1.0 - f
    qh_buf[...] = (q * eb).astype(BF16)
    kh_buf[...] = (kk * enb).astype(BF16)
    kbar_buf[...] = (kk * erest).astype(BF16)
    v_buf[...] = v.astype(BF16)
    etot_buf[...] = jnp.exp(total)
    return eb, enb, erest


def _short_conv(u, edge, cw):
    return cw[0:1, :] * _shift_rows(u, 2, edge) + cw[1:2, :] * _shift_rows(u, 1, edge) + cw[2:3, :] * u


def _block_causal_mask(n):
    row = lax.broadcasted_iota(jnp.int32, (n, n), 0)
    col = lax.broadcasted_iota(jnp.int32, (n, n), 1)
    return jnp.logical_and(row // CHUNK == col // CHUNK, col <= row)


def _spread(v, chunk_of_row, nc):
    return jnp.concatenate([jnp.where(chunk_of_row == c, v, jnp.zeros_like(v)) for c in range(nc)], axis=1)


def _pick(r, chunk_of_row, nc):
    out = jnp.where(chunk_of_row == 0, r[:, 0:HGRN_DK], 0.0)
    for c in range(1, nc):
        out = out + jnp.where(chunk_of_row == c, r[:, c * HGRN_DK:(c + 1) * HGRN_DK], 0.0)
    return out


def _mix_fwd(x, g, w_in, lbp, gh, convw_t, w_out, exchange=None):
    t, d = x.shape
    zw = w_in.shape[0]
    w = HGRN_W
    tm = min(TOKEN_TILE, t)
    nc = tm // CHUNK
    n_chunks = t // CHUNK

    def body(x_ref, g_ref, win_ref, lbp_ref, gh_ref, cw_ref, wout_ref,
             xo_ref, z_ref, o_ref, st_ref, y_ref, state, ucarry, qh_buf, kh_buf, kbar_buf, v_buf, etot_buf):
        _zero_at_start(state, ucarry)
        xv = x_ref[...]
        h, _, _ = _rms(xv, g_ref[...])
        z_ref[...] = _dot(h.astype(BF16), win_ref[...], NT)
        z = z_ref[...]
        _, _, f, _, q = _gates(z, lbp_ref[...])
        _decayed_operands(q, f, z[:, 2 * w:3 * w], qh_buf, kh_buf, kbar_buf, v_buf, etot_buf)
        mask = _block_causal_mask(tm)
        chunk_of_row = lax.broadcasted_iota(jnp.int32, (tm, 1), 0) // CHUNK
        heads = range(HGRN_HEADS)
        hcols = [slice(hd * HGRN_DK, (hd + 1) * HGRN_DK) for hd in heads]
        qh = [qh_buf[:, hcols[hd]] for hd in heads]
        vb = [v_buf[:, hcols[hd]] for hd in heads]
        scores = [jnp.where(mask, _dot(qh[hd], kh_buf[:, hcols[hd]], NT), 0.0).astype(BF16) for hd in heads]
        gains = [_dot(_spread(vb[hd], chunk_of_row, nc), kbar_buf[:, hcols[hd]], TN) for hd in heads]
        entering = []
        for hd in heads:
            states, st = [], state[hd]
            for c in range(nc):
                states.append(st)
                st_ref[c, hd] = st
                st = st * etot_buf[c * CHUNK:c * CHUNK + 1, hcols[hd]] + gains[hd][c * HGRN_DK:(c + 1) * HGRN_DK, :]
            state[hd] = st
            entering.append(jnp.concatenate(states, axis=0).astype(BF16))
        from_states = [_dot(qh[hd], entering[hd], NT) for hd in heads]
        o_heads = [_dot(scores[hd], vb[hd]) + _pick(from_states[hd], chunk_of_row, nc) for hd in heads]
        o_ref[...] = jnp.concatenate(o_heads, axis=1)
        ghv = gh_ref[...]
        normed = jnp.concatenate([_rms(o_heads[hd], ghv[:, hcols[hd]])[0] for hd in heads], axis=1)
        zg = z[:, 3 * w:4 * w]
        u = z[:, 5 * w:6 * w] * z[:, 6 * w:7 * w]
        conv = _short_conv(u, ucarry[...], cw_ref[...])
        ucarry[...] = u[tm - 8:tm, :]
        y = jnp.concatenate([normed * (zg * _sigmoid(zg)), z[:, 4 * w:5 * w] * conv], axis=1).astype(BF16)
        y_ref[...] = y
        xo_ref[...] = xv + _dot(y, wout_ref[...])

    return _call(
        body,
        name="mix_fwd",
        grid=(t // tm,),
        in_specs=[
            _rows(tm, d), _full((1, d)), _full((zw, d)), _full((2, w)), _full((1, w)), _full((3, w)),
            _full((2 * w, d)),
        ],
        out_specs=[
            _rows(tm, d), _rows(tm, zw), _rows(tm, w),
            pl.BlockSpec((nc, HGRN_HEADS, HGRN_DK, HGRN_DK), lambda i: (i, 0, 0, 0)),
            _rows(tm, 2 * w),
        ],
        out_shape=[
            jax.ShapeDtypeStruct((t, d), F32),
            jax.ShapeDtypeStruct((t, zw), F32),
            jax.ShapeDtypeStruct((t, w), F32),
            jax.ShapeDtypeStruct((n_chunks, HGRN_HEADS, HGRN_DK, HGRN_DK), F32),
            jax.ShapeDtypeStruct((t, 2 * w), BF16),
        ],
        scratch_shapes=[
            pltpu.VMEM((HGRN_HEADS, HGRN_DK, HGRN_DK), F32), pltpu.VMEM((8, w), F32),
            pltpu.VMEM((tm, w), BF16), pltpu.VMEM((tm, w), BF16), pltpu.VMEM((tm, w), BF16),
            pltpu.VMEM((tm, w), BF16), pltpu.VMEM((tm, w), F32),
        ],
        args=(x, g, w_in, lbp, gh, convw_t, w_out),
        exchange=exchange,
    )


def _mix_bwd(x, g, dxo, z, o, states, w_in, lbp, gh, convw_t, w_out, exchange=None):
    t, d = x.shape
    zw = w_in.shape[0]
    w = HGRN_W
    tm = min(TOKEN_TILE, t)
    nc = tm // CHUNK
    n = t // tm

    def body(x_ref, g_ref, dxo_ref, z_ref, zprev_ref, o_ref, st_ref, win_ref, lbp_ref, gh_ref, cw_ref, wout_ref,
             dx_ref, dz_ref, h_ref, dg_ref, dlbp_ref, dgh_ref, dcw_ref,
             dstate, dcarry, do_buf, qh_buf, kh_buf, kbar_buf, v_buf, etot_buf):
        _zero_at_start(dstate, dcarry, dg_ref, dlbp_ref, dgh_ref, dcw_ref)
        gv = g_ref[...]
        h, xh, r = _rms(x_ref[...], gv)
        h_ref[...] = h.astype(BF16)
        dxo = dxo_ref[...]
        dy = _dot(dxo.astype(BF16), wout_ref[...], NT)
        z = z_ref[...]
        lb, sig, f, sq, q = _gates(z, lbp_ref[...])
        eb, enb, erest = _decayed_operands(q, f, z[:, 2 * w:3 * w], qh_buf, kh_buf, kbar_buf, v_buf, etot_buf)

        ghv = gh_ref[...]
        zg = z[:, 3 * w:4 * w]
        sgz = _sigmoid(zg)
        dyh = dy[:, 0:w]
        don = dyh * (zg * sgz)
        heads = range(HGRN_HEADS)
        hcols = [slice(hd * HGRN_DK, (hd + 1) * HGRN_DK) for hd in heads]
        norms = [_rms(o_ref[:, hcols[hd]], ghv[:, hcols[hd]]) for hd in heads]
        on = jnp.concatenate([norms[hd][0] for hd in heads], axis=1)
        oh = jnp.concatenate([norms[hd][1] for hd in heads], axis=1)
        dz_ref[:, 3 * w:4 * w] = (dyh * on * (sgz * (1.0 + zg * (1.0 - sgz)))).astype(BF16)
        dgh_ref[...] += jnp.sum(don * oh, axis=0, keepdims=True)
        do_buf[...] = jnp.concatenate(
            [_rms_bwd(don[:, hcols[hd]], norms[hd][1], norms[hd][2], ghv[:, hcols[hd]]) for hd in heads],
            axis=1).astype(BF16)

        zb = z[:, 4 * w:5 * w]
        zc = z[:, 5 * w:6 * w]
        zu = z[:, 6 * w:7 * w]
        u = zc * zu
        cw = cw_ref[...]
        zp = zprev_ref[...]
        uprev = jnp.where(pl.program_id(0) == n - 1, 0.0, zp[:, 5 * w:6 * w] * zp[:, 6 * w:7 * w])
        dyc = dy[:, w:2 * w]
        dz_ref[:, 4 * w:5 * w] = (dyc * _short_conv(u, uprev, cw)).astype(BF16)
        dconv = dyc * zb
        edge = dcarry[...]
        dconv1 = _shift_rows(dconv, -1, edge)
        dconv2 = _shift_rows(dconv, -2, edge)
        dcarry[...] = dconv[0:8, :]
        du = cw[2:3, :] * dconv + cw[1:2, :] * dconv1 + cw[0:1, :] * dconv2
        dz_ref[:, 5 * w:6 * w] = (du * zu).astype(BF16)
        dz_ref[:, 6 * w:7 * w] = (du * zc).astype(BF16)
        dcw_ref[...] += jnp.concatenate([
            jnp.sum(u * dconv2, axis=0, keepdims=True),
            jnp.sum(u * dconv1, axis=0, keepdims=True),
            jnp.sum(u * dconv, axis=0, keepdims=True)], axis=0)

        mask = _block_causal_mask(tm)
        chunk_of_row = lax.broadcasted_iota(jnp.int32, (tm, 1), 0) // CHUNK
        heads = range(HGRN_HEADS)
        hcols = [slice(hd * HGRN_DK, (hd + 1) * HGRN_DK) for hd in heads]
        qhb = [qh_buf[:, hcols[hd]] for hd in heads]
        khb = [kh_buf[:, hcols[hd]] for hd in heads]
        vb = [v_buf[:, hcols[hd]] for hd in heads]
        dob = [do_buf[:, hcols[hd]] for hd in heads]
        scores = [jnp.where(mask, _dot(qhb[hd], khb[hd], NT), 0.0).astype(BF16) for hd in heads]
        dscores = [jnp.where(mask, _dot(dob[hd], vb[hd], NT), 0.0).astype(BF16) for hd in heads]
        gains = [_dot(_spread(dob[hd], chunk_of_row, nc), qhb[hd], TN) for hd in heads]
        dst_rows, dst_lanes, st_lanes, carries = [], [], [], []
        for hd in heads:
            entering = [st_ref[c, hd] for c in range(nc)]
            leaving, carried_back = [None] * nc, [None] * nc
            dst = dstate[hd]
            for c in reversed(range(nc)):
                elast = etot_buf[c * CHUNK:c * CHUNK + 1, hcols[hd]]
                leaving[c] = dst
                carried_back[c] = jnp.sum(dst * entering[c], axis=0, keepdims=True) * elast
                dst = dst * elast + gains[hd][c * HGRN_DK:(c + 1) * HGRN_DK, :]
            dstate[hd] = dst
            dst_rows.append(jnp.concatenate(leaving, axis=0).astype(BF16))
            dst_lanes.append(jnp.concatenate(leaving, axis=1).astype(BF16))
            st_lanes.append(jnp.concatenate(entering, axis=1).astype(BF16))
            carries.append(carried_back)
        dv = [_dot(scores[hd], dob[hd], TN) + _pick(_dot(kbar_buf[:, hcols[hd]], dst_rows[hd], NT), chunk_of_row, nc)
              for hd in heads]
        dz_ref[:, 2 * w:3 * w] = jnp.concatenate(dv, axis=1).astype(BF16)
        dqh = jnp.concatenate(
            [_dot(dscores[hd], khb[hd]) + _pick(_dot(dob[hd], st_lanes[hd]), chunk_of_row, nc) for hd in heads], axis=1)
        dkh = jnp.concatenate([_dot(dscores[hd], qhb[hd], TN) for hd in heads], axis=1)
        dkbar = jnp.concatenate([_pick(_dot(vb[hd], dst_lanes[hd]), chunk_of_row, nc) for hd in heads], axis=1)

        kbar_dkbar = kbar_buf[...].astype(F32) * dkbar
        db = qh_buf[...].astype(F32) * dqh - kh_buf[...].astype(F32) * dkh - kbar_dkbar
        through_last = jnp.concatenate([
            jnp.broadcast_to(
                jnp.sum(kbar_dkbar[c * CHUNK:(c + 1) * CHUNK], axis=0, keepdims=True)
                + jnp.concatenate([carries[hd][c] for hd in heads], axis=1),
                (CHUNK, w))
            for c in range(nc)], axis=0)
        dlogf = _chunk_cumsum(db, reverse=True) + through_last
        df = dlogf / f - (dkh * enb + dkbar * erest)
        zq = z[:, 0:w]
        dz_ref[:, 0:w] = (dqh * eb * HGRN_DK ** -0.5 * (sq * (1.0 + zq * (1.0 - sq)))).astype(BF16)
        dz_ref[:, w:2 * w] = (df * (1.0 - lb) * sig * (1.0 - sig)).astype(BF16)
        dlb = jnp.sum(df * (1.0 - sig), axis=0, keepdims=True) * lb * (1.0 - lb)
        dlbp_ref[...] += jnp.concatenate([dlb, -dlb], axis=0)

        dh = _dot(dz_ref[...], win_ref[...])
        dx_ref[...] = _rms_bwd(dh, xh, r, gv) + dxo
        dg_ref[...] += jnp.sum(dh * xh, axis=0, keepdims=True)

    return _call(
        body,
        name="mix_bwd",
        grid=(n,),
        in_specs=[
            _rows_rev(tm, d, n), _full((1, d)), _rows_rev(tm, d, n), _rows_rev(tm, zw, n),
            pl.BlockSpec((8, zw), lambda i: (jnp.maximum((n - 1 - i) * (tm // 8) - 1, 0), 0)),
            _rows_rev(tm, w, n),
            pl.BlockSpec((nc, HGRN_HEADS, HGRN_DK, HGRN_DK), lambda i: (n - 1 - i, 0, 0, 0)),
            _full((zw, d)), _full((2, w)), _full((1, w)), _full((3, w)), _full((2 * w, d)),
        ],
        out_specs=[
            _rows_rev(tm, d, n), _rows_rev(tm, zw, n), _rows_rev(tm, d, n),
            _full((1, d)), _full((2, w)), _full((1, w)), _full((3, w)),
        ],
        out_shape=[
            jax.ShapeDtypeStruct((t, d), F32),
            jax.ShapeDtypeStruct((t, zw), BF16),
            jax.ShapeDtypeStruct((t, d), BF16),
            jax.ShapeDtypeStruct((1, d), F32),
            jax.ShapeDtypeStruct((2, w), F32),
            jax.ShapeDtypeStruct((1, w), F32),
            jax.ShapeDtypeStruct((3, w), F32),
        ],
        scratch_shapes=[
            pltpu.VMEM((HGRN_HEADS, HGRN_DK, HGRN_DK), F32), pltpu.VMEM((8, w), F32),
            pltpu.VMEM((tm, w), BF16),
            pltpu.VMEM((tm, w), BF16), pltpu.VMEM((tm, w), BF16), pltpu.VMEM((tm, w), BF16),
            pltpu.VMEM((tm, w), BF16), pltpu.VMEM((tm, w), F32),
        ],
        args=(x, g, dxo, z, z, o, states, w_in, lbp, gh, convw_t, w_out),
        exchange=exchange,
    )


def _memkv_fwd(mem, g, wkv):
    m, d = mem.shape
    nb, _, cb = wkv.shape

    def body(mem_ref, g_ref, wkv_ref, kv_ref):
        mn, _, _ = _rms(mem_ref[...], g_ref[...])
        mnb = mn.astype(BF16)
        for j in range(nb):
            kv_ref[:, j * cb:(j + 1) * cb] = _dot(mnb, wkv_ref[j]).astype(BF16)

    return pl.pallas_call(
        body,
        name="memkv_fwd",
        out_shape=jax.ShapeDtypeStruct((m, nb * cb), BF16),
        compiler_params=_params(),
    )(mem, g, wkv)


def _memkv_bwd(mem, g, dkv, wkv):
    m, d = mem.shape
    nb, _, cb = wkv.shape
    chips = nb // 2

    def body(mem_ref, g_ref, dkv_ref, wkv_ref, dw_ref, dg_ref, dw_all, send_buf, recv_buf, send_sem, recv_sem):
        x, y, c, _ = _mesh_place()
        sibling, _ = _peer(x, y, c, 1)
        mn, xh, _ = _rms(mem_ref[...], g_ref[...])
        mnb = mn.astype(BF16)
        dmn = jnp.zeros((m, d), F32)
        for j in range(nb):
            dkvb = dkv_ref[:, j * cb:(j + 1) * cb].astype(BF16)
            dw_all[j] = _dot(mnb, dkvb, TN)
            dmn = dmn + _dot(dkvb, wkv_ref[j], NT)
        dg_ref[...] = jnp.sum(dmn * xh, axis=0, keepdims=True)
        for q in range(chips):
            send_buf[q] = dw_all[2 * q + 1 - c].astype(BF16)
        to_sibling = _remote(send_buf, recv_buf, send_sem, recv_sem, sibling)
        to_sibling.start()
        to_sibling.wait_send()
        to_sibling.wait_recv()
        for q in range(chips):
            dw_ref[q] = (dw_all[2 * q + c] + recv_buf[q].astype(F32)).astype(BF16)

    return pl.pallas_call(
        body,
        name="memkv_bwd",
        out_shape=[jax.ShapeDtypeStruct((chips, d, cb), BF16), jax.ShapeDtypeStruct((1, d), F32)],
        scratch_shapes=[
            pltpu.VMEM((nb, d, cb), F32), pltpu.VMEM((chips, d, cb), BF16), pltpu.VMEM((chips, d, cb), BF16),
            pltpu.SemaphoreType.DMA, pltpu.SemaphoreType.DMA,
        ],
        compiler_params=_params(),
    )(mem, g, dkv, wkv)


def _softmax_rows(qm_h, k_h):
    sc = _dot(qm_h, k_h, NT) * MEM_HD ** -0.5
    e = jnp.exp(sc - jnp.max(sc, axis=-1, keepdims=True))
    return e / jnp.sum(e, axis=-1, keepdims=True)


def _xattn_fwd(x, g, wq, kv, wo, exchange=None):
    t, d = x.shape
    m = kv.shape[0]
    tm = min(XATTN_TILE, t)

    def body(x_ref, g_ref, wq_ref, kv_ref, wo_ref, xo_ref, hq_ref, qm_ref, att_ref):
        xv = x_ref[...]
        h, _, _ = _rms(xv, g_ref[...])
        hb = h.astype(BF16)
        hq_ref[...] = hb
        qm = _dot(hb, wq_ref[...]).astype(BF16)
        qm_ref[...] = qm
        heads = range(MEM_HEADS)
        kcols = [slice(hd * MEM_HD, (hd + 1) * MEM_HD) for hd in heads]
        p = [_softmax_rows(qm[:, kcols[hd]], kv_ref[:, kcols[hd]]) for hd in heads]
        att = jnp.concatenate(
            [_dot(p[hd].astype(BF16), kv_ref[:, d + hd * MEM_HD:d + (hd + 1) * MEM_HD]) for hd in heads],
            axis=1).astype(BF16)
        att_ref[...] = att
        xo_ref[...] = xv + _dot(att, wo_ref[...])

    return _call(
        body,
        name="xattn_fwd",
        grid=(t // tm,),
        in_specs=[_rows(tm, d), _full((1, d)), _full((d, d)), _full((m, 2 * d)), _full((d, d))],
        out_specs=[_rows(tm, d), _rows(tm, d), _rows(tm, d), _rows(tm, d)],
        out_shape=[
            jax.ShapeDtypeStruct((t, d), F32),
            jax.ShapeDtypeStruct((t, d), BF16),
            jax.ShapeDtypeStruct((t, d), BF16),
            jax.ShapeDtypeStruct((t, d), BF16),
        ],
        args=(x, g, wq, kv, wo),
        exchange=exchange,
    )


def _xattn_bwd(x, g, dxo, qm, kv, wq, wo, exchange=None):
    t, d = x.shape
    m = kv.shape[0]
    tm = min(XATTN_TILE, t)

    def body(x_ref, g_ref, dxo_ref, qm_ref, kv_ref, wq_ref, wo_ref, dx_ref, dqm_ref, dkv_ref, dg_ref):
        _zero_at_start(dkv_ref, dg_ref)
        gv = g_ref[...]
        _, xh, r = _rms(x_ref[...], gv)
        dxo = dxo_ref[...]
        datt = _dot(dxo.astype(BF16), wo_ref[...], NT).astype(BF16)
        heads = range(MEM_HEADS)
        kcols = [slice(hd * MEM_HD, (hd + 1) * MEM_HD) for hd in heads]
        vcols = [slice(d + hd * MEM_HD, d + (hd + 1) * MEM_HD) for hd in heads]
        qm_h = [qm_ref[:, kcols[hd]] for hd in heads]
        p = [_softmax_rows(qm_h[hd], kv_ref[:, kcols[hd]]) for hd in heads]
        dp = [_dot(datt[:, kcols[hd]], kv_ref[:, vcols[hd]], NT) for hd in heads]
        dsc = [(p[hd] * (dp[hd] - jnp.sum(p[hd] * dp[hd], axis=-1, keepdims=True)) * MEM_HD ** -0.5).astype(BF16)
               for hd in heads]
        dqm = jnp.concatenate([_dot(dsc[hd], kv_ref[:, kcols[hd]]) for hd in heads], axis=1).astype(BF16)
        dqm_ref[...] = dqm
        dkv_ref[...] += jnp.concatenate(
            [_dot(dsc[hd], qm_h[hd], TN) for hd in heads]
            + [_dot(p[hd].astype(BF16), datt[:, kcols[hd]], TN) for hd in heads], axis=1)
        dh = _dot(dqm, wq_ref[...], NT)
        dx_ref[...] = _rms_bwd(dh, xh, r, gv) + dxo
        dg_ref[...] += jnp.sum(dh * xh, axis=0, keepdims=True)

    return _call(
        body,
        name="xattn_bwd",
        grid=(t // tm,),
        in_specs=[
            _rows(tm, d), _full((1, d)), _rows(tm, d), _rows(tm, d), _full((m, 2 * d)), _full((d, d)), _full((d, d)),
        ],
        out_specs=[_rows(tm, d), _rows(tm, d), _full((m, 2 * d)), _full((1, d))],
        out_shape=[
            jax.ShapeDtypeStruct((t, d), F32),
            jax.ShapeDtypeStruct((t, d), BF16),
            jax.ShapeDtypeStruct((m, 2 * d), F32),
            jax.ShapeDtypeStruct((1, d), F32),
        ],
        args=(x, g, dxo, qm, kv, wq, wo),
        exchange=exchange,
    )


def _mesh_place():
    x, y, c = lax.axis_index("x"), lax.axis_index("y"), lax.axis_index("c")
    return x, y, c, 4 * x + 2 * y + c


def _peer(x, y, c, k):
    px = 1 - x if k & 4 else x
    py = 1 - y if k & 2 else y
    pc = 1 - c if k & 1 else c
    return (px, py, pc), 4 * px + 2 * py + pc


ICI_HOPS = (2, 4, 6)
N_HOPS = len(ICI_HOPS)


def _remote(src, dst, send_sem, recv_sem, peer):
    return pltpu.make_async_remote_copy(
        src_ref=src, dst_ref=dst, send_sem=send_sem, recv_sem=recv_sem, device_id=peer, device_id_type=MESH_IDS)


def _gather_exchange(shards, middle_eighths=MIDDLE_EIGHTHS):
    n = len(shards)

    def place():
        x, y, c, me = _mesh_place()
        sibling, _ = _peer(x, y, c, 1)
        to_x, from_x = _peer(x, y, c, 4)
        to_y, from_y = _peer(x, y, c, 2)
        _, from_diagonal = _peer(x, y, c, 6)
        onward = (c * to_y[0] + (1 - c) * to_x[0], c * to_y[1] + (1 - c) * to_x[1], c)
        passed_on = c * from_x + (1 - c) * from_y
        return me, sibling, (to_x, to_y, onward), (from_x, from_y, from_diagonal), passed_on

    def start(src, dst, sems):
        ici_send, ici_recv, pair_send, pair_recv, local = sems
        me, sibling, targets, _, _ = place()
        for a in range(n):
            pltpu.make_async_copy(src[a], dst[a].at[me], local.at[a]).start()
            for j in range(2):
                _remote(src[a], dst[a].at[me], ici_send.at[a, j], ici_recv.at[a, j], targets[j]).start()
            _remote(src[a], dst[a].at[me], pair_send.at[a, 0], pair_recv.at[a, 0], sibling).start()

    def to_sibling(dst, sems, a, j, origin, sibling):
        _, _, pair_send, pair_recv, _ = sems
        slot = dst[a].at[origin]
        return _remote(slot, slot, pair_send.at[a, 1 + j], pair_recv.at[a, 1 + j], sibling)

    def middle(src, dst, sems):
        ici_send, ici_recv, _, _, _ = sems
        _, sibling, targets, origins, passed_on = place()
        for a in range(n):
            for j in range(2):
                _remote(src[a], dst[a].at[origins[j]], ici_send.at[a, j], ici_recv.at[a, j], targets[j]).wait_recv()
            slot = dst[a].at[passed_on]
            _remote(slot, slot, ici_send.at[a, 2], ici_recv.at[a, 2], targets[2]).start()
            for j in range(2):
                to_sibling(dst, sems, a, j, origins[j], sibling).start()

    def finish(src, dst, sems):
        ici_send, ici_recv, pair_send, pair_recv, local = sems
        me, sibling, targets, origins, _ = place()
        for a in range(n):
            _remote(src[a], dst[a].at[origins[2]], ici_send.at[a, 2], ici_recv.at[a, 2], targets[2]).wait_recv()
            to_sibling(dst, sems, a, 2, origins[2], sibling).start()
        for a in range(n):
            pltpu.make_async_copy(src[a], dst[a].at[me], local.at[a]).wait()
            for j in range(N_HOPS):
                _remote(src[a], dst[a].at[me], ici_send.at[a, j], ici_recv.at[a, j], targets[j]).wait_send()
            for j, origin in enumerate((me,) + origins):
                from_sibling = origin + 1 - 2 * (origin % 2)
                passed = _remote(src[a], dst[a].at[from_sibling], pair_send.at[a, j], pair_recv.at[a, j], sibling)
                passed.wait_send()
                passed.wait_recv()

    return _Exchange(
        shards,
        [jax.ShapeDtypeStruct((N_DEV,) + s.shape, s.dtype) for s in shards],
        [
            pltpu.SemaphoreType.DMA((n, N_HOPS)), pltpu.SemaphoreType.DMA((n, N_HOPS)),
            pltpu.SemaphoreType.DMA((n, N_HOPS + 1)), pltpu.SemaphoreType.DMA((n, N_HOPS + 1)),
            pltpu.SemaphoreType.DMA((n,)),
        ],
        start, finish, middle, middle_eighths)


def _scatter_copies(src, dst, sems, n, arrivals=False):
    send, recv, local = sems
    x, y, c, _ = _mesh_place()
    chip = 2 * x + y
    if arrivals is None:
        return [pltpu.make_async_copy(src[a].at[chip], dst[a].at[chip], local.at[a]) for a in range(n)]
    copies = []
    for a in range(n):
        for j, k in enumerate(ICI_HOPS):
            peer, _ = _peer(x, y, c, k)
            peer_chip = 2 * peer[0] + peer[1]
            slot = dst[a].at[peer_chip if arrivals else chip]
            copies.append(_remote(src[a].at[peer_chip], slot, send.at[a, j], recv.at[a, j], peer))
    return copies


def _scatter_start(src, dst, sems, n):
    for cp in _scatter_copies(src, dst, sems, n, arrivals=None) + _scatter_copies(src, dst, sems, n):
        cp.start()


def _scatter_finish(src, dst, sems, n):
    for cp in _scatter_copies(src, dst, sems, n, arrivals=None):
        cp.wait()
    for cp in _scatter_copies(src, dst, sems, n):
        cp.wait_send()
    for cp in _scatter_copies(src, dst, sems, n, arrivals=True):
        cp.wait_recv()


def _scatter_scratch(n):
    return [pltpu.SemaphoreType.DMA((n, N_HOPS)), pltpu.SemaphoreType.DMA((n, N_HOPS)), pltpu.SemaphoreType.DMA((n,))]


def _scatter_exchange(partials):
    n = len(partials)
    return _Exchange(
        partials, [jax.ShapeDtypeStruct(p.shape, p.dtype) for p in partials], _scatter_scratch(n),
        lambda src, dst, sems: _scatter_start(src, dst, sems, n),
        lambda src, dst, sems: _scatter_finish(src, dst, sems, n))


SMALL_LAYOUT = {
    "ffn1_norm": (0, 1, 1024), "mix_norm": (1, 1, 1024), "xattn_norm": (2, 1, 1024), "mem_norm": (3, 1, 1024),
    "ffn2_norm": (4, 1, 1024), "final_norm": (5, 1, 1024), "lb_param": (6, 2, 512), "hgrn_out_norm": (8, 1, 512),
    "conv_w": (9, 3, 512), "loss": (12, 1, 128),
}


def _final_exchange(partials, small):
    n = len(partials)
    names = list(small)
    width = 1024

    def body(*refs):
        src = refs[:n]
        pieces = refs[n:n + len(names)]
        dst = refs[n + len(names):2 * n + len(names)]
        total_ref = refs[2 * n + len(names)]
        pack, gathered, small_send, small_recv = refs[2 * n + len(names) + 1:2 * n + len(names) + 5]
        sems = refs[2 * n + len(names) + 5:]
        x, y, c, me = _mesh_place()
        pack[...] = jnp.zeros_like(pack)
        for name, piece in zip(names, pieces):
            row, nrows, ncols = SMALL_LAYOUT[name]
            pack[row:row + nrows, 0:ncols] = piece[...]
        for k in range(1, N_DEV):
            peer, _ = _peer(x, y, c, k)
            _remote(pack, gathered.at[me], small_send.at[k - 1], small_recv.at[k - 1], peer).start()
        _scatter_start(src, dst, sems, n)
        gathered[me] = pack[...]
        for k in range(1, N_DEV):
            peer, peer_index = _peer(x, y, c, k)
            landed = _remote(pack, gathered.at[peer_index], small_send.at[k - 1], small_recv.at[k - 1], peer)
            landed.wait_send()
            landed.wait_recv()
        total = gathered[0]
        for j in range(1, N_DEV):
            total = total + gathered[j]
        total_ref[...] = total
        _scatter_finish(src, dst, sems, n)

    hbm = pl.BlockSpec(memory_space=pltpu.HBM)
    vmem = pl.BlockSpec(memory_space=pltpu.VMEM)
    out = pl.pallas_call(
        body,
        name="final_exchange",
        in_specs=[hbm] * n + [vmem] * len(names),
        out_specs=[hbm] * n + [vmem],
        out_shape=[jax.ShapeDtypeStruct(p.shape, p.dtype) for p in partials]
        + [jax.ShapeDtypeStruct((SMALL_ROWS, width), F32)],
        scratch_shapes=[
            pltpu.VMEM((SMALL_ROWS, width), F32), pltpu.VMEM((N_DEV, SMALL_ROWS, width), F32),
            pltpu.SemaphoreType.DMA((N_DEV - 1,)), pltpu.SemaphoreType.DMA((N_DEV - 1,)),
        ] + _scatter_scratch(n),
        compiler_params=pltpu.CompilerParams(has_side_effects=True),
    )(*partials, *[small[k] for k in names])
    return out[:n], out[n]


def _adamw_math(w, g, m, v):
    m = ADAM_B1 * m + (1.0 - ADAM_B1) * g
    v = ADAM_B2 * v + (1.0 - ADAM_B2) * (g * g)
    m_hat = m / (1.0 - ADAM_B1 ** ADAM_STEP)
    v_hat = v / (1.0 - ADAM_B2 ** ADAM_STEP)
    delta = -ADAM_LR * (m_hat / (jnp.sqrt(v_hat) + ADAM_EPS) + ADAM_WD * w)
    return delta, m, v


def _adamw_shard(parts, w, m, v):
    r, c = w.shape
    n_parts = parts.shape[0]
    tr = max(rows for rows in range(16, r + 1, 16) if r % rows == 0 and rows * c <= ADAMW_TILE_ELEMENTS)

    def body(p_ref, w_ref, m_ref, v_ref, g_ref, d_ref, mo_ref, vo_ref):
        g = p_ref[0].astype(F32)
        for j in range(1, n_parts):
            g = g + p_ref[j].astype(F32)
        delta, mn, vn = _adamw_math(w_ref[...], g, m_ref[...], v_ref[...])
        g_ref[...] = g
        d_ref[...] = delta
        mo_ref[...] = mn
        vo_ref[...] = vn

    tile = pl.BlockSpec((tr, c), lambda i: (i, 0))
    return pl.pallas_call(
        body,
        name="adamw_shard",
        grid=(r // tr,),
        in_specs=[pl.BlockSpec((n_parts, tr, c), lambda i: (0, i, 0)), tile, tile, tile],
        out_specs=[tile] * 4,
        out_shape=[jax.ShapeDtypeStruct((r, c), F32)] * 4,
        compiler_params=_params(("parallel",)),
    )(parts, w, m, v)


def _adamw_small(gs, ws, ms, vs):
    n = len(gs)

    def body(*refs):
        g_refs, w_refs, m_refs, v_refs = refs[:n], refs[n:2 * n], refs[2 * n:3 * n], refs[3 * n:4 * n]
        d_out, m_out, v_out = refs[4 * n:5 * n], refs[5 * n:6 * n], refs[6 * n:7 * n]
        for i in range(n):
            delta, mn, vn = _adamw_math(w_refs[i][...], g_refs[i][...], m_refs[i][...], v_refs[i][...])
            d_out[i][...] = delta
            m_out[i][...] = mn
            v_out[i][...] = vn

    shapes = [jax.ShapeDtypeStruct(w.shape, F32) for w in ws]
    out = pl.pallas_call(
        body,
        name="adamw_small",
        out_shape=shapes * 3,
        compiler_params=_params(),
    )(*gs, *ws, *ms, *vs)
    return out[:n], out[n:2 * n], out[2 * n:]


TRANSPOSED = ("ffn1_gate", "ffn1_up", "w_in", "ffn2_gate", "ffn2_up", "conv_w")
GROUP_FFN1 = ("ffn1_gate", "ffn1_up", "ffn1_down")
GROUP_MIX = ("w_in", "w_out")
GROUP_XATTN = ("w_q_mem", "w_kv_mem", "w_o_mem")
GROUP_FFN2 = ("ffn2_gate", "ffn2_up", "ffn2_down")
LARGE = GROUP_FFN1 + GROUP_MIX + GROUP_XATTN + GROUP_FFN2
SMALL = ("ffn1_norm", "mix_norm", "lb_param", "hgrn_out_norm", "conv_w", "xattn_norm", "mem_norm", "ffn2_norm",
         "final_norm")
WEIGHTS = ("ffn1_norm", "ffn1_gate", "ffn1_up", "ffn1_down", "mix_norm", "w_in", "lb_param", "hgrn_out_norm",
           "conv_w", "w_out", "xattn_norm", "mem_norm", "w_q_mem", "w_kv_mem", "w_o_mem", "ffn2_norm", "ffn2_gate",
           "ffn2_up", "ffn2_down", "final_norm")


def kernel(x, mem, ffn1_norm, ffn1_gate, ffn1_up, ffn1_down, mix_norm, w_in, lb_param, hgrn_out_norm, conv_w, w_out, xattn_norm, mem_norm, w_q_mem, w_kv_mem, w_o_mem, ffn2_norm, ffn2_gate, ffn2_up, ffn2_down, final_norm, loss_target, m_ffn1_norm, m_ffn1_gate, m_ffn1_up, m_ffn1_down, m_mix_norm, m_w_in, m_lb_param, m_hgrn_out_norm, m_conv_w, m_w_out, m_xattn_norm, m_mem_norm, m_w_q_mem, m_w_kv_mem, m_w_o_mem, m_ffn2_norm, m_ffn2_gate, m_ffn2_up, m_ffn2_down, m_final_norm, v_ffn1_norm, v_ffn1_gate, v_ffn1_up, v_ffn1_down, v_mix_norm, v_w_in, v_lb_param, v_hgrn_out_norm, v_conv_w, v_w_out, v_xattn_norm, v_mem_norm, v_w_q_mem, v_w_kv_mem, v_w_o_mem, v_ffn2_norm, v_ffn2_gate, v_ffn2_up, v_ffn2_down, v_final_norm):
    given = dict(locals())
    me = 4 * lax.axis_index("x") + 2 * lax.axis_index("y") + lax.axis_index("c")
    x0, memv, target = x[0], mem[0], loss_target[0]

    def shard(prefix, name):
        v = given[prefix + name]
        if v.ndim == 1:
            return v.reshape(1, -1)
        if v.ndim == 2:
            return v
        return v[0].T if name in TRANSPOSED else v[0]

    w = {name: shard("", name) for name in WEIGHTS}
    m = {name: shard("m_", name) for name in WEIGHTS}
    v = {name: shard("v_", name) for name in WEIGHTS}

    conv_taps, conv_rows = w["conv_w"].shape
    conv_tile = jnp.pad(w["conv_w"], ((0, 8 - conv_taps), (0, 128 - conv_rows)))
    wire = {name: w[name].astype(BF16) for name in LARGE}
    full = {}

    def landed(names, gathered):
        for name, blocks in zip(names, gathered):
            _, r, c = blocks.shape
            full[name] = blocks if name == "w_kv_mem" else blocks.reshape(N_DEV * r, c)

    first = ("ffn1_gate", "ffn1_up")
    landed(first, _run_exchange(_gather_exchange([wire[k] for k in first]), "gather_first"))

    riders = (("ffn1_down", "w_in"), ("w_out", "w_kv_mem"), ("w_q_mem", "w_o_mem", "ffn2_gate", "ffn2_up"),
              ("ffn2_down",))
    (a1, b1, s1), gathered = _ffn_up(
        x0, w["ffn1_norm"], full["ffn1_gate"], full["ffn1_up"],
        exchange=_gather_exchange([wire[k] for k in riders[0]]))
    landed(riders[0], gathered)
    (x1,), gathered = _ffn_down(
        x0, s1, full["ffn1_down"], exchange=_gather_exchange([wire[k] for k in riders[1]] + [conv_tile]))
    landed(riders[1], gathered)
    convw_t = gathered[-1][:, :conv_taps, :conv_rows].transpose(1, 0, 2).reshape(conv_taps, N_DEV * conv_rows)
    (x2, z, o_raw, states, ycat), gathered = _mix_fwd(
        x1, w["mix_norm"], full["w_in"], w["lb_param"], w["hgrn_out_norm"], convw_t, full["w_out"],
        exchange=_gather_exchange([wire[k] for k in riders[2]]))
    landed(riders[2], gathered)
    kv = _memkv_fwd(memv, w["mem_norm"], full["w_kv_mem"])
    (x3, hq, qm, att), gathered = _xattn_fwd(
        x2, w["xattn_norm"], full["w_q_mem"], kv, full["w_o_mem"],
        exchange=_gather_exchange([wire[k] for k in riders[3]], middle_eighths=EARLY_MIDDLE_EIGHTHS))
    landed(riders[3], gathered)
    (dx4, a2, b2, s2, loss_part, d_final), _ = _ffn_fwd(
        x3, w["ffn2_norm"], full["ffn2_gate"], full["ffn2_up"], full["ffn2_down"], head=(w["final_norm"], target))

    parts = {}
    waiting = []

    def carried():
        names = [name for name, _ in waiting]
        exchange = _scatter_exchange([p for _, p in waiting]) if waiting else None
        del waiting[:]
        return names, exchange

    def weight_grad(name, a, b, scale=1.0):
        names, exchange = carried()
        partial, arrived = _weight_grad(a, b, scale, exchange=exchange)
        parts.update(zip(names, arrived))
        waiting.append((name, partial))

    (dx3, da2, db2, h4, d_ffn2_norm), _ = _ffn_bwd(
        x3, w["ffn2_norm"], dx4, a2, b2, full["ffn2_gate"], full["ffn2_up"], full["ffn2_down"])
    weight_grad("ffn2_down", s2, dx4, 0.5)
    weight_grad("ffn2_gate", da2, h4)
    weight_grad("ffn2_up", db2, h4)
    names, exchange = carried()
    (dx2, dqm, dkv, d_xattn_norm), arrived = _xattn_bwd(
        x2, w["xattn_norm"], dx3, qm, kv, full["w_q_mem"], full["w_o_mem"], exchange=exchange)
    parts.update(zip(names, arrived))
    d_wkv, d_mem_norm = _memkv_bwd(memv, w["mem_norm"], dkv, full["w_kv_mem"])
    waiting.append(("w_kv_mem", d_wkv))
    names, exchange = carried()
    (dx1, dz, h2, d_mix_norm, d_lbp, d_gh, d_convw_t), arrived = _mix_bwd(
        x1, w["mix_norm"], dx2, z, o_raw, states, full["w_in"], w["lb_param"], w["hgrn_out_norm"], convw_t,
        full["w_out"], exchange=exchange)
    parts.update(zip(names, arrived))
    weight_grad("w_in", dz, h2)
    weight_grad("ffn1_down", s1, dx1, 0.5)
    (dx0, da1, db1, h1, d_ffn1_norm), _ = _ffn_bwd(
        x0, w["ffn1_norm"], dx1, a1, b1, full["ffn1_gate"], full["ffn1_up"], full["ffn1_down"])
    weight_grad("ffn1_gate", da1, h1)
    weight_grad("ffn1_up", db1, h1)
    weight_grad("w_o_mem", att, dx3)
    weight_grad("w_q_mem", hq, dqm)
    weight_grad("w_out", ycat, dx2)

    small_parts = {
        "ffn1_norm": d_ffn1_norm, "mix_norm": d_mix_norm, "xattn_norm": d_xattn_norm, "mem_norm": d_mem_norm,
        "ffn2_norm": d_ffn2_norm, "final_norm": d_final, "lb_param": d_lbp, "hgrn_out_norm": d_gh,
        "conv_w": d_convw_t, "loss": loss_part,
    }
    names = [name for name, _ in waiting]
    arrived, total = _final_exchange([p for _, p in waiting], small_parts)
    parts.update(zip(names, arrived))

    g_out, d_out, m_out, v_out = {}, {}, {}, {}
    for name in LARGE:
        g_out[name], d_out[name], m_out[name], v_out[name] = _adamw_shard(parts[name], w[name], m[name], v[name])
    g_small = {}
    for name in SMALL:
        row, nrows, ncols = SMALL_LAYOUT[name]
        g_small[name] = total[row:row + nrows, 0:ncols]
    g_small["conv_w"] = lax.dynamic_slice_in_dim(g_small["conv_w"], me * conv_rows, conv_rows, axis=1)
    ds, ms, vs = _adamw_small(
        [g_small[k] for k in SMALL], [w[k] for k in SMALL], [m[k] for k in SMALL], [v[k] for k in SMALL])
    for i, name in enumerate(SMALL):
        g_out[name], d_out[name], m_out[name], v_out[name] = g_small[name], ds[i], ms[i], vs[i]

    def shaped(value, name):
        return (value.T if name in TRANSPOSED else value).reshape(given[name].shape)

    loss = total[SMALL_LAYOUT["loss"][0], 0]
    outs = [loss, dx0.reshape(x.shape)]
    for group in (g_out, d_out, m_out, v_out):
        outs += [shaped(group[name], name) for name in WEIGHTS]
    return tuple(outs)
```

```python
import jax
import jax.numpy as jnp
from jax import lax
from jax.experimental import pallas as pl
from jax.experimental.pallas import tpu as pltpu

F32 = jnp.float32
BF16 = jnp.bfloat16
MESH_IDS = pl.DeviceIdType.MESH

N_DEV = 8
EPS = 1e-6
HGRN_HEADS = 4
HGRN_DK = 128
HGRN_W = 512
CHUNK = 64
MEM_HEADS = 4
MEM_HD = 256
ADAM_LR = 0.001
ADAM_B1 = 0.9
ADAM_B2 = 0.999
ADAM_EPS = 1e-08
ADAM_WD = 0.01
ADAM_STEP = 10

TOKEN_TILE = 256
XATTN_TILE = 512
WIDE_TILE = 512
REDUCE_TILE = 1024
ADAMW_TILE_ELEMENTS = 256 * 1024
MIDDLE_EIGHTHS = 5
EARLY_MIDDLE_EIGHTHS = 4
MXU_ROWS = 256
VMEM_LIMIT = 60 * 1024 * 1024
SMALL_ROWS = 16
NT = (((1,), (1,)), ((), ()))
TN = (((0,), (0,)), ((), ()))


def _params(sem=None):
    return pltpu.CompilerParams(dimension_semantics=sem, vmem_limit_bytes=VMEM_LIMIT)


def _dot(a, b, dims=None):
    if dims is None:
        return jnp.dot(a, b, preferred_element_type=F32)
    return lax.dot_general(a, b, dims, preferred_element_type=F32)


def _sigmoid(v):
    return 1.0 / (1.0 + jnp.exp(-v))


def _rms(x, g):
    r = lax.rsqrt(jnp.mean(x * x, axis=-1, keepdims=True) + EPS)
    xh = x * r
    return xh * g, xh, r


def _rms_bwd(dh, xh, r, g):
    dxh = dh * g
    return r * (dxh - xh * jnp.mean(dxh * xh, axis=-1, keepdims=True))


def _full(shape):
    return pl.BlockSpec(shape, lambda *_: (0,) * len(shape))


def _full_once(shape):
    return pl.BlockSpec(shape, lambda *_: (0,) * len(shape), pipeline_mode=pl.Buffered(1))


def _rows(tm, width):
    return pl.BlockSpec((tm, width), lambda i: (i, 0))


def _rows_rev(tm, width, n):
    return pl.BlockSpec((tm, width), lambda i: (n - 1 - i, 0))


def _zero_at_start(*refs):
    @pl.when(pl.program_id(0) == 0)
    def _():
        for ref in refs:
            ref[...] = jnp.zeros_like(ref)


class _Exchange:
    def __init__(self, operands, out_shapes, scratch, start, finish, middle=None, middle_eighths=MIDDLE_EIGHTHS):
        self.operands, self.out_shapes, self.scratch = list(operands), list(out_shapes), list(scratch)
        self.start, self.middle, self.finish, self.middle_eighths = start, middle, finish, middle_eighths


def _call(body, *, name, grid, in_specs, out_specs, out_shape, args, scratch_shapes=(), exchange=None):
    semantics = ("arbitrary",) * len(grid)
    if exchange is None:
        out = pl.pallas_call(
            body, name=name, grid=grid, in_specs=in_specs, out_specs=out_specs, out_shape=out_shape,
            scratch_shapes=list(scratch_shapes), compiler_params=_params(semantics))(*args)
        return out, []
    hbm = pl.BlockSpec(memory_space=pltpu.HBM)
    n_in, n_out, n_scr = len(in_specs), len(out_specs), len(scratch_shapes)
    e_in, e_out = len(exchange.operands), len(exchange.out_shapes)

    def carried(*refs):
        ins, rest = refs[:n_in], refs[n_in:]
        e_ins, rest = rest[:e_in], rest[e_in:]
        outs, rest = rest[:n_out], rest[n_out:]
        e_outs, rest = rest[:e_out], rest[e_out:]
        scr, e_scr = rest[:n_scr], rest[n_scr:]
        first = last = None
        for axis, size in enumerate(grid):
            at_start, at_end = pl.program_id(axis) == 0, pl.program_id(axis) == size - 1
            first = at_start if first is None else jnp.logical_and(first, at_start)
            last = at_end if last is None else jnp.logical_and(last, at_end)

        @pl.when(first)
        def _():
            exchange.start(e_ins, e_outs, e_scr)

        body(*ins, *outs, *scr)

        if exchange.middle is not None:
            assert len(grid) == 1

            @pl.when(pl.program_id(0) == (grid[0] * exchange.middle_eighths) // 8)
            def _():
                exchange.middle(e_ins, e_outs, e_scr)

        @pl.when(last)
        def _():
            exchange.finish(e_ins, e_outs, e_scr)

    out = pl.pallas_call(
        carried, name=name, grid=grid, in_specs=list(in_specs) + [hbm] * e_in,
        out_specs=list(out_specs) + [hbm] * e_out, out_shape=list(out_shape) + exchange.out_shapes,
        scratch_shapes=list(scratch_shapes) + exchange.scratch,
        compiler_params=pltpu.CompilerParams(
            dimension_semantics=semantics, vmem_limit_bytes=VMEM_LIMIT, has_side_effects=True),
    )(*args, *exchange.operands)
    return out[:n_out], out[n_out:]


def _run_exchange(exchange, name):
    hbm = pl.BlockSpec(memory_space=pltpu.HBM)
    e_in, e_out = len(exchange.operands), len(exchange.out_shapes)

    def body(*refs):
        e_ins, e_outs, e_scr = refs[:e_in], refs[e_in:e_in + e_out], refs[e_in + e_out:]
        exchange.start(e_ins, e_outs, e_scr)
        if exchange.middle is not None:
            exchange.middle(e_ins, e_outs, e_scr)
        exchange.finish(e_ins, e_outs, e_scr)

    return pl.pallas_call(
        body, name=name, in_specs=[hbm] * e_in, out_specs=[hbm] * e_out, out_shape=exchange.out_shapes,
        scratch_shapes=exchange.scratch, compiler_params=pltpu.CompilerParams(has_side_effects=True),
    )(*exchange.operands)


def _loss_head(xo, gf, tgt):
    d = xo.shape[1]
    y, xh, r = _rms(xo, gf)
    err = y - tgt
    dy = err * (1.0 / d)
    loss = 0.5 * jnp.sum(jnp.sum(err * err, axis=-1, keepdims=True) * (1.0 / d), axis=0, keepdims=True)
    return _rms_bwd(dy, xh, r, gf), loss, jnp.sum(dy * xh, axis=0, keepdims=True)


def _ffn_fwd(x, g, wg, wu, wd, exchange=None, head=None):
    t, d = x.shape
    f = wg.shape[0]
    tm = min(WIDE_TILE, t)

    def body(x_ref, g_ref, wg_ref, wu_ref, wd_ref, *rest):
        if head is None:
            xo_ref, a_ref, b_ref, s_ref = rest
        else:
            gf_ref, tgt_ref, xo_ref, a_ref, b_ref, s_ref, loss_ref, dgf_ref = rest
            _zero_at_start(loss_ref, dgf_ref)
        xv = x_ref[...]
        h, _, _ = _rms(xv, g_ref[...])
        hb = h.astype(BF16)
        a = _dot(hb, wg_ref[...], NT)
        b = _dot(hb, wu_ref[...], NT)
        s = (a * _sigmoid(a) * b).astype(BF16)
        xo = xv + 0.5 * _dot(s, wd_ref[...])
        if head is None:
            xo_ref[...] = xo
        else:
            xo_ref[...], loss, dgf = _loss_head(xo, gf_ref[...], tgt_ref[...])
            loss_ref[...] += jnp.broadcast_to(loss, (1, 128))
            dgf_ref[...] += dgf
        a_ref[...] = a.astype(BF16)
        b_ref[...] = b.astype(BF16)
        s_ref[...] = s

    in_specs = [_rows(tm, d), _full((1, d)), _full_once((f, d)), _full_once((f, d)), _full_once((f, d))]
    out_specs = [_rows(tm, d), _rows(tm, f), _rows(tm, f), _rows(tm, f)]
    out_shape = [
        jax.ShapeDtypeStruct((t, d), F32),
        jax.ShapeDtypeStruct((t, f), BF16),
        jax.ShapeDtypeStruct((t, f), BF16),
        jax.ShapeDtypeStruct((t, f), BF16),
    ]
    args = (x, g, wg, wu, wd)
    if head is not None:
        in_specs += [_full((1, d)), _rows(tm, d)]
        out_specs += [_full((1, 128)), _full((1, d))]
        out_shape += [jax.ShapeDtypeStruct((1, 128), F32), jax.ShapeDtypeStruct((1, d), F32)]
        args += tuple(head)
    return _call(
        body, name="ffn_fwd", grid=(t // tm,), in_specs=in_specs, out_specs=out_specs, out_shape=out_shape,
        args=args, exchange=exchange)


def _ffn_up(x, g, wg, wu, exchange=None):
    t, d = x.shape
    f = wg.shape[0]
    tm = min(WIDE_TILE, t)

    def body(x_ref, g_ref, wg_ref, wu_ref, a_ref, b_ref, s_ref):
        h, _, _ = _rms(x_ref[...], g_ref[...])
        hb = h.astype(BF16)
        a = _dot(hb, wg_ref[...], NT)
        b = _dot(hb, wu_ref[...], NT)
        a_ref[...] = a.astype(BF16)
        b_ref[...] = b.astype(BF16)
        s_ref[...] = (a * _sigmoid(a) * b).astype(BF16)

    return _call(
        body, name="ffn_up", grid=(t // tm,),
        in_specs=[_rows(tm, d), _full((1, d)), _full_once((f, d)), _full_once((f, d))],
        out_specs=[_rows(tm, f)] * 3, out_shape=[jax.ShapeDtypeStruct((t, f), BF16)] * 3,
        args=(x, g, wg, wu), exchange=exchange)


def _ffn_down(x, s, wd, exchange=None):
    t, d = x.shape
    f = wd.shape[0]
    tm = min(WIDE_TILE, t)

    def body(x_ref, s_ref, wd_ref, xo_ref):
        xo_ref[...] = x_ref[...] + 0.5 * _dot(s_ref[...], wd_ref[...])

    return _call(
        body, name="ffn_down", grid=(t // tm,),
        in_specs=[_rows(tm, d), _rows(tm, f), _full_once((f, d))],
        out_specs=[_rows(tm, d)], out_shape=[jax.ShapeDtypeStruct((t, d), F32)],
        args=(x, s, wd), exchange=exchange)


def _ffn_bwd(x, g, dxo, a, b, wg, wu, wd, exchange=None):
    t, d = x.shape
    f = wg.shape[0]
    tm = min(TOKEN_TILE, t)

    def body(x_ref, g_ref, dxo_ref, a_ref, b_ref, wg_ref, wu_ref, wd_ref, dx_ref, da_ref, db_ref, h_ref, dg_ref):
        _zero_at_start(dg_ref)
        gv = g_ref[...]
        h, xh, r = _rms(x_ref[...], gv)
        dxo = dxo_ref[...]
        ds = _dot((0.5 * dxo).astype(BF16), wd_ref[...], NT)
        af = a_ref[...].astype(F32)
        bf = b_ref[...].astype(F32)
        sg = _sigmoid(af)
        da = (ds * bf * (sg * (1.0 + af * (1.0 - sg)))).astype(BF16)
        db = (ds * (af * sg)).astype(BF16)
        dh = _dot(da, wg_ref[...]) + _dot(db, wu_ref[...])
        dx_ref[...] = _rms_bwd(dh, xh, r, gv) + dxo
        da_ref[...] = da
        db_ref[...] = db
        h_ref[...] = h.astype(BF16)
        dg_ref[...] += jnp.sum(dh * xh, axis=0, keepdims=True)

    return _call(
        body,
        name="ffn_bwd",
        grid=(t // tm,),
        in_specs=[
            _rows(tm, d), _full((1, d)), _rows(tm, d), _rows(tm, f), _rows(tm, f),
            _full_once((f, d)), _full_once((f, d)), _full_once((f, d)),
        ],
        out_specs=[_rows(tm, d), _rows(tm, f), _rows(tm, f), _rows(tm, d), _full((1, d))],
        out_shape=[
            jax.ShapeDtypeStruct((t, d), F32),
            jax.ShapeDtypeStruct((t, f), BF16),
            jax.ShapeDtypeStruct((t, f), BF16),
            jax.ShapeDtypeStruct((t, d), BF16),
            jax.ShapeDtypeStruct((1, d), F32),
        ],
        args=(x, g, dxo, a, b, wg, wu, wd),
        exchange=exchange,
    )


def _weight_grad(a, b, scale=1.0, exchange=None):
    t, m = a.shape
    n = b.shape[1]
    chips = N_DEV // 2
    r = m // N_DEV
    tk = min(REDUCE_TILE, t)
    halves = 2
    nb = n // halves
    nk = t // tk

    def body(a_ref, b_ref, o_ref, acc, send_buf, recv_buf, send_sems, recv_sems):
        k, j = pl.program_id(0), pl.program_id(1)
        x, y, c, _ = _mesh_place()
        sibling, _ = _peer(x, y, c, 1)
        bv = b_ref[...]
        if scale != 1.0:
            bv = bv * scale
        bb = bv.astype(BF16)
        acc_half = acc.at[j]

        @pl.when(k == 0)
        def _():
            acc_half[...] = jnp.zeros_like(acc_half)

        for i in range(m // MXU_ROWS):
            rows = slice(i * MXU_ROWS, (i + 1) * MXU_ROWS)
            acc_half[rows, :] += _dot(a_ref[:, rows].astype(BF16), bb, TN)

        def to_sibling(half):
            return _remote(send_buf.at[half], recv_buf.at[half], send_sems.at[half], recv_sems.at[half], sibling)

        def owned_rows(q, core):
            return pl.ds(pl.multiple_of((2 * q + core) * r, 8), r)

        for half in range(halves):
            @pl.when(jnp.logical_and(k == nk - 1, j == half))
            def _():
                for q in range(chips):
                    send_buf[half, q] = acc[half, owned_rows(q, 1 - c), :].astype(BF16)
                to_sibling(half).start()

        @pl.when(jnp.logical_and(k == nk - 1, j == halves - 1))
        def _():
            for half in range(halves):
                to_sibling(half).wait_send()
                to_sibling(half).wait_recv()
                for q in range(chips):
                    o_ref[q, :, half * nb:(half + 1) * nb] = (
                        acc[half, owned_rows(q, c), :] + recv_buf[half, q].astype(F32)).astype(BF16)

    (partial,), arrived = _call(
        body,
        name="weight_grad",
        grid=(nk, halves),
        in_specs=[pl.BlockSpec((tk, m), lambda k, j: (k, 0)), pl.BlockSpec((tk, nb), lambda k, j: (k, j))],
        out_specs=[pl.BlockSpec((chips, r, n), lambda k, j: (0, 0, 0))],
        out_shape=[jax.ShapeDtypeStruct((chips, r, n), BF16)],
        scratch_shapes=[
            pltpu.VMEM((halves, m, nb), F32),
            pltpu.VMEM((halves, chips, r, nb), BF16), pltpu.VMEM((halves, chips, r, nb), BF16),
            pltpu.SemaphoreType.DMA((halves,)), pltpu.SemaphoreType.DMA((halves,)),
        ],
        args=(a, b),
        exchange=exchange,
    )
    return partial, arrived


def _chunk_cumsum(v, reverse=False):
    n, width = v.shape
    row = lax.broadcasted_iota(jnp.int32, (n, n), 0)
    col = lax.broadcasted_iota(jnp.int32, (n, n), 1)
    earlier = col >= row if reverse else col <= row
    tri = jnp.where(jnp.logical_and(row // CHUNK == col // CHUNK, earlier), 1.0, 0.0).astype(BF16)
    hi = v.astype(BF16)
    rest = v - hi.astype(F32)
    mid = rest.astype(BF16)
    low = (rest - mid.astype(F32)).astype(BF16)
    sums = _dot(tri, jnp.concatenate([hi, mid, low], axis=1))
    return sums[:, 0:width] + sums[:, width:2 * width] + sums[:, 2 * width:3 * width]


def _shift_rows(v, shift, edge):
    n = v.shape[0]
    row = lax.broadcasted_iota(jnp.int32, (n, 1), 0)
    out = pltpu.roll(v, shift % n, axis=0)
    if shift > 0:
        for j in range(shift):
            out = jnp.where(row == j, edge[8 - shift + j:8 - shift + j + 1, :], out)
    else:
        for j in range(-shift):
            out = jnp.where(row == n + shift + j, edge[j:j + 1, :], out)
    return out


def _gates(z, lbp):
    w = HGRN_W
    lb = _sigmoid(lbp[0:1, :] - lbp[1:2, :])
    zq = z[:, 0:w]
    sig = _sigmoid(z[:, w:2 * w])
    f = lb + (1.0 - lb) * sig
    sq = _sigmoid(zq)
    q = zq * sq * HGRN_DK ** -0.5
    return lb, sig, f, sq, q


def _decayed_operands(q, f, v, qh_buf, kh_buf, kbar_buf, v_buf, etot_buf):
    n, width = f.shape
    bcum = _chunk_cumsum(jnp.log(f))
    total = jnp.concatenate(
        [jnp.broadcast_to(bcum[c + CHUNK - 1:c + CHUNK, :], (CHUNK, width)) for c in range(0, n, CHUNK)], axis=0)
    eb, enb, erest = jnp.exp(bcum), jnp.exp(-bcum), jnp.exp(total - bcum)
    kk = 1.0 - f
    qh_buf[...] = (q * eb).astype(BF16)
    kh_buf[...] = (kk * enb).astype(BF16)
    kbar_buf[...] = (kk * erest).astype(BF16)
    v_buf[...] = v.astype(BF16)
    etot_buf[...] = jnp.exp(total)
    return eb, enb, erest


def _short_conv(u, edge, cw):
    return cw[0:1, :] * _shift_rows(u, 2, edge) + cw[1:2, :] * _shift_rows(u, 1, edge) + cw[2:3, :] * u


def _block_causal_mask(n):
    row = lax.broadcasted_iota(jnp.int32, (n, n), 0)
    col = lax.broadcasted_iota(jnp.int32, (n, n), 1)
    return jnp.logical_and(row // CHUNK == col // CHUNK, col <= row)


def _spread(v, chunk_of_row, nc):
    return jnp.concatenate([jnp.where(chunk_of_row == c, v, jnp.zeros_like(v)) for c in range(nc)], axis=1)


def _pick(r, chunk_of_row, nc):
    out = jnp.where(chunk_of_row == 0, r[:, 0:HGRN_DK], 0.0)
    for c in range(1, nc):
        out = out + jnp.where(chunk_of_row == c, r[:, c * HGRN_DK:(c + 1) * HGRN_DK], 0.0)
    return out


def _mix_fwd(x, g, w_in, lbp, gh, convw_t, w_out, exchange=None):
    t, d = x.shape
    zw = w_in.shape[0]
    w = HGRN_W
    tm = min(TOKEN_TILE, t)
    nc = tm // CHUNK
    n_chunks = t // CHUNK

    def body(x_ref, g_ref, win_ref, lbp_ref, gh_ref, cw_ref, wout_ref,
             xo_ref, z_ref, o_ref, st_ref, y_ref, state, ucarry, qh_buf, kh_buf, kbar_buf, v_buf, etot_buf):
        _zero_at_start(state, ucarry)
        xv = x_ref[...]
        h, _, _ = _rms(xv, g_ref[...])
        z_ref[...] = _dot(h.astype(BF16), win_ref[...], NT)
        z = z_ref[...]
        _, _, f, _, q = _gates(z, lbp_ref[...])
        _decayed_operands(q, f, z[:, 2 * w:3 * w], qh_buf, kh_buf, kbar_buf, v_buf, etot_buf)
        mask = _block_causal_mask(tm)
        chunk_of_row = lax.broadcasted_iota(jnp.int32, (tm, 1), 0) // CHUNK
        heads = range(HGRN_HEADS)
        hcols = [slice(hd * HGRN_DK, (hd + 1) * HGRN_DK) for hd in heads]
        qh = [qh_buf[:, hcols[hd]] for hd in heads]
        vb = [v_buf[:, hcols[hd]] for hd in heads]
        scores = [jnp.where(mask, _dot(qh[hd], kh_buf[:, hcols[hd]], NT), 0.0).astype(BF16) for hd in heads]
        gains = [_dot(_spread(vb[hd], chunk_of_row, nc), kbar_buf[:, hcols[hd]], TN) for hd in heads]
        entering = []
        for hd in heads:
            states, st = [], state[hd]
            for c in range(nc):
                states.append(st)
                st_ref[c, hd] = st
                st = st * etot_buf[c * CHUNK:c * CHUNK + 1, hcols[hd]] + gains[hd][c * HGRN_DK:(c + 1) * HGRN_DK, :]
            state[hd] = st
            entering.append(jnp.concatenate(states, axis=0).astype(BF16))
        from_states = [_dot(qh[hd], entering[hd], NT) for hd in heads]
        o_heads = [_dot(scores[hd], vb[hd]) + _pick(from_states[hd], chunk_of_row, nc) for hd in heads]
        o_ref[...] = jnp.concatenate(o_heads, axis=1)
        ghv = gh_ref[...]
        normed = jnp.concatenate([_rms(o_heads[hd], ghv[:, hcols[hd]])[0] for hd in heads], axis=1)
        zg = z[:, 3 * w:4 * w]
        u = z[:, 5 * w:6 * w] * z[:, 6 * w:7 * w]
        conv = _short_conv(u, ucarry[...], cw_ref[...])
        ucarry[...] = u[tm - 8:tm, :]
        y = jnp.concatenate([normed * (zg * _sigmoid(zg)), z[:, 4 * w:5 * w] * conv], axis=1).astype(BF16)
        y_ref[...] = y
        xo_ref[...] = xv + _dot(y, wout_ref[...])

    return _call(
        body,
        name="mix_fwd",
        grid=(t // tm,),
        in_specs=[
            _rows(tm, d), _full((1, d)), _full((zw, d)), _full((2, w)), _full((1, w)), _full((3, w)),
            _full((2 * w, d)),
        ],
        out_specs=[
            _rows(tm, d), _rows(tm, zw), _rows(tm, w),
            pl.BlockSpec((nc, HGRN_HEADS, HGRN_DK, HGRN_DK), lambda i: (i, 0, 0, 0)),
            _rows(tm, 2 * w),
        ],
        out_shape=[
            jax.ShapeDtypeStruct((t, d), F32),
            jax.ShapeDtypeStruct((t, zw), F32),
            jax.ShapeDtypeStruct((t, w), F32),
            jax.ShapeDtypeStruct((n_chunks, HGRN_HEADS, HGRN_DK, HGRN_DK), F32),
            jax.ShapeDtypeStruct((t, 2 * w), BF16),
        ],
        scratch_shapes=[
            pltpu.VMEM((HGRN_HEADS, HGRN_DK, HGRN_DK), F32), pltpu.VMEM((8, w), F32),
            pltpu.VMEM((tm, w), BF16), pltpu.VMEM((tm, w), BF16), pltpu.VMEM((tm, w), BF16),
            pltpu.VMEM((tm, w), BF16), pltpu.VMEM((tm, w), F32),
        ],
        args=(x, g, w_in, lbp, gh, convw_t, w_out),
        exchange=exchange,
    )


def _mix_bwd(x, g, dxo, z, o, states, w_in, lbp, gh, convw_t, w_out, exchange=None):
    t, d = x.shape
    zw = w_in.shape[0]
    w = HGRN_W
    tm = min(TOKEN_TILE, t)
    nc = tm // CHUNK
    n = t // tm

    def body(x_ref, g_ref, dxo_ref, z_ref, zprev_ref, o_ref, st_ref, win_ref, lbp_ref, gh_ref, cw_ref, wout_ref,
             dx_ref, dz_ref, h_ref, dg_ref, dlbp_ref, dgh_ref, dcw_ref,
             dstate, dcarry, do_buf, qh_buf, kh_buf, kbar_buf, v_buf, etot_buf):
        _zero_at_start(dstate, dcarry, dg_ref, dlbp_ref, dgh_ref, dcw_ref)
        gv = g_ref[...]
        h, xh, r = _rms(x_ref[...], gv)
        h_ref[...] = h.astype(BF16)
        dxo = dxo_ref[...]
        dy = _dot(dxo.astype(BF16), wout_ref[...], NT)
        z = z_ref[...]
        lb, sig, f, sq, q = _gates(z, lbp_ref[...])
        eb, enb, erest = _decayed_operands(q, f, z[:, 2 * w:3 * w], qh_buf, kh_buf, kbar_buf, v_buf, etot_buf)

        ghv = gh_ref[...]
        zg = z[:, 3 * w:4 * w]
        sgz = _sigmoid(zg)
        dyh = dy[:, 0:w]
        don = dyh * (zg * sgz)
        heads = range(HGRN_HEADS)
        hcols = [slice(hd * HGRN_DK, (hd + 1) * HGRN_DK) for hd in heads]
        norms = [_rms(o_ref[:, hcols[hd]], ghv[:, hcols[hd]]) for hd in heads]
        on = jnp.concatenate([norms[hd][0] for hd in heads], axis=1)
        oh = jnp.concatenate([norms[hd][1] for hd in heads], axis=1)
        dz_ref[:, 3 * w:4 * w] = (dyh * on * (sgz * (1.0 + zg * (1.0 - sgz)))).astype(BF16)
        dgh_ref[...] += jnp.sum(don * oh, axis=0, keepdims=True)
        do_buf[...] = jnp.concatenate(
            [_rms_bwd(don[:, hcols[hd]], norms[hd][1], norms[hd][2], ghv[:, hcols[hd]]) for hd in heads],
            axis=1).astype(BF16)

        zb = z[:, 4 * w:5 * w]
        zc = z[:, 5 * w:6 * w]
        zu = z[:, 6 * w:7 * w]
        u = zc * zu
        cw = cw_ref[...]
        zp = zprev_ref[...]
        uprev = jnp.where(pl.program_id(0) == n - 1, 0.0, zp[:, 5 * w:6 * w] * zp[:, 6 * w:7 * w])
        dyc = dy[:, w:2 * w]
        dz_ref[:, 4 * w:5 * w] = (dyc * _short_conv(u, uprev, cw)).astype(BF16)
        dconv = dyc * zb
        edge = dcarry[...]
        dconv1 = _shift_rows(dconv, -1, edge)
        dconv2 = _shift_rows(dconv, -2, edge)
        dcarry[...] = dconv[0:8, :]
        du = cw[2:3, :] * dconv + cw[1:2, :] * dconv1 + cw[0:1, :] * dconv2
        dz_ref[:, 5 * w:6 * w] = (du * zu).astype(BF16)
        dz_ref[:, 6 * w:7 * w] = (du * zc).astype(BF16)
        dcw_ref[...] += jnp.concatenate([
            jnp.sum(u * dconv2, axis=0, keepdims=True),
            jnp.sum(u * dconv1, axis=0, keepdims=True),
            jnp.sum(u * dconv, axis=0, keepdims=True)], axis=0)

        mask = _block_causal_mask(tm)
        chunk_of_row = lax.broadcasted_iota(jnp.int32, (tm, 1), 0) // CHUNK
        heads = range(HGRN_HEADS)
        hcols = [slice(hd * HGRN_DK, (hd + 1) * HGRN_DK) for hd in heads]
        qhb = [qh_buf[:, hcols[hd]] for hd in heads]
        khb = [kh_buf[:, hcols[hd]] for hd in heads]
        vb = [v_buf[:, hcols[hd]] for hd in heads]
        dob = [do_buf[:, hcols[hd]] for hd in heads]
        scores = [jnp.where(mask, _dot(qhb[hd], khb[hd], NT), 0.0).astype(BF16) for hd in heads]
        dscores = [jnp.where(mask, _dot(dob[hd], vb[hd], NT), 0.0).astype(BF16) for hd in heads]
        gains = [_dot(_spread(dob[hd], chunk_of_row, nc), qhb[hd], TN) for hd in heads]
        dst_rows, dst_lanes, st_lanes, carries = [], [], [], []
        for hd in heads:
            entering = [st_ref[c, hd] for c in range(nc)]
            leaving, carried_back = [None] * nc, [None] * nc
            dst = dstate[hd]
            for c in reversed(range(nc)):
                elast = etot_buf[c * CHUNK:c * CHUNK + 1, hcols[hd]]
                leaving[c] = dst
                carried_back[c] = jnp.sum(dst * entering[c], axis=0, keepdims=True) * elast
                dst = dst * elast + gains[hd][c * HGRN_DK:(c + 1) * HGRN_DK, :]
            dstate[hd] = dst
            dst_rows.append(jnp.concatenate(leaving, axis=0).astype(BF16))
            dst_lanes.append(jnp.concatenate(leaving, axis=1).astype(BF16))
            st_lanes.append(jnp.concatenate(entering, axis=1).astype(BF16))
            carries.append(carried_back)
        dv = [_dot(scores[hd], dob[hd], TN) + _pick(_dot(kbar_buf[:, hcols[hd]], dst_rows[hd], NT), chunk_of_row, nc)
              for hd in heads]
        dz_ref[:, 2 * w:3 * w] = jnp.concatenate(dv, axis=1).astype(BF16)
        dqh = jnp.concatenate(
            [_dot(dscores[hd], khb[hd]) + _pick(_dot(dob[hd], st_lanes[hd]), chunk_of_row, nc) for hd in heads], axis=1)
        dkh = jnp.concatenate([_dot(dscores[hd], qhb[hd], TN) for hd in heads], axis=1)
        dkbar = jnp.concatenate([_pick(_dot(vb[hd], dst_lanes[hd]), chunk_of_row, nc) for hd in heads], axis=1)

        kbar_dkbar = kbar_buf[...].astype(F32) * dkbar
        db = qh_buf[...].astype(F32) * dqh - kh_buf[...].astype(F32) * dkh - kbar_dkbar
        through_last = jnp.concatenate([
            jnp.broadcast_to(
                jnp.sum(kbar_dkbar[c * CHUNK:(c + 1) * CHUNK], axis=0, keepdims=True)
                + jnp.concatenate([carries[hd][c] for hd in heads], axis=1),
                (CHUNK, w))
            for c in range(nc)], axis=0)
        dlogf = _chunk_cumsum(db, reverse=True) + through_last
        df = dlogf / f - (dkh * enb + dkbar * erest)
        zq = z[:, 0:w]
        dz_ref[:, 0:w] = (dqh * eb * HGRN_DK ** -0.5 * (sq * (1.0 + zq * (1.0 - sq)))).astype(BF16)
        dz_ref[:, w:2 * w] = (df * (1.0 - lb) * sig * (1.0 - sig)).astype(BF16)
        dlb = jnp.sum(df * (1.0 - sig), axis=0, keepdims=True) * lb * (1.0 - lb)
        dlbp_ref[...] += jnp.concatenate([dlb, -dlb], axis=0)

        dh = _dot(dz_ref[...], win_ref[...])
        dx_ref[...] = _rms_bwd(dh, xh, r, gv) + dxo
        dg_ref[...] += jnp.sum(dh * xh, axis=0, keepdims=True)

    return _call(
        body,
        name="mix_bwd",
        grid=(n,),
        in_specs=[
            _rows_rev(tm, d, n), _full((1, d)), _rows_rev(tm, d, n), _rows_rev(tm, zw, n),
            pl.BlockSpec((8, zw), lambda i: (jnp.maximum((n - 1 - i) * (tm // 8) - 1, 0), 0)),
            _rows_rev(tm, w, n),
            pl.BlockSpec((nc, HGRN_HEADS, HGRN_DK, HGRN_DK), lambda i: (n - 1 - i, 0, 0, 0)),
            _full((zw, d)), _full((2, w)), _full((1, w)), _full((3, w)), _full((2 * w, d)),
        ],
        out_specs=[
            _rows_rev(tm, d, n), _rows_rev(tm, zw, n), _rows_rev(tm, d, n),
            _full((1, d)), _full((2, w)), _full((1, w)), _full((3, w)),
        ],
        out_shape=[
            jax.ShapeDtypeStruct((t, d), F32),
            jax.ShapeDtypeStruct((t, zw), BF16),
            jax.ShapeDtypeStruct((t, d), BF16),
            jax.ShapeDtypeStruct((1, d), F32),
            jax.ShapeDtypeStruct((2, w), F32),
            jax.ShapeDtypeStruct((1, w), F32),
            jax.ShapeDtypeStruct((3, w), F32),
        ],
        scratch_shapes=[
            pltpu.VMEM((HGRN_HEADS, HGRN_DK, HGRN_DK), F32), pltpu.VMEM((8, w), F32),
            pltpu.VMEM((tm, w), BF16),
            pltpu.VMEM((tm, w), BF16), pltpu.VMEM((tm, w), BF16), pltpu.VMEM((tm, w), BF16),
            pltpu.VMEM((tm, w), BF16), pltpu.VMEM((tm, w), F32),
        ],
        args=(x, g, dxo, z, z, o, states, w_in, lbp, gh, convw_t, w_out),
        exchange=exchange,
    )


def _memkv_fwd(mem, g, wkv):
    m, d = mem.shape
    nb, _, cb = wkv.shape

    def body(mem_ref, g_ref, wkv_ref, kv_ref):
        mn, _, _ = _rms(mem_ref[...], g_ref[...])
        mnb = mn.astype(BF16)
        for j in range(nb):
            kv_ref[:, j * cb:(j + 1) * cb] = _dot(mnb, wkv_ref[j]).astype(BF16)

    return pl.pallas_call(
        body,
        name="memkv_fwd",
        out_shape=jax.ShapeDtypeStruct((m, nb * cb), BF16),
        compiler_params=_params(),
    )(mem, g, wkv)


def _memkv_bwd(mem, g, dkv, wkv):
    m, d = mem.shape
    nb, _, cb = wkv.shape
    chips = nb // 2

    def body(mem_ref, g_ref, dkv_ref, wkv_ref, dw_ref, dg_ref, dw_all, send_buf, recv_buf, send_sem, recv_sem):
        x, y, c, _ = _mesh_place()
        sibling, _ = _peer(x, y, c, 1)
        mn, xh, _ = _rms(mem_ref[...], g_ref[...])
        mnb = mn.astype(BF16)
        dmn = jnp.zeros((m, d), F32)
        for j in range(nb):
            dkvb = dkv_ref[:, j * cb:(j + 1) * cb].astype(BF16)
            dw_all[j] = _dot(mnb, dkvb, TN)
            dmn = dmn + _dot(dkvb, wkv_ref[j], NT)
        dg_ref[...] = jnp.sum(dmn * xh, axis=0, keepdims=True)
        for q in range(chips):
            send_buf[q] = dw_all[2 * q + 1 - c].astype(BF16)
        to_sibling = _remote(send_buf, recv_buf, send_sem, recv_sem, sibling)
        to_sibling.start()
        to_sibling.wait_send()
        to_sibling.wait_recv()
        for q in range(chips):
            dw_ref[q] = (dw_all[2 * q + c] + recv_buf[q].astype(F32)).astype(BF16)

    return pl.pallas_call(
        body,
        name="memkv_bwd",
        out_shape=[jax.ShapeDtypeStruct((chips, d, cb), BF16), jax.ShapeDtypeStruct((1, d), F32)],
        scratch_shapes=[
            pltpu.VMEM((nb, d, cb), F32), pltpu.VMEM((chips, d, cb), BF16), pltpu.VMEM((chips, d, cb), BF16),
            pltpu.SemaphoreType.DMA, pltpu.SemaphoreType.DMA,
        ],
        compiler_params=_params(),
    )(mem, g, dkv, wkv)


def _softmax_rows(qm_h, k_h):
    sc = _dot(qm_h, k_h, NT) * MEM_HD ** -0.5
    e = jnp.exp(sc - jnp.max(sc, axis=-1, keepdims=True))
    return e / jnp.sum(e, axis=-1, keepdims=True)


def _xattn_fwd(x, g, wq, kv, wo, exchange=None):
    t, d = x.shape
    m = kv.shape[0]
    tm = min(XATTN_TILE, t)

    def body(x_ref, g_ref, wq_ref, kv_ref, wo_ref, xo_ref, hq_ref, qm_ref, att_ref):
        xv = x_ref[...]
        h, _, _ = _rms(xv, g_ref[...])
        hb = h.astype(BF16)
        hq_ref[...] = hb
        qm = _dot(hb, wq_ref[...]).astype(BF16)
        qm_ref[...] = qm
        heads = range(MEM_HEADS)
        kcols = [slice(hd * MEM_HD, (hd + 1) * MEM_HD) for hd in heads]
        p = [_softmax_rows(qm[:, kcols[hd]], kv_ref[:, kcols[hd]]) for hd in heads]
        att = jnp.concatenate(
            [_dot(p[hd].astype(BF16), kv_ref[:, d + hd * MEM_HD:d + (hd + 1) * MEM_HD]) for hd in heads],
            axis=1).astype(BF16)
        att_ref[...] = att
        xo_ref[...] = xv + _dot(att, wo_ref[...])

    return _call(
        body,
        name="xattn_fwd",
        grid=(t // tm,),
        in_specs=[_rows(tm, d), _full((1, d)), _full((d, d)), _full((m, 2 * d)), _full((d, d))],
        out_specs=[_rows(tm, d), _rows(tm, d), _rows(tm, d), _rows(tm, d)],
        out_shape=[
            jax.ShapeDtypeStruct((t, d), F32),
            jax.ShapeDtypeStruct((t, d), BF16),
            jax.ShapeDtypeStruct((t, d), BF16),
            jax.ShapeDtypeStruct((t, d), BF16),
        ],
        args=(x, g, wq, kv, wo),
        exchange=exchange,
    )


def _xattn_bwd(x, g, dxo, qm, kv, wq, wo, exchange=None):
    t, d = x.shape
    m = kv.shape[0]
    tm = min(XATTN_TILE, t)

    def body(x_ref, g_ref, dxo_ref, qm_ref, kv_ref, wq_ref, wo_ref, dx_ref, dqm_ref, dkv_ref, dg_ref):
        _zero_at_start(dkv_ref, dg_ref)
        gv = g_ref[...]
        _, xh, r = _rms(x_ref[...], gv)
        dxo = dxo_ref[...]
        datt = _dot(dxo.astype(BF16), wo_ref[...], NT).astype(BF16)
        heads = range(MEM_HEADS)
        kcols = [slice(hd * MEM_HD, (hd + 1) * MEM_HD) for hd in heads]
        vcols = [slice(d + hd * MEM_HD, d + (hd + 1) * MEM_HD) for hd in heads]
        qm_h = [qm_ref[:, kcols[hd]] for hd in heads]
        p = [_softmax_rows(qm_h[hd], kv_ref[:, kcols[hd]]) for hd in heads]
        dp = [_dot(datt[:, kcols[hd]], kv_ref[:, vcols[hd]], NT) for hd in heads]
        dsc = [(p[hd] * (dp[hd] - jnp.sum(p[hd] * dp[hd], axis=-1, keepdims=True)) * MEM_HD ** -0.5).astype(BF16)
               for hd in heads]
        dqm = jnp.concatenate([_dot(dsc[hd], kv_ref[:, kcols[hd]]) for hd in heads], axis=1).astype(BF16)
        dqm_ref[...] = dqm
        dkv_ref[...] += jnp.concatenate(
            [_dot(dsc[hd], qm_h[hd], TN) for hd in heads]
            + [_dot(p[hd].astype(BF16), datt[:, kcols[hd]], TN) for hd in heads], axis=1)
        dh = _dot(dqm, wq_ref[...], NT)
        dx_ref[...] = _rms_bwd(dh, xh, r, gv) + dxo
        dg_ref[...] += jnp.sum(dh * xh, axis=0, keepdims=True)

    return _call(
        body,
        name="xattn_bwd",
        grid=(t // tm,),
        in_specs=[
            _rows(tm, d), _full((1, d)), _rows(tm, d), _rows(tm, d), _full((m, 2 * d)), _full((d, d)), _full((d, d)),
        ],
        out_specs=[_rows(tm, d), _rows(tm, d), _full((m, 2 * d)), _full((1, d))],
        out_shape=[
            jax.ShapeDtypeStruct((t, d), F32),
            jax.ShapeDtypeStruct((t, d), BF16),
            jax.ShapeDtypeStruct((m, 2 * d), F32),
            jax.ShapeDtypeStruct((1, d), F32),
        ],
        args=(x, g, dxo, qm, kv, wq, wo),
        exchange=exchange,
    )


def _mesh_place():
    x, y, c = lax.axis_index("x"), lax.axis_index("y"), lax.axis_index("c")
    return x, y, c, 4 * x + 2 * y + c


def _peer(x, y, c, k):
    px = 1 - x if k & 4 else x
    py = 1 - y if k & 2 else y
    pc = 1 - c if k & 1 else c
    return (px, py, pc), 4 * px + 2 * py + pc


ICI_HOPS = (2, 4, 6)
N_HOPS = len(ICI_HOPS)


def _remote(src, dst, send_sem, recv_sem, peer):
    return pltpu.make_async_remote_copy(
        src_ref=src, dst_ref=dst, send_sem=send_sem, recv_sem=recv_sem, device_id=peer, device_id_type=MESH_IDS)


def _gather_exchange(shards, middle_eighths=MIDDLE_EIGHTHS):
    n = len(shards)

    def place():
        x, y, c, me = _mesh_place()
        sibling, _ = _peer(x, y, c, 1)
        to_x, from_x = _peer(x, y, c, 4)
        to_y, from_y = _peer(x, y, c, 2)
        _, from_diagonal = _peer(x, y, c, 6)
        onward = (c * to_y[0] + (1 - c) * to_x[0], c * to_y[1] + (1 - c) * to_x[1], c)
        passed_on = c * from_x + (1 - c) * from_y
        return me, sibling, (to_x, to_y, onward), (from_x, from_y, from_diagonal), passed_on

    def start(src, dst, sems):
        ici_send, ici_recv, pair_send, pair_recv, local = sems
        me, sibling, targets, _, _ = place()
        for a in range(n):
            pltpu.make_async_copy(src[a], dst[a].at[me], local.at[a]).start()
            for j in range(2):
                _remote(src[a], dst[a].at[me], ici_send.at[a, j], ici_recv.at[a, j], targets[j]).start()
            _remote(src[a], dst[a].at[me], pair_send.at[a, 0], pair_recv.at[a, 0], sibling).start()

    def to_sibling(dst, sems, a, j, origin, sibling):
        _, _, pair_send, pair_recv, _ = sems
        slot = dst[a].at[origin]
        return _remote(slot, slot, pair_send.at[a, 1 + j], pair_recv.at[a, 1 + j], sibling)

    def middle(src, dst, sems):
        ici_send, ici_recv, _, _, _ = sems
        _, sibling, targets, origins, passed_on = place()
        for a in range(n):
            for j in range(2):
                _remote(src[a], dst[a].at[origins[j]], ici_send.at[a, j], ici_recv.at[a, j], targets[j]).wait_recv()
            slot = dst[a].at[passed_on]
            _remote(slot, slot, ici_send.at[a, 2], ici_recv.at[a, 2], targets[2]).start()
            for j in range(2):
                to_sibling(dst, sems, a, j, origins[j], sibling).start()

    def finish(src, dst, sems):
        ici_send, ici_recv, pair_send, pair_recv, local = sems
        me, sibling, targets, origins, _ = place()
        for a in range(n):
            _remote(src[a], dst[a].at[origins[2]], ici_send.at[a, 2], ici_recv.at[a, 2], targets[2]).wait_recv()
            to_sibling(dst, sems, a, 2, origins[2], sibling).start()
        for a in range(n):
            pltpu.make_async_copy(src[a], dst[a].at[me], local.at[a]).wait()
            for j in range(N_HOPS):
                _remote(src[a], dst[a].at[me], ici_send.at[a, j], ici_recv.at[a, j], targets[j]).wait_send()
            for j, origin in enumerate((me,) + origins):
                from_sibling = origin + 1 - 2 * (origin % 2)
                passed = _remote(src[a], dst[a].at[from_sibling], pair_send.at[a, j], pair_recv.at[a, j], sibling)
                passed.wait_send()
                passed.wait_recv()

    return _Exchange(
        shards,
        [jax.ShapeDtypeStruct((N_DEV,) + s.shape, s.dtype) for s in shards],
        [
            pltpu.SemaphoreType.DMA((n, N_HOPS)), pltpu.SemaphoreType.DMA((n, N_HOPS)),
            pltpu.SemaphoreType.DMA((n, N_HOPS + 1)), pltpu.SemaphoreType.DMA((n, N_HOPS + 1)),
            pltpu.SemaphoreType.DMA((n,)),
        ],
        start, finish, middle, middle_eighths)


def _scatter_copies(src, dst, sems, n, arrivals=False):
    send, recv, local = sems
    x, y, c, _ = _mesh_place()
    chip = 2 * x + y
    if arrivals is None:
        return [pltpu.make_async_copy(src[a].at[chip], dst[a].at[chip], local.at[a]) for a in range(n)]
    copies = []
    for a in range(n):
        for j, k in enumerate(ICI_HOPS):
            peer, _ = _peer(x, y, c, k)
            peer_chip = 2 * peer[0] + peer[1]
            slot = dst[a].at[peer_chip if arrivals else chip]
            copies.append(_remote(src[a].at[peer_chip], slot, send.at[a, j], recv.at[a, j], peer))
    return copies


def _scatter_start(src, dst, sems, n):
    for cp in _scatter_copies(src, dst, sems, n, arrivals=None) + _scatter_copies(src, dst, sems, n):
        cp.start()


def _scatter_finish(src, dst, sems, n):
    for cp in _scatter_copies(src, dst, sems, n, arrivals=None):
        cp.wait()
    for cp in _scatter_copies(src, dst, sems, n):
        cp.wait_send()
    for cp in _scatter_copies(src, dst, sems, n, arrivals=True):
        cp.wait_recv()


def _scatter_scratch(n):
    return [pltpu.SemaphoreType.DMA((n, N_HOPS)), pltpu.SemaphoreType.DMA((n, N_HOPS)), pltpu.SemaphoreType.DMA((n,))]


def _scatter_exchange(partials):
    n = len(partials)
    return _Exchange(
        partials, [jax.ShapeDtypeStruct(p.shape, p.dtype) for p in partials], _scatter_scratch(n),
        lambda src, dst, sems: _scatter_start(src, dst, sems, n),
        lambda src, dst, sems: _scatter_finish(src, dst, sems, n))


SMALL_LAYOUT = {
    "ffn1_norm": (0, 1, 1024), "mix_norm": (1, 1, 1024), "xattn_norm": (2, 1, 1024), "mem_norm": (3, 1, 1024),
    "ffn2_norm": (4, 1, 1024), "final_norm": (5, 1, 1024), "lb_param": (6, 2, 512), "hgrn_out_norm": (8, 1, 512),
    "conv_w": (9, 3, 512), "loss": (12, 1, 128),
}


def _final_exchange(partials, small):
    n = len(partials)
    names = list(small)
    width = 1024

    def body(*refs):
        src = refs[:n]
        pieces = refs[n:n + len(names)]
        dst = refs[n + len(names):2 * n + len(names)]
        total_ref = refs[2 * n + len(names)]
        pack, gathered, small_send, small_recv = refs[2 * n + len(names) + 1:2 * n + len(names) + 5]
        sems = refs[2 * n + len(names) + 5:]
        x, y, c, me = _mesh_place()
        pack[...] = jnp.zeros_like(pack)
        for name, piece in zip(names, pieces):
            row, nrows, ncols = SMALL_LAYOUT[name]
            pack[row:row + nrows, 0:ncols] = piece[...]
        for k in range(1, N_DEV):
            peer, _ = _peer(x, y, c, k)
            _remote(pack, gathered.at[me], small_send.at[k - 1], small_recv.at[k - 1], peer).start()
        _scatter_start(src, dst, sems, n)
        gathered[me] = pack[...]
        for k in range(1, N_DEV):
            peer, peer_index = _peer(x, y, c, k)
            landed = _remote(pack, gathered.at[peer_index], small_send.at[k - 1], small_recv.at[k - 1], peer)
            landed.wait_send()
            landed.wait_recv()
        total = gathered[0]
        for j in range(1, N_DEV):
            total = total + gathered[j]
        total_ref[...] = total
        _scatter_finish(src, dst, sems, n)

    hbm = pl.BlockSpec(memory_space=pltpu.HBM)
    vmem = pl.BlockSpec(memory_space=pltpu.VMEM)
    out = pl.pallas_call(
        body,
        name="final_exchange",
        in_specs=[hbm] * n + [vmem] * len(names),
        out_specs=[hbm] * n + [vmem],
        out_shape=[jax.ShapeDtypeStruct(p.shape, p.dtype) for p in partials]
        + [jax.ShapeDtypeStruct((SMALL_ROWS, width), F32)],
        scratch_shapes=[
            pltpu.VMEM((SMALL_ROWS, width), F32), pltpu.VMEM((N_DEV, SMALL_ROWS, width), F32),
            pltpu.SemaphoreType.DMA((N_DEV - 1,)), pltpu.SemaphoreType.DMA((N_DEV - 1,)),
        ] + _scatter_scratch(n),
        compiler_params=pltpu.CompilerParams(has_side_effects=True),
    )(*partials, *[small[k] for k in names])
    return out[:n], out[n]


def _adamw_math(w, g, m, v):
    m = ADAM_B1 * m + (1.0 - ADAM_B1) * g
    v = ADAM_B2 * v + (1.0 - ADAM_B2) * (g * g)
    m_hat = m / (1.0 - ADAM_B1 ** ADAM_STEP)
    v_hat = v / (1.0 - ADAM_B2 ** ADAM_STEP)
    delta = -ADAM_LR * (m_hat / (jnp.sqrt(v_hat) + ADAM_EPS) + ADAM_WD * w)
    return delta, m, v


def _adamw_shard(parts, w, m, v):
    r, c = w.shape
    n_parts = parts.shape[0]
    tr = max(rows for rows in range(16, r + 1, 16) if r % rows == 0 and rows * c <= ADAMW_TILE_ELEMENTS)

    def body(p_ref, w_ref, m_ref, v_ref, g_ref, d_ref, mo_ref, vo_ref):
        g = p_ref[0].astype(F32)
        for j in range(1, n_parts):
            g = g + p_ref[j].astype(F32)
        delta, mn, vn = _adamw_math(w_ref[...], g, m_ref[...], v_ref[...])
        g_ref[...] = g
        d_ref[...] = delta
        mo_ref[...] = mn
        vo_ref[...] = vn

    tile = pl.BlockSpec((tr, c), lambda i: (i, 0))
    return pl.pallas_call(
        body,
        name="adamw_shard",
        grid=(r // tr,),
        in_specs=[pl.BlockSpec((n_parts, tr, c), lambda i: (0, i, 0)), tile, tile, tile],
        out_specs=[tile] * 4,
        out_shape=[jax.ShapeDtypeStruct((r, c), F32)] * 4,
        compiler_params=_params(("parallel",)),
    )(parts, w, m, v)


def _adamw_small(gs, ws, ms, vs):
    n = len(gs)

    def body(*refs):
        g_refs, w_refs, m_refs, v_refs = refs[:n], refs[n:2 * n], refs[2 * n:3 * n], refs[3 * n:4 * n]
        d_out, m_out, v_out = refs[4 * n:5 * n], refs[5 * n:6 * n], refs[6 * n:7 * n]
        for i in range(n):
            delta, mn, vn = _adamw_math(w_refs[i][...], g_refs[i][...], m_refs[i][...], v_refs[i][...])
            d_out[i][...] = delta
            m_out[i][...] = mn
            v_out[i][...] = vn

    shapes = [jax.ShapeDtypeStruct(w.shape, F32) for w in ws]
    out = pl.pallas_call(
        body,
        name="adamw_small",
        out_shape=shapes * 3,
        compiler_params=_params(),
    )(*gs, *ws, *ms, *vs)
    return out[:n], out[n:2 * n], out[2 * n:]


TRANSPOSED = ("ffn1_gate", "ffn1_up", "w_in", "ffn2_gate", "ffn2_up", "conv_w")
GROUP_FFN1 = ("ffn1_gate", "ffn1_up", "ffn1_down")
GROUP_MIX = ("w_in", "w_out")
GROUP_XATTN = ("w_q_mem", "w_kv_mem", "w_o_mem")
GROUP_FFN2 = ("ffn2_gate", "ffn2_up", "ffn2_down")
LARGE = GROUP_FFN1 + GROUP_MIX + GROUP_XATTN + GROUP_FFN2
SMALL = ("ffn1_norm", "mix_norm", "lb_param", "hgrn_out_norm", "conv_w", "xattn_norm", "mem_norm", "ffn2_norm",
         "final_norm")
WEIGHTS = ("ffn1_norm", "ffn1_gate", "ffn1_up", "ffn1_down", "mix_norm", "w_in", "lb_param", "hgrn_out_norm",
           "conv_w", "w_out", "xattn_norm", "mem_norm", "w_q_mem", "w_kv_mem", "w_o_mem", "ffn2_norm", "ffn2_gate",
           "ffn2_up", "ffn2_down", "final_norm")


def kernel(x, mem, ffn1_norm, ffn1_gate, ffn1_up, ffn1_down, mix_norm, w_in, lb_param, hgrn_out_norm, conv_w, w_out, xattn_norm, mem_norm, w_q_mem, w_kv_mem, w_o_mem, ffn2_norm, ffn2_gate, ffn2_up, ffn2_down, final_norm, loss_target, m_ffn1_norm, m_ffn1_gate, m_ffn1_up, m_ffn1_down, m_mix_norm, m_w_in, m_lb_param, m_hgrn_out_norm, m_conv_w, m_w_out, m_xattn_norm, m_mem_norm, m_w_q_mem, m_w_kv_mem, m_w_o_mem, m_ffn2_norm, m_ffn2_gate, m_ffn2_up, m_ffn2_down, m_final_norm, v_ffn1_norm, v_ffn1_gate, v_ffn1_up, v_ffn1_down, v_mix_norm, v_w_in, v_lb_param, v_hgrn_out_norm, v_conv_w, v_w_out, v_xattn_norm, v_mem_norm, v_w_q_mem, v_w_kv_mem, v_w_o_mem, v_ffn2_norm, v_ffn2_gate, v_ffn2_up, v_ffn2_down, v_final_norm):
    given = dict(locals())
    me = 4 * lax.axis_index("x") + 2 * lax.axis_index("y") + lax.axis_index("c")
    x0, memv, target = x[0], mem[0], loss_target[0]

    def shard(prefix, name):
        v = given[prefix + name]
        if v.ndim == 1:
            return v.reshape(1, -1)
        if v.ndim == 2:
            return v
        return v[0].T if name in TRANSPOSED else v[0]

    w = {name: shard("", name) for name in WEIGHTS}
    m = {name: shard("m_", name) for name in WEIGHTS}
    v = {name: shard("v_", name) for name in WEIGHTS}

    conv_taps, conv_rows = w["conv_w"].shape
    conv_tile = jnp.pad(w["conv_w"], ((0, 8 - conv_taps), (0, 128 - conv_rows)))
    wire = {name: w[name].astype(BF16) for name in LARGE}
    full = {}

    def landed(names, gathered):
        for name, blocks in zip(names, gathered):
            _, r, c = blocks.shape
            full[name] = blocks if name == "w_kv_mem" else blocks.reshape(N_DEV * r, c)

    first = ("ffn1_gate", "ffn1_up")
    landed(first, _run_exchange(_gather_exchange([wire[k] for k in first]), "gather_first"))

    riders = (("ffn1_down", "w_in"), ("w_out", "w_kv_mem"), ("w_q_mem", "w_o_mem", "ffn2_gate", "ffn2_up"),
              ("ffn2_down",))
    (a1, b1, s1), gathered = _ffn_up(
        x0, w["ffn1_norm"], full["ffn1_gate"], full["ffn1_up"],
        exchange=_gather_exchange([wire[k] for k in riders[0]]))
    landed(riders[0], gathered)
    (x1,), gathered = _ffn_down(
        x0, s1, full["ffn1_down"], exchange=_gather_exchange([wire[k] for k in riders[1]] + [conv_tile]))
    landed(riders[1], gathered)
    convw_t = gathered[-1][:, :conv_taps, :conv_rows].transpose(1, 0, 2).reshape(conv_taps, N_DEV * conv_rows)
    (x2, z, o_raw, states, ycat), gathered = _mix_fwd(
        x1, w["mix_norm"], full["w_in"], w["lb_param"], w["hgrn_out_norm"], convw_t, full["w_out"],
        exchange=_gather_exchange([wire[k] for k in riders[2]]))
    landed(riders[2], gathered)
    kv = _memkv_fwd(memv, w["mem_norm"], full["w_kv_mem"])
    (x3, hq, qm, att), gathered = _xattn_fwd(
        x2, w["xattn_norm"], full["w_q_mem"], kv, full["w_o_mem"],
        exchange=_gather_exchange([wire[k] for k in riders[3]], middle_eighths=EARLY_MIDDLE_EIGHTHS))
    landed(riders[3], gathered)
    (dx4, a2, b2, s2, loss_part, d_final), _ = _ffn_fwd(
        x3, w["ffn2_norm"], full["ffn2_gate"], full["ffn2_up"], full["ffn2_down"], head=(w["final_norm"], target))

    parts = {}
    waiting = []

    def carried():
        names = [name for name, _ in waiting]
        exchange = _scatter_exchange([p for _, p in waiting]) if waiting else None
        del waiting[:]
        return names, exchange

    def weight_grad(name, a, b, scale=1.0):
        names, exchange = carried()
        partial, arrived = _weight_grad(a, b, scale, exchange=exchange)
        parts.update(zip(names, arrived))
        waiting.append((name, partial))

    (dx3, da2, db2, h4, d_ffn2_norm), _ = _ffn_bwd(
        x3, w["ffn2_norm"], dx4, a2, b2, full["ffn2_gate"], full["ffn2_up"], full["ffn2_down"])
    weight_grad("ffn2_down", s2, dx4, 0.5)
    weight_grad("ffn2_gate", da2, h4)
    weight_grad("ffn2_up", db2, h4)
    names, exchange = carried()
    (dx2, dqm, dkv, d_xattn_norm), arrived = _xattn_bwd(
        x2, w["xattn_norm"], dx3, qm, kv, full["w_q_mem"], full["w_o_mem"], exchange=exchange)
    parts.update(zip(names, arrived))
    d_wkv, d_mem_norm = _memkv_bwd(memv, w["mem_norm"], dkv, full["w_kv_mem"])
    waiting.append(("w_kv_mem", d_wkv))
    names, exchange = carried()
    (dx1, dz, h2, d_mix_norm, d_lbp, d_gh, d_convw_t), arrived = _mix_bwd(
        x1, w["mix_norm"], dx2, z, o_raw, states, full["w_in"], w["lb_param"], w["hgrn_out_norm"], convw_t,
        full["w_out"], exchange=exchange)
    parts.update(zip(names, arrived))
    weight_grad("w_in", dz, h2)
    weight_grad("ffn1_down", s1, dx1, 0.5)
    (dx0, da1, db1, h1, d_ffn1_norm), _ = _ffn_bwd(
        x0, w["ffn1_norm"], dx1, a1, b1, full["ffn1_gate"], full["ffn1_up"], full["ffn1_down"])
    weight_grad("ffn1_gate", da1, h1)
    weight_grad("ffn1_up", db1, h1)
    weight_grad("w_o_mem", att, dx3)
    weight_grad("w_q_mem", hq, dqm)
    weight_grad("w_out", ycat, dx2)

    small_parts = {
        "ffn1_norm": d_ffn1_norm, "mix_norm": d_mix_norm, "xattn_norm": d_xattn_norm, "mem_norm": d_mem_norm,
        "ffn2_norm": d_ffn2_norm, "final_norm": d_final, "lb_param": d_lbp, "hgrn_out_norm": d_gh,
        "conv_w": d_convw_t, "loss": loss_part,
    }
    names = [name for name, _ in waiting]
    arrived, total = _final_exchange([p for _, p in waiting], small_parts)
    parts.update(zip(names, arrived))

    g_out, d_out, m_out, v_out = {}, {}, {}, {}
    for name in LARGE:
        g_out[name], d_out[name], m_out[name], v_out[name] = _adamw_shard(parts[name], w[name], m[name], v[name])
    g_small = {}
    for name in SMALL:
        row, nrows, ncols = SMALL_LAYOUT[name]
        g_small[name] = total[row:row + nrows, 0:ncols]
    g_small["conv_w"] = lax.dynamic_slice_in_dim(g_small["conv_w"], me * conv_rows, conv_rows, axis=1)
    ds, ms, vs = _adamw_small(
        [g_small[k] for k in SMALL], [w[k] for k in SMALL], [m[k] for k in SMALL], [v[k] for k in SMALL])
    for i, name in enumerate(SMALL):
        g_out[name], d_out[name], m_out[name], v_out[name] = g_small[name], ds[i], ms[i], vs[i]

    def shaped(value, name):
        return (value.T if name in TRANSPOSED else value).reshape(given[name].shape)

    loss = total[SMALL_LAYOUT["loss"][0], 0]
    outs = [loss, dx0.reshape(x.shape)]
    for group in (g_out, d_out, m_out, v_out):
        outs += [shaped(group[name], name) for name in WEIGHTS]
    return tuple(outs)
```

```python
import jax
import jax.numpy as jnp
from jax import lax
from jax.experimental import pallas as pl
from jax.experimental.pallas import tpu as pltpu

F32 = jnp.float32
BF16 = jnp.bfloat16
MESH_IDS = pl.DeviceIdType.MESH

N_DEV = 8
EPS = 1e-6
HGRN_HEADS = 4
HGRN_DK = 128
HGRN_W = 512
CHUNK = 64
MEM_HEADS = 4
MEM_HD = 256
ADAM_LR = 0.001
ADAM_B1 = 0.9
ADAM_B2 = 0.999
ADAM_EPS = 1e-08
ADAM_WD = 0.01
ADAM_STEP = 10

TOKEN_TILE = 256
XATTN_TILE = 512
WIDE_TILE = 512
REDUCE_TILE = 1024
ADAMW_TILE_ELEMENTS = 256 * 1024
MIDDLE_EIGHTHS = 5
EARLY_MIDDLE_EIGHTHS = 4
MXU_ROWS = 256
VMEM_LIMIT = 60 * 1024 * 1024
SMALL_ROWS = 16
NT = (((1,), (1,)), ((), ()))
TN = (((0,), (0,)), ((), ()))


def _params(sem=None):
    return pltpu.CompilerParams(dimension_semantics=sem, vmem_limit_bytes=VMEM_LIMIT)


def _dot(a, b, dims=None):
    if dims is None:
        return jnp.dot(a, b, preferred_element_type=F32)
    return lax.dot_general(a, b, dims, preferred_element_type=F32)


def _sigmoid(v):
    return 1.0 / (1.0 + jnp.exp(-v))


def _rms(x, g):
    r = lax.rsqrt(jnp.mean(x * x, axis=-1, keepdims=True) + EPS)
    xh = x * r
    return xh * g, xh, r


def _rms_bwd(dh, xh, r, g):
    dxh = dh * g
    return r * (dxh - xh * jnp.mean(dxh * xh, axis=-1, keepdims=True))


def _full(shape):
    return pl.BlockSpec(shape, lambda *_: (0,) * len(shape))


def _full_once(shape):
    return pl.BlockSpec(shape, lambda *_: (0,) * len(shape), pipeline_mode=pl.Buffered(1))


def _rows(tm, width):
    return pl.BlockSpec((tm, width), lambda i: (i, 0))


def _rows_rev(tm, width, n):
    return pl.BlockSpec((tm, width), lambda i: (n - 1 - i, 0))


def _zero_at_start(*refs):
    @pl.when(pl.program_id(0) == 0)
    def _():
        for ref in refs:
            ref[...] = jnp.zeros_like(ref)


class _Exchange:
    def __init__(self, operands, out_shapes, scratch, start, finish, middle=None, middle_eighths=MIDDLE_EIGHTHS):
        self.operands, self.out_shapes, self.scratch = list(operands), list(out_shapes), list(scratch)
        self.start, self.middle, self.finish, self.middle_eighths = start, middle, finish, middle_eighths


def _call(body, *, name, grid, in_specs, out_specs, out_shape, args, scratch_shapes=(), exchange=None):
    semantics = ("arbitrary",) * len(grid)
    if exchange is None:
        out = pl.pallas_call(
            body, name=name, grid=grid, in_specs=in_specs, out_specs=out_specs, out_shape=out_shape,
            scratch_shapes=list(scratch_shapes), compiler_params=_params(semantics))(*args)
        return out, []
    hbm = pl.BlockSpec(memory_space=pltpu.HBM)
    n_in, n_out, n_scr = len(in_specs), len(out_specs), len(scratch_shapes)
    e_in, e_out = len(exchange.operands), len(exchange.out_shapes)

    def carried(*refs):
        ins, rest = refs[:n_in], refs[n_in:]
        e_ins, rest = rest[:e_in], rest[e_in:]
        outs, rest = rest[:n_out], rest[n_out:]
        e_outs, rest = rest[:e_out], rest[e_out:]
        scr, e_scr = rest[:n_scr], rest[n_scr:]
        first = last = None
        for axis, size in enumerate(grid):
            at_start, at_end = pl.program_id(axis) == 0, pl.program_id(axis) == size - 1
            first = at_start if first is None else jnp.logical_and(first, at_start)
            last = at_end if last is None else jnp.logical_and(last, at_end)

        @pl.when(first)
        def _():
            exchange.start(e_ins, e_outs, e_scr)

        body(*ins, *outs, *scr)

        if exchange.middle is not None:
            assert len(grid) == 1

            @pl.when(pl.program_id(0) == (grid[0] * exchange.middle_eighths) // 8)
            def _():
                exchange.middle(e_ins, e_outs, e_scr)

        @pl.when(last)
        def _():
            exchange.finish(e_ins, e_outs, e_scr)

    out = pl.pallas_call(
        carried, name=name, grid=grid, in_specs=list(in_specs) + [hbm] * e_in,
        out_specs=list(out_specs) + [hbm] * e_out, out_shape=list(out_shape) + exchange.out_shapes,
        scratch_shapes=list(scratch_shapes) + exchange.scratch,
        compiler_params=pltpu.CompilerParams(
            dimension_semantics=semantics, vmem_limit_bytes=VMEM_LIMIT, has_side_effects=True),
    )(*args, *exchange.operands)
    return out[:n_out], out[n_out:]


def _run_exchange(exchange, name):
    hbm = pl.BlockSpec(memory_space=pltpu.HBM)
    e_in, e_out = len(exchange.operands), len(exchange.out_shapes)

    def body(*refs):
        e_ins, e_outs, e_scr = refs[:e_in], refs[e_in:e_in + e_out], refs[e_in + e_out:]
        exchange.start(e_ins, e_outs, e_scr)
        if exchange.middle is not None:
            exchange.middle(e_ins, e_outs, e_scr)
        exchange.finish(e_ins, e_outs, e_scr)

    return pl.pallas_call(
        body, name=name, in_specs=[hbm] * e_in, out_specs=[hbm] * e_out, out_shape=exchange.out_shapes,
        scratch_shapes=exchange.scratch, compiler_params=pltpu.CompilerParams(has_side_effects=True),
    )(*exchange.operands)


def _loss_head(xo, gf, tgt):
    d = xo.shape[1]
    y, xh, r = _rms(xo, gf)
    err = y - tgt
    dy = err * (1.0 / d)
    loss = 0.5 * jnp.sum(jnp.sum(err * err, axis=-1, keepdims=True) * (1.0 / d), axis=0, keepdims=True)
    return _rms_bwd(dy, xh, r, gf), loss, jnp.sum(dy * xh, axis=0, keepdims=True)


def _ffn_fwd(x, g, wg, wu, wd, exchange=None, head=None):
    t, d = x.shape
    f = wg.shape[0]
    tm = min(WIDE_TILE, t)

    def body(x_ref, g_ref, wg_ref, wu_ref, wd_ref, *rest):
        if head is None:
            xo_ref, a_ref, b_ref, s_ref = rest
        else:
            gf_ref, tgt_ref, xo_ref, a_ref, b_ref, s_ref, loss_ref, dgf_ref = rest
            _zero_at_start(loss_ref, dgf_ref)
        xv = x_ref[...]
        h, _, _ = _rms(xv, g_ref[...])
        hb = h.astype(BF16)
        a = _dot(hb, wg_ref[...], NT)
        b = _dot(hb, wu_ref[...], NT)
        s = (a * _sigmoid(a) * b).astype(BF16)
        xo = xv + 0.5 * _dot(s, wd_ref[...])
        if head is None:
            xo_ref[...] = xo
        else:
            xo_ref[...], loss, dgf = _loss_head(xo, gf_ref[...], tgt_ref[...])
            loss_ref[...] += jnp.broadcast_to(loss, (1, 128))
            dgf_ref[...] += dgf
        a_ref[...] = a.astype(BF16)
        b_ref[...] = b.astype(BF16)
        s_ref[...] = s

    in_specs = [_rows(tm, d), _full((1, d)), _full_once((f, d)), _full_once((f, d)), _full_once((f, d))]
    out_specs = [_rows(tm, d), _rows(tm, f), _rows(tm, f), _rows(tm, f)]
    out_shape = [
        jax.ShapeDtypeStruct((t, d), F32),
        jax.ShapeDtypeStruct((t, f), BF16),
        jax.ShapeDtypeStruct((t, f), BF16),
        jax.ShapeDtypeStruct((t, f), BF16),
    ]
    args = (x, g, wg, wu, wd)
    if head is not None:
        in_specs += [_full((1, d)), _rows(tm, d)]
        out_specs += [_full((1, 128)), _full((1, d))]
        out_shape += [jax.ShapeDtypeStruct((1, 128), F32), jax.ShapeDtypeStruct((1, d), F32)]
        args += tuple(head)
    return _call(
        body, name="ffn_fwd", grid=(t // tm,), in_specs=in_specs, out_specs=out_specs, out_shape=out_shape,
        args=args, exchange=exchange)


def _ffn_up(x, g, wg, wu, exchange=None):
    t, d = x.shape
    f = wg.shape[0]
    tm = min(TOKEN_TILE, t)

    def body(x_ref, g_ref, wg_ref, wu_ref, a_ref, b_ref, s_ref):
        h, _, _ = _rms(x_ref[...], g_ref[...])
        hb = h.astype(BF16)
        a = _dot(hb, wg_ref[...], NT)
        b = _dot(hb, wu_ref[...], NT)
        a_ref[...] = a.astype(BF16)
        b_ref[...] = b.astype(BF16)
        s_ref[...] = (a * _sigmoid(a) * b).astype(BF16)

    return _call(
        body, name="ffn_up", grid=(t // tm,),
        in_specs=[_rows(tm, d), _full((1, d)), _full_once((f, d)), _full_once((f, d))],
        out_specs=[_rows(tm, f)] * 3, out_shape=[jax.ShapeDtypeStruct((t, f), BF16)] * 3,
        args=(x, g, wg, wu), exchange=exchange)


def _ffn_down(x, s, wd, exchange=None):
    t, d = x.shape
    f = wd.shape[0]
    tm = min(TOKEN_TILE, t)

    def body(x_ref, s_ref, wd_ref, xo_ref):
        xo_ref[...] = x_ref[...] + 0.5 * _dot(s_ref[...], wd_ref[...])

    return _call(
        body, name="ffn_down", grid=(t // tm,),
        in_specs=[_rows(tm, d), _rows(tm, f), _full_once((f, d))],
        out_specs=[_rows(tm, d)], out_shape=[jax.ShapeDtypeStruct((t, d), F32)],
        args=(x, s, wd), exchange=exchange)


def _ffn_bwd(x, g, dxo, a, b, wg, wu, wd, exchange=None):
    t, d = x.shape
    f = wg.shape[0]
    tm = min(TOKEN_TILE, t)

    def body(x_ref, g_ref, dxo_ref, a_ref, b_ref, wg_ref, wu_ref, wd_ref, dx_ref, da_ref, db_ref, h_ref, dg_ref):
        _zero_at_start(dg_ref)
        gv = g_ref[...]
        h, xh, r = _rms(x_ref[...], gv)
        dxo = dxo_ref[...]
        ds = _dot((0.5 * dxo).astype(BF16), wd_ref[...], NT)
        af = a_ref[...].astype(F32)
        bf = b_ref[...].astype(F32)
        sg = _sigmoid(af)
        da = (ds * bf * (sg * (1.0 + af * (1.0 - sg)))).astype(BF16)
        db = (ds * (af * sg)).astype(BF16)
        dh = _dot(da, wg_ref[...]) + _dot(db, wu_ref[...])
        dx_ref[...] = _rms_bwd(dh, xh, r, gv) + dxo
        da_ref[...] = da
        db_ref[...] = db
        h_ref[...] = h.astype(BF16)
        dg_ref[...] += jnp.sum(dh * xh, axis=0, keepdims=True)

    return _call(
        body,
        name="ffn_bwd",
        grid=(t // tm,),
        in_specs=[
            _rows(tm, d), _full((1, d)), _rows(tm, d), _rows(tm, f), _rows(tm, f),
            _full_once((f, d)), _full_once((f, d)), _full_once((f, d)),
        ],
        out_specs=[_rows(tm, d), _rows(tm, f), _rows(tm, f), _rows(tm, d), _full((1, d))],
        out_shape=[
            jax.ShapeDtypeStruct((t, d), F32),
            jax.ShapeDtypeStruct((t, f), BF16),
            jax.ShapeDtypeStruct((t, f), BF16),
            jax.ShapeDtypeStruct((t, d), BF16),
            jax.ShapeDtypeStruct((1, d), F32),
        ],
        args=(x, g, dxo, a, b, wg, wu, wd),
        exchange=exchange,
    )


def _weight_grad(a, b, scale=1.0, exchange=None):
    t, m = a.shape
    n = b.shape[1]
    chips = N_DEV // 2
    r = m // N_DEV
    tk = min(REDUCE_TILE, t)
    halves = 2
    nb = n // halves
    nk = t // tk

    def body(a_ref, b_ref, o_ref, acc, send_buf, recv_buf, send_sems, recv_sems):
        k, j = pl.program_id(0), pl.program_id(1)
        x, y, c, _ = _mesh_place()
        sibling, _ = _peer(x, y, c, 1)
        bv = b_ref[...]
        if scale != 1.0:
            bv = bv * scale
        bb = bv.astype(BF16)
        acc_half = acc.at[j]

        @pl.when(k == 0)
        def _():
            acc_half[...] = jnp.zeros_like(acc_half)

        for i in range(m // MXU_ROWS):
            rows = slice(i * MXU_ROWS, (i + 1) * MXU_ROWS)
            acc_half[rows, :] += _dot(a_ref[:, rows].astype(BF16), bb, TN)

        def to_sibling(half):
            return _remote(send_buf.at[half], recv_buf.at[half], send_sems.at[half], recv_sems.at[half], sibling)

        def owned_rows(q, core):
            return pl.ds(pl.multiple_of((2 * q + core) * r, 8), r)

        for half in range(halves):
            @pl.when(jnp.logical_and(k == nk - 1, j == half))
            def _():
                for q in range(chips):
                    send_buf[half, q] = acc[half, owned_rows(q, 1 - c), :].astype(BF16)
                to_sibling(half).start()

        @pl.when(jnp.logical_and(k == nk - 1, j == halves - 1))
        def _():
            for half in range(halves):
                to_sibling(half).wait_send()
                to_sibling(half).wait_recv()
                for q in range(chips):
                    o_ref[q, :, half * nb:(half + 1) * nb] = (
                        acc[half, owned_rows(q, c), :] + recv_buf[half, q].astype(F32)).astype(BF16)

    (partial,), arrived = _call(
        body,
        name="weight_grad",
        grid=(nk, halves),
        in_specs=[pl.BlockSpec((tk, m), lambda k, j: (k, 0)), pl.BlockSpec((tk, nb), lambda k, j: (k, j))],
        out_specs=[pl.BlockSpec((chips, r, n), lambda k, j: (0, 0, 0))],
        out_shape=[jax.ShapeDtypeStruct((chips, r, n), BF16)],
        scratch_shapes=[
            pltpu.VMEM((halves, m, nb), F32),
            pltpu.VMEM((halves, chips, r, nb), BF16), pltpu.VMEM((halves, chips, r, nb), BF16),
            pltpu.SemaphoreType.DMA((halves,)), pltpu.SemaphoreType.DMA((halves,)),
        ],
        args=(a, b),
        exchange=exchange,
    )
    return partial, arrived


def _chunk_cumsum(v, reverse=False):
    n, width = v.shape
    row = lax.broadcasted_iota(jnp.int32, (n, n), 0)
    col = lax.broadcasted_iota(jnp.int32, (n, n), 1)
    earlier = col >= row if reverse else col <= row
    tri = jnp.where(jnp.logical_and(row // CHUNK == col // CHUNK, earlier), 1.0, 0.0).astype(BF16)
    hi = v.astype(BF16)
    rest = v - hi.astype(F32)
    mid = rest.astype(BF16)
    low = (rest - mid.astype(F32)).astype(BF16)
    sums = _dot(tri, jnp.concatenate([hi, mid, low], axis=1))
    return sums[:, 0:width] + sums[:, width:2 * width] + sums[:, 2 * width:3 * width]


def _shift_rows(v, shift, edge):
    n = v.shape[0]
    row = lax.broadcasted_iota(jnp.int32, (n, 1), 0)
    out = pltpu.roll(v, shift % n, axis=0)
    if shift > 0:
        for j in range(shift):
            out = jnp.where(row == j, edge[8 - shift + j:8 - shift + j + 1, :], out)
    else:
        for j in range(-shift):
            out = jnp.where(row == n + shift + j, edge[j:j + 1, :], out)
    return out


def _gates(z, lbp):
    w = HGRN_W
    lb = _sigmoid(lbp[0:1, :] - lbp[1:2, :])
    zq = z[:, 0:w]
    sig = _sigmoid(z[:, w:2 * w])
    f = lb + (1.0 - lb) * sig
    sq = _sigmoid(zq)
    q = zq * sq * HGRN_DK ** -0.5
    return lb, sig, f, sq, q


def _decayed_operands(q, f, v, qh_buf, kh_buf, kbar_buf, v_buf, etot_buf):
    n, width = f.shape
    bcum = _chunk_cumsum(jnp.log(f))
    total = jnp.concatenate(
        [jnp.broadcast_to(bcum[c + CHUNK - 1:c + CHUNK, :], (CHUNK, width)) for c in range(0, n, CHUNK)], axis=0)
    eb, enb, erest = jnp.exp(bcum), jnp.exp(-bcum), jnp.exp(total - bcum)
    kk = 1.0 - f
    qh_buf[...] = (q * eb).astype(BF16)
    kh_buf[...] = (kk * enb).astype(BF16)
    kbar_buf[...] = (kk * erest).astype(BF16)
    v_buf[...] = v.astype(BF16)
    etot_buf[...] = jnp.exp(total)
    return eb, enb, erest


def _short_conv(u, edge, cw):
    return cw[0:1, :] * _shift_rows(u, 2, edge) + cw[1:2, :] * _shift_rows(u, 1, edge) + cw[2:3, :] * u


def _block_causal_mask(n):
    row = lax.broadcasted_iota(jnp.int32, (n, n), 0)
    col = lax.broadcasted_iota(jnp.int32, (n, n), 1)
    return jnp.logical_and(row // CHUNK == col // CHUNK, col <= row)


def _spread(v, chunk_of_row, nc):
    return jnp.concatenate([jnp.where(chunk_of_row == c, v, jnp.zeros_like(v)) for c in range(nc)], axis=1)


def _pick(r, chunk_of_row, nc):
    out = jnp.where(chunk_of_row == 0, r[:, 0:HGRN_DK], 0.0)
    for c in range(1, nc):
        out = out + jnp.where(chunk_of_row == c, r[:, c * HGRN_DK:(c + 1) * HGRN_DK], 0.0)
    return out


def _mix_fwd(x, g, w_in, lbp, gh, convw_t, w_out, exchange=None):
    t, d = x.shape
    zw = w_in.shape[0]
    w = HGRN_W
    tm = min(TOKEN_TILE, t)
    nc = tm // CHUNK
    n_chunks = t // CHUNK

    def body(x_ref, g_ref, win_ref, lbp_ref, gh_ref, cw_ref, wout_ref,
             xo_ref, z_ref, o_ref, st_ref, y_ref, state, ucarry, qh_buf, kh_buf, kbar_buf, v_buf, etot_buf):
        _zero_at_start(state, ucarry)
        xv = x_ref[...]
        h, _, _ = _rms(xv, g_ref[...])
        z_ref[...] = _dot(h.astype(BF16), win_ref[...], NT)
        z = z_ref[...]
        _, _, f, _, q = _gates(z, lbp_ref[...])
        _decayed_operands(q, f, z[:, 2 * w:3 * w], qh_buf, kh_buf, kbar_buf, v_buf, etot_buf)
        mask = _block_causal_mask(tm)
        chunk_of_row = lax.broadcasted_iota(jnp.int32, (tm, 1), 0) // CHUNK
        heads = range(HGRN_HEADS)
        hcols = [slice(hd * HGRN_DK, (hd + 1) * HGRN_DK) for hd in heads]
        qh = [qh_buf[:, hcols[hd]] for hd in heads]
        vb = [v_buf[:, hcols[hd]] for hd in heads]
        scores = [jnp.where(mask, _dot(qh[hd], kh_buf[:, hcols[hd]], NT), 0.0).astype(BF16) for hd in heads]
        gains = [_dot(_spread(vb[hd], chunk_of_row, nc), kbar_buf[:, hcols[hd]], TN) for hd in heads]
        entering = []
        for hd in heads:
            states, st = [], state[hd]
            for c in range(nc):
                states.append(st)
                st_ref[c, hd] = st
                st = st * etot_buf[c * CHUNK:c * CHUNK + 1, hcols[hd]] + gains[hd][c * HGRN_DK:(c + 1) * HGRN_DK, :]
            state[hd] = st
            entering.append(jnp.concatenate(states, axis=0).astype(BF16))
        from_states = [_dot(qh[hd], entering[hd], NT) for hd in heads]
        o_heads = [_dot(scores[hd], vb[hd]) + _pick(from_states[hd], chunk_of_row, nc) for hd in heads]
        o_ref[...] = jnp.concatenate(o_heads, axis=1)
        ghv = gh_ref[...]
        normed = jnp.concatenate([_rms(o_heads[hd], ghv[:, hcols[hd]])[0] for hd in heads], axis=1)
        zg = z[:, 3 * w:4 * w]
        u = z[:, 5 * w:6 * w] * z[:, 6 * w:7 * w]
        conv = _short_conv(u, ucarry[...], cw_ref[...])
        ucarry[...] = u[tm - 8:tm, :]
        y = jnp.concatenate([normed * (zg * _sigmoid(zg)), z[:, 4 * w:5 * w] * conv], axis=1).astype(BF16)
        y_ref[...] = y
        xo_ref[...] = xv + _dot(y, wout_ref[...])

    return _call(
        body,
        name="mix_fwd",
        grid=(t // tm,),
        in_specs=[
            _rows(tm, d), _full((1, d)), _full((zw, d)), _full((2, w)), _full((1, w)), _full((3, w)),
            _full((2 * w, d)),
        ],
        out_specs=[
            _rows(tm, d), _rows(tm, zw), _rows(tm, w),
            pl.BlockSpec((nc, HGRN_HEADS, HGRN_DK, HGRN_DK), lambda i: (i, 0, 0, 0)),
            _rows(tm, 2 * w),
        ],
        out_shape=[
            jax.ShapeDtypeStruct((t, d), F32),
            jax.ShapeDtypeStruct((t, zw), F32),
            jax.ShapeDtypeStruct((t, w), F32),
            jax.ShapeDtypeStruct((n_chunks, HGRN_HEADS, HGRN_DK, HGRN_DK), F32),
            jax.ShapeDtypeStruct((t, 2 * w), BF16),
        ],
        scratch_shapes=[
            pltpu.VMEM((HGRN_HEADS, HGRN_DK, HGRN_DK), F32), pltpu.VMEM((8, w), F32),
            pltpu.VMEM((tm, w), BF16), pltpu.VMEM((tm, w), BF16), pltpu.VMEM((tm, w), BF16),
            pltpu.VMEM((tm, w), BF16), pltpu.VMEM((tm, w), F32),
        ],
        args=(x, g, w_in, lbp, gh, convw_t, w_out),
        exchange=exchange,
    )


def _mix_bwd(x, g, dxo, z, o, states, w_in, lbp, gh, convw_t, w_out, exchange=None):
    t, d = x.shape
    zw = w_in.shape[0]
    w = HGRN_W
    tm = min(TOKEN_TILE, t)
    nc = tm // CHUNK
    n = t // tm

    def body(x_ref, g_ref, dxo_ref, z_ref, zprev_ref, o_ref, st_ref, win_ref, lbp_ref, gh_ref, cw_ref, wout_ref,
             dx_ref, dz_ref, h_ref, dg_ref, dlbp_ref, dgh_ref, dcw_ref,
             dstate, dcarry, do_buf, qh_buf, kh_buf, kbar_buf, v_buf, etot_buf):
        _zero_at_start(dstate, dcarry, dg_ref, dlbp_ref, dgh_ref, dcw_ref)
        gv = g_ref[...]
        h, xh, r = _rms(x_ref[...], gv)
        h_ref[...] = h.astype(BF16)
        dxo = dxo_ref[...]
        dy = _dot(dxo.astype(BF16), wout_ref[...], NT)
        z = z_ref[...]
        lb, sig, f, sq, q = _gates(z, lbp_ref[...])
        eb, enb, erest = _decayed_operands(q, f, z[:, 2 * w:3 * w], qh_buf, kh_buf, kbar_buf, v_buf, etot_buf)

        ghv = gh_ref[...]
        zg = z[:, 3 * w:4 * w]
        sgz = _sigmoid(zg)
        dyh = dy[:, 0:w]
        don = dyh * (zg * sgz)
        heads = range(HGRN_HEADS)
        hcols = [slice(hd * HGRN_DK, (hd + 1) * HGRN_DK) for hd in heads]
        norms = [_rms(o_ref[:, hcols[hd]], ghv[:, hcols[hd]]) for hd in heads]
        on = jnp.concatenate([norms[hd][0] for hd in heads], axis=1)
        oh = jnp.concatenate([norms[hd][1] for hd in heads], axis=1)
        dz_ref[:, 3 * w:4 * w] = (dyh * on * (sgz * (1.0 + zg * (1.0 - sgz)))).astype(BF16)
        dgh_ref[...] += jnp.sum(don * oh, axis=0, keepdims=True)
        do_buf[...] = jnp.concatenate(
            [_rms_bwd(don[:, hcols[hd]], norms[hd][1], norms[hd][2], ghv[:, hcols[hd]]) for hd in heads],
            axis=1).astype(BF16)

        zb = z[:, 4 * w:5 * w]
        zc = z[:, 5 * w:6 * w]
        zu = z[:, 6 * w:7 * w]
        u = zc * zu
        cw = cw_ref[...]
        zp = zprev_ref[...]
        uprev = jnp.where(pl.program_id(0) == n - 1, 0.0, zp[:, 5 * w:6 * w] * zp[:, 6 * w:7 * w])
        dyc = dy[:, w:2 * w]
        dz_ref[:, 4 * w:5 * w] = (dyc * _short_conv(u, uprev, cw)).astype(BF16)
        dconv = dyc * zb
        edge = dcarry[...]
        dconv1 = _shift_rows(dconv, -1, edge)
        dconv2 = _shift_rows(dconv, -2, edge)
        dcarry[...] = dconv[0:8, :]
        du = cw[2:3, :] * dconv + cw[1:2, :] * dconv1 + cw[0:1, :] * dconv2
        dz_ref[:, 5 * w:6 * w] = (du * zu).astype(BF16)
        dz_ref[:, 6 * w:7 * w] = (du * zc).astype(BF16)
        dcw_ref[...] += jnp.concatenate([
            jnp.sum(u * dconv2, axis=0, keepdims=True),
            jnp.sum(u * dconv1, axis=0, keepdims=True),
            jnp.sum(u * dconv, axis=0, keepdims=True)], axis=0)

        mask = _block_causal_mask(tm)
        chunk_of_row = lax.broadcasted_iota(jnp.int32, (tm, 1), 0) // CHUNK
        heads = range(HGRN_HEADS)
        hcols = [slice(hd * HGRN_DK, (hd + 1) * HGRN_DK) for hd in heads]
        qhb = [qh_buf[:, hcols[hd]] for hd in heads]
        khb = [kh_buf[:, hcols[hd]] for hd in heads]
        vb = [v_buf[:, hcols[hd]] for hd in heads]
        dob = [do_buf[:, hcols[hd]] for hd in heads]
        scores = [jnp.where(mask, _dot(qhb[hd], khb[hd], NT), 0.0).astype(BF16) for hd in heads]
        dscores = [jnp.where(mask, _dot(dob[hd], vb[hd], NT), 0.0).astype(BF16) for hd in heads]
        gains = [_dot(_spread(dob[hd], chunk_of_row, nc), qhb[hd], TN) for hd in heads]
        dst_rows, dst_lanes, st_lanes, carries = [], [], [], []
        for hd in heads:
            entering = [st_ref[c, hd] for c in range(nc)]
            leaving, carried_back = [None] * nc, [None] * nc
            dst = dstate[hd]
            for c in reversed(range(nc)):
                elast = etot_buf[c * CHUNK:c * CHUNK + 1, hcols[hd]]
                leaving[c] = dst
                carried_back[c] = jnp.sum(dst * entering[c], axis=0, keepdims=True) * elast
                dst = dst * elast + gains[hd][c * HGRN_DK:(c + 1) * HGRN_DK, :]
            dstate[hd] = dst
            dst_rows.append(jnp.concatenate(leaving, axis=0).astype(BF16))
            dst_lanes.append(jnp.concatenate(leaving, axis=1).astype(BF16))
            st_lanes.append(jnp.concatenate(entering, axis=1).astype(BF16))
            carries.append(carried_back)
        dv = [_dot(scores[hd], dob[hd], TN) + _pick(_dot(kbar_buf[:, hcols[hd]], dst_rows[hd], NT), chunk_of_row, nc)
              for hd in heads]
        dz_ref[:, 2 * w:3 * w] = jnp.concatenate(dv, axis=1).astype(BF16)
        dqh = jnp.concatenate(
            [_dot(dscores[hd], khb[hd]) + _pick(_dot(dob[hd], st_lanes[hd]), chunk_of_row, nc) for hd in heads], axis=1)
        dkh = jnp.concatenate([_dot(dscores[hd], qhb[hd], TN) for hd in heads], axis=1)
        dkbar = jnp.concatenate([_pick(_dot(vb[hd], dst_lanes[hd]), chunk_of_row, nc) for hd in heads], axis=1)

        kbar_dkbar = kbar_buf[...].astype(F32) * dkbar
        db = qh_buf[...].astype(F32) * dqh - kh_buf[...].astype(F32) * dkh - kbar_dkbar
        through_last = jnp.concatenate([
            jnp.broadcast_to(
                jnp.sum(kbar_dkbar[c * CHUNK:(c + 1) * CHUNK], axis=0, keepdims=True)
                + jnp.concatenate([carries[hd][c] for hd in heads], axis=1),
                (CHUNK, w))
            for c in range(nc)], axis=0)
        dlogf = _chunk_cumsum(db, reverse=True) + through_last
        df = dlogf / f - (dkh * enb + dkbar * erest)
        zq = z[:, 0:w]
        dz_ref[:, 0:w] = (dqh * eb * HGRN_DK ** -0.5 * (sq * (1.0 + zq * (1.0 - sq)))).astype(BF16)
        dz_ref[:, w:2 * w] = (df * (1.0 - lb) * sig * (1.0 - sig)).astype(BF16)
        dlb = jnp.sum(df * (1.0 - sig), axis=0, keepdims=True) * lb * (1.0 - lb)
        dlbp_ref[...] += jnp.concatenate([dlb, -dlb], axis=0)

        dh = _dot(dz_ref[...], win_ref[...])
        dx_ref[...] = _rms_bwd(dh, xh, r, gv) + dxo
        dg_ref[...] += jnp.sum(dh * xh, axis=0, keepdims=True)

    return _call(
        body,
        name="mix_bwd",
        grid=(n,),
        in_specs=[
            _rows_rev(tm, d, n), _full((1, d)), _rows_rev(tm, d, n), _rows_rev(tm, zw, n),
            pl.BlockSpec((8, zw), lambda i: (jnp.maximum((n - 1 - i) * (tm // 8) - 1, 0), 0)),
            _rows_rev(tm, w, n),
            pl.BlockSpec((nc, HGRN_HEADS, HGRN_DK, HGRN_DK), lambda i: (n - 1 - i, 0, 0, 0)),
            _full((zw, d)), _full((2, w)), _full((1, w)), _full((3, w)), _full((2 * w, d)),
        ],
        out_specs=[
            _rows_rev(tm, d, n), _rows_rev(tm, zw, n), _rows_rev(tm, d, n),
            _full((1, d)), _full((2, w)), _full((1, w)), _full((3, w)),
        ],
        out_shape=[
            jax.ShapeDtypeStruct((t, d), F32),
            jax.ShapeDtypeStruct((t, zw), BF16),
            jax.ShapeDtypeStruct((t, d), BF16),
            jax.ShapeDtypeStruct((1, d), F32),
            jax.ShapeDtypeStruct((2, w), F32),
            jax.ShapeDtypeStruct((1, w), F32),
            jax.ShapeDtypeStruct((3, w), F32),
        ],
        scratch_shapes=[
            pltpu.VMEM((HGRN_HEADS, HGRN_DK, HGRN_DK), F32), pltpu.VMEM((8, w), F32),
            pltpu.VMEM((tm, w), BF16),
            pltpu.VMEM((tm, w), BF16), pltpu.VMEM((tm, w), BF16), pltpu.VMEM((tm, w), BF16),
            pltpu.VMEM((tm, w), BF16), pltpu.VMEM((tm, w), F32),
        ],
        args=(x, g, dxo, z, z, o, states, w_in, lbp, gh, convw_t, w_out),
        exchange=exchange,
    )


def _memkv_fwd(mem, g, wkv):
    m, d = mem.shape
    nb, _, cb = wkv.shape

    def body(mem_ref, g_ref, wkv_ref, kv_ref):
        mn, _, _ = _rms(mem_ref[...], g_ref[...])
        mnb = mn.astype(BF16)
        for j in range(nb):
            kv_ref[:, j * cb:(j + 1) * cb] = _dot(mnb, wkv_ref[j]).astype(BF16)

    return pl.pallas_call(
        body,
        name="memkv_fwd",
        out_shape=jax.ShapeDtypeStruct((m, nb * cb), BF16),
        compiler_params=_params(),
    )(mem, g, wkv)


def _memkv_bwd(mem, g, dkv, wkv):
    m, d = mem.shape
    nb, _, cb = wkv.shape
    chips = nb // 2

    def body(mem_ref, g_ref, dkv_ref, wkv_ref, dw_ref, dg_ref, dw_all, send_buf, recv_buf, send_sem, recv_sem):
        x, y, c, _ = _mesh_place()
        sibling, _ = _peer(x, y, c, 1)
        mn, xh, _ = _rms(mem_ref[...], g_ref[...])
        mnb = mn.astype(BF16)
        dmn = jnp.zeros((m, d), F32)
        for j in range(nb):
            dkvb = dkv_ref[:, j * cb:(j + 1) * cb].astype(BF16)
            dw_all[j] = _dot(mnb, dkvb, TN)
            dmn = dmn + _dot(dkvb, wkv_ref[j], NT)
        dg_ref[...] = jnp.sum(dmn * xh, axis=0, keepdims=True)
        for q in range(chips):
            send_buf[q] = dw_all[2 * q + 1 - c].astype(BF16)
        to_sibling = _remote(send_buf, recv_buf, send_sem, recv_sem, sibling)
        to_sibling.start()
        to_sibling.wait_send()
        to_sibling.wait_recv()
        for q in range(chips):
            dw_ref[q] = (dw_all[2 * q + c] + recv_buf[q].astype(F32)).astype(BF16)

    return pl.pallas_call(
        body,
        name="memkv_bwd",
        out_shape=[jax.ShapeDtypeStruct((chips, d, cb), BF16), jax.ShapeDtypeStruct((1, d), F32)],
        scratch_shapes=[
            pltpu.VMEM((nb, d, cb), F32), pltpu.VMEM((chips, d, cb), BF16), pltpu.VMEM((chips, d, cb), BF16),
            pltpu.SemaphoreType.DMA, pltpu.SemaphoreType.DMA,
        ],
        compiler_params=_params(),
    )(mem, g, dkv, wkv)


def _softmax_rows(qm_h, k_h):
    sc = _dot(qm_h, k_h, NT) * MEM_HD ** -0.5
    e = jnp.exp(sc - jnp.max(sc, axis=-1, keepdims=True))
    return e / jnp.sum(e, axis=-1, keepdims=True)


def _xattn_fwd(x, g, wq, kv, wo, exchange=None):
    t, d = x.shape
    m = kv.shape[0]
    tm = min(XATTN_TILE, t)

    def body(x_ref, g_ref, wq_ref, kv_ref, wo_ref, xo_ref, hq_ref, qm_ref, att_ref):
        xv = x_ref[...]
        h, _, _ = _rms(xv, g_ref[...])
        hb = h.astype(BF16)
        hq_ref[...] = hb
        qm = _dot(hb, wq_ref[...]).astype(BF16)
        qm_ref[...] = qm
        heads = range(MEM_HEADS)
        kcols = [slice(hd * MEM_HD, (hd + 1) * MEM_HD) for hd in heads]
        p = [_softmax_rows(qm[:, kcols[hd]], kv_ref[:, kcols[hd]]) for hd in heads]
        att = jnp.concatenate(
            [_dot(p[hd].astype(BF16), kv_ref[:, d + hd * MEM_HD:d + (hd + 1) * MEM_HD]) for hd in heads],
            axis=1).astype(BF16)
        att_ref[...] = att
        xo_ref[...] = xv + _dot(att, wo_ref[...])

    return _call(
        body,
        name="xattn_fwd",
        grid=(t // tm,),
        in_specs=[_rows(tm, d), _full((1, d)), _full((d, d)), _full((m, 2 * d)), _full((d, d))],
        out_specs=[_rows(tm, d), _rows(tm, d), _rows(tm, d), _rows(tm, d)],
        out_shape=[
            jax.ShapeDtypeStruct((t, d), F32),
            jax.ShapeDtypeStruct((t, d), BF16),
            jax.ShapeDtypeStruct((t, d), BF16),
            jax.ShapeDtypeStruct((t, d), BF16),
        ],
        args=(x, g, wq, kv, wo),
        exchange=exchange,
    )


def _xattn_bwd(x, g, dxo, qm, kv, wq, wo, exchange=None):
    t, d = x.shape
    m = kv.shape[0]
    tm = min(XATTN_TILE, t)

    def body(x_ref, g_ref, dxo_ref, qm_ref, kv_ref, wq_ref, wo_ref, dx_ref, dqm_ref, dkv_ref, dg_ref):
        _zero_at_start(dkv_ref, dg_ref)
        gv = g_ref[...]
        _, xh, r = _rms(x_ref[...], gv)
        dxo = dxo_ref[...]
        datt = _dot(dxo.astype(BF16), wo_ref[...], NT).astype(BF16)
        heads = range(MEM_HEADS)
        kcols = [slice(hd * MEM_HD, (hd + 1) * MEM_HD) for hd in heads]
        vcols = [slice(d + hd * MEM_HD, d + (hd + 1) * MEM_HD) for hd in heads]
        qm_h = [qm_ref[:, kcols[hd]] for hd in heads]
        p = [_softmax_rows(qm_h[hd], kv_ref[:, kcols[hd]]) for hd in heads]
        dp = [_dot(datt[:, kcols[hd]], kv_ref[:, vcols[hd]], NT) for hd in heads]
        dsc = [(p[hd] * (dp[hd] - jnp.sum(p[hd] * dp[hd], axis=-1, keepdims=True)) * MEM_HD ** -0.5).astype(BF16)
               for hd in heads]
        dqm = jnp.concatenate([_dot(dsc[hd], kv_ref[:, kcols[hd]]) for hd in heads], axis=1).astype(BF16)
        dqm_ref[...] = dqm
        dkv_ref[...] += jnp.concatenate(
            [_dot(dsc[hd], qm_h[hd], TN) for hd in heads]
            + [_dot(p[hd].astype(BF16), datt[:, kcols[hd]], TN) for hd in heads], axis=1)
        dh = _dot(dqm, wq_ref[...], NT)
        dx_ref[...] = _rms_bwd(dh, xh, r, gv) + dxo
        dg_ref[...] += jnp.sum(dh * xh, axis=0, keepdims=True)

    return _call(
        body,
        name="xattn_bwd",
        grid=(t // tm,),
        in_specs=[
            _rows(tm, d), _full((1, d)), _rows(tm, d), _rows(tm, d), _full((m, 2 * d)), _full((d, d)), _full((d, d)),
        ],
        out_specs=[_rows(tm, d), _rows(tm, d), _full((m, 2 * d)), _full((1, d))],
        out_shape=[
            jax.ShapeDtypeStruct((t, d), F32),
            jax.ShapeDtypeStruct((t, d), BF16),
            jax.ShapeDtypeStruct((m, 2 * d), F32),
            jax.ShapeDtypeStruct((1, d), F32),
        ],
        args=(x, g, dxo, qm, kv, wq, wo),
        exchange=exchange,
    )


def _mesh_place():
    x, y, c = lax.axis_index("x"), lax.axis_index("y"), lax.axis_index("c")
    return x, y, c, 4 * x + 2 * y + c


def _peer(x, y, c, k):
    px = 1 - x if k & 4 else x
    py = 1 - y if k & 2 else y
    pc = 1 - c if k & 1 else c
    return (px, py, pc), 4 * px + 2 * py + pc


ICI_HOPS = (2, 4, 6)
N_HOPS = len(ICI_HOPS)


def _remote(src, dst, send_sem, recv_sem, peer):
    return pltpu.make_async_remote_copy(
        src_ref=src, dst_ref=dst, send_sem=send_sem, recv_sem=recv_sem, device_id=peer, device_id_type=MESH_IDS)


def _gather_exchange(shards, middle_eighths=MIDDLE_EIGHTHS):
    n = len(shards)

    def place():
        x, y, c, me = _mesh_place()
        sibling, _ = _peer(x, y, c, 1)
        to_x, from_x = _peer(x, y, c, 4)
        to_y, from_y = _peer(x, y, c, 2)
        _, from_diagonal = _peer(x, y, c, 6)
        onward = (c * to_y[0] + (1 - c) * to_x[0], c * to_y[1] + (1 - c) * to_x[1], c)
        passed_on = c * from_x + (1 - c) * from_y
        return me, sibling, (to_x, to_y, onward), (from_x, from_y, from_diagonal), passed_on

    def start(src, dst, sems):
        ici_send, ici_recv, pair_send, pair_recv, local = sems
        me, sibling, targets, _, _ = place()
        for a in range(n):
            pltpu.make_async_copy(src[a], dst[a].at[me], local.at[a]).start()
            for j in range(2):
                _remote(src[a], dst[a].at[me], ici_send.at[a, j], ici_recv.at[a, j], targets[j]).start()
            _remote(src[a], dst[a].at[me], pair_send.at[a, 0], pair_recv.at[a, 0], sibling).start()

    def to_sibling(dst, sems, a, j, origin, sibling):
        _, _, pair_send, pair_recv, _ = sems
        slot = dst[a].at[origin]
        return _remote(slot, slot, pair_send.at[a, 1 + j], pair_recv.at[a, 1 + j], sibling)

    def middle(src, dst, sems):
        ici_send, ici_recv, _, _, _ = sems
        _, sibling, targets, origins, passed_on = place()
        for a in range(n):
            for j in range(2):
                _remote(src[a], dst[a].at[origins[j]], ici_send.at[a, j], ici_recv.at[a, j], targets[j]).wait_recv()
            slot = dst[a].at[passed_on]
            _remote(slot, slot, ici_send.at[a, 2], ici_recv.at[a, 2], targets[2]).start()
            for j in range(2):
                to_sibling(dst, sems, a, j, origins[j], sibling).start()

    def finish(src, dst, sems):
        ici_send, ici_recv, pair_send, pair_recv, local = sems
        me, sibling, targets, origins, _ = place()
        for a in range(n):
            _remote(src[a], dst[a].at[origins[2]], ici_send.at[a, 2], ici_recv.at[a, 2], targets[2]).wait_recv()
            to_sibling(dst, sems, a, 2, origins[2], sibling).start()
        for a in range(n):
            pltpu.make_async_copy(src[a], dst[a].at[me], local.at[a]).wait()
            for j in range(N_HOPS):
                _remote(src[a], dst[a].at[me], ici_send.at[a, j], ici_recv.at[a, j], targets[j]).wait_send()
            for j, origin in enumerate((me,) + origins):
                from_sibling = origin + 1 - 2 * (origin % 2)
                passed = _remote(src[a], dst[a].at[from_sibling], pair_send.at[a, j], pair_recv.at[a, j], sibling)
                passed.wait_send()
                passed.wait_recv()

    return _Exchange(
        shards,
        [jax.ShapeDtypeStruct((N_DEV,) + s.shape, s.dtype) for s in shards],
        [
            pltpu.SemaphoreType.DMA((n, N_HOPS)), pltpu.SemaphoreType.DMA((n, N_HOPS)),
            pltpu.SemaphoreType.DMA((n, N_HOPS + 1)), pltpu.SemaphoreType.DMA((n, N_HOPS + 1)),
            pltpu.SemaphoreType.DMA((n,)),
        ],
        start, finish, middle, middle_eighths)


def _scatter_copies(src, dst, sems, n, arrivals=False):
    send, recv, local = sems
    x, y, c, _ = _mesh_place()
    chip = 2 * x + y
    if arrivals is None:
        return [pltpu.make_async_copy(src[a].at[chip], dst[a].at[chip], local.at[a]) for a in range(n)]
    copies = []
    for a in range(n):
        for j, k in enumerate(ICI_HOPS):
            peer, _ = _peer(x, y, c, k)
            peer_chip = 2 * peer[0] + peer[1]
            slot = dst[a].at[peer_chip if arrivals else chip]
            copies.append(_remote(src[a].at[peer_chip], slot, send.at[a, j], recv.at[a, j], peer))
    return copies


def _scatter_start(src, dst, sems, n):
    for cp in _scatter_copies(src, dst, sems, n, arrivals=None) + _scatter_copies(src, dst, sems, n):
        cp.start()


def _scatter_finish(src, dst, sems, n):
    for cp in _scatter_copies(src, dst, sems, n, arrivals=None):
        cp.wait()
    for cp in _scatter_copies(src, dst, sems, n):
        cp.wait_send()
    for cp in _scatter_copies(src, dst, sems, n, arrivals=True):
        cp.wait_recv()


def _scatter_scratch(n):
    return [pltpu.SemaphoreType.DMA((n, N_HOPS)), pltpu.SemaphoreType.DMA((n, N_HOPS)), pltpu.SemaphoreType.DMA((n,))]


def _scatter_exchange(partials):
    n = len(partials)
    return _Exchange(
        partials, [jax.ShapeDtypeStruct(p.shape, p.dtype) for p in partials], _scatter_scratch(n),
        lambda src, dst, sems: _scatter_start(src, dst, sems, n),
        lambda src, dst, sems: _scatter_finish(src, dst, sems, n))


SMALL_LAYOUT = {
    "ffn1_norm": (0, 1, 1024), "mix_norm": (1, 1, 1024), "xattn_norm": (2, 1, 1024), "mem_norm": (3, 1, 1024),
    "ffn2_norm": (4, 1, 1024), "final_norm": (5, 1, 1024), "lb_param": (6, 2, 512), "hgrn_out_norm": (8, 1, 512),
    "conv_w": (9, 3, 512), "loss": (12, 1, 128),
}


def _final_exchange(partials, small):
    n = len(partials)
    names = list(small)
    width = 1024

    def body(*refs):
        src = refs[:n]
        pieces = refs[n:n + len(names)]
        dst = refs[n + len(names):2 * n + len(names)]
        total_ref = refs[2 * n + len(names)]
        pack, gathered, small_send, small_recv = refs[2 * n + len(names) + 1:2 * n + len(names) + 5]
        sems = refs[2 * n + len(names) + 5:]
        x, y, c, me = _mesh_place()
        pack[...] = jnp.zeros_like(pack)
        for name, piece in zip(names, pieces):
            row, nrows, ncols = SMALL_LAYOUT[name]
            pack[row:row + nrows, 0:ncols] = piece[...]
        for k in range(1, N_DEV):
            peer, _ = _peer(x, y, c, k)
            _remote(pack, gathered.at[me], small_send.at[k - 1], small_recv.at[k - 1], peer).start()
        _scatter_start(src, dst, sems, n)
        gathered[me] = pack[...]
        for k in range(1, N_DEV):
            peer, peer_index = _peer(x, y, c, k)
            landed = _remote(pack, gathered.at[peer_index], small_send.at[k - 1], small_recv.at[k - 1], peer)
            landed.wait_send()
            landed.wait_recv()
        total = gathered[0]
        for j in range(1, N_DEV):
            total = total + gathered[j]
        total_ref[...] = total
        _scatter_finish(src, dst, sems, n)

    hbm = pl.BlockSpec(memory_space=pltpu.HBM)
    vmem = pl.BlockSpec(memory_space=pltpu.VMEM)
    out = pl.pallas_call(
        body,
        name="final_exchange",
        in_specs=[hbm] * n + [vmem] * len(names),
        out_specs=[hbm] * n + [vmem],
        out_shape=[jax.ShapeDtypeStruct(p.shape, p.dtype) for p in partials]
        + [jax.ShapeDtypeStruct((SMALL_ROWS, width), F32)],
        scratch_shapes=[
            pltpu.VMEM((SMALL_ROWS, width), F32), pltpu.VMEM((N_DEV, SMALL_ROWS, width), F32),
            pltpu.SemaphoreType.DMA((N_DEV - 1,)), pltpu.SemaphoreType.DMA((N_DEV - 1,)),
        ] + _scatter_scratch(n),
        compiler_params=pltpu.CompilerParams(has_side_effects=True),
    )(*partials, *[small[k] for k in names])
    return out[:n], out[n]


def _adamw_math(w, g, m, v):
    m = ADAM_B1 * m + (1.0 - ADAM_B1) * g
    v = ADAM_B2 * v + (1.0 - ADAM_B2) * (g * g)
    m_hat = m / (1.0 - ADAM_B1 ** ADAM_STEP)
    v_hat = v / (1.0 - ADAM_B2 ** ADAM_STEP)
    delta = -ADAM_LR * (m_hat / (jnp.sqrt(v_hat) + ADAM_EPS) + ADAM_WD * w)
    return delta, m, v


def _adamw_shard(parts, w, m, v):
    r, c = w.shape
    n_parts = parts.shape[0]
    tr = max(rows for rows in range(16, r + 1, 16) if r % rows == 0 and rows * c <= ADAMW_TILE_ELEMENTS)

    def body(p_ref, w_ref, m_ref, v_ref, g_ref, d_ref, mo_ref, vo_ref):
        g = p_ref[0].astype(F32)
        for j in range(1, n_parts):
            g = g + p_ref[j].astype(F32)
        delta, mn, vn = _adamw_math(w_ref[...], g, m_ref[...], v_ref[...])
        g_ref[...] = g
        d_ref[...] = delta
        mo_ref[...] = mn
        vo_ref[...] = vn

    tile = pl.BlockSpec((tr, c), lambda i: (i, 0))
    return pl.pallas_call(
        body,
        name="adamw_shard",
        grid=(r // tr,),
        in_specs=[pl.BlockSpec((n_parts, tr, c), lambda i: (0, i, 0)), tile, tile, tile],
        out_specs=[tile] * 4,
        out_shape=[jax.ShapeDtypeStruct((r, c), F32)] * 4,
        compiler_params=_params(("parallel",)),
    )(parts, w, m, v)


def _adamw_small(gs, ws, ms, vs):
    n = len(gs)

    def body(*refs):
        g_refs, w_refs, m_refs, v_refs = refs[:n], refs[n:2 * n], refs[2 * n:3 * n], refs[3 * n:4 * n]
        d_out, m_out, v_out = refs[4 * n:5 * n], refs[5 * n:6 * n], refs[6 * n:7 * n]
        for i in range(n):
            delta, mn, vn = _adamw_math(w_refs[i][...], g_refs[i][...], m_refs[i][...], v_refs[i][...])
            d_out[i][...] = delta
            m_out[i][...] = mn
            v_out[i][...] = vn

    shapes = [jax.ShapeDtypeStruct(w.shape, F32) for w in ws]
    out = pl.pallas_call(
        body,
        name="adamw_small",
        out_shape=shapes * 3,
        compiler_params=_params(),
    )(*gs, *ws, *ms, *vs)
    return out[:n], out[n:2 * n], out[2 * n:]


TRANSPOSED = ("ffn1_gate", "ffn1_up", "w_in", "ffn2_gate", "ffn2_up", "conv_w")
GROUP_FFN1 = ("ffn1_gate", "ffn1_up", "ffn1_down")
GROUP_MIX = ("w_in", "w_out")
GROUP_XATTN = ("w_q_mem", "w_kv_mem", "w_o_mem")
GROUP_FFN2 = ("ffn2_gate", "ffn2_up", "ffn2_down")
LARGE = GROUP_FFN1 + GROUP_MIX + GROUP_XATTN + GROUP_FFN2
SMALL = ("ffn1_norm", "mix_norm", "lb_param", "hgrn_out_norm", "conv_w", "xattn_norm", "mem_norm", "ffn2_norm",
         "final_norm")
WEIGHTS = ("ffn1_norm", "ffn1_gate", "ffn1_up", "ffn1_down", "mix_norm", "w_in", "lb_param", "hgrn_out_norm",
           "conv_w", "w_out", "xattn_norm", "mem_norm", "w_q_mem", "w_kv_mem", "w_o_mem", "ffn2_norm", "ffn2_gate",
           "ffn2_up", "ffn2_down", "final_norm")


def kernel(x, mem, ffn1_norm, ffn1_gate, ffn1_up, ffn1_down, mix_norm, w_in, lb_param, hgrn_out_norm, conv_w, w_out, xattn_norm, mem_norm, w_q_mem, w_kv_mem, w_o_mem, ffn2_norm, ffn2_gate, ffn2_up, ffn2_down, final_norm, loss_target, m_ffn1_norm, m_ffn1_gate, m_ffn1_up, m_ffn1_down, m_mix_norm, m_w_in, m_lb_param, m_hgrn_out_norm, m_conv_w, m_w_out, m_xattn_norm, m_mem_norm, m_w_q_mem, m_w_kv_mem, m_w_o_mem, m_ffn2_norm, m_ffn2_gate, m_ffn2_up, m_ffn2_down, m_final_norm, v_ffn1_norm, v_ffn1_gate, v_ffn1_up, v_ffn1_down, v_mix_norm, v_w_in, v_lb_param, v_hgrn_out_norm, v_conv_w, v_w_out, v_xattn_norm, v_mem_norm, v_w_q_mem, v_w_kv_mem, v_w_o_mem, v_ffn2_norm, v_ffn2_gate, v_ffn2_up, v_ffn2_down, v_final_norm):
    given = dict(locals())
    me = 4 * lax.axis_index("x") + 2 * lax.axis_index("y") + lax.axis_index("c")
    x0, memv, target = x[0], mem[0], loss_target[0]

    def shard(prefix, name):
        v = given[prefix + name]
        if v.ndim == 1:
            return v.reshape(1, -1)
        if v.ndim == 2:
            return v
        return v[0].T if name in TRANSPOSED else v[0]

    w = {name: shard("", name) for name in WEIGHTS}
    m = {name: shard("m_", name) for name in WEIGHTS}
    v = {name: shard("v_", name) for name in WEIGHTS}

    conv_taps, conv_rows = w["conv_w"].shape
    conv_tile = jnp.pad(w["conv_w"], ((0, 8 - conv_taps), (0, 128 - conv_rows)))
    wire = {name: w[name].astype(BF16) for name in LARGE}
    full = {}

    def landed(names, gathered):
        for name, blocks in zip(names, gathered):
            _, r, c = blocks.shape
            full[name] = blocks if name == "w_kv_mem" else blocks.reshape(N_DEV * r, c)

    first = ("ffn1_gate", "ffn1_up")
    landed(first, _run_exchange(_gather_exchange([wire[k] for k in first]), "gather_first"))

    riders = (("ffn1_down", "w_in"), ("w_out", "w_kv_mem"), ("w_q_mem", "w_o_mem", "ffn2_gate", "ffn2_up"),
              ("ffn2_down",))
    (a1, b1, s1), gathered = _ffn_up(
        x0, w["ffn1_norm"], full["ffn1_gate"], full["ffn1_up"],
        exchange=_gather_exchange([wire[k] for k in riders[0]]))
    landed(riders[0], gathered)
    (x1,), gathered = _ffn_down(
        x0, s1, full["ffn1_down"], exchange=_gather_exchange([wire[k] for k in riders[1]] + [conv_tile]))
    landed(riders[1], gathered)
    convw_t = gathered[-1][:, :conv_taps, :conv_rows].transpose(1, 0, 2).reshape(conv_taps, N_DEV * conv_rows)
    (x2, z, o_raw, states, ycat), gathered = _mix_fwd(
        x1, w["mix_norm"], full["w_in"], w["lb_param"], w["hgrn_out_norm"], convw_t, full["w_out"],
        exchange=_gather_exchange([wire[k] for k in riders[2]]))
    landed(riders[2], gathered)
    kv = _memkv_fwd(memv, w["mem_norm"], full["w_kv_mem"])
    (x3, hq, qm, att), gathered = _xattn_fwd(
        x2, w["xattn_norm"], full["w_q_mem"], kv, full["w_o_mem"],
        exchange=_gather_exchange([wire[k] for k in riders[3]], middle_eighths=EARLY_MIDDLE_EIGHTHS))
    landed(riders[3], gathered)
    (dx4, a2, b2, s2, loss_part, d_final), _ = _ffn_fwd(
        x3, w["ffn2_norm"], full["ffn2_gate"], full["ffn2_up"], full["ffn2_down"], head=(w["final_norm"], target))

    parts = {}
    waiting = []

    def carried():
        names = [name for name, _ in waiting]
        exchange = _scatter_exchange([p for _, p in waiting]) if waiting else None
        del waiting[:]
        return names, exchange

    def weight_grad(name, a, b, scale=1.0):
        names, exchange = carried()
        partial, arrived = _weight_grad(a, b, scale, exchange=exchange)
        parts.update(zip(names, arrived))
        waiting.append((name, partial))

    (dx3, da2, db2, h4, d_ffn2_norm), _ = _ffn_bwd(
        x3, w["ffn2_norm"], dx4, a2, b2, full["ffn2_gate"], full["ffn2_up"], full["ffn2_down"])
    weight_grad("ffn2_down", s2, dx4, 0.5)
    weight_grad("ffn2_gate", da2, h4)
    weight_grad("ffn2_up", db2, h4)
    names, exchange = carried()
    (dx2, dqm, dkv, d_xattn_norm), arrived = _xattn_bwd(
        x2, w["xattn_norm"], dx3, qm, kv, full["w_q_mem"], full["w_o_mem"], exchange=exchange)
    parts.update(zip(names, arrived))
    d_wkv, d_mem_norm = _memkv_bwd(memv, w["mem_norm"], dkv, full["w_kv_mem"])
    waiting.append(("w_kv_mem", d_wkv))
    names, exchange = carried()
    (dx1, dz, h2, d_mix_norm, d_lbp, d_gh, d_convw_t), arrived = _mix_bwd(
        x1, w["mix_norm"], dx2, z, o_raw, states, full["w_in"], w["lb_param"], w["hgrn_out_norm"], convw_t,
        full["w_out"], exchange=exchange)
    parts.update(zip(names, arrived))
    weight_grad("w_in", dz, h2)
    weight_grad("ffn1_down", s1, dx1, 0.5)
    (dx0, da1, db1, h1, d_ffn1_norm), _ = _ffn_bwd(
        x0, w["ffn1_norm"], dx1, a1, b1, full["ffn1_gate"], full["ffn1_up"], full["ffn1_down"])
    weight_grad("ffn1_gate", da1, h1)
    weight_grad("ffn1_up", db1, h1)
    weight_grad("w_o_mem", att, dx3)
    weight_grad("w_q_mem", hq, dqm)
    weight_grad("w_out", ycat, dx2)

    small_parts = {
        "ffn1_norm": d_ffn1_norm, "mix_norm": d_mix_norm, "xattn_norm": d_xattn_norm, "mem_norm": d_mem_norm,
        "ffn2_norm": d_ffn2_norm, "final_norm": d_final, "lb_param": d_lbp, "hgrn_out_norm": d_gh,
        "conv_w": d_convw_t, "loss": loss_part,
    }
    names = [name for name, _ in waiting]
    arrived, total = _final_exchange([p for _, p in waiting], small_parts)
    parts.update(zip(names, arrived))

    g_out, d_out, m_out, v_out = {}, {}, {}, {}
    for name in LARGE:
        g_out[name], d_out[name], m_out[name], v_out[name] = _adamw_shard(parts[name], w[name], m[name], v[name])
    g_small = {}
    for name in SMALL:
        row, nrows, ncols = SMALL_LAYOUT[name]
        g_small[name] = total[row:row + nrows, 0:ncols]
    g_small["conv_w"] = lax.dynamic_slice_in_dim(g_small["conv_w"], me * conv_rows, conv_rows, axis=1)
    ds, ms, vs = _adamw_small(
        [g_small[k] for k in SMALL], [w[k] for k in SMALL], [m[k] for k in SMALL], [v[k] for k in SMALL])
    for i, name in enumerate(SMALL):
        g_out[name], d_out[name], m_out[name], v_out[name] = g_small[name], ds[i], ms[i], vs[i]

    def shaped(value, name):
        return (value.T if name in TRANSPOSED else value).reshape(given[name].shape)

    loss = total[SMALL_LAYOUT["loss"][0], 0]
    outs = [loss, dx0.reshape(x.shape)]
    for group in (g_out, d_out, m_out, v_out):
        outs += [shaped(group[name], name) for name in WEIGHTS]
    return tuple(outs)
```

```python
import jax
import jax.numpy as jnp
from jax import lax
from jax.experimental import pallas as pl
from jax.experimental.pallas import tpu as pltpu

F32 = jnp.float32
BF16 = jnp.bfloat16
MESH_IDS = pl.DeviceIdType.MESH

N_DEV = 8
EPS = 1e-6
HGRN_HEADS = 4
HGRN_DK = 128
HGRN_W = 512
CHUNK = 64
MEM_HEADS = 4
MEM_HD = 256
ADAM_LR = 0.001
ADAM_B1 = 0.9
ADAM_B2 = 0.999
ADAM_EPS = 1e-08
ADAM_WD = 0.01
ADAM_STEP = 10

TOKEN_TILE = 256
XATTN_TILE = 512
WIDE_TILE = 512
REDUCE_TILE = 1024
ADAMW_TILE_ELEMENTS = 256 * 1024
MIDDLE_EIGHTHS = 5
EARLY_MIDDLE_EIGHTHS = 4
MXU_ROWS = 256
VMEM_LIMIT = 60 * 1024 * 1024
SMALL_ROWS = 16
NT = (((1,), (1,)), ((), ()))
TN = (((0,), (0,)), ((), ()))


def _params(sem=None):
    return pltpu.CompilerParams(dimension_semantics=sem, vmem_limit_bytes=VMEM_LIMIT)


def _dot(a, b, dims=None):
    if dims is None:
        return jnp.dot(a, b, preferred_element_type=F32)
    return lax.dot_general(a, b, dims, preferred_element_type=F32)


def _sigmoid(v):
    return 1.0 / (1.0 + jnp.exp(-v))


def _rms(x, g):
    r = lax.rsqrt(jnp.mean(x * x, axis=-1, keepdims=True) + EPS)
    xh = x * r
    return xh * g, xh, r


def _rms_bwd(dh, xh, r, g):
    dxh = dh * g
    return r * (dxh - xh * jnp.mean(dxh * xh, axis=-1, keepdims=True))


def _full(shape):
    return pl.BlockSpec(shape, lambda *_: (0,) * len(shape))


def _full_once(shape):
    return pl.BlockSpec(shape, lambda *_: (0,) * len(shape), pipeline_mode=pl.Buffered(1))


def _rows(tm, width):
    return pl.BlockSpec((tm, width), lambda i: (i, 0))


def _rows_rev(tm, width, n):
    return pl.BlockSpec((tm, width), lambda i: (n - 1 - i, 0))


def _zero_at_start(*refs):
    @pl.when(pl.program_id(0) == 0)
    def _():
        for ref in refs:
            ref[...] = jnp.zeros_like(ref)


class _Exchange:
    def __init__(self, operands, out_shapes, scratch, start, finish, middle=None, middle_eighths=MIDDLE_EIGHTHS):
        self.operands, self.out_shapes, self.scratch = list(operands), list(out_shapes), list(scratch)
        self.start, self.middle, self.finish, self.middle_eighths = start, middle, finish, middle_eighths


def _call(body, *, name, grid, in_specs, out_specs, out_shape, args, scratch_shapes=(), exchange=None):
    semantics = ("arbitrary",) * len(grid)
    if exchange is None:
        out = pl.pallas_call(
            body, name=name, grid=grid, in_specs=in_specs, out_specs=out_specs, out_shape=out_shape,
            scratch_shapes=list(scratch_shapes), compiler_params=_params(semantics))(*args)
        return out, []
    hbm = pl.BlockSpec(memory_space=pltpu.HBM)
    n_in, n_out, n_scr = len(in_specs), len(out_specs), len(scratch_shapes)
    e_in, e_out = len(exchange.operands), len(exchange.out_shapes)

    def carried(*refs):
        ins, rest = refs[:n_in], refs[n_in:]
        e_ins, rest = rest[:e_in], rest[e_in:]
        outs, rest = rest[:n_out], rest[n_out:]
        e_outs, rest = rest[:e_out], rest[e_out:]
        scr, e_scr = rest[:n_scr], rest[n_scr:]
        first = last = None
        for axis, size in enumerate(grid):
            at_start, at_end = pl.program_id(axis) == 0, pl.program_id(axis) == size - 1
            first = at_start if first is None else jnp.logical_and(first, at_start)
            last = at_end if last is None else jnp.logical_and(last, at_end)

        @pl.when(first)
        def _():
            exchange.start(e_ins, e_outs, e_scr)

        body(*ins, *outs, *scr)

        if exchange.middle is not None:
            assert len(grid) == 1

            @pl.when(pl.program_id(0) == (grid[0] * exchange.middle_eighths) // 8)
            def _():
                exchange.middle(e_ins, e_outs, e_scr)

        @pl.when(last)
        def _():
            exchange.finish(e_ins, e_outs, e_scr)

    out = pl.pallas_call(
        carried, name=name, grid=grid, in_specs=list(in_specs) + [hbm] * e_in,
        out_specs=list(out_specs) + [hbm] * e_out, out_shape=list(out_shape) + exchange.out_shapes,
        scratch_shapes=list(scratch_shapes) + exchange.scratch,
        compiler_params=pltpu.CompilerParams(
            dimension_semantics=semantics, vmem_limit_bytes=VMEM_LIMIT, has_side_effects=True),
    )(*args, *exchange.operands)
    return out[:n_out], out[n_out:]


def _run_exchange(exchange, name):
    hbm = pl.BlockSpec(memory_space=pltpu.HBM)
    e_in, e_out = len(exchange.operands), len(exchange.out_shapes)

    def body(*refs):
        e_ins, e_outs, e_scr = refs[:e_in], refs[e_in:e_in + e_out], refs[e_in + e_out:]
        exchange.start(e_ins, e_outs, e_scr)
        if exchange.middle is not None:
            exchange.middle(e_ins, e_outs, e_scr)
        exchange.finish(e_ins, e_outs, e_scr)

    return pl.pallas_call(
        body, name=name, in_specs=[hbm] * e_in, out_specs=[hbm] * e_out, out_shape=exchange.out_shapes,
        scratch_shapes=exchange.scratch, compiler_params=pltpu.CompilerParams(has_side_effects=True),
    )(*exchange.operands)


def _loss_head(xo, gf, tgt):
    d = xo.shape[1]
    y, xh, r = _rms(xo, gf)
    err = y - tgt
    dy = err * (1.0 / d)
    loss = 0.5 * jnp.sum(jnp.sum(err * err, axis=-1, keepdims=True) * (1.0 / d), axis=0, keepdims=True)
    return _rms_bwd(dy, xh, r, gf), loss, jnp.sum(dy * xh, axis=0, keepdims=True)


def _ffn_fwd(x, g, wg, wu, wd, exchange=None, head=None):
    t, d = x.shape
    f = wg.shape[0]
    tm = min(WIDE_TILE, t)

    def body(x_ref, g_ref, wg_ref, wu_ref, wd_ref, *rest):
        if head is None:
            xo_ref, a_ref, b_ref, s_ref = rest
        else:
            gf_ref, tgt_ref, xo_ref, a_ref, b_ref, s_ref, loss_ref, dgf_ref = rest
            _zero_at_start(loss_ref, dgf_ref)
        xv = x_ref[...]
        h, _, _ = _rms(xv, g_ref[...])
        hb = h.astype(BF16)
        a = _dot(hb, wg_ref[...], NT)
        b = _dot(hb, wu_ref[...], NT)
        s = (a * _sigmoid(a) * b).astype(BF16)
        xo = xv + 0.5 * _dot(s, wd_ref[...])
        if head is None:
            xo_ref[...] = xo
        else:
            xo_ref[...], loss, dgf = _loss_head(xo, gf_ref[...], tgt_ref[...])
            loss_ref[...] += jnp.broadcast_to(loss, (1, 128))
            dgf_ref[...] += dgf
        a_ref[...] = a.astype(BF16)
        b_ref[...] = b.astype(BF16)
        s_ref[...] = s

    in_specs = [_rows(tm, d), _full((1, d)), _full_once((f, d)), _full_once((f, d)), _full_once((f, d))]
    out_specs = [_rows(tm, d), _rows(tm, f), _rows(tm, f), _rows(tm, f)]
    out_shape = [
        jax.ShapeDtypeStruct((t, d), F32),
        jax.ShapeDtypeStruct((t, f), BF16),
        jax.ShapeDtypeStruct((t, f), BF16),
        jax.ShapeDtypeStruct((t, f), BF16),
    ]
    args = (x, g, wg, wu, wd)
    if head is not None:
        in_specs += [_full((1, d)), _rows(tm, d)]
        out_specs += [_full((1, 128)), _full((1, d))]
        out_shape += [jax.ShapeDtypeStruct((1, 128), F32), jax.ShapeDtypeStruct((1, d), F32)]
        args += tuple(head)
    return _call(
        body, name="ffn_fwd", grid=(t // tm,), in_specs=in_specs, out_specs=out_specs, out_shape=out_shape,
        args=args, exchange=exchange)


def _ffn_up(x, g, wg, wu, exchange=None):
    t, d = x.shape
    f = wg.shape[0]
    tm = min(TOKEN_TILE, t)

    def body(x_ref, g_ref, wg_ref, wu_ref, a_ref, b_ref, s_ref):
        h, _, _ = _rms(x_ref[...], g_ref[...])
        hb = h.astype(BF16)
        a = _dot(hb, wg_ref[...], NT)
        b = _dot(hb, wu_ref[...], NT)
        a_ref[...] = a.astype(BF16)
        b_ref[...] = b.astype(BF16)
        s_ref[...] = (a * _sigmoid(a) * b).astype(BF16)

    return _call(
        body, name="ffn_up", grid=(t // tm,),
        in_specs=[_rows(tm, d), _full((1, d)), _full_once((f, d)), _full_once((f, d))],
        out_specs=[_rows(tm, f)] * 3, out_shape=[jax.ShapeDtypeStruct((t, f), BF16)] * 3,
        args=(x, g, wg, wu), exchange=exchange)


def _ffn_down(x, s, wd, exchange=None):
    t, d = x.shape
    f = wd.shape[0]
    tm = min(TOKEN_TILE, t)

    def body(x_ref, s_ref, wd_ref, xo_ref):
        xo_ref[...] = x_ref[...] + 0.5 * _dot(s_ref[...], wd_ref[...])

    return _call(
        body, name="ffn_down", grid=(t // tm,),
        in_specs=[_rows(tm, d), _rows(tm, f), _full_once((f, d))],
        out_specs=[_rows(tm, d)], out_shape=[jax.ShapeDtypeStruct((t, d), F32)],
        args=(x, s, wd), exchange=exchange)


def _ffn_bwd(x, g, dxo, a, b, wg, wu, wd, exchange=None):
    t, d = x.shape
    f = wg.shape[0]
    tm = min(TOKEN_TILE, t)

    def body(x_ref, g_ref, dxo_ref, a_ref, b_ref, wg_ref, wu_ref, wd_ref, dx_ref, da_ref, db_ref, h_ref, dg_ref):
        _zero_at_start(dg_ref)
        gv = g_ref[...]
        h, xh, r = _rms(x_ref[...], gv)
        dxo = dxo_ref[...]
        ds = _dot((0.5 * dxo).astype(BF16), wd_ref[...], NT)
        af = a_ref[...].astype(F32)
        bf = b_ref[...].astype(F32)
        sg = _sigmoid(af)
        da = (ds * bf * (sg * (1.0 + af * (1.0 - sg)))).astype(BF16)
        db = (ds * (af * sg)).astype(BF16)
        dh = _dot(da, wg_ref[...]) + _dot(db, wu_ref[...])
        dx_ref[...] = _rms_bwd(dh, xh, r, gv) + dxo
        da_ref[...] = da
        db_ref[...] = db
        h_ref[...] = h.astype(BF16)
        dg_ref[...] += jnp.sum(dh * xh, axis=0, keepdims=True)

    return _call(
        body,
        name="ffn_bwd",
        grid=(t // tm,),
        in_specs=[
            _rows(tm, d), _full((1, d)), _rows(tm, d), _rows(tm, f), _rows(tm, f),
            _full_once((f, d)), _full_once((f, d)), _full_once((f, d)),
        ],
        out_specs=[_rows(tm, d), _rows(tm, f), _rows(tm, f), _rows(tm, d), _full((1, d))],
        out_shape=[
            jax.ShapeDtypeStruct((t, d), F32),
            jax.ShapeDtypeStruct((t, f), BF16),
            jax.ShapeDtypeStruct((t, f), BF16),
            jax.ShapeDtypeStruct((t, d), BF16),
            jax.ShapeDtypeStruct((1, d), F32),
        ],
        args=(x, g, dxo, a, b, wg, wu, wd),
        exchange=exchange,
    )


def _weight_grad(a, b, scale=1.0, exchange=None):
    t, m = a.shape
    n = b.shape[1]
    chips = N_DEV // 2
    r = m // N_DEV
    tk = min(REDUCE_TILE, t)
    halves = 2
    nb = n // halves
    nk = t // tk

    def body(a_ref, b_ref, o_ref, acc, send_buf, recv_buf, send_sems, recv_sems):
        k, j = pl.program_id(0), pl.program_id(1)
        x, y, c, _ = _mesh_place()
        sibling, _ = _peer(x, y, c, 1)
        bv = b_ref[...]
        if scale != 1.0:
            bv = bv * scale
        bb = bv.astype(BF16)
        acc_half = acc.at[j]

        @pl.when(k == 0)
        def _():
            acc_half[...] = jnp.zeros_like(acc_half)

        for i in range(m // MXU_ROWS):
            rows = slice(i * MXU_ROWS, (i + 1) * MXU_ROWS)
            acc_half[rows, :] += _dot(a_ref[:, rows].astype(BF16), bb, TN)

        def to_sibling(half):
            return _remote(send_buf.at[half], recv_buf.at[half], send_sems.at[half], recv_sems.at[half], sibling)

        def owned_rows(q, core):
            return pl.ds(pl.multiple_of((2 * q + core) * r, 8), r)

        for half in range(halves):
            @pl.when(jnp.logical_and(k == nk - 1, j == half))
            def _():
                for q in range(chips):
                    send_buf[half, q] = acc[half, owned_rows(q, 1 - c), :].astype(BF16)
                to_sibling(half).start()

        @pl.when(jnp.logical_and(k == nk - 1, j == halves - 1))
        def _():
            for half in range(halves):
                to_sibling(half).wait_send()
                to_sibling(half).wait_recv()
                for q in range(chips):
                    o_ref[q, :, half * nb:(half + 1) * nb] = (
                        acc[half, owned_rows(q, c), :] + recv_buf[half, q].astype(F32)).astype(BF16)

    (partial,), arrived = _call(
        body,
        name="weight_grad",
        grid=(nk, halves),
        in_specs=[pl.BlockSpec((tk, m), lambda k, j: (k, 0)), pl.BlockSpec((tk, nb), lambda k, j: (k, j))],
        out_specs=[pl.BlockSpec((chips, r, n), lambda k, j: (0, 0, 0))],
        out_shape=[jax.ShapeDtypeStruct((chips, r, n), BF16)],
        scratch_shapes=[
            pltpu.VMEM((halves, m, nb), F32),
            pltpu.VMEM((halves, chips, r, nb), BF16), pltpu.VMEM((halves, chips, r, nb), BF16),
            pltpu.SemaphoreType.DMA((halves,)), pltpu.SemaphoreType.DMA((halves,)),
        ],
        args=(a, b),
        exchange=exchange,
    )
    return partial, arrived


def _chunk_cumsum(v, reverse=False):
    n, width = v.shape
    row = lax.broadcasted_iota(jnp.int32, (n, n), 0)
    col = lax.broadcasted_iota(jnp.int32, (n, n), 1)
    earlier = col >= row if reverse else col <= row
    tri = jnp.where(jnp.logical_and(row // CHUNK == col // CHUNK, earlier), 1.0, 0.0).astype(BF16)
    hi = v.astype(BF16)
    rest = v - hi.astype(F32)
    mid = rest.astype(BF16)
    low = (rest - mid.astype(F32)).astype(BF16)
    sums = _dot(tri, jnp.concatenate([hi, mid, low], axis=1))
    return sums[:, 0:width] + sums[:, width:2 * width] + sums[:, 2 * width:3 * width]


def _shift_rows(v, shift, edge):
    n = v.shape[0]
    row = lax.broadcasted_iota(jnp.int32, (n, 1), 0)
    out = pltpu.roll(v, shift % n, axis=0)
    if shift > 0:
        for j in range(shift):
            out = jnp.where(row == j, edge[8 - shift + j:8 - shift + j + 1, :], out)
    else:
        for j in range(-shift):
            out = jnp.where(row == n + shift + j, edge[j:j + 1, :], out)
    return out


def _gates(z, lbp):
    w = HGRN_W
    lb = _sigmoid(lbp[0:1, :] - lbp[1:2, :])
    zq = z[:, 0:w]
    sig = _sigmoid(z[:, w:2 * w])
    f = lb + (1.0 - lb) * sig
    sq = _sigmoid(zq)
    q = zq * sq * HGRN_DK ** -0.5
    return lb, sig, f, sq, q


def _decayed_operands(q, f, v, qh_buf, kh_buf, kbar_buf, v_buf, etot_buf):
    n, width = f.shape
    bcum = _chunk_cumsum(jnp.log(f))
    total = jnp.concatenate(
        [jnp.broadcast_to(bcum[c + CHUNK - 1:c + CHUNK, :], (CHUNK, width)) for c in range(0, n, CHUNK)], axis=0)
    eb, enb, erest = jnp.exp(bcum), jnp.exp(-bcum), jnp.exp(total - bcum)
    kk = 1.0 - f
    qh_buf[...] = (q * eb).astype(BF16)
    kh_buf[...] = (kk * enb).astype(BF16)
    kbar_buf[...] = (kk * erest).astype(BF16)
    v_buf[...] = v.astype(BF16)
    etot_buf[...] = jnp.exp(total)
    return eb, enb, erest


def _short_conv(u, edge, cw):
    return cw[0:1, :] * _shift_rows(u, 2, edge) + cw[1:2, :] * _shift_rows(u, 1, edge) + cw[2:3, :] * u


def _block_causal_mask(n):
    row = lax.broadcasted_iota(jnp.int32, (n, n), 0)
    col = lax.broadcasted_iota(jnp.int32, (n, n), 1)
    return jnp.logical_and(row // CHUNK == col // CHUNK, col <= row)


def _spread(v, chunk_of_row, nc):
    return jnp.concatenate([jnp.where(chunk_of_row == c, v, jnp.zeros_like(v)) for c in range(nc)], axis=1)


def _pick(r, chunk_of_row, nc):
    out = jnp.where(chunk_of_row == 0, r[:, 0:HGRN_DK], 0.0)
    for c in range(1, nc):
        out = out + jnp.where(chunk_of_row == c, r[:, c * HGRN_DK:(c + 1) * HGRN_DK], 0.0)
    return out


def _mix_fwd(x, g, w_in, lbp, gh, convw_t, w_out, exchange=None):
    t, d = x.shape
    zw = w_in.shape[0]
    w = HGRN_W
    tm = min(TOKEN_TILE, t)
    nc = tm // CHUNK
    n_chunks = t // CHUNK

    def body(x_ref, g_ref, win_ref, lbp_ref, gh_ref, cw_ref, wout_ref,
             xo_ref, z_ref, o_ref, st_ref, y_ref, state, ucarry, qh_buf, kh_buf, kbar_buf, v_buf, etot_buf):
        _zero_at_start(state, ucarry)
        xv = x_ref[...]
        h, _, _ = _rms(xv, g_ref[...])
        z_ref[...] = _dot(h.astype(BF16), win_ref[...], NT)
        z = z_ref[...]
        _, _, f, _, q = _gates(z, lbp_ref[...])
        _decayed_operands(q, f, z[:, 2 * w:3 * w], qh_buf, kh_buf, kbar_buf, v_buf, etot_buf)
        mask = _block_causal_mask(tm)
        chunk_of_row = lax.broadcasted_iota(jnp.int32, (tm, 1), 0) // CHUNK
        heads = range(HGRN_HEADS)
        hcols = [slice(hd * HGRN_DK, (hd + 1) * HGRN_DK) for hd in heads]
        qh = [qh_buf[:, hcols[hd]] for hd in heads]
        vb = [v_buf[:, hcols[hd]] for hd in heads]
        scores = [jnp.where(mask, _dot(qh[hd], kh_buf[:, hcols[hd]], NT), 0.0).astype(BF16) for hd in heads]
        gains = [_dot(_spread(vb[hd], chunk_of_row, nc), kbar_buf[:, hcols[hd]], TN) for hd in heads]
        entering = []
        for hd in heads:
            states, st = [], state[hd]
            for c in range(nc):
                states.append(st)
                st_ref[c, hd] = st
                st = st * etot_buf[c * CHUNK:c * CHUNK + 1, hcols[hd]] + gains[hd][c * HGRN_DK:(c + 1) * HGRN_DK, :]
            state[hd] = st
            entering.append(jnp.concatenate(states, axis=0).astype(BF16))
        from_states = [_dot(qh[hd], entering[hd], NT) for hd in heads]
        o_heads = [_dot(scores[hd], vb[hd]) + _pick(from_states[hd], chunk_of_row, nc) for hd in heads]
        o_ref[...] = jnp.concatenate(o_heads, axis=1)
        ghv = gh_ref[...]
        normed = jnp.concatenate([_rms(o_heads[hd], ghv[:, hcols[hd]])[0] for hd in heads], axis=1)
        zg = z[:, 3 * w:4 * w]
        u = z[:, 5 * w:6 * w] * z[:, 6 * w:7 * w]
        conv = _short_conv(u, ucarry[...], cw_ref[...])
        ucarry[...] = u[tm - 8:tm, :]
        y = jnp.concatenate([normed * (zg * _sigmoid(zg)), z[:, 4 * w:5 * w] * conv], axis=1).astype(BF16)
        y_ref[...] = y
        xo_ref[...] = xv + _dot(y, wout_ref[...])

    return _call(
        body,
        name="mix_fwd",
        grid=(t // tm,),
        in_specs=[
            _rows(tm, d), _full((1, d)), _full((zw, d)), _full((2, w)), _full((1, w)), _full((3, w)),
            _full((2 * w, d)),
        ],
        out_specs=[
            _rows(tm, d), _rows(tm, zw), _rows(tm, w),
            pl.BlockSpec((nc, HGRN_HEADS, HGRN_DK, HGRN_DK), lambda i: (i, 0, 0, 0)),
            _rows(tm, 2 * w),
        ],
        out_shape=[
            jax.ShapeDtypeStruct((t, d), F32),
            jax.ShapeDtypeStruct((t, zw), F32),
            jax.ShapeDtypeStruct((t, w), F32),
            jax.ShapeDtypeStruct((n_chunks, HGRN_HEADS, HGRN_DK, HGRN_DK), F32),
            jax.ShapeDtypeStruct((t, 2 * w), BF16),
        ],
        scratch_shapes=[
            pltpu.VMEM((HGRN_HEADS, HGRN_DK, HGRN_DK), F32), pltpu.VMEM((8, w), F32),
            pltpu.VMEM((tm, w), BF16), pltpu.VMEM((tm, w), BF16), pltpu.VMEM((tm, w), BF16),
            pltpu.VMEM((tm, w), BF16), pltpu.VMEM((tm, w), F32),
        ],
        args=(x, g, w_in, lbp, gh, convw_t, w_out),
        exchange=exchange,
    )


def _mix_bwd(x, g, dxo, z, o, states, w_in, lbp, gh, convw_t, w_out, exchange=None):
    t, d = x.shape
    zw = w_in.shape[0]
    w = HGRN_W
    tm = min(TOKEN_TILE, t)
    nc = tm // CHUNK
    n = t // tm

    def body(x_ref, g_ref, dxo_ref, z_ref, zprev_ref, o_ref, st_ref, win_ref, lbp_ref, gh_ref, cw_ref, wout_ref,
             dx_ref, dz_ref, h_ref, dg_ref, dlbp_ref, dgh_ref, dcw_ref,
             dstate, dcarry, do_buf, qh_buf, kh_buf, kbar_buf, v_buf, etot_buf):
        _zero_at_start(dstate, dcarry, dg_ref, dlbp_ref, dgh_ref, dcw_ref)
        gv = g_ref[...]
        h, xh, r = _rms(x_ref[...], gv)
        h_ref[...] = h.astype(BF16)
        dxo = dxo_ref[...]
        dy = _dot(dxo.astype(BF16), wout_ref[...], NT)
        z = z_ref[...]
        lb, sig, f, sq, q = _gates(z, lbp_ref[...])
        eb, enb, erest = _decayed_operands(q, f, z[:, 2 * w:3 * w], qh_buf, kh_buf, kbar_buf, v_buf, etot_buf)

        ghv = gh_ref[...]
        zg = z[:, 3 * w:4 * w]
        sgz = _sigmoid(zg)
        dyh = dy[:, 0:w]
        don = dyh * (zg * sgz)
        heads = range(HGRN_HEADS)
        hcols = [slice(hd * HGRN_DK, (hd + 1) * HGRN_DK) for hd in heads]
        norms = [_rms(o_ref[:, hcols[hd]], ghv[:, hcols[hd]]) for hd in heads]
        on = jnp.concatenate([norms[hd][0] for hd in heads], axis=1)
        oh = jnp.concatenate([norms[hd][1] for hd in heads], axis=1)
        dz_ref[:, 3 * w:4 * w] = (dyh * on * (sgz * (1.0 + zg * (1.0 - sgz)))).astype(BF16)
        dgh_ref[...] += jnp.sum(don * oh, axis=0, keepdims=True)
        do_buf[...] = jnp.concatenate(
            [_rms_bwd(don[:, hcols[hd]], norms[hd][1], norms[hd][2], ghv[:, hcols[hd]]) for hd in heads],
            axis=1).astype(BF16)

        zb = z[:, 4 * w:5 * w]
        zc = z[:, 5 * w:6 * w]
        zu = z[:, 6 * w:7 * w]
        u = zc * zu
        cw = cw_ref[...]
        zp = zprev_ref[...]
        uprev = jnp.where(pl.program_id(0) == n - 1, 0.0, zp[:, 5 * w:6 * w] * zp[:, 6 * w:7 * w])
        dyc = dy[:, w:2 * w]
        dz_ref[:, 4 * w:5 * w] = (dyc * _short_conv(u, uprev, cw)).astype(BF16)
        dconv = dyc * zb
        edge = dcarry[...]
        dconv1 = _shift_rows(dconv, -1, edge)
        dconv2 = _shift_rows(dconv, -2, edge)
        dcarry[...] = dconv[0:8, :]
        du = cw[2:3, :] * dconv + cw[1:2, :] * dconv1 + cw[0:1, :] * dconv2
        dz_ref[:, 5 * w:6 * w] = (du * zu).astype(BF16)
        dz_ref[:, 6 * w:7 * w] = (du * zc).astype(BF16)
        dcw_ref[...] += jnp.concatenate([
            jnp.sum(u * dconv2, axis=0, keepdims=True),
            jnp.sum(u * dconv1, axis=0, keepdims=True),
            jnp.sum(u * dconv, axis=0, keepdims=True)], axis=0)

        mask = _block_causal_mask(tm)
        chunk_of_row = lax.broadcasted_iota(jnp.int32, (tm, 1), 0) // CHUNK
        heads = range(HGRN_HEADS)
        hcols = [slice(hd * HGRN_DK, (hd + 1) * HGRN_DK) for hd in heads]
        qhb = [qh_buf[:, hcols[hd]] for hd in heads]
        khb = [kh_buf[:, hcols[hd]] for hd in heads]
        vb = [v_buf[:, hcols[hd]] for hd in heads]
        dob = [do_buf[:, hcols[hd]] for hd in heads]
        scores = [jnp.where(mask, _dot(qhb[hd], khb[hd], NT), 0.0).astype(BF16) for hd in heads]
        dscores = [jnp.where(mask, _dot(dob[hd], vb[hd], NT), 0.0).astype(BF16) for hd in heads]
        gains = [_dot(_spread(dob[hd], chunk_of_row, nc), qhb[hd], TN) for hd in heads]
        dst_rows, dst_lanes, st_lanes, carries = [], [], [], []
        for hd in heads:
            entering = [st_ref[c, hd] for c in range(nc)]
            leaving, carried_back = [None] * nc, [None] * nc
            dst = dstate[hd]
            for c in reversed(range(nc)):
                elast = etot_buf[c * CHUNK:c * CHUNK + 1, hcols[hd]]
                leaving[c] = dst
                carried_back[c] = jnp.sum(dst * entering[c], axis=0, keepdims=True) * elast
                dst = dst * elast + gains[hd][c * HGRN_DK:(c + 1) * HGRN_DK, :]
            dstate[hd] = dst
            dst_rows.append(jnp.concatenate(leaving, axis=0).astype(BF16))
            dst_lanes.append(jnp.concatenate(leaving, axis=1).astype(BF16))
            st_lanes.append(jnp.concatenate(entering, axis=1).astype(BF16))
            carries.append(carried_back)
        dv = [_dot(scores[hd], dob[hd], TN) + _pick(_dot(kbar_buf[:, hcols[hd]], dst_rows[hd], NT), chunk_of_row, nc)
              for hd in heads]
        dz_ref[:, 2 * w:3 * w] = jnp.concatenate(dv, axis=1).astype(BF16)
        dqh = jnp.concatenate(
            [_dot(dscores[hd], khb[hd]) + _pick(_dot(dob[hd], st_lanes[hd]), chunk_of_row, nc) for hd in heads], axis=1)
        dkh = jnp.concatenate([_dot(dscores[hd], qhb[hd], TN) for hd in heads], axis=1)
        dkbar = jnp.concatenate([_pick(_dot(vb[hd], dst_lanes[hd]), chunk_of_row, nc) for hd in heads], axis=1)

        kbar_dkbar = kbar_buf[...].astype(F32) * dkbar
        db = qh_buf[...].astype(F32) * dqh - kh_buf[...].astype(F32) * dkh - kbar_dkbar
        through_last = jnp.concatenate([
            jnp.broadcast_to(
                jnp.sum(kbar_dkbar[c * CHUNK:(c + 1) * CHUNK], axis=0, keepdims=True)
                + jnp.concatenate([carries[hd][c] for hd in heads], axis=1),
                (CHUNK, w))
            for c in range(nc)], axis=0)
        dlogf = _chunk_cumsum(db, reverse=True) + through_last
        df = dlogf / f - (dkh * enb + dkbar * erest)
        zq = z[:, 0:w]
        dz_ref[:, 0:w] = (dqh * eb * HGRN_DK ** -0.5 * (sq * (1.0 + zq * (1.0 - sq)))).astype(BF16)
        dz_ref[:, w:2 * w] = (df * (1.0 - lb) * sig * (1.0 - sig)).astype(BF16)
        dlb = jnp.sum(df * (1.0 - sig), axis=0, keepdims=True) * lb * (1.0 - lb)
        dlbp_ref[...] += jnp.concatenate([dlb, -dlb], axis=0)

        dh = _dot(dz_ref[...], win_ref[...])
        dx_ref[...] = _rms_bwd(dh, xh, r, gv) + dxo
        dg_ref[...] += jnp.sum(dh * xh, axis=0, keepdims=True)

    return _call(
        body,
        name="mix_bwd",
        grid=(n,),
        in_specs=[
            _rows_rev(tm, d, n), _full((1, d)), _rows_rev(tm, d, n), _rows_rev(tm, zw, n),
            pl.BlockSpec((8, zw), lambda i: (jnp.maximum((n - 1 - i) * (tm // 8) - 1, 0), 0)),
            _rows_rev(tm, w, n),
            pl.BlockSpec((nc, HGRN_HEADS, HGRN_DK, HGRN_DK), lambda i: (n - 1 - i, 0, 0, 0)),
            _full((zw, d)), _full((2, w)), _full((1, w)), _full((3, w)), _full((2 * w, d)),
        ],
        out_specs=[
            _rows_rev(tm, d, n), _rows_rev(tm, zw, n), _rows_rev(tm, d, n),
            _full((1, d)), _full((2, w)), _full((1, w)), _full((3, w)),
        ],
        out_shape=[
            jax.ShapeDtypeStruct((t, d), F32),
            jax.ShapeDtypeStruct((t, zw), BF16),
            jax.ShapeDtypeStruct((t, d), BF16),
            jax.ShapeDtypeStruct((1, d), F32),
            jax.ShapeDtypeStruct((2, w), F32),
            jax.ShapeDtypeStruct((1, w), F32),
            jax.ShapeDtypeStruct((3, w), F32),
        ],
        scratch_shapes=[
            pltpu.VMEM((HGRN_HEADS, HGRN_DK, HGRN_DK), F32), pltpu.VMEM((8, w), F32),
            pltpu.VMEM((tm, w), BF16),
            pltpu.VMEM((tm, w), BF16), pltpu.VMEM((tm, w), BF16), pltpu.VMEM((tm, w), BF16),
            pltpu.VMEM((tm, w), BF16), pltpu.VMEM((tm, w), F32),
        ],
        args=(x, g, dxo, z, z, o, states, w_in, lbp, gh, convw_t, w_out),
        exchange=exchange,
    )


def _memkv_fwd(mem, g, wkv):
    m, d = mem.shape
    nb, _, cb = wkv.shape

    def body(mem_ref, g_ref, wkv_ref, kv_ref):
        mn, _, _ = _rms(mem_ref[...], g_ref[...])
        mnb = mn.astype(BF16)
        for j in range(nb):
            kv_ref[:, j * cb:(j + 1) * cb] = _dot(mnb, wkv_ref[j]).astype(BF16)

    return pl.pallas_call(
        body,
        name="memkv_fwd",
        out_shape=jax.ShapeDtypeStruct((m, nb * cb), BF16),
        compiler_params=_params(),
    )(mem, g, wkv)


def _memkv_bwd(mem, g, dkv, wkv):
    m, d = mem.shape
    nb, _, cb = wkv.shape
    chips = nb // 2

    def body(mem_ref, g_ref, dkv_ref, wkv_ref, dw_ref, dg_ref, dw_all, send_buf, recv_buf, send_sem, recv_sem):
        x, y, c, _ = _mesh_place()
        sibling, _ = _peer(x, y, c, 1)
        mn, xh, _ = _rms(mem_ref[...], g_ref[...])
        mnb = mn.astype(BF16)
        dmn = jnp.zeros((m, d), F32)
        for j in range(nb):
            dkvb = dkv_ref[:, j * cb:(j + 1) * cb].astype(BF16)
            dw_all[j] = _dot(mnb, dkvb, TN)
            dmn = dmn + _dot(dkvb, wkv_ref[j], NT)
        dg_ref[...] = jnp.sum(dmn * xh, axis=0, keepdims=True)
        for q in range(chips):
            send_buf[q] = dw_all[2 * q + 1 - c].astype(BF16)
        to_sibling = _remote(send_buf, recv_buf, send_sem, recv_sem, sibling)
        to_sibling.start()
        to_sibling.wait_send()
        to_sibling.wait_recv()
        for q in range(chips):
            dw_ref[q] = (dw_all[2 * q + c] + recv_buf[q].astype(F32)).astype(BF16)

    return pl.pallas_call(
        body,
        name="memkv_bwd",
        out_shape=[jax.ShapeDtypeStruct((chips, d, cb), BF16), jax.ShapeDtypeStruct((1, d), F32)],
        scratch_shapes=[
            pltpu.VMEM((nb, d, cb), F32), pltpu.VMEM((chips, d, cb), BF16), pltpu.VMEM((chips, d, cb), BF16),
            pltpu.SemaphoreType.DMA, pltpu.SemaphoreType.DMA,
        ],
        compiler_params=_params(),
    )(mem, g, dkv, wkv)


def _softmax_rows(qm_h, k_h):
    sc = _dot(qm_h, k_h, NT) * MEM_HD ** -0.5
    e = jnp.exp(sc - jnp.max(sc, axis=-1, keepdims=True))
    return e / jnp.sum(e, axis=-1, keepdims=True)


def _xattn_fwd(x, g, wq, kv, wo, exchange=None):
    t, d = x.shape
    m = kv.shape[0]
    tm = min(XATTN_TILE, t)

    def body(x_ref, g_ref, wq_ref, kv_ref, wo_ref, xo_ref, hq_ref, qm_ref, att_ref):
        xv = x_ref[...]
        h, _, _ = _rms(xv, g_ref[...])
        hb = h.astype(BF16)
        hq_ref[...] = hb
        qm = _dot(hb, wq_ref[...]).astype(BF16)
        qm_ref[...] = qm
        heads = range(MEM_HEADS)
        kcols = [slice(hd * MEM_HD, (hd + 1) * MEM_HD) for hd in heads]
        p = [_softmax_rows(qm[:, kcols[hd]], kv_ref[:, kcols[hd]]) for hd in heads]
        att = jnp.concatenate(
            [_dot(p[hd].astype(BF16), kv_ref[:, d + hd * MEM_HD:d + (hd + 1) * MEM_HD]) for hd in heads],
            axis=1).astype(BF16)
        att_ref[...] = att
        xo_ref[...] = xv + _dot(att, wo_ref[...])

    return _call(
        body,
        name="xattn_fwd",
        grid=(t // tm,),
        in_specs=[_rows(tm, d), _full((1, d)), _full((d, d)), _full((m, 2 * d)), _full((d, d))],
        out_specs=[_rows(tm, d), _rows(tm, d), _rows(tm, d), _rows(tm, d)],
        out_shape=[
            jax.ShapeDtypeStruct((t, d), F32),
            jax.ShapeDtypeStruct((t, d), BF16),
            jax.ShapeDtypeStruct((t, d), BF16),
            jax.ShapeDtypeStruct((t, d), BF16),
        ],
        args=(x, g, wq, kv, wo),
        exchange=exchange,
    )


def _xattn_bwd(x, g, dxo, qm, kv, wq, wo, exchange=None):
    t, d = x.shape
    m = kv.shape[0]
    tm = min(XATTN_TILE, t)

    def body(x_ref, g_ref, dxo_ref, qm_ref, kv_ref, wq_ref, wo_ref, dx_ref, dqm_ref, dkv_ref, dg_ref):
        _zero_at_start(dkv_ref, dg_ref)
        gv = g_ref[...]
        _, xh, r = _rms(x_ref[...], gv)
        dxo = dxo_ref[...]
        datt = _dot(dxo.astype(BF16), wo_ref[...], NT).astype(BF16)
        heads = range(MEM_HEADS)
        kcols = [slice(hd * MEM_HD, (hd + 1) * MEM_HD) for hd in heads]
        vcols = [slice(d + hd * MEM_HD, d + (hd + 1) * MEM_HD) for hd in heads]
        qm_h = [qm_ref[:, kcols[hd]] for hd in heads]
        p = [_softmax_rows(qm_h[hd], kv_ref[:, kcols[hd]]) for hd in heads]
        dp = [_dot(datt[:, kcols[hd]], kv_ref[:, vcols[hd]], NT) for hd in heads]
        dsc = [(p[hd] * (dp[hd] - jnp.sum(p[hd] * dp[hd], axis=-1, keepdims=True)) * MEM_HD ** -0.5).astype(BF16)
               for hd in heads]
        dqm = jnp.concatenate([_dot(dsc[hd], kv_ref[:, kcols[hd]]) for hd in heads], axis=1).astype(BF16)
        dqm_ref[...] = dqm
        dkv_ref[...] += jnp.concatenate(
            [_dot(dsc[hd], qm_h[hd], TN) for hd in heads]
            + [_dot(p[hd].astype(BF16), datt[:, kcols[hd]], TN) for hd in heads], axis=1)
        dh = _dot(dqm, wq_ref[...], NT)
        dx_ref[...] = _rms_bwd(dh, xh, r, gv) + dxo
        dg_ref[...] += jnp.sum(dh * xh, axis=0, keepdims=True)

    return _call(
        body,
        name="xattn_bwd",
        grid=(t // tm,),
        in_specs=[
            _rows(tm, d), _full((1, d)), _rows(tm, d), _rows(tm, d), _full((m, 2 * d)), _full((d, d)), _full((d, d)),
        ],
        out_specs=[_rows(tm, d), _rows(tm, d), _full((m, 2 * d)), _full((1, d))],
        out_shape=[
            jax.ShapeDtypeStruct((t, d), F32),
            jax.ShapeDtypeStruct((t, d), BF16),
            jax.ShapeDtypeStruct((m, 2 * d), F32),
            jax.ShapeDtypeStruct((1, d), F32),
        ],
        args=(x, g, dxo, qm, kv, wq, wo),
        exchange=exchange,
    )


def _mesh_place():
    x, y, c = lax.axis_index("x"), lax.axis_index("y"), lax.axis_index("c")
    return x, y, c, 4 * x + 2 * y + c


def _peer(x, y, c, k):
    px = 1 - x if k & 4 else x
    py = 1 - y if k & 2 else y
    pc = 1 - c if k & 1 else c
    return (px, py, pc), 4 * px + 2 * py + pc


ICI_HOPS = (2, 4, 6)
N_HOPS = len(ICI_HOPS)


def _remote(src, dst, send_sem, recv_sem, peer):
    return pltpu.make_async_remote_copy(
        src_ref=src, dst_ref=dst, send_sem=send_sem, recv_sem=recv_sem, device_id=peer, device_id_type=MESH_IDS)


def _gather_exchange(shards, middle_eighths=MIDDLE_EIGHTHS):
    n = len(shards)

    def place():
        x, y, c, me = _mesh_place()
        sibling, _ = _peer(x, y, c, 1)
        to_x, from_x = _peer(x, y, c, 4)
        to_y, from_y = _peer(x, y, c, 2)
        _, from_diagonal = _peer(x, y, c, 6)
        onward = (c * to_y[0] + (1 - c) * to_x[0], c * to_y[1] + (1 - c) * to_x[1], c)
        passed_on = c * from_x + (1 - c) * from_y
        return me, sibling, (to_x, to_y, onward), (from_x, from_y, from_diagonal), passed_on

    def start(src, dst, sems):
        ici_send, ici_recv, pair_send, pair_recv, local = sems
        me, sibling, targets, _, _ = place()
        for a in range(n):
            pltpu.make_async_copy(src[a], dst[a].at[me], local.at[a]).start()
            for j in range(2):
                _remote(src[a], dst[a].at[me], ici_send.at[a, j], ici_recv.at[a, j], targets[j]).start()
            _remote(src[a], dst[a].at[me], pair_send.at[a, 0], pair_recv.at[a, 0], sibling).start()

    def to_sibling(dst, sems, a, j, origin, sibling):
        _, _, pair_send, pair_recv, _ = sems
        slot = dst[a].at[origin]
        return _remote(slot, slot, pair_send.at[a, 1 + j], pair_recv.at[a, 1 + j], sibling)

    def middle(src, dst, sems):
        ici_send, ici_recv, _, _, _ = sems
        _, sibling, targets, origins, passed_on = place()
        for a in range(n):
            for j in range(2):
                _remote(src[a], dst[a].at[origins[j]], ici_send.at[a, j], ici_recv.at[a, j], targets[j]).wait_recv()
            slot = dst[a].at[passed_on]
            _remote(slot, slot, ici_send.at[a, 2], ici_recv.at[a, 2], targets[2]).start()
            for j in range(2):
                to_sibling(dst, sems, a, j, origins[j], sibling).start()

    def finish(src, dst, sems):
        ici_send, ici_recv, pair_send, pair_recv, local = sems
        me, sibling, targets, origins, _ = place()
        for a in range(n):
            _remote(src[a], dst[a].at[origins[2]], ici_send.at[a, 2], ici_recv.at[a, 2], targets[2]).wait_recv()
            to_sibling(dst, sems, a, 2, origins[2], sibling).start()
        for a in range(n):
            pltpu.make_async_copy(src[a], dst[a].at[me], local.at[a]).wait()
            for j in range(N_HOPS):
                _remote(src[a], dst[a].at[me], ici_send.at[a, j], ici_recv.at[a, j], targets[j]).wait_send()
            for j, origin in enumerate((me,) + origins):
                from_sibling = origin + 1 - 2 * (origin % 2)
                passed = _remote(src[a], dst[a].at[from_sibling], pair_send.at[a, j], pair_recv.at[a, j], sibling)
                passed.wait_send()
                passed.wait_recv()

    return _Exchange(
        shards,
        [jax.ShapeDtypeStruct((N_DEV,) + s.shape, s.dtype) for s in shards],
        [
            pltpu.SemaphoreType.DMA((n, N_HOPS)), pltpu.SemaphoreType.DMA((n, N_HOPS)),
            pltpu.SemaphoreType.DMA((n, N_HOPS + 1)), pltpu.SemaphoreType.DMA((n, N_HOPS + 1)),
            pltpu.SemaphoreType.DMA((n,)),
        ],
        start, finish, middle, middle_eighths)


def _scatter_copies(src, dst, sems, n, arrivals=False):
    send, recv, local = sems
    x, y, c, _ = _mesh_place()
    chip = 2 * x + y
    if arrivals is None:
        return [pltpu.make_async_copy(src[a].at[chip], dst[a].at[chip], local.at[a]) for a in range(n)]
    copies = []
    for a in range(n):
        for j, k in enumerate(ICI_HOPS):
            peer, _ = _peer(x, y, c, k)
            peer_chip = 2 * peer[0] + peer[1]
            slot = dst[a].at[peer_chip if arrivals else chip]
            copies.append(_remote(src[a].at[peer_chip], slot, send.at[a, j], recv.at[a, j], peer))
    return copies


def _scatter_start(src, dst, sems, n):
    for cp in _scatter_copies(src, dst, sems, n, arrivals=None) + _scatter_copies(src, dst, sems, n):
        cp.start()


def _scatter_finish(src, dst, sems, n):
    for cp in _scatter_copies(src, dst, sems, n, arrivals=None):
        cp.wait()
    for cp in _scatter_copies(src, dst, sems, n):
        cp.wait_send()
    for cp in _scatter_copies(src, dst, sems, n, arrivals=True):
        cp.wait_recv()


def _scatter_scratch(n):
    return [pltpu.SemaphoreType.DMA((n, N_HOPS)), pltpu.SemaphoreType.DMA((n, N_HOPS)), pltpu.SemaphoreType.DMA((n,))]


def _scatter_exchange(partials):
    n = len(partials)
    return _Exchange(
        partials, [jax.ShapeDtypeStruct(p.shape, p.dtype) for p in partials], _scatter_scratch(n),
        lambda src, dst, sems: _scatter_start(src, dst, sems, n),
        lambda src, dst, sems: _scatter_finish(src, dst, sems, n))


SMALL_LAYOUT = {
    "ffn1_norm": (0, 1, 1024), "mix_norm": (1, 1, 1024), "xattn_norm": (2, 1, 1024), "mem_norm": (3, 1, 1024),
    "ffn2_norm": (4, 1, 1024), "final_norm": (5, 1, 1024), "lb_param": (6, 2, 512), "hgrn_out_norm": (8, 1, 512),
    "conv_w": (9, 3, 512), "loss": (12, 1, 128),
}


def _final_exchange(partials, small):
    n = len(partials)
    names = list(small)
    width = 1024

    def body(*refs):
        src = refs[:n]
        pieces = refs[n:n + len(names)]
        dst = refs[n + len(names):2 * n + len(names)]
        total_ref = refs[2 * n + len(names)]
        pack, gathered, small_send, small_recv = refs[2 * n + len(names) + 1:2 * n + len(names) + 5]
        sems = refs[2 * n + len(names) + 5:]
        x, y, c, me = _mesh_place()
        pack[...] = jnp.zeros_like(pack)
        for name, piece in zip(names, pieces):
            row, nrows, ncols = SMALL_LAYOUT[name]
            pack[row:row + nrows, 0:ncols] = piece[...]
        for k in range(1, N_DEV):
            peer, _ = _peer(x, y, c, k)
            _remote(pack, gathered.at[me], small_send.at[k - 1], small_recv.at[k - 1], peer).start()
        _scatter_start(src, dst, sems, n)
        gathered[me] = pack[...]
        for k in range(1, N_DEV):
            peer, peer_index = _peer(x, y, c, k)
            landed = _remote(pack, gathered.at[peer_index], small_send.at[k - 1], small_recv.at[k - 1], peer)
            landed.wait_send()
            landed.wait_recv()
        total = gathered[0]
        for j in range(1, N_DEV):
            total = total + gathered[j]
        total_ref[...] = total
        _scatter_finish(src, dst, sems, n)

    hbm = pl.BlockSpec(memory_space=pltpu.HBM)
    vmem = pl.BlockSpec(memory_space=pltpu.VMEM)
    out = pl.pallas_call(
        body,
        name="final_exchange",
        in_specs=[hbm] * n + [vmem] * len(names),
        out_specs=[hbm] * n + [vmem],
        out_shape=[jax.ShapeDtypeStruct(p.shape, p.dtype) for p in partials]
        + [jax.ShapeDtypeStruct((SMALL_ROWS, width), F32)],
        scratch_shapes=[
            pltpu.VMEM((SMALL_ROWS, width), F32), pltpu.VMEM((N_DEV, SMALL_ROWS, width), F32),
            pltpu.SemaphoreType.DMA((N_DEV - 1,)), pltpu.SemaphoreType.DMA((N_DEV - 1,)),
        ] + _scatter_scratch(n),
        compiler_params=pltpu.CompilerParams(has_side_effects=True),
    )(*partials, *[small[k] for k in names])
    return out[:n], out[n]


def _adamw_math(w, g, m, v):
    m = ADAM_B1 * m + (1.0 - ADAM_B1) * g
    v = ADAM_B2 * v + (1.0 - ADAM_B2) * (g * g)
    m_hat = m / (1.0 - ADAM_B1 ** ADAM_STEP)
    v_hat = v / (1.0 - ADAM_B2 ** ADAM_STEP)
    delta = -ADAM_LR * (m_hat / (jnp.sqrt(v_hat) + ADAM_EPS) + ADAM_WD * w)
    return delta, m, v


def _adamw_shard(parts, w, m, v):
    r, c = w.shape
    n_parts = parts.shape[0]
    tr = max(rows for rows in range(16, r + 1, 16) if r % rows == 0 and rows * c <= ADAMW_TILE_ELEMENTS)

    def body(p_ref, w_ref, m_ref, v_ref, g_ref, d_ref, mo_ref, vo_ref):
        g = p_ref[0].astype(F32)
        for j in range(1, n_parts):
            g = g + p_ref[j].astype(F32)
        delta, mn, vn = _adamw_math(w_ref[...], g, m_ref[...], v_ref[...])
        g_ref[...] = g
        d_ref[...] = delta
        mo_ref[...] = mn
        vo_ref[...] = vn

    tile = pl.BlockSpec((tr, c), lambda i: (i, 0))
    return pl.pallas_call(
        body,
        name="adamw_shard",
        grid=(r // tr,),
        in_specs=[pl.BlockSpec((n_parts, tr, c), lambda i: (0, i, 0)), tile, tile, tile],
        out_specs=[tile] * 4,
        out_shape=[jax.ShapeDtypeStruct((r, c), F32)] * 4,
        compiler_params=_params(("parallel",)),
    )(parts, w, m, v)


def _adamw_small(gs, ws, ms, vs):
    n = len(gs)

    def body(*refs):
        g_refs, w_refs, m_refs, v_refs = refs[:n], refs[n:2 * n], refs[2 * n:3 * n], refs[3 * n:4 * n]
        g_out, d_out, m_out, v_out = refs[4 * n:5 * n], refs[5 * n:6 * n], refs[6 * n:7 * n], refs[7 * n:8 * n]
        for i in range(n):
            if gs[i].ndim == ws[i].ndim:
                g = g_refs[i][...]
            else:
                g = g_refs[i][0].astype(F32)
                for j in range(1, gs[i].shape[0]):
                    g = g + g_refs[i][j].astype(F32)
            delta, mn, vn = _adamw_math(w_refs[i][...], g, m_refs[i][...], v_refs[i][...])
            g_out[i][...] = g
            d_out[i][...] = delta
            m_out[i][...] = mn
            v_out[i][...] = vn

    shapes = [jax.ShapeDtypeStruct(w.shape, F32) for w in ws]
    out = pl.pallas_call(
        body,
        name="adamw_small",
        out_shape=shapes * 4,
        compiler_params=_params(),
    )(*gs, *ws, *ms, *vs)
    return out[:n], out[n:2 * n], out[2 * n:3 * n], out[3 * n:]


TRANSPOSED = ("ffn1_gate", "ffn1_up", "w_in", "ffn2_gate", "ffn2_up", "conv_w")
GROUP_FFN1 = ("ffn1_gate", "ffn1_up", "ffn1_down")
GROUP_MIX = ("w_in", "w_out")
GROUP_XATTN = ("w_q_mem", "w_kv_mem", "w_o_mem")
GROUP_FFN2 = ("ffn2_gate", "ffn2_up", "ffn2_down")
LARGE = GROUP_FFN1 + GROUP_MIX + GROUP_XATTN + GROUP_FFN2
SHORT_SHARDS = ("w_out", "w_q_mem", "w_kv_mem", "w_o_mem")
SMALL = ("ffn1_norm", "mix_norm", "lb_param", "hgrn_out_norm", "conv_w", "xattn_norm", "mem_norm", "ffn2_norm",
         "final_norm")
WEIGHTS = ("ffn1_norm", "ffn1_gate", "ffn1_up", "ffn1_down", "mix_norm", "w_in", "lb_param", "hgrn_out_norm",
           "conv_w", "w_out", "xattn_norm", "mem_norm", "w_q_mem", "w_kv_mem", "w_o_mem", "ffn2_norm", "ffn2_gate",
           "ffn2_up", "ffn2_down", "final_norm")


def kernel(x, mem, ffn1_norm, ffn1_gate, ffn1_up, ffn1_down, mix_norm, w_in, lb_param, hgrn_out_norm, conv_w, w_out, xattn_norm, mem_norm, w_q_mem, w_kv_mem, w_o_mem, ffn2_norm, ffn2_gate, ffn2_up, ffn2_down, final_norm, loss_target, m_ffn1_norm, m_ffn1_gate, m_ffn1_up, m_ffn1_down, m_mix_norm, m_w_in, m_lb_param, m_hgrn_out_norm, m_conv_w, m_w_out, m_xattn_norm, m_mem_norm, m_w_q_mem, m_w_kv_mem, m_w_o_mem, m_ffn2_norm, m_ffn2_gate, m_ffn2_up, m_ffn2_down, m_final_norm, v_ffn1_norm, v_ffn1_gate, v_ffn1_up, v_ffn1_down, v_mix_norm, v_w_in, v_lb_param, v_hgrn_out_norm, v_conv_w, v_w_out, v_xattn_norm, v_mem_norm, v_w_q_mem, v_w_kv_mem, v_w_o_mem, v_ffn2_norm, v_ffn2_gate, v_ffn2_up, v_ffn2_down, v_final_norm):
    given = dict(locals())
    me = 4 * lax.axis_index("x") + 2 * lax.axis_index("y") + lax.axis_index("c")
    x0, memv, target = x[0], mem[0], loss_target[0]

    def shard(prefix, name):
        v = given[prefix + name]
        if v.ndim == 1:
            return v.reshape(1, -1)
        if v.ndim == 2:
            return v
        return v[0].T if name in TRANSPOSED else v[0]

    w = {name: shard("", name) for name in WEIGHTS}
    m = {name: shard("m_", name) for name in WEIGHTS}
    v = {name: shard("v_", name) for name in WEIGHTS}

    conv_taps, conv_rows = w["conv_w"].shape
    conv_tile = jnp.pad(w["conv_w"], ((0, 8 - conv_taps), (0, 128 - conv_rows)))
    wire = {name: w[name].astype(BF16) for name in LARGE}
    full = {}

    def landed(names, gathered):
        for name, blocks in zip(names, gathered):
            _, r, c = blocks.shape
            full[name] = blocks if name == "w_kv_mem" else blocks.reshape(N_DEV * r, c)

    first = ("ffn1_gate", "ffn1_up")
    landed(first, _run_exchange(_gather_exchange([wire[k] for k in first]), "gather_first"))

    riders = (("ffn1_down", "w_in"), ("w_out", "w_kv_mem"), ("w_q_mem", "w_o_mem", "ffn2_gate", "ffn2_up"),
              ("ffn2_down",))
    (a1, b1, s1), gathered = _ffn_up(
        x0, w["ffn1_norm"], full["ffn1_gate"], full["ffn1_up"],
        exchange=_gather_exchange([wire[k] for k in riders[0]]))
    landed(riders[0], gathered)
    (x1,), gathered = _ffn_down(
        x0, s1, full["ffn1_down"], exchange=_gather_exchange([wire[k] for k in riders[1]] + [conv_tile]))
    landed(riders[1], gathered)
    convw_t = gathered[-1][:, :conv_taps, :conv_rows].transpose(1, 0, 2).reshape(conv_taps, N_DEV * conv_rows)
    (x2, z, o_raw, states, ycat), gathered = _mix_fwd(
        x1, w["mix_norm"], full["w_in"], w["lb_param"], w["hgrn_out_norm"], convw_t, full["w_out"],
        exchange=_gather_exchange([wire[k] for k in riders[2]]))
    landed(riders[2], gathered)
    kv = _memkv_fwd(memv, w["mem_norm"], full["w_kv_mem"])
    (x3, hq, qm, att), gathered = _xattn_fwd(
        x2, w["xattn_norm"], full["w_q_mem"], kv, full["w_o_mem"],
        exchange=_gather_exchange([wire[k] for k in riders[3]], middle_eighths=EARLY_MIDDLE_EIGHTHS))
    landed(riders[3], gathered)
    (dx4, a2, b2, s2, loss_part, d_final), _ = _ffn_fwd(
        x3, w["ffn2_norm"], full["ffn2_gate"], full["ffn2_up"], full["ffn2_down"], head=(w["final_norm"], target))

    parts = {}
    waiting = []

    def carried():
        names = [name for name, _ in waiting]
        exchange = _scatter_exchange([p for _, p in waiting]) if waiting else None
        del waiting[:]
        return names, exchange

    def weight_grad(name, a, b, scale=1.0):
        names, exchange = carried()
        partial, arrived = _weight_grad(a, b, scale, exchange=exchange)
        parts.update(zip(names, arrived))
        waiting.append((name, partial))

    (dx3, da2, db2, h4, d_ffn2_norm), _ = _ffn_bwd(
        x3, w["ffn2_norm"], dx4, a2, b2, full["ffn2_gate"], full["ffn2_up"], full["ffn2_down"])
    weight_grad("ffn2_down", s2, dx4, 0.5)
    weight_grad("ffn2_gate", da2, h4)
    weight_grad("ffn2_up", db2, h4)
    names, exchange = carried()
    (dx2, dqm, dkv, d_xattn_norm), arrived = _xattn_bwd(
        x2, w["xattn_norm"], dx3, qm, kv, full["w_q_mem"], full["w_o_mem"], exchange=exchange)
    parts.update(zip(names, arrived))
    d_wkv, d_mem_norm = _memkv_bwd(memv, w["mem_norm"], dkv, full["w_kv_mem"])
    waiting.append(("w_kv_mem", d_wkv))
    names, exchange = carried()
    (dx1, dz, h2, d_mix_norm, d_lbp, d_gh, d_convw_t), arrived = _mix_bwd(
        x1, w["mix_norm"], dx2, z, o_raw, states, full["w_in"], w["lb_param"], w["hgrn_out_norm"], convw_t,
        full["w_out"], exchange=exchange)
    parts.update(zip(names, arrived))
    weight_grad("w_in", dz, h2)
    weight_grad("ffn1_down", s1, dx1, 0.5)
    (dx0, da1, db1, h1, d_ffn1_norm), _ = _ffn_bwd(
        x0, w["ffn1_norm"], dx1, a1, b1, full["ffn1_gate"], full["ffn1_up"], full["ffn1_down"])
    weight_grad("ffn1_gate", da1, h1)
    weight_grad("ffn1_up", db1, h1)
    weight_grad("w_o_mem", att, dx3)
    weight_grad("w_q_mem", hq, dqm)
    weight_grad("w_out", ycat, dx2)

    small_parts = {
        "ffn1_norm": d_ffn1_norm, "mix_norm": d_mix_norm, "xattn_norm": d_xattn_norm, "mem_norm": d_mem_norm,
        "ffn2_norm": d_ffn2_norm, "final_norm": d_final, "lb_param": d_lbp, "hgrn_out_norm": d_gh,
        "conv_w": d_convw_t, "loss": loss_part,
    }
    names = [name for name, _ in waiting]
    arrived, total = _final_exchange([p for _, p in waiting], small_parts)
    parts.update(zip(names, arrived))

    g_out, d_out, m_out, v_out = {}, {}, {}, {}
    for name in LARGE:
        if name not in SHORT_SHARDS:
            g_out[name], d_out[name], m_out[name], v_out[name] = _adamw_shard(parts[name], w[name], m[name], v[name])
    g_small = {name: parts[name] for name in SHORT_SHARDS}
    for name in SMALL:
        row, nrows, ncols = SMALL_LAYOUT[name]
        g_small[name] = total[row:row + nrows, 0:ncols]
    g_small["conv_w"] = lax.dynamic_slice_in_dim(g_small["conv_w"], me * conv_rows, conv_rows, axis=1)
    together = SMALL + SHORT_SHARDS
    gs, ds, ms, vs = _adamw_small(
        [g_small[k] for k in together], [w[k] for k in together], [m[k] for k in together],
        [v[k] for k in together])
    for i, name in enumerate(together):
        g_out[name], d_out[name], m_out[name], v_out[name] = gs[i], ds[i], ms[i], vs[i]

    def shaped(value, name):
        return (value.T if name in TRANSPOSED else value).reshape(given[name].shape)

    loss = total[SMALL_LAYOUT["loss"][0], 0]
    outs = [loss, dx0.reshape(x.shape)]
    for group in (g_out, d_out, m_out, v_out):
        outs += [shaped(group[name], name) for name in WEIGHTS]
    return tuple(outs)
```

```python
import jax
import jax.numpy as jnp
from jax import lax
from jax.experimental import pallas as pl
from jax.experimental.pallas import tpu as pltpu

F32 = jnp.float32
BF16 = jnp.bfloat16
MESH_IDS = pl.DeviceIdType.MESH

N_DEV = 8
EPS = 1e-6
HGRN_HEADS = 4
HGRN_DK = 128
HGRN_W = 512
CHUNK = 64
MEM_HEADS = 4
MEM_HD = 256
ADAM_LR = 0.001
ADAM_B1 = 0.9
ADAM_B2 = 0.999
ADAM_EPS = 1e-08
ADAM_WD = 0.01
ADAM_STEP = 10

TOKEN_TILE = 256
XATTN_TILE = 512
WIDE_TILE = 512
REDUCE_TILE = 1024
ADAMW_TILE_ELEMENTS = 256 * 1024
MIDDLE_EIGHTHS = 5
EARLY_MIDDLE_EIGHTHS = 4
LATE_MIDDLE_EIGHTHS = 6
MXU_ROWS = 256
VMEM_LIMIT = 60 * 1024 * 1024
SMALL_ROWS = 16
NT = (((1,), (1,)), ((), ()))
TN = (((0,), (0,)), ((), ()))


def _params(sem=None):
    return pltpu.CompilerParams(dimension_semantics=sem, vmem_limit_bytes=VMEM_LIMIT)


def _dot(a, b, dims=None):
    if dims is None:
        return jnp.dot(a, b, preferred_element_type=F32)
    return lax.dot_general(a, b, dims, preferred_element_type=F32)


def _sigmoid(v):
    return 1.0 / (1.0 + jnp.exp(-v))


def _rms(x, g):
    r = lax.rsqrt(jnp.mean(x * x, axis=-1, keepdims=True) + EPS)
    xh = x * r
    return xh * g, xh, r


def _rms_bwd(dh, xh, r, g):
    dxh = dh * g
    return r * (dxh - xh * jnp.mean(dxh * xh, axis=-1, keepdims=True))


def _full(shape):
    return pl.BlockSpec(shape, lambda *_: (0,) * len(shape))


def _full_once(shape):
    return pl.BlockSpec(shape, lambda *_: (0,) * len(shape), pipeline_mode=pl.Buffered(1))


def _rows(tm, width):
    return pl.BlockSpec((tm, width), lambda i: (i, 0))


def _rows_rev(tm, width, n):
    return pl.BlockSpec((tm, width), lambda i: (n - 1 - i, 0))


def _zero_at_start(*refs):
    @pl.when(pl.program_id(0) == 0)
    def _():
        for ref in refs:
            ref[...] = jnp.zeros_like(ref)


class _Exchange:
    def __init__(self, operands, out_shapes, scratch, start, finish, middle=None, middle_eighths=MIDDLE_EIGHTHS):
        self.operands, self.out_shapes, self.scratch = list(operands), list(out_shapes), list(scratch)
        self.start, self.middle, self.finish, self.middle_eighths = start, middle, finish, middle_eighths


def _call(body, *, name, grid, in_specs, out_specs, out_shape, args, scratch_shapes=(), exchange=None):
    semantics = ("arbitrary",) * len(grid)
    if exchange is None:
        out = pl.pallas_call(
            body, name=name, grid=grid, in_specs=in_specs, out_specs=out_specs, out_shape=out_shape,
            scratch_shapes=list(scratch_shapes), compiler_params=_params(semantics))(*args)
        return out, []
    hbm = pl.BlockSpec(memory_space=pltpu.HBM)
    n_in, n_out, n_scr = len(in_specs), len(out_specs), len(scratch_shapes)
    e_in, e_out = len(exchange.operands), len(exchange.out_shapes)

    def carried(*refs):
        ins, rest = refs[:n_in], refs[n_in:]
        e_ins, rest = rest[:e_in], rest[e_in:]
        outs, rest = rest[:n_out], rest[n_out:]
        e_outs, rest = rest[:e_out], rest[e_out:]
        scr, e_scr = rest[:n_scr], rest[n_scr:]
        first = last = None
        for axis, size in enumerate(grid):
            at_start, at_end = pl.program_id(axis) == 0, pl.program_id(axis) == size - 1
            first = at_start if first is None else jnp.logical_and(first, at_start)
            last = at_end if last is None else jnp.logical_and(last, at_end)

        @pl.when(first)
        def _():
            exchange.start(e_ins, e_outs, e_scr)

        body(*ins, *outs, *scr)

        if exchange.middle is not None:
            assert len(grid) == 1

            @pl.when(pl.program_id(0) == (grid[0] * exchange.middle_eighths) // 8)
            def _():
                exchange.middle(e_ins, e_outs, e_scr)

        @pl.when(last)
        def _():
            exchange.finish(e_ins, e_outs, e_scr)

    out = pl.pallas_call(
        carried, name=name, grid=grid, in_specs=list(in_specs) + [hbm] * e_in,
        out_specs=list(out_specs) + [hbm] * e_out, out_shape=list(out_shape) + exchange.out_shapes,
        scratch_shapes=list(scratch_shapes) + exchange.scratch,
        compiler_params=pltpu.CompilerParams(
            dimension_semantics=semantics, vmem_limit_bytes=VMEM_LIMIT, has_side_effects=True),
    )(*args, *exchange.operands)
    return out[:n_out], out[n_out:]


def _run_exchange(exchange, name):
    hbm = pl.BlockSpec(memory_space=pltpu.HBM)
    e_in, e_out = len(exchange.operands), len(exchange.out_shapes)

    def body(*refs):
        e_ins, e_outs, e_scr = refs[:e_in], refs[e_in:e_in + e_out], refs[e_in + e_out:]
        exchange.start(e_ins, e_outs, e_scr)
        if exchange.middle is not None:
            exchange.middle(e_ins, e_outs, e_scr)
        exchange.finish(e_ins, e_outs, e_scr)

    return pl.pallas_call(
        body, name=name, in_specs=[hbm] * e_in, out_specs=[hbm] * e_out, out_shape=exchange.out_shapes,
        scratch_shapes=exchange.scratch, compiler_params=pltpu.CompilerParams(has_side_effects=True),
    )(*exchange.operands)


def _loss_head(xo, gf, tgt):
    d = xo.shape[1]
    y, xh, r = _rms(xo, gf)
    err = y - tgt
    dy = err * (1.0 / d)
    loss = 0.5 * jnp.sum(jnp.sum(err * err, axis=-1, keepdims=True) * (1.0 / d), axis=0, keepdims=True)
    return _rms_bwd(dy, xh, r, gf), loss, jnp.sum(dy * xh, axis=0, keepdims=True)


def _ffn_fwd(x, g, wg, wu, wd, exchange=None, head=None):
    t, d = x.shape
    f = wg.shape[0]
    tm = min(WIDE_TILE, t)

    def body(x_ref, g_ref, wg_ref, wu_ref, wd_ref, *rest):
        if head is None:
            xo_ref, a_ref, b_ref, s_ref = rest
        else:
            gf_ref, tgt_ref, xo_ref, a_ref, b_ref, s_ref, loss_ref, dgf_ref = rest
            _zero_at_start(loss_ref, dgf_ref)
        xv = x_ref[...]
        h, _, _ = _rms(xv, g_ref[...])
        hb = h.astype(BF16)
        a = _dot(hb, wg_ref[...], NT)
        b = _dot(hb, wu_ref[...], NT)
        s = (a * _sigmoid(a) * b).astype(BF16)
        xo = xv + 0.5 * _dot(s, wd_ref[...])
        if head is None:
            xo_ref[...] = xo
        else:
            xo_ref[...], loss, dgf = _loss_head(xo, gf_ref[...], tgt_ref[...])
            loss_ref[...] += jnp.broadcast_to(loss, (1, 128))
            dgf_ref[...] += dgf
        a_ref[...] = a.astype(BF16)
        b_ref[...] = b.astype(BF16)
        s_ref[...] = s

    in_specs = [_rows(tm, d), _full((1, d)), _full_once((f, d)), _full_once((f, d)), _full_once((f, d))]
    out_specs = [_rows(tm, d), _rows(tm, f), _rows(tm, f), _rows(tm, f)]
    out_shape = [
        jax.ShapeDtypeStruct((t, d), F32),
        jax.ShapeDtypeStruct((t, f), BF16),
        jax.ShapeDtypeStruct((t, f), BF16),
        jax.ShapeDtypeStruct((t, f), BF16),
    ]
    args = (x, g, wg, wu, wd)
    if head is not None:
        in_specs += [_full((1, d)), _rows(tm, d)]
        out_specs += [_full((1, 128)), _full((1, d))]
        out_shape += [jax.ShapeDtypeStruct((1, 128), F32), jax.ShapeDtypeStruct((1, d), F32)]
        args += tuple(head)
    return _call(
        body, name="ffn_fwd", grid=(t // tm,), in_specs=in_specs, out_specs=out_specs, out_shape=out_shape,
        args=args, exchange=exchange)


def _ffn_up(x, g, wg, wu, exchange=None):
    t, d = x.shape
    f = wg.shape[0]
    tm = min(TOKEN_TILE, t)

    def body(x_ref, g_ref, wg_ref, wu_ref, a_ref, b_ref, s_ref):
        h, _, _ = _rms(x_ref[...], g_ref[...])
        hb = h.astype(BF16)
        a = _dot(hb, wg_ref[...], NT)
        b = _dot(hb, wu_ref[...], NT)
        a_ref[...] = a.astype(BF16)
        b_ref[...] = b.astype(BF16)
        s_ref[...] = (a * _sigmoid(a) * b).astype(BF16)

    return _call(
        body, name="ffn_up", grid=(t // tm,),
        in_specs=[_rows(tm, d), _full((1, d)), _full_once((f, d)), _full_once((f, d))],
        out_specs=[_rows(tm, f)] * 3, out_shape=[jax.ShapeDtypeStruct((t, f), BF16)] * 3,
        args=(x, g, wg, wu), exchange=exchange)


def _ffn_down(x, s, wd, exchange=None):
    t, d = x.shape
    f = wd.shape[0]
    tm = min(TOKEN_TILE, t)

    def body(x_ref, s_ref, wd_ref, xo_ref):
        xo_ref[...] = x_ref[...] + 0.5 * _dot(s_ref[...], wd_ref[...])

    return _call(
        body, name="ffn_down", grid=(t // tm,),
        in_specs=[_rows(tm, d), _rows(tm, f), _full_once((f, d))],
        out_specs=[_rows(tm, d)], out_shape=[jax.ShapeDtypeStruct((t, d), F32)],
        args=(x, s, wd), exchange=exchange)


def _ffn_bwd(x, g, dxo, a, b, wg, wu, wd, exchange=None):
    t, d = x.shape
    f = wg.shape[0]
    tm = min(TOKEN_TILE, t)

    def body(x_ref, g_ref, dxo_ref, a_ref, b_ref, wg_ref, wu_ref, wd_ref, dx_ref, da_ref, db_ref, h_ref, dg_ref):
        _zero_at_start(dg_ref)
        gv = g_ref[...]
        h, xh, r = _rms(x_ref[...], gv)
        dxo = dxo_ref[...]
        ds = _dot((0.5 * dxo).astype(BF16), wd_ref[...], NT)
        af = a_ref[...].astype(F32)
        bf = b_ref[...].astype(F32)
        sg = _sigmoid(af)
        da = (ds * bf * (sg * (1.0 + af * (1.0 - sg)))).astype(BF16)
        db = (ds * (af * sg)).astype(BF16)
        dh = _dot(da, wg_ref[...]) + _dot(db, wu_ref[...])
        dx_ref[...] = _rms_bwd(dh, xh, r, gv) + dxo
        da_ref[...] = da
        db_ref[...] = db
        h_ref[...] = h.astype(BF16)
        dg_ref[...] += jnp.sum(dh * xh, axis=0, keepdims=True)

    return _call(
        body,
        name="ffn_bwd",
        grid=(t // tm,),
        in_specs=[
            _rows(tm, d), _full((1, d)), _rows(tm, d), _rows(tm, f), _rows(tm, f),
            _full_once((f, d)), _full_once((f, d)), _full_once((f, d)),
        ],
        out_specs=[_rows(tm, d), _rows(tm, f), _rows(tm, f), _rows(tm, d), _full((1, d))],
        out_shape=[
            jax.ShapeDtypeStruct((t, d), F32),
            jax.ShapeDtypeStruct((t, f), BF16),
            jax.ShapeDtypeStruct((t, f), BF16),
            jax.ShapeDtypeStruct((t, d), BF16),
            jax.ShapeDtypeStruct((1, d), F32),
        ],
        args=(x, g, dxo, a, b, wg, wu, wd),
        exchange=exchange,
    )


def _weight_grad(a, b, scale=1.0, exchange=None):
    t, m = a.shape
    n = b.shape[1]
    chips = N_DEV // 2
    r = m // N_DEV
    tk = min(REDUCE_TILE, t)
    halves = 2
    nb = n // halves
    nk = t // tk

    def body(a_ref, b_ref, o_ref, acc, send_buf, recv_buf, send_sems, recv_sems):
        k, j = pl.program_id(0), pl.program_id(1)
        x, y, c, _ = _mesh_place()
        sibling, _ = _peer(x, y, c, 1)
        bv = b_ref[...]
        if scale != 1.0:
            bv = bv * scale
        bb = bv.astype(BF16)
        acc_half = acc.at[j]

        @pl.when(k == 0)
        def _():
            acc_half[...] = jnp.zeros_like(acc_half)

        for i in range(m // MXU_ROWS):
            rows = slice(i * MXU_ROWS, (i + 1) * MXU_ROWS)
            acc_half[rows, :] += _dot(a_ref[:, rows].astype(BF16), bb, TN)

        def to_sibling(half):
            return _remote(send_buf.at[half], recv_buf.at[half], send_sems.at[half], recv_sems.at[half], sibling)

        def owned_rows(q, core):
            return pl.ds(pl.multiple_of((2 * q + core) * r, 8), r)

        for half in range(halves):
            @pl.when(jnp.logical_and(k == nk - 1, j == half))
            def _():
                for q in range(chips):
                    send_buf[half, q] = acc[half, owned_rows(q, 1 - c), :].astype(BF16)
                to_sibling(half).start()

        @pl.when(jnp.logical_and(k == nk - 1, j == halves - 1))
        def _():
            for half in range(halves):
                to_sibling(half).wait_send()
                to_sibling(half).wait_recv()
                for q in range(chips):
                    o_ref[q, :, half * nb:(half + 1) * nb] = (
                        acc[half, owned_rows(q, c), :] + recv_buf[half, q].astype(F32)).astype(BF16)

    (partial,), arrived = _call(
        body,
        name="weight_grad",
        grid=(nk, halves),
        in_specs=[pl.BlockSpec((tk, m), lambda k, j: (k, 0)), pl.BlockSpec((tk, nb), lambda k, j: (k, j))],
        out_specs=[pl.BlockSpec((chips, r, n), lambda k, j: (0, 0, 0))],
        out_shape=[jax.ShapeDtypeStruct((chips, r, n), BF16)],
        scratch_shapes=[
            pltpu.VMEM((halves, m, nb), F32),
            pltpu.VMEM((halves, chips, r, nb), BF16), pltpu.VMEM((halves, chips, r, nb), BF16),
            pltpu.SemaphoreType.DMA((halves,)), pltpu.SemaphoreType.DMA((halves,)),
        ],
        args=(a, b),
        exchange=exchange,
    )
    return partial, arrived


def _chunk_cumsum(v, reverse=False):
    n, width = v.shape
    row = lax.broadcasted_iota(jnp.int32, (n, n), 0)
    col = lax.broadcasted_iota(jnp.int32, (n, n), 1)
    earlier = col >= row if reverse else col <= row
    tri = jnp.where(jnp.logical_and(row // CHUNK == col // CHUNK, earlier), 1.0, 0.0).astype(BF16)
    hi = v.astype(BF16)
    rest = v - hi.astype(F32)
    mid = rest.astype(BF16)
    low = (rest - mid.astype(F32)).astype(BF16)
    sums = _dot(tri, jnp.concatenate([hi, mid, low], axis=1))
    return sums[:, 0:width] + sums[:, width:2 * width] + sums[:, 2 * width:3 * width]


def _shift_rows(v, shift, edge):
    n = v.shape[0]
    row = lax.broadcasted_iota(jnp.int32, (n, 1), 0)
    out = pltpu.roll(v, shift % n, axis=0)
    if shift > 0:
        for j in range(shift):
            out = jnp.where(row == j, edge[8 - shift + j:8 - shift + j + 1, :], out)
    else:
        for j in range(-shift):
            out = jnp.where(row == n + shift + j, edge[j:j + 1, :], out)
    return out


def _gates(z, lbp):
    w = HGRN_W
    lb = _sigmoid(lbp[0:1, :] - lbp[1:2, :])
    zq = z[:, 0:w]
    sig = _sigmoid(z[:, w:2 * w])
    f = lb + (1.0 - lb) * sig
    sq = _sigmoid(zq)
    q = zq * sq * HGRN_DK ** -0.5
    return lb, sig, f, sq, q


def _decayed_operands(q, f, v, qh_buf, kh_buf, kbar_buf, v_buf, etot_buf):
    n, width = f.shape
    bcum = _chunk_cumsum(jnp.log(f))
    total = jnp.concatenate(
        [jnp.broadcast_to(bcum[c + CHUNK - 1:c + CHUNK, :], (CHUNK, width)) for c in range(0, n, CHUNK)], axis=0)
    eb, enb, erest = jnp.exp(bcum), jnp.exp(-bcum), jnp.exp(total - bcum)
    kk = 1.0 - f
    qh_buf[...] = (q * eb).astype(BF16)
    kh_buf[...] = (kk * enb).astype(BF16)
    kbar_buf[...] = (kk * erest).astype(BF16)
    v_buf[...] = v.astype(BF16)
    etot_buf[...] = jnp.exp(total)
    return eb, enb, erest


def _short_conv(u, edge, cw):
    return cw[0:1, :] * _shift_rows(u, 2, edge) + cw[1:2, :] * _shift_rows(u, 1, edge) + cw[2:3, :] * u


def _block_causal_mask(n):
    row = lax.broadcasted_iota(jnp.int32, (n, n), 0)
    col = lax.broadcasted_iota(jnp.int32, (n, n), 1)
    return jnp.logical_and(row // CHUNK == col // CHUNK, col <= row)


def _spread(v, chunk_of_row, nc):
    return jnp.concatenate([jnp.where(chunk_of_row == c, v, jnp.zeros_like(v)) for c in range(nc)], axis=1)


def _pick(r, chunk_of_row, nc):
    out = jnp.where(chunk_of_row == 0, r[:, 0:HGRN_DK], 0.0)
    for c in range(1, nc):
        out = out + jnp.where(chunk_of_row == c, r[:, c * HGRN_DK:(c + 1) * HGRN_DK], 0.0)
    return out


def _mix_fwd(x, g, w_in, lbp, gh, convw_t, w_out, exchange=None):
    t, d = x.shape
    zw = w_in.shape[0]
    w = HGRN_W
    tm = min(TOKEN_TILE, t)
    nc = tm // CHUNK
    n_chunks = t // CHUNK

    def body(x_ref, g_ref, win_ref, lbp_ref, gh_ref, cw_ref, wout_ref,
             xo_ref, z_ref, o_ref, st_ref, y_ref, state, ucarry, qh_buf, kh_buf, kbar_buf, v_buf, etot_buf):
        _zero_at_start(state, ucarry)
        xv = x_ref[...]
        h, _, _ = _rms(xv, g_ref[...])
        z_ref[...] = _dot(h.astype(BF16), win_ref[...], NT)
        z = z_ref[...]
        _, _, f, _, q = _gates(z, lbp_ref[...])
        _decayed_operands(q, f, z[:, 2 * w:3 * w], qh_buf, kh_buf, kbar_buf, v_buf, etot_buf)
        mask = _block_causal_mask(tm)
        chunk_of_row = lax.broadcasted_iota(jnp.int32, (tm, 1), 0) // CHUNK
        heads = range(HGRN_HEADS)
        hcols = [slice(hd * HGRN_DK, (hd + 1) * HGRN_DK) for hd in heads]
        qh = [qh_buf[:, hcols[hd]] for hd in heads]
        vb = [v_buf[:, hcols[hd]] for hd in heads]
        scores = [jnp.where(mask, _dot(qh[hd], kh_buf[:, hcols[hd]], NT), 0.0).astype(BF16) for hd in heads]
        gains = [_dot(_spread(vb[hd], chunk_of_row, nc), kbar_buf[:, hcols[hd]], TN) for hd in heads]
        entering = []
        for hd in heads:
            states, st = [], state[hd]
            for c in range(nc):
                states.append(st)
                st_ref[c, hd] = st
                st = st * etot_buf[c * CHUNK:c * CHUNK + 1, hcols[hd]] + gains[hd][c * HGRN_DK:(c + 1) * HGRN_DK, :]
            state[hd] = st
            entering.append(jnp.concatenate(states, axis=0).astype(BF16))
        from_states = [_dot(qh[hd], entering[hd], NT) for hd in heads]
        o_heads = [_dot(scores[hd], vb[hd]) + _pick(from_states[hd], chunk_of_row, nc) for hd in heads]
        o_ref[...] = jnp.concatenate(o_heads, axis=1)
        ghv = gh_ref[...]
        normed = jnp.concatenate([_rms(o_heads[hd], ghv[:, hcols[hd]])[0] for hd in heads], axis=1)
        zg = z[:, 3 * w:4 * w]
        u = z[:, 5 * w:6 * w] * z[:, 6 * w:7 * w]
        conv = _short_conv(u, ucarry[...], cw_ref[...])
        ucarry[...] = u[tm - 8:tm, :]
        y = jnp.concatenate([normed * (zg * _sigmoid(zg)), z[:, 4 * w:5 * w] * conv], axis=1).astype(BF16)
        y_ref[...] = y
        xo_ref[...] = xv + _dot(y, wout_ref[...])

    return _call(
        body,
        name="mix_fwd",
        grid=(t // tm,),
        in_specs=[
            _rows(tm, d), _full((1, d)), _full((zw, d)), _full((2, w)), _full((1, w)), _full((3, w)),
            _full((2 * w, d)),
        ],
        out_specs=[
            _rows(tm, d), _rows(tm, zw), _rows(tm, w),
            pl.BlockSpec((nc, HGRN_HEADS, HGRN_DK, HGRN_DK), lambda i: (i, 0, 0, 0)),
            _rows(tm, 2 * w),
        ],
        out_shape=[
            jax.ShapeDtypeStruct((t, d), F32),
            jax.ShapeDtypeStruct((t, zw), F32),
            jax.ShapeDtypeStruct((t, w), F32),
            jax.ShapeDtypeStruct((n_chunks, HGRN_HEADS, HGRN_DK, HGRN_DK), F32),
            jax.ShapeDtypeStruct((t, 2 * w), BF16),
        ],
        scratch_shapes=[
            pltpu.VMEM((HGRN_HEADS, HGRN_DK, HGRN_DK), F32), pltpu.VMEM((8, w), F32),
            pltpu.VMEM((tm, w), BF16), pltpu.VMEM((tm, w), BF16), pltpu.VMEM((tm, w), BF16),
            pltpu.VMEM((tm, w), BF16), pltpu.VMEM((tm, w), F32),
        ],
        args=(x, g, w_in, lbp, gh, convw_t, w_out),
        exchange=exchange,
    )


def _mix_bwd(x, g, dxo, z, o, states, w_in, lbp, gh, convw_t, w_out, exchange=None):
    t, d = x.shape
    zw = w_in.shape[0]
    w = HGRN_W
    tm = min(TOKEN_TILE, t)
    nc = tm // CHUNK
    n = t // tm

    def body(x_ref, g_ref, dxo_ref, z_ref, zprev_ref, o_ref, st_ref, win_ref, lbp_ref, gh_ref, cw_ref, wout_ref,
             dx_ref, dz_ref, h_ref, dg_ref, dlbp_ref, dgh_ref, dcw_ref,
             dstate, dcarry, do_buf, qh_buf, kh_buf, kbar_buf, v_buf, etot_buf):
        _zero_at_start(dstate, dcarry, dg_ref, dlbp_ref, dgh_ref, dcw_ref)
        gv = g_ref[...]
        h, xh, r = _rms(x_ref[...], gv)
        h_ref[...] = h.astype(BF16)
        dxo = dxo_ref[...]
        dy = _dot(dxo.astype(BF16), wout_ref[...], NT)
        z = z_ref[...]
        lb, sig, f, sq, q = _gates(z, lbp_ref[...])
        eb, enb, erest = _decayed_operands(q, f, z[:, 2 * w:3 * w], qh_buf, kh_buf, kbar_buf, v_buf, etot_buf)

        ghv = gh_ref[...]
        zg = z[:, 3 * w:4 * w]
        sgz = _sigmoid(zg)
        dyh = dy[:, 0:w]
        don = dyh * (zg * sgz)
        heads = range(HGRN_HEADS)
        hcols = [slice(hd * HGRN_DK, (hd + 1) * HGRN_DK) for hd in heads]
        norms = [_rms(o_ref[:, hcols[hd]], ghv[:, hcols[hd]]) for hd in heads]
        on = jnp.concatenate([norms[hd][0] for hd in heads], axis=1)
        oh = jnp.concatenate([norms[hd][1] for hd in heads], axis=1)
        dz_ref[:, 3 * w:4 * w] = (dyh * on * (sgz * (1.0 + zg * (1.0 - sgz)))).astype(BF16)
        dgh_ref[...] += jnp.sum(don * oh, axis=0, keepdims=True)
        do_buf[...] = jnp.concatenate(
            [_rms_bwd(don[:, hcols[hd]], norms[hd][1], norms[hd][2], ghv[:, hcols[hd]]) for hd in heads],
            axis=1).astype(BF16)

        zb = z[:, 4 * w:5 * w]
        zc = z[:, 5 * w:6 * w]
        zu = z[:, 6 * w:7 * w]
        u = zc * zu
        cw = cw_ref[...]
        zp = zprev_ref[...]
        uprev = jnp.where(pl.program_id(0) == n - 1, 0.0, zp[:, 5 * w:6 * w] * zp[:, 6 * w:7 * w])
        dyc = dy[:, w:2 * w]
        dz_ref[:, 4 * w:5 * w] = (dyc * _short_conv(u, uprev, cw)).astype(BF16)
        dconv = dyc * zb
        edge = dcarry[...]
        dconv1 = _shift_rows(dconv, -1, edge)
        dconv2 = _shift_rows(dconv, -2, edge)
        dcarry[...] = dconv[0:8, :]
        du = cw[2:3, :] * dconv + cw[1:2, :] * dconv1 + cw[0:1, :] * dconv2
        dz_ref[:, 5 * w:6 * w] = (du * zu).astype(BF16)
        dz_ref[:, 6 * w:7 * w] = (du * zc).astype(BF16)
        dcw_ref[...] += jnp.concatenate([
            jnp.sum(u * dconv2, axis=0, keepdims=True),
            jnp.sum(u * dconv1, axis=0, keepdims=True),
            jnp.sum(u * dconv, axis=0, keepdims=True)], axis=0)

        mask = _block_causal_mask(tm)
        chunk_of_row = lax.broadcasted_iota(jnp.int32, (tm, 1), 0) // CHUNK
        heads = range(HGRN_HEADS)
        hcols = [slice(hd * HGRN_DK, (hd + 1) * HGRN_DK) for hd in heads]
        qhb = [qh_buf[:, hcols[hd]] for hd in heads]
        khb = [kh_buf[:, hcols[hd]] for hd in heads]
        vb = [v_buf[:, hcols[hd]] for hd in heads]
        dob = [do_buf[:, hcols[hd]] for hd in heads]
        scores = [jnp.where(mask, _dot(qhb[hd], khb[hd], NT), 0.0).astype(BF16) for hd in heads]
        dscores = [jnp.where(mask, _dot(dob[hd], vb[hd], NT), 0.0).astype(BF16) for hd in heads]
        gains = [_dot(_spread(dob[hd], chunk_of_row, nc), qhb[hd], TN) for hd in heads]
        dst_rows, dst_lanes, st_lanes, carries = [], [], [], []
        for hd in heads:
            entering = [st_ref[c, hd] for c in range(nc)]
            leaving, carried_back = [None] * nc, [None] * nc
            dst = dstate[hd]
            for c in reversed(range(nc)):
                elast = etot_buf[c * CHUNK:c * CHUNK + 1, hcols[hd]]
                leaving[c] = dst
                carried_back[c] = jnp.sum(dst * entering[c], axis=0, keepdims=True) * elast
                dst = dst * elast + gains[hd][c * HGRN_DK:(c + 1) * HGRN_DK, :]
            dstate[hd] = dst
            dst_rows.append(jnp.concatenate(leaving, axis=0).astype(BF16))
            dst_lanes.append(jnp.concatenate(leaving, axis=1).astype(BF16))
            st_lanes.append(jnp.concatenate(entering, axis=1).astype(BF16))
            carries.append(carried_back)
        dv = [_dot(scores[hd], dob[hd], TN) + _pick(_dot(kbar_buf[:, hcols[hd]], dst_rows[hd], NT), chunk_of_row, nc)
              for hd in heads]
        dz_ref[:, 2 * w:3 * w] = jnp.concatenate(dv, axis=1).astype(BF16)
        dqh = jnp.concatenate(
            [_dot(dscores[hd], khb[hd]) + _pick(_dot(dob[hd], st_lanes[hd]), chunk_of_row, nc) for hd in heads], axis=1)
        dkh = jnp.concatenate([_dot(dscores[hd], qhb[hd], TN) for hd in heads], axis=1)
        dkbar = jnp.concatenate([_pick(_dot(vb[hd], dst_lanes[hd]), chunk_of_row, nc) for hd in heads], axis=1)

        kbar_dkbar = kbar_buf[...].astype(F32) * dkbar
        db = qh_buf[...].astype(F32) * dqh - kh_buf[...].astype(F32) * dkh - kbar_dkbar
        through_last = jnp.concatenate([
            jnp.broadcast_to(
                jnp.sum(kbar_dkbar[c * CHUNK:(c + 1) * CHUNK], axis=0, keepdims=True)
                + jnp.concatenate([carries[hd][c] for hd in heads], axis=1),
                (CHUNK, w))
            for c in range(nc)], axis=0)
        dlogf = _chunk_cumsum(db, reverse=True) + through_last
        df = dlogf / f - (dkh * enb + dkbar * erest)
        zq = z[:, 0:w]
        dz_ref[:, 0:w] = (dqh * eb * HGRN_DK ** -0.5 * (sq * (1.0 + zq * (1.0 - sq)))).astype(BF16)
        dz_ref[:, w:2 * w] = (df * (1.0 - lb) * sig * (1.0 - sig)).astype(BF16)
        dlb = jnp.sum(df * (1.0 - sig), axis=0, keepdims=True) * lb * (1.0 - lb)
        dlbp_ref[...] += jnp.concatenate([dlb, -dlb], axis=0)

        dh = _dot(dz_ref[...], win_ref[...])
        dx_ref[...] = _rms_bwd(dh, xh, r, gv) + dxo
        dg_ref[...] += jnp.sum(dh * xh, axis=0, keepdims=True)

    return _call(
        body,
        name="mix_bwd",
        grid=(n,),
        in_specs=[
            _rows_rev(tm, d, n), _full((1, d)), _rows_rev(tm, d, n), _rows_rev(tm, zw, n),
            pl.BlockSpec((8, zw), lambda i: (jnp.maximum((n - 1 - i) * (tm // 8) - 1, 0), 0)),
            _rows_rev(tm, w, n),
            pl.BlockSpec((nc, HGRN_HEADS, HGRN_DK, HGRN_DK), lambda i: (n - 1 - i, 0, 0, 0)),
            _full((zw, d)), _full((2, w)), _full((1, w)), _full((3, w)), _full((2 * w, d)),
        ],
        out_specs=[
            _rows_rev(tm, d, n), _rows_rev(tm, zw, n), _rows_rev(tm, d, n),
            _full((1, d)), _full((2, w)), _full((1, w)), _full((3, w)),
        ],
        out_shape=[
            jax.ShapeDtypeStruct((t, d), F32),
            jax.ShapeDtypeStruct((t, zw), BF16),
            jax.ShapeDtypeStruct((t, d), BF16),
            jax.ShapeDtypeStruct((1, d), F32),
            jax.ShapeDtypeStruct((2, w), F32),
            jax.ShapeDtypeStruct((1, w), F32),
            jax.ShapeDtypeStruct((3, w), F32),
        ],
        scratch_shapes=[
            pltpu.VMEM((HGRN_HEADS, HGRN_DK, HGRN_DK), F32), pltpu.VMEM((8, w), F32),
            pltpu.VMEM((tm, w), BF16),
            pltpu.VMEM((tm, w), BF16), pltpu.VMEM((tm, w), BF16), pltpu.VMEM((tm, w), BF16),
            pltpu.VMEM((tm, w), BF16), pltpu.VMEM((tm, w), F32),
        ],
        args=(x, g, dxo, z, z, o, states, w_in, lbp, gh, convw_t, w_out),
        exchange=exchange,
    )


def _memkv_fwd(mem, g, wkv):
    m, d = mem.shape
    nb, _, cb = wkv.shape

    def body(mem_ref, g_ref, wkv_ref, kv_ref):
        mn, _, _ = _rms(mem_ref[...], g_ref[...])
        mnb = mn.astype(BF16)
        for j in range(nb):
            kv_ref[:, j * cb:(j + 1) * cb] = _dot(mnb, wkv_ref[j]).astype(BF16)

    return pl.pallas_call(
        body,
        name="memkv_fwd",
        out_shape=jax.ShapeDtypeStruct((m, nb * cb), BF16),
        compiler_params=_params(),
    )(mem, g, wkv)


def _memkv_bwd(mem, g, dkv, wkv):
    m, d = mem.shape
    nb, _, cb = wkv.shape
    chips = nb // 2

    def body(mem_ref, g_ref, dkv_ref, wkv_ref, dw_ref, dg_ref, dw_all, send_buf, recv_buf, send_sem, recv_sem):
        x, y, c, _ = _mesh_place()
        sibling, _ = _peer(x, y, c, 1)
        mn, xh, _ = _rms(mem_ref[...], g_ref[...])
        mnb = mn.astype(BF16)
        dmn = jnp.zeros((m, d), F32)
        for j in range(nb):
            dkvb = dkv_ref[:, j * cb:(j + 1) * cb].astype(BF16)
            dw_all[j] = _dot(mnb, dkvb, TN)
            dmn = dmn + _dot(dkvb, wkv_ref[j], NT)
        dg_ref[...] = jnp.sum(dmn * xh, axis=0, keepdims=True)
        for q in range(chips):
            send_buf[q] = dw_all[2 * q + 1 - c].astype(BF16)
        to_sibling = _remote(send_buf, recv_buf, send_sem, recv_sem, sibling)
        to_sibling.start()
        to_sibling.wait_send()
        to_sibling.wait_recv()
        for q in range(chips):
            dw_ref[q] = (dw_all[2 * q + c] + recv_buf[q].astype(F32)).astype(BF16)

    return pl.pallas_call(
        body,
        name="memkv_bwd",
        out_shape=[jax.ShapeDtypeStruct((chips, d, cb), BF16), jax.ShapeDtypeStruct((1, d), F32)],
        scratch_shapes=[
            pltpu.VMEM((nb, d, cb), F32), pltpu.VMEM((chips, d, cb), BF16), pltpu.VMEM((chips, d, cb), BF16),
            pltpu.SemaphoreType.DMA, pltpu.SemaphoreType.DMA,
        ],
        compiler_params=_params(),
    )(mem, g, dkv, wkv)


def _softmax_rows(qm_h, k_h):
    sc = _dot(qm_h, k_h, NT) * MEM_HD ** -0.5
    e = jnp.exp(sc - jnp.max(sc, axis=-1, keepdims=True))
    return e / jnp.sum(e, axis=-1, keepdims=True)


def _xattn_fwd(x, g, wq, kv, wo, exchange=None):
    t, d = x.shape
    m = kv.shape[0]
    tm = min(XATTN_TILE, t)

    def body(x_ref, g_ref, wq_ref, kv_ref, wo_ref, xo_ref, hq_ref, qm_ref, att_ref):
        xv = x_ref[...]
        h, _, _ = _rms(xv, g_ref[...])
        hb = h.astype(BF16)
        hq_ref[...] = hb
        qm = _dot(hb, wq_ref[...]).astype(BF16)
        qm_ref[...] = qm
        heads = range(MEM_HEADS)
        kcols = [slice(hd * MEM_HD, (hd + 1) * MEM_HD) for hd in heads]
        p = [_softmax_rows(qm[:, kcols[hd]], kv_ref[:, kcols[hd]]) for hd in heads]
        att = jnp.concatenate(
            [_dot(p[hd].astype(BF16), kv_ref[:, d + hd * MEM_HD:d + (hd + 1) * MEM_HD]) for hd in heads],
            axis=1).astype(BF16)
        att_ref[...] = att
        xo_ref[...] = xv + _dot(att, wo_ref[...])

    return _call(
        body,
        name="xattn_fwd",
        grid=(t // tm,),
        in_specs=[_rows(tm, d), _full((1, d)), _full((d, d)), _full((m, 2 * d)), _full((d, d))],
        out_specs=[_rows(tm, d), _rows(tm, d), _rows(tm, d), _rows(tm, d)],
        out_shape=[
            jax.ShapeDtypeStruct((t, d), F32),
            jax.ShapeDtypeStruct((t, d), BF16),
            jax.ShapeDtypeStruct((t, d), BF16),
            jax.ShapeDtypeStruct((t, d), BF16),
        ],
        args=(x, g, wq, kv, wo),
        exchange=exchange,
    )


def _xattn_bwd(x, g, dxo, qm, kv, wq, wo, exchange=None):
    t, d = x.shape
    m = kv.shape[0]
    tm = min(XATTN_TILE, t)

    def body(x_ref, g_ref, dxo_ref, qm_ref, kv_ref, wq_ref, wo_ref, dx_ref, dqm_ref, dkv_ref, dg_ref):
        _zero_at_start(dkv_ref, dg_ref)
        gv = g_ref[...]
        _, xh, r = _rms(x_ref[...], gv)
        dxo = dxo_ref[...]
        datt = _dot(dxo.astype(BF16), wo_ref[...], NT).astype(BF16)
        heads = range(MEM_HEADS)
        kcols = [slice(hd * MEM_HD, (hd + 1) * MEM_HD) for hd in heads]
        vcols = [slice(d + hd * MEM_HD, d + (hd + 1) * MEM_HD) for hd in heads]
        qm_h = [qm_ref[:, kcols[hd]] for hd in heads]
        p = [_softmax_rows(qm_h[hd], kv_ref[:, kcols[hd]]) for hd in heads]
        dp = [_dot(datt[:, kcols[hd]], kv_ref[:, vcols[hd]], NT) for hd in heads]
        dsc = [(p[hd] * (dp[hd] - jnp.sum(p[hd] * dp[hd], axis=-1, keepdims=True)) * MEM_HD ** -0.5).astype(BF16)
               for hd in heads]
        dqm = jnp.concatenate([_dot(dsc[hd], kv_ref[:, kcols[hd]]) for hd in heads], axis=1).astype(BF16)
        dqm_ref[...] = dqm
        dkv_ref[...] += jnp.concatenate(
            [_dot(dsc[hd], qm_h[hd], TN) for hd in heads]
            + [_dot(p[hd].astype(BF16), datt[:, kcols[hd]], TN) for hd in heads], axis=1)
        dh = _dot(dqm, wq_ref[...], NT)
        dx_ref[...] = _rms_bwd(dh, xh, r, gv) + dxo
        dg_ref[...] += jnp.sum(dh * xh, axis=0, keepdims=True)

    return _call(
        body,
        name="xattn_bwd",
        grid=(t // tm,),
        in_specs=[
            _rows(tm, d), _full((1, d)), _rows(tm, d), _rows(tm, d), _full((m, 2 * d)), _full((d, d)), _full((d, d)),
        ],
        out_specs=[_rows(tm, d), _rows(tm, d), _full((m, 2 * d)), _full((1, d))],
        out_shape=[
            jax.ShapeDtypeStruct((t, d), F32),
            jax.ShapeDtypeStruct((t, d), BF16),
            jax.ShapeDtypeStruct((m, 2 * d), F32),
            jax.ShapeDtypeStruct((1, d), F32),
        ],
        args=(x, g, dxo, qm, kv, wq, wo),
        exchange=exchange,
    )


def _mesh_place():
    x, y, c = lax.axis_index("x"), lax.axis_index("y"), lax.axis_index("c")
    return x, y, c, 4 * x + 2 * y + c


def _peer(x, y, c, k):
    px = 1 - x if k & 4 else x
    py = 1 - y if k & 2 else y
    pc = 1 - c if k & 1 else c
    return (px, py, pc), 4 * px + 2 * py + pc


ICI_HOPS = (2, 4, 6)
N_HOPS = len(ICI_HOPS)


def _remote(src, dst, send_sem, recv_sem, peer):
    return pltpu.make_async_remote_copy(
        src_ref=src, dst_ref=dst, send_sem=send_sem, recv_sem=recv_sem, device_id=peer, device_id_type=MESH_IDS)


def _gather_exchange(shards, middle_eighths=MIDDLE_EIGHTHS):
    n = len(shards)

    def place():
        x, y, c, me = _mesh_place()
        sibling, _ = _peer(x, y, c, 1)
        to_x, from_x = _peer(x, y, c, 4)
        to_y, from_y = _peer(x, y, c, 2)
        _, from_diagonal = _peer(x, y, c, 6)
        onward = (c * to_y[0] + (1 - c) * to_x[0], c * to_y[1] + (1 - c) * to_x[1], c)
        passed_on = c * from_x + (1 - c) * from_y
        return me, sibling, (to_x, to_y, onward), (from_x, from_y, from_diagonal), passed_on

    def start(src, dst, sems):
        ici_send, ici_recv, pair_send, pair_recv, local = sems
        me, sibling, targets, _, _ = place()
        for a in range(n):
            pltpu.make_async_copy(src[a], dst[a].at[me], local.at[a]).start()
            for j in range(2):
                _remote(src[a], dst[a].at[me], ici_send.at[a, j], ici_recv.at[a, j], targets[j]).start()
            _remote(src[a], dst[a].at[me], pair_send.at[a, 0], pair_recv.at[a, 0], sibling).start()

    def to_sibling(dst, sems, a, j, origin, sibling):
        _, _, pair_send, pair_recv, _ = sems
        slot = dst[a].at[origin]
        return _remote(slot, slot, pair_send.at[a, 1 + j], pair_recv.at[a, 1 + j], sibling)

    def middle(src, dst, sems):
        ici_send, ici_recv, _, _, _ = sems
        _, sibling, targets, origins, passed_on = place()
        for a in range(n):
            for j in range(2):
                _remote(src[a], dst[a].at[origins[j]], ici_send.at[a, j], ici_recv.at[a, j], targets[j]).wait_recv()
            slot = dst[a].at[passed_on]
            _remote(slot, slot, ici_send.at[a, 2], ici_recv.at[a, 2], targets[2]).start()
            for j in range(2):
                to_sibling(dst, sems, a, j, origins[j], sibling).start()

    def finish(src, dst, sems):
        ici_send, ici_recv, pair_send, pair_recv, local = sems
        me, sibling, targets, origins, _ = place()
        for a in range(n):
            _remote(src[a], dst[a].at[origins[2]], ici_send.at[a, 2], ici_recv.at[a, 2], targets[2]).wait_recv()
            to_sibling(dst, sems, a, 2, origins[2], sibling).start()
        for a in range(n):
            pltpu.make_async_copy(src[a], dst[a].at[me], local.at[a]).wait()
            for j in range(N_HOPS):
                _remote(src[a], dst[a].at[me], ici_send.at[a, j], ici_recv.at[a, j], targets[j]).wait_send()
            for j, origin in enumerate((me,) + origins):
                from_sibling = origin + 1 - 2 * (origin % 2)
                passed = _remote(src[a], dst[a].at[from_sibling], pair_send.at[a, j], pair_recv.at[a, j], sibling)
                passed.wait_send()
                passed.wait_recv()

    return _Exchange(
        shards,
        [jax.ShapeDtypeStruct((N_DEV,) + s.shape, s.dtype) for s in shards],
        [
            pltpu.SemaphoreType.DMA((n, N_HOPS)), pltpu.SemaphoreType.DMA((n, N_HOPS)),
            pltpu.SemaphoreType.DMA((n, N_HOPS + 1)), pltpu.SemaphoreType.DMA((n, N_HOPS + 1)),
            pltpu.SemaphoreType.DMA((n,)),
        ],
        start, finish, middle, middle_eighths)


def _scatter_copies(src, dst, sems, n, arrivals=False):
    send, recv, local = sems
    x, y, c, _ = _mesh_place()
    chip = 2 * x + y
    if arrivals is None:
        return [pltpu.make_async_copy(src[a].at[chip], dst[a].at[chip], local.at[a]) for a in range(n)]
    copies = []
    for a in range(n):
        for j, k in enumerate(ICI_HOPS):
            peer, _ = _peer(x, y, c, k)
            peer_chip = 2 * peer[0] + peer[1]
            slot = dst[a].at[peer_chip if arrivals else chip]
            copies.append(_remote(src[a].at[peer_chip], slot, send.at[a, j], recv.at[a, j], peer))
    return copies


def _scatter_start(src, dst, sems, n):
    for cp in _scatter_copies(src, dst, sems, n, arrivals=None) + _scatter_copies(src, dst, sems, n):
        cp.start()


def _scatter_finish(src, dst, sems, n):
    for cp in _scatter_copies(src, dst, sems, n, arrivals=None):
        cp.wait()
    for cp in _scatter_copies(src, dst, sems, n):
        cp.wait_send()
    for cp in _scatter_copies(src, dst, sems, n, arrivals=True):
        cp.wait_recv()


def _scatter_scratch(n):
    return [pltpu.SemaphoreType.DMA((n, N_HOPS)), pltpu.SemaphoreType.DMA((n, N_HOPS)), pltpu.SemaphoreType.DMA((n,))]


def _scatter_exchange(partials):
    n = len(partials)
    return _Exchange(
        partials, [jax.ShapeDtypeStruct(p.shape, p.dtype) for p in partials], _scatter_scratch(n),
        lambda src, dst, sems: _scatter_start(src, dst, sems, n),
        lambda src, dst, sems: _scatter_finish(src, dst, sems, n))


SMALL_LAYOUT = {
    "ffn1_norm": (0, 1, 1024), "mix_norm": (1, 1, 1024), "xattn_norm": (2, 1, 1024), "mem_norm": (3, 1, 1024),
    "ffn2_norm": (4, 1, 1024), "final_norm": (5, 1, 1024), "lb_param": (6, 2, 512), "hgrn_out_norm": (8, 1, 512),
    "conv_w": (9, 3, 512), "loss": (12, 1, 128),
}


def _final_exchange(partials, small):
    n = len(partials)
    names = list(small)
    width = 1024

    def body(*refs):
        src = refs[:n]
        pieces = refs[n:n + len(names)]
        dst = refs[n + len(names):2 * n + len(names)]
        total_ref = refs[2 * n + len(names)]
        pack, gathered, small_send, small_recv = refs[2 * n + len(names) + 1:2 * n + len(names) + 5]
        sems = refs[2 * n + len(names) + 5:]
        x, y, c, me = _mesh_place()
        pack[...] = jnp.zeros_like(pack)
        for name, piece in zip(names, pieces):
            row, nrows, ncols = SMALL_LAYOUT[name]
            pack[row:row + nrows, 0:ncols] = piece[...]
        for k in range(1, N_DEV):
            peer, _ = _peer(x, y, c, k)
            _remote(pack, gathered.at[me], small_send.at[k - 1], small_recv.at[k - 1], peer).start()
        _scatter_start(src, dst, sems, n)
        gathered[me] = pack[...]
        for k in range(1, N_DEV):
            peer, peer_index = _peer(x, y, c, k)
            landed = _remote(pack, gathered.at[peer_index], small_send.at[k - 1], small_recv.at[k - 1], peer)
            landed.wait_send()
            landed.wait_recv()
        total = gathered[0]
        for j in range(1, N_DEV):
            total = total + gathered[j]
        total_ref[...] = total
        _scatter_finish(src, dst, sems, n)

    hbm = pl.BlockSpec(memory_space=pltpu.HBM)
    vmem = pl.BlockSpec(memory_space=pltpu.VMEM)
    out = pl.pallas_call(
        body,
        name="final_exchange",
        in_specs=[hbm] * n + [vmem] * len(names),
        out_specs=[hbm] * n + [vmem],
        out_shape=[jax.ShapeDtypeStruct(p.shape, p.dtype) for p in partials]
        + [jax.ShapeDtypeStruct((SMALL_ROWS, width), F32)],
        scratch_shapes=[
            pltpu.VMEM((SMALL_ROWS, width), F32), pltpu.VMEM((N_DEV, SMALL_ROWS, width), F32),
            pltpu.SemaphoreType.DMA((N_DEV - 1,)), pltpu.SemaphoreType.DMA((N_DEV - 1,)),
        ] + _scatter_scratch(n),
        compiler_params=pltpu.CompilerParams(has_side_effects=True),
    )(*partials, *[small[k] for k in names])
    return out[:n], out[n]


def _adamw_math(w, g, m, v):
    m = ADAM_B1 * m + (1.0 - ADAM_B1) * g
    v = ADAM_B2 * v + (1.0 - ADAM_B2) * (g * g)
    m_hat = m / (1.0 - ADAM_B1 ** ADAM_STEP)
    v_hat = v / (1.0 - ADAM_B2 ** ADAM_STEP)
    delta = -ADAM_LR * (m_hat / (jnp.sqrt(v_hat) + ADAM_EPS) + ADAM_WD * w)
    return delta, m, v


def _adamw_shard(parts, w, m, v):
    r, c = w.shape
    n_parts = parts.shape[0]
    tr = max(rows for rows in range(16, r + 1, 16) if r % rows == 0 and rows * c <= ADAMW_TILE_ELEMENTS)

    def body(p_ref, w_ref, m_ref, v_ref, g_ref, d_ref, mo_ref, vo_ref):
        g = p_ref[0].astype(F32)
        for j in range(1, n_parts):
            g = g + p_ref[j].astype(F32)
        delta, mn, vn = _adamw_math(w_ref[...], g, m_ref[...], v_ref[...])
        g_ref[...] = g
        d_ref[...] = delta
        mo_ref[...] = mn
        vo_ref[...] = vn

    tile = pl.BlockSpec((tr, c), lambda i: (i, 0))
    return pl.pallas_call(
        body,
        name="adamw_shard",
        grid=(r // tr,),
        in_specs=[pl.BlockSpec((n_parts, tr, c), lambda i: (0, i, 0)), tile, tile, tile],
        out_specs=[tile] * 4,
        out_shape=[jax.ShapeDtypeStruct((r, c), F32)] * 4,
        compiler_params=_params(("parallel",)),
    )(parts, w, m, v)


def _adamw_small(gs, ws, ms, vs):
    n = len(gs)

    def body(*refs):
        g_refs, w_refs, m_refs, v_refs = refs[:n], refs[n:2 * n], refs[2 * n:3 * n], refs[3 * n:4 * n]
        g_out, d_out, m_out, v_out = refs[4 * n:5 * n], refs[5 * n:6 * n], refs[6 * n:7 * n], refs[7 * n:8 * n]
        for i in range(n):
            if gs[i].ndim == ws[i].ndim:
                g = g_refs[i][...]
            else:
                g = g_refs[i][0].astype(F32)
                for j in range(1, gs[i].shape[0]):
                    g = g + g_refs[i][j].astype(F32)
            delta, mn, vn = _adamw_math(w_refs[i][...], g, m_refs[i][...], v_refs[i][...])
            g_out[i][...] = g
            d_out[i][...] = delta
            m_out[i][...] = mn
            v_out[i][...] = vn

    shapes = [jax.ShapeDtypeStruct(w.shape, F32) for w in ws]
    out = pl.pallas_call(
        body,
        name="adamw_small",
        out_shape=shapes * 4,
        compiler_params=_params(),
    )(*gs, *ws, *ms, *vs)
    return out[:n], out[n:2 * n], out[2 * n:3 * n], out[3 * n:]


TRANSPOSED = ("ffn1_gate", "ffn1_up", "w_in", "ffn2_gate", "ffn2_up", "conv_w")
GROUP_FFN1 = ("ffn1_gate", "ffn1_up", "ffn1_down")
GROUP_MIX = ("w_in", "w_out")
GROUP_XATTN = ("w_q_mem", "w_kv_mem", "w_o_mem")
GROUP_FFN2 = ("ffn2_gate", "ffn2_up", "ffn2_down")
LARGE = GROUP_FFN1 + GROUP_MIX + GROUP_XATTN + GROUP_FFN2
SHORT_SHARDS = ("w_out", "w_q_mem", "w_kv_mem", "w_o_mem")
SMALL = ("ffn1_norm", "mix_norm", "lb_param", "hgrn_out_norm", "conv_w", "xattn_norm", "mem_norm", "ffn2_norm",
         "final_norm")
WEIGHTS = ("ffn1_norm", "ffn1_gate", "ffn1_up", "ffn1_down", "mix_norm", "w_in", "lb_param", "hgrn_out_norm",
           "conv_w", "w_out", "xattn_norm", "mem_norm", "w_q_mem", "w_kv_mem", "w_o_mem", "ffn2_norm", "ffn2_gate",
           "ffn2_up", "ffn2_down", "final_norm")


def kernel(x, mem, ffn1_norm, ffn1_gate, ffn1_up, ffn1_down, mix_norm, w_in, lb_param, hgrn_out_norm, conv_w, w_out, xattn_norm, mem_norm, w_q_mem, w_kv_mem, w_o_mem, ffn2_norm, ffn2_gate, ffn2_up, ffn2_down, final_norm, loss_target, m_ffn1_norm, m_ffn1_gate, m_ffn1_up, m_ffn1_down, m_mix_norm, m_w_in, m_lb_param, m_hgrn_out_norm, m_conv_w, m_w_out, m_xattn_norm, m_mem_norm, m_w_q_mem, m_w_kv_mem, m_w_o_mem, m_ffn2_norm, m_ffn2_gate, m_ffn2_up, m_ffn2_down, m_final_norm, v_ffn1_norm, v_ffn1_gate, v_ffn1_up, v_ffn1_down, v_mix_norm, v_w_in, v_lb_param, v_hgrn_out_norm, v_conv_w, v_w_out, v_xattn_norm, v_mem_norm, v_w_q_mem, v_w_kv_mem, v_w_o_mem, v_ffn2_norm, v_ffn2_gate, v_ffn2_up, v_ffn2_down, v_final_norm):
    given = dict(locals())
    me = 4 * lax.axis_index("x") + 2 * lax.axis_index("y") + lax.axis_index("c")
    x0, memv, target = x[0], mem[0], loss_target[0]

    def shard(prefix, name):
        v = given[prefix + name]
        if v.ndim == 1:
            return v.reshape(1, -1)
        if v.ndim == 2:
            return v
        return v[0].T if name in TRANSPOSED else v[0]

    w = {name: shard("", name) for name in WEIGHTS}
    m = {name: shard("m_", name) for name in WEIGHTS}
    v = {name: shard("v_", name) for name in WEIGHTS}

    conv_taps, conv_rows = w["conv_w"].shape
    conv_tile = jnp.pad(w["conv_w"], ((0, 8 - conv_taps), (0, 128 - conv_rows)))
    wire = {name: w[name].astype(BF16) for name in LARGE}
    full = {}

    def landed(names, gathered):
        for name, blocks in zip(names, gathered):
            _, r, c = blocks.shape
            full[name] = blocks if name == "w_kv_mem" else blocks.reshape(N_DEV * r, c)

    first = ("ffn1_gate", "ffn1_up")
    landed(first, _run_exchange(_gather_exchange([wire[k] for k in first]), "gather_first"))

    riders = (("ffn1_down", "w_in"), ("w_out", "w_kv_mem"), ("w_q_mem", "w_o_mem", "ffn2_gate", "ffn2_up"),
              ("ffn2_down",))
    (a1, b1, s1), gathered = _ffn_up(
        x0, w["ffn1_norm"], full["ffn1_gate"], full["ffn1_up"],
        exchange=_gather_exchange([wire[k] for k in riders[0]], middle_eighths=LATE_MIDDLE_EIGHTHS))
    landed(riders[0], gathered)
    (x1,), gathered = _ffn_down(
        x0, s1, full["ffn1_down"],
        exchange=_gather_exchange([wire[k] for k in riders[1]] + [conv_tile], middle_eighths=EARLY_MIDDLE_EIGHTHS))
    landed(riders[1], gathered)
    convw_t = gathered[-1][:, :conv_taps, :conv_rows].transpose(1, 0, 2).reshape(conv_taps, N_DEV * conv_rows)
    (x2, z, o_raw, states, ycat), gathered = _mix_fwd(
        x1, w["mix_norm"], full["w_in"], w["lb_param"], w["hgrn_out_norm"], convw_t, full["w_out"],
        exchange=_gather_exchange([wire[k] for k in riders[2]]))
    landed(riders[2], gathered)
    kv = _memkv_fwd(memv, w["mem_norm"], full["w_kv_mem"])
    (x3, hq, qm, att), gathered = _xattn_fwd(
        x2, w["xattn_norm"], full["w_q_mem"], kv, full["w_o_mem"],
        exchange=_gather_exchange([wire[k] for k in riders[3]], middle_eighths=EARLY_MIDDLE_EIGHTHS))
    landed(riders[3], gathered)
    (dx4, a2, b2, s2, loss_part, d_final), _ = _ffn_fwd(
        x3, w["ffn2_norm"], full["ffn2_gate"], full["ffn2_up"], full["ffn2_down"], head=(w["final_norm"], target))

    parts = {}
    waiting = []

    def carried():
        names = [name for name, _ in waiting]
        exchange = _scatter_exchange([p for _, p in waiting]) if waiting else None
        del waiting[:]
        return names, exchange

    def weight_grad(name, a, b, scale=1.0):
        names, exchange = carried()
        partial, arrived = _weight_grad(a, b, scale, exchange=exchange)
        parts.update(zip(names, arrived))
        waiting.append((name, partial))

    (dx3, da2, db2, h4, d_ffn2_norm), _ = _ffn_bwd(
        x3, w["ffn2_norm"], dx4, a2, b2, full["ffn2_gate"], full["ffn2_up"], full["ffn2_down"])
    weight_grad("ffn2_down", s2, dx4, 0.5)
    weight_grad("ffn2_gate", da2, h4)
    weight_grad("ffn2_up", db2, h4)
    names, exchange = carried()
    (dx2, dqm, dkv, d_xattn_norm), arrived = _xattn_bwd(
        x2, w["xattn_norm"], dx3, qm, kv, full["w_q_mem"], full["w_o_mem"], exchange=exchange)
    parts.update(zip(names, arrived))
    d_wkv, d_mem_norm = _memkv_bwd(memv, w["mem_norm"], dkv, full["w_kv_mem"])
    waiting.append(("w_kv_mem", d_wkv))
    names, exchange = carried()
    (dx1, dz, h2, d_mix_norm, d_lbp, d_gh, d_convw_t), arrived = _mix_bwd(
        x1, w["mix_norm"], dx2, z, o_raw, states, full["w_in"], w["lb_param"], w["hgrn_out_norm"], convw_t,
        full["w_out"], exchange=exchange)
    parts.update(zip(names, arrived))
    weight_grad("w_in", dz, h2)
    weight_grad("ffn1_down", s1, dx1, 0.5)
    (dx0, da1, db1, h1, d_ffn1_norm), _ = _ffn_bwd(
        x0, w["ffn1_norm"], dx1, a1, b1, full["ffn1_gate"], full["ffn1_up"], full["ffn1_down"])
    weight_grad("ffn1_gate", da1, h1)
    weight_grad("ffn1_up", db1, h1)
    weight_grad("w_o_mem", att, dx3)
    weight_grad("w_q_mem", hq, dqm)
    weight_grad("w_out", ycat, dx2)

    small_parts = {
        "ffn1_norm": d_ffn1_norm, "mix_norm": d_mix_norm, "xattn_norm": d_xattn_norm, "mem_norm": d_mem_norm,
        "ffn2_norm": d_ffn2_norm, "final_norm": d_final, "lb_param": d_lbp, "hgrn_out_norm": d_gh,
        "conv_w": d_convw_t, "loss": loss_part,
    }
    names = [name for name, _ in waiting]
    arrived, total = _final_exchange([p for _, p in waiting], small_parts)
    parts.update(zip(names, arrived))

    g_out, d_out, m_out, v_out = {}, {}, {}, {}
    for name in LARGE:
        if name not in SHORT_SHARDS:
            g_out[name], d_out[name], m_out[name], v_out[name] = _adamw_shard(parts[name], w[name], m[name], v[name])
    g_small = {name: parts[name] for name in SHORT_SHARDS}
    for name in SMALL:
        row, nrows, ncols = SMALL_LAYOUT[name]
        g_small[name] = total[row:row + nrows, 0:ncols]
    g_small["conv_w"] = lax.dynamic_slice_in_dim(g_small["conv_w"], me * conv_rows, conv_rows, axis=1)
    together = SMALL + SHORT_SHARDS
    gs, ds, ms, vs = _adamw_small(
        [g_small[k] for k in together], [w[k] for k in together], [m[k] for k in together],
        [v[k] for k in together])
    for i, name in enumerate(together):
        g_out[name], d_out[name], m_out[name], v_out[name] = gs[i], ds[i], ms[i], vs[i]

    def shaped(value, name):
        return (value.T if name in TRANSPOSED else value).reshape(given[name].shape)

    loss = total[SMALL_LAYOUT["loss"][0], 0]
    outs = [loss, dx0.reshape(x.shape)]
    for group in (g_out, d_out, m_out, v_out):
        outs += [shaped(group[name], name) for name in WEIGHTS]
    return tuple(outs)
```

```python
import jax
import jax.numpy as jnp
from jax import lax
from jax.experimental import pallas as pl
from jax.experimental.pallas import tpu as pltpu

F32 = jnp.float32
BF16 = jnp.bfloat16
MESH_IDS = pl.DeviceIdType.MESH

N_DEV = 8
EPS = 1e-6
HGRN_HEADS = 4
HGRN_DK = 128
HGRN_W = 512
CHUNK = 64
MEM_HEADS = 4
MEM_HD = 256
ADAM_LR = 0.001
ADAM_B1 = 0.9
ADAM_B2 = 0.999
ADAM_EPS = 1e-08
ADAM_WD = 0.01
ADAM_STEP = 10

TOKEN_TILE = 256
XATTN_TILE = 512
WIDE_TILE = 512
REDUCE_TILE = 1024
ADAMW_TILE_ELEMENTS = 256 * 1024
MIDDLE_EIGHTHS = 5
EARLY_MIDDLE_EIGHTHS = 4
MXU_ROWS = 256
VMEM_LIMIT = 60 * 1024 * 1024
SMALL_ROWS = 16
NT = (((1,), (1,)), ((), ()))
TN = (((0,), (0,)), ((), ()))


def _params(sem=None):
    return pltpu.CompilerParams(dimension_semantics=sem, vmem_limit_bytes=VMEM_LIMIT)


def _dot(a, b, dims=None):
    if dims is None:
        return jnp.dot(a, b, preferred_element_type=F32)
    return lax.dot_general(a, b, dims, preferred_element_type=F32)


def _sigmoid(v):
    return 1.0 / (1.0 + jnp.exp(-v))


def _rms(x, g):
    r = lax.rsqrt(jnp.mean(x * x, axis=-1, keepdims=True) + EPS)
    xh = x * r
    return xh * g, xh, r


def _rms_bwd(dh, xh, r, g):
    dxh = dh * g
    return r * (dxh - xh * jnp.mean(dxh * xh, axis=-1, keepdims=True))


def _full(shape):
    return pl.BlockSpec(shape, lambda *_: (0,) * len(shape))


def _full_once(shape):
    return pl.BlockSpec(shape, lambda *_: (0,) * len(shape), pipeline_mode=pl.Buffered(1))


def _rows(tm, width):
    return pl.BlockSpec((tm, width), lambda i: (i, 0))


def _rows_rev(tm, width, n):
    return pl.BlockSpec((tm, width), lambda i: (n - 1 - i, 0))


def _zero_at_start(*refs):
    @pl.when(pl.program_id(0) == 0)
    def _():
        for ref in refs:
            ref[...] = jnp.zeros_like(ref)


class _Exchange:
    def __init__(self, operands, out_shapes, scratch, start, finish, middle=None, middle_eighths=MIDDLE_EIGHTHS):
        self.operands, self.out_shapes, self.scratch = list(operands), list(out_shapes), list(scratch)
        self.start, self.middle, self.finish, self.middle_eighths = start, middle, finish, middle_eighths


def _call(body, *, name, grid, in_specs, out_specs, out_shape, args, scratch_shapes=(), exchange=None):
    semantics = ("arbitrary",) * len(grid)
    if exchange is None:
        out = pl.pallas_call(
            body, name=name, grid=grid, in_specs=in_specs, out_specs=out_specs, out_shape=out_shape,
            scratch_shapes=list(scratch_shapes), compiler_params=_params(semantics))(*args)
        return out, []
    hbm = pl.BlockSpec(memory_space=pltpu.HBM)
    n_in, n_out, n_scr = len(in_specs), len(out_specs), len(scratch_shapes)
    e_in, e_out = len(exchange.operands), len(exchange.out_shapes)

    def carried(*refs):
        ins, rest = refs[:n_in], refs[n_in:]
        e_ins, rest = rest[:e_in], rest[e_in:]
        outs, rest = rest[:n_out], rest[n_out:]
        e_outs, rest = rest[:e_out], rest[e_out:]
        scr, e_scr = rest[:n_scr], rest[n_scr:]
        first = last = None
        for axis, size in enumerate(grid):
            at_start, at_end = pl.program_id(axis) == 0, pl.program_id(axis) == size - 1
            first = at_start if first is None else jnp.logical_and(first, at_start)
            last = at_end if last is None else jnp.logical_and(last, at_end)

        @pl.when(first)
        def _():
            exchange.start(e_ins, e_outs, e_scr)

        body(*ins, *outs, *scr)

        if exchange.middle is not None:
            assert len(grid) == 1

            @pl.when(pl.program_id(0) == (grid[0] * exchange.middle_eighths) // 8)
            def _():
                exchange.middle(e_ins, e_outs, e_scr)

        @pl.when(last)
        def _():
            exchange.finish(e_ins, e_outs, e_scr)

    out = pl.pallas_call(
        carried, name=name, grid=grid, in_specs=list(in_specs) + [hbm] * e_in,
        out_specs=list(out_specs) + [hbm] * e_out, out_shape=list(out_shape) + exchange.out_shapes,
        scratch_shapes=list(scratch_shapes) + exchange.scratch,
        compiler_params=pltpu.CompilerParams(
            dimension_semantics=semantics, vmem_limit_bytes=VMEM_LIMIT, has_side_effects=True),
    )(*args, *exchange.operands)
    return out[:n_out], out[n_out:]


def _run_exchange(exchange, name):
    hbm = pl.BlockSpec(memory_space=pltpu.HBM)
    e_in, e_out = len(exchange.operands), len(exchange.out_shapes)

    def body(*refs):
        e_ins, e_outs, e_scr = refs[:e_in], refs[e_in:e_in + e_out], refs[e_in + e_out:]
        exchange.start(e_ins, e_outs, e_scr)
        if exchange.middle is not None:
            exchange.middle(e_ins, e_outs, e_scr)
        exchange.finish(e_ins, e_outs, e_scr)

    return pl.pallas_call(
        body, name=name, in_specs=[hbm] * e_in, out_specs=[hbm] * e_out, out_shape=exchange.out_shapes,
        scratch_shapes=exchange.scratch, compiler_params=pltpu.CompilerParams(has_side_effects=True),
    )(*exchange.operands)


def _loss_head(xo, gf, tgt):
    d = xo.shape[1]
    y, xh, r = _rms(xo, gf)
    err = y - tgt
    dy = err * (1.0 / d)
    loss = 0.5 * jnp.sum(jnp.sum(err * err, axis=-1, keepdims=True) * (1.0 / d), axis=0, keepdims=True)
    return _rms_bwd(dy, xh, r, gf), loss, jnp.sum(dy * xh, axis=0, keepdims=True)


def _ffn_fwd(x, g, wg, wu, wd, exchange=None, head=None):
    t, d = x.shape
    f = wg.shape[0]
    tm = min(WIDE_TILE, t)

    def body(x_ref, g_ref, wg_ref, wu_ref, wd_ref, *rest):
        if head is None:
            xo_ref, a_ref, b_ref, s_ref = rest
        else:
            gf_ref, tgt_ref, xo_ref, a_ref, b_ref, s_ref, loss_ref, dgf_ref = rest
            _zero_at_start(loss_ref, dgf_ref)
        xv = x_ref[...]
        h, _, _ = _rms(xv, g_ref[...])
        hb = h.astype(BF16)
        a = _dot(hb, wg_ref[...], NT)
        b = _dot(hb, wu_ref[...], NT)
        s = (a * _sigmoid(a) * b).astype(BF16)
        xo = xv + 0.5 * _dot(s, wd_ref[...])
        if head is None:
            xo_ref[...] = xo
        else:
            xo_ref[...], loss, dgf = _loss_head(xo, gf_ref[...], tgt_ref[...])
            loss_ref[...] += jnp.broadcast_to(loss, (1, 128))
            dgf_ref[...] += dgf
        a_ref[...] = a.astype(BF16)
        b_ref[...] = b.astype(BF16)
        s_ref[...] = s

    in_specs = [_rows(tm, d), _full((1, d)), _full_once((f, d)), _full_once((f, d)), _full_once((f, d))]
    out_specs = [_rows(tm, d), _rows(tm, f), _rows(tm, f), _rows(tm, f)]
    out_shape = [
        jax.ShapeDtypeStruct((t, d), F32),
        jax.ShapeDtypeStruct((t, f), BF16),
        jax.ShapeDtypeStruct((t, f), BF16),
        jax.ShapeDtypeStruct((t, f), BF16),
    ]
    args = (x, g, wg, wu, wd)
    if head is not None:
        in_specs += [_full((1, d)), _rows(tm, d)]
        out_specs += [_full((1, 128)), _full((1, d))]
        out_shape += [jax.ShapeDtypeStruct((1, 128), F32), jax.ShapeDtypeStruct((1, d), F32)]
        args += tuple(head)
    return _call(
        body, name="ffn_fwd", grid=(t // tm,), in_specs=in_specs, out_specs=out_specs, out_shape=out_shape,
        args=args, exchange=exchange)


def _ffn_up(x, g, wg, wu, exchange=None):
    t, d = x.shape
    f = wg.shape[0]
    tm = min(TOKEN_TILE, t)

    def body(x_ref, g_ref, wg_ref, wu_ref, a_ref, b_ref, s_ref):
        h, _, _ = _rms(x_ref[...], g_ref[...])
        hb = h.astype(BF16)
        a = _dot(hb, wg_ref[...], NT)
        b = _dot(hb, wu_ref[...], NT)
        a_ref[...] = a.astype(BF16)
        b_ref[...] = b.astype(BF16)
        s_ref[...] = (a * _sigmoid(a) * b).astype(BF16)

    return _call(
        body, name="ffn_up", grid=(t // tm,),
        in_specs=[_rows(tm, d), _full((1, d)), _full_once((f, d)), _full_once((f, d))],
        out_specs=[_rows(tm, f)] * 3, out_shape=[jax.ShapeDtypeStruct((t, f), BF16)] * 3,
        args=(x, g, wg, wu), exchange=exchange)


def _ffn_down(x, s, wd, exchange=None):
    t, d = x.shape
    f = wd.shape[0]
    tm = min(TOKEN_TILE, t)

    def body(x_ref, s_ref, wd_ref, xo_ref):
        xo_ref[...] = x_ref[...] + 0.5 * _dot(s_ref[...], wd_ref[...])

    return _call(
        body, name="ffn_down", grid=(t // tm,),
        in_specs=[_rows(tm, d), _rows(tm, f), _full_once((f, d))],
        out_specs=[_rows(tm, d)], out_shape=[jax.ShapeDtypeStruct((t, d), F32)],
        args=(x, s, wd), exchange=exchange)


def _ffn_bwd(x, g, dxo, a, b, wg, wu, wd, exchange=None):
    t, d = x.shape
    f = wg.shape[0]
    tm = min(TOKEN_TILE, t)

    def body(x_ref, g_ref, dxo_ref, a_ref, b_ref, wg_ref, wu_ref, wd_ref, dx_ref, da_ref, db_ref, h_ref, dg_ref):
        _zero_at_start(dg_ref)
        gv = g_ref[...]
        h, xh, r = _rms(x_ref[...], gv)
        dxo = dxo_ref[...]
        ds = _dot((0.5 * dxo).astype(BF16), wd_ref[...], NT)
        af = a_ref[...].astype(F32)
        bf = b_ref[...].astype(F32)
        sg = _sigmoid(af)
        da = (ds * bf * (sg * (1.0 + af * (1.0 - sg)))).astype(BF16)
        db = (ds * (af * sg)).astype(BF16)
        dh = _dot(da, wg_ref[...]) + _dot(db, wu_ref[...])
        dx_ref[...] = _rms_bwd(dh, xh, r, gv) + dxo
        da_ref[...] = da
        db_ref[...] = db
        h_ref[...] = h.astype(BF16)
        dg_ref[...] += jnp.sum(dh * xh, axis=0, keepdims=True)

    return _call(
        body,
        name="ffn_bwd",
        grid=(t // tm,),
        in_specs=[
            _rows(tm, d), _full((1, d)), _rows(tm, d), _rows(tm, f), _rows(tm, f),
            _full_once((f, d)), _full_once((f, d)), _full_once((f, d)),
        ],
        out_specs=[_rows(tm, d), _rows(tm, f), _rows(tm, f), _rows(tm, d), _full((1, d))],
        out_shape=[
            jax.ShapeDtypeStruct((t, d), F32),
            jax.ShapeDtypeStruct((t, f), BF16),
            jax.ShapeDtypeStruct((t, f), BF16),
            jax.ShapeDtypeStruct((t, d), BF16),
            jax.ShapeDtypeStruct((1, d), F32),
        ],
        args=(x, g, dxo, a, b, wg, wu, wd),
        exchange=exchange,
    )


def _weight_grad(a, b, scale=1.0, exchange=None):
    t, m = a.shape
    n = b.shape[1]
    chips = N_DEV // 2
    r = m // N_DEV
    tk = min(REDUCE_TILE, t)
    halves = 2
    nb = n // halves
    nk = t // tk

    def body(a_ref, b_ref, o_ref, acc, send_buf, recv_buf, send_sems, recv_sems):
        k, j = pl.program_id(0), pl.program_id(1)
        x, y, c, _ = _mesh_place()
        sibling, _ = _peer(x, y, c, 1)
        bv = b_ref[...]
        if scale != 1.0:
            bv = bv * scale
        bb = bv.astype(BF16)
        acc_half = acc.at[j]

        @pl.when(k == 0)
        def _():
            acc_half[...] = jnp.zeros_like(acc_half)

        for i in range(m // MXU_ROWS):
            rows = slice(i * MXU_ROWS, (i + 1) * MXU_ROWS)
            acc_half[rows, :] += _dot(a_ref[:, rows].astype(BF16), bb, TN)

        def to_sibling(half):
            return _remote(send_buf.at[half], recv_buf.at[half], send_sems.at[half], recv_sems.at[half], sibling)

        def owned_rows(q, core):
            return pl.ds(pl.multiple_of((2 * q + core) * r, 8), r)

        for half in range(halves):
            @pl.when(jnp.logical_and(k == nk - 1, j == half))
            def _():
                for q in range(chips):
                    send_buf[half, q] = acc[half, owned_rows(q, 1 - c), :].astype(BF16)
                to_sibling(half).start()

        @pl.when(jnp.logical_and(k == nk - 1, j == halves - 1))
        def _():
            for half in range(halves):
                to_sibling(half).wait_send()
                to_sibling(half).wait_recv()
                for q in range(chips):
                    o_ref[q, :, half * nb:(half + 1) * nb] = (
                        acc[half, owned_rows(q, c), :] + recv_buf[half, q].astype(F32)).astype(BF16)

    (partial,), arrived = _call(
        body,
        name="weight_grad",
        grid=(nk, halves),
        in_specs=[pl.BlockSpec((tk, m), lambda k, j: (k, 0)), pl.BlockSpec((tk, nb), lambda k, j: (k, j))],
        out_specs=[pl.BlockSpec((chips, r, n), lambda k, j: (0, 0, 0))],
        out_shape=[jax.ShapeDtypeStruct((chips, r, n), BF16)],
        scratch_shapes=[
            pltpu.VMEM((halves, m, nb), F32),
            pltpu.VMEM((halves, chips, r, nb), BF16), pltpu.VMEM((halves, chips, r, nb), BF16),
            pltpu.SemaphoreType.DMA((halves,)), pltpu.SemaphoreType.DMA((halves,)),
        ],
        args=(a, b),
        exchange=exchange,
    )
    return partial, arrived


def _chunk_cumsum(v, reverse=False):
    n, width = v.shape
    row = lax.broadcasted_iota(jnp.int32, (n, n), 0)
    col = lax.broadcasted_iota(jnp.int32, (n, n), 1)
    earlier = col >= row if reverse else col <= row
    tri = jnp.where(jnp.logical_and(row // CHUNK == col // CHUNK, earlier), 1.0, 0.0).astype(BF16)
    hi = v.astype(BF16)
    rest = v - hi.astype(F32)
    mid = rest.astype(BF16)
    low = (rest - mid.astype(F32)).astype(BF16)
    sums = _dot(tri, jnp.concatenate([hi, mid, low], axis=1))
    return sums[:, 0:width] + sums[:, width:2 * width] + sums[:, 2 * width:3 * width]


def _shift_rows(v, shift, edge):
    n = v.shape[0]
    row = lax.broadcasted_iota(jnp.int32, (n, 1), 0)
    out = pltpu.roll(v, shift % n, axis=0)
    if shift > 0:
        for j in range(shift):
            out = jnp.where(row == j, edge[8 - shift + j:8 - shift + j + 1, :], out)
    else:
        for j in range(-shift):
            out = jnp.where(row == n + shift + j, edge[j:j + 1, :], out)
    return out


def _gates(z, lbp):
    w = HGRN_W
    lb = _sigmoid(lbp[0:1, :] - lbp[1:2, :])
    zq = z[:, 0:w]
    sig = _sigmoid(z[:, w:2 * w])
    f = lb + (1.0 - lb) * sig
    sq = _sigmoid(zq)
    q = zq * sq * HGRN_DK ** -0.5
    return lb, sig, f, sq, q


def _decayed_operands(q, f, v, qh_buf, qm_buf, km_buf, kbar_buf, v_buf, etot_buf):
    n, width = f.shape
    bcum = _chunk_cumsum(jnp.log(f))

    def row_of_chunk(offset):
        return jnp.concatenate(
            [jnp.broadcast_to(bcum[c + offset:c + offset + 1, :], (CHUNK, width)) for c in range(0, n, CHUNK)], axis=0)

    total, mid = row_of_chunk(CHUNK - 1), row_of_chunk(CHUNK // 2 - 1)
    eb, em, enm, erest = jnp.exp(bcum), jnp.exp(bcum - mid), jnp.exp(mid - bcum), jnp.exp(total - bcum)
    kk = 1.0 - f
    qh_buf[...] = (q * eb).astype(BF16)
    qm_buf[...] = (q * em).astype(BF16)
    km_buf[...] = (kk * enm).astype(BF16)
    kbar_buf[...] = (kk * erest).astype(BF16)
    v_buf[...] = v.astype(BF16)
    etot_buf[...] = jnp.exp(total)
    return eb, em, enm, erest


def _short_conv(u, edge, cw):
    return cw[0:1, :] * _shift_rows(u, 2, edge) + cw[1:2, :] * _shift_rows(u, 1, edge) + cw[2:3, :] * u


def _block_causal_mask(n):
    row = lax.broadcasted_iota(jnp.int32, (n, n), 0)
    col = lax.broadcasted_iota(jnp.int32, (n, n), 1)
    return jnp.logical_and(row // CHUNK == col // CHUNK, col <= row)


def _spread(v, chunk_of_row, nc):
    return jnp.concatenate([jnp.where(chunk_of_row == c, v, jnp.zeros_like(v)) for c in range(nc)], axis=1)


def _pick(r, chunk_of_row, nc):
    out = jnp.where(chunk_of_row == 0, r[:, 0:HGRN_DK], 0.0)
    for c in range(1, nc):
        out = out + jnp.where(chunk_of_row == c, r[:, c * HGRN_DK:(c + 1) * HGRN_DK], 0.0)
    return out


def _mix_fwd(x, g, w_in, lbp, gh, convw_t, w_out, exchange=None):
    t, d = x.shape
    zw = w_in.shape[0]
    w = HGRN_W
    tm = min(TOKEN_TILE, t)
    nc = tm // CHUNK
    n_chunks = t // CHUNK

    def body(x_ref, g_ref, win_ref, lbp_ref, gh_ref, cw_ref, wout_ref,
             xo_ref, z_ref, o_ref, st_ref, y_ref, state, ucarry, qh_buf, qm_buf, km_buf, kbar_buf, v_buf, etot_buf):
        _zero_at_start(state, ucarry)
        xv = x_ref[...]
        h, _, _ = _rms(xv, g_ref[...])
        z_ref[...] = _dot(h.astype(BF16), win_ref[...], NT)
        z = z_ref[...]
        _, _, f, _, q = _gates(z, lbp_ref[...])
        _decayed_operands(q, f, z[:, 2 * w:3 * w], qh_buf, qm_buf, km_buf, kbar_buf, v_buf, etot_buf)
        mask = _block_causal_mask(tm)
        chunk_of_row = lax.broadcasted_iota(jnp.int32, (tm, 1), 0) // CHUNK
        heads = range(HGRN_HEADS)
        hcols = [slice(hd * HGRN_DK, (hd + 1) * HGRN_DK) for hd in heads]
        qh = [qh_buf[:, hcols[hd]] for hd in heads]
        vb = [v_buf[:, hcols[hd]] for hd in heads]
        scores = [jnp.where(mask, _dot(qm_buf[:, hcols[hd]], km_buf[:, hcols[hd]], NT), 0.0).astype(BF16)
                  for hd in heads]
        gains = [_dot(_spread(vb[hd], chunk_of_row, nc), kbar_buf[:, hcols[hd]], TN) for hd in heads]
        entering = []
        for hd in heads:
            states, st = [], state[hd]
            for c in range(nc):
                states.append(st)
                st_ref[c, hd] = st
                st = st * etot_buf[c * CHUNK:c * CHUNK + 1, hcols[hd]] + gains[hd][c * HGRN_DK:(c + 1) * HGRN_DK, :]
            state[hd] = st
            entering.append(jnp.concatenate(states, axis=0).astype(BF16))
        from_states = [_dot(qh[hd], entering[hd], NT) for hd in heads]
        o_heads = [_dot(scores[hd], vb[hd]) + _pick(from_states[hd], chunk_of_row, nc) for hd in heads]
        o_ref[...] = jnp.concatenate(o_heads, axis=1)
        ghv = gh_ref[...]
        normed = jnp.concatenate([_rms(o_heads[hd], ghv[:, hcols[hd]])[0] for hd in heads], axis=1)
        zg = z[:, 3 * w:4 * w]
        u = z[:, 5 * w:6 * w] * z[:, 6 * w:7 * w]
        conv = _short_conv(u, ucarry[...], cw_ref[...])
        ucarry[...] = u[tm - 8:tm, :]
        y = jnp.concatenate([normed * (zg * _sigmoid(zg)), z[:, 4 * w:5 * w] * conv], axis=1).astype(BF16)
        y_ref[...] = y
        xo_ref[...] = xv + _dot(y, wout_ref[...])

    return _call(
        body,
        name="mix_fwd",
        grid=(t // tm,),
        in_specs=[
            _rows(tm, d), _full((1, d)), _full((zw, d)), _full((2, w)), _full((1, w)), _full((3, w)),
            _full((2 * w, d)),
        ],
        out_specs=[
            _rows(tm, d), _rows(tm, zw), _rows(tm, w),
            pl.BlockSpec((nc, HGRN_HEADS, HGRN_DK, HGRN_DK), lambda i: (i, 0, 0, 0)),
            _rows(tm, 2 * w),
        ],
        out_shape=[
            jax.ShapeDtypeStruct((t, d), F32),
            jax.ShapeDtypeStruct((t, zw), F32),
            jax.ShapeDtypeStruct((t, w), F32),
            jax.ShapeDtypeStruct((n_chunks, HGRN_HEADS, HGRN_DK, HGRN_DK), F32),
            jax.ShapeDtypeStruct((t, 2 * w), BF16),
        ],
        scratch_shapes=[
            pltpu.VMEM((HGRN_HEADS, HGRN_DK, HGRN_DK), F32), pltpu.VMEM((8, w), F32),
            pltpu.VMEM((tm, w), BF16), pltpu.VMEM((tm, w), BF16), pltpu.VMEM((tm, w), BF16),
            pltpu.VMEM((tm, w), BF16), pltpu.VMEM((tm, w), BF16), pltpu.VMEM((tm, w), F32),
        ],
        args=(x, g, w_in, lbp, gh, convw_t, w_out),
        exchange=exchange,
    )


def _mix_bwd(x, g, dxo, z, o, states, w_in, lbp, gh, convw_t, w_out, exchange=None):
    t, d = x.shape
    zw = w_in.shape[0]
    w = HGRN_W
    tm = min(TOKEN_TILE, t)
    nc = tm // CHUNK
    n = t // tm

    def body(x_ref, g_ref, dxo_ref, z_ref, zprev_ref, o_ref, st_ref, win_ref, lbp_ref, gh_ref, cw_ref, wout_ref,
             dx_ref, dz_ref, h_ref, dg_ref, dlbp_ref, dgh_ref, dcw_ref,
             dstate, dcarry, do_buf, qh_buf, qm_buf, km_buf, kbar_buf, v_buf, etot_buf):
        _zero_at_start(dstate, dcarry, dg_ref, dlbp_ref, dgh_ref, dcw_ref)
        gv = g_ref[...]
        h, xh, r = _rms(x_ref[...], gv)
        h_ref[...] = h.astype(BF16)
        dxo = dxo_ref[...]
        dy = _dot(dxo.astype(BF16), wout_ref[...], NT)
        z = z_ref[...]
        lb, sig, f, sq, q = _gates(z, lbp_ref[...])
        eb, em, enm, erest = _decayed_operands(
            q, f, z[:, 2 * w:3 * w], qh_buf, qm_buf, km_buf, kbar_buf, v_buf, etot_buf)

        ghv = gh_ref[...]
        zg = z[:, 3 * w:4 * w]
        sgz = _sigmoid(zg)
        dyh = dy[:, 0:w]
        don = dyh * (zg * sgz)
        heads = range(HGRN_HEADS)
        hcols = [slice(hd * HGRN_DK, (hd + 1) * HGRN_DK) for hd in heads]
        norms = [_rms(o_ref[:, hcols[hd]], ghv[:, hcols[hd]]) for hd in heads]
        on = jnp.concatenate([norms[hd][0] for hd in heads], axis=1)
        oh = jnp.concatenate([norms[hd][1] for hd in heads], axis=1)
        dz_ref[:, 3 * w:4 * w] = (dyh * on * (sgz * (1.0 + zg * (1.0 - sgz)))).astype(BF16)
        dgh_ref[...] += jnp.sum(don * oh, axis=0, keepdims=True)
        do_buf[...] = jnp.concatenate(
            [_rms_bwd(don[:, hcols[hd]], norms[hd][1], norms[hd][2], ghv[:, hcols[hd]]) for hd in heads],
            axis=1).astype(BF16)

        zb = z[:, 4 * w:5 * w]
        zc = z[:, 5 * w:6 * w]
        zu = z[:, 6 * w:7 * w]
        u = zc * zu
        cw = cw_ref[...]
        zp = zprev_ref[...]
        uprev = jnp.where(pl.program_id(0) == n - 1, 0.0, zp[:, 5 * w:6 * w] * zp[:, 6 * w:7 * w])
        dyc = dy[:, w:2 * w]
        dz_ref[:, 4 * w:5 * w] = (dyc * _short_conv(u, uprev, cw)).astype(BF16)
        dconv = dyc * zb
        edge = dcarry[...]
        dconv1 = _shift_rows(dconv, -1, edge)
        dconv2 = _shift_rows(dconv, -2, edge)
        dcarry[...] = dconv[0:8, :]
        du = cw[2:3, :] * dconv + cw[1:2, :] * dconv1 + cw[0:1, :] * dconv2
        dz_ref[:, 5 * w:6 * w] = (du * zu).astype(BF16)
        dz_ref[:, 6 * w:7 * w] = (du * zc).astype(BF16)
        dcw_ref[...] += jnp.concatenate([
            jnp.sum(u * dconv2, axis=0, keepdims=True),
            jnp.sum(u * dconv1, axis=0, keepdims=True),
            jnp.sum(u * dconv, axis=0, keepdims=True)], axis=0)

        mask = _block_causal_mask(tm)
        chunk_of_row = lax.broadcasted_iota(jnp.int32, (tm, 1), 0) // CHUNK
        heads = range(HGRN_HEADS)
        hcols = [slice(hd * HGRN_DK, (hd + 1) * HGRN_DK) for hd in heads]
        qhb = [qh_buf[:, hcols[hd]] for hd in heads]
        qmb = [qm_buf[:, hcols[hd]] for hd in heads]
        kmb = [km_buf[:, hcols[hd]] for hd in heads]
        vb = [v_buf[:, hcols[hd]] for hd in heads]
        dob = [do_buf[:, hcols[hd]] for hd in heads]
        scores = [jnp.where(mask, _dot(qmb[hd], kmb[hd], NT), 0.0).astype(BF16) for hd in heads]
        dscores = [jnp.where(mask, _dot(dob[hd], vb[hd], NT), 0.0).astype(BF16) for hd in heads]
        gains = [_dot(_spread(dob[hd], chunk_of_row, nc), qhb[hd], TN) for hd in heads]
        dst_rows, dst_lanes, st_lanes, carries = [], [], [], []
        for hd in heads:
            entering = [st_ref[c, hd] for c in range(nc)]
            leaving, carried_back = [None] * nc, [None] * nc
            dst = dstate[hd]
            for c in reversed(range(nc)):
                elast = etot_buf[c * CHUNK:c * CHUNK + 1, hcols[hd]]
                leaving[c] = dst
                carried_back[c] = jnp.sum(dst * entering[c], axis=0, keepdims=True) * elast
                dst = dst * elast + gains[hd][c * HGRN_DK:(c + 1) * HGRN_DK, :]
            dstate[hd] = dst
            dst_rows.append(jnp.concatenate(leaving, axis=0).astype(BF16))
            dst_lanes.append(jnp.concatenate(leaving, axis=1).astype(BF16))
            st_lanes.append(jnp.concatenate(entering, axis=1).astype(BF16))
            carries.append(carried_back)
        dv = [_dot(scores[hd], dob[hd], TN) + _pick(_dot(kbar_buf[:, hcols[hd]], dst_rows[hd], NT), chunk_of_row, nc)
              for hd in heads]
        dz_ref[:, 2 * w:3 * w] = jnp.concatenate(dv, axis=1).astype(BF16)
        dqm = jnp.concatenate([_dot(dscores[hd], kmb[hd]) for hd in heads], axis=1)
        dqh = jnp.concatenate([_pick(_dot(dob[hd], st_lanes[hd]), chunk_of_row, nc) for hd in heads], axis=1)
        dkm = jnp.concatenate([_dot(dscores[hd], qmb[hd], TN) for hd in heads], axis=1)
        dkbar = jnp.concatenate([_pick(_dot(vb[hd], dst_lanes[hd]), chunk_of_row, nc) for hd in heads], axis=1)

        kbar_dkbar = kbar_buf[...].astype(F32) * dkbar
        db = (qm_buf[...].astype(F32) * dqm - km_buf[...].astype(F32) * dkm
              + qh_buf[...].astype(F32) * dqh - kbar_dkbar)
        through_last = jnp.concatenate([
            jnp.broadcast_to(
                jnp.sum(kbar_dkbar[c * CHUNK:(c + 1) * CHUNK], axis=0, keepdims=True)
                + jnp.concatenate([carries[hd][c] for hd in heads], axis=1),
                (CHUNK, w))
            for c in range(nc)], axis=0)
        dlogf = _chunk_cumsum(db, reverse=True) + through_last
        df = dlogf / f - (dkm * enm + dkbar * erest)
        zq = z[:, 0:w]
        dz_ref[:, 0:w] = ((dqm * em + dqh * eb) * HGRN_DK ** -0.5 * (sq * (1.0 + zq * (1.0 - sq)))).astype(BF16)
        dz_ref[:, w:2 * w] = (df * (1.0 - lb) * sig * (1.0 - sig)).astype(BF16)
        dlb = jnp.sum(df * (1.0 - sig), axis=0, keepdims=True) * lb * (1.0 - lb)
        dlbp_ref[...] += jnp.concatenate([dlb, -dlb], axis=0)

        dh = _dot(dz_ref[...], win_ref[...])
        dx_ref[...] = _rms_bwd(dh, xh, r, gv) + dxo
        dg_ref[...] += jnp.sum(dh * xh, axis=0, keepdims=True)

    return _call(
        body,
        name="mix_bwd",
        grid=(n,),
        in_specs=[
            _rows_rev(tm, d, n), _full((1, d)), _rows_rev(tm, d, n), _rows_rev(tm, zw, n),
            pl.BlockSpec((8, zw), lambda i: (jnp.maximum((n - 1 - i) * (tm // 8) - 1, 0), 0)),
            _rows_rev(tm, w, n),
            pl.BlockSpec((nc, HGRN_HEADS, HGRN_DK, HGRN_DK), lambda i: (n - 1 - i, 0, 0, 0)),
            _full((zw, d)), _full((2, w)), _full((1, w)), _full((3, w)), _full((2 * w, d)),
        ],
        out_specs=[
            _rows_rev(tm, d, n), _rows_rev(tm, zw, n), _rows_rev(tm, d, n),
            _full((1, d)), _full((2, w)), _full((1, w)), _full((3, w)),
        ],
        out_shape=[
            jax.ShapeDtypeStruct((t, d), F32),
            jax.ShapeDtypeStruct((t, zw), BF16),
            jax.ShapeDtypeStruct((t, d), BF16),
            jax.ShapeDtypeStruct((1, d), F32),
            jax.ShapeDtypeStruct((2, w), F32),
            jax.ShapeDtypeStruct((1, w), F32),
            jax.ShapeDtypeStruct((3, w), F32),
        ],
        scratch_shapes=[
            pltpu.VMEM((HGRN_HEADS, HGRN_DK, HGRN_DK), F32), pltpu.VMEM((8, w), F32),
            pltpu.VMEM((tm, w), BF16),
            pltpu.VMEM((tm, w), BF16), pltpu.VMEM((tm, w), BF16), pltpu.VMEM((tm, w), BF16),
            pltpu.VMEM((tm, w), BF16), pltpu.VMEM((tm, w), BF16), pltpu.VMEM((tm, w), F32),
        ],
        args=(x, g, dxo, z, z, o, states, w_in, lbp, gh, convw_t, w_out),
        exchange=exchange,
    )


def _memkv_fwd(mem, g, wkv):
    m, d = mem.shape
    nb, _, cb = wkv.shape

    def body(mem_ref, g_ref, wkv_ref, kv_ref):
        mn, _, _ = _rms(mem_ref[...], g_ref[...])
        mnb = mn.astype(BF16)
        for j in range(nb):
            kv_ref[:, j * cb:(j + 1) * cb] = _dot(mnb, wkv_ref[j]).astype(BF16)

    return pl.pallas_call(
        body,
        name="memkv_fwd",
        out_shape=jax.ShapeDtypeStruct((m, nb * cb), BF16),
        compiler_params=_params(),
    )(mem, g, wkv)


def _memkv_bwd(mem, g, dkv, wkv):
    m, d = mem.shape
    nb, _, cb = wkv.shape
    chips = nb // 2

    def body(mem_ref, g_ref, dkv_ref, wkv_ref, dw_ref, dg_ref, dw_all, send_buf, recv_buf, send_sem, recv_sem):
        x, y, c, _ = _mesh_place()
        sibling, _ = _peer(x, y, c, 1)
        mn, xh, _ = _rms(mem_ref[...], g_ref[...])
        mnb = mn.astype(BF16)
        dmn = jnp.zeros((m, d), F32)
        for j in range(nb):
            dkvb = dkv_ref[:, j * cb:(j + 1) * cb].astype(BF16)
            dw_all[j] = _dot(mnb, dkvb, TN)
            dmn = dmn + _dot(dkvb, wkv_ref[j], NT)
        dg_ref[...] = jnp.sum(dmn * xh, axis=0, keepdims=True)
        for q in range(chips):
            send_buf[q] = dw_all[2 * q + 1 - c].astype(BF16)
        to_sibling = _remote(send_buf, recv_buf, send_sem, recv_sem, sibling)
        to_sibling.start()
        to_sibling.wait_send()
        to_sibling.wait_recv()
        for q in range(chips):
            dw_ref[q] = (dw_all[2 * q + c] + recv_buf[q].astype(F32)).astype(BF16)

    return pl.pallas_call(
        body,
        name="memkv_bwd",
        out_shape=[jax.ShapeDtypeStruct((chips, d, cb), BF16), jax.ShapeDtypeStruct((1, d), F32)],
        scratch_shapes=[
            pltpu.VMEM((nb, d, cb), F32), pltpu.VMEM((chips, d, cb), BF16), pltpu.VMEM((chips, d, cb), BF16),
            pltpu.SemaphoreType.DMA, pltpu.SemaphoreType.DMA,
        ],
        compiler_params=_params(),
    )(mem, g, dkv, wkv)


def _softmax_rows(qm_h, k_h):
    sc = _dot(qm_h, k_h, NT) * MEM_HD ** -0.5
    e = jnp.exp(sc - jnp.max(sc, axis=-1, keepdims=True))
    return e / jnp.sum(e, axis=-1, keepdims=True)


def _xattn_fwd(x, g, wq, kv, wo, exchange=None):
    t, d = x.shape
    m = kv.shape[0]
    tm = min(XATTN_TILE, t)

    def body(x_ref, g_ref, wq_ref, kv_ref, wo_ref, xo_ref, hq_ref, qm_ref, att_ref):
        xv = x_ref[...]
        h, _, _ = _rms(xv, g_ref[...])
        hb = h.astype(BF16)
        hq_ref[...] = hb
        qm = _dot(hb, wq_ref[...]).astype(BF16)
        qm_ref[...] = qm
        heads = range(MEM_HEADS)
        kcols = [slice(hd * MEM_HD, (hd + 1) * MEM_HD) for hd in heads]
        p = [_softmax_rows(qm[:, kcols[hd]], kv_ref[:, kcols[hd]]) for hd in heads]
        att = jnp.concatenate(
            [_dot(p[hd].astype(BF16), kv_ref[:, d + hd * MEM_HD:d + (hd + 1) * MEM_HD]) for hd in heads],
            axis=1).astype(BF16)
        att_ref[...] = att
        xo_ref[...] = xv + _dot(att, wo_ref[...])

    return _call(
        body,
        name="xattn_fwd",
        grid=(t // tm,),
        in_specs=[_rows(tm, d), _full((1, d)), _full((d, d)), _full((m, 2 * d)), _full((d, d))],
        out_specs=[_rows(tm, d), _rows(tm, d), _rows(tm, d), _rows(tm, d)],
        out_shape=[
            jax.ShapeDtypeStruct((t, d), F32),
            jax.ShapeDtypeStruct((t, d), BF16),
            jax.ShapeDtypeStruct((t, d), BF16),
            jax.ShapeDtypeStruct((t, d), BF16),
        ],
        args=(x, g, wq, kv, wo),
        exchange=exchange,
    )


def _xattn_bwd(x, g, dxo, qm, kv, wq, wo, exchange=None):
    t, d = x.shape
    m = kv.shape[0]
    tm = min(XATTN_TILE, t)

    def body(x_ref, g_ref, dxo_ref, qm_ref, kv_ref, wq_ref, wo_ref, dx_ref, dqm_ref, dkv_ref, dg_ref):
        _zero_at_start(dkv_ref, dg_ref)
        gv = g_ref[...]
        _, xh, r = _rms(x_ref[...], gv)
        dxo = dxo_ref[...]
        datt = _dot(dxo.astype(BF16), wo_ref[...], NT).astype(BF16)
        heads = range(MEM_HEADS)
        kcols = [slice(hd * MEM_HD, (hd + 1) * MEM_HD) for hd in heads]
        vcols = [slice(d + hd * MEM_HD, d + (hd + 1) * MEM_HD) for hd in heads]
        qm_h = [qm_ref[:, kcols[hd]] for hd in heads]
        p = [_softmax_rows(qm_h[hd], kv_ref[:, kcols[hd]]) for hd in heads]
        dp = [_dot(datt[:, kcols[hd]], kv_ref[:, vcols[hd]], NT) for hd in heads]
        dsc = [(p[hd] * (dp[hd] - jnp.sum(p[hd] * dp[hd], axis=-1, keepdims=True)) * MEM_HD ** -0.5).astype(BF16)
               for hd in heads]
        dqm = jnp.concatenate([_dot(dsc[hd], kv_ref[:, kcols[hd]]) for hd in heads], axis=1).astype(BF16)
        dqm_ref[...] = dqm
        dkv_ref[...] += jnp.concatenate(
            [_dot(dsc[hd], qm_h[hd], TN) for hd in heads]
            + [_dot(p[hd].astype(BF16), datt[:, kcols[hd]], TN) for hd in heads], axis=1)
        dh = _dot(dqm, wq_ref[...], NT)
        dx_ref[...] = _rms_bwd(dh, xh, r, gv) + dxo
        dg_ref[...] += jnp.sum(dh * xh, axis=0, keepdims=True)

    return _call(
        body,
        name="xattn_bwd",
        grid=(t // tm,),
        in_specs=[
            _rows(tm, d), _full((1, d)), _rows(tm, d), _rows(tm, d), _full((m, 2 * d)), _full((d, d)), _full((d, d)),
        ],
        out_specs=[_rows(tm, d), _rows(tm, d), _full((m, 2 * d)), _full((1, d))],
        out_shape=[
            jax.ShapeDtypeStruct((t, d), F32),
            jax.ShapeDtypeStruct((t, d), BF16),
            jax.ShapeDtypeStruct((m, 2 * d), F32),
            jax.ShapeDtypeStruct((1, d), F32),
        ],
        args=(x, g, dxo, qm, kv, wq, wo),
        exchange=exchange,
    )


def _mesh_place():
    x, y, c = lax.axis_index("x"), lax.axis_index("y"), lax.axis_index("c")
    return x, y, c, 4 * x + 2 * y + c


def _peer(x, y, c, k):
    px = 1 - x if k & 4 else x
    py = 1 - y if k & 2 else y
    pc = 1 - c if k & 1 else c
    return (px, py, pc), 4 * px + 2 * py + pc


ICI_HOPS = (2, 4, 6)
N_HOPS = len(ICI_HOPS)


def _remote(src, dst, send_sem, recv_sem, peer):
    return pltpu.make_async_remote_copy(
        src_ref=src, dst_ref=dst, send_sem=send_sem, recv_sem=recv_sem, device_id=peer, device_id_type=MESH_IDS)


def _gather_exchange(shards, middle_eighths=MIDDLE_EIGHTHS):
    n = len(shards)

    def place():
        x, y, c, me = _mesh_place()
        sibling, _ = _peer(x, y, c, 1)
        to_x, from_x = _peer(x, y, c, 4)
        to_y, from_y = _peer(x, y, c, 2)
        _, from_diagonal = _peer(x, y, c, 6)
        onward = (c * to_y[0] + (1 - c) * to_x[0], c * to_y[1] + (1 - c) * to_x[1], c)
        passed_on = c * from_x + (1 - c) * from_y
        return me, sibling, (to_x, to_y, onward), (from_x, from_y, from_diagonal), passed_on

    def start(src, dst, sems):
        ici_send, ici_recv, pair_send, pair_recv, local = sems
        me, sibling, targets, _, _ = place()
        for a in range(n):
            pltpu.make_async_copy(src[a], dst[a].at[me], local.at[a]).start()
            for j in range(2):
                _remote(src[a], dst[a].at[me], ici_send.at[a, j], ici_recv.at[a, j], targets[j]).start()
            _remote(src[a], dst[a].at[me], pair_send.at[a, 0], pair_recv.at[a, 0], sibling).start()

    def to_sibling(dst, sems, a, j, origin, sibling):
        _, _, pair_send, pair_recv, _ = sems
        slot = dst[a].at[origin]
        return _remote(slot, slot, pair_send.at[a, 1 + j], pair_recv.at[a, 1 + j], sibling)

    def middle(src, dst, sems):
        ici_send, ici_recv, _, _, _ = sems
        _, sibling, targets, origins, passed_on = place()
        for a in range(n):
            for j in range(2):
                _remote(src[a], dst[a].at[origins[j]], ici_send.at[a, j], ici_recv.at[a, j], targets[j]).wait_recv()
            slot = dst[a].at[passed_on]
            _remote(slot, slot, ici_send.at[a, 2], ici_recv.at[a, 2], targets[2]).start()
            for j in range(2):
                to_sibling(dst, sems, a, j, origins[j], sibling).start()

    def finish(src, dst, sems):
        ici_send, ici_recv, pair_send, pair_recv, local = sems
        me, sibling, targets, origins, _ = place()
        for a in range(n):
            _remote(src[a], dst[a].at[origins[2]], ici_send.at[a, 2], ici_recv.at[a, 2], targets[2]).wait_recv()
            to_sibling(dst, sems, a, 2, origins[2], sibling).start()
        for a in range(n):
            pltpu.make_async_copy(src[a], dst[a].at[me], local.at[a]).wait()
            for j in range(N_HOPS):
                _remote(src[a], dst[a].at[me], ici_send.at[a, j], ici_recv.at[a, j], targets[j]).wait_send()
            for j, origin in enumerate((me,) + origins):
                from_sibling = origin + 1 - 2 * (origin % 2)
                passed = _remote(src[a], dst[a].at[from_sibling], pair_send.at[a, j], pair_recv.at[a, j], sibling)
                passed.wait_send()
                passed.wait_recv()

    return _Exchange(
        shards,
        [jax.ShapeDtypeStruct((N_DEV,) + s.shape, s.dtype) for s in shards],
        [
            pltpu.SemaphoreType.DMA((n, N_HOPS)), pltpu.SemaphoreType.DMA((n, N_HOPS)),
            pltpu.SemaphoreType.DMA((n, N_HOPS + 1)), pltpu.SemaphoreType.DMA((n, N_HOPS + 1)),
            pltpu.SemaphoreType.DMA((n,)),
        ],
        start, finish, middle, middle_eighths)


def _scatter_copies(src, dst, sems, n, arrivals=False):
    send, recv, local = sems
    x, y, c, _ = _mesh_place()
    chip = 2 * x + y
    if arrivals is None:
        return [pltpu.make_async_copy(src[a].at[chip], dst[a].at[chip], local.at[a]) for a in range(n)]
    copies = []
    for a in range(n):
        for j, k in enumerate(ICI_HOPS):
            peer, _ = _peer(x, y, c, k)
            peer_chip = 2 * peer[0] + peer[1]
            slot = dst[a].at[peer_chip if arrivals else chip]
            copies.append(_remote(src[a].at[peer_chip], slot, send.at[a, j], recv.at[a, j], peer))
    return copies


def _scatter_start(src, dst, sems, n):
    for cp in _scatter_copies(src, dst, sems, n, arrivals=None) + _scatter_copies(src, dst, sems, n):
        cp.start()


def _scatter_finish(src, dst, sems, n):
    for cp in _scatter_copies(src, dst, sems, n, arrivals=None):
        cp.wait()
    for cp in _scatter_copies(src, dst, sems, n):
        cp.wait_send()
    for cp in _scatter_copies(src, dst, sems, n, arrivals=True):
        cp.wait_recv()


def _scatter_scratch(n):
    return [pltpu.SemaphoreType.DMA((n, N_HOPS)), pltpu.SemaphoreType.DMA((n, N_HOPS)), pltpu.SemaphoreType.DMA((n,))]


def _scatter_exchange(partials):
    n = len(partials)
    return _Exchange(
        partials, [jax.ShapeDtypeStruct(p.shape, p.dtype) for p in partials], _scatter_scratch(n),
        lambda src, dst, sems: _scatter_start(src, dst, sems, n),
        lambda src, dst, sems: _scatter_finish(src, dst, sems, n))


SMALL_LAYOUT = {
    "ffn1_norm": (0, 1, 1024), "mix_norm": (1, 1, 1024), "xattn_norm": (2, 1, 1024), "mem_norm": (3, 1, 1024),
    "ffn2_norm": (4, 1, 1024), "final_norm": (5, 1, 1024), "lb_param": (6, 2, 512), "hgrn_out_norm": (8, 1, 512),
    "conv_w": (9, 3, 512), "loss": (12, 1, 128),
}


def _final_exchange(partials, small):
    n = len(partials)
    names = list(small)
    width = 1024

    def body(*refs):
        src = refs[:n]
        pieces = refs[n:n + len(names)]
        dst = refs[n + len(names):2 * n + len(names)]
        total_ref = refs[2 * n + len(names)]
        pack, gathered, small_send, small_recv = refs[2 * n + len(names) + 1:2 * n + len(names) + 5]
        sems = refs[2 * n + len(names) + 5:]
        x, y, c, me = _mesh_place()
        pack[...] = jnp.zeros_like(pack)
        for name, piece in zip(names, pieces):
            row, nrows, ncols = SMALL_LAYOUT[name]
            pack[row:row + nrows, 0:ncols] = piece[...]
        for k in range(1, N_DEV):
            peer, _ = _peer(x, y, c, k)
            _remote(pack, gathered.at[me], small_send.at[k - 1], small_recv.at[k - 1], peer).start()
        _scatter_start(src, dst, sems, n)
        gathered[me] = pack[...]
        for k in range(1, N_DEV):
            peer, peer_index = _peer(x, y, c, k)
            landed = _remote(pack, gathered.at[peer_index], small_send.at[k - 1], small_recv.at[k - 1], peer)
            landed.wait_send()
            landed.wait_recv()
        total = gathered[0]
        for j in range(1, N_DEV):
            total = total + gathered[j]
        total_ref[...] = total
        _scatter_finish(src, dst, sems, n)

    hbm = pl.BlockSpec(memory_space=pltpu.HBM)
    vmem = pl.BlockSpec(memory_space=pltpu.VMEM)
    out = pl.pallas_call(
        body,
        name="final_exchange",
        in_specs=[hbm] * n + [vmem] * len(names),
        out_specs=[hbm] * n + [vmem],
        out_shape=[jax.ShapeDtypeStruct(p.shape, p.dtype) for p in partials]
        + [jax.ShapeDtypeStruct((SMALL_ROWS, width), F32)],
        scratch_shapes=[
            pltpu.VMEM((SMALL_ROWS, width), F32), pltpu.VMEM((N_DEV, SMALL_ROWS, width), F32),
            pltpu.SemaphoreType.DMA((N_DEV - 1,)), pltpu.SemaphoreType.DMA((N_DEV - 1,)),
        ] + _scatter_scratch(n),
        compiler_params=pltpu.CompilerParams(has_side_effects=True),
    )(*partials, *[small[k] for k in names])
    return out[:n], out[n]


def _adamw_math(w, g, m, v):
    m = ADAM_B1 * m + (1.0 - ADAM_B1) * g
    v = ADAM_B2 * v + (1.0 - ADAM_B2) * (g * g)
    m_hat = m / (1.0 - ADAM_B1 ** ADAM_STEP)
    v_hat = v / (1.0 - ADAM_B2 ** ADAM_STEP)
    delta = -ADAM_LR * (m_hat / (jnp.sqrt(v_hat) + ADAM_EPS) + ADAM_WD * w)
    return delta, m, v


def _adamw_shard(parts, w, m, v):
    r, c = w.shape
    n_parts = parts.shape[0]
    tr = max(rows for rows in range(16, r + 1, 16) if r % rows == 0 and rows * c <= ADAMW_TILE_ELEMENTS)

    def body(p_ref, w_ref, m_ref, v_ref, g_ref, d_ref, mo_ref, vo_ref):
        g = p_ref[0].astype(F32)
        for j in range(1, n_parts):
            g = g + p_ref[j].astype(F32)
        delta, mn, vn = _adamw_math(w_ref[...], g, m_ref[...], v_ref[...])
        g_ref[...] = g
        d_ref[...] = delta
        mo_ref[...] = mn
        vo_ref[...] = vn

    tile = pl.BlockSpec((tr, c), lambda i: (i, 0))
    return pl.pallas_call(
        body,
        name="adamw_shard",
        grid=(r // tr,),
        in_specs=[pl.BlockSpec((n_parts, tr, c), lambda i: (0, i, 0)), tile, tile, tile],
        out_specs=[tile] * 4,
        out_shape=[jax.ShapeDtypeStruct((r, c), F32)] * 4,
        compiler_params=_params(("parallel",)),
    )(parts, w, m, v)


def _adamw_small(gs, ws, ms, vs):
    n = len(gs)

    def body(*refs):
        g_refs, w_refs, m_refs, v_refs = refs[:n], refs[n:2 * n], refs[2 * n:3 * n], refs[3 * n:4 * n]
        g_out, d_out, m_out, v_out = refs[4 * n:5 * n], refs[5 * n:6 * n], refs[6 * n:7 * n], refs[7 * n:8 * n]
        for i in range(n):
            if gs[i].ndim == ws[i].ndim:
                g = g_refs[i][...]
            else:
                g = g_refs[i][0].astype(F32)
                for j in range(1, gs[i].shape[0]):
                    g = g + g_refs[i][j].astype(F32)
            delta, mn, vn = _adamw_math(w_refs[i][...], g, m_refs[i][...], v_refs[i][...])
            g_out[i][...] = g
            d_out[i][...] = delta
            m_out[i][...] = mn
            v_out[i][...] = vn

    shapes = [jax.ShapeDtypeStruct(w.shape, F32) for w in ws]
    out = pl.pallas_call(
        body,
        name="adamw_small",
        out_shape=shapes * 4,
        compiler_params=_params(),
    )(*gs, *ws, *ms, *vs)
    return out[:n], out[n:2 * n], out[2 * n:3 * n], out[3 * n:]


TRANSPOSED = ("ffn1_gate", "ffn1_up", "w_in", "ffn2_gate", "ffn2_up", "conv_w")
GROUP_FFN1 = ("ffn1_gate", "ffn1_up", "ffn1_down")
GROUP_MIX = ("w_in", "w_out")
GROUP_XATTN = ("w_q_mem", "w_kv_mem", "w_o_mem")
GROUP_FFN2 = ("ffn2_gate", "ffn2_up", "ffn2_down")
LARGE = GROUP_FFN1 + GROUP_MIX + GROUP_XATTN + GROUP_FFN2
SHORT_SHARDS = ("w_out", "w_q_mem", "w_kv_mem", "w_o_mem")
SMALL = ("ffn1_norm", "mix_norm", "lb_param", "hgrn_out_norm", "conv_w", "xattn_norm", "mem_norm", "ffn2_norm",
         "final_norm")
WEIGHTS = ("ffn1_norm", "ffn1_gate", "ffn1_up", "ffn1_down", "mix_norm", "w_in", "lb_param", "hgrn_out_norm",
           "conv_w", "w_out", "xattn_norm", "mem_norm", "w_q_mem", "w_kv_mem", "w_o_mem", "ffn2_norm", "ffn2_gate",
           "ffn2_up", "ffn2_down", "final_norm")


def kernel(x, mem, ffn1_norm, ffn1_gate, ffn1_up, ffn1_down, mix_norm, w_in, lb_param, hgrn_out_norm, conv_w, w_out, xattn_norm, mem_norm, w_q_mem, w_kv_mem, w_o_mem, ffn2_norm, ffn2_gate, ffn2_up, ffn2_down, final_norm, loss_target, m_ffn1_norm, m_ffn1_gate, m_ffn1_up, m_ffn1_down, m_mix_norm, m_w_in, m_lb_param, m_hgrn_out_norm, m_conv_w, m_w_out, m_xattn_norm, m_mem_norm, m_w_q_mem, m_w_kv_mem, m_w_o_mem, m_ffn2_norm, m_ffn2_gate, m_ffn2_up, m_ffn2_down, m_final_norm, v_ffn1_norm, v_ffn1_gate, v_ffn1_up, v_ffn1_down, v_mix_norm, v_w_in, v_lb_param, v_hgrn_out_norm, v_conv_w, v_w_out, v_xattn_norm, v_mem_norm, v_w_q_mem, v_w_kv_mem, v_w_o_mem, v_ffn2_norm, v_ffn2_gate, v_ffn2_up, v_ffn2_down, v_final_norm):
    given = dict(locals())
    me = 4 * lax.axis_index("x") + 2 * lax.axis_index("y") + lax.axis_index("c")
    x0, memv, target = x[0], mem[0], loss_target[0]

    def shard(prefix, name):
        v = given[prefix + name]
        if v.ndim == 1:
            return v.reshape(1, -1)
        if v.ndim == 2:
            return v
        return v[0].T if name in TRANSPOSED else v[0]

    w = {name: shard("", name) for name in WEIGHTS}
    m = {name: shard("m_", name) for name in WEIGHTS}
    v = {name: shard("v_", name) for name in WEIGHTS}

    conv_taps, conv_rows = w["conv_w"].shape
    conv_tile = jnp.pad(w["conv_w"], ((0, 8 - conv_taps), (0, 128 - conv_rows)))
    wire = {name: w[name].astype(BF16) for name in LARGE}
    full = {}

    def landed(names, gathered):
        for name, blocks in zip(names, gathered):
            _, r, c = blocks.shape
            full[name] = blocks if name == "w_kv_mem" else blocks.reshape(N_DEV * r, c)

    first = ("ffn1_gate", "ffn1_up")
    landed(first, _run_exchange(_gather_exchange([wire[k] for k in first]), "gather_first"))

    riders = (("ffn1_down", "w_in"), ("w_out", "w_kv_mem"), ("w_q_mem", "w_o_mem", "ffn2_gate", "ffn2_up"),
              ("ffn2_down",))
    (a1, b1, s1), gathered = _ffn_up(
        x0, w["ffn1_norm"], full["ffn1_gate"], full["ffn1_up"],
        exchange=_gather_exchange([wire[k] for k in riders[0]]))
    landed(riders[0], gathered)
    (x1,), gathered = _ffn_down(
        x0, s1, full["ffn1_down"], exchange=_gather_exchange([wire[k] for k in riders[1]] + [conv_tile]))
    landed(riders[1], gathered)
    convw_t = gathered[-1][:, :conv_taps, :conv_rows].transpose(1, 0, 2).reshape(conv_taps, N_DEV * conv_rows)
    (x2, z, o_raw, states, ycat), gathered = _mix_fwd(
        x1, w["mix_norm"], full["w_in"], w["lb_param"], w["hgrn_out_norm"], convw_t, full["w_out"],
        exchange=_gather_exchange([wire[k] for k in riders[2]]))
    landed(riders[2], gathered)
    kv = _memkv_fwd(memv, w["mem_norm"], full["w_kv_mem"])
    (x3, hq, qm, att), gathered = _xattn_fwd(
        x2, w["xattn_norm"], full["w_q_mem"], kv, full["w_o_mem"],
        exchange=_gather_exchange([wire[k] for k in riders[3]], middle_eighths=EARLY_MIDDLE_EIGHTHS))
    landed(riders[3], gathered)
    (dx4, a2, b2, s2, loss_part, d_final), _ = _ffn_fwd(
        x3, w["ffn2_norm"], full["ffn2_gate"], full["ffn2_up"], full["ffn2_down"], head=(w["final_norm"], target))

    parts = {}
    waiting = []

    def carried():
        names = [name for name, _ in waiting]
        exchange = _scatter_exchange([p for _, p in waiting]) if waiting else None
        del waiting[:]
        return names, exchange

    def weight_grad(name, a, b, scale=1.0):
        names, exchange = carried()
        partial, arrived = _weight_grad(a, b, scale, exchange=exchange)
        parts.update(zip(names, arrived))
        waiting.append((name, partial))

    (dx3, da2, db2, h4, d_ffn2_norm), _ = _ffn_bwd(
        x3, w["ffn2_norm"], dx4, a2, b2, full["ffn2_gate"], full["ffn2_up"], full["ffn2_down"])
    weight_grad("ffn2_down", s2, dx4, 0.5)
    weight_grad("ffn2_gate", da2, h4)
    weight_grad("ffn2_up", db2, h4)
    names, exchange = carried()
    (dx2, dqm, dkv, d_xattn_norm), arrived = _xattn_bwd(
        x2, w["xattn_norm"], dx3, qm, kv, full["w_q_mem"], full["w_o_mem"], exchange=exchange)
    parts.update(zip(names, arrived))
    d_wkv, d_mem_norm = _memkv_bwd(memv, w["mem_norm"], dkv, full["w_kv_mem"])
    waiting.append(("w_kv_mem", d_wkv))
    names, exchange = carried()
    (dx1, dz, h2, d_mix_norm, d_lbp, d_gh, d_convw_t), arrived = _mix_bwd(
        x1, w["mix_norm"], dx2, z, o_raw, states, full["w_in"], w["lb_param"], w["hgrn_out_norm"], convw_t,
        full["w_out"], exchange=exchange)
    parts.update(zip(names, arrived))
    weight_grad("w_in", dz, h2)
    weight_grad("ffn1_down", s1, dx1, 0.5)
    (dx0, da1, db1, h1, d_ffn1_norm), _ = _ffn_bwd(
        x0, w["ffn1_norm"], dx1, a1, b1, full["ffn1_gate"], full["ffn1_up"], full["ffn1_down"])
    weight_grad("ffn1_gate", da1, h1)
    weight_grad("ffn1_up", db1, h1)
    weight_grad("w_o_mem", att, dx3)
    weight_grad("w_q_mem", hq, dqm)
    weight_grad("w_out", ycat, dx2)

    small_parts = {
        "ffn1_norm": d_ffn1_norm, "mix_norm": d_mix_norm, "xattn_norm": d_xattn_norm, "mem_norm": d_mem_norm,
        "ffn2_norm": d_ffn2_norm, "final_norm": d_final, "lb_param": d_lbp, "hgrn_out_norm": d_gh,
        "conv_w": d_convw_t, "loss": loss_part,
    }
    names = [name for name, _ in waiting]
    arrived, total = _final_exchange([p for _, p in waiting], small_parts)
    parts.update(zip(names, arrived))

    g_out, d_out, m_out, v_out = {}, {}, {}, {}
    for name in LARGE:
        if name not in SHORT_SHARDS:
            g_out[name], d_out[name], m_out[name], v_out[name] = _adamw_shard(parts[name], w[name], m[name], v[name])
    g_small = {name: parts[name] for name in SHORT_SHARDS}
    for name in SMALL:
        row, nrows, ncols = SMALL_LAYOUT[name]
        g_small[name] = total[row:row + nrows, 0:ncols]
    g_small["conv_w"] = lax.dynamic_slice_in_dim(g_small["conv_w"], me * conv_rows, conv_rows, axis=1)
    together = SMALL + SHORT_SHARDS
    gs, ds, ms, vs = _adamw_small(
        [g_small[k] for k in together], [w[k] for k in together], [m[k] for k in together],
        [v[k] for k in together])
    for i, name in enumerate(together):
        g_out[name], d_out[name], m_out[name], v_out[name] = gs[i], ds[i], ms[i], vs[i]

    def shaped(value, name):
        return (value.T if name in TRANSPOSED else value).reshape(given[name].shape)

    loss = total[SMALL_LAYOUT["loss"][0], 0]
    outs = [loss, dx0.reshape(x.shape)]
    for group in (g_out, d_out, m_out, v_out):
        outs += [shaped(group[name], name) for name in WEIGHTS]
    return tuple(outs)
```

```python
import jax
import jax.numpy as jnp
from jax import lax
from jax.experimental import pallas as pl
from jax.experimental.pallas import tpu as pltpu

F32 = jnp.float32
BF16 = jnp.bfloat16
MESH_IDS = pl.DeviceIdType.MESH

N_DEV = 8
EPS = 1e-6
HGRN_HEADS = 4
HGRN_DK = 128
HGRN_W = 512
CHUNK = 64
MEM_HEADS = 4
MEM_HD = 256
ADAM_LR = 0.001
ADAM_B1 = 0.9
ADAM_B2 = 0.999
ADAM_EPS = 1e-08
ADAM_WD = 0.01
ADAM_STEP = 10

TOKEN_TILE = 256
XATTN_TILE = 512
WIDE_TILE = 512
REDUCE_TILE = 1024
ADAMW_TILE_ELEMENTS = 256 * 1024
MIDDLE_EIGHTHS = 5
EARLY_MIDDLE_EIGHTHS = 4
MXU_ROWS = 256
VMEM_LIMIT = 60 * 1024 * 1024
SMALL_ROWS = 16
NT = (((1,), (1,)), ((), ()))
TN = (((0,), (0,)), ((), ()))


def _params(sem=None):
    return pltpu.CompilerParams(dimension_semantics=sem, vmem_limit_bytes=VMEM_LIMIT)


def _dot(a, b, dims=None):
    if dims is None:
        return jnp.dot(a, b, preferred_element_type=F32)
    return lax.dot_general(a, b, dims, preferred_element_type=F32)


def _sigmoid(v):
    return 1.0 / (1.0 + jnp.exp(-v))


def _rms(x, g):
    r = lax.rsqrt(jnp.mean(x * x, axis=-1, keepdims=True) + EPS)
    xh = x * r
    return xh * g, xh, r


def _rms_bwd(dh, xh, r, g):
    dxh = dh * g
    return r * (dxh - xh * jnp.mean(dxh * xh, axis=-1, keepdims=True))


def _full(shape):
    return pl.BlockSpec(shape, lambda *_: (0,) * len(shape))


def _full_once(shape):
    return pl.BlockSpec(shape, lambda *_: (0,) * len(shape), pipeline_mode=pl.Buffered(1))


def _rows(tm, width):
    return pl.BlockSpec((tm, width), lambda i: (i, 0))


def _rows_rev(tm, width, n):
    return pl.BlockSpec((tm, width), lambda i: (n - 1 - i, 0))


def _zero_at_start(*refs):
    @pl.when(pl.program_id(0) == 0)
    def _():
        for ref in refs:
            ref[...] = jnp.zeros_like(ref)


class _Exchange:
    def __init__(self, operands, out_shapes, scratch, start, finish, middle=None, middle_eighths=MIDDLE_EIGHTHS):
        self.operands, self.out_shapes, self.scratch = list(operands), list(out_shapes), list(scratch)
        self.start, self.middle, self.finish, self.middle_eighths = start, middle, finish, middle_eighths


def _call(body, *, name, grid, in_specs, out_specs, out_shape, args, scratch_shapes=(), exchange=None):
    semantics = ("arbitrary",) * len(grid)
    if exchange is None:
        out = pl.pallas_call(
            body, name=name, grid=grid, in_specs=in_specs, out_specs=out_specs, out_shape=out_shape,
            scratch_shapes=list(scratch_shapes), compiler_params=_params(semantics))(*args)
        return out, []
    hbm = pl.BlockSpec(memory_space=pltpu.HBM)
    n_in, n_out, n_scr = len(in_specs), len(out_specs), len(scratch_shapes)
    e_in, e_out = len(exchange.operands), len(exchange.out_shapes)

    def carried(*refs):
        ins, rest = refs[:n_in], refs[n_in:]
        e_ins, rest = rest[:e_in], rest[e_in:]
        outs, rest = rest[:n_out], rest[n_out:]
        e_outs, rest = rest[:e_out], rest[e_out:]
        scr, e_scr = rest[:n_scr], rest[n_scr:]
        first = last = None
        for axis, size in enumerate(grid):
            at_start, at_end = pl.program_id(axis) == 0, pl.program_id(axis) == size - 1
            first = at_start if first is None else jnp.logical_and(first, at_start)
            last = at_end if last is None else jnp.logical_and(last, at_end)

        @pl.when(first)
        def _():
            exchange.start(e_ins, e_outs, e_scr)

        body(*ins, *outs, *scr)

        if exchange.middle is not None:
            assert len(grid) == 1

            @pl.when(pl.program_id(0) == (grid[0] * exchange.middle_eighths) // 8)
            def _():
                exchange.middle(e_ins, e_outs, e_scr)

        @pl.when(last)
        def _():
            exchange.finish(e_ins, e_outs, e_scr)

    out = pl.pallas_call(
        carried, name=name, grid=grid, in_specs=list(in_specs) + [hbm] * e_in,
        out_specs=list(out_specs) + [hbm] * e_out, out_shape=list(out_shape) + exchange.out_shapes,
        scratch_shapes=list(scratch_shapes) + exchange.scratch,
        compiler_params=pltpu.CompilerParams(
            dimension_semantics=semantics, vmem_limit_bytes=VMEM_LIMIT, has_side_effects=True),
    )(*args, *exchange.operands)
    return out[:n_out], out[n_out:]


def _run_exchange(exchange, name):
    hbm = pl.BlockSpec(memory_space=pltpu.HBM)
    e_in, e_out = len(exchange.operands), len(exchange.out_shapes)

    def body(*refs):
        e_ins, e_outs, e_scr = refs[:e_in], refs[e_in:e_in + e_out], refs[e_in + e_out:]
        exchange.start(e_ins, e_outs, e_scr)
        if exchange.middle is not None:
            exchange.middle(e_ins, e_outs, e_scr)
        exchange.finish(e_ins, e_outs, e_scr)

    return pl.pallas_call(
        body, name=name, in_specs=[hbm] * e_in, out_specs=[hbm] * e_out, out_shape=exchange.out_shapes,
        scratch_shapes=exchange.scratch, compiler_params=pltpu.CompilerParams(has_side_effects=True),
    )(*exchange.operands)


def _loss_head(xo, gf, tgt):
    d = xo.shape[1]
    y, xh, r = _rms(xo, gf)
    err = y - tgt
    dy = err * (1.0 / d)
    loss = 0.5 * jnp.sum(jnp.sum(err * err, axis=-1, keepdims=True) * (1.0 / d), axis=0, keepdims=True)
    return _rms_bwd(dy, xh, r, gf), loss, jnp.sum(dy * xh, axis=0, keepdims=True)


def _ffn_fwd(x, g, wg, wu, wd, exchange=None, head=None):
    t, d = x.shape
    f = wg.shape[0]
    tm = min(WIDE_TILE, t)

    def body(x_ref, g_ref, wg_ref, wu_ref, wd_ref, *rest):
        if head is None:
            xo_ref, a_ref, b_ref, s_ref = rest
        else:
            gf_ref, tgt_ref, xo_ref, a_ref, b_ref, s_ref, loss_ref, dgf_ref = rest
            _zero_at_start(loss_ref, dgf_ref)
        xv = x_ref[...]
        h, _, _ = _rms(xv, g_ref[...])
        hb = h.astype(BF16)
        a = _dot(hb, wg_ref[...], NT)
        b = _dot(hb, wu_ref[...], NT)
        s = (a * _sigmoid(a) * b).astype(BF16)
        xo = xv + 0.5 * _dot(s, wd_ref[...])
        if head is None:
            xo_ref[...] = xo
        else:
            xo_ref[...], loss, dgf = _loss_head(xo, gf_ref[...], tgt_ref[...])
            loss_ref[...] += jnp.broadcast_to(loss, (1, 128))
            dgf_ref[...] += dgf
        a_ref[...] = a.astype(BF16)
        b_ref[...] = b.astype(BF16)
        s_ref[...] = s

    in_specs = [_rows(tm, d), _full((1, d)), _full_once((f, d)), _full_once((f, d)), _full_once((f, d))]
    out_specs = [_rows(tm, d), _rows(tm, f), _rows(tm, f), _rows(tm, f)]
    out_shape = [
        jax.ShapeDtypeStruct((t, d), F32),
        jax.ShapeDtypeStruct((t, f), BF16),
        jax.ShapeDtypeStruct((t, f), BF16),
        jax.ShapeDtypeStruct((t, f), BF16),
    ]
    args = (x, g, wg, wu, wd)
    if head is not None:
        in_specs += [_full((1, d)), _rows(tm, d)]
        out_specs += [_full((1, 128)), _full((1, d))]
        out_shape += [jax.ShapeDtypeStruct((1, 128), F32), jax.ShapeDtypeStruct((1, d), F32)]
        args += tuple(head)
    return _call(
        body, name="ffn_fwd", grid=(t // tm,), in_specs=in_specs, out_specs=out_specs, out_shape=out_shape,
        args=args, exchange=exchange)


def _ffn_up(x, g, wg, wu, exchange=None):
    t, d = x.shape
    f = wg.shape[0]
    tm = min(TOKEN_TILE, t)

    def body(x_ref, g_ref, wg_ref, wu_ref, a_ref, b_ref, s_ref):
        h, _, _ = _rms(x_ref[...], g_ref[...])
        hb = h.astype(BF16)
        a = _dot(hb, wg_ref[...], NT)
        b = _dot(hb, wu_ref[...], NT)
        a_ref[...] = a.astype(BF16)
        b_ref[...] = b.astype(BF16)
        s_ref[...] = (a * _sigmoid(a) * b).astype(BF16)

    return _call(
        body, name="ffn_up", grid=(t // tm,),
        in_specs=[_rows(tm, d), _full((1, d)), _full_once((f, d)), _full_once((f, d))],
        out_specs=[_rows(tm, f)] * 3, out_shape=[jax.ShapeDtypeStruct((t, f), BF16)] * 3,
        args=(x, g, wg, wu), exchange=exchange)


def _ffn_down(x, s, wd, exchange=None):
    t, d = x.shape
    f = wd.shape[0]
    tm = min(TOKEN_TILE, t)

    def body(x_ref, s_ref, wd_ref, xo_ref):
        xo_ref[...] = x_ref[...] + 0.5 * _dot(s_ref[...], wd_ref[...])

    return _call(
        body, name="ffn_down", grid=(t // tm,),
        in_specs=[_rows(tm, d), _rows(tm, f), _full_once((f, d))],
        out_specs=[_rows(tm, d)], out_shape=[jax.ShapeDtypeStruct((t, d), F32)],
        args=(x, s, wd), exchange=exchange)


def _ffn_bwd(x, g, dxo, a, b, wg, wu, wd, exchange=None):
    t, d = x.shape
    f = wg.shape[0]
    tm = min(TOKEN_TILE, t)

    def body(x_ref, g_ref, dxo_ref, a_ref, b_ref, wg_ref, wu_ref, wd_ref, dx_ref, da_ref, db_ref, h_ref, dg_ref):
        _zero_at_start(dg_ref)
        gv = g_ref[...]
        h, xh, r = _rms(x_ref[...], gv)
        dxo = dxo_ref[...]
        ds = _dot((0.5 * dxo).astype(BF16), wd_ref[...], NT)
        af = a_ref[...].astype(F32)
        bf = b_ref[...].astype(F32)
        sg = _sigmoid(af)
        da = (ds * bf * (sg * (1.0 + af * (1.0 - sg)))).astype(BF16)
        db = (ds * (af * sg)).astype(BF16)
        dh = _dot(da, wg_ref[...]) + _dot(db, wu_ref[...])
        dx_ref[...] = _rms_bwd(dh, xh, r, gv) + dxo
        da_ref[...] = da
        db_ref[...] = db
        h_ref[...] = h.astype(BF16)
        dg_ref[...] += jnp.sum(dh * xh, axis=0, keepdims=True)

    return _call(
        body,
        name="ffn_bwd",
        grid=(t // tm,),
        in_specs=[
            _rows(tm, d), _full((1, d)), _rows(tm, d), _rows(tm, f), _rows(tm, f),
            _full_once((f, d)), _full_once((f, d)), _full_once((f, d)),
        ],
        out_specs=[_rows(tm, d), _rows(tm, f), _rows(tm, f), _rows(tm, d), _full((1, d))],
        out_shape=[
            jax.ShapeDtypeStruct((t, d), F32),
            jax.ShapeDtypeStruct((t, f), BF16),
            jax.ShapeDtypeStruct((t, f), BF16),
            jax.ShapeDtypeStruct((t, d), BF16),
            jax.ShapeDtypeStruct((1, d), F32),
        ],
        args=(x, g, dxo, a, b, wg, wu, wd),
        exchange=exchange,
    )


def _weight_grad(a, b, scale=1.0, exchange=None):
    t, m = a.shape
    n = b.shape[1]
    chips = N_DEV // 2
    r = m // N_DEV
    tk = min(REDUCE_TILE, t)
    halves = 2
    nb = n // halves
    nk = t // tk

    def body(a_ref, b_ref, o_ref, acc, send_buf, recv_buf, send_sems, recv_sems):
        k, j = pl.program_id(0), pl.program_id(1)
        x, y, c, _ = _mesh_place()
        sibling, _ = _peer(x, y, c, 1)
        bv = b_ref[...]
        if scale != 1.0:
            bv = bv * scale
        bb = bv.astype(BF16)
        acc_half = acc.at[j]

        @pl.when(k == 0)
        def _():
            acc_half[...] = jnp.zeros_like(acc_half)

        for i in range(m // MXU_ROWS):
            rows = slice(i * MXU_ROWS, (i + 1) * MXU_ROWS)
            acc_half[rows, :] += _dot(a_ref[:, rows].astype(BF16), bb, TN)

        def to_sibling(half):
            return _remote(send_buf.at[half], recv_buf.at[half], send_sems.at[half], recv_sems.at[half], sibling)

        def owned_rows(q, core):
            return pl.ds(pl.multiple_of((2 * q + core) * r, 8), r)

        for half in range(halves):
            @pl.when(jnp.logical_and(k == nk - 1, j == half))
            def _():
                for q in range(chips):
                    send_buf[half, q] = acc[half, owned_rows(q, 1 - c), :].astype(BF16)
                to_sibling(half).start()

        @pl.when(jnp.logical_and(k == nk - 1, j == halves - 1))
        def _():
            for half in range(halves):
                to_sibling(half).wait_send()
                to_sibling(half).wait_recv()
                for q in range(chips):
                    o_ref[q, :, half * nb:(half + 1) * nb] = (
                        acc[half, owned_rows(q, c), :] + recv_buf[half, q].astype(F32)).astype(BF16)

    (partial,), arrived = _call(
        body,
        name="weight_grad",
        grid=(nk, halves),
        in_specs=[pl.BlockSpec((tk, m), lambda k, j: (k, 0)), pl.BlockSpec((tk, nb), lambda k, j: (k, j))],
        out_specs=[pl.BlockSpec((chips, r, n), lambda k, j: (0, 0, 0))],
        out_shape=[jax.ShapeDtypeStruct((chips, r, n), BF16)],
        scratch_shapes=[
            pltpu.VMEM((halves, m, nb), F32),
            pltpu.VMEM((halves, chips, r, nb), BF16), pltpu.VMEM((halves, chips, r, nb), BF16),
            pltpu.SemaphoreType.DMA((halves,)), pltpu.SemaphoreType.DMA((halves,)),
        ],
        args=(a, b),
        exchange=exchange,
    )
    return partial, arrived


def _chunk_cumsum(v, reverse=False):
    n, width = v.shape
    row = lax.broadcasted_iota(jnp.int32, (n, n), 0)
    col = lax.broadcasted_iota(jnp.int32, (n, n), 1)
    earlier = col >= row if reverse else col <= row
    tri = jnp.where(jnp.logical_and(row // CHUNK == col // CHUNK, earlier), 1.0, 0.0).astype(BF16)
    hi = v.astype(BF16)
    rest = v - hi.astype(F32)
    mid = rest.astype(BF16)
    low = (rest - mid.astype(F32)).astype(BF16)
    sums = _dot(tri, jnp.concatenate([hi, mid, low], axis=1))
    return sums[:, 0:width] + sums[:, width:2 * width] + sums[:, 2 * width:3 * width]


def _shift_rows(v, shift, edge):
    n = v.shape[0]
    row = lax.broadcasted_iota(jnp.int32, (n, 1), 0)
    out = pltpu.roll(v, shift % n, axis=0)
    if shift > 0:
        for j in range(shift):
            out = jnp.where(row == j, edge[8 - shift + j:8 - shift + j + 1, :], out)
    else:
        for j in range(-shift):
            out = jnp.where(row == n + shift + j, edge[j:j + 1, :], out)
    return out


def _gates(z, lbp):
    w = HGRN_W
    lb = _sigmoid(lbp[0:1, :] - lbp[1:2, :])
    zq = z[:, 0:w]
    sig = _sigmoid(z[:, w:2 * w])
    f = lb + (1.0 - lb) * sig
    sq = _sigmoid(zq)
    q = zq * sq * HGRN_DK ** -0.5
    return lb, sig, f, sq, q


def _decayed_operands(q, f, v, qm_buf, km_buf, kbar_buf, v_buf, etot_buf, emid_buf):
    n, width = f.shape
    bcum = _chunk_cumsum(jnp.log(f))

    def row_of_chunk(offset):
        return jnp.concatenate(
            [jnp.broadcast_to(bcum[c + offset:c + offset + 1, :], (CHUNK, width)) for c in range(0, n, CHUNK)], axis=0)

    total, mid = row_of_chunk(CHUNK - 1), row_of_chunk(CHUNK // 2 - 1)
    em, enm, erest = jnp.exp(bcum - mid), jnp.exp(mid - bcum), jnp.exp(total - bcum)
    kk = 1.0 - f
    qm_buf[...] = (q * em).astype(BF16)
    km_buf[...] = (kk * enm).astype(BF16)
    kbar_buf[...] = (kk * erest).astype(BF16)
    v_buf[...] = v.astype(BF16)
    etot_buf[...] = jnp.exp(total)
    emid_buf[...] = jnp.exp(mid)
    return em, enm, erest


def _short_conv(u, edge, cw):
    return cw[0:1, :] * _shift_rows(u, 2, edge) + cw[1:2, :] * _shift_rows(u, 1, edge) + cw[2:3, :] * u


def _block_causal_mask(n):
    row = lax.broadcasted_iota(jnp.int32, (n, n), 0)
    col = lax.broadcasted_iota(jnp.int32, (n, n), 1)
    return jnp.logical_and(row // CHUNK == col // CHUNK, col <= row)


def _spread(v, chunk_of_row, nc):
    return jnp.concatenate([jnp.where(chunk_of_row == c, v, jnp.zeros_like(v)) for c in range(nc)], axis=1)


def _pick(r, chunk_of_row, nc):
    out = jnp.where(chunk_of_row == 0, r[:, 0:HGRN_DK], 0.0)
    for c in range(1, nc):
        out = out + jnp.where(chunk_of_row == c, r[:, c * HGRN_DK:(c + 1) * HGRN_DK], 0.0)
    return out


def _mix_fwd(x, g, w_in, lbp, gh, convw_t, w_out, exchange=None):
    t, d = x.shape
    zw = w_in.shape[0]
    w = HGRN_W
    tm = min(TOKEN_TILE, t)
    nc = tm // CHUNK
    n_chunks = t // CHUNK

    def body(x_ref, g_ref, win_ref, lbp_ref, gh_ref, cw_ref, wout_ref,
             xo_ref, z_ref, o_ref, st_ref, y_ref, state, ucarry, qm_buf, km_buf, kbar_buf, v_buf, etot_buf, emid_buf):
        _zero_at_start(state, ucarry)
        xv = x_ref[...]
        h, _, _ = _rms(xv, g_ref[...])
        z_ref[...] = _dot(h.astype(BF16), win_ref[...], NT)
        z = z_ref[...]
        _, _, f, _, q = _gates(z, lbp_ref[...])
        _decayed_operands(q, f, z[:, 2 * w:3 * w], qm_buf, km_buf, kbar_buf, v_buf, etot_buf, emid_buf)
        mask = _block_causal_mask(tm)
        chunk_of_row = lax.broadcasted_iota(jnp.int32, (tm, 1), 0) // CHUNK
        heads = range(HGRN_HEADS)
        hcols = [slice(hd * HGRN_DK, (hd + 1) * HGRN_DK) for hd in heads]
        qm = [qm_buf[:, hcols[hd]] for hd in heads]
        vb = [v_buf[:, hcols[hd]] for hd in heads]
        scores = [jnp.where(mask, _dot(qm[hd], km_buf[:, hcols[hd]], NT), 0.0).astype(BF16) for hd in heads]
        gains = [_dot(_spread(vb[hd], chunk_of_row, nc), kbar_buf[:, hcols[hd]], TN) for hd in heads]
        entering = []
        for hd in heads:
            states, st = [], state[hd]
            for c in range(nc):
                first_row = slice(c * CHUNK, c * CHUNK + 1)
                states.append(st * emid_buf[first_row, hcols[hd]])
                st_ref[c, hd] = st
                st = st * etot_buf[first_row, hcols[hd]] + gains[hd][c * HGRN_DK:(c + 1) * HGRN_DK, :]
            state[hd] = st
            entering.append(jnp.concatenate(states, axis=0).astype(BF16))
        from_states = [_dot(qm[hd], entering[hd], NT) for hd in heads]
        o_heads = [_dot(scores[hd], vb[hd]) + _pick(from_states[hd], chunk_of_row, nc) for hd in heads]
        o_ref[...] = jnp.concatenate(o_heads, axis=1)
        ghv = gh_ref[...]
        normed = jnp.concatenate([_rms(o_heads[hd], ghv[:, hcols[hd]])[0] for hd in heads], axis=1)
        zg = z[:, 3 * w:4 * w]
        u = z[:, 5 * w:6 * w] * z[:, 6 * w:7 * w]
        conv = _short_conv(u, ucarry[...], cw_ref[...])
        ucarry[...] = u[tm - 8:tm, :]
        y = jnp.concatenate([normed * (zg * _sigmoid(zg)), z[:, 4 * w:5 * w] * conv], axis=1).astype(BF16)
        y_ref[...] = y
        xo_ref[...] = xv + _dot(y, wout_ref[...])

    return _call(
        body,
        name="mix_fwd",
        grid=(t // tm,),
        in_specs=[
            _rows(tm, d), _full((1, d)), _full((zw, d)), _full((2, w)), _full((1, w)), _full((3, w)),
            _full((2 * w, d)),
        ],
        out_specs=[
            _rows(tm, d), _rows(tm, zw), _rows(tm, w),
            pl.BlockSpec((nc, HGRN_HEADS, HGRN_DK, HGRN_DK), lambda i: (i, 0, 0, 0)),
            _rows(tm, 2 * w),
        ],
        out_shape=[
            jax.ShapeDtypeStruct((t, d), F32),
            jax.ShapeDtypeStruct((t, zw), F32),
            jax.ShapeDtypeStruct((t, w), F32),
            jax.ShapeDtypeStruct((n_chunks, HGRN_HEADS, HGRN_DK, HGRN_DK), F32),
            jax.ShapeDtypeStruct((t, 2 * w), BF16),
        ],
        scratch_shapes=[
            pltpu.VMEM((HGRN_HEADS, HGRN_DK, HGRN_DK), F32), pltpu.VMEM((8, w), F32),
            pltpu.VMEM((tm, w), BF16), pltpu.VMEM((tm, w), BF16), pltpu.VMEM((tm, w), BF16),
            pltpu.VMEM((tm, w), BF16), pltpu.VMEM((tm, w), F32), pltpu.VMEM((tm, w), F32),
        ],
        args=(x, g, w_in, lbp, gh, convw_t, w_out),
        exchange=exchange,
    )


def _mix_bwd(x, g, dxo, z, o, states, w_in, lbp, gh, convw_t, w_out, exchange=None):
    t, d = x.shape
    zw = w_in.shape[0]
    w = HGRN_W
    tm = min(TOKEN_TILE, t)
    nc = tm // CHUNK
    n = t // tm

    def body(x_ref, g_ref, dxo_ref, z_ref, zprev_ref, o_ref, st_ref, win_ref, lbp_ref, gh_ref, cw_ref, wout_ref,
             dx_ref, dz_ref, h_ref, dg_ref, dlbp_ref, dgh_ref, dcw_ref,
             dstate, dcarry, do_buf, qm_buf, km_buf, kbar_buf, v_buf, etot_buf, emid_buf):
        _zero_at_start(dstate, dcarry, dg_ref, dlbp_ref, dgh_ref, dcw_ref)
        gv = g_ref[...]
        h, xh, r = _rms(x_ref[...], gv)
        h_ref[...] = h.astype(BF16)
        dxo = dxo_ref[...]
        dy = _dot(dxo.astype(BF16), wout_ref[...], NT)
        z = z_ref[...]
        lb, sig, f, sq, q = _gates(z, lbp_ref[...])
        em, enm, erest = _decayed_operands(
            q, f, z[:, 2 * w:3 * w], qm_buf, km_buf, kbar_buf, v_buf, etot_buf, emid_buf)

        ghv = gh_ref[...]
        zg = z[:, 3 * w:4 * w]
        sgz = _sigmoid(zg)
        dyh = dy[:, 0:w]
        don = dyh * (zg * sgz)
        heads = range(HGRN_HEADS)
        hcols = [slice(hd * HGRN_DK, (hd + 1) * HGRN_DK) for hd in heads]
        norms = [_rms(o_ref[:, hcols[hd]], ghv[:, hcols[hd]]) for hd in heads]
        on = jnp.concatenate([norms[hd][0] for hd in heads], axis=1)
        oh = jnp.concatenate([norms[hd][1] for hd in heads], axis=1)
        dz_ref[:, 3 * w:4 * w] = (dyh * on * (sgz * (1.0 + zg * (1.0 - sgz)))).astype(BF16)
        dgh_ref[...] += jnp.sum(don * oh, axis=0, keepdims=True)
        do_buf[...] = jnp.concatenate(
            [_rms_bwd(don[:, hcols[hd]], norms[hd][1], norms[hd][2], ghv[:, hcols[hd]]) for hd in heads],
            axis=1).astype(BF16)

        zb = z[:, 4 * w:5 * w]
        zc = z[:, 5 * w:6 * w]
        zu = z[:, 6 * w:7 * w]
        u = zc * zu
        cw = cw_ref[...]
        zp = zprev_ref[...]
        uprev = jnp.where(pl.program_id(0) == n - 1, 0.0, zp[:, 5 * w:6 * w] * zp[:, 6 * w:7 * w])
        dyc = dy[:, w:2 * w]
        dz_ref[:, 4 * w:5 * w] = (dyc * _short_conv(u, uprev, cw)).astype(BF16)
        dconv = dyc * zb
        edge = dcarry[...]
        dconv1 = _shift_rows(dconv, -1, edge)
        dconv2 = _shift_rows(dconv, -2, edge)
        dcarry[...] = dconv[0:8, :]
        du = cw[2:3, :] * dconv + cw[1:2, :] * dconv1 + cw[0:1, :] * dconv2
        dz_ref[:, 5 * w:6 * w] = (du * zu).astype(BF16)
        dz_ref[:, 6 * w:7 * w] = (du * zc).astype(BF16)
        dcw_ref[...] += jnp.concatenate([
            jnp.sum(u * dconv2, axis=0, keepdims=True),
            jnp.sum(u * dconv1, axis=0, keepdims=True),
            jnp.sum(u * dconv, axis=0, keepdims=True)], axis=0)

        mask = _block_causal_mask(tm)
        chunk_of_row = lax.broadcasted_iota(jnp.int32, (tm, 1), 0) // CHUNK
        heads = range(HGRN_HEADS)
        hcols = [slice(hd * HGRN_DK, (hd + 1) * HGRN_DK) for hd in heads]
        qmb = [qm_buf[:, hcols[hd]] for hd in heads]
        kmb = [km_buf[:, hcols[hd]] for hd in heads]
        vb = [v_buf[:, hcols[hd]] for hd in heads]
        dob = [do_buf[:, hcols[hd]] for hd in heads]
        scores = [jnp.where(mask, _dot(qmb[hd], kmb[hd], NT), 0.0).astype(BF16) for hd in heads]
        dscores = [jnp.where(mask, _dot(dob[hd], vb[hd], NT), 0.0).astype(BF16) for hd in heads]
        gains = [_dot(_spread(dob[hd], chunk_of_row, nc), qmb[hd], TN) for hd in heads]
        dst_rows, dst_lanes, st_lanes, carries = [], [], [], []
        for hd in heads:
            entering = [st_ref[c, hd] for c in range(nc)]
            emid = [emid_buf[c * CHUNK:c * CHUNK + 1, hcols[hd]] for c in range(nc)]
            leaving, carried_back = [None] * nc, [None] * nc
            dst = dstate[hd]
            for c in reversed(range(nc)):
                elast = etot_buf[c * CHUNK:c * CHUNK + 1, hcols[hd]]
                leaving[c] = dst
                carried_back[c] = jnp.sum(dst * entering[c], axis=0, keepdims=True) * elast
                dst = dst * elast + gains[hd][c * HGRN_DK:(c + 1) * HGRN_DK, :] * emid[c]
            dstate[hd] = dst
            dst_rows.append(jnp.concatenate(leaving, axis=0).astype(BF16))
            dst_lanes.append(jnp.concatenate(leaving, axis=1).astype(BF16))
            st_lanes.append(jnp.concatenate([entering[c] * emid[c] for c in range(nc)], axis=1).astype(BF16))
            carries.append(carried_back)
        dv = [_dot(scores[hd], dob[hd], TN) + _pick(_dot(kbar_buf[:, hcols[hd]], dst_rows[hd], NT), chunk_of_row, nc)
              for hd in heads]
        dz_ref[:, 2 * w:3 * w] = jnp.concatenate(dv, axis=1).astype(BF16)
        dqm = jnp.concatenate([_dot(dscores[hd], kmb[hd]) + _pick(_dot(dob[hd], st_lanes[hd]), chunk_of_row, nc)
                               for hd in heads], axis=1)
        dkm = jnp.concatenate([_dot(dscores[hd], qmb[hd], TN) for hd in heads], axis=1)
        dkbar = jnp.concatenate([_pick(_dot(vb[hd], dst_lanes[hd]), chunk_of_row, nc) for hd in heads], axis=1)

        kbar_dkbar = kbar_buf[...].astype(F32) * dkbar
        db = qm_buf[...].astype(F32) * dqm - km_buf[...].astype(F32) * dkm - kbar_dkbar
        through_last = jnp.concatenate([
            jnp.broadcast_to(
                jnp.sum(kbar_dkbar[c * CHUNK:(c + 1) * CHUNK], axis=0, keepdims=True)
                + jnp.concatenate([carries[hd][c] for hd in heads], axis=1),
                (CHUNK, w))
            for c in range(nc)], axis=0)
        dlogf = _chunk_cumsum(db, reverse=True) + through_last
        df = dlogf / f - (dkm * enm + dkbar * erest)
        zq = z[:, 0:w]
        dz_ref[:, 0:w] = (dqm * em * HGRN_DK ** -0.5 * (sq * (1.0 + zq * (1.0 - sq)))).astype(BF16)
        dz_ref[:, w:2 * w] = (df * (1.0 - lb) * sig * (1.0 - sig)).astype(BF16)
        dlb = jnp.sum(df * (1.0 - sig), axis=0, keepdims=True) * lb * (1.0 - lb)
        dlbp_ref[...] += jnp.concatenate([dlb, -dlb], axis=0)

        dh = _dot(dz_ref[...], win_ref[...])
        dx_ref[...] = _rms_bwd(dh, xh, r, gv) + dxo
        dg_ref[...] += jnp.sum(dh * xh, axis=0, keepdims=True)

    return _call(
        body,
        name="mix_bwd",
        grid=(n,),
        in_specs=[
            _rows_rev(tm, d, n), _full((1, d)), _rows_rev(tm, d, n), _rows_rev(tm, zw, n),
            pl.BlockSpec((8, zw), lambda i: (jnp.maximum((n - 1 - i) * (tm // 8) - 1, 0), 0)),
            _rows_rev(tm, w, n),
            pl.BlockSpec((nc, HGRN_HEADS, HGRN_DK, HGRN_DK), lambda i: (n - 1 - i, 0, 0, 0)),
            _full((zw, d)), _full((2, w)), _full((1, w)), _full((3, w)), _full((2 * w, d)),
        ],
        out_specs=[
            _rows_rev(tm, d, n), _rows_rev(tm, zw, n), _rows_rev(tm, d, n),
            _full((1, d)), _full((2, w)), _full((1, w)), _full((3, w)),
        ],
        out_shape=[
            jax.ShapeDtypeStruct((t, d), F32),
            jax.ShapeDtypeStruct((t, zw), BF16),
            jax.ShapeDtypeStruct((t, d), BF16),
            jax.ShapeDtypeStruct((1, d), F32),
            jax.ShapeDtypeStruct((2, w), F32),
            jax.ShapeDtypeStruct((1, w), F32),
            jax.ShapeDtypeStruct((3, w), F32),
        ],
        scratch_shapes=[
            pltpu.VMEM((HGRN_HEADS, HGRN_DK, HGRN_DK), F32), pltpu.VMEM((8, w), F32),
            pltpu.VMEM((tm, w), BF16),
            pltpu.VMEM((tm, w), BF16), pltpu.VMEM((tm, w), BF16), pltpu.VMEM((tm, w), BF16),
            pltpu.VMEM((tm, w), BF16), pltpu.VMEM((tm, w), F32), pltpu.VMEM((tm, w), F32),
        ],
        args=(x, g, dxo, z, z, o, states, w_in, lbp, gh, convw_t, w_out),
        exchange=exchange,
    )


def _memkv_fwd(mem, g, wkv):
    m, d = mem.shape
    nb, _, cb = wkv.shape

    def body(mem_ref, g_ref, wkv_ref, kv_ref):
        mn, _, _ = _rms(mem_ref[...], g_ref[...])
        mnb = mn.astype(BF16)
        for j in range(nb):
            kv_ref[:, j * cb:(j + 1) * cb] = _dot(mnb, wkv_ref[j]).astype(BF16)

    return pl.pallas_call(
        body,
        name="memkv_fwd",
        out_shape=jax.ShapeDtypeStruct((m, nb * cb), BF16),
        compiler_params=_params(),
    )(mem, g, wkv)


def _memkv_bwd(mem, g, dkv, wkv):
    m, d = mem.shape
    nb, _, cb = wkv.shape
    chips = nb // 2

    def body(mem_ref, g_ref, dkv_ref, wkv_ref, dw_ref, dg_ref, dw_all, send_buf, recv_buf, send_sem, recv_sem):
        x, y, c, _ = _mesh_place()
        sibling, _ = _peer(x, y, c, 1)
        mn, xh, _ = _rms(mem_ref[...], g_ref[...])
        mnb = mn.astype(BF16)
        dmn = jnp.zeros((m, d), F32)
        for j in range(nb):
            dkvb = dkv_ref[:, j * cb:(j + 1) * cb].astype(BF16)
            dw_all[j] = _dot(mnb, dkvb, TN)
            dmn = dmn + _dot(dkvb, wkv_ref[j], NT)
        dg_ref[...] = jnp.sum(dmn * xh, axis=0, keepdims=True)
        for q in range(chips):
            send_buf[q] = dw_all[2 * q + 1 - c].astype(BF16)
        to_sibling = _remote(send_buf, recv_buf, send_sem, recv_sem, sibling)
        to_sibling.start()
        to_sibling.wait_send()
        to_sibling.wait_recv()
        for q in range(chips):
            dw_ref[q] = (dw_all[2 * q + c] + recv_buf[q].astype(F32)).astype(BF16)

    return pl.pallas_call(
        body,
        name="memkv_bwd",
        out_shape=[jax.ShapeDtypeStruct((chips, d, cb), BF16), jax.ShapeDtypeStruct((1, d), F32)],
        scratch_shapes=[
            pltpu.VMEM((nb, d, cb), F32), pltpu.VMEM((chips, d, cb), BF16), pltpu.VMEM((chips, d, cb), BF16),
            pltpu.SemaphoreType.DMA, pltpu.SemaphoreType.DMA,
        ],
        compiler_params=_params(),
    )(mem, g, dkv, wkv)


def _softmax_rows(qm_h, k_h):
    sc = _dot(qm_h, k_h, NT) * MEM_HD ** -0.5
    e = jnp.exp(sc - jnp.max(sc, axis=-1, keepdims=True))
    return e / jnp.sum(e, axis=-1, keepdims=True)


def _xattn_fwd(x, g, wq, kv, wo, exchange=None):
    t, d = x.shape
    m = kv.shape[0]
    tm = min(XATTN_TILE, t)

    def body(x_ref, g_ref, wq_ref, kv_ref, wo_ref, xo_ref, hq_ref, qm_ref, att_ref):
        xv = x_ref[...]
        h, _, _ = _rms(xv, g_ref[...])
        hb = h.astype(BF16)
        hq_ref[...] = hb
        qm = _dot(hb, wq_ref[...]).astype(BF16)
        qm_ref[...] = qm
        heads = range(MEM_HEADS)
        kcols = [slice(hd * MEM_HD, (hd + 1) * MEM_HD) for hd in heads]
        p = [_softmax_rows(qm[:, kcols[hd]], kv_ref[:, kcols[hd]]) for hd in heads]
        att = jnp.concatenate(
            [_dot(p[hd].astype(BF16), kv_ref[:, d + hd * MEM_HD:d + (hd + 1) * MEM_HD]) for hd in heads],
            axis=1).astype(BF16)
        att_ref[...] = att
        xo_ref[...] = xv + _dot(att, wo_ref[...])

    return _call(
        body,
        name="xattn_fwd",
        grid=(t // tm,),
        in_specs=[_rows(tm, d), _full((1, d)), _full((d, d)), _full((m, 2 * d)), _full((d, d))],
        out_specs=[_rows(tm, d), _rows(tm, d), _rows(tm, d), _rows(tm, d)],
        out_shape=[
            jax.ShapeDtypeStruct((t, d), F32),
            jax.ShapeDtypeStruct((t, d), BF16),
            jax.ShapeDtypeStruct((t, d), BF16),
            jax.ShapeDtypeStruct((t, d), BF16),
        ],
        args=(x, g, wq, kv, wo),
        exchange=exchange,
    )


def _xattn_bwd(x, g, dxo, qm, kv, wq, wo, exchange=None):
    t, d = x.shape
    m = kv.shape[0]
    tm = min(XATTN_TILE, t)

    def body(x_ref, g_ref, dxo_ref, qm_ref, kv_ref, wq_ref, wo_ref, dx_ref, dqm_ref, dkv_ref, dg_ref):
        _zero_at_start(dkv_ref, dg_ref)
        gv = g_ref[...]
        _, xh, r = _rms(x_ref[...], gv)
        dxo = dxo_ref[...]
        datt = _dot(dxo.astype(BF16), wo_ref[...], NT).astype(BF16)
        heads = range(MEM_HEADS)
        kcols = [slice(hd * MEM_HD, (hd + 1) * MEM_HD) for hd in heads]
        vcols = [slice(d + hd * MEM_HD, d + (hd + 1) * MEM_HD) for hd in heads]
        qm_h = [qm_ref[:, kcols[hd]] for hd in heads]
        p = [_softmax_rows(qm_h[hd], kv_ref[:, kcols[hd]]) for hd in heads]
        dp = [_dot(datt[:, kcols[hd]], kv_ref[:, vcols[hd]], NT) for hd in heads]
        dsc = [(p[hd] * (dp[hd] - jnp.sum(p[hd] * dp[hd], axis=-1, keepdims=True)) * MEM_HD ** -0.5).astype(BF16)
               for hd in heads]
        dqm = jnp.concatenate([_dot(dsc[hd], kv_ref[:, kcols[hd]]) for hd in heads], axis=1).astype(BF16)
        dqm_ref[...] = dqm
        dkv_ref[...] += jnp.concatenate(
            [_dot(dsc[hd], qm_h[hd], TN) for hd in heads]
            + [_dot(p[hd].astype(BF16), datt[:, kcols[hd]], TN) for hd in heads], axis=1)
        dh = _dot(dqm, wq_ref[...], NT)
        dx_ref[...] = _rms_bwd(dh, xh, r, gv) + dxo
        dg_ref[...] += jnp.sum(dh * xh, axis=0, keepdims=True)

    return _call(
        body,
        name="xattn_bwd",
        grid=(t // tm,),
        in_specs=[
            _rows(tm, d), _full((1, d)), _rows(tm, d), _rows(tm, d), _full((m, 2 * d)), _full((d, d)), _full((d, d)),
        ],
        out_specs=[_rows(tm, d), _rows(tm, d), _full((m, 2 * d)), _full((1, d))],
        out_shape=[
            jax.ShapeDtypeStruct((t, d), F32),
            jax.ShapeDtypeStruct((t, d), BF16),
            jax.ShapeDtypeStruct((m, 2 * d), F32),
            jax.ShapeDtypeStruct((1, d), F32),
        ],
        args=(x, g, dxo, qm, kv, wq, wo),
        exchange=exchange,
    )


def _mesh_place():
    x, y, c = lax.axis_index("x"), lax.axis_index("y"), lax.axis_index("c")
    return x, y, c, 4 * x + 2 * y + c


def _peer(x, y, c, k):
    px = 1 - x if k & 4 else x
    py = 1 - y if k & 2 else y
    pc = 1 - c if k & 1 else c
    return (px, py, pc), 4 * px + 2 * py + pc


ICI_HOPS = (2, 4, 6)
N_HOPS = len(ICI_HOPS)


def _remote(src, dst, send_sem, recv_sem, peer):
    return pltpu.make_async_remote_copy(
        src_ref=src, dst_ref=dst, send_sem=send_sem, recv_sem=recv_sem, device_id=peer, device_id_type=MESH_IDS)


def _gather_exchange(shards, middle_eighths=MIDDLE_EIGHTHS):
    n = len(shards)

    def place():
        x, y, c, me = _mesh_place()
        sibling, _ = _peer(x, y, c, 1)
        to_x, from_x = _peer(x, y, c, 4)
        to_y, from_y = _peer(x, y, c, 2)
        _, from_diagonal = _peer(x, y, c, 6)
        onward = (c * to_y[0] + (1 - c) * to_x[0], c * to_y[1] + (1 - c) * to_x[1], c)
        passed_on = c * from_x + (1 - c) * from_y
        return me, sibling, (to_x, to_y, onward), (from_x, from_y, from_diagonal), passed_on

    def start(src, dst, sems):
        ici_send, ici_recv, pair_send, pair_recv, local = sems
        me, sibling, targets, _, _ = place()
        for a in range(n):
            pltpu.make_async_copy(src[a], dst[a].at[me], local.at[a]).start()
            for j in range(2):
                _remote(src[a], dst[a].at[me], ici_send.at[a, j], ici_recv.at[a, j], targets[j]).start()
            _remote(src[a], dst[a].at[me], pair_send.at[a, 0], pair_recv.at[a, 0], sibling).start()

    def to_sibling(dst, sems, a, j, origin, sibling):
        _, _, pair_send, pair_recv, _ = sems
        slot = dst[a].at[origin]
        return _remote(slot, slot, pair_send.at[a, 1 + j], pair_recv.at[a, 1 + j], sibling)

    def middle(src, dst, sems):
        ici_send, ici_recv, _, _, _ = sems
        _, sibling, targets, origins, passed_on = place()
        for a in range(n):
            for j in range(2):
                _remote(src[a], dst[a].at[origins[j]], ici_send.at[a, j], ici_recv.at[a, j], targets[j]).wait_recv()
            slot = dst[a].at[passed_on]
            _remote(slot, slot, ici_send.at[a, 2], ici_recv.at[a, 2], targets[2]).start()
            for j in range(2):
                to_sibling(dst, sems, a, j, origins[j], sibling).start()

    def finish(src, dst, sems):
        ici_send, ici_recv, pair_send, pair_recv, local = sems
        me, sibling, targets, origins, _ = place()
        for a in range(n):
            _remote(src[a], dst[a].at[origins[2]], ici_send.at[a, 2], ici_recv.at[a, 2], targets[2]).wait_recv()
            to_sibling(dst, sems, a, 2, origins[2], sibling).start()
        for a in range(n):
            pltpu.make_async_copy(src[a], dst[a].at[me], local.at[a]).wait()
            for j in range(N_HOPS):
                _remote(src[a], dst[a].at[me], ici_send.at[a, j], ici_recv.at[a, j], targets[j]).wait_send()
            for j, origin in enumerate((me,) + origins):
                from_sibling = origin + 1 - 2 * (origin % 2)
                passed = _remote(src[a], dst[a].at[from_sibling], pair_send.at[a, j], pair_recv.at[a, j], sibling)
                passed.wait_send()
                passed.wait_recv()

    return _Exchange(
        shards,
        [jax.ShapeDtypeStruct((N_DEV,) + s.shape, s.dtype) for s in shards],
        [
            pltpu.SemaphoreType.DMA((n, N_HOPS)), pltpu.SemaphoreType.DMA((n, N_HOPS)),
            pltpu.SemaphoreType.DMA((n, N_HOPS + 1)), pltpu.SemaphoreType.DMA((n, N_HOPS + 1)),
            pltpu.SemaphoreType.DMA((n,)),
        ],
        start, finish, middle, middle_eighths)


def _scatter_copies(src, dst, sems, n, arrivals=False):
    send, recv, local = sems
    x, y, c, _ = _mesh_place()
    chip = 2 * x + y
    if arrivals is None:
        return [pltpu.make_async_copy(src[a].at[chip], dst[a].at[chip], local.at[a]) for a in range(n)]
    copies = []
    for a in range(n):
        for j, k in enumerate(ICI_HOPS):
            peer, _ = _peer(x, y, c, k)
            peer_chip = 2 * peer[0] + peer[1]
            slot = dst[a].at[peer_chip if arrivals else chip]
            copies.append(_remote(src[a].at[peer_chip], slot, send.at[a, j], recv.at[a, j], peer))
    return copies


def _scatter_start(src, dst, sems, n):
    for cp in _scatter_copies(src, dst, sems, n, arrivals=None) + _scatter_copies(src, dst, sems, n):
        cp.start()


def _scatter_finish(src, dst, sems, n):
    for cp in _scatter_copies(src, dst, sems, n, arrivals=None):
        cp.wait()
    for cp in _scatter_copies(src, dst, sems, n):
        cp.wait_send()
    for cp in _scatter_copies(src, dst, sems, n, arrivals=True):
        cp.wait_recv()


def _scatter_scratch(n):
    return [pltpu.SemaphoreType.DMA((n, N_HOPS)), pltpu.SemaphoreType.DMA((n, N_HOPS)), pltpu.SemaphoreType.DMA((n,))]


def _scatter_exchange(partials):
    n = len(partials)
    return _Exchange(
        partials, [jax.ShapeDtypeStruct(p.shape, p.dtype) for p in partials], _scatter_scratch(n),
        lambda src, dst, sems: _scatter_start(src, dst, sems, n),
        lambda src, dst, sems: _scatter_finish(src, dst, sems, n))


SMALL_LAYOUT = {
    "ffn1_norm": (0, 1, 1024), "mix_norm": (1, 1, 1024), "xattn_norm": (2, 1, 1024), "mem_norm": (3, 1, 1024),
    "ffn2_norm": (4, 1, 1024), "final_norm": (5, 1, 1024), "lb_param": (6, 2, 512), "hgrn_out_norm": (8, 1, 512),
    "conv_w": (9, 3, 512), "loss": (12, 1, 128),
}


def _final_exchange(partials, small):
    n = len(partials)
    names = list(small)
    width = 1024

    def body(*refs):
        src = refs[:n]
        pieces = refs[n:n + len(names)]
        dst = refs[n + len(names):2 * n + len(names)]
        total_ref = refs[2 * n + len(names)]
        pack, gathered, small_send, small_recv = refs[2 * n + len(names) + 1:2 * n + len(names) + 5]
        sems = refs[2 * n + len(names) + 5:]
        x, y, c, me = _mesh_place()
        pack[...] = jnp.zeros_like(pack)
        for name, piece in zip(names, pieces):
            row, nrows, ncols = SMALL_LAYOUT[name]
            pack[row:row + nrows, 0:ncols] = piece[...]
        for k in range(1, N_DEV):
            peer, _ = _peer(x, y, c, k)
            _remote(pack, gathered.at[me], small_send.at[k - 1], small_recv.at[k - 1], peer).start()
        _scatter_start(src, dst, sems, n)
        gathered[me] = pack[...]
        for k in range(1, N_DEV):
            peer, peer_index = _peer(x, y, c, k)
            landed = _remote(pack, gathered.at[peer_index], small_send.at[k - 1], small_recv.at[k - 1], peer)
            landed.wait_send()
            landed.wait_recv()
        total = gathered[0]
        for j in range(1, N_DEV):
            total = total + gathered[j]
        total_ref[...] = total
        _scatter_finish(src, dst, sems, n)

    hbm = pl.BlockSpec(memory_space=pltpu.HBM)
    vmem = pl.BlockSpec(memory_space=pltpu.VMEM)
    out = pl.pallas_call(
        body,
        name="final_exchange",
        in_specs=[hbm] * n + [vmem] * len(names),
        out_specs=[hbm] * n + [vmem],
        out_shape=[jax.ShapeDtypeStruct(p.shape, p.dtype) for p in partials]
        + [jax.ShapeDtypeStruct((SMALL_ROWS, width), F32)],
        scratch_shapes=[
            pltpu.VMEM((SMALL_ROWS, width), F32), pltpu.VMEM((N_DEV, SMALL_ROWS, width), F32),
            pltpu.SemaphoreType.DMA((N_DEV - 1,)), pltpu.SemaphoreType.DMA((N_DEV - 1,)),
        ] + _scatter_scratch(n),
        compiler_params=pltpu.CompilerParams(has_side_effects=True),
    )(*partials, *[small[k] for k in names])
    return out[:n], out[n]


def _adamw_math(w, g, m, v):
    m = ADAM_B1 * m + (1.0 - ADAM_B1) * g
    v = ADAM_B2 * v + (1.0 - ADAM_B2) * (g * g)
    m_hat = m / (1.0 - ADAM_B1 ** ADAM_STEP)
    v_hat = v / (1.0 - ADAM_B2 ** ADAM_STEP)
    delta = -ADAM_LR * (m_hat / (jnp.sqrt(v_hat) + ADAM_EPS) + ADAM_WD * w)
    return delta, m, v


def _adamw_shard(parts, w, m, v):
    r, c = w.shape
    n_parts = parts.shape[0]
    tr = max(rows for rows in range(16, r + 1, 16) if r % rows == 0 and rows * c <= ADAMW_TILE_ELEMENTS)

    def body(p_ref, w_ref, m_ref, v_ref, g_ref, d_ref, mo_ref, vo_ref):
        g = p_ref[0].astype(F32)
        for j in range(1, n_parts):
            g = g + p_ref[j].astype(F32)
        delta, mn, vn = _adamw_math(w_ref[...], g, m_ref[...], v_ref[...])
        g_ref[...] = g
        d_ref[...] = delta
        mo_ref[...] = mn
        vo_ref[...] = vn

    tile = pl.BlockSpec((tr, c), lambda i: (i, 0))
    return pl.pallas_call(
        body,
        name="adamw_shard",
        grid=(r // tr,),
        in_specs=[pl.BlockSpec((n_parts, tr, c), lambda i: (0, i, 0)), tile, tile, tile],
        out_specs=[tile] * 4,
        out_shape=[jax.ShapeDtypeStruct((r, c), F32)] * 4,
        compiler_params=_params(("parallel",)),
    )(parts, w, m, v)


def _adamw_small(gs, ws, ms, vs):
    n = len(gs)

    def body(*refs):
        g_refs, w_refs, m_refs, v_refs = refs[:n], refs[n:2 * n], refs[2 * n:3 * n], refs[3 * n:4 * n]
        g_out, d_out, m_out, v_out = refs[4 * n:5 * n], refs[5 * n:6 * n], refs[6 * n:7 * n], refs[7 * n:8 * n]
        for i in range(n):
            if gs[i].ndim == ws[i].ndim:
                g = g_refs[i][...]
            else:
                g = g_refs[i][0].astype(F32)
                for j in range(1, gs[i].shape[0]):
                    g = g + g_refs[i][j].astype(F32)
            delta, mn, vn = _adamw_math(w_refs[i][...], g, m_refs[i][...], v_refs[i][...])
            g_out[i][...] = g
            d_out[i][...] = delta
            m_out[i][...] = mn
            v_out[i][...] = vn

    shapes = [jax.ShapeDtypeStruct(w.shape, F32) for w in ws]
    out = pl.pallas_call(
        body,
        name="adamw_small",
        out_shape=shapes * 4,
        compiler_params=_params(),
    )(*gs, *ws, *ms, *vs)
    return out[:n], out[n:2 * n], out[2 * n:3 * n], out[3 * n:]


TRANSPOSED = ("ffn1_gate", "ffn1_up", "w_in", "ffn2_gate", "ffn2_up", "conv_w")
GROUP_FFN1 = ("ffn1_gate", "ffn1_up", "ffn1_down")
GROUP_MIX = ("w_in", "w_out")
GROUP_XATTN = ("w_q_mem", "w_kv_mem", "w_o_mem")
GROUP_FFN2 = ("ffn2_gate", "ffn2_up", "ffn2_down")
LARGE = GROUP_FFN1 + GROUP_MIX + GROUP_XATTN + GROUP_FFN2
SHORT_SHARDS = ("w_out", "w_q_mem", "w_kv_mem", "w_o_mem")
SMALL = ("ffn1_norm", "mix_norm", "lb_param", "hgrn_out_norm", "conv_w", "xattn_norm", "mem_norm", "ffn2_norm",
         "final_norm")
WEIGHTS = ("ffn1_norm", "ffn1_gate", "ffn1_up", "ffn1_down", "mix_norm", "w_in", "lb_param", "hgrn_out_norm",
           "conv_w", "w_out", "xattn_norm", "mem_norm", "w_q_mem", "w_kv_mem", "w_o_mem", "ffn2_norm", "ffn2_gate",
           "ffn2_up", "ffn2_down", "final_norm")


def kernel(x, mem, ffn1_norm, ffn1_gate, ffn1_up, ffn1_down, mix_norm, w_in, lb_param, hgrn_out_norm, conv_w, w_out, xattn_norm, mem_norm, w_q_mem, w_kv_mem, w_o_mem, ffn2_norm, ffn2_gate, ffn2_up, ffn2_down, final_norm, loss_target, m_ffn1_norm, m_ffn1_gate, m_ffn1_up, m_ffn1_down, m_mix_norm, m_w_in, m_lb_param, m_hgrn_out_norm, m_conv_w, m_w_out, m_xattn_norm, m_mem_norm, m_w_q_mem, m_w_kv_mem, m_w_o_mem, m_ffn2_norm, m_ffn2_gate, m_ffn2_up, m_ffn2_down, m_final_norm, v_ffn1_norm, v_ffn1_gate, v_ffn1_up, v_ffn1_down, v_mix_norm, v_w_in, v_lb_param, v_hgrn_out_norm, v_conv_w, v_w_out, v_xattn_norm, v_mem_norm, v_w_q_mem, v_w_kv_mem, v_w_o_mem, v_ffn2_norm, v_ffn2_gate, v_ffn2_up, v_ffn2_down, v_final_norm):
    given = dict(locals())
    me = 4 * lax.axis_index("x") + 2 * lax.axis_index("y") + lax.axis_index("c")
    x0, memv, target = x[0], mem[0], loss_target[0]

    def shard(prefix, name):
        v = given[prefix + name]
        if v.ndim == 1:
            return v.reshape(1, -1)
        if v.ndim == 2:
            return v
        return v[0].T if name in TRANSPOSED else v[0]

    w = {name: shard("", name) for name in WEIGHTS}
    m = {name: shard("m_", name) for name in WEIGHTS}
    v = {name: shard("v_", name) for name in WEIGHTS}

    conv_taps, conv_rows = w["conv_w"].shape
    conv_tile = jnp.pad(w["conv_w"], ((0, 8 - conv_taps), (0, 128 - conv_rows)))
    wire = {name: w[name].astype(BF16) for name in LARGE}
    full = {}

    def landed(names, gathered):
        for name, blocks in zip(names, gathered):
            _, r, c = blocks.shape
            full[name] = blocks if name == "w_kv_mem" else blocks.reshape(N_DEV * r, c)

    first = ("ffn1_gate", "ffn1_up")
    landed(first, _run_exchange(_gather_exchange([wire[k] for k in first]), "gather_first"))

    riders = (("ffn1_down", "w_in"), ("w_out", "w_kv_mem"), ("w_q_mem", "w_o_mem", "ffn2_gate", "ffn2_up"),
              ("ffn2_down",))
    (a1, b1, s1), gathered = _ffn_up(
        x0, w["ffn1_norm"], full["ffn1_gate"], full["ffn1_up"],
        exchange=_gather_exchange([wire[k] for k in riders[0]]))
    landed(riders[0], gathered)
    (x1,), gathered = _ffn_down(
        x0, s1, full["ffn1_down"], exchange=_gather_exchange([wire[k] for k in riders[1]] + [conv_tile]))
    landed(riders[1], gathered)
    convw_t = gathered[-1][:, :conv_taps, :conv_rows].transpose(1, 0, 2).reshape(conv_taps, N_DEV * conv_rows)
    (x2, z, o_raw, states, ycat), gathered = _mix_fwd(
        x1, w["mix_norm"], full["w_in"], w["lb_param"], w["hgrn_out_norm"], convw_t, full["w_out"],
        exchange=_gather_exchange([wire[k] for k in riders[2]]))
    landed(riders[2], gathered)
    kv = _memkv_fwd(memv, w["mem_norm"], full["w_kv_mem"])
    (x3, hq, qm, att), gathered = _xattn_fwd(
        x2, w["xattn_norm"], full["w_q_mem"], kv, full["w_o_mem"],
        exchange=_gather_exchange([wire[k] for k in riders[3]], middle_eighths=EARLY_MIDDLE_EIGHTHS))
    landed(riders[3], gathered)
    (dx4, a2, b2, s2, loss_part, d_final), _ = _ffn_fwd(
        x3, w["ffn2_norm"], full["ffn2_gate"], full["ffn2_up"], full["ffn2_down"], head=(w["final_norm"], target))

    parts = {}
    waiting = []

    def carried():
        names = [name for name, _ in waiting]
        exchange = _scatter_exchange([p for _, p in waiting]) if waiting else None
        del waiting[:]
        return names, exchange

    def weight_grad(name, a, b, scale=1.0):
        names, exchange = carried()
        partial, arrived = _weight_grad(a, b, scale, exchange=exchange)
        parts.update(zip(names, arrived))
        waiting.append((name, partial))

    (dx3, da2, db2, h4, d_ffn2_norm), _ = _ffn_bwd(
        x3, w["ffn2_norm"], dx4, a2, b2, full["ffn2_gate"], full["ffn2_up"], full["ffn2_down"])
    weight_grad("ffn2_down", s2, dx4, 0.5)
    weight_grad("ffn2_gate", da2, h4)
    weight_grad("ffn2_up", db2, h4)
    names, exchange = carried()
    (dx2, dqm, dkv, d_xattn_norm), arrived = _xattn_bwd(
        x2, w["xattn_norm"], dx3, qm, kv, full["w_q_mem"], full["w_o_mem"], exchange=exchange)
    parts.update(zip(names, arrived))
    d_wkv, d_mem_norm = _memkv_bwd(memv, w["mem_norm"], dkv, full["w_kv_mem"])
    waiting.append(("w_kv_mem", d_wkv))
    names, exchange = carried()
    (dx1, dz, h2, d_mix_norm, d_lbp, d_gh, d_convw_t), arrived = _mix_bwd(
        x1, w["mix_norm"], dx2, z, o_raw, states, full["w_in"], w["lb_param"], w["hgrn_out_norm"], convw_t,
        full["w_out"], exchange=exchange)
    parts.update(zip(names, arrived))
    weight_grad("w_in", dz, h2)
    weight_grad("ffn1_down", s1, dx1, 0.5)
    (dx0, da1, db1, h1, d_ffn1_norm), _ = _ffn_bwd(
        x0, w["ffn1_norm"], dx1, a1, b1, full["ffn1_gate"], full["ffn1_up"], full["ffn1_down"])
    weight_grad("ffn1_gate", da1, h1)
    weight_grad("ffn1_up", db1, h1)
    weight_grad("w_o_mem", att, dx3)
    weight_grad("w_q_mem", hq, dqm)
    weight_grad("w_out", ycat, dx2)

    small_parts = {
        "ffn1_norm": d_ffn1_norm, "mix_norm": d_mix_norm, "xattn_norm": d_xattn_norm, "mem_norm": d_mem_norm,
        "ffn2_norm": d_ffn2_norm, "final_norm": d_final, "lb_param": d_lbp, "hgrn_out_norm": d_gh,
        "conv_w": d_convw_t, "loss": loss_part,
    }
    names = [name for name, _ in waiting]
    arrived, total = _final_exchange([p for _, p in waiting], small_parts)
    parts.update(zip(names, arrived))

    g_out, d_out, m_out, v_out = {}, {}, {}, {}
    for name in LARGE:
        if name not in SHORT_SHARDS:
            g_out[name], d_out[name], m_out[name], v_out[name] = _adamw_shard(parts[name], w[name], m[name], v[name])
    g_small = {name: parts[name] for name in SHORT_SHARDS}
    for name in SMALL:
        row, nrows, ncols = SMALL_LAYOUT[name]
        g_small[name] = total[row:row + nrows, 0:ncols]
    g_small["conv_w"] = lax.dynamic_slice_in_dim(g_small["conv_w"], me * conv_rows, conv_rows, axis=1)
    together = SMALL + SHORT_SHARDS
    gs, ds, ms, vs = _adamw_small(
        [g_small[k] for k in together], [w[k] for k in together], [m[k] for k in together],
        [v[k] for k in together])
    for i, name in enumerate(together):
        g_out[name], d_out[name], m_out[name], v_out[name] = gs[i], ds[i], ms[i], vs[i]

    def shaped(value, name):
        return (value.T if name in TRANSPOSED else value).reshape(given[name].shape)

    loss = total[SMALL_LAYOUT["loss"][0], 0]
    outs = [loss, dx0.reshape(x.shape)]
    for group in (g_out, d_out, m_out, v_out):
        outs += [shaped(group[name], name) for name in WEIGHTS]
    return tuple(outs)
```

```python
import jax
import jax.numpy as jnp
from jax import lax
from jax.experimental import pallas as pl
from jax.experimental.pallas import tpu as pltpu

F32 = jnp.float32
BF16 = jnp.bfloat16
MESH_IDS = pl.DeviceIdType.MESH

N_DEV = 8
EPS = 1e-6
HGRN_HEADS = 4
HGRN_DK = 128
HGRN_W = 512
CHUNK = 64
MEM_HEADS = 4
MEM_HD = 256
ADAM_LR = 0.001
ADAM_B1 = 0.9
ADAM_B2 = 0.999
ADAM_EPS = 1e-08
ADAM_WD = 0.01
ADAM_STEP = 10

TOKEN_TILE = 256
XATTN_TILE = 512
WIDE_TILE = 512
REDUCE_TILE = 1024
ADAMW_TILE_ELEMENTS = 256 * 1024
MIDDLE_EIGHTHS = 5
EARLY_MIDDLE_EIGHTHS = 4
MXU_ROWS = 256
VMEM_LIMIT = 60 * 1024 * 1024
SMALL_ROWS = 16
NT = (((1,), (1,)), ((), ()))
TN = (((0,), (0,)), ((), ()))


def _params(sem=None):
    return pltpu.CompilerParams(dimension_semantics=sem, vmem_limit_bytes=VMEM_LIMIT)


def _dot(a, b, dims=None):
    if dims is None:
        return jnp.dot(a, b, preferred_element_type=F32)
    return lax.dot_general(a, b, dims, preferred_element_type=F32)


def _sigmoid(v):
    return 1.0 / (1.0 + jnp.exp(-v))


def _rms(x, g):
    r = lax.rsqrt(jnp.mean(x * x, axis=-1, keepdims=True) + EPS)
    xh = x * r
    return xh * g, xh, r


def _rms_bwd(dh, xh, r, g):
    dxh = dh * g
    return r * (dxh - xh * jnp.mean(dxh * xh, axis=-1, keepdims=True))


def _full(shape):
    return pl.BlockSpec(shape, lambda *_: (0,) * len(shape))


def _full_once(shape):
    return pl.BlockSpec(shape, lambda *_: (0,) * len(shape), pipeline_mode=pl.Buffered(1))


def _rows(tm, width):
    return pl.BlockSpec((tm, width), lambda i: (i, 0))


def _rows_rev(tm, width, n):
    return pl.BlockSpec((tm, width), lambda i: (n - 1 - i, 0))


def _zero_at_start(*refs):
    @pl.when(pl.program_id(0) == 0)
    def _():
        for ref in refs:
            ref[...] = jnp.zeros_like(ref)


class _Exchange:
    def __init__(self, operands, out_shapes, scratch, start, finish, middle=None, middle_eighths=MIDDLE_EIGHTHS):
        self.operands, self.out_shapes, self.scratch = list(operands), list(out_shapes), list(scratch)
        self.start, self.middle, self.finish, self.middle_eighths = start, middle, finish, middle_eighths


def _call(body, *, name, grid, in_specs, out_specs, out_shape, args, scratch_shapes=(), exchange=None):
    semantics = ("arbitrary",) * len(grid)
    if exchange is None:
        out = pl.pallas_call(
            body, name=name, grid=grid, in_specs=in_specs, out_specs=out_specs, out_shape=out_shape,
            scratch_shapes=list(scratch_shapes), compiler_params=_params(semantics))(*args)
        return out, []
    hbm = pl.BlockSpec(memory_space=pltpu.HBM)
    n_in, n_out, n_scr = len(in_specs), len(out_specs), len(scratch_shapes)
    e_in, e_out = len(exchange.operands), len(exchange.out_shapes)

    def carried(*refs):
        ins, rest = refs[:n_in], refs[n_in:]
        e_ins, rest = rest[:e_in], rest[e_in:]
        outs, rest = rest[:n_out], rest[n_out:]
        e_outs, rest = rest[:e_out], rest[e_out:]
        scr, e_scr = rest[:n_scr], rest[n_scr:]
        first = last = None
        for axis, size in enumerate(grid):
            at_start, at_end = pl.program_id(axis) == 0, pl.program_id(axis) == size - 1
            first = at_start if first is None else jnp.logical_and(first, at_start)
            last = at_end if last is None else jnp.logical_and(last, at_end)

        @pl.when(first)
        def _():
            exchange.start(e_ins, e_outs, e_scr)

        body(*ins, *outs, *scr)

        if exchange.middle is not None:
            assert len(grid) == 1

            @pl.when(pl.program_id(0) == (grid[0] * exchange.middle_eighths) // 8)
            def _():
                exchange.middle(e_ins, e_outs, e_scr)

        @pl.when(last)
        def _():
            exchange.finish(e_ins, e_outs, e_scr)

    out = pl.pallas_call(
        carried, name=name, grid=grid, in_specs=list(in_specs) + [hbm] * e_in,
        out_specs=list(out_specs) + [hbm] * e_out, out_shape=list(out_shape) + exchange.out_shapes,
        scratch_shapes=list(scratch_shapes) + exchange.scratch,
        compiler_params=pltpu.CompilerParams(
            dimension_semantics=semantics, vmem_limit_bytes=VMEM_LIMIT, has_side_effects=True),
    )(*args, *exchange.operands)
    return out[:n_out], out[n_out:]


def _run_exchange(exchange, name):
    hbm = pl.BlockSpec(memory_space=pltpu.HBM)
    e_in, e_out = len(exchange.operands), len(exchange.out_shapes)

    def body(*refs):
        e_ins, e_outs, e_scr = refs[:e_in], refs[e_in:e_in + e_out], refs[e_in + e_out:]
        exchange.start(e_ins, e_outs, e_scr)
        if exchange.middle is not None:
            exchange.middle(e_ins, e_outs, e_scr)
        exchange.finish(e_ins, e_outs, e_scr)

    return pl.pallas_call(
        body, name=name, in_specs=[hbm] * e_in, out_specs=[hbm] * e_out, out_shape=exchange.out_shapes,
        scratch_shapes=exchange.scratch, compiler_params=pltpu.CompilerParams(has_side_effects=True),
    )(*exchange.operands)


def _loss_head(xo, gf, tgt):
    d = xo.shape[1]
    y, xh, r = _rms(xo, gf)
    err = y - tgt
    dy = err * (1.0 / d)
    loss = 0.5 * jnp.sum(jnp.sum(err * err, axis=-1, keepdims=True) * (1.0 / d), axis=0, keepdims=True)
    return _rms_bwd(dy, xh, r, gf), loss, jnp.sum(dy * xh, axis=0, keepdims=True)


def _ffn_fwd(x, g, wg, wu, wd, exchange=None, head=None):
    t, d = x.shape
    f = wg.shape[0]
    tm = min(WIDE_TILE, t)

    def body(x_ref, g_ref, wg_ref, wu_ref, wd_ref, *rest):
        if head is None:
            xo_ref, a_ref, b_ref, s_ref = rest
        else:
            gf_ref, tgt_ref, xo_ref, a_ref, b_ref, s_ref, loss_ref, dgf_ref = rest
            _zero_at_start(loss_ref, dgf_ref)
        xv = x_ref[...]
        h, _, _ = _rms(xv, g_ref[...])
        hb = h.astype(BF16)
        a = _dot(hb, wg_ref[...], NT)
        b = _dot(hb, wu_ref[...], NT)
        s = (a * _sigmoid(a) * b).astype(BF16)
        xo = xv + 0.5 * _dot(s, wd_ref[...])
        if head is None:
            xo_ref[...] = xo
        else:
            xo_ref[...], loss, dgf = _loss_head(xo, gf_ref[...], tgt_ref[...])
            loss_ref[...] += jnp.broadcast_to(loss, (1, 128))
            dgf_ref[...] += dgf
        a_ref[...] = a.astype(BF16)
        b_ref[...] = b.astype(BF16)
        s_ref[...] = s

    in_specs = [_rows(tm, d), _full((1, d)), _full_once((f, d)), _full_once((f, d)), _full_once((f, d))]
    out_specs = [_rows(tm, d), _rows(tm, f), _rows(tm, f), _rows(tm, f)]
    out_shape = [
        jax.ShapeDtypeStruct((t, d), F32),
        jax.ShapeDtypeStruct((t, f), BF16),
        jax.ShapeDtypeStruct((t, f), BF16),
        jax.ShapeDtypeStruct((t, f), BF16),
    ]
    args = (x, g, wg, wu, wd)
    if head is not None:
        in_specs += [_full((1, d)), _rows(tm, d)]
        out_specs += [_full((1, 128)), _full((1, d))]
        out_shape += [jax.ShapeDtypeStruct((1, 128), F32), jax.ShapeDtypeStruct((1, d), F32)]
        args += tuple(head)
    return _call(
        body, name="ffn_fwd", grid=(t // tm,), in_specs=in_specs, out_specs=out_specs, out_shape=out_shape,
        args=args, exchange=exchange)


def _ffn_up(x, g, wu, exchange=None):
    t, d = x.shape
    f = wu.shape[0]
    tm = min(TOKEN_TILE, t)

    def body(x_ref, g_ref, wu_ref, b_ref):
        h, _, _ = _rms(x_ref[...], g_ref[...])
        b_ref[...] = _dot(h.astype(BF16), wu_ref[...], NT).astype(BF16)

    return _call(
        body, name="ffn_up", grid=(t // tm,),
        in_specs=[_rows(tm, d), _full((1, d)), _full_once((f, d))],
        out_specs=[_rows(tm, f)], out_shape=[jax.ShapeDtypeStruct((t, f), BF16)],
        args=(x, g, wu), exchange=exchange)


def _ffn_gate(x, g, b, wg, exchange=None):
    t, d = x.shape
    f = wg.shape[0]
    tm = min(TOKEN_TILE, t)

    def body(x_ref, g_ref, b_ref, wg_ref, a_ref, s_ref):
        h, _, _ = _rms(x_ref[...], g_ref[...])
        a = _dot(h.astype(BF16), wg_ref[...], NT)
        a_ref[...] = a.astype(BF16)
        s_ref[...] = (a * _sigmoid(a) * b_ref[...].astype(F32)).astype(BF16)

    return _call(
        body, name="ffn_gate", grid=(t // tm,),
        in_specs=[_rows(tm, d), _full((1, d)), _rows(tm, f), _full_once((f, d))],
        out_specs=[_rows(tm, f)] * 2, out_shape=[jax.ShapeDtypeStruct((t, f), BF16)] * 2,
        args=(x, g, b, wg), exchange=exchange)


def _ffn_down(x, s, wd, exchange=None):
    t, d = x.shape
    f = wd.shape[0]
    tm = min(TOKEN_TILE, t)

    def body(x_ref, s_ref, wd_ref, xo_ref):
        xo_ref[...] = x_ref[...] + 0.5 * _dot(s_ref[...], wd_ref[...])

    return _call(
        body, name="ffn_down", grid=(t // tm,),
        in_specs=[_rows(tm, d), _rows(tm, f), _full_once((f, d))],
        out_specs=[_rows(tm, d)], out_shape=[jax.ShapeDtypeStruct((t, d), F32)],
        args=(x, s, wd), exchange=exchange)


def _ffn_bwd(x, g, dxo, a, b, wg, wu, wd, exchange=None):
    t, d = x.shape
    f = wg.shape[0]
    tm = min(TOKEN_TILE, t)

    def body(x_ref, g_ref, dxo_ref, a_ref, b_ref, wg_ref, wu_ref, wd_ref, dx_ref, da_ref, db_ref, h_ref, dg_ref):
        _zero_at_start(dg_ref)
        gv = g_ref[...]
        h, xh, r = _rms(x_ref[...], gv)
        dxo = dxo_ref[...]
        ds = _dot((0.5 * dxo).astype(BF16), wd_ref[...], NT)
        af = a_ref[...].astype(F32)
        bf = b_ref[...].astype(F32)
        sg = _sigmoid(af)
        da = (ds * bf * (sg * (1.0 + af * (1.0 - sg)))).astype(BF16)
        db = (ds * (af * sg)).astype(BF16)
        dh = _dot(da, wg_ref[...]) + _dot(db, wu_ref[...])
        dx_ref[...] = _rms_bwd(dh, xh, r, gv) + dxo
        da_ref[...] = da
        db_ref[...] = db
        h_ref[...] = h.astype(BF16)
        dg_ref[...] += jnp.sum(dh * xh, axis=0, keepdims=True)

    return _call(
        body,
        name="ffn_bwd",
        grid=(t // tm,),
        in_specs=[
            _rows(tm, d), _full((1, d)), _rows(tm, d), _rows(tm, f), _rows(tm, f),
            _full_once((f, d)), _full_once((f, d)), _full_once((f, d)),
        ],
        out_specs=[_rows(tm, d), _rows(tm, f), _rows(tm, f), _rows(tm, d), _full((1, d))],
        out_shape=[
            jax.ShapeDtypeStruct((t, d), F32),
            jax.ShapeDtypeStruct((t, f), BF16),
            jax.ShapeDtypeStruct((t, f), BF16),
            jax.ShapeDtypeStruct((t, d), BF16),
            jax.ShapeDtypeStruct((1, d), F32),
        ],
        args=(x, g, dxo, a, b, wg, wu, wd),
        exchange=exchange,
    )


def _weight_grad(a, b, scale=1.0, exchange=None):
    t, m = a.shape
    n = b.shape[1]
    chips = N_DEV // 2
    r = m // N_DEV
    tk = min(REDUCE_TILE, t)
    halves = 2
    nb = n // halves
    nk = t // tk

    def body(a_ref, b_ref, o_ref, acc, send_buf, recv_buf, send_sems, recv_sems):
        k, j = pl.program_id(0), pl.program_id(1)
        x, y, c, _ = _mesh_place()
        sibling, _ = _peer(x, y, c, 1)
        bv = b_ref[...]
        if scale != 1.0:
            bv = bv * scale
        bb = bv.astype(BF16)
        acc_half = acc.at[j]

        @pl.when(k == 0)
        def _():
            acc_half[...] = jnp.zeros_like(acc_half)

        for i in range(m // MXU_ROWS):
            rows = slice(i * MXU_ROWS, (i + 1) * MXU_ROWS)
            acc_half[rows, :] += _dot(a_ref[:, rows].astype(BF16), bb, TN)

        def to_sibling(half):
            return _remote(send_buf.at[half], recv_buf.at[half], send_sems.at[half], recv_sems.at[half], sibling)

        def owned_rows(q, core):
            return pl.ds(pl.multiple_of((2 * q + core) * r, 8), r)

        for half in range(halves):
            @pl.when(jnp.logical_and(k == nk - 1, j == half))
            def _():
                for q in range(chips):
                    send_buf[half, q] = acc[half, owned_rows(q, 1 - c), :].astype(BF16)
                to_sibling(half).start()

        @pl.when(jnp.logical_and(k == nk - 1, j == halves - 1))
        def _():
            for half in range(halves):
                to_sibling(half).wait_send()
                to_sibling(half).wait_recv()
                for q in range(chips):
                    o_ref[q, :, half * nb:(half + 1) * nb] = (
                        acc[half, owned_rows(q, c), :] + recv_buf[half, q].astype(F32)).astype(BF16)

    (partial,), arrived = _call(
        body,
        name="weight_grad",
        grid=(nk, halves),
        in_specs=[pl.BlockSpec((tk, m), lambda k, j: (k, 0)), pl.BlockSpec((tk, nb), lambda k, j: (k, j))],
        out_specs=[pl.BlockSpec((chips, r, n), lambda k, j: (0, 0, 0))],
        out_shape=[jax.ShapeDtypeStruct((chips, r, n), BF16)],
        scratch_shapes=[
            pltpu.VMEM((halves, m, nb), F32),
            pltpu.VMEM((halves, chips, r, nb), BF16), pltpu.VMEM((halves, chips, r, nb), BF16),
            pltpu.SemaphoreType.DMA((halves,)), pltpu.SemaphoreType.DMA((halves,)),
        ],
        args=(a, b),
        exchange=exchange,
    )
    return partial, arrived


def _chunk_cumsum(v, reverse=False):
    n, width = v.shape
    row = lax.broadcasted_iota(jnp.int32, (n, n), 0)
    col = lax.broadcasted_iota(jnp.int32, (n, n), 1)
    earlier = col >= row if reverse else col <= row
    tri = jnp.where(jnp.logical_and(row // CHUNK == col // CHUNK, earlier), 1.0, 0.0).astype(BF16)
    hi = v.astype(BF16)
    rest = v - hi.astype(F32)
    mid = rest.astype(BF16)
    low = (rest - mid.astype(F32)).astype(BF16)
    sums = _dot(tri, jnp.concatenate([hi, mid, low], axis=1))
    return sums[:, 0:width] + sums[:, width:2 * width] + sums[:, 2 * width:3 * width]


def _shift_rows(v, shift, edge):
    n = v.shape[0]
    row = lax.broadcasted_iota(jnp.int32, (n, 1), 0)
    out = pltpu.roll(v, shift % n, axis=0)
    if shift > 0:
        for j in range(shift):
            out = jnp.where(row == j, edge[8 - shift + j:8 - shift + j + 1, :], out)
    else:
        for j in range(-shift):
            out = jnp.where(row == n + shift + j, edge[j:j + 1, :], out)
    return out


def _gates(z, lbp):
    w = HGRN_W
    lb = _sigmoid(lbp[0:1, :] - lbp[1:2, :])
    zq = z[:, 0:w]
    sig = _sigmoid(z[:, w:2 * w])
    f = lb + (1.0 - lb) * sig
    sq = _sigmoid(zq)
    q = zq * sq * HGRN_DK ** -0.5
    return lb, sig, f, sq, q


def _decayed_operands(q, f, v, qm_buf, km_buf, kbar_buf, v_buf, etot_buf, emid_buf):
    n, width = f.shape
    bcum = _chunk_cumsum(jnp.log(f))

    def row_of_chunk(offset):
        return jnp.concatenate(
            [jnp.broadcast_to(bcum[c + offset:c + offset + 1, :], (CHUNK, width)) for c in range(0, n, CHUNK)], axis=0)

    total, mid = row_of_chunk(CHUNK - 1), row_of_chunk(CHUNK // 2 - 1)
    em, enm, erest = jnp.exp(bcum - mid), jnp.exp(mid - bcum), jnp.exp(total - bcum)
    kk = 1.0 - f
    qm_buf[...] = (q * em).astype(BF16)
    km_buf[...] = (kk * enm).astype(BF16)
    kbar_buf[...] = (kk * erest).astype(BF16)
    v_buf[...] = v.astype(BF16)
    etot_buf[...] = jnp.exp(total)
    emid_buf[...] = jnp.exp(mid)
    return em, enm, erest


def _short_conv(u, edge, cw):
    return cw[0:1, :] * _shift_rows(u, 2, edge) + cw[1:2, :] * _shift_rows(u, 1, edge) + cw[2:3, :] * u


def _block_causal_mask(n):
    row = lax.broadcasted_iota(jnp.int32, (n, n), 0)
    col = lax.broadcasted_iota(jnp.int32, (n, n), 1)
    return jnp.logical_and(row // CHUNK == col // CHUNK, col <= row)


def _spread(v, chunk_of_row, nc):
    return jnp.concatenate([jnp.where(chunk_of_row == c, v, jnp.zeros_like(v)) for c in range(nc)], axis=1)


def _pick(r, chunk_of_row, nc):
    out = jnp.where(chunk_of_row == 0, r[:, 0:HGRN_DK], 0.0)
    for c in range(1, nc):
        out = out + jnp.where(chunk_of_row == c, r[:, c * HGRN_DK:(c + 1) * HGRN_DK], 0.0)
    return out


def _mix_fwd(x, g, w_in, lbp, gh, convw_t, w_out, exchange=None):
    t, d = x.shape
    zw = w_in.shape[0]
    w = HGRN_W
    tm = min(TOKEN_TILE, t)
    nc = tm // CHUNK
    n_chunks = t // CHUNK

    def body(x_ref, g_ref, win_ref, lbp_ref, gh_ref, cw_ref, wout_ref,
             xo_ref, z_ref, o_ref, st_ref, y_ref, state, ucarry, qm_buf, km_buf, kbar_buf, v_buf, etot_buf, emid_buf):
        _zero_at_start(state, ucarry)
        xv = x_ref[...]
        h, _, _ = _rms(xv, g_ref[...])
        z_ref[...] = _dot(h.astype(BF16), win_ref[...], NT)
        z = z_ref[...]
        _, _, f, _, q = _gates(z, lbp_ref[...])
        _decayed_operands(q, f, z[:, 2 * w:3 * w], qm_buf, km_buf, kbar_buf, v_buf, etot_buf, emid_buf)
        mask = _block_causal_mask(tm)
        chunk_of_row = lax.broadcasted_iota(jnp.int32, (tm, 1), 0) // CHUNK
        heads = range(HGRN_HEADS)
        hcols = [slice(hd * HGRN_DK, (hd + 1) * HGRN_DK) for hd in heads]
        qm = [qm_buf[:, hcols[hd]] for hd in heads]
        vb = [v_buf[:, hcols[hd]] for hd in heads]
        scores = [jnp.where(mask, _dot(qm[hd], km_buf[:, hcols[hd]], NT), 0.0).astype(BF16) for hd in heads]
        gains = [_dot(_spread(vb[hd], chunk_of_row, nc), kbar_buf[:, hcols[hd]], TN) for hd in heads]
        entering = []
        for hd in heads:
            states, st = [], state[hd]
            for c in range(nc):
                first_row = slice(c * CHUNK, c * CHUNK + 1)
                states.append(st * emid_buf[first_row, hcols[hd]])
                st_ref[c, hd] = st
                st = st * etot_buf[first_row, hcols[hd]] + gains[hd][c * HGRN_DK:(c + 1) * HGRN_DK, :]
            state[hd] = st
            entering.append(jnp.concatenate(states, axis=0).astype(BF16))
        from_states = [_dot(qm[hd], entering[hd], NT) for hd in heads]
        o_heads = [_dot(scores[hd], vb[hd]) + _pick(from_states[hd], chunk_of_row, nc) for hd in heads]
        o_ref[...] = jnp.concatenate(o_heads, axis=1)
        ghv = gh_ref[...]
        normed = jnp.concatenate([_rms(o_heads[hd], ghv[:, hcols[hd]])[0] for hd in heads], axis=1)
        zg = z[:, 3 * w:4 * w]
        u = z[:, 5 * w:6 * w] * z[:, 6 * w:7 * w]
        conv = _short_conv(u, ucarry[...], cw_ref[...])
        ucarry[...] = u[tm - 8:tm, :]
        y = jnp.concatenate([normed * (zg * _sigmoid(zg)), z[:, 4 * w:5 * w] * conv], axis=1).astype(BF16)
        y_ref[...] = y
        xo_ref[...] = xv + _dot(y, wout_ref[...])

    return _call(
        body,
        name="mix_fwd",
        grid=(t // tm,),
        in_specs=[
            _rows(tm, d), _full((1, d)), _full((zw, d)), _full((2, w)), _full((1, w)), _full((3, w)),
            _full((2 * w, d)),
        ],
        out_specs=[
            _rows(tm, d), _rows(tm, zw), _rows(tm, w),
            pl.BlockSpec((nc, HGRN_HEADS, HGRN_DK, HGRN_DK), lambda i: (i, 0, 0, 0)),
            _rows(tm, 2 * w),
        ],
        out_shape=[
            jax.ShapeDtypeStruct((t, d), F32),
            jax.ShapeDtypeStruct((t, zw), F32),
            jax.ShapeDtypeStruct((t, w), F32),
            jax.ShapeDtypeStruct((n_chunks, HGRN_HEADS, HGRN_DK, HGRN_DK), F32),
            jax.ShapeDtypeStruct((t, 2 * w), BF16),
        ],
        scratch_shapes=[
            pltpu.VMEM((HGRN_HEADS, HGRN_DK, HGRN_DK), F32), pltpu.VMEM((8, w), F32),
            pltpu.VMEM((tm, w), BF16), pltpu.VMEM((tm, w), BF16), pltpu.VMEM((tm, w), BF16),
            pltpu.VMEM((tm, w), BF16), pltpu.VMEM((tm, w), F32), pltpu.VMEM((tm, w), F32),
        ],
        args=(x, g, w_in, lbp, gh, convw_t, w_out),
        exchange=exchange,
    )


def _mix_bwd(x, g, dxo, z, o, states, w_in, lbp, gh, convw_t, w_out, exchange=None):
    t, d = x.shape
    zw = w_in.shape[0]
    w = HGRN_W
    tm = min(TOKEN_TILE, t)
    nc = tm // CHUNK
    n = t // tm

    def body(x_ref, g_ref, dxo_ref, z_ref, zprev_ref, o_ref, st_ref, win_ref, lbp_ref, gh_ref, cw_ref, wout_ref,
             dx_ref, dz_ref, h_ref, dg_ref, dlbp_ref, dgh_ref, dcw_ref,
             dstate, dcarry, do_buf, qm_buf, km_buf, kbar_buf, v_buf, etot_buf, emid_buf):
        _zero_at_start(dstate, dcarry, dg_ref, dlbp_ref, dgh_ref, dcw_ref)
        gv = g_ref[...]
        h, xh, r = _rms(x_ref[...], gv)
        h_ref[...] = h.astype(BF16)
        dxo = dxo_ref[...]
        dy = _dot(dxo.astype(BF16), wout_ref[...], NT)
        z = z_ref[...]
        lb, sig, f, sq, q = _gates(z, lbp_ref[...])
        em, enm, erest = _decayed_operands(
            q, f, z[:, 2 * w:3 * w], qm_buf, km_buf, kbar_buf, v_buf, etot_buf, emid_buf)

        ghv = gh_ref[...]
        zg = z[:, 3 * w:4 * w]
        sgz = _sigmoid(zg)
        dyh = dy[:, 0:w]
        don = dyh * (zg * sgz)
        heads = range(HGRN_HEADS)
        hcols = [slice(hd * HGRN_DK, (hd + 1) * HGRN_DK) for hd in heads]
        norms = [_rms(o_ref[:, hcols[hd]], ghv[:, hcols[hd]]) for hd in heads]
        on = jnp.concatenate([norms[hd][0] for hd in heads], axis=1)
        oh = jnp.concatenate([norms[hd][1] for hd in heads], axis=1)
        dz_ref[:, 3 * w:4 * w] = (dyh * on * (sgz * (1.0 + zg * (1.0 - sgz)))).astype(BF16)
        dgh_ref[...] += jnp.sum(don * oh, axis=0, keepdims=True)
        do_buf[...] = jnp.concatenate(
            [_rms_bwd(don[:, hcols[hd]], norms[hd][1], norms[hd][2], ghv[:, hcols[hd]]) for hd in heads],
            axis=1).astype(BF16)

        zb = z[:, 4 * w:5 * w]
        zc = z[:, 5 * w:6 * w]
        zu = z[:, 6 * w:7 * w]
        u = zc * zu
        cw = cw_ref[...]
        zp = zprev_ref[...]
        uprev = jnp.where(pl.program_id(0) == n - 1, 0.0, zp[:, 5 * w:6 * w] * zp[:, 6 * w:7 * w])
        dyc = dy[:, w:2 * w]
        dz_ref[:, 4 * w:5 * w] = (dyc * _short_conv(u, uprev, cw)).astype(BF16)
        dconv = dyc * zb
        edge = dcarry[...]
        dconv1 = _shift_rows(dconv, -1, edge)
        dconv2 = _shift_rows(dconv, -2, edge)
        dcarry[...] = dconv[0:8, :]
        du = cw[2:3, :] * dconv + cw[1:2, :] * dconv1 + cw[0:1, :] * dconv2
        dz_ref[:, 5 * w:6 * w] = (du * zu).astype(BF16)
        dz_ref[:, 6 * w:7 * w] = (du * zc).astype(BF16)
        dcw_ref[...] += jnp.concatenate([
            jnp.sum(u * dconv2, axis=0, keepdims=True),
            jnp.sum(u * dconv1, axis=0, keepdims=True),
            jnp.sum(u * dconv, axis=0, keepdims=True)], axis=0)

        mask = _block_causal_mask(tm)
        chunk_of_row = lax.broadcasted_iota(jnp.int32, (tm, 1), 0) // CHUNK
        heads = range(HGRN_HEADS)
        hcols = [slice(hd * HGRN_DK, (hd + 1) * HGRN_DK) for hd in heads]
        qmb = [qm_buf[:, hcols[hd]] for hd in heads]
        kmb = [km_buf[:, hcols[hd]] for hd in heads]
        vb = [v_buf[:, hcols[hd]] for hd in heads]
        dob = [do_buf[:, hcols[hd]] for hd in heads]
        scores = [jnp.where(mask, _dot(qmb[hd], kmb[hd], NT), 0.0).astype(BF16) for hd in heads]
        dscores = [jnp.where(mask, _dot(dob[hd], vb[hd], NT), 0.0).astype(BF16) for hd in heads]
        gains = [_dot(_spread(dob[hd], chunk_of_row, nc), qmb[hd], TN) for hd in heads]
        dst_rows, dst_lanes, st_lanes, carries = [], [], [], []
        for hd in heads:
            entering = [st_ref[c, hd] for c in range(nc)]
            emid = [emid_buf[c * CHUNK:c * CHUNK + 1, hcols[hd]] for c in range(nc)]
            leaving, carried_back = [None] * nc, [None] * nc
            dst = dstate[hd]
            for c in reversed(range(nc)):
                elast = etot_buf[c * CHUNK:c * CHUNK + 1, hcols[hd]]
                leaving[c] = dst
                carried_back[c] = jnp.sum(dst * entering[c], axis=0, keepdims=True) * elast
                dst = dst * elast + gains[hd][c * HGRN_DK:(c + 1) * HGRN_DK, :] * emid[c]
            dstate[hd] = dst
            dst_rows.append(jnp.concatenate(leaving, axis=0).astype(BF16))
            dst_lanes.append(jnp.concatenate(leaving, axis=1).astype(BF16))
            st_lanes.append(jnp.concatenate([entering[c] * emid[c] for c in range(nc)], axis=1).astype(BF16))
            carries.append(carried_back)
        dv = [_dot(scores[hd], dob[hd], TN) + _pick(_dot(kbar_buf[:, hcols[hd]], dst_rows[hd], NT), chunk_of_row, nc)
              for hd in heads]
        dz_ref[:, 2 * w:3 * w] = jnp.concatenate(dv, axis=1).astype(BF16)
        dqm = jnp.concatenate([_dot(dscores[hd], kmb[hd]) + _pick(_dot(dob[hd], st_lanes[hd]), chunk_of_row, nc)
                               for hd in heads], axis=1)
        dkm = jnp.concatenate([_dot(dscores[hd], qmb[hd], TN) for hd in heads], axis=1)
        dkbar = jnp.concatenate([_pick(_dot(vb[hd], dst_lanes[hd]), chunk_of_row, nc) for hd in heads], axis=1)

        kbar_dkbar = kbar_buf[...].astype(F32) * dkbar
        db = qm_buf[...].astype(F32) * dqm - km_buf[...].astype(F32) * dkm - kbar_dkbar
        through_last = jnp.concatenate([
            jnp.broadcast_to(
                jnp.sum(kbar_dkbar[c * CHUNK:(c + 1) * CHUNK], axis=0, keepdims=True)
                + jnp.concatenate([carries[hd][c] for hd in heads], axis=1),
                (CHUNK, w))
            for c in range(nc)], axis=0)
        dlogf = _chunk_cumsum(db, reverse=True) + through_last
        df = dlogf / f - (dkm * enm + dkbar * erest)
        zq = z[:, 0:w]
        dz_ref[:, 0:w] = (dqm * em * HGRN_DK ** -0.5 * (sq * (1.0 + zq * (1.0 - sq)))).astype(BF16)
        dz_ref[:, w:2 * w] = (df * (1.0 - lb) * sig * (1.0 - sig)).astype(BF16)
        dlb = jnp.sum(df * (1.0 - sig), axis=0, keepdims=True) * lb * (1.0 - lb)
        dlbp_ref[...] += jnp.concatenate([dlb, -dlb], axis=0)

        dh = _dot(dz_ref[...], win_ref[...])
        dx_ref[...] = _rms_bwd(dh, xh, r, gv) + dxo
        dg_ref[...] += jnp.sum(dh * xh, axis=0, keepdims=True)

    return _call(
        body,
        name="mix_bwd",
        grid=(n,),
        in_specs=[
            _rows_rev(tm, d, n), _full((1, d)), _rows_rev(tm, d, n), _rows_rev(tm, zw, n),
            pl.BlockSpec((8, zw), lambda i: (jnp.maximum((n - 1 - i) * (tm // 8) - 1, 0), 0)),
            _rows_rev(tm, w, n),
            pl.BlockSpec((nc, HGRN_HEADS, HGRN_DK, HGRN_DK), lambda i: (n - 1 - i, 0, 0, 0)),
            _full((zw, d)), _full((2, w)), _full((1, w)), _full((3, w)), _full((2 * w, d)),
        ],
        out_specs=[
            _rows_rev(tm, d, n), _rows_rev(tm, zw, n), _rows_rev(tm, d, n),
            _full((1, d)), _full((2, w)), _full((1, w)), _full((3, w)),
        ],
        out_shape=[
            jax.ShapeDtypeStruct((t, d), F32),
            jax.ShapeDtypeStruct((t, zw), BF16),
            jax.ShapeDtypeStruct((t, d), BF16),
            jax.ShapeDtypeStruct((1, d), F32),
            jax.ShapeDtypeStruct((2, w), F32),
            jax.ShapeDtypeStruct((1, w), F32),
            jax.ShapeDtypeStruct((3, w), F32),
        ],
        scratch_shapes=[
            pltpu.VMEM((HGRN_HEADS, HGRN_DK, HGRN_DK), F32), pltpu.VMEM((8, w), F32),
            pltpu.VMEM((tm, w), BF16),
            pltpu.VMEM((tm, w), BF16), pltpu.VMEM((tm, w), BF16), pltpu.VMEM((tm, w), BF16),
            pltpu.VMEM((tm, w), BF16), pltpu.VMEM((tm, w), F32), pltpu.VMEM((tm, w), F32),
        ],
        args=(x, g, dxo, z, z, o, states, w_in, lbp, gh, convw_t, w_out),
        exchange=exchange,
    )


def _memkv_fwd(mem, g, wkv):
    m, d = mem.shape
    nb, _, cb = wkv.shape

    def body(mem_ref, g_ref, wkv_ref, kv_ref):
        mn, _, _ = _rms(mem_ref[...], g_ref[...])
        mnb = mn.astype(BF16)
        for j in range(nb):
            kv_ref[:, j * cb:(j + 1) * cb] = _dot(mnb, wkv_ref[j]).astype(BF16)

    return pl.pallas_call(
        body,
        name="memkv_fwd",
        out_shape=jax.ShapeDtypeStruct((m, nb * cb), BF16),
        compiler_params=_params(),
    )(mem, g, wkv)


def _memkv_bwd(mem, g, dkv, wkv):
    m, d = mem.shape
    nb, _, cb = wkv.shape
    chips = nb // 2

    def body(mem_ref, g_ref, dkv_ref, wkv_ref, dw_ref, dg_ref, dw_all, send_buf, recv_buf, send_sem, recv_sem):
        x, y, c, _ = _mesh_place()
        sibling, _ = _peer(x, y, c, 1)
        mn, xh, _ = _rms(mem_ref[...], g_ref[...])
        mnb = mn.astype(BF16)
        dmn = jnp.zeros((m, d), F32)
        for j in range(nb):
            dkvb = dkv_ref[:, j * cb:(j + 1) * cb].astype(BF16)
            dw_all[j] = _dot(mnb, dkvb, TN)
            dmn = dmn + _dot(dkvb, wkv_ref[j], NT)
        dg_ref[...] = jnp.sum(dmn * xh, axis=0, keepdims=True)
        for q in range(chips):
            send_buf[q] = dw_all[2 * q + 1 - c].astype(BF16)
        to_sibling = _remote(send_buf, recv_buf, send_sem, recv_sem, sibling)
        to_sibling.start()
        to_sibling.wait_send()
        to_sibling.wait_recv()
        for q in range(chips):
            dw_ref[q] = (dw_all[2 * q + c] + recv_buf[q].astype(F32)).astype(BF16)

    return pl.pallas_call(
        body,
        name="memkv_bwd",
        out_shape=[jax.ShapeDtypeStruct((chips, d, cb), BF16), jax.ShapeDtypeStruct((1, d), F32)],
        scratch_shapes=[
            pltpu.VMEM((nb, d, cb), F32), pltpu.VMEM((chips, d, cb), BF16), pltpu.VMEM((chips, d, cb), BF16),
            pltpu.SemaphoreType.DMA, pltpu.SemaphoreType.DMA,
        ],
        compiler_params=_params(),
    )(mem, g, dkv, wkv)


def _softmax_rows(qm_h, k_h):
    sc = _dot(qm_h, k_h, NT) * MEM_HD ** -0.5
    e = jnp.exp(sc - jnp.max(sc, axis=-1, keepdims=True))
    return e / jnp.sum(e, axis=-1, keepdims=True)


def _xattn_fwd(x, g, wq, kv, wo, exchange=None):
    t, d = x.shape
    m = kv.shape[0]
    tm = min(XATTN_TILE, t)

    def body(x_ref, g_ref, wq_ref, kv_ref, wo_ref, xo_ref, hq_ref, qm_ref, att_ref):
        xv = x_ref[...]
        h, _, _ = _rms(xv, g_ref[...])
        hb = h.astype(BF16)
        hq_ref[...] = hb
        qm = _dot(hb, wq_ref[...]).astype(BF16)
        qm_ref[...] = qm
        heads = range(MEM_HEADS)
        kcols = [slice(hd * MEM_HD, (hd + 1) * MEM_HD) for hd in heads]
        p = [_softmax_rows(qm[:, kcols[hd]], kv_ref[:, kcols[hd]]) for hd in heads]
        att = jnp.concatenate(
            [_dot(p[hd].astype(BF16), kv_ref[:, d + hd * MEM_HD:d + (hd + 1) * MEM_HD]) for hd in heads],
            axis=1).astype(BF16)
        att_ref[...] = att
        xo_ref[...] = xv + _dot(att, wo_ref[...])

    return _call(
        body,
        name="xattn_fwd",
        grid=(t // tm,),
        in_specs=[_rows(tm, d), _full((1, d)), _full((d, d)), _full((m, 2 * d)), _full((d, d))],
        out_specs=[_rows(tm, d), _rows(tm, d), _rows(tm, d), _rows(tm, d)],
        out_shape=[
            jax.ShapeDtypeStruct((t, d), F32),
            jax.ShapeDtypeStruct((t, d), BF16),
            jax.ShapeDtypeStruct((t, d), BF16),
            jax.ShapeDtypeStruct((t, d), BF16),
        ],
        args=(x, g, wq, kv, wo),
        exchange=exchange,
    )


def _xattn_bwd(x, g, dxo, qm, kv, wq, wo, exchange=None):
    t, d = x.shape
    m = kv.shape[0]
    tm = min(XATTN_TILE, t)

    def body(x_ref, g_ref, dxo_ref, qm_ref, kv_ref, wq_ref, wo_ref, dx_ref, dqm_ref, dkv_ref, dg_ref):
        _zero_at_start(dkv_ref, dg_ref)
        gv = g_ref[...]
        _, xh, r = _rms(x_ref[...], gv)
        dxo = dxo_ref[...]
        datt = _dot(dxo.astype(BF16), wo_ref[...], NT).astype(BF16)
        heads = range(MEM_HEADS)
        kcols = [slice(hd * MEM_HD, (hd + 1) * MEM_HD) for hd in heads]
        vcols = [slice(d + hd * MEM_HD, d + (hd + 1) * MEM_HD) for hd in heads]
        qm_h = [qm_ref[:, kcols[hd]] for hd in heads]
        p = [_softmax_rows(qm_h[hd], kv_ref[:, kcols[hd]]) for hd in heads]
        dp = [_dot(datt[:, kcols[hd]], kv_ref[:, vcols[hd]], NT) for hd in heads]
        dsc = [(p[hd] * (dp[hd] - jnp.sum(p[hd] * dp[hd], axis=-1, keepdims=True)) * MEM_HD ** -0.5).astype(BF16)
               for hd in heads]
        dqm = jnp.concatenate([_dot(dsc[hd], kv_ref[:, kcols[hd]]) for hd in heads], axis=1).astype(BF16)
        dqm_ref[...] = dqm
        dkv_ref[...] += jnp.concatenate(
            [_dot(dsc[hd], qm_h[hd], TN) for hd in heads]
            + [_dot(p[hd].astype(BF16), datt[:, kcols[hd]], TN) for hd in heads], axis=1)
        dh = _dot(dqm, wq_ref[...], NT)
        dx_ref[...] = _rms_bwd(dh, xh, r, gv) + dxo
        dg_ref[...] += jnp.sum(dh * xh, axis=0, keepdims=True)

    return _call(
        body,
        name="xattn_bwd",
        grid=(t // tm,),
        in_specs=[
            _rows(tm, d), _full((1, d)), _rows(tm, d), _rows(tm, d), _full((m, 2 * d)), _full((d, d)), _full((d, d)),
        ],
        out_specs=[_rows(tm, d), _rows(tm, d), _full((m, 2 * d)), _full((1, d))],
        out_shape=[
            jax.ShapeDtypeStruct((t, d), F32),
            jax.ShapeDtypeStruct((t, d), BF16),
            jax.ShapeDtypeStruct((m, 2 * d), F32),
            jax.ShapeDtypeStruct((1, d), F32),
        ],
        args=(x, g, dxo, qm, kv, wq, wo),
        exchange=exchange,
    )


def _mesh_place():
    x, y, c = lax.axis_index("x"), lax.axis_index("y"), lax.axis_index("c")
    return x, y, c, 4 * x + 2 * y + c


def _peer(x, y, c, k):
    px = 1 - x if k & 4 else x
    py = 1 - y if k & 2 else y
    pc = 1 - c if k & 1 else c
    return (px, py, pc), 4 * px + 2 * py + pc


ICI_HOPS = (2, 4, 6)
N_HOPS = len(ICI_HOPS)


def _remote(src, dst, send_sem, recv_sem, peer):
    return pltpu.make_async_remote_copy(
        src_ref=src, dst_ref=dst, send_sem=send_sem, recv_sem=recv_sem, device_id=peer, device_id_type=MESH_IDS)


def _gather_exchange(shards, middle_eighths=MIDDLE_EIGHTHS):
    n = len(shards)

    def place():
        x, y, c, me = _mesh_place()
        sibling, _ = _peer(x, y, c, 1)
        to_x, from_x = _peer(x, y, c, 4)
        to_y, from_y = _peer(x, y, c, 2)
        _, from_diagonal = _peer(x, y, c, 6)
        onward = (c * to_y[0] + (1 - c) * to_x[0], c * to_y[1] + (1 - c) * to_x[1], c)
        passed_on = c * from_x + (1 - c) * from_y
        return me, sibling, (to_x, to_y, onward), (from_x, from_y, from_diagonal), passed_on

    def start(src, dst, sems):
        ici_send, ici_recv, pair_send, pair_recv, local = sems
        me, sibling, targets, _, _ = place()
        for a in range(n):
            pltpu.make_async_copy(src[a], dst[a].at[me], local.at[a]).start()
            for j in range(2):
                _remote(src[a], dst[a].at[me], ici_send.at[a, j], ici_recv.at[a, j], targets[j]).start()
            _remote(src[a], dst[a].at[me], pair_send.at[a, 0], pair_recv.at[a, 0], sibling).start()

    def to_sibling(dst, sems, a, j, origin, sibling):
        _, _, pair_send, pair_recv, _ = sems
        slot = dst[a].at[origin]
        return _remote(slot, slot, pair_send.at[a, 1 + j], pair_recv.at[a, 1 + j], sibling)

    def middle(src, dst, sems):
        ici_send, ici_recv, _, _, _ = sems
        _, sibling, targets, origins, passed_on = place()
        for a in range(n):
            for j in range(2):
                _remote(src[a], dst[a].at[origins[j]], ici_send.at[a, j], ici_recv.at[a, j], targets[j]).wait_recv()
            slot = dst[a].at[passed_on]
            _remote(slot, slot, ici_send.at[a, 2], ici_recv.at[a, 2], targets[2]).start()
            for j in range(2):
                to_sibling(dst, sems, a, j, origins[j], sibling).start()

    def finish(src, dst, sems):
        ici_send, ici_recv, pair_send, pair_recv, local = sems
        me, sibling, targets, origins, _ = place()
        for a in range(n):
            _remote(src[a], dst[a].at[origins[2]], ici_send.at[a, 2], ici_recv.at[a, 2], targets[2]).wait_recv()
            to_sibling(dst, sems, a, 2, origins[2], sibling).start()
        for a in range(n):
            pltpu.make_async_copy(src[a], dst[a].at[me], local.at[a]).wait()
            for j in range(N_HOPS):
                _remote(src[a], dst[a].at[me], ici_send.at[a, j], ici_recv.at[a, j], targets[j]).wait_send()
            for j, origin in enumerate((me,) + origins):
                from_sibling = origin + 1 - 2 * (origin % 2)
                passed = _remote(src[a], dst[a].at[from_sibling], pair_send.at[a, j], pair_recv.at[a, j], sibling)
                passed.wait_send()
                passed.wait_recv()

    return _Exchange(
        shards,
        [jax.ShapeDtypeStruct((N_DEV,) + s.shape, s.dtype) for s in shards],
        [
            pltpu.SemaphoreType.DMA((n, N_HOPS)), pltpu.SemaphoreType.DMA((n, N_HOPS)),
            pltpu.SemaphoreType.DMA((n, N_HOPS + 1)), pltpu.SemaphoreType.DMA((n, N_HOPS + 1)),
            pltpu.SemaphoreType.DMA((n,)),
        ],
        start, finish, middle, middle_eighths)


def _scatter_copies(src, dst, sems, n, arrivals=False):
    send, recv, local = sems
    x, y, c, _ = _mesh_place()
    chip = 2 * x + y
    if arrivals is None:
        return [pltpu.make_async_copy(src[a].at[chip], dst[a].at[chip], local.at[a]) for a in range(n)]
    copies = []
    for a in range(n):
        for j, k in enumerate(ICI_HOPS):
            peer, _ = _peer(x, y, c, k)
            peer_chip = 2 * peer[0] + peer[1]
            slot = dst[a].at[peer_chip if arrivals else chip]
            copies.append(_remote(src[a].at[peer_chip], slot, send.at[a, j], recv.at[a, j], peer))
    return copies


def _scatter_start(src, dst, sems, n):
    for cp in _scatter_copies(src, dst, sems, n, arrivals=None) + _scatter_copies(src, dst, sems, n):
        cp.start()


def _scatter_finish(src, dst, sems, n):
    for cp in _scatter_copies(src, dst, sems, n, arrivals=None):
        cp.wait()
    for cp in _scatter_copies(src, dst, sems, n):
        cp.wait_send()
    for cp in _scatter_copies(src, dst, sems, n, arrivals=True):
        cp.wait_recv()


def _scatter_scratch(n):
    return [pltpu.SemaphoreType.DMA((n, N_HOPS)), pltpu.SemaphoreType.DMA((n, N_HOPS)), pltpu.SemaphoreType.DMA((n,))]


def _scatter_exchange(partials):
    n = len(partials)
    return _Exchange(
        partials, [jax.ShapeDtypeStruct(p.shape, p.dtype) for p in partials], _scatter_scratch(n),
        lambda src, dst, sems: _scatter_start(src, dst, sems, n),
        lambda src, dst, sems: _scatter_finish(src, dst, sems, n))


SMALL_LAYOUT = {
    "ffn1_norm": (0, 1, 1024), "mix_norm": (1, 1, 1024), "xattn_norm": (2, 1, 1024), "mem_norm": (3, 1, 1024),
    "ffn2_norm": (4, 1, 1024), "final_norm": (5, 1, 1024), "lb_param": (6, 2, 512), "hgrn_out_norm": (8, 1, 512),
    "conv_w": (9, 3, 512), "loss": (12, 1, 128),
}


def _final_exchange(partials, small):
    n = len(partials)
    names = list(small)
    width = 1024

    def body(*refs):
        src = refs[:n]
        pieces = refs[n:n + len(names)]
        dst = refs[n + len(names):2 * n + len(names)]
        total_ref = refs[2 * n + len(names)]
        pack, gathered, small_send, small_recv = refs[2 * n + len(names) + 1:2 * n + len(names) + 5]
        sems = refs[2 * n + len(names) + 5:]
        x, y, c, me = _mesh_place()
        pack[...] = jnp.zeros_like(pack)
        for name, piece in zip(names, pieces):
            row, nrows, ncols = SMALL_LAYOUT[name]
            pack[row:row + nrows, 0:ncols] = piece[...]
        for k in range(1, N_DEV):
            peer, _ = _peer(x, y, c, k)
            _remote(pack, gathered.at[me], small_send.at[k - 1], small_recv.at[k - 1], peer).start()
        _scatter_start(src, dst, sems, n)
        gathered[me] = pack[...]
        for k in range(1, N_DEV):
            peer, peer_index = _peer(x, y, c, k)
            landed = _remote(pack, gathered.at[peer_index], small_send.at[k - 1], small_recv.at[k - 1], peer)
            landed.wait_send()
            landed.wait_recv()
        total = gathered[0]
        for j in range(1, N_DEV):
            total = total + gathered[j]
        total_ref[...] = total
        _scatter_finish(src, dst, sems, n)

    hbm = pl.BlockSpec(memory_space=pltpu.HBM)
    vmem = pl.BlockSpec(memory_space=pltpu.VMEM)
    out = pl.pallas_call(
        body,
        name="final_exchange",
        in_specs=[hbm] * n + [vmem] * len(names),
        out_specs=[hbm] * n + [vmem],
        out_shape=[jax.ShapeDtypeStruct(p.shape, p.dtype) for p in partials]
        + [jax.ShapeDtypeStruct((SMALL_ROWS, width), F32)],
        scratch_shapes=[
            pltpu.VMEM((SMALL_ROWS, width), F32), pltpu.VMEM((N_DEV, SMALL_ROWS, width), F32),
            pltpu.SemaphoreType.DMA((N_DEV - 1,)), pltpu.SemaphoreType.DMA((N_DEV - 1,)),
        ] + _scatter_scratch(n),
        compiler_params=pltpu.CompilerParams(has_side_effects=True),
    )(*partials, *[small[k] for k in names])
    return out[:n], out[n]


def _adamw_math(w, g, m, v):
    m = ADAM_B1 * m + (1.0 - ADAM_B1) * g
    v = ADAM_B2 * v + (1.0 - ADAM_B2) * (g * g)
    m_hat = m / (1.0 - ADAM_B1 ** ADAM_STEP)
    v_hat = v / (1.0 - ADAM_B2 ** ADAM_STEP)
    delta = -ADAM_LR * (m_hat / (jnp.sqrt(v_hat) + ADAM_EPS) + ADAM_WD * w)
    return delta, m, v


def _adamw_shard(parts, w, m, v):
    r, c = w.shape
    n_parts = parts.shape[0]
    tr = max(rows for rows in range(16, r + 1, 16) if r % rows == 0 and rows * c <= ADAMW_TILE_ELEMENTS)

    def body(p_ref, w_ref, m_ref, v_ref, g_ref, d_ref, mo_ref, vo_ref):
        g = p_ref[0].astype(F32)
        for j in range(1, n_parts):
            g = g + p_ref[j].astype(F32)
        delta, mn, vn = _adamw_math(w_ref[...], g, m_ref[...], v_ref[...])
        g_ref[...] = g
        d_ref[...] = delta
        mo_ref[...] = mn
        vo_ref[...] = vn

    tile = pl.BlockSpec((tr, c), lambda i: (i, 0))
    return pl.pallas_call(
        body,
        name="adamw_shard",
        grid=(r // tr,),
        in_specs=[pl.BlockSpec((n_parts, tr, c), lambda i: (0, i, 0)), tile, tile, tile],
        out_specs=[tile] * 4,
        out_shape=[jax.ShapeDtypeStruct((r, c), F32)] * 4,
        compiler_params=_params(("parallel",)),
    )(parts, w, m, v)


def _adamw_small(gs, ws, ms, vs):
    n = len(gs)

    def body(*refs):
        g_refs, w_refs, m_refs, v_refs = refs[:n], refs[n:2 * n], refs[2 * n:3 * n], refs[3 * n:4 * n]
        g_out, d_out, m_out, v_out = refs[4 * n:5 * n], refs[5 * n:6 * n], refs[6 * n:7 * n], refs[7 * n:8 * n]
        for i in range(n):
            if gs[i].ndim == ws[i].ndim:
                g = g_refs[i][...]
            else:
                g = g_refs[i][0].astype(F32)
                for j in range(1, gs[i].shape[0]):
                    g = g + g_refs[i][j].astype(F32)
            delta, mn, vn = _adamw_math(w_refs[i][...], g, m_refs[i][...], v_refs[i][...])
            g_out[i][...] = g
            d_out[i][...] = delta
            m_out[i][...] = mn
            v_out[i][...] = vn

    shapes = [jax.ShapeDtypeStruct(w.shape, F32) for w in ws]
    out = pl.pallas_call(
        body,
        name="adamw_small",
        out_shape=shapes * 4,
        compiler_params=_params(),
    )(*gs, *ws, *ms, *vs)
    return out[:n], out[n:2 * n], out[2 * n:3 * n], out[3 * n:]


TRANSPOSED = ("ffn1_gate", "ffn1_up", "w_in", "ffn2_gate", "ffn2_up", "conv_w")
GROUP_FFN1 = ("ffn1_gate", "ffn1_up", "ffn1_down")
GROUP_MIX = ("w_in", "w_out")
GROUP_XATTN = ("w_q_mem", "w_kv_mem", "w_o_mem")
GROUP_FFN2 = ("ffn2_gate", "ffn2_up", "ffn2_down")
LARGE = GROUP_FFN1 + GROUP_MIX + GROUP_XATTN + GROUP_FFN2
SHORT_SHARDS = ("w_out", "w_q_mem", "w_kv_mem", "w_o_mem")
SMALL = ("ffn1_norm", "mix_norm", "lb_param", "hgrn_out_norm", "conv_w", "xattn_norm", "mem_norm", "ffn2_norm",
         "final_norm")
WEIGHTS = ("ffn1_norm", "ffn1_gate", "ffn1_up", "ffn1_down", "mix_norm", "w_in", "lb_param", "hgrn_out_norm",
           "conv_w", "w_out", "xattn_norm", "mem_norm", "w_q_mem", "w_kv_mem", "w_o_mem", "ffn2_norm", "ffn2_gate",
           "ffn2_up", "ffn2_down", "final_norm")


def kernel(x, mem, ffn1_norm, ffn1_gate, ffn1_up, ffn1_down, mix_norm, w_in, lb_param, hgrn_out_norm, conv_w, w_out, xattn_norm, mem_norm, w_q_mem, w_kv_mem, w_o_mem, ffn2_norm, ffn2_gate, ffn2_up, ffn2_down, final_norm, loss_target, m_ffn1_norm, m_ffn1_gate, m_ffn1_up, m_ffn1_down, m_mix_norm, m_w_in, m_lb_param, m_hgrn_out_norm, m_conv_w, m_w_out, m_xattn_norm, m_mem_norm, m_w_q_mem, m_w_kv_mem, m_w_o_mem, m_ffn2_norm, m_ffn2_gate, m_ffn2_up, m_ffn2_down, m_final_norm, v_ffn1_norm, v_ffn1_gate, v_ffn1_up, v_ffn1_down, v_mix_norm, v_w_in, v_lb_param, v_hgrn_out_norm, v_conv_w, v_w_out, v_xattn_norm, v_mem_norm, v_w_q_mem, v_w_kv_mem, v_w_o_mem, v_ffn2_norm, v_ffn2_gate, v_ffn2_up, v_ffn2_down, v_final_norm):
    given = dict(locals())
    me = 4 * lax.axis_index("x") + 2 * lax.axis_index("y") + lax.axis_index("c")
    x0, memv, target = x[0], mem[0], loss_target[0]

    def shard(prefix, name):
        v = given[prefix + name]
        if v.ndim == 1:
            return v.reshape(1, -1)
        if v.ndim == 2:
            return v
        return v[0].T if name in TRANSPOSED else v[0]

    w = {name: shard("", name) for name in WEIGHTS}
    m = {name: shard("m_", name) for name in WEIGHTS}
    v = {name: shard("v_", name) for name in WEIGHTS}

    conv_taps, conv_rows = w["conv_w"].shape
    conv_tile = jnp.pad(w["conv_w"], ((0, 8 - conv_taps), (0, 128 - conv_rows)))
    wire = {name: w[name].astype(BF16) for name in LARGE}
    full = {}

    def landed(names, gathered):
        for name, blocks in zip(names, gathered):
            _, r, c = blocks.shape
            full[name] = blocks if name == "w_kv_mem" else blocks.reshape(N_DEV * r, c)

    first = ("ffn1_up",)
    landed(first, _run_exchange(_gather_exchange([wire[k] for k in first]), "gather_first"))

    riders = (("ffn1_gate",), ("ffn1_down", "w_out"), ("w_in", "w_kv_mem"),
              ("w_q_mem", "w_o_mem", "ffn2_gate", "ffn2_up"), ("ffn2_down",))
    (b1,), gathered = _ffn_up(
        x0, w["ffn1_norm"], full["ffn1_up"], exchange=_gather_exchange([wire[k] for k in riders[0]]))
    landed(riders[0], gathered)
    (a1, s1), gathered = _ffn_gate(
        x0, w["ffn1_norm"], b1, full["ffn1_gate"], exchange=_gather_exchange([wire[k] for k in riders[1]]))
    landed(riders[1], gathered)
    (x1,), gathered = _ffn_down(
        x0, s1, full["ffn1_down"], exchange=_gather_exchange([wire[k] for k in riders[2]] + [conv_tile]))
    landed(riders[2], gathered)
    convw_t = gathered[-1][:, :conv_taps, :conv_rows].transpose(1, 0, 2).reshape(conv_taps, N_DEV * conv_rows)
    (x2, z, o_raw, states, ycat), gathered = _mix_fwd(
        x1, w["mix_norm"], full["w_in"], w["lb_param"], w["hgrn_out_norm"], convw_t, full["w_out"],
        exchange=_gather_exchange([wire[k] for k in riders[3]]))
    landed(riders[3], gathered)
    kv = _memkv_fwd(memv, w["mem_norm"], full["w_kv_mem"])
    (x3, hq, qm, att), gathered = _xattn_fwd(
        x2, w["xattn_norm"], full["w_q_mem"], kv, full["w_o_mem"],
        exchange=_gather_exchange([wire[k] for k in riders[4]], middle_eighths=EARLY_MIDDLE_EIGHTHS))
    landed(riders[4], gathered)
    (dx4, a2, b2, s2, loss_part, d_final), _ = _ffn_fwd(
        x3, w["ffn2_norm"], full["ffn2_gate"], full["ffn2_up"], full["ffn2_down"], head=(w["final_norm"], target))

    parts = {}
    waiting = []

    def carried():
        names = [name for name, _ in waiting]
        exchange = _scatter_exchange([p for _, p in waiting]) if waiting else None
        del waiting[:]
        return names, exchange

    def weight_grad(name, a, b, scale=1.0):
        names, exchange = carried()
        partial, arrived = _weight_grad(a, b, scale, exchange=exchange)
        parts.update(zip(names, arrived))
        waiting.append((name, partial))

    (dx3, da2, db2, h4, d_ffn2_norm), _ = _ffn_bwd(
        x3, w["ffn2_norm"], dx4, a2, b2, full["ffn2_gate"], full["ffn2_up"], full["ffn2_down"])
    weight_grad("ffn2_down", s2, dx4, 0.5)
    weight_grad("ffn2_gate", da2, h4)
    weight_grad("ffn2_up", db2, h4)
    names, exchange = carried()
    (dx2, dqm, dkv, d_xattn_norm), arrived = _xattn_bwd(
        x2, w["xattn_norm"], dx3, qm, kv, full["w_q_mem"], full["w_o_mem"], exchange=exchange)
    parts.update(zip(names, arrived))
    d_wkv, d_mem_norm = _memkv_bwd(memv, w["mem_norm"], dkv, full["w_kv_mem"])
    waiting.append(("w_kv_mem", d_wkv))
    names, exchange = carried()
    (dx1, dz, h2, d_mix_norm, d_lbp, d_gh, d_convw_t), arrived = _mix_bwd(
        x1, w["mix_norm"], dx2, z, o_raw, states, full["w_in"], w["lb_param"], w["hgrn_out_norm"], convw_t,
        full["w_out"], exchange=exchange)
    parts.update(zip(names, arrived))
    weight_grad("w_in", dz, h2)
    weight_grad("ffn1_down", s1, dx1, 0.5)
    (dx0, da1, db1, h1, d_ffn1_norm), _ = _ffn_bwd(
        x0, w["ffn1_norm"], dx1, a1, b1, full["ffn1_gate"], full["ffn1_up"], full["ffn1_down"])
    weight_grad("ffn1_gate", da1, h1)
    weight_grad("ffn1_up", db1, h1)
    weight_grad("w_o_mem", att, dx3)
    weight_grad("w_q_mem", hq, dqm)
    weight_grad("w_out", ycat, dx2)

    small_parts = {
        "ffn1_norm": d_ffn1_norm, "mix_norm": d_mix_norm, "xattn_norm": d_xattn_norm, "mem_norm": d_mem_norm,
        "ffn2_norm": d_ffn2_norm, "final_norm": d_final, "lb_param": d_lbp, "hgrn_out_norm": d_gh,
        "conv_w": d_convw_t, "loss": loss_part,
    }
    names = [name for name, _ in waiting]
    arrived, total = _final_exchange([p for _, p in waiting], small_parts)
    parts.update(zip(names, arrived))

    g_out, d_out, m_out, v_out = {}, {}, {}, {}
    for name in LARGE:
        if name not in SHORT_SHARDS:
            g_out[name], d_out[name], m_out[name], v_out[name] = _adamw_shard(parts[name], w[name], m[name], v[name])
    g_small = {name: parts[name] for name in SHORT_SHARDS}
    for name in SMALL:
        row, nrows, ncols = SMALL_LAYOUT[name]
        g_small[name] = total[row:row + nrows, 0:ncols]
    g_small["conv_w"] = lax.dynamic_slice_in_dim(g_small["conv_w"], me * conv_rows, conv_rows, axis=1)
    together = SMALL + SHORT_SHARDS
    gs, ds, ms, vs = _adamw_small(
        [g_small[k] for k in together], [w[k] for k in together], [m[k] for k in together],
        [v[k] for k in together])
    for i, name in enumerate(together):
        g_out[name], d_out[name], m_out[name], v_out[name] = gs[i], ds[i], ms[i], vs[i]

    def shaped(value, name):
        return (value.T if name in TRANSPOSED else value).reshape(given[name].shape)

    loss = total[SMALL_LAYOUT["loss"][0], 0]
    outs = [loss, dx0.reshape(x.shape)]
    for group in (g_out, d_out, m_out, v_out):
        outs += [shaped(group[name], name) for name in WEIGHTS]
    return tuple(outs)
```

```python
import jax
import jax.numpy as jnp
from jax import lax
from jax.experimental import pallas as pl
from jax.experimental.pallas import tpu as pltpu

F32 = jnp.float32
BF16 = jnp.bfloat16
MESH_IDS = pl.DeviceIdType.MESH

N_DEV = 8
EPS = 1e-6
HGRN_HEADS = 4
HGRN_DK = 128
HGRN_W = 512
CHUNK = 64
MEM_HEADS = 4
MEM_HD = 256
ADAM_LR = 0.001
ADAM_B1 = 0.9
ADAM_B2 = 0.999
ADAM_EPS = 1e-08
ADAM_WD = 0.01
ADAM_STEP = 10

TOKEN_TILE = 256
XATTN_TILE = 512
WIDE_TILE = 512
REDUCE_TILE = 1024
ADAMW_TILE_ELEMENTS = 256 * 1024
MIDDLE_EIGHTHS = 5
EARLY_MIDDLE_EIGHTHS = 4
MXU_ROWS = 256
VMEM_LIMIT = 60 * 1024 * 1024
SMALL_ROWS = 16
NT = (((1,), (1,)), ((), ()))
TN = (((0,), (0,)), ((), ()))


def _params(sem=None):
    return pltpu.CompilerParams(dimension_semantics=sem, vmem_limit_bytes=VMEM_LIMIT)


def _dot(a, b, dims=None):
    if dims is None:
        return jnp.dot(a, b, preferred_element_type=F32)
    return lax.dot_general(a, b, dims, preferred_element_type=F32)


def _sigmoid(v):
    return 1.0 / (1.0 + jnp.exp(-v))


def _rms(x, g):
    r = lax.rsqrt(jnp.mean(x * x, axis=-1, keepdims=True) + EPS)
    xh = x * r
    return xh * g, xh, r


def _rms_bwd(dh, xh, r, g):
    dxh = dh * g
    return r * (dxh - xh * jnp.mean(dxh * xh, axis=-1, keepdims=True))


def _full(shape):
    return pl.BlockSpec(shape, lambda *_: (0,) * len(shape))


def _full_once(shape):
    return pl.BlockSpec(shape, lambda *_: (0,) * len(shape), pipeline_mode=pl.Buffered(1))


def _rows(tm, width):
    return pl.BlockSpec((tm, width), lambda i: (i, 0))


def _rows_rev(tm, width, n):
    return pl.BlockSpec((tm, width), lambda i: (n - 1 - i, 0))


def _zero_at_start(*refs):
    @pl.when(pl.program_id(0) == 0)
    def _():
        for ref in refs:
            ref[...] = jnp.zeros_like(ref)


class _Exchange:
    def __init__(self, operands, out_shapes, scratch, start, finish, middle=None, middle_eighths=MIDDLE_EIGHTHS):
        self.operands, self.out_shapes, self.scratch = list(operands), list(out_shapes), list(scratch)
        self.start, self.middle, self.finish, self.middle_eighths = start, middle, finish, middle_eighths


def _call(body, *, name, grid, in_specs, out_specs, out_shape, args, scratch_shapes=(), exchange=None):
    semantics = ("arbitrary",) * len(grid)
    if exchange is None:
        out = pl.pallas_call(
            body, name=name, grid=grid, in_specs=in_specs, out_specs=out_specs, out_shape=out_shape,
            scratch_shapes=list(scratch_shapes), compiler_params=_params(semantics))(*args)
        return out, []
    hbm = pl.BlockSpec(memory_space=pltpu.HBM)
    n_in, n_out, n_scr = len(in_specs), len(out_specs), len(scratch_shapes)
    e_in, e_out = len(exchange.operands), len(exchange.out_shapes)

    def carried(*refs):
        ins, rest = refs[:n_in], refs[n_in:]
        e_ins, rest = rest[:e_in], rest[e_in:]
        outs, rest = rest[:n_out], rest[n_out:]
        e_outs, rest = rest[:e_out], rest[e_out:]
        scr, e_scr = rest[:n_scr], rest[n_scr:]
        first = last = None
        for axis, size in enumerate(grid):
            at_start, at_end = pl.program_id(axis) == 0, pl.program_id(axis) == size - 1
            first = at_start if first is None else jnp.logical_and(first, at_start)
            last = at_end if last is None else jnp.logical_and(last, at_end)

        @pl.when(first)
        def _():
            exchange.start(e_ins, e_outs, e_scr)

        body(*ins, *outs, *scr)

        if exchange.middle is not None:
            assert len(grid) == 1

            @pl.when(pl.program_id(0) == (grid[0] * exchange.middle_eighths) // 8)
            def _():
                exchange.middle(e_ins, e_outs, e_scr)

        @pl.when(last)
        def _():
            exchange.finish(e_ins, e_outs, e_scr)

    out = pl.pallas_call(
        carried, name=name, grid=grid, in_specs=list(in_specs) + [hbm] * e_in,
        out_specs=list(out_specs) + [hbm] * e_out, out_shape=list(out_shape) + exchange.out_shapes,
        scratch_shapes=list(scratch_shapes) + exchange.scratch,
        compiler_params=pltpu.CompilerParams(
            dimension_semantics=semantics, vmem_limit_bytes=VMEM_LIMIT, has_side_effects=True),
    )(*args, *exchange.operands)
    return out[:n_out], out[n_out:]


def _run_exchange(exchange, name):
    hbm = pl.BlockSpec(memory_space=pltpu.HBM)
    e_in, e_out = len(exchange.operands), len(exchange.out_shapes)

    def body(*refs):
        e_ins, e_outs, e_scr = refs[:e_in], refs[e_in:e_in + e_out], refs[e_in + e_out:]
        exchange.start(e_ins, e_outs, e_scr)
        if exchange.middle is not None:
            exchange.middle(e_ins, e_outs, e_scr)
        exchange.finish(e_ins, e_outs, e_scr)

    return pl.pallas_call(
        body, name=name, in_specs=[hbm] * e_in, out_specs=[hbm] * e_out, out_shape=exchange.out_shapes,
        scratch_shapes=exchange.scratch, compiler_params=pltpu.CompilerParams(has_side_effects=True),
    )(*exchange.operands)


def _loss_head(xo, gf, tgt):
    d = xo.shape[1]
    y, xh, r = _rms(xo, gf)
    err = y - tgt
    dy = err * (1.0 / d)
    loss = 0.5 * jnp.sum(jnp.sum(err * err, axis=-1, keepdims=True) * (1.0 / d), axis=0, keepdims=True)
    return _rms_bwd(dy, xh, r, gf), loss, jnp.sum(dy * xh, axis=0, keepdims=True)


def _ffn_fwd(x, g, wg, wu, wd, exchange=None, head=None):
    t, d = x.shape
    f = wg.shape[0]
    tm = min(WIDE_TILE, t)

    def body(x_ref, g_ref, wg_ref, wu_ref, wd_ref, *rest):
        if head is None:
            xo_ref, a_ref, b_ref, s_ref = rest
        else:
            gf_ref, tgt_ref, xo_ref, a_ref, b_ref, s_ref, loss_ref, dgf_ref = rest
            _zero_at_start(loss_ref, dgf_ref)
        xv = x_ref[...]
        h, _, _ = _rms(xv, g_ref[...])
        hb = h.astype(BF16)
        a = _dot(hb, wg_ref[...], NT)
        b = _dot(hb, wu_ref[...], NT)
        s = (a * _sigmoid(a) * b).astype(BF16)
        xo = xv + 0.5 * _dot(s, wd_ref[...])
        if head is None:
            xo_ref[...] = xo
        else:
            xo_ref[...], loss, dgf = _loss_head(xo, gf_ref[...], tgt_ref[...])
            loss_ref[...] += jnp.broadcast_to(loss, (1, 128))
            dgf_ref[...] += dgf
        a_ref[...] = a.astype(BF16)
        b_ref[...] = b.astype(BF16)
        s_ref[...] = s

    in_specs = [_rows(tm, d), _full((1, d)), _full_once((f, d)), _full_once((f, d)), _full_once((f, d))]
    out_specs = [_rows(tm, d), _rows(tm, f), _rows(tm, f), _rows(tm, f)]
    out_shape = [
        jax.ShapeDtypeStruct((t, d), F32),
        jax.ShapeDtypeStruct((t, f), BF16),
        jax.ShapeDtypeStruct((t, f), BF16),
        jax.ShapeDtypeStruct((t, f), BF16),
    ]
    args = (x, g, wg, wu, wd)
    if head is not None:
        in_specs += [_full((1, d)), _rows(tm, d)]
        out_specs += [_full((1, 128)), _full((1, d))]
        out_shape += [jax.ShapeDtypeStruct((1, 128), F32), jax.ShapeDtypeStruct((1, d), F32)]
        args += tuple(head)
    return _call(
        body, name="ffn_fwd", grid=(t // tm,), in_specs=in_specs, out_specs=out_specs, out_shape=out_shape,
        args=args, exchange=exchange)


def _ffn_up(x, g, wg, wu, exchange=None):
    t, d = x.shape
    f = wg.shape[0]
    tm = min(TOKEN_TILE, t)

    def body(x_ref, g_ref, wg_ref, wu_ref, a_ref, b_ref, s_ref):
        h, _, _ = _rms(x_ref[...], g_ref[...])
        hb = h.astype(BF16)
        a = _dot(hb, wg_ref[...], NT)
        b = _dot(hb, wu_ref[...], NT)
        a_ref[...] = a.astype(BF16)
        b_ref[...] = b.astype(BF16)
        s_ref[...] = (a * _sigmoid(a) * b).astype(BF16)

    return _call(
        body, name="ffn_up", grid=(t // tm,),
        in_specs=[_rows(tm, d), _full((1, d)), _full_once((f, d)), _full_once((f, d))],
        out_specs=[_rows(tm, f)] * 3, out_shape=[jax.ShapeDtypeStruct((t, f), BF16)] * 3,
        args=(x, g, wg, wu), exchange=exchange)


def _ffn_down(x, s, wd, exchange=None):
    t, d = x.shape
    f = wd.shape[0]
    tm = min(TOKEN_TILE, t)

    def body(x_ref, s_ref, wd_ref, xo_ref):
        xo_ref[...] = x_ref[...] + 0.5 * _dot(s_ref[...], wd_ref[...])

    return _call(
        body, name="ffn_down", grid=(t // tm,),
        in_specs=[_rows(tm, d), _rows(tm, f), _full_once((f, d))],
        out_specs=[_rows(tm, d)], out_shape=[jax.ShapeDtypeStruct((t, d), F32)],
        args=(x, s, wd), exchange=exchange)


def _ffn_bwd(x, g, dxo, a, b, wg, wu, wd, exchange=None):
    t, d = x.shape
    f = wg.shape[0]
    tm = min(TOKEN_TILE, t)

    def body(x_ref, g_ref, dxo_ref, a_ref, b_ref, wg_ref, wu_ref, wd_ref, dx_ref, da_ref, db_ref, h_ref, dg_ref):
        _zero_at_start(dg_ref)
        gv = g_ref[...]
        h, xh, r = _rms(x_ref[...], gv)
        dxo = dxo_ref[...]
        ds = _dot((0.5 * dxo).astype(BF16), wd_ref[...], NT)
        af = a_ref[...].astype(F32)
        bf = b_ref[...].astype(F32)
        sg = _sigmoid(af)
        da = (ds * bf * (sg * (1.0 + af * (1.0 - sg)))).astype(BF16)
        db = (ds * (af * sg)).astype(BF16)
        dh = _dot(da, wg_ref[...]) + _dot(db, wu_ref[...])
        dx_ref[...] = _rms_bwd(dh, xh, r, gv) + dxo
        da_ref[...] = da
        db_ref[...] = db
        h_ref[...] = h.astype(BF16)
        dg_ref[...] += jnp.sum(dh * xh, axis=0, keepdims=True)

    return _call(
        body,
        name="ffn_bwd",
        grid=(t // tm,),
        in_specs=[
            _rows(tm, d), _full((1, d)), _rows(tm, d), _rows(tm, f), _rows(tm, f),
            _full_once((f, d)), _full_once((f, d)), _full_once((f, d)),
        ],
        out_specs=[_rows(tm, d), _rows(tm, f), _rows(tm, f), _rows(tm, d), _full((1, d))],
        out_shape=[
            jax.ShapeDtypeStruct((t, d), F32),
            jax.ShapeDtypeStruct((t, f), BF16),
            jax.ShapeDtypeStruct((t, f), BF16),
            jax.ShapeDtypeStruct((t, d), BF16),
            jax.ShapeDtypeStruct((1, d), F32),
        ],
        args=(x, g, dxo, a, b, wg, wu, wd),
        exchange=exchange,
    )


def _weight_grad(a, b, scale=1.0, exchange=None):
    t, m = a.shape
    n = b.shape[1]
    chips = N_DEV // 2
    r = m // N_DEV
    tk = min(REDUCE_TILE, t)
    halves = 2
    nb = n // halves
    nk = t // tk

    def body(a_ref, b_ref, o_ref, acc, send_buf, recv_buf, send_sems, recv_sems):
        k, j = pl.program_id(0), pl.program_id(1)
        x, y, c, _ = _mesh_place()
        sibling, _ = _peer(x, y, c, 1)
        bv = b_ref[...]
        if scale != 1.0:
            bv = bv * scale
        bb = bv.astype(BF16)
        acc_half = acc.at[j]

        @pl.when(k == 0)
        def _():
            acc_half[...] = jnp.zeros_like(acc_half)

        for i in range(m // MXU_ROWS):
            rows = slice(i * MXU_ROWS, (i + 1) * MXU_ROWS)
            acc_half[rows, :] += _dot(a_ref[:, rows].astype(BF16), bb, TN)

        def to_sibling(half):
            return _remote(send_buf.at[half], recv_buf.at[half], send_sems.at[half], recv_sems.at[half], sibling)

        def owned_rows(q, core):
            return pl.ds(pl.multiple_of((2 * q + core) * r, 8), r)

        for half in range(halves):
            @pl.when(jnp.logical_and(k == nk - 1, j == half))
            def _():
                for q in range(chips):
                    send_buf[half, q] = acc[half, owned_rows(q, 1 - c), :].astype(BF16)
                to_sibling(half).start()

        @pl.when(jnp.logical_and(k == nk - 1, j == halves - 1))
        def _():
            for half in range(halves):
                to_sibling(half).wait_send()
                to_sibling(half).wait_recv()
                for q in range(chips):
                    o_ref[q, :, half * nb:(half + 1) * nb] = (
                        acc[half, owned_rows(q, c), :] + recv_buf[half, q].astype(F32)).astype(BF16)

    (partial,), arrived = _call(
        body,
        name="weight_grad",
        grid=(nk, halves),
        in_specs=[pl.BlockSpec((tk, m), lambda k, j: (k, 0)), pl.BlockSpec((tk, nb), lambda k, j: (k, j))],
        out_specs=[pl.BlockSpec((chips, r, n), lambda k, j: (0, 0, 0))],
        out_shape=[jax.ShapeDtypeStruct((chips, r, n), BF16)],
        scratch_shapes=[
            pltpu.VMEM((halves, m, nb), F32),
            pltpu.VMEM((halves, chips, r, nb), BF16), pltpu.VMEM((halves, chips, r, nb), BF16),
            pltpu.SemaphoreType.DMA((halves,)), pltpu.SemaphoreType.DMA((halves,)),
        ],
        args=(a, b),
        exchange=exchange,
    )
    return partial, arrived


def _chunk_cumsum(v, reverse=False):
    n, width = v.shape
    row = lax.broadcasted_iota(jnp.int32, (n, n), 0)
    col = lax.broadcasted_iota(jnp.int32, (n, n), 1)
    earlier = col >= row if reverse else col <= row
    tri = jnp.where(jnp.logical_and(row // CHUNK == col // CHUNK, earlier), 1.0, 0.0).astype(BF16)
    hi = v.astype(BF16)
    rest = v - hi.astype(F32)
    mid = rest.astype(BF16)
    low = (rest - mid.astype(F32)).astype(BF16)
    sums = _dot(tri, jnp.concatenate([hi, mid, low], axis=1))
    return sums[:, 0:width] + sums[:, width:2 * width] + sums[:, 2 * width:3 * width]


def _shift_rows(v, shift, edge):
    n = v.shape[0]
    row = lax.broadcasted_iota(jnp.int32, (n, 1), 0)
    out = pltpu.roll(v, shift % n, axis=0)
    if shift > 0:
        for j in range(shift):
            out = jnp.where(row == j, edge[8 - shift + j:8 - shift + j + 1, :], out)
    else:
        for j in range(-shift):
            out = jnp.where(row == n + shift + j, edge[j:j + 1, :], out)
    return out


def _gates(z, lbp):
    w = HGRN_W
    lb = _sigmoid(lbp[0:1, :] - lbp[1:2, :])
    zq = z[:, 0:w]
    sig = _sigmoid(z[:, w:2 * w])
    f = lb + (1.0 - lb) * sig
    sq = _sigmoid(zq)
    q = zq * sq * HGRN_DK ** -0.5
    return lb, sig, f, sq, q


def _decayed_operands(q, f, v, qm_buf, km_buf, kbar_buf, v_buf, etot_buf, emid_buf):
    n, width = f.shape
    bcum = _chunk_cumsum(jnp.log(f))

    def row_of_chunk(offset):
        return jnp.concatenate(
            [jnp.broadcast_to(bcum[c + offset:c + offset + 1, :], (CHUNK, width)) for c in range(0, n, CHUNK)], axis=0)

    total, mid = row_of_chunk(CHUNK - 1), row_of_chunk(CHUNK // 2 - 1)
    em, enm, erest = jnp.exp(bcum - mid), jnp.exp(mid - bcum), jnp.exp(total - bcum)
    kk = 1.0 - f
    qm_buf[...] = (q * em).astype(BF16)
    km_buf[...] = (kk * enm).astype(BF16)
    kbar_buf[...] = (kk * erest).astype(BF16)
    v_buf[...] = v.astype(BF16)
    etot_buf[...] = jnp.exp(total)
    emid_buf[...] = jnp.exp(mid)
    return em, enm, erest


def _short_conv(u, edge, cw):
    return cw[0:1, :] * _shift_rows(u, 2, edge) + cw[1:2, :] * _shift_rows(u, 1, edge) + cw[2:3, :] * u


def _block_causal_mask(n):
    row = lax.broadcasted_iota(jnp.int32, (n, n), 0)
    col = lax.broadcasted_iota(jnp.int32, (n, n), 1)
    return jnp.logical_and(row // CHUNK == col // CHUNK, col <= row)


def _spread(v, chunk_of_row, nc):
    return jnp.concatenate([jnp.where(chunk_of_row == c, v, jnp.zeros_like(v)) for c in range(nc)], axis=1)


def _pick(r, chunk_of_row, nc):
    out = jnp.where(chunk_of_row == 0, r[:, 0:HGRN_DK], 0.0)
    for c in range(1, nc):
        out = out + jnp.where(chunk_of_row == c, r[:, c * HGRN_DK:(c + 1) * HGRN_DK], 0.0)
    return out


def _mix_fwd(x, g, w_in, lbp, gh, convw_t, w_out, exchange=None):
    t, d = x.shape
    zw = w_in.shape[0]
    w = HGRN_W
    tm = min(TOKEN_TILE, t)
    nc = tm // CHUNK
    n_chunks = t // CHUNK

    def body(x_ref, g_ref, win_ref, lbp_ref, gh_ref, cw_ref, wout_ref,
             xo_ref, z_ref, o_ref, st_ref, y_ref, state, ucarry, qm_buf, km_buf, kbar_buf, v_buf, etot_buf, emid_buf):
        _zero_at_start(state, ucarry)
        xv = x_ref[...]
        h, _, _ = _rms(xv, g_ref[...])
        z_ref[...] = _dot(h.astype(BF16), win_ref[...], NT)
        z = z_ref[...]
        _, _, f, _, q = _gates(z, lbp_ref[...])
        _decayed_operands(q, f, z[:, 2 * w:3 * w], qm_buf, km_buf, kbar_buf, v_buf, etot_buf, emid_buf)
        mask = _block_causal_mask(tm)
        chunk_of_row = lax.broadcasted_iota(jnp.int32, (tm, 1), 0) // CHUNK
        heads = range(HGRN_HEADS)
        hcols = [slice(hd * HGRN_DK, (hd + 1) * HGRN_DK) for hd in heads]
        qm = [qm_buf[:, hcols[hd]] for hd in heads]
        vb = [v_buf[:, hcols[hd]] for hd in heads]
        scores = [jnp.where(mask, _dot(qm[hd], km_buf[:, hcols[hd]], NT), 0.0).astype(BF16) for hd in heads]
        gains = [_dot(_spread(vb[hd], chunk_of_row, nc), kbar_buf[:, hcols[hd]], TN) for hd in heads]
        entering = []
        for hd in heads:
            states, st = [], state[hd]
            for c in range(nc):
                first_row = slice(c * CHUNK, c * CHUNK + 1)
                states.append(st * emid_buf[first_row, hcols[hd]])
                st_ref[c, hd] = st
                st = st * etot_buf[first_row, hcols[hd]] + gains[hd][c * HGRN_DK:(c + 1) * HGRN_DK, :]
            state[hd] = st
            entering.append(jnp.concatenate(states, axis=0).astype(BF16))
        from_states = [_dot(qm[hd], entering[hd], NT) for hd in heads]
        o_heads = [_dot(scores[hd], vb[hd]) + _pick(from_states[hd], chunk_of_row, nc) for hd in heads]
        o_ref[...] = jnp.concatenate(o_heads, axis=1)
        ghv = gh_ref[...]
        normed = jnp.concatenate([_rms(o_heads[hd], ghv[:, hcols[hd]])[0] for hd in heads], axis=1)
        zg = z[:, 3 * w:4 * w]
        u = z[:, 5 * w:6 * w] * z[:, 6 * w:7 * w]
        conv = _short_conv(u, ucarry[...], cw_ref[...])
        ucarry[...] = u[tm - 8:tm, :]
        y = jnp.concatenate([normed * (zg * _sigmoid(zg)), z[:, 4 * w:5 * w] * conv], axis=1).astype(BF16)
        y_ref[...] = y
        xo_ref[...] = xv + _dot(y, wout_ref[...])

    return _call(
        body,
        name="mix_fwd",
        grid=(t // tm,),
        in_specs=[
            _rows(tm, d), _full((1, d)), _full((zw, d)), _full((2, w)), _full((1, w)), _full((3, w)),
            _full((2 * w, d)),
        ],
        out_specs=[
            _rows(tm, d), _rows(tm, zw), _rows(tm, w),
            pl.BlockSpec((nc, HGRN_HEADS, HGRN_DK, HGRN_DK), lambda i: (i, 0, 0, 0)),
            _rows(tm, 2 * w),
        ],
        out_shape=[
            jax.ShapeDtypeStruct((t, d), F32),
            jax.ShapeDtypeStruct((t, zw), F32),
            jax.ShapeDtypeStruct((t, w), F32),
            jax.ShapeDtypeStruct((n_chunks, HGRN_HEADS, HGRN_DK, HGRN_DK), F32),
            jax.ShapeDtypeStruct((t, 2 * w), BF16),
        ],
        scratch_shapes=[
            pltpu.VMEM((HGRN_HEADS, HGRN_DK, HGRN_DK), F32), pltpu.VMEM((8, w), F32),
            pltpu.VMEM((tm, w), BF16), pltpu.VMEM((tm, w), BF16), pltpu.VMEM((tm, w), BF16),
            pltpu.VMEM((tm, w), BF16), pltpu.VMEM((tm, w), F32), pltpu.VMEM((tm, w), F32),
        ],
        args=(x, g, w_in, lbp, gh, convw_t, w_out),
        exchange=exchange,
    )


def _mix_bwd(x, g, dxo, z, o, states, w_in, lbp, gh, convw_t, w_out, exchange=None):
    t, d = x.shape
    zw = w_in.shape[0]
    w = HGRN_W
    tm = min(TOKEN_TILE, t)
    nc = tm // CHUNK
    n = t // tm

    def body(x_ref, g_ref, dxo_ref, z_ref, zprev_ref, o_ref, st_ref, win_ref, lbp_ref, gh_ref, cw_ref, wout_ref,
             dx_ref, dz_ref, h_ref, dg_ref, dlbp_ref, dgh_ref, dcw_ref,
             dstate, dcarry, do_buf, qm_buf, km_buf, kbar_buf, v_buf, etot_buf, emid_buf):
        _zero_at_start(dstate, dcarry, dg_ref, dlbp_ref, dgh_ref, dcw_ref)
        gv = g_ref[...]
        h, xh, r = _rms(x_ref[...], gv)
        h_ref[...] = h.astype(BF16)
        dxo = dxo_ref[...]
        dy = _dot(dxo.astype(BF16), wout_ref[...], NT)
        z = z_ref[...]
        lb, sig, f, sq, q = _gates(z, lbp_ref[...])
        em, enm, erest = _decayed_operands(
            q, f, z[:, 2 * w:3 * w], qm_buf, km_buf, kbar_buf, v_buf, etot_buf, emid_buf)

        ghv = gh_ref[...]
        zg = z[:, 3 * w:4 * w]
        sgz = _sigmoid(zg)
        dyh = dy[:, 0:w]
        don = dyh * (zg * sgz)
        heads = range(HGRN_HEADS)
        hcols = [slice(hd * HGRN_DK, (hd + 1) * HGRN_DK) for hd in heads]
        norms = [_rms(o_ref[:, hcols[hd]], ghv[:, hcols[hd]]) for hd in heads]
        on = jnp.concatenate([norms[hd][0] for hd in heads], axis=1)
        oh = jnp.concatenate([norms[hd][1] for hd in heads], axis=1)
        dz_ref[:, 3 * w:4 * w] = (dyh * on * (sgz * (1.0 + zg * (1.0 - sgz)))).astype(BF16)
        dgh_ref[...] += jnp.sum(don * oh, axis=0, keepdims=True)
        do_buf[...] = jnp.concatenate(
            [_rms_bwd(don[:, hcols[hd]], norms[hd][1], norms[hd][2], ghv[:, hcols[hd]]) for hd in heads],
            axis=1).astype(BF16)

        zb = z[:, 4 * w:5 * w]
        zc = z[:, 5 * w:6 * w]
        zu = z[:, 6 * w:7 * w]
        u = zc * zu
        cw = cw_ref[...]
        zp = zprev_ref[...]
        uprev = jnp.where(pl.program_id(0) == n - 1, 0.0, zp[:, 5 * w:6 * w] * zp[:, 6 * w:7 * w])
        dyc = dy[:, w:2 * w]
        dz_ref[:, 4 * w:5 * w] = (dyc * _short_conv(u, uprev, cw)).astype(BF16)
        dconv = dyc * zb
        edge = dcarry[...]
        dconv1 = _shift_rows(dconv, -1, edge)
        dconv2 = _shift_rows(dconv, -2, edge)
        dcarry[...] = dconv[0:8, :]
        du = cw[2:3, :] * dconv + cw[1:2, :] * dconv1 + cw[0:1, :] * dconv2
        dz_ref[:, 5 * w:6 * w] = (du * zu).astype(BF16)
        dz_ref[:, 6 * w:7 * w] = (du * zc).astype(BF16)
        dcw_ref[...] += jnp.concatenate([
            jnp.sum(u * dconv2, axis=0, keepdims=True),
            jnp.sum(u * dconv1, axis=0, keepdims=True),
            jnp.sum(u * dconv, axis=0, keepdims=True)], axis=0)

        mask = _block_causal_mask(tm)
        chunk_of_row = lax.broadcasted_iota(jnp.int32, (tm, 1), 0) // CHUNK
        heads = range(HGRN_HEADS)
        hcols = [slice(hd * HGRN_DK, (hd + 1) * HGRN_DK) for hd in heads]
        qmb = [qm_buf[:, hcols[hd]] for hd in heads]
        kmb = [km_buf[:, hcols[hd]] for hd in heads]
        vb = [v_buf[:, hcols[hd]] for hd in heads]
        dob = [do_buf[:, hcols[hd]] for hd in heads]
        scores = [jnp.where(mask, _dot(qmb[hd], kmb[hd], NT), 0.0).astype(BF16) for hd in heads]
        dscores = [jnp.where(mask, _dot(dob[hd], vb[hd], NT), 0.0).astype(BF16) for hd in heads]
        gains = [_dot(_spread(dob[hd], chunk_of_row, nc), qmb[hd], TN) for hd in heads]
        dst_rows, dst_lanes, st_lanes, carries = [], [], [], []
        for hd in heads:
            entering = [st_ref[c, hd] for c in range(nc)]
            emid = [emid_buf[c * CHUNK:c * CHUNK + 1, hcols[hd]] for c in range(nc)]
            leaving, carried_back = [None] * nc, [None] * nc
            dst = dstate[hd]
            for c in reversed(range(nc)):
                elast = etot_buf[c * CHUNK:c * CHUNK + 1, hcols[hd]]
                leaving[c] = dst
                carried_back[c] = jnp.sum(dst * entering[c], axis=0, keepdims=True) * elast
                dst = dst * elast + gains[hd][c * HGRN_DK:(c + 1) * HGRN_DK, :] * emid[c]
            dstate[hd] = dst
            dst_rows.append(jnp.concatenate(leaving, axis=0).astype(BF16))
            dst_lanes.append(jnp.concatenate(leaving, axis=1).astype(BF16))
            st_lanes.append(jnp.concatenate([entering[c] * emid[c] for c in range(nc)], axis=1).astype(BF16))
            carries.append(carried_back)
        dv = [_dot(scores[hd], dob[hd], TN) + _pick(_dot(kbar_buf[:, hcols[hd]], dst_rows[hd], NT), chunk_of_row, nc)
              for hd in heads]
        dz_ref[:, 2 * w:3 * w] = jnp.concatenate(dv, axis=1).astype(BF16)
        dqm = jnp.concatenate([_dot(dscores[hd], kmb[hd]) + _pick(_dot(dob[hd], st_lanes[hd]), chunk_of_row, nc)
                               for hd in heads], axis=1)
        dkm = jnp.concatenate([_dot(dscores[hd], qmb[hd], TN) for hd in heads], axis=1)
        dkbar = jnp.concatenate([_pick(_dot(vb[hd], dst_lanes[hd]), chunk_of_row, nc) for hd in heads], axis=1)

        kbar_dkbar = kbar_buf[...].astype(F32) * dkbar
        db = qm_buf[...].astype(F32) * dqm - km_buf[...].astype(F32) * dkm - kbar_dkbar
        through_last = jnp.concatenate([
            jnp.broadcast_to(
                jnp.sum(kbar_dkbar[c * CHUNK:(c + 1) * CHUNK], axis=0, keepdims=True)
                + jnp.concatenate([carries[hd][c] for hd in heads], axis=1),
                (CHUNK, w))
            for c in range(nc)], axis=0)
        dlogf = _chunk_cumsum(db, reverse=True) + through_last
        df = dlogf / f - (dkm * enm + dkbar * erest)
        zq = z[:, 0:w]
        dz_ref[:, 0:w] = (dqm * em * HGRN_DK ** -0.5 * (sq * (1.0 + zq * (1.0 - sq)))).astype(BF16)
        dz_ref[:, w:2 * w] = (df * (1.0 - lb) * sig * (1.0 - sig)).astype(BF16)
        dlb = jnp.sum(df * (1.0 - sig), axis=0, keepdims=True) * lb * (1.0 - lb)
        dlbp_ref[...] += jnp.concatenate([dlb, -dlb], axis=0)

        dh = _dot(dz_ref[...], win_ref[...])
        dx_ref[...] = _rms_bwd(dh, xh, r, gv) + dxo
        dg_ref[...] += jnp.sum(dh * xh, axis=0, keepdims=True)

    return _call(
        body,
        name="mix_bwd",
        grid=(n,),
        in_specs=[
            _rows_rev(tm, d, n), _full((1, d)), _rows_rev(tm, d, n), _rows_rev(tm, zw, n),
            pl.BlockSpec((8, zw), lambda i: (jnp.maximum((n - 1 - i) * (tm // 8) - 1, 0), 0)),
            _rows_rev(tm, w, n),
            pl.BlockSpec((nc, HGRN_HEADS, HGRN_DK, HGRN_DK), lambda i: (n - 1 - i, 0, 0, 0)),
            _full((zw, d)), _full((2, w)), _full((1, w)), _full((3, w)), _full((2 * w, d)),
        ],
        out_specs=[
            _rows_rev(tm, d, n), _rows_rev(tm, zw, n), _rows_rev(tm, d, n),
            _full((1, d)), _full((2, w)), _full((1, w)), _full((3, w)),
        ],
        out_shape=[
            jax.ShapeDtypeStruct((t, d), F32),
            jax.ShapeDtypeStruct((t, zw), BF16),
            jax.ShapeDtypeStruct((t, d), BF16),
            jax.ShapeDtypeStruct((1, d), F32),
            jax.ShapeDtypeStruct((2, w), F32),
            jax.ShapeDtypeStruct((1, w), F32),
            jax.ShapeDtypeStruct((3, w), F32),
        ],
        scratch_shapes=[
            pltpu.VMEM((HGRN_HEADS, HGRN_DK, HGRN_DK), F32), pltpu.VMEM((8, w), F32),
            pltpu.VMEM((tm, w), BF16),
            pltpu.VMEM((tm, w), BF16), pltpu.VMEM((tm, w), BF16), pltpu.VMEM((tm, w), BF16),
            pltpu.VMEM((tm, w), BF16), pltpu.VMEM((tm, w), F32), pltpu.VMEM((tm, w), F32),
        ],
        args=(x, g, dxo, z, z, o, states, w_in, lbp, gh, convw_t, w_out),
        exchange=exchange,
    )


def _memkv_fwd(mem, g, wkv):
    m, d = mem.shape
    nb, _, cb = wkv.shape

    def body(mem_ref, g_ref, wkv_ref, kv_ref):
        mn, _, _ = _rms(mem_ref[...], g_ref[...])
        mnb = mn.astype(BF16)
        for j in range(nb):
            kv_ref[:, j * cb:(j + 1) * cb] = _dot(mnb, wkv_ref[j]).astype(BF16)

    return pl.pallas_call(
        body,
        name="memkv_fwd",
        out_shape=jax.ShapeDtypeStruct((m, nb * cb), BF16),
        compiler_params=_params(),
    )(mem, g, wkv)


def _memkv_bwd(mem, g, dkv, wkv):
    m, d = mem.shape
    nb, _, cb = wkv.shape
    chips = nb // 2

    def body(mem_ref, g_ref, dkv_ref, wkv_ref, dw_ref, dg_ref, dw_all, send_buf, recv_buf, send_sem, recv_sem):
        x, y, c, _ = _mesh_place()
        sibling, _ = _peer(x, y, c, 1)
        mn, xh, _ = _rms(mem_ref[...], g_ref[...])
        mnb = mn.astype(BF16)
        dmn = jnp.zeros((m, d), F32)
        for j in range(nb):
            dkvb = dkv_ref[:, j * cb:(j + 1) * cb].astype(BF16)
            dw_all[j] = _dot(mnb, dkvb, TN)
            dmn = dmn + _dot(dkvb, wkv_ref[j], NT)
        dg_ref[...] = jnp.sum(dmn * xh, axis=0, keepdims=True)
        for q in range(chips):
            send_buf[q] = dw_all[2 * q + 1 - c].astype(BF16)
        to_sibling = _remote(send_buf, recv_buf, send_sem, recv_sem, sibling)
        to_sibling.start()
        to_sibling.wait_send()
        to_sibling.wait_recv()
        for q in range(chips):
            dw_ref[q] = (dw_all[2 * q + c] + recv_buf[q].astype(F32)).astype(BF16)

    return pl.pallas_call(
        body,
        name="memkv_bwd",
        out_shape=[jax.ShapeDtypeStruct((chips, d, cb), BF16), jax.ShapeDtypeStruct((1, d), F32)],
        scratch_shapes=[
            pltpu.VMEM((nb, d, cb), F32), pltpu.VMEM((chips, d, cb), BF16), pltpu.VMEM((chips, d, cb), BF16),
            pltpu.SemaphoreType.DMA, pltpu.SemaphoreType.DMA,
        ],
        compiler_params=_params(),
    )(mem, g, dkv, wkv)


def _softmax_rows(qm_h, k_h):
    sc = _dot(qm_h, k_h, NT) * MEM_HD ** -0.5
    e = jnp.exp(sc - jnp.max(sc, axis=-1, keepdims=True))
    return e / jnp.sum(e, axis=-1, keepdims=True)


def _xattn_fwd(x, g, wq, kv, wo, exchange=None):
    t, d = x.shape
    m = kv.shape[0]
    tm = min(XATTN_TILE, t)

    def body(x_ref, g_ref, wq_ref, kv_ref, wo_ref, xo_ref, hq_ref, qm_ref, att_ref):
        xv = x_ref[...]
        h, _, _ = _rms(xv, g_ref[...])
        hb = h.astype(BF16)
        hq_ref[...] = hb
        qm = _dot(hb, wq_ref[...]).astype(BF16)
        qm_ref[...] = qm
        heads = range(MEM_HEADS)
        kcols = [slice(hd * MEM_HD, (hd + 1) * MEM_HD) for hd in heads]
        p = [_softmax_rows(qm[:, kcols[hd]], kv_ref[:, kcols[hd]]) for hd in heads]
        att = jnp.concatenate(
            [_dot(p[hd].astype(BF16), kv_ref[:, d + hd * MEM_HD:d + (hd + 1) * MEM_HD]) for hd in heads],
            axis=1).astype(BF16)
        att_ref[...] = att
        xo_ref[...] = xv + _dot(att, wo_ref[...])

    return _call(
        body,
        name="xattn_fwd",
        grid=(t // tm,),
        in_specs=[_rows(tm, d), _full((1, d)), _full((d, d)), _full((m, 2 * d)), _full((d, d))],
        out_specs=[_rows(tm, d), _rows(tm, d), _rows(tm, d), _rows(tm, d)],
        out_shape=[
            jax.ShapeDtypeStruct((t, d), F32),
            jax.ShapeDtypeStruct((t, d), BF16),
            jax.ShapeDtypeStruct((t, d), BF16),
            jax.ShapeDtypeStruct((t, d), BF16),
        ],
        args=(x, g, wq, kv, wo),
        exchange=exchange,
    )


def _xattn_bwd(x, g, dxo, qm, kv, wq, wo, exchange=None):
    t, d = x.shape
    m = kv.shape[0]
    tm = min(XATTN_TILE, t)

    def body(x_ref, g_ref, dxo_ref, qm_ref, kv_ref, wq_ref, wo_ref, dx_ref, dqm_ref, dkv_ref, dg_ref):
        _zero_at_start(dkv_ref, dg_ref)
        gv = g_ref[...]
        _, xh, r = _rms(x_ref[...], gv)
        dxo = dxo_ref[...]
        datt = _dot(dxo.astype(BF16), wo_ref[...], NT).astype(BF16)
        heads = range(MEM_HEADS)
        kcols = [slice(hd * MEM_HD, (hd + 1) * MEM_HD) for hd in heads]
        vcols = [slice(d + hd * MEM_HD, d + (hd + 1) * MEM_HD) for hd in heads]
        qm_h = [qm_ref[:, kcols[hd]] for hd in heads]
        p = [_softmax_rows(qm_h[hd], kv_ref[:, kcols[hd]]) for hd in heads]
        dp = [_dot(datt[:, kcols[hd]], kv_ref[:, vcols[hd]], NT) for hd in heads]
        dsc = [(p[hd] * (dp[hd] - jnp.sum(p[hd] * dp[hd], axis=-1, keepdims=True)) * MEM_HD ** -0.5).astype(BF16)
               for hd in heads]
        dqm = jnp.concatenate([_dot(dsc[hd], kv_ref[:, kcols[hd]]) for hd in heads], axis=1).astype(BF16)
        dqm_ref[...] = dqm
        dkv_ref[...] += jnp.concatenate(
            [_dot(dsc[hd], qm_h[hd], TN) for hd in heads]
            + [_dot(p[hd].astype(BF16), datt[:, kcols[hd]], TN) for hd in heads], axis=1)
        dh = _dot(dqm, wq_ref[...], NT)
        dx_ref[...] = _rms_bwd(dh, xh, r, gv) + dxo
        dg_ref[...] += jnp.sum(dh * xh, axis=0, keepdims=True)

    return _call(
        body,
        name="xattn_bwd",
        grid=(t // tm,),
        in_specs=[
            _rows(tm, d), _full((1, d)), _rows(tm, d), _rows(tm, d), _full((m, 2 * d)), _full((d, d)), _full((d, d)),
        ],
        out_specs=[_rows(tm, d), _rows(tm, d), _full((m, 2 * d)), _full((1, d))],
        out_shape=[
            jax.ShapeDtypeStruct((t, d), F32),
            jax.ShapeDtypeStruct((t, d), BF16),
            jax.ShapeDtypeStruct((m, 2 * d), F32),
            jax.ShapeDtypeStruct((1, d), F32),
        ],
        args=(x, g, dxo, qm, kv, wq, wo),
        exchange=exchange,
    )


def _mesh_place():
    x, y, c = lax.axis_index("x"), lax.axis_index("y"), lax.axis_index("c")
    return x, y, c, 4 * x + 2 * y + c


def _peer(x, y, c, k):
    px = 1 - x if k & 4 else x
    py = 1 - y if k & 2 else y
    pc = 1 - c if k & 1 else c
    return (px, py, pc), 4 * px + 2 * py + pc


ICI_HOPS = (2, 4, 6)
N_HOPS = len(ICI_HOPS)


def _remote(src, dst, send_sem, recv_sem, peer):
    return pltpu.make_async_remote_copy(
        src_ref=src, dst_ref=dst, send_sem=send_sem, recv_sem=recv_sem, device_id=peer, device_id_type=MESH_IDS)


def _gather_exchange(shards, middle_eighths=MIDDLE_EIGHTHS):
    n = len(shards)

    def place():
        x, y, c, me = _mesh_place()
        sibling, _ = _peer(x, y, c, 1)
        to_x, from_x = _peer(x, y, c, 4)
        to_y, from_y = _peer(x, y, c, 2)
        _, from_diagonal = _peer(x, y, c, 6)
        onward = (c * to_y[0] + (1 - c) * to_x[0], c * to_y[1] + (1 - c) * to_x[1], c)
        passed_on = c * from_x + (1 - c) * from_y
        return me, sibling, (to_x, to_y, onward), (from_x, from_y, from_diagonal), passed_on

    def start(src, dst, sems):
        ici_send, ici_recv, pair_send, pair_recv, local = sems
        me, sibling, targets, _, _ = place()
        for a in range(n):
            pltpu.make_async_copy(src[a], dst[a].at[me], local.at[a]).start()
            for j in range(2):
                _remote(src[a], dst[a].at[me], ici_send.at[a, j], ici_recv.at[a, j], targets[j]).start()
            _remote(src[a], dst[a].at[me], pair_send.at[a, 0], pair_recv.at[a, 0], sibling).start()

    def to_sibling(dst, sems, a, j, origin, sibling):
        _, _, pair_send, pair_recv, _ = sems
        slot = dst[a].at[origin]
        return _remote(slot, slot, pair_send.at[a, 1 + j], pair_recv.at[a, 1 + j], sibling)

    def middle(src, dst, sems):
        ici_send, ici_recv, _, _, _ = sems
        _, sibling, targets, origins, passed_on = place()
        for a in range(n):
            for j in range(2):
                _remote(src[a], dst[a].at[origins[j]], ici_send.at[a, j], ici_recv.at[a, j], targets[j]).wait_recv()
            slot = dst[a].at[passed_on]
            _remote(slot, slot, ici_send.at[a, 2], ici_recv.at[a, 2], targets[2]).start()
            for j in range(2):
                to_sibling(dst, sems, a, j, origins[j], sibling).start()

    def finish(src, dst, sems):
        ici_send, ici_recv, pair_send, pair_recv, local = sems
        me, sibling, targets, origins, _ = place()
        for a in range(n):
            _remote(src[a], dst[a].at[origins[2]], ici_send.at[a, 2], ici_recv.at[a, 2], targets[2]).wait_recv()
            to_sibling(dst, sems, a, 2, origins[2], sibling).start()
        for a in range(n):
            pltpu.make_async_copy(src[a], dst[a].at[me], local.at[a]).wait()
            for j in range(N_HOPS):
                _remote(src[a], dst[a].at[me], ici_send.at[a, j], ici_recv.at[a, j], targets[j]).wait_send()
            for j, origin in enumerate((me,) + origins):
                from_sibling = origin + 1 - 2 * (origin % 2)
                passed = _remote(src[a], dst[a].at[from_sibling], pair_send.at[a, j], pair_recv.at[a, j], sibling)
                passed.wait_send()
                passed.wait_recv()

    return _Exchange(
        shards,
        [jax.ShapeDtypeStruct((N_DEV,) + s.shape, s.dtype) for s in shards],
        [
            pltpu.SemaphoreType.DMA((n, N_HOPS)), pltpu.SemaphoreType.DMA((n, N_HOPS)),
            pltpu.SemaphoreType.DMA((n, N_HOPS + 1)), pltpu.SemaphoreType.DMA((n, N_HOPS + 1)),
            pltpu.SemaphoreType.DMA((n,)),
        ],
        start, finish, middle, middle_eighths)


def _scatter_copies(src, dst, sems, n, arrivals=False):
    send, recv, local = sems
    x, y, c, _ = _mesh_place()
    chip = 2 * x + y
    if arrivals is None:
        return [pltpu.make_async_copy(src[a].at[chip], dst[a].at[chip], local.at[a]) for a in range(n)]
    copies = []
    for a in range(n):
        for j, k in enumerate(ICI_HOPS):
            peer, _ = _peer(x, y, c, k)
            peer_chip = 2 * peer[0] + peer[1]
            slot = dst[a].at[peer_chip if arrivals else chip]
            copies.append(_remote(src[a].at[peer_chip], slot, send.at[a, j], recv.at[a, j], peer))
    return copies


def _scatter_start(src, dst, sems, n):
    for cp in _scatter_copies(src, dst, sems, n, arrivals=None) + _scatter_copies(src, dst, sems, n):
        cp.start()


def _scatter_finish(src, dst, sems, n):
    for cp in _scatter_copies(src, dst, sems, n, arrivals=None):
        cp.wait()
    for cp in _scatter_copies(src, dst, sems, n):
        cp.wait_send()
    for cp in _scatter_copies(src, dst, sems, n, arrivals=True):
        cp.wait_recv()


def _scatter_scratch(n):
    return [pltpu.SemaphoreType.DMA((n, N_HOPS)), pltpu.SemaphoreType.DMA((n, N_HOPS)), pltpu.SemaphoreType.DMA((n,))]


def _scatter_exchange(partials):
    n = len(partials)
    return _Exchange(
        partials, [jax.ShapeDtypeStruct(p.shape, p.dtype) for p in partials], _scatter_scratch(n),
        lambda src, dst, sems: _scatter_start(src, dst, sems, n),
        lambda src, dst, sems: _scatter_finish(src, dst, sems, n))


SMALL_LAYOUT = {
    "ffn1_norm": (0, 1, 1024), "mix_norm": (1, 1, 1024), "xattn_norm": (2, 1, 1024), "mem_norm": (3, 1, 1024),
    "ffn2_norm": (4, 1, 1024), "final_norm": (5, 1, 1024), "lb_param": (6, 2, 512), "hgrn_out_norm": (8, 1, 512),
    "conv_w": (9, 3, 512), "loss": (12, 1, 128),
}


def _final_exchange(partials, small):
    n = len(partials)
    names = list(small)
    width = 1024

    def body(*refs):
        src = refs[:n]
        pieces = refs[n:n + len(names)]
        dst = refs[n + len(names):2 * n + len(names)]
        total_ref = refs[2 * n + len(names)]
        pack, gathered, small_send, small_recv = refs[2 * n + len(names) + 1:2 * n + len(names) + 5]
        sems = refs[2 * n + len(names) + 5:]
        x, y, c, me = _mesh_place()
        pack[...] = jnp.zeros_like(pack)
        for name, piece in zip(names, pieces):
            row, nrows, ncols = SMALL_LAYOUT[name]
            pack[row:row + nrows, 0:ncols] = piece[...]
        for k in range(1, N_DEV):
            peer, _ = _peer(x, y, c, k)
            _remote(pack, gathered.at[me], small_send.at[k - 1], small_recv.at[k - 1], peer).start()
        _scatter_start(src, dst, sems, n)
        gathered[me] = pack[...]
        for k in range(1, N_DEV):
            peer, peer_index = _peer(x, y, c, k)
            landed = _remote(pack, gathered.at[peer_index], small_send.at[k - 1], small_recv.at[k - 1], peer)
            landed.wait_send()
            landed.wait_recv()
        total = gathered[0]
        for j in range(1, N_DEV):
            total = total + gathered[j]
        total_ref[...] = total
        _scatter_finish(src, dst, sems, n)

    hbm = pl.BlockSpec(memory_space=pltpu.HBM)
    vmem = pl.BlockSpec(memory_space=pltpu.VMEM)
    out = pl.pallas_call(
        body,
        name="final_exchange",
        in_specs=[hbm] * n + [vmem] * len(names),
        out_specs=[hbm] * n + [vmem],
        out_shape=[jax.ShapeDtypeStruct(p.shape, p.dtype) for p in partials]
        + [jax.ShapeDtypeStruct((SMALL_ROWS, width), F32)],
        scratch_shapes=[
            pltpu.VMEM((SMALL_ROWS, width), F32), pltpu.VMEM((N_DEV, SMALL_ROWS, width), F32),
            pltpu.SemaphoreType.DMA((N_DEV - 1,)), pltpu.SemaphoreType.DMA((N_DEV - 1,)),
        ] + _scatter_scratch(n),
        compiler_params=pltpu.CompilerParams(has_side_effects=True),
    )(*partials, *[small[k] for k in names])
    return out[:n], out[n]


def _adamw_math(w, g, m, v):
    m = ADAM_B1 * m + (1.0 - ADAM_B1) * g
    v = ADAM_B2 * v + (1.0 - ADAM_B2) * (g * g)
    m_hat = m / (1.0 - ADAM_B1 ** ADAM_STEP)
    v_hat = v / (1.0 - ADAM_B2 ** ADAM_STEP)
    delta = -ADAM_LR * (m_hat / (jnp.sqrt(v_hat) + ADAM_EPS) + ADAM_WD * w)
    return delta, m, v


def _adamw_shard(parts, w, m, v):
    r, c = w.shape
    n_parts = parts.shape[0]
    tr = max(rows for rows in range(16, r + 1, 16) if r % rows == 0 and rows * c <= ADAMW_TILE_ELEMENTS)

    def body(p_ref, w_ref, m_ref, v_ref, g_ref, d_ref, mo_ref, vo_ref):
        g = p_ref[0].astype(F32)
        for j in range(1, n_parts):
            g = g + p_ref[j].astype(F32)
        delta, mn, vn = _adamw_math(w_ref[...], g, m_ref[...], v_ref[...])
        g_ref[...] = g
        d_ref[...] = delta
        mo_ref[...] = mn
        vo_ref[...] = vn

    tile = pl.BlockSpec((tr, c), lambda i: (i, 0))
    return pl.pallas_call(
        body,
        name="adamw_shard",
        grid=(r // tr,),
        in_specs=[pl.BlockSpec((n_parts, tr, c), lambda i: (0, i, 0)), tile, tile, tile],
        out_specs=[tile] * 4,
        out_shape=[jax.ShapeDtypeStruct((r, c), F32)] * 4,
        compiler_params=_params(("parallel",)),
    )(parts, w, m, v)


def _adamw_small(gs, ws, ms, vs):
    n = len(gs)

    def body(*refs):
        g_refs, w_refs, m_refs, v_refs = refs[:n], refs[n:2 * n], refs[2 * n:3 * n], refs[3 * n:4 * n]
        g_out, d_out, m_out, v_out = refs[4 * n:5 * n], refs[5 * n:6 * n], refs[6 * n:7 * n], refs[7 * n:8 * n]
        for i in range(n):
            if gs[i].ndim == ws[i].ndim:
                g = g_refs[i][...]
            else:
                g = g_refs[i][0].astype(F32)
                for j in range(1, gs[i].shape[0]):
                    g = g + g_refs[i][j].astype(F32)
            delta, mn, vn = _adamw_math(w_refs[i][...], g, m_refs[i][...], v_refs[i][...])
            g_out[i][...] = g
            d_out[i][...] = delta
            m_out[i][...] = mn
            v_out[i][...] = vn

    shapes = [jax.ShapeDtypeStruct(w.shape, F32) for w in ws]
    out = pl.pallas_call(
        body,
        name="adamw_small",
        out_shape=shapes * 4,
        compiler_params=_params(),
    )(*gs, *ws, *ms, *vs)
    return out[:n], out[n:2 * n], out[2 * n:3 * n], out[3 * n:]


TRANSPOSED = ("ffn1_gate", "ffn1_up", "w_in", "ffn2_gate", "ffn2_up", "conv_w")
GROUP_FFN1 = ("ffn1_gate", "ffn1_up", "ffn1_down")
GROUP_MIX = ("w_in", "w_out")
GROUP_XATTN = ("w_q_mem", "w_kv_mem", "w_o_mem")
GROUP_FFN2 = ("ffn2_gate", "ffn2_up", "ffn2_down")
LARGE = GROUP_FFN1 + GROUP_MIX + GROUP_XATTN + GROUP_FFN2
SHORT_SHARDS = ("w_out", "w_q_mem", "w_kv_mem", "w_o_mem")
SMALL = ("ffn1_norm", "mix_norm", "lb_param", "hgrn_out_norm", "conv_w", "xattn_norm", "mem_norm", "ffn2_norm",
         "final_norm")
WEIGHTS = ("ffn1_norm", "ffn1_gate", "ffn1_up", "ffn1_down", "mix_norm", "w_in", "lb_param", "hgrn_out_norm",
           "conv_w", "w_out", "xattn_norm", "mem_norm", "w_q_mem", "w_kv_mem", "w_o_mem", "ffn2_norm", "ffn2_gate",
           "ffn2_up", "ffn2_down", "final_norm")


def kernel(x, mem, ffn1_norm, ffn1_gate, ffn1_up, ffn1_down, mix_norm, w_in, lb_param, hgrn_out_norm, conv_w, w_out, xattn_norm, mem_norm, w_q_mem, w_kv_mem, w_o_mem, ffn2_norm, ffn2_gate, ffn2_up, ffn2_down, final_norm, loss_target, m_ffn1_norm, m_ffn1_gate, m_ffn1_up, m_ffn1_down, m_mix_norm, m_w_in, m_lb_param, m_hgrn_out_norm, m_conv_w, m_w_out, m_xattn_norm, m_mem_norm, m_w_q_mem, m_w_kv_mem, m_w_o_mem, m_ffn2_norm, m_ffn2_gate, m_ffn2_up, m_ffn2_down, m_final_norm, v_ffn1_norm, v_ffn1_gate, v_ffn1_up, v_ffn1_down, v_mix_norm, v_w_in, v_lb_param, v_hgrn_out_norm, v_conv_w, v_w_out, v_xattn_norm, v_mem_norm, v_w_q_mem, v_w_kv_mem, v_w_o_mem, v_ffn2_norm, v_ffn2_gate, v_ffn2_up, v_ffn2_down, v_final_norm):
    given = dict(locals())
    me = 4 * lax.axis_index("x") + 2 * lax.axis_index("y") + lax.axis_index("c")
    x0, memv, target = x[0], mem[0], loss_target[0]

    def shard(prefix, name):
        v = given[prefix + name]
        if v.ndim == 1:
            return v.reshape(1, -1)
        if v.ndim == 2:
            return v
        return v[0].T if name in TRANSPOSED else v[0]

    w = {name: shard("", name) for name in WEIGHTS}
    m = {name: shard("m_", name) for name in WEIGHTS}
    v = {name: shard("v_", name) for name in WEIGHTS}

    conv_taps, conv_rows = w["conv_w"].shape
    conv_tile = jnp.pad(w["conv_w"], ((0, 8 - conv_taps), (0, 128 - conv_rows)))
    wire = {name: w[name].astype(BF16) for name in LARGE}
    full = {}

    def landed(names, gathered):
        for name, blocks in zip(names, gathered):
            _, r, c = blocks.shape
            full[name] = blocks if name == "w_kv_mem" else blocks.reshape(N_DEV * r, c)

    first = ("ffn1_gate", "ffn1_up")
    landed(first, _run_exchange(_gather_exchange([wire[k] for k in first]), "gather_first"))

    riders = (("ffn1_down", "w_in", "w_out", "w_kv_mem"), ("w_q_mem", "w_o_mem", "ffn2_gate", "ffn2_up"),
              ("ffn2_down",))
    (a1, b1, s1), gathered = _ffn_up(
        x0, w["ffn1_norm"], full["ffn1_gate"], full["ffn1_up"],
        exchange=_gather_exchange([wire[k] for k in riders[0]] + [conv_tile]))
    landed(riders[0], gathered)
    convw_t = gathered[-1][:, :conv_taps, :conv_rows].transpose(1, 0, 2).reshape(conv_taps, N_DEV * conv_rows)
    (x1,), _ = _ffn_down(x0, s1, full["ffn1_down"])
    (x2, z, o_raw, states, ycat), gathered = _mix_fwd(
        x1, w["mix_norm"], full["w_in"], w["lb_param"], w["hgrn_out_norm"], convw_t, full["w_out"],
        exchange=_gather_exchange([wire[k] for k in riders[1]]))
    landed(riders[1], gathered)
    kv = _memkv_fwd(memv, w["mem_norm"], full["w_kv_mem"])
    (x3, hq, qm, att), gathered = _xattn_fwd(
        x2, w["xattn_norm"], full["w_q_mem"], kv, full["w_o_mem"],
        exchange=_gather_exchange([wire[k] for k in riders[2]], middle_eighths=EARLY_MIDDLE_EIGHTHS))
    landed(riders[2], gathered)
    (dx4, a2, b2, s2, loss_part, d_final), _ = _ffn_fwd(
        x3, w["ffn2_norm"], full["ffn2_gate"], full["ffn2_up"], full["ffn2_down"], head=(w["final_norm"], target))

    parts = {}
    waiting = []

    def carried():
        names = [name for name, _ in waiting]
        exchange = _scatter_exchange([p for _, p in waiting]) if waiting else None
        del waiting[:]
        return names, exchange

    def weight_grad(name, a, b, scale=1.0):
        names, exchange = carried()
        partial, arrived = _weight_grad(a, b, scale, exchange=exchange)
        parts.update(zip(names, arrived))
        waiting.append((name, partial))

    (dx3, da2, db2, h4, d_ffn2_norm), _ = _ffn_bwd(
        x3, w["ffn2_norm"], dx4, a2, b2, full["ffn2_gate"], full["ffn2_up"], full["ffn2_down"])
    weight_grad("ffn2_down", s2, dx4, 0.5)
    weight_grad("ffn2_gate", da2, h4)
    weight_grad("ffn2_up", db2, h4)
    names, exchange = carried()
    (dx2, dqm, dkv, d_xattn_norm), arrived = _xattn_bwd(
        x2, w["xattn_norm"], dx3, qm, kv, full["w_q_mem"], full["w_o_mem"], exchange=exchange)
    parts.update(zip(names, arrived))
    d_wkv, d_mem_norm = _memkv_bwd(memv, w["mem_norm"], dkv, full["w_kv_mem"])
    waiting.append(("w_kv_mem", d_wkv))
    names, exchange = carried()
    (dx1, dz, h2, d_mix_norm, d_lbp, d_gh, d_convw_t), arrived = _mix_bwd(
        x1, w["mix_norm"], dx2, z, o_raw, states, full["w_in"], w["lb_param"], w["hgrn_out_norm"], convw_t,
        full["w_out"], exchange=exchange)
    parts.update(zip(names, arrived))
    weight_grad("w_in", dz, h2)
    weight_grad("ffn1_down", s1, dx1, 0.5)
    (dx0, da1, db1, h1, d_ffn1_norm), _ = _ffn_bwd(
        x0, w["ffn1_norm"], dx1, a1, b1, full["ffn1_gate"], full["ffn1_up"], full["ffn1_down"])
    weight_grad("ffn1_gate", da1, h1)
    weight_grad("ffn1_up", db1, h1)
    weight_grad("w_o_mem", att, dx3)
    weight_grad("w_q_mem", hq, dqm)
    weight_grad("w_out", ycat, dx2)

    small_parts = {
        "ffn1_norm": d_ffn1_norm, "mix_norm": d_mix_norm, "xattn_norm": d_xattn_norm, "mem_norm": d_mem_norm,
        "ffn2_norm": d_ffn2_norm, "final_norm": d_final, "lb_param": d_lbp, "hgrn_out_norm": d_gh,
        "conv_w": d_convw_t, "loss": loss_part,
    }
    names = [name for name, _ in waiting]
    arrived, total = _final_exchange([p for _, p in waiting], small_parts)
    parts.update(zip(names, arrived))

    g_out, d_out, m_out, v_out = {}, {}, {}, {}
    for name in LARGE:
        if name not in SHORT_SHARDS:
            g_out[name], d_out[name], m_out[name], v_out[name] = _adamw_shard(parts[name], w[name], m[name], v[name])
    g_small = {name: parts[name] for name in SHORT_SHARDS}
    for name in SMALL:
        row, nrows, ncols = SMALL_LAYOUT[name]
        g_small[name] = total[row:row + nrows, 0:ncols]
    g_small["conv_w"] = lax.dynamic_slice_in_dim(g_small["conv_w"], me * conv_rows, conv_rows, axis=1)
    together = SMALL + SHORT_SHARDS
    gs, ds, ms, vs = _adamw_small(
        [g_small[k] for k in together], [w[k] for k in together], [m[k] for k in together],
        [v[k] for k in together])
    for i, name in enumerate(together):
        g_out[name], d_out[name], m_out[name], v_out[name] = gs[i], ds[i], ms[i], vs[i]

    def shaped(value, name):
        return (value.T if name in TRANSPOSED else value).reshape(given[name].shape)

    loss = total[SMALL_LAYOUT["loss"][0], 0]
    outs = [loss, dx0.reshape(x.shape)]
    for group in (g_out, d_out, m_out, v_out):
        outs += [shaped(group[name], name) for name in WEIGHTS]
    return tuple(outs)
```

```python
import jax
import jax.numpy as jnp
from jax import lax
from jax.experimental import pallas as pl
from jax.experimental.pallas import tpu as pltpu

F32 = jnp.float32
BF16 = jnp.bfloat16
MESH_IDS = pl.DeviceIdType.MESH

N_DEV = 8
EPS = 1e-6
HGRN_HEADS = 4
HGRN_DK = 128
HGRN_W = 512
CHUNK = 64
MEM_HEADS = 4
MEM_HD = 256
ADAM_LR = 0.001
ADAM_B1 = 0.9
ADAM_B2 = 0.999
ADAM_EPS = 1e-08
ADAM_WD = 0.01
ADAM_STEP = 10

TOKEN_TILE = 256
XATTN_TILE = 512
WIDE_TILE = 512
REDUCE_TILE = 1024
ADAMW_TILE_ELEMENTS = 256 * 1024
MIDDLE_EIGHTHS = 5
EARLY_MIDDLE_EIGHTHS = 4
MXU_ROWS = 256
VMEM_LIMIT = 60 * 1024 * 1024
SMALL_ROWS = 16
NT = (((1,), (1,)), ((), ()))
TN = (((0,), (0,)), ((), ()))


def _params(sem=None):
    return pltpu.CompilerParams(dimension_semantics=sem, vmem_limit_bytes=VMEM_LIMIT)


def _dot(a, b, dims=None):
    if dims is None:
        return jnp.dot(a, b, preferred_element_type=F32)
    return lax.dot_general(a, b, dims, preferred_element_type=F32)


def _sigmoid(v):
    return 1.0 / (1.0 + jnp.exp(-v))


def _rms(x, g):
    r = lax.rsqrt(jnp.mean(x * x, axis=-1, keepdims=True) + EPS)
    xh = x * r
    return xh * g, xh, r


def _rms_bwd(dh, xh, r, g):
    dxh = dh * g
    return r * (dxh - xh * jnp.mean(dxh * xh, axis=-1, keepdims=True))


def _full(shape):
    return pl.BlockSpec(shape, lambda *_: (0,) * len(shape))


def _full_once(shape):
    return pl.BlockSpec(shape, lambda *_: (0,) * len(shape), pipeline_mode=pl.Buffered(1))


def _rows(tm, width):
    return pl.BlockSpec((tm, width), lambda i: (i, 0))


def _rows_rev(tm, width, n):
    return pl.BlockSpec((tm, width), lambda i: (n - 1 - i, 0))


def _zero_at_start(*refs):
    @pl.when(pl.program_id(0) == 0)
    def _():
        for ref in refs:
            ref[...] = jnp.zeros_like(ref)


class _Exchange:
    def __init__(self, operands, out_shapes, scratch, start, finish, middle=None, middle_eighths=MIDDLE_EIGHTHS):
        self.operands, self.out_shapes, self.scratch = list(operands), list(out_shapes), list(scratch)
        self.start, self.middle, self.finish, self.middle_eighths = start, middle, finish, middle_eighths


def _call(body, *, name, grid, in_specs, out_specs, out_shape, args, scratch_shapes=(), exchange=None):
    semantics = ("arbitrary",) * len(grid)
    if exchange is None:
        out = pl.pallas_call(
            body, name=name, grid=grid, in_specs=in_specs, out_specs=out_specs, out_shape=out_shape,
            scratch_shapes=list(scratch_shapes), compiler_params=_params(semantics))(*args)
        return out, []
    hbm = pl.BlockSpec(memory_space=pltpu.HBM)
    n_in, n_out, n_scr = len(in_specs), len(out_specs), len(scratch_shapes)
    e_in, e_out = len(exchange.operands), len(exchange.out_shapes)

    def carried(*refs):
        ins, rest = refs[:n_in], refs[n_in:]
        e_ins, rest = rest[:e_in], rest[e_in:]
        outs, rest = rest[:n_out], rest[n_out:]
        e_outs, rest = rest[:e_out], rest[e_out:]
        scr, e_scr = rest[:n_scr], rest[n_scr:]
        first = last = None
        for axis, size in enumerate(grid):
            at_start, at_end = pl.program_id(axis) == 0, pl.program_id(axis) == size - 1
            first = at_start if first is None else jnp.logical_and(first, at_start)
            last = at_end if last is None else jnp.logical_and(last, at_end)

        @pl.when(first)
        def _():
            exchange.start(e_ins, e_outs, e_scr)

        body(*ins, *outs, *scr)

        if exchange.middle is not None:
            assert len(grid) == 1

            @pl.when(pl.program_id(0) == (grid[0] * exchange.middle_eighths) // 8)
            def _():
                exchange.middle(e_ins, e_outs, e_scr)

        @pl.when(last)
        def _():
            exchange.finish(e_ins, e_outs, e_scr)

    out = pl.pallas_call(
        carried, name=name, grid=grid, in_specs=list(in_specs) + [hbm] * e_in,
        out_specs=list(out_specs) + [hbm] * e_out, out_shape=list(out_shape) + exchange.out_shapes,
        scratch_shapes=list(scratch_shapes) + exchange.scratch,
        compiler_params=pltpu.CompilerParams(
            dimension_semantics=semantics, vmem_limit_bytes=VMEM_LIMIT, has_side_effects=True),
    )(*args, *exchange.operands)
    return out[:n_out], out[n_out:]


def _run_exchange(exchange, name):
    hbm = pl.BlockSpec(memory_space=pltpu.HBM)
    e_in, e_out = len(exchange.operands), len(exchange.out_shapes)

    def body(*refs):
        e_ins, e_outs, e_scr = refs[:e_in], refs[e_in:e_in + e_out], refs[e_in + e_out:]
        exchange.start(e_ins, e_outs, e_scr)
        if exchange.middle is not None:
            exchange.middle(e_ins, e_outs, e_scr)
        exchange.finish(e_ins, e_outs, e_scr)

    return pl.pallas_call(
        body, name=name, in_specs=[hbm] * e_in, out_specs=[hbm] * e_out, out_shape=exchange.out_shapes,
        scratch_shapes=exchange.scratch, compiler_params=pltpu.CompilerParams(has_side_effects=True),
    )(*exchange.operands)


def _loss_head(xo, gf, tgt):
    d = xo.shape[1]
    y, xh, r = _rms(xo, gf)
    err = y - tgt
    dy = err * (1.0 / d)
    loss = 0.5 * jnp.sum(jnp.sum(err * err, axis=-1, keepdims=True) * (1.0 / d), axis=0, keepdims=True)
    return _rms_bwd(dy, xh, r, gf), loss, jnp.sum(dy * xh, axis=0, keepdims=True)


def _ffn_fwd(x, g, wg, wu, wd, exchange=None, head=None):
    t, d = x.shape
    f = wg.shape[0]
    tm = min(WIDE_TILE, t)

    def body(x_ref, g_ref, wg_ref, wu_ref, wd_ref, *rest):
        if head is None:
            xo_ref, a_ref, b_ref, s_ref = rest
        else:
            gf_ref, tgt_ref, xo_ref, a_ref, b_ref, s_ref, loss_ref, dgf_ref = rest
            _zero_at_start(loss_ref, dgf_ref)
        xv = x_ref[...]
        h, _, _ = _rms(xv, g_ref[...])
        hb = h.astype(BF16)
        a = _dot(hb, wg_ref[...], NT)
        b = _dot(hb, wu_ref[...], NT)
        s = (a * _sigmoid(a) * b).astype(BF16)
        xo = xv + 0.5 * _dot(s, wd_ref[...])
        if head is None:
            xo_ref[...] = xo
        else:
            xo_ref[...], loss, dgf = _loss_head(xo, gf_ref[...], tgt_ref[...])
            loss_ref[...] += jnp.broadcast_to(loss, (1, 128))
            dgf_ref[...] += dgf
        a_ref[...] = a.astype(BF16)
        b_ref[...] = b.astype(BF16)
        s_ref[...] = s

    in_specs = [_rows(tm, d), _full((1, d)), _full_once((f, d)), _full_once((f, d)), _full_once((f, d))]
    out_specs = [_rows(tm, d), _rows(tm, f), _rows(tm, f), _rows(tm, f)]
    out_shape = [
        jax.ShapeDtypeStruct((t, d), F32),
        jax.ShapeDtypeStruct((t, f), BF16),
        jax.ShapeDtypeStruct((t, f), BF16),
        jax.ShapeDtypeStruct((t, f), BF16),
    ]
    args = (x, g, wg, wu, wd)
    if head is not None:
        in_specs += [_full((1, d)), _rows(tm, d)]
        out_specs += [_full((1, 128)), _full((1, d))]
        out_shape += [jax.ShapeDtypeStruct((1, 128), F32), jax.ShapeDtypeStruct((1, d), F32)]
        args += tuple(head)
    return _call(
        body, name="ffn_fwd", grid=(t // tm,), in_specs=in_specs, out_specs=out_specs, out_shape=out_shape,
        args=args, exchange=exchange)


def _ffn_up(x, g, wg, wu, exchange=None):
    t, d = x.shape
    f = wg.shape[0]
    tm = min(TOKEN_TILE, t)

    def body(x_ref, g_ref, wg_ref, wu_ref, a_ref, b_ref, s_ref):
        h, _, _ = _rms(x_ref[...], g_ref[...])
        hb = h.astype(BF16)
        a = _dot(hb, wg_ref[...], NT)
        b = _dot(hb, wu_ref[...], NT)
        a_ref[...] = a.astype(BF16)
        b_ref[...] = b.astype(BF16)
        s_ref[...] = (a * _sigmoid(a) * b).astype(BF16)

    return _call(
        body, name="ffn_up", grid=(t // tm,),
        in_specs=[_rows(tm, d), _full((1, d)), _full_once((f, d)), _full_once((f, d))],
        out_specs=[_rows(tm, f)] * 3, out_shape=[jax.ShapeDtypeStruct((t, f), BF16)] * 3,
        args=(x, g, wg, wu), exchange=exchange)


def _ffn_down(x, s, wd, exchange=None):
    t, d = x.shape
    f = wd.shape[0]
    tm = min(TOKEN_TILE, t)

    def body(x_ref, s_ref, wd_ref, xo_ref):
        xo_ref[...] = x_ref[...] + 0.5 * _dot(s_ref[...], wd_ref[...])

    return _call(
        body, name="ffn_down", grid=(t // tm,),
        in_specs=[_rows(tm, d), _rows(tm, f), _full_once((f, d))],
        out_specs=[_rows(tm, d)], out_shape=[jax.ShapeDtypeStruct((t, d), F32)],
        args=(x, s, wd), exchange=exchange)


def _ffn_bwd(x, g, dxo, a, b, wg, wu, wd, exchange=None):
    t, d = x.shape
    f = wg.shape[0]
    tm = min(TOKEN_TILE, t)

    def body(x_ref, g_ref, dxo_ref, a_ref, b_ref, wg_ref, wu_ref, wd_ref, dx_ref, da_ref, db_ref, h_ref, dg_ref):
        _zero_at_start(dg_ref)
        gv = g_ref[...]
        h, xh, r = _rms(x_ref[...], gv)
        dxo = dxo_ref[...]
        ds = _dot((0.5 * dxo).astype(BF16), wd_ref[...], NT)
        af = a_ref[...].astype(F32)
        bf = b_ref[...].astype(F32)
        sg = _sigmoid(af)
        da = (ds * bf * (sg * (1.0 + af * (1.0 - sg)))).astype(BF16)
        db = (ds * (af * sg)).astype(BF16)
        dh = _dot(da, wg_ref[...]) + _dot(db, wu_ref[...])
        dx_ref[...] = _rms_bwd(dh, xh, r, gv) + dxo
        da_ref[...] = da
        db_ref[...] = db
        h_ref[...] = h.astype(BF16)
        dg_ref[...] += jnp.sum(dh * xh, axis=0, keepdims=True)

    return _call(
        body,
        name="ffn_bwd",
        grid=(t // tm,),
        in_specs=[
            _rows(tm, d), _full((1, d)), _rows(tm, d), _rows(tm, f), _rows(tm, f),
            _full_once((f, d)), _full_once((f, d)), _full_once((f, d)),
        ],
        out_specs=[_rows(tm, d), _rows(tm, f), _rows(tm, f), _rows(tm, d), _full((1, d))],
        out_shape=[
            jax.ShapeDtypeStruct((t, d), F32),
            jax.ShapeDtypeStruct((t, f), BF16),
            jax.ShapeDtypeStruct((t, f), BF16),
            jax.ShapeDtypeStruct((t, d), BF16),
            jax.ShapeDtypeStruct((1, d), F32),
        ],
        args=(x, g, dxo, a, b, wg, wu, wd),
        exchange=exchange,
    )


def _weight_grad(a, b, scale=1.0, exchange=None):
    t, m = a.shape
    n = b.shape[1]
    chips = N_DEV // 2
    r = m // N_DEV
    tk = min(REDUCE_TILE, t)
    halves = 2
    nb = n // halves
    nk = t // tk

    def body(a_ref, b_ref, o_ref, acc, send_buf, recv_buf, send_sems, recv_sems):
        k, j = pl.program_id(0), pl.program_id(1)
        x, y, c, _ = _mesh_place()
        sibling, _ = _peer(x, y, c, 1)
        bv = b_ref[...]
        if scale != 1.0:
            bv = bv * scale
        bb = bv.astype(BF16)
        acc_half = acc.at[j]

        @pl.when(k == 0)
        def _():
            acc_half[...] = jnp.zeros_like(acc_half)

        for i in range(m // MXU_ROWS):
            rows = slice(i * MXU_ROWS, (i + 1) * MXU_ROWS)
            acc_half[rows, :] += _dot(a_ref[:, rows].astype(BF16), bb, TN)

        def to_sibling(half):
            return _remote(send_buf.at[half], recv_buf.at[half], send_sems.at[half], recv_sems.at[half], sibling)

        def owned_rows(q, core):
            return pl.ds(pl.multiple_of((2 * q + core) * r, 8), r)

        for half in range(halves):
            @pl.when(jnp.logical_and(k == nk - 1, j == half))
            def _():
                for q in range(chips):
                    send_buf[half, q] = acc[half, owned_rows(q, 1 - c), :].astype(BF16)
                to_sibling(half).start()

        @pl.when(jnp.logical_and(k == nk - 1, j == halves - 1))
        def _():
            for half in range(halves):
                to_sibling(half).wait_send()
                to_sibling(half).wait_recv()
                for q in range(chips):
                    o_ref[q, :, half * nb:(half + 1) * nb] = (
                        acc[half, owned_rows(q, c), :] + recv_buf[half, q].astype(F32)).astype(BF16)

    (partial,), arrived = _call(
        body,
        name="weight_grad",
        grid=(nk, halves),
        in_specs=[pl.BlockSpec((tk, m), lambda k, j: (k, 0)), pl.BlockSpec((tk, nb), lambda k, j: (k, j))],
        out_specs=[pl.BlockSpec((chips, r, n), lambda k, j: (0, 0, 0))],
        out_shape=[jax.ShapeDtypeStruct((chips, r, n), BF16)],
        scratch_shapes=[
            pltpu.VMEM((halves, m, nb), F32),
            pltpu.VMEM((halves, chips, r, nb), BF16), pltpu.VMEM((halves, chips, r, nb), BF16),
            pltpu.SemaphoreType.DMA((halves,)), pltpu.SemaphoreType.DMA((halves,)),
        ],
        args=(a, b),
        exchange=exchange,
    )
    return partial, arrived


def _chunk_cumsum(v, reverse=False):
    n, width = v.shape
    row = lax.broadcasted_iota(jnp.int32, (n, n), 0)
    col = lax.broadcasted_iota(jnp.int32, (n, n), 1)
    earlier = col >= row if reverse else col <= row
    tri = jnp.where(jnp.logical_and(row // CHUNK == col // CHUNK, earlier), 1.0, 0.0).astype(BF16)
    hi = v.astype(BF16)
    rest = v - hi.astype(F32)
    mid = rest.astype(BF16)
    low = (rest - mid.astype(F32)).astype(BF16)
    sums = _dot(tri, jnp.concatenate([hi, mid, low], axis=1))
    return sums[:, 0:width] + sums[:, width:2 * width] + sums[:, 2 * width:3 * width]


def _shift_rows(v, shift, edge):
    n = v.shape[0]
    row = lax.broadcasted_iota(jnp.int32, (n, 1), 0)
    out = pltpu.roll(v, shift % n, axis=0)
    if shift > 0:
        for j in range(shift):
            out = jnp.where(row == j, edge[8 - shift + j:8 - shift + j + 1, :], out)
    else:
        for j in range(-shift):
            out = jnp.where(row == n + shift + j, edge[j:j + 1, :], out)
    return out


def _gates(z, lbp):
    w = HGRN_W
    lb = _sigmoid(lbp[0:1, :] - lbp[1:2, :])
    zq = z[:, 0:w]
    sig = _sigmoid(z[:, w:2 * w])
    f = lb + (1.0 - lb) * sig
    sq = _sigmoid(zq)
    q = zq * sq * HGRN_DK ** -0.5
    return lb, sig, f, sq, q


def _decayed_operands(q, f, v, qm_buf, km_buf, kbar_buf, v_buf, etot_buf, emid_buf):
    n, width = f.shape
    bcum = _chunk_cumsum(jnp.log(f))

    def row_of_chunk(offset):
        return jnp.concatenate(
            [jnp.broadcast_to(bcum[c + offset:c + offset + 1, :], (CHUNK, width)) for c in range(0, n, CHUNK)], axis=0)

    total, mid = row_of_chunk(CHUNK - 1), row_of_chunk(CHUNK // 2 - 1)
    em, enm, erest = jnp.exp(bcum - mid), jnp.exp(mid - bcum), jnp.exp(total - bcum)
    kk = 1.0 - f
    qm_buf[...] = (q * em).astype(BF16)
    km_buf[...] = (kk * enm).astype(BF16)
    kbar_buf[...] = (kk * erest).astype(BF16)
    v_buf[...] = v.astype(BF16)
    etot_buf[...] = jnp.exp(total)
    emid_buf[...] = jnp.exp(mid)
    return em, enm, erest


def _short_conv(u, edge, cw):
    return cw[0:1, :] * _shift_rows(u, 2, edge) + cw[1:2, :] * _shift_rows(u, 1, edge) + cw[2:3, :] * u


def _block_causal_mask(n):
    row = lax.broadcasted_iota(jnp.int32, (n, n), 0)
    col = lax.broadcasted_iota(jnp.int32, (n, n), 1)
    return jnp.logical_and(row // CHUNK == col // CHUNK, col <= row)


def _spread(v, chunk_of_row, nc):
    return jnp.concatenate([jnp.where(chunk_of_row == c, v, jnp.zeros_like(v)) for c in range(nc)], axis=1)


def _pick(r, chunk_of_row, nc):
    out = jnp.where(chunk_of_row == 0, r[:, 0:HGRN_DK], 0.0)
    for c in range(1, nc):
        out = out + jnp.where(chunk_of_row == c, r[:, c * HGRN_DK:(c + 1) * HGRN_DK], 0.0)
    return out


def _mix_fwd(x, g, w_in, lbp, gh, convw_t, w_out, exchange=None):
    t, d = x.shape
    zw = w_in.shape[0]
    w = HGRN_W
    tm = min(TOKEN_TILE, t)
    nc = tm // CHUNK
    n_chunks = t // CHUNK

    def body(x_ref, g_ref, win_ref, lbp_ref, gh_ref, cw_ref, wout_ref,
             xo_ref, z_ref, o_ref, st_ref, y_ref, state, ucarry, qm_buf, km_buf, kbar_buf, v_buf, etot_buf, emid_buf):
        _zero_at_start(state, ucarry)
        xv = x_ref[...]
        h, _, _ = _rms(xv, g_ref[...])
        z_ref[...] = _dot(h.astype(BF16), win_ref[...], NT)
        z = z_ref[...]
        _, _, f, _, q = _gates(z, lbp_ref[...])
        _decayed_operands(q, f, z[:, 2 * w:3 * w], qm_buf, km_buf, kbar_buf, v_buf, etot_buf, emid_buf)
        mask = _block_causal_mask(tm)
        chunk_of_row = lax.broadcasted_iota(jnp.int32, (tm, 1), 0) // CHUNK
        heads = range(HGRN_HEADS)
        hcols = [slice(hd * HGRN_DK, (hd + 1) * HGRN_DK) for hd in heads]
        qm = [qm_buf[:, hcols[hd]] for hd in heads]
        vb = [v_buf[:, hcols[hd]] for hd in heads]
        scores = [jnp.where(mask, _dot(qm[hd], km_buf[:, hcols[hd]], NT), 0.0).astype(BF16) for hd in heads]
        gains = [_dot(_spread(vb[hd], chunk_of_row, nc), kbar_buf[:, hcols[hd]], TN) for hd in heads]
        entering = []
        for hd in heads:
            states, st = [], state[hd]
            for c in range(nc):
                first_row = slice(c * CHUNK, c * CHUNK + 1)
                states.append(st * emid_buf[first_row, hcols[hd]])
                st_ref[c, hd] = st
                st = st * etot_buf[first_row, hcols[hd]] + gains[hd][c * HGRN_DK:(c + 1) * HGRN_DK, :]
            state[hd] = st
            entering.append(jnp.concatenate(states, axis=0).astype(BF16))
        from_states = [_dot(qm[hd], entering[hd], NT) for hd in heads]
        o_heads = [_dot(scores[hd], vb[hd]) + _pick(from_states[hd], chunk_of_row, nc) for hd in heads]
        o_ref[...] = jnp.concatenate(o_heads, axis=1)
        ghv = gh_ref[...]
        normed = jnp.concatenate([_rms(o_heads[hd], ghv[:, hcols[hd]])[0] for hd in heads], axis=1)
        zg = z[:, 3 * w:4 * w]
        u = z[:, 5 * w:6 * w] * z[:, 6 * w:7 * w]
        conv = _short_conv(u, ucarry[...], cw_ref[...])
        ucarry[...] = u[tm - 8:tm, :]
        y = jnp.concatenate([normed * (zg * _sigmoid(zg)), z[:, 4 * w:5 * w] * conv], axis=1).astype(BF16)
        y_ref[...] = y
        xo_ref[...] = xv + _dot(y, wout_ref[...])

    return _call(
        body,
        name="mix_fwd",
        grid=(t // tm,),
        in_specs=[
            _rows(tm, d), _full((1, d)), _full((zw, d)), _full((2, w)), _full((1, w)), _full((3, w)),
            _full((2 * w, d)),
        ],
        out_specs=[
            _rows(tm, d), _rows(tm, zw), _rows(tm, w),
            pl.BlockSpec((nc, HGRN_HEADS, HGRN_DK, HGRN_DK), lambda i: (i, 0, 0, 0)),
            _rows(tm, 2 * w),
        ],
        out_shape=[
            jax.ShapeDtypeStruct((t, d), F32),
            jax.ShapeDtypeStruct((t, zw), F32),
            jax.ShapeDtypeStruct((t, w), F32),
            jax.ShapeDtypeStruct((n_chunks, HGRN_HEADS, HGRN_DK, HGRN_DK), F32),
            jax.ShapeDtypeStruct((t, 2 * w), BF16),
        ],
        scratch_shapes=[
            pltpu.VMEM((HGRN_HEADS, HGRN_DK, HGRN_DK), F32), pltpu.VMEM((8, w), F32),
            pltpu.VMEM((tm, w), BF16), pltpu.VMEM((tm, w), BF16), pltpu.VMEM((tm, w), BF16),
            pltpu.VMEM((tm, w), BF16), pltpu.VMEM((tm, w), F32), pltpu.VMEM((tm, w), F32),
        ],
        args=(x, g, w_in, lbp, gh, convw_t, w_out),
        exchange=exchange,
    )


def _mix_bwd(x, g, dxo, z, o, states, w_in, lbp, gh, convw_t, w_out, exchange=None):
    t, d = x.shape
    zw = w_in.shape[0]
    w = HGRN_W
    tm = min(TOKEN_TILE, t)
    nc = tm // CHUNK
    n = t // tm

    def body(x_ref, g_ref, dxo_ref, z_ref, zprev_ref, o_ref, st_ref, win_ref, lbp_ref, gh_ref, cw_ref, wout_ref,
             dx_ref, dz_ref, h_ref, dg_ref, dlbp_ref, dgh_ref, dcw_ref,
             dstate, dcarry, do_buf, qm_buf, km_buf, kbar_buf, v_buf, etot_buf, emid_buf):
        _zero_at_start(dstate, dcarry, dg_ref, dlbp_ref, dgh_ref, dcw_ref)
        gv = g_ref[...]
        h, xh, r = _rms(x_ref[...], gv)
        h_ref[...] = h.astype(BF16)
        dxo = dxo_ref[...]
        dy = _dot(dxo.astype(BF16), wout_ref[...], NT)
        z = z_ref[...]
        lb, sig, f, sq, q = _gates(z, lbp_ref[...])
        em, enm, erest = _decayed_operands(
            q, f, z[:, 2 * w:3 * w], qm_buf, km_buf, kbar_buf, v_buf, etot_buf, emid_buf)

        ghv = gh_ref[...]
        zg = z[:, 3 * w:4 * w]
        sgz = _sigmoid(zg)
        dyh = dy[:, 0:w]
        don = dyh * (zg * sgz)
        heads = range(HGRN_HEADS)
        hcols = [slice(hd * HGRN_DK, (hd + 1) * HGRN_DK) for hd in heads]
        norms = [_rms(o_ref[:, hcols[hd]], ghv[:, hcols[hd]]) for hd in heads]
        on = jnp.concatenate([norms[hd][0] for hd in heads], axis=1)
        oh = jnp.concatenate([norms[hd][1] for hd in heads], axis=1)
        dz_ref[:, 3 * w:4 * w] = (dyh * on * (sgz * (1.0 + zg * (1.0 - sgz)))).astype(BF16)
        dgh_ref[...] += jnp.sum(don * oh, axis=0, keepdims=True)
        do_buf[...] = jnp.concatenate(
            [_rms_bwd(don[:, hcols[hd]], norms[hd][1], norms[hd][2], ghv[:, hcols[hd]]) for hd in heads],
            axis=1).astype(BF16)

        zb = z[:, 4 * w:5 * w]
        zc = z[:, 5 * w:6 * w]
        zu = z[:, 6 * w:7 * w]
        u = zc * zu
        cw = cw_ref[...]
        zp = zprev_ref[...]
        uprev = jnp.where(pl.program_id(0) == n - 1, 0.0, zp[:, 5 * w:6 * w] * zp[:, 6 * w:7 * w])
        dyc = dy[:, w:2 * w]
        dz_ref[:, 4 * w:5 * w] = (dyc * _short_conv(u, uprev, cw)).astype(BF16)
        dconv = dyc * zb
        edge = dcarry[...]
        dconv1 = _shift_rows(dconv, -1, edge)
        dconv2 = _shift_rows(dconv, -2, edge)
        dcarry[...] = dconv[0:8, :]
        du = cw[2:3, :] * dconv + cw[1:2, :] * dconv1 + cw[0:1, :] * dconv2
        dz_ref[:, 5 * w:6 * w] = (du * zu).astype(BF16)
        dz_ref[:, 6 * w:7 * w] = (du * zc).astype(BF16)
        dcw_ref[...] += jnp.concatenate([
            jnp.sum(u * dconv2, axis=0, keepdims=True),
            jnp.sum(u * dconv1, axis=0, keepdims=True),
            jnp.sum(u * dconv, axis=0, keepdims=True)], axis=0)

        mask = _block_causal_mask(tm)
        chunk_of_row = lax.broadcasted_iota(jnp.int32, (tm, 1), 0) // CHUNK
        heads = range(HGRN_HEADS)
        hcols = [slice(hd * HGRN_DK, (hd + 1) * HGRN_DK) for hd in heads]
        qmb = [qm_buf[:, hcols[hd]] for hd in heads]
        kmb = [km_buf[:, hcols[hd]] for hd in heads]
        vb = [v_buf[:, hcols[hd]] for hd in heads]
        dob = [do_buf[:, hcols[hd]] for hd in heads]
        scores = [jnp.where(mask, _dot(qmb[hd], kmb[hd], NT), 0.0).astype(BF16) for hd in heads]
        dscores = [jnp.where(mask, _dot(dob[hd], vb[hd], NT), 0.0).astype(BF16) for hd in heads]
        gains = [_dot(_spread(dob[hd], chunk_of_row, nc), qmb[hd], TN) for hd in heads]
        dst_rows, dst_lanes, st_lanes, carries = [], [], [], []
        for hd in heads:
            entering = [st_ref[c, hd] for c in range(nc)]
            emid = [emid_buf[c * CHUNK:c * CHUNK + 1, hcols[hd]] for c in range(nc)]
            leaving, carried_back = [None] * nc, [None] * nc
            dst = dstate[hd]
            for c in reversed(range(nc)):
                elast = etot_buf[c * CHUNK:c * CHUNK + 1, hcols[hd]]
                leaving[c] = dst
                carried_back[c] = jnp.sum(dst * entering[c], axis=0, keepdims=True) * elast
                dst = dst * elast + gains[hd][c * HGRN_DK:(c + 1) * HGRN_DK, :] * emid[c]
            dstate[hd] = dst
            dst_rows.append(jnp.concatenate(leaving, axis=0).astype(BF16))
            dst_lanes.append(jnp.concatenate(leaving, axis=1).astype(BF16))
            st_lanes.append(jnp.concatenate([entering[c] * emid[c] for c in range(nc)], axis=1).astype(BF16))
            carries.append(carried_back)
        dv = [_dot(scores[hd], dob[hd], TN) + _pick(_dot(kbar_buf[:, hcols[hd]], dst_rows[hd], NT), chunk_of_row, nc)
              for hd in heads]
        dz_ref[:, 2 * w:3 * w] = jnp.concatenate(dv, axis=1).astype(BF16)
        dqm = jnp.concatenate([_dot(dscores[hd], kmb[hd]) + _pick(_dot(dob[hd], st_lanes[hd]), chunk_of_row, nc)
                               for hd in heads], axis=1)
        dkm = jnp.concatenate([_dot(dscores[hd], qmb[hd], TN) for hd in heads], axis=1)
        dkbar = jnp.concatenate([_pick(_dot(vb[hd], dst_lanes[hd]), chunk_of_row, nc) for hd in heads], axis=1)

        kbar_dkbar = kbar_buf[...].astype(F32) * dkbar
        db = qm_buf[...].astype(F32) * dqm - km_buf[...].astype(F32) * dkm - kbar_dkbar
        through_last = jnp.concatenate([
            jnp.broadcast_to(
                jnp.sum(kbar_dkbar[c * CHUNK:(c + 1) * CHUNK], axis=0, keepdims=True)
                + jnp.concatenate([carries[hd][c] for hd in heads], axis=1),
                (CHUNK, w))
            for c in range(nc)], axis=0)
        dlogf = _chunk_cumsum(db, reverse=True) + through_last
        df = dlogf / f - (dkm * enm + dkbar * erest)
        zq = z[:, 0:w]
        dz_ref[:, 0:w] = (dqm * em * HGRN_DK ** -0.5 * (sq * (1.0 + zq * (1.0 - sq)))).astype(BF16)
        dz_ref[:, w:2 * w] = (df * (1.0 - lb) * sig * (1.0 - sig)).astype(BF16)
        dlb = jnp.sum(df * (1.0 - sig), axis=0, keepdims=True) * lb * (1.0 - lb)
        dlbp_ref[...] += jnp.concatenate([dlb, -dlb], axis=0)

        dh = _dot(dz_ref[...], win_ref[...])
        dx_ref[...] = _rms_bwd(dh, xh, r, gv) + dxo
        dg_ref[...] += jnp.sum(dh * xh, axis=0, keepdims=True)

    return _call(
        body,
        name="mix_bwd",
        grid=(n,),
        in_specs=[
            _rows_rev(tm, d, n), _full((1, d)), _rows_rev(tm, d, n), _rows_rev(tm, zw, n),
            pl.BlockSpec((8, zw), lambda i: (jnp.maximum((n - 1 - i) * (tm // 8) - 1, 0), 0)),
            _rows_rev(tm, w, n),
            pl.BlockSpec((nc, HGRN_HEADS, HGRN_DK, HGRN_DK), lambda i: (n - 1 - i, 0, 0, 0)),
            _full((zw, d)), _full((2, w)), _full((1, w)), _full((3, w)), _full((2 * w, d)),
        ],
        out_specs=[
            _rows_rev(tm, d, n), _rows_rev(tm, zw, n), _rows_rev(tm, d, n),
            _full((1, d)), _full((2, w)), _full((1, w)), _full((3, w)),
        ],
        out_shape=[
            jax.ShapeDtypeStruct((t, d), F32),
            jax.ShapeDtypeStruct((t, zw), BF16),
            jax.ShapeDtypeStruct((t, d), BF16),
            jax.ShapeDtypeStruct((1, d), F32),
            jax.ShapeDtypeStruct((2, w), F32),
            jax.ShapeDtypeStruct((1, w), F32),
            jax.ShapeDtypeStruct((3, w), F32),
        ],
        scratch_shapes=[
            pltpu.VMEM((HGRN_HEADS, HGRN_DK, HGRN_DK), F32), pltpu.VMEM((8, w), F32),
            pltpu.VMEM((tm, w), BF16),
            pltpu.VMEM((tm, w), BF16), pltpu.VMEM((tm, w), BF16), pltpu.VMEM((tm, w), BF16),
            pltpu.VMEM((tm, w), BF16), pltpu.VMEM((tm, w), F32), pltpu.VMEM((tm, w), F32),
        ],
        args=(x, g, dxo, z, z, o, states, w_in, lbp, gh, convw_t, w_out),
        exchange=exchange,
    )


def _memkv_fwd(mem, g, wkv):
    m, d = mem.shape
    nb, _, cb = wkv.shape

    def body(mem_ref, g_ref, wkv_ref, kv_ref):
        mn, _, _ = _rms(mem_ref[...], g_ref[...])
        mnb = mn.astype(BF16)
        for j in range(nb):
            kv_ref[:, j * cb:(j + 1) * cb] = _dot(mnb, wkv_ref[j]).astype(BF16)

    return pl.pallas_call(
        body,
        name="memkv_fwd",
        out_shape=jax.ShapeDtypeStruct((m, nb * cb), BF16),
        compiler_params=_params(),
    )(mem, g, wkv)


def _memkv_bwd(mem, g, dkv, wkv):
    m, d = mem.shape
    nb, _, cb = wkv.shape
    chips = nb // 2

    def body(mem_ref, g_ref, dkv_ref, wkv_ref, dw_ref, dg_ref, dw_all, send_buf, recv_buf, send_sem, recv_sem):
        x, y, c, _ = _mesh_place()
        sibling, _ = _peer(x, y, c, 1)
        mn, xh, _ = _rms(mem_ref[...], g_ref[...])
        mnb = mn.astype(BF16)
        dmn = jnp.zeros((m, d), F32)
        for j in range(nb):
            dkvb = dkv_ref[:, j * cb:(j + 1) * cb].astype(BF16)
            dw_all[j] = _dot(mnb, dkvb, TN)
            dmn = dmn + _dot(dkvb, wkv_ref[j], NT)
        dg_ref[...] = jnp.sum(dmn * xh, axis=0, keepdims=True)
        for q in range(chips):
            send_buf[q] = dw_all[2 * q + 1 - c].astype(BF16)
        to_sibling = _remote(send_buf, recv_buf, send_sem, recv_sem, sibling)
        to_sibling.start()
        to_sibling.wait_send()
        to_sibling.wait_recv()
        for q in range(chips):
            dw_ref[q] = (dw_all[2 * q + c] + recv_buf[q].astype(F32)).astype(BF16)

    return pl.pallas_call(
        body,
        name="memkv_bwd",
        out_shape=[jax.ShapeDtypeStruct((chips, d, cb), BF16), jax.ShapeDtypeStruct((1, d), F32)],
        scratch_shapes=[
            pltpu.VMEM((nb, d, cb), F32), pltpu.VMEM((chips, d, cb), BF16), pltpu.VMEM((chips, d, cb), BF16),
            pltpu.SemaphoreType.DMA, pltpu.SemaphoreType.DMA,
        ],
        compiler_params=_params(),
    )(mem, g, dkv, wkv)


def _softmax_rows(qm_h, k_h):
    sc = _dot(qm_h, k_h, NT) * MEM_HD ** -0.5
    e = jnp.exp(sc - jnp.max(sc, axis=-1, keepdims=True))
    return e / jnp.sum(e, axis=-1, keepdims=True)


def _xattn_fwd(x, g, wq, kv, wo, exchange=None):
    t, d = x.shape
    m = kv.shape[0]
    tm = min(XATTN_TILE, t)

    def body(x_ref, g_ref, wq_ref, kv_ref, wo_ref, xo_ref, hq_ref, qm_ref, att_ref):
        xv = x_ref[...]
        h, _, _ = _rms(xv, g_ref[...])
        hb = h.astype(BF16)
        hq_ref[...] = hb
        qm = _dot(hb, wq_ref[...]).astype(BF16)
        qm_ref[...] = qm
        heads = range(MEM_HEADS)
        kcols = [slice(hd * MEM_HD, (hd + 1) * MEM_HD) for hd in heads]
        p = [_softmax_rows(qm[:, kcols[hd]], kv_ref[:, kcols[hd]]) for hd in heads]
        att = jnp.concatenate(
            [_dot(p[hd].astype(BF16), kv_ref[:, d + hd * MEM_HD:d + (hd + 1) * MEM_HD]) for hd in heads],
            axis=1).astype(BF16)
        att_ref[...] = att
        xo_ref[...] = xv + _dot(att, wo_ref[...])

    return _call(
        body,
        name="xattn_fwd",
        grid=(t // tm,),
        in_specs=[_rows(tm, d), _full((1, d)), _full((d, d)), _full((m, 2 * d)), _full((d, d))],
        out_specs=[_rows(tm, d), _rows(tm, d), _rows(tm, d), _rows(tm, d)],
        out_shape=[
            jax.ShapeDtypeStruct((t, d), F32),
            jax.ShapeDtypeStruct((t, d), BF16),
            jax.ShapeDtypeStruct((t, d), BF16),
            jax.ShapeDtypeStruct((t, d), BF16),
        ],
        args=(x, g, wq, kv, wo),
        exchange=exchange,
    )


def _xattn_bwd(x, g, dxo, qm, kv, wq, wo, exchange=None):
    t, d = x.shape
    m = kv.shape[0]
    tm = min(XATTN_TILE, t)

    def body(x_ref, g_ref, dxo_ref, qm_ref, kv_ref, wq_ref, wo_ref, dx_ref, dqm_ref, dkv_ref, dg_ref):
        _zero_at_start(dkv_ref, dg_ref)
        gv = g_ref[...]
        _, xh, r = _rms(x_ref[...], gv)
        dxo = dxo_ref[...]
        datt = _dot(dxo.astype(BF16), wo_ref[...], NT).astype(BF16)
        heads = range(MEM_HEADS)
        kcols = [slice(hd * MEM_HD, (hd + 1) * MEM_HD) for hd in heads]
        vcols = [slice(d + hd * MEM_HD, d + (hd + 1) * MEM_HD) for hd in heads]
        qm_h = [qm_ref[:, kcols[hd]] for hd in heads]
        p = [_softmax_rows(qm_h[hd], kv_ref[:, kcols[hd]]) for hd in heads]
        dp = [_dot(datt[:, kcols[hd]], kv_ref[:, vcols[hd]], NT) for hd in heads]
        dsc = [(p[hd] * (dp[hd] - jnp.sum(p[hd] * dp[hd], axis=-1, keepdims=True)) * MEM_HD ** -0.5).astype(BF16)
               for hd in heads]
        dqm = jnp.concatenate([_dot(dsc[hd], kv_ref[:, kcols[hd]]) for hd in heads], axis=1).astype(BF16)
        dqm_ref[...] = dqm
        dkv_ref[...] += jnp.concatenate(
            [_dot(dsc[hd], qm_h[hd], TN) for hd in heads]
            + [_dot(p[hd].astype(BF16), datt[:, kcols[hd]], TN) for hd in heads], axis=1)
        dh = _dot(dqm, wq_ref[...], NT)
        dx_ref[...] = _rms_bwd(dh, xh, r, gv) + dxo
        dg_ref[...] += jnp.sum(dh * xh, axis=0, keepdims=True)

    return _call(
        body,
        name="xattn_bwd",
        grid=(t // tm,),
        in_specs=[
            _rows(tm, d), _full((1, d)), _rows(tm, d), _rows(tm, d), _full((m, 2 * d)), _full((d, d)), _full((d, d)),
        ],
        out_specs=[_rows(tm, d), _rows(tm, d), _full((m, 2 * d)), _full((1, d))],
        out_shape=[
            jax.ShapeDtypeStruct((t, d), F32),
            jax.ShapeDtypeStruct((t, d), BF16),
            jax.ShapeDtypeStruct((m, 2 * d), F32),
            jax.ShapeDtypeStruct((1, d), F32),
        ],
        args=(x, g, dxo, qm, kv, wq, wo),
        exchange=exchange,
    )


def _mesh_place():
    x, y, c = lax.axis_index("x"), lax.axis_index("y"), lax.axis_index("c")
    return x, y, c, 4 * x + 2 * y + c


def _peer(x, y, c, k):
    px = 1 - x if k & 4 else x
    py = 1 - y if k & 2 else y
    pc = 1 - c if k & 1 else c
    return (px, py, pc), 4 * px + 2 * py + pc


ICI_HOPS = (2, 4, 6)
N_HOPS = len(ICI_HOPS)


def _remote(src, dst, send_sem, recv_sem, peer):
    return pltpu.make_async_remote_copy(
        src_ref=src, dst_ref=dst, send_sem=send_sem, recv_sem=recv_sem, device_id=peer, device_id_type=MESH_IDS)


def _gather_exchange(shards, middle_eighths=MIDDLE_EIGHTHS):
    n = len(shards)

    def place():
        x, y, c, me = _mesh_place()
        sibling, _ = _peer(x, y, c, 1)
        to_x, from_x = _peer(x, y, c, 4)
        to_y, from_y = _peer(x, y, c, 2)
        _, from_diagonal = _peer(x, y, c, 6)
        onward = (c * to_y[0] + (1 - c) * to_x[0], c * to_y[1] + (1 - c) * to_x[1], c)
        passed_on = c * from_x + (1 - c) * from_y
        return me, sibling, (to_x, to_y, onward), (from_x, from_y, from_diagonal), passed_on

    def start(src, dst, sems):
        ici_send, ici_recv, pair_send, pair_recv, local = sems
        me, sibling, targets, _, _ = place()
        for a in range(n):
            pltpu.make_async_copy(src[a], dst[a].at[me], local.at[a]).start()
            for j in range(2):
                _remote(src[a], dst[a].at[me], ici_send.at[a, j], ici_recv.at[a, j], targets[j]).start()
            _remote(src[a], dst[a].at[me], pair_send.at[a, 0], pair_recv.at[a, 0], sibling).start()

    def to_sibling(dst, sems, a, j, origin, sibling):
        _, _, pair_send, pair_recv, _ = sems
        slot = dst[a].at[origin]
        return _remote(slot, slot, pair_send.at[a, 1 + j], pair_recv.at[a, 1 + j], sibling)

    def middle(src, dst, sems):
        ici_send, ici_recv, _, _, _ = sems
        _, sibling, targets, origins, passed_on = place()
        for a in range(n):
            for j in range(2):
                _remote(src[a], dst[a].at[origins[j]], ici_send.at[a, j], ici_recv.at[a, j], targets[j]).wait_recv()
            slot = dst[a].at[passed_on]
            _remote(slot, slot, ici_send.at[a, 2], ici_recv.at[a, 2], targets[2]).start()
            for j in range(2):
                to_sibling(dst, sems, a, j, origins[j], sibling).start()

    def finish(src, dst, sems):
        ici_send, ici_recv, pair_send, pair_recv, local = sems
        me, sibling, targets, origins, _ = place()
        for a in range(n):
            _remote(src[a], dst[a].at[origins[2]], ici_send.at[a, 2], ici_recv.at[a, 2], targets[2]).wait_recv()
            to_sibling(dst, sems, a, 2, origins[2], sibling).start()
        for a in range(n):
            pltpu.make_async_copy(src[a], dst[a].at[me], local.at[a]).wait()
            for j in range(N_HOPS):
                _remote(src[a], dst[a].at[me], ici_send.at[a, j], ici_recv.at[a, j], targets[j]).wait_send()
            for j, origin in enumerate((me,) + origins):
                from_sibling = origin + 1 - 2 * (origin % 2)
                passed = _remote(src[a], dst[a].at[from_sibling], pair_send.at[a, j], pair_recv.at[a, j], sibling)
                passed.wait_send()
                passed.wait_recv()

    return _Exchange(
        shards,
        [jax.ShapeDtypeStruct((N_DEV,) + s.shape, s.dtype) for s in shards],
        [
            pltpu.SemaphoreType.DMA((n, N_HOPS)), pltpu.SemaphoreType.DMA((n, N_HOPS)),
            pltpu.SemaphoreType.DMA((n, N_HOPS + 1)), pltpu.SemaphoreType.DMA((n, N_HOPS + 1)),
            pltpu.SemaphoreType.DMA((n,)),
        ],
        start, finish, middle, middle_eighths)


def _scatter_copies(src, dst, sems, n, arrivals=False):
    send, recv, local = sems
    x, y, c, _ = _mesh_place()
    chip = 2 * x + y
    if arrivals is None:
        return [pltpu.make_async_copy(src[a].at[chip], dst[a].at[chip], local.at[a]) for a in range(n)]
    copies = []
    for a in range(n):
        for j, k in enumerate(ICI_HOPS):
            peer, _ = _peer(x, y, c, k)
            peer_chip = 2 * peer[0] + peer[1]
            slot = dst[a].at[peer_chip if arrivals else chip]
            copies.append(_remote(src[a].at[peer_chip], slot, send.at[a, j], recv.at[a, j], peer))
    return copies


def _scatter_start(src, dst, sems, n):
    for cp in _scatter_copies(src, dst, sems, n, arrivals=None) + _scatter_copies(src, dst, sems, n):
        cp.start()


def _scatter_finish(src, dst, sems, n):
    for cp in _scatter_copies(src, dst, sems, n, arrivals=None):
        cp.wait()
    for cp in _scatter_copies(src, dst, sems, n):
        cp.wait_send()
    for cp in _scatter_copies(src, dst, sems, n, arrivals=True):
        cp.wait_recv()


def _scatter_scratch(n):
    return [pltpu.SemaphoreType.DMA((n, N_HOPS)), pltpu.SemaphoreType.DMA((n, N_HOPS)), pltpu.SemaphoreType.DMA((n,))]


def _scatter_exchange(partials):
    n = len(partials)
    return _Exchange(
        partials, [jax.ShapeDtypeStruct(p.shape, p.dtype) for p in partials], _scatter_scratch(n),
        lambda src, dst, sems: _scatter_start(src, dst, sems, n),
        lambda src, dst, sems: _scatter_finish(src, dst, sems, n))


SMALL_LAYOUT = {
    "ffn1_norm": (0, 1, 1024), "mix_norm": (1, 1, 1024), "xattn_norm": (2, 1, 1024), "mem_norm": (3, 1, 1024),
    "ffn2_norm": (4, 1, 1024), "final_norm": (5, 1, 1024), "lb_param": (6, 2, 512), "hgrn_out_norm": (8, 1, 512),
    "conv_w": (9, 3, 512), "loss": (12, 1, 128),
}


def _final_exchange(partials, small):
    n = len(partials)
    names = list(small)
    width = 1024

    def body(*refs):
        src = refs[:n]
        pieces = refs[n:n + len(names)]
        dst = refs[n + len(names):2 * n + len(names)]
        total_ref = refs[2 * n + len(names)]
        pack, gathered, small_send, small_recv = refs[2 * n + len(names) + 1:2 * n + len(names) + 5]
        sems = refs[2 * n + len(names) + 5:]
        x, y, c, me = _mesh_place()
        pack[...] = jnp.zeros_like(pack)
        for name, piece in zip(names, pieces):
            row, nrows, ncols = SMALL_LAYOUT[name]
            pack[row:row + nrows, 0:ncols] = piece[...]
        for k in range(1, N_DEV):
            peer, _ = _peer(x, y, c, k)
            _remote(pack, gathered.at[me], small_send.at[k - 1], small_recv.at[k - 1], peer).start()
        _scatter_start(src, dst, sems, n)
        gathered[me] = pack[...]
        for k in range(1, N_DEV):
            peer, peer_index = _peer(x, y, c, k)
            landed = _remote(pack, gathered.at[peer_index], small_send.at[k - 1], small_recv.at[k - 1], peer)
            landed.wait_send()
            landed.wait_recv()
        total = gathered[0]
        for j in range(1, N_DEV):
            total = total + gathered[j]
        total_ref[...] = total
        _scatter_finish(src, dst, sems, n)

    hbm = pl.BlockSpec(memory_space=pltpu.HBM)
    vmem = pl.BlockSpec(memory_space=pltpu.VMEM)
    out = pl.pallas_call(
        body,
        name="final_exchange",
        in_specs=[hbm] * n + [vmem] * len(names),
        out_specs=[hbm] * n + [vmem],
        out_shape=[jax.ShapeDtypeStruct(p.shape, p.dtype) for p in partials]
        + [jax.ShapeDtypeStruct((SMALL_ROWS, width), F32)],
        scratch_shapes=[
            pltpu.VMEM((SMALL_ROWS, width), F32), pltpu.VMEM((N_DEV, SMALL_ROWS, width), F32),
            pltpu.SemaphoreType.DMA((N_DEV - 1,)), pltpu.SemaphoreType.DMA((N_DEV - 1,)),
        ] + _scatter_scratch(n),
        compiler_params=pltpu.CompilerParams(has_side_effects=True),
    )(*partials, *[small[k] for k in names])
    return out[:n], out[n]


def _adamw_math(w, g, m, v):
    m = ADAM_B1 * m + (1.0 - ADAM_B1) * g
    v = ADAM_B2 * v + (1.0 - ADAM_B2) * (g * g)
    m_hat = m / (1.0 - ADAM_B1 ** ADAM_STEP)
    v_hat = v / (1.0 - ADAM_B2 ** ADAM_STEP)
    delta = -ADAM_LR * (m_hat / (jnp.sqrt(v_hat) + ADAM_EPS) + ADAM_WD * w)
    return delta, m, v


def _adamw_shard(parts, w, m, v):
    r, c = w.shape
    n_parts = parts.shape[0]
    tr = max(rows for rows in range(16, r + 1, 16) if r % rows == 0 and rows * c <= ADAMW_TILE_ELEMENTS)

    def body(p_ref, w_ref, m_ref, v_ref, g_ref, d_ref, mo_ref, vo_ref):
        g = p_ref[0].astype(F32)
        for j in range(1, n_parts):
            g = g + p_ref[j].astype(F32)
        delta, mn, vn = _adamw_math(w_ref[...], g, m_ref[...], v_ref[...])
        g_ref[...] = g
        d_ref[...] = delta
        mo_ref[...] = mn
        vo_ref[...] = vn

    tile = pl.BlockSpec((tr, c), lambda i: (i, 0))
    return pl.pallas_call(
        body,
        name="adamw_shard",
        grid=(r // tr,),
        in_specs=[pl.BlockSpec((n_parts, tr, c), lambda i: (0, i, 0)), tile, tile, tile],
        out_specs=[tile] * 4,
        out_shape=[jax.ShapeDtypeStruct((r, c), F32)] * 4,
        compiler_params=_params(("parallel",)),
    )(parts, w, m, v)


def _adamw_small(gs, ws, ms, vs):
    n = len(gs)

    def body(*refs):
        g_refs, w_refs, m_refs, v_refs = refs[:n], refs[n:2 * n], refs[2 * n:3 * n], refs[3 * n:4 * n]
        g_out, d_out, m_out, v_out = refs[4 * n:5 * n], refs[5 * n:6 * n], refs[6 * n:7 * n], refs[7 * n:8 * n]
        for i in range(n):
            if gs[i].ndim == ws[i].ndim:
                g = g_refs[i][...]
            else:
                g = g_refs[i][0].astype(F32)
                for j in range(1, gs[i].shape[0]):
                    g = g + g_refs[i][j].astype(F32)
            delta, mn, vn = _adamw_math(w_refs[i][...], g, m_refs[i][...], v_refs[i][...])
            g_out[i][...] = g
            d_out[i][...] = delta
            m_out[i][...] = mn
            v_out[i][...] = vn

    shapes = [jax.ShapeDtypeStruct(w.shape, F32) for w in ws]
    out = pl.pallas_call(
        body,
        name="adamw_small",
        out_shape=shapes * 4,
        compiler_params=_params(),
    )(*gs, *ws, *ms, *vs)
    return out[:n], out[n:2 * n], out[2 * n:3 * n], out[3 * n:]


TRANSPOSED = ("ffn1_gate", "ffn1_up", "w_in", "ffn2_gate", "ffn2_up", "conv_w")
GROUP_FFN1 = ("ffn1_gate", "ffn1_up", "ffn1_down")
GROUP_MIX = ("w_in", "w_out")
GROUP_XATTN = ("w_q_mem", "w_kv_mem", "w_o_mem")
GROUP_FFN2 = ("ffn2_gate", "ffn2_up", "ffn2_down")
LARGE = GROUP_FFN1 + GROUP_MIX + GROUP_XATTN + GROUP_FFN2
SHORT_SHARDS = ("w_out", "w_q_mem", "w_kv_mem", "w_o_mem")
SMALL = ("ffn1_norm", "mix_norm", "lb_param", "hgrn_out_norm", "conv_w", "xattn_norm", "mem_norm", "ffn2_norm",
         "final_norm")
WEIGHTS = ("ffn1_norm", "ffn1_gate", "ffn1_up", "ffn1_down", "mix_norm", "w_in", "lb_param", "hgrn_out_norm",
           "conv_w", "w_out", "xattn_norm", "mem_norm", "w_q_mem", "w_kv_mem", "w_o_mem", "ffn2_norm", "ffn2_gate",
           "ffn2_up", "ffn2_down", "final_norm")


def kernel(x, mem, ffn1_norm, ffn1_gate, ffn1_up, ffn1_down, mix_norm, w_in, lb_param, hgrn_out_norm, conv_w, w_out, xattn_norm, mem_norm, w_q_mem, w_kv_mem, w_o_mem, ffn2_norm, ffn2_gate, ffn2_up, ffn2_down, final_norm, loss_target, m_ffn1_norm, m_ffn1_gate, m_ffn1_up, m_ffn1_down, m_mix_norm, m_w_in, m_lb_param, m_hgrn_out_norm, m_conv_w, m_w_out, m_xattn_norm, m_mem_norm, m_w_q_mem, m_w_kv_mem, m_w_o_mem, m_ffn2_norm, m_ffn2_gate, m_ffn2_up, m_ffn2_down, m_final_norm, v_ffn1_norm, v_ffn1_gate, v_ffn1_up, v_ffn1_down, v_mix_norm, v_w_in, v_lb_param, v_hgrn_out_norm, v_conv_w, v_w_out, v_xattn_norm, v_mem_norm, v_w_q_mem, v_w_kv_mem, v_w_o_mem, v_ffn2_norm, v_ffn2_gate, v_ffn2_up, v_ffn2_down, v_final_norm):
    given = dict(locals())
    me = 4 * lax.axis_index("x") + 2 * lax.axis_index("y") + lax.axis_index("c")
    x0, memv, target = x[0], mem[0], loss_target[0]

    def shard(prefix, name):
        v = given[prefix + name]
        if v.ndim == 1:
            return v.reshape(1, -1)
        if v.ndim == 2:
            return v
        return v[0].T if name in TRANSPOSED else v[0]

    w = {name: shard("", name) for name in WEIGHTS}
    m = {name: shard("m_", name) for name in WEIGHTS}
    v = {name: shard("v_", name) for name in WEIGHTS}

    conv_taps, conv_rows = w["conv_w"].shape
    conv_tile = jnp.pad(w["conv_w"], ((0, 8 - conv_taps), (0, 128 - conv_rows)))
    wire = {name: w[name].astype(BF16) for name in LARGE}
    full = {}

    def landed(names, gathered):
        for name, blocks in zip(names, gathered):
            _, r, c = blocks.shape
            full[name] = blocks if name == "w_kv_mem" else blocks.reshape(N_DEV * r, c)

    first = ("ffn1_gate", "ffn1_up")
    landed(first, _run_exchange(_gather_exchange([wire[k] for k in first]), "gather_first"))

    riders = (("ffn1_down", "w_in"), ("w_out", "w_kv_mem"), ("w_q_mem", "w_o_mem", "ffn2_gate", "ffn2_up"),
              ("ffn2_down",))
    (a1, b1, s1), gathered = _ffn_up(
        x0, w["ffn1_norm"], full["ffn1_gate"], full["ffn1_up"],
        exchange=_gather_exchange([wire[k] for k in riders[0]]))
    landed(riders[0], gathered)
    (x1,), gathered = _ffn_down(
        x0, s1, full["ffn1_down"], exchange=_gather_exchange([wire[k] for k in riders[1]] + [conv_tile]))
    landed(riders[1], gathered)
    convw_t = gathered[-1][:, :conv_taps, :conv_rows].transpose(1, 0, 2).reshape(conv_taps, N_DEV * conv_rows)
    (x2, z, o_raw, states, ycat), gathered = _mix_fwd(
        x1, w["mix_norm"], full["w_in"], w["lb_param"], w["hgrn_out_norm"], convw_t, full["w_out"],
        exchange=_gather_exchange([wire[k] for k in riders[2]]))
    landed(riders[2], gathered)
    kv = _memkv_fwd(memv, w["mem_norm"], full["w_kv_mem"])
    (x3, hq, qm, att), gathered = _xattn_fwd(
        x2, w["xattn_norm"], full["w_q_mem"], kv, full["w_o_mem"],
        exchange=_gather_exchange([wire[k] for k in riders[3]], middle_eighths=EARLY_MIDDLE_EIGHTHS))
    landed(riders[3], gathered)
    (dx4, a2, b2, s2, loss_part, d_final), _ = _ffn_fwd(
        x3, w["ffn2_norm"], full["ffn2_gate"], full["ffn2_up"], full["ffn2_down"], head=(w["final_norm"], target))

    parts = {}
    waiting = []

    def carried():
        names = [name for name, _ in waiting]
        exchange = _scatter_exchange([p for _, p in waiting]) if waiting else None
        del waiting[:]
        return names, exchange

    def weight_grad(name, a, b, scale=1.0):
        names, exchange = carried()
        partial, arrived = _weight_grad(a, b, scale, exchange=exchange)
        parts.update(zip(names, arrived))
        waiting.append((name, partial))

    (dx3, da2, db2, h4, d_ffn2_norm), _ = _ffn_bwd(
        x3, w["ffn2_norm"], dx4, a2, b2, full["ffn2_gate"], full["ffn2_up"], full["ffn2_down"])
    weight_grad("ffn2_down", s2, dx4, 0.5)
    weight_grad("ffn2_gate", da2, h4)
    weight_grad("ffn2_up", db2, h4)
    names, exchange = carried()
    (dx2, dqm, dkv, d_xattn_norm), arrived = _xattn_bwd(
        x2, w["xattn_norm"], dx3, qm, kv, full["w_q_mem"], full["w_o_mem"], exchange=exchange)
    parts.update(zip(names, arrived))
    weight_grad("w_o_mem", att, dx3)
    weight_grad("w_q_mem", hq, dqm)
    d_wkv, d_mem_norm = _memkv_bwd(memv, w["mem_norm"], dkv, full["w_kv_mem"])
    waiting.append(("w_kv_mem", d_wkv))
    names, exchange = carried()
    (dx1, dz, h2, d_mix_norm, d_lbp, d_gh, d_convw_t), arrived = _mix_bwd(
        x1, w["mix_norm"], dx2, z, o_raw, states, full["w_in"], w["lb_param"], w["hgrn_out_norm"], convw_t,
        full["w_out"], exchange=exchange)
    parts.update(zip(names, arrived))
    weight_grad("w_in", dz, h2)
    weight_grad("ffn1_down", s1, dx1, 0.5)
    (dx0, da1, db1, h1, d_ffn1_norm), _ = _ffn_bwd(
        x0, w["ffn1_norm"], dx1, a1, b1, full["ffn1_gate"], full["ffn1_up"], full["ffn1_down"])
    weight_grad("ffn1_gate", da1, h1)
    weight_grad("ffn1_up", db1, h1)
    weight_grad("w_out", ycat, dx2)

    small_parts = {
        "ffn1_norm": d_ffn1_norm, "mix_norm": d_mix_norm, "xattn_norm": d_xattn_norm, "mem_norm": d_mem_norm,
        "ffn2_norm": d_ffn2_norm, "final_norm": d_final, "lb_param": d_lbp, "hgrn_out_norm": d_gh,
        "conv_w": d_convw_t, "loss": loss_part,
    }
    names = [name for name, _ in waiting]
    arrived, total = _final_exchange([p for _, p in waiting], small_parts)
    parts.update(zip(names, arrived))

    g_out, d_out, m_out, v_out = {}, {}, {}, {}
    for name in LARGE:
        if name not in SHORT_SHARDS:
            g_out[name], d_out[name], m_out[name], v_out[name] = _adamw_shard(parts[name], w[name], m[name], v[name])
    g_small = {name: parts[name] for name in SHORT_SHARDS}
    for name in SMALL:
        row, nrows, ncols = SMALL_LAYOUT[name]
        g_small[name] = total[row:row + nrows, 0:ncols]
    g_small["conv_w"] = lax.dynamic_slice_in_dim(g_small["conv_w"], me * conv_rows, conv_rows, axis=1)
    together = SMALL + SHORT_SHARDS
    gs, ds, ms, vs = _adamw_small(
        [g_small[k] for k in together], [w[k] for k in together], [m[k] for k in together],
        [v[k] for k in together])
    for i, name in enumerate(together):
        g_out[name], d_out[name], m_out[name], v_out[name] = gs[i], ds[i], ms[i], vs[i]

    def shaped(value, name):
        return (value.T if name in TRANSPOSED else value).reshape(given[name].shape)

    loss = total[SMALL_LAYOUT["loss"][0], 0]
    outs = [loss, dx0.reshape(x.shape)]
    for group in (g_out, d_out, m_out, v_out):
        outs += [shaped(group[name], name) for name in WEIGHTS]
    return tuple(outs)
```

```python
import jax
import jax.numpy as jnp
from jax import lax
from jax.experimental import pallas as pl
from jax.experimental.pallas import tpu as pltpu

F32 = jnp.float32
BF16 = jnp.bfloat16
MESH_IDS = pl.DeviceIdType.MESH

N_DEV = 8
EPS = 1e-6
HGRN_HEADS = 4
HGRN_DK = 128
HGRN_W = 512
CHUNK = 64
MEM_HEADS = 4
MEM_HD = 256
ADAM_LR = 0.001
ADAM_B1 = 0.9
ADAM_B2 = 0.999
ADAM_EPS = 1e-08
ADAM_WD = 0.01
ADAM_STEP = 10

TOKEN_TILE = 256
XATTN_TILE = 512
WIDE_TILE = 512
REDUCE_TILE = 1024
ADAMW_TILE_ELEMENTS = 256 * 1024
MIDDLE_EIGHTHS = 5
EARLY_MIDDLE_EIGHTHS = 4
MXU_ROWS = 256
VMEM_LIMIT = 60 * 1024 * 1024
SMALL_ROWS = 16
NT = (((1,), (1,)), ((), ()))
TN = (((0,), (0,)), ((), ()))


def _params(sem=None):
    return pltpu.CompilerParams(dimension_semantics=sem, vmem_limit_bytes=VMEM_LIMIT)


def _dot(a, b, dims=None):
    if dims is None:
        return jnp.dot(a, b, preferred_element_type=F32)
    return lax.dot_general(a, b, dims, preferred_element_type=F32)


def _sigmoid(v):
    return 1.0 / (1.0 + jnp.exp(-v))


def _rms(x, g):
    r = lax.rsqrt(jnp.mean(x * x, axis=-1, keepdims=True) + EPS)
    xh = x * r
    return xh * g, xh, r


def _rms_bwd(dh, xh, r, g):
    dxh = dh * g
    return r * (dxh - xh * jnp.mean(dxh * xh, axis=-1, keepdims=True))


def _full(shape):
    return pl.BlockSpec(shape, lambda *_: (0,) * len(shape))


def _full_once(shape):
    return pl.BlockSpec(shape, lambda *_: (0,) * len(shape), pipeline_mode=pl.Buffered(1))


def _rows(tm, width):
    return pl.BlockSpec((tm, width), lambda i: (i, 0))


def _rows_rev(tm, width, n):
    return pl.BlockSpec((tm, width), lambda i: (n - 1 - i, 0))


def _zero_at_start(*refs):
    @pl.when(pl.program_id(0) == 0)
    def _():
        for ref in refs:
            ref[...] = jnp.zeros_like(ref)


class _Exchange:
    def __init__(self, operands, out_shapes, scratch, start, finish, middle=None, middle_eighths=MIDDLE_EIGHTHS):
        self.operands, self.out_shapes, self.scratch = list(operands), list(out_shapes), list(scratch)
        self.start, self.middle, self.finish, self.middle_eighths = start, middle, finish, middle_eighths


def _call(body, *, name, grid, in_specs, out_specs, out_shape, args, scratch_shapes=(), exchange=None):
    semantics = ("arbitrary",) * len(grid)
    if exchange is None:
        out = pl.pallas_call(
            body, name=name, grid=grid, in_specs=in_specs, out_specs=out_specs, out_shape=out_shape,
            scratch_shapes=list(scratch_shapes), compiler_params=_params(semantics))(*args)
        return out, []
    hbm = pl.BlockSpec(memory_space=pltpu.HBM)
    n_in, n_out, n_scr = len(in_specs), len(out_specs), len(scratch_shapes)
    e_in, e_out = len(exchange.operands), len(exchange.out_shapes)

    def carried(*refs):
        ins, rest = refs[:n_in], refs[n_in:]
        e_ins, rest = rest[:e_in], rest[e_in:]
        outs, rest = rest[:n_out], rest[n_out:]
        e_outs, rest = rest[:e_out], rest[e_out:]
        scr, e_scr = rest[:n_scr], rest[n_scr:]
        first = last = None
        for axis, size in enumerate(grid):
            at_start, at_end = pl.program_id(axis) == 0, pl.program_id(axis) == size - 1
            first = at_start if first is None else jnp.logical_and(first, at_start)
            last = at_end if last is None else jnp.logical_and(last, at_end)

        @pl.when(first)
        def _():
            exchange.start(e_ins, e_outs, e_scr)

        body(*ins, *outs, *scr)

        if exchange.middle is not None:
            assert len(grid) == 1

            @pl.when(pl.program_id(0) == (grid[0] * exchange.middle_eighths) // 8)
            def _():
                exchange.middle(e_ins, e_outs, e_scr)

        @pl.when(last)
        def _():
            exchange.finish(e_ins, e_outs, e_scr)

    out = pl.pallas_call(
        carried, name=name, grid=grid, in_specs=list(in_specs) + [hbm] * e_in,
        out_specs=list(out_specs) + [hbm] * e_out, out_shape=list(out_shape) + exchange.out_shapes,
        scratch_shapes=list(scratch_shapes) + exchange.scratch,
        compiler_params=pltpu.CompilerParams(
            dimension_semantics=semantics, vmem_limit_bytes=VMEM_LIMIT, has_side_effects=True),
    )(*args, *exchange.operands)
    return out[:n_out], out[n_out:]


def _run_exchange(exchange, name):
    hbm = pl.BlockSpec(memory_space=pltpu.HBM)
    e_in, e_out = len(exchange.operands), len(exchange.out_shapes)

    def body(*refs):
        e_ins, e_outs, e_scr = refs[:e_in], refs[e_in:e_in + e_out], refs[e_in + e_out:]
        exchange.start(e_ins, e_outs, e_scr)
        if exchange.middle is not None:
            exchange.middle(e_ins, e_outs, e_scr)
        exchange.finish(e_ins, e_outs, e_scr)

    return pl.pallas_call(
        body, name=name, in_specs=[hbm] * e_in, out_specs=[hbm] * e_out, out_shape=exchange.out_shapes,
        scratch_shapes=exchange.scratch, compiler_params=pltpu.CompilerParams(has_side_effects=True),
    )(*exchange.operands)


def _loss_head(xo, gf, tgt):
    d = xo.shape[1]
    y, xh, r = _rms(xo, gf)
    err = y - tgt
    dy = err * (1.0 / d)
    loss = 0.5 * jnp.sum(jnp.sum(err * err, axis=-1, keepdims=True) * (1.0 / d), axis=0, keepdims=True)
    return _rms_bwd(dy, xh, r, gf), loss, jnp.sum(dy * xh, axis=0, keepdims=True)


def _ffn_fwd(x, g, wg, wu, wd, exchange=None, head=None):
    t, d = x.shape
    f = wg.shape[0]
    tm = min(WIDE_TILE, t)

    def body(x_ref, g_ref, wg_ref, wu_ref, wd_ref, *rest):
        if head is None:
            xo_ref, a_ref, b_ref, s_ref = rest
        else:
            gf_ref, tgt_ref, xo_ref, a_ref, b_ref, s_ref, loss_ref, dgf_ref = rest
            _zero_at_start(loss_ref, dgf_ref)
        xv = x_ref[...]
        h, _, _ = _rms(xv, g_ref[...])
        hb = h.astype(BF16)
        a = _dot(hb, wg_ref[...], NT)
        b = _dot(hb, wu_ref[...], NT)
        s = (a * _sigmoid(a) * b).astype(BF16)
        xo = xv + 0.5 * _dot(s, wd_ref[...])
        if head is None:
            xo_ref[...] = xo
        else:
            xo_ref[...], loss, dgf = _loss_head(xo, gf_ref[...], tgt_ref[...])
            loss_ref[...] += jnp.broadcast_to(loss, (1, 128))
            dgf_ref[...] += dgf
        a_ref[...] = a.astype(BF16)
        b_ref[...] = b.astype(BF16)
        s_ref[...] = s

    in_specs = [_rows(tm, d), _full((1, d)), _full_once((f, d)), _full_once((f, d)), _full_once((f, d))]
    out_specs = [_rows(tm, d), _rows(tm, f), _rows(tm, f), _rows(tm, f)]
    out_shape = [
        jax.ShapeDtypeStruct((t, d), F32),
        jax.ShapeDtypeStruct((t, f), BF16),
        jax.ShapeDtypeStruct((t, f), BF16),
        jax.ShapeDtypeStruct((t, f), BF16),
    ]
    args = (x, g, wg, wu, wd)
    if head is not None:
        in_specs += [_full((1, d)), _rows(tm, d)]
        out_specs += [_full((1, 128)), _full((1, d))]
        out_shape += [jax.ShapeDtypeStruct((1, 128), F32), jax.ShapeDtypeStruct((1, d), F32)]
        args += tuple(head)
    return _call(
        body, name="ffn_fwd", grid=(t // tm,), in_specs=in_specs, out_specs=out_specs, out_shape=out_shape,
        args=args, exchange=exchange)


def _ffn_up(x, g, wg, wu, exchange=None):
    t, d = x.shape
    f = wg.shape[0]
    tm = min(TOKEN_TILE, t)

    def body(x_ref, g_ref, wg_ref, wu_ref, a_ref, b_ref, s_ref):
        h, _, _ = _rms(x_ref[...], g_ref[...])
        hb = h.astype(BF16)
        a = _dot(hb, wg_ref[...], NT)
        b = _dot(hb, wu_ref[...], NT)
        a_ref[...] = a.astype(BF16)
        b_ref[...] = b.astype(BF16)
        s_ref[...] = (a * _sigmoid(a) * b).astype(BF16)

    return _call(
        body, name="ffn_up", grid=(t // tm,),
        in_specs=[_rows(tm, d), _full((1, d)), _full_once((f, d)), _full_once((f, d))],
        out_specs=[_rows(tm, f)] * 3, out_shape=[jax.ShapeDtypeStruct((t, f), BF16)] * 3,
        args=(x, g, wg, wu), exchange=exchange)


def _ffn_down(x, s, wd, exchange=None):
    t, d = x.shape
    f = wd.shape[0]
    tm = min(TOKEN_TILE, t)

    def body(x_ref, s_ref, wd_ref, xo_ref):
        xo_ref[...] = x_ref[...] + 0.5 * _dot(s_ref[...], wd_ref[...])

    return _call(
        body, name="ffn_down", grid=(t // tm,),
        in_specs=[_rows(tm, d), _rows(tm, f), _full_once((f, d))],
        out_specs=[_rows(tm, d)], out_shape=[jax.ShapeDtypeStruct((t, d), F32)],
        args=(x, s, wd), exchange=exchange)


def _ffn_bwd(x, g, dxo, a, b, wg, wu, wd, exchange=None):
    t, d = x.shape
    f = wg.shape[0]
    tm = min(TOKEN_TILE, t)

    def body(x_ref, g_ref, dxo_ref, a_ref, b_ref, wg_ref, wu_ref, wd_ref, dx_ref, da_ref, db_ref, h_ref, dg_ref):
        _zero_at_start(dg_ref)
        gv = g_ref[...]
        h, xh, r = _rms(x_ref[...], gv)
        dxo = dxo_ref[...]
        ds = _dot((0.5 * dxo).astype(BF16), wd_ref[...], NT)
        af = a_ref[...].astype(F32)
        bf = b_ref[...].astype(F32)
        sg = _sigmoid(af)
        da = (ds * bf * (sg * (1.0 + af * (1.0 - sg)))).astype(BF16)
        db = (ds * (af * sg)).astype(BF16)
        dh = _dot(da, wg_ref[...]) + _dot(db, wu_ref[...])
        dx_ref[...] = _rms_bwd(dh, xh, r, gv) + dxo
        da_ref[...] = da
        db_ref[...] = db
        h_ref[...] = h.astype(BF16)
        dg_ref[...] += jnp.sum(dh * xh, axis=0, keepdims=True)

    return _call(
        body,
        name="ffn_bwd",
        grid=(t // tm,),
        in_specs=[
            _rows(tm, d), _full((1, d)), _rows(tm, d), _rows(tm, f), _rows(tm, f),
            _full_once((f, d)), _full_once((f, d)), _full_once((f, d)),
        ],
        out_specs=[_rows(tm, d), _rows(tm, f), _rows(tm, f), _rows(tm, d), _full((1, d))],
        out_shape=[
            jax.ShapeDtypeStruct((t, d), F32),
            jax.ShapeDtypeStruct((t, f), BF16),
            jax.ShapeDtypeStruct((t, f), BF16),
            jax.ShapeDtypeStruct((t, d), BF16),
            jax.ShapeDtypeStruct((1, d), F32),
        ],
        args=(x, g, dxo, a, b, wg, wu, wd),
        exchange=exchange,
    )


def _weight_grad(products, exchange=None):
    count = len(products)
    t, m = products[0][0].shape
    n = products[0][1].shape[1]
    assert all(a.shape == (t, m) and b.shape == (t, n) for a, b, _ in products)
    chips = N_DEV // 2
    r = m // N_DEV
    tk = min(REDUCE_TILE, t)
    halves = 2
    nb = n // halves
    nk = t // tk

    def body(*refs):
        a_refs, b_refs, o_refs = refs[0:count], refs[count:2 * count], refs[2 * count:3 * count]
        acc, send_buf, recv_buf, send_sems, recv_sems = refs[3 * count:]
        k, j = pl.program_id(0), pl.program_id(1)
        x, y, c, _ = _mesh_place()
        sibling, _ = _peer(x, y, c, 1)
        for p, (_, _, scale) in enumerate(products):
            bv = b_refs[p][...]
            if scale != 1.0:
                bv = bv * scale
            bb = bv.astype(BF16)
            acc_half = acc.at[p, j]

            @pl.when(k == 0)
            def _():
                acc_half[...] = jnp.zeros_like(acc_half)

            for i in range(m // MXU_ROWS):
                rows = slice(i * MXU_ROWS, (i + 1) * MXU_ROWS)
                acc_half[rows, :] += _dot(a_refs[p][:, rows].astype(BF16), bb, TN)

        def to_sibling(half):
            return _remote(send_buf.at[half], recv_buf.at[half], send_sems.at[half], recv_sems.at[half], sibling)

        def owned_rows(q, core):
            return pl.ds(pl.multiple_of((2 * q + core) * r, 8), r)

        for half in range(halves):
            @pl.when(jnp.logical_and(k == nk - 1, j == half))
            def _():
                for p in range(count):
                    for q in range(chips):
                        send_buf[half, p, q] = acc[p, half, owned_rows(q, 1 - c), :].astype(BF16)
                to_sibling(half).start()

        @pl.when(jnp.logical_and(k == nk - 1, j == halves - 1))
        def _():
            for half in range(halves):
                to_sibling(half).wait_send()
                to_sibling(half).wait_recv()
                for p in range(count):
                    for q in range(chips):
                        o_refs[p][q, :, half * nb:(half + 1) * nb] = (
                            acc[p, half, owned_rows(q, c), :] + recv_buf[half, p, q].astype(F32)).astype(BF16)

    partials, arrived = _call(
        body,
        name="weight_grad",
        grid=(nk, halves),
        in_specs=[pl.BlockSpec((tk, m), lambda k, j: (k, 0))] * count
        + [pl.BlockSpec((tk, nb), lambda k, j: (k, j))] * count,
        out_specs=[pl.BlockSpec((chips, r, n), lambda k, j: (0, 0, 0))] * count,
        out_shape=[jax.ShapeDtypeStruct((chips, r, n), BF16)] * count,
        scratch_shapes=[
            pltpu.VMEM((count, halves, m, nb), F32),
            pltpu.VMEM((halves, count, chips, r, nb), BF16), pltpu.VMEM((halves, count, chips, r, nb), BF16),
            pltpu.SemaphoreType.DMA((halves,)), pltpu.SemaphoreType.DMA((halves,)),
        ],
        args=tuple(a for a, _, _ in products) + tuple(b for _, b, _ in products),
        exchange=exchange,
    )
    return partials, arrived


def _chunk_cumsum(v, reverse=False):
    n, width = v.shape
    row = lax.broadcasted_iota(jnp.int32, (n, n), 0)
    col = lax.broadcasted_iota(jnp.int32, (n, n), 1)
    earlier = col >= row if reverse else col <= row
    tri = jnp.where(jnp.logical_and(row // CHUNK == col // CHUNK, earlier), 1.0, 0.0).astype(BF16)
    hi = v.astype(BF16)
    rest = v - hi.astype(F32)
    mid = rest.astype(BF16)
    low = (rest - mid.astype(F32)).astype(BF16)
    sums = _dot(tri, jnp.concatenate([hi, mid, low], axis=1))
    return sums[:, 0:width] + sums[:, width:2 * width] + sums[:, 2 * width:3 * width]


def _shift_rows(v, shift, edge):
    n = v.shape[0]
    row = lax.broadcasted_iota(jnp.int32, (n, 1), 0)
    out = pltpu.roll(v, shift % n, axis=0)
    if shift > 0:
        for j in range(shift):
            out = jnp.where(row == j, edge[8 - shift + j:8 - shift + j + 1, :], out)
    else:
        for j in range(-shift):
            out = jnp.where(row == n + shift + j, edge[j:j + 1, :], out)
    return out


def _gates(z, lbp):
    w = HGRN_W
    lb = _sigmoid(lbp[0:1, :] - lbp[1:2, :])
    zq = z[:, 0:w]
    sig = _sigmoid(z[:, w:2 * w])
    f = lb + (1.0 - lb) * sig
    sq = _sigmoid(zq)
    q = zq * sq * HGRN_DK ** -0.5
    return lb, sig, f, sq, q


def _decayed_operands(q, f, v, qm_buf, km_buf, kbar_buf, v_buf, etot_buf, emid_buf):
    n, width = f.shape
    bcum = _chunk_cumsum(jnp.log(f))

    def row_of_chunk(offset):
        return jnp.concatenate(
            [jnp.broadcast_to(bcum[c + offset:c + offset + 1, :], (CHUNK, width)) for c in range(0, n, CHUNK)], axis=0)

    total, mid = row_of_chunk(CHUNK - 1), row_of_chunk(CHUNK // 2 - 1)
    em, enm, erest = jnp.exp(bcum - mid), jnp.exp(mid - bcum), jnp.exp(total - bcum)
    kk = 1.0 - f
    qm_buf[...] = (q * em).astype(BF16)
    km_buf[...] = (kk * enm).astype(BF16)
    kbar_buf[...] = (kk * erest).astype(BF16)
    v_buf[...] = v.astype(BF16)
    etot_buf[...] = jnp.exp(total)
    emid_buf[...] = jnp.exp(mid)
    return em, enm, erest


def _short_conv(u, edge, cw):
    return cw[0:1, :] * _shift_rows(u, 2, edge) + cw[1:2, :] * _shift_rows(u, 1, edge) + cw[2:3, :] * u


def _block_causal_mask(n):
    row = lax.broadcasted_iota(jnp.int32, (n, n), 0)
    col = lax.broadcasted_iota(jnp.int32, (n, n), 1)
    return jnp.logical_and(row // CHUNK == col // CHUNK, col <= row)


def _spread(v, chunk_of_row, nc):
    return jnp.concatenate([jnp.where(chunk_of_row == c, v, jnp.zeros_like(v)) for c in range(nc)], axis=1)


def _pick(r, chunk_of_row, nc):
    out = jnp.where(chunk_of_row == 0, r[:, 0:HGRN_DK], 0.0)
    for c in range(1, nc):
        out = out + jnp.where(chunk_of_row == c, r[:, c * HGRN_DK:(c + 1) * HGRN_DK], 0.0)
    return out


def _mix_fwd(x, g, w_in, lbp, gh, convw_t, w_out, exchange=None):
    t, d = x.shape
    zw = w_in.shape[0]
    w = HGRN_W
    tm = min(TOKEN_TILE, t)
    nc = tm // CHUNK
    n_chunks = t // CHUNK

    def body(x_ref, g_ref, win_ref, lbp_ref, gh_ref, cw_ref, wout_ref,
             xo_ref, z_ref, o_ref, st_ref, y_ref, state, ucarry, qm_buf, km_buf, kbar_buf, v_buf, etot_buf, emid_buf):
        _zero_at_start(state, ucarry)
        xv = x_ref[...]
        h, _, _ = _rms(xv, g_ref[...])
        z_ref[...] = _dot(h.astype(BF16), win_ref[...], NT)
        z = z_ref[...]
        _, _, f, _, q = _gates(z, lbp_ref[...])
        _decayed_operands(q, f, z[:, 2 * w:3 * w], qm_buf, km_buf, kbar_buf, v_buf, etot_buf, emid_buf)
        mask = _block_causal_mask(tm)
        chunk_of_row = lax.broadcasted_iota(jnp.int32, (tm, 1), 0) // CHUNK
        heads = range(HGRN_HEADS)
        hcols = [slice(hd * HGRN_DK, (hd + 1) * HGRN_DK) for hd in heads]
        qm = [qm_buf[:, hcols[hd]] for hd in heads]
        vb = [v_buf[:, hcols[hd]] for hd in heads]
        scores = [jnp.where(mask, _dot(qm[hd], km_buf[:, hcols[hd]], NT), 0.0).astype(BF16) for hd in heads]
        gains = [_dot(_spread(vb[hd], chunk_of_row, nc), kbar_buf[:, hcols[hd]], TN) for hd in heads]
        entering = []
        for hd in heads:
            states, st = [], state[hd]
            for c in range(nc):
                first_row = slice(c * CHUNK, c * CHUNK + 1)
                states.append(st * emid_buf[first_row, hcols[hd]])
                st_ref[c, hd] = st
                st = st * etot_buf[first_row, hcols[hd]] + gains[hd][c * HGRN_DK:(c + 1) * HGRN_DK, :]
            state[hd] = st
            entering.append(jnp.concatenate(states, axis=0).astype(BF16))
        from_states = [_dot(qm[hd], entering[hd], NT) for hd in heads]
        o_heads = [_dot(scores[hd], vb[hd]) + _pick(from_states[hd], chunk_of_row, nc) for hd in heads]
        o_ref[...] = jnp.concatenate(o_heads, axis=1)
        ghv = gh_ref[...]
        normed = jnp.concatenate([_rms(o_heads[hd], ghv[:, hcols[hd]])[0] for hd in heads], axis=1)
        zg = z[:, 3 * w:4 * w]
        u = z[:, 5 * w:6 * w] * z[:, 6 * w:7 * w]
        conv = _short_conv(u, ucarry[...], cw_ref[...])
        ucarry[...] = u[tm - 8:tm, :]
        y = jnp.concatenate([normed * (zg * _sigmoid(zg)), z[:, 4 * w:5 * w] * conv], axis=1).astype(BF16)
        y_ref[...] = y
        xo_ref[...] = xv + _dot(y, wout_ref[...])

    return _call(
        body,
        name="mix_fwd",
        grid=(t // tm,),
        in_specs=[
            _rows(tm, d), _full((1, d)), _full((zw, d)), _full((2, w)), _full((1, w)), _full((3, w)),
            _full((2 * w, d)),
        ],
        out_specs=[
            _rows(tm, d), _rows(tm, zw), _rows(tm, w),
            pl.BlockSpec((nc, HGRN_HEADS, HGRN_DK, HGRN_DK), lambda i: (i, 0, 0, 0)),
            _rows(tm, 2 * w),
        ],
        out_shape=[
            jax.ShapeDtypeStruct((t, d), F32),
            jax.ShapeDtypeStruct((t, zw), F32),
            jax.ShapeDtypeStruct((t, w), F32),
            jax.ShapeDtypeStruct((n_chunks, HGRN_HEADS, HGRN_DK, HGRN_DK), F32),
            jax.ShapeDtypeStruct((t, 2 * w), BF16),
        ],
        scratch_shapes=[
            pltpu.VMEM((HGRN_HEADS, HGRN_DK, HGRN_DK), F32), pltpu.VMEM((8, w), F32),
            pltpu.VMEM((tm, w), BF16), pltpu.VMEM((tm, w), BF16), pltpu.VMEM((tm, w), BF16),
            pltpu.VMEM((tm, w), BF16), pltpu.VMEM((tm, w), F32), pltpu.VMEM((tm, w), F32),
        ],
        args=(x, g, w_in, lbp, gh, convw_t, w_out),
        exchange=exchange,
    )


def _mix_bwd(x, g, dxo, z, o, states, w_in, lbp, gh, convw_t, w_out, exchange=None):
    t, d = x.shape
    zw = w_in.shape[0]
    w = HGRN_W
    tm = min(TOKEN_TILE, t)
    nc = tm // CHUNK
    n = t // tm

    def body(x_ref, g_ref, dxo_ref, z_ref, zprev_ref, o_ref, st_ref, win_ref, lbp_ref, gh_ref, cw_ref, wout_ref,
             dx_ref, dz_ref, h_ref, dg_ref, dlbp_ref, dgh_ref, dcw_ref,
             dstate, dcarry, do_buf, qm_buf, km_buf, kbar_buf, v_buf, etot_buf, emid_buf):
        _zero_at_start(dstate, dcarry, dg_ref, dlbp_ref, dgh_ref, dcw_ref)
        gv = g_ref[...]
        h, xh, r = _rms(x_ref[...], gv)
        h_ref[...] = h.astype(BF16)
        dxo = dxo_ref[...]
        dy = _dot(dxo.astype(BF16), wout_ref[...], NT)
        z = z_ref[...]
        lb, sig, f, sq, q = _gates(z, lbp_ref[...])
        em, enm, erest = _decayed_operands(
            q, f, z[:, 2 * w:3 * w], qm_buf, km_buf, kbar_buf, v_buf, etot_buf, emid_buf)

        ghv = gh_ref[...]
        zg = z[:, 3 * w:4 * w]
        sgz = _sigmoid(zg)
        dyh = dy[:, 0:w]
        don = dyh * (zg * sgz)
        heads = range(HGRN_HEADS)
        hcols = [slice(hd * HGRN_DK, (hd + 1) * HGRN_DK) for hd in heads]
        norms = [_rms(o_ref[:, hcols[hd]], ghv[:, hcols[hd]]) for hd in heads]
        on = jnp.concatenate([norms[hd][0] for hd in heads], axis=1)
        oh = jnp.concatenate([norms[hd][1] for hd in heads], axis=1)
        dz_ref[:, 3 * w:4 * w] = (dyh * on * (sgz * (1.0 + zg * (1.0 - sgz)))).astype(BF16)
        dgh_ref[...] += jnp.sum(don * oh, axis=0, keepdims=True)
        do_buf[...] = jnp.concatenate(
            [_rms_bwd(don[:, hcols[hd]], norms[hd][1], norms[hd][2], ghv[:, hcols[hd]]) for hd in heads],
            axis=1).astype(BF16)

        zb = z[:, 4 * w:5 * w]
        zc = z[:, 5 * w:6 * w]
        zu = z[:, 6 * w:7 * w]
        u = zc * zu
        cw = cw_ref[...]
        zp = zprev_ref[...]
        uprev = jnp.where(pl.program_id(0) == n - 1, 0.0, zp[:, 5 * w:6 * w] * zp[:, 6 * w:7 * w])
        dyc = dy[:, w:2 * w]
        dz_ref[:, 4 * w:5 * w] = (dyc * _short_conv(u, uprev, cw)).astype(BF16)
        dconv = dyc * zb
        edge = dcarry[...]
        dconv1 = _shift_rows(dconv, -1, edge)
        dconv2 = _shift_rows(dconv, -2, edge)
        dcarry[...] = dconv[0:8, :]
        du = cw[2:3, :] * dconv + cw[1:2, :] * dconv1 + cw[0:1, :] * dconv2
        dz_ref[:, 5 * w:6 * w] = (du * zu).astype(BF16)
        dz_ref[:, 6 * w:7 * w] = (du * zc).astype(BF16)
        dcw_ref[...] += jnp.concatenate([
            jnp.sum(u * dconv2, axis=0, keepdims=True),
            jnp.sum(u * dconv1, axis=0, keepdims=True),
            jnp.sum(u * dconv, axis=0, keepdims=True)], axis=0)

        mask = _block_causal_mask(tm)
        chunk_of_row = lax.broadcasted_iota(jnp.int32, (tm, 1), 0) // CHUNK
        heads = range(HGRN_HEADS)
        hcols = [slice(hd * HGRN_DK, (hd + 1) * HGRN_DK) for hd in heads]
        qmb = [qm_buf[:, hcols[hd]] for hd in heads]
        kmb = [km_buf[:, hcols[hd]] for hd in heads]
        vb = [v_buf[:, hcols[hd]] for hd in heads]
        dob = [do_buf[:, hcols[hd]] for hd in heads]
        scores = [jnp.where(mask, _dot(qmb[hd], kmb[hd], NT), 0.0).astype(BF16) for hd in heads]
        dscores = [jnp.where(mask, _dot(dob[hd], vb[hd], NT), 0.0).astype(BF16) for hd in heads]
        gains = [_dot(_spread(dob[hd], chunk_of_row, nc), qmb[hd], TN) for hd in heads]
        dst_rows, dst_lanes, st_lanes, carries = [], [], [], []
        for hd in heads:
            entering = [st_ref[c, hd] for c in range(nc)]
            emid = [emid_buf[c * CHUNK:c * CHUNK + 1, hcols[hd]] for c in range(nc)]
            leaving, carried_back = [None] * nc, [None] * nc
            dst = dstate[hd]
            for c in reversed(range(nc)):
                elast = etot_buf[c * CHUNK:c * CHUNK + 1, hcols[hd]]
                leaving[c] = dst
                carried_back[c] = jnp.sum(dst * entering[c], axis=0, keepdims=True) * elast
                dst = dst * elast + gains[hd][c * HGRN_DK:(c + 1) * HGRN_DK, :] * emid[c]
            dstate[hd] = dst
            dst_rows.append(jnp.concatenate(leaving, axis=0).astype(BF16))
            dst_lanes.append(jnp.concatenate(leaving, axis=1).astype(BF16))
            st_lanes.append(jnp.concatenate([entering[c] * emid[c] for c in range(nc)], axis=1).astype(BF16))
            carries.append(carried_back)
        dv = [_dot(scores[hd], dob[hd], TN) + _pick(_dot(kbar_buf[:, hcols[hd]], dst_rows[hd], NT), chunk_of_row, nc)
              for hd in heads]
        dz_ref[:, 2 * w:3 * w] = jnp.concatenate(dv, axis=1).astype(BF16)
        dqm = jnp.concatenate([_dot(dscores[hd], kmb[hd]) + _pick(_dot(dob[hd], st_lanes[hd]), chunk_of_row, nc)
                               for hd in heads], axis=1)
        dkm = jnp.concatenate([_dot(dscores[hd], qmb[hd], TN) for hd in heads], axis=1)
        dkbar = jnp.concatenate([_pick(_dot(vb[hd], dst_lanes[hd]), chunk_of_row, nc) for hd in heads], axis=1)

        kbar_dkbar = kbar_buf[...].astype(F32) * dkbar
        db = qm_buf[...].astype(F32) * dqm - km_buf[...].astype(F32) * dkm - kbar_dkbar
        through_last = jnp.concatenate([
            jnp.broadcast_to(
                jnp.sum(kbar_dkbar[c * CHUNK:(c + 1) * CHUNK], axis=0, keepdims=True)
                + jnp.concatenate([carries[hd][c] for hd in heads], axis=1),
                (CHUNK, w))
            for c in range(nc)], axis=0)
        dlogf = _chunk_cumsum(db, reverse=True) + through_last
        df = dlogf / f - (dkm * enm + dkbar * erest)
        zq = z[:, 0:w]
        dz_ref[:, 0:w] = (dqm * em * HGRN_DK ** -0.5 * (sq * (1.0 + zq * (1.0 - sq)))).astype(BF16)
        dz_ref[:, w:2 * w] = (df * (1.0 - lb) * sig * (1.0 - sig)).astype(BF16)
        dlb = jnp.sum(df * (1.0 - sig), axis=0, keepdims=True) * lb * (1.0 - lb)
        dlbp_ref[...] += jnp.concatenate([dlb, -dlb], axis=0)

        dh = _dot(dz_ref[...], win_ref[...])
        dx_ref[...] = _rms_bwd(dh, xh, r, gv) + dxo
        dg_ref[...] += jnp.sum(dh * xh, axis=0, keepdims=True)

    return _call(
        body,
        name="mix_bwd",
        grid=(n,),
        in_specs=[
            _rows_rev(tm, d, n), _full((1, d)), _rows_rev(tm, d, n), _rows_rev(tm, zw, n),
            pl.BlockSpec((8, zw), lambda i: (jnp.maximum((n - 1 - i) * (tm // 8) - 1, 0), 0)),
            _rows_rev(tm, w, n),
            pl.BlockSpec((nc, HGRN_HEADS, HGRN_DK, HGRN_DK), lambda i: (n - 1 - i, 0, 0, 0)),
            _full((zw, d)), _full((2, w)), _full((1, w)), _full((3, w)), _full((2 * w, d)),
        ],
        out_specs=[
            _rows_rev(tm, d, n), _rows_rev(tm, zw, n), _rows_rev(tm, d, n),
            _full((1, d)), _full((2, w)), _full((1, w)), _full((3, w)),
        ],
        out_shape=[
            jax.ShapeDtypeStruct((t, d), F32),
            jax.ShapeDtypeStruct((t, zw), BF16),
            jax.ShapeDtypeStruct((t, d), BF16),
            jax.ShapeDtypeStruct((1, d), F32),
            jax.ShapeDtypeStruct((2, w), F32),
            jax.ShapeDtypeStruct((1, w), F32),
            jax.ShapeDtypeStruct((3, w), F32),
        ],
        scratch_shapes=[
            pltpu.VMEM((HGRN_HEADS, HGRN_DK, HGRN_DK), F32), pltpu.VMEM((8, w), F32),
            pltpu.VMEM((tm, w), BF16),
            pltpu.VMEM((tm, w), BF16), pltpu.VMEM((tm, w), BF16), pltpu.VMEM((tm, w), BF16),
            pltpu.VMEM((tm, w), BF16), pltpu.VMEM((tm, w), F32), pltpu.VMEM((tm, w), F32),
        ],
        args=(x, g, dxo, z, z, o, states, w_in, lbp, gh, convw_t, w_out),
        exchange=exchange,
    )


def _memkv_fwd(mem, g, wkv):
    m, d = mem.shape
    nb, _, cb = wkv.shape

    def body(mem_ref, g_ref, wkv_ref, kv_ref):
        mn, _, _ = _rms(mem_ref[...], g_ref[...])
        mnb = mn.astype(BF16)
        for j in range(nb):
            kv_ref[:, j * cb:(j + 1) * cb] = _dot(mnb, wkv_ref[j]).astype(BF16)

    return pl.pallas_call(
        body,
        name="memkv_fwd",
        out_shape=jax.ShapeDtypeStruct((m, nb * cb), BF16),
        compiler_params=_params(),
    )(mem, g, wkv)


def _memkv_bwd(mem, g, dkv, wkv):
    m, d = mem.shape
    nb, _, cb = wkv.shape
    chips = nb // 2

    def body(mem_ref, g_ref, dkv_ref, wkv_ref, dw_ref, dg_ref, dw_all, send_buf, recv_buf, send_sem, recv_sem):
        x, y, c, _ = _mesh_place()
        sibling, _ = _peer(x, y, c, 1)
        mn, xh, _ = _rms(mem_ref[...], g_ref[...])
        mnb = mn.astype(BF16)
        dmn = jnp.zeros((m, d), F32)
        for j in range(nb):
            dkvb = dkv_ref[:, j * cb:(j + 1) * cb].astype(BF16)
            dw_all[j] = _dot(mnb, dkvb, TN)
            dmn = dmn + _dot(dkvb, wkv_ref[j], NT)
        dg_ref[...] = jnp.sum(dmn * xh, axis=0, keepdims=True)
        for q in range(chips):
            send_buf[q] = dw_all[2 * q + 1 - c].astype(BF16)
        to_sibling = _remote(send_buf, recv_buf, send_sem, recv_sem, sibling)
        to_sibling.start()
        to_sibling.wait_send()
        to_sibling.wait_recv()
        for q in range(chips):
            dw_ref[q] = (dw_all[2 * q + c] + recv_buf[q].astype(F32)).astype(BF16)

    return pl.pallas_call(
        body,
        name="memkv_bwd",
        out_shape=[jax.ShapeDtypeStruct((chips, d, cb), BF16), jax.ShapeDtypeStruct((1, d), F32)],
        scratch_shapes=[
            pltpu.VMEM((nb, d, cb), F32), pltpu.VMEM((chips, d, cb), BF16), pltpu.VMEM((chips, d, cb), BF16),
            pltpu.SemaphoreType.DMA, pltpu.SemaphoreType.DMA,
        ],
        compiler_params=_params(),
    )(mem, g, dkv, wkv)


def _softmax_rows(qm_h, k_h):
    sc = _dot(qm_h, k_h, NT) * MEM_HD ** -0.5
    e = jnp.exp(sc - jnp.max(sc, axis=-1, keepdims=True))
    return e / jnp.sum(e, axis=-1, keepdims=True)


def _xattn_fwd(x, g, wq, kv, wo, exchange=None):
    t, d = x.shape
    m = kv.shape[0]
    tm = min(XATTN_TILE, t)

    def body(x_ref, g_ref, wq_ref, kv_ref, wo_ref, xo_ref, hq_ref, qm_ref, att_ref):
        xv = x_ref[...]
        h, _, _ = _rms(xv, g_ref[...])
        hb = h.astype(BF16)
        hq_ref[...] = hb
        qm = _dot(hb, wq_ref[...]).astype(BF16)
        qm_ref[...] = qm
        heads = range(MEM_HEADS)
        kcols = [slice(hd * MEM_HD, (hd + 1) * MEM_HD) for hd in heads]
        p = [_softmax_rows(qm[:, kcols[hd]], kv_ref[:, kcols[hd]]) for hd in heads]
        att = jnp.concatenate(
            [_dot(p[hd].astype(BF16), kv_ref[:, d + hd * MEM_HD:d + (hd + 1) * MEM_HD]) for hd in heads],
            axis=1).astype(BF16)
        att_ref[...] = att
        xo_ref[...] = xv + _dot(att, wo_ref[...])

    return _call(
        body,
        name="xattn_fwd",
        grid=(t // tm,),
        in_specs=[_rows(tm, d), _full((1, d)), _full((d, d)), _full((m, 2 * d)), _full((d, d))],
        out_specs=[_rows(tm, d), _rows(tm, d), _rows(tm, d), _rows(tm, d)],
        out_shape=[
            jax.ShapeDtypeStruct((t, d), F32),
            jax.ShapeDtypeStruct((t, d), BF16),
            jax.ShapeDtypeStruct((t, d), BF16),
            jax.ShapeDtypeStruct((t, d), BF16),
        ],
        args=(x, g, wq, kv, wo),
        exchange=exchange,
    )


def _xattn_bwd(x, g, dxo, qm, kv, wq, wo, exchange=None):
    t, d = x.shape
    m = kv.shape[0]
    tm = min(XATTN_TILE, t)

    def body(x_ref, g_ref, dxo_ref, qm_ref, kv_ref, wq_ref, wo_ref, dx_ref, dqm_ref, dkv_ref, dg_ref):
        _zero_at_start(dkv_ref, dg_ref)
        gv = g_ref[...]
        _, xh, r = _rms(x_ref[...], gv)
        dxo = dxo_ref[...]
        datt = _dot(dxo.astype(BF16), wo_ref[...], NT).astype(BF16)
        heads = range(MEM_HEADS)
        kcols = [slice(hd * MEM_HD, (hd + 1) * MEM_HD) for hd in heads]
        vcols = [slice(d + hd * MEM_HD, d + (hd + 1) * MEM_HD) for hd in heads]
        qm_h = [qm_ref[:, kcols[hd]] for hd in heads]
        p = [_softmax_rows(qm_h[hd], kv_ref[:, kcols[hd]]) for hd in heads]
        dp = [_dot(datt[:, kcols[hd]], kv_ref[:, vcols[hd]], NT) for hd in heads]
        dsc = [(p[hd] * (dp[hd] - jnp.sum(p[hd] * dp[hd], axis=-1, keepdims=True)) * MEM_HD ** -0.5).astype(BF16)
               for hd in heads]
        dqm = jnp.concatenate([_dot(dsc[hd], kv_ref[:, kcols[hd]]) for hd in heads], axis=1).astype(BF16)
        dqm_ref[...] = dqm
        dkv_ref[...] += jnp.concatenate(
            [_dot(dsc[hd], qm_h[hd], TN) for hd in heads]
            + [_dot(p[hd].astype(BF16), datt[:, kcols[hd]], TN) for hd in heads], axis=1)
        dh = _dot(dqm, wq_ref[...], NT)
        dx_ref[...] = _rms_bwd(dh, xh, r, gv) + dxo
        dg_ref[...] += jnp.sum(dh * xh, axis=0, keepdims=True)

    return _call(
        body,
        name="xattn_bwd",
        grid=(t // tm,),
        in_specs=[
            _rows(tm, d), _full((1, d)), _rows(tm, d), _rows(tm, d), _full((m, 2 * d)), _full((d, d)), _full((d, d)),
        ],
        out_specs=[_rows(tm, d), _rows(tm, d), _full((m, 2 * d)), _full((1, d))],
        out_shape=[
            jax.ShapeDtypeStruct((t, d), F32),
            jax.ShapeDtypeStruct((t, d), BF16),
            jax.ShapeDtypeStruct((m, 2 * d), F32),
            jax.ShapeDtypeStruct((1, d), F32),
        ],
        args=(x, g, dxo, qm, kv, wq, wo),
        exchange=exchange,
    )


def _mesh_place():
    x, y, c = lax.axis_index("x"), lax.axis_index("y"), lax.axis_index("c")
    return x, y, c, 4 * x + 2 * y + c


def _peer(x, y, c, k):
    px = 1 - x if k & 4 else x
    py = 1 - y if k & 2 else y
    pc = 1 - c if k & 1 else c
    return (px, py, pc), 4 * px + 2 * py + pc


ICI_HOPS = (2, 4, 6)
N_HOPS = len(ICI_HOPS)


def _remote(src, dst, send_sem, recv_sem, peer):
    return pltpu.make_async_remote_copy(
        src_ref=src, dst_ref=dst, send_sem=send_sem, recv_sem=recv_sem, device_id=peer, device_id_type=MESH_IDS)


def _gather_exchange(shards, middle_eighths=MIDDLE_EIGHTHS):
    n = len(shards)

    def place():
        x, y, c, me = _mesh_place()
        sibling, _ = _peer(x, y, c, 1)
        to_x, from_x = _peer(x, y, c, 4)
        to_y, from_y = _peer(x, y, c, 2)
        _, from_diagonal = _peer(x, y, c, 6)
        onward = (c * to_y[0] + (1 - c) * to_x[0], c * to_y[1] + (1 - c) * to_x[1], c)
        passed_on = c * from_x + (1 - c) * from_y
        return me, sibling, (to_x, to_y, onward), (from_x, from_y, from_diagonal), passed_on

    def start(src, dst, sems):
        ici_send, ici_recv, pair_send, pair_recv, local = sems
        me, sibling, targets, _, _ = place()
        for a in range(n):
            pltpu.make_async_copy(src[a], dst[a].at[me], local.at[a]).start()
            for j in range(2):
                _remote(src[a], dst[a].at[me], ici_send.at[a, j], ici_recv.at[a, j], targets[j]).start()
            _remote(src[a], dst[a].at[me], pair_send.at[a, 0], pair_recv.at[a, 0], sibling).start()

    def to_sibling(dst, sems, a, j, origin, sibling):
        _, _, pair_send, pair_recv, _ = sems
        slot = dst[a].at[origin]
        return _remote(slot, slot, pair_send.at[a, 1 + j], pair_recv.at[a, 1 + j], sibling)

    def middle(src, dst, sems):
        ici_send, ici_recv, _, _, _ = sems
        _, sibling, targets, origins, passed_on = place()
        for a in range(n):
            for j in range(2):
                _remote(src[a], dst[a].at[origins[j]], ici_send.at[a, j], ici_recv.at[a, j], targets[j]).wait_recv()
            slot = dst[a].at[passed_on]
            _remote(slot, slot, ici_send.at[a, 2], ici_recv.at[a, 2], targets[2]).start()
            for j in range(2):
                to_sibling(dst, sems, a, j, origins[j], sibling).start()

    def finish(src, dst, sems):
        ici_send, ici_recv, pair_send, pair_recv, local = sems
        me, sibling, targets, origins, _ = place()
        for a in range(n):
            _remote(src[a], dst[a].at[origins[2]], ici_send.at[a, 2], ici_recv.at[a, 2], targets[2]).wait_recv()
            to_sibling(dst, sems, a, 2, origins[2], sibling).start()
        for a in range(n):
            pltpu.make_async_copy(src[a], dst[a].at[me], local.at[a]).wait()
            for j in range(N_HOPS):
                _remote(src[a], dst[a].at[me], ici_send.at[a, j], ici_recv.at[a, j], targets[j]).wait_send()
            for j, origin in enumerate((me,) + origins):
                from_sibling = origin + 1 - 2 * (origin % 2)
                passed = _remote(src[a], dst[a].at[from_sibling], pair_send.at[a, j], pair_recv.at[a, j], sibling)
                passed.wait_send()
                passed.wait_recv()

    return _Exchange(
        shards,
        [jax.ShapeDtypeStruct((N_DEV,) + s.shape, s.dtype) for s in shards],
        [
            pltpu.SemaphoreType.DMA((n, N_HOPS)), pltpu.SemaphoreType.DMA((n, N_HOPS)),
            pltpu.SemaphoreType.DMA((n, N_HOPS + 1)), pltpu.SemaphoreType.DMA((n, N_HOPS + 1)),
            pltpu.SemaphoreType.DMA((n,)),
        ],
        start, finish, middle, middle_eighths)


def _scatter_copies(src, dst, sems, n, arrivals=False):
    send, recv, local = sems
    x, y, c, _ = _mesh_place()
    chip = 2 * x + y
    if arrivals is None:
        return [pltpu.make_async_copy(src[a].at[chip], dst[a].at[chip], local.at[a]) for a in range(n)]
    copies = []
    for a in range(n):
        for j, k in enumerate(ICI_HOPS):
            peer, _ = _peer(x, y, c, k)
            peer_chip = 2 * peer[0] + peer[1]
            slot = dst[a].at[peer_chip if arrivals else chip]
            copies.append(_remote(src[a].at[peer_chip], slot, send.at[a, j], recv.at[a, j], peer))
    return copies


def _scatter_start(src, dst, sems, n):
    for cp in _scatter_copies(src, dst, sems, n, arrivals=None) + _scatter_copies(src, dst, sems, n):
        cp.start()


def _scatter_finish(src, dst, sems, n):
    for cp in _scatter_copies(src, dst, sems, n, arrivals=None):
        cp.wait()
    for cp in _scatter_copies(src, dst, sems, n):
        cp.wait_send()
    for cp in _scatter_copies(src, dst, sems, n, arrivals=True):
        cp.wait_recv()


def _scatter_scratch(n):
    return [pltpu.SemaphoreType.DMA((n, N_HOPS)), pltpu.SemaphoreType.DMA((n, N_HOPS)), pltpu.SemaphoreType.DMA((n,))]


def _scatter_exchange(partials):
    n = len(partials)
    return _Exchange(
        partials, [jax.ShapeDtypeStruct(p.shape, p.dtype) for p in partials], _scatter_scratch(n),
        lambda src, dst, sems: _scatter_start(src, dst, sems, n),
        lambda src, dst, sems: _scatter_finish(src, dst, sems, n))


SMALL_LAYOUT = {
    "ffn1_norm": (0, 1, 1024), "mix_norm": (1, 1, 1024), "xattn_norm": (2, 1, 1024), "mem_norm": (3, 1, 1024),
    "ffn2_norm": (4, 1, 1024), "final_norm": (5, 1, 1024), "lb_param": (6, 2, 512), "hgrn_out_norm": (8, 1, 512),
    "conv_w": (9, 3, 512), "loss": (12, 1, 128),
}


def _final_exchange(partials, small):
    n = len(partials)
    names = list(small)
    width = 1024

    def body(*refs):
        src = refs[:n]
        pieces = refs[n:n + len(names)]
        dst = refs[n + len(names):2 * n + len(names)]
        total_ref = refs[2 * n + len(names)]
        pack, gathered, small_send, small_recv = refs[2 * n + len(names) + 1:2 * n + len(names) + 5]
        sems = refs[2 * n + len(names) + 5:]
        x, y, c, me = _mesh_place()
        pack[...] = jnp.zeros_like(pack)
        for name, piece in zip(names, pieces):
            row, nrows, ncols = SMALL_LAYOUT[name]
            pack[row:row + nrows, 0:ncols] = piece[...]
        for k in range(1, N_DEV):
            peer, _ = _peer(x, y, c, k)
            _remote(pack, gathered.at[me], small_send.at[k - 1], small_recv.at[k - 1], peer).start()
        _scatter_start(src, dst, sems, n)
        gathered[me] = pack[...]
        for k in range(1, N_DEV):
            peer, peer_index = _peer(x, y, c, k)
            landed = _remote(pack, gathered.at[peer_index], small_send.at[k - 1], small_recv.at[k - 1], peer)
            landed.wait_send()
            landed.wait_recv()
        total = gathered[0]
        for j in range(1, N_DEV):
            total = total + gathered[j]
        total_ref[...] = total
        _scatter_finish(src, dst, sems, n)

    hbm = pl.BlockSpec(memory_space=pltpu.HBM)
    vmem = pl.BlockSpec(memory_space=pltpu.VMEM)
    out = pl.pallas_call(
        body,
        name="final_exchange",
        in_specs=[hbm] * n + [vmem] * len(names),
        out_specs=[hbm] * n + [vmem],
        out_shape=[jax.ShapeDtypeStruct(p.shape, p.dtype) for p in partials]
        + [jax.ShapeDtypeStruct((SMALL_ROWS, width), F32)],
        scratch_shapes=[
            pltpu.VMEM((SMALL_ROWS, width), F32), pltpu.VMEM((N_DEV, SMALL_ROWS, width), F32),
            pltpu.SemaphoreType.DMA((N_DEV - 1,)), pltpu.SemaphoreType.DMA((N_DEV - 1,)),
        ] + _scatter_scratch(n),
        compiler_params=pltpu.CompilerParams(has_side_effects=True),
    )(*partials, *[small[k] for k in names])
    return out[:n], out[n]


def _adamw_math(w, g, m, v):
    m = ADAM_B1 * m + (1.0 - ADAM_B1) * g
    v = ADAM_B2 * v + (1.0 - ADAM_B2) * (g * g)
    m_hat = m / (1.0 - ADAM_B1 ** ADAM_STEP)
    v_hat = v / (1.0 - ADAM_B2 ** ADAM_STEP)
    delta = -ADAM_LR * (m_hat / (jnp.sqrt(v_hat) + ADAM_EPS) + ADAM_WD * w)
    return delta, m, v


def _adamw_shard(parts, w, m, v):
    r, c = w.shape
    n_parts = parts.shape[0]
    tr = max(rows for rows in range(16, r + 1, 16) if r % rows == 0 and rows * c <= ADAMW_TILE_ELEMENTS)

    def body(p_ref, w_ref, m_ref, v_ref, g_ref, d_ref, mo_ref, vo_ref):
        g = p_ref[0].astype(F32)
        for j in range(1, n_parts):
            g = g + p_ref[j].astype(F32)
        delta, mn, vn = _adamw_math(w_ref[...], g, m_ref[...], v_ref[...])
        g_ref[...] = g
        d_ref[...] = delta
        mo_ref[...] = mn
        vo_ref[...] = vn

    tile = pl.BlockSpec((tr, c), lambda i: (i, 0))
    return pl.pallas_call(
        body,
        name="adamw_shard",
        grid=(r // tr,),
        in_specs=[pl.BlockSpec((n_parts, tr, c), lambda i: (0, i, 0)), tile, tile, tile],
        out_specs=[tile] * 4,
        out_shape=[jax.ShapeDtypeStruct((r, c), F32)] * 4,
        compiler_params=_params(("parallel",)),
    )(parts, w, m, v)


def _adamw_small(gs, ws, ms, vs):
    n = len(gs)

    def body(*refs):
        g_refs, w_refs, m_refs, v_refs = refs[:n], refs[n:2 * n], refs[2 * n:3 * n], refs[3 * n:4 * n]
        g_out, d_out, m_out, v_out = refs[4 * n:5 * n], refs[5 * n:6 * n], refs[6 * n:7 * n], refs[7 * n:8 * n]
        for i in range(n):
            if gs[i].ndim == ws[i].ndim:
                g = g_refs[i][...]
            else:
                g = g_refs[i][0].astype(F32)
                for j in range(1, gs[i].shape[0]):
                    g = g + g_refs[i][j].astype(F32)
            delta, mn, vn = _adamw_math(w_refs[i][...], g, m_refs[i][...], v_refs[i][...])
            g_out[i][...] = g
            d_out[i][...] = delta
            m_out[i][...] = mn
            v_out[i][...] = vn

    shapes = [jax.ShapeDtypeStruct(w.shape, F32) for w in ws]
    out = pl.pallas_call(
        body,
        name="adamw_small",
        out_shape=shapes * 4,
        compiler_params=_params(),
    )(*gs, *ws, *ms, *vs)
    return out[:n], out[n:2 * n], out[2 * n:3 * n], out[3 * n:]


TRANSPOSED = ("ffn1_gate", "ffn1_up", "w_in", "ffn2_gate", "ffn2_up", "conv_w")
GROUP_FFN1 = ("ffn1_gate", "ffn1_up", "ffn1_down")
GROUP_MIX = ("w_in", "w_out")
GROUP_XATTN = ("w_q_mem", "w_kv_mem", "w_o_mem")
GROUP_FFN2 = ("ffn2_gate", "ffn2_up", "ffn2_down")
LARGE = GROUP_FFN1 + GROUP_MIX + GROUP_XATTN + GROUP_FFN2
SHORT_SHARDS = ("w_out", "w_q_mem", "w_kv_mem", "w_o_mem")
SMALL = ("ffn1_norm", "mix_norm", "lb_param", "hgrn_out_norm", "conv_w", "xattn_norm", "mem_norm", "ffn2_norm",
         "final_norm")
WEIGHTS = ("ffn1_norm", "ffn1_gate", "ffn1_up", "ffn1_down", "mix_norm", "w_in", "lb_param", "hgrn_out_norm",
           "conv_w", "w_out", "xattn_norm", "mem_norm", "w_q_mem", "w_kv_mem", "w_o_mem", "ffn2_norm", "ffn2_gate",
           "ffn2_up", "ffn2_down", "final_norm")


def kernel(x, mem, ffn1_norm, ffn1_gate, ffn1_up, ffn1_down, mix_norm, w_in, lb_param, hgrn_out_norm, conv_w, w_out, xattn_norm, mem_norm, w_q_mem, w_kv_mem, w_o_mem, ffn2_norm, ffn2_gate, ffn2_up, ffn2_down, final_norm, loss_target, m_ffn1_norm, m_ffn1_gate, m_ffn1_up, m_ffn1_down, m_mix_norm, m_w_in, m_lb_param, m_hgrn_out_norm, m_conv_w, m_w_out, m_xattn_norm, m_mem_norm, m_w_q_mem, m_w_kv_mem, m_w_o_mem, m_ffn2_norm, m_ffn2_gate, m_ffn2_up, m_ffn2_down, m_final_norm, v_ffn1_norm, v_ffn1_gate, v_ffn1_up, v_ffn1_down, v_mix_norm, v_w_in, v_lb_param, v_hgrn_out_norm, v_conv_w, v_w_out, v_xattn_norm, v_mem_norm, v_w_q_mem, v_w_kv_mem, v_w_o_mem, v_ffn2_norm, v_ffn2_gate, v_ffn2_up, v_ffn2_down, v_final_norm):
    given = dict(locals())
    me = 4 * lax.axis_index("x") + 2 * lax.axis_index("y") + lax.axis_index("c")
    x0, memv, target = x[0], mem[0], loss_target[0]

    def shard(prefix, name):
        v = given[prefix + name]
        if v.ndim == 1:
            return v.reshape(1, -1)
        if v.ndim == 2:
            return v
        return v[0].T if name in TRANSPOSED else v[0]

    w = {name: shard("", name) for name in WEIGHTS}
    m = {name: shard("m_", name) for name in WEIGHTS}
    v = {name: shard("v_", name) for name in WEIGHTS}

    conv_taps, conv_rows = w["conv_w"].shape
    conv_tile = jnp.pad(w["conv_w"], ((0, 8 - conv_taps), (0, 128 - conv_rows)))
    wire = {name: w[name].astype(BF16) for name in LARGE}
    full = {}

    def landed(names, gathered):
        for name, blocks in zip(names, gathered):
            _, r, c = blocks.shape
            full[name] = blocks if name == "w_kv_mem" else blocks.reshape(N_DEV * r, c)

    first = ("ffn1_gate", "ffn1_up")
    landed(first, _run_exchange(_gather_exchange([wire[k] for k in first]), "gather_first"))

    riders = (("ffn1_down", "w_in"), ("w_out", "w_kv_mem"), ("w_q_mem", "w_o_mem", "ffn2_gate", "ffn2_up"),
              ("ffn2_down",))
    (a1, b1, s1), gathered = _ffn_up(
        x0, w["ffn1_norm"], full["ffn1_gate"], full["ffn1_up"],
        exchange=_gather_exchange([wire[k] for k in riders[0]]))
    landed(riders[0], gathered)
    (x1,), gathered = _ffn_down(
        x0, s1, full["ffn1_down"], exchange=_gather_exchange([wire[k] for k in riders[1]] + [conv_tile]))
    landed(riders[1], gathered)
    convw_t = gathered[-1][:, :conv_taps, :conv_rows].transpose(1, 0, 2).reshape(conv_taps, N_DEV * conv_rows)
    (x2, z, o_raw, states, ycat), gathered = _mix_fwd(
        x1, w["mix_norm"], full["w_in"], w["lb_param"], w["hgrn_out_norm"], convw_t, full["w_out"],
        exchange=_gather_exchange([wire[k] for k in riders[2]]))
    landed(riders[2], gathered)
    kv = _memkv_fwd(memv, w["mem_norm"], full["w_kv_mem"])
    (x3, hq, qm, att), gathered = _xattn_fwd(
        x2, w["xattn_norm"], full["w_q_mem"], kv, full["w_o_mem"],
        exchange=_gather_exchange([wire[k] for k in riders[3]], middle_eighths=EARLY_MIDDLE_EIGHTHS))
    landed(riders[3], gathered)
    (dx4, a2, b2, s2, loss_part, d_final), _ = _ffn_fwd(
        x3, w["ffn2_norm"], full["ffn2_gate"], full["ffn2_up"], full["ffn2_down"], head=(w["final_norm"], target))

    parts = {}
    waiting = []

    def carried():
        names = [name for name, _ in waiting]
        exchange = _scatter_exchange([p for _, p in waiting]) if waiting else None
        del waiting[:]
        return names, exchange

    def weight_grads(products):
        names, exchange = carried()
        partials, arrived = _weight_grad(list(products.values()), exchange=exchange)
        parts.update(zip(names, arrived))
        waiting.extend(zip(products, partials))

    def weight_grad(name, a, b, scale=1.0):
        weight_grads({name: (a, b, scale)})

    (dx3, da2, db2, h4, d_ffn2_norm), _ = _ffn_bwd(
        x3, w["ffn2_norm"], dx4, a2, b2, full["ffn2_gate"], full["ffn2_up"], full["ffn2_down"])
    weight_grad("ffn2_down", s2, dx4, 0.5)
    weight_grad("ffn2_gate", da2, h4)
    weight_grad("ffn2_up", db2, h4)
    names, exchange = carried()
    (dx2, dqm, dkv, d_xattn_norm), arrived = _xattn_bwd(
        x2, w["xattn_norm"], dx3, qm, kv, full["w_q_mem"], full["w_o_mem"], exchange=exchange)
    parts.update(zip(names, arrived))
    d_wkv, d_mem_norm = _memkv_bwd(memv, w["mem_norm"], dkv, full["w_kv_mem"])
    waiting.append(("w_kv_mem", d_wkv))
    names, exchange = carried()
    (dx1, dz, h2, d_mix_norm, d_lbp, d_gh, d_convw_t), arrived = _mix_bwd(
        x1, w["mix_norm"], dx2, z, o_raw, states, full["w_in"], w["lb_param"], w["hgrn_out_norm"], convw_t,
        full["w_out"], exchange=exchange)
    parts.update(zip(names, arrived))
    weight_grad("w_in", dz, h2)
    weight_grad("ffn1_down", s1, dx1, 0.5)
    (dx0, da1, db1, h1, d_ffn1_norm), _ = _ffn_bwd(
        x0, w["ffn1_norm"], dx1, a1, b1, full["ffn1_gate"], full["ffn1_up"], full["ffn1_down"])
    weight_grad("ffn1_gate", da1, h1)
    weight_grad("ffn1_up", db1, h1)
    weight_grads({"w_o_mem": (att, dx3, 1.0), "w_q_mem": (hq, dqm, 1.0), "w_out": (ycat, dx2, 1.0)})

    small_parts = {
        "ffn1_norm": d_ffn1_norm, "mix_norm": d_mix_norm, "xattn_norm": d_xattn_norm, "mem_norm": d_mem_norm,
        "ffn2_norm": d_ffn2_norm, "final_norm": d_final, "lb_param": d_lbp, "hgrn_out_norm": d_gh,
        "conv_w": d_convw_t, "loss": loss_part,
    }
    names = [name for name, _ in waiting]
    arrived, total = _final_exchange([p for _, p in waiting], small_parts)
    parts.update(zip(names, arrived))

    g_out, d_out, m_out, v_out = {}, {}, {}, {}
    for name in LARGE:
        if name not in SHORT_SHARDS:
            g_out[name], d_out[name], m_out[name], v_out[name] = _adamw_shard(parts[name], w[name], m[name], v[name])
    g_small = {name: parts[name] for name in SHORT_SHARDS}
    for name in SMALL:
        row, nrows, ncols = SMALL_LAYOUT[name]
        g_small[name] = total[row:row + nrows, 0:ncols]
    g_small["conv_w"] = lax.dynamic_slice_in_dim(g_small["conv_w"], me * conv_rows, conv_rows, axis=1)
    together = SMALL + SHORT_SHARDS
    gs, ds, ms, vs = _adamw_small(
        [g_small[k] for k in together], [w[k] for k in together], [m[k] for k in together],
        [v[k] for k in together])
    for i, name in enumerate(together):
        g_out[name], d_out[name], m_out[name], v_out[name] = gs[i], ds[i], ms[i], vs[i]

    def shaped(value, name):
        return (value.T if name in TRANSPOSED else value).reshape(given[name].shape)

    loss = total[SMALL_LAYOUT["loss"][0], 0]
    outs = [loss, dx0.reshape(x.shape)]
    for group in (g_out, d_out, m_out, v_out):
        outs += [shaped(group[name], name) for name in WEIGHTS]
    return tuple(outs)
```

```python
import jax
import jax.numpy as jnp
from jax import lax
from jax.experimental import pallas as pl
from jax.experimental.pallas import tpu as pltpu

F32 = jnp.float32
BF16 = jnp.bfloat16
MESH_IDS = pl.DeviceIdType.MESH

N_DEV = 8
EPS = 1e-6
HGRN_HEADS = 4
HGRN_DK = 128
HGRN_W = 512
CHUNK = 64
MEM_HEADS = 4
MEM_HD = 256
ADAM_LR = 0.001
ADAM_B1 = 0.9
ADAM_B2 = 0.999
ADAM_EPS = 1e-08
ADAM_WD = 0.01
ADAM_STEP = 10

TOKEN_TILE = 256
XATTN_TILE = 512
WIDE_TILE = 512
REDUCE_TILE = 1024
ADAMW_TILE_ELEMENTS = 256 * 1024
MIDDLE_EIGHTHS = 5
EARLY_MIDDLE_EIGHTHS = 4
MXU_ROWS = 256
VMEM_LIMIT = 60 * 1024 * 1024
SMALL_ROWS = 16
NT = (((1,), (1,)), ((), ()))
TN = (((0,), (0,)), ((), ()))


def _params(sem=None):
    return pltpu.CompilerParams(dimension_semantics=sem, vmem_limit_bytes=VMEM_LIMIT)


def _dot(a, b, dims=None):
    if dims is None:
        return jnp.dot(a, b, preferred_element_type=F32)
    return lax.dot_general(a, b, dims, preferred_element_type=F32)


def _sigmoid(v):
    return 1.0 / (1.0 + jnp.exp(-v))


def _rms(x, g):
    r = lax.rsqrt(jnp.mean(x * x, axis=-1, keepdims=True) + EPS)
    xh = x * r
    return xh * g, xh, r


def _rms_bwd(dh, xh, r, g):
    dxh = dh * g
    return r * (dxh - xh * jnp.mean(dxh * xh, axis=-1, keepdims=True))


def _full(shape):
    return pl.BlockSpec(shape, lambda *_: (0,) * len(shape))


def _full_once(shape):
    return pl.BlockSpec(shape, lambda *_: (0,) * len(shape), pipeline_mode=pl.Buffered(1))


def _rows(tm, width):
    return pl.BlockSpec((tm, width), lambda i: (i, 0))


def _rows_rev(tm, width, n):
    return pl.BlockSpec((tm, width), lambda i: (n - 1 - i, 0))


def _zero_at_start(*refs):
    @pl.when(pl.program_id(0) == 0)
    def _():
        for ref in refs:
            ref[...] = jnp.zeros_like(ref)


class _Exchange:
    def __init__(self, operands, out_shapes, scratch, start, finish, middle=None, middle_eighths=MIDDLE_EIGHTHS):
        self.operands, self.out_shapes, self.scratch = list(operands), list(out_shapes), list(scratch)
        self.start, self.middle, self.finish, self.middle_eighths = start, middle, finish, middle_eighths


def _call(body, *, name, grid, in_specs, out_specs, out_shape, args, scratch_shapes=(), exchange=None):
    semantics = ("arbitrary",) * len(grid)
    if exchange is None:
        out = pl.pallas_call(
            body, name=name, grid=grid, in_specs=in_specs, out_specs=out_specs, out_shape=out_shape,
            scratch_shapes=list(scratch_shapes), compiler_params=_params(semantics))(*args)
        return out, []
    hbm = pl.BlockSpec(memory_space=pltpu.HBM)
    n_in, n_out, n_scr = len(in_specs), len(out_specs), len(scratch_shapes)
    e_in, e_out = len(exchange.operands), len(exchange.out_shapes)

    def carried(*refs):
        ins, rest = refs[:n_in], refs[n_in:]
        e_ins, rest = rest[:e_in], rest[e_in:]
        outs, rest = rest[:n_out], rest[n_out:]
        e_outs, rest = rest[:e_out], rest[e_out:]
        scr, e_scr = rest[:n_scr], rest[n_scr:]
        first = last = None
        for axis, size in enumerate(grid):
            at_start, at_end = pl.program_id(axis) == 0, pl.program_id(axis) == size - 1
            first = at_start if first is None else jnp.logical_and(first, at_start)
            last = at_end if last is None else jnp.logical_and(last, at_end)

        @pl.when(first)
        def _():
            exchange.start(e_ins, e_outs, e_scr)

        body(*ins, *outs, *scr)

        if exchange.middle is not None:
            assert len(grid) == 1

            @pl.when(pl.program_id(0) == (grid[0] * exchange.middle_eighths) // 8)
            def _():
                exchange.middle(e_ins, e_outs, e_scr)

        @pl.when(last)
        def _():
            exchange.finish(e_ins, e_outs, e_scr)

    out = pl.pallas_call(
        carried, name=name, grid=grid, in_specs=list(in_specs) + [hbm] * e_in,
        out_specs=list(out_specs) + [hbm] * e_out, out_shape=list(out_shape) + exchange.out_shapes,
        scratch_shapes=list(scratch_shapes) + exchange.scratch,
        compiler_params=pltpu.CompilerParams(
            dimension_semantics=semantics, vmem_limit_bytes=VMEM_LIMIT, has_side_effects=True),
    )(*args, *exchange.operands)
    return out[:n_out], out[n_out:]


def _run_exchange(exchange, name):
    hbm = pl.BlockSpec(memory_space=pltpu.HBM)
    e_in, e_out = len(exchange.operands), len(exchange.out_shapes)

    def body(*refs):
        e_ins, e_outs, e_scr = refs[:e_in], refs[e_in:e_in + e_out], refs[e_in + e_out:]
        exchange.start(e_ins, e_outs, e_scr)
        if exchange.middle is not None:
            exchange.middle(e_ins, e_outs, e_scr)
        exchange.finish(e_ins, e_outs, e_scr)

    return pl.pallas_call(
        body, name=name, in_specs=[hbm] * e_in, out_specs=[hbm] * e_out, out_shape=exchange.out_shapes,
        scratch_shapes=exchange.scratch, compiler_params=pltpu.CompilerParams(has_side_effects=True),
    )(*exchange.operands)


def _loss_head(xo, gf, tgt):
    d = xo.shape[1]
    y, xh, r = _rms(xo, gf)
    err = y - tgt
    dy = err * (1.0 / d)
    loss = 0.5 * jnp.sum(jnp.sum(err * err, axis=-1, keepdims=True) * (1.0 / d), axis=0, keepdims=True)
    return _rms_bwd(dy, xh, r, gf), loss, jnp.sum(dy * xh, axis=0, keepdims=True)


def _ffn_fwd(x, g, wg, wu, wd, exchange=None, head=None):
    t, d = x.shape
    f = wg.shape[0]
    tm = min(WIDE_TILE, t)

    def body(x_ref, g_ref, wg_ref, wu_ref, wd_ref, *rest):
        if head is None:
            xo_ref, a_ref, b_ref, s_ref = rest
        else:
            gf_ref, tgt_ref, xo_ref, a_ref, b_ref, s_ref, loss_ref, dgf_ref = rest
            _zero_at_start(loss_ref, dgf_ref)
        xv = x_ref[...]
        h, _, _ = _rms(xv, g_ref[...])
        hb = h.astype(BF16)
        a = _dot(hb, wg_ref[...], NT)
        b = _dot(hb, wu_ref[...], NT)
        s = (a * _sigmoid(a) * b).astype(BF16)
        xo = xv + 0.5 * _dot(s, wd_ref[...])
        if head is None:
            xo_ref[...] = xo
        else:
            xo_ref[...], loss, dgf = _loss_head(xo, gf_ref[...], tgt_ref[...])
            loss_ref[...] += jnp.broadcast_to(loss, (1, 128))
            dgf_ref[...] += dgf
        a_ref[...] = a.astype(BF16)
        b_ref[...] = b.astype(BF16)
        s_ref[...] = s

    in_specs = [_rows(tm, d), _full((1, d)), _full_once((f, d)), _full_once((f, d)), _full_once((f, d))]
    out_specs = [_rows(tm, d), _rows(tm, f), _rows(tm, f), _rows(tm, f)]
    out_shape = [
        jax.ShapeDtypeStruct((t, d), F32),
        jax.ShapeDtypeStruct((t, f), BF16),
        jax.ShapeDtypeStruct((t, f), BF16),
        jax.ShapeDtypeStruct((t, f), BF16),
    ]
    args = (x, g, wg, wu, wd)
    if head is not None:
        in_specs += [_full((1, d)), _rows(tm, d)]
        out_specs += [_full((1, 128)), _full((1, d))]
        out_shape += [jax.ShapeDtypeStruct((1, 128), F32), jax.ShapeDtypeStruct((1, d), F32)]
        args += tuple(head)
    return _call(
        body, name="ffn_fwd", grid=(t // tm,), in_specs=in_specs, out_specs=out_specs, out_shape=out_shape,
        args=args, exchange=exchange)


def _ffn_up(x, g, wg, wu, exchange=None):
    t, d = x.shape
    f = wg.shape[0]
    tm = min(TOKEN_TILE, t)

    def body(x_ref, g_ref, wg_ref, wu_ref, a_ref, b_ref, s_ref):
        h, _, _ = _rms(x_ref[...], g_ref[...])
        hb = h.astype(BF16)
        a = _dot(hb, wg_ref[...], NT)
        b = _dot(hb, wu_ref[...], NT)
        a_ref[...] = a.astype(BF16)
        b_ref[...] = b.astype(BF16)
        s_ref[...] = (a * _sigmoid(a) * b).astype(BF16)

    return _call(
        body, name="ffn_up", grid=(t // tm,),
        in_specs=[_rows(tm, d), _full((1, d)), _full_once((f, d)), _full_once((f, d))],
        out_specs=[_rows(tm, f)] * 3, out_shape=[jax.ShapeDtypeStruct((t, f), BF16)] * 3,
        args=(x, g, wg, wu), exchange=exchange)


def _ffn_down(x, s, wd, exchange=None):
    t, d = x.shape
    f = wd.shape[0]
    tm = min(TOKEN_TILE, t)

    def body(x_ref, s_ref, wd_ref, xo_ref):
        xo_ref[...] = x_ref[...] + 0.5 * _dot(s_ref[...], wd_ref[...])

    return _call(
        body, name="ffn_down", grid=(t // tm,),
        in_specs=[_rows(tm, d), _rows(tm, f), _full_once((f, d))],
        out_specs=[_rows(tm, d)], out_shape=[jax.ShapeDtypeStruct((t, d), F32)],
        args=(x, s, wd), exchange=exchange)


def _ffn_bwd(x, g, dxo, a, b, wg, wu, wd, exchange=None):
    t, d = x.shape
    f = wg.shape[0]
    tm = min(TOKEN_TILE, t)

    def body(x_ref, g_ref, dxo_ref, a_ref, b_ref, wg_ref, wu_ref, wd_ref, dx_ref, da_ref, db_ref, h_ref, dg_ref):
        _zero_at_start(dg_ref)
        gv = g_ref[...]
        h, xh, r = _rms(x_ref[...], gv)
        dxo = dxo_ref[...]
        ds = _dot((0.5 * dxo).astype(BF16), wd_ref[...], NT)
        af = a_ref[...].astype(F32)
        bf = b_ref[...].astype(F32)
        sg = _sigmoid(af)
        da = (ds * bf * (sg * (1.0 + af * (1.0 - sg)))).astype(BF16)
        db = (ds * (af * sg)).astype(BF16)
        dh = _dot(da, wg_ref[...]) + _dot(db, wu_ref[...])
        dx_ref[...] = _rms_bwd(dh, xh, r, gv) + dxo
        da_ref[...] = da
        db_ref[...] = db
        h_ref[...] = h.astype(BF16)
        dg_ref[...] += jnp.sum(dh * xh, axis=0, keepdims=True)

    return _call(
        body,
        name="ffn_bwd",
        grid=(t // tm,),
        in_specs=[
            _rows(tm, d), _full((1, d)), _rows(tm, d), _rows(tm, f), _rows(tm, f),
            _full_once((f, d)), _full_once((f, d)), _full_once((f, d)),
        ],
        out_specs=[_rows(tm, d), _rows(tm, f), _rows(tm, f), _rows(tm, d), _full((1, d))],
        out_shape=[
            jax.ShapeDtypeStruct((t, d), F32),
            jax.ShapeDtypeStruct((t, f), BF16),
            jax.ShapeDtypeStruct((t, f), BF16),
            jax.ShapeDtypeStruct((t, d), BF16),
            jax.ShapeDtypeStruct((1, d), F32),
        ],
        args=(x, g, dxo, a, b, wg, wu, wd),
        exchange=exchange,
    )


def _weight_grad(products, exchange=None):
    count = len(products)
    t, m = products[0][0].shape
    n = products[0][1].shape[1]
    assert all(a.shape == (t, m) and b.shape == (t, n) for a, b, _ in products)
    chips = N_DEV // 2
    r = m // N_DEV
    tk = min(REDUCE_TILE, t)
    halves = 2
    nb = n // halves
    nk = t // tk

    def body(*refs):
        a_refs, b_refs, o_ref = refs[0:count], refs[count:2 * count], refs[2 * count]
        acc, send_buf, recv_buf, send_sems, recv_sems = refs[2 * count + 1:]
        k, j = pl.program_id(0), pl.program_id(1)
        x, y, c, _ = _mesh_place()
        sibling, _ = _peer(x, y, c, 1)
        for p, (_, _, scale) in enumerate(products):
            bv = b_refs[p][...]
            if scale != 1.0:
                bv = bv * scale
            bb = bv.astype(BF16)
            acc_half = acc.at[p, j]

            @pl.when(k == 0)
            def _():
                acc_half[...] = jnp.zeros_like(acc_half)

            for i in range(m // MXU_ROWS):
                rows = slice(i * MXU_ROWS, (i + 1) * MXU_ROWS)
                acc_half[rows, :] += _dot(a_refs[p][:, rows].astype(BF16), bb, TN)

        def to_sibling(half):
            return _remote(send_buf.at[half], recv_buf.at[half], send_sems.at[half], recv_sems.at[half], sibling)

        def owned_rows(q, core):
            return pl.ds(pl.multiple_of((2 * q + core) * r, 8), r)

        for half in range(halves):
            @pl.when(jnp.logical_and(k == nk - 1, j == half))
            def _():
                for p in range(count):
                    for q in range(chips):
                        send_buf[half, p, q] = acc[p, half, owned_rows(q, 1 - c), :].astype(BF16)
                to_sibling(half).start()

        @pl.when(jnp.logical_and(k == nk - 1, j == halves - 1))
        def _():
            for half in range(halves):
                to_sibling(half).wait_send()
                to_sibling(half).wait_recv()
                for p in range(count):
                    for q in range(chips):
                        o_ref[q, p * r:(p + 1) * r, half * nb:(half + 1) * nb] = (
                            acc[p, half, owned_rows(q, c), :] + recv_buf[half, p, q].astype(F32)).astype(BF16)

    (partial,), arrived = _call(
        body,
        name="weight_grad",
        grid=(nk, halves),
        in_specs=[pl.BlockSpec((tk, m), lambda k, j: (k, 0))] * count
        + [pl.BlockSpec((tk, nb), lambda k, j: (k, j))] * count,
        out_specs=[pl.BlockSpec((chips, count * r, n), lambda k, j: (0, 0, 0))],
        out_shape=[jax.ShapeDtypeStruct((chips, count * r, n), BF16)],
        scratch_shapes=[
            pltpu.VMEM((count, halves, m, nb), F32),
            pltpu.VMEM((halves, count, chips, r, nb), BF16), pltpu.VMEM((halves, count, chips, r, nb), BF16),
            pltpu.SemaphoreType.DMA((halves,)), pltpu.SemaphoreType.DMA((halves,)),
        ],
        args=tuple(a for a, _, _ in products) + tuple(b for _, b, _ in products),
        exchange=exchange,
    )
    return partial, arrived


def _chunk_cumsum(v, reverse=False):
    n, width = v.shape
    row = lax.broadcasted_iota(jnp.int32, (n, n), 0)
    col = lax.broadcasted_iota(jnp.int32, (n, n), 1)
    earlier = col >= row if reverse else col <= row
    tri = jnp.where(jnp.logical_and(row // CHUNK == col // CHUNK, earlier), 1.0, 0.0).astype(BF16)
    hi = v.astype(BF16)
    rest = v - hi.astype(F32)
    mid = rest.astype(BF16)
    low = (rest - mid.astype(F32)).astype(BF16)
    sums = _dot(tri, jnp.concatenate([hi, mid, low], axis=1))
    return sums[:, 0:width] + sums[:, width:2 * width] + sums[:, 2 * width:3 * width]


def _shift_rows(v, shift, edge):
    n = v.shape[0]
    row = lax.broadcasted_iota(jnp.int32, (n, 1), 0)
    out = pltpu.roll(v, shift % n, axis=0)
    if shift > 0:
        for j in range(shift):
            out = jnp.where(row == j, edge[8 - shift + j:8 - shift + j + 1, :], out)
    else:
        for j in range(-shift):
            out = jnp.where(row == n + shift + j, edge[j:j + 1, :], out)
    return out


def _gates(z, lbp):
    w = HGRN_W
    lb = _sigmoid(lbp[0:1, :] - lbp[1:2, :])
    zq = z[:, 0:w]
    sig = _sigmoid(z[:, w:2 * w])
    f = lb + (1.0 - lb) * sig
    sq = _sigmoid(zq)
    q = zq * sq * HGRN_DK ** -0.5
    return lb, sig, f, sq, q


def _decayed_operands(q, f, v, qm_buf, km_buf, kbar_buf, v_buf, etot_buf, emid_buf):
    n, width = f.shape
    bcum = _chunk_cumsum(jnp.log(f))

    def row_of_chunk(offset):
        return jnp.concatenate(
            [jnp.broadcast_to(bcum[c + offset:c + offset + 1, :], (CHUNK, width)) for c in range(0, n, CHUNK)], axis=0)

    total, mid = row_of_chunk(CHUNK - 1), row_of_chunk(CHUNK // 2 - 1)
    em, enm, erest = jnp.exp(bcum - mid), jnp.exp(mid - bcum), jnp.exp(total - bcum)
    kk = 1.0 - f
    qm_buf[...] = (q * em).astype(BF16)
    km_buf[...] = (kk * enm).astype(BF16)
    kbar_buf[...] = (kk * erest).astype(BF16)
    v_buf[...] = v.astype(BF16)
    etot_buf[...] = jnp.exp(total)
    emid_buf[...] = jnp.exp(mid)
    return em, enm, erest


def _short_conv(u, edge, cw):
    return cw[0:1, :] * _shift_rows(u, 2, edge) + cw[1:2, :] * _shift_rows(u, 1, edge) + cw[2:3, :] * u


def _block_causal_mask(n):
    row = lax.broadcasted_iota(jnp.int32, (n, n), 0)
    col = lax.broadcasted_iota(jnp.int32, (n, n), 1)
    return jnp.logical_and(row // CHUNK == col // CHUNK, col <= row)


def _spread(v, chunk_of_row, nc):
    return jnp.concatenate([jnp.where(chunk_of_row == c, v, jnp.zeros_like(v)) for c in range(nc)], axis=1)


def _pick(r, chunk_of_row, nc):
    out = jnp.where(chunk_of_row == 0, r[:, 0:HGRN_DK], 0.0)
    for c in range(1, nc):
        out = out + jnp.where(chunk_of_row == c, r[:, c * HGRN_DK:(c + 1) * HGRN_DK], 0.0)
    return out


def _mix_fwd(x, g, w_in, lbp, gh, convw_t, w_out, exchange=None):
    t, d = x.shape
    zw = w_in.shape[0]
    w = HGRN_W
    tm = min(TOKEN_TILE, t)
    nc = tm // CHUNK
    n_chunks = t // CHUNK

    def body(x_ref, g_ref, win_ref, lbp_ref, gh_ref, cw_ref, wout_ref,
             xo_ref, z_ref, o_ref, st_ref, y_ref, state, ucarry, qm_buf, km_buf, kbar_buf, v_buf, etot_buf, emid_buf):
        _zero_at_start(state, ucarry)
        xv = x_ref[...]
        h, _, _ = _rms(xv, g_ref[...])
        z_ref[...] = _dot(h.astype(BF16), win_ref[...], NT)
        z = z_ref[...]
        _, _, f, _, q = _gates(z, lbp_ref[...])
        _decayed_operands(q, f, z[:, 2 * w:3 * w], qm_buf, km_buf, kbar_buf, v_buf, etot_buf, emid_buf)
        mask = _block_causal_mask(tm)
        chunk_of_row = lax.broadcasted_iota(jnp.int32, (tm, 1), 0) // CHUNK
        heads = range(HGRN_HEADS)
        hcols = [slice(hd * HGRN_DK, (hd + 1) * HGRN_DK) for hd in heads]
        qm = [qm_buf[:, hcols[hd]] for hd in heads]
        vb = [v_buf[:, hcols[hd]] for hd in heads]
        scores = [jnp.where(mask, _dot(qm[hd], km_buf[:, hcols[hd]], NT), 0.0).astype(BF16) for hd in heads]
        gains = [_dot(_spread(vb[hd], chunk_of_row, nc), kbar_buf[:, hcols[hd]], TN) for hd in heads]
        entering = []
        for hd in heads:
            states, st = [], state[hd]
            for c in range(nc):
                first_row = slice(c * CHUNK, c * CHUNK + 1)
                states.append(st * emid_buf[first_row, hcols[hd]])
                st_ref[c, hd] = st
                st = st * etot_buf[first_row, hcols[hd]] + gains[hd][c * HGRN_DK:(c + 1) * HGRN_DK, :]
            state[hd] = st
            entering.append(jnp.concatenate(states, axis=0).astype(BF16))
        from_states = [_dot(qm[hd], entering[hd], NT) for hd in heads]
        o_heads = [_dot(scores[hd], vb[hd]) + _pick(from_states[hd], chunk_of_row, nc) for hd in heads]
        o_ref[...] = jnp.concatenate(o_heads, axis=1)
        ghv = gh_ref[...]
        normed = jnp.concatenate([_rms(o_heads[hd], ghv[:, hcols[hd]])[0] for hd in heads], axis=1)
        zg = z[:, 3 * w:4 * w]
        u = z[:, 5 * w:6 * w] * z[:, 6 * w:7 * w]
        conv = _short_conv(u, ucarry[...], cw_ref[...])
        ucarry[...] = u[tm - 8:tm, :]
        y = jnp.concatenate([normed * (zg * _sigmoid(zg)), z[:, 4 * w:5 * w] * conv], axis=1).astype(BF16)
        y_ref[...] = y
        xo_ref[...] = xv + _dot(y, wout_ref[...])

    return _call(
        body,
        name="mix_fwd",
        grid=(t // tm,),
        in_specs=[
            _rows(tm, d), _full((1, d)), _full((zw, d)), _full((2, w)), _full((1, w)), _full((3, w)),
            _full((2 * w, d)),
        ],
        out_specs=[
            _rows(tm, d), _rows(tm, zw), _rows(tm, w),
            pl.BlockSpec((nc, HGRN_HEADS, HGRN_DK, HGRN_DK), lambda i: (i, 0, 0, 0)),
            _rows(tm, 2 * w),
        ],
        out_shape=[
            jax.ShapeDtypeStruct((t, d), F32),
            jax.ShapeDtypeStruct((t, zw), F32),
            jax.ShapeDtypeStruct((t, w), F32),
            jax.ShapeDtypeStruct((n_chunks, HGRN_HEADS, HGRN_DK, HGRN_DK), F32),
            jax.ShapeDtypeStruct((t, 2 * w), BF16),
        ],
        scratch_shapes=[
            pltpu.VMEM((HGRN_HEADS, HGRN_DK, HGRN_DK), F32), pltpu.VMEM((8, w), F32),
            pltpu.VMEM((tm, w), BF16), pltpu.VMEM((tm, w), BF16), pltpu.VMEM((tm, w), BF16),
            pltpu.VMEM((tm, w), BF16), pltpu.VMEM((tm, w), F32), pltpu.VMEM((tm, w), F32),
        ],
        args=(x, g, w_in, lbp, gh, convw_t, w_out),
        exchange=exchange,
    )


def _mix_bwd(x, g, dxo, z, o, states, w_in, lbp, gh, convw_t, w_out, exchange=None):
    t, d = x.shape
    zw = w_in.shape[0]
    w = HGRN_W
    tm = min(TOKEN_TILE, t)
    nc = tm // CHUNK
    n = t // tm

    def body(x_ref, g_ref, dxo_ref, z_ref, zprev_ref, o_ref, st_ref, win_ref, lbp_ref, gh_ref, cw_ref, wout_ref,
             dx_ref, dz_ref, h_ref, dg_ref, dlbp_ref, dgh_ref, dcw_ref,
             dstate, dcarry, do_buf, qm_buf, km_buf, kbar_buf, v_buf, etot_buf, emid_buf):
        _zero_at_start(dstate, dcarry, dg_ref, dlbp_ref, dgh_ref, dcw_ref)
        gv = g_ref[...]
        h, xh, r = _rms(x_ref[...], gv)
        h_ref[...] = h.astype(BF16)
        dxo = dxo_ref[...]
        dy = _dot(dxo.astype(BF16), wout_ref[...], NT)
        z = z_ref[...]
        lb, sig, f, sq, q = _gates(z, lbp_ref[...])
        em, enm, erest = _decayed_operands(
            q, f, z[:, 2 * w:3 * w], qm_buf, km_buf, kbar_buf, v_buf, etot_buf, emid_buf)

        ghv = gh_ref[...]
        zg = z[:, 3 * w:4 * w]
        sgz = _sigmoid(zg)
        dyh = dy[:, 0:w]
        don = dyh * (zg * sgz)
        heads = range(HGRN_HEADS)
        hcols = [slice(hd * HGRN_DK, (hd + 1) * HGRN_DK) for hd in heads]
        norms = [_rms(o_ref[:, hcols[hd]], ghv[:, hcols[hd]]) for hd in heads]
        on = jnp.concatenate([norms[hd][0] for hd in heads], axis=1)
        oh = jnp.concatenate([norms[hd][1] for hd in heads], axis=1)
        dz_ref[:, 3 * w:4 * w] = (dyh * on * (sgz * (1.0 + zg * (1.0 - sgz)))).astype(BF16)
        dgh_ref[...] += jnp.sum(don * oh, axis=0, keepdims=True)
        do_buf[...] = jnp.concatenate(
            [_rms_bwd(don[:, hcols[hd]], norms[hd][1], norms[hd][2], ghv[:, hcols[hd]]) for hd in heads],
            axis=1).astype(BF16)

        zb = z[:, 4 * w:5 * w]
        zc = z[:, 5 * w:6 * w]
        zu = z[:, 6 * w:7 * w]
        u = zc * zu
        cw = cw_ref[...]
        zp = zprev_ref[...]
        uprev = jnp.where(pl.program_id(0) == n - 1, 0.0, zp[:, 5 * w:6 * w] * zp[:, 6 * w:7 * w])
        dyc = dy[:, w:2 * w]
        dz_ref[:, 4 * w:5 * w] = (dyc * _short_conv(u, uprev, cw)).astype(BF16)
        dconv = dyc * zb
        edge = dcarry[...]
        dconv1 = _shift_rows(dconv, -1, edge)
        dconv2 = _shift_rows(dconv, -2, edge)
        dcarry[...] = dconv[0:8, :]
        du = cw[2:3, :] * dconv + cw[1:2, :] * dconv1 + cw[0:1, :] * dconv2
        dz_ref[:, 5 * w:6 * w] = (du * zu).astype(BF16)
        dz_ref[:, 6 * w:7 * w] = (du * zc).astype(BF16)
        dcw_ref[...] += jnp.concatenate([
            jnp.sum(u * dconv2, axis=0, keepdims=True),
            jnp.sum(u * dconv1, axis=0, keepdims=True),
            jnp.sum(u * dconv, axis=0, keepdims=True)], axis=0)

        mask = _block_causal_mask(tm)
        chunk_of_row = lax.broadcasted_iota(jnp.int32, (tm, 1), 0) // CHUNK
        heads = range(HGRN_HEADS)
        hcols = [slice(hd * HGRN_DK, (hd + 1) * HGRN_DK) for hd in heads]
        qmb = [qm_buf[:, hcols[hd]] for hd in heads]
        kmb = [km_buf[:, hcols[hd]] for hd in heads]
        vb = [v_buf[:, hcols[hd]] for hd in heads]
        dob = [do_buf[:, hcols[hd]] for hd in heads]
        scores = [jnp.where(mask, _dot(qmb[hd], kmb[hd], NT), 0.0).astype(BF16) for hd in heads]
        dscores = [jnp.where(mask, _dot(dob[hd], vb[hd], NT), 0.0).astype(BF16) for hd in heads]
        gains = [_dot(_spread(dob[hd], chunk_of_row, nc), qmb[hd], TN) for hd in heads]
        dst_rows, dst_lanes, st_lanes, carries = [], [], [], []
        for hd in heads:
            entering = [st_ref[c, hd] for c in range(nc)]
            emid = [emid_buf[c * CHUNK:c * CHUNK + 1, hcols[hd]] for c in range(nc)]
            leaving, carried_back = [None] * nc, [None] * nc
            dst = dstate[hd]
            for c in reversed(range(nc)):
                elast = etot_buf[c * CHUNK:c * CHUNK + 1, hcols[hd]]
                leaving[c] = dst
                carried_back[c] = jnp.sum(dst * entering[c], axis=0, keepdims=True) * elast
                dst = dst * elast + gains[hd][c * HGRN_DK:(c + 1) * HGRN_DK, :] * emid[c]
            dstate[hd] = dst
            dst_rows.append(jnp.concatenate(leaving, axis=0).astype(BF16))
            dst_lanes.append(jnp.concatenate(leaving, axis=1).astype(BF16))
            st_lanes.append(jnp.concatenate([entering[c] * emid[c] for c in range(nc)], axis=1).astype(BF16))
            carries.append(carried_back)
        dv = [_dot(scores[hd], dob[hd], TN) + _pick(_dot(kbar_buf[:, hcols[hd]], dst_rows[hd], NT), chunk_of_row, nc)
              for hd in heads]
        dz_ref[:, 2 * w:3 * w] = jnp.concatenate(dv, axis=1).astype(BF16)
        dqm = jnp.concatenate([_dot(dscores[hd], kmb[hd]) + _pick(_dot(dob[hd], st_lanes[hd]), chunk_of_row, nc)
                               for hd in heads], axis=1)
        dkm = jnp.concatenate([_dot(dscores[hd], qmb[hd], TN) for hd in heads], axis=1)
        dkbar = jnp.concatenate([_pick(_dot(vb[hd], dst_lanes[hd]), chunk_of_row, nc) for hd in heads], axis=1)

        kbar_dkbar = kbar_buf[...].astype(F32) * dkbar
        db = qm_buf[...].astype(F32) * dqm - km_buf[...].astype(F32) * dkm - kbar_dkbar
        through_last = jnp.concatenate([
            jnp.broadcast_to(
                jnp.sum(kbar_dkbar[c * CHUNK:(c + 1) * CHUNK], axis=0, keepdims=True)
                + jnp.concatenate([carries[hd][c] for hd in heads], axis=1),
                (CHUNK, w))
            for c in range(nc)], axis=0)
        dlogf = _chunk_cumsum(db, reverse=True) + through_last
        df = dlogf / f - (dkm * enm + dkbar * erest)
        zq = z[:, 0:w]
        dz_ref[:, 0:w] = (dqm * em * HGRN_DK ** -0.5 * (sq * (1.0 + zq * (1.0 - sq)))).astype(BF16)
        dz_ref[:, w:2 * w] = (df * (1.0 - lb) * sig * (1.0 - sig)).astype(BF16)
        dlb = jnp.sum(df * (1.0 - sig), axis=0, keepdims=True) * lb * (1.0 - lb)
        dlbp_ref[...] += jnp.concatenate([dlb, -dlb], axis=0)

        dh = _dot(dz_ref[...], win_ref[...])
        dx_ref[...] = _rms_bwd(dh, xh, r, gv) + dxo
        dg_ref[...] += jnp.sum(dh * xh, axis=0, keepdims=True)

    return _call(
        body,
        name="mix_bwd",
        grid=(n,),
        in_specs=[
            _rows_rev(tm, d, n), _full((1, d)), _rows_rev(tm, d, n), _rows_rev(tm, zw, n),
            pl.BlockSpec((8, zw), lambda i: (jnp.maximum((n - 1 - i) * (tm // 8) - 1, 0), 0)),
            _rows_rev(tm, w, n),
            pl.BlockSpec((nc, HGRN_HEADS, HGRN_DK, HGRN_DK), lambda i: (n - 1 - i, 0, 0, 0)),
            _full((zw, d)), _full((2, w)), _full((1, w)), _full((3, w)), _full((2 * w, d)),
        ],
        out_specs=[
            _rows_rev(tm, d, n), _rows_rev(tm, zw, n), _rows_rev(tm, d, n),
            _full((1, d)), _full((2, w)), _full((1, w)), _full((3, w)),
        ],
        out_shape=[
            jax.ShapeDtypeStruct((t, d), F32),
            jax.ShapeDtypeStruct((t, zw), BF16),
            jax.ShapeDtypeStruct((t, d), BF16),
            jax.ShapeDtypeStruct((1, d), F32),
            jax.ShapeDtypeStruct((2, w), F32),
            jax.ShapeDtypeStruct((1, w), F32),
            jax.ShapeDtypeStruct((3, w), F32),
        ],
        scratch_shapes=[
            pltpu.VMEM((HGRN_HEADS, HGRN_DK, HGRN_DK), F32), pltpu.VMEM((8, w), F32),
            pltpu.VMEM((tm, w), BF16),
            pltpu.VMEM((tm, w), BF16), pltpu.VMEM((tm, w), BF16), pltpu.VMEM((tm, w), BF16),
            pltpu.VMEM((tm, w), BF16), pltpu.VMEM((tm, w), F32), pltpu.VMEM((tm, w), F32),
        ],
        args=(x, g, dxo, z, z, o, states, w_in, lbp, gh, convw_t, w_out),
        exchange=exchange,
    )


def _memkv_fwd(mem, g, wkv):
    m, d = mem.shape
    nb, _, cb = wkv.shape

    def body(mem_ref, g_ref, wkv_ref, kv_ref):
        mn, _, _ = _rms(mem_ref[...], g_ref[...])
        mnb = mn.astype(BF16)
        for j in range(nb):
            kv_ref[:, j * cb:(j + 1) * cb] = _dot(mnb, wkv_ref[j]).astype(BF16)

    return pl.pallas_call(
        body,
        name="memkv_fwd",
        out_shape=jax.ShapeDtypeStruct((m, nb * cb), BF16),
        compiler_params=_params(),
    )(mem, g, wkv)


def _memkv_bwd(mem, g, dkv, wkv):
    m, d = mem.shape
    nb, _, cb = wkv.shape
    chips = nb // 2

    def body(mem_ref, g_ref, dkv_ref, wkv_ref, dw_ref, dg_ref, dw_all, send_buf, recv_buf, send_sem, recv_sem):
        x, y, c, _ = _mesh_place()
        sibling, _ = _peer(x, y, c, 1)
        mn, xh, _ = _rms(mem_ref[...], g_ref[...])
        mnb = mn.astype(BF16)
        dmn = jnp.zeros((m, d), F32)
        for j in range(nb):
            dkvb = dkv_ref[:, j * cb:(j + 1) * cb].astype(BF16)
            dw_all[j] = _dot(mnb, dkvb, TN)
            dmn = dmn + _dot(dkvb, wkv_ref[j], NT)
        dg_ref[...] = jnp.sum(dmn * xh, axis=0, keepdims=True)
        for q in range(chips):
            send_buf[q] = dw_all[2 * q + 1 - c].astype(BF16)
        to_sibling = _remote(send_buf, recv_buf, send_sem, recv_sem, sibling)
        to_sibling.start()
        to_sibling.wait_send()
        to_sibling.wait_recv()
        for q in range(chips):
            dw_ref[q] = (dw_all[2 * q + c] + recv_buf[q].astype(F32)).astype(BF16)

    return pl.pallas_call(
        body,
        name="memkv_bwd",
        out_shape=[jax.ShapeDtypeStruct((chips, d, cb), BF16), jax.ShapeDtypeStruct((1, d), F32)],
        scratch_shapes=[
            pltpu.VMEM((nb, d, cb), F32), pltpu.VMEM((chips, d, cb), BF16), pltpu.VMEM((chips, d, cb), BF16),
            pltpu.SemaphoreType.DMA, pltpu.SemaphoreType.DMA,
        ],
        compiler_params=_params(),
    )(mem, g, dkv, wkv)


def _softmax_rows(qm_h, k_h):
    sc = _dot(qm_h, k_h, NT) * MEM_HD ** -0.5
    e = jnp.exp(sc - jnp.max(sc, axis=-1, keepdims=True))
    return e / jnp.sum(e, axis=-1, keepdims=True)


def _xattn_fwd(x, g, wq, kv, wo, exchange=None):
    t, d = x.shape
    m = kv.shape[0]
    tm = min(XATTN_TILE, t)

    def body(x_ref, g_ref, wq_ref, kv_ref, wo_ref, xo_ref, hq_ref, qm_ref, att_ref):
        xv = x_ref[...]
        h, _, _ = _rms(xv, g_ref[...])
        hb = h.astype(BF16)
        hq_ref[...] = hb
        qm = _dot(hb, wq_ref[...]).astype(BF16)
        qm_ref[...] = qm
        heads = range(MEM_HEADS)
        kcols = [slice(hd * MEM_HD, (hd + 1) * MEM_HD) for hd in heads]
        p = [_softmax_rows(qm[:, kcols[hd]], kv_ref[:, kcols[hd]]) for hd in heads]
        att = jnp.concatenate(
            [_dot(p[hd].astype(BF16), kv_ref[:, d + hd * MEM_HD:d + (hd + 1) * MEM_HD]) for hd in heads],
            axis=1).astype(BF16)
        att_ref[...] = att
        xo_ref[...] = xv + _dot(att, wo_ref[...])

    return _call(
        body,
        name="xattn_fwd",
        grid=(t // tm,),
        in_specs=[_rows(tm, d), _full((1, d)), _full((d, d)), _full((m, 2 * d)), _full((d, d))],
        out_specs=[_rows(tm, d), _rows(tm, d), _rows(tm, d), _rows(tm, d)],
        out_shape=[
            jax.ShapeDtypeStruct((t, d), F32),
            jax.ShapeDtypeStruct((t, d), BF16),
            jax.ShapeDtypeStruct((t, d), BF16),
            jax.ShapeDtypeStruct((t, d), BF16),
        ],
        args=(x, g, wq, kv, wo),
        exchange=exchange,
    )


def _xattn_bwd(x, g, dxo, qm, kv, wq, wo, exchange=None):
    t, d = x.shape
    m = kv.shape[0]
    tm = min(XATTN_TILE, t)

    def body(x_ref, g_ref, dxo_ref, qm_ref, kv_ref, wq_ref, wo_ref, dx_ref, dqm_ref, dkv_ref, dg_ref):
        _zero_at_start(dkv_ref, dg_ref)
        gv = g_ref[...]
        _, xh, r = _rms(x_ref[...], gv)
        dxo = dxo_ref[...]
        datt = _dot(dxo.astype(BF16), wo_ref[...], NT).astype(BF16)
        heads = range(MEM_HEADS)
        kcols = [slice(hd * MEM_HD, (hd + 1) * MEM_HD) for hd in heads]
        vcols = [slice(d + hd * MEM_HD, d + (hd + 1) * MEM_HD) for hd in heads]
        qm_h = [qm_ref[:, kcols[hd]] for hd in heads]
        p = [_softmax_rows(qm_h[hd], kv_ref[:, kcols[hd]]) for hd in heads]
        dp = [_dot(datt[:, kcols[hd]], kv_ref[:, vcols[hd]], NT) for hd in heads]
        dsc = [(p[hd] * (dp[hd] - jnp.sum(p[hd] * dp[hd], axis=-1, keepdims=True)) * MEM_HD ** -0.5).astype(BF16)
               for hd in heads]
        dqm = jnp.concatenate([_dot(dsc[hd], kv_ref[:, kcols[hd]]) for hd in heads], axis=1).astype(BF16)
        dqm_ref[...] = dqm
        dkv_ref[...] += jnp.concatenate(
            [_dot(dsc[hd], qm_h[hd], TN) for hd in heads]
            + [_dot(p[hd].astype(BF16), datt[:, kcols[hd]], TN) for hd in heads], axis=1)
        dh = _dot(dqm, wq_ref[...], NT)
        dx_ref[...] = _rms_bwd(dh, xh, r, gv) + dxo
        dg_ref[...] += jnp.sum(dh * xh, axis=0, keepdims=True)

    return _call(
        body,
        name="xattn_bwd",
        grid=(t // tm,),
        in_specs=[
            _rows(tm, d), _full((1, d)), _rows(tm, d), _rows(tm, d), _full((m, 2 * d)), _full((d, d)), _full((d, d)),
        ],
        out_specs=[_rows(tm, d), _rows(tm, d), _full((m, 2 * d)), _full((1, d))],
        out_shape=[
            jax.ShapeDtypeStruct((t, d), F32),
            jax.ShapeDtypeStruct((t, d), BF16),
            jax.ShapeDtypeStruct((m, 2 * d), F32),
            jax.ShapeDtypeStruct((1, d), F32),
        ],
        args=(x, g, dxo, qm, kv, wq, wo),
        exchange=exchange,
    )


def _mesh_place():
    x, y, c = lax.axis_index("x"), lax.axis_index("y"), lax.axis_index("c")
    return x, y, c, 4 * x + 2 * y + c


def _peer(x, y, c, k):
    px = 1 - x if k & 4 else x
    py = 1 - y if k & 2 else y
    pc = 1 - c if k & 1 else c
    return (px, py, pc), 4 * px + 2 * py + pc


ICI_HOPS = (2, 4, 6)
N_HOPS = len(ICI_HOPS)


def _remote(src, dst, send_sem, recv_sem, peer):
    return pltpu.make_async_remote_copy(
        src_ref=src, dst_ref=dst, send_sem=send_sem, recv_sem=recv_sem, device_id=peer, device_id_type=MESH_IDS)


def _gather_exchange(shards, middle_eighths=MIDDLE_EIGHTHS):
    n = len(shards)

    def place():
        x, y, c, me = _mesh_place()
        sibling, _ = _peer(x, y, c, 1)
        to_x, from_x = _peer(x, y, c, 4)
        to_y, from_y = _peer(x, y, c, 2)
        _, from_diagonal = _peer(x, y, c, 6)
        onward = (c * to_y[0] + (1 - c) * to_x[0], c * to_y[1] + (1 - c) * to_x[1], c)
        passed_on = c * from_x + (1 - c) * from_y
        return me, sibling, (to_x, to_y, onward), (from_x, from_y, from_diagonal), passed_on

    def start(src, dst, sems):
        ici_send, ici_recv, pair_send, pair_recv, local = sems
        me, sibling, targets, _, _ = place()
        for a in range(n):
            pltpu.make_async_copy(src[a], dst[a].at[me], local.at[a]).start()
            for j in range(2):
                _remote(src[a], dst[a].at[me], ici_send.at[a, j], ici_recv.at[a, j], targets[j]).start()
            _remote(src[a], dst[a].at[me], pair_send.at[a, 0], pair_recv.at[a, 0], sibling).start()

    def to_sibling(dst, sems, a, j, origin, sibling):
        _, _, pair_send, pair_recv, _ = sems
        slot = dst[a].at[origin]
        return _remote(slot, slot, pair_send.at[a, 1 + j], pair_recv.at[a, 1 + j], sibling)

    def middle(src, dst, sems):
        ici_send, ici_recv, _, _, _ = sems
        _, sibling, targets, origins, passed_on = place()
        for a in range(n):
            for j in range(2):
                _remote(src[a], dst[a].at[origins[j]], ici_send.at[a, j], ici_recv.at[a, j], targets[j]).wait_recv()
            slot = dst[a].at[passed_on]
            _remote(slot, slot, ici_send.at[a, 2], ici_recv.at[a, 2], targets[2]).start()
            for j in range(2):
                to_sibling(dst, sems, a, j, origins[j], sibling).start()

    def finish(src, dst, sems):
        ici_send, ici_recv, pair_send, pair_recv, local = sems
        me, sibling, targets, origins, _ = place()
        for a in range(n):
            _remote(src[a], dst[a].at[origins[2]], ici_send.at[a, 2], ici_recv.at[a, 2], targets[2]).wait_recv()
            to_sibling(dst, sems, a, 2, origins[2], sibling).start()
        for a in range(n):
            pltpu.make_async_copy(src[a], dst[a].at[me], local.at[a]).wait()
            for j in range(N_HOPS):
                _remote(src[a], dst[a].at[me], ici_send.at[a, j], ici_recv.at[a, j], targets[j]).wait_send()
            for j, origin in enumerate((me,) + origins):
                from_sibling = origin + 1 - 2 * (origin % 2)
                passed = _remote(src[a], dst[a].at[from_sibling], pair_send.at[a, j], pair_recv.at[a, j], sibling)
                passed.wait_send()
                passed.wait_recv()

    return _Exchange(
        shards,
        [jax.ShapeDtypeStruct((N_DEV,) + s.shape, s.dtype) for s in shards],
        [
            pltpu.SemaphoreType.DMA((n, N_HOPS)), pltpu.SemaphoreType.DMA((n, N_HOPS)),
            pltpu.SemaphoreType.DMA((n, N_HOPS + 1)), pltpu.SemaphoreType.DMA((n, N_HOPS + 1)),
            pltpu.SemaphoreType.DMA((n,)),
        ],
        start, finish, middle, middle_eighths)


def _scatter_copies(src, dst, sems, n, arrivals=False):
    send, recv, local = sems
    x, y, c, _ = _mesh_place()
    chip = 2 * x + y
    if arrivals is None:
        return [pltpu.make_async_copy(src[a].at[chip], dst[a].at[chip], local.at[a]) for a in range(n)]
    copies = []
    for a in range(n):
        for j, k in enumerate(ICI_HOPS):
            peer, _ = _peer(x, y, c, k)
            peer_chip = 2 * peer[0] + peer[1]
            slot = dst[a].at[peer_chip if arrivals else chip]
            copies.append(_remote(src[a].at[peer_chip], slot, send.at[a, j], recv.at[a, j], peer))
    return copies


def _scatter_start(src, dst, sems, n):
    for cp in _scatter_copies(src, dst, sems, n, arrivals=None) + _scatter_copies(src, dst, sems, n):
        cp.start()


def _scatter_finish(src, dst, sems, n):
    for cp in _scatter_copies(src, dst, sems, n, arrivals=None):
        cp.wait()
    for cp in _scatter_copies(src, dst, sems, n):
        cp.wait_send()
    for cp in _scatter_copies(src, dst, sems, n, arrivals=True):
        cp.wait_recv()


def _scatter_scratch(n):
    return [pltpu.SemaphoreType.DMA((n, N_HOPS)), pltpu.SemaphoreType.DMA((n, N_HOPS)), pltpu.SemaphoreType.DMA((n,))]


def _scatter_exchange(partials):
    n = len(partials)
    return _Exchange(
        partials, [jax.ShapeDtypeStruct(p.shape, p.dtype) for p in partials], _scatter_scratch(n),
        lambda src, dst, sems: _scatter_start(src, dst, sems, n),
        lambda src, dst, sems: _scatter_finish(src, dst, sems, n))


SMALL_LAYOUT = {
    "ffn1_norm": (0, 1, 1024), "mix_norm": (1, 1, 1024), "xattn_norm": (2, 1, 1024), "mem_norm": (3, 1, 1024),
    "ffn2_norm": (4, 1, 1024), "final_norm": (5, 1, 1024), "lb_param": (6, 2, 512), "hgrn_out_norm": (8, 1, 512),
    "conv_w": (9, 3, 512), "loss": (12, 1, 128),
}


def _final_exchange(partials, small):
    n = len(partials)
    names = list(small)
    width = 1024

    def body(*refs):
        src = refs[:n]
        pieces = refs[n:n + len(names)]
        dst = refs[n + len(names):2 * n + len(names)]
        total_ref = refs[2 * n + len(names)]
        pack, gathered, small_send, small_recv = refs[2 * n + len(names) + 1:2 * n + len(names) + 5]
        sems = refs[2 * n + len(names) + 5:]
        x, y, c, me = _mesh_place()
        pack[...] = jnp.zeros_like(pack)
        for name, piece in zip(names, pieces):
            row, nrows, ncols = SMALL_LAYOUT[name]
            pack[row:row + nrows, 0:ncols] = piece[...]
        for k in range(1, N_DEV):
            peer, _ = _peer(x, y, c, k)
            _remote(pack, gathered.at[me], small_send.at[k - 1], small_recv.at[k - 1], peer).start()
        _scatter_start(src, dst, sems, n)
        gathered[me] = pack[...]
        for k in range(1, N_DEV):
            peer, peer_index = _peer(x, y, c, k)
            landed = _remote(pack, gathered.at[peer_index], small_send.at[k - 1], small_recv.at[k - 1], peer)
            landed.wait_send()
            landed.wait_recv()
        total = gathered[0]
        for j in range(1, N_DEV):
            total = total + gathered[j]
        total_ref[...] = total
        _scatter_finish(src, dst, sems, n)

    hbm = pl.BlockSpec(memory_space=pltpu.HBM)
    vmem = pl.BlockSpec(memory_space=pltpu.VMEM)
    out = pl.pallas_call(
        body,
        name="final_exchange",
        in_specs=[hbm] * n + [vmem] * len(names),
        out_specs=[hbm] * n + [vmem],
        out_shape=[jax.ShapeDtypeStruct(p.shape, p.dtype) for p in partials]
        + [jax.ShapeDtypeStruct((SMALL_ROWS, width), F32)],
        scratch_shapes=[
            pltpu.VMEM((SMALL_ROWS, width), F32), pltpu.VMEM((N_DEV, SMALL_ROWS, width), F32),
            pltpu.SemaphoreType.DMA((N_DEV - 1,)), pltpu.SemaphoreType.DMA((N_DEV - 1,)),
        ] + _scatter_scratch(n),
        compiler_params=pltpu.CompilerParams(has_side_effects=True),
    )(*partials, *[small[k] for k in names])
    return out[:n], out[n]


def _adamw_math(w, g, m, v):
    m = ADAM_B1 * m + (1.0 - ADAM_B1) * g
    v = ADAM_B2 * v + (1.0 - ADAM_B2) * (g * g)
    m_hat = m / (1.0 - ADAM_B1 ** ADAM_STEP)
    v_hat = v / (1.0 - ADAM_B2 ** ADAM_STEP)
    delta = -ADAM_LR * (m_hat / (jnp.sqrt(v_hat) + ADAM_EPS) + ADAM_WD * w)
    return delta, m, v


def _adamw_shard(parts, w, m, v):
    r, c = w.shape
    n_parts = parts.shape[0]
    tr = max(rows for rows in range(16, r + 1, 16) if r % rows == 0 and rows * c <= ADAMW_TILE_ELEMENTS)

    def body(p_ref, w_ref, m_ref, v_ref, g_ref, d_ref, mo_ref, vo_ref):
        g = p_ref[0].astype(F32)
        for j in range(1, n_parts):
            g = g + p_ref[j].astype(F32)
        delta, mn, vn = _adamw_math(w_ref[...], g, m_ref[...], v_ref[...])
        g_ref[...] = g
        d_ref[...] = delta
        mo_ref[...] = mn
        vo_ref[...] = vn

    tile = pl.BlockSpec((tr, c), lambda i: (i, 0))
    return pl.pallas_call(
        body,
        name="adamw_shard",
        grid=(r // tr,),
        in_specs=[pl.BlockSpec((n_parts, tr, c), lambda i: (0, i, 0)), tile, tile, tile],
        out_specs=[tile] * 4,
        out_shape=[jax.ShapeDtypeStruct((r, c), F32)] * 4,
        compiler_params=_params(("parallel",)),
    )(parts, w, m, v)


def _adamw_small(gs, ws, ms, vs):
    n = len(gs)

    def body(*refs):
        g_refs, w_refs, m_refs, v_refs = refs[:n], refs[n:2 * n], refs[2 * n:3 * n], refs[3 * n:4 * n]
        g_out, d_out, m_out, v_out = refs[4 * n:5 * n], refs[5 * n:6 * n], refs[6 * n:7 * n], refs[7 * n:8 * n]
        for i in range(n):
            if gs[i].ndim == ws[i].ndim:
                g = g_refs[i][...]
            else:
                g = g_refs[i][0].astype(F32)
                for j in range(1, gs[i].shape[0]):
                    g = g + g_refs[i][j].astype(F32)
            delta, mn, vn = _adamw_math(w_refs[i][...], g, m_refs[i][...], v_refs[i][...])
            g_out[i][...] = g
            d_out[i][...] = delta
            m_out[i][...] = mn
            v_out[i][...] = vn

    shapes = [jax.ShapeDtypeStruct(w.shape, F32) for w in ws]
    out = pl.pallas_call(
        body,
        name="adamw_small",
        out_shape=shapes * 4,
        compiler_params=_params(),
    )(*gs, *ws, *ms, *vs)
    return out[:n], out[n:2 * n], out[2 * n:3 * n], out[3 * n:]


TRANSPOSED = ("ffn1_gate", "ffn1_up", "w_in", "ffn2_gate", "ffn2_up", "conv_w")
GROUP_FFN1 = ("ffn1_gate", "ffn1_up", "ffn1_down")
GROUP_MIX = ("w_in", "w_out")
GROUP_XATTN = ("w_q_mem", "w_kv_mem", "w_o_mem")
GROUP_FFN2 = ("ffn2_gate", "ffn2_up", "ffn2_down")
LARGE = GROUP_FFN1 + GROUP_MIX + GROUP_XATTN + GROUP_FFN2
SHORT_SHARDS = ("w_out", "w_q_mem", "w_kv_mem", "w_o_mem")
SMALL = ("ffn1_norm", "mix_norm", "lb_param", "hgrn_out_norm", "conv_w", "xattn_norm", "mem_norm", "ffn2_norm",
         "final_norm")
WEIGHTS = ("ffn1_norm", "ffn1_gate", "ffn1_up", "ffn1_down", "mix_norm", "w_in", "lb_param", "hgrn_out_norm",
           "conv_w", "w_out", "xattn_norm", "mem_norm", "w_q_mem", "w_kv_mem", "w_o_mem", "ffn2_norm", "ffn2_gate",
           "ffn2_up", "ffn2_down", "final_norm")


def kernel(x, mem, ffn1_norm, ffn1_gate, ffn1_up, ffn1_down, mix_norm, w_in, lb_param, hgrn_out_norm, conv_w, w_out, xattn_norm, mem_norm, w_q_mem, w_kv_mem, w_o_mem, ffn2_norm, ffn2_gate, ffn2_up, ffn2_down, final_norm, loss_target, m_ffn1_norm, m_ffn1_gate, m_ffn1_up, m_ffn1_down, m_mix_norm, m_w_in, m_lb_param, m_hgrn_out_norm, m_conv_w, m_w_out, m_xattn_norm, m_mem_norm, m_w_q_mem, m_w_kv_mem, m_w_o_mem, m_ffn2_norm, m_ffn2_gate, m_ffn2_up, m_ffn2_down, m_final_norm, v_ffn1_norm, v_ffn1_gate, v_ffn1_up, v_ffn1_down, v_mix_norm, v_w_in, v_lb_param, v_hgrn_out_norm, v_conv_w, v_w_out, v_xattn_norm, v_mem_norm, v_w_q_mem, v_w_kv_mem, v_w_o_mem, v_ffn2_norm, v_ffn2_gate, v_ffn2_up, v_ffn2_down, v_final_norm):
    given = dict(locals())
    me = 4 * lax.axis_index("x") + 2 * lax.axis_index("y") + lax.axis_index("c")
    x0, memv, target = x[0], mem[0], loss_target[0]

    def shard(prefix, name):
        v = given[prefix + name]
        if v.ndim == 1:
            return v.reshape(1, -1)
        if v.ndim == 2:
            return v
        return v[0].T if name in TRANSPOSED else v[0]

    w = {name: shard("", name) for name in WEIGHTS}
    m = {name: shard("m_", name) for name in WEIGHTS}
    v = {name: shard("v_", name) for name in WEIGHTS}

    conv_taps, conv_rows = w["conv_w"].shape
    conv_tile = jnp.pad(w["conv_w"], ((0, 8 - conv_taps), (0, 128 - conv_rows)))
    wire = {name: w[name].astype(BF16) for name in LARGE}
    full = {}

    def landed(names, gathered):
        for name, blocks in zip(names, gathered):
            _, r, c = blocks.shape
            full[name] = blocks if name == "w_kv_mem" else blocks.reshape(N_DEV * r, c)

    first = ("ffn1_gate", "ffn1_up")
    landed(first, _run_exchange(_gather_exchange([wire[k] for k in first]), "gather_first"))

    riders = (("ffn1_down", "w_in"), ("w_out", "w_kv_mem"), ("w_q_mem", "w_o_mem", "ffn2_gate", "ffn2_up"),
              ("ffn2_down",))
    (a1, b1, s1), gathered = _ffn_up(
        x0, w["ffn1_norm"], full["ffn1_gate"], full["ffn1_up"],
        exchange=_gather_exchange([wire[k] for k in riders[0]]))
    landed(riders[0], gathered)
    (x1,), gathered = _ffn_down(
        x0, s1, full["ffn1_down"], exchange=_gather_exchange([wire[k] for k in riders[1]] + [conv_tile]))
    landed(riders[1], gathered)
    convw_t = gathered[-1][:, :conv_taps, :conv_rows].transpose(1, 0, 2).reshape(conv_taps, N_DEV * conv_rows)
    (x2, z, o_raw, states, ycat), gathered = _mix_fwd(
        x1, w["mix_norm"], full["w_in"], w["lb_param"], w["hgrn_out_norm"], convw_t, full["w_out"],
        exchange=_gather_exchange([wire[k] for k in riders[2]]))
    landed(riders[2], gathered)
    kv = _memkv_fwd(memv, w["mem_norm"], full["w_kv_mem"])
    (x3, hq, qm, att), gathered = _xattn_fwd(
        x2, w["xattn_norm"], full["w_q_mem"], kv, full["w_o_mem"],
        exchange=_gather_exchange([wire[k] for k in riders[3]], middle_eighths=EARLY_MIDDLE_EIGHTHS))
    landed(riders[3], gathered)
    (dx4, a2, b2, s2, loss_part, d_final), _ = _ffn_fwd(
        x3, w["ffn2_norm"], full["ffn2_gate"], full["ffn2_up"], full["ffn2_down"], head=(w["final_norm"], target))

    parts = {}
    waiting = []

    def carried():
        names = [name for name, _ in waiting]
        exchange = _scatter_exchange([p for _, p in waiting]) if waiting else None
        del waiting[:]
        return names, exchange

    def received(names, arrived):
        for name, blocks in zip(names, arrived):
            group = name if isinstance(name, tuple) else (name,)
            rows = blocks.shape[1] // len(group)
            for p, member in enumerate(group):
                parts[member] = blocks[:, p * rows:(p + 1) * rows]

    def weight_grads(products):
        names, exchange = carried()
        partial, arrived = _weight_grad(list(products.values()), exchange=exchange)
        received(names, arrived)
        waiting.append((tuple(products), partial))

    def weight_grad(name, a, b, scale=1.0):
        weight_grads({name: (a, b, scale)})

    (dx3, da2, db2, h4, d_ffn2_norm), _ = _ffn_bwd(
        x3, w["ffn2_norm"], dx4, a2, b2, full["ffn2_gate"], full["ffn2_up"], full["ffn2_down"])
    weight_grad("ffn2_down", s2, dx4, 0.5)
    weight_grad("ffn2_gate", da2, h4)
    weight_grad("ffn2_up", db2, h4)
    names, exchange = carried()
    (dx2, dqm, dkv, d_xattn_norm), arrived = _xattn_bwd(
        x2, w["xattn_norm"], dx3, qm, kv, full["w_q_mem"], full["w_o_mem"], exchange=exchange)
    received(names, arrived)
    d_wkv, d_mem_norm = _memkv_bwd(memv, w["mem_norm"], dkv, full["w_kv_mem"])
    waiting.append(("w_kv_mem", d_wkv))
    names, exchange = carried()
    (dx1, dz, h2, d_mix_norm, d_lbp, d_gh, d_convw_t), arrived = _mix_bwd(
        x1, w["mix_norm"], dx2, z, o_raw, states, full["w_in"], w["lb_param"], w["hgrn_out_norm"], convw_t,
        full["w_out"], exchange=exchange)
    received(names, arrived)
    weight_grad("w_in", dz, h2)
    weight_grad("ffn1_down", s1, dx1, 0.5)
    (dx0, da1, db1, h1, d_ffn1_norm), _ = _ffn_bwd(
        x0, w["ffn1_norm"], dx1, a1, b1, full["ffn1_gate"], full["ffn1_up"], full["ffn1_down"])
    weight_grad("ffn1_gate", da1, h1)
    weight_grad("ffn1_up", db1, h1)
    weight_grads({"w_o_mem": (att, dx3, 1.0), "w_q_mem": (hq, dqm, 1.0), "w_out": (ycat, dx2, 1.0)})

    small_parts = {
        "ffn1_norm": d_ffn1_norm, "mix_norm": d_mix_norm, "xattn_norm": d_xattn_norm, "mem_norm": d_mem_norm,
        "ffn2_norm": d_ffn2_norm, "final_norm": d_final, "lb_param": d_lbp, "hgrn_out_norm": d_gh,
        "conv_w": d_convw_t, "loss": loss_part,
    }
    names = [name for name, _ in waiting]
    arrived, total = _final_exchange([p for _, p in waiting], small_parts)
    received(names, arrived)

    g_out, d_out, m_out, v_out = {}, {}, {}, {}
    for name in LARGE:
        if name not in SHORT_SHARDS:
            g_out[name], d_out[name], m_out[name], v_out[name] = _adamw_shard(parts[name], w[name], m[name], v[name])
    g_small = {name: parts[name] for name in SHORT_SHARDS}
    for name in SMALL:
        row, nrows, ncols = SMALL_LAYOUT[name]
        g_small[name] = total[row:row + nrows, 0:ncols]
    g_small["conv_w"] = lax.dynamic_slice_in_dim(g_small["conv_w"], me * conv_rows, conv_rows, axis=1)
    together = SMALL + SHORT_SHARDS
    gs, ds, ms, vs = _adamw_small(
        [g_small[k] for k in together], [w[k] for k in together], [m[k] for k in together],
        [v[k] for k in together])
    for i, name in enumerate(together):
        g_out[name], d_out[name], m_out[name], v_out[name] = gs[i], ds[i], ms[i], vs[i]

    def shaped(value, name):
        return (value.T if name in TRANSPOSED else value).reshape(given[name].shape)

    loss = total[SMALL_LAYOUT["loss"][0], 0]
    outs = [loss, dx0.reshape(x.shape)]
    for group in (g_out, d_out, m_out, v_out):
        outs += [shaped(group[name], name) for name in WEIGHTS]
    return tuple(outs)
```

```python
import jax
import jax.numpy as jnp
from jax import lax
from jax.experimental import pallas as pl
from jax.experimental.pallas import tpu as pltpu

F32 = jnp.float32
BF16 = jnp.bfloat16
MESH_IDS = pl.DeviceIdType.MESH

N_DEV = 8
EPS = 1e-6
HGRN_HEADS = 4
HGRN_DK = 128
HGRN_W = 512
CHUNK = 64
MEM_HEADS = 4
MEM_HD = 256
ADAM_LR = 0.001
ADAM_B1 = 0.9
ADAM_B2 = 0.999
ADAM_EPS = 1e-08
ADAM_WD = 0.01
ADAM_STEP = 10

TOKEN_TILE = 256
XATTN_TILE = 512
WIDE_TILE = 512
REDUCE_TILE = 1024
ADAMW_TILE_ELEMENTS = 256 * 1024
ADAMW_CARRIED_ROWS = 32
MIDDLE_EIGHTHS = 5
EARLY_MIDDLE_EIGHTHS = 4
MXU_ROWS = 256
VMEM_LIMIT = 60 * 1024 * 1024
SMALL_ROWS = 16
NT = (((1,), (1,)), ((), ()))
TN = (((0,), (0,)), ((), ()))


def _params(sem=None):
    return pltpu.CompilerParams(dimension_semantics=sem, vmem_limit_bytes=VMEM_LIMIT)


def _dot(a, b, dims=None):
    if dims is None:
        return jnp.dot(a, b, preferred_element_type=F32)
    return lax.dot_general(a, b, dims, preferred_element_type=F32)


def _sigmoid(v):
    return 1.0 / (1.0 + jnp.exp(-v))


def _rms(x, g):
    r = lax.rsqrt(jnp.mean(x * x, axis=-1, keepdims=True) + EPS)
    xh = x * r
    return xh * g, xh, r


def _rms_bwd(dh, xh, r, g):
    dxh = dh * g
    return r * (dxh - xh * jnp.mean(dxh * xh, axis=-1, keepdims=True))


def _full(shape):
    return pl.BlockSpec(shape, lambda *_: (0,) * len(shape))


def _full_once(shape):
    return pl.BlockSpec(shape, lambda *_: (0,) * len(shape), pipeline_mode=pl.Buffered(1))


def _rows(tm, width):
    return pl.BlockSpec((tm, width), lambda i: (i, 0))


def _rows_rev(tm, width, n):
    return pl.BlockSpec((tm, width), lambda i: (n - 1 - i, 0))


def _zero_at_start(*refs):
    @pl.when(pl.program_id(0) == 0)
    def _():
        for ref in refs:
            ref[...] = jnp.zeros_like(ref)


class _Exchange:
    def __init__(self, operands, out_shapes, scratch, start, finish, middle=None, middle_eighths=MIDDLE_EIGHTHS):
        self.operands, self.out_shapes, self.scratch = list(operands), list(out_shapes), list(scratch)
        self.start, self.middle, self.finish, self.middle_eighths = start, middle, finish, middle_eighths


def _call(body, *, name, grid, in_specs, out_specs, out_shape, args, scratch_shapes=(), exchange=None):
    semantics = ("arbitrary",) * len(grid)
    if exchange is None:
        out = pl.pallas_call(
            body, name=name, grid=grid, in_specs=in_specs, out_specs=out_specs, out_shape=out_shape,
            scratch_shapes=list(scratch_shapes), compiler_params=_params(semantics))(*args)
        return out, []
    hbm = pl.BlockSpec(memory_space=pltpu.HBM)
    n_in, n_out, n_scr = len(in_specs), len(out_specs), len(scratch_shapes)
    e_in, e_out = len(exchange.operands), len(exchange.out_shapes)

    def carried(*refs):
        ins, rest = refs[:n_in], refs[n_in:]
        e_ins, rest = rest[:e_in], rest[e_in:]
        outs, rest = rest[:n_out], rest[n_out:]
        e_outs, rest = rest[:e_out], rest[e_out:]
        scr, e_scr = rest[:n_scr], rest[n_scr:]
        first = last = None
        for axis, size in enumerate(grid):
            at_start, at_end = pl.program_id(axis) == 0, pl.program_id(axis) == size - 1
            first = at_start if first is None else jnp.logical_and(first, at_start)
            last = at_end if last is None else jnp.logical_and(last, at_end)

        @pl.when(first)
        def _():
            exchange.start(e_ins, e_outs, e_scr)

        body(*ins, *outs, *scr)

        if exchange.middle is not None:
            assert len(grid) == 1

            @pl.when(pl.program_id(0) == (grid[0] * exchange.middle_eighths) // 8)
            def _():
                exchange.middle(e_ins, e_outs, e_scr)

        @pl.when(last)
        def _():
            exchange.finish(e_ins, e_outs, e_scr)

    out = pl.pallas_call(
        carried, name=name, grid=grid, in_specs=list(in_specs) + [hbm] * e_in,
        out_specs=list(out_specs) + [hbm] * e_out, out_shape=list(out_shape) + exchange.out_shapes,
        scratch_shapes=list(scratch_shapes) + exchange.scratch,
        compiler_params=pltpu.CompilerParams(
            dimension_semantics=semantics, vmem_limit_bytes=VMEM_LIMIT, has_side_effects=True),
    )(*args, *exchange.operands)
    return out[:n_out], out[n_out:]


def _run_exchange(exchange, name):
    hbm = pl.BlockSpec(memory_space=pltpu.HBM)
    e_in, e_out = len(exchange.operands), len(exchange.out_shapes)

    def body(*refs):
        e_ins, e_outs, e_scr = refs[:e_in], refs[e_in:e_in + e_out], refs[e_in + e_out:]
        exchange.start(e_ins, e_outs, e_scr)
        if exchange.middle is not None:
            exchange.middle(e_ins, e_outs, e_scr)
        exchange.finish(e_ins, e_outs, e_scr)

    return pl.pallas_call(
        body, name=name, in_specs=[hbm] * e_in, out_specs=[hbm] * e_out, out_shape=exchange.out_shapes,
        scratch_shapes=exchange.scratch, compiler_params=pltpu.CompilerParams(has_side_effects=True),
    )(*exchange.operands)


def _loss_head(xo, gf, tgt):
    d = xo.shape[1]
    y, xh, r = _rms(xo, gf)
    err = y - tgt
    dy = err * (1.0 / d)
    loss = 0.5 * jnp.sum(jnp.sum(err * err, axis=-1, keepdims=True) * (1.0 / d), axis=0, keepdims=True)
    return _rms_bwd(dy, xh, r, gf), loss, jnp.sum(dy * xh, axis=0, keepdims=True)


def _ffn_fwd(x, g, wg, wu, wd, exchange=None, head=None):
    t, d = x.shape
    f = wg.shape[0]
    tm = min(WIDE_TILE, t)

    def body(x_ref, g_ref, wg_ref, wu_ref, wd_ref, *rest):
        if head is None:
            xo_ref, a_ref, b_ref, s_ref = rest
        else:
            gf_ref, tgt_ref, xo_ref, a_ref, b_ref, s_ref, loss_ref, dgf_ref = rest
            _zero_at_start(loss_ref, dgf_ref)
        xv = x_ref[...]
        h, _, _ = _rms(xv, g_ref[...])
        hb = h.astype(BF16)
        a = _dot(hb, wg_ref[...], NT)
        b = _dot(hb, wu_ref[...], NT)
        s = (a * _sigmoid(a) * b).astype(BF16)
        xo = xv + 0.5 * _dot(s, wd_ref[...])
        if head is None:
            xo_ref[...] = xo
        else:
            xo_ref[...], loss, dgf = _loss_head(xo, gf_ref[...], tgt_ref[...])
            loss_ref[...] += jnp.broadcast_to(loss, (1, 128))
            dgf_ref[...] += dgf
        a_ref[...] = a.astype(BF16)
        b_ref[...] = b.astype(BF16)
        s_ref[...] = s

    in_specs = [_rows(tm, d), _full((1, d)), _full_once((f, d)), _full_once((f, d)), _full_once((f, d))]
    out_specs = [_rows(tm, d), _rows(tm, f), _rows(tm, f), _rows(tm, f)]
    out_shape = [
        jax.ShapeDtypeStruct((t, d), F32),
        jax.ShapeDtypeStruct((t, f), BF16),
        jax.ShapeDtypeStruct((t, f), BF16),
        jax.ShapeDtypeStruct((t, f), BF16),
    ]
    args = (x, g, wg, wu, wd)
    if head is not None:
        in_specs += [_full((1, d)), _rows(tm, d)]
        out_specs += [_full((1, 128)), _full((1, d))]
        out_shape += [jax.ShapeDtypeStruct((1, 128), F32), jax.ShapeDtypeStruct((1, d), F32)]
        args += tuple(head)
    return _call(
        body, name="ffn_fwd", grid=(t // tm,), in_specs=in_specs, out_specs=out_specs, out_shape=out_shape,
        args=args, exchange=exchange)


def _ffn_up(x, g, wg, wu, exchange=None):
    t, d = x.shape
    f = wg.shape[0]
    tm = min(TOKEN_TILE, t)

    def body(x_ref, g_ref, wg_ref, wu_ref, a_ref, b_ref, s_ref):
        h, _, _ = _rms(x_ref[...], g_ref[...])
        hb = h.astype(BF16)
        a = _dot(hb, wg_ref[...], NT)
        b = _dot(hb, wu_ref[...], NT)
        a_ref[...] = a.astype(BF16)
        b_ref[...] = b.astype(BF16)
        s_ref[...] = (a * _sigmoid(a) * b).astype(BF16)

    return _call(
        body, name="ffn_up", grid=(t // tm,),
        in_specs=[_rows(tm, d), _full((1, d)), _full_once((f, d)), _full_once((f, d))],
        out_specs=[_rows(tm, f)] * 3, out_shape=[jax.ShapeDtypeStruct((t, f), BF16)] * 3,
        args=(x, g, wg, wu), exchange=exchange)


def _ffn_down(x, s, wd, exchange=None):
    t, d = x.shape
    f = wd.shape[0]
    tm = min(TOKEN_TILE, t)

    def body(x_ref, s_ref, wd_ref, xo_ref):
        xo_ref[...] = x_ref[...] + 0.5 * _dot(s_ref[...], wd_ref[...])

    return _call(
        body, name="ffn_down", grid=(t // tm,),
        in_specs=[_rows(tm, d), _rows(tm, f), _full_once((f, d))],
        out_specs=[_rows(tm, d)], out_shape=[jax.ShapeDtypeStruct((t, d), F32)],
        args=(x, s, wd), exchange=exchange)


def _ffn_bwd(x, g, dxo, a, b, wg, wu, wd, exchange=None):
    t, d = x.shape
    f = wg.shape[0]
    tm = min(TOKEN_TILE, t)

    def body(x_ref, g_ref, dxo_ref, a_ref, b_ref, wg_ref, wu_ref, wd_ref, dx_ref, da_ref, db_ref, h_ref, dg_ref):
        _zero_at_start(dg_ref)
        gv = g_ref[...]
        h, xh, r = _rms(x_ref[...], gv)
        dxo = dxo_ref[...]
        ds = _dot((0.5 * dxo).astype(BF16), wd_ref[...], NT)
        af = a_ref[...].astype(F32)
        bf = b_ref[...].astype(F32)
        sg = _sigmoid(af)
        da = (ds * bf * (sg * (1.0 + af * (1.0 - sg)))).astype(BF16)
        db = (ds * (af * sg)).astype(BF16)
        dh = _dot(da, wg_ref[...]) + _dot(db, wu_ref[...])
        dx_ref[...] = _rms_bwd(dh, xh, r, gv) + dxo
        da_ref[...] = da
        db_ref[...] = db
        h_ref[...] = h.astype(BF16)
        dg_ref[...] += jnp.sum(dh * xh, axis=0, keepdims=True)

    return _call(
        body,
        name="ffn_bwd",
        grid=(t // tm,),
        in_specs=[
            _rows(tm, d), _full((1, d)), _rows(tm, d), _rows(tm, f), _rows(tm, f),
            _full_once((f, d)), _full_once((f, d)), _full_once((f, d)),
        ],
        out_specs=[_rows(tm, d), _rows(tm, f), _rows(tm, f), _rows(tm, d), _full((1, d))],
        out_shape=[
            jax.ShapeDtypeStruct((t, d), F32),
            jax.ShapeDtypeStruct((t, f), BF16),
            jax.ShapeDtypeStruct((t, f), BF16),
            jax.ShapeDtypeStruct((t, d), BF16),
            jax.ShapeDtypeStruct((1, d), F32),
        ],
        args=(x, g, dxo, a, b, wg, wu, wd),
        exchange=exchange,
    )


def _weight_grad(products, exchange=None):
    count = len(products)
    t, m = products[0][0].shape
    n = products[0][1].shape[1]
    assert all(a.shape == (t, m) and b.shape == (t, n) for a, b, _ in products)
    chips = N_DEV // 2
    r = m // N_DEV
    tk = min(REDUCE_TILE, t)
    halves = 2
    nb = n // halves
    nk = t // tk

    def body(*refs):
        a_refs, b_refs, o_ref = refs[0:count], refs[count:2 * count], refs[2 * count]
        acc, send_buf, recv_buf, send_sems, recv_sems = refs[2 * count + 1:]
        k, j = pl.program_id(0), pl.program_id(1)
        x, y, c, _ = _mesh_place()
        sibling, _ = _peer(x, y, c, 1)
        for p, (_, _, scale) in enumerate(products):
            bv = b_refs[p][...]
            if scale != 1.0:
                bv = bv * scale
            bb = bv.astype(BF16)
            acc_half = acc.at[p, j]

            @pl.when(k == 0)
            def _():
                acc_half[...] = jnp.zeros_like(acc_half)

            for i in range(m // MXU_ROWS):
                rows = slice(i * MXU_ROWS, (i + 1) * MXU_ROWS)
                acc_half[rows, :] += _dot(a_refs[p][:, rows].astype(BF16), bb, TN)

        def to_sibling(half):
            return _remote(send_buf.at[half], recv_buf.at[half], send_sems.at[half], recv_sems.at[half], sibling)

        def owned_rows(q, core):
            return pl.ds(pl.multiple_of((2 * q + core) * r, 8), r)

        for half in range(halves):
            @pl.when(jnp.logical_and(k == nk - 1, j == half))
            def _():
                for p in range(count):
                    for q in range(chips):
                        send_buf[half, p, q] = acc[p, half, owned_rows(q, 1 - c), :].astype(BF16)
                to_sibling(half).start()

        @pl.when(jnp.logical_and(k == nk - 1, j == halves - 1))
        def _():
            for half in range(halves):
                to_sibling(half).wait_send()
                to_sibling(half).wait_recv()
                for p in range(count):
                    for q in range(chips):
                        o_ref[q, p * r:(p + 1) * r, half * nb:(half + 1) * nb] = (
                            acc[p, half, owned_rows(q, c), :] + recv_buf[half, p, q].astype(F32)).astype(BF16)

    (partial,), arrived = _call(
        body,
        name="weight_grad",
        grid=(nk, halves),
        in_specs=[pl.BlockSpec((tk, m), lambda k, j: (k, 0))] * count
        + [pl.BlockSpec((tk, nb), lambda k, j: (k, j))] * count,
        out_specs=[pl.BlockSpec((chips, count * r, n), lambda k, j: (0, 0, 0))],
        out_shape=[jax.ShapeDtypeStruct((chips, count * r, n), BF16)],
        scratch_shapes=[
            pltpu.VMEM((count, halves, m, nb), F32),
            pltpu.VMEM((halves, count, chips, r, nb), BF16), pltpu.VMEM((halves, count, chips, r, nb), BF16),
            pltpu.SemaphoreType.DMA((halves,)), pltpu.SemaphoreType.DMA((halves,)),
        ],
        args=tuple(a for a, _, _ in products) + tuple(b for _, b, _ in products),
        exchange=exchange,
    )
    return partial, arrived


def _chunk_cumsum(v, reverse=False):
    n, width = v.shape
    row = lax.broadcasted_iota(jnp.int32, (n, n), 0)
    col = lax.broadcasted_iota(jnp.int32, (n, n), 1)
    earlier = col >= row if reverse else col <= row
    tri = jnp.where(jnp.logical_and(row // CHUNK == col // CHUNK, earlier), 1.0, 0.0).astype(BF16)
    hi = v.astype(BF16)
    rest = v - hi.astype(F32)
    mid = rest.astype(BF16)
    low = (rest - mid.astype(F32)).astype(BF16)
    sums = _dot(tri, jnp.concatenate([hi, mid, low], axis=1))
    return sums[:, 0:width] + sums[:, width:2 * width] + sums[:, 2 * width:3 * width]


def _shift_rows(v, shift, edge):
    n = v.shape[0]
    row = lax.broadcasted_iota(jnp.int32, (n, 1), 0)
    out = pltpu.roll(v, shift % n, axis=0)
    if shift > 0:
        for j in range(shift):
            out = jnp.where(row == j, edge[8 - shift + j:8 - shift + j + 1, :], out)
    else:
        for j in range(-shift):
            out = jnp.where(row == n + shift + j, edge[j:j + 1, :], out)
    return out


def _gates(z, lbp):
    w = HGRN_W
    lb = _sigmoid(lbp[0:1, :] - lbp[1:2, :])
    zq = z[:, 0:w]
    sig = _sigmoid(z[:, w:2 * w])
    f = lb + (1.0 - lb) * sig
    sq = _sigmoid(zq)
    q = zq * sq * HGRN_DK ** -0.5
    return lb, sig, f, sq, q


def _decayed_operands(q, f, v, qm_buf, km_buf, kbar_buf, v_buf, etot_buf, emid_buf):
    n, width = f.shape
    bcum = _chunk_cumsum(jnp.log(f))

    def row_of_chunk(offset):
        return jnp.concatenate(
            [jnp.broadcast_to(bcum[c + offset:c + offset + 1, :], (CHUNK, width)) for c in range(0, n, CHUNK)], axis=0)

    total, mid = row_of_chunk(CHUNK - 1), row_of_chunk(CHUNK // 2 - 1)
    em, enm, erest = jnp.exp(bcum - mid), jnp.exp(mid - bcum), jnp.exp(total - bcum)
    kk = 1.0 - f
    qm_buf[...] = (q * em).astype(BF16)
    km_buf[...] = (kk * enm).astype(BF16)
    kbar_buf[...] = (kk * erest).astype(BF16)
    v_buf[...] = v.astype(BF16)
    etot_buf[...] = jnp.exp(total)
    emid_buf[...] = jnp.exp(mid)
    return em, enm, erest


def _short_conv(u, edge, cw):
    return cw[0:1, :] * _shift_rows(u, 2, edge) + cw[1:2, :] * _shift_rows(u, 1, edge) + cw[2:3, :] * u


def _block_causal_mask(n):
    row = lax.broadcasted_iota(jnp.int32, (n, n), 0)
    col = lax.broadcasted_iota(jnp.int32, (n, n), 1)
    return jnp.logical_and(row // CHUNK == col // CHUNK, col <= row)


def _spread(v, chunk_of_row, nc):
    return jnp.concatenate([jnp.where(chunk_of_row == c, v, jnp.zeros_like(v)) for c in range(nc)], axis=1)


def _pick(r, chunk_of_row, nc):
    out = jnp.where(chunk_of_row == 0, r[:, 0:HGRN_DK], 0.0)
    for c in range(1, nc):
        out = out + jnp.where(chunk_of_row == c, r[:, c * HGRN_DK:(c + 1) * HGRN_DK], 0.0)
    return out


def _mix_fwd(x, g, w_in, lbp, gh, convw_t, w_out, exchange=None):
    t, d = x.shape
    zw = w_in.shape[0]
    w = HGRN_W
    tm = min(TOKEN_TILE, t)
    nc = tm // CHUNK
    n_chunks = t // CHUNK

    def body(x_ref, g_ref, win_ref, lbp_ref, gh_ref, cw_ref, wout_ref,
             xo_ref, z_ref, o_ref, st_ref, y_ref, state, ucarry, qm_buf, km_buf, kbar_buf, v_buf, etot_buf, emid_buf):
        _zero_at_start(state, ucarry)
        xv = x_ref[...]
        h, _, _ = _rms(xv, g_ref[...])
        z_ref[...] = _dot(h.astype(BF16), win_ref[...], NT)
        z = z_ref[...]
        _, _, f, _, q = _gates(z, lbp_ref[...])
        _decayed_operands(q, f, z[:, 2 * w:3 * w], qm_buf, km_buf, kbar_buf, v_buf, etot_buf, emid_buf)
        mask = _block_causal_mask(tm)
        chunk_of_row = lax.broadcasted_iota(jnp.int32, (tm, 1), 0) // CHUNK
        heads = range(HGRN_HEADS)
        hcols = [slice(hd * HGRN_DK, (hd + 1) * HGRN_DK) for hd in heads]
        qm = [qm_buf[:, hcols[hd]] for hd in heads]
        vb = [v_buf[:, hcols[hd]] for hd in heads]
        scores = [jnp.where(mask, _dot(qm[hd], km_buf[:, hcols[hd]], NT), 0.0).astype(BF16) for hd in heads]
        gains = [_dot(_spread(vb[hd], chunk_of_row, nc), kbar_buf[:, hcols[hd]], TN) for hd in heads]
        entering = []
        for hd in heads:
            states, st = [], state[hd]
            for c in range(nc):
                first_row = slice(c * CHUNK, c * CHUNK + 1)
                states.append(st * emid_buf[first_row, hcols[hd]])
                st_ref[c, hd] = st
                st = st * etot_buf[first_row, hcols[hd]] + gains[hd][c * HGRN_DK:(c + 1) * HGRN_DK, :]
            state[hd] = st
            entering.append(jnp.concatenate(states, axis=0).astype(BF16))
        from_states = [_dot(qm[hd], entering[hd], NT) for hd in heads]
        o_heads = [_dot(scores[hd], vb[hd]) + _pick(from_states[hd], chunk_of_row, nc) for hd in heads]
        o_ref[...] = jnp.concatenate(o_heads, axis=1)
        ghv = gh_ref[...]
        normed = jnp.concatenate([_rms(o_heads[hd], ghv[:, hcols[hd]])[0] for hd in heads], axis=1)
        zg = z[:, 3 * w:4 * w]
        u = z[:, 5 * w:6 * w] * z[:, 6 * w:7 * w]
        conv = _short_conv(u, ucarry[...], cw_ref[...])
        ucarry[...] = u[tm - 8:tm, :]
        y = jnp.concatenate([normed * (zg * _sigmoid(zg)), z[:, 4 * w:5 * w] * conv], axis=1).astype(BF16)
        y_ref[...] = y
        xo_ref[...] = xv + _dot(y, wout_ref[...])

    return _call(
        body,
        name="mix_fwd",
        grid=(t // tm,),
        in_specs=[
            _rows(tm, d), _full((1, d)), _full((zw, d)), _full((2, w)), _full((1, w)), _full((3, w)),
            _full((2 * w, d)),
        ],
        out_specs=[
            _rows(tm, d), _rows(tm, zw), _rows(tm, w),
            pl.BlockSpec((nc, HGRN_HEADS, HGRN_DK, HGRN_DK), lambda i: (i, 0, 0, 0)),
            _rows(tm, 2 * w),
        ],
        out_shape=[
            jax.ShapeDtypeStruct((t, d), F32),
            jax.ShapeDtypeStruct((t, zw), F32),
            jax.ShapeDtypeStruct((t, w), F32),
            jax.ShapeDtypeStruct((n_chunks, HGRN_HEADS, HGRN_DK, HGRN_DK), F32),
            jax.ShapeDtypeStruct((t, 2 * w), BF16),
        ],
        scratch_shapes=[
            pltpu.VMEM((HGRN_HEADS, HGRN_DK, HGRN_DK), F32), pltpu.VMEM((8, w), F32),
            pltpu.VMEM((tm, w), BF16), pltpu.VMEM((tm, w), BF16), pltpu.VMEM((tm, w), BF16),
            pltpu.VMEM((tm, w), BF16), pltpu.VMEM((tm, w), F32), pltpu.VMEM((tm, w), F32),
        ],
        args=(x, g, w_in, lbp, gh, convw_t, w_out),
        exchange=exchange,
    )


def _mix_bwd(x, g, dxo, z, o, states, w_in, lbp, gh, convw_t, w_out, exchange=None):
    t, d = x.shape
    zw = w_in.shape[0]
    w = HGRN_W
    tm = min(TOKEN_TILE, t)
    nc = tm // CHUNK
    n = t // tm

    def body(x_ref, g_ref, dxo_ref, z_ref, zprev_ref, o_ref, st_ref, win_ref, lbp_ref, gh_ref, cw_ref, wout_ref,
             dx_ref, dz_ref, h_ref, dg_ref, dlbp_ref, dgh_ref, dcw_ref,
             dstate, dcarry, do_buf, qm_buf, km_buf, kbar_buf, v_buf, etot_buf, emid_buf):
        _zero_at_start(dstate, dcarry, dg_ref, dlbp_ref, dgh_ref, dcw_ref)
        gv = g_ref[...]
        h, xh, r = _rms(x_ref[...], gv)
        h_ref[...] = h.astype(BF16)
        dxo = dxo_ref[...]
        dy = _dot(dxo.astype(BF16), wout_ref[...], NT)
        z = z_ref[...]
        lb, sig, f, sq, q = _gates(z, lbp_ref[...])
        em, enm, erest = _decayed_operands(
            q, f, z[:, 2 * w:3 * w], qm_buf, km_buf, kbar_buf, v_buf, etot_buf, emid_buf)

        ghv = gh_ref[...]
        zg = z[:, 3 * w:4 * w]
        sgz = _sigmoid(zg)
        dyh = dy[:, 0:w]
        don = dyh * (zg * sgz)
        heads = range(HGRN_HEADS)
        hcols = [slice(hd * HGRN_DK, (hd + 1) * HGRN_DK) for hd in heads]
        norms = [_rms(o_ref[:, hcols[hd]], ghv[:, hcols[hd]]) for hd in heads]
        on = jnp.concatenate([norms[hd][0] for hd in heads], axis=1)
        oh = jnp.concatenate([norms[hd][1] for hd in heads], axis=1)
        dz_ref[:, 3 * w:4 * w] = (dyh * on * (sgz * (1.0 + zg * (1.0 - sgz)))).astype(BF16)
        dgh_ref[...] += jnp.sum(don * oh, axis=0, keepdims=True)
        do_buf[...] = jnp.concatenate(
            [_rms_bwd(don[:, hcols[hd]], norms[hd][1], norms[hd][2], ghv[:, hcols[hd]]) for hd in heads],
            axis=1).astype(BF16)

        zb = z[:, 4 * w:5 * w]
        zc = z[:, 5 * w:6 * w]
        zu = z[:, 6 * w:7 * w]
        u = zc * zu
        cw = cw_ref[...]
        zp = zprev_ref[...]
        uprev = jnp.where(pl.program_id(0) == n - 1, 0.0, zp[:, 5 * w:6 * w] * zp[:, 6 * w:7 * w])
        dyc = dy[:, w:2 * w]
        dz_ref[:, 4 * w:5 * w] = (dyc * _short_conv(u, uprev, cw)).astype(BF16)
        dconv = dyc * zb
        edge = dcarry[...]
        dconv1 = _shift_rows(dconv, -1, edge)
        dconv2 = _shift_rows(dconv, -2, edge)
        dcarry[...] = dconv[0:8, :]
        du = cw[2:3, :] * dconv + cw[1:2, :] * dconv1 + cw[0:1, :] * dconv2
        dz_ref[:, 5 * w:6 * w] = (du * zu).astype(BF16)
        dz_ref[:, 6 * w:7 * w] = (du * zc).astype(BF16)
        dcw_ref[...] += jnp.concatenate([
            jnp.sum(u * dconv2, axis=0, keepdims=True),
            jnp.sum(u * dconv1, axis=0, keepdims=True),
            jnp.sum(u * dconv, axis=0, keepdims=True)], axis=0)

        mask = _block_causal_mask(tm)
        chunk_of_row = lax.broadcasted_iota(jnp.int32, (tm, 1), 0) // CHUNK
        heads = range(HGRN_HEADS)
        hcols = [slice(hd * HGRN_DK, (hd + 1) * HGRN_DK) for hd in heads]
        qmb = [qm_buf[:, hcols[hd]] for hd in heads]
        kmb = [km_buf[:, hcols[hd]] for hd in heads]
        vb = [v_buf[:, hcols[hd]] for hd in heads]
        dob = [do_buf[:, hcols[hd]] for hd in heads]
        scores = [jnp.where(mask, _dot(qmb[hd], kmb[hd], NT), 0.0).astype(BF16) for hd in heads]
        dscores = [jnp.where(mask, _dot(dob[hd], vb[hd], NT), 0.0).astype(BF16) for hd in heads]
        gains = [_dot(_spread(dob[hd], chunk_of_row, nc), qmb[hd], TN) for hd in heads]
        dst_rows, dst_lanes, st_lanes, carries = [], [], [], []
        for hd in heads:
            entering = [st_ref[c, hd] for c in range(nc)]
            emid = [emid_buf[c * CHUNK:c * CHUNK + 1, hcols[hd]] for c in range(nc)]
            leaving, carried_back = [None] * nc, [None] * nc
            dst = dstate[hd]
            for c in reversed(range(nc)):
                elast = etot_buf[c * CHUNK:c * CHUNK + 1, hcols[hd]]
                leaving[c] = dst
                carried_back[c] = jnp.sum(dst * entering[c], axis=0, keepdims=True) * elast
                dst = dst * elast + gains[hd][c * HGRN_DK:(c + 1) * HGRN_DK, :] * emid[c]
            dstate[hd] = dst
            dst_rows.append(jnp.concatenate(leaving, axis=0).astype(BF16))
            dst_lanes.append(jnp.concatenate(leaving, axis=1).astype(BF16))
            st_lanes.append(jnp.concatenate([entering[c] * emid[c] for c in range(nc)], axis=1).astype(BF16))
            carries.append(carried_back)
        dv = [_dot(scores[hd], dob[hd], TN) + _pick(_dot(kbar_buf[:, hcols[hd]], dst_rows[hd], NT), chunk_of_row, nc)
              for hd in heads]
        dz_ref[:, 2 * w:3 * w] = jnp.concatenate(dv, axis=1).astype(BF16)
        dqm = jnp.concatenate([_dot(dscores[hd], kmb[hd]) + _pick(_dot(dob[hd], st_lanes[hd]), chunk_of_row, nc)
                               for hd in heads], axis=1)
        dkm = jnp.concatenate([_dot(dscores[hd], qmb[hd], TN) for hd in heads], axis=1)
        dkbar = jnp.concatenate([_pick(_dot(vb[hd], dst_lanes[hd]), chunk_of_row, nc) for hd in heads], axis=1)

        kbar_dkbar = kbar_buf[...].astype(F32) * dkbar
        db = qm_buf[...].astype(F32) * dqm - km_buf[...].astype(F32) * dkm - kbar_dkbar
        through_last = jnp.concatenate([
            jnp.broadcast_to(
                jnp.sum(kbar_dkbar[c * CHUNK:(c + 1) * CHUNK], axis=0, keepdims=True)
                + jnp.concatenate([carries[hd][c] for hd in heads], axis=1),
                (CHUNK, w))
            for c in range(nc)], axis=0)
        dlogf = _chunk_cumsum(db, reverse=True) + through_last
        df = dlogf / f - (dkm * enm + dkbar * erest)
        zq = z[:, 0:w]
        dz_ref[:, 0:w] = (dqm * em * HGRN_DK ** -0.5 * (sq * (1.0 + zq * (1.0 - sq)))).astype(BF16)
        dz_ref[:, w:2 * w] = (df * (1.0 - lb) * sig * (1.0 - sig)).astype(BF16)
        dlb = jnp.sum(df * (1.0 - sig), axis=0, keepdims=True) * lb * (1.0 - lb)
        dlbp_ref[...] += jnp.concatenate([dlb, -dlb], axis=0)

        dh = _dot(dz_ref[...], win_ref[...])
        dx_ref[...] = _rms_bwd(dh, xh, r, gv) + dxo
        dg_ref[...] += jnp.sum(dh * xh, axis=0, keepdims=True)

    return _call(
        body,
        name="mix_bwd",
        grid=(n,),
        in_specs=[
            _rows_rev(tm, d, n), _full((1, d)), _rows_rev(tm, d, n), _rows_rev(tm, zw, n),
            pl.BlockSpec((8, zw), lambda i: (jnp.maximum((n - 1 - i) * (tm // 8) - 1, 0), 0)),
            _rows_rev(tm, w, n),
            pl.BlockSpec((nc, HGRN_HEADS, HGRN_DK, HGRN_DK), lambda i: (n - 1 - i, 0, 0, 0)),
            _full((zw, d)), _full((2, w)), _full((1, w)), _full((3, w)), _full((2 * w, d)),
        ],
        out_specs=[
            _rows_rev(tm, d, n), _rows_rev(tm, zw, n), _rows_rev(tm, d, n),
            _full((1, d)), _full((2, w)), _full((1, w)), _full((3, w)),
        ],
        out_shape=[
            jax.ShapeDtypeStruct((t, d), F32),
            jax.ShapeDtypeStruct((t, zw), BF16),
            jax.ShapeDtypeStruct((t, d), BF16),
            jax.ShapeDtypeStruct((1, d), F32),
            jax.ShapeDtypeStruct((2, w), F32),
            jax.ShapeDtypeStruct((1, w), F32),
            jax.ShapeDtypeStruct((3, w), F32),
        ],
        scratch_shapes=[
            pltpu.VMEM((HGRN_HEADS, HGRN_DK, HGRN_DK), F32), pltpu.VMEM((8, w), F32),
            pltpu.VMEM((tm, w), BF16),
            pltpu.VMEM((tm, w), BF16), pltpu.VMEM((tm, w), BF16), pltpu.VMEM((tm, w), BF16),
            pltpu.VMEM((tm, w), BF16), pltpu.VMEM((tm, w), F32), pltpu.VMEM((tm, w), F32),
        ],
        args=(x, g, dxo, z, z, o, states, w_in, lbp, gh, convw_t, w_out),
        exchange=exchange,
    )


def _memkv_fwd(mem, g, wkv):
    m, d = mem.shape
    nb, _, cb = wkv.shape

    def body(mem_ref, g_ref, wkv_ref, kv_ref):
        mn, _, _ = _rms(mem_ref[...], g_ref[...])
        mnb = mn.astype(BF16)
        for j in range(nb):
            kv_ref[:, j * cb:(j + 1) * cb] = _dot(mnb, wkv_ref[j]).astype(BF16)

    return pl.pallas_call(
        body,
        name="memkv_fwd",
        out_shape=jax.ShapeDtypeStruct((m, nb * cb), BF16),
        compiler_params=_params(),
    )(mem, g, wkv)


def _memkv_bwd(mem, g, dkv, wkv):
    m, d = mem.shape
    nb, _, cb = wkv.shape
    chips = nb // 2

    def body(mem_ref, g_ref, dkv_ref, wkv_ref, dw_ref, dg_ref, dw_all, send_buf, recv_buf, send_sem, recv_sem):
        x, y, c, _ = _mesh_place()
        sibling, _ = _peer(x, y, c, 1)
        mn, xh, _ = _rms(mem_ref[...], g_ref[...])
        mnb = mn.astype(BF16)
        dmn = jnp.zeros((m, d), F32)
        for j in range(nb):
            dkvb = dkv_ref[:, j * cb:(j + 1) * cb].astype(BF16)
            dw_all[j] = _dot(mnb, dkvb, TN)
            dmn = dmn + _dot(dkvb, wkv_ref[j], NT)
        dg_ref[...] = jnp.sum(dmn * xh, axis=0, keepdims=True)
        for q in range(chips):
            send_buf[q] = dw_all[2 * q + 1 - c].astype(BF16)
        to_sibling = _remote(send_buf, recv_buf, send_sem, recv_sem, sibling)
        to_sibling.start()
        to_sibling.wait_send()
        to_sibling.wait_recv()
        for q in range(chips):
            dw_ref[q] = (dw_all[2 * q + c] + recv_buf[q].astype(F32)).astype(BF16)

    return pl.pallas_call(
        body,
        name="memkv_bwd",
        out_shape=[jax.ShapeDtypeStruct((chips, d, cb), BF16), jax.ShapeDtypeStruct((1, d), F32)],
        scratch_shapes=[
            pltpu.VMEM((nb, d, cb), F32), pltpu.VMEM((chips, d, cb), BF16), pltpu.VMEM((chips, d, cb), BF16),
            pltpu.SemaphoreType.DMA, pltpu.SemaphoreType.DMA,
        ],
        compiler_params=_params(),
    )(mem, g, dkv, wkv)


def _softmax_rows(qm_h, k_h):
    sc = _dot(qm_h, k_h, NT) * MEM_HD ** -0.5
    e = jnp.exp(sc - jnp.max(sc, axis=-1, keepdims=True))
    return e / jnp.sum(e, axis=-1, keepdims=True)


def _xattn_fwd(x, g, wq, kv, wo, exchange=None):
    t, d = x.shape
    m = kv.shape[0]
    tm = min(XATTN_TILE, t)

    def body(x_ref, g_ref, wq_ref, kv_ref, wo_ref, xo_ref, hq_ref, qm_ref, att_ref):
        xv = x_ref[...]
        h, _, _ = _rms(xv, g_ref[...])
        hb = h.astype(BF16)
        hq_ref[...] = hb
        qm = _dot(hb, wq_ref[...]).astype(BF16)
        qm_ref[...] = qm
        heads = range(MEM_HEADS)
        kcols = [slice(hd * MEM_HD, (hd + 1) * MEM_HD) for hd in heads]
        p = [_softmax_rows(qm[:, kcols[hd]], kv_ref[:, kcols[hd]]) for hd in heads]
        att = jnp.concatenate(
            [_dot(p[hd].astype(BF16), kv_ref[:, d + hd * MEM_HD:d + (hd + 1) * MEM_HD]) for hd in heads],
            axis=1).astype(BF16)
        att_ref[...] = att
        xo_ref[...] = xv + _dot(att, wo_ref[...])

    return _call(
        body,
        name="xattn_fwd",
        grid=(t // tm,),
        in_specs=[_rows(tm, d), _full((1, d)), _full((d, d)), _full((m, 2 * d)), _full((d, d))],
        out_specs=[_rows(tm, d), _rows(tm, d), _rows(tm, d), _rows(tm, d)],
        out_shape=[
            jax.ShapeDtypeStruct((t, d), F32),
            jax.ShapeDtypeStruct((t, d), BF16),
            jax.ShapeDtypeStruct((t, d), BF16),
            jax.ShapeDtypeStruct((t, d), BF16),
        ],
        args=(x, g, wq, kv, wo),
        exchange=exchange,
    )


def _xattn_bwd(x, g, dxo, qm, kv, wq, wo, exchange=None):
    t, d = x.shape
    m = kv.shape[0]
    tm = min(XATTN_TILE, t)

    def body(x_ref, g_ref, dxo_ref, qm_ref, kv_ref, wq_ref, wo_ref, dx_ref, dqm_ref, dkv_ref, dg_ref):
        _zero_at_start(dkv_ref, dg_ref)
        gv = g_ref[...]
        _, xh, r = _rms(x_ref[...], gv)
        dxo = dxo_ref[...]
        datt = _dot(dxo.astype(BF16), wo_ref[...], NT).astype(BF16)
        heads = range(MEM_HEADS)
        kcols = [slice(hd * MEM_HD, (hd + 1) * MEM_HD) for hd in heads]
        vcols = [slice(d + hd * MEM_HD, d + (hd + 1) * MEM_HD) for hd in heads]
        qm_h = [qm_ref[:, kcols[hd]] for hd in heads]
        p = [_softmax_rows(qm_h[hd], kv_ref[:, kcols[hd]]) for hd in heads]
        dp = [_dot(datt[:, kcols[hd]], kv_ref[:, vcols[hd]], NT) for hd in heads]
        dsc = [(p[hd] * (dp[hd] - jnp.sum(p[hd] * dp[hd], axis=-1, keepdims=True)) * MEM_HD ** -0.5).astype(BF16)
               for hd in heads]
        dqm = jnp.concatenate([_dot(dsc[hd], kv_ref[:, kcols[hd]]) for hd in heads], axis=1).astype(BF16)
        dqm_ref[...] = dqm
        dkv_ref[...] += jnp.concatenate(
            [_dot(dsc[hd], qm_h[hd], TN) for hd in heads]
            + [_dot(p[hd].astype(BF16), datt[:, kcols[hd]], TN) for hd in heads], axis=1)
        dh = _dot(dqm, wq_ref[...], NT)
        dx_ref[...] = _rms_bwd(dh, xh, r, gv) + dxo
        dg_ref[...] += jnp.sum(dh * xh, axis=0, keepdims=True)

    return _call(
        body,
        name="xattn_bwd",
        grid=(t // tm,),
        in_specs=[
            _rows(tm, d), _full((1, d)), _rows(tm, d), _rows(tm, d), _full((m, 2 * d)), _full((d, d)), _full((d, d)),
        ],
        out_specs=[_rows(tm, d), _rows(tm, d), _full((m, 2 * d)), _full((1, d))],
        out_shape=[
            jax.ShapeDtypeStruct((t, d), F32),
            jax.ShapeDtypeStruct((t, d), BF16),
            jax.ShapeDtypeStruct((m, 2 * d), F32),
            jax.ShapeDtypeStruct((1, d), F32),
        ],
        args=(x, g, dxo, qm, kv, wq, wo),
        exchange=exchange,
    )


def _mesh_place():
    x, y, c = lax.axis_index("x"), lax.axis_index("y"), lax.axis_index("c")
    return x, y, c, 4 * x + 2 * y + c


def _peer(x, y, c, k):
    px = 1 - x if k & 4 else x
    py = 1 - y if k & 2 else y
    pc = 1 - c if k & 1 else c
    return (px, py, pc), 4 * px + 2 * py + pc


ICI_HOPS = (2, 4, 6)
N_HOPS = len(ICI_HOPS)


def _remote(src, dst, send_sem, recv_sem, peer):
    return pltpu.make_async_remote_copy(
        src_ref=src, dst_ref=dst, send_sem=send_sem, recv_sem=recv_sem, device_id=peer, device_id_type=MESH_IDS)


def _gather_exchange(shards, middle_eighths=MIDDLE_EIGHTHS):
    n = len(shards)

    def place():
        x, y, c, me = _mesh_place()
        sibling, _ = _peer(x, y, c, 1)
        to_x, from_x = _peer(x, y, c, 4)
        to_y, from_y = _peer(x, y, c, 2)
        _, from_diagonal = _peer(x, y, c, 6)
        onward = (c * to_y[0] + (1 - c) * to_x[0], c * to_y[1] + (1 - c) * to_x[1], c)
        passed_on = c * from_x + (1 - c) * from_y
        return me, sibling, (to_x, to_y, onward), (from_x, from_y, from_diagonal), passed_on

    def start(src, dst, sems):
        ici_send, ici_recv, pair_send, pair_recv, local = sems
        me, sibling, targets, _, _ = place()
        for a in range(n):
            pltpu.make_async_copy(src[a], dst[a].at[me], local.at[a]).start()
            for j in range(2):
                _remote(src[a], dst[a].at[me], ici_send.at[a, j], ici_recv.at[a, j], targets[j]).start()
            _remote(src[a], dst[a].at[me], pair_send.at[a, 0], pair_recv.at[a, 0], sibling).start()

    def to_sibling(dst, sems, a, j, origin, sibling):
        _, _, pair_send, pair_recv, _ = sems
        slot = dst[a].at[origin]
        return _remote(slot, slot, pair_send.at[a, 1 + j], pair_recv.at[a, 1 + j], sibling)

    def middle(src, dst, sems):
        ici_send, ici_recv, _, _, _ = sems
        _, sibling, targets, origins, passed_on = place()
        for a in range(n):
            for j in range(2):
                _remote(src[a], dst[a].at[origins[j]], ici_send.at[a, j], ici_recv.at[a, j], targets[j]).wait_recv()
            slot = dst[a].at[passed_on]
            _remote(slot, slot, ici_send.at[a, 2], ici_recv.at[a, 2], targets[2]).start()
            for j in range(2):
                to_sibling(dst, sems, a, j, origins[j], sibling).start()

    def finish(src, dst, sems):
        ici_send, ici_recv, pair_send, pair_recv, local = sems
        me, sibling, targets, origins, _ = place()
        for a in range(n):
            _remote(src[a], dst[a].at[origins[2]], ici_send.at[a, 2], ici_recv.at[a, 2], targets[2]).wait_recv()
            to_sibling(dst, sems, a, 2, origins[2], sibling).start()
        for a in range(n):
            pltpu.make_async_copy(src[a], dst[a].at[me], local.at[a]).wait()
            for j in range(N_HOPS):
                _remote(src[a], dst[a].at[me], ici_send.at[a, j], ici_recv.at[a, j], targets[j]).wait_send()
            for j, origin in enumerate((me,) + origins):
                from_sibling = origin + 1 - 2 * (origin % 2)
                passed = _remote(src[a], dst[a].at[from_sibling], pair_send.at[a, j], pair_recv.at[a, j], sibling)
                passed.wait_send()
                passed.wait_recv()

    return _Exchange(
        shards,
        [jax.ShapeDtypeStruct((N_DEV,) + s.shape, s.dtype) for s in shards],
        [
            pltpu.SemaphoreType.DMA((n, N_HOPS)), pltpu.SemaphoreType.DMA((n, N_HOPS)),
            pltpu.SemaphoreType.DMA((n, N_HOPS + 1)), pltpu.SemaphoreType.DMA((n, N_HOPS + 1)),
            pltpu.SemaphoreType.DMA((n,)),
        ],
        start, finish, middle, middle_eighths)


def _scatter_copies(src, dst, sems, n, arrivals=False):
    send, recv, local = sems
    x, y, c, _ = _mesh_place()
    chip = 2 * x + y
    if arrivals is None:
        return [pltpu.make_async_copy(src[a].at[chip], dst[a].at[chip], local.at[a]) for a in range(n)]
    copies = []
    for a in range(n):
        for j, k in enumerate(ICI_HOPS):
            peer, _ = _peer(x, y, c, k)
            peer_chip = 2 * peer[0] + peer[1]
            slot = dst[a].at[peer_chip if arrivals else chip]
            copies.append(_remote(src[a].at[peer_chip], slot, send.at[a, j], recv.at[a, j], peer))
    return copies


def _scatter_start(src, dst, sems, n):
    for cp in _scatter_copies(src, dst, sems, n, arrivals=None) + _scatter_copies(src, dst, sems, n):
        cp.start()


def _scatter_finish(src, dst, sems, n):
    for cp in _scatter_copies(src, dst, sems, n, arrivals=None):
        cp.wait()
    for cp in _scatter_copies(src, dst, sems, n):
        cp.wait_send()
    for cp in _scatter_copies(src, dst, sems, n, arrivals=True):
        cp.wait_recv()


def _scatter_scratch(n):
    return [pltpu.SemaphoreType.DMA((n, N_HOPS)), pltpu.SemaphoreType.DMA((n, N_HOPS)), pltpu.SemaphoreType.DMA((n,))]


def _scatter_exchange(partials):
    n = len(partials)
    return _Exchange(
        partials, [jax.ShapeDtypeStruct(p.shape, p.dtype) for p in partials], _scatter_scratch(n),
        lambda src, dst, sems: _scatter_start(src, dst, sems, n),
        lambda src, dst, sems: _scatter_finish(src, dst, sems, n))


SMALL_LAYOUT = {
    "ffn1_norm": (0, 1, 1024), "mix_norm": (1, 1, 1024), "xattn_norm": (2, 1, 1024), "mem_norm": (3, 1, 1024),
    "ffn2_norm": (4, 1, 1024), "final_norm": (5, 1, 1024), "lb_param": (6, 2, 512), "hgrn_out_norm": (8, 1, 512),
    "conv_w": (9, 3, 512), "loss": (12, 1, 128),
}


def _final_exchange(partials, small, shards):
    n = len(partials)
    names = list(small)
    width = 1024
    count = len(shards)
    n_parts, r, c = shards[0][0].shape
    assert all(p.shape == (n_parts, r, c) and w.shape == (r, c) for p, w, _, _ in shards)
    steps = r // ADAMW_CARRIED_ROWS

    def body(*refs):
        src, rest = refs[:n], refs[n:]
        pieces, rest = rest[:len(names)], rest[len(names):]
        shard_in, rest = rest[:4 * count], rest[4 * count:]
        dst, total_ref, rest = rest[:n], rest[n], rest[n + 1:]
        shard_out, rest = rest[:4 * count], rest[4 * count:]
        pack, gathered, small_send, small_recv = rest[:4]
        sems = rest[4:]
        x, y, c, me = _mesh_place()
        step = pl.program_id(0)

        @pl.when(step == 0)
        def _():
            pack[...] = jnp.zeros_like(pack)
            for name, piece in zip(names, pieces):
                row, nrows, ncols = SMALL_LAYOUT[name]
                pack[row:row + nrows, 0:ncols] = piece[...]
            for k in range(1, N_DEV):
                peer, _ = _peer(x, y, c, k)
                _remote(pack, gathered.at[me], small_send.at[k - 1], small_recv.at[k - 1], peer).start()
            _scatter_start(src, dst, sems, n)
            gathered[me] = pack[...]

        for i in range(count):
            p_ref, w_ref, m_ref, v_ref = shard_in[4 * i:4 * i + 4]
            g = p_ref[0].astype(F32)
            for j in range(1, n_parts):
                g = g + p_ref[j].astype(F32)
            delta, mn, vn = _adamw_math(w_ref[...], g, m_ref[...], v_ref[...])
            for ref, value in zip(shard_out[4 * i:4 * i + 4], (g, delta, mn, vn)):
                ref[...] = value

        @pl.when(step == steps - 1)
        def _():
            for k in range(1, N_DEV):
                peer, peer_index = _peer(x, y, c, k)
                landed = _remote(pack, gathered.at[peer_index], small_send.at[k - 1], small_recv.at[k - 1], peer)
                landed.wait_send()
                landed.wait_recv()
            total = gathered[0]
            for j in range(1, N_DEV):
                total = total + gathered[j]
            total_ref[...] = total
            _scatter_finish(src, dst, sems, n)

    hbm = pl.BlockSpec(memory_space=pltpu.HBM)
    vmem = pl.BlockSpec(memory_space=pltpu.VMEM)
    tile = pl.BlockSpec((ADAMW_CARRIED_ROWS, c), lambda i: (i, 0))
    tiles_in = [pl.BlockSpec((n_parts, ADAMW_CARRIED_ROWS, c), lambda i: (0, i, 0)), tile, tile, tile]
    out = pl.pallas_call(
        body,
        name="final_exchange",
        grid=(steps,),
        in_specs=[hbm] * n + [vmem] * len(names) + tiles_in * count,
        out_specs=[hbm] * n + [vmem] + [tile] * (4 * count),
        out_shape=[jax.ShapeDtypeStruct(p.shape, p.dtype) for p in partials]
        + [jax.ShapeDtypeStruct((SMALL_ROWS, width), F32)] + [jax.ShapeDtypeStruct((r, c), F32)] * (4 * count),
        scratch_shapes=[
            pltpu.VMEM((SMALL_ROWS, width), F32), pltpu.VMEM((N_DEV, SMALL_ROWS, width), F32),
            pltpu.SemaphoreType.DMA((N_DEV - 1,)), pltpu.SemaphoreType.DMA((N_DEV - 1,)),
        ] + _scatter_scratch(n),
        compiler_params=pltpu.CompilerParams(
            dimension_semantics=("arbitrary",), vmem_limit_bytes=VMEM_LIMIT, has_side_effects=True),
    )(*partials, *[small[k] for k in names], *[operand for shard in shards for operand in shard])
    updated = out[n + 1:]
    return out[:n], out[n], [updated[4 * i:4 * i + 4] for i in range(count)]


def _adamw_math(w, g, m, v):
    m = ADAM_B1 * m + (1.0 - ADAM_B1) * g
    v = ADAM_B2 * v + (1.0 - ADAM_B2) * (g * g)
    m_hat = m / (1.0 - ADAM_B1 ** ADAM_STEP)
    v_hat = v / (1.0 - ADAM_B2 ** ADAM_STEP)
    delta = -ADAM_LR * (m_hat / (jnp.sqrt(v_hat) + ADAM_EPS) + ADAM_WD * w)
    return delta, m, v


def _adamw_shard(parts, w, m, v):
    r, c = w.shape
    n_parts = parts.shape[0]
    tr = max(rows for rows in range(16, r + 1, 16) if r % rows == 0 and rows * c <= ADAMW_TILE_ELEMENTS)

    def body(p_ref, w_ref, m_ref, v_ref, g_ref, d_ref, mo_ref, vo_ref):
        g = p_ref[0].astype(F32)
        for j in range(1, n_parts):
            g = g + p_ref[j].astype(F32)
        delta, mn, vn = _adamw_math(w_ref[...], g, m_ref[...], v_ref[...])
        g_ref[...] = g
        d_ref[...] = delta
        mo_ref[...] = mn
        vo_ref[...] = vn

    tile = pl.BlockSpec((tr, c), lambda i: (i, 0))
    return pl.pallas_call(
        body,
        name="adamw_shard",
        grid=(r // tr,),
        in_specs=[pl.BlockSpec((n_parts, tr, c), lambda i: (0, i, 0)), tile, tile, tile],
        out_specs=[tile] * 4,
        out_shape=[jax.ShapeDtypeStruct((r, c), F32)] * 4,
        compiler_params=_params(("parallel",)),
    )(parts, w, m, v)


def _adamw_small(gs, ws, ms, vs):
    n = len(gs)

    def body(*refs):
        g_refs, w_refs, m_refs, v_refs = refs[:n], refs[n:2 * n], refs[2 * n:3 * n], refs[3 * n:4 * n]
        g_out, d_out, m_out, v_out = refs[4 * n:5 * n], refs[5 * n:6 * n], refs[6 * n:7 * n], refs[7 * n:8 * n]
        for i in range(n):
            if gs[i].ndim == ws[i].ndim:
                g = g_refs[i][...]
            else:
                g = g_refs[i][0].astype(F32)
                for j in range(1, gs[i].shape[0]):
                    g = g + g_refs[i][j].astype(F32)
            delta, mn, vn = _adamw_math(w_refs[i][...], g, m_refs[i][...], v_refs[i][...])
            g_out[i][...] = g
            d_out[i][...] = delta
            m_out[i][...] = mn
            v_out[i][...] = vn

    shapes = [jax.ShapeDtypeStruct(w.shape, F32) for w in ws]
    out = pl.pallas_call(
        body,
        name="adamw_small",
        out_shape=shapes * 4,
        compiler_params=_params(),
    )(*gs, *ws, *ms, *vs)
    return out[:n], out[n:2 * n], out[2 * n:3 * n], out[3 * n:]


TRANSPOSED = ("ffn1_gate", "ffn1_up", "w_in", "ffn2_gate", "ffn2_up", "conv_w")
GROUP_FFN1 = ("ffn1_gate", "ffn1_up", "ffn1_down")
GROUP_MIX = ("w_in", "w_out")
GROUP_XATTN = ("w_q_mem", "w_kv_mem", "w_o_mem")
GROUP_FFN2 = ("ffn2_gate", "ffn2_up", "ffn2_down")
LARGE = GROUP_FFN1 + GROUP_MIX + GROUP_XATTN + GROUP_FFN2
SHORT_SHARDS = ("w_out", "w_q_mem", "w_kv_mem", "w_o_mem")
SMALL = ("ffn1_norm", "mix_norm", "lb_param", "hgrn_out_norm", "conv_w", "xattn_norm", "mem_norm", "ffn2_norm",
         "final_norm")
WEIGHTS = ("ffn1_norm", "ffn1_gate", "ffn1_up", "ffn1_down", "mix_norm", "w_in", "lb_param", "hgrn_out_norm",
           "conv_w", "w_out", "xattn_norm", "mem_norm", "w_q_mem", "w_kv_mem", "w_o_mem", "ffn2_norm", "ffn2_gate",
           "ffn2_up", "ffn2_down", "final_norm")


def kernel(x, mem, ffn1_norm, ffn1_gate, ffn1_up, ffn1_down, mix_norm, w_in, lb_param, hgrn_out_norm, conv_w, w_out, xattn_norm, mem_norm, w_q_mem, w_kv_mem, w_o_mem, ffn2_norm, ffn2_gate, ffn2_up, ffn2_down, final_norm, loss_target, m_ffn1_norm, m_ffn1_gate, m_ffn1_up, m_ffn1_down, m_mix_norm, m_w_in, m_lb_param, m_hgrn_out_norm, m_conv_w, m_w_out, m_xattn_norm, m_mem_norm, m_w_q_mem, m_w_kv_mem, m_w_o_mem, m_ffn2_norm, m_ffn2_gate, m_ffn2_up, m_ffn2_down, m_final_norm, v_ffn1_norm, v_ffn1_gate, v_ffn1_up, v_ffn1_down, v_mix_norm, v_w_in, v_lb_param, v_hgrn_out_norm, v_conv_w, v_w_out, v_xattn_norm, v_mem_norm, v_w_q_mem, v_w_kv_mem, v_w_o_mem, v_ffn2_norm, v_ffn2_gate, v_ffn2_up, v_ffn2_down, v_final_norm):
    given = dict(locals())
    me = 4 * lax.axis_index("x") + 2 * lax.axis_index("y") + lax.axis_index("c")
    x0, memv, target = x[0], mem[0], loss_target[0]

    def shard(prefix, name):
        v = given[prefix + name]
        if v.ndim == 1:
            return v.reshape(1, -1)
        if v.ndim == 2:
            return v
        return v[0].T if name in TRANSPOSED else v[0]

    w = {name: shard("", name) for name in WEIGHTS}
    m = {name: shard("m_", name) for name in WEIGHTS}
    v = {name: shard("v_", name) for name in WEIGHTS}

    conv_taps, conv_rows = w["conv_w"].shape
    conv_tile = jnp.pad(w["conv_w"], ((0, 8 - conv_taps), (0, 128 - conv_rows)))
    wire = {name: w[name].astype(BF16) for name in LARGE}
    full = {}

    def landed(names, gathered):
        for name, blocks in zip(names, gathered):
            _, r, c = blocks.shape
            full[name] = blocks if name == "w_kv_mem" else blocks.reshape(N_DEV * r, c)

    first = ("ffn1_gate", "ffn1_up")
    landed(first, _run_exchange(_gather_exchange([wire[k] for k in first]), "gather_first"))

    riders = (("ffn1_down", "w_in"), ("w_out", "w_kv_mem"), ("w_q_mem", "w_o_mem", "ffn2_gate", "ffn2_up"),
              ("ffn2_down",))
    (a1, b1, s1), gathered = _ffn_up(
        x0, w["ffn1_norm"], full["ffn1_gate"], full["ffn1_up"],
        exchange=_gather_exchange([wire[k] for k in riders[0]]))
    landed(riders[0], gathered)
    (x1,), gathered = _ffn_down(
        x0, s1, full["ffn1_down"], exchange=_gather_exchange([wire[k] for k in riders[1]] + [conv_tile]))
    landed(riders[1], gathered)
    convw_t = gathered[-1][:, :conv_taps, :conv_rows].transpose(1, 0, 2).reshape(conv_taps, N_DEV * conv_rows)
    (x2, z, o_raw, states, ycat), gathered = _mix_fwd(
        x1, w["mix_norm"], full["w_in"], w["lb_param"], w["hgrn_out_norm"], convw_t, full["w_out"],
        exchange=_gather_exchange([wire[k] for k in riders[2]]))
    landed(riders[2], gathered)
    kv = _memkv_fwd(memv, w["mem_norm"], full["w_kv_mem"])
    (x3, hq, qm, att), gathered = _xattn_fwd(
        x2, w["xattn_norm"], full["w_q_mem"], kv, full["w_o_mem"],
        exchange=_gather_exchange([wire[k] for k in riders[3]], middle_eighths=EARLY_MIDDLE_EIGHTHS))
    landed(riders[3], gathered)
    (dx4, a2, b2, s2, loss_part, d_final), _ = _ffn_fwd(
        x3, w["ffn2_norm"], full["ffn2_gate"], full["ffn2_up"], full["ffn2_down"], head=(w["final_norm"], target))

    parts = {}
    waiting = []

    def carried():
        names = [name for name, _ in waiting]
        exchange = _scatter_exchange([p for _, p in waiting]) if waiting else None
        del waiting[:]
        return names, exchange

    def received(names, arrived):
        for name, blocks in zip(names, arrived):
            group = name if isinstance(name, tuple) else (name,)
            rows = blocks.shape[1] // len(group)
            for p, member in enumerate(group):
                parts[member] = blocks[:, p * rows:(p + 1) * rows]

    def weight_grads(products):
        names, exchange = carried()
        partial, arrived = _weight_grad(list(products.values()), exchange=exchange)
        received(names, arrived)
        waiting.append((tuple(products), partial))

    def weight_grad(name, a, b, scale=1.0):
        weight_grads({name: (a, b, scale)})

    (dx3, da2, db2, h4, d_ffn2_norm), _ = _ffn_bwd(
        x3, w["ffn2_norm"], dx4, a2, b2, full["ffn2_gate"], full["ffn2_up"], full["ffn2_down"])
    weight_grad("ffn2_down", s2, dx4, 0.5)
    weight_grad("ffn2_gate", da2, h4)
    weight_grad("ffn2_up", db2, h4)
    names, exchange = carried()
    (dx2, dqm, dkv, d_xattn_norm), arrived = _xattn_bwd(
        x2, w["xattn_norm"], dx3, qm, kv, full["w_q_mem"], full["w_o_mem"], exchange=exchange)
    received(names, arrived)
    d_wkv, d_mem_norm = _memkv_bwd(memv, w["mem_norm"], dkv, full["w_kv_mem"])
    waiting.append(("w_kv_mem", d_wkv))
    names, exchange = carried()
    (dx1, dz, h2, d_mix_norm, d_lbp, d_gh, d_convw_t), arrived = _mix_bwd(
        x1, w["mix_norm"], dx2, z, o_raw, states, full["w_in"], w["lb_param"], w["hgrn_out_norm"], convw_t,
        full["w_out"], exchange=exchange)
    received(names, arrived)
    weight_grad("w_in", dz, h2)
    weight_grad("ffn1_down", s1, dx1, 0.5)
    (dx0, da1, db1, h1, d_ffn1_norm), _ = _ffn_bwd(
        x0, w["ffn1_norm"], dx1, a1, b1, full["ffn1_gate"], full["ffn1_up"], full["ffn1_down"])
    weight_grad("ffn1_gate", da1, h1)
    weight_grad("ffn1_up", db1, h1)
    weight_grads({"w_o_mem": (att, dx3, 1.0), "w_q_mem": (hq, dqm, 1.0), "w_out": (ycat, dx2, 1.0)})

    small_parts = {
        "ffn1_norm": d_ffn1_norm, "mix_norm": d_mix_norm, "xattn_norm": d_xattn_norm, "mem_norm": d_mem_norm,
        "ffn2_norm": d_ffn2_norm, "final_norm": d_final, "lb_param": d_lbp, "hgrn_out_norm": d_gh,
        "conv_w": d_convw_t, "loss": loss_part,
    }
    names = [name for name, _ in waiting]
    beside = GROUP_FFN2 + GROUP_FFN1
    arrived, total, updated = _final_exchange(
        [p for _, p in waiting], small_parts, [(parts[k], w[k], m[k], v[k]) for k in beside])
    received(names, arrived)
    g_out, d_out, m_out, v_out = {}, {}, {}, {}
    for name, results in zip(beside, updated):
        g_out[name], d_out[name], m_out[name], v_out[name] = results
    for name in LARGE:
        if name not in SHORT_SHARDS + beside:
            g_out[name], d_out[name], m_out[name], v_out[name] = _adamw_shard(parts[name], w[name], m[name], v[name])
    g_small = {name: parts[name] for name in SHORT_SHARDS}
    for name in SMALL:
        row, nrows, ncols = SMALL_LAYOUT[name]
        g_small[name] = total[row:row + nrows, 0:ncols]
    g_small["conv_w"] = lax.dynamic_slice_in_dim(g_small["conv_w"], me * conv_rows, conv_rows, axis=1)
    together = SMALL + SHORT_SHARDS
    gs, ds, ms, vs = _adamw_small(
        [g_small[k] for k in together], [w[k] for k in together], [m[k] for k in together],
        [v[k] for k in together])
    for i, name in enumerate(together):
        g_out[name], d_out[name], m_out[name], v_out[name] = gs[i], ds[i], ms[i], vs[i]

    def shaped(value, name):
        return (value.T if name in TRANSPOSED else value).reshape(given[name].shape)

    loss = total[SMALL_LAYOUT["loss"][0], 0]
    outs = [loss, dx0.reshape(x.shape)]
    for group in (g_out, d_out, m_out, v_out):
        outs += [shaped(group[name], name) for name in WEIGHTS]
    return tuple(outs)
```

```python
import jax
import jax.numpy as jnp
from jax import lax
from jax.experimental import pallas as pl
from jax.experimental.pallas import tpu as pltpu

F32 = jnp.float32
BF16 = jnp.bfloat16
MESH_IDS = pl.DeviceIdType.MESH

N_DEV = 8
EPS = 1e-6
HGRN_HEADS = 4
HGRN_DK = 128
HGRN_W = 512
CHUNK = 64
MEM_HEADS = 4
MEM_HD = 256
ADAM_LR = 0.001
ADAM_B1 = 0.9
ADAM_B2 = 0.999
ADAM_EPS = 1e-08
ADAM_WD = 0.01
ADAM_STEP = 10

TOKEN_TILE = 256
XATTN_TILE = 512
WIDE_TILE = 512
REDUCE_TILE = 1024
ADAMW_TILE_ELEMENTS = 256 * 1024
MIDDLE_EIGHTHS = 5
EARLY_MIDDLE_EIGHTHS = 4
MXU_ROWS = 256
VMEM_LIMIT = 60 * 1024 * 1024
SMALL_ROWS = 16
NT = (((1,), (1,)), ((), ()))
TN = (((0,), (0,)), ((), ()))


def _params(sem=None):
    return pltpu.CompilerParams(dimension_semantics=sem, vmem_limit_bytes=VMEM_LIMIT)


def _dot(a, b, dims=None):
    if dims is None:
        return jnp.dot(a, b, preferred_element_type=F32)
    return lax.dot_general(a, b, dims, preferred_element_type=F32)


def _sigmoid(v):
    return 1.0 / (1.0 + jnp.exp(-v))


def _rms(x, g):
    r = lax.rsqrt(jnp.mean(x * x, axis=-1, keepdims=True) + EPS)
    xh = x * r
    return xh * g, xh, r


def _rms_bwd(dh, xh, r, g):
    dxh = dh * g
    return r * (dxh - xh * jnp.mean(dxh * xh, axis=-1, keepdims=True))


def _full(shape):
    return pl.BlockSpec(shape, lambda *_: (0,) * len(shape))


def _full_once(shape):
    return pl.BlockSpec(shape, lambda *_: (0,) * len(shape), pipeline_mode=pl.Buffered(1))


def _rows(tm, width):
    return pl.BlockSpec((tm, width), lambda i: (i, 0))


def _rows_rev(tm, width, n):
    return pl.BlockSpec((tm, width), lambda i: (n - 1 - i, 0))


def _zero_at_start(*refs):
    @pl.when(pl.program_id(0) == 0)
    def _():
        for ref in refs:
            ref[...] = jnp.zeros_like(ref)


class _Exchange:
    def __init__(self, operands, out_shapes, scratch, start, finish, middle=None, middle_eighths=MIDDLE_EIGHTHS):
        self.operands, self.out_shapes, self.scratch = list(operands), list(out_shapes), list(scratch)
        self.start, self.middle, self.finish, self.middle_eighths = start, middle, finish, middle_eighths


def _call(body, *, name, grid, in_specs, out_specs, out_shape, args, scratch_shapes=(), exchange=None):
    semantics = ("arbitrary",) * len(grid)
    if exchange is None:
        out = pl.pallas_call(
            body, name=name, grid=grid, in_specs=in_specs, out_specs=out_specs, out_shape=out_shape,
            scratch_shapes=list(scratch_shapes), compiler_params=_params(semantics))(*args)
        return out, []
    hbm = pl.BlockSpec(memory_space=pltpu.HBM)
    n_in, n_out, n_scr = len(in_specs), len(out_specs), len(scratch_shapes)
    e_in, e_out = len(exchange.operands), len(exchange.out_shapes)

    def carried(*refs):
        ins, rest = refs[:n_in], refs[n_in:]
        e_ins, rest = rest[:e_in], rest[e_in:]
        outs, rest = rest[:n_out], rest[n_out:]
        e_outs, rest = rest[:e_out], rest[e_out:]
        scr, e_scr = rest[:n_scr], rest[n_scr:]
        first = last = None
        for axis, size in enumerate(grid):
            at_start, at_end = pl.program_id(axis) == 0, pl.program_id(axis) == size - 1
            first = at_start if first is None else jnp.logical_and(first, at_start)
            last = at_end if last is None else jnp.logical_and(last, at_end)

        @pl.when(first)
        def _():
            exchange.start(e_ins, e_outs, e_scr)

        body(*ins, *outs, *scr)

        if exchange.middle is not None:
            assert len(grid) == 1

            @pl.when(pl.program_id(0) == (grid[0] * exchange.middle_eighths) // 8)
            def _():
                exchange.middle(e_ins, e_outs, e_scr)

        @pl.when(last)
        def _():
            exchange.finish(e_ins, e_outs, e_scr)

    out = pl.pallas_call(
        carried, name=name, grid=grid, in_specs=list(in_specs) + [hbm] * e_in,
        out_specs=list(out_specs) + [hbm] * e_out, out_shape=list(out_shape) + exchange.out_shapes,
        scratch_shapes=list(scratch_shapes) + exchange.scratch,
        compiler_params=pltpu.CompilerParams(
            dimension_semantics=semantics, vmem_limit_bytes=VMEM_LIMIT, has_side_effects=True),
    )(*args, *exchange.operands)
    return out[:n_out], out[n_out:]


def _run_exchange(exchange, name):
    hbm = pl.BlockSpec(memory_space=pltpu.HBM)
    e_in, e_out = len(exchange.operands), len(exchange.out_shapes)

    def body(*refs):
        e_ins, e_outs, e_scr = refs[:e_in], refs[e_in:e_in + e_out], refs[e_in + e_out:]
        exchange.start(e_ins, e_outs, e_scr)
        if exchange.middle is not None:
            exchange.middle(e_ins, e_outs, e_scr)
        exchange.finish(e_ins, e_outs, e_scr)

    return pl.pallas_call(
        body, name=name, in_specs=[hbm] * e_in, out_specs=[hbm] * e_out, out_shape=exchange.out_shapes,
        scratch_shapes=exchange.scratch, compiler_params=pltpu.CompilerParams(has_side_effects=True),
    )(*exchange.operands)


def _loss_head(xo, gf, tgt):
    d = xo.shape[1]
    y, xh, r = _rms(xo, gf)
    err = y - tgt
    dy = err * (1.0 / d)
    loss = 0.5 * jnp.sum(jnp.sum(err * err, axis=-1, keepdims=True) * (1.0 / d), axis=0, keepdims=True)
    return _rms_bwd(dy, xh, r, gf), loss, jnp.sum(dy * xh, axis=0, keepdims=True)


def _ffn_fwd(x, g, wg, wu, wd, exchange=None, head=None):
    t, d = x.shape
    f = wg.shape[0]
    tm = min(WIDE_TILE, t)

    def body(x_ref, g_ref, wg_ref, wu_ref, wd_ref, *rest):
        if head is None:
            xo_ref, a_ref, b_ref, s_ref = rest
        else:
            gf_ref, tgt_ref, xo_ref, a_ref, b_ref, s_ref, loss_ref, dgf_ref = rest
            _zero_at_start(loss_ref, dgf_ref)
        xv = x_ref[...]
        h, _, _ = _rms(xv, g_ref[...])
        hb = h.astype(BF16)
        a = _dot(hb, wg_ref[...], NT)
        b = _dot(hb, wu_ref[...], NT)
        s = (a * _sigmoid(a) * b).astype(BF16)
        xo = xv + 0.5 * _dot(s, wd_ref[...])
        if head is None:
            xo_ref[...] = xo
        else:
            xo_ref[...], loss, dgf = _loss_head(xo, gf_ref[...], tgt_ref[...])
            loss_ref[...] += jnp.broadcast_to(loss, (1, 128))
            dgf_ref[...] += dgf
        a_ref[...] = a.astype(BF16)
        b_ref[...] = b.astype(BF16)
        s_ref[...] = s

    in_specs = [_rows(tm, d), _full((1, d)), _full_once((f, d)), _full_once((f, d)), _full_once((f, d))]
    out_specs = [_rows(tm, d), _rows(tm, f), _rows(tm, f), _rows(tm, f)]
    out_shape = [
        jax.ShapeDtypeStruct((t, d), F32),
        jax.ShapeDtypeStruct((t, f), BF16),
        jax.ShapeDtypeStruct((t, f), BF16),
        jax.ShapeDtypeStruct((t, f), BF16),
    ]
    args = (x, g, wg, wu, wd)
    if head is not None:
        in_specs += [_full((1, d)), _rows(tm, d)]
        out_specs += [_full((1, 128)), _full((1, d))]
        out_shape += [jax.ShapeDtypeStruct((1, 128), F32), jax.ShapeDtypeStruct((1, d), F32)]
        args += tuple(head)
    return _call(
        body, name="ffn_fwd", grid=(t // tm,), in_specs=in_specs, out_specs=out_specs, out_shape=out_shape,
        args=args, exchange=exchange)


def _ffn_up(x, g, wg, wu, exchange=None):
    t, d = x.shape
    f = wg.shape[0]
    tm = min(TOKEN_TILE, t)

    def body(x_ref, g_ref, wg_ref, wu_ref, a_ref, b_ref, s_ref):
        h, _, _ = _rms(x_ref[...], g_ref[...])
        hb = h.astype(BF16)
        a = _dot(hb, wg_ref[...], NT)
        b = _dot(hb, wu_ref[...], NT)
        a_ref[...] = a.astype(BF16)
        b_ref[...] = b.astype(BF16)
        s_ref[...] = (a * _sigmoid(a) * b).astype(BF16)

    return _call(
        body, name="ffn_up", grid=(t // tm,),
        in_specs=[_rows(tm, d), _full((1, d)), _full_once((f, d)), _full_once((f, d))],
        out_specs=[_rows(tm, f)] * 3, out_shape=[jax.ShapeDtypeStruct((t, f), BF16)] * 3,
        args=(x, g, wg, wu), exchange=exchange)


def _ffn_down(x, s, wd, exchange=None):
    t, d = x.shape
    f = wd.shape[0]
    tm = min(TOKEN_TILE, t)

    def body(x_ref, s_ref, wd_ref, xo_ref):
        xo_ref[...] = x_ref[...] + 0.5 * _dot(s_ref[...], wd_ref[...])

    return _call(
        body, name="ffn_down", grid=(t // tm,),
        in_specs=[_rows(tm, d), _rows(tm, f), _full_once((f, d))],
        out_specs=[_rows(tm, d)], out_shape=[jax.ShapeDtypeStruct((t, d), F32)],
        args=(x, s, wd), exchange=exchange)


def _ffn_bwd(x, g, dxo, a, b, wg, wu, wd, exchange=None):
    t, d = x.shape
    f = wg.shape[0]
    tm = min(TOKEN_TILE, t)

    def body(x_ref, g_ref, dxo_ref, a_ref, b_ref, wg_ref, wu_ref, wd_ref, dx_ref, da_ref, db_ref, h_ref, dg_ref):
        _zero_at_start(dg_ref)
        gv = g_ref[...]
        h, xh, r = _rms(x_ref[...], gv)
        dxo = dxo_ref[...]
        ds = _dot((0.5 * dxo).astype(BF16), wd_ref[...], NT)
        af = a_ref[...].astype(F32)
        bf = b_ref[...].astype(F32)
        sg = _sigmoid(af)
        da = (ds * bf * (sg * (1.0 + af * (1.0 - sg)))).astype(BF16)
        db = (ds * (af * sg)).astype(BF16)
        dh = _dot(da, wg_ref[...]) + _dot(db, wu_ref[...])
        dx_ref[...] = _rms_bwd(dh, xh, r, gv) + dxo
        da_ref[...] = da
        db_ref[...] = db
        h_ref[...] = h.astype(BF16)
        dg_ref[...] += jnp.sum(dh * xh, axis=0, keepdims=True)

    return _call(
        body,
        name="ffn_bwd",
        grid=(t // tm,),
        in_specs=[
            _rows(tm, d), _full((1, d)), _rows(tm, d), _rows(tm, f), _rows(tm, f),
            _full_once((f, d)), _full_once((f, d)), _full_once((f, d)),
        ],
        out_specs=[_rows(tm, d), _rows(tm, f), _rows(tm, f), _rows(tm, d), _full((1, d))],
        out_shape=[
            jax.ShapeDtypeStruct((t, d), F32),
            jax.ShapeDtypeStruct((t, f), BF16),
            jax.ShapeDtypeStruct((t, f), BF16),
            jax.ShapeDtypeStruct((t, d), BF16),
            jax.ShapeDtypeStruct((1, d), F32),
        ],
        args=(x, g, dxo, a, b, wg, wu, wd),
        exchange=exchange,
    )


def _weight_grad(products, exchange=None):
    count = len(products)
    t, m = products[0][0].shape
    n = products[0][1].shape[1]
    assert all(a.shape == (t, m) and b.shape == (t, n) for a, b, _ in products)
    chips = N_DEV // 2
    r = m // N_DEV
    tk = min(REDUCE_TILE, t)
    halves = 2
    nb = n // halves
    nk = t // tk

    def body(*refs):
        a_refs, b_refs, o_ref = refs[0:count], refs[count:2 * count], refs[2 * count]
        acc, send_buf, recv_buf, send_sems, recv_sems = refs[2 * count + 1:]
        k, j = pl.program_id(0), pl.program_id(1)
        x, y, c, _ = _mesh_place()
        sibling, _ = _peer(x, y, c, 1)
        for p, (_, _, scale) in enumerate(products):
            bv = b_refs[p][...]
            if scale != 1.0:
                bv = bv * scale
            bb = bv.astype(BF16)
            acc_half = acc.at[p, j]

            @pl.when(k == 0)
            def _():
                acc_half[...] = jnp.zeros_like(acc_half)

            for i in range(m // MXU_ROWS):
                rows = slice(i * MXU_ROWS, (i + 1) * MXU_ROWS)
                acc_half[rows, :] += _dot(a_refs[p][:, rows].astype(BF16), bb, TN)

        def to_sibling(half):
            return _remote(send_buf.at[half], recv_buf.at[half], send_sems.at[half], recv_sems.at[half], sibling)

        def owned_rows(q, core):
            return pl.ds(pl.multiple_of((2 * q + core) * r, 8), r)

        for half in range(halves):
            @pl.when(jnp.logical_and(k == nk - 1, j == half))
            def _():
                for p in range(count):
                    for q in range(chips):
                        send_buf[half, p, q] = acc[p, half, owned_rows(q, 1 - c), :].astype(BF16)
                to_sibling(half).start()

        @pl.when(jnp.logical_and(k == nk - 1, j == halves - 1))
        def _():
            for half in range(halves):
                to_sibling(half).wait_send()
                to_sibling(half).wait_recv()
                for p in range(count):
                    for q in range(chips):
                        o_ref[q, p * r:(p + 1) * r, half * nb:(half + 1) * nb] = (
                            acc[p, half, owned_rows(q, c), :] + recv_buf[half, p, q].astype(F32)).astype(BF16)

    (partial,), arrived = _call(
        body,
        name="weight_grad",
        grid=(nk, halves),
        in_specs=[pl.BlockSpec((tk, m), lambda k, j: (k, 0))] * count
        + [pl.BlockSpec((tk, nb), lambda k, j: (k, j))] * count,
        out_specs=[pl.BlockSpec((chips, count * r, n), lambda k, j: (0, 0, 0))],
        out_shape=[jax.ShapeDtypeStruct((chips, count * r, n), BF16)],
        scratch_shapes=[
            pltpu.VMEM((count, halves, m, nb), F32),
            pltpu.VMEM((halves, count, chips, r, nb), BF16), pltpu.VMEM((halves, count, chips, r, nb), BF16),
            pltpu.SemaphoreType.DMA((halves,)), pltpu.SemaphoreType.DMA((halves,)),
        ],
        args=tuple(a for a, _, _ in products) + tuple(b for _, b, _ in products),
        exchange=exchange,
    )
    return partial, arrived


def _chunk_cumsum(v, reverse=False):
    n, width = v.shape
    row = lax.broadcasted_iota(jnp.int32, (n, n), 0)
    col = lax.broadcasted_iota(jnp.int32, (n, n), 1)
    earlier = col >= row if reverse else col <= row
    tri = jnp.where(jnp.logical_and(row // CHUNK == col // CHUNK, earlier), 1.0, 0.0).astype(BF16)
    hi = v.astype(BF16)
    rest = v - hi.astype(F32)
    mid = rest.astype(BF16)
    low = (rest - mid.astype(F32)).astype(BF16)
    sums = _dot(tri, jnp.concatenate([hi, mid, low], axis=1))
    return sums[:, 0:width] + sums[:, width:2 * width] + sums[:, 2 * width:3 * width]


def _shift_rows(v, shift, edge):
    n = v.shape[0]
    row = lax.broadcasted_iota(jnp.int32, (n, 1), 0)
    out = pltpu.roll(v, shift % n, axis=0)
    if shift > 0:
        for j in range(shift):
            out = jnp.where(row == j, edge[8 - shift + j:8 - shift + j + 1, :], out)
    else:
        for j in range(-shift):
            out = jnp.where(row == n + shift + j, edge[j:j + 1, :], out)
    return out


def _gates(z, lbp):
    w = HGRN_W
    lb = _sigmoid(lbp[0:1, :] - lbp[1:2, :])
    zq = z[:, 0:w]
    sig = _sigmoid(z[:, w:2 * w])
    f = lb + (1.0 - lb) * sig
    sq = _sigmoid(zq)
    q = zq * sq * HGRN_DK ** -0.5
    return lb, sig, f, sq, q


def _decayed_operands(q, f, v, qm_buf, km_buf, kbar_buf, v_buf, etot_buf, emid_buf):
    n, width = f.shape
    bcum = _chunk_cumsum(jnp.log(f))

    def row_of_chunk(offset):
        return jnp.concatenate(
            [jnp.broadcast_to(bcum[c + offset:c + offset + 1, :], (CHUNK, width)) for c in range(0, n, CHUNK)], axis=0)

    total, mid = row_of_chunk(CHUNK - 1), row_of_chunk(CHUNK // 2 - 1)
    em, enm, erest = jnp.exp(bcum - mid), jnp.exp(mid - bcum), jnp.exp(total - bcum)
    kk = 1.0 - f
    qm_buf[...] = (q * em).astype(BF16)
    km_buf[...] = (kk * enm).astype(BF16)
    kbar_buf[...] = (kk * erest).astype(BF16)
    v_buf[...] = v.astype(BF16)
    etot_buf[...] = jnp.exp(total)
    emid_buf[...] = jnp.exp(mid)
    return em, enm, erest


def _short_conv(u, edge, cw):
    return cw[0:1, :] * _shift_rows(u, 2, edge) + cw[1:2, :] * _shift_rows(u, 1, edge) + cw[2:3, :] * u


def _block_causal_mask(n):
    row = lax.broadcasted_iota(jnp.int32, (n, n), 0)
    col = lax.broadcasted_iota(jnp.int32, (n, n), 1)
    return jnp.logical_and(row // CHUNK == col // CHUNK, col <= row)


def _spread(v, chunk_of_row, nc):
    return jnp.concatenate([jnp.where(chunk_of_row == c, v, jnp.zeros_like(v)) for c in range(nc)], axis=1)


def _pick(r, chunk_of_row, nc):
    out = jnp.where(chunk_of_row == 0, r[:, 0:HGRN_DK], 0.0)
    for c in range(1, nc):
        out = out + jnp.where(chunk_of_row == c, r[:, c * HGRN_DK:(c + 1) * HGRN_DK], 0.0)
    return out


def _mix_fwd(x, g, w_in, lbp, gh, convw_t, w_out, exchange=None):
    t, d = x.shape
    zw = w_in.shape[0]
    w = HGRN_W
    tm = min(TOKEN_TILE, t)
    nc = tm // CHUNK
    n_chunks = t // CHUNK

    def body(x_ref, g_ref, win_ref, lbp_ref, gh_ref, cw_ref, wout_ref,
             xo_ref, z_ref, o_ref, st_ref, y_ref, state, ucarry, qm_buf, km_buf, kbar_buf, v_buf, etot_buf, emid_buf):
        _zero_at_start(state, ucarry)
        xv = x_ref[...]
        h, _, _ = _rms(xv, g_ref[...])
        z_ref[...] = _dot(h.astype(BF16), win_ref[...], NT)
        z = z_ref[...]
        _, _, f, _, q = _gates(z, lbp_ref[...])
        _decayed_operands(q, f, z[:, 2 * w:3 * w], qm_buf, km_buf, kbar_buf, v_buf, etot_buf, emid_buf)
        mask = _block_causal_mask(tm)
        chunk_of_row = lax.broadcasted_iota(jnp.int32, (tm, 1), 0) // CHUNK
        heads = range(HGRN_HEADS)
        hcols = [slice(hd * HGRN_DK, (hd + 1) * HGRN_DK) for hd in heads]
        qm = [qm_buf[:, hcols[hd]] for hd in heads]
        vb = [v_buf[:, hcols[hd]] for hd in heads]
        scores = [jnp.where(mask, _dot(qm[hd], km_buf[:, hcols[hd]], NT), 0.0).astype(BF16) for hd in heads]
        gains = [_dot(_spread(vb[hd], chunk_of_row, nc), kbar_buf[:, hcols[hd]], TN) for hd in heads]
        entering = []
        for hd in heads:
            states, st = [], state[hd]
            for c in range(nc):
                first_row = slice(c * CHUNK, c * CHUNK + 1)
                states.append(st * emid_buf[first_row, hcols[hd]])
                st_ref[c, hd] = st
                st = st * etot_buf[first_row, hcols[hd]] + gains[hd][c * HGRN_DK:(c + 1) * HGRN_DK, :]
            state[hd] = st
            entering.append(jnp.concatenate(states, axis=0).astype(BF16))
        from_states = [_dot(qm[hd], entering[hd], NT) for hd in heads]
        o_heads = [_dot(scores[hd], vb[hd]) + _pick(from_states[hd], chunk_of_row, nc) for hd in heads]
        o_ref[...] = jnp.concatenate(o_heads, axis=1)
        ghv = gh_ref[...]
        normed = jnp.concatenate([_rms(o_heads[hd], ghv[:, hcols[hd]])[0] for hd in heads], axis=1)
        zg = z[:, 3 * w:4 * w]
        u = z[:, 5 * w:6 * w] * z[:, 6 * w:7 * w]
        conv = _short_conv(u, ucarry[...], cw_ref[...])
        ucarry[...] = u[tm - 8:tm, :]
        y = jnp.concatenate([normed * (zg * _sigmoid(zg)), z[:, 4 * w:5 * w] * conv], axis=1).astype(BF16)
        y_ref[...] = y
        xo_ref[...] = xv + _dot(y, wout_ref[...])

    return _call(
        body,
        name="mix_fwd",
        grid=(t // tm,),
        in_specs=[
            _rows(tm, d), _full((1, d)), _full((zw, d)), _full((2, w)), _full((1, w)), _full((3, w)),
            _full((2 * w, d)),
        ],
        out_specs=[
            _rows(tm, d), _rows(tm, zw), _rows(tm, w),
            pl.BlockSpec((nc, HGRN_HEADS, HGRN_DK, HGRN_DK), lambda i: (i, 0, 0, 0)),
            _rows(tm, 2 * w),
        ],
        out_shape=[
            jax.ShapeDtypeStruct((t, d), F32),
            jax.ShapeDtypeStruct((t, zw), F32),
            jax.ShapeDtypeStruct((t, w), F32),
            jax.ShapeDtypeStruct((n_chunks, HGRN_HEADS, HGRN_DK, HGRN_DK), F32),
            jax.ShapeDtypeStruct((t, 2 * w), BF16),
        ],
        scratch_shapes=[
            pltpu.VMEM((HGRN_HEADS, HGRN_DK, HGRN_DK), F32), pltpu.VMEM((8, w), F32),
            pltpu.VMEM((tm, w), BF16), pltpu.VMEM((tm, w), BF16), pltpu.VMEM((tm, w), BF16),
            pltpu.VMEM((tm, w), BF16), pltpu.VMEM((tm, w), F32), pltpu.VMEM((tm, w), F32),
        ],
        args=(x, g, w_in, lbp, gh, convw_t, w_out),
        exchange=exchange,
    )


def _mix_bwd(x, g, dxo, z, o, states, w_in, lbp, gh, convw_t, w_out, exchange=None):
    t, d = x.shape
    zw = w_in.shape[0]
    w = HGRN_W
    tm = min(TOKEN_TILE, t)
    nc = tm // CHUNK
    n = t // tm

    def body(x_ref, g_ref, dxo_ref, z_ref, zprev_ref, o_ref, st_ref, win_ref, lbp_ref, gh_ref, cw_ref, wout_ref,
             dx_ref, dz_ref, h_ref, dg_ref, dlbp_ref, dgh_ref, dcw_ref,
             dstate, dcarry, do_buf, qm_buf, km_buf, kbar_buf, v_buf, etot_buf, emid_buf):
        _zero_at_start(dstate, dcarry, dg_ref, dlbp_ref, dgh_ref, dcw_ref)
        gv = g_ref[...]
        h, xh, r = _rms(x_ref[...], gv)
        h_ref[...] = h.astype(BF16)
        dxo = dxo_ref[...]
        dy = _dot(dxo.astype(BF16), wout_ref[...], NT)
        z = z_ref[...]
        lb, sig, f, sq, q = _gates(z, lbp_ref[...])
        em, enm, erest = _decayed_operands(
            q, f, z[:, 2 * w:3 * w], qm_buf, km_buf, kbar_buf, v_buf, etot_buf, emid_buf)

        ghv = gh_ref[...]
        zg = z[:, 3 * w:4 * w]
        sgz = _sigmoid(zg)
        dyh = dy[:, 0:w]
        don = dyh * (zg * sgz)
        heads = range(HGRN_HEADS)
        hcols = [slice(hd * HGRN_DK, (hd + 1) * HGRN_DK) for hd in heads]
        norms = [_rms(o_ref[:, hcols[hd]], ghv[:, hcols[hd]]) for hd in heads]
        on = jnp.concatenate([norms[hd][0] for hd in heads], axis=1)
        oh = jnp.concatenate([norms[hd][1] for hd in heads], axis=1)
        dz_ref[:, 3 * w:4 * w] = (dyh * on * (sgz * (1.0 + zg * (1.0 - sgz)))).astype(BF16)
        dgh_ref[...] += jnp.sum(don * oh, axis=0, keepdims=True)
        do_buf[...] = jnp.concatenate(
            [_rms_bwd(don[:, hcols[hd]], norms[hd][1], norms[hd][2], ghv[:, hcols[hd]]) for hd in heads],
            axis=1).astype(BF16)

        zb = z[:, 4 * w:5 * w]
        zc = z[:, 5 * w:6 * w]
        zu = z[:, 6 * w:7 * w]
        u = zc * zu
        cw = cw_ref[...]
        zp = zprev_ref[...]
        uprev = jnp.where(pl.program_id(0) == n - 1, 0.0, zp[:, 5 * w:6 * w] * zp[:, 6 * w:7 * w])
        dyc = dy[:, w:2 * w]
        dz_ref[:, 4 * w:5 * w] = (dyc * _short_conv(u, uprev, cw)).astype(BF16)
        dconv = dyc * zb
        edge = dcarry[...]
        dconv1 = _shift_rows(dconv, -1, edge)
        dconv2 = _shift_rows(dconv, -2, edge)
        dcarry[...] = dconv[0:8, :]
        du = cw[2:3, :] * dconv + cw[1:2, :] * dconv1 + cw[0:1, :] * dconv2
        dz_ref[:, 5 * w:6 * w] = (du * zu).astype(BF16)
        dz_ref[:, 6 * w:7 * w] = (du * zc).astype(BF16)
        dcw_ref[...] += jnp.concatenate([
            jnp.sum(u * dconv2, axis=0, keepdims=True),
            jnp.sum(u * dconv1, axis=0, keepdims=True),
            jnp.sum(u * dconv, axis=0, keepdims=True)], axis=0)

        mask = _block_causal_mask(tm)
        chunk_of_row = lax.broadcasted_iota(jnp.int32, (tm, 1), 0) // CHUNK
        heads = range(HGRN_HEADS)
        hcols = [slice(hd * HGRN_DK, (hd + 1) * HGRN_DK) for hd in heads]
        qmb = [qm_buf[:, hcols[hd]] for hd in heads]
        kmb = [km_buf[:, hcols[hd]] for hd in heads]
        vb = [v_buf[:, hcols[hd]] for hd in heads]
        dob = [do_buf[:, hcols[hd]] for hd in heads]
        scores = [jnp.where(mask, _dot(qmb[hd], kmb[hd], NT), 0.0).astype(BF16) for hd in heads]
        dscores = [jnp.where(mask, _dot(dob[hd], vb[hd], NT), 0.0).astype(BF16) for hd in heads]
        gains = [_dot(_spread(dob[hd], chunk_of_row, nc), qmb[hd], TN) for hd in heads]
        dst_rows, dst_lanes, st_lanes, carries = [], [], [], []
        for hd in heads:
            entering = [st_ref[c, hd] for c in range(nc)]
            emid = [emid_buf[c * CHUNK:c * CHUNK + 1, hcols[hd]] for c in range(nc)]
            leaving, carried_back = [None] * nc, [None] * nc
            dst = dstate[hd]
            for c in reversed(range(nc)):
                elast = etot_buf[c * CHUNK:c * CHUNK + 1, hcols[hd]]
                leaving[c] = dst
                carried_back[c] = jnp.sum(dst * entering[c], axis=0, keepdims=True) * elast
                dst = dst * elast + gains[hd][c * HGRN_DK:(c + 1) * HGRN_DK, :] * emid[c]
            dstate[hd] = dst
            dst_rows.append(jnp.concatenate(leaving, axis=0).astype(BF16))
            dst_lanes.append(jnp.concatenate(leaving, axis=1).astype(BF16))
            st_lanes.append(jnp.concatenate([entering[c] * emid[c] for c in range(nc)], axis=1).astype(BF16))
            carries.append(carried_back)
        dv = [_dot(scores[hd], dob[hd], TN) + _pick(_dot(kbar_buf[:, hcols[hd]], dst_rows[hd], NT), chunk_of_row, nc)
              for hd in heads]
        dz_ref[:, 2 * w:3 * w] = jnp.concatenate(dv, axis=1).astype(BF16)
        dqm = jnp.concatenate([_dot(dscores[hd], kmb[hd]) + _pick(_dot(dob[hd], st_lanes[hd]), chunk_of_row, nc)
                               for hd in heads], axis=1)
        dkm = jnp.concatenate([_dot(dscores[hd], qmb[hd], TN) for hd in heads], axis=1)
        dkbar = jnp.concatenate([_pick(_dot(vb[hd], dst_lanes[hd]), chunk_of_row, nc) for hd in heads], axis=1)

        kbar_dkbar = kbar_buf[...].astype(F32) * dkbar
        db = qm_buf[...].astype(F32) * dqm - km_buf[...].astype(F32) * dkm - kbar_dkbar
        through_last = jnp.concatenate([
            jnp.broadcast_to(
                jnp.sum(kbar_dkbar[c * CHUNK:(c + 1) * CHUNK], axis=0, keepdims=True)
                + jnp.concatenate([carries[hd][c] for hd in heads], axis=1),
                (CHUNK, w))
            for c in range(nc)], axis=0)
        dlogf = _chunk_cumsum(db, reverse=True) + through_last
        df = dlogf / f - (dkm * enm + dkbar * erest)
        zq = z[:, 0:w]
        dz_ref[:, 0:w] = (dqm * em * HGRN_DK ** -0.5 * (sq * (1.0 + zq * (1.0 - sq)))).astype(BF16)
        dz_ref[:, w:2 * w] = (df * (1.0 - lb) * sig * (1.0 - sig)).astype(BF16)
        dlb = jnp.sum(df * (1.0 - sig), axis=0, keepdims=True) * lb * (1.0 - lb)
        dlbp_ref[...] += jnp.concatenate([dlb, -dlb], axis=0)

        dh = _dot(dz_ref[...], win_ref[...])
        dx_ref[...] = _rms_bwd(dh, xh, r, gv) + dxo
        dg_ref[...] += jnp.sum(dh * xh, axis=0, keepdims=True)

    return _call(
        body,
        name="mix_bwd",
        grid=(n,),
        in_specs=[
            _rows_rev(tm, d, n), _full((1, d)), _rows_rev(tm, d, n), _rows_rev(tm, zw, n),
            pl.BlockSpec((8, zw), lambda i: (jnp.maximum((n - 1 - i) * (tm // 8) - 1, 0), 0)),
            _rows_rev(tm, w, n),
            pl.BlockSpec((nc, HGRN_HEADS, HGRN_DK, HGRN_DK), lambda i: (n - 1 - i, 0, 0, 0)),
            _full((zw, d)), _full((2, w)), _full((1, w)), _full((3, w)), _full((2 * w, d)),
        ],
        out_specs=[
            _rows_rev(tm, d, n), _rows_rev(tm, zw, n), _rows_rev(tm, d, n),
            _full((1, d)), _full((2, w)), _full((1, w)), _full((3, w)),
        ],
        out_shape=[
            jax.ShapeDtypeStruct((t, d), F32),
            jax.ShapeDtypeStruct((t, zw), BF16),
            jax.ShapeDtypeStruct((t, d), BF16),
            jax.ShapeDtypeStruct((1, d), F32),
            jax.ShapeDtypeStruct((2, w), F32),
            jax.ShapeDtypeStruct((1, w), F32),
            jax.ShapeDtypeStruct((3, w), F32),
        ],
        scratch_shapes=[
            pltpu.VMEM((HGRN_HEADS, HGRN_DK, HGRN_DK), F32), pltpu.VMEM((8, w), F32),
            pltpu.VMEM((tm, w), BF16),
            pltpu.VMEM((tm, w), BF16), pltpu.VMEM((tm, w), BF16), pltpu.VMEM((tm, w), BF16),
            pltpu.VMEM((tm, w), BF16), pltpu.VMEM((tm, w), F32), pltpu.VMEM((tm, w), F32),
        ],
        args=(x, g, dxo, z, z, o, states, w_in, lbp, gh, convw_t, w_out),
        exchange=exchange,
    )


def _memkv_fwd(mem, g, wkv):
    m, d = mem.shape
    nb, _, cb = wkv.shape

    def body(mem_ref, g_ref, wkv_ref, kv_ref):
        mn, _, _ = _rms(mem_ref[...], g_ref[...])
        mnb = mn.astype(BF16)
        for j in range(nb):
            kv_ref[:, j * cb:(j + 1) * cb] = _dot(mnb, wkv_ref[j]).astype(BF16)

    return pl.pallas_call(
        body,
        name="memkv_fwd",
        out_shape=jax.ShapeDtypeStruct((m, nb * cb), BF16),
        compiler_params=_params(),
    )(mem, g, wkv)


def _memkv_bwd(mem, g, dkv, wkv):
    m, d = mem.shape
    nb, _, cb = wkv.shape
    chips = nb // 2

    def body(mem_ref, g_ref, dkv_ref, wkv_ref, dw_ref, dg_ref, dw_all, send_buf, recv_buf, send_sem, recv_sem):
        x, y, c, _ = _mesh_place()
        sibling, _ = _peer(x, y, c, 1)
        mn, xh, _ = _rms(mem_ref[...], g_ref[...])
        mnb = mn.astype(BF16)
        dmn = jnp.zeros((m, d), F32)
        for j in range(nb):
            dkvb = dkv_ref[:, j * cb:(j + 1) * cb].astype(BF16)
            dw_all[j] = _dot(mnb, dkvb, TN)
            dmn = dmn + _dot(dkvb, wkv_ref[j], NT)
        dg_ref[...] = jnp.sum(dmn * xh, axis=0, keepdims=True)
        for q in range(chips):
            send_buf[q] = dw_all[2 * q + 1 - c].astype(BF16)
        to_sibling = _remote(send_buf, recv_buf, send_sem, recv_sem, sibling)
        to_sibling.start()
        to_sibling.wait_send()
        to_sibling.wait_recv()
        for q in range(chips):
            dw_ref[q] = (dw_all[2 * q + c] + recv_buf[q].astype(F32)).astype(BF16)

    return pl.pallas_call(
        body,
        name="memkv_bwd",
        out_shape=[jax.ShapeDtypeStruct((chips, d, cb), BF16), jax.ShapeDtypeStruct((1, d), F32)],
        scratch_shapes=[
            pltpu.VMEM((nb, d, cb), F32), pltpu.VMEM((chips, d, cb), BF16), pltpu.VMEM((chips, d, cb), BF16),
            pltpu.SemaphoreType.DMA, pltpu.SemaphoreType.DMA,
        ],
        compiler_params=_params(),
    )(mem, g, dkv, wkv)


def _softmax_rows(qm_h, k_h):
    sc = _dot(qm_h, k_h, NT) * MEM_HD ** -0.5
    e = jnp.exp(sc - jnp.max(sc, axis=-1, keepdims=True))
    return e / jnp.sum(e, axis=-1, keepdims=True)


def _xattn_fwd(x, g, wq, kv, wo, exchange=None):
    t, d = x.shape
    m = kv.shape[0]
    tm = min(XATTN_TILE, t)

    def body(x_ref, g_ref, wq_ref, kv_ref, wo_ref, xo_ref, hq_ref, qm_ref, att_ref):
        xv = x_ref[...]
        h, _, _ = _rms(xv, g_ref[...])
        hb = h.astype(BF16)
        hq_ref[...] = hb
        qm = _dot(hb, wq_ref[...]).astype(BF16)
        qm_ref[...] = qm
        heads = range(MEM_HEADS)
        kcols = [slice(hd * MEM_HD, (hd + 1) * MEM_HD) for hd in heads]
        p = [_softmax_rows(qm[:, kcols[hd]], kv_ref[:, kcols[hd]]) for hd in heads]
        att = jnp.concatenate(
            [_dot(p[hd].astype(BF16), kv_ref[:, d + hd * MEM_HD:d + (hd + 1) * MEM_HD]) for hd in heads],
            axis=1).astype(BF16)
        att_ref[...] = att
        xo_ref[...] = xv + _dot(att, wo_ref[...])

    return _call(
        body,
        name="xattn_fwd",
        grid=(t // tm,),
        in_specs=[_rows(tm, d), _full((1, d)), _full((d, d)), _full((m, 2 * d)), _full((d, d))],
        out_specs=[_rows(tm, d), _rows(tm, d), _rows(tm, d), _rows(tm, d)],
        out_shape=[
            jax.ShapeDtypeStruct((t, d), F32),
            jax.ShapeDtypeStruct((t, d), BF16),
            jax.ShapeDtypeStruct((t, d), BF16),
            jax.ShapeDtypeStruct((t, d), BF16),
        ],
        args=(x, g, wq, kv, wo),
        exchange=exchange,
    )


def _xattn_bwd(x, g, dxo, qm, kv, wq, wo, exchange=None):
    t, d = x.shape
    m = kv.shape[0]
    tm = min(XATTN_TILE, t)

    def body(x_ref, g_ref, dxo_ref, qm_ref, kv_ref, wq_ref, wo_ref, dx_ref, dqm_ref, dkv_ref, dg_ref):
        _zero_at_start(dkv_ref, dg_ref)
        gv = g_ref[...]
        _, xh, r = _rms(x_ref[...], gv)
        dxo = dxo_ref[...]
        datt = _dot(dxo.astype(BF16), wo_ref[...], NT).astype(BF16)
        heads = range(MEM_HEADS)
        kcols = [slice(hd * MEM_HD, (hd + 1) * MEM_HD) for hd in heads]
        vcols = [slice(d + hd * MEM_HD, d + (hd + 1) * MEM_HD) for hd in heads]
        qm_h = [qm_ref[:, kcols[hd]] for hd in heads]
        p = [_softmax_rows(qm_h[hd], kv_ref[:, kcols[hd]]) for hd in heads]
        dp = [_dot(datt[:, kcols[hd]], kv_ref[:, vcols[hd]], NT) for hd in heads]
        dsc = [(p[hd] * (dp[hd] - jnp.sum(p[hd] * dp[hd], axis=-1, keepdims=True)) * MEM_HD ** -0.5).astype(BF16)
               for hd in heads]
        dqm = jnp.concatenate([_dot(dsc[hd], kv_ref[:, kcols[hd]]) for hd in heads], axis=1).astype(BF16)
        dqm_ref[...] = dqm
        dkv_ref[...] += jnp.concatenate(
            [_dot(dsc[hd], qm_h[hd], TN) for hd in heads]
            + [_dot(p[hd].astype(BF16), datt[:, kcols[hd]], TN) for hd in heads], axis=1)
        dh = _dot(dqm, wq_ref[...], NT)
        dx_ref[...] = _rms_bwd(dh, xh, r, gv) + dxo
        dg_ref[...] += jnp.sum(dh * xh, axis=0, keepdims=True)

    return _call(
        body,
        name="xattn_bwd",
        grid=(t // tm,),
        in_specs=[
            _rows(tm, d), _full((1, d)), _rows(tm, d), _rows(tm, d), _full((m, 2 * d)), _full((d, d)), _full((d, d)),
        ],
        out_specs=[_rows(tm, d), _rows(tm, d), _full((m, 2 * d)), _full((1, d))],
        out_shape=[
            jax.ShapeDtypeStruct((t, d), F32),
            jax.ShapeDtypeStruct((t, d), BF16),
            jax.ShapeDtypeStruct((m, 2 * d), F32),
            jax.ShapeDtypeStruct((1, d), F32),
        ],
        args=(x, g, dxo, qm, kv, wq, wo),
        exchange=exchange,
    )


def _mesh_place():
    x, y, c = lax.axis_index("x"), lax.axis_index("y"), lax.axis_index("c")
    return x, y, c, 4 * x + 2 * y + c


def _peer(x, y, c, k):
    px = 1 - x if k & 4 else x
    py = 1 - y if k & 2 else y
    pc = 1 - c if k & 1 else c
    return (px, py, pc), 4 * px + 2 * py + pc


ICI_HOPS = (2, 4, 6)
N_HOPS = len(ICI_HOPS)


def _remote(src, dst, send_sem, recv_sem, peer):
    return pltpu.make_async_remote_copy(
        src_ref=src, dst_ref=dst, send_sem=send_sem, recv_sem=recv_sem, device_id=peer, device_id_type=MESH_IDS)


def _gather_exchange(shards, middle_eighths=MIDDLE_EIGHTHS):
    n = len(shards)

    def place():
        x, y, c, me = _mesh_place()
        sibling, _ = _peer(x, y, c, 1)
        to_x, from_x = _peer(x, y, c, 4)
        to_y, from_y = _peer(x, y, c, 2)
        _, from_diagonal = _peer(x, y, c, 6)
        onward = (c * to_y[0] + (1 - c) * to_x[0], c * to_y[1] + (1 - c) * to_x[1], c)
        passed_on = c * from_x + (1 - c) * from_y
        return me, sibling, (to_x, to_y, onward), (from_x, from_y, from_diagonal), passed_on

    def start(src, dst, sems):
        ici_send, ici_recv, pair_send, pair_recv, local = sems
        me, sibling, targets, _, _ = place()
        for a in range(n):
            pltpu.make_async_copy(src[a], dst[a].at[me], local.at[a]).start()
            for j in range(2):
                _remote(src[a], dst[a].at[me], ici_send.at[a, j], ici_recv.at[a, j], targets[j]).start()
            _remote(src[a], dst[a].at[me], pair_send.at[a, 0], pair_recv.at[a, 0], sibling).start()

    def to_sibling(dst, sems, a, j, origin, sibling):
        _, _, pair_send, pair_recv, _ = sems
        slot = dst[a].at[origin]
        return _remote(slot, slot, pair_send.at[a, 1 + j], pair_recv.at[a, 1 + j], sibling)

    def middle(src, dst, sems):
        ici_send, ici_recv, _, _, _ = sems
        _, sibling, targets, origins, passed_on = place()
        for a in range(n):
            for j in range(2):
                _remote(src[a], dst[a].at[origins[j]], ici_send.at[a, j], ici_recv.at[a, j], targets[j]).wait_recv()
            slot = dst[a].at[passed_on]
            _remote(slot, slot, ici_send.at[a, 2], ici_recv.at[a, 2], targets[2]).start()
            for j in range(2):
                to_sibling(dst, sems, a, j, origins[j], sibling).start()

    def finish(src, dst, sems):
        ici_send, ici_recv, pair_send, pair_recv, local = sems
        me, sibling, targets, origins, _ = place()
        for a in range(n):
            _remote(src[a], dst[a].at[origins[2]], ici_send.at[a, 2], ici_recv.at[a, 2], targets[2]).wait_recv()
            to_sibling(dst, sems, a, 2, origins[2], sibling).start()
        for a in range(n):
            pltpu.make_async_copy(src[a], dst[a].at[me], local.at[a]).wait()
            for j in range(N_HOPS):
                _remote(src[a], dst[a].at[me], ici_send.at[a, j], ici_recv.at[a, j], targets[j]).wait_send()
            for j, origin in enumerate((me,) + origins):
                from_sibling = origin + 1 - 2 * (origin % 2)
                passed = _remote(src[a], dst[a].at[from_sibling], pair_send.at[a, j], pair_recv.at[a, j], sibling)
                passed.wait_send()
                passed.wait_recv()

    return _Exchange(
        shards,
        [jax.ShapeDtypeStruct((N_DEV,) + s.shape, s.dtype) for s in shards],
        [
            pltpu.SemaphoreType.DMA((n, N_HOPS)), pltpu.SemaphoreType.DMA((n, N_HOPS)),
            pltpu.SemaphoreType.DMA((n, N_HOPS + 1)), pltpu.SemaphoreType.DMA((n, N_HOPS + 1)),
            pltpu.SemaphoreType.DMA((n,)),
        ],
        start, finish, middle, middle_eighths)


def _scatter_copies(src, dst, sems, n, arrivals=False):
    send, recv, local = sems
    x, y, c, _ = _mesh_place()
    chip = 2 * x + y
    if arrivals is None:
        return [pltpu.make_async_copy(src[a].at[chip], dst[a].at[chip], local.at[a]) for a in range(n)]
    copies = []
    for a in range(n):
        for j, k in enumerate(ICI_HOPS):
            peer, _ = _peer(x, y, c, k)
            peer_chip = 2 * peer[0] + peer[1]
            slot = dst[a].at[peer_chip if arrivals else chip]
            copies.append(_remote(src[a].at[peer_chip], slot, send.at[a, j], recv.at[a, j], peer))
    return copies


def _scatter_start(src, dst, sems, n):
    for cp in _scatter_copies(src, dst, sems, n, arrivals=None) + _scatter_copies(src, dst, sems, n):
        cp.start()


def _scatter_finish(src, dst, sems, n):
    for cp in _scatter_copies(src, dst, sems, n, arrivals=None):
        cp.wait()
    for cp in _scatter_copies(src, dst, sems, n):
        cp.wait_send()
    for cp in _scatter_copies(src, dst, sems, n, arrivals=True):
        cp.wait_recv()


def _scatter_scratch(n):
    return [pltpu.SemaphoreType.DMA((n, N_HOPS)), pltpu.SemaphoreType.DMA((n, N_HOPS)), pltpu.SemaphoreType.DMA((n,))]


def _scatter_exchange(partials):
    n = len(partials)
    return _Exchange(
        partials, [jax.ShapeDtypeStruct(p.shape, p.dtype) for p in partials], _scatter_scratch(n),
        lambda src, dst, sems: _scatter_start(src, dst, sems, n),
        lambda src, dst, sems: _scatter_finish(src, dst, sems, n))


SMALL_LAYOUT = {
    "ffn1_norm": (0, 1, 1024), "mix_norm": (1, 1, 1024), "xattn_norm": (2, 1, 1024), "mem_norm": (3, 1, 1024),
    "ffn2_norm": (4, 1, 1024), "final_norm": (5, 1, 1024), "lb_param": (6, 2, 512), "hgrn_out_norm": (8, 1, 512),
    "conv_w": (9, 3, 512), "loss": (12, 1, 128),
}


def _final_exchange(partials, small):
    n = len(partials)
    names = list(small)
    width = 1024

    def body(*refs):
        src = refs[:n]
        pieces = refs[n:n + len(names)]
        dst = refs[n + len(names):2 * n + len(names)]
        total_ref = refs[2 * n + len(names)]
        pack, gathered, small_send, small_recv = refs[2 * n + len(names) + 1:2 * n + len(names) + 5]
        sems = refs[2 * n + len(names) + 5:]
        x, y, c, me = _mesh_place()
        pack[...] = jnp.zeros_like(pack)
        for name, piece in zip(names, pieces):
            row, nrows, ncols = SMALL_LAYOUT[name]
            pack[row:row + nrows, 0:ncols] = piece[...]
        for k in range(1, N_DEV):
            peer, _ = _peer(x, y, c, k)
            _remote(pack, gathered.at[me], small_send.at[k - 1], small_recv.at[k - 1], peer).start()
        _scatter_start(src, dst, sems, n)
        gathered[me] = pack[...]
        for k in range(1, N_DEV):
            peer, peer_index = _peer(x, y, c, k)
            landed = _remote(pack, gathered.at[peer_index], small_send.at[k - 1], small_recv.at[k - 1], peer)
            landed.wait_send()
            landed.wait_recv()
        total = gathered[0]
        for j in range(1, N_DEV):
            total = total + gathered[j]
        total_ref[...] = total
        _scatter_finish(src, dst, sems, n)

    hbm = pl.BlockSpec(memory_space=pltpu.HBM)
    vmem = pl.BlockSpec(memory_space=pltpu.VMEM)
    out = pl.pallas_call(
        body,
        name="final_exchange",
        in_specs=[hbm] * n + [vmem] * len(names),
        out_specs=[hbm] * n + [vmem],
        out_shape=[jax.ShapeDtypeStruct(p.shape, p.dtype) for p in partials]
        + [jax.ShapeDtypeStruct((SMALL_ROWS, width), F32)],
        scratch_shapes=[
            pltpu.VMEM((SMALL_ROWS, width), F32), pltpu.VMEM((N_DEV, SMALL_ROWS, width), F32),
            pltpu.SemaphoreType.DMA((N_DEV - 1,)), pltpu.SemaphoreType.DMA((N_DEV - 1,)),
        ] + _scatter_scratch(n),
        compiler_params=pltpu.CompilerParams(has_side_effects=True),
    )(*partials, *[small[k] for k in names])
    return out[:n], out[n]


def _adamw_math(w, g, m, v):
    m = ADAM_B1 * m + (1.0 - ADAM_B1) * g
    v = ADAM_B2 * v + (1.0 - ADAM_B2) * (g * g)
    m_hat = m / (1.0 - ADAM_B1 ** ADAM_STEP)
    v_hat = v / (1.0 - ADAM_B2 ** ADAM_STEP)
    delta = -ADAM_LR * (m_hat / (jnp.sqrt(v_hat) + ADAM_EPS) + ADAM_WD * w)
    return delta, m, v


def _adamw_shard(parts, w, m, v):
    r, c = w.shape
    n_parts = parts.shape[0]
    tr = max(rows for rows in range(16, r + 1, 16) if r % rows == 0 and rows * c <= ADAMW_TILE_ELEMENTS)

    def body(p_ref, w_ref, m_ref, v_ref, g_ref, d_ref, mo_ref, vo_ref):
        g = p_ref[0].astype(F32)
        for j in range(1, n_parts):
            g = g + p_ref[j].astype(F32)
        delta, mn, vn = _adamw_math(w_ref[...], g, m_ref[...], v_ref[...])
        g_ref[...] = g
        d_ref[...] = delta
        mo_ref[...] = mn
        vo_ref[...] = vn

    tile = pl.BlockSpec((tr, c), lambda i: (i, 0))
    return pl.pallas_call(
        body,
        name="adamw_shard",
        grid=(r // tr,),
        in_specs=[pl.BlockSpec((n_parts, tr, c), lambda i: (0, i, 0)), tile, tile, tile],
        out_specs=[tile] * 4,
        out_shape=[jax.ShapeDtypeStruct((r, c), F32)] * 4,
        compiler_params=_params(("parallel",)),
    )(parts, w, m, v)


def _adamw_small(gs, ws, ms, vs):
    n = len(gs)

    def body(*refs):
        g_refs, w_refs, m_refs, v_refs = refs[:n], refs[n:2 * n], refs[2 * n:3 * n], refs[3 * n:4 * n]
        g_out, d_out, m_out, v_out = refs[4 * n:5 * n], refs[5 * n:6 * n], refs[6 * n:7 * n], refs[7 * n:8 * n]
        for i in range(n):
            if gs[i].ndim == ws[i].ndim:
                g = g_refs[i][...]
            else:
                g = g_refs[i][0].astype(F32)
                for j in range(1, gs[i].shape[0]):
                    g = g + g_refs[i][j].astype(F32)
            delta, mn, vn = _adamw_math(w_refs[i][...], g, m_refs[i][...], v_refs[i][...])
            g_out[i][...] = g
            d_out[i][...] = delta
            m_out[i][...] = mn
            v_out[i][...] = vn

    shapes = [jax.ShapeDtypeStruct(w.shape, F32) for w in ws]
    out = pl.pallas_call(
        body,
        name="adamw_small",
        out_shape=shapes * 4,
        compiler_params=_params(),
    )(*gs, *ws, *ms, *vs)
    return out[:n], out[n:2 * n], out[2 * n:3 * n], out[3 * n:]


TRANSPOSED = ("ffn1_gate", "ffn1_up", "w_in", "ffn2_gate", "ffn2_up", "conv_w")
GROUP_FFN1 = ("ffn1_gate", "ffn1_up", "ffn1_down")
GROUP_MIX = ("w_in", "w_out")
GROUP_XATTN = ("w_q_mem", "w_kv_mem", "w_o_mem")
GROUP_FFN2 = ("ffn2_gate", "ffn2_up", "ffn2_down")
LARGE = GROUP_FFN1 + GROUP_MIX + GROUP_XATTN + GROUP_FFN2
SHORT_SHARDS = ("w_out", "w_q_mem", "w_kv_mem", "w_o_mem")
SMALL = ("ffn1_norm", "mix_norm", "lb_param", "hgrn_out_norm", "conv_w", "xattn_norm", "mem_norm", "ffn2_norm",
         "final_norm")
WEIGHTS = ("ffn1_norm", "ffn1_gate", "ffn1_up", "ffn1_down", "mix_norm", "w_in", "lb_param", "hgrn_out_norm",
           "conv_w", "w_out", "xattn_norm", "mem_norm", "w_q_mem", "w_kv_mem", "w_o_mem", "ffn2_norm", "ffn2_gate",
           "ffn2_up", "ffn2_down", "final_norm")


def kernel(x, mem, ffn1_norm, ffn1_gate, ffn1_up, ffn1_down, mix_norm, w_in, lb_param, hgrn_out_norm, conv_w, w_out, xattn_norm, mem_norm, w_q_mem, w_kv_mem, w_o_mem, ffn2_norm, ffn2_gate, ffn2_up, ffn2_down, final_norm, loss_target, m_ffn1_norm, m_ffn1_gate, m_ffn1_up, m_ffn1_down, m_mix_norm, m_w_in, m_lb_param, m_hgrn_out_norm, m_conv_w, m_w_out, m_xattn_norm, m_mem_norm, m_w_q_mem, m_w_kv_mem, m_w_o_mem, m_ffn2_norm, m_ffn2_gate, m_ffn2_up, m_ffn2_down, m_final_norm, v_ffn1_norm, v_ffn1_gate, v_ffn1_up, v_ffn1_down, v_mix_norm, v_w_in, v_lb_param, v_hgrn_out_norm, v_conv_w, v_w_out, v_xattn_norm, v_mem_norm, v_w_q_mem, v_w_kv_mem, v_w_o_mem, v_ffn2_norm, v_ffn2_gate, v_ffn2_up, v_ffn2_down, v_final_norm):
    given = dict(locals())
    me = 4 * lax.axis_index("x") + 2 * lax.axis_index("y") + lax.axis_index("c")
    x0, memv, target = x[0], mem[0], loss_target[0]

    def shard(prefix, name):
        v = given[prefix + name]
        if v.ndim == 1:
            return v.reshape(1, -1)
        if v.ndim == 2:
            return v
        return v[0].T if name in TRANSPOSED else v[0]

    w = {name: shard("", name) for name in WEIGHTS}
    m = {name: shard("m_", name) for name in WEIGHTS}
    v = {name: shard("v_", name) for name in WEIGHTS}

    conv_taps, conv_rows = w["conv_w"].shape
    conv_tile = jnp.pad(w["conv_w"], ((0, 8 - conv_taps), (0, 128 - conv_rows)))
    wire = {name: w[name].astype(BF16) for name in LARGE}
    full = {}

    def landed(names, gathered):
        for name, blocks in zip(names, gathered):
            _, r, c = blocks.shape
            full[name] = blocks if name == "w_kv_mem" else blocks.reshape(N_DEV * r, c)

    first = ("ffn1_gate", "ffn1_up")
    landed(first, _run_exchange(_gather_exchange([wire[k] for k in first]), "gather_first"))

    riders = (("ffn1_down", "w_in"), ("w_out", "w_kv_mem"), ("w_q_mem", "w_o_mem", "ffn2_gate", "ffn2_up"),
              ("ffn2_down",))
    (a1, b1, s1), gathered = _ffn_up(
        x0, w["ffn1_norm"], full["ffn1_gate"], full["ffn1_up"],
        exchange=_gather_exchange([wire[k] for k in riders[0]]))
    landed(riders[0], gathered)
    (x1,), gathered = _ffn_down(
        x0, s1, full["ffn1_down"], exchange=_gather_exchange([wire[k] for k in riders[1]] + [conv_tile]))
    landed(riders[1], gathered)
    convw_t = gathered[-1][:, :conv_taps, :conv_rows].transpose(1, 0, 2).reshape(conv_taps, N_DEV * conv_rows)
    (x2, z, o_raw, states, ycat), gathered = _mix_fwd(
        x1, w["mix_norm"], full["w_in"], w["lb_param"], w["hgrn_out_norm"], convw_t, full["w_out"],
        exchange=_gather_exchange([wire[k] for k in riders[2]]))
    landed(riders[2], gathered)
    kv = _memkv_fwd(memv, w["mem_norm"], full["w_kv_mem"])
    (x3, hq, qm, att), gathered = _xattn_fwd(
        x2, w["xattn_norm"], full["w_q_mem"], kv, full["w_o_mem"],
        exchange=_gather_exchange([wire[k] for k in riders[3]], middle_eighths=EARLY_MIDDLE_EIGHTHS))
    landed(riders[3], gathered)
    (dx4, a2, b2, s2, loss_part, d_final), _ = _ffn_fwd(
        x3, w["ffn2_norm"], full["ffn2_gate"], full["ffn2_up"], full["ffn2_down"], head=(w["final_norm"], target))

    parts = {}
    waiting = []

    def carried():
        names = [name for name, _ in waiting]
        exchange = _scatter_exchange([p for _, p in waiting]) if waiting else None
        del waiting[:]
        return names, exchange

    def received(names, arrived):
        for name, blocks in zip(names, arrived):
            group = name if isinstance(name, tuple) else (name,)
            rows = blocks.shape[1] // len(group)
            for p, member in enumerate(group):
                parts[member] = blocks[:, p * rows:(p + 1) * rows]

    def weight_grads(products):
        names, exchange = carried()
        partial, arrived = _weight_grad(list(products.values()), exchange=exchange)
        received(names, arrived)
        waiting.append((tuple(products), partial))

    def weight_grad(name, a, b, scale=1.0):
        weight_grads({name: (a, b, scale)})

    (dx3, da2, db2, h4, d_ffn2_norm), _ = _ffn_bwd(
        x3, w["ffn2_norm"], dx4, a2, b2, full["ffn2_gate"], full["ffn2_up"], full["ffn2_down"])
    weight_grad("ffn2_down", s2, dx4, 0.5)
    weight_grad("ffn2_gate", da2, h4)
    weight_grad("ffn2_up", db2, h4)
    names, exchange = carried()
    (dx2, dqm, dkv, d_xattn_norm), arrived = _xattn_bwd(
        x2, w["xattn_norm"], dx3, qm, kv, full["w_q_mem"], full["w_o_mem"], exchange=exchange)
    received(names, arrived)
    weight_grads({"w_o_mem": (att, dx3, 1.0), "w_q_mem": (hq, dqm, 1.0)})
    d_wkv, d_mem_norm = _memkv_bwd(memv, w["mem_norm"], dkv, full["w_kv_mem"])
    waiting.append(("w_kv_mem", d_wkv))
    names, exchange = carried()
    (dx1, dz, h2, d_mix_norm, d_lbp, d_gh, d_convw_t), arrived = _mix_bwd(
        x1, w["mix_norm"], dx2, z, o_raw, states, full["w_in"], w["lb_param"], w["hgrn_out_norm"], convw_t,
        full["w_out"], exchange=exchange)
    received(names, arrived)
    weight_grad("w_in", dz, h2)
    weight_grad("ffn1_down", s1, dx1, 0.5)
    (dx0, da1, db1, h1, d_ffn1_norm), _ = _ffn_bwd(
        x0, w["ffn1_norm"], dx1, a1, b1, full["ffn1_gate"], full["ffn1_up"], full["ffn1_down"])
    weight_grad("ffn1_gate", da1, h1)
    weight_grad("ffn1_up", db1, h1)
    weight_grad("w_out", ycat, dx2)

    small_parts = {
        "ffn1_norm": d_ffn1_norm, "mix_norm": d_mix_norm, "xattn_norm": d_xattn_norm, "mem_norm": d_mem_norm,
        "ffn2_norm": d_ffn2_norm, "final_norm": d_final, "lb_param": d_lbp, "hgrn_out_norm": d_gh,
        "conv_w": d_convw_t, "loss": loss_part,
    }
    names = [name for name, _ in waiting]
    arrived, total = _final_exchange([p for _, p in waiting], small_parts)
    received(names, arrived)

    g_out, d_out, m_out, v_out = {}, {}, {}, {}
    for name in LARGE:
        if name not in SHORT_SHARDS:
            g_out[name], d_out[name], m_out[name], v_out[name] = _adamw_shard(parts[name], w[name], m[name], v[name])
    g_small = {name: parts[name] for name in SHORT_SHARDS}
    for name in SMALL:
        row, nrows, ncols = SMALL_LAYOUT[name]
        g_small[name] = total[row:row + nrows, 0:ncols]
    g_small["conv_w"] = lax.dynamic_slice_in_dim(g_small["conv_w"], me * conv_rows, conv_rows, axis=1)
    together = SMALL + SHORT_SHARDS
    gs, ds, ms, vs = _adamw_small(
        [g_small[k] for k in together], [w[k] for k in together], [m[k] for k in together],
        [v[k] for k in together])
    for i, name in enumerate(together):
        g_out[name], d_out[name], m_out[name], v_out[name] = gs[i], ds[i], ms[i], vs[i]

    def shaped(value, name):
        return (value.T if name in TRANSPOSED else value).reshape(given[name].shape)

    loss = total[SMALL_LAYOUT["loss"][0], 0]
    outs = [loss, dx0.reshape(x.shape)]
    for group in (g_out, d_out, m_out, v_out):
        outs += [shaped(group[name], name) for name in WEIGHTS]
    return tuple(outs)
```
